```python
import jax, jax.numpy as jnp
from jax import lax
import numpy as np

D_MODEL = 1024
BATCH = 8
SEQ = 8192
DEPTH = 1

HEAD_DIM = 64
N_FOX_HEADS = 8
N_SB_HEADS = 8
FOX_WIDTH = N_FOX_HEADS * HEAD_DIM
SB_WIDTH = N_SB_HEADS * HEAD_DIM
IN_COLS = 3 * FOX_WIDTH + N_FOX_HEADS + 3 * SB_WIDTH + 2 * D_MODEL
D_FF = 2816
CONV_WIDTH = 3
Q_BLOCK = 128
LN_EPS = 1e-5
N_MOD = 6
DEEPNORM_ALPHA = (2.0 * DEPTH) ** 0.25
DEEPNORM_BETA = (8.0 * DEPTH) ** -0.25
ATTN_SCALE = HEAD_DIM ** -0.5

kernel_name = "fox_stickbreak_gated_hybrid_deepnorm_adaln"


def _split_points():
    cols = [FOX_WIDTH] * 3 + [N_FOX_HEADS] + [SB_WIDTH] * 3 + [D_MODEL] * 2
    return [int(v) for v in np.cumsum(cols)[:-1]]


def _layer_norm(x, g, b):
    xf = x.astype(jnp.float32)
    mu = jnp.mean(xf, axis=-1, keepdims=True)
    var = jnp.mean(jnp.square(xf - mu), axis=-1, keepdims=True)
    y = (xf - mu) * lax.rsqrt(var + LN_EPS)
    return (y * g.astype(jnp.float32) + b.astype(jnp.float32)).astype(x.dtype)


def _to_heads(t, n_heads):
    b, s, _ = t.shape
    return t.reshape(b, s, n_heads, HEAD_DIM).transpose(0, 2, 1, 3)


def _merge_heads(t):
    b, h, s, d = t.shape
    return t.transpose(0, 2, 1, 3).reshape(b, s, h * d)


def _blocks(t):
    b, h, s = t.shape[:3]
    nb = s // Q_BLOCK
    t = t.reshape((b, h, nb, Q_BLOCK) + t.shape[3:])
    return jnp.moveaxis(t, 2, 0)


def _unblocks(t):
    nb, b, h, qb, d = t.shape
    return jnp.moveaxis(t, 0, 2).reshape(b, h, nb * qb, d)


def _forgetting_attention(q, k, v, log_f):
    s_len = q.shape[2]
    cum = jnp.cumsum(log_f, axis=-1)
    kpos = jnp.arange(s_len)

    def block(args):
        qi, cqi, bi = args
        qpos = bi * Q_BLOCK + jnp.arange(Q_BLOCK)
        logits = (jnp.einsum('bhqd,bhkd->bhqk', qi, k).astype(jnp.float32) * ATTN_SCALE
                  + cqi[..., None] - cum[:, :, None, :])
        causal = kpos[None, :] <= qpos[:, None]
        logits = jnp.where(causal, logits, -jnp.inf)
        p = jax.nn.softmax(logits, axis=-1)
        return jnp.einsum('bhqk,bhkd->bhqd', p.astype(v.dtype), v)

    nb = s_len // Q_BLOCK
    out = lax.map(block, (_blocks(q), _blocks(cum), jnp.arange(nb, dtype=jnp.int32)))
    return _unblocks(out)


def _stick_breaking_attention(q, k, v):
    s_len = q.shape[2]
    kpos = jnp.arange(s_len)

    def block(args):
        qi, bi = args
        qpos = bi * Q_BLOCK + jnp.arange(Q_BLOCK)
        z = jnp.einsum('bhqd,bhkd->bhqk', qi, k).astype(jnp.float32) * ATTN_SCALE
        strict = kpos[None, :] < qpos[:, None]
        log_beta = jax.nn.log_sigmoid(z)
        log_one_minus = jnp.where(strict, jax.nn.log_sigmoid(-z), 0.0)
        rest = lax.cumsum(log_one_minus, axis=3, reverse=True) - log_one_minus
        a = jnp.where(strict, jnp.exp(log_beta + rest), 0.0)
        return jnp.einsum('bhqk,bhkd->bhqd', a.astype(v.dtype), v)

    nb = s_len // Q_BLOCK
    out = lax.map(block, (_blocks(q), jnp.arange(nb, dtype=jnp.int32)))
    return _unblocks(out)


def _causal_depthwise_conv(h, w, b):
    s_len = h.shape[1]
    hp = jnp.pad(h, ((0, 0), (CONV_WIDTH - 1, 0), (0, 0)))
    out = b
    for tap in range(CONV_WIDTH):
        out = out + hp[:, tap:tap + s_len, :] * w[tap]
    return out


def _fwd_setup_inputs(seed: int = 0) -> dict:
    key = jax.random.key(seed)
    ks = jax.random.split(key, 17)
    f32 = jnp.float32
    d = D_MODEL
    nrm = lambda k, shape, s: jax.random.normal(k, shape, f32) * s
    return {
        "x": nrm(ks[0], (BATCH, SEQ, d), 1.0),
        "c": nrm(ks[1], (BATCH, d), 1.0),
        "w_ada": nrm(ks[2], (d, N_MOD * d), 0.1 * d ** -0.5),
        "b_ada": nrm(ks[3], (N_MOD * d,), 0.01),
        "w_in": nrm(ks[4], (d, IN_COLS), d ** -0.5),
        "b_forget": jnp.linspace(1.0, 5.0, N_FOX_HEADS, dtype=f32) + nrm(ks[5], (N_FOX_HEADS,), 0.1),
        "w_fox_proj": nrm(ks[6], (FOX_WIDTH, d), FOX_WIDTH ** -0.5),
        "w_sb_proj": nrm(ks[7], (SB_WIDTH, d), SB_WIDTH ** -0.5),
        "w_o": nrm(ks[8], (d, d), DEEPNORM_BETA * d ** -0.5),
        "ln1_g": 1.0 + nrm(ks[9], (d,), 0.02),
        "ln1_b": nrm(ks[10], (d,), 0.02),
        "w_up": nrm(ks[11], (d, 2 * D_FF), d ** -0.5),
        "conv_w": nrm(ks[12], (CONV_WIDTH, 2 * D_FF), CONV_WIDTH ** -0.5),
        "conv_b": nrm(ks[13], (2 * D_FF,), 0.01),
        "w_down": nrm(ks[14], (D_FF, d), DEEPNORM_BETA * D_FF ** -0.5),
        "ln2_g": 1.0 + nrm(ks[15], (d,), 0.02),
        "ln2_b": nrm(ks[16], (d,), 0.02),
    }


def _fwd_reference(x, c, w_ada, b_ada, w_in, b_forget, w_fox_proj, w_sb_proj, w_o,
              ln1_g, ln1_b, w_up, conv_w, conv_b, w_down, ln2_g, ln2_b):
    for _ in range(DEPTH):
        mod = c @ w_ada + b_ada
        sh1, sc1, gt1, sh2, sc2, gt2 = [m[:, None, :] for m in jnp.split(mod, N_MOD, axis=-1)]

        u = x * (1.0 + sc1) + sh1
        proj = u @ w_in
        q_a, k_a, v_a, f_a, q_b, k_b, v_b, g_a, g_b = jnp.split(proj, _split_points(), axis=-1)

        log_f = jax.nn.log_sigmoid((f_a + b_forget).astype(jnp.float32)).transpose(0, 2, 1)
        y_fox = _merge_heads(_forgetting_attention(
            _to_heads(q_a, N_FOX_HEADS), _to_heads(k_a, N_FOX_HEADS), _to_heads(v_a, N_FOX_HEADS), log_f))
        y_sb = _merge_heads(_stick_breaking_attention(
            _to_heads(q_b, N_SB_HEADS), _to_heads(k_b, N_SB_HEADS), _to_heads(v_b, N_SB_HEADS)))

        merged = jax.nn.sigmoid(g_a) * (y_fox @ w_fox_proj) + jax.nn.sigmoid(g_b) * (y_sb @ w_sb_proj)
        attn_out = merged @ w_o
        x = _layer_norm(DEEPNORM_ALPHA * x + (1.0 + gt1) * attn_out, ln1_g, ln1_b)

        u2 = x * (1.0 + sc2) + sh2
        h = _causal_depthwise_conv(u2 @ w_up, conv_w, conv_b)
        h_gate, h_val = jnp.split(h, 2, axis=-1)
        ffn_out = (jax.nn.silu(h_gate) * h_val) @ w_down
        x = _layer_norm(DEEPNORM_ALPHA * x + (1.0 + gt2) * ffn_out, ln2_g, ln2_b)
    return x


import jax as _jax
import jax.numpy as _jnp

TWIN_FORMAT = 'train_step'
FWD_PARAMS = ['x', 'c', 'w_ada', 'b_ada', 'w_in', 'b_forget', 'w_fox_proj', 'w_sb_proj', 'w_o', 'ln1_g', 'ln1_b', 'w_up', 'conv_w', 'conv_b', 'w_down', 'ln2_g', 'ln2_b']
TWIN_WEIGHTS = ['w_ada', 'b_ada', 'w_in', 'b_forget', 'w_fox_proj', 'w_sb_proj', 'w_o', 'ln1_g', 'ln1_b', 'w_up', 'conv_w', 'conv_b', 'w_down', 'ln2_g', 'ln2_b']
TWIN_DIFF_INPUT = 'x'
TWIN_INPUTS = ['x', 'c', 'w_ada', 'b_ada', 'w_in', 'b_forget', 'w_fox_proj', 'w_sb_proj', 'w_o', 'ln1_g', 'ln1_b', 'w_up', 'conv_w', 'conv_b', 'w_down', 'ln2_g', 'ln2_b', 'loss_target', 'm_w_ada', 'm_b_ada', 'm_w_in', 'm_b_forget', 'm_w_fox_proj', 'm_w_sb_proj', 'm_w_o', 'm_ln1_g', 'm_ln1_b', 'm_w_up', 'm_conv_w', 'm_conv_b', 'm_w_down', 'm_ln2_g', 'm_ln2_b', 'v_w_ada', 'v_b_ada', 'v_w_in', 'v_b_forget', 'v_w_fox_proj', 'v_w_sb_proj', 'v_w_o', 'v_ln1_g', 'v_ln1_b', 'v_w_up', 'v_conv_w', 'v_conv_b', 'v_w_down', 'v_ln2_g', 'v_ln2_b']
TWIN_OUTPUTS = ['loss', 'grad_x', 'grad_w_ada', 'grad_b_ada', 'grad_w_in', 'grad_b_forget', 'grad_w_fox_proj', 'grad_w_sb_proj', 'grad_w_o', 'grad_ln1_g', 'grad_ln1_b', 'grad_w_up', 'grad_conv_w', 'grad_conv_b', 'grad_w_down', 'grad_ln2_g', 'grad_ln2_b', 'delta_w_ada', 'delta_b_ada', 'delta_w_in', 'delta_b_forget', 'delta_w_fox_proj', 'delta_w_sb_proj', 'delta_w_o', 'delta_ln1_g', 'delta_ln1_b', 'delta_w_up', 'delta_conv_w', 'delta_conv_b', 'delta_w_down', 'delta_ln2_g', 'delta_ln2_b', 'new_m_w_ada', 'new_m_b_ada', 'new_m_w_in', 'new_m_b_forget', 'new_m_w_fox_proj', 'new_m_w_sb_proj', 'new_m_w_o', 'new_m_ln1_g', 'new_m_ln1_b', 'new_m_w_up', 'new_m_conv_w', 'new_m_conv_b', 'new_m_w_down', 'new_m_ln2_g', 'new_m_ln2_b', 'new_v_w_ada', 'new_v_b_ada', 'new_v_w_in', 'new_v_b_forget', 'new_v_w_fox_proj', 'new_v_w_sb_proj', 'new_v_w_o', 'new_v_ln1_g', 'new_v_ln1_b', 'new_v_w_up', 'new_v_conv_w', 'new_v_conv_b', 'new_v_w_down', 'new_v_ln2_g', 'new_v_ln2_b']
TWIN_LEAF_KINDS = {'loss': 'loss', 'grad_x': 'grad_x', 'grad_w_ada': 'grad_w', 'grad_b_ada': 'grad_w', 'grad_w_in': 'grad_w', 'grad_b_forget': 'grad_w', 'grad_w_fox_proj': 'grad_w', 'grad_w_sb_proj': 'grad_w', 'grad_w_o': 'grad_w', 'grad_ln1_g': 'grad_w', 'grad_ln1_b': 'grad_w', 'grad_w_up': 'grad_w', 'grad_conv_w': 'grad_w', 'grad_conv_b': 'grad_w', 'grad_w_down': 'grad_w', 'grad_ln2_g': 'grad_w', 'grad_ln2_b': 'grad_w', 'delta_w_ada': 'delta_w', 'delta_b_ada': 'delta_w', 'delta_w_in': 'delta_w', 'delta_b_forget': 'delta_w', 'delta_w_fox_proj': 'delta_w', 'delta_w_sb_proj': 'delta_w', 'delta_w_o': 'delta_w', 'delta_ln1_g': 'delta_w', 'delta_ln1_b': 'delta_w', 'delta_w_up': 'delta_w', 'delta_conv_w': 'delta_w', 'delta_conv_b': 'delta_w', 'delta_w_down': 'delta_w', 'delta_ln2_g': 'delta_w', 'delta_ln2_b': 'delta_w', 'new_m_w_ada': 'new_m', 'new_m_b_ada': 'new_m', 'new_m_w_in': 'new_m', 'new_m_b_forget': 'new_m', 'new_m_w_fox_proj': 'new_m', 'new_m_w_sb_proj': 'new_m', 'new_m_w_o': 'new_m', 'new_m_ln1_g': 'new_m', 'new_m_ln1_b': 'new_m', 'new_m_w_up': 'new_m', 'new_m_conv_w': 'new_m', 'new_m_conv_b': 'new_m', 'new_m_w_down': 'new_m', 'new_m_ln2_g': 'new_m', 'new_m_ln2_b': 'new_m', 'new_v_w_ada': 'new_v', 'new_v_b_ada': 'new_v', 'new_v_w_in': 'new_v', 'new_v_b_forget': 'new_v', 'new_v_w_fox_proj': 'new_v', 'new_v_w_sb_proj': 'new_v', 'new_v_w_o': 'new_v', 'new_v_ln1_g': 'new_v', 'new_v_ln1_b': 'new_v', 'new_v_w_up': 'new_v', 'new_v_conv_w': 'new_v', 'new_v_conv_b': 'new_v', 'new_v_w_down': 'new_v', 'new_v_ln2_g': 'new_v', 'new_v_ln2_b': 'new_v'}


def _forward(args):
    return _fwd_reference(*[args[k] for k in FWD_PARAMS])


def _output_shape():
    out = _jax.eval_shape(lambda: _forward(_fwd_setup_inputs(0)))
    return out.shape, out.dtype

N_MICROBATCH = 1
ADAM_LR = 0.001
ADAM_B1 = 0.9
ADAM_B2 = 0.999
ADAM_EPS = 1e-08
ADAM_WD = 0.01
ADAM_STEP = 10
PER_EXAMPLE_BATCH_AXIS = {'x': 0, 'c': 0, 'loss_target': 0}
SHARED_INPUTS = []
_WEIGHT_DTYPES = {'w_ada': _jnp.float32, 'b_ada': _jnp.float32, 'w_in': _jnp.float32, 'b_forget': _jnp.float32, 'w_fox_proj': _jnp.float32, 'w_sb_proj': _jnp.float32, 'w_o': _jnp.float32, 'ln1_g': _jnp.float32, 'ln1_b': _jnp.float32, 'w_up': _jnp.float32, 'conv_w': _jnp.float32, 'conv_b': _jnp.float32, 'w_down': _jnp.float32, 'ln2_g': _jnp.float32, 'ln2_b': _jnp.float32}
MOMENT_SCALE = {'w_ada': 1.000623e-01, 'b_ada': 1.241921e-01, 'w_in': 3.124106e-02, 'b_forget': 2.089767e-01, 'w_fox_proj': 2.601312e-02, 'w_sb_proj': 4.702207e-02, 'w_o': 8.899626e-02, 'ln1_g': 1.742106e+00, 'ln1_b': 6.441914e-01, 'w_up': 4.564292e-02, 'conv_w': 4.593575e-02, 'conv_b': 5.215511e-02, 'w_down': 1.249959e-01, 'ln2_g': 6.391158e+01, 'ln2_b': 1.303268e+00}


def _to_microbatches(a, axis):
    t = _jnp.moveaxis(a, axis, 0)
    t = t.reshape((N_MICROBATCH, t.shape[0] // N_MICROBATCH) + t.shape[1:])
    return _jnp.moveaxis(t, 1, axis + 1)


def setup_inputs(seed: int = 0) -> dict:
    inp = _fwd_setup_inputs(seed)
    key = _jax.random.fold_in(_jax.random.key(seed), 7919)
    shape, _ = _output_shape()
    out = dict(inp)
    out["loss_target"] = _jax.random.normal(_jax.random.fold_in(key, 0), shape, _jnp.float32)
    for i, name in enumerate(TWIN_WEIGHTS):
        w = inp[name].astype(_jnp.float32)
        if MOMENT_SCALE is None:
            s = _jnp.sqrt(_jnp.mean(_jnp.square(w)) + 1e-30)
        else:
            s = MOMENT_SCALE[name]
        km, kv = _jax.random.split(_jax.random.fold_in(key, i + 1))
        out[name] = w
        out["m_" + name] = s * _jax.random.normal(km, w.shape, _jnp.float32)
        out["v_" + name] = (s * s) * _jax.random.uniform(kv, w.shape, _jnp.float32, 0.5, 1.5)
    if N_MICROBATCH > 1:
        for name, axis in PER_EXAMPLE_BATCH_AXIS.items():
            out[name] = _to_microbatches(out[name], axis)
    return {'x': out['x'], 'c': out['c'], 'w_ada': out['w_ada'], 'b_ada': out['b_ada'], 'w_in': out['w_in'], 'b_forget': out['b_forget'], 'w_fox_proj': out['w_fox_proj'], 'w_sb_proj': out['w_sb_proj'], 'w_o': out['w_o'], 'ln1_g': out['ln1_g'], 'ln1_b': out['ln1_b'], 'w_up': out['w_up'], 'conv_w': out['conv_w'], 'conv_b': out['conv_b'], 'w_down': out['w_down'], 'ln2_g': out['ln2_g'], 'ln2_b': out['ln2_b'], 'loss_target': out['loss_target'], 'm_w_ada': out['m_w_ada'], 'm_b_ada': out['m_b_ada'], 'm_w_in': out['m_w_in'], 'm_b_forget': out['m_b_forget'], 'm_w_fox_proj': out['m_w_fox_proj'], 'm_w_sb_proj': out['m_w_sb_proj'], 'm_w_o': out['m_w_o'], 'm_ln1_g': out['m_ln1_g'], 'm_ln1_b': out['m_ln1_b'], 'm_w_up': out['m_w_up'], 'm_conv_w': out['m_conv_w'], 'm_conv_b': out['m_conv_b'], 'm_w_down': out['m_w_down'], 'm_ln2_g': out['m_ln2_g'], 'm_ln2_b': out['m_ln2_b'], 'v_w_ada': out['v_w_ada'], 'v_b_ada': out['v_b_ada'], 'v_w_in': out['v_w_in'], 'v_b_forget': out['v_b_forget'], 'v_w_fox_proj': out['v_w_fox_proj'], 'v_w_sb_proj': out['v_w_sb_proj'], 'v_w_o': out['v_w_o'], 'v_ln1_g': out['v_ln1_g'], 'v_ln1_b': out['v_ln1_b'], 'v_w_up': out['v_w_up'], 'v_conv_w': out['v_conv_w'], 'v_conv_b': out['v_conv_b'], 'v_w_down': out['v_w_down'], 'v_ln2_g': out['v_ln2_g'], 'v_ln2_b': out['v_ln2_b']}


def _loss(weights, diff, rest, loss_target):
    with _jax.named_scope("forward"):
        args = {**rest, TWIN_DIFF_INPUT: diff, **{k: w.astype(_WEIGHT_DTYPES[k]) for k, w in weights.items()}}
        y = _forward(args)
    with _jax.named_scope("loss_head"):
        err = _jnp.square(y.astype(_jnp.float32) - loss_target)
        return 0.5 * _jnp.sum(_jnp.mean(err, axis=-1)) if err.ndim else 0.5 * err


def _adamw(w, g, m, v):
    m = ADAM_B1 * m + (1.0 - ADAM_B1) * g
    v = ADAM_B2 * v + (1.0 - ADAM_B2) * _jnp.square(g)
    m_hat = m / (1.0 - ADAM_B1 ** ADAM_STEP)
    v_hat = v / (1.0 - ADAM_B2 ** ADAM_STEP)
    delta = -ADAM_LR * (m_hat / (_jnp.sqrt(v_hat) + ADAM_EPS) + ADAM_WD * w)
    return delta, m, v


def reference(x, c, w_ada, b_ada, w_in, b_forget, w_fox_proj, w_sb_proj, w_o, ln1_g, ln1_b, w_up, conv_w, conv_b, w_down, ln2_g, ln2_b, loss_target, m_w_ada, m_b_ada, m_w_in, m_b_forget, m_w_fox_proj, m_w_sb_proj, m_w_o, m_ln1_g, m_ln1_b, m_w_up, m_conv_w, m_conv_b, m_w_down, m_ln2_g, m_ln2_b, v_w_ada, v_b_ada, v_w_in, v_b_forget, v_w_fox_proj, v_w_sb_proj, v_w_o, v_ln1_g, v_ln1_b, v_w_up, v_conv_w, v_conv_b, v_w_down, v_ln2_g, v_ln2_b):
    given = dict(x=x, c=c, w_ada=w_ada, b_ada=b_ada, w_in=w_in, b_forget=b_forget, w_fox_proj=w_fox_proj, w_sb_proj=w_sb_proj, w_o=w_o, ln1_g=ln1_g, ln1_b=ln1_b, w_up=w_up, conv_w=conv_w, conv_b=conv_b, w_down=w_down, ln2_g=ln2_g, ln2_b=ln2_b, loss_target=loss_target, m_w_ada=m_w_ada, m_b_ada=m_b_ada, m_w_in=m_w_in, m_b_forget=m_b_forget, m_w_fox_proj=m_w_fox_proj, m_w_sb_proj=m_w_sb_proj, m_w_o=m_w_o, m_ln1_g=m_ln1_g, m_ln1_b=m_ln1_b, m_w_up=m_w_up, m_conv_w=m_conv_w, m_conv_b=m_conv_b, m_w_down=m_w_down, m_ln2_g=m_ln2_g, m_ln2_b=m_ln2_b, v_w_ada=v_w_ada, v_b_ada=v_b_ada, v_w_in=v_w_in, v_b_forget=v_b_forget, v_w_fox_proj=v_w_fox_proj, v_w_sb_proj=v_w_sb_proj, v_w_o=v_w_o, v_ln1_g=v_ln1_g, v_ln1_b=v_ln1_b, v_w_up=v_w_up, v_conv_w=v_conv_w, v_conv_b=v_conv_b, v_w_down=v_w_down, v_ln2_g=v_ln2_g, v_ln2_b=v_ln2_b)
    weights = {n: given[n] for n in TWIN_WEIGHTS}
    shared = {n: given[n] for n in SHARED_INPUTS}
    per_example = {n: given[n] for n in ['x', 'c']}
    grad_fn = _jax.value_and_grad(_loss, argnums=(0, 1))

    def one_microbatch(ex, loss_target):
        ex = dict(ex)
        diff = ex.pop(TWIN_DIFF_INPUT)
        return grad_fn(weights, diff, {**shared, **ex}, loss_target)

    if N_MICROBATCH == 1:
        loss, (grad_w, grad_x) = one_microbatch(per_example, given["loss_target"])
    else:
        def body(carry, xs):
            loss_sum, grad_sum = carry
            l_k, (gw_k, gx_k) = one_microbatch(xs[0], xs[1])
            with _jax.named_scope("update"):
                return (loss_sum + l_k, _jax.tree.map(_jnp.add, grad_sum, gw_k)), gx_k

        init = (_jnp.zeros((), _jnp.float32), _jax.tree.map(_jnp.zeros_like, weights))
        (loss, grad_w), grad_x = _jax.lax.scan(body, init, (per_example, given["loss_target"]))
    with _jax.named_scope("update"):
        delta_w, new_m, new_v = {}, {}, {}
        for n in TWIN_WEIGHTS:
            delta_w[n], new_m[n], new_v[n] = _adamw(weights[n], grad_w[n], given["m_" + n], given["v_" + n])
    return (loss, grad_x, *[grad_w[n] for n in TWIN_WEIGHTS], *[delta_w[n] for n in TWIN_WEIGHTS],
            *[new_m[n] for n in TWIN_WEIGHTS], *[new_v[n] for n in TWIN_WEIGHTS])
```

```python
import functools

import jax
import jax.numpy as jnp
from jax import lax
from jax.experimental import pallas as pl
from jax.experimental.pallas import tpu as pltpu

F32 = jnp.float32
BF16 = jnp.bfloat16
MESH = pl.DeviceIdType.MESH
ANY = pl.BlockSpec(memory_space=pl.ANY)

N_DEV = 8
D_MODEL = 1024
HEAD_DIM = 64
N_HEADS = 8
ATTN_W = N_HEADS * HEAD_DIM
D_FF = 2816
FF_HALF = D_FF // 2
N_MOD = 6
ADA_SHARD = N_MOD * D_MODEL // N_DEV
IN_SHARD = 641
UP_SHARD = 704
ATTN_SCALE = HEAD_DIM ** -0.5
ALPHA = 2.0 ** 0.25
LN_EPS = 1e-5
LANES = 128
TQ = 256
VMEM_LIMIT = 56 * 1024 * 1024

ADAM_LR, ADAM_B1, ADAM_B2, ADAM_EPS, ADAM_WD, ADAM_STEP = 0.001, 0.9, 0.999, 1e-08, 0.01, 10

W_QKV, W_GATES, W_F = 3072, 2048, 128
W_PROJ = 5376


def _params(sem=None):
    return pltpu.CompilerParams(dimension_semantics=sem, vmem_limit_bytes=VMEM_LIMIT)


def _tile(n, cap):
    if n <= cap:
        return n
    best = None
    for t in range(LANES, cap + 1, LANES):
        if n % t == 0:
            best = t
    assert best is not None, (n, cap)
    return best


def _row_tile(r, width, budget=128 * 1024):
    if r * width <= budget or r % 8:
        return r
    best = 8
    for t in range(8, r + 1, 8):
        if r % t == 0 and t * width <= budget:
            best = t
    return best


def _mm(a, b, *, name, ta=False, tb=False, out_dtype=F32, tm=1024, tn=1024, tk=1024):
    m, k = (a.shape[1], a.shape[0]) if ta else a.shape
    n = b.shape[0] if tb else b.shape[1]
    assert (b.shape[1] if tb else b.shape[0]) == k
    tm, tn, tk = _tile(m, tm), _tile(n, tn), _tile(k, tk)
    nk = k // tk
    a_spec = pl.BlockSpec((tk, tm), lambda i, j, l: (l, i)) if ta else pl.BlockSpec((tm, tk), lambda i, j, l: (i, l))
    b_spec = pl.BlockSpec((tn, tk), lambda i, j, l: (j, l)) if tb else pl.BlockSpec((tk, tn), lambda i, j, l: (l, j))
    dims = (((0,) if ta else (1,), (1,) if tb else (0,)), ((), ()))

    def body(a_ref, b_ref, o_ref, *acc):
        p = lax.dot_general(a_ref[...].astype(BF16), b_ref[...].astype(BF16), dims, preferred_element_type=F32)
        if nk == 1:
            o_ref[...] = p.astype(out_dtype)
            return
        acc_ref = acc[0]
        step = pl.program_id(2)

        @pl.when(step == 0)
        def _():
            acc_ref[...] = p

        @pl.when(step > 0)
        def _():
            acc_ref[...] += p

        @pl.when(step == nk - 1)
        def _():
            o_ref[...] = acc_ref[...].astype(out_dtype)

    return pl.pallas_call(
        body, name=name, grid=(m // tm, n // tn, nk),
        in_specs=[a_spec, b_spec], out_specs=pl.BlockSpec((tm, tn), lambda i, j, l: (i, j)),
        out_shape=jax.ShapeDtypeStruct((m, n), out_dtype),
        scratch_shapes=[] if nk == 1 else [pltpu.VMEM((tm, tn), F32)],
        compiler_params=_params(("parallel", "parallel", "arbitrary")),
    )(a, b)


def _rowwise(name, fn, rows, vecs, outs, sums=(), tr=256):
    s = rows[0][0].shape[0]
    tr = min(tr, s)
    nr, nv, no = len(rows), len(vecs), len(outs)

    def body(*refs):
        vals = [r[...] for r in refs[:nr + nv]]
        res = fn(*vals)
        for o_ref, val in zip(refs[nr + nv:nr + nv + no], res[:no]):
            o_ref[...] = val.astype(o_ref.dtype)
        step = pl.program_id(0)
        for s_ref, val in zip(refs[nr + nv + no:], res[no:]):
            @pl.when(step == 0)
            def _(s_ref=s_ref, val=val):
                s_ref[...] = val

            @pl.when(step > 0)
            def _(s_ref=s_ref, val=val):
                s_ref[...] += val

    in_specs = [pl.BlockSpec((tr, w), functools.partial(lambda i, cb: (i, cb), cb=cb)) for _, w, cb in rows]
    in_specs += [pl.BlockSpec(v.shape, lambda i: (0, 0)) for v in vecs]
    out_specs = [pl.BlockSpec((tr, w), lambda i: (i, 0)) for w, _ in outs]
    out_specs += [pl.BlockSpec((1, w), lambda i: (0, 0)) for w in sums]
    out_shape = [jax.ShapeDtypeStruct((s, w), dt) for w, dt in outs]
    out_shape += [jax.ShapeDtypeStruct((1, w), F32) for w in sums]
    return pl.pallas_call(
        body, name=name, grid=(s // tr,), in_specs=in_specs, out_specs=out_specs, out_shape=out_shape,
        compiler_params=_params(("arbitrary",) if sums else ("parallel",)),
    )(*[r[0] for r in rows], *vecs)


def _colsum(x):
    return jnp.sum(x, axis=0, keepdims=True)


def _sigmoid(x):
    return 1.0 / (1.0 + jnp.exp(-x))


def _log_sigmoid(x):
    return jnp.minimum(x, 0.0) - jnp.log(1.0 + jnp.exp(-jnp.abs(x)))


def _ln_stats(r):
    mu = jnp.mean(r, axis=-1, keepdims=True)
    xc = r - mu
    var = jnp.mean(xc * xc, axis=-1, keepdims=True)
    rstd = lax.rsqrt(var + LN_EPS)
    return xc * rstd, rstd


def _ln_bwd(dy, xhat, rstd, g):
    dxh = dy * g
    m1 = jnp.mean(dxh, axis=-1, keepdims=True)
    m2 = jnp.mean(dxh * xhat, axis=-1, keepdims=True)
    return rstd * (dxh - m1 - xhat * m2)


def _me():
    x, y, c = lax.axis_index("x"), lax.axis_index("y"), lax.axis_index("c")
    return x, y, c, 4 * x + 2 * y + c


def _peer(r):
    x, y, c, _ = _me()
    px = 1 - x if r & 4 else x
    py = 1 - y if r & 2 else y
    pc = 1 - c if r & 1 else c
    return (px, py, pc), 4 * px + 2 * py + pc


def _exchange(name, arrays, scatter):
    n = len(arrays)
    out_shape = [jax.ShapeDtypeStruct(a.shape if scatter else (N_DEV,) + a.shape, a.dtype) for a in arrays]

    def body(*refs):
        ins, outs = refs[:n], refs[n:2 * n]
        send_sems, recv_sems, local_sems = refs[2 * n:]
        me = _me()[3]
        local = []
        for a in range(n):
            cp = pltpu.make_async_copy(ins[a].at[me] if scatter else ins[a], outs[a].at[me], local_sems.at[a])
            cp.start()
            local.append(cp)
        sent = []
        for r in range(1, N_DEV):
            peer, pidx = _peer(r)
            for a in range(n):
                cp = pltpu.make_async_remote_copy(
                    src_ref=ins[a].at[pidx] if scatter else ins[a], dst_ref=outs[a].at[me],
                    send_sem=send_sems.at[a, r - 1], recv_sem=recv_sems.at[a, r - 1],
                    device_id=peer, device_id_type=MESH)
                cp.start()
                sent.append(cp)
        for r in range(1, N_DEV):
            peer, pidx = _peer(r)
            for a in range(n):
                pltpu.make_async_remote_copy(
                    src_ref=ins[a].at[me] if scatter else ins[a], dst_ref=outs[a].at[pidx],
                    send_sem=send_sems.at[a, r - 1], recv_sem=recv_sems.at[a, r - 1],
                    device_id=peer, device_id_type=MESH).wait_recv()
        for cp in sent:
            cp.wait_send()
        for cp in local:
            cp.wait()

    return pl.pallas_call(
        body, name=name, in_specs=[ANY] * n, out_specs=[ANY] * n, out_shape=out_shape,
        scratch_shapes=[pltpu.SemaphoreType.DMA((n, N_DEV - 1)), pltpu.SemaphoreType.DMA((n, N_DEV - 1)),
                        pltpu.SemaphoreType.DMA((n,))],
    )(*arrays)


def _mod_exchange(c_row, w_ada, b_ada_loc):
    def body(c_ref, w_ref, b_ref, call_ref, mod_ref, piece_ref, send_sems, recv_sems):
        me = _me()[3]
        call_ref[me] = c_ref[...]
        sent = []
        for r in range(1, N_DEV):
            peer, _ = _peer(r)
            cp = pltpu.make_async_remote_copy(
                src_ref=c_ref, dst_ref=call_ref.at[me], send_sem=send_sems.at[0, r - 1],
                recv_sem=recv_sems.at[0, r - 1], device_id=peer, device_id_type=MESH)
            cp.start()
            sent.append(cp)
        for r in range(1, N_DEV):
            peer, pidx = _peer(r)
            pltpu.make_async_remote_copy(
                src_ref=c_ref, dst_ref=call_ref.at[pidx], send_sem=send_sems.at[0, r - 1],
                recv_sem=recv_sems.at[0, r - 1], device_id=peer, device_id_type=MESH).wait_recv()
        c_all = jnp.concatenate([call_ref[d] for d in range(N_DEV)], axis=0)
        mod_loc = jnp.dot(c_all, w_ref[...], preferred_element_type=F32,
                          precision=lax.Precision.HIGHEST) + b_ref[...]
        for d in range(N_DEV):
            piece_ref[d] = mod_loc[d:d + 1, :]
        mod_ref[me] = piece_ref[me]
        for r in range(1, N_DEV):
            peer, pidx = _peer(r)
            cp = pltpu.make_async_remote_copy(
                src_ref=piece_ref.at[pidx], dst_ref=mod_ref.at[me], send_sem=send_sems.at[1, r - 1],
                recv_sem=recv_sems.at[1, r - 1], device_id=peer, device_id_type=MESH)
            cp.start()
            sent.append(cp)
        for r in range(1, N_DEV):
            peer, pidx = _peer(r)
            pltpu.make_async_remote_copy(
                src_ref=piece_ref.at[me], dst_ref=mod_ref.at[pidx], send_sem=send_sems.at[1, r - 1],
                recv_sem=recv_sems.at[1, r - 1], device_id=peer, device_id_type=MESH).wait_recv()
        for cp in sent:
            cp.wait_send()

    vmem = pl.BlockSpec(memory_space=pltpu.VMEM)
    return pl.pallas_call(
        body, name="mod_exchange", in_specs=[vmem, vmem, vmem], out_specs=[vmem, vmem],
        out_shape=[jax.ShapeDtypeStruct((N_DEV, 1, D_MODEL), F32), jax.ShapeDtypeStruct((N_DEV, 1, ADA_SHARD), F32)],
        scratch_shapes=[pltpu.VMEM((N_DEV, 1, ADA_SHARD), F32),
                        pltpu.SemaphoreType.DMA((2, N_DEV - 1)), pltpu.SemaphoreType.DMA((2, N_DEV - 1))],
        compiler_params=_params(),
    )(c_row, w_ada, b_ada_loc)


def _split3(x):
    hi = x.astype(BF16)
    r1 = x - hi.astype(F32)
    mid = r1.astype(BF16)
    lo = (r1 - mid.astype(F32)).astype(BF16)
    return hi, mid, lo


def _scan_rows(x_ref, o_ref, s, reverse, pre=None, post=None):
    tb = min(TQ, s)
    nb = s // tb
    row = lax.broadcasted_iota(jnp.int32, (tb, tb), 0)
    col = lax.broadcasted_iota(jnp.int32, (tb, tb), 1)
    tri = jnp.where((col >= row) if reverse else (col <= row), 1.0, 0.0).astype(BF16)

    def step(i, carry):
        blk = (nb - 1 - i) if reverse else i
        off = pl.multiple_of(blk * tb, tb)
        x = x_ref[pl.ds(off, tb), :]
        if pre is not None:
            x = pre(x, off)
        acc = carry
        for piece in _split3(x):
            acc = acc + jnp.dot(tri, piece, preferred_element_type=F32)
        o_ref[pl.ds(off, tb), :] = acc if post is None else post(acc, off)
        edge = acc[0:1, :] if reverse else acc[tb - 1:tb, :]
        return jnp.broadcast_to(edge, (tb, LANES))

    lax.fori_loop(0, nb, step, jnp.zeros((tb, LANES), F32))


def _forget_cumsum(f_raw, b_pad):
    s = f_raw.shape[0]

    def body(f_ref, b_ref, cum_ref):
        b = b_ref[...]
        _scan_rows(f_ref, cum_ref, s, False, pre=lambda x, off: _log_sigmoid(x + b))

    vmem = pl.BlockSpec(memory_space=pltpu.VMEM)
    return pl.pallas_call(body, name="forget_cumsum", in_specs=[vmem, vmem], out_specs=vmem,
                          out_shape=jax.ShapeDtypeStruct((s, LANES), F32), compiler_params=_params())(f_raw, b_pad)


def _forget_bwd(dcum, f_raw, b_pad):
    s = f_raw.shape[0]

    def body(d_ref, f_ref, b_ref, df_ref, db_ref, tmp_ref):
        b = b_ref[...]
        _scan_rows(d_ref, tmp_ref, s, True)
        df = tmp_ref[...] * _sigmoid(-(f_ref[...] + b))
        df_ref[...] = df.astype(BF16)
        db_ref[...] = _colsum(df)

    vmem = pl.BlockSpec(memory_space=pltpu.VMEM)
    return pl.pallas_call(
        body, name="forget_bwd", in_specs=[vmem, vmem, vmem], out_specs=[vmem, vmem],
        out_shape=[jax.ShapeDtypeStruct((s, LANES), BF16), jax.ShapeDtypeStruct((1, LANES), F32)],
        scratch_shapes=[pltpu.VMEM((s, LANES), F32)], compiler_params=_params())(dcum, f_raw, b_pad)


def _dot_nt(a, b):
    return lax.dot_general(a, b, (((1,), (1,)), ((), ())), preferred_element_type=F32)


def _dot_tn(a, b):
    return lax.dot_general(a, b, (((0,), (0,)), ((), ())), preferred_element_type=F32)


def _head_masks():
    lane = lax.broadcasted_iota(jnp.int32, (TQ, LANES), 1)
    return lane, [lane < HEAD_DIM, lane >= HEAD_DIM]


def _pick(mask, x):
    return jnp.where(mask, x, jnp.zeros_like(x))


def _qkv_specs(s, col0):
    nb = ATTN_W // LANES
    return [pl.BlockSpec((TQ, LANES), lambda hp, qi: (qi, col0 + hp)),
            pl.BlockSpec((s, LANES), lambda hp, qi: (0, col0 + nb + hp)),
            pl.BlockSpec((s, LANES), lambda hp, qi: (0, col0 + 2 * nb + hp))]


def _pair_spec():
    return pl.BlockSpec((TQ, LANES), lambda hp, qi: (qi, hp))


def _fox_fwd(qkv, cum_col, cum_row):
    s = qkv.shape[0]
    nq = s // TQ

    def body(q_ref, k_ref, v_ref, cc_ref, cr_ref, o_ref, o32_ref, lse_ref):
        hp, qi = pl.program_id(0), pl.program_id(1)
        lane, masks = _head_masks()
        row = lax.broadcasted_iota(jnp.int32, (TQ, TQ), 0)
        col = lax.broadcasted_iota(jnp.int32, (TQ, TQ), 1)
        causal = col <= row
        q2 = q_ref[...] * jnp.asarray(ATTN_SCALE, BF16)
        cc = cc_ref[...]
        outs, lses = [], []
        for e in range(2):
            qm = _pick(masks[e], q2)
            cq = jnp.sum(jnp.where(lane == 2 * hp + e, cc, 0.0), axis=1, keepdims=True)

            def tile(kb, carry, masked, qm=qm, cq=cq, e=e):
                m, l, acc = carry
                off = pl.multiple_of(kb * TQ, TQ)
                sc = _dot_nt(qm, k_ref[pl.ds(off, TQ), :]) + (cq - cr_ref[e:e + 1, pl.ds(off, TQ)])
                if masked:
                    sc = jnp.where(causal, sc, -jnp.inf)
                m_new = jnp.maximum(m, jnp.max(sc, axis=1, keepdims=True))
                p = jnp.exp(sc - m_new)
                corr = jnp.exp(m - m_new)
                l = corr * l + jnp.sum(p, axis=1, keepdims=True)
                acc = corr * acc + jnp.dot(p.astype(BF16), v_ref[pl.ds(off, TQ), :], preferred_element_type=F32)
                return m_new, l, acc

            init = (jnp.full((TQ, 1), -jnp.inf, F32), jnp.zeros((TQ, 1), F32), jnp.zeros((TQ, LANES), F32))
            carry = tile(qi, init, True)
            m, l, acc = lax.fori_loop(0, qi, lambda kb, cr: tile(kb, cr, False), carry)
            outs.append(acc / l)
            lses.append(m + jnp.log(l))
        out = jnp.where(masks[0], outs[0], outs[1])
        o_ref[...] = out.astype(BF16)
        o32_ref[...] = out
        lse_ref[...] = jnp.where(masks[0], lses[0], lses[1])

    return pl.pallas_call(
        body, name="fox_fwd", grid=(N_HEADS // 2, nq),
        in_specs=_qkv_specs(s, 0) + [pl.BlockSpec((TQ, LANES), lambda hp, qi: (qi, 0)),
                                     pl.BlockSpec((None, 2, s), lambda hp, qi: (hp, 0, 0))],
        out_specs=[_pair_spec(), _pair_spec(), _pair_spec()],
        out_shape=[jax.ShapeDtypeStruct((s, ATTN_W), BF16), jax.ShapeDtypeStruct((s, ATTN_W), F32),
                   jax.ShapeDtypeStruct((s, ATTN_W), F32)],
        compiler_params=_params(("parallel", "parallel")),
    )(qkv, qkv, qkv, cum_col, cum_row)


def _fox_bwd(qkv, cum_col, cum_row, o, lse, do):
    s = qkv.shape[0]
    nq = s // TQ

    def body(q_ref, k_ref, v_ref, cc_ref, cr_ref, o_ref, lse_ref, do_ref,
             dq_ref, dk_ref, dv_ref, dcr_ref, dcq_ref, dk_acc, dv_acc):
        hp, qi = pl.program_id(0), pl.program_id(1)

        @pl.when(qi == 0)
        def _():
            dk_acc[...] = jnp.zeros_like(dk_acc)
            dv_acc[...] = jnp.zeros_like(dv_acc)
            dcr_ref[...] = jnp.zeros_like(dcr_ref)

        lane, masks = _head_masks()
        row = lax.broadcasted_iota(jnp.int32, (TQ, TQ), 0)
        col = lax.broadcasted_iota(jnp.int32, (TQ, TQ), 1)
        causal = col <= row
        q2 = q_ref[...] * jnp.asarray(ATTN_SCALE, BF16)
        do2 = do_ref[...]
        prod = do2.astype(F32) * o_ref[...].astype(F32)
        lse2 = lse_ref[...]
        cc = cc_ref[...]
        dqs, rowsums = [], []
        for e in range(2):
            qm = _pick(masks[e], q2)
            dom = _pick(masks[e], do2)
            delta = jnp.sum(jnp.where(masks[e], prod, 0.0), axis=1, keepdims=True)
            lse_e = jnp.max(jnp.where(masks[e], lse2, -jnp.inf), axis=1, keepdims=True)
            cq = jnp.sum(jnp.where(lane == 2 * hp + e, cc, 0.0), axis=1, keepdims=True)

            def tile(kb, carry, masked, qm=qm, dom=dom, delta=delta, lse_e=lse_e, cq=cq, e=e):
                dq, rowsum = carry
                off = pl.multiple_of(kb * TQ, TQ)
                k2 = k_ref[pl.ds(off, TQ), :]
                v2 = v_ref[pl.ds(off, TQ), :]
                sc = _dot_nt(qm, k2) + (cq - cr_ref[e:e + 1, pl.ds(off, TQ)])
                p = jnp.exp(sc - lse_e)
                if masked:
                    p = jnp.where(causal, p, 0.0)
                ds = p * (_dot_nt(dom, v2) - delta)
                dsb = ds.astype(BF16)
                dk_acc[pl.ds(off, TQ), :] += _dot_tn(dsb, qm)
                dv_acc[pl.ds(off, TQ), :] += _dot_tn(p.astype(BF16), dom)
                dcr_ref[e:e + 1, pl.ds(off, TQ)] -= _colsum(ds)
                return (dq + jnp.dot(dsb, k2, preferred_element_type=F32),
                        rowsum + jnp.sum(ds, axis=1, keepdims=True))

            init = (jnp.zeros((TQ, LANES), F32), jnp.zeros((TQ, 1), F32))
            dq, rowsum = tile(qi, lax.fori_loop(0, qi, lambda kb, cr: tile(kb, cr, False), init), True)
            dqs.append(dq)
            rowsums.append(rowsum)
        dq_ref[...] = (jnp.where(masks[0], dqs[0], dqs[1]) * ATTN_SCALE).astype(BF16)
        dcq_ref[...] = jnp.where(masks[0], rowsums[0], rowsums[1])

        @pl.when(qi == nq - 1)
        def _():
            dk_ref[...] = dk_acc[...].astype(BF16)
            dv_ref[...] = dv_acc[...].astype(BF16)

    seq_spec = pl.BlockSpec((s, LANES), lambda hp, qi: (0, hp))
    return pl.pallas_call(
        body, name="fox_bwd", grid=(N_HEADS // 2, nq),
        in_specs=_qkv_specs(s, 0) + [pl.BlockSpec((TQ, LANES), lambda hp, qi: (qi, 0)),
                                     pl.BlockSpec((None, 2, s), lambda hp, qi: (hp, 0, 0)),
                                     _pair_spec(), _pair_spec(), _pair_spec()],
        out_specs=[_pair_spec(), seq_spec, seq_spec, pl.BlockSpec((None, 2, s), lambda hp, qi: (hp, 0, 0)),
                   _pair_spec()],
        out_shape=[jax.ShapeDtypeStruct((s, ATTN_W), BF16)] * 3 + [jax.ShapeDtypeStruct((N_HEADS // 2, 2, s), F32),
                                                                    jax.ShapeDtypeStruct((s, ATTN_W), F32)],
        scratch_shapes=[pltpu.VMEM((s, LANES), F32), pltpu.VMEM((s, LANES), F32)],
        compiler_params=_params(("parallel", "arbitrary")),
    )(qkv, qkv, qkv, cum_col, cum_row, o, lse, do)


def _split2(x):
    hi = x.astype(BF16)
    return hi, (x - hi.astype(F32)).astype(BF16)


def _sb_logits(qm, k2):
    z = _dot_nt(qm, k2)
    soft = jnp.log(1.0 + jnp.exp(-jnp.abs(z)))
    lb = jnp.minimum(z, 0.0) - soft
    return lb, lb - z


def _sb_fwd(qkv):
    s = qkv.shape[0]
    nq = s // TQ
    assert nq <= LANES

    def body(q_ref, k_ref, v_ref, o_ref, r_ref):
        qi = pl.program_id(1)
        lane, masks = _head_masks()
        row = lax.broadcasted_iota(jnp.int32, (TQ, TQ), 0)
        col = lax.broadcasted_iota(jnp.int32, (TQ, TQ), 1)
        strict = col < row
        suffix = jnp.where(row > col, 1.0, 0.0).astype(BF16)
        q2 = q_ref[...] * jnp.asarray(ATTN_SCALE, BF16)
        outs = []
        for e in range(2):
            qm = _pick(masks[e], q2)

            def tile(kb, carry, masked, qm=qm):
                run, saved, acc = carry
                off = pl.multiple_of(kb * TQ, TQ)
                lb, lo = _sb_logits(qm, k_ref[pl.ds(off, TQ), :])
                if masked:
                    lo = jnp.where(strict, lo, 0.0)
                hi, low = _split2(lo)
                rest = jnp.dot(hi, suffix, preferred_element_type=F32) + jnp.dot(low, suffix, preferred_element_type=F32)
                a = jnp.exp(lb + rest + run)
                if masked:
                    a = jnp.where(strict, a, 0.0)
                acc = acc + jnp.dot(a.astype(BF16), v_ref[pl.ds(off, TQ), :], preferred_element_type=F32)
                saved = jnp.where(lane == kb, run, saved)
                return run + jnp.sum(lo, axis=1, keepdims=True), saved, acc

            init = (jnp.zeros((TQ, 1), F32), jnp.zeros((TQ, LANES), F32), jnp.zeros((TQ, LANES), F32))
            carry = tile(qi, init, True)
            _, saved, acc = lax.fori_loop(0, qi, lambda it, cr: tile(qi - 1 - it, cr, False), carry)
            outs.append(acc)
            r_ref[:, e * LANES:(e + 1) * LANES] = saved
        o_ref[...] = jnp.where(masks[0], outs[0], outs[1]).astype(BF16)

    return pl.pallas_call(
        body, name="sb_fwd", grid=(N_HEADS // 2, nq), in_specs=_qkv_specs(s, 3 * ATTN_W // LANES),
        out_specs=[_pair_spec(), pl.BlockSpec((TQ, 2 * LANES), lambda hp, qi: (qi, hp))],
        out_shape=[jax.ShapeDtypeStruct((s, ATTN_W), BF16), jax.ShapeDtypeStruct((s, N_HEADS * LANES), F32)],
        compiler_params=_params(("parallel", "parallel")),
    )(qkv, qkv, qkv)


def _sb_bwd(qkv, saved_run, do):
    s = qkv.shape[0]
    nq = s // TQ

    def body(q_ref, k_ref, v_ref, r_ref, do_ref, dq_ref, dk_ref, dv_ref, dk_acc, dv_acc):
        qi = pl.program_id(1)

        @pl.when(qi == 0)
        def _():
            dk_acc[...] = jnp.zeros_like(dk_acc)
            dv_acc[...] = jnp.zeros_like(dv_acc)

        lane, masks = _head_masks()
        row = lax.broadcasted_iota(jnp.int32, (TQ, TQ), 0)
        col = lax.broadcasted_iota(jnp.int32, (TQ, TQ), 1)
        strict = col < row
        suffix = jnp.where(row > col, 1.0, 0.0).astype(BF16)
        prefix = jnp.where(row < col, 1.0, 0.0).astype(BF16)
        q2 = q_ref[...] * jnp.asarray(ATTN_SCALE, BF16)
        do2 = do_ref[...]
        dqs = []
        for e in range(2):
            qm = _pick(masks[e], q2)
            dom = _pick(masks[e], do2)
            saved = r_ref[:, e * LANES:(e + 1) * LANES]

            def tile(kb, carry, masked, qm=qm, dom=dom, saved=saved):
                gsum, dq = carry
                off = pl.multiple_of(kb * TQ, TQ)
                k2 = k_ref[pl.ds(off, TQ), :]
                v2 = v_ref[pl.ds(off, TQ), :]
                lb, lo = _sb_logits(qm, k2)
                beta = jnp.exp(lb)
                if masked:
                    lo = jnp.where(strict, lo, 0.0)
                hi, low = _split2(lo)
                rest = jnp.dot(hi, suffix, preferred_element_type=F32) + jnp.dot(low, suffix, preferred_element_type=F32)
                run = jnp.sum(jnp.where(lane == kb, saved, 0.0), axis=1, keepdims=True)
                a = jnp.exp(lb + rest + run)
                if masked:
                    a = jnp.where(strict, a, 0.0)
                g = a * _dot_nt(dom, v2)
                ghi, glo = _split2(g)
                before = jnp.dot(ghi, prefix, preferred_element_type=F32) + jnp.dot(glo, prefix, preferred_element_type=F32)
                dz = g * (1.0 - beta) - beta * (before + gsum)
                if masked:
                    dz = jnp.where(strict, dz, 0.0)
                dzb = dz.astype(BF16)
                dk_acc[pl.ds(off, TQ), :] += _dot_tn(dzb, qm)
                dv_acc[pl.ds(off, TQ), :] += _dot_tn(a.astype(BF16), dom)
                return gsum + jnp.sum(g, axis=1, keepdims=True), dq + jnp.dot(dzb, k2, preferred_element_type=F32)

            init = (jnp.zeros((TQ, 1), F32), jnp.zeros((TQ, LANES), F32))
            carry = lax.fori_loop(0, qi, lambda kb, cr: tile(kb, cr, False), init)
            dqs.append(tile(qi, carry, True)[1])
        dq_ref[...] = (jnp.where(masks[0], dqs[0], dqs[1]) * ATTN_SCALE).astype(BF16)

        @pl.when(qi == nq - 1)
        def _():
            dk_ref[...] = dk_acc[...].astype(BF16)
            dv_ref[...] = dv_acc[...].astype(BF16)

    seq_spec = pl.BlockSpec((s, LANES), lambda hp, qi: (0, hp))
    return pl.pallas_call(
        body, name="sb_bwd", grid=(N_HEADS // 2, nq),
        in_specs=_qkv_specs(s, 3 * ATTN_W // LANES) + [pl.BlockSpec((TQ, 2 * LANES), lambda hp, qi: (qi, hp)),
                                                        _pair_spec()],
        out_specs=[_pair_spec(), seq_spec, seq_spec],
        out_shape=[jax.ShapeDtypeStruct((s, ATTN_W), BF16)] * 3,
        scratch_shapes=[pltpu.VMEM((s, LANES), F32), pltpu.VMEM((s, LANES), F32)],
        compiler_params=_params(("parallel", "arbitrary")),
    )(qkv, qkv, qkv, saved_run, do)


CONV_TR = 256


def _shift_down(x, halo, n):
    rolled = pltpu.roll(x, n, 0)
    rid = lax.broadcasted_iota(jnp.int32, x.shape, 0)
    for j in range(n):
        rolled = jnp.where(rid == j, halo[8 - n + j:8 - n + j + 1, :], rolled)
    return rolled


def _shift_up(x, halo, n):
    rows = x.shape[0]
    rolled = pltpu.roll(x, rows - n, 0)
    rid = lax.broadcasted_iota(jnp.int32, x.shape, 0)
    for j in range(n):
        rolled = jnp.where(rid == rows - n + j, halo[j:j + 1, :], rolled)
    return rolled


def _conv_fwd_block(x, halo, w, b):
    return b + _shift_down(x, halo, 2) * w[0:1, :] + _shift_down(x, halo, 1) * w[1:2, :] + x * w[2:3, :]


def _conv_specs(tr, s):
    pair = 2 * FF_HALF
    blk = pl.BlockSpec((tr, pair), lambda j, i: (i, j))
    prev = pl.BlockSpec((8, pair), lambda j, i: (jnp.maximum(i * (tr // 8) - 1, 0), j))
    nxt = pl.BlockSpec((8, pair), lambda j, i: (jnp.minimum((i + 1) * (tr // 8), s // 8 - 1), j))
    return blk, prev, nxt


def _conv_gate_fwd(hpre, conv_w, conv_b):
    s = hpre.shape[0]
    tr = min(CONV_TR, s)
    blk, prev, _ = _conv_specs(tr, s)

    def body(x_ref, halo_ref, w_ref, b_ref, a_ref):
        i = pl.program_id(1)
        halo = jnp.where(i > 0, halo_ref[...], 0.0)
        h = _conv_fwd_block(x_ref[...], halo, w_ref[...], b_ref[...])
        hg, hv = h[:, :FF_HALF], h[:, FF_HALF:]
        a_ref[...] = (hg * _sigmoid(hg) * hv).astype(BF16)

    return pl.pallas_call(
        body, name="conv_gate_fwd", grid=(2, s // tr),
        in_specs=[blk, prev, pl.BlockSpec((3, 2 * FF_HALF), lambda j, i: (0, j)),
                  pl.BlockSpec((1, 2 * FF_HALF), lambda j, i: (0, j))],
        out_specs=pl.BlockSpec((tr, FF_HALF), lambda j, i: (i, j)),
        out_shape=jax.ShapeDtypeStruct((s, D_FF), BF16),
        compiler_params=_params(("parallel", "parallel")),
    )(hpre, hpre, conv_w, conv_b)


def _conv_gate_bwd(hpre, da, conv_w, conv_b):
    s = hpre.shape[0]
    tr = min(CONV_TR, s)
    blk, prev, _ = _conv_specs(tr, s)

    def body(x_ref, halo_ref, da_ref, w_ref, b_ref, dh_ref, db_ref, dw_ref):
        i = pl.program_id(1)
        halo = jnp.where(i > 0, halo_ref[...], 0.0)
        x = x_ref[...]
        h = _conv_fwd_block(x, halo, w_ref[...], b_ref[...])
        hg, hv = h[:, :FF_HALF], h[:, FF_HALF:]
        da_blk = da_ref[...]
        sg = _sigmoid(hg)
        dhg = da_blk * hv * (sg * (1.0 + hg * (1.0 - sg)))
        dhv = da_blk * (hg * sg)
        dh_ref[:, :FF_HALF] = dhg
        dh_ref[:, FF_HALF:] = dhv
        x2, x1 = _shift_down(x, halo, 2), _shift_down(x, halo, 1)
        parts = []
        for lo, dpart in ((0, dhg), (FF_HALF, dhv)):
            cols = slice(lo, lo + FF_HALF)
            parts.append((cols, _colsum(dpart), _colsum(dpart * x2[:, cols]), _colsum(dpart * x1[:, cols]),
                          _colsum(dpart * x[:, cols])))

        @pl.when(i == 0)
        def _():
            for cols, db, dw0, dw1, dw2 in parts:
                db_ref[:, cols] = db
                dw_ref[0:1, cols] = dw0
                dw_ref[1:2, cols] = dw1
                dw_ref[2:3, cols] = dw2

        @pl.when(i > 0)
        def _():
            for cols, db, dw0, dw1, dw2 in parts:
                db_ref[:, cols] += db
                dw_ref[0:1, cols] += dw0
                dw_ref[1:2, cols] += dw1
                dw_ref[2:3, cols] += dw2

    pair = 2 * FF_HALF
    return pl.pallas_call(
        body, name="conv_gate_bwd", grid=(2, s // tr),
        in_specs=[blk, prev, pl.BlockSpec((tr, FF_HALF), lambda j, i: (i, j)),
                  pl.BlockSpec((3, pair), lambda j, i: (0, j)), pl.BlockSpec((1, pair), lambda j, i: (0, j))],
        out_specs=[blk, pl.BlockSpec((1, pair), lambda j, i: (0, j)), pl.BlockSpec((3, pair), lambda j, i: (0, j))],
        out_shape=[jax.ShapeDtypeStruct((s, 2 * D_FF), F32), jax.ShapeDtypeStruct((1, 2 * D_FF), F32),
                   jax.ShapeDtypeStruct((3, 2 * D_FF), F32)],
        compiler_params=_params(("parallel", "arbitrary")),
    )(hpre, hpre, da, conv_w, conv_b)


def _conv_input_bwd(dh, conv_w):
    s = dh.shape[0]
    tr = min(CONV_TR, s)
    blk, _, nxt = _conv_specs(tr, s)
    nblk = s // tr

    def body(x_ref, halo_ref, w_ref, o_ref):
        i = pl.program_id(1)
        halo = jnp.where(i < nblk - 1, halo_ref[...], 0.0)
        x, w = x_ref[...], w_ref[...]
        o_ref[...] = (x * w[2:3, :] + _shift_up(x, halo, 1) * w[1:2, :] + _shift_up(x, halo, 2) * w[0:1, :]).astype(BF16)

    return pl.pallas_call(
        body, name="conv_input_bwd", grid=(2, nblk),
        in_specs=[blk, nxt, pl.BlockSpec((3, 2 * FF_HALF), lambda j, i: (0, j))], out_specs=blk,
        out_shape=jax.ShapeDtypeStruct((s, 2 * D_FF), BF16),
        compiler_params=_params(("parallel", "parallel")),
    )(dh, dh, conv_w)


def _adamw_math(w, g, m, v):
    m = ADAM_B1 * m + (1.0 - ADAM_B1) * g
    v = ADAM_B2 * v + (1.0 - ADAM_B2) * (g * g)
    m_hat = m / (1.0 - ADAM_B1 ** ADAM_STEP)
    v_hat = v / (1.0 - ADAM_B2 ** ADAM_STEP)
    delta = -ADAM_LR * (m_hat / (jnp.sqrt(v_hat) + ADAM_EPS) + ADAM_WD * w)
    return delta, m, v


def _adamw(name, g8, w, m, v):
    r, c = w.shape
    tr = _row_tile(r, c)

    def body(g_ref, w_ref, m_ref, v_ref, go_ref, d_ref, mo_ref, vo_ref):
        g = g_ref[0].astype(F32)
        for d in range(1, N_DEV):
            g = g + g_ref[d].astype(F32)
        delta, mn, vn = _adamw_math(w_ref[...], g, m_ref[...], v_ref[...])
        go_ref[...] = g
        d_ref[...] = delta
        mo_ref[...] = mn
        vo_ref[...] = vn

    spec = pl.BlockSpec((tr, c), lambda i: (i, 0))
    return pl.pallas_call(
        body, name=name, grid=(r // tr,),
        in_specs=[pl.BlockSpec((N_DEV, tr, c), lambda i: (0, i, 0)), spec, spec, spec], out_specs=[spec] * 4,
        out_shape=[jax.ShapeDtypeStruct((r, c), F32)] * 4, compiler_params=_params(("parallel",)),
    )(g8, w, m, v)


def _adamw_ada(c_t, dmod, w, m, v):
    r, c = w.shape
    tr = _row_tile(r, c)

    def body(ct_ref, dm_ref, w_ref, m_ref, v_ref, go_ref, d_ref, mo_ref, vo_ref):
        ct, dm = ct_ref[...], dm_ref[...]
        g = ct[:, 0:1] * dm[0:1, :]
        for b in range(1, N_DEV):
            g = g + ct[:, b:b + 1] * dm[b:b + 1, :]
        delta, mn, vn = _adamw_math(w_ref[...], g, m_ref[...], v_ref[...])
        go_ref[...] = g
        d_ref[...] = delta
        mo_ref[...] = mn
        vo_ref[...] = vn

    spec = pl.BlockSpec((tr, c), lambda i: (i, 0))
    return pl.pallas_call(
        body, name="adamw_w_ada", grid=(r // tr,),
        in_specs=[pl.BlockSpec((tr, N_DEV), lambda i: (i, 0)), pl.BlockSpec((N_DEV, c), lambda i: (0, 0)),
                  spec, spec, spec],
        out_specs=[spec] * 4, out_shape=[jax.ShapeDtypeStruct((r, c), F32)] * 4,
        compiler_params=_params(("parallel",)),
    )(c_t, dmod, w, m, v)


def _cols_from_slots(g):
    n, r, c = g.shape
    return jnp.transpose(g, (1, 0, 2)).reshape(r, n * c)


def _cols_to_slots(w):
    r, c = w.shape
    return jnp.transpose(w.reshape(r, N_DEV, c // N_DEV), (1, 0, 2))


def _pair_cols(w):
    g0, g1 = w[..., 0:FF_HALF], w[..., FF_HALF:D_FF]
    v0, v1 = w[..., D_FF:D_FF + FF_HALF], w[..., D_FF + FF_HALF:]
    return jnp.concatenate([g0, v0, g1, v1], axis=-1)


def _unpair_cols(w):
    g0, v0 = w[..., 0:FF_HALF], w[..., FF_HALF:D_FF]
    g1, v1 = w[..., D_FF:D_FF + FF_HALF], w[..., D_FF + FF_HALF:]
    return jnp.concatenate([g0, g1, v0, v1], axis=-1)


def _row(v):
    return v.reshape(1, -1)


def kernel(x, c, w_ada, b_ada, w_in, b_forget, w_fox_proj, w_sb_proj, w_o, ln1_g, ln1_b, w_up, conv_w, conv_b, w_down, ln2_g, ln2_b, loss_target, m_w_ada, m_b_ada, m_w_in, m_b_forget, m_w_fox_proj, m_w_sb_proj, m_w_o, m_ln1_g, m_ln1_b, m_w_up, m_conv_w, m_conv_b, m_w_down, m_ln2_g, m_ln2_b, v_w_ada, v_b_ada, v_w_in, v_b_forget, v_w_fox_proj, v_w_sb_proj, v_w_o, v_ln1_g, v_ln1_b, v_w_up, v_conv_w, v_conv_b, v_w_down, v_ln2_g, v_ln2_b):
    s = x.shape[1]
    me = 4 * lax.axis_index("x") + 2 * lax.axis_index("y") + lax.axis_index("c")
    x2 = x.reshape(s, D_MODEL)
    tgt = loss_target.reshape(s, D_MODEL)

    b_ada_loc = lax.dynamic_slice(b_ada, (me * ADA_SHARD,), (ADA_SHARD,)).reshape(1, ADA_SHARD)
    c_all, mod = _mod_exchange(c, w_ada, b_ada_loc)
    mod = mod.reshape(N_MOD, 1, D_MODEL)
    sh1, sc1, gt1, sh2, sc2, gt2 = [mod[i] for i in range(N_MOD)]

    g_in, g_fox, g_sb, g_o, g_up, g_down, g_cw = _exchange(
        "ag_weights",
        [w_in.astype(BF16), w_fox_proj.astype(BF16), w_sb_proj.astype(BF16), w_o.astype(BF16), w_up.astype(BF16),
         w_down.astype(BF16), conv_w], scatter=False)
    w_in_f = _cols_from_slots(g_in)
    w_proj = jnp.concatenate(
        [w_in_f[:, 0:1536], w_in_f[:, 1544:3080], w_in_f[:, 3080:5128], w_in_f[:, 1536:1544],
         jnp.zeros((D_MODEL, W_PROJ - 5128), BF16)], axis=1)
    w_qkv, w_gates, w_f = w_proj[:, :W_QKV], w_proj[:, W_QKV:W_QKV + W_GATES], w_proj[:, W_QKV + W_GATES:W_QKV + W_GATES + W_F]
    w_fox_f = _cols_from_slots(g_fox)
    w_sb_f = _cols_from_slots(g_sb)
    w_o_f = g_o.reshape(D_MODEL, D_MODEL)
    w_up_p = _pair_cols(_cols_from_slots(g_up))
    w_down_f = g_down.reshape(D_FF, D_MODEL)
    conv_w_p = _pair_cols(_cols_from_slots(g_cw))
    conv_b_p = _pair_cols(_row(conv_b))
    b_f_pad = jnp.pad(_row(b_forget), ((0, 0), (0, LANES - N_HEADS)))

    (u1,) = _rowwise("modulate1", lambda xb, sc, sh: (xb * (1.0 + sc) + sh,),
                     [(x2, D_MODEL, 0)], [sc1, sh1], [(D_MODEL, BF16)], tr=512)
    qkv = _mm(u1, w_qkv, name="mm_qkv", out_dtype=BF16)
    gates = _mm(u1, w_gates, name="mm_gates")
    f_raw = _mm(u1, w_f, name="mm_forget")
    cum_col = _forget_cumsum(f_raw, b_f_pad)
    cum_row = jnp.transpose(cum_col[:, :N_HEADS]).reshape(N_HEADS // 2, 2, s)
    y_fox, y_fox32, lse = _fox_fwd(qkv, cum_col, cum_row)
    y_sb, sb_run = _sb_fwd(qkv)
    pf = _mm(y_fox, w_fox_f, name="mm_fox_proj")
    ps = _mm(y_sb, w_sb_f, name="mm_sb_proj")
    (merged,) = _rowwise("gate_merge", lambda ga, gb, a, b: (_sigmoid(ga) * a + _sigmoid(gb) * b,),
                         [(gates, D_MODEL, 0), (gates, D_MODEL, 1), (pf, D_MODEL, 0), (ps, D_MODEL, 0)], [],
                         [(D_MODEL, BF16)])
    attn_out = _mm(merged, w_o_f, name="mm_w_o")

    def ln_fwd(xb, fb, gt, g, b):
        xhat, _ = _ln_stats(ALPHA * xb + (1.0 + gt) * fb)
        return xhat * g + b

    def ln1_mod(xb, fb, gt, g, b, sc, sh):
        y = ln_fwd(xb, fb, gt, g, b)
        return y, y * (1.0 + sc) + sh

    x1, u2 = _rowwise("ln1_modulate2", ln1_mod, [(x2, D_MODEL, 0), (attn_out, D_MODEL, 0)],
                      [gt1, _row(ln1_g), _row(ln1_b), sc2, sh2], [(D_MODEL, F32), (D_MODEL, BF16)])

    hpre = _mm(u2, w_up_p, name="mm_w_up", tn=1408)
    act = _conv_gate_fwd(hpre, conv_w_p, conv_b_p)
    ffn_out = _mm(act, w_down_f, name="mm_w_down", tk=1408)

    def ln2_bwd(xb, fb, tb, gt, g, b):
        xhat, rstd = _ln_stats(ALPHA * xb + (1.0 + gt) * fb)
        err = (xhat * g + b) - tb
        dy = err * (1.0 / D_MODEL)
        dr = _ln_bwd(dy, xhat, rstd, g)
        return (dr * (1.0 + gt), ALPHA * dr,
                _colsum(err * err), _colsum(dy * xhat), _colsum(dy), _colsum(dr * fb))

    dffn, dx1_res, sq_err, d_ln2_g, d_ln2_b, d_gt2 = _rowwise(
        "ln2_bwd", ln2_bwd, [(x1, D_MODEL, 0), (ffn_out, D_MODEL, 0), (tgt, D_MODEL, 0)],
        [gt2, _row(ln2_g), _row(ln2_b)], [(D_MODEL, BF16), (D_MODEL, F32)], sums=[D_MODEL] * 4)
    loss = lax.psum(0.5 * jnp.sum(sq_err) / D_MODEL, ("x", "y", "c"))

    d_w_down = _mm(act, dffn, name="mm_d_w_down", ta=True, tm=1408)
    d_act = _mm(dffn, w_down_f, name="mm_d_act", tb=True, tn=1408)
    dh, d_conv_b_p, d_conv_w_p = _conv_gate_bwd(hpre, d_act, conv_w_p, conv_b_p)
    dhpre = _conv_input_bwd(dh, conv_w_p)
    d_w_up_p = _mm(u2, dhpre, name="mm_d_w_up", ta=True, tn=1408)
    du2 = _mm(dhpre, w_up_p, name="mm_d_u2", tb=True, tk=1408)

    def ln1_bwd(du, dres, x1b, xb, fb, sc, gt, g):
        dx1 = dres + du * (1.0 + sc)
        xhat, rstd = _ln_stats(ALPHA * xb + (1.0 + gt) * fb)
        dr = _ln_bwd(dx1, xhat, rstd, g)
        return (dr * (1.0 + gt), ALPHA * dr,
                _colsum(du * x1b), _colsum(du), _colsum(dx1 * xhat), _colsum(dx1), _colsum(dr * fb))

    d_attn, dx_res, d_sc2, d_sh2, d_ln1_g, d_ln1_b, d_gt1 = _rowwise(
        "ln1_bwd", ln1_bwd,
        [(du2, D_MODEL, 0), (dx1_res, D_MODEL, 0), (x1, D_MODEL, 0), (x2, D_MODEL, 0), (attn_out, D_MODEL, 0)],
        [sc2, gt1, _row(ln1_g)], [(D_MODEL, BF16), (D_MODEL, F32)], sums=[D_MODEL] * 5)

    d_w_o = _mm(merged, d_attn, name="mm_d_w_o", ta=True)
    d_merged = _mm(d_attn, w_o_f, name="mm_d_merged", tb=True)

    def merge_bwd(dm, ga, gb, a, b):
        sa, sb = _sigmoid(ga), _sigmoid(gb)
        return dm * a * sa * (1.0 - sa), dm * b * sb * (1.0 - sb), dm * sa, dm * sb

    d_ga, d_gb, d_pf, d_ps = _rowwise(
        "gate_merge_bwd", merge_bwd,
        [(d_merged, D_MODEL, 0), (gates, D_MODEL, 0), (gates, D_MODEL, 1), (pf, D_MODEL, 0), (ps, D_MODEL, 0)], [],
        [(D_MODEL, BF16)] * 4)
    d_w_fox = _mm(y_fox, d_pf, name="mm_d_w_fox", ta=True)
    d_w_sb = _mm(y_sb, d_ps, name="mm_d_w_sb", ta=True)
    d_y_fox = _mm(d_pf, w_fox_f, name="mm_d_y_fox", tb=True, out_dtype=BF16)
    d_y_sb = _mm(d_ps, w_sb_f, name="mm_d_y_sb", tb=True, out_dtype=BF16)
    dq_a, dk_a, dv_a, d_cum_row, d_cum_q = _fox_bwd(qkv, cum_col, cum_row, y_fox32, lse, d_y_fox)
    dq_b, dk_b, dv_b = _sb_bwd(qkv, sb_run, d_y_sb)
    d_cum = jnp.transpose(d_cum_row.reshape(N_HEADS, s)) + d_cum_q[:, ::HEAD_DIM]
    d_cum = jnp.pad(d_cum, ((0, 0), (0, LANES - N_HEADS)))
    d_f, d_b_forget = _forget_bwd(d_cum, f_raw, b_f_pad)
    d_proj = jnp.concatenate([dq_a, dk_a, dv_a, dq_b, dk_b, dv_b, d_ga, d_gb, d_f,
                              jnp.zeros((s, W_PROJ - W_QKV - W_GATES - W_F), BF16)], axis=1)
    d_w_proj = _mm(u1, d_proj, name="mm_d_w_in", ta=True, tn=896)
    du1 = _mm(d_proj, w_proj, name="mm_d_u1", tb=True, tk=896)

    def x_bwd(du, dres, xb, sc):
        return dres + du * (1.0 + sc), _colsum(du * xb), _colsum(du)

    grad_x, d_sc1, d_sh1 = _rowwise("x_bwd", x_bwd, [(du1, D_MODEL, 0), (dx_res, D_MODEL, 0), (x2, D_MODEL, 0)],
                                    [sc1], [(D_MODEL, F32)], sums=[D_MODEL] * 2, tr=512)

    d_conv_b = _unpair_cols(d_conv_b_p)
    d_conv_w = _unpair_cols(d_conv_w_p)
    n_rep = N_MOD * D_MODEL + LANES + 4 * D_MODEL + 2 * D_FF
    small = jnp.concatenate(
        [d_sh1, d_sc1, d_gt1, d_sh2, d_sc2, d_gt2, d_b_forget, d_ln1_g, d_ln1_b, d_ln2_g, d_ln2_b, d_conv_b,
         d_conv_w.reshape(1, 6 * D_FF)], axis=1)
    n_small = small.shape[1] // LANES
    small = jnp.pad(small.reshape(n_small, LANES), ((0, 264 - n_small), (0, 0)))
    (small_all,) = _exchange("ag_small_grads", [small], scatter=False)
    rep8 = small_all[:, :n_rep // LANES, :]
    cw8 = small_all[:, n_rep // LANES:n_small, :].reshape(N_DEV, 3, 2 * D_FF)
    cw8 = lax.dynamic_slice(cw8, (0, 0, me * UP_SHARD), (N_DEV, 3, UP_SHARD))
    dmod8 = small_all[:, :N_MOD * D_MODEL // LANES, :].reshape(N_DEV, N_MOD * D_MODEL)
    dmod_loc = lax.dynamic_slice(dmod8, (0, me * ADA_SHARD), (N_DEV, ADA_SHARD))

    def pack_rep(b_a, b_f, g1, b1, g2, b2, cb):
        flat = jnp.concatenate([b_a, jnp.pad(b_f, (0, LANES - N_HEADS)), g1, b1, g2, b2, cb])
        return flat.reshape(n_rep // LANES, LANES)

    rep = _adamw("adamw_small", rep8, pack_rep(b_ada, b_forget, ln1_g, ln1_b, ln2_g, ln2_b, conv_b),
                 pack_rep(m_b_ada, m_b_forget, m_ln1_g, m_ln1_b, m_ln2_g, m_ln2_b, m_conv_b),
                 pack_rep(v_b_ada, v_b_forget, v_ln1_g, v_ln1_b, v_ln2_g, v_ln2_b, v_conv_b))

    def unpack_rep(p):
        flat = p.reshape(-1)
        o = N_MOD * D_MODEL
        return {"b_ada": flat[:o], "b_forget": flat[o:o + N_HEADS],
                "ln1_g": flat[o + 128:o + 1152], "ln1_b": flat[o + 1152:o + 2176],
                "ln2_g": flat[o + 2176:o + 3200], "ln2_b": flat[o + 3200:o + 4224], "conv_b": flat[o + 4224:]}

    rep = [unpack_rep(p) for p in rep]
    r_conv_w = _adamw("adamw_conv_w", cw8, conv_w, m_conv_w, v_conv_w)
    r_ada = _adamw_ada(jnp.transpose(c_all.reshape(N_DEV, D_MODEL)), dmod_loc, w_ada, m_w_ada, v_w_ada)

    d_w_in_f = jnp.concatenate([d_w_proj[:, 0:1536], d_w_proj[:, 5120:5128], d_w_proj[:, 1536:3072],
                                d_w_proj[:, 3072:5120]], axis=1)
    slots = _exchange(
        "rs_weight_grads",
        [_cols_to_slots(d_w_in_f), _cols_to_slots(d_w_fox), _cols_to_slots(d_w_sb),
         d_w_o.reshape(N_DEV, D_MODEL // N_DEV, D_MODEL), _cols_to_slots(_unpair_cols(d_w_up_p)),
         d_w_down.reshape(N_DEV, D_FF // N_DEV, D_MODEL)], scatter=True)
    r_in = _adamw("adamw_w_in", slots[0], w_in, m_w_in, v_w_in)
    r_fox = _adamw("adamw_w_fox", slots[1], w_fox_proj, m_w_fox_proj, v_w_fox_proj)
    r_sb = _adamw("adamw_w_sb", slots[2], w_sb_proj, m_w_sb_proj, v_w_sb_proj)
    r_o = _adamw("adamw_w_o", slots[3], w_o, m_w_o, v_w_o)
    r_up = _adamw("adamw_w_up", slots[4], w_up, m_w_up, v_w_up)
    r_down = _adamw("adamw_w_down", slots[5], w_down, m_w_down, v_w_down)

    def leaf(i):
        return [r_ada[i], rep[i]["b_ada"], r_in[i], rep[i]["b_forget"], r_fox[i], r_sb[i], r_o[i], rep[i]["ln1_g"],
                rep[i]["ln1_b"], r_up[i], r_conv_w[i], rep[i]["conv_b"], r_down[i], rep[i]["ln2_g"], rep[i]["ln2_b"]]

    return (loss, grad_x.reshape(1, s, D_MODEL), *leaf(0), *leaf(1), *leaf(2), *leaf(3))
```

```python
import functools

import jax
import jax.numpy as jnp
from jax import lax
from jax.experimental import pallas as pl
from jax.experimental.pallas import tpu as pltpu

F32 = jnp.float32
BF16 = jnp.bfloat16
MESH = pl.DeviceIdType.MESH
ANY = pl.BlockSpec(memory_space=pl.ANY)

N_DEV = 8
D_MODEL = 1024
HEAD_DIM = 64
N_HEADS = 8
ATTN_W = N_HEADS * HEAD_DIM
D_FF = 2816
FF_HALF = D_FF // 2
N_MOD = 6
ADA_SHARD = N_MOD * D_MODEL // N_DEV
IN_SHARD = 641
UP_SHARD = 704
ATTN_SCALE = HEAD_DIM ** -0.5
ALPHA = 2.0 ** 0.25
LN_EPS = 1e-5
LANES = 128
TQ = 512
SCAN_W = 256
VMEM_LIMIT = 56 * 1024 * 1024

ADAM_LR, ADAM_B1, ADAM_B2, ADAM_EPS, ADAM_WD, ADAM_STEP = 0.001, 0.9, 0.999, 1e-08, 0.01, 10

W_QKV, W_GATES, W_F = 3072, 2048, 128
W_PROJ = 5376


def _params(sem=None):
    return pltpu.CompilerParams(dimension_semantics=sem, vmem_limit_bytes=VMEM_LIMIT)


def _tile(n, cap):
    if n <= cap:
        return n
    best = None
    for t in range(LANES, cap + 1, LANES):
        if n % t == 0:
            best = t
    assert best is not None, (n, cap)
    return best


def _row_tile(r, width, budget=128 * 1024):
    if r * width <= budget or r % 8:
        return r
    best = 8
    for t in range(8, r + 1, 8):
        if r % t == 0 and t * width <= budget:
            best = t
    return best


def _mm(a, b, *, name, ta=False, tb=False, out_dtype=F32, tm=1024, tn=1024, tk=1024):
    m, k = (a.shape[1], a.shape[0]) if ta else a.shape
    n = b.shape[0] if tb else b.shape[1]
    assert (b.shape[1] if tb else b.shape[0]) == k
    tm, tn, tk = _tile(m, tm), _tile(n, tn), _tile(k, tk)
    nk = k // tk
    a_spec = pl.BlockSpec((tk, tm), lambda i, j, l: (l, i)) if ta else pl.BlockSpec((tm, tk), lambda i, j, l: (i, l))
    b_spec = pl.BlockSpec((tn, tk), lambda i, j, l: (j, l)) if tb else pl.BlockSpec((tk, tn), lambda i, j, l: (l, j))
    dims = (((0,) if ta else (1,), (1,) if tb else (0,)), ((), ()))

    def body(a_ref, b_ref, o_ref, *acc):
        p = lax.dot_general(a_ref[...].astype(BF16), b_ref[...].astype(BF16), dims, preferred_element_type=F32)
        if nk == 1:
            o_ref[...] = p.astype(out_dtype)
            return
        acc_ref = acc[0]
        step = pl.program_id(2)

        @pl.when(step == 0)
        def _():
            acc_ref[...] = p

        @pl.when(step > 0)
        def _():
            acc_ref[...] += p

        @pl.when(step == nk - 1)
        def _():
            o_ref[...] = acc_ref[...].astype(out_dtype)

    return pl.pallas_call(
        body, name=name, grid=(m // tm, n // tn, nk),
        in_specs=[a_spec, b_spec], out_specs=pl.BlockSpec((tm, tn), lambda i, j, l: (i, j)),
        out_shape=jax.ShapeDtypeStruct((m, n), out_dtype),
        scratch_shapes=[] if nk == 1 else [pltpu.VMEM((tm, tn), F32)],
        compiler_params=_params(("parallel", "parallel", "arbitrary")),
    )(a, b)


def _rowwise(name, fn, rows, vecs, outs, sums=(), tr=256):
    s = rows[0][0].shape[0]
    tr = min(tr, s)
    nr, nv, no = len(rows), len(vecs), len(outs)

    def body(*refs):
        vals = [r[...] for r in refs[:nr + nv]]
        res = fn(*vals)
        for o_ref, val in zip(refs[nr + nv:nr + nv + no], res[:no]):
            o_ref[...] = val.astype(o_ref.dtype)
        step = pl.program_id(0)
        for s_ref, val in zip(refs[nr + nv + no:], res[no:]):
            @pl.when(step == 0)
            def _(s_ref=s_ref, val=val):
                s_ref[...] = val

            @pl.when(step > 0)
            def _(s_ref=s_ref, val=val):
                s_ref[...] += val

    in_specs = [pl.BlockSpec((tr, w), functools.partial(lambda i, cb: (i, cb), cb=cb)) for _, w, cb in rows]
    in_specs += [pl.BlockSpec(v.shape, lambda i: (0, 0)) for v in vecs]
    out_specs = [pl.BlockSpec((tr, w), lambda i: (i, 0)) for w, _ in outs]
    out_specs += [pl.BlockSpec((1, w), lambda i: (0, 0)) for w in sums]
    out_shape = [jax.ShapeDtypeStruct((s, w), dt) for w, dt in outs]
    out_shape += [jax.ShapeDtypeStruct((1, w), F32) for w in sums]
    return pl.pallas_call(
        body, name=name, grid=(s // tr,), in_specs=in_specs, out_specs=out_specs, out_shape=out_shape,
        compiler_params=_params(("arbitrary",) if sums else ("parallel",)),
    )(*[r[0] for r in rows], *vecs)


def _colsum(x):
    return jnp.sum(x, axis=0, keepdims=True)


def _sigmoid(x):
    return 1.0 / (1.0 + jnp.exp(-x))


def _log_sigmoid(x):
    return jnp.minimum(x, 0.0) - jnp.log(1.0 + jnp.exp(-jnp.abs(x)))


def _ln_stats(r):
    mu = jnp.mean(r, axis=-1, keepdims=True)
    xc = r - mu
    var = jnp.mean(xc * xc, axis=-1, keepdims=True)
    rstd = lax.rsqrt(var + LN_EPS)
    return xc * rstd, rstd


def _ln_bwd(dy, xhat, rstd, g):
    dxh = dy * g
    m1 = jnp.mean(dxh, axis=-1, keepdims=True)
    m2 = jnp.mean(dxh * xhat, axis=-1, keepdims=True)
    return rstd * (dxh - m1 - xhat * m2)


def _me():
    x, y, c = lax.axis_index("x"), lax.axis_index("y"), lax.axis_index("c")
    return x, y, c, 4 * x + 2 * y + c


def _peer(r):
    x, y, c, _ = _me()
    px = 1 - x if r & 4 else x
    py = 1 - y if r & 2 else y
    pc = 1 - c if r & 1 else c
    return (px, py, pc), 4 * px + 2 * py + pc


def _exchange(name, arrays, scatter):
    n = len(arrays)
    out_shape = [jax.ShapeDtypeStruct(a.shape if scatter else (N_DEV,) + a.shape, a.dtype) for a in arrays]

    def body(*refs):
        ins, outs = refs[:n], refs[n:2 * n]
        send_sems, recv_sems, local_sems = refs[2 * n:]
        me = _me()[3]
        local = []
        for a in range(n):
            cp = pltpu.make_async_copy(ins[a].at[me] if scatter else ins[a], outs[a].at[me], local_sems.at[a])
            cp.start()
            local.append(cp)
        sent = []
        for r in range(1, N_DEV):
            peer, pidx = _peer(r)
            for a in range(n):
                cp = pltpu.make_async_remote_copy(
                    src_ref=ins[a].at[pidx] if scatter else ins[a], dst_ref=outs[a].at[me],
                    send_sem=send_sems.at[a, r - 1], recv_sem=recv_sems.at[a, r - 1],
                    device_id=peer, device_id_type=MESH)
                cp.start()
                sent.append(cp)
        for r in range(1, N_DEV):
            peer, pidx = _peer(r)
            for a in range(n):
                pltpu.make_async_remote_copy(
                    src_ref=ins[a].at[me] if scatter else ins[a], dst_ref=outs[a].at[pidx],
                    send_sem=send_sems.at[a, r - 1], recv_sem=recv_sems.at[a, r - 1],
                    device_id=peer, device_id_type=MESH).wait_recv()
        for cp in sent:
            cp.wait_send()
        for cp in local:
            cp.wait()

    return pl.pallas_call(
        body, name=name, in_specs=[ANY] * n, out_specs=[ANY] * n, out_shape=out_shape,
        scratch_shapes=[pltpu.SemaphoreType.DMA((n, N_DEV - 1)), pltpu.SemaphoreType.DMA((n, N_DEV - 1)),
                        pltpu.SemaphoreType.DMA((n,))],
    )(*arrays)


def _mod_exchange(c_row, w_ada, b_ada_loc):
    def body(c_ref, w_ref, b_ref, call_ref, mod_ref, piece_ref, send_sems, recv_sems):
        me = _me()[3]
        call_ref[me] = c_ref[...]
        sent = []
        for r in range(1, N_DEV):
            peer, _ = _peer(r)
            cp = pltpu.make_async_remote_copy(
                src_ref=c_ref, dst_ref=call_ref.at[me], send_sem=send_sems.at[0, r - 1],
                recv_sem=recv_sems.at[0, r - 1], device_id=peer, device_id_type=MESH)
            cp.start()
            sent.append(cp)
        for r in range(1, N_DEV):
            peer, pidx = _peer(r)
            pltpu.make_async_remote_copy(
                src_ref=c_ref, dst_ref=call_ref.at[pidx], send_sem=send_sems.at[0, r - 1],
                recv_sem=recv_sems.at[0, r - 1], device_id=peer, device_id_type=MESH).wait_recv()
        c_all = jnp.concatenate([call_ref[d] for d in range(N_DEV)], axis=0)
        mod_loc = jnp.dot(c_all, w_ref[...], preferred_element_type=F32,
                          precision=lax.Precision.HIGHEST) + b_ref[...]
        for d in range(N_DEV):
            piece_ref[d] = mod_loc[d:d + 1, :]
        mod_ref[me] = piece_ref[me]
        for r in range(1, N_DEV):
            peer, pidx = _peer(r)
            cp = pltpu.make_async_remote_copy(
                src_ref=piece_ref.at[pidx], dst_ref=mod_ref.at[me], send_sem=send_sems.at[1, r - 1],
                recv_sem=recv_sems.at[1, r - 1], device_id=peer, device_id_type=MESH)
            cp.start()
            sent.append(cp)
        for r in range(1, N_DEV):
            peer, pidx = _peer(r)
            pltpu.make_async_remote_copy(
                src_ref=piece_ref.at[me], dst_ref=mod_ref.at[pidx], send_sem=send_sems.at[1, r - 1],
                recv_sem=recv_sems.at[1, r - 1], device_id=peer, device_id_type=MESH).wait_recv()
        for cp in sent:
            cp.wait_send()

    vmem = pl.BlockSpec(memory_space=pltpu.VMEM)
    return pl.pallas_call(
        body, name="mod_exchange", in_specs=[vmem, vmem, vmem], out_specs=[vmem, vmem],
        out_shape=[jax.ShapeDtypeStruct((N_DEV, 1, D_MODEL), F32), jax.ShapeDtypeStruct((N_DEV, 1, ADA_SHARD), F32)],
        scratch_shapes=[pltpu.VMEM((N_DEV, 1, ADA_SHARD), F32),
                        pltpu.SemaphoreType.DMA((2, N_DEV - 1)), pltpu.SemaphoreType.DMA((2, N_DEV - 1))],
        compiler_params=_params(),
    )(c_row, w_ada, b_ada_loc)


def _split3(x):
    hi = x.astype(BF16)
    r1 = x - hi.astype(F32)
    mid = r1.astype(BF16)
    lo = (r1 - mid.astype(F32)).astype(BF16)
    return hi, mid, lo


def _scan_rows(x_ref, o_ref, s, reverse, pre=None, post=None):
    tb = min(TQ, s)
    nb = s // tb
    row = lax.broadcasted_iota(jnp.int32, (tb, tb), 0)
    col = lax.broadcasted_iota(jnp.int32, (tb, tb), 1)
    tri = jnp.where((col >= row) if reverse else (col <= row), 1.0, 0.0).astype(BF16)

    def step(i, carry):
        blk = (nb - 1 - i) if reverse else i
        off = pl.multiple_of(blk * tb, tb)
        x = x_ref[pl.ds(off, tb), :]
        if pre is not None:
            x = pre(x, off)
        acc = carry
        for piece in _split3(x):
            acc = acc + jnp.dot(tri, piece, preferred_element_type=F32)
        o_ref[pl.ds(off, tb), :] = acc if post is None else post(acc, off)
        edge = acc[0:1, :] if reverse else acc[tb - 1:tb, :]
        return jnp.broadcast_to(edge, (tb, LANES))

    lax.fori_loop(0, nb, step, jnp.zeros((tb, LANES), F32))


def _forget_cumsum(f_raw, b_pad):
    s = f_raw.shape[0]

    def body(f_ref, b_ref, cum_ref):
        b = b_ref[...]
        _scan_rows(f_ref, cum_ref, s, False, pre=lambda x, off: _log_sigmoid(x + b))

    vmem = pl.BlockSpec(memory_space=pltpu.VMEM)
    return pl.pallas_call(body, name="forget_cumsum", in_specs=[vmem, vmem], out_specs=vmem,
                          out_shape=jax.ShapeDtypeStruct((s, LANES), F32), compiler_params=_params())(f_raw, b_pad)


def _forget_bwd(dcum, f_raw, b_pad):
    s = f_raw.shape[0]

    def body(d_ref, f_ref, b_ref, df_ref, db_ref, tmp_ref):
        b = b_ref[...]
        _scan_rows(d_ref, tmp_ref, s, True)
        df = tmp_ref[...] * _sigmoid(-(f_ref[...] + b))
        df_ref[...] = df.astype(BF16)
        db_ref[...] = _colsum(df)

    vmem = pl.BlockSpec(memory_space=pltpu.VMEM)
    return pl.pallas_call(
        body, name="forget_bwd", in_specs=[vmem, vmem, vmem], out_specs=[vmem, vmem],
        out_shape=[jax.ShapeDtypeStruct((s, LANES), BF16), jax.ShapeDtypeStruct((1, LANES), F32)],
        scratch_shapes=[pltpu.VMEM((s, LANES), F32)], compiler_params=_params())(dcum, f_raw, b_pad)


def _dot_nt(a, b):
    return lax.dot_general(a, b, (((1,), (1,)), ((), ())), preferred_element_type=F32)


def _dot_tn(a, b):
    return lax.dot_general(a, b, (((0,), (0,)), ((), ())), preferred_element_type=F32)


def _head_masks():
    lane = lax.broadcasted_iota(jnp.int32, (TQ, LANES), 1)
    return lane, [lane < HEAD_DIM, lane >= HEAD_DIM]


def _pick(mask, x):
    return jnp.where(mask, x, jnp.zeros_like(x))


def _qkv_specs(s, col0):
    nb = ATTN_W // LANES
    return [pl.BlockSpec((TQ, LANES), lambda hp, qi: (qi, col0 + hp)),
            pl.BlockSpec((s, LANES), lambda hp, qi: (0, col0 + nb + hp)),
            pl.BlockSpec((s, LANES), lambda hp, qi: (0, col0 + 2 * nb + hp))]


def _pair_spec():
    return pl.BlockSpec((TQ, LANES), lambda hp, qi: (qi, hp))


def _fox_fwd(qkv, cum_col, cum_row):
    s = qkv.shape[0]
    nq = s // TQ

    def body(q_ref, k_ref, v_ref, cc_ref, cr_ref, o_ref, o32_ref, lse_ref):
        hp, qi = pl.program_id(0), pl.program_id(1)
        lane, masks = _head_masks()
        row = lax.broadcasted_iota(jnp.int32, (TQ, TQ), 0)
        col = lax.broadcasted_iota(jnp.int32, (TQ, TQ), 1)
        causal = col <= row
        q2 = q_ref[...] * jnp.asarray(ATTN_SCALE, BF16)
        cc = cc_ref[...]
        qms = [_pick(masks[e], q2) for e in range(2)]
        cqs = [jnp.sum(jnp.where(lane == 2 * hp + e, cc, 0.0), axis=1, keepdims=True) for e in range(2)]

        def tile(kb, carry, masked):
            off = pl.multiple_of(kb * TQ, TQ)
            k2, v2 = k_ref[pl.ds(off, TQ), :], v_ref[pl.ds(off, TQ), :]
            new = []
            for e in range(2):
                m, l, acc = carry[e]
                sc = _dot_nt(qms[e], k2) + (cqs[e] - cr_ref[e:e + 1, pl.ds(off, TQ)])
                if masked:
                    sc = jnp.where(causal, sc, -jnp.inf)
                m_new = jnp.maximum(m, jnp.max(sc, axis=1, keepdims=True))
                p = jnp.exp(sc - m_new)
                corr = jnp.exp(m - m_new)
                l = corr * l + jnp.sum(p, axis=1, keepdims=True)
                acc = corr * acc + jnp.dot(p.astype(BF16), v2, preferred_element_type=F32)
                new.append((m_new, l, acc))
            return tuple(new)

        init = (jnp.full((TQ, 1), -jnp.inf, F32), jnp.zeros((TQ, 1), F32), jnp.zeros((TQ, LANES), F32))
        carry = lax.fori_loop(0, qi, lambda kb, cr: tile(kb, cr, False), tile(qi, (init, init), True))
        outs = [acc / l for _, l, acc in carry]
        lses = [m + jnp.log(l) for m, l, _ in carry]
        out = jnp.where(masks[0], outs[0], outs[1])
        o_ref[...] = out.astype(BF16)
        o32_ref[...] = out
        lse_ref[...] = jnp.where(masks[0], lses[0], lses[1])

    return pl.pallas_call(
        body, name="fox_fwd", grid=(N_HEADS // 2, nq),
        in_specs=_qkv_specs(s, 0) + [pl.BlockSpec((TQ, LANES), lambda hp, qi: (qi, 0)),
                                     pl.BlockSpec((None, 2, s), lambda hp, qi: (hp, 0, 0))],
        out_specs=[_pair_spec(), _pair_spec(), _pair_spec()],
        out_shape=[jax.ShapeDtypeStruct((s, ATTN_W), BF16), jax.ShapeDtypeStruct((s, ATTN_W), F32),
                   jax.ShapeDtypeStruct((s, ATTN_W), F32)],
        compiler_params=_params(("parallel", "parallel")),
    )(qkv, qkv, qkv, cum_col, cum_row)


def _fox_bwd(qkv, cum_col, cum_row, o, lse, do):
    s = qkv.shape[0]
    nq = s // TQ

    def body(q_ref, k_ref, v_ref, cc_ref, cr_ref, o_ref, lse_ref, do_ref,
             dq_ref, dk_ref, dv_ref, dcr_ref, dcq_ref, dk_acc, dv_acc):
        hp, qi = pl.program_id(0), pl.program_id(1)

        @pl.when(qi == 0)
        def _():
            dk_acc[...] = jnp.zeros_like(dk_acc)
            dv_acc[...] = jnp.zeros_like(dv_acc)
            dcr_ref[...] = jnp.zeros_like(dcr_ref)

        lane, masks = _head_masks()
        row = lax.broadcasted_iota(jnp.int32, (TQ, TQ), 0)
        col = lax.broadcasted_iota(jnp.int32, (TQ, TQ), 1)
        causal = col <= row
        q2 = q_ref[...] * jnp.asarray(ATTN_SCALE, BF16)
        do2 = do_ref[...]
        prod = do2.astype(F32) * o_ref[...].astype(F32)
        lse2 = lse_ref[...]
        cc = cc_ref[...]
        qms = [_pick(masks[e], q2) for e in range(2)]
        doms = [_pick(masks[e], do2) for e in range(2)]
        deltas = [jnp.sum(jnp.where(masks[e], prod, 0.0), axis=1, keepdims=True) for e in range(2)]
        lses = [jnp.max(jnp.where(masks[e], lse2, -jnp.inf), axis=1, keepdims=True) for e in range(2)]
        cqs = [jnp.sum(jnp.where(lane == 2 * hp + e, cc, 0.0), axis=1, keepdims=True) for e in range(2)]

        def tile(kb, carry, masked):
            off = pl.multiple_of(kb * TQ, TQ)
            k2, v2 = k_ref[pl.ds(off, TQ), :], v_ref[pl.ds(off, TQ), :]
            new, dk, dv = [], None, None
            for e in range(2):
                dq, rowsum = carry[e]
                sc = _dot_nt(qms[e], k2) + (cqs[e] - cr_ref[e:e + 1, pl.ds(off, TQ)])
                p = jnp.exp(sc - lses[e])
                if masked:
                    p = jnp.where(causal, p, 0.0)
                ds = p * (_dot_nt(doms[e], v2) - deltas[e])
                dsb = ds.astype(BF16)
                dk_e, dv_e = _dot_tn(dsb, qms[e]), _dot_tn(p.astype(BF16), doms[e])
                dk, dv = (dk_e, dv_e) if e == 0 else (dk + dk_e, dv + dv_e)
                dcr_ref[e:e + 1, pl.ds(off, TQ)] -= _colsum(ds)
                new.append((dq + jnp.dot(dsb, k2, preferred_element_type=F32),
                            rowsum + jnp.sum(ds, axis=1, keepdims=True)))
            dk_acc[pl.ds(off, TQ), :] += dk
            dv_acc[pl.ds(off, TQ), :] += dv
            return tuple(new)

        init = (jnp.zeros((TQ, LANES), F32), jnp.zeros((TQ, 1), F32))
        carry = tile(qi, lax.fori_loop(0, qi, lambda kb, cr: tile(kb, cr, False), (init, init)), True)
        dq_ref[...] = (jnp.where(masks[0], carry[0][0], carry[1][0]) * ATTN_SCALE).astype(BF16)
        dcq_ref[...] = jnp.where(masks[0], carry[0][1], carry[1][1])

        @pl.when(qi == nq - 1)
        def _():
            dk_ref[...] = dk_acc[...].astype(BF16)
            dv_ref[...] = dv_acc[...].astype(BF16)

    seq_spec = pl.BlockSpec((s, LANES), lambda hp, qi: (0, hp))
    return pl.pallas_call(
        body, name="fox_bwd", grid=(N_HEADS // 2, nq),
        in_specs=_qkv_specs(s, 0) + [pl.BlockSpec((TQ, LANES), lambda hp, qi: (qi, 0)),
                                     pl.BlockSpec((None, 2, s), lambda hp, qi: (hp, 0, 0)),
                                     _pair_spec(), _pair_spec(), _pair_spec()],
        out_specs=[_pair_spec(), seq_spec, seq_spec, pl.BlockSpec((None, 2, s), lambda hp, qi: (hp, 0, 0)),
                   _pair_spec()],
        out_shape=[jax.ShapeDtypeStruct((s, ATTN_W), BF16)] * 3 + [jax.ShapeDtypeStruct((N_HEADS // 2, 2, s), F32),
                                                                    jax.ShapeDtypeStruct((s, ATTN_W), F32)],
        scratch_shapes=[pltpu.VMEM((s, LANES), F32), pltpu.VMEM((s, LANES), F32)],
        compiler_params=_params(("parallel", "arbitrary")),
    )(qkv, qkv, qkv, cum_col, cum_row, o, lse, do)


def _split2(x):
    hi = x.astype(BF16)
    return hi, (x - hi.astype(F32)).astype(BF16)


def _scan_matrix(reverse):
    row = lax.broadcasted_iota(jnp.int32, (SCAN_W, SCAN_W), 0)
    col = lax.broadcasted_iota(jnp.int32, (SCAN_W, SCAN_W), 1)
    return jnp.where((row > col) if reverse else (row < col), 1.0, 0.0).astype(BF16)


def _scan_cols(x, tri, reverse):
    nblk = x.shape[1] // SCAN_W
    parts, total = [None] * nblk, None
    for b in (reversed(range(nblk)) if reverse else range(nblk)):
        blk = x[:, b * SCAN_W:(b + 1) * SCAN_W]
        hi, low = _split2(blk)
        part = jnp.dot(hi, tri, preferred_element_type=F32) + jnp.dot(low, tri, preferred_element_type=F32)
        parts[b] = part if total is None else part + total
        rowsum = jnp.sum(blk, axis=1, keepdims=True)
        total = rowsum if total is None else total + rowsum
    return (parts[0] if nblk == 1 else jnp.concatenate(parts, axis=1)), total


def _sb_logits(qm, k2):
    z = _dot_nt(qm, k2)
    soft = jnp.log(1.0 + jnp.exp(-jnp.abs(z)))
    lb = jnp.minimum(z, 0.0) - soft
    return lb, lb - z


def _sb_fwd(qkv):
    s = qkv.shape[0]
    nq = s // TQ
    assert nq <= LANES

    def body(q_ref, k_ref, v_ref, o_ref, r_ref):
        qi = pl.program_id(1)
        lane, masks = _head_masks()
        row = lax.broadcasted_iota(jnp.int32, (TQ, TQ), 0)
        col = lax.broadcasted_iota(jnp.int32, (TQ, TQ), 1)
        strict = col < row
        suffix = _scan_matrix(True)
        q2 = q_ref[...] * jnp.asarray(ATTN_SCALE, BF16)
        qms = [_pick(masks[e], q2) for e in range(2)]

        def tile(kb, carry, masked):
            off = pl.multiple_of(kb * TQ, TQ)
            k2, v2 = k_ref[pl.ds(off, TQ), :], v_ref[pl.ds(off, TQ), :]
            new = []
            for e in range(2):
                run, saved, acc = carry[e]
                lb, lo = _sb_logits(qms[e], k2)
                if masked:
                    lo = jnp.where(strict, lo, 0.0)
                rest, total = _scan_cols(lo, suffix, True)
                a = jnp.exp(lb + rest + run)
                if masked:
                    a = jnp.where(strict, a, 0.0)
                acc = acc + jnp.dot(a.astype(BF16), v2, preferred_element_type=F32)
                new.append((run + total, jnp.where(lane == kb, run, saved), acc))
            return tuple(new)

        init = (jnp.zeros((TQ, 1), F32), jnp.zeros((TQ, LANES), F32), jnp.zeros((TQ, LANES), F32))
        carry = lax.fori_loop(0, qi, lambda it, cr: tile(qi - 1 - it, cr, False), tile(qi, (init, init), True))
        for e in range(2):
            r_ref[:, e * LANES:(e + 1) * LANES] = carry[e][1]
        o_ref[...] = jnp.where(masks[0], carry[0][2], carry[1][2]).astype(BF16)

    return pl.pallas_call(
        body, name="sb_fwd", grid=(N_HEADS // 2, nq), in_specs=_qkv_specs(s, 3 * ATTN_W // LANES),
        out_specs=[_pair_spec(), pl.BlockSpec((TQ, 2 * LANES), lambda hp, qi: (qi, hp))],
        out_shape=[jax.ShapeDtypeStruct((s, ATTN_W), BF16), jax.ShapeDtypeStruct((s, N_HEADS * LANES), F32)],
        compiler_params=_params(("parallel", "parallel")),
    )(qkv, qkv, qkv)


def _sb_bwd(qkv, saved_run, do):
    s = qkv.shape[0]
    nq = s // TQ

    def body(q_ref, k_ref, v_ref, r_ref, do_ref, dq_ref, dk_ref, dv_ref, dk_acc, dv_acc):
        qi = pl.program_id(1)

        @pl.when(qi == 0)
        def _():
            dk_acc[...] = jnp.zeros_like(dk_acc)
            dv_acc[...] = jnp.zeros_like(dv_acc)

        lane, masks = _head_masks()
        row = lax.broadcasted_iota(jnp.int32, (TQ, TQ), 0)
        col = lax.broadcasted_iota(jnp.int32, (TQ, TQ), 1)
        strict = col < row
        suffix, prefix = _scan_matrix(True), _scan_matrix(False)
        q2 = q_ref[...] * jnp.asarray(ATTN_SCALE, BF16)
        do2 = do_ref[...]
        qms = [_pick(masks[e], q2) for e in range(2)]
        doms = [_pick(masks[e], do2) for e in range(2)]
        saveds = [r_ref[:, e * LANES:(e + 1) * LANES] for e in range(2)]

        def tile(kb, carry, masked):
            off = pl.multiple_of(kb * TQ, TQ)
            k2, v2 = k_ref[pl.ds(off, TQ), :], v_ref[pl.ds(off, TQ), :]
            new, dk, dv = [], None, None
            for e in range(2):
                gsum, dq = carry[e]
                lb, lo = _sb_logits(qms[e], k2)
                beta = jnp.exp(lb)
                if masked:
                    lo = jnp.where(strict, lo, 0.0)
                rest, _ = _scan_cols(lo, suffix, True)
                run = jnp.sum(jnp.where(lane == kb, saveds[e], 0.0), axis=1, keepdims=True)
                a = jnp.exp(lb + rest + run)
                if masked:
                    a = jnp.where(strict, a, 0.0)
                g = a * _dot_nt(doms[e], v2)
                before, gtotal = _scan_cols(g, prefix, False)
                dz = g * (1.0 - beta) - beta * (before + gsum)
                if masked:
                    dz = jnp.where(strict, dz, 0.0)
                dzb = dz.astype(BF16)
                dk_e, dv_e = _dot_tn(dzb, qms[e]), _dot_tn(a.astype(BF16), doms[e])
                dk, dv = (dk_e, dv_e) if e == 0 else (dk + dk_e, dv + dv_e)
                new.append((gsum + gtotal, dq + jnp.dot(dzb, k2, preferred_element_type=F32)))
            dk_acc[pl.ds(off, TQ), :] += dk
            dv_acc[pl.ds(off, TQ), :] += dv
            return tuple(new)

        init = (jnp.zeros((TQ, 1), F32), jnp.zeros((TQ, LANES), F32))
        carry = tile(qi, lax.fori_loop(0, qi, lambda kb, cr: tile(kb, cr, False), (init, init)), True)
        dq_ref[...] = (jnp.where(masks[0], carry[0][1], carry[1][1]) * ATTN_SCALE).astype(BF16)

        @pl.when(qi == nq - 1)
        def _():
            dk_ref[...] = dk_acc[...].astype(BF16)
            dv_ref[...] = dv_acc[...].astype(BF16)

    seq_spec = pl.BlockSpec((s, LANES), lambda hp, qi: (0, hp))
    return pl.pallas_call(
        body, name="sb_bwd", grid=(N_HEADS // 2, nq),
        in_specs=_qkv_specs(s, 3 * ATTN_W // LANES) + [pl.BlockSpec((TQ, 2 * LANES), lambda hp, qi: (qi, hp)),
                                                        _pair_spec()],
        out_specs=[_pair_spec(), seq_spec, seq_spec],
        out_shape=[jax.ShapeDtypeStruct((s, ATTN_W), BF16)] * 3,
        scratch_shapes=[pltpu.VMEM((s, LANES), F32), pltpu.VMEM((s, LANES), F32)],
        compiler_params=_params(("parallel", "arbitrary")),
    )(qkv, qkv, qkv, saved_run, do)


CONV_TR = 256


def _shift_down(x, halo, n):
    rolled = pltpu.roll(x, n, 0)
    rid = lax.broadcasted_iota(jnp.int32, x.shape, 0)
    for j in range(n):
        rolled = jnp.where(rid == j, halo[8 - n + j:8 - n + j + 1, :], rolled)
    return rolled


def _shift_up(x, halo, n):
    rows = x.shape[0]
    rolled = pltpu.roll(x, rows - n, 0)
    rid = lax.broadcasted_iota(jnp.int32, x.shape, 0)
    for j in range(n):
        rolled = jnp.where(rid == rows - n + j, halo[j:j + 1, :], rolled)
    return rolled


def _conv_fwd_block(x, halo, w, b):
    return b + _shift_down(x, halo, 2) * w[0:1, :] + _shift_down(x, halo, 1) * w[1:2, :] + x * w[2:3, :]


def _conv_specs(tr, s):
    pair = 2 * FF_HALF
    blk = pl.BlockSpec((tr, pair), lambda j, i: (i, j))
    prev = pl.BlockSpec((8, pair), lambda j, i: (jnp.maximum(i * (tr // 8) - 1, 0), j))
    nxt = pl.BlockSpec((8, pair), lambda j, i: (jnp.minimum((i + 1) * (tr // 8), s // 8 - 1), j))
    return blk, prev, nxt


def _conv_gate_fwd(hpre, conv_w, conv_b):
    s = hpre.shape[0]
    tr = min(CONV_TR, s)
    blk, prev, _ = _conv_specs(tr, s)

    def body(x_ref, halo_ref, w_ref, b_ref, a_ref):
        i = pl.program_id(1)
        halo = jnp.where(i > 0, halo_ref[...], 0.0)
        h = _conv_fwd_block(x_ref[...], halo, w_ref[...], b_ref[...])
        hg, hv = h[:, :FF_HALF], h[:, FF_HALF:]
        a_ref[...] = (hg * _sigmoid(hg) * hv).astype(BF16)

    return pl.pallas_call(
        body, name="conv_gate_fwd", grid=(2, s // tr),
        in_specs=[blk, prev, pl.BlockSpec((3, 2 * FF_HALF), lambda j, i: (0, j)),
                  pl.BlockSpec((1, 2 * FF_HALF), lambda j, i: (0, j))],
        out_specs=pl.BlockSpec((tr, FF_HALF), lambda j, i: (i, j)),
        out_shape=jax.ShapeDtypeStruct((s, D_FF), BF16),
        compiler_params=_params(("parallel", "parallel")),
    )(hpre, hpre, conv_w, conv_b)


def _conv_gate_bwd(hpre, da, conv_w, conv_b):
    s = hpre.shape[0]
    tr = min(CONV_TR, s)
    blk, prev, _ = _conv_specs(tr, s)

    def body(x_ref, halo_ref, da_ref, w_ref, b_ref, dh_ref, db_ref, dw_ref):
        i = pl.program_id(1)
        halo = jnp.where(i > 0, halo_ref[...], 0.0)
        x = x_ref[...]
        h = _conv_fwd_block(x, halo, w_ref[...], b_ref[...])
        hg, hv = h[:, :FF_HALF], h[:, FF_HALF:]
        da_blk = da_ref[...]
        sg = _sigmoid(hg)
        dhg = da_blk * hv * (sg * (1.0 + hg * (1.0 - sg)))
        dhv = da_blk * (hg * sg)
        dh_ref[:, :FF_HALF] = dhg
        dh_ref[:, FF_HALF:] = dhv
        x2, x1 = _shift_down(x, halo, 2), _shift_down(x, halo, 1)
        parts = []
        for lo, dpart in ((0, dhg), (FF_HALF, dhv)):
            cols = slice(lo, lo + FF_HALF)
            parts.append((cols, _colsum(dpart), _colsum(dpart * x2[:, cols]), _colsum(dpart * x1[:, cols]),
                          _colsum(dpart * x[:, cols])))

        @pl.when(i == 0)
        def _():
            for cols, db, dw0, dw1, dw2 in parts:
                db_ref[:, cols] = db
                dw_ref[0:1, cols] = dw0
                dw_ref[1:2, cols] = dw1
                dw_ref[2:3, cols] = dw2

        @pl.when(i > 0)
        def _():
            for cols, db, dw0, dw1, dw2 in parts:
                db_ref[:, cols] += db
                dw_ref[0:1, cols] += dw0
                dw_ref[1:2, cols] += dw1
                dw_ref[2:3, cols] += dw2

    pair = 2 * FF_HALF
    return pl.pallas_call(
        body, name="conv_gate_bwd", grid=(2, s // tr),
        in_specs=[blk, prev, pl.BlockSpec((tr, FF_HALF), lambda j, i: (i, j)),
                  pl.BlockSpec((3, pair), lambda j, i: (0, j)), pl.BlockSpec((1, pair), lambda j, i: (0, j))],
        out_specs=[blk, pl.BlockSpec((1, pair), lambda j, i: (0, j)), pl.BlockSpec((3, pair), lambda j, i: (0, j))],
        out_shape=[jax.ShapeDtypeStruct((s, 2 * D_FF), F32), jax.ShapeDtypeStruct((1, 2 * D_FF), F32),
                   jax.ShapeDtypeStruct((3, 2 * D_FF), F32)],
        compiler_params=_params(("parallel", "arbitrary")),
    )(hpre, hpre, da, conv_w, conv_b)


def _conv_input_bwd(dh, conv_w):
    s = dh.shape[0]
    tr = min(CONV_TR, s)
    blk, _, nxt = _conv_specs(tr, s)
    nblk = s // tr

    def body(x_ref, halo_ref, w_ref, o_ref):
        i = pl.program_id(1)
        halo = jnp.where(i < nblk - 1, halo_ref[...], 0.0)
        x, w = x_ref[...], w_ref[...]
        o_ref[...] = (x * w[2:3, :] + _shift_up(x, halo, 1) * w[1:2, :] + _shift_up(x, halo, 2) * w[0:1, :]).astype(BF16)

    return pl.pallas_call(
        body, name="conv_input_bwd", grid=(2, nblk),
        in_specs=[blk, nxt, pl.BlockSpec((3, 2 * FF_HALF), lambda j, i: (0, j))], out_specs=blk,
        out_shape=jax.ShapeDtypeStruct((s, 2 * D_FF), BF16),
        compiler_params=_params(("parallel", "parallel")),
    )(dh, dh, conv_w)


def _adamw_math(w, g, m, v):
    m = ADAM_B1 * m + (1.0 - ADAM_B1) * g
    v = ADAM_B2 * v + (1.0 - ADAM_B2) * (g * g)
    m_hat = m / (1.0 - ADAM_B1 ** ADAM_STEP)
    v_hat = v / (1.0 - ADAM_B2 ** ADAM_STEP)
    delta = -ADAM_LR * (m_hat / (jnp.sqrt(v_hat) + ADAM_EPS) + ADAM_WD * w)
    return delta, m, v


def _adamw(name, g8, w, m, v):
    r, c = w.shape
    tr = _row_tile(r, c)

    def body(g_ref, w_ref, m_ref, v_ref, go_ref, d_ref, mo_ref, vo_ref):
        g = g_ref[0].astype(F32)
        for d in range(1, N_DEV):
            g = g + g_ref[d].astype(F32)
        delta, mn, vn = _adamw_math(w_ref[...], g, m_ref[...], v_ref[...])
        go_ref[...] = g
        d_ref[...] = delta
        mo_ref[...] = mn
        vo_ref[...] = vn

    spec = pl.BlockSpec((tr, c), lambda i: (i, 0))
    return pl.pallas_call(
        body, name=name, grid=(r // tr,),
        in_specs=[pl.BlockSpec((N_DEV, tr, c), lambda i: (0, i, 0)), spec, spec, spec], out_specs=[spec] * 4,
        out_shape=[jax.ShapeDtypeStruct((r, c), F32)] * 4, compiler_params=_params(("parallel",)),
    )(g8, w, m, v)


def _adamw_ada(c_t, dmod, w, m, v):
    r, c = w.shape
    tr = _row_tile(r, c)

    def body(ct_ref, dm_ref, w_ref, m_ref, v_ref, go_ref, d_ref, mo_ref, vo_ref):
        ct, dm = ct_ref[...], dm_ref[...]
        g = ct[:, 0:1] * dm[0:1, :]
        for b in range(1, N_DEV):
            g = g + ct[:, b:b + 1] * dm[b:b + 1, :]
        delta, mn, vn = _adamw_math(w_ref[...], g, m_ref[...], v_ref[...])
        go_ref[...] = g
        d_ref[...] = delta
        mo_ref[...] = mn
        vo_ref[...] = vn

    spec = pl.BlockSpec((tr, c), lambda i: (i, 0))
    return pl.pallas_call(
        body, name="adamw_w_ada", grid=(r // tr,),
        in_specs=[pl.BlockSpec((tr, N_DEV), lambda i: (i, 0)), pl.BlockSpec((N_DEV, c), lambda i: (0, 0)),
                  spec, spec, spec],
        out_specs=[spec] * 4, out_shape=[jax.ShapeDtypeStruct((r, c), F32)] * 4,
        compiler_params=_params(("parallel",)),
    )(c_t, dmod, w, m, v)


def _cols_from_slots(g):
    n, r, c = g.shape
    return jnp.transpose(g, (1, 0, 2)).reshape(r, n * c)


def _cols_to_slots(w):
    r, c = w.shape
    return jnp.transpose(w.reshape(r, N_DEV, c // N_DEV), (1, 0, 2))


def _pair_cols(w):
    g0, g1 = w[..., 0:FF_HALF], w[..., FF_HALF:D_FF]
    v0, v1 = w[..., D_FF:D_FF + FF_HALF], w[..., D_FF + FF_HALF:]
    return jnp.concatenate([g0, v0, g1, v1], axis=-1)


def _unpair_cols(w):
    g0, v0 = w[..., 0:FF_HALF], w[..., FF_HALF:D_FF]
    g1, v1 = w[..., D_FF:D_FF + FF_HALF], w[..., D_FF + FF_HALF:]
    return jnp.concatenate([g0, g1, v0, v1], axis=-1)


def _row(v):
    return v.reshape(1, -1)


def kernel(x, c, w_ada, b_ada, w_in, b_forget, w_fox_proj, w_sb_proj, w_o, ln1_g, ln1_b, w_up, conv_w, conv_b, w_down, ln2_g, ln2_b, loss_target, m_w_ada, m_b_ada, m_w_in, m_b_forget, m_w_fox_proj, m_w_sb_proj, m_w_o, m_ln1_g, m_ln1_b, m_w_up, m_conv_w, m_conv_b, m_w_down, m_ln2_g, m_ln2_b, v_w_ada, v_b_ada, v_w_in, v_b_forget, v_w_fox_proj, v_w_sb_proj, v_w_o, v_ln1_g, v_ln1_b, v_w_up, v_conv_w, v_conv_b, v_w_down, v_ln2_g, v_ln2_b):
    s = x.shape[1]
    me = 4 * lax.axis_index("x") + 2 * lax.axis_index("y") + lax.axis_index("c")
    x2 = x.reshape(s, D_MODEL)
    tgt = loss_target.reshape(s, D_MODEL)

    b_ada_loc = lax.dynamic_slice(b_ada, (me * ADA_SHARD,), (ADA_SHARD,)).reshape(1, ADA_SHARD)
    c_all, mod = _mod_exchange(c, w_ada, b_ada_loc)
    mod = mod.reshape(N_MOD, 1, D_MODEL)
    sh1, sc1, gt1, sh2, sc2, gt2 = [mod[i] for i in range(N_MOD)]

    g_in, g_fox, g_sb, g_o, g_up, g_down, g_cw = _exchange(
        "ag_weights",
        [w_in.astype(BF16), w_fox_proj.astype(BF16), w_sb_proj.astype(BF16), w_o.astype(BF16), w_up.astype(BF16),
         w_down.astype(BF16), conv_w], scatter=False)
    w_in_f = _cols_from_slots(g_in)
    w_proj = jnp.concatenate(
        [w_in_f[:, 0:1536], w_in_f[:, 1544:3080], w_in_f[:, 3080:5128], w_in_f[:, 1536:1544],
         jnp.zeros((D_MODEL, W_PROJ - 5128), BF16)], axis=1)
    w_qkv, w_gates, w_f = w_proj[:, :W_QKV], w_proj[:, W_QKV:W_QKV + W_GATES], w_proj[:, W_QKV + W_GATES:W_QKV + W_GATES + W_F]
    w_fox_f = _cols_from_slots(g_fox)
    w_sb_f = _cols_from_slots(g_sb)
    w_o_f = g_o.reshape(D_MODEL, D_MODEL)
    w_up_p = _pair_cols(_cols_from_slots(g_up))
    w_down_f = g_down.reshape(D_FF, D_MODEL)
    conv_w_p = _pair_cols(_cols_from_slots(g_cw))
    conv_b_p = _pair_cols(_row(conv_b))
    b_f_pad = jnp.pad(_row(b_forget), ((0, 0), (0, LANES - N_HEADS)))

    (u1,) = _rowwise("modulate1", lambda xb, sc, sh: (xb * (1.0 + sc) + sh,),
                     [(x2, D_MODEL, 0)], [sc1, sh1], [(D_MODEL, BF16)], tr=512)
    qkv = _mm(u1, w_qkv, name="mm_qkv", out_dtype=BF16)
    gates = _mm(u1, w_gates, name="mm_gates")
    f_raw = _mm(u1, w_f, name="mm_forget")
    cum_col = _forget_cumsum(f_raw, b_f_pad)
    cum_row = jnp.transpose(cum_col[:, :N_HEADS]).reshape(N_HEADS // 2, 2, s)
    y_fox, y_fox32, lse = _fox_fwd(qkv, cum_col, cum_row)
    y_sb, sb_run = _sb_fwd(qkv)
    pf = _mm(y_fox, w_fox_f, name="mm_fox_proj")
    ps = _mm(y_sb, w_sb_f, name="mm_sb_proj")
    (merged,) = _rowwise("gate_merge", lambda ga, gb, a, b: (_sigmoid(ga) * a + _sigmoid(gb) * b,),
                         [(gates, D_MODEL, 0), (gates, D_MODEL, 1), (pf, D_MODEL, 0), (ps, D_MODEL, 0)], [],
                         [(D_MODEL, BF16)])
    attn_out = _mm(merged, w_o_f, name="mm_w_o")

    def ln_fwd(xb, fb, gt, g, b):
        xhat, _ = _ln_stats(ALPHA * xb + (1.0 + gt) * fb)
        return xhat * g + b

    def ln1_mod(xb, fb, gt, g, b, sc, sh):
        y = ln_fwd(xb, fb, gt, g, b)
        return y, y * (1.0 + sc) + sh

    x1, u2 = _rowwise("ln1_modulate2", ln1_mod, [(x2, D_MODEL, 0), (attn_out, D_MODEL, 0)],
                      [gt1, _row(ln1_g), _row(ln1_b), sc2, sh2], [(D_MODEL, F32), (D_MODEL, BF16)])

    hpre = _mm(u2, w_up_p, name="mm_w_up", tn=1408)
    act = _conv_gate_fwd(hpre, conv_w_p, conv_b_p)
    ffn_out = _mm(act, w_down_f, name="mm_w_down", tk=1408)

    def ln2_bwd(xb, fb, tb, gt, g, b):
        xhat, rstd = _ln_stats(ALPHA * xb + (1.0 + gt) * fb)
        err = (xhat * g + b) - tb
        dy = err * (1.0 / D_MODEL)
        dr = _ln_bwd(dy, xhat, rstd, g)
        return (dr * (1.0 + gt), ALPHA * dr,
                _colsum(err * err), _colsum(dy * xhat), _colsum(dy), _colsum(dr * fb))

    dffn, dx1_res, sq_err, d_ln2_g, d_ln2_b, d_gt2 = _rowwise(
        "ln2_bwd", ln2_bwd, [(x1, D_MODEL, 0), (ffn_out, D_MODEL, 0), (tgt, D_MODEL, 0)],
        [gt2, _row(ln2_g), _row(ln2_b)], [(D_MODEL, BF16), (D_MODEL, F32)], sums=[D_MODEL] * 4)
    loss = lax.psum(0.5 * jnp.sum(sq_err) / D_MODEL, ("x", "y", "c"))

    d_w_down = _mm(act, dffn, name="mm_d_w_down", ta=True, tm=1408)
    d_act = _mm(dffn, w_down_f, name="mm_d_act", tb=True, tn=1408)
    dh, d_conv_b_p, d_conv_w_p = _conv_gate_bwd(hpre, d_act, conv_w_p, conv_b_p)
    dhpre = _conv_input_bwd(dh, conv_w_p)
    d_w_up_p = _mm(u2, dhpre, name="mm_d_w_up", ta=True, tn=1408)
    du2 = _mm(dhpre, w_up_p, name="mm_d_u2", tb=True, tk=1408)

    def ln1_bwd(du, dres, x1b, xb, fb, sc, gt, g):
        dx1 = dres + du * (1.0 + sc)
        xhat, rstd = _ln_stats(ALPHA * xb + (1.0 + gt) * fb)
        dr = _ln_bwd(dx1, xhat, rstd, g)
        return (dr * (1.0 + gt), ALPHA * dr,
                _colsum(du * x1b), _colsum(du), _colsum(dx1 * xhat), _colsum(dx1), _colsum(dr * fb))

    d_attn, dx_res, d_sc2, d_sh2, d_ln1_g, d_ln1_b, d_gt1 = _rowwise(
        "ln1_bwd", ln1_bwd,
        [(du2, D_MODEL, 0), (dx1_res, D_MODEL, 0), (x1, D_MODEL, 0), (x2, D_MODEL, 0), (attn_out, D_MODEL, 0)],
        [sc2, gt1, _row(ln1_g)], [(D_MODEL, BF16), (D_MODEL, F32)], sums=[D_MODEL] * 5)

    d_w_o = _mm(merged, d_attn, name="mm_d_w_o", ta=True)
    d_merged = _mm(d_attn, w_o_f, name="mm_d_merged", tb=True)

    def merge_bwd(dm, ga, gb, a, b):
        sa, sb = _sigmoid(ga), _sigmoid(gb)
        return dm * a * sa * (1.0 - sa), dm * b * sb * (1.0 - sb), dm * sa, dm * sb

    d_ga, d_gb, d_pf, d_ps = _rowwise(
        "gate_merge_bwd", merge_bwd,
        [(d_merged, D_MODEL, 0), (gates, D_MODEL, 0), (gates, D_MODEL, 1), (pf, D_MODEL, 0), (ps, D_MODEL, 0)], [],
        [(D_MODEL, BF16)] * 4)
    d_w_fox = _mm(y_fox, d_pf, name="mm_d_w_fox", ta=True)
    d_w_sb = _mm(y_sb, d_ps, name="mm_d_w_sb", ta=True)
    d_y_fox = _mm(d_pf, w_fox_f, name="mm_d_y_fox", tb=True, out_dtype=BF16)
    d_y_sb = _mm(d_ps, w_sb_f, name="mm_d_y_sb", tb=True, out_dtype=BF16)
    dq_a, dk_a, dv_a, d_cum_row, d_cum_q = _fox_bwd(qkv, cum_col, cum_row, y_fox32, lse, d_y_fox)
    dq_b, dk_b, dv_b = _sb_bwd(qkv, sb_run, d_y_sb)
    d_cum = jnp.transpose(d_cum_row.reshape(N_HEADS, s)) + d_cum_q[:, ::HEAD_DIM]
    d_cum = jnp.pad(d_cum, ((0, 0), (0, LANES - N_HEADS)))
    d_f, d_b_forget = _forget_bwd(d_cum, f_raw, b_f_pad)
    d_proj = jnp.concatenate([dq_a, dk_a, dv_a, dq_b, dk_b, dv_b, d_ga, d_gb, d_f,
                              jnp.zeros((s, W_PROJ - W_QKV - W_GATES - W_F), BF16)], axis=1)
    d_w_proj = _mm(u1, d_proj, name="mm_d_w_in", ta=True, tn=896)
    du1 = _mm(d_proj, w_proj, name="mm_d_u1", tb=True, tk=896)

    def x_bwd(du, dres, xb, sc):
        return dres + du * (1.0 + sc), _colsum(du * xb), _colsum(du)

    grad_x, d_sc1, d_sh1 = _rowwise("x_bwd", x_bwd, [(du1, D_MODEL, 0), (dx_res, D_MODEL, 0), (x2, D_MODEL, 0)],
                                    [sc1], [(D_MODEL, F32)], sums=[D_MODEL] * 2, tr=512)

    d_conv_b = _unpair_cols(d_conv_b_p)
    d_conv_w = _unpair_cols(d_conv_w_p)
    n_rep = N_MOD * D_MODEL + LANES + 4 * D_MODEL + 2 * D_FF
    small = jnp.concatenate(
        [d_sh1, d_sc1, d_gt1, d_sh2, d_sc2, d_gt2, d_b_forget, d_ln1_g, d_ln1_b, d_ln2_g, d_ln2_b, d_conv_b,
         d_conv_w.reshape(1, 6 * D_FF)], axis=1)
    n_small = small.shape[1] // LANES
    small = jnp.pad(small.reshape(n_small, LANES), ((0, 264 - n_small), (0, 0)))
    (small_all,) = _exchange("ag_small_grads", [small], scatter=False)
    rep8 = small_all[:, :n_rep // LANES, :]
    cw8 = small_all[:, n_rep // LANES:n_small, :].reshape(N_DEV, 3, 2 * D_FF)
    cw8 = lax.dynamic_slice(cw8, (0, 0, me * UP_SHARD), (N_DEV, 3, UP_SHARD))
    dmod8 = small_all[:, :N_MOD * D_MODEL // LANES, :].reshape(N_DEV, N_MOD * D_MODEL)
    dmod_loc = lax.dynamic_slice(dmod8, (0, me * ADA_SHARD), (N_DEV, ADA_SHARD))

    def pack_rep(b_a, b_f, g1, b1, g2, b2, cb):
        flat = jnp.concatenate([b_a, jnp.pad(b_f, (0, LANES - N_HEADS)), g1, b1, g2, b2, cb])
        return flat.reshape(n_rep // LANES, LANES)

    rep = _adamw("adamw_small", rep8, pack_rep(b_ada, b_forget, ln1_g, ln1_b, ln2_g, ln2_b, conv_b),
                 pack_rep(m_b_ada, m_b_forget, m_ln1_g, m_ln1_b, m_ln2_g, m_ln2_b, m_conv_b),
                 pack_rep(v_b_ada, v_b_forget, v_ln1_g, v_ln1_b, v_ln2_g, v_ln2_b, v_conv_b))

    def unpack_rep(p):
        flat = p.reshape(-1)
        o = N_MOD * D_MODEL
        return {"b_ada": flat[:o], "b_forget": flat[o:o + N_HEADS],
                "ln1_g": flat[o + 128:o + 1152], "ln1_b": flat[o + 1152:o + 2176],
                "ln2_g": flat[o + 2176:o + 3200], "ln2_b": flat[o + 3200:o + 4224], "conv_b": flat[o + 4224:]}

    rep = [unpack_rep(p) for p in rep]
    r_conv_w = _adamw("adamw_conv_w", cw8, conv_w, m_conv_w, v_conv_w)
    r_ada = _adamw_ada(jnp.transpose(c_all.reshape(N_DEV, D_MODEL)), dmod_loc, w_ada, m_w_ada, v_w_ada)

    d_w_in_f = jnp.concatenate([d_w_proj[:, 0:1536], d_w_proj[:, 5120:5128], d_w_proj[:, 1536:3072],
                                d_w_proj[:, 3072:5120]], axis=1)
    slots = _exchange(
        "rs_weight_grads",
        [_cols_to_slots(d_w_in_f), _cols_to_slots(d_w_fox), _cols_to_slots(d_w_sb),
         d_w_o.reshape(N_DEV, D_MODEL // N_DEV, D_MODEL), _cols_to_slots(_unpair_cols(d_w_up_p)),
         d_w_down.reshape(N_DEV, D_FF // N_DEV, D_MODEL)], scatter=True)
    r_in = _adamw("adamw_w_in", slots[0], w_in, m_w_in, v_w_in)
    r_fox = _adamw("adamw_w_fox", slots[1], w_fox_proj, m_w_fox_proj, v_w_fox_proj)
    r_sb = _adamw("adamw_w_sb", slots[2], w_sb_proj, m_w_sb_proj, v_w_sb_proj)
    r_o = _adamw("adamw_w_o", slots[3], w_o, m_w_o, v_w_o)
    r_up = _adamw("adamw_w_up", slots[4], w_up, m_w_up, v_w_up)
    r_down = _adamw("adamw_w_down", slots[5], w_down, m_w_down, v_w_down)

    def leaf(i):
        return [r_ada[i], rep[i]["b_ada"], r_in[i], rep[i]["b_forget"], r_fox[i], r_sb[i], r_o[i], rep[i]["ln1_g"],
                rep[i]["ln1_b"], r_up[i], r_conv_w[i], rep[i]["conv_b"], r_down[i], rep[i]["ln2_g"], rep[i]["ln2_b"]]

    return (loss, grad_x.reshape(1, s, D_MODEL), *leaf(0), *leaf(1), *leaf(2), *leaf(3))
```

```python
import functools

import jax
import jax.numpy as jnp
from jax import lax
from jax.experimental import pallas as pl
from jax.experimental.pallas import tpu as pltpu

F32 = jnp.float32
BF16 = jnp.bfloat16
MESH = pl.DeviceIdType.MESH
ANY = pl.BlockSpec(memory_space=pl.ANY)

N_DEV = 8
D_MODEL = 1024
HEAD_DIM = 64
N_HEADS = 8
ATTN_W = N_HEADS * HEAD_DIM
D_FF = 2816
FF_HALF = D_FF // 2
N_MOD = 6
ADA_SHARD = N_MOD * D_MODEL // N_DEV
IN_SHARD = 641
UP_SHARD = 704
ATTN_SCALE = HEAD_DIM ** -0.5
ALPHA = 2.0 ** 0.25
LN_EPS = 1e-5
LANES = 128
TQ = 512
SCAN_W = 256
VMEM_LIMIT = 56 * 1024 * 1024

ADAM_LR, ADAM_B1, ADAM_B2, ADAM_EPS, ADAM_WD, ADAM_STEP = 0.001, 0.9, 0.999, 1e-08, 0.01, 10

W_QKV, W_GATES, W_F = 3072, 2048, 128
W_PROJ = 5376


def _params(sem=None):
    return pltpu.CompilerParams(dimension_semantics=sem, vmem_limit_bytes=VMEM_LIMIT)


def _tile(n, cap):
    if n <= cap:
        return n
    best = None
    for t in range(LANES, cap + 1, LANES):
        if n % t == 0:
            best = t
    assert best is not None, (n, cap)
    return best


def _row_tile(r, width, budget=192 * 1024):
    if r * width <= budget or r % 16:
        return r
    best = 16
    for t in range(16, r + 1, 16):
        if r % t == 0 and t * width <= budget:
            best = t
    return best


def _me():
    x, y, c = lax.axis_index("x"), lax.axis_index("y"), lax.axis_index("c")
    return x, y, c, 4 * x + 2 * y + c


def _peer(r):
    x, y, c, _ = _me()
    px = 1 - x if r & 4 else x
    py = 1 - y if r & 2 else y
    pc = 1 - c if r & 1 else c
    return (px, py, pc), 4 * px + 2 * py + pc


class _Ride:
    def __init__(self, arrays, scatter):
        self.arrays, self.scatter, self.n = list(arrays), scatter, len(arrays)
        self.in_specs = [ANY] * self.n
        self.out_specs = [ANY] * self.n
        self.out_shape = [jax.ShapeDtypeStruct(a.shape if scatter else (N_DEV,) + a.shape, a.dtype) for a in arrays]
        self.scratch = [pltpu.SemaphoreType.DMA((self.n, N_DEV - 1)), pltpu.SemaphoreType.DMA((self.n, N_DEV - 1)),
                        pltpu.SemaphoreType.DMA((self.n,))]

    def _local(self, ins, outs, sems, a):
        me = _me()[3]
        return pltpu.make_async_copy(ins[a].at[me] if self.scatter else ins[a], outs[a].at[me], sems[2].at[a])

    def _remote(self, ins, outs, sems, a, r, arriving):
        me = _me()[3]
        peer, pidx = _peer(r)
        src = ins[a].at[me if arriving else pidx] if self.scatter else ins[a]
        return pltpu.make_async_remote_copy(
            src_ref=src, dst_ref=outs[a].at[pidx if arriving else me], send_sem=sems[0].at[a, r - 1],
            recv_sem=sems[1].at[a, r - 1], device_id=peer, device_id_type=MESH)

    def start(self, ins, outs, sems):
        for a in range(self.n):
            self._local(ins, outs, sems, a).start()
        for r in range(1, N_DEV):
            for a in range(self.n):
                self._remote(ins, outs, sems, a, r, False).start()

    def wait(self, ins, outs, sems):
        for r in range(1, N_DEV):
            for a in range(self.n):
                self._remote(ins, outs, sems, a, r, True).wait_recv()
        for r in range(1, N_DEV):
            for a in range(self.n):
                self._remote(ins, outs, sems, a, r, False).wait_send()
        for a in range(self.n):
            self._local(ins, outs, sems, a).wait()


def _exchange(name, arrays, scatter):
    ride = _Ride(arrays, scatter)

    def body(*refs):
        ins, outs, sems = refs[:ride.n], refs[ride.n:2 * ride.n], refs[2 * ride.n:]
        ride.start(ins, outs, sems)
        ride.wait(ins, outs, sems)

    return pl.pallas_call(body, name=name, in_specs=ride.in_specs, out_specs=ride.out_specs, out_shape=ride.out_shape,
                          scratch_shapes=ride.scratch)(*arrays)


def _with_ride(body, ride, n_in, n_out, grid):
    if ride is None:
        return body
    n = ride.n

    def wrapped(*refs):
        ins, rins = refs[:n_in], refs[n_in:n_in + n]
        outs, routs = refs[n_in + n:n_in + n + n_out], refs[n_in + n + n_out:n_in + 2 * n + n_out]
        rest = refs[n_in + 2 * n + n_out:]
        scratch, sems = rest[:len(rest) - 3], rest[len(rest) - 3:]
        ids = [pl.program_id(d) for d in range(len(grid))]
        first = functools.reduce(lambda p, q: p & q, [i == 0 for i in ids])
        last = functools.reduce(lambda p, q: p & q, [i == g - 1 for i, g in zip(ids, grid)])

        @pl.when(first)
        def _():
            ride.start(rins, routs, sems)

        body(*ins, *outs, *scratch)

        @pl.when(last)
        def _():
            ride.wait(rins, routs, sems)

    return wrapped


def _ride_call(body, ride, *, name, grid, in_specs, out_specs, out_shape, scratch_shapes, sem, args):
    n_in, n_out = len(in_specs), len(out_specs)
    if ride is None:
        res = pl.pallas_call(body, name=name, grid=grid, in_specs=in_specs, out_specs=out_specs, out_shape=out_shape,
                             scratch_shapes=scratch_shapes, compiler_params=_params(sem))(*args)
        return list(res), []
    res = pl.pallas_call(
        _with_ride(body, ride, n_in, n_out, grid), name=name, grid=grid,
        in_specs=list(in_specs) + ride.in_specs, out_specs=list(out_specs) + ride.out_specs,
        out_shape=list(out_shape) + ride.out_shape, scratch_shapes=list(scratch_shapes) + ride.scratch,
        compiler_params=_params(("arbitrary",) * len(grid)))(*args, *ride.arrays)
    return list(res[:n_out]), list(res[n_out:])


def _mm(a, b, *, name, ta=False, tb=False, out_dtype=F32, tm=1024, tn=1024, tk=1024, ride=None):
    m, k = (a.shape[1], a.shape[0]) if ta else a.shape
    n = b.shape[0] if tb else b.shape[1]
    assert (b.shape[1] if tb else b.shape[0]) == k
    tm, tn, tk = _tile(m, tm), _tile(n, tn), _tile(k, tk)
    nk = k // tk
    a_spec = pl.BlockSpec((tk, tm), lambda i, j, l: (l, i)) if ta else pl.BlockSpec((tm, tk), lambda i, j, l: (i, l))
    b_spec = pl.BlockSpec((tn, tk), lambda i, j, l: (j, l)) if tb else pl.BlockSpec((tk, tn), lambda i, j, l: (l, j))
    dims = (((0,) if ta else (1,), (1,) if tb else (0,)), ((), ()))

    def body(a_ref, b_ref, o_ref, *acc):
        p = lax.dot_general(a_ref[...].astype(BF16), b_ref[...].astype(BF16), dims, preferred_element_type=F32)
        if nk == 1:
            o_ref[...] = p.astype(out_dtype)
            return
        acc_ref = acc[0]
        step = pl.program_id(2)

        @pl.when(step == 0)
        def _():
            acc_ref[...] = p

        @pl.when(step > 0)
        def _():
            acc_ref[...] += p

        @pl.when(step == nk - 1)
        def _():
            o_ref[...] = acc_ref[...].astype(out_dtype)

    outs, rode = _ride_call(
        body, ride, name=name, grid=(m // tm, n // tn, nk), in_specs=[a_spec, b_spec],
        out_specs=[pl.BlockSpec((tm, tn), lambda i, j, l: (i, j))], out_shape=[jax.ShapeDtypeStruct((m, n), out_dtype)],
        scratch_shapes=[] if nk == 1 else [pltpu.VMEM((tm, tn), F32)], sem=("parallel", "parallel", "arbitrary"),
        args=(a, b))
    return outs[0] if ride is None else (outs[0], rode)


def _rowwise(name, fn, rows, vecs, outs, sums=(), tr=256):
    s = rows[0][0].shape[0]
    tr = min(tr, s)
    nr, nv, no = len(rows), len(vecs), len(outs)

    def body(*refs):
        vals = [r[...] for r in refs[:nr + nv]]
        res = fn(*vals)
        for o_ref, val in zip(refs[nr + nv:nr + nv + no], res[:no]):
            o_ref[...] = val.astype(o_ref.dtype)
        step = pl.program_id(0)
        for s_ref, val in zip(refs[nr + nv + no:], res[no:]):
            @pl.when(step == 0)
            def _(s_ref=s_ref, val=val):
                s_ref[...] = val

            @pl.when(step > 0)
            def _(s_ref=s_ref, val=val):
                s_ref[...] += val

    in_specs = [pl.BlockSpec((tr, w), functools.partial(lambda i, cb: (i, cb), cb=cb)) for _, w, cb in rows]
    in_specs += [pl.BlockSpec(v.shape, lambda i: (0, 0)) for v in vecs]
    out_specs = [pl.BlockSpec((tr, w), lambda i: (i, 0)) for w, _ in outs]
    out_specs += [pl.BlockSpec((1, w), lambda i: (0, 0)) for w in sums]
    out_shape = [jax.ShapeDtypeStruct((s, w), dt) for w, dt in outs]
    out_shape += [jax.ShapeDtypeStruct((1, w), F32) for w in sums]
    return pl.pallas_call(
        body, name=name, grid=(s // tr,), in_specs=in_specs, out_specs=out_specs, out_shape=out_shape,
        compiler_params=_params(("arbitrary",) if sums else ("parallel",)),
    )(*[r[0] for r in rows], *vecs)


def _colsum(x):
    return jnp.sum(x, axis=0, keepdims=True)


def _sigmoid(x):
    return 1.0 / (1.0 + jnp.exp(-x))


def _log_sigmoid(x):
    return jnp.minimum(x, 0.0) - jnp.log(1.0 + jnp.exp(-jnp.abs(x)))


def _ln_stats(r):
    mu = jnp.mean(r, axis=-1, keepdims=True)
    xc = r - mu
    var = jnp.mean(xc * xc, axis=-1, keepdims=True)
    rstd = lax.rsqrt(var + LN_EPS)
    return xc * rstd, rstd


def _ln_bwd(dy, xhat, rstd, g):
    dxh = dy * g
    m1 = jnp.mean(dxh, axis=-1, keepdims=True)
    m2 = jnp.mean(dxh * xhat, axis=-1, keepdims=True)
    return rstd * (dxh - m1 - xhat * m2)


def _mod_exchange(c_row, w_ada, b_ada_loc):
    def body(c_ref, w_ref, b_ref, call_ref, mod_ref, piece_ref, send_sems, recv_sems):
        me = _me()[3]
        call_ref[me] = c_ref[...]
        sent = []
        for r in range(1, N_DEV):
            peer, _ = _peer(r)
            cp = pltpu.make_async_remote_copy(
                src_ref=c_ref, dst_ref=call_ref.at[me], send_sem=send_sems.at[0, r - 1],
                recv_sem=recv_sems.at[0, r - 1], device_id=peer, device_id_type=MESH)
            cp.start()
            sent.append(cp)
        for r in range(1, N_DEV):
            peer, pidx = _peer(r)
            pltpu.make_async_remote_copy(
                src_ref=c_ref, dst_ref=call_ref.at[pidx], send_sem=send_sems.at[0, r - 1],
                recv_sem=recv_sems.at[0, r - 1], device_id=peer, device_id_type=MESH).wait_recv()
        c_all = jnp.concatenate([call_ref[d] for d in range(N_DEV)], axis=0)
        mod_loc = jnp.dot(c_all, w_ref[...], preferred_element_type=F32,
                          precision=lax.Precision.HIGHEST) + b_ref[...]
        for d in range(N_DEV):
            piece_ref[d] = mod_loc[d:d + 1, :]
        mod_ref[me] = piece_ref[me]
        for r in range(1, N_DEV):
            peer, pidx = _peer(r)
            cp = pltpu.make_async_remote_copy(
                src_ref=piece_ref.at[pidx], dst_ref=mod_ref.at[me], send_sem=send_sems.at[1, r - 1],
                recv_sem=recv_sems.at[1, r - 1], device_id=peer, device_id_type=MESH)
            cp.start()
            sent.append(cp)
        for r in range(1, N_DEV):
            peer, pidx = _peer(r)
            pltpu.make_async_remote_copy(
                src_ref=piece_ref.at[me], dst_ref=mod_ref.at[pidx], send_sem=send_sems.at[1, r - 1],
                recv_sem=recv_sems.at[1, r - 1], device_id=peer, device_id_type=MESH).wait_recv()
        for cp in sent:
            cp.wait_send()

    vmem = pl.BlockSpec(memory_space=pltpu.VMEM)
    return pl.pallas_call(
        body, name="mod_exchange", in_specs=[vmem, vmem, vmem], out_specs=[vmem, vmem],
        out_shape=[jax.ShapeDtypeStruct((N_DEV, 1, D_MODEL), F32), jax.ShapeDtypeStruct((N_DEV, 1, ADA_SHARD), F32)],
        scratch_shapes=[pltpu.VMEM((N_DEV, 1, ADA_SHARD), F32),
                        pltpu.SemaphoreType.DMA((2, N_DEV - 1)), pltpu.SemaphoreType.DMA((2, N_DEV - 1))],
        compiler_params=_params(),
    )(c_row, w_ada, b_ada_loc)


def _split3(x):
    hi = x.astype(BF16)
    r1 = x - hi.astype(F32)
    mid = r1.astype(BF16)
    lo = (r1 - mid.astype(F32)).astype(BF16)
    return hi, mid, lo


def _scan_rows(x_ref, o_ref, s, reverse, pre=None, post=None):
    tb = min(TQ, s)
    nb = s // tb
    row = lax.broadcasted_iota(jnp.int32, (tb, tb), 0)
    col = lax.broadcasted_iota(jnp.int32, (tb, tb), 1)
    tri = jnp.where((col >= row) if reverse else (col <= row), 1.0, 0.0).astype(BF16)

    def step(i, carry):
        blk = (nb - 1 - i) if reverse else i
        off = pl.multiple_of(blk * tb, tb)
        x = x_ref[pl.ds(off, tb), :]
        if pre is not None:
            x = pre(x, off)
        acc = carry
        for piece in _split3(x):
            acc = acc + jnp.dot(tri, piece, preferred_element_type=F32)
        o_ref[pl.ds(off, tb), :] = acc if post is None else post(acc, off)
        edge = acc[0:1, :] if reverse else acc[tb - 1:tb, :]
        return jnp.broadcast_to(edge, (tb, LANES))

    lax.fori_loop(0, nb, step, jnp.zeros((tb, LANES), F32))


def _forget_cumsum(f_raw, b_pad):
    s = f_raw.shape[0]

    def body(f_ref, b_ref, cum_ref):
        b = b_ref[...]
        _scan_rows(f_ref, cum_ref, s, False, pre=lambda x, off: _log_sigmoid(x + b))

    vmem = pl.BlockSpec(memory_space=pltpu.VMEM)
    return pl.pallas_call(body, name="forget_cumsum", in_specs=[vmem, vmem], out_specs=vmem,
                          out_shape=jax.ShapeDtypeStruct((s, LANES), F32), compiler_params=_params())(f_raw, b_pad)


def _forget_bwd(dcum, f_raw, b_pad):
    s = f_raw.shape[0]

    def body(d_ref, f_ref, b_ref, df_ref, db_ref, tmp_ref):
        b = b_ref[...]
        _scan_rows(d_ref, tmp_ref, s, True)
        df = tmp_ref[...] * _sigmoid(-(f_ref[...] + b))
        df_ref[...] = df.astype(BF16)
        db_ref[...] = _colsum(df)

    vmem = pl.BlockSpec(memory_space=pltpu.VMEM)
    return pl.pallas_call(
        body, name="forget_bwd", in_specs=[vmem, vmem, vmem], out_specs=[vmem, vmem],
        out_shape=[jax.ShapeDtypeStruct((s, LANES), BF16), jax.ShapeDtypeStruct((1, LANES), F32)],
        scratch_shapes=[pltpu.VMEM((s, LANES), F32)], compiler_params=_params())(dcum, f_raw, b_pad)


def _dot_nt(a, b):
    return lax.dot_general(a, b, (((1,), (1,)), ((), ())), preferred_element_type=F32)


def _dot_tn(a, b):
    return lax.dot_general(a, b, (((0,), (0,)), ((), ())), preferred_element_type=F32)


def _head_masks():
    lane = lax.broadcasted_iota(jnp.int32, (TQ, LANES), 1)
    return lane, [lane < HEAD_DIM, lane >= HEAD_DIM]


def _pick(mask, x):
    return jnp.where(mask, x, jnp.zeros_like(x))


def _qkv_specs(s, col0):
    nb = ATTN_W // LANES
    return [pl.BlockSpec((TQ, LANES), lambda hp, qi: (qi, col0 + hp)),
            pl.BlockSpec((s, LANES), lambda hp, qi: (0, col0 + nb + hp)),
            pl.BlockSpec((s, LANES), lambda hp, qi: (0, col0 + 2 * nb + hp))]


def _pair_spec():
    return pl.BlockSpec((TQ, LANES), lambda hp, qi: (qi, hp))


def _fox_fwd(qkv, cum_col, cum_row, ride=None):
    s = qkv.shape[0]
    nq = s // TQ

    def body(q_ref, k_ref, v_ref, cc_ref, cr_ref, o_ref, o32_ref, lse_ref):
        hp, qi = pl.program_id(0), pl.program_id(1)
        lane, masks = _head_masks()
        row = lax.broadcasted_iota(jnp.int32, (TQ, TQ), 0)
        col = lax.broadcasted_iota(jnp.int32, (TQ, TQ), 1)
        causal = col <= row
        q2 = q_ref[...] * jnp.asarray(ATTN_SCALE, BF16)
        cc = cc_ref[...]
        qms = [_pick(masks[e], q2) for e in range(2)]
        cqs = [jnp.sum(jnp.where(lane == 2 * hp + e, cc, 0.0), axis=1, keepdims=True) for e in range(2)]

        def tile(kb, carry, masked):
            off = pl.multiple_of(kb * TQ, TQ)
            k2, v2 = k_ref[pl.ds(off, TQ), :], v_ref[pl.ds(off, TQ), :]
            new = []
            for e in range(2):
                m, l, acc = carry[e]
                sc = _dot_nt(qms[e], k2) + (cqs[e] - cr_ref[e:e + 1, pl.ds(off, TQ)])
                if masked:
                    sc = jnp.where(causal, sc, -jnp.inf)
                m_new = jnp.maximum(m, jnp.max(sc, axis=1, keepdims=True))
                p = jnp.exp(sc - m_new)
                corr = jnp.exp(m - m_new)
                l = corr * l + jnp.sum(p, axis=1, keepdims=True)
                acc = corr * acc + jnp.dot(p.astype(BF16), v2, preferred_element_type=F32)
                new.append((m_new, l, acc))
            return tuple(new)

        init = (jnp.full((TQ, 1), -jnp.inf, F32), jnp.zeros((TQ, 1), F32), jnp.zeros((TQ, LANES), F32))
        carry = lax.fori_loop(0, qi, lambda kb, cr: tile(kb, cr, False), tile(qi, (init, init), True))
        outs = [acc / l for _, l, acc in carry]
        lses = [m + jnp.log(l) for m, l, _ in carry]
        out = jnp.where(masks[0], outs[0], outs[1])
        o_ref[...] = out.astype(BF16)
        o32_ref[...] = out
        lse_ref[...] = jnp.where(masks[0], lses[0], lses[1])

    return _ride_call(
        body, ride, name="fox_fwd", grid=(N_HEADS // 2, nq),
        in_specs=_qkv_specs(s, 0) + [pl.BlockSpec((TQ, LANES), lambda hp, qi: (qi, 0)),
                                     pl.BlockSpec((None, 2, s), lambda hp, qi: (hp, 0, 0))],
        out_specs=[_pair_spec(), _pair_spec(), _pair_spec()],
        out_shape=[jax.ShapeDtypeStruct((s, ATTN_W), BF16), jax.ShapeDtypeStruct((s, ATTN_W), F32),
                   jax.ShapeDtypeStruct((s, ATTN_W), F32)],
        scratch_shapes=[], sem=("parallel", "parallel"), args=(qkv, qkv, qkv, cum_col, cum_row))


def _fox_bwd(qkv, cum_col, cum_row, o, lse, do, ride=None):
    s = qkv.shape[0]
    nq = s // TQ

    def body(q_ref, k_ref, v_ref, cc_ref, cr_ref, o_ref, lse_ref, do_ref,
             dq_ref, dk_ref, dv_ref, dcr_ref, dcq_ref, dk_acc, dv_acc):
        hp, qi = pl.program_id(0), pl.program_id(1)

        @pl.when(qi == 0)
        def _():
            dk_acc[...] = jnp.zeros_like(dk_acc)
            dv_acc[...] = jnp.zeros_like(dv_acc)
            dcr_ref[...] = jnp.zeros_like(dcr_ref)

        lane, masks = _head_masks()
        row = lax.broadcasted_iota(jnp.int32, (TQ, TQ), 0)
        col = lax.broadcasted_iota(jnp.int32, (TQ, TQ), 1)
        causal = col <= row
        q2 = q_ref[...] * jnp.asarray(ATTN_SCALE, BF16)
        do2 = do_ref[...]
        prod = do2.astype(F32) * o_ref[...].astype(F32)
        lse2 = lse_ref[...]
        cc = cc_ref[...]
        qms = [_pick(masks[e], q2) for e in range(2)]
        doms = [_pick(masks[e], do2) for e in range(2)]
        deltas = [jnp.sum(jnp.where(masks[e], prod, 0.0), axis=1, keepdims=True) for e in range(2)]
        lses = [jnp.max(jnp.where(masks[e], lse2, -jnp.inf), axis=1, keepdims=True) for e in range(2)]
        cqs = [jnp.sum(jnp.where(lane == 2 * hp + e, cc, 0.0), axis=1, keepdims=True) for e in range(2)]

        def tile(kb, carry, masked):
            off = pl.multiple_of(kb * TQ, TQ)
            k2, v2 = k_ref[pl.ds(off, TQ), :], v_ref[pl.ds(off, TQ), :]
            new, dk, dv = [], None, None
            for e in range(2):
                dq, rowsum = carry[e]
                sc = _dot_nt(qms[e], k2) + (cqs[e] - cr_ref[e:e + 1, pl.ds(off, TQ)])
                p = jnp.exp(sc - lses[e])
                if masked:
                    p = jnp.where(causal, p, 0.0)
                ds = p * (_dot_nt(doms[e], v2) - deltas[e])
                dsb = ds.astype(BF16)
                dk_e, dv_e = _dot_tn(dsb, qms[e]), _dot_tn(p.astype(BF16), doms[e])
                dk, dv = (dk_e, dv_e) if e == 0 else (dk + dk_e, dv + dv_e)
                dcr_ref[e:e + 1, pl.ds(off, TQ)] -= _colsum(ds)
                new.append((dq + jnp.dot(dsb, k2, preferred_element_type=F32),
                            rowsum + jnp.sum(ds, axis=1, keepdims=True)))
            dk_acc[pl.ds(off, TQ), :] += dk
            dv_acc[pl.ds(off, TQ), :] += dv
            return tuple(new)

        init = (jnp.zeros((TQ, LANES), F32), jnp.zeros((TQ, 1), F32))
        carry = tile(qi, lax.fori_loop(0, qi, lambda kb, cr: tile(kb, cr, False), (init, init)), True)
        dq_ref[...] = (jnp.where(masks[0], carry[0][0], carry[1][0]) * ATTN_SCALE).astype(BF16)
        dcq_ref[...] = jnp.where(masks[0], carry[0][1], carry[1][1])

        @pl.when(qi == nq - 1)
        def _():
            dk_ref[...] = dk_acc[...].astype(BF16)
            dv_ref[...] = dv_acc[...].astype(BF16)

    seq_spec = pl.BlockSpec((s, LANES), lambda hp, qi: (0, hp))
    return _ride_call(
        body, ride, name="fox_bwd", grid=(N_HEADS // 2, nq),
        in_specs=_qkv_specs(s, 0) + [pl.BlockSpec((TQ, LANES), lambda hp, qi: (qi, 0)),
                                     pl.BlockSpec((None, 2, s), lambda hp, qi: (hp, 0, 0)),
                                     _pair_spec(), _pair_spec(), _pair_spec()],
        out_specs=[_pair_spec(), seq_spec, seq_spec, pl.BlockSpec((None, 2, s), lambda hp, qi: (hp, 0, 0)),
                   _pair_spec()],
        out_shape=[jax.ShapeDtypeStruct((s, ATTN_W), BF16)] * 3 + [jax.ShapeDtypeStruct((N_HEADS // 2, 2, s), F32),
                                                                    jax.ShapeDtypeStruct((s, ATTN_W), F32)],
        scratch_shapes=[pltpu.VMEM((s, LANES), F32), pltpu.VMEM((s, LANES), F32)],
        sem=("parallel", "arbitrary"), args=(qkv, qkv, qkv, cum_col, cum_row, o, lse, do))


def _split2(x):
    hi = x.astype(BF16)
    return hi, (x - hi.astype(F32)).astype(BF16)


def _scan_matrix(reverse):
    row = lax.broadcasted_iota(jnp.int32, (SCAN_W, SCAN_W), 0)
    col = lax.broadcasted_iota(jnp.int32, (SCAN_W, SCAN_W), 1)
    return jnp.where((row > col) if reverse else (row < col), 1.0, 0.0).astype(BF16)


def _scan_cols(x, tri, reverse):
    nblk = x.shape[1] // SCAN_W
    parts, total = [None] * nblk, None
    for b in (reversed(range(nblk)) if reverse else range(nblk)):
        blk = x[:, b * SCAN_W:(b + 1) * SCAN_W]
        hi, low = _split2(blk)
        part = jnp.dot(hi, tri, preferred_element_type=F32) + jnp.dot(low, tri, preferred_element_type=F32)
        parts[b] = part if total is None else part + total
        rowsum = jnp.sum(blk, axis=1, keepdims=True)
        total = rowsum if total is None else total + rowsum
    return (parts[0] if nblk == 1 else jnp.concatenate(parts, axis=1)), total


def _sb_logits(qm, k2):
    z = _dot_nt(qm, k2)
    soft = jnp.log(1.0 + jnp.exp(-jnp.abs(z)))
    lb = jnp.minimum(z, 0.0) - soft
    return lb, lb - z


def _sb_fwd(qkv):
    s = qkv.shape[0]
    nq = s // TQ
    assert nq <= LANES

    def body(q_ref, k_ref, v_ref, o_ref, r_ref):
        qi = pl.program_id(1)
        lane, masks = _head_masks()
        row = lax.broadcasted_iota(jnp.int32, (TQ, TQ), 0)
        col = lax.broadcasted_iota(jnp.int32, (TQ, TQ), 1)
        strict = col < row
        suffix = _scan_matrix(True)
        q2 = q_ref[...] * jnp.asarray(ATTN_SCALE, BF16)
        qms = [_pick(masks[e], q2) for e in range(2)]

        def tile(kb, carry, masked):
            off = pl.multiple_of(kb * TQ, TQ)
            k2, v2 = k_ref[pl.ds(off, TQ), :], v_ref[pl.ds(off, TQ), :]
            new = []
            for e in range(2):
                run, saved, acc = carry[e]
                lb, lo = _sb_logits(qms[e], k2)
                if masked:
                    lo = jnp.where(strict, lo, 0.0)
                rest, total = _scan_cols(lo, suffix, True)
                a = jnp.exp(lb + rest + run)
                if masked:
                    a = jnp.where(strict, a, 0.0)
                acc = acc + jnp.dot(a.astype(BF16), v2, preferred_element_type=F32)
                new.append((run + total, jnp.where(lane == kb, run, saved), acc))
            return tuple(new)

        init = (jnp.zeros((TQ, 1), F32), jnp.zeros((TQ, LANES), F32), jnp.zeros((TQ, LANES), F32))
        carry = lax.fori_loop(0, qi, lambda it, cr: tile(qi - 1 - it, cr, False), tile(qi, (init, init), True))
        for e in range(2):
            r_ref[:, e * LANES:(e + 1) * LANES] = carry[e][1]
        o_ref[...] = jnp.where(masks[0], carry[0][2], carry[1][2]).astype(BF16)

    return pl.pallas_call(
        body, name="sb_fwd", grid=(N_HEADS // 2, nq), in_specs=_qkv_specs(s, 3 * ATTN_W // LANES),
        out_specs=[_pair_spec(), pl.BlockSpec((TQ, 2 * LANES), lambda hp, qi: (qi, hp))],
        out_shape=[jax.ShapeDtypeStruct((s, ATTN_W), BF16), jax.ShapeDtypeStruct((s, N_HEADS * LANES), F32)],
        compiler_params=_params(("parallel", "parallel")),
    )(qkv, qkv, qkv)


def _sb_bwd(qkv, saved_run, do):
    s = qkv.shape[0]
    nq = s // TQ

    def body(q_ref, k_ref, v_ref, r_ref, do_ref, dq_ref, dk_ref, dv_ref, dk_acc, dv_acc):
        qi = pl.program_id(1)

        @pl.when(qi == 0)
        def _():
            dk_acc[...] = jnp.zeros_like(dk_acc)
            dv_acc[...] = jnp.zeros_like(dv_acc)

        lane, masks = _head_masks()
        row = lax.broadcasted_iota(jnp.int32, (TQ, TQ), 0)
        col = lax.broadcasted_iota(jnp.int32, (TQ, TQ), 1)
        strict = col < row
        suffix, prefix = _scan_matrix(True), _scan_matrix(False)
        q2 = q_ref[...] * jnp.asarray(ATTN_SCALE, BF16)
        do2 = do_ref[...]
        qms = [_pick(masks[e], q2) for e in range(2)]
        doms = [_pick(masks[e], do2) for e in range(2)]
        saveds = [r_ref[:, e * LANES:(e + 1) * LANES] for e in range(2)]

        def tile(kb, carry, masked):
            off = pl.multiple_of(kb * TQ, TQ)
            k2, v2 = k_ref[pl.ds(off, TQ), :], v_ref[pl.ds(off, TQ), :]
            new, dk, dv = [], None, None
            for e in range(2):
                gsum, dq = carry[e]
                lb, lo = _sb_logits(qms[e], k2)
                beta = jnp.exp(lb)
                if masked:
                    lo = jnp.where(strict, lo, 0.0)
                rest, _ = _scan_cols(lo, suffix, True)
                run = jnp.sum(jnp.where(lane == kb, saveds[e], 0.0), axis=1, keepdims=True)
                a = jnp.exp(lb + rest + run)
                if masked:
                    a = jnp.where(strict, a, 0.0)
                g = a * _dot_nt(doms[e], v2)
                before, gtotal = _scan_cols(g, prefix, False)
                dz = g * (1.0 - beta) - beta * (before + gsum)
                if masked:
                    dz = jnp.where(strict, dz, 0.0)
                dzb = dz.astype(BF16)
                dk_e, dv_e = _dot_tn(dzb, qms[e]), _dot_tn(a.astype(BF16), doms[e])
                dk, dv = (dk_e, dv_e) if e == 0 else (dk + dk_e, dv + dv_e)
                new.append((gsum + gtotal, dq + jnp.dot(dzb, k2, preferred_element_type=F32)))
            dk_acc[pl.ds(off, TQ), :] += dk
            dv_acc[pl.ds(off, TQ), :] += dv
            return tuple(new)

        init = (jnp.zeros((TQ, 1), F32), jnp.zeros((TQ, LANES), F32))
        carry = tile(qi, lax.fori_loop(0, qi, lambda kb, cr: tile(kb, cr, False), (init, init)), True)
        dq_ref[...] = (jnp.where(masks[0], carry[0][1], carry[1][1]) * ATTN_SCALE).astype(BF16)

        @pl.when(qi == nq - 1)
        def _():
            dk_ref[...] = dk_acc[...].astype(BF16)
            dv_ref[...] = dv_acc[...].astype(BF16)

    seq_spec = pl.BlockSpec((s, LANES), lambda hp, qi: (0, hp))
    return pl.pallas_call(
        body, name="sb_bwd", grid=(N_HEADS // 2, nq),
        in_specs=_qkv_specs(s, 3 * ATTN_W // LANES) + [pl.BlockSpec((TQ, 2 * LANES), lambda hp, qi: (qi, hp)),
                                                        _pair_spec()],
        out_specs=[_pair_spec(), seq_spec, seq_spec],
        out_shape=[jax.ShapeDtypeStruct((s, ATTN_W), BF16)] * 3,
        scratch_shapes=[pltpu.VMEM((s, LANES), F32), pltpu.VMEM((s, LANES), F32)],
        compiler_params=_params(("parallel", "arbitrary")),
    )(qkv, qkv, qkv, saved_run, do)


CONV_TR = 256


def _shift_down(x, halo, n):
    rolled = pltpu.roll(x, n, 0)
    rid = lax.broadcasted_iota(jnp.int32, x.shape, 0)
    for j in range(n):
        rolled = jnp.where(rid == j, halo[8 - n + j:8 - n + j + 1, :], rolled)
    return rolled


def _shift_up(x, halo, n):
    rows = x.shape[0]
    rolled = pltpu.roll(x, rows - n, 0)
    rid = lax.broadcasted_iota(jnp.int32, x.shape, 0)
    for j in range(n):
        rolled = jnp.where(rid == rows - n + j, halo[j:j + 1, :], rolled)
    return rolled


def _conv_fwd_block(x, halo, w, b):
    return b + _shift_down(x, halo, 2) * w[0:1, :] + _shift_down(x, halo, 1) * w[1:2, :] + x * w[2:3, :]


def _conv_specs(tr, s):
    pair = 2 * FF_HALF
    blk = pl.BlockSpec((tr, pair), lambda j, i: (i, j))
    prev = pl.BlockSpec((8, pair), lambda j, i: (jnp.maximum(i * (tr // 8) - 1, 0), j))
    nxt = pl.BlockSpec((8, pair), lambda j, i: (jnp.minimum((i + 1) * (tr // 8), s // 8 - 1), j))
    return blk, prev, nxt


def _conv_gate_fwd(hpre, conv_w, conv_b):
    s = hpre.shape[0]
    tr = min(CONV_TR, s)
    blk, prev, _ = _conv_specs(tr, s)

    def body(x_ref, halo_ref, w_ref, b_ref, a_ref):
        i = pl.program_id(1)
        halo = jnp.where(i > 0, halo_ref[...], 0.0)
        h = _conv_fwd_block(x_ref[...], halo, w_ref[...], b_ref[...])
        hg, hv = h[:, :FF_HALF], h[:, FF_HALF:]
        a_ref[...] = (hg * _sigmoid(hg) * hv).astype(BF16)

    return pl.pallas_call(
        body, name="conv_gate_fwd", grid=(2, s // tr),
        in_specs=[blk, prev, pl.BlockSpec((3, 2 * FF_HALF), lambda j, i: (0, j)),
                  pl.BlockSpec((1, 2 * FF_HALF), lambda j, i: (0, j))],
        out_specs=pl.BlockSpec((tr, FF_HALF), lambda j, i: (i, j)),
        out_shape=jax.ShapeDtypeStruct((s, D_FF), BF16),
        compiler_params=_params(("parallel", "parallel")),
    )(hpre, hpre, conv_w, conv_b)


def _conv_gate_bwd(hpre, da, conv_w, conv_b):
    s = hpre.shape[0]
    tr = min(CONV_TR, s)
    blk, prev, _ = _conv_specs(tr, s)

    def body(x_ref, halo_ref, da_ref, w_ref, b_ref, dh_ref, db_ref, dw_ref):
        i = pl.program_id(1)
        halo = jnp.where(i > 0, halo_ref[...], 0.0)
        x = x_ref[...]
        h = _conv_fwd_block(x, halo, w_ref[...], b_ref[...])
        hg, hv = h[:, :FF_HALF], h[:, FF_HALF:]
        da_blk = da_ref[...]
        sg = _sigmoid(hg)
        dhg = da_blk * hv * (sg * (1.0 + hg * (1.0 - sg)))
        dhv = da_blk * (hg * sg)
        dh_ref[:, :FF_HALF] = dhg
        dh_ref[:, FF_HALF:] = dhv
        x2, x1 = _shift_down(x, halo, 2), _shift_down(x, halo, 1)
        parts = []
        for lo, dpart in ((0, dhg), (FF_HALF, dhv)):
            cols = slice(lo, lo + FF_HALF)
            parts.append((cols, _colsum(dpart), _colsum(dpart * x2[:, cols]), _colsum(dpart * x1[:, cols]),
                          _colsum(dpart * x[:, cols])))

        @pl.when(i == 0)
        def _():
            for cols, db, dw0, dw1, dw2 in parts:
                db_ref[:, cols] = db
                dw_ref[0:1, cols] = dw0
                dw_ref[1:2, cols] = dw1
                dw_ref[2:3, cols] = dw2

        @pl.when(i > 0)
        def _():
            for cols, db, dw0, dw1, dw2 in parts:
                db_ref[:, cols] += db
                dw_ref[0:1, cols] += dw0
                dw_ref[1:2, cols] += dw1
                dw_ref[2:3, cols] += dw2

    pair = 2 * FF_HALF
    return pl.pallas_call(
        body, name="conv_gate_bwd", grid=(2, s // tr),
        in_specs=[blk, prev, pl.BlockSpec((tr, FF_HALF), lambda j, i: (i, j)),
                  pl.BlockSpec((3, pair), lambda j, i: (0, j)), pl.BlockSpec((1, pair), lambda j, i: (0, j))],
        out_specs=[blk, pl.BlockSpec((1, pair), lambda j, i: (0, j)), pl.BlockSpec((3, pair), lambda j, i: (0, j))],
        out_shape=[jax.ShapeDtypeStruct((s, 2 * D_FF), F32), jax.ShapeDtypeStruct((1, 2 * D_FF), F32),
                   jax.ShapeDtypeStruct((3, 2 * D_FF), F32)],
        compiler_params=_params(("parallel", "arbitrary")),
    )(hpre, hpre, da, conv_w, conv_b)


def _conv_input_bwd(dh, conv_w):
    s = dh.shape[0]
    tr = min(CONV_TR, s)
    blk, _, nxt = _conv_specs(tr, s)
    nblk = s // tr

    def body(x_ref, halo_ref, w_ref, o_ref):
        i = pl.program_id(1)
        halo = jnp.where(i < nblk - 1, halo_ref[...], 0.0)
        x, w = x_ref[...], w_ref[...]
        o_ref[...] = (x * w[2:3, :] + _shift_up(x, halo, 1) * w[1:2, :] + _shift_up(x, halo, 2) * w[0:1, :]).astype(BF16)

    return pl.pallas_call(
        body, name="conv_input_bwd", grid=(2, nblk),
        in_specs=[blk, nxt, pl.BlockSpec((3, 2 * FF_HALF), lambda j, i: (0, j))], out_specs=blk,
        out_shape=jax.ShapeDtypeStruct((s, 2 * D_FF), BF16),
        compiler_params=_params(("parallel", "parallel")),
    )(dh, dh, conv_w)


def _adamw_math(w, g, m, v):
    m = ADAM_B1 * m + (1.0 - ADAM_B1) * g
    v = ADAM_B2 * v + (1.0 - ADAM_B2) * (g * g)
    m_hat = m / (1.0 - ADAM_B1 ** ADAM_STEP)
    v_hat = v / (1.0 - ADAM_B2 ** ADAM_STEP)
    delta = -ADAM_LR * (m_hat / (jnp.sqrt(v_hat) + ADAM_EPS) + ADAM_WD * w)
    return delta, m, v


def _adamw(name, g8, w, m, v):
    r, c = w.shape
    tr = _row_tile(r, c)

    def body(g_ref, w_ref, m_ref, v_ref, go_ref, d_ref, mo_ref, vo_ref):
        g = g_ref[0].astype(F32)
        for d in range(1, N_DEV):
            g = g + g_ref[d].astype(F32)
        delta, mn, vn = _adamw_math(w_ref[...], g, m_ref[...], v_ref[...])
        go_ref[...] = g
        d_ref[...] = delta
        mo_ref[...] = mn
        vo_ref[...] = vn

    spec = pl.BlockSpec((tr, c), lambda i: (i, 0))
    return pl.pallas_call(
        body, name=name, grid=(r // tr,),
        in_specs=[pl.BlockSpec((N_DEV, tr, c), lambda i: (0, i, 0)), spec, spec, spec], out_specs=[spec] * 4,
        out_shape=[jax.ShapeDtypeStruct((r, c), F32)] * 4, compiler_params=_params(("parallel",)),
    )(g8, w, m, v)


def _adamw_ada(c_t, dmod, w, m, v):
    r, c = w.shape
    tr = _row_tile(r, c)

    def body(ct_ref, dm_ref, w_ref, m_ref, v_ref, go_ref, d_ref, mo_ref, vo_ref):
        ct, dm = ct_ref[...], dm_ref[...]
        g = ct[:, 0:1] * dm[0:1, :]
        for b in range(1, N_DEV):
            g = g + ct[:, b:b + 1] * dm[b:b + 1, :]
        delta, mn, vn = _adamw_math(w_ref[...], g, m_ref[...], v_ref[...])
        go_ref[...] = g
        d_ref[...] = delta
        mo_ref[...] = mn
        vo_ref[...] = vn

    spec = pl.BlockSpec((tr, c), lambda i: (i, 0))
    return pl.pallas_call(
        body, name="adamw_w_ada", grid=(r // tr,),
        in_specs=[pl.BlockSpec((tr, N_DEV), lambda i: (i, 0)), pl.BlockSpec((N_DEV, c), lambda i: (0, 0)),
                  spec, spec, spec],
        out_specs=[spec] * 4, out_shape=[jax.ShapeDtypeStruct((r, c), F32)] * 4,
        compiler_params=_params(("parallel",)),
    )(c_t, dmod, w, m, v)


def _cols_from_slots(g):
    n, r, c = g.shape
    return jnp.transpose(g, (1, 0, 2)).reshape(r, n * c)


def _cols_to_slots(w):
    r, c = w.shape
    return jnp.transpose(w.reshape(r, N_DEV, c // N_DEV), (1, 0, 2))


def _pair_cols(w):
    g0, g1 = w[..., 0:FF_HALF], w[..., FF_HALF:D_FF]
    v0, v1 = w[..., D_FF:D_FF + FF_HALF], w[..., D_FF + FF_HALF:]
    return jnp.concatenate([g0, v0, g1, v1], axis=-1)


def _unpair_cols(w):
    g0, v0 = w[..., 0:FF_HALF], w[..., FF_HALF:D_FF]
    g1, v1 = w[..., D_FF:D_FF + FF_HALF], w[..., D_FF + FF_HALF:]
    return jnp.concatenate([g0, g1, v0, v1], axis=-1)


def _row(v):
    return v.reshape(1, -1)


def kernel(x, c, w_ada, b_ada, w_in, b_forget, w_fox_proj, w_sb_proj, w_o, ln1_g, ln1_b, w_up, conv_w, conv_b, w_down, ln2_g, ln2_b, loss_target, m_w_ada, m_b_ada, m_w_in, m_b_forget, m_w_fox_proj, m_w_sb_proj, m_w_o, m_ln1_g, m_ln1_b, m_w_up, m_conv_w, m_conv_b, m_w_down, m_ln2_g, m_ln2_b, v_w_ada, v_b_ada, v_w_in, v_b_forget, v_w_fox_proj, v_w_sb_proj, v_w_o, v_ln1_g, v_ln1_b, v_w_up, v_conv_w, v_conv_b, v_w_down, v_ln2_g, v_ln2_b):
    s = x.shape[1]
    me = 4 * lax.axis_index("x") + 2 * lax.axis_index("y") + lax.axis_index("c")
    x2 = x.reshape(s, D_MODEL)
    tgt = loss_target.reshape(s, D_MODEL)

    b_ada_loc = lax.dynamic_slice(b_ada, (me * ADA_SHARD,), (ADA_SHARD,)).reshape(1, ADA_SHARD)
    c_all, mod = _mod_exchange(c, w_ada, b_ada_loc)
    mod = mod.reshape(N_MOD, 1, D_MODEL)
    sh1, sc1, gt1, sh2, sc2, gt2 = [mod[i] for i in range(N_MOD)]

    (g_in,) = _exchange("ag_w_in", [w_in.astype(BF16)], scatter=False)
    late_weights = _Ride([w_fox_proj.astype(BF16), w_sb_proj.astype(BF16), w_o.astype(BF16), w_up.astype(BF16),
                          w_down.astype(BF16), conv_w], scatter=False)
    w_in_f = _cols_from_slots(g_in)
    w_proj = jnp.concatenate(
        [w_in_f[:, 0:1536], w_in_f[:, 1544:3080], w_in_f[:, 3080:5128], w_in_f[:, 1536:1544],
         jnp.zeros((D_MODEL, W_PROJ - 5128), BF16)], axis=1)
    w_qkv, w_gates, w_f = w_proj[:, :W_QKV], w_proj[:, W_QKV:W_QKV + W_GATES], w_proj[:, W_QKV + W_GATES:W_QKV + W_GATES + W_F]
    conv_b_p = _pair_cols(_row(conv_b))
    b_f_pad = jnp.pad(_row(b_forget), ((0, 0), (0, LANES - N_HEADS)))

    (u1,) = _rowwise("modulate1", lambda xb, sc, sh: (xb * (1.0 + sc) + sh,),
                     [(x2, D_MODEL, 0)], [sc1, sh1], [(D_MODEL, BF16)], tr=512)
    qkv = _mm(u1, w_qkv, name="mm_qkv", out_dtype=BF16)
    gates = _mm(u1, w_gates, name="mm_gates")
    f_raw = _mm(u1, w_f, name="mm_forget")
    cum_col = _forget_cumsum(f_raw, b_f_pad)
    cum_row = jnp.transpose(cum_col[:, :N_HEADS]).reshape(N_HEADS // 2, 2, s)
    (y_fox, y_fox32, lse), (g_fox, g_sb, g_o, g_up, g_down, g_cw) = _fox_fwd(qkv, cum_col, cum_row, ride=late_weights)
    w_fox_f = _cols_from_slots(g_fox)
    w_sb_f = _cols_from_slots(g_sb)
    w_o_f = g_o.reshape(D_MODEL, D_MODEL)
    w_up_p = _pair_cols(_cols_from_slots(g_up))
    w_down_f = g_down.reshape(D_FF, D_MODEL)
    conv_w_p = _pair_cols(_cols_from_slots(g_cw))
    y_sb, sb_run = _sb_fwd(qkv)
    pf = _mm(y_fox, w_fox_f, name="mm_fox_proj")
    ps = _mm(y_sb, w_sb_f, name="mm_sb_proj")
    (merged,) = _rowwise("gate_merge", lambda ga, gb, a, b: (_sigmoid(ga) * a + _sigmoid(gb) * b,),
                         [(gates, D_MODEL, 0), (gates, D_MODEL, 1), (pf, D_MODEL, 0), (ps, D_MODEL, 0)], [],
                         [(D_MODEL, BF16)])
    attn_out = _mm(merged, w_o_f, name="mm_w_o")

    def ln_fwd(xb, fb, gt, g, b):
        xhat, _ = _ln_stats(ALPHA * xb + (1.0 + gt) * fb)
        return xhat * g + b

    def ln1_mod(xb, fb, gt, g, b, sc, sh):
        y = ln_fwd(xb, fb, gt, g, b)
        return y, y * (1.0 + sc) + sh

    x1, u2 = _rowwise("ln1_modulate2", ln1_mod, [(x2, D_MODEL, 0), (attn_out, D_MODEL, 0)],
                      [gt1, _row(ln1_g), _row(ln1_b), sc2, sh2], [(D_MODEL, F32), (D_MODEL, BF16)])

    hpre = _mm(u2, w_up_p, name="mm_w_up", tn=1408)
    act = _conv_gate_fwd(hpre, conv_w_p, conv_b_p)
    ffn_out = _mm(act, w_down_f, name="mm_w_down", tk=1408)

    def ln2_bwd(xb, fb, tb, gt, g, b):
        xhat, rstd = _ln_stats(ALPHA * xb + (1.0 + gt) * fb)
        err = (xhat * g + b) - tb
        dy = err * (1.0 / D_MODEL)
        dr = _ln_bwd(dy, xhat, rstd, g)
        return (dr * (1.0 + gt), ALPHA * dr,
                _colsum(err * err), _colsum(dy * xhat), _colsum(dy), _colsum(dr * fb))

    dffn, dx1_res, sq_err, d_ln2_g, d_ln2_b, d_gt2 = _rowwise(
        "ln2_bwd", ln2_bwd, [(x1, D_MODEL, 0), (ffn_out, D_MODEL, 0), (tgt, D_MODEL, 0)],
        [gt2, _row(ln2_g), _row(ln2_b)], [(D_MODEL, BF16), (D_MODEL, F32)], sums=[D_MODEL] * 4)
    loss = lax.psum(0.5 * jnp.sum(sq_err) / D_MODEL, ("x", "y", "c"))

    d_w_down = _mm(act, dffn, name="mm_d_w_down", ta=True, tm=1408, out_dtype=BF16)
    d_act = _mm(dffn, w_down_f, name="mm_d_act", tb=True, tn=1408)
    dh, d_conv_b_p, d_conv_w_p = _conv_gate_bwd(hpre, d_act, conv_w_p, conv_b_p)
    dhpre = _conv_input_bwd(dh, conv_w_p)
    d_w_up_p = _mm(u2, dhpre, name="mm_d_w_up", ta=True, tn=1408, out_dtype=BF16)
    du2 = _mm(dhpre, w_up_p, name="mm_d_u2", tb=True, tk=1408)

    def ln1_bwd(du, dres, x1b, xb, fb, sc, gt, g):
        dx1 = dres + du * (1.0 + sc)
        xhat, rstd = _ln_stats(ALPHA * xb + (1.0 + gt) * fb)
        dr = _ln_bwd(dx1, xhat, rstd, g)
        return (dr * (1.0 + gt), ALPHA * dr,
                _colsum(du * x1b), _colsum(du), _colsum(dx1 * xhat), _colsum(dx1), _colsum(dr * fb))

    d_attn, dx_res, d_sc2, d_sh2, d_ln1_g, d_ln1_b, d_gt1 = _rowwise(
        "ln1_bwd", ln1_bwd,
        [(du2, D_MODEL, 0), (dx1_res, D_MODEL, 0), (x1, D_MODEL, 0), (x2, D_MODEL, 0), (attn_out, D_MODEL, 0)],
        [sc2, gt1, _row(ln1_g)], [(D_MODEL, BF16), (D_MODEL, F32)], sums=[D_MODEL] * 5)

    d_w_o = _mm(merged, d_attn, name="mm_d_w_o", ta=True, out_dtype=BF16)
    d_merged = _mm(d_attn, w_o_f, name="mm_d_merged", tb=True)

    def merge_bwd(dm, ga, gb, a, b):
        sa, sb = _sigmoid(ga), _sigmoid(gb)
        return dm * a * sa * (1.0 - sa), dm * b * sb * (1.0 - sb), dm * sa, dm * sb

    d_ga, d_gb, d_pf, d_ps = _rowwise(
        "gate_merge_bwd", merge_bwd,
        [(d_merged, D_MODEL, 0), (gates, D_MODEL, 0), (gates, D_MODEL, 1), (pf, D_MODEL, 0), (ps, D_MODEL, 0)], [],
        [(D_MODEL, BF16)] * 4)
    d_w_fox = _mm(y_fox, d_pf, name="mm_d_w_fox", ta=True, out_dtype=BF16)
    d_w_sb = _mm(y_sb, d_ps, name="mm_d_w_sb", ta=True, out_dtype=BF16)
    d_y_fox = _mm(d_pf, w_fox_f, name="mm_d_y_fox", tb=True, out_dtype=BF16)
    d_y_sb = _mm(d_ps, w_sb_f, name="mm_d_y_sb", tb=True, out_dtype=BF16)
    early_grads = _Ride(
        [_cols_to_slots(d_w_fox), _cols_to_slots(d_w_sb), d_w_o.reshape(N_DEV, D_MODEL // N_DEV, D_MODEL),
         _cols_to_slots(_unpair_cols(d_w_up_p)), d_w_down.reshape(N_DEV, D_FF // N_DEV, D_MODEL)], scatter=True)
    (dq_a, dk_a, dv_a, d_cum_row, d_cum_q), early_slots = _fox_bwd(qkv, cum_col, cum_row, y_fox32, lse, d_y_fox,
                                                                    ride=early_grads)
    dq_b, dk_b, dv_b = _sb_bwd(qkv, sb_run, d_y_sb)
    d_cum = jnp.transpose(d_cum_row.reshape(N_HEADS, s)) + d_cum_q[:, ::HEAD_DIM]
    d_cum = jnp.pad(d_cum, ((0, 0), (0, LANES - N_HEADS)))
    d_f, d_b_forget = _forget_bwd(d_cum, f_raw, b_f_pad)
    d_proj = jnp.concatenate([dq_a, dk_a, dv_a, dq_b, dk_b, dv_b, d_ga, d_gb, d_f,
                              jnp.zeros((s, W_PROJ - W_QKV - W_GATES - W_F), BF16)], axis=1)
    d_w_proj = _mm(u1, d_proj, name="mm_d_w_in", ta=True, tn=896, out_dtype=BF16)
    d_w_in_f = jnp.concatenate([d_w_proj[:, 0:1536], d_w_proj[:, 5120:5128], d_w_proj[:, 1536:3072],
                                d_w_proj[:, 3072:5120]], axis=1)
    du1, (in_slots,) = _mm(d_proj, w_proj, name="mm_d_u1", tb=True, tk=896,
                           ride=_Ride([_cols_to_slots(d_w_in_f)], scatter=True))

    def x_bwd(du, dres, xb, sc):
        return dres + du * (1.0 + sc), _colsum(du * xb), _colsum(du)

    grad_x, d_sc1, d_sh1 = _rowwise("x_bwd", x_bwd, [(du1, D_MODEL, 0), (dx_res, D_MODEL, 0), (x2, D_MODEL, 0)],
                                    [sc1], [(D_MODEL, F32)], sums=[D_MODEL] * 2, tr=512)

    d_conv_b = _unpair_cols(d_conv_b_p)
    d_conv_w = _unpair_cols(d_conv_w_p)
    n_rep = N_MOD * D_MODEL + LANES + 4 * D_MODEL + 2 * D_FF
    small = jnp.concatenate(
        [d_sh1, d_sc1, d_gt1, d_sh2, d_sc2, d_gt2, d_b_forget, d_ln1_g, d_ln1_b, d_ln2_g, d_ln2_b, d_conv_b,
         d_conv_w.reshape(1, 6 * D_FF)], axis=1)
    n_small = small.shape[1] // LANES
    small = jnp.pad(small.reshape(n_small, LANES), ((0, 264 - n_small), (0, 0)))
    (small_all,) = _exchange("ag_small_grads", [small], scatter=False)
    rep8 = small_all[:, :n_rep // LANES, :]
    cw8 = small_all[:, n_rep // LANES:n_small, :].reshape(N_DEV, 3, 2 * D_FF)
    cw8 = lax.dynamic_slice(cw8, (0, 0, me * UP_SHARD), (N_DEV, 3, UP_SHARD))
    dmod8 = small_all[:, :N_MOD * D_MODEL // LANES, :].reshape(N_DEV, N_MOD * D_MODEL)
    dmod_loc = lax.dynamic_slice(dmod8, (0, me * ADA_SHARD), (N_DEV, ADA_SHARD))

    def pack_rep(b_a, b_f, g1, b1, g2, b2, cb):
        flat = jnp.concatenate([b_a, jnp.pad(b_f, (0, LANES - N_HEADS)), g1, b1, g2, b2, cb])
        return flat.reshape(n_rep // LANES, LANES)

    rep = _adamw("adamw_small", rep8, pack_rep(b_ada, b_forget, ln1_g, ln1_b, ln2_g, ln2_b, conv_b),
                 pack_rep(m_b_ada, m_b_forget, m_ln1_g, m_ln1_b, m_ln2_g, m_ln2_b, m_conv_b),
                 pack_rep(v_b_ada, v_b_forget, v_ln1_g, v_ln1_b, v_ln2_g, v_ln2_b, v_conv_b))

    def unpack_rep(p):
        flat = p.reshape(-1)
        o = N_MOD * D_MODEL
        return {"b_ada": flat[:o], "b_forget": flat[o:o + N_HEADS],
                "ln1_g": flat[o + 128:o + 1152], "ln1_b": flat[o + 1152:o + 2176],
                "ln2_g": flat[o + 2176:o + 3200], "ln2_b": flat[o + 3200:o + 4224], "conv_b": flat[o + 4224:]}

    rep = [unpack_rep(p) for p in rep]
    r_conv_w = _adamw("adamw_conv_w", cw8, conv_w, m_conv_w, v_conv_w)
    r_ada = _adamw_ada(jnp.transpose(c_all.reshape(N_DEV, D_MODEL)), dmod_loc, w_ada, m_w_ada, v_w_ada)

    r_in = _adamw("adamw_w_in", in_slots, w_in, m_w_in, v_w_in)
    r_fox = _adamw("adamw_w_fox", early_slots[0], w_fox_proj, m_w_fox_proj, v_w_fox_proj)
    r_sb = _adamw("adamw_w_sb", early_slots[1], w_sb_proj, m_w_sb_proj, v_w_sb_proj)
    r_o = _adamw("adamw_w_o", early_slots[2], w_o, m_w_o, v_w_o)
    r_up = _adamw("adamw_w_up", early_slots[3], w_up, m_w_up, v_w_up)
    r_down = _adamw("adamw_w_down", early_slots[4], w_down, m_w_down, v_w_down)

    def leaf(i):
        return [r_ada[i], rep[i]["b_ada"], r_in[i], rep[i]["b_forget"], r_fox[i], r_sb[i], r_o[i], rep[i]["ln1_g"],
                rep[i]["ln1_b"], r_up[i], r_conv_w[i], rep[i]["conv_b"], r_down[i], rep[i]["ln2_g"], rep[i]["ln2_b"]]

    return (loss, grad_x.reshape(1, s, D_MODEL), *leaf(0), *leaf(1), *leaf(2), *leaf(3))
```

```python
import functools

import jax
import jax.numpy as jnp
from jax import lax
from jax.experimental import pallas as pl
from jax.experimental.pallas import tpu as pltpu

F32 = jnp.float32
BF16 = jnp.bfloat16
MESH = pl.DeviceIdType.MESH
ANY = pl.BlockSpec(memory_space=pl.ANY)

N_DEV = 8
D_MODEL = 1024
HEAD_DIM = 64
N_HEADS = 8
ATTN_W = N_HEADS * HEAD_DIM
D_FF = 2816
FF_HALF = D_FF // 2
N_MOD = 6
ADA_SHARD = N_MOD * D_MODEL // N_DEV
IN_SHARD = 641
UP_SHARD = 704
ATTN_SCALE = HEAD_DIM ** -0.5
ALPHA = 2.0 ** 0.25
LN_EPS = 1e-5
LANES = 128
TQ = 512
SCAN_W = 256
VMEM_LIMIT = 56 * 1024 * 1024

ADAM_LR, ADAM_B1, ADAM_B2, ADAM_EPS, ADAM_WD, ADAM_STEP = 0.001, 0.9, 0.999, 1e-08, 0.01, 10

W_QKV, W_GATES, W_F = 3072, 2048, 128
W_PROJ = 5376


def _params(sem=None):
    return pltpu.CompilerParams(dimension_semantics=sem, vmem_limit_bytes=VMEM_LIMIT)


def _tile(n, cap):
    if n <= cap:
        return n
    best = None
    for t in range(LANES, cap + 1, LANES):
        if n % t == 0:
            best = t
    assert best is not None, (n, cap)
    return best


def _row_tile(r, width, budget=192 * 1024):
    if r * width <= budget or r % 16:
        return r
    best = 16
    for t in range(16, r + 1, 16):
        if r % t == 0 and t * width <= budget:
            best = t
    return best


def _me():
    x, y, c = lax.axis_index("x"), lax.axis_index("y"), lax.axis_index("c")
    return x, y, c, 4 * x + 2 * y + c


def _peer(r):
    x, y, c, _ = _me()
    px = 1 - x if r & 4 else x
    py = 1 - y if r & 2 else y
    pc = 1 - c if r & 1 else c
    return (px, py, pc), 4 * px + 2 * py + pc


class _Ride:
    def __init__(self, arrays, scatter):
        self.arrays, self.scatter, self.n = list(arrays), scatter, len(arrays)
        self.in_specs = [ANY] * self.n
        self.out_specs = [ANY] * self.n
        self.out_shape = [jax.ShapeDtypeStruct(a.shape if scatter else (N_DEV,) + a.shape, a.dtype) for a in arrays]
        self.scratch = [pltpu.SemaphoreType.DMA((self.n, N_DEV - 1)), pltpu.SemaphoreType.DMA((self.n, N_DEV - 1)),
                        pltpu.SemaphoreType.DMA((self.n,))]

    def _local(self, ins, outs, sems, a):
        me = _me()[3]
        return pltpu.make_async_copy(ins[a].at[me] if self.scatter else ins[a], outs[a].at[me], sems[2].at[a])

    def _remote(self, ins, outs, sems, a, r, arriving):
        me = _me()[3]
        peer, pidx = _peer(r)
        src = ins[a].at[me if arriving else pidx] if self.scatter else ins[a]
        return pltpu.make_async_remote_copy(
            src_ref=src, dst_ref=outs[a].at[pidx if arriving else me], send_sem=sems[0].at[a, r - 1],
            recv_sem=sems[1].at[a, r - 1], device_id=peer, device_id_type=MESH)

    def start(self, ins, outs, sems):
        for a in range(self.n):
            self._local(ins, outs, sems, a).start()
        for r in range(1, N_DEV):
            for a in range(self.n):
                self._remote(ins, outs, sems, a, r, False).start()

    def wait(self, ins, outs, sems):
        for r in range(1, N_DEV):
            for a in range(self.n):
                self._remote(ins, outs, sems, a, r, True).wait_recv()
        for r in range(1, N_DEV):
            for a in range(self.n):
                self._remote(ins, outs, sems, a, r, False).wait_send()
        for a in range(self.n):
            self._local(ins, outs, sems, a).wait()


def _exchange(name, arrays, scatter):
    ride = _Ride(arrays, scatter)

    def body(*refs):
        ins, outs, sems = refs[:ride.n], refs[ride.n:2 * ride.n], refs[2 * ride.n:]
        ride.start(ins, outs, sems)
        ride.wait(ins, outs, sems)

    return pl.pallas_call(body, name=name, in_specs=ride.in_specs, out_specs=ride.out_specs, out_shape=ride.out_shape,
                          scratch_shapes=ride.scratch)(*arrays)


def _with_ride(body, ride, n_in, n_out, grid):
    if ride is None:
        return body
    n = ride.n

    def wrapped(*refs):
        ins, rins = refs[:n_in], refs[n_in:n_in + n]
        outs, routs = refs[n_in + n:n_in + n + n_out], refs[n_in + n + n_out:n_in + 2 * n + n_out]
        rest = refs[n_in + 2 * n + n_out:]
        scratch, sems = rest[:len(rest) - 3], rest[len(rest) - 3:]
        ids = [pl.program_id(d) for d in range(len(grid))]
        first = functools.reduce(lambda p, q: p & q, [i == 0 for i in ids])
        last = functools.reduce(lambda p, q: p & q, [i == g - 1 for i, g in zip(ids, grid)])

        @pl.when(first)
        def _():
            ride.start(rins, routs, sems)

        body(*ins, *outs, *scratch)

        @pl.when(last)
        def _():
            ride.wait(rins, routs, sems)

    return wrapped


def _ride_call(body, ride, *, name, grid, in_specs, out_specs, out_shape, scratch_shapes, sem, args):
    n_in, n_out = len(in_specs), len(out_specs)
    if ride is None:
        res = pl.pallas_call(body, name=name, grid=grid, in_specs=in_specs, out_specs=out_specs, out_shape=out_shape,
                             scratch_shapes=scratch_shapes, compiler_params=_params(sem))(*args)
        return list(res), []
    res = pl.pallas_call(
        _with_ride(body, ride, n_in, n_out, grid), name=name, grid=grid,
        in_specs=list(in_specs) + ride.in_specs, out_specs=list(out_specs) + ride.out_specs,
        out_shape=list(out_shape) + ride.out_shape, scratch_shapes=list(scratch_shapes) + ride.scratch,
        compiler_params=_params(("arbitrary",) * len(grid)))(*args, *ride.arrays)
    return list(res[:n_out]), list(res[n_out:])


def _mm(a, b, *, name, ta=False, tb=False, out_dtype=F32, tm=1024, tn=1024, tk=1024, ride=None):
    m, k = (a.shape[1], a.shape[0]) if ta else a.shape
    n = b.shape[0] if tb else b.shape[1]
    assert (b.shape[1] if tb else b.shape[0]) == k
    tm, tn, tk = _tile(m, tm), _tile(n, tn), _tile(k, tk)
    nk = k // tk
    a_spec = pl.BlockSpec((tk, tm), lambda i, j, l: (l, i)) if ta else pl.BlockSpec((tm, tk), lambda i, j, l: (i, l))
    b_spec = pl.BlockSpec((tn, tk), lambda i, j, l: (j, l)) if tb else pl.BlockSpec((tk, tn), lambda i, j, l: (l, j))
    dims = (((0,) if ta else (1,), (1,) if tb else (0,)), ((), ()))

    def body(a_ref, b_ref, o_ref, *acc):
        p = lax.dot_general(a_ref[...].astype(BF16), b_ref[...].astype(BF16), dims, preferred_element_type=F32)
        if nk == 1:
            o_ref[...] = p.astype(out_dtype)
            return
        acc_ref = acc[0]
        step = pl.program_id(2)

        @pl.when(step == 0)
        def _():
            acc_ref[...] = p

        @pl.when(step > 0)
        def _():
            acc_ref[...] += p

        @pl.when(step == nk - 1)
        def _():
            o_ref[...] = acc_ref[...].astype(out_dtype)

    outs, rode = _ride_call(
        body, ride, name=name, grid=(m // tm, n // tn, nk), in_specs=[a_spec, b_spec],
        out_specs=[pl.BlockSpec((tm, tn), lambda i, j, l: (i, j))], out_shape=[jax.ShapeDtypeStruct((m, n), out_dtype)],
        scratch_shapes=[] if nk == 1 else [pltpu.VMEM((tm, tn), F32)], sem=("parallel", "parallel", "arbitrary"),
        args=(a, b))
    return outs[0] if ride is None else (outs[0], rode)


def _rowwise(name, fn, rows, vecs, outs, sums=(), tr=256):
    s = rows[0][0].shape[0]
    tr = min(tr, s)
    nr, nv, no = len(rows), len(vecs), len(outs)

    def body(*refs):
        vals = [r[...] for r in refs[:nr + nv]]
        res = fn(*vals)
        for o_ref, val in zip(refs[nr + nv:nr + nv + no], res[:no]):
            o_ref[...] = val.astype(o_ref.dtype)
        step = pl.program_id(0)
        for s_ref, val in zip(refs[nr + nv + no:], res[no:]):
            @pl.when(step == 0)
            def _(s_ref=s_ref, val=val):
                s_ref[...] = val

            @pl.when(step > 0)
            def _(s_ref=s_ref, val=val):
                s_ref[...] += val

    in_specs = [pl.BlockSpec((tr, w), functools.partial(lambda i, cb: (i, cb), cb=cb)) for _, w, cb in rows]
    in_specs += [pl.BlockSpec(v.shape, lambda i: (0, 0)) for v in vecs]
    out_specs = [pl.BlockSpec((tr, w), lambda i: (i, 0)) for w, _ in outs]
    out_specs += [pl.BlockSpec((1, w), lambda i: (0, 0)) for w in sums]
    out_shape = [jax.ShapeDtypeStruct((s, w), dt) for w, dt in outs]
    out_shape += [jax.ShapeDtypeStruct((1, w), F32) for w in sums]
    return pl.pallas_call(
        body, name=name, grid=(s // tr,), in_specs=in_specs, out_specs=out_specs, out_shape=out_shape,
        compiler_params=_params(("arbitrary",) if sums else ("parallel",)),
    )(*[r[0] for r in rows], *vecs)


def _colsum(x):
    return jnp.sum(x, axis=0, keepdims=True)


def _sigmoid(x):
    return 1.0 / (1.0 + jnp.exp(-x))


def _log_sigmoid(x):
    return jnp.minimum(x, 0.0) - jnp.log(1.0 + jnp.exp(-jnp.abs(x)))


def _ln_stats(r):
    mu = jnp.mean(r, axis=-1, keepdims=True)
    xc = r - mu
    var = jnp.mean(xc * xc, axis=-1, keepdims=True)
    rstd = lax.rsqrt(var + LN_EPS)
    return xc * rstd, rstd


def _ln_bwd(dy, xhat, rstd, g):
    dxh = dy * g
    m1 = jnp.mean(dxh, axis=-1, keepdims=True)
    m2 = jnp.mean(dxh * xhat, axis=-1, keepdims=True)
    return rstd * (dxh - m1 - xhat * m2)


def _mod_exchange(c_row, w_ada, b_ada_loc):
    def body(c_ref, w_ref, b_ref, call_ref, mod_ref, piece_ref, send_sems, recv_sems):
        me = _me()[3]
        call_ref[me] = c_ref[...]
        sent = []
        for r in range(1, N_DEV):
            peer, _ = _peer(r)
            cp = pltpu.make_async_remote_copy(
                src_ref=c_ref, dst_ref=call_ref.at[me], send_sem=send_sems.at[0, r - 1],
                recv_sem=recv_sems.at[0, r - 1], device_id=peer, device_id_type=MESH)
            cp.start()
            sent.append(cp)
        for r in range(1, N_DEV):
            peer, pidx = _peer(r)
            pltpu.make_async_remote_copy(
                src_ref=c_ref, dst_ref=call_ref.at[pidx], send_sem=send_sems.at[0, r - 1],
                recv_sem=recv_sems.at[0, r - 1], device_id=peer, device_id_type=MESH).wait_recv()
        c_all = jnp.concatenate([call_ref[d] for d in range(N_DEV)], axis=0)
        mod_loc = jnp.dot(c_all, w_ref[...], preferred_element_type=F32,
                          precision=lax.Precision.HIGHEST) + b_ref[...]
        for d in range(N_DEV):
            piece_ref[d] = mod_loc[d:d + 1, :]
        mod_ref[me] = piece_ref[me]
        for r in range(1, N_DEV):
            peer, pidx = _peer(r)
            cp = pltpu.make_async_remote_copy(
                src_ref=piece_ref.at[pidx], dst_ref=mod_ref.at[me], send_sem=send_sems.at[1, r - 1],
                recv_sem=recv_sems.at[1, r - 1], device_id=peer, device_id_type=MESH)
            cp.start()
            sent.append(cp)
        for r in range(1, N_DEV):
            peer, pidx = _peer(r)
            pltpu.make_async_remote_copy(
                src_ref=piece_ref.at[me], dst_ref=mod_ref.at[pidx], send_sem=send_sems.at[1, r - 1],
                recv_sem=recv_sems.at[1, r - 1], device_id=peer, device_id_type=MESH).wait_recv()
        for cp in sent:
            cp.wait_send()

    vmem = pl.BlockSpec(memory_space=pltpu.VMEM)
    return pl.pallas_call(
        body, name="mod_exchange", in_specs=[vmem, vmem, vmem], out_specs=[vmem, vmem],
        out_shape=[jax.ShapeDtypeStruct((N_DEV, 1, D_MODEL), F32), jax.ShapeDtypeStruct((N_DEV, 1, ADA_SHARD), F32)],
        scratch_shapes=[pltpu.VMEM((N_DEV, 1, ADA_SHARD), F32),
                        pltpu.SemaphoreType.DMA((2, N_DEV - 1)), pltpu.SemaphoreType.DMA((2, N_DEV - 1))],
        compiler_params=_params(),
    )(c_row, w_ada, b_ada_loc)


def _split3(x):
    hi = x.astype(BF16)
    r1 = x - hi.astype(F32)
    mid = r1.astype(BF16)
    lo = (r1 - mid.astype(F32)).astype(BF16)
    return hi, mid, lo


def _scan_rows(x_ref, o_ref, s, reverse, pre=None, post=None):
    tb = min(TQ, s)
    nb = s // tb
    row = lax.broadcasted_iota(jnp.int32, (tb, tb), 0)
    col = lax.broadcasted_iota(jnp.int32, (tb, tb), 1)
    tri = jnp.where((col >= row) if reverse else (col <= row), 1.0, 0.0).astype(BF16)

    def step(i, carry):
        blk = (nb - 1 - i) if reverse else i
        off = pl.multiple_of(blk * tb, tb)
        x = x_ref[pl.ds(off, tb), :]
        if pre is not None:
            x = pre(x, off)
        acc = carry
        for piece in _split3(x):
            acc = acc + jnp.dot(tri, piece, preferred_element_type=F32)
        o_ref[pl.ds(off, tb), :] = acc if post is None else post(acc, off)
        edge = acc[0:1, :] if reverse else acc[tb - 1:tb, :]
        return jnp.broadcast_to(edge, (tb, LANES))

    lax.fori_loop(0, nb, step, jnp.zeros((tb, LANES), F32))


def _forget_cumsum(f_raw, b_pad):
    s = f_raw.shape[0]

    def body(f_ref, b_ref, cum_ref):
        b = b_ref[...]
        _scan_rows(f_ref, cum_ref, s, False, pre=lambda x, off: _log_sigmoid(x + b))

    vmem = pl.BlockSpec(memory_space=pltpu.VMEM)
    return pl.pallas_call(body, name="forget_cumsum", in_specs=[vmem, vmem], out_specs=vmem,
                          out_shape=jax.ShapeDtypeStruct((s, LANES), F32), compiler_params=_params())(f_raw, b_pad)


def _forget_bwd(dcum, f_raw, b_pad):
    s = f_raw.shape[0]

    def body(d_ref, f_ref, b_ref, df_ref, db_ref, tmp_ref):
        b = b_ref[...]
        _scan_rows(d_ref, tmp_ref, s, True)
        df = tmp_ref[...] * _sigmoid(-(f_ref[...] + b))
        df_ref[...] = df.astype(BF16)
        db_ref[...] = _colsum(df)

    vmem = pl.BlockSpec(memory_space=pltpu.VMEM)
    return pl.pallas_call(
        body, name="forget_bwd", in_specs=[vmem, vmem, vmem], out_specs=[vmem, vmem],
        out_shape=[jax.ShapeDtypeStruct((s, LANES), BF16), jax.ShapeDtypeStruct((1, LANES), F32)],
        scratch_shapes=[pltpu.VMEM((s, LANES), F32)], compiler_params=_params())(dcum, f_raw, b_pad)


def _dot_nt(a, b):
    return lax.dot_general(a, b, (((1,), (1,)), ((), ())), preferred_element_type=F32)


def _dot_tn(a, b):
    return lax.dot_general(a, b, (((0,), (0,)), ((), ())), preferred_element_type=F32)


def _head_masks():
    lane = lax.broadcasted_iota(jnp.int32, (TQ, LANES), 1)
    return lane, [lane < HEAD_DIM, lane >= HEAD_DIM]


def _pick(mask, x):
    return jnp.where(mask, x, jnp.zeros_like(x))


def _qkv_specs(s, col0):
    nb = ATTN_W // LANES
    return [pl.BlockSpec((TQ, LANES), lambda hp, qi: (qi, col0 + hp)),
            pl.BlockSpec((s, LANES), lambda hp, qi: (0, col0 + nb + hp)),
            pl.BlockSpec((s, LANES), lambda hp, qi: (0, col0 + 2 * nb + hp))]


def _pair_spec():
    return pl.BlockSpec((TQ, LANES), lambda hp, qi: (qi, hp))


def _fox_fwd(qkv, cum_col, cum_row, ride=None):
    s = qkv.shape[0]
    nq = s // TQ

    def body(q_ref, k_ref, v_ref, cc_ref, cr_ref, o_ref, o32_ref, lse_ref):
        hp, qi = pl.program_id(0), pl.program_id(1)
        lane, masks = _head_masks()
        row = lax.broadcasted_iota(jnp.int32, (TQ, TQ), 0)
        col = lax.broadcasted_iota(jnp.int32, (TQ, TQ), 1)
        causal = col <= row
        q2 = q_ref[...] * jnp.asarray(ATTN_SCALE, BF16)
        cc = cc_ref[...]
        qms = [_pick(masks[e], q2) for e in range(2)]
        cqs = [jnp.sum(jnp.where(lane == 2 * hp + e, cc, 0.0), axis=1, keepdims=True) for e in range(2)]

        def tile(kb, carry, masked):
            off = pl.multiple_of(kb * TQ, TQ)
            k2, v2 = k_ref[pl.ds(off, TQ), :], v_ref[pl.ds(off, TQ), :]
            new = []
            for e in range(2):
                m, l, acc = carry[e]
                sc = _dot_nt(qms[e], k2) + (cqs[e] - cr_ref[e:e + 1, pl.ds(off, TQ)])
                if masked:
                    sc = jnp.where(causal, sc, -jnp.inf)
                m_new = jnp.maximum(m, jnp.max(sc, axis=1, keepdims=True))
                p = jnp.exp(sc - m_new)
                corr = jnp.exp(m - m_new)
                l = corr * l + jnp.sum(p, axis=1, keepdims=True)
                acc = corr * acc + jnp.dot(p.astype(BF16), v2, preferred_element_type=F32)
                new.append((m_new, l, acc))
            return tuple(new)

        init = (jnp.full((TQ, 1), -jnp.inf, F32), jnp.zeros((TQ, 1), F32), jnp.zeros((TQ, LANES), F32))
        carry = lax.fori_loop(0, qi, lambda kb, cr: tile(kb, cr, False), tile(qi, (init, init), True))
        outs = [acc / l for _, l, acc in carry]
        lses = [m + jnp.log(l) for m, l, _ in carry]
        out = jnp.where(masks[0], outs[0], outs[1])
        o_ref[...] = out.astype(BF16)
        o32_ref[...] = out
        lse_ref[...] = jnp.where(masks[0], lses[0], lses[1])

    return _ride_call(
        body, ride, name="fox_fwd", grid=(N_HEADS // 2, nq),
        in_specs=_qkv_specs(s, 0) + [pl.BlockSpec((TQ, LANES), lambda hp, qi: (qi, 0)),
                                     pl.BlockSpec((None, 2, s), lambda hp, qi: (hp, 0, 0))],
        out_specs=[_pair_spec(), _pair_spec(), _pair_spec()],
        out_shape=[jax.ShapeDtypeStruct((s, ATTN_W), BF16), jax.ShapeDtypeStruct((s, ATTN_W), F32),
                   jax.ShapeDtypeStruct((s, ATTN_W), F32)],
        scratch_shapes=[], sem=("parallel", "parallel"), args=(qkv, qkv, qkv, cum_col, cum_row))


def _fox_bwd(qkv, cum_col, cum_row, o, lse, do, ride=None):
    s = qkv.shape[0]
    nq = s // TQ

    def body(q_ref, k_ref, v_ref, cc_ref, cr_ref, o_ref, lse_ref, do_ref,
             dq_ref, dk_ref, dv_ref, dcr_ref, dcq_ref, dk_acc, dv_acc):
        hp, qi = pl.program_id(0), pl.program_id(1)

        @pl.when(qi == 0)
        def _():
            dk_acc[...] = jnp.zeros_like(dk_acc)
            dv_acc[...] = jnp.zeros_like(dv_acc)
            dcr_ref[...] = jnp.zeros_like(dcr_ref)

        lane, masks = _head_masks()
        row = lax.broadcasted_iota(jnp.int32, (TQ, TQ), 0)
        col = lax.broadcasted_iota(jnp.int32, (TQ, TQ), 1)
        causal = col <= row
        q2 = q_ref[...] * jnp.asarray(ATTN_SCALE, BF16)
        do2 = do_ref[...]
        prod = do2.astype(F32) * o_ref[...].astype(F32)
        lse2 = lse_ref[...]
        cc = cc_ref[...]
        qms = [_pick(masks[e], q2) for e in range(2)]
        doms = [_pick(masks[e], do2) for e in range(2)]
        deltas = [jnp.sum(jnp.where(masks[e], prod, 0.0), axis=1, keepdims=True) for e in range(2)]
        lses = [jnp.max(jnp.where(masks[e], lse2, -jnp.inf), axis=1, keepdims=True) for e in range(2)]
        cqs = [jnp.sum(jnp.where(lane == 2 * hp + e, cc, 0.0), axis=1, keepdims=True) for e in range(2)]

        def tile(kb, carry, masked):
            off = pl.multiple_of(kb * TQ, TQ)
            k2, v2 = k_ref[pl.ds(off, TQ), :], v_ref[pl.ds(off, TQ), :]
            new, dk, dv = [], None, None
            for e in range(2):
                dq, rowsum = carry[e]
                sc = _dot_nt(qms[e], k2) + (cqs[e] - cr_ref[e:e + 1, pl.ds(off, TQ)])
                p = jnp.exp(sc - lses[e])
                if masked:
                    p = jnp.where(causal, p, 0.0)
                ds = p * (_dot_nt(doms[e], v2) - deltas[e])
                dsb = ds.astype(BF16)
                dk_e, dv_e = _dot_tn(dsb, qms[e]), _dot_tn(p.astype(BF16), doms[e])
                dk, dv = (dk_e, dv_e) if e == 0 else (dk + dk_e, dv + dv_e)
                dcr_ref[e:e + 1, pl.ds(off, TQ)] -= _colsum(ds)
                new.append((dq + jnp.dot(dsb, k2, preferred_element_type=F32),
                            rowsum + jnp.sum(ds, axis=1, keepdims=True)))
            dk_acc[pl.ds(off, TQ), :] += dk
            dv_acc[pl.ds(off, TQ), :] += dv
            return tuple(new)

        init = (jnp.zeros((TQ, LANES), F32), jnp.zeros((TQ, 1), F32))
        carry = tile(qi, lax.fori_loop(0, qi, lambda kb, cr: tile(kb, cr, False), (init, init)), True)
        dq_ref[...] = (jnp.where(masks[0], carry[0][0], carry[1][0]) * ATTN_SCALE).astype(BF16)
        dcq_ref[...] = jnp.where(masks[0], carry[0][1], carry[1][1])

        @pl.when(qi == nq - 1)
        def _():
            dk_ref[...] = dk_acc[...].astype(BF16)
            dv_ref[...] = dv_acc[...].astype(BF16)

    seq_spec = pl.BlockSpec((s, LANES), lambda hp, qi: (0, hp))
    return _ride_call(
        body, ride, name="fox_bwd", grid=(N_HEADS // 2, nq),
        in_specs=_qkv_specs(s, 0) + [pl.BlockSpec((TQ, LANES), lambda hp, qi: (qi, 0)),
                                     pl.BlockSpec((None, 2, s), lambda hp, qi: (hp, 0, 0)),
                                     _pair_spec(), _pair_spec(), _pair_spec()],
        out_specs=[_pair_spec(), seq_spec, seq_spec, pl.BlockSpec((None, 2, s), lambda hp, qi: (hp, 0, 0)),
                   _pair_spec()],
        out_shape=[jax.ShapeDtypeStruct((s, ATTN_W), BF16)] * 3 + [jax.ShapeDtypeStruct((N_HEADS // 2, 2, s), F32),
                                                                    jax.ShapeDtypeStruct((s, ATTN_W), F32)],
        scratch_shapes=[pltpu.VMEM((s, LANES), F32), pltpu.VMEM((s, LANES), F32)],
        sem=("parallel", "arbitrary"), args=(qkv, qkv, qkv, cum_col, cum_row, o, lse, do))


def _scan_matrix(reverse):
    row = lax.broadcasted_iota(jnp.int32, (SCAN_W, SCAN_W), 0)
    col = lax.broadcasted_iota(jnp.int32, (SCAN_W, SCAN_W), 1)
    return jnp.where((row > col) if reverse else (row < col), 1.0, 0.0).astype(BF16)


def _scan_cols(x, tri, reverse):
    nblk = x.shape[1] // SCAN_W
    parts, total = [None] * nblk, None
    for b in (reversed(range(nblk)) if reverse else range(nblk)):
        blk = x[:, b * SCAN_W:(b + 1) * SCAN_W]
        part = jnp.dot(blk.astype(BF16), tri, preferred_element_type=F32)
        parts[b] = part if total is None else part + total
        rowsum = jnp.sum(blk, axis=1, keepdims=True)
        total = rowsum if total is None else total + rowsum
    return (parts[0] if nblk == 1 else jnp.concatenate(parts, axis=1)), total


def _sb_logits(qm, k2):
    z = _dot_nt(qm, k2)
    soft = jnp.log(1.0 + jnp.exp(-jnp.abs(z)))
    lb = jnp.minimum(z, 0.0) - soft
    return lb, lb - z


def _sb_fwd(qkv):
    s = qkv.shape[0]
    nq = s // TQ
    assert nq <= LANES

    def body(q_ref, k_ref, v_ref, o_ref, r_ref):
        qi = pl.program_id(1)
        lane, masks = _head_masks()
        row = lax.broadcasted_iota(jnp.int32, (TQ, TQ), 0)
        col = lax.broadcasted_iota(jnp.int32, (TQ, TQ), 1)
        strict = col < row
        suffix = _scan_matrix(True)
        q2 = q_ref[...] * jnp.asarray(ATTN_SCALE, BF16)
        qms = [_pick(masks[e], q2) for e in range(2)]

        def tile(kb, carry, masked):
            off = pl.multiple_of(kb * TQ, TQ)
            k2, v2 = k_ref[pl.ds(off, TQ), :], v_ref[pl.ds(off, TQ), :]
            new = []
            for e in range(2):
                run, saved, acc = carry[e]
                lb, lo = _sb_logits(qms[e], k2)
                if masked:
                    lo = jnp.where(strict, lo, 0.0)
                rest, total = _scan_cols(lo, suffix, True)
                a = jnp.exp(lb + rest + run)
                if masked:
                    a = jnp.where(strict, a, 0.0)
                acc = acc + jnp.dot(a.astype(BF16), v2, preferred_element_type=F32)
                new.append((run + total, jnp.where(lane == kb, run, saved), acc))
            return tuple(new)

        init = (jnp.zeros((TQ, 1), F32), jnp.zeros((TQ, LANES), F32), jnp.zeros((TQ, LANES), F32))
        carry = lax.fori_loop(0, qi, lambda it, cr: tile(qi - 1 - it, cr, False), tile(qi, (init, init), True))
        for e in range(2):
            r_ref[:, e * LANES:(e + 1) * LANES] = carry[e][1]
        o_ref[...] = jnp.where(masks[0], carry[0][2], carry[1][2]).astype(BF16)

    return pl.pallas_call(
        body, name="sb_fwd", grid=(N_HEADS // 2, nq), in_specs=_qkv_specs(s, 3 * ATTN_W // LANES),
        out_specs=[_pair_spec(), pl.BlockSpec((TQ, 2 * LANES), lambda hp, qi: (qi, hp))],
        out_shape=[jax.ShapeDtypeStruct((s, ATTN_W), BF16), jax.ShapeDtypeStruct((s, N_HEADS * LANES), F32)],
        compiler_params=_params(("parallel", "parallel")),
    )(qkv, qkv, qkv)


def _sb_bwd(qkv, saved_run, do):
    s = qkv.shape[0]
    nq = s // TQ

    def body(q_ref, k_ref, v_ref, r_ref, do_ref, dq_ref, dk_ref, dv_ref, dk_acc, dv_acc):
        qi = pl.program_id(1)

        @pl.when(qi == 0)
        def _():
            dk_acc[...] = jnp.zeros_like(dk_acc)
            dv_acc[...] = jnp.zeros_like(dv_acc)

        lane, masks = _head_masks()
        row = lax.broadcasted_iota(jnp.int32, (TQ, TQ), 0)
        col = lax.broadcasted_iota(jnp.int32, (TQ, TQ), 1)
        strict = col < row
        suffix, prefix = _scan_matrix(True), _scan_matrix(False)
        q2 = q_ref[...] * jnp.asarray(ATTN_SCALE, BF16)
        do2 = do_ref[...]
        qms = [_pick(masks[e], q2) for e in range(2)]
        doms = [_pick(masks[e], do2) for e in range(2)]
        saveds = [r_ref[:, e * LANES:(e + 1) * LANES] for e in range(2)]

        def tile(kb, carry, masked):
            off = pl.multiple_of(kb * TQ, TQ)
            k2, v2 = k_ref[pl.ds(off, TQ), :], v_ref[pl.ds(off, TQ), :]
            new, dk, dv = [], None, None
            for e in range(2):
                gsum, dq = carry[e]
                lb, lo = _sb_logits(qms[e], k2)
                beta = jnp.exp(lb)
                if masked:
                    lo = jnp.where(strict, lo, 0.0)
                rest, _ = _scan_cols(lo, suffix, True)
                run = jnp.sum(jnp.where(lane == kb, saveds[e], 0.0), axis=1, keepdims=True)
                a = jnp.exp(lb + rest + run)
                if masked:
                    a = jnp.where(strict, a, 0.0)
                g = a * _dot_nt(doms[e], v2)
                before, gtotal = _scan_cols(g, prefix, False)
                dz = g * (1.0 - beta) - beta * (before + gsum)
                if masked:
                    dz = jnp.where(strict, dz, 0.0)
                dzb = dz.astype(BF16)
                dk_e, dv_e = _dot_tn(dzb, qms[e]), _dot_tn(a.astype(BF16), doms[e])
                dk, dv = (dk_e, dv_e) if e == 0 else (dk + dk_e, dv + dv_e)
                new.append((gsum + gtotal, dq + jnp.dot(dzb, k2, preferred_element_type=F32)))
            dk_acc[pl.ds(off, TQ), :] += dk
            dv_acc[pl.ds(off, TQ), :] += dv
            return tuple(new)

        init = (jnp.zeros((TQ, 1), F32), jnp.zeros((TQ, LANES), F32))
        carry = tile(qi, lax.fori_loop(0, qi, lambda kb, cr: tile(kb, cr, False), (init, init)), True)
        dq_ref[...] = (jnp.where(masks[0], carry[0][1], carry[1][1]) * ATTN_SCALE).astype(BF16)

        @pl.when(qi == nq - 1)
        def _():
            dk_ref[...] = dk_acc[...].astype(BF16)
            dv_ref[...] = dv_acc[...].astype(BF16)

    seq_spec = pl.BlockSpec((s, LANES), lambda hp, qi: (0, hp))
    return pl.pallas_call(
        body, name="sb_bwd", grid=(N_HEADS // 2, nq),
        in_specs=_qkv_specs(s, 3 * ATTN_W // LANES) + [pl.BlockSpec((TQ, 2 * LANES), lambda hp, qi: (qi, hp)),
                                                        _pair_spec()],
        out_specs=[_pair_spec(), seq_spec, seq_spec],
        out_shape=[jax.ShapeDtypeStruct((s, ATTN_W), BF16)] * 3,
        scratch_shapes=[pltpu.VMEM((s, LANES), F32), pltpu.VMEM((s, LANES), F32)],
        compiler_params=_params(("parallel", "arbitrary")),
    )(qkv, qkv, qkv, saved_run, do)


CONV_TR = 256


def _shift_down(x, halo, n):
    rolled = pltpu.roll(x, n, 0)
    rid = lax.broadcasted_iota(jnp.int32, x.shape, 0)
    for j in range(n):
        rolled = jnp.where(rid == j, halo[8 - n + j:8 - n + j + 1, :], rolled)
    return rolled


def _shift_up(x, halo, n):
    rows = x.shape[0]
    rolled = pltpu.roll(x, rows - n, 0)
    rid = lax.broadcasted_iota(jnp.int32, x.shape, 0)
    for j in range(n):
        rolled = jnp.where(rid == rows - n + j, halo[j:j + 1, :], rolled)
    return rolled


def _conv_fwd_block(x, halo, w, b):
    return b + _shift_down(x, halo, 2) * w[0:1, :] + _shift_down(x, halo, 1) * w[1:2, :] + x * w[2:3, :]


def _conv_specs(tr, s):
    pair = 2 * FF_HALF
    blk = pl.BlockSpec((tr, pair), lambda j, i: (i, j))
    prev = pl.BlockSpec((8, pair), lambda j, i: (jnp.maximum(i * (tr // 8) - 1, 0), j))
    nxt = pl.BlockSpec((8, pair), lambda j, i: (jnp.minimum((i + 1) * (tr // 8), s // 8 - 1), j))
    return blk, prev, nxt


def _conv_gate_fwd(hpre, conv_w, conv_b):
    s = hpre.shape[0]
    tr = min(CONV_TR, s)
    blk, prev, _ = _conv_specs(tr, s)

    def body(x_ref, halo_ref, w_ref, b_ref, a_ref):
        i = pl.program_id(1)
        halo = jnp.where(i > 0, halo_ref[...], 0.0)
        h = _conv_fwd_block(x_ref[...], halo, w_ref[...], b_ref[...])
        hg, hv = h[:, :FF_HALF], h[:, FF_HALF:]
        a_ref[...] = (hg * _sigmoid(hg) * hv).astype(BF16)

    return pl.pallas_call(
        body, name="conv_gate_fwd", grid=(2, s // tr),
        in_specs=[blk, prev, pl.BlockSpec((3, 2 * FF_HALF), lambda j, i: (0, j)),
                  pl.BlockSpec((1, 2 * FF_HALF), lambda j, i: (0, j))],
        out_specs=pl.BlockSpec((tr, FF_HALF), lambda j, i: (i, j)),
        out_shape=jax.ShapeDtypeStruct((s, D_FF), BF16),
        compiler_params=_params(("parallel", "parallel")),
    )(hpre, hpre, conv_w, conv_b)


def _conv_gate_bwd(hpre, da, conv_w, conv_b):
    s = hpre.shape[0]
    tr = min(CONV_TR, s)
    blk, prev, _ = _conv_specs(tr, s)

    def body(x_ref, halo_ref, da_ref, w_ref, b_ref, dh_ref, db_ref, dw_ref):
        i = pl.program_id(1)
        halo = jnp.where(i > 0, halo_ref[...], 0.0)
        x = x_ref[...]
        h = _conv_fwd_block(x, halo, w_ref[...], b_ref[...])
        hg, hv = h[:, :FF_HALF], h[:, FF_HALF:]
        da_blk = da_ref[...].astype(F32)
        sg = _sigmoid(hg)
        dhg = da_blk * hv * (sg * (1.0 + hg * (1.0 - sg)))
        dhv = da_blk * (hg * sg)
        dh_ref[:, :FF_HALF] = dhg.astype(BF16)
        dh_ref[:, FF_HALF:] = dhv.astype(BF16)
        x2, x1 = _shift_down(x, halo, 2), _shift_down(x, halo, 1)
        parts = []
        for lo, dpart in ((0, dhg), (FF_HALF, dhv)):
            cols = slice(lo, lo + FF_HALF)
            parts.append((cols, _colsum(dpart), _colsum(dpart * x2[:, cols]), _colsum(dpart * x1[:, cols]),
                          _colsum(dpart * x[:, cols])))

        @pl.when(i == 0)
        def _():
            for cols, db, dw0, dw1, dw2 in parts:
                db_ref[:, cols] = db
                dw_ref[0:1, cols] = dw0
                dw_ref[1:2, cols] = dw1
                dw_ref[2:3, cols] = dw2

        @pl.when(i > 0)
        def _():
            for cols, db, dw0, dw1, dw2 in parts:
                db_ref[:, cols] += db
                dw_ref[0:1, cols] += dw0
                dw_ref[1:2, cols] += dw1
                dw_ref[2:3, cols] += dw2

    pair = 2 * FF_HALF
    return pl.pallas_call(
        body, name="conv_gate_bwd", grid=(2, s // tr),
        in_specs=[blk, prev, pl.BlockSpec((tr, FF_HALF), lambda j, i: (i, j)),
                  pl.BlockSpec((3, pair), lambda j, i: (0, j)), pl.BlockSpec((1, pair), lambda j, i: (0, j))],
        out_specs=[blk, pl.BlockSpec((1, pair), lambda j, i: (0, j)), pl.BlockSpec((3, pair), lambda j, i: (0, j))],
        out_shape=[jax.ShapeDtypeStruct((s, 2 * D_FF), BF16), jax.ShapeDtypeStruct((1, 2 * D_FF), F32),
                   jax.ShapeDtypeStruct((3, 2 * D_FF), F32)],
        compiler_params=_params(("parallel", "arbitrary")),
    )(hpre, hpre, da, conv_w, conv_b)


def _conv_input_bwd(dh, conv_w):
    s = dh.shape[0]
    tr = min(CONV_TR, s)
    blk, _, _ = _conv_specs(tr, s)
    nblk = s // tr

    def body(x_ref, halo_ref, w_ref, o_ref):
        i = pl.program_id(1)
        halo = jnp.where(i < nblk - 1, halo_ref[...].astype(F32), 0.0)
        x, w = x_ref[...].astype(F32), w_ref[...]
        o_ref[...] = (x * w[2:3, :] + _shift_up(x, halo, 1) * w[1:2, :] + _shift_up(x, halo, 2) * w[0:1, :]).astype(BF16)

    nxt = pl.BlockSpec((16, 2 * FF_HALF), lambda j, i: (jnp.minimum((i + 1) * (tr // 16), s // 16 - 1), j))
    return pl.pallas_call(
        body, name="conv_input_bwd", grid=(2, nblk),
        in_specs=[blk, nxt, pl.BlockSpec((3, 2 * FF_HALF), lambda j, i: (0, j))], out_specs=blk,
        out_shape=jax.ShapeDtypeStruct((s, 2 * D_FF), BF16),
        compiler_params=_params(("parallel", "parallel")),
    )(dh, dh, conv_w)


def _adamw_math(w, g, m, v):
    m = ADAM_B1 * m + (1.0 - ADAM_B1) * g
    v = ADAM_B2 * v + (1.0 - ADAM_B2) * (g * g)
    m_hat = m / (1.0 - ADAM_B1 ** ADAM_STEP)
    v_hat = v / (1.0 - ADAM_B2 ** ADAM_STEP)
    delta = -ADAM_LR * (m_hat / (jnp.sqrt(v_hat) + ADAM_EPS) + ADAM_WD * w)
    return delta, m, v


def _adamw(name, g8, w, m, v):
    r, c = w.shape
    tr = _row_tile(r, c)

    def body(g_ref, w_ref, m_ref, v_ref, go_ref, d_ref, mo_ref, vo_ref):
        g = g_ref[0].astype(F32)
        for d in range(1, N_DEV):
            g = g + g_ref[d].astype(F32)
        delta, mn, vn = _adamw_math(w_ref[...], g, m_ref[...], v_ref[...])
        go_ref[...] = g
        d_ref[...] = delta
        mo_ref[...] = mn
        vo_ref[...] = vn

    spec = pl.BlockSpec((tr, c), lambda i: (i, 0))
    return pl.pallas_call(
        body, name=name, grid=(r // tr,),
        in_specs=[pl.BlockSpec((N_DEV, tr, c), lambda i: (0, i, 0)), spec, spec, spec], out_specs=[spec] * 4,
        out_shape=[jax.ShapeDtypeStruct((r, c), F32)] * 4, compiler_params=_params(("parallel",)),
    )(g8, w, m, v)


def _adamw_ada(c_t, dmod, w, m, v):
    r, c = w.shape
    tr = _row_tile(r, c)

    def body(ct_ref, dm_ref, w_ref, m_ref, v_ref, go_ref, d_ref, mo_ref, vo_ref):
        ct, dm = ct_ref[...], dm_ref[...]
        g = ct[:, 0:1] * dm[0:1, :]
        for b in range(1, N_DEV):
            g = g + ct[:, b:b + 1] * dm[b:b + 1, :]
        delta, mn, vn = _adamw_math(w_ref[...], g, m_ref[...], v_ref[...])
        go_ref[...] = g
        d_ref[...] = delta
        mo_ref[...] = mn
        vo_ref[...] = vn

    spec = pl.BlockSpec((tr, c), lambda i: (i, 0))
    return pl.pallas_call(
        body, name="adamw_w_ada", grid=(r // tr,),
        in_specs=[pl.BlockSpec((tr, N_DEV), lambda i: (i, 0)), pl.BlockSpec((N_DEV, c), lambda i: (0, 0)),
                  spec, spec, spec],
        out_specs=[spec] * 4, out_shape=[jax.ShapeDtypeStruct((r, c), F32)] * 4,
        compiler_params=_params(("parallel",)),
    )(c_t, dmod, w, m, v)


def _cols_from_slots(g):
    n, r, c = g.shape
    return jnp.transpose(g, (1, 0, 2)).reshape(r, n * c)


def _cols_to_slots(w):
    r, c = w.shape
    return jnp.transpose(w.reshape(r, N_DEV, c // N_DEV), (1, 0, 2))


def _pair_cols(w):
    g0, g1 = w[..., 0:FF_HALF], w[..., FF_HALF:D_FF]
    v0, v1 = w[..., D_FF:D_FF + FF_HALF], w[..., D_FF + FF_HALF:]
    return jnp.concatenate([g0, v0, g1, v1], axis=-1)


def _unpair_cols(w):
    g0, v0 = w[..., 0:FF_HALF], w[..., FF_HALF:D_FF]
    g1, v1 = w[..., D_FF:D_FF + FF_HALF], w[..., D_FF + FF_HALF:]
    return jnp.concatenate([g0, g1, v0, v1], axis=-1)


def _row(v):
    return v.reshape(1, -1)


def kernel(x, c, w_ada, b_ada, w_in, b_forget, w_fox_proj, w_sb_proj, w_o, ln1_g, ln1_b, w_up, conv_w, conv_b, w_down, ln2_g, ln2_b, loss_target, m_w_ada, m_b_ada, m_w_in, m_b_forget, m_w_fox_proj, m_w_sb_proj, m_w_o, m_ln1_g, m_ln1_b, m_w_up, m_conv_w, m_conv_b, m_w_down, m_ln2_g, m_ln2_b, v_w_ada, v_b_ada, v_w_in, v_b_forget, v_w_fox_proj, v_w_sb_proj, v_w_o, v_ln1_g, v_ln1_b, v_w_up, v_conv_w, v_conv_b, v_w_down, v_ln2_g, v_ln2_b):
    s = x.shape[1]
    me = 4 * lax.axis_index("x") + 2 * lax.axis_index("y") + lax.axis_index("c")
    x2 = x.reshape(s, D_MODEL)
    tgt = loss_target.reshape(s, D_MODEL)

    b_ada_loc = lax.dynamic_slice(b_ada, (me * ADA_SHARD,), (ADA_SHARD,)).reshape(1, ADA_SHARD)
    c_all, mod = _mod_exchange(c, w_ada, b_ada_loc)
    mod = mod.reshape(N_MOD, 1, D_MODEL)
    sh1, sc1, gt1, sh2, sc2, gt2 = [mod[i] for i in range(N_MOD)]

    (g_in,) = _exchange("ag_w_in", [w_in.astype(BF16)], scatter=False)
    late_weights = _Ride([w_fox_proj.astype(BF16), w_sb_proj.astype(BF16), w_o.astype(BF16), w_up.astype(BF16),
                          w_down.astype(BF16), conv_w], scatter=False)
    w_in_f = _cols_from_slots(g_in)
    w_proj = jnp.concatenate(
        [w_in_f[:, 0:1536], w_in_f[:, 1544:3080], w_in_f[:, 3080:5128], w_in_f[:, 1536:1544],
         jnp.zeros((D_MODEL, W_PROJ - 5128), BF16)], axis=1)
    w_qkv, w_gates, w_f = w_proj[:, :W_QKV], w_proj[:, W_QKV:W_QKV + W_GATES], w_proj[:, W_QKV + W_GATES:W_QKV + W_GATES + W_F]
    conv_b_p = _pair_cols(_row(conv_b))
    b_f_pad = jnp.pad(_row(b_forget), ((0, 0), (0, LANES - N_HEADS)))

    (u1,) = _rowwise("modulate1", lambda xb, sc, sh: (xb * (1.0 + sc) + sh,),
                     [(x2, D_MODEL, 0)], [sc1, sh1], [(D_MODEL, BF16)], tr=512)
    qkv = _mm(u1, w_qkv, name="mm_qkv", out_dtype=BF16)
    gates = _mm(u1, w_gates, name="mm_gates")
    f_raw = _mm(u1, w_f, name="mm_forget")
    cum_col = _forget_cumsum(f_raw, b_f_pad)
    cum_row = jnp.transpose(cum_col[:, :N_HEADS]).reshape(N_HEADS // 2, 2, s)
    (y_fox, y_fox32, lse), (g_fox, g_sb, g_o, g_up, g_down, g_cw) = _fox_fwd(qkv, cum_col, cum_row, ride=late_weights)
    w_fox_f = _cols_from_slots(g_fox)
    w_sb_f = _cols_from_slots(g_sb)
    w_o_f = g_o.reshape(D_MODEL, D_MODEL)
    w_up_p = _pair_cols(_cols_from_slots(g_up))
    w_down_f = g_down.reshape(D_FF, D_MODEL)
    conv_w_p = _pair_cols(_cols_from_slots(g_cw))
    y_sb, sb_run = _sb_fwd(qkv)
    pf = _mm(y_fox, w_fox_f, name="mm_fox_proj")
    ps = _mm(y_sb, w_sb_f, name="mm_sb_proj")
    (merged,) = _rowwise("gate_merge", lambda ga, gb, a, b: (_sigmoid(ga) * a + _sigmoid(gb) * b,),
                         [(gates, D_MODEL, 0), (gates, D_MODEL, 1), (pf, D_MODEL, 0), (ps, D_MODEL, 0)], [],
                         [(D_MODEL, BF16)])
    attn_out = _mm(merged, w_o_f, name="mm_w_o")

    def ln_fwd(xb, fb, gt, g, b):
        xhat, _ = _ln_stats(ALPHA * xb + (1.0 + gt) * fb)
        return xhat * g + b

    def ln1_mod(xb, fb, gt, g, b, sc, sh):
        y = ln_fwd(xb, fb, gt, g, b)
        return y, y * (1.0 + sc) + sh

    x1, u2 = _rowwise("ln1_modulate2", ln1_mod, [(x2, D_MODEL, 0), (attn_out, D_MODEL, 0)],
                      [gt1, _row(ln1_g), _row(ln1_b), sc2, sh2], [(D_MODEL, F32), (D_MODEL, BF16)])

    hpre = _mm(u2, w_up_p, name="mm_w_up", tn=1408)
    act = _conv_gate_fwd(hpre, conv_w_p, conv_b_p)
    ffn_out = _mm(act, w_down_f, name="mm_w_down", tk=1408)

    def ln2_bwd(xb, fb, tb, gt, g, b):
        xhat, rstd = _ln_stats(ALPHA * xb + (1.0 + gt) * fb)
        err = (xhat * g + b) - tb
        dy = err * (1.0 / D_MODEL)
        dr = _ln_bwd(dy, xhat, rstd, g)
        return (dr * (1.0 + gt), ALPHA * dr,
                _colsum(err * err), _colsum(dy * xhat), _colsum(dy), _colsum(dr * fb))

    dffn, dx1_res, sq_err, d_ln2_g, d_ln2_b, d_gt2 = _rowwise(
        "ln2_bwd", ln2_bwd, [(x1, D_MODEL, 0), (ffn_out, D_MODEL, 0), (tgt, D_MODEL, 0)],
        [gt2, _row(ln2_g), _row(ln2_b)], [(D_MODEL, BF16), (D_MODEL, F32)], sums=[D_MODEL] * 4)
    loss = lax.psum(0.5 * jnp.sum(sq_err) / D_MODEL, ("x", "y", "c"))

    d_w_down = _mm(act, dffn, name="mm_d_w_down", ta=True, tm=1408, out_dtype=BF16)
    d_act = _mm(dffn, w_down_f, name="mm_d_act", tb=True, tn=1408, out_dtype=BF16)
    dh, d_conv_b_p, d_conv_w_p = _conv_gate_bwd(hpre, d_act, conv_w_p, conv_b_p)
    dhpre = _conv_input_bwd(dh, conv_w_p)
    d_w_up_p = _mm(u2, dhpre, name="mm_d_w_up", ta=True, tn=1408, out_dtype=BF16)
    du2 = _mm(dhpre, w_up_p, name="mm_d_u2", tb=True, tk=1408)

    def ln1_bwd(du, dres, x1b, xb, fb, sc, gt, g):
        dx1 = dres + du * (1.0 + sc)
        xhat, rstd = _ln_stats(ALPHA * xb + (1.0 + gt) * fb)
        dr = _ln_bwd(dx1, xhat, rstd, g)
        return (dr * (1.0 + gt), ALPHA * dr,
                _colsum(du * x1b), _colsum(du), _colsum(dx1 * xhat), _colsum(dx1), _colsum(dr * fb))

    d_attn, dx_res, d_sc2, d_sh2, d_ln1_g, d_ln1_b, d_gt1 = _rowwise(
        "ln1_bwd", ln1_bwd,
        [(du2, D_MODEL, 0), (dx1_res, D_MODEL, 0), (x1, D_MODEL, 0), (x2, D_MODEL, 0), (attn_out, D_MODEL, 0)],
        [sc2, gt1, _row(ln1_g)], [(D_MODEL, BF16), (D_MODEL, F32)], sums=[D_MODEL] * 5)

    d_w_o = _mm(merged, d_attn, name="mm_d_w_o", ta=True, out_dtype=BF16)
    d_merged = _mm(d_attn, w_o_f, name="mm_d_merged", tb=True)

    def merge_bwd(dm, ga, gb, a, b):
        sa, sb = _sigmoid(ga), _sigmoid(gb)
        return dm * a * sa * (1.0 - sa), dm * b * sb * (1.0 - sb), dm * sa, dm * sb

    d_ga, d_gb, d_pf, d_ps = _rowwise(
        "gate_merge_bwd", merge_bwd,
        [(d_merged, D_MODEL, 0), (gates, D_MODEL, 0), (gates, D_MODEL, 1), (pf, D_MODEL, 0), (ps, D_MODEL, 0)], [],
        [(D_MODEL, BF16)] * 4)
    d_w_fox = _mm(y_fox, d_pf, name="mm_d_w_fox", ta=True, out_dtype=BF16)
    d_w_sb = _mm(y_sb, d_ps, name="mm_d_w_sb", ta=True, out_dtype=BF16)
    d_y_fox = _mm(d_pf, w_fox_f, name="mm_d_y_fox", tb=True, out_dtype=BF16)
    d_y_sb = _mm(d_ps, w_sb_f, name="mm_d_y_sb", tb=True, out_dtype=BF16)
    early_grads = _Ride(
        [_cols_to_slots(d_w_fox), _cols_to_slots(d_w_sb), d_w_o.reshape(N_DEV, D_MODEL // N_DEV, D_MODEL),
         _cols_to_slots(_unpair_cols(d_w_up_p)), d_w_down.reshape(N_DEV, D_FF // N_DEV, D_MODEL)], scatter=True)
    (dq_a, dk_a, dv_a, d_cum_row, d_cum_q), early_slots = _fox_bwd(qkv, cum_col, cum_row, y_fox32, lse, d_y_fox,
                                                                    ride=early_grads)
    dq_b, dk_b, dv_b = _sb_bwd(qkv, sb_run, d_y_sb)
    d_cum = jnp.transpose(d_cum_row.reshape(N_HEADS, s)) + d_cum_q[:, ::HEAD_DIM]
    d_cum = jnp.pad(d_cum, ((0, 0), (0, LANES - N_HEADS)))
    d_f, d_b_forget = _forget_bwd(d_cum, f_raw, b_f_pad)
    d_proj = jnp.concatenate([dq_a, dk_a, dv_a, dq_b, dk_b, dv_b, d_ga, d_gb, d_f,
                              jnp.zeros((s, W_PROJ - W_QKV - W_GATES - W_F), BF16)], axis=1)
    d_w_proj = _mm(u1, d_proj, name="mm_d_w_in", ta=True, tn=896, out_dtype=BF16)
    d_w_in_f = jnp.concatenate([d_w_proj[:, 0:1536], d_w_proj[:, 5120:5128], d_w_proj[:, 1536:3072],
                                d_w_proj[:, 3072:5120]], axis=1)
    du1, (in_slots,) = _mm(d_proj, w_proj, name="mm_d_u1", tb=True, tk=896,
                           ride=_Ride([_cols_to_slots(d_w_in_f)], scatter=True))

    def x_bwd(du, dres, xb, sc):
        return dres + du * (1.0 + sc), _colsum(du * xb), _colsum(du)

    grad_x, d_sc1, d_sh1 = _rowwise("x_bwd", x_bwd, [(du1, D_MODEL, 0), (dx_res, D_MODEL, 0), (x2, D_MODEL, 0)],
                                    [sc1], [(D_MODEL, F32)], sums=[D_MODEL] * 2, tr=512)

    d_conv_b = _unpair_cols(d_conv_b_p)
    d_conv_w = _unpair_cols(d_conv_w_p)
    n_rep = N_MOD * D_MODEL + LANES + 4 * D_MODEL + 2 * D_FF
    small = jnp.concatenate(
        [d_sh1, d_sc1, d_gt1, d_sh2, d_sc2, d_gt2, d_b_forget, d_ln1_g, d_ln1_b, d_ln2_g, d_ln2_b, d_conv_b,
         d_conv_w.reshape(1, 6 * D_FF)], axis=1)
    n_small = small.shape[1] // LANES
    small = jnp.pad(small.reshape(n_small, LANES), ((0, 264 - n_small), (0, 0)))
    (small_all,) = _exchange("ag_small_grads", [small], scatter=False)
    rep8 = small_all[:, :n_rep // LANES, :]
    cw8 = small_all[:, n_rep // LANES:n_small, :].reshape(N_DEV, 3, 2 * D_FF)
    cw8 = lax.dynamic_slice(cw8, (0, 0, me * UP_SHARD), (N_DEV, 3, UP_SHARD))
    dmod8 = small_all[:, :N_MOD * D_MODEL // LANES, :].reshape(N_DEV, N_MOD * D_MODEL)
    dmod_loc = lax.dynamic_slice(dmod8, (0, me * ADA_SHARD), (N_DEV, ADA_SHARD))

    def pack_rep(b_a, b_f, g1, b1, g2, b2, cb):
        flat = jnp.concatenate([b_a, jnp.pad(b_f, (0, LANES - N_HEADS)), g1, b1, g2, b2, cb])
        return flat.reshape(n_rep // LANES, LANES)

    rep = _adamw("adamw_small", rep8, pack_rep(b_ada, b_forget, ln1_g, ln1_b, ln2_g, ln2_b, conv_b),
                 pack_rep(m_b_ada, m_b_forget, m_ln1_g, m_ln1_b, m_ln2_g, m_ln2_b, m_conv_b),
                 pack_rep(v_b_ada, v_b_forget, v_ln1_g, v_ln1_b, v_ln2_g, v_ln2_b, v_conv_b))

    def unpack_rep(p):
        flat = p.reshape(-1)
        o = N_MOD * D_MODEL
        return {"b_ada": flat[:o], "b_forget": flat[o:o + N_HEADS],
                "ln1_g": flat[o + 128:o + 1152], "ln1_b": flat[o + 1152:o + 2176],
                "ln2_g": flat[o + 2176:o + 3200], "ln2_b": flat[o + 3200:o + 4224], "conv_b": flat[o + 4224:]}

    rep = [unpack_rep(p) for p in rep]
    r_conv_w = _adamw("adamw_conv_w", cw8, conv_w, m_conv_w, v_conv_w)
    r_ada = _adamw_ada(jnp.transpose(c_all.reshape(N_DEV, D_MODEL)), dmod_loc, w_ada, m_w_ada, v_w_ada)

    r_in = _adamw("adamw_w_in", in_slots, w_in, m_w_in, v_w_in)
    r_fox = _adamw("adamw_w_fox", early_slots[0], w_fox_proj, m_w_fox_proj, v_w_fox_proj)
    r_sb = _adamw("adamw_w_sb", early_slots[1], w_sb_proj, m_w_sb_proj, v_w_sb_proj)
    r_o = _adamw("adamw_w_o", early_slots[2], w_o, m_w_o, v_w_o)
    r_up = _adamw("adamw_w_up", early_slots[3], w_up, m_w_up, v_w_up)
    r_down = _adamw("adamw_w_down", early_slots[4], w_down, m_w_down, v_w_down)

    def leaf(i):
        return [r_ada[i], rep[i]["b_ada"], r_in[i], rep[i]["b_forget"], r_fox[i], r_sb[i], r_o[i], rep[i]["ln1_g"],
                rep[i]["ln1_b"], r_up[i], r_conv_w[i], rep[i]["conv_b"], r_down[i], rep[i]["ln2_g"], rep[i]["ln2_b"]]

    return (loss, grad_x.reshape(1, s, D_MODEL), *leaf(0), *leaf(1), *leaf(2), *leaf(3))
```

```python
import functools

import jax
import jax.numpy as jnp
from jax import lax
from jax.experimental import pallas as pl
from jax.experimental.pallas import tpu as pltpu

F32 = jnp.float32
BF16 = jnp.bfloat16
MESH = pl.DeviceIdType.MESH
ANY = pl.BlockSpec(memory_space=pl.ANY)

N_DEV = 8
D_MODEL = 1024
HEAD_DIM = 64
N_HEADS = 8
ATTN_W = N_HEADS * HEAD_DIM
D_FF = 2816
FF_HALF = D_FF // 2
N_MOD = 6
ADA_SHARD = N_MOD * D_MODEL // N_DEV
IN_SHARD = 641
UP_SHARD = 704
ATTN_SCALE = HEAD_DIM ** -0.5
ALPHA = 2.0 ** 0.25
LN_EPS = 1e-5
LANES = 128
TQ = 512
SCAN_W = 256
VMEM_LIMIT = 56 * 1024 * 1024

ADAM_LR, ADAM_B1, ADAM_B2, ADAM_EPS, ADAM_WD, ADAM_STEP = 0.001, 0.9, 0.999, 1e-08, 0.01, 10

W_QKV, W_GATES, W_F = 3072, 2048, 128
W_PROJ = 5376


def _params(sem=None):
    return pltpu.CompilerParams(dimension_semantics=sem, vmem_limit_bytes=VMEM_LIMIT)


def _tile(n, cap):
    if n <= cap:
        return n
    best = None
    for t in range(LANES, cap + 1, LANES):
        if n % t == 0:
            best = t
    assert best is not None, (n, cap)
    return best


def _row_tile(r, width, budget=192 * 1024):
    if r * width <= budget or r % 16:
        return r
    best = 16
    for t in range(16, r + 1, 16):
        if r % t == 0 and t * width <= budget:
            best = t
    return best


def _me():
    x, y, c = lax.axis_index("x"), lax.axis_index("y"), lax.axis_index("c")
    return x, y, c, 4 * x + 2 * y + c


def _peer(r):
    x, y, c, _ = _me()
    px = 1 - x if r & 4 else x
    py = 1 - y if r & 2 else y
    pc = 1 - c if r & 1 else c
    return (px, py, pc), 4 * px + 2 * py + pc


class _Ride:
    def __init__(self, arrays, scatter):
        self.arrays, self.scatter, self.n = list(arrays), scatter, len(arrays)
        self.in_specs = [ANY] * self.n
        self.out_specs = [ANY] * self.n
        self.out_shape = [jax.ShapeDtypeStruct(a.shape if scatter else (N_DEV,) + a.shape, a.dtype) for a in arrays]
        self.scratch = [pltpu.SemaphoreType.DMA((self.n, N_DEV - 1)), pltpu.SemaphoreType.DMA((self.n, N_DEV - 1)),
                        pltpu.SemaphoreType.DMA((self.n,))]

    def _local(self, ins, outs, sems, a):
        me = _me()[3]
        return pltpu.make_async_copy(ins[a].at[me] if self.scatter else ins[a], outs[a].at[me], sems[2].at[a])

    def _remote(self, ins, outs, sems, a, r, arriving):
        me = _me()[3]
        peer, pidx = _peer(r)
        src = ins[a].at[me if arriving else pidx] if self.scatter else ins[a]
        return pltpu.make_async_remote_copy(
            src_ref=src, dst_ref=outs[a].at[pidx if arriving else me], send_sem=sems[0].at[a, r - 1],
            recv_sem=sems[1].at[a, r - 1], device_id=peer, device_id_type=MESH)

    def start(self, ins, outs, sems):
        for a in range(self.n):
            self._local(ins, outs, sems, a).start()
        for r in range(1, N_DEV):
            for a in range(self.n):
                self._remote(ins, outs, sems, a, r, False).start()

    def wait(self, ins, outs, sems):
        for r in range(1, N_DEV):
            for a in range(self.n):
                self._remote(ins, outs, sems, a, r, True).wait_recv()
        for r in range(1, N_DEV):
            for a in range(self.n):
                self._remote(ins, outs, sems, a, r, False).wait_send()
        for a in range(self.n):
            self._local(ins, outs, sems, a).wait()


def _exchange(name, arrays, scatter):
    ride = _Ride(arrays, scatter)

    def body(*refs):
        ins, outs, sems = refs[:ride.n], refs[ride.n:2 * ride.n], refs[2 * ride.n:]
        ride.start(ins, outs, sems)
        ride.wait(ins, outs, sems)

    return pl.pallas_call(body, name=name, in_specs=ride.in_specs, out_specs=ride.out_specs, out_shape=ride.out_shape,
                          scratch_shapes=ride.scratch)(*arrays)


def _with_ride(body, ride, n_in, n_out, grid):
    if ride is None:
        return body
    n = ride.n

    def wrapped(*refs):
        ins, rins = refs[:n_in], refs[n_in:n_in + n]
        outs, routs = refs[n_in + n:n_in + n + n_out], refs[n_in + n + n_out:n_in + 2 * n + n_out]
        rest = refs[n_in + 2 * n + n_out:]
        scratch, sems = rest[:len(rest) - 3], rest[len(rest) - 3:]
        ids = [pl.program_id(d) for d in range(len(grid))]
        first = functools.reduce(lambda p, q: p & q, [i == 0 for i in ids])
        last = functools.reduce(lambda p, q: p & q, [i == g - 1 for i, g in zip(ids, grid)])

        @pl.when(first)
        def _():
            ride.start(rins, routs, sems)

        body(*ins, *outs, *scratch)

        @pl.when(last)
        def _():
            ride.wait(rins, routs, sems)

    return wrapped


def _ride_call(body, ride, *, name, grid, in_specs, out_specs, out_shape, scratch_shapes, sem, args):
    n_in, n_out = len(in_specs), len(out_specs)
    if ride is None:
        res = pl.pallas_call(body, name=name, grid=grid, in_specs=in_specs, out_specs=out_specs, out_shape=out_shape,
                             scratch_shapes=scratch_shapes, compiler_params=_params(sem))(*args)
        return list(res), []
    res = pl.pallas_call(
        _with_ride(body, ride, n_in, n_out, grid), name=name, grid=grid,
        in_specs=list(in_specs) + ride.in_specs, out_specs=list(out_specs) + ride.out_specs,
        out_shape=list(out_shape) + ride.out_shape, scratch_shapes=list(scratch_shapes) + ride.scratch,
        compiler_params=_params(("arbitrary",) * len(grid)))(*args, *ride.arrays)
    return list(res[:n_out]), list(res[n_out:])


def _mm(a, b, *, name, ta=False, tb=False, out_dtype=F32, tm=1024, tn=1024, tk=1024, ride=None):
    m, k = (a.shape[1], a.shape[0]) if ta else a.shape
    n = b.shape[0] if tb else b.shape[1]
    assert (b.shape[1] if tb else b.shape[0]) == k
    tm, tn, tk = _tile(m, tm), _tile(n, tn), _tile(k, tk)
    nk = k // tk
    a_spec = pl.BlockSpec((tk, tm), lambda i, j, l: (l, i)) if ta else pl.BlockSpec((tm, tk), lambda i, j, l: (i, l))
    b_spec = pl.BlockSpec((tn, tk), lambda i, j, l: (j, l)) if tb else pl.BlockSpec((tk, tn), lambda i, j, l: (l, j))
    dims = (((0,) if ta else (1,), (1,) if tb else (0,)), ((), ()))

    def body(a_ref, b_ref, o_ref, *acc):
        p = lax.dot_general(a_ref[...].astype(BF16), b_ref[...].astype(BF16), dims, preferred_element_type=F32)
        if nk == 1:
            o_ref[...] = p.astype(out_dtype)
            return
        acc_ref = acc[0]
        step = pl.program_id(2)

        @pl.when(step == 0)
        def _():
            acc_ref[...] = p

        @pl.when(step > 0)
        def _():
            acc_ref[...] += p

        @pl.when(step == nk - 1)
        def _():
            o_ref[...] = acc_ref[...].astype(out_dtype)

    outs, rode = _ride_call(
        body, ride, name=name, grid=(m // tm, n // tn, nk), in_specs=[a_spec, b_spec],
        out_specs=[pl.BlockSpec((tm, tn), lambda i, j, l: (i, j))], out_shape=[jax.ShapeDtypeStruct((m, n), out_dtype)],
        scratch_shapes=[] if nk == 1 else [pltpu.VMEM((tm, tn), F32)], sem=("parallel", "parallel", "arbitrary"),
        args=(a, b))
    return outs[0] if ride is None else (outs[0], rode)


def _rowwise(name, fn, rows, vecs, outs, sums=(), tr=256):
    s = rows[0][0].shape[0]
    tr = min(tr, s)
    nr, nv, no = len(rows), len(vecs), len(outs)

    def body(*refs):
        vals = [r[...] for r in refs[:nr + nv]]
        res = fn(*vals)
        for o_ref, val in zip(refs[nr + nv:nr + nv + no], res[:no]):
            o_ref[...] = val.astype(o_ref.dtype)
        step = pl.program_id(0)
        for s_ref, val in zip(refs[nr + nv + no:], res[no:]):
            @pl.when(step == 0)
            def _(s_ref=s_ref, val=val):
                s_ref[...] = val

            @pl.when(step > 0)
            def _(s_ref=s_ref, val=val):
                s_ref[...] += val

    in_specs = [pl.BlockSpec((tr, w), functools.partial(lambda i, cb: (i, cb), cb=cb)) for _, w, cb in rows]
    in_specs += [pl.BlockSpec(v.shape, lambda i: (0, 0)) for v in vecs]
    out_specs = [pl.BlockSpec((tr, w), lambda i: (i, 0)) for w, _ in outs]
    out_specs += [pl.BlockSpec((1, w), lambda i: (0, 0)) for w in sums]
    out_shape = [jax.ShapeDtypeStruct((s, w), dt) for w, dt in outs]
    out_shape += [jax.ShapeDtypeStruct((1, w), F32) for w in sums]
    return pl.pallas_call(
        body, name=name, grid=(s // tr,), in_specs=in_specs, out_specs=out_specs, out_shape=out_shape,
        compiler_params=_params(("arbitrary",) if sums else ("parallel",)),
    )(*[r[0] for r in rows], *vecs)


def _colsum(x):
    return jnp.sum(x, axis=0, keepdims=True)


def _sigmoid(x):
    return 1.0 / (1.0 + jnp.exp(-x))


def _log_sigmoid(x):
    return jnp.minimum(x, 0.0) - jnp.log(1.0 + jnp.exp(-jnp.abs(x)))


def _ln_stats(r):
    mu = jnp.mean(r, axis=-1, keepdims=True)
    xc = r - mu
    var = jnp.mean(xc * xc, axis=-1, keepdims=True)
    rstd = lax.rsqrt(var + LN_EPS)
    return xc * rstd, rstd


def _ln_bwd(dy, xhat, rstd, g):
    dxh = dy * g
    m1 = jnp.mean(dxh, axis=-1, keepdims=True)
    m2 = jnp.mean(dxh * xhat, axis=-1, keepdims=True)
    return rstd * (dxh - m1 - xhat * m2)


def _mod_exchange(c_row, w_ada, b_ada_loc):
    def body(c_ref, w_ref, b_ref, call_ref, mod_ref, piece_ref, send_sems, recv_sems):
        me = _me()[3]
        call_ref[me] = c_ref[...]
        sent = []
        for r in range(1, N_DEV):
            peer, _ = _peer(r)
            cp = pltpu.make_async_remote_copy(
                src_ref=c_ref, dst_ref=call_ref.at[me], send_sem=send_sems.at[0, r - 1],
                recv_sem=recv_sems.at[0, r - 1], device_id=peer, device_id_type=MESH)
            cp.start()
            sent.append(cp)
        for r in range(1, N_DEV):
            peer, pidx = _peer(r)
            pltpu.make_async_remote_copy(
                src_ref=c_ref, dst_ref=call_ref.at[pidx], send_sem=send_sems.at[0, r - 1],
                recv_sem=recv_sems.at[0, r - 1], device_id=peer, device_id_type=MESH).wait_recv()
        c_all = jnp.concatenate([call_ref[d] for d in range(N_DEV)], axis=0)
        mod_loc = jnp.dot(c_all, w_ref[...], preferred_element_type=F32,
                          precision=lax.Precision.HIGHEST) + b_ref[...]
        for d in range(N_DEV):
            piece_ref[d] = mod_loc[d:d + 1, :]
        mod_ref[me] = piece_ref[me]
        for r in range(1, N_DEV):
            peer, pidx = _peer(r)
            cp = pltpu.make_async_remote_copy(
                src_ref=piece_ref.at[pidx], dst_ref=mod_ref.at[me], send_sem=send_sems.at[1, r - 1],
                recv_sem=recv_sems.at[1, r - 1], device_id=peer, device_id_type=MESH)
            cp.start()
            sent.append(cp)
        for r in range(1, N_DEV):
            peer, pidx = _peer(r)
            pltpu.make_async_remote_copy(
                src_ref=piece_ref.at[me], dst_ref=mod_ref.at[pidx], send_sem=send_sems.at[1, r - 1],
                recv_sem=recv_sems.at[1, r - 1], device_id=peer, device_id_type=MESH).wait_recv()
        for cp in sent:
            cp.wait_send()

    vmem = pl.BlockSpec(memory_space=pltpu.VMEM)
    return pl.pallas_call(
        body, name="mod_exchange", in_specs=[vmem, vmem, vmem], out_specs=[vmem, vmem],
        out_shape=[jax.ShapeDtypeStruct((N_DEV, 1, D_MODEL), F32), jax.ShapeDtypeStruct((N_DEV, 1, ADA_SHARD), F32)],
        scratch_shapes=[pltpu.VMEM((N_DEV, 1, ADA_SHARD), F32),
                        pltpu.SemaphoreType.DMA((2, N_DEV - 1)), pltpu.SemaphoreType.DMA((2, N_DEV - 1))],
        compiler_params=_params(),
    )(c_row, w_ada, b_ada_loc)


def _split3(x):
    hi = x.astype(BF16)
    r1 = x - hi.astype(F32)
    mid = r1.astype(BF16)
    lo = (r1 - mid.astype(F32)).astype(BF16)
    return hi, mid, lo


def _scan_rows(x_ref, o_ref, s, reverse, pre=None, post=None):
    tb = min(TQ, s)
    nb = s // tb
    row = lax.broadcasted_iota(jnp.int32, (tb, tb), 0)
    col = lax.broadcasted_iota(jnp.int32, (tb, tb), 1)
    tri = jnp.where((col >= row) if reverse else (col <= row), 1.0, 0.0).astype(BF16)

    def step(i, carry):
        blk = (nb - 1 - i) if reverse else i
        off = pl.multiple_of(blk * tb, tb)
        x = x_ref[pl.ds(off, tb), :]
        if pre is not None:
            x = pre(x, off)
        acc = carry
        for piece in _split3(x):
            acc = acc + jnp.dot(tri, piece, preferred_element_type=F32)
        o_ref[pl.ds(off, tb), :] = acc if post is None else post(acc, off)
        edge = acc[0:1, :] if reverse else acc[tb - 1:tb, :]
        return jnp.broadcast_to(edge, (tb, LANES))

    lax.fori_loop(0, nb, step, jnp.zeros((tb, LANES), F32))


def _forget_cumsum(f_raw, b_pad):
    s = f_raw.shape[0]

    def body(f_ref, b_ref, cum_ref):
        b = b_ref[...]
        _scan_rows(f_ref, cum_ref, s, False, pre=lambda x, off: _log_sigmoid(x + b))

    vmem = pl.BlockSpec(memory_space=pltpu.VMEM)
    return pl.pallas_call(body, name="forget_cumsum", in_specs=[vmem, vmem], out_specs=vmem,
                          out_shape=jax.ShapeDtypeStruct((s, LANES), F32), compiler_params=_params())(f_raw, b_pad)


def _forget_bwd(dcum, f_raw, b_pad):
    s = f_raw.shape[0]

    def body(d_ref, f_ref, b_ref, df_ref, db_ref, tmp_ref):
        b = b_ref[...]
        _scan_rows(d_ref, tmp_ref, s, True)
        df = tmp_ref[...] * _sigmoid(-(f_ref[...] + b))
        df_ref[...] = df.astype(BF16)
        db_ref[...] = _colsum(df)

    vmem = pl.BlockSpec(memory_space=pltpu.VMEM)
    return pl.pallas_call(
        body, name="forget_bwd", in_specs=[vmem, vmem, vmem], out_specs=[vmem, vmem],
        out_shape=[jax.ShapeDtypeStruct((s, LANES), BF16), jax.ShapeDtypeStruct((1, LANES), F32)],
        scratch_shapes=[pltpu.VMEM((s, LANES), F32)], compiler_params=_params())(dcum, f_raw, b_pad)


def _dot_nt(a, b):
    return lax.dot_general(a, b, (((1,), (1,)), ((), ())), preferred_element_type=F32)


def _dot_tn(a, b):
    return lax.dot_general(a, b, (((0,), (0,)), ((), ())), preferred_element_type=F32)


def _head_masks():
    lane = lax.broadcasted_iota(jnp.int32, (TQ, LANES), 1)
    return lane, [lane < HEAD_DIM, lane >= HEAD_DIM]


def _pick(mask, x):
    return jnp.where(mask, x, jnp.zeros_like(x))


def _qkv_specs(s, col0):
    nb = ATTN_W // LANES
    return [pl.BlockSpec((TQ, LANES), lambda hp, qi: (qi, col0 + hp)),
            pl.BlockSpec((s, LANES), lambda hp, qi: (0, col0 + nb + hp)),
            pl.BlockSpec((s, LANES), lambda hp, qi: (0, col0 + 2 * nb + hp))]


def _pair_spec():
    return pl.BlockSpec((TQ, LANES), lambda hp, qi: (qi, hp))


def _fox_fwd(qkv, cum_col, cum_row, ride=None):
    s = qkv.shape[0]
    nq = s // TQ

    def body(q_ref, k_ref, v_ref, cc_ref, cr_ref, o_ref, o32_ref, lse_ref):
        hp, qi = pl.program_id(0), pl.program_id(1)
        lane, masks = _head_masks()
        row = lax.broadcasted_iota(jnp.int32, (TQ, TQ), 0)
        col = lax.broadcasted_iota(jnp.int32, (TQ, TQ), 1)
        causal = col <= row
        q2 = q_ref[...] * jnp.asarray(ATTN_SCALE, BF16)
        cc = cc_ref[...]
        qms = [_pick(masks[e], q2) for e in range(2)]
        cqs = [jnp.sum(jnp.where(lane == 2 * hp + e, cc, 0.0), axis=1, keepdims=True) for e in range(2)]

        def tile(kb, carry, masked):
            off = pl.multiple_of(kb * TQ, TQ)
            k2, v2 = k_ref[pl.ds(off, TQ), :], v_ref[pl.ds(off, TQ), :]
            new = []
            for e in range(2):
                m, l, acc = carry[e]
                sc = _dot_nt(qms[e], k2) + (cqs[e] - cr_ref[e:e + 1, pl.ds(off, TQ)])
                if masked:
                    sc = jnp.where(causal, sc, -jnp.inf)
                m_new = jnp.maximum(m, jnp.max(sc, axis=1, keepdims=True))
                p = jnp.exp(sc - m_new)
                corr = jnp.exp(m - m_new)
                l = corr * l + jnp.sum(p, axis=1, keepdims=True)
                acc = corr * acc + jnp.dot(p.astype(BF16), v2, preferred_element_type=F32)
                new.append((m_new, l, acc))
            return tuple(new)

        init = (jnp.full((TQ, 1), -jnp.inf, F32), jnp.zeros((TQ, 1), F32), jnp.zeros((TQ, LANES), F32))
        carry = lax.fori_loop(0, qi, lambda kb, cr: tile(kb, cr, False), tile(qi, (init, init), True))
        outs = [acc / l for _, l, acc in carry]
        lses = [m + jnp.log(l) for m, l, _ in carry]
        out = jnp.where(masks[0], outs[0], outs[1])
        o_ref[...] = out.astype(BF16)
        o32_ref[...] = out
        lse_ref[...] = jnp.where(masks[0], lses[0], lses[1])

    return _ride_call(
        body, ride, name="fox_fwd", grid=(N_HEADS // 2, nq),
        in_specs=_qkv_specs(s, 0) + [pl.BlockSpec((TQ, LANES), lambda hp, qi: (qi, 0)),
                                     pl.BlockSpec((None, 2, s), lambda hp, qi: (hp, 0, 0))],
        out_specs=[_pair_spec(), _pair_spec(), _pair_spec()],
        out_shape=[jax.ShapeDtypeStruct((s, ATTN_W), BF16), jax.ShapeDtypeStruct((s, ATTN_W), F32),
                   jax.ShapeDtypeStruct((s, ATTN_W), F32)],
        scratch_shapes=[], sem=("parallel", "parallel"), args=(qkv, qkv, qkv, cum_col, cum_row))


def _fox_bwd(qkv, cum_col, cum_row, o, lse, do, ride=None):
    s = qkv.shape[0]
    nq = s // TQ

    def body(q_ref, k_ref, v_ref, cc_ref, cr_ref, o_ref, lse_ref, do_ref,
             dq_ref, dk_ref, dv_ref, dcr_ref, dcq_ref, dk_acc, dv_acc):
        hp, qi = pl.program_id(0), pl.program_id(1)

        @pl.when(qi == 0)
        def _():
            dk_acc[...] = jnp.zeros_like(dk_acc)
            dv_acc[...] = jnp.zeros_like(dv_acc)
            dcr_ref[...] = jnp.zeros_like(dcr_ref)

        lane, masks = _head_masks()
        row = lax.broadcasted_iota(jnp.int32, (TQ, TQ), 0)
        col = lax.broadcasted_iota(jnp.int32, (TQ, TQ), 1)
        causal = col <= row
        q2 = q_ref[...] * jnp.asarray(ATTN_SCALE, BF16)
        do2 = do_ref[...]
        prod = do2.astype(F32) * o_ref[...].astype(F32)
        lse2 = lse_ref[...]
        cc = cc_ref[...]
        qms = [_pick(masks[e], q2) for e in range(2)]
        doms = [_pick(masks[e], do2) for e in range(2)]
        deltas = [jnp.sum(jnp.where(masks[e], prod, 0.0), axis=1, keepdims=True) for e in range(2)]
        lses = [jnp.max(jnp.where(masks[e], lse2, -jnp.inf), axis=1, keepdims=True) for e in range(2)]
        cqs = [jnp.sum(jnp.where(lane == 2 * hp + e, cc, 0.0), axis=1, keepdims=True) for e in range(2)]

        def tile(kb, carry, masked):
            off = pl.multiple_of(kb * TQ, TQ)
            k2, v2 = k_ref[pl.ds(off, TQ), :], v_ref[pl.ds(off, TQ), :]
            new, dk, dv = [], None, None
            for e in range(2):
                dq, rowsum = carry[e]
                sc = _dot_nt(qms[e], k2) + (cqs[e] - cr_ref[e:e + 1, pl.ds(off, TQ)])
                p = jnp.exp(sc - lses[e])
                if masked:
                    p = jnp.where(causal, p, 0.0)
                ds = p * (_dot_nt(doms[e], v2) - deltas[e])
                dsb = ds.astype(BF16)
                dk_e, dv_e = _dot_tn(dsb, qms[e]), _dot_tn(p.astype(BF16), doms[e])
                dk, dv = (dk_e, dv_e) if e == 0 else (dk + dk_e, dv + dv_e)
                dcr_ref[e:e + 1, pl.ds(off, TQ)] -= _colsum(ds)
                new.append((dq + jnp.dot(dsb, k2, preferred_element_type=F32),
                            rowsum + jnp.sum(ds, axis=1, keepdims=True)))
            dk_acc[pl.ds(off, TQ), :] += dk
            dv_acc[pl.ds(off, TQ), :] += dv
            return tuple(new)

        init = (jnp.zeros((TQ, LANES), F32), jnp.zeros((TQ, 1), F32))
        carry = tile(qi, lax.fori_loop(0, qi, lambda kb, cr: tile(kb, cr, False), (init, init)), True)
        dq_ref[...] = (jnp.where(masks[0], carry[0][0], carry[1][0]) * ATTN_SCALE).astype(BF16)
        dcq_ref[...] = jnp.where(masks[0], carry[0][1], carry[1][1])

        @pl.when(qi == nq - 1)
        def _():
            dk_ref[...] = dk_acc[...].astype(BF16)
            dv_ref[...] = dv_acc[...].astype(BF16)

    seq_spec = pl.BlockSpec((s, LANES), lambda hp, qi: (0, hp))
    return _ride_call(
        body, ride, name="fox_bwd", grid=(N_HEADS // 2, nq),
        in_specs=_qkv_specs(s, 0) + [pl.BlockSpec((TQ, LANES), lambda hp, qi: (qi, 0)),
                                     pl.BlockSpec((None, 2, s), lambda hp, qi: (hp, 0, 0)),
                                     _pair_spec(), _pair_spec(), _pair_spec()],
        out_specs=[_pair_spec(), seq_spec, seq_spec, pl.BlockSpec((None, 2, s), lambda hp, qi: (hp, 0, 0)),
                   _pair_spec()],
        out_shape=[jax.ShapeDtypeStruct((s, ATTN_W), BF16)] * 3 + [jax.ShapeDtypeStruct((N_HEADS // 2, 2, s), F32),
                                                                    jax.ShapeDtypeStruct((s, ATTN_W), F32)],
        scratch_shapes=[pltpu.VMEM((s, LANES), F32), pltpu.VMEM((s, LANES), F32)],
        sem=("parallel", "arbitrary"), args=(qkv, qkv, qkv, cum_col, cum_row, o, lse, do))


def _scan_matrix(reverse):
    row = lax.broadcasted_iota(jnp.int32, (SCAN_W, SCAN_W), 0)
    col = lax.broadcasted_iota(jnp.int32, (SCAN_W, SCAN_W), 1)
    return jnp.where((row > col) if reverse else (row < col), 1.0, 0.0).astype(BF16)


def _scan_cols(x, tri, reverse):
    nblk = x.shape[1] // SCAN_W
    parts, total = [None] * nblk, None
    for b in (reversed(range(nblk)) if reverse else range(nblk)):
        blk = x[:, b * SCAN_W:(b + 1) * SCAN_W]
        part = jnp.dot(blk.astype(BF16), tri, preferred_element_type=F32)
        parts[b] = part if total is None else part + total
        rowsum = jnp.sum(blk, axis=1, keepdims=True)
        total = rowsum if total is None else total + rowsum
    return (parts[0] if nblk == 1 else jnp.concatenate(parts, axis=1)), total


LOG2E = 1.4426950408889634


def _sb_logits(qm, k2):
    z = _dot_nt(qm, k2) * LOG2E
    soft = jnp.log2(1.0 + jnp.exp2(-jnp.abs(z)))
    lb = jnp.minimum(z, 0.0) - soft
    return lb, lb - z


def _tri_base(qi):
    return (qi * (qi + 1)) // 2


def _sb_fwd(qkv):
    s = qkv.shape[0]
    nq = s // TQ

    def body(q_ref, k_ref, v_ref, o_ref, t_ref, buf, sems):
        hp, qi = pl.program_id(0), pl.program_id(1)
        _, masks = _head_masks()
        row = lax.broadcasted_iota(jnp.int32, (TQ, TQ), 0)
        col = lax.broadcasted_iota(jnp.int32, (TQ, TQ), 1)
        strict = col < row
        suffix = _scan_matrix(True)
        q2 = q_ref[...] * jnp.asarray(ATTN_SCALE, BF16)
        qms = [_pick(masks[e], q2) for e in range(2)]
        base = _tri_base(qi)

        def store(e, slot, kb):
            return pltpu.make_async_copy(buf.at[e, slot], t_ref.at[2 * hp + e, base + kb], sems.at[e, slot])

        def tile(n, carry, masked):
            kb = qi - n
            off = pl.multiple_of(kb * TQ, TQ)
            k2, v2 = k_ref[pl.ds(off, TQ), :], v_ref[pl.ds(off, TQ), :]
            slot = n % 2
            new = []
            for e in range(2):
                run, acc = carry[e]
                lb, lo = _sb_logits(qms[e], k2)
                if masked:
                    lo = jnp.where(strict, lo, 0.0)
                rest, total = _scan_cols(lo, suffix, True)
                a = jnp.exp2(lb + rest + run)
                if masked:
                    a = jnp.where(strict, a, 0.0)
                ab = a.astype(BF16)
                acc = acc + jnp.dot(ab, v2, preferred_element_type=F32)
                if not masked:
                    @pl.when(n >= 2)
                    def _(e=e, slot=slot, kb=kb):
                        store(e, slot, kb + 2).wait()
                buf[e, slot, 0] = ab
                buf[e, slot, 1] = jnp.exp2(lb).astype(BF16)
                store(e, slot, kb).start()
                new.append((run + total, acc))
            return tuple(new)

        init = (jnp.zeros((TQ, 1), F32), jnp.zeros((TQ, LANES), F32))
        carry = lax.fori_loop(0, qi, lambda it, cr: tile(it + 1, cr, False), tile(0, (init, init), True))
        for e in range(2):
            store(e, qi % 2, 0).wait()

            @pl.when(qi >= 1)
            def _(e=e):
                store(e, (qi - 1) % 2, 1).wait()
        o_ref[...] = jnp.where(masks[0], carry[0][1], carry[1][1]).astype(BF16)

    ntri = nq * (nq + 1) // 2
    return pl.pallas_call(
        body, name="sb_fwd", grid=(N_HEADS // 2, nq), in_specs=_qkv_specs(s, 3 * ATTN_W // LANES),
        out_specs=[_pair_spec(), ANY],
        out_shape=[jax.ShapeDtypeStruct((s, ATTN_W), BF16), jax.ShapeDtypeStruct((N_HEADS, ntri, 2, TQ, TQ), BF16)],
        scratch_shapes=[pltpu.VMEM((2, 2, 2, TQ, TQ), BF16), pltpu.SemaphoreType.DMA((2, 2))],
        compiler_params=_params(("arbitrary", "arbitrary")),
    )(qkv, qkv, qkv)


def _sb_bwd(qkv, tiles, do):
    s = qkv.shape[0]
    nq = s // TQ

    def body(q_ref, k_ref, v_ref, t_ref, do_ref, dq_ref, dk_ref, dv_ref, dk_acc, dv_acc, buf, sems):
        hp, qi = pl.program_id(0), pl.program_id(1)

        @pl.when(qi == 0)
        def _():
            dk_acc[...] = jnp.zeros_like(dk_acc)
            dv_acc[...] = jnp.zeros_like(dv_acc)

        _, masks = _head_masks()
        row = lax.broadcasted_iota(jnp.int32, (TQ, TQ), 0)
        col = lax.broadcasted_iota(jnp.int32, (TQ, TQ), 1)
        strict = col < row
        prefix = _scan_matrix(False)
        q2 = q_ref[...] * jnp.asarray(ATTN_SCALE, BF16)
        do2 = do_ref[...]
        qms = [_pick(masks[e], q2) for e in range(2)]
        doms = [_pick(masks[e], do2) for e in range(2)]
        base = _tri_base(qi)

        def fetch(e, kb):
            return pltpu.make_async_copy(t_ref.at[2 * hp + e, base + kb], buf.at[e, kb % 2], sems.at[e, kb % 2])

        for e in range(2):
            fetch(e, 0).start()

        def tile(kb, carry, masked):
            off = pl.multiple_of(kb * TQ, TQ)
            k2, v2 = k_ref[pl.ds(off, TQ), :], v_ref[pl.ds(off, TQ), :]
            new, dk, dv = [], None, None
            for e in range(2):
                gsum, dq = carry[e]
                if not masked:
                    fetch(e, kb + 1).start()
                fetch(e, kb).wait()
                ab = buf[e, kb % 2, 0]
                beta = buf[e, kb % 2, 1].astype(F32)
                g = ab.astype(F32) * _dot_nt(doms[e], v2)
                before, gtotal = _scan_cols(g, prefix, False)
                dz = g - beta * (g + before + gsum)
                if masked:
                    dz = jnp.where(strict, dz, 0.0)
                dzb = dz.astype(BF16)
                dk_e, dv_e = _dot_tn(dzb, qms[e]), _dot_tn(ab, doms[e])
                dk, dv = (dk_e, dv_e) if e == 0 else (dk + dk_e, dv + dv_e)
                new.append((gsum + gtotal, dq + jnp.dot(dzb, k2, preferred_element_type=F32)))
            dk_acc[pl.ds(off, TQ), :] += dk
            dv_acc[pl.ds(off, TQ), :] += dv
            return tuple(new)

        init = (jnp.zeros((TQ, 1), F32), jnp.zeros((TQ, LANES), F32))
        carry = tile(qi, lax.fori_loop(0, qi, lambda kb, cr: tile(kb, cr, False), (init, init)), True)
        dq_ref[...] = (jnp.where(masks[0], carry[0][1], carry[1][1]) * ATTN_SCALE).astype(BF16)

        @pl.when(qi == nq - 1)
        def _():
            dk_ref[...] = dk_acc[...].astype(BF16)
            dv_ref[...] = dv_acc[...].astype(BF16)

    seq_spec = pl.BlockSpec((s, LANES), lambda hp, qi: (0, hp))
    return pl.pallas_call(
        body, name="sb_bwd", grid=(N_HEADS // 2, nq),
        in_specs=_qkv_specs(s, 3 * ATTN_W // LANES) + [ANY, _pair_spec()],
        out_specs=[_pair_spec(), seq_spec, seq_spec],
        out_shape=[jax.ShapeDtypeStruct((s, ATTN_W), BF16)] * 3,
        scratch_shapes=[pltpu.VMEM((s, LANES), F32), pltpu.VMEM((s, LANES), F32),
                        pltpu.VMEM((2, 2, 2, TQ, TQ), BF16), pltpu.SemaphoreType.DMA((2, 2))],
        compiler_params=_params(("arbitrary", "arbitrary")),
    )(qkv, qkv, qkv, tiles, do)


CONV_TR = 256


def _shift_down(x, halo, n):
    rolled = pltpu.roll(x, n, 0)
    rid = lax.broadcasted_iota(jnp.int32, x.shape, 0)
    for j in range(n):
        rolled = jnp.where(rid == j, halo[8 - n + j:8 - n + j + 1, :], rolled)
    return rolled


def _shift_up(x, halo, n):
    rows = x.shape[0]
    rolled = pltpu.roll(x, rows - n, 0)
    rid = lax.broadcasted_iota(jnp.int32, x.shape, 0)
    for j in range(n):
        rolled = jnp.where(rid == rows - n + j, halo[j:j + 1, :], rolled)
    return rolled


def _conv_fwd_block(x, halo, w, b):
    return b + _shift_down(x, halo, 2) * w[0:1, :] + _shift_down(x, halo, 1) * w[1:2, :] + x * w[2:3, :]


def _conv_specs(tr, s):
    pair = 2 * FF_HALF
    blk = pl.BlockSpec((tr, pair), lambda j, i: (i, j))
    prev = pl.BlockSpec((8, pair), lambda j, i: (jnp.maximum(i * (tr // 8) - 1, 0), j))
    nxt = pl.BlockSpec((8, pair), lambda j, i: (jnp.minimum((i + 1) * (tr // 8), s // 8 - 1), j))
    return blk, prev, nxt


def _conv_gate_fwd(hpre, conv_w, conv_b):
    s = hpre.shape[0]
    tr = min(CONV_TR, s)
    blk, prev, _ = _conv_specs(tr, s)

    def body(x_ref, halo_ref, w_ref, b_ref, a_ref):
        i = pl.program_id(1)
        halo = jnp.where(i > 0, halo_ref[...], 0.0)
        h = _conv_fwd_block(x_ref[...], halo, w_ref[...], b_ref[...])
        hg, hv = h[:, :FF_HALF], h[:, FF_HALF:]
        a_ref[...] = (hg * _sigmoid(hg) * hv).astype(BF16)

    return pl.pallas_call(
        body, name="conv_gate_fwd", grid=(2, s // tr),
        in_specs=[blk, prev, pl.BlockSpec((3, 2 * FF_HALF), lambda j, i: (0, j)),
                  pl.BlockSpec((1, 2 * FF_HALF), lambda j, i: (0, j))],
        out_specs=pl.BlockSpec((tr, FF_HALF), lambda j, i: (i, j)),
        out_shape=jax.ShapeDtypeStruct((s, D_FF), BF16),
        compiler_params=_params(("parallel", "parallel")),
    )(hpre, hpre, conv_w, conv_b)


def _conv_gate_bwd(hpre, da, conv_w, conv_b):
    s = hpre.shape[0]
    tr = min(CONV_TR, s)
    blk, prev, _ = _conv_specs(tr, s)

    def body(x_ref, halo_ref, da_ref, w_ref, b_ref, dh_ref, db_ref, dw_ref):
        i = pl.program_id(1)
        halo = jnp.where(i > 0, halo_ref[...], 0.0)
        x = x_ref[...]
        h = _conv_fwd_block(x, halo, w_ref[...], b_ref[...])
        hg, hv = h[:, :FF_HALF], h[:, FF_HALF:]
        da_blk = da_ref[...].astype(F32)
        sg = _sigmoid(hg)
        dhg = da_blk * hv * (sg * (1.0 + hg * (1.0 - sg)))
        dhv = da_blk * (hg * sg)
        dh_ref[:, :FF_HALF] = dhg.astype(BF16)
        dh_ref[:, FF_HALF:] = dhv.astype(BF16)
        x2, x1 = _shift_down(x, halo, 2), _shift_down(x, halo, 1)
        parts = []
        for lo, dpart in ((0, dhg), (FF_HALF, dhv)):
            cols = slice(lo, lo + FF_HALF)
            parts.append((cols, _colsum(dpart), _colsum(dpart * x2[:, cols]), _colsum(dpart * x1[:, cols]),
                          _colsum(dpart * x[:, cols])))

        @pl.when(i == 0)
        def _():
            for cols, db, dw0, dw1, dw2 in parts:
                db_ref[:, cols] = db
                dw_ref[0:1, cols] = dw0
                dw_ref[1:2, cols] = dw1
                dw_ref[2:3, cols] = dw2

        @pl.when(i > 0)
        def _():
            for cols, db, dw0, dw1, dw2 in parts:
                db_ref[:, cols] += db
                dw_ref[0:1, cols] += dw0
                dw_ref[1:2, cols] += dw1
                dw_ref[2:3, cols] += dw2

    pair = 2 * FF_HALF
    return pl.pallas_call(
        body, name="conv_gate_bwd", grid=(2, s // tr),
        in_specs=[blk, prev, pl.BlockSpec((tr, FF_HALF), lambda j, i: (i, j)),
                  pl.BlockSpec((3, pair), lambda j, i: (0, j)), pl.BlockSpec((1, pair), lambda j, i: (0, j))],
        out_specs=[blk, pl.BlockSpec((1, pair), lambda j, i: (0, j)), pl.BlockSpec((3, pair), lambda j, i: (0, j))],
        out_shape=[jax.ShapeDtypeStruct((s, 2 * D_FF), BF16), jax.ShapeDtypeStruct((1, 2 * D_FF), F32),
                   jax.ShapeDtypeStruct((3, 2 * D_FF), F32)],
        compiler_params=_params(("parallel", "arbitrary")),
    )(hpre, hpre, da, conv_w, conv_b)


def _conv_input_bwd(dh, conv_w):
    s = dh.shape[0]
    tr = min(CONV_TR, s)
    blk, _, _ = _conv_specs(tr, s)
    nblk = s // tr

    def body(x_ref, halo_ref, w_ref, o_ref):
        i = pl.program_id(1)
        halo = jnp.where(i < nblk - 1, halo_ref[...].astype(F32), 0.0)
        x, w = x_ref[...].astype(F32), w_ref[...]
        o_ref[...] = (x * w[2:3, :] + _shift_up(x, halo, 1) * w[1:2, :] + _shift_up(x, halo, 2) * w[0:1, :]).astype(BF16)

    nxt = pl.BlockSpec((16, 2 * FF_HALF), lambda j, i: (jnp.minimum((i + 1) * (tr // 16), s // 16 - 1), j))
    return pl.pallas_call(
        body, name="conv_input_bwd", grid=(2, nblk),
        in_specs=[blk, nxt, pl.BlockSpec((3, 2 * FF_HALF), lambda j, i: (0, j))], out_specs=blk,
        out_shape=jax.ShapeDtypeStruct((s, 2 * D_FF), BF16),
        compiler_params=_params(("parallel", "parallel")),
    )(dh, dh, conv_w)


def _adamw_math(w, g, m, v):
    m = ADAM_B1 * m + (1.0 - ADAM_B1) * g
    v = ADAM_B2 * v + (1.0 - ADAM_B2) * (g * g)
    m_hat = m / (1.0 - ADAM_B1 ** ADAM_STEP)
    v_hat = v / (1.0 - ADAM_B2 ** ADAM_STEP)
    delta = -ADAM_LR * (m_hat / (jnp.sqrt(v_hat) + ADAM_EPS) + ADAM_WD * w)
    return delta, m, v


def _adamw(name, g8, w, m, v):
    r, c = w.shape
    tr = _row_tile(r, c)

    def body(g_ref, w_ref, m_ref, v_ref, go_ref, d_ref, mo_ref, vo_ref):
        g = g_ref[0].astype(F32)
        for d in range(1, N_DEV):
            g = g + g_ref[d].astype(F32)
        delta, mn, vn = _adamw_math(w_ref[...], g, m_ref[...], v_ref[...])
        go_ref[...] = g
        d_ref[...] = delta
        mo_ref[...] = mn
        vo_ref[...] = vn

    spec = pl.BlockSpec((tr, c), lambda i: (i, 0))
    return pl.pallas_call(
        body, name=name, grid=(r // tr,),
        in_specs=[pl.BlockSpec((N_DEV, tr, c), lambda i: (0, i, 0)), spec, spec, spec], out_specs=[spec] * 4,
        out_shape=[jax.ShapeDtypeStruct((r, c), F32)] * 4, compiler_params=_params(("parallel",)),
    )(g8, w, m, v)


def _adamw_ada(c_t, dmod, w, m, v):
    r, c = w.shape
    tr = _row_tile(r, c)

    def body(ct_ref, dm_ref, w_ref, m_ref, v_ref, go_ref, d_ref, mo_ref, vo_ref):
        ct, dm = ct_ref[...], dm_ref[...]
        g = ct[:, 0:1] * dm[0:1, :]
        for b in range(1, N_DEV):
            g = g + ct[:, b:b + 1] * dm[b:b + 1, :]
        delta, mn, vn = _adamw_math(w_ref[...], g, m_ref[...], v_ref[...])
        go_ref[...] = g
        d_ref[...] = delta
        mo_ref[...] = mn
        vo_ref[...] = vn

    spec = pl.BlockSpec((tr, c), lambda i: (i, 0))
    return pl.pallas_call(
        body, name="adamw_w_ada", grid=(r // tr,),
        in_specs=[pl.BlockSpec((tr, N_DEV), lambda i: (i, 0)), pl.BlockSpec((N_DEV, c), lambda i: (0, 0)),
                  spec, spec, spec],
        out_specs=[spec] * 4, out_shape=[jax.ShapeDtypeStruct((r, c), F32)] * 4,
        compiler_params=_params(("parallel",)),
    )(c_t, dmod, w, m, v)


def _cols_from_slots(g):
    n, r, c = g.shape
    return jnp.transpose(g, (1, 0, 2)).reshape(r, n * c)


def _cols_to_slots(w):
    r, c = w.shape
    return jnp.transpose(w.reshape(r, N_DEV, c // N_DEV), (1, 0, 2))


def _pair_cols(w):
    g0, g1 = w[..., 0:FF_HALF], w[..., FF_HALF:D_FF]
    v0, v1 = w[..., D_FF:D_FF + FF_HALF], w[..., D_FF + FF_HALF:]
    return jnp.concatenate([g0, v0, g1, v1], axis=-1)


def _unpair_cols(w):
    g0, v0 = w[..., 0:FF_HALF], w[..., FF_HALF:D_FF]
    g1, v1 = w[..., D_FF:D_FF + FF_HALF], w[..., D_FF + FF_HALF:]
    return jnp.concatenate([g0, g1, v0, v1], axis=-1)


def _row(v):
    return v.reshape(1, -1)


def kernel(x, c, w_ada, b_ada, w_in, b_forget, w_fox_proj, w_sb_proj, w_o, ln1_g, ln1_b, w_up, conv_w, conv_b, w_down, ln2_g, ln2_b, loss_target, m_w_ada, m_b_ada, m_w_in, m_b_forget, m_w_fox_proj, m_w_sb_proj, m_w_o, m_ln1_g, m_ln1_b, m_w_up, m_conv_w, m_conv_b, m_w_down, m_ln2_g, m_ln2_b, v_w_ada, v_b_ada, v_w_in, v_b_forget, v_w_fox_proj, v_w_sb_proj, v_w_o, v_ln1_g, v_ln1_b, v_w_up, v_conv_w, v_conv_b, v_w_down, v_ln2_g, v_ln2_b):
    s = x.shape[1]
    me = 4 * lax.axis_index("x") + 2 * lax.axis_index("y") + lax.axis_index("c")
    x2 = x.reshape(s, D_MODEL)
    tgt = loss_target.reshape(s, D_MODEL)

    b_ada_loc = lax.dynamic_slice(b_ada, (me * ADA_SHARD,), (ADA_SHARD,)).reshape(1, ADA_SHARD)
    c_all, mod = _mod_exchange(c, w_ada, b_ada_loc)
    mod = mod.reshape(N_MOD, 1, D_MODEL)
    sh1, sc1, gt1, sh2, sc2, gt2 = [mod[i] for i in range(N_MOD)]

    (g_in,) = _exchange("ag_w_in", [w_in.astype(BF16)], scatter=False)
    late_weights = _Ride([w_fox_proj.astype(BF16), w_sb_proj.astype(BF16), w_o.astype(BF16), w_up.astype(BF16),
                          w_down.astype(BF16), conv_w], scatter=False)
    w_in_f = _cols_from_slots(g_in)
    w_proj = jnp.concatenate(
        [w_in_f[:, 0:1536], w_in_f[:, 1544:3080], w_in_f[:, 3080:5128], w_in_f[:, 1536:1544],
         jnp.zeros((D_MODEL, W_PROJ - 5128), BF16)], axis=1)
    w_qkv, w_gates, w_f = w_proj[:, :W_QKV], w_proj[:, W_QKV:W_QKV + W_GATES], w_proj[:, W_QKV + W_GATES:W_QKV + W_GATES + W_F]
    conv_b_p = _pair_cols(_row(conv_b))
    b_f_pad = jnp.pad(_row(b_forget), ((0, 0), (0, LANES - N_HEADS)))

    (u1,) = _rowwise("modulate1", lambda xb, sc, sh: (xb * (1.0 + sc) + sh,),
                     [(x2, D_MODEL, 0)], [sc1, sh1], [(D_MODEL, BF16)], tr=512)
    qkv = _mm(u1, w_qkv, name="mm_qkv", out_dtype=BF16)
    gates = _mm(u1, w_gates, name="mm_gates")
    f_raw = _mm(u1, w_f, name="mm_forget")
    cum_col = _forget_cumsum(f_raw, b_f_pad)
    cum_row = jnp.transpose(cum_col[:, :N_HEADS]).reshape(N_HEADS // 2, 2, s)
    (y_fox, y_fox32, lse), (g_fox, g_sb, g_o, g_up, g_down, g_cw) = _fox_fwd(qkv, cum_col, cum_row, ride=late_weights)
    w_fox_f = _cols_from_slots(g_fox)
    w_sb_f = _cols_from_slots(g_sb)
    w_o_f = g_o.reshape(D_MODEL, D_MODEL)
    w_up_p = _pair_cols(_cols_from_slots(g_up))
    w_down_f = g_down.reshape(D_FF, D_MODEL)
    conv_w_p = _pair_cols(_cols_from_slots(g_cw))
    y_sb, sb_run = _sb_fwd(qkv)
    pf = _mm(y_fox, w_fox_f, name="mm_fox_proj")
    ps = _mm(y_sb, w_sb_f, name="mm_sb_proj")
    (merged,) = _rowwise("gate_merge", lambda ga, gb, a, b: (_sigmoid(ga) * a + _sigmoid(gb) * b,),
                         [(gates, D_MODEL, 0), (gates, D_MODEL, 1), (pf, D_MODEL, 0), (ps, D_MODEL, 0)], [],
                         [(D_MODEL, BF16)])
    attn_out = _mm(merged, w_o_f, name="mm_w_o")

    def ln_fwd(xb, fb, gt, g, b):
        xhat, _ = _ln_stats(ALPHA * xb + (1.0 + gt) * fb)
        return xhat * g + b

    def ln1_mod(xb, fb, gt, g, b, sc, sh):
        y = ln_fwd(xb, fb, gt, g, b)
        return y, y * (1.0 + sc) + sh

    x1, u2 = _rowwise("ln1_modulate2", ln1_mod, [(x2, D_MODEL, 0), (attn_out, D_MODEL, 0)],
                      [gt1, _row(ln1_g), _row(ln1_b), sc2, sh2], [(D_MODEL, F32), (D_MODEL, BF16)])

    hpre = _mm(u2, w_up_p, name="mm_w_up", tn=1408)
    act = _conv_gate_fwd(hpre, conv_w_p, conv_b_p)
    ffn_out = _mm(act, w_down_f, name="mm_w_down", tk=1408)

    def ln2_bwd(xb, fb, tb, gt, g, b):
        xhat, rstd = _ln_stats(ALPHA * xb + (1.0 + gt) * fb)
        err = (xhat * g + b) - tb
        dy = err * (1.0 / D_MODEL)
        dr = _ln_bwd(dy, xhat, rstd, g)
        return (dr * (1.0 + gt), ALPHA * dr,
                _colsum(err * err), _colsum(dy * xhat), _colsum(dy), _colsum(dr * fb))

    dffn, dx1_res, sq_err, d_ln2_g, d_ln2_b, d_gt2 = _rowwise(
        "ln2_bwd", ln2_bwd, [(x1, D_MODEL, 0), (ffn_out, D_MODEL, 0), (tgt, D_MODEL, 0)],
        [gt2, _row(ln2_g), _row(ln2_b)], [(D_MODEL, BF16), (D_MODEL, F32)], sums=[D_MODEL] * 4)
    loss = lax.psum(0.5 * jnp.sum(sq_err) / D_MODEL, ("x", "y", "c"))

    d_w_down = _mm(act, dffn, name="mm_d_w_down", ta=True, tm=1408, out_dtype=BF16)
    d_act = _mm(dffn, w_down_f, name="mm_d_act", tb=True, tn=1408, out_dtype=BF16)
    dh, d_conv_b_p, d_conv_w_p = _conv_gate_bwd(hpre, d_act, conv_w_p, conv_b_p)
    dhpre = _conv_input_bwd(dh, conv_w_p)
    d_w_up_p = _mm(u2, dhpre, name="mm_d_w_up", ta=True, tn=1408, out_dtype=BF16)
    du2 = _mm(dhpre, w_up_p, name="mm_d_u2", tb=True, tk=1408)

    def ln1_bwd(du, dres, x1b, xb, fb, sc, gt, g):
        dx1 = dres + du * (1.0 + sc)
        xhat, rstd = _ln_stats(ALPHA * xb + (1.0 + gt) * fb)
        dr = _ln_bwd(dx1, xhat, rstd, g)
        return (dr * (1.0 + gt), ALPHA * dr,
                _colsum(du * x1b), _colsum(du), _colsum(dx1 * xhat), _colsum(dx1), _colsum(dr * fb))

    d_attn, dx_res, d_sc2, d_sh2, d_ln1_g, d_ln1_b, d_gt1 = _rowwise(
        "ln1_bwd", ln1_bwd,
        [(du2, D_MODEL, 0), (dx1_res, D_MODEL, 0), (x1, D_MODEL, 0), (x2, D_MODEL, 0), (attn_out, D_MODEL, 0)],
        [sc2, gt1, _row(ln1_g)], [(D_MODEL, BF16), (D_MODEL, F32)], sums=[D_MODEL] * 5)

    d_w_o = _mm(merged, d_attn, name="mm_d_w_o", ta=True, out_dtype=BF16)
    d_merged = _mm(d_attn, w_o_f, name="mm_d_merged", tb=True)

    def merge_bwd(dm, ga, gb, a, b):
        sa, sb = _sigmoid(ga), _sigmoid(gb)
        return dm * a * sa * (1.0 - sa), dm * b * sb * (1.0 - sb), dm * sa, dm * sb

    d_ga, d_gb, d_pf, d_ps = _rowwise(
        "gate_merge_bwd", merge_bwd,
        [(d_merged, D_MODEL, 0), (gates, D_MODEL, 0), (gates, D_MODEL, 1), (pf, D_MODEL, 0), (ps, D_MODEL, 0)], [],
        [(D_MODEL, BF16)] * 4)
    d_w_fox = _mm(y_fox, d_pf, name="mm_d_w_fox", ta=True, out_dtype=BF16)
    d_w_sb = _mm(y_sb, d_ps, name="mm_d_w_sb", ta=True, out_dtype=BF16)
    d_y_fox = _mm(d_pf, w_fox_f, name="mm_d_y_fox", tb=True, out_dtype=BF16)
    d_y_sb = _mm(d_ps, w_sb_f, name="mm_d_y_sb", tb=True, out_dtype=BF16)
    early_grads = _Ride(
        [_cols_to_slots(d_w_fox), _cols_to_slots(d_w_sb), d_w_o.reshape(N_DEV, D_MODEL // N_DEV, D_MODEL),
         _cols_to_slots(_unpair_cols(d_w_up_p)), d_w_down.reshape(N_DEV, D_FF // N_DEV, D_MODEL)], scatter=True)
    (dq_a, dk_a, dv_a, d_cum_row, d_cum_q), early_slots = _fox_bwd(qkv, cum_col, cum_row, y_fox32, lse, d_y_fox,
                                                                    ride=early_grads)
    dq_b, dk_b, dv_b = _sb_bwd(qkv, sb_run, d_y_sb)
    d_cum = jnp.transpose(d_cum_row.reshape(N_HEADS, s)) + d_cum_q[:, ::HEAD_DIM]
    d_cum = jnp.pad(d_cum, ((0, 0), (0, LANES - N_HEADS)))
    d_f, d_b_forget = _forget_bwd(d_cum, f_raw, b_f_pad)
    d_proj = jnp.concatenate([dq_a, dk_a, dv_a, dq_b, dk_b, dv_b, d_ga, d_gb, d_f,
                              jnp.zeros((s, W_PROJ - W_QKV - W_GATES - W_F), BF16)], axis=1)
    d_w_proj = _mm(u1, d_proj, name="mm_d_w_in", ta=True, tn=896, out_dtype=BF16)
    d_w_in_f = jnp.concatenate([d_w_proj[:, 0:1536], d_w_proj[:, 5120:5128], d_w_proj[:, 1536:3072],
                                d_w_proj[:, 3072:5120]], axis=1)
    du1, (in_slots,) = _mm(d_proj, w_proj, name="mm_d_u1", tb=True, tk=896,
                           ride=_Ride([_cols_to_slots(d_w_in_f)], scatter=True))

    def x_bwd(du, dres, xb, sc):
        return dres + du * (1.0 + sc), _colsum(du * xb), _colsum(du)

    grad_x, d_sc1, d_sh1 = _rowwise("x_bwd", x_bwd, [(du1, D_MODEL, 0), (dx_res, D_MODEL, 0), (x2, D_MODEL, 0)],
                                    [sc1], [(D_MODEL, F32)], sums=[D_MODEL] * 2, tr=512)

    d_conv_b = _unpair_cols(d_conv_b_p)
    d_conv_w = _unpair_cols(d_conv_w_p)
    n_rep = N_MOD * D_MODEL + LANES + 4 * D_MODEL + 2 * D_FF
    small = jnp.concatenate(
        [d_sh1, d_sc1, d_gt1, d_sh2, d_sc2, d_gt2, d_b_forget, d_ln1_g, d_ln1_b, d_ln2_g, d_ln2_b, d_conv_b,
         d_conv_w.reshape(1, 6 * D_FF)], axis=1)
    n_small = small.shape[1] // LANES
    small = jnp.pad(small.reshape(n_small, LANES), ((0, 264 - n_small), (0, 0)))
    (small_all,) = _exchange("ag_small_grads", [small], scatter=False)
    rep8 = small_all[:, :n_rep // LANES, :]
    cw8 = small_all[:, n_rep // LANES:n_small, :].reshape(N_DEV, 3, 2 * D_FF)
    cw8 = lax.dynamic_slice(cw8, (0, 0, me * UP_SHARD), (N_DEV, 3, UP_SHARD))
    dmod8 = small_all[:, :N_MOD * D_MODEL // LANES, :].reshape(N_DEV, N_MOD * D_MODEL)
    dmod_loc = lax.dynamic_slice(dmod8, (0, me * ADA_SHARD), (N_DEV, ADA_SHARD))

    def pack_rep(b_a, b_f, g1, b1, g2, b2, cb):
        flat = jnp.concatenate([b_a, jnp.pad(b_f, (0, LANES - N_HEADS)), g1, b1, g2, b2, cb])
        return flat.reshape(n_rep // LANES, LANES)

    rep = _adamw("adamw_small", rep8, pack_rep(b_ada, b_forget, ln1_g, ln1_b, ln2_g, ln2_b, conv_b),
                 pack_rep(m_b_ada, m_b_forget, m_ln1_g, m_ln1_b, m_ln2_g, m_ln2_b, m_conv_b),
                 pack_rep(v_b_ada, v_b_forget, v_ln1_g, v_ln1_b, v_ln2_g, v_ln2_b, v_conv_b))

    def unpack_rep(p):
        flat = p.reshape(-1)
        o = N_MOD * D_MODEL
        return {"b_ada": flat[:o], "b_forget": flat[o:o + N_HEADS],
                "ln1_g": flat[o + 128:o + 1152], "ln1_b": flat[o + 1152:o + 2176],
                "ln2_g": flat[o + 2176:o + 3200], "ln2_b": flat[o + 3200:o + 4224], "conv_b": flat[o + 4224:]}

    rep = [unpack_rep(p) for p in rep]
    r_conv_w = _adamw("adamw_conv_w", cw8, conv_w, m_conv_w, v_conv_w)
    r_ada = _adamw_ada(jnp.transpose(c_all.reshape(N_DEV, D_MODEL)), dmod_loc, w_ada, m_w_ada, v_w_ada)

    r_in = _adamw("adamw_w_in", in_slots, w_in, m_w_in, v_w_in)
    r_fox = _adamw("adamw_w_fox", early_slots[0], w_fox_proj, m_w_fox_proj, v_w_fox_proj)
    r_sb = _adamw("adamw_w_sb", early_slots[1], w_sb_proj, m_w_sb_proj, v_w_sb_proj)
    r_o = _adamw("adamw_w_o", early_slots[2], w_o, m_w_o, v_w_o)
    r_up = _adamw("adamw_w_up", early_slots[3], w_up, m_w_up, v_w_up)
    r_down = _adamw("adamw_w_down", early_slots[4], w_down, m_w_down, v_w_down)

    def leaf(i):
        return [r_ada[i], rep[i]["b_ada"], r_in[i], rep[i]["b_forget"], r_fox[i], r_sb[i], r_o[i], rep[i]["ln1_g"],
                rep[i]["ln1_b"], r_up[i], r_conv_w[i], rep[i]["conv_b"], r_down[i], rep[i]["ln2_g"], rep[i]["ln2_b"]]

    return (loss, grad_x.reshape(1, s, D_MODEL), *leaf(0), *leaf(1), *leaf(2), *leaf(3))
```

```python
import functools

import jax
import jax.numpy as jnp
from jax import lax
from jax.experimental import pallas as pl
from jax.experimental.pallas import tpu as pltpu

F32 = jnp.float32
BF16 = jnp.bfloat16
MESH = pl.DeviceIdType.MESH
ANY = pl.BlockSpec(memory_space=pl.ANY)

N_DEV = 8
D_MODEL = 1024
HEAD_DIM = 64
N_HEADS = 8
ATTN_W = N_HEADS * HEAD_DIM
D_FF = 2816
FF_HALF = D_FF // 2
N_MOD = 6
ADA_SHARD = N_MOD * D_MODEL // N_DEV
IN_SHARD = 641
UP_SHARD = 704
ATTN_SCALE = HEAD_DIM ** -0.5
ALPHA = 2.0 ** 0.25
LN_EPS = 1e-5
LANES = 128
TQ = 512
SCAN_W = 256
VMEM_LIMIT = 56 * 1024 * 1024

ADAM_LR, ADAM_B1, ADAM_B2, ADAM_EPS, ADAM_WD, ADAM_STEP = 0.001, 0.9, 0.999, 1e-08, 0.01, 10

W_QKV, W_GATES, W_F = 3072, 2048, 128
W_PROJ = 5376


def _params(sem=None):
    return pltpu.CompilerParams(dimension_semantics=sem, vmem_limit_bytes=VMEM_LIMIT)


def _tile(n, cap):
    if n <= cap:
        return n
    best = None
    for t in range(LANES, cap + 1, LANES):
        if n % t == 0:
            best = t
    assert best is not None, (n, cap)
    return best


def _row_tile(r, width, budget=192 * 1024):
    if r * width <= budget or r % 16:
        return r
    best = 16
    for t in range(16, r + 1, 16):
        if r % t == 0 and t * width <= budget:
            best = t
    return best


def _me():
    x, y, c = lax.axis_index("x"), lax.axis_index("y"), lax.axis_index("c")
    return x, y, c, 4 * x + 2 * y + c


def _peer(r):
    x, y, c, _ = _me()
    px = 1 - x if r & 4 else x
    py = 1 - y if r & 2 else y
    pc = 1 - c if r & 1 else c
    return (px, py, pc), 4 * px + 2 * py + pc


class _Ride:
    def __init__(self, arrays, scatter):
        self.arrays, self.scatter, self.n = list(arrays), scatter, len(arrays)
        self.in_specs = [ANY] * self.n
        self.out_specs = [ANY] * self.n
        self.out_shape = [jax.ShapeDtypeStruct(a.shape if scatter else (N_DEV,) + a.shape, a.dtype) for a in arrays]
        self.scratch = [pltpu.SemaphoreType.DMA((self.n, N_DEV - 1)), pltpu.SemaphoreType.DMA((self.n, N_DEV - 1)),
                        pltpu.SemaphoreType.DMA((self.n,))]

    def _local(self, ins, outs, sems, a):
        me = _me()[3]
        return pltpu.make_async_copy(ins[a].at[me] if self.scatter else ins[a], outs[a].at[me], sems[2].at[a])

    def _remote(self, ins, outs, sems, a, r, arriving):
        me = _me()[3]
        peer, pidx = _peer(r)
        src = ins[a].at[me if arriving else pidx] if self.scatter else ins[a]
        return pltpu.make_async_remote_copy(
            src_ref=src, dst_ref=outs[a].at[pidx if arriving else me], send_sem=sems[0].at[a, r - 1],
            recv_sem=sems[1].at[a, r - 1], device_id=peer, device_id_type=MESH)

    def start(self, ins, outs, sems):
        for a in range(self.n):
            self._local(ins, outs, sems, a).start()
        for r in range(1, N_DEV):
            for a in range(self.n):
                self._remote(ins, outs, sems, a, r, False).start()

    def wait(self, ins, outs, sems):
        for r in range(1, N_DEV):
            for a in range(self.n):
                self._remote(ins, outs, sems, a, r, True).wait_recv()
        for r in range(1, N_DEV):
            for a in range(self.n):
                self._remote(ins, outs, sems, a, r, False).wait_send()
        for a in range(self.n):
            self._local(ins, outs, sems, a).wait()


def _exchange(name, arrays, scatter):
    ride = _Ride(arrays, scatter)

    def body(*refs):
        ins, outs, sems = refs[:ride.n], refs[ride.n:2 * ride.n], refs[2 * ride.n:]
        ride.start(ins, outs, sems)
        ride.wait(ins, outs, sems)

    return pl.pallas_call(body, name=name, in_specs=ride.in_specs, out_specs=ride.out_specs, out_shape=ride.out_shape,
                          scratch_shapes=ride.scratch)(*arrays)


def _with_ride(body, ride, n_in, n_out, grid):
    if ride is None:
        return body
    n = ride.n

    def wrapped(*refs):
        ins, rins = refs[:n_in], refs[n_in:n_in + n]
        outs, routs = refs[n_in + n:n_in + n + n_out], refs[n_in + n + n_out:n_in + 2 * n + n_out]
        rest = refs[n_in + 2 * n + n_out:]
        scratch, sems = rest[:len(rest) - 3], rest[len(rest) - 3:]
        ids = [pl.program_id(d) for d in range(len(grid))]
        first = functools.reduce(lambda p, q: p & q, [i == 0 for i in ids])
        last = functools.reduce(lambda p, q: p & q, [i == g - 1 for i, g in zip(ids, grid)])

        @pl.when(first)
        def _():
            ride.start(rins, routs, sems)

        body(*ins, *outs, *scratch)

        @pl.when(last)
        def _():
            ride.wait(rins, routs, sems)

    return wrapped


def _ride_call(body, ride, *, name, grid, in_specs, out_specs, out_shape, scratch_shapes, sem, args):
    n_in, n_out = len(in_specs), len(out_specs)
    if ride is None:
        res = pl.pallas_call(body, name=name, grid=grid, in_specs=in_specs, out_specs=out_specs, out_shape=out_shape,
                             scratch_shapes=scratch_shapes, compiler_params=_params(sem))(*args)
        return list(res), []
    res = pl.pallas_call(
        _with_ride(body, ride, n_in, n_out, grid), name=name, grid=grid,
        in_specs=list(in_specs) + ride.in_specs, out_specs=list(out_specs) + ride.out_specs,
        out_shape=list(out_shape) + ride.out_shape, scratch_shapes=list(scratch_shapes) + ride.scratch,
        compiler_params=_params(("arbitrary",) * len(grid)))(*args, *ride.arrays)
    return list(res[:n_out]), list(res[n_out:])


def _mm(a, b, *, name, ta=False, tb=False, out_dtype=F32, tm=1024, tn=1024, tk=1024, ride=None):
    m, k = (a.shape[1], a.shape[0]) if ta else a.shape
    n = b.shape[0] if tb else b.shape[1]
    assert (b.shape[1] if tb else b.shape[0]) == k
    tm, tn, tk = _tile(m, tm), _tile(n, tn), _tile(k, tk)
    nk = k // tk
    a_spec = pl.BlockSpec((tk, tm), lambda i, j, l: (l, i)) if ta else pl.BlockSpec((tm, tk), lambda i, j, l: (i, l))
    b_spec = pl.BlockSpec((tn, tk), lambda i, j, l: (j, l)) if tb else pl.BlockSpec((tk, tn), lambda i, j, l: (l, j))
    dims = (((0,) if ta else (1,), (1,) if tb else (0,)), ((), ()))

    def body(a_ref, b_ref, o_ref, *acc):
        p = lax.dot_general(a_ref[...].astype(BF16), b_ref[...].astype(BF16), dims, preferred_element_type=F32)
        if nk == 1:
            o_ref[...] = p.astype(out_dtype)
            return
        acc_ref = acc[0]
        step = pl.program_id(2)

        @pl.when(step == 0)
        def _():
            acc_ref[...] = p

        @pl.when(step > 0)
        def _():
            acc_ref[...] += p

        @pl.when(step == nk - 1)
        def _():
            o_ref[...] = acc_ref[...].astype(out_dtype)

    outs, rode = _ride_call(
        body, ride, name=name, grid=(m // tm, n // tn, nk), in_specs=[a_spec, b_spec],
        out_specs=[pl.BlockSpec((tm, tn), lambda i, j, l: (i, j))], out_shape=[jax.ShapeDtypeStruct((m, n), out_dtype)],
        scratch_shapes=[] if nk == 1 else [pltpu.VMEM((tm, tn), F32)], sem=("parallel", "parallel", "arbitrary"),
        args=(a, b))
    return outs[0] if ride is None else (outs[0], rode)


def _rowwise(name, fn, rows, vecs, outs, sums=(), tr=256):
    s = rows[0][0].shape[0]
    tr = min(tr, s)
    nr, nv, no = len(rows), len(vecs), len(outs)

    def body(*refs):
        vals = [r[...] for r in refs[:nr + nv]]
        res = fn(*vals)
        for o_ref, val in zip(refs[nr + nv:nr + nv + no], res[:no]):
            o_ref[...] = val.astype(o_ref.dtype)
        step = pl.program_id(0)
        for s_ref, val in zip(refs[nr + nv + no:], res[no:]):
            @pl.when(step == 0)
            def _(s_ref=s_ref, val=val):
                s_ref[...] = val

            @pl.when(step > 0)
            def _(s_ref=s_ref, val=val):
                s_ref[...] += val

    in_specs = [pl.BlockSpec((tr, w), functools.partial(lambda i, cb: (i, cb), cb=cb)) for _, w, cb in rows]
    in_specs += [pl.BlockSpec(v.shape, lambda i: (0, 0)) for v in vecs]
    out_specs = [pl.BlockSpec((tr, w), lambda i: (i, 0)) for w, _ in outs]
    out_specs += [pl.BlockSpec((1, w), lambda i: (0, 0)) for w in sums]
    out_shape = [jax.ShapeDtypeStruct((s, w), dt) for w, dt in outs]
    out_shape += [jax.ShapeDtypeStruct((1, w), F32) for w in sums]
    return pl.pallas_call(
        body, name=name, grid=(s // tr,), in_specs=in_specs, out_specs=out_specs, out_shape=out_shape,
        compiler_params=_params(("arbitrary",) if sums else ("parallel",)),
    )(*[r[0] for r in rows], *vecs)


def _colsum(x):
    return jnp.sum(x, axis=0, keepdims=True)


def _sigmoid(x):
    return 1.0 / (1.0 + jnp.exp(-x))


def _log_sigmoid(x):
    return jnp.minimum(x, 0.0) - jnp.log(1.0 + jnp.exp(-jnp.abs(x)))


def _ln_stats(r):
    mu = jnp.mean(r, axis=-1, keepdims=True)
    xc = r - mu
    var = jnp.mean(xc * xc, axis=-1, keepdims=True)
    rstd = lax.rsqrt(var + LN_EPS)
    return xc * rstd, rstd


def _ln_bwd(dy, xhat, rstd, g):
    dxh = dy * g
    m1 = jnp.mean(dxh, axis=-1, keepdims=True)
    m2 = jnp.mean(dxh * xhat, axis=-1, keepdims=True)
    return rstd * (dxh - m1 - xhat * m2)


def _mod_exchange(c_row, w_ada, b_ada_loc):
    def body(c_ref, w_ref, b_ref, call_ref, mod_ref, piece_ref, send_sems, recv_sems):
        me = _me()[3]
        call_ref[me] = c_ref[...]
        sent = []
        for r in range(1, N_DEV):
            peer, _ = _peer(r)
            cp = pltpu.make_async_remote_copy(
                src_ref=c_ref, dst_ref=call_ref.at[me], send_sem=send_sems.at[0, r - 1],
                recv_sem=recv_sems.at[0, r - 1], device_id=peer, device_id_type=MESH)
            cp.start()
            sent.append(cp)
        for r in range(1, N_DEV):
            peer, pidx = _peer(r)
            pltpu.make_async_remote_copy(
                src_ref=c_ref, dst_ref=call_ref.at[pidx], send_sem=send_sems.at[0, r - 1],
                recv_sem=recv_sems.at[0, r - 1], device_id=peer, device_id_type=MESH).wait_recv()
        c_all = jnp.concatenate([call_ref[d] for d in range(N_DEV)], axis=0)
        mod_loc = jnp.dot(c_all, w_ref[...], preferred_element_type=F32,
                          precision=lax.Precision.HIGHEST) + b_ref[...]
        for d in range(N_DEV):
            piece_ref[d] = mod_loc[d:d + 1, :]
        mod_ref[me] = piece_ref[me]
        for r in range(1, N_DEV):
            peer, pidx = _peer(r)
            cp = pltpu.make_async_remote_copy(
                src_ref=piece_ref.at[pidx], dst_ref=mod_ref.at[me], send_sem=send_sems.at[1, r - 1],
                recv_sem=recv_sems.at[1, r - 1], device_id=peer, device_id_type=MESH)
            cp.start()
            sent.append(cp)
        for r in range(1, N_DEV):
            peer, pidx = _peer(r)
            pltpu.make_async_remote_copy(
                src_ref=piece_ref.at[me], dst_ref=mod_ref.at[pidx], send_sem=send_sems.at[1, r - 1],
                recv_sem=recv_sems.at[1, r - 1], device_id=peer, device_id_type=MESH).wait_recv()
        for cp in sent:
            cp.wait_send()

    vmem = pl.BlockSpec(memory_space=pltpu.VMEM)
    return pl.pallas_call(
        body, name="mod_exchange", in_specs=[vmem, vmem, vmem], out_specs=[vmem, vmem],
        out_shape=[jax.ShapeDtypeStruct((N_DEV, 1, D_MODEL), F32), jax.ShapeDtypeStruct((N_DEV, 1, ADA_SHARD), F32)],
        scratch_shapes=[pltpu.VMEM((N_DEV, 1, ADA_SHARD), F32),
                        pltpu.SemaphoreType.DMA((2, N_DEV - 1)), pltpu.SemaphoreType.DMA((2, N_DEV - 1))],
        compiler_params=_params(),
    )(c_row, w_ada, b_ada_loc)


def _split3(x):
    hi = x.astype(BF16)
    r1 = x - hi.astype(F32)
    mid = r1.astype(BF16)
    lo = (r1 - mid.astype(F32)).astype(BF16)
    return hi, mid, lo


def _scan_rows(x_ref, o_ref, s, reverse, pre=None, post=None):
    tb = min(TQ, s)
    nb = s // tb
    row = lax.broadcasted_iota(jnp.int32, (tb, tb), 0)
    col = lax.broadcasted_iota(jnp.int32, (tb, tb), 1)
    tri = jnp.where((col >= row) if reverse else (col <= row), 1.0, 0.0).astype(BF16)

    def step(i, carry):
        blk = (nb - 1 - i) if reverse else i
        off = pl.multiple_of(blk * tb, tb)
        x = x_ref[pl.ds(off, tb), :]
        if pre is not None:
            x = pre(x, off)
        acc = carry
        for piece in _split3(x):
            acc = acc + jnp.dot(tri, piece, preferred_element_type=F32)
        o_ref[pl.ds(off, tb), :] = acc if post is None else post(acc, off)
        edge = acc[0:1, :] if reverse else acc[tb - 1:tb, :]
        return jnp.broadcast_to(edge, (tb, LANES))

    lax.fori_loop(0, nb, step, jnp.zeros((tb, LANES), F32))


def _forget_cumsum(f_raw, b_pad):
    s = f_raw.shape[0]

    def body(f_ref, b_ref, cum_ref):
        b = b_ref[...]
        _scan_rows(f_ref, cum_ref, s, False, pre=lambda x, off: _log_sigmoid(x + b))

    vmem = pl.BlockSpec(memory_space=pltpu.VMEM)
    return pl.pallas_call(body, name="forget_cumsum", in_specs=[vmem, vmem], out_specs=vmem,
                          out_shape=jax.ShapeDtypeStruct((s, LANES), F32), compiler_params=_params())(f_raw, b_pad)


def _forget_bwd(dcum, f_raw, b_pad):
    s = f_raw.shape[0]

    def body(d_ref, f_ref, b_ref, df_ref, db_ref, tmp_ref):
        b = b_ref[...]
        _scan_rows(d_ref, tmp_ref, s, True)
        df = tmp_ref[...] * _sigmoid(-(f_ref[...] + b))
        df_ref[...] = df.astype(BF16)
        db_ref[...] = _colsum(df)

    vmem = pl.BlockSpec(memory_space=pltpu.VMEM)
    return pl.pallas_call(
        body, name="forget_bwd", in_specs=[vmem, vmem, vmem], out_specs=[vmem, vmem],
        out_shape=[jax.ShapeDtypeStruct((s, LANES), BF16), jax.ShapeDtypeStruct((1, LANES), F32)],
        scratch_shapes=[pltpu.VMEM((s, LANES), F32)], compiler_params=_params())(dcum, f_raw, b_pad)


def _dot_nt(a, b):
    return lax.dot_general(a, b, (((1,), (1,)), ((), ())), preferred_element_type=F32)


def _dot_tn(a, b):
    return lax.dot_general(a, b, (((0,), (0,)), ((), ())), preferred_element_type=F32)


def _head_masks():
    lane = lax.broadcasted_iota(jnp.int32, (TQ, LANES), 1)
    return lane, [lane < HEAD_DIM, lane >= HEAD_DIM]


def _pick(mask, x):
    return jnp.where(mask, x, jnp.zeros_like(x))


def _qkv_specs(s, col0):
    nb = ATTN_W // LANES
    return [pl.BlockSpec((TQ, LANES), lambda hp, qi: (qi, col0 + hp)),
            pl.BlockSpec((s, LANES), lambda hp, qi: (0, col0 + nb + hp)),
            pl.BlockSpec((s, LANES), lambda hp, qi: (0, col0 + 2 * nb + hp))]


def _pair_spec():
    return pl.BlockSpec((TQ, LANES), lambda hp, qi: (qi, hp))


def _fox_fwd(qkv, cum_col, cum_row, ride=None):
    s = qkv.shape[0]
    nq = s // TQ

    def body(q_ref, k_ref, v_ref, cc_ref, cr_ref, o_ref, o32_ref, lse_ref):
        hp, qi = pl.program_id(0), pl.program_id(1)
        lane, masks = _head_masks()
        row = lax.broadcasted_iota(jnp.int32, (TQ, TQ), 0)
        col = lax.broadcasted_iota(jnp.int32, (TQ, TQ), 1)
        causal = col <= row
        q2 = q_ref[...] * jnp.asarray(ATTN_SCALE, BF16)
        cc = cc_ref[...]
        qms = [_pick(masks[e], q2) for e in range(2)]
        cqs = [jnp.sum(jnp.where(lane == 2 * hp + e, cc, 0.0), axis=1, keepdims=True) for e in range(2)]

        def tile(kb, carry, masked):
            off = pl.multiple_of(kb * TQ, TQ)
            k2, v2 = k_ref[pl.ds(off, TQ), :], v_ref[pl.ds(off, TQ), :]
            new = []
            for e in range(2):
                m, l, acc = carry[e]
                sc = _dot_nt(qms[e], k2) + (cqs[e] - cr_ref[e:e + 1, pl.ds(off, TQ)])
                if masked:
                    sc = jnp.where(causal, sc, -jnp.inf)
                m_new = jnp.maximum(m, jnp.max(sc, axis=1, keepdims=True))
                p = jnp.exp(sc - m_new)
                corr = jnp.exp(m - m_new)
                l = corr * l + jnp.sum(p, axis=1, keepdims=True)
                acc = corr * acc + jnp.dot(p.astype(BF16), v2, preferred_element_type=F32)
                new.append((m_new, l, acc))
            return tuple(new)

        init = (jnp.full((TQ, 1), -jnp.inf, F32), jnp.zeros((TQ, 1), F32), jnp.zeros((TQ, LANES), F32))
        carry = lax.fori_loop(0, qi, lambda kb, cr: tile(kb, cr, False), tile(qi, (init, init), True))
        outs = [acc / l for _, l, acc in carry]
        lses = [m + jnp.log(l) for m, l, _ in carry]
        out = jnp.where(masks[0], outs[0], outs[1])
        o_ref[...] = out.astype(BF16)
        o32_ref[...] = out
        lse_ref[...] = jnp.where(masks[0], lses[0], lses[1])

    return _ride_call(
        body, ride, name="fox_fwd", grid=(N_HEADS // 2, nq),
        in_specs=_qkv_specs(s, 0) + [pl.BlockSpec((TQ, LANES), lambda hp, qi: (qi, 0)),
                                     pl.BlockSpec((None, 2, s), lambda hp, qi: (hp, 0, 0))],
        out_specs=[_pair_spec(), _pair_spec(), _pair_spec()],
        out_shape=[jax.ShapeDtypeStruct((s, ATTN_W), BF16), jax.ShapeDtypeStruct((s, ATTN_W), F32),
                   jax.ShapeDtypeStruct((s, ATTN_W), F32)],
        scratch_shapes=[], sem=("parallel", "parallel"), args=(qkv, qkv, qkv, cum_col, cum_row))


def _fox_bwd(qkv, cum_col, cum_row, o, lse, do, ride=None):
    s = qkv.shape[0]
    nq = s // TQ

    def body(q_ref, k_ref, v_ref, cc_ref, cr_ref, o_ref, lse_ref, do_ref,
             dq_ref, dk_ref, dv_ref, dcr_ref, dcq_ref, dk_acc, dv_acc):
        hp, qi = pl.program_id(0), pl.program_id(1)

        @pl.when(qi == 0)
        def _():
            dk_acc[...] = jnp.zeros_like(dk_acc)
            dv_acc[...] = jnp.zeros_like(dv_acc)
            dcr_ref[...] = jnp.zeros_like(dcr_ref)

        lane, masks = _head_masks()
        row = lax.broadcasted_iota(jnp.int32, (TQ, TQ), 0)
        col = lax.broadcasted_iota(jnp.int32, (TQ, TQ), 1)
        causal = col <= row
        q2 = q_ref[...] * jnp.asarray(ATTN_SCALE, BF16)
        do2 = do_ref[...]
        prod = do2.astype(F32) * o_ref[...].astype(F32)
        lse2 = lse_ref[...]
        cc = cc_ref[...]
        qms = [_pick(masks[e], q2) for e in range(2)]
        doms = [_pick(masks[e], do2) for e in range(2)]
        deltas = [jnp.sum(jnp.where(masks[e], prod, 0.0), axis=1, keepdims=True) for e in range(2)]
        lses = [jnp.max(jnp.where(masks[e], lse2, -jnp.inf), axis=1, keepdims=True) for e in range(2)]
        cqs = [jnp.sum(jnp.where(lane == 2 * hp + e, cc, 0.0), axis=1, keepdims=True) for e in range(2)]

        def tile(kb, carry, masked):
            off = pl.multiple_of(kb * TQ, TQ)
            k2, v2 = k_ref[pl.ds(off, TQ), :], v_ref[pl.ds(off, TQ), :]
            new, dk, dv = [], None, None
            for e in range(2):
                dq, rowsum = carry[e]
                sc = _dot_nt(qms[e], k2) + (cqs[e] - cr_ref[e:e + 1, pl.ds(off, TQ)])
                p = jnp.exp(sc - lses[e])
                if masked:
                    p = jnp.where(causal, p, 0.0)
                ds = p * (_dot_nt(doms[e], v2) - deltas[e])
                dsb = ds.astype(BF16)
                dk_e, dv_e = _dot_tn(dsb, qms[e]), _dot_tn(p.astype(BF16), doms[e])
                dk, dv = (dk_e, dv_e) if e == 0 else (dk + dk_e, dv + dv_e)
                dcr_ref[e:e + 1, pl.ds(off, TQ)] -= _colsum(ds)
                new.append((dq + jnp.dot(dsb, k2, preferred_element_type=F32),
                            rowsum + jnp.sum(ds, axis=1, keepdims=True)))
            dk_acc[pl.ds(off, TQ), :] += dk
            dv_acc[pl.ds(off, TQ), :] += dv
            return tuple(new)

        init = (jnp.zeros((TQ, LANES), F32), jnp.zeros((TQ, 1), F32))
        carry = tile(qi, lax.fori_loop(0, qi, lambda kb, cr: tile(kb, cr, False), (init, init)), True)
        dq_ref[...] = (jnp.where(masks[0], carry[0][0], carry[1][0]) * ATTN_SCALE).astype(BF16)
        dcq_ref[...] = jnp.where(masks[0], carry[0][1], carry[1][1])

        @pl.when(qi == nq - 1)
        def _():
            dk_ref[...] = dk_acc[...].astype(BF16)
            dv_ref[...] = dv_acc[...].astype(BF16)

    seq_spec = pl.BlockSpec((s, LANES), lambda hp, qi: (0, hp))
    return _ride_call(
        body, ride, name="fox_bwd", grid=(N_HEADS // 2, nq),
        in_specs=_qkv_specs(s, 0) + [pl.BlockSpec((TQ, LANES), lambda hp, qi: (qi, 0)),
                                     pl.BlockSpec((None, 2, s), lambda hp, qi: (hp, 0, 0)),
                                     _pair_spec(), _pair_spec(), _pair_spec()],
        out_specs=[_pair_spec(), seq_spec, seq_spec, pl.BlockSpec((None, 2, s), lambda hp, qi: (hp, 0, 0)),
                   _pair_spec()],
        out_shape=[jax.ShapeDtypeStruct((s, ATTN_W), BF16)] * 3 + [jax.ShapeDtypeStruct((N_HEADS // 2, 2, s), F32),
                                                                    jax.ShapeDtypeStruct((s, ATTN_W), F32)],
        scratch_shapes=[pltpu.VMEM((s, LANES), F32), pltpu.VMEM((s, LANES), F32)],
        sem=("parallel", "arbitrary"), args=(qkv, qkv, qkv, cum_col, cum_row, o, lse, do))


def _scan_matrix(reverse):
    row = lax.broadcasted_iota(jnp.int32, (SCAN_W, SCAN_W), 0)
    col = lax.broadcasted_iota(jnp.int32, (SCAN_W, SCAN_W), 1)
    return jnp.where((row > col) if reverse else (row < col), 1.0, 0.0).astype(BF16)


def _scan_cols(x, tri, reverse, init):
    nblk = x.shape[1] // SCAN_W
    parts, total = [None] * nblk, init
    far = 0 if reverse else SCAN_W - 1
    for b in (reversed(range(nblk)) if reverse else range(nblk)):
        blk = x[:, b * SCAN_W:(b + 1) * SCAN_W]
        part = jnp.dot(blk.astype(BF16), tri, preferred_element_type=F32)
        parts[b] = part + total
        total = total + (part[:, far:far + 1] + blk[:, far:far + 1])
    return (parts[0] if nblk == 1 else jnp.concatenate(parts, axis=1)), total


def _sb_logits(qm, k2):
    z = _dot_nt(qm, k2)
    neg_abs = lax.bitcast_convert_type(lax.bitcast_convert_type(z, jnp.uint32) | jnp.uint32(0x80000000), F32)
    soft = jnp.log(1.0 + jnp.exp(neg_abs))
    lb = jnp.minimum(z, 0.0) - soft
    return lb, lb - z


def _tri_base(qi):
    return (qi * (qi + 1)) // 2


def _sb_fwd(qkv):
    s = qkv.shape[0]
    nq = s // TQ

    def body(q_ref, k_ref, v_ref, o_ref, t_ref, buf, sems):
        hp, qi = pl.program_id(0), pl.program_id(1)
        _, masks = _head_masks()
        row = lax.broadcasted_iota(jnp.int32, (TQ, TQ), 0)
        col = lax.broadcasted_iota(jnp.int32, (TQ, TQ), 1)
        strict = col < row
        suffix = _scan_matrix(True)
        q2 = q_ref[...] * jnp.asarray(ATTN_SCALE, BF16)
        qms = [_pick(masks[e], q2) for e in range(2)]
        base = _tri_base(qi)

        def store(e, slot, kb):
            return pltpu.make_async_copy(buf.at[e, slot], t_ref.at[2 * hp + e, base + kb], sems.at[e, slot])

        def tile(n, carry, masked):
            kb = qi - n
            off = pl.multiple_of(kb * TQ, TQ)
            k2, v2 = k_ref[pl.ds(off, TQ), :], v_ref[pl.ds(off, TQ), :]
            slot = n % 2
            new = []
            for e in range(2):
                run, acc = carry[e]
                lb, lo = _sb_logits(qms[e], k2)
                if masked:
                    lo = jnp.where(strict, lo, 0.0)
                rest, run = _scan_cols(lo, suffix, True, run)
                a = jnp.exp(lb + rest)
                if masked:
                    a = jnp.where(strict, a, 0.0)
                ab = a.astype(BF16)
                acc = acc + jnp.dot(ab, v2, preferred_element_type=F32)
                if not masked:
                    @pl.when(n >= 2)
                    def _(e=e, slot=slot, kb=kb):
                        store(e, slot, kb + 2).wait()
                buf[e, slot, 0] = ab
                buf[e, slot, 1] = lb.astype(BF16)
                store(e, slot, kb).start()
                new.append((run, acc))
            return tuple(new)

        init = (jnp.zeros((TQ, 1), F32), jnp.zeros((TQ, LANES), F32))
        carry = lax.fori_loop(0, qi, lambda it, cr: tile(it + 1, cr, False), tile(0, (init, init), True))
        for e in range(2):
            store(e, qi % 2, 0).wait()

            @pl.when(qi >= 1)
            def _(e=e):
                store(e, (qi - 1) % 2, 1).wait()
        o_ref[...] = jnp.where(masks[0], carry[0][1], carry[1][1]).astype(BF16)

    ntri = nq * (nq + 1) // 2
    return pl.pallas_call(
        body, name="sb_fwd", grid=(N_HEADS // 2, nq), in_specs=_qkv_specs(s, 3 * ATTN_W // LANES),
        out_specs=[_pair_spec(), ANY],
        out_shape=[jax.ShapeDtypeStruct((s, ATTN_W), BF16), jax.ShapeDtypeStruct((N_HEADS, ntri, 2, TQ, TQ), BF16)],
        scratch_shapes=[pltpu.VMEM((2, 2, 2, TQ, TQ), BF16), pltpu.SemaphoreType.DMA((2, 2))],
        compiler_params=_params(("arbitrary", "arbitrary")),
    )(qkv, qkv, qkv)


def _sb_bwd(qkv, tiles, do):
    s = qkv.shape[0]
    nq = s // TQ

    def body(q_ref, k_ref, v_ref, t_ref, do_ref, dq_ref, dk_ref, dv_ref, dk_acc, dv_acc, buf, sems):
        hp, qi = pl.program_id(0), pl.program_id(1)

        @pl.when(qi == 0)
        def _():
            dk_acc[...] = jnp.zeros_like(dk_acc)
            dv_acc[...] = jnp.zeros_like(dv_acc)

        _, masks = _head_masks()
        row = lax.broadcasted_iota(jnp.int32, (TQ, TQ), 0)
        col = lax.broadcasted_iota(jnp.int32, (TQ, TQ), 1)
        strict = col < row
        prefix = _scan_matrix(False)
        q2 = q_ref[...] * jnp.asarray(ATTN_SCALE, BF16)
        do2 = do_ref[...]
        qms = [_pick(masks[e], q2) for e in range(2)]
        doms = [_pick(masks[e], do2) for e in range(2)]
        base = _tri_base(qi)

        def fetch(e, kb):
            return pltpu.make_async_copy(t_ref.at[2 * hp + e, base + kb], buf.at[e, kb % 2], sems.at[e, kb % 2])

        for e in range(2):
            fetch(e, 0).start()

        def tile(kb, carry, masked):
            off = pl.multiple_of(kb * TQ, TQ)
            k2, v2 = k_ref[pl.ds(off, TQ), :], v_ref[pl.ds(off, TQ), :]
            new, dk, dv = [], None, None
            for e in range(2):
                gsum, dq = carry[e]
                if not masked:
                    fetch(e, kb + 1).start()
                fetch(e, kb).wait()
                ab = buf[e, kb % 2, 0]
                beta = jnp.exp(buf[e, kb % 2, 1].astype(F32))
                g = ab.astype(F32) * _dot_nt(doms[e], v2)
                before, gsum = _scan_cols(g, prefix, False, gsum)
                dz = g - beta * (g + before)
                if masked:
                    dz = jnp.where(strict, dz, 0.0)
                dzb = dz.astype(BF16)
                dk_e, dv_e = _dot_tn(dzb, qms[e]), _dot_tn(ab, doms[e])
                dk, dv = (dk_e, dv_e) if e == 0 else (dk + dk_e, dv + dv_e)
                new.append((gsum, dq + jnp.dot(dzb, k2, preferred_element_type=F32)))
            dk_acc[pl.ds(off, TQ), :] += dk
            dv_acc[pl.ds(off, TQ), :] += dv
            return tuple(new)

        init = (jnp.zeros((TQ, 1), F32), jnp.zeros((TQ, LANES), F32))
        carry = tile(qi, lax.fori_loop(0, qi, lambda kb, cr: tile(kb, cr, False), (init, init)), True)
        dq_ref[...] = (jnp.where(masks[0], carry[0][1], carry[1][1]) * ATTN_SCALE).astype(BF16)

        @pl.when(qi == nq - 1)
        def _():
            dk_ref[...] = dk_acc[...].astype(BF16)
            dv_ref[...] = dv_acc[...].astype(BF16)

    seq_spec = pl.BlockSpec((s, LANES), lambda hp, qi: (0, hp))
    return pl.pallas_call(
        body, name="sb_bwd", grid=(N_HEADS // 2, nq),
        in_specs=_qkv_specs(s, 3 * ATTN_W // LANES) + [ANY, _pair_spec()],
        out_specs=[_pair_spec(), seq_spec, seq_spec],
        out_shape=[jax.ShapeDtypeStruct((s, ATTN_W), BF16)] * 3,
        scratch_shapes=[pltpu.VMEM((s, LANES), F32), pltpu.VMEM((s, LANES), F32),
                        pltpu.VMEM((2, 2, 2, TQ, TQ), BF16), pltpu.SemaphoreType.DMA((2, 2))],
        compiler_params=_params(("arbitrary", "arbitrary")),
    )(qkv, qkv, qkv, tiles, do)


CONV_TR = 256


def _shift_down(x, halo, n):
    rolled = pltpu.roll(x, n, 0)
    rid = lax.broadcasted_iota(jnp.int32, x.shape, 0)
    for j in range(n):
        rolled = jnp.where(rid == j, halo[8 - n + j:8 - n + j + 1, :], rolled)
    return rolled


def _shift_up(x, halo, n):
    rows = x.shape[0]
    rolled = pltpu.roll(x, rows - n, 0)
    rid = lax.broadcasted_iota(jnp.int32, x.shape, 0)
    for j in range(n):
        rolled = jnp.where(rid == rows - n + j, halo[j:j + 1, :], rolled)
    return rolled


def _conv_fwd_block(x, halo, w, b):
    return b + _shift_down(x, halo, 2) * w[0:1, :] + _shift_down(x, halo, 1) * w[1:2, :] + x * w[2:3, :]


def _conv_specs(tr, s):
    pair = 2 * FF_HALF
    blk = pl.BlockSpec((tr, pair), lambda j, i: (i, j))
    prev = pl.BlockSpec((8, pair), lambda j, i: (jnp.maximum(i * (tr // 8) - 1, 0), j))
    nxt = pl.BlockSpec((8, pair), lambda j, i: (jnp.minimum((i + 1) * (tr // 8), s // 8 - 1), j))
    return blk, prev, nxt


def _conv_gate_fwd(hpre, conv_w, conv_b):
    s = hpre.shape[0]
    tr = min(CONV_TR, s)
    blk, prev, _ = _conv_specs(tr, s)

    def body(x_ref, halo_ref, w_ref, b_ref, a_ref):
        i = pl.program_id(1)
        halo = jnp.where(i > 0, halo_ref[...], 0.0)
        h = _conv_fwd_block(x_ref[...], halo, w_ref[...], b_ref[...])
        hg, hv = h[:, :FF_HALF], h[:, FF_HALF:]
        a_ref[...] = (hg * _sigmoid(hg) * hv).astype(BF16)

    return pl.pallas_call(
        body, name="conv_gate_fwd", grid=(2, s // tr),
        in_specs=[blk, prev, pl.BlockSpec((3, 2 * FF_HALF), lambda j, i: (0, j)),
                  pl.BlockSpec((1, 2 * FF_HALF), lambda j, i: (0, j))],
        out_specs=pl.BlockSpec((tr, FF_HALF), lambda j, i: (i, j)),
        out_shape=jax.ShapeDtypeStruct((s, D_FF), BF16),
        compiler_params=_params(("parallel", "parallel")),
    )(hpre, hpre, conv_w, conv_b)


def _conv_gate_bwd(hpre, da, conv_w, conv_b):
    s = hpre.shape[0]
    tr = min(CONV_TR, s)
    blk, prev, _ = _conv_specs(tr, s)

    def body(x_ref, halo_ref, da_ref, w_ref, b_ref, dh_ref, db_ref, dw_ref):
        i = pl.program_id(1)
        halo = jnp.where(i > 0, halo_ref[...], 0.0)
        x = x_ref[...]
        h = _conv_fwd_block(x, halo, w_ref[...], b_ref[...])
        hg, hv = h[:, :FF_HALF], h[:, FF_HALF:]
        da_blk = da_ref[...].astype(F32)
        sg = _sigmoid(hg)
        dhg = da_blk * hv * (sg * (1.0 + hg * (1.0 - sg)))
        dhv = da_blk * (hg * sg)
        dh_ref[:, :FF_HALF] = dhg.astype(BF16)
        dh_ref[:, FF_HALF:] = dhv.astype(BF16)
        x2, x1 = _shift_down(x, halo, 2), _shift_down(x, halo, 1)
        parts = []
        for lo, dpart in ((0, dhg), (FF_HALF, dhv)):
            cols = slice(lo, lo + FF_HALF)
            parts.append((cols, _colsum(dpart), _colsum(dpart * x2[:, cols]), _colsum(dpart * x1[:, cols]),
                          _colsum(dpart * x[:, cols])))

        @pl.when(i == 0)
        def _():
            for cols, db, dw0, dw1, dw2 in parts:
                db_ref[:, cols] = db
                dw_ref[0:1, cols] = dw0
                dw_ref[1:2, cols] = dw1
                dw_ref[2:3, cols] = dw2

        @pl.when(i > 0)
        def _():
            for cols, db, dw0, dw1, dw2 in parts:
                db_ref[:, cols] += db
                dw_ref[0:1, cols] += dw0
                dw_ref[1:2, cols] += dw1
                dw_ref[2:3, cols] += dw2

    pair = 2 * FF_HALF
    return pl.pallas_call(
        body, name="conv_gate_bwd", grid=(2, s // tr),
        in_specs=[blk, prev, pl.BlockSpec((tr, FF_HALF), lambda j, i: (i, j)),
                  pl.BlockSpec((3, pair), lambda j, i: (0, j)), pl.BlockSpec((1, pair), lambda j, i: (0, j))],
        out_specs=[blk, pl.BlockSpec((1, pair), lambda j, i: (0, j)), pl.BlockSpec((3, pair), lambda j, i: (0, j))],
        out_shape=[jax.ShapeDtypeStruct((s, 2 * D_FF), BF16), jax.ShapeDtypeStruct((1, 2 * D_FF), F32),
                   jax.ShapeDtypeStruct((3, 2 * D_FF), F32)],
        compiler_params=_params(("parallel", "arbitrary")),
    )(hpre, hpre, da, conv_w, conv_b)


def _conv_input_bwd(dh, conv_w):
    s = dh.shape[0]
    tr = min(CONV_TR, s)
    blk, _, _ = _conv_specs(tr, s)
    nblk = s // tr

    def body(x_ref, halo_ref, w_ref, o_ref):
        i = pl.program_id(1)
        halo = jnp.where(i < nblk - 1, halo_ref[...].astype(F32), 0.0)
        x, w = x_ref[...].astype(F32), w_ref[...]
        o_ref[...] = (x * w[2:3, :] + _shift_up(x, halo, 1) * w[1:2, :] + _shift_up(x, halo, 2) * w[0:1, :]).astype(BF16)

    nxt = pl.BlockSpec((16, 2 * FF_HALF), lambda j, i: (jnp.minimum((i + 1) * (tr // 16), s // 16 - 1), j))
    return pl.pallas_call(
        body, name="conv_input_bwd", grid=(2, nblk),
        in_specs=[blk, nxt, pl.BlockSpec((3, 2 * FF_HALF), lambda j, i: (0, j))], out_specs=blk,
        out_shape=jax.ShapeDtypeStruct((s, 2 * D_FF), BF16),
        compiler_params=_params(("parallel", "parallel")),
    )(dh, dh, conv_w)


def _adamw_math(w, g, m, v):
    m = ADAM_B1 * m + (1.0 - ADAM_B1) * g
    v = ADAM_B2 * v + (1.0 - ADAM_B2) * (g * g)
    m_hat = m / (1.0 - ADAM_B1 ** ADAM_STEP)
    v_hat = v / (1.0 - ADAM_B2 ** ADAM_STEP)
    delta = -ADAM_LR * (m_hat / (jnp.sqrt(v_hat) + ADAM_EPS) + ADAM_WD * w)
    return delta, m, v


def _adamw(name, g8, w, m, v):
    r, c = w.shape
    tr = _row_tile(r, c)

    def body(g_ref, w_ref, m_ref, v_ref, go_ref, d_ref, mo_ref, vo_ref):
        g = g_ref[0].astype(F32)
        for d in range(1, N_DEV):
            g = g + g_ref[d].astype(F32)
        delta, mn, vn = _adamw_math(w_ref[...], g, m_ref[...], v_ref[...])
        go_ref[...] = g
        d_ref[...] = delta
        mo_ref[...] = mn
        vo_ref[...] = vn

    spec = pl.BlockSpec((tr, c), lambda i: (i, 0))
    return pl.pallas_call(
        body, name=name, grid=(r // tr,),
        in_specs=[pl.BlockSpec((N_DEV, tr, c), lambda i: (0, i, 0)), spec, spec, spec], out_specs=[spec] * 4,
        out_shape=[jax.ShapeDtypeStruct((r, c), F32)] * 4, compiler_params=_params(("parallel",)),
    )(g8, w, m, v)


def _adamw_ada(c_t, dmod, w, m, v):
    r, c = w.shape
    tr = _row_tile(r, c)

    def body(ct_ref, dm_ref, w_ref, m_ref, v_ref, go_ref, d_ref, mo_ref, vo_ref):
        ct, dm = ct_ref[...], dm_ref[...]
        g = ct[:, 0:1] * dm[0:1, :]
        for b in range(1, N_DEV):
            g = g + ct[:, b:b + 1] * dm[b:b + 1, :]
        delta, mn, vn = _adamw_math(w_ref[...], g, m_ref[...], v_ref[...])
        go_ref[...] = g
        d_ref[...] = delta
        mo_ref[...] = mn
        vo_ref[...] = vn

    spec = pl.BlockSpec((tr, c), lambda i: (i, 0))
    return pl.pallas_call(
        body, name="adamw_w_ada", grid=(r // tr,),
        in_specs=[pl.BlockSpec((tr, N_DEV), lambda i: (i, 0)), pl.BlockSpec((N_DEV, c), lambda i: (0, 0)),
                  spec, spec, spec],
        out_specs=[spec] * 4, out_shape=[jax.ShapeDtypeStruct((r, c), F32)] * 4,
        compiler_params=_params(("parallel",)),
    )(c_t, dmod, w, m, v)


def _cols_from_slots(g):
    n, r, c = g.shape
    return jnp.transpose(g, (1, 0, 2)).reshape(r, n * c)


def _cols_to_slots(w):
    r, c = w.shape
    return jnp.transpose(w.reshape(r, N_DEV, c // N_DEV), (1, 0, 2))


def _pair_cols(w):
    g0, g1 = w[..., 0:FF_HALF], w[..., FF_HALF:D_FF]
    v0, v1 = w[..., D_FF:D_FF + FF_HALF], w[..., D_FF + FF_HALF:]
    return jnp.concatenate([g0, v0, g1, v1], axis=-1)


def _unpair_cols(w):
    g0, v0 = w[..., 0:FF_HALF], w[..., FF_HALF:D_FF]
    g1, v1 = w[..., D_FF:D_FF + FF_HALF], w[..., D_FF + FF_HALF:]
    return jnp.concatenate([g0, g1, v0, v1], axis=-1)


def _row(v):
    return v.reshape(1, -1)


def kernel(x, c, w_ada, b_ada, w_in, b_forget, w_fox_proj, w_sb_proj, w_o, ln1_g, ln1_b, w_up, conv_w, conv_b, w_down, ln2_g, ln2_b, loss_target, m_w_ada, m_b_ada, m_w_in, m_b_forget, m_w_fox_proj, m_w_sb_proj, m_w_o, m_ln1_g, m_ln1_b, m_w_up, m_conv_w, m_conv_b, m_w_down, m_ln2_g, m_ln2_b, v_w_ada, v_b_ada, v_w_in, v_b_forget, v_w_fox_proj, v_w_sb_proj, v_w_o, v_ln1_g, v_ln1_b, v_w_up, v_conv_w, v_conv_b, v_w_down, v_ln2_g, v_ln2_b):
    s = x.shape[1]
    me = 4 * lax.axis_index("x") + 2 * lax.axis_index("y") + lax.axis_index("c")
    x2 = x.reshape(s, D_MODEL)
    tgt = loss_target.reshape(s, D_MODEL)

    b_ada_loc = lax.dynamic_slice(b_ada, (me * ADA_SHARD,), (ADA_SHARD,)).reshape(1, ADA_SHARD)
    c_all, mod = _mod_exchange(c, w_ada, b_ada_loc)
    mod = mod.reshape(N_MOD, 1, D_MODEL)
    sh1, sc1, gt1, sh2, sc2, gt2 = [mod[i] for i in range(N_MOD)]

    (g_in,) = _exchange("ag_w_in", [w_in.astype(BF16)], scatter=False)
    late_weights = _Ride([w_fox_proj.astype(BF16), w_sb_proj.astype(BF16), w_o.astype(BF16), w_up.astype(BF16),
                          w_down.astype(BF16), conv_w], scatter=False)
    w_in_f = _cols_from_slots(g_in)
    w_proj = jnp.concatenate(
        [w_in_f[:, 0:1536], w_in_f[:, 1544:3080], w_in_f[:, 3080:5128], w_in_f[:, 1536:1544],
         jnp.zeros((D_MODEL, W_PROJ - 5128), BF16)], axis=1)
    w_qkv, w_gates, w_f = w_proj[:, :W_QKV], w_proj[:, W_QKV:W_QKV + W_GATES], w_proj[:, W_QKV + W_GATES:W_QKV + W_GATES + W_F]
    conv_b_p = _pair_cols(_row(conv_b))
    b_f_pad = jnp.pad(_row(b_forget), ((0, 0), (0, LANES - N_HEADS)))

    (u1,) = _rowwise("modulate1", lambda xb, sc, sh: (xb * (1.0 + sc) + sh,),
                     [(x2, D_MODEL, 0)], [sc1, sh1], [(D_MODEL, BF16)], tr=512)
    qkv = _mm(u1, w_qkv, name="mm_qkv", out_dtype=BF16)
    gates = _mm(u1, w_gates, name="mm_gates")
    f_raw = _mm(u1, w_f, name="mm_forget")
    cum_col = _forget_cumsum(f_raw, b_f_pad)
    cum_row = jnp.transpose(cum_col[:, :N_HEADS]).reshape(N_HEADS // 2, 2, s)
    (y_fox, y_fox32, lse), (g_fox, g_sb, g_o, g_up, g_down, g_cw) = _fox_fwd(qkv, cum_col, cum_row, ride=late_weights)
    w_fox_f = _cols_from_slots(g_fox)
    w_sb_f = _cols_from_slots(g_sb)
    w_o_f = g_o.reshape(D_MODEL, D_MODEL)
    w_up_p = _pair_cols(_cols_from_slots(g_up))
    w_down_f = g_down.reshape(D_FF, D_MODEL)
    conv_w_p = _pair_cols(_cols_from_slots(g_cw))
    y_sb, sb_run = _sb_fwd(qkv)
    pf = _mm(y_fox, w_fox_f, name="mm_fox_proj")
    ps = _mm(y_sb, w_sb_f, name="mm_sb_proj")
    (merged,) = _rowwise("gate_merge", lambda ga, gb, a, b: (_sigmoid(ga) * a + _sigmoid(gb) * b,),
                         [(gates, D_MODEL, 0), (gates, D_MODEL, 1), (pf, D_MODEL, 0), (ps, D_MODEL, 0)], [],
                         [(D_MODEL, BF16)])
    attn_out = _mm(merged, w_o_f, name="mm_w_o")

    def ln_fwd(xb, fb, gt, g, b):
        xhat, _ = _ln_stats(ALPHA * xb + (1.0 + gt) * fb)
        return xhat * g + b

    def ln1_mod(xb, fb, gt, g, b, sc, sh):
        y = ln_fwd(xb, fb, gt, g, b)
        return y, y * (1.0 + sc) + sh

    x1, u2 = _rowwise("ln1_modulate2", ln1_mod, [(x2, D_MODEL, 0), (attn_out, D_MODEL, 0)],
                      [gt1, _row(ln1_g), _row(ln1_b), sc2, sh2], [(D_MODEL, F32), (D_MODEL, BF16)])

    hpre = _mm(u2, w_up_p, name="mm_w_up", tn=1408)
    act = _conv_gate_fwd(hpre, conv_w_p, conv_b_p)
    ffn_out = _mm(act, w_down_f, name="mm_w_down", tk=1408)

    def ln2_bwd(xb, fb, tb, gt, g, b):
        xhat, rstd = _ln_stats(ALPHA * xb + (1.0 + gt) * fb)
        err = (xhat * g + b) - tb
        dy = err * (1.0 / D_MODEL)
        dr = _ln_bwd(dy, xhat, rstd, g)
        return (dr * (1.0 + gt), ALPHA * dr,
                _colsum(err * err), _colsum(dy * xhat), _colsum(dy), _colsum(dr * fb))

    dffn, dx1_res, sq_err, d_ln2_g, d_ln2_b, d_gt2 = _rowwise(
        "ln2_bwd", ln2_bwd, [(x1, D_MODEL, 0), (ffn_out, D_MODEL, 0), (tgt, D_MODEL, 0)],
        [gt2, _row(ln2_g), _row(ln2_b)], [(D_MODEL, BF16), (D_MODEL, F32)], sums=[D_MODEL] * 4)
    loss = lax.psum(0.5 * jnp.sum(sq_err) / D_MODEL, ("x", "y", "c"))

    d_w_down = _mm(act, dffn, name="mm_d_w_down", ta=True, tm=1408, out_dtype=BF16)
    d_act = _mm(dffn, w_down_f, name="mm_d_act", tb=True, tn=1408, out_dtype=BF16)
    dh, d_conv_b_p, d_conv_w_p = _conv_gate_bwd(hpre, d_act, conv_w_p, conv_b_p)
    dhpre = _conv_input_bwd(dh, conv_w_p)
    d_w_up_p = _mm(u2, dhpre, name="mm_d_w_up", ta=True, tn=1408, out_dtype=BF16)
    du2 = _mm(dhpre, w_up_p, name="mm_d_u2", tb=True, tk=1408)

    def ln1_bwd(du, dres, x1b, xb, fb, sc, gt, g):
        dx1 = dres + du * (1.0 + sc)
        xhat, rstd = _ln_stats(ALPHA * xb + (1.0 + gt) * fb)
        dr = _ln_bwd(dx1, xhat, rstd, g)
        return (dr * (1.0 + gt), ALPHA * dr,
                _colsum(du * x1b), _colsum(du), _colsum(dx1 * xhat), _colsum(dx1), _colsum(dr * fb))

    d_attn, dx_res, d_sc2, d_sh2, d_ln1_g, d_ln1_b, d_gt1 = _rowwise(
        "ln1_bwd", ln1_bwd,
        [(du2, D_MODEL, 0), (dx1_res, D_MODEL, 0), (x1, D_MODEL, 0), (x2, D_MODEL, 0), (attn_out, D_MODEL, 0)],
        [sc2, gt1, _row(ln1_g)], [(D_MODEL, BF16), (D_MODEL, F32)], sums=[D_MODEL] * 5)

    d_w_o = _mm(merged, d_attn, name="mm_d_w_o", ta=True, out_dtype=BF16)
    d_merged = _mm(d_attn, w_o_f, name="mm_d_merged", tb=True)

    def merge_bwd(dm, ga, gb, a, b):
        sa, sb = _sigmoid(ga), _sigmoid(gb)
        return dm * a * sa * (1.0 - sa), dm * b * sb * (1.0 - sb), dm * sa, dm * sb

    d_ga, d_gb, d_pf, d_ps = _rowwise(
        "gate_merge_bwd", merge_bwd,
        [(d_merged, D_MODEL, 0), (gates, D_MODEL, 0), (gates, D_MODEL, 1), (pf, D_MODEL, 0), (ps, D_MODEL, 0)], [],
        [(D_MODEL, BF16)] * 4)
    d_w_fox = _mm(y_fox, d_pf, name="mm_d_w_fox", ta=True, out_dtype=BF16)
    d_w_sb = _mm(y_sb, d_ps, name="mm_d_w_sb", ta=True, out_dtype=BF16)
    d_y_fox = _mm(d_pf, w_fox_f, name="mm_d_y_fox", tb=True, out_dtype=BF16)
    d_y_sb = _mm(d_ps, w_sb_f, name="mm_d_y_sb", tb=True, out_dtype=BF16)
    early_grads = _Ride(
        [_cols_to_slots(d_w_fox), _cols_to_slots(d_w_sb), d_w_o.reshape(N_DEV, D_MODEL // N_DEV, D_MODEL),
         _cols_to_slots(_unpair_cols(d_w_up_p)), d_w_down.reshape(N_DEV, D_FF // N_DEV, D_MODEL)], scatter=True)
    (dq_a, dk_a, dv_a, d_cum_row, d_cum_q), early_slots = _fox_bwd(qkv, cum_col, cum_row, y_fox32, lse, d_y_fox,
                                                                    ride=early_grads)
    dq_b, dk_b, dv_b = _sb_bwd(qkv, sb_run, d_y_sb)
    d_cum = jnp.transpose(d_cum_row.reshape(N_HEADS, s)) + d_cum_q[:, ::HEAD_DIM]
    d_cum = jnp.pad(d_cum, ((0, 0), (0, LANES - N_HEADS)))
    d_f, d_b_forget = _forget_bwd(d_cum, f_raw, b_f_pad)
    d_proj = jnp.concatenate([dq_a, dk_a, dv_a, dq_b, dk_b, dv_b, d_ga, d_gb, d_f,
                              jnp.zeros((s, W_PROJ - W_QKV - W_GATES - W_F), BF16)], axis=1)
    d_w_proj = _mm(u1, d_proj, name="mm_d_w_in", ta=True, tn=896, out_dtype=BF16)
    d_w_in_f = jnp.concatenate([d_w_proj[:, 0:1536], d_w_proj[:, 5120:5128], d_w_proj[:, 1536:3072],
                                d_w_proj[:, 3072:5120]], axis=1)
    du1, (in_slots,) = _mm(d_proj, w_proj, name="mm_d_u1", tb=True, tk=896,
                           ride=_Ride([_cols_to_slots(d_w_in_f)], scatter=True))

    def x_bwd(du, dres, xb, sc):
        return dres + du * (1.0 + sc), _colsum(du * xb), _colsum(du)

    grad_x, d_sc1, d_sh1 = _rowwise("x_bwd", x_bwd, [(du1, D_MODEL, 0), (dx_res, D_MODEL, 0), (x2, D_MODEL, 0)],
                                    [sc1], [(D_MODEL, F32)], sums=[D_MODEL] * 2, tr=512)

    d_conv_b = _unpair_cols(d_conv_b_p)
    d_conv_w = _unpair_cols(d_conv_w_p)
    n_rep = N_MOD * D_MODEL + LANES + 4 * D_MODEL + 2 * D_FF
    small = jnp.concatenate(
        [d_sh1, d_sc1, d_gt1, d_sh2, d_sc2, d_gt2, d_b_forget, d_ln1_g, d_ln1_b, d_ln2_g, d_ln2_b, d_conv_b,
         d_conv_w.reshape(1, 6 * D_FF)], axis=1)
    n_small = small.shape[1] // LANES
    small = jnp.pad(small.reshape(n_small, LANES), ((0, 264 - n_small), (0, 0)))
    (small_all,) = _exchange("ag_small_grads", [small], scatter=False)
    rep8 = small_all[:, :n_rep // LANES, :]
    cw8 = small_all[:, n_rep // LANES:n_small, :].reshape(N_DEV, 3, 2 * D_FF)
    cw8 = lax.dynamic_slice(cw8, (0, 0, me * UP_SHARD), (N_DEV, 3, UP_SHARD))
    dmod8 = small_all[:, :N_MOD * D_MODEL // LANES, :].reshape(N_DEV, N_MOD * D_MODEL)
    dmod_loc = lax.dynamic_slice(dmod8, (0, me * ADA_SHARD), (N_DEV, ADA_SHARD))

    def pack_rep(b_a, b_f, g1, b1, g2, b2, cb):
        flat = jnp.concatenate([b_a, jnp.pad(b_f, (0, LANES - N_HEADS)), g1, b1, g2, b2, cb])
        return flat.reshape(n_rep // LANES, LANES)

    rep = _adamw("adamw_small", rep8, pack_rep(b_ada, b_forget, ln1_g, ln1_b, ln2_g, ln2_b, conv_b),
                 pack_rep(m_b_ada, m_b_forget, m_ln1_g, m_ln1_b, m_ln2_g, m_ln2_b, m_conv_b),
                 pack_rep(v_b_ada, v_b_forget, v_ln1_g, v_ln1_b, v_ln2_g, v_ln2_b, v_conv_b))

    def unpack_rep(p):
        flat = p.reshape(-1)
        o = N_MOD * D_MODEL
        return {"b_ada": flat[:o], "b_forget": flat[o:o + N_HEADS],
                "ln1_g": flat[o + 128:o + 1152], "ln1_b": flat[o + 1152:o + 2176],
                "ln2_g": flat[o + 2176:o + 3200], "ln2_b": flat[o + 3200:o + 4224], "conv_b": flat[o + 4224:]}

    rep = [unpack_rep(p) for p in rep]
    r_conv_w = _adamw("adamw_conv_w", cw8, conv_w, m_conv_w, v_conv_w)
    r_ada = _adamw_ada(jnp.transpose(c_all.reshape(N_DEV, D_MODEL)), dmod_loc, w_ada, m_w_ada, v_w_ada)

    r_in = _adamw("adamw_w_in", in_slots, w_in, m_w_in, v_w_in)
    r_fox = _adamw("adamw_w_fox", early_slots[0], w_fox_proj, m_w_fox_proj, v_w_fox_proj)
    r_sb = _adamw("adamw_w_sb", early_slots[1], w_sb_proj, m_w_sb_proj, v_w_sb_proj)
    r_o = _adamw("adamw_w_o", early_slots[2], w_o, m_w_o, v_w_o)
    r_up = _adamw("adamw_w_up", early_slots[3], w_up, m_w_up, v_w_up)
    r_down = _adamw("adamw_w_down", early_slots[4], w_down, m_w_down, v_w_down)

    def leaf(i):
        return [r_ada[i], rep[i]["b_ada"], r_in[i], rep[i]["b_forget"], r_fox[i], r_sb[i], r_o[i], rep[i]["ln1_g"],
                rep[i]["ln1_b"], r_up[i], r_conv_w[i], rep[i]["conv_b"], r_down[i], rep[i]["ln2_g"], rep[i]["ln2_b"]]

    return (loss, grad_x.reshape(1, s, D_MODEL), *leaf(0), *leaf(1), *leaf(2), *leaf(3))
```

```python
import functools

import jax
import jax.numpy as jnp
from jax import lax
from jax.experimental import pallas as pl
from jax.experimental.pallas import tpu as pltpu

F32 = jnp.float32
BF16 = jnp.bfloat16
MESH = pl.DeviceIdType.MESH
ANY = pl.BlockSpec(memory_space=pl.ANY)

N_DEV = 8
D_MODEL = 1024
HEAD_DIM = 64
N_HEADS = 8
ATTN_W = N_HEADS * HEAD_DIM
D_FF = 2816
FF_HALF = D_FF // 2
N_MOD = 6
ADA_SHARD = N_MOD * D_MODEL // N_DEV
IN_SHARD = 641
UP_SHARD = 704
ATTN_SCALE = HEAD_DIM ** -0.5
ALPHA = 2.0 ** 0.25
LN_EPS = 1e-5
LANES = 128
TQ = 512
SCAN_W = 256
VMEM_LIMIT = 56 * 1024 * 1024

ADAM_LR, ADAM_B1, ADAM_B2, ADAM_EPS, ADAM_WD, ADAM_STEP = 0.001, 0.9, 0.999, 1e-08, 0.01, 10

W_QKV, W_GATES, W_F = 3072, 2048, 128
W_PROJ = 5376


def _params(sem=None):
    return pltpu.CompilerParams(dimension_semantics=sem, vmem_limit_bytes=VMEM_LIMIT)


def _tile(n, cap):
    if n <= cap:
        return n
    best = None
    for t in range(LANES, cap + 1, LANES):
        if n % t == 0:
            best = t
    assert best is not None, (n, cap)
    return best


def _row_tile(r, width, budget=192 * 1024):
    if r * width <= budget or r % 16:
        return r
    best = 16
    for t in range(16, r + 1, 16):
        if r % t == 0 and t * width <= budget:
            best = t
    return best


def _me():
    x, y, c = lax.axis_index("x"), lax.axis_index("y"), lax.axis_index("c")
    return x, y, c, 4 * x + 2 * y + c


def _peer(r):
    x, y, c, _ = _me()
    px = 1 - x if r & 4 else x
    py = 1 - y if r & 2 else y
    pc = 1 - c if r & 1 else c
    return (px, py, pc), 4 * px + 2 * py + pc


class _Ride:
    def __init__(self, arrays, scatter):
        self.arrays, self.scatter, self.n = list(arrays), scatter, len(arrays)
        self.in_specs = [ANY] * self.n
        self.out_specs = [ANY] * self.n
        self.out_shape = [jax.ShapeDtypeStruct(a.shape if scatter else (N_DEV,) + a.shape, a.dtype) for a in arrays]
        self.scratch = [pltpu.SemaphoreType.DMA((self.n, N_DEV - 1)), pltpu.SemaphoreType.DMA((self.n, N_DEV - 1)),
                        pltpu.SemaphoreType.DMA((self.n,))]

    def _local(self, ins, outs, sems, a):
        me = _me()[3]
        return pltpu.make_async_copy(ins[a].at[me] if self.scatter else ins[a], outs[a].at[me], sems[2].at[a])

    def _remote(self, ins, outs, sems, a, r, arriving):
        me = _me()[3]
        peer, pidx = _peer(r)
        src = ins[a].at[me if arriving else pidx] if self.scatter else ins[a]
        return pltpu.make_async_remote_copy(
            src_ref=src, dst_ref=outs[a].at[pidx if arriving else me], send_sem=sems[0].at[a, r - 1],
            recv_sem=sems[1].at[a, r - 1], device_id=peer, device_id_type=MESH)

    def start(self, ins, outs, sems):
        for a in range(self.n):
            self._local(ins, outs, sems, a).start()
        for r in range(1, N_DEV):
            for a in range(self.n):
                self._remote(ins, outs, sems, a, r, False).start()

    def wait(self, ins, outs, sems):
        for r in range(1, N_DEV):
            for a in range(self.n):
                self._remote(ins, outs, sems, a, r, True).wait_recv()
        for r in range(1, N_DEV):
            for a in range(self.n):
                self._remote(ins, outs, sems, a, r, False).wait_send()
        for a in range(self.n):
            self._local(ins, outs, sems, a).wait()


def _exchange(name, arrays, scatter):
    ride = _Ride(arrays, scatter)

    def body(*refs):
        ins, outs, sems = refs[:ride.n], refs[ride.n:2 * ride.n], refs[2 * ride.n:]
        ride.start(ins, outs, sems)
        ride.wait(ins, outs, sems)

    return pl.pallas_call(body, name=name, in_specs=ride.in_specs, out_specs=ride.out_specs, out_shape=ride.out_shape,
                          scratch_shapes=ride.scratch)(*arrays)


def _with_ride(body, ride, n_in, n_out, grid):
    if ride is None:
        return body
    n = ride.n

    def wrapped(*refs):
        ins, rins = refs[:n_in], refs[n_in:n_in + n]
        outs, routs = refs[n_in + n:n_in + n + n_out], refs[n_in + n + n_out:n_in + 2 * n + n_out]
        rest = refs[n_in + 2 * n + n_out:]
        scratch, sems = rest[:len(rest) - 3], rest[len(rest) - 3:]
        ids = [pl.program_id(d) for d in range(len(grid))]
        first = functools.reduce(lambda p, q: p & q, [i == 0 for i in ids])
        last = functools.reduce(lambda p, q: p & q, [i == g - 1 for i, g in zip(ids, grid)])

        @pl.when(first)
        def _():
            ride.start(rins, routs, sems)

        body(*ins, *outs, *scratch)

        @pl.when(last)
        def _():
            ride.wait(rins, routs, sems)

    return wrapped


def _ride_call(body, ride, *, name, grid, in_specs, out_specs, out_shape, scratch_shapes, sem, args):
    n_in, n_out = len(in_specs), len(out_specs)
    if ride is None:
        res = pl.pallas_call(body, name=name, grid=grid, in_specs=in_specs, out_specs=out_specs, out_shape=out_shape,
                             scratch_shapes=scratch_shapes, compiler_params=_params(sem))(*args)
        return list(res), []
    res = pl.pallas_call(
        _with_ride(body, ride, n_in, n_out, grid), name=name, grid=grid,
        in_specs=list(in_specs) + ride.in_specs, out_specs=list(out_specs) + ride.out_specs,
        out_shape=list(out_shape) + ride.out_shape, scratch_shapes=list(scratch_shapes) + ride.scratch,
        compiler_params=_params(("arbitrary",) * len(grid)))(*args, *ride.arrays)
    return list(res[:n_out]), list(res[n_out:])


def _mm(a, b, *, name, ta=False, tb=False, out_dtype=F32, tm=1024, tn=1024, tk=1024, ride=None):
    m, k = (a.shape[1], a.shape[0]) if ta else a.shape
    n = b.shape[0] if tb else b.shape[1]
    assert (b.shape[1] if tb else b.shape[0]) == k
    tm, tn, tk = _tile(m, tm), _tile(n, tn), _tile(k, tk)
    nk = k // tk
    a_spec = pl.BlockSpec((tk, tm), lambda i, j, l: (l, i)) if ta else pl.BlockSpec((tm, tk), lambda i, j, l: (i, l))
    b_spec = pl.BlockSpec((tn, tk), lambda i, j, l: (j, l)) if tb else pl.BlockSpec((tk, tn), lambda i, j, l: (l, j))
    dims = (((0,) if ta else (1,), (1,) if tb else (0,)), ((), ()))

    def body(a_ref, b_ref, o_ref, *acc):
        p = lax.dot_general(a_ref[...].astype(BF16), b_ref[...].astype(BF16), dims, preferred_element_type=F32)
        if nk == 1:
            o_ref[...] = p.astype(out_dtype)
            return
        acc_ref = acc[0]
        step = pl.program_id(2)

        @pl.when(step == 0)
        def _():
            acc_ref[...] = p

        @pl.when(step > 0)
        def _():
            acc_ref[...] += p

        @pl.when(step == nk - 1)
        def _():
            o_ref[...] = acc_ref[...].astype(out_dtype)

    outs, rode = _ride_call(
        body, ride, name=name, grid=(m // tm, n // tn, nk), in_specs=[a_spec, b_spec],
        out_specs=[pl.BlockSpec((tm, tn), lambda i, j, l: (i, j))], out_shape=[jax.ShapeDtypeStruct((m, n), out_dtype)],
        scratch_shapes=[] if nk == 1 else [pltpu.VMEM((tm, tn), F32)], sem=("parallel", "parallel", "arbitrary"),
        args=(a, b))
    return outs[0] if ride is None else (outs[0], rode)


def _rowwise(name, fn, rows, vecs, outs, sums=(), tr=256):
    s = rows[0][0].shape[0]
    tr = min(tr, s)
    nr, nv, no = len(rows), len(vecs), len(outs)

    def body(*refs):
        vals = [r[...] for r in refs[:nr + nv]]
        res = fn(*vals)
        for o_ref, val in zip(refs[nr + nv:nr + nv + no], res[:no]):
            o_ref[...] = val.astype(o_ref.dtype)
        step = pl.program_id(0)
        for s_ref, val in zip(refs[nr + nv + no:], res[no:]):
            @pl.when(step == 0)
            def _(s_ref=s_ref, val=val):
                s_ref[...] = val

            @pl.when(step > 0)
            def _(s_ref=s_ref, val=val):
                s_ref[...] += val

    in_specs = [pl.BlockSpec((tr, w), functools.partial(lambda i, cb: (i, cb), cb=cb)) for _, w, cb in rows]
    in_specs += [pl.BlockSpec(v.shape, lambda i: (0, 0)) for v in vecs]
    out_specs = [pl.BlockSpec((tr, w), lambda i: (i, 0)) for w, _ in outs]
    out_specs += [pl.BlockSpec((1, w), lambda i: (0, 0)) for w in sums]
    out_shape = [jax.ShapeDtypeStruct((s, w), dt) for w, dt in outs]
    out_shape += [jax.ShapeDtypeStruct((1, w), F32) for w in sums]
    return pl.pallas_call(
        body, name=name, grid=(s // tr,), in_specs=in_specs, out_specs=out_specs, out_shape=out_shape,
        compiler_params=_params(("arbitrary",) if sums else ("parallel",)),
    )(*[r[0] for r in rows], *vecs)


def _colsum(x):
    return jnp.sum(x, axis=0, keepdims=True)


def _sigmoid(x):
    return 1.0 / (1.0 + jnp.exp(-x))


def _log_sigmoid(x):
    return jnp.minimum(x, 0.0) - jnp.log(1.0 + jnp.exp(-jnp.abs(x)))


def _ln_stats(r):
    mu = jnp.mean(r, axis=-1, keepdims=True)
    xc = r - mu
    var = jnp.mean(xc * xc, axis=-1, keepdims=True)
    rstd = lax.rsqrt(var + LN_EPS)
    return xc * rstd, rstd


def _ln_bwd(dy, xhat, rstd, g):
    dxh = dy * g
    m1 = jnp.mean(dxh, axis=-1, keepdims=True)
    m2 = jnp.mean(dxh * xhat, axis=-1, keepdims=True)
    return rstd * (dxh - m1 - xhat * m2)


def _mod_exchange(c_row, w_ada, b_ada_loc):
    def body(c_ref, w_ref, b_ref, call_ref, mod_ref, piece_ref, send_sems, recv_sems):
        me = _me()[3]
        call_ref[me] = c_ref[...]
        sent = []
        for r in range(1, N_DEV):
            peer, _ = _peer(r)
            cp = pltpu.make_async_remote_copy(
                src_ref=c_ref, dst_ref=call_ref.at[me], send_sem=send_sems.at[0, r - 1],
                recv_sem=recv_sems.at[0, r - 1], device_id=peer, device_id_type=MESH)
            cp.start()
            sent.append(cp)
        for r in range(1, N_DEV):
            peer, pidx = _peer(r)
            pltpu.make_async_remote_copy(
                src_ref=c_ref, dst_ref=call_ref.at[pidx], send_sem=send_sems.at[0, r - 1],
                recv_sem=recv_sems.at[0, r - 1], device_id=peer, device_id_type=MESH).wait_recv()
        c_all = jnp.concatenate([call_ref[d] for d in range(N_DEV)], axis=0)
        mod_loc = jnp.dot(c_all, w_ref[...], preferred_element_type=F32,
                          precision=lax.Precision.HIGHEST) + b_ref[...]
        for d in range(N_DEV):
            piece_ref[d] = mod_loc[d:d + 1, :]
        mod_ref[me] = piece_ref[me]
        for r in range(1, N_DEV):
            peer, pidx = _peer(r)
            cp = pltpu.make_async_remote_copy(
                src_ref=piece_ref.at[pidx], dst_ref=mod_ref.at[me], send_sem=send_sems.at[1, r - 1],
                recv_sem=recv_sems.at[1, r - 1], device_id=peer, device_id_type=MESH)
            cp.start()
            sent.append(cp)
        for r in range(1, N_DEV):
            peer, pidx = _peer(r)
            pltpu.make_async_remote_copy(
                src_ref=piece_ref.at[me], dst_ref=mod_ref.at[pidx], send_sem=send_sems.at[1, r - 1],
                recv_sem=recv_sems.at[1, r - 1], device_id=peer, device_id_type=MESH).wait_recv()
        for cp in sent:
            cp.wait_send()

    vmem = pl.BlockSpec(memory_space=pltpu.VMEM)
    return pl.pallas_call(
        body, name="mod_exchange", in_specs=[vmem, vmem, vmem], out_specs=[vmem, vmem],
        out_shape=[jax.ShapeDtypeStruct((N_DEV, 1, D_MODEL), F32), jax.ShapeDtypeStruct((N_DEV, 1, ADA_SHARD), F32)],
        scratch_shapes=[pltpu.VMEM((N_DEV, 1, ADA_SHARD), F32),
                        pltpu.SemaphoreType.DMA((2, N_DEV - 1)), pltpu.SemaphoreType.DMA((2, N_DEV - 1))],
        compiler_params=_params(),
    )(c_row, w_ada, b_ada_loc)


def _split3(x):
    hi = x.astype(BF16)
    r1 = x - hi.astype(F32)
    mid = r1.astype(BF16)
    lo = (r1 - mid.astype(F32)).astype(BF16)
    return hi, mid, lo


def _scan_rows(x_ref, o_ref, s, reverse, pre=None, post=None):
    tb = min(TQ, s)
    nb = s // tb
    row = lax.broadcasted_iota(jnp.int32, (tb, tb), 0)
    col = lax.broadcasted_iota(jnp.int32, (tb, tb), 1)
    tri = jnp.where((col >= row) if reverse else (col <= row), 1.0, 0.0).astype(BF16)

    def step(i, carry):
        blk = (nb - 1 - i) if reverse else i
        off = pl.multiple_of(blk * tb, tb)
        x = x_ref[pl.ds(off, tb), :]
        if pre is not None:
            x = pre(x, off)
        acc = carry
        for piece in _split3(x):
            acc = acc + jnp.dot(tri, piece, preferred_element_type=F32)
        o_ref[pl.ds(off, tb), :] = acc if post is None else post(acc, off)
        edge = acc[0:1, :] if reverse else acc[tb - 1:tb, :]
        return jnp.broadcast_to(edge, (tb, LANES))

    lax.fori_loop(0, nb, step, jnp.zeros((tb, LANES), F32))


def _forget_cumsum(f_raw, b_pad):
    s = f_raw.shape[0]

    def body(f_ref, b_ref, cum_ref):
        b = b_ref[...]
        _scan_rows(f_ref, cum_ref, s, False, pre=lambda x, off: _log_sigmoid(x + b))

    vmem = pl.BlockSpec(memory_space=pltpu.VMEM)
    return pl.pallas_call(body, name="forget_cumsum", in_specs=[vmem, vmem], out_specs=vmem,
                          out_shape=jax.ShapeDtypeStruct((s, LANES), F32), compiler_params=_params())(f_raw, b_pad)


def _forget_bwd(dcum, f_raw, b_pad):
    s = f_raw.shape[0]

    def body(d_ref, f_ref, b_ref, df_ref, db_ref, tmp_ref):
        b = b_ref[...]
        _scan_rows(d_ref, tmp_ref, s, True)
        df = tmp_ref[...] * _sigmoid(-(f_ref[...] + b))
        df_ref[...] = df.astype(BF16)
        db_ref[...] = _colsum(df)

    vmem = pl.BlockSpec(memory_space=pltpu.VMEM)
    return pl.pallas_call(
        body, name="forget_bwd", in_specs=[vmem, vmem, vmem], out_specs=[vmem, vmem],
        out_shape=[jax.ShapeDtypeStruct((s, LANES), BF16), jax.ShapeDtypeStruct((1, LANES), F32)],
        scratch_shapes=[pltpu.VMEM((s, LANES), F32)], compiler_params=_params())(dcum, f_raw, b_pad)


def _dot_nt(a, b):
    return lax.dot_general(a, b, (((1,), (1,)), ((), ())), preferred_element_type=F32)


def _head_masks():
    lane = lax.broadcasted_iota(jnp.int32, (TQ, LANES), 1)
    return lane, [lane < HEAD_DIM, lane >= HEAD_DIM]


def _pick(mask, x):
    return jnp.where(mask, x, jnp.zeros_like(x))


def _qkv_specs(s, col0):
    nb = ATTN_W // LANES
    return [pl.BlockSpec((TQ, LANES), lambda hp, qi: (qi, col0 + hp)),
            pl.BlockSpec((s, LANES), lambda hp, qi: (0, col0 + nb + hp)),
            pl.BlockSpec((s, LANES), lambda hp, qi: (0, col0 + 2 * nb + hp))]


def _pair_spec():
    return pl.BlockSpec((TQ, LANES), lambda hp, qi: (qi, hp))


def _fox_fwd(qkv, cum_col, cum_row, ride=None):
    s = qkv.shape[0]
    nq = s // TQ

    def body(q_ref, k_ref, v_ref, cc_ref, cr_ref, o_ref, o32_ref, lse_ref):
        hp, qi = pl.program_id(0), pl.program_id(1)
        lane, masks = _head_masks()
        row = lax.broadcasted_iota(jnp.int32, (TQ, TQ), 0)
        col = lax.broadcasted_iota(jnp.int32, (TQ, TQ), 1)
        causal = col <= row
        q2 = q_ref[...] * jnp.asarray(ATTN_SCALE, BF16)
        cc = cc_ref[...]
        qms = [_pick(masks[e], q2) for e in range(2)]
        cqs = [jnp.sum(jnp.where(lane == 2 * hp + e, cc, 0.0), axis=1, keepdims=True) for e in range(2)]

        def tile(kb, carry, masked):
            off = pl.multiple_of(kb * TQ, TQ)
            k2, v2 = k_ref[pl.ds(off, TQ), :], v_ref[pl.ds(off, TQ), :]
            new = []
            for e in range(2):
                m, l, acc = carry[e]
                sc = _dot_nt(qms[e], k2) + (cqs[e] - cr_ref[e:e + 1, pl.ds(off, TQ)])
                if masked:
                    sc = jnp.where(causal, sc, -jnp.inf)
                m_new = jnp.maximum(m, jnp.max(sc, axis=1, keepdims=True))
                p = jnp.exp(sc - m_new)
                corr = jnp.exp(m - m_new)
                l = corr * l + jnp.sum(p, axis=1, keepdims=True)
                acc = corr * acc + jnp.dot(p.astype(BF16), v2, preferred_element_type=F32)
                new.append((m_new, l, acc))
            return tuple(new)

        init = (jnp.full((TQ, 1), -jnp.inf, F32), jnp.zeros((TQ, 1), F32), jnp.zeros((TQ, LANES), F32))
        carry = lax.fori_loop(0, qi, lambda kb, cr: tile(kb, cr, False), tile(qi, (init, init), True))
        outs = [acc / l for _, l, acc in carry]
        lses = [m + jnp.log(l) for m, l, _ in carry]
        out = jnp.where(masks[0], outs[0], outs[1])
        o_ref[...] = out.astype(BF16)
        o32_ref[...] = out
        lse_ref[...] = jnp.where(masks[0], lses[0], lses[1])

    return _ride_call(
        body, ride, name="fox_fwd", grid=(N_HEADS // 2, nq),
        in_specs=_qkv_specs(s, 0) + [pl.BlockSpec((TQ, LANES), lambda hp, qi: (qi, 0)),
                                     pl.BlockSpec((None, 2, s), lambda hp, qi: (hp, 0, 0))],
        out_specs=[_pair_spec(), _pair_spec(), _pair_spec()],
        out_shape=[jax.ShapeDtypeStruct((s, ATTN_W), BF16), jax.ShapeDtypeStruct((s, ATTN_W), F32),
                   jax.ShapeDtypeStruct((s, ATTN_W), F32)],
        scratch_shapes=[], sem=("parallel", "parallel"), args=(qkv, qkv, qkv, cum_col, cum_row))


def _fox_bwd(qkv, cum_col, cum_row, o, lse, do, ride=None):
    s = qkv.shape[0]
    nq = s // TQ

    def body(q_ref, k_ref, v_ref, cc_ref, cr_ref, o_ref, lse_ref, do_ref,
             dq_ref, dk_ref, dv_ref, dcr_ref, dcq_ref, dk_acc, dv_acc):
        hp, qi = pl.program_id(0), pl.program_id(1)

        @pl.when(qi == 0)
        def _():
            dk_acc[...] = jnp.zeros_like(dk_acc)
            dv_acc[...] = jnp.zeros_like(dv_acc)
            dcr_ref[...] = jnp.zeros_like(dcr_ref)

        lane, masks = _head_masks()
        row = lax.broadcasted_iota(jnp.int32, (TQ, TQ), 0)
        col = lax.broadcasted_iota(jnp.int32, (TQ, TQ), 1)
        causal = col <= row
        q2 = q_ref[...] * jnp.asarray(ATTN_SCALE, BF16)
        do2 = do_ref[...]
        prod = do2.astype(F32) * o_ref[...].astype(F32)
        lse2 = lse_ref[...]
        cc = cc_ref[...]
        qms = [_pick(masks[e], q2) for e in range(2)]
        doms = [_pick(masks[e], do2) for e in range(2)]
        deltas = [jnp.sum(jnp.where(masks[e], prod, 0.0), axis=1, keepdims=True) for e in range(2)]
        lses = [jnp.max(jnp.where(masks[e], lse2, -jnp.inf), axis=1, keepdims=True) for e in range(2)]
        cqs = [jnp.sum(jnp.where(lane == 2 * hp + e, cc, 0.0), axis=1, keepdims=True) for e in range(2)]
        qts = [jnp.transpose(qms[e].astype(F32)).astype(BF16) for e in range(2)]
        dots = [jnp.transpose(doms[e].astype(F32)).astype(BF16) for e in range(2)]

        def tile(kb, carry, masked):
            off = pl.multiple_of(kb * TQ, TQ)
            k2, v2 = k_ref[pl.ds(off, TQ), :], v_ref[pl.ds(off, TQ), :]
            new, dk, dv = [], None, None
            for e in range(2):
                dq, rowsum = carry[e]
                sc = _dot_nt(qms[e], k2) + (cqs[e] - cr_ref[e:e + 1, pl.ds(off, TQ)])
                p = jnp.exp(sc - lses[e])
                if masked:
                    p = jnp.where(causal, p, 0.0)
                ds = p * (_dot_nt(doms[e], v2) - deltas[e])
                dsb = ds.astype(BF16)
                dk_e = jnp.dot(qts[e], dsb, preferred_element_type=F32)
                dv_e = jnp.dot(dots[e], p.astype(BF16), preferred_element_type=F32)
                dk, dv = (dk_e, dv_e) if e == 0 else (dk + dk_e, dv + dv_e)
                dcr_ref[e:e + 1, pl.ds(off, TQ)] -= _colsum(ds)
                new.append((dq + jnp.dot(dsb, k2, preferred_element_type=F32),
                            rowsum + jnp.sum(ds, axis=1, keepdims=True)))
            dk_acc[:, pl.ds(off, TQ)] += dk
            dv_acc[:, pl.ds(off, TQ)] += dv
            return tuple(new)

        init = (jnp.zeros((TQ, LANES), F32), jnp.zeros((TQ, 1), F32))
        carry = tile(qi, lax.fori_loop(0, qi, lambda kb, cr: tile(kb, cr, False), (init, init)), True)
        dq_ref[...] = (jnp.where(masks[0], carry[0][0], carry[1][0]) * ATTN_SCALE).astype(BF16)
        dcq_ref[...] = jnp.where(masks[0], carry[0][1], carry[1][1])

        @pl.when(qi == nq - 1)
        def _():
            dk_ref[...] = dk_acc[...].astype(BF16)
            dv_ref[...] = dv_acc[...].astype(BF16)

    seq_spec = pl.BlockSpec((LANES, s), lambda hp, qi: (hp, 0))
    (dq, dk_t, dv_t, dcr, dcq), rode = _ride_call(
        body, ride, name="fox_bwd", grid=(N_HEADS // 2, nq),
        in_specs=_qkv_specs(s, 0) + [pl.BlockSpec((TQ, LANES), lambda hp, qi: (qi, 0)),
                                     pl.BlockSpec((None, 2, s), lambda hp, qi: (hp, 0, 0)),
                                     _pair_spec(), _pair_spec(), _pair_spec()],
        out_specs=[_pair_spec(), seq_spec, seq_spec, pl.BlockSpec((None, 2, s), lambda hp, qi: (hp, 0, 0)),
                   _pair_spec()],
        out_shape=[jax.ShapeDtypeStruct((s, ATTN_W), BF16)] + [jax.ShapeDtypeStruct((ATTN_W, s), BF16)] * 2
        + [jax.ShapeDtypeStruct((N_HEADS // 2, 2, s), F32), jax.ShapeDtypeStruct((s, ATTN_W), F32)],
        scratch_shapes=[pltpu.VMEM((LANES, s), F32), pltpu.VMEM((LANES, s), F32)],
        sem=("parallel", "arbitrary"), args=(qkv, qkv, qkv, cum_col, cum_row, o, lse, do))
    return (dq, jnp.transpose(dk_t), jnp.transpose(dv_t), dcr, dcq), rode


def _scan_matrix(reverse):
    row = lax.broadcasted_iota(jnp.int32, (SCAN_W, SCAN_W), 0)
    col = lax.broadcasted_iota(jnp.int32, (SCAN_W, SCAN_W), 1)
    return jnp.where((row > col) if reverse else (row < col), 1.0, 0.0).astype(BF16)


def _scan_cols(x, tri, reverse, init):
    nblk = x.shape[1] // SCAN_W
    parts, total = [None] * nblk, init
    far = 0 if reverse else SCAN_W - 1
    for b in (reversed(range(nblk)) if reverse else range(nblk)):
        blk = x[:, b * SCAN_W:(b + 1) * SCAN_W]
        part = jnp.dot(blk.astype(BF16), tri, preferred_element_type=F32)
        parts[b] = part + total
        total = total + (part[:, far:far + 1] + blk[:, far:far + 1])
    return (parts[0] if nblk == 1 else jnp.concatenate(parts, axis=1)), total


def _sb_logits(qm, k2):
    z = _dot_nt(qm, k2)
    neg_abs = lax.bitcast_convert_type(lax.bitcast_convert_type(z, jnp.uint32) | jnp.uint32(0x80000000), F32)
    soft = jnp.log(1.0 + jnp.exp(neg_abs))
    lb = jnp.minimum(z, 0.0) - soft
    return lb, lb - z


def _tri_base(qi):
    return (qi * (qi + 1)) // 2


def _sb_fwd(qkv):
    s = qkv.shape[0]
    nq = s // TQ

    def body(q_ref, k_ref, v_ref, o_ref, t_ref, buf, sems):
        hp, qi = pl.program_id(0), pl.program_id(1)
        _, masks = _head_masks()
        row = lax.broadcasted_iota(jnp.int32, (TQ, TQ), 0)
        col = lax.broadcasted_iota(jnp.int32, (TQ, TQ), 1)
        strict = col < row
        suffix = _scan_matrix(True)
        q2 = q_ref[...] * jnp.asarray(ATTN_SCALE, BF16)
        qms = [_pick(masks[e], q2) for e in range(2)]
        base = _tri_base(qi)

        def store(e, slot, kb):
            return pltpu.make_async_copy(buf.at[e, slot], t_ref.at[2 * hp + e, base + kb], sems.at[e, slot])

        def tile(n, carry, masked):
            kb = qi - n
            off = pl.multiple_of(kb * TQ, TQ)
            k2, v2 = k_ref[pl.ds(off, TQ), :], v_ref[pl.ds(off, TQ), :]
            slot = n % 2
            new = []
            for e in range(2):
                run, acc = carry[e]
                lb, lo = _sb_logits(qms[e], k2)
                if masked:
                    lo = jnp.where(strict, lo, 0.0)
                rest, run = _scan_cols(lo, suffix, True, run)
                a = jnp.exp(lb + rest)
                if masked:
                    a = jnp.where(strict, a, 0.0)
                ab = a.astype(BF16)
                acc = acc + jnp.dot(ab, v2, preferred_element_type=F32)
                if not masked:
                    @pl.when(n >= 2)
                    def _(e=e, slot=slot, kb=kb):
                        store(e, slot, kb + 2).wait()
                buf[e, slot, 0] = ab
                buf[e, slot, 1] = lb.astype(BF16)
                store(e, slot, kb).start()
                new.append((run, acc))
            return tuple(new)

        init = (jnp.zeros((TQ, 1), F32), jnp.zeros((TQ, LANES), F32))
        carry = lax.fori_loop(0, qi, lambda it, cr: tile(it + 1, cr, False), tile(0, (init, init), True))
        for e in range(2):
            store(e, qi % 2, 0).wait()

            @pl.when(qi >= 1)
            def _(e=e):
                store(e, (qi - 1) % 2, 1).wait()
        o_ref[...] = jnp.where(masks[0], carry[0][1], carry[1][1]).astype(BF16)

    ntri = nq * (nq + 1) // 2
    return pl.pallas_call(
        body, name="sb_fwd", grid=(N_HEADS // 2, nq), in_specs=_qkv_specs(s, 3 * ATTN_W // LANES),
        out_specs=[_pair_spec(), ANY],
        out_shape=[jax.ShapeDtypeStruct((s, ATTN_W), BF16), jax.ShapeDtypeStruct((N_HEADS, ntri, 2, TQ, TQ), BF16)],
        scratch_shapes=[pltpu.VMEM((2, 2, 2, TQ, TQ), BF16), pltpu.SemaphoreType.DMA((2, 2))],
        compiler_params=_params(("arbitrary", "arbitrary")),
    )(qkv, qkv, qkv)


def _sb_bwd(qkv, tiles, do):
    s = qkv.shape[0]
    nq = s // TQ

    def body(q_ref, k_ref, v_ref, t_ref, do_ref, dq_ref, dk_ref, dv_ref, dk_acc, dv_acc, buf, sems):
        hp, qi = pl.program_id(0), pl.program_id(1)

        @pl.when(qi == 0)
        def _():
            dk_acc[...] = jnp.zeros_like(dk_acc)
            dv_acc[...] = jnp.zeros_like(dv_acc)

        _, masks = _head_masks()
        row = lax.broadcasted_iota(jnp.int32, (TQ, TQ), 0)
        col = lax.broadcasted_iota(jnp.int32, (TQ, TQ), 1)
        strict = col < row
        prefix = _scan_matrix(False)
        q2 = q_ref[...] * jnp.asarray(ATTN_SCALE, BF16)
        do2 = do_ref[...]
        qms = [_pick(masks[e], q2) for e in range(2)]
        doms = [_pick(masks[e], do2) for e in range(2)]
        qts = [jnp.transpose(qms[e].astype(F32)).astype(BF16) for e in range(2)]
        dots = [jnp.transpose(doms[e].astype(F32)).astype(BF16) for e in range(2)]
        base = _tri_base(qi)

        def fetch(e, kb):
            return pltpu.make_async_copy(t_ref.at[2 * hp + e, base + kb], buf.at[e, kb % 2], sems.at[e, kb % 2])

        for e in range(2):
            fetch(e, 0).start()

        def tile(kb, carry, masked):
            off = pl.multiple_of(kb * TQ, TQ)
            k2, v2 = k_ref[pl.ds(off, TQ), :], v_ref[pl.ds(off, TQ), :]
            new, dk, dv = [], None, None
            for e in range(2):
                gsum, dq = carry[e]
                if not masked:
                    fetch(e, kb + 1).start()
                fetch(e, kb).wait()
                ab = buf[e, kb % 2, 0]
                beta = jnp.exp(buf[e, kb % 2, 1].astype(F32))
                g = ab.astype(F32) * _dot_nt(doms[e], v2)
                before, gsum = _scan_cols(g, prefix, False, gsum)
                dz = g - beta * (g + before)
                if masked:
                    dz = jnp.where(strict, dz, 0.0)
                dzb = dz.astype(BF16)
                dk_e = jnp.dot(qts[e], dzb, preferred_element_type=F32)
                dv_e = jnp.dot(dots[e], ab, preferred_element_type=F32)
                dk, dv = (dk_e, dv_e) if e == 0 else (dk + dk_e, dv + dv_e)
                new.append((gsum, dq + jnp.dot(dzb, k2, preferred_element_type=F32)))
            dk_acc[:, pl.ds(off, TQ)] += dk
            dv_acc[:, pl.ds(off, TQ)] += dv
            return tuple(new)

        init = (jnp.zeros((TQ, 1), F32), jnp.zeros((TQ, LANES), F32))
        carry = tile(qi, lax.fori_loop(0, qi, lambda kb, cr: tile(kb, cr, False), (init, init)), True)
        dq_ref[...] = (jnp.where(masks[0], carry[0][1], carry[1][1]) * ATTN_SCALE).astype(BF16)

        @pl.when(qi == nq - 1)
        def _():
            dk_ref[...] = dk_acc[...].astype(BF16)
            dv_ref[...] = dv_acc[...].astype(BF16)

    seq_spec = pl.BlockSpec((LANES, s), lambda hp, qi: (hp, 0))
    dq, dk_t, dv_t = pl.pallas_call(
        body, name="sb_bwd", grid=(N_HEADS // 2, nq),
        in_specs=_qkv_specs(s, 3 * ATTN_W // LANES) + [ANY, _pair_spec()],
        out_specs=[_pair_spec(), seq_spec, seq_spec],
        out_shape=[jax.ShapeDtypeStruct((s, ATTN_W), BF16)] + [jax.ShapeDtypeStruct((ATTN_W, s), BF16)] * 2,
        scratch_shapes=[pltpu.VMEM((LANES, s), F32), pltpu.VMEM((LANES, s), F32),
                        pltpu.VMEM((2, 2, 2, TQ, TQ), BF16), pltpu.SemaphoreType.DMA((2, 2))],
        compiler_params=_params(("arbitrary", "arbitrary")),
    )(qkv, qkv, qkv, tiles, do)
    return dq, jnp.transpose(dk_t), jnp.transpose(dv_t)


CONV_TR = 256


def _shift_down(x, halo, n):
    rolled = pltpu.roll(x, n, 0)
    rid = lax.broadcasted_iota(jnp.int32, x.shape, 0)
    for j in range(n):
        rolled = jnp.where(rid == j, halo[8 - n + j:8 - n + j + 1, :], rolled)
    return rolled


def _shift_up(x, halo, n):
    rows = x.shape[0]
    rolled = pltpu.roll(x, rows - n, 0)
    rid = lax.broadcasted_iota(jnp.int32, x.shape, 0)
    for j in range(n):
        rolled = jnp.where(rid == rows - n + j, halo[j:j + 1, :], rolled)
    return rolled


def _conv_fwd_block(x, halo, w, b):
    return b + _shift_down(x, halo, 2) * w[0:1, :] + _shift_down(x, halo, 1) * w[1:2, :] + x * w[2:3, :]


def _conv_specs(tr, s):
    pair = 2 * FF_HALF
    blk = pl.BlockSpec((tr, pair), lambda j, i: (i, j))
    prev = pl.BlockSpec((8, pair), lambda j, i: (jnp.maximum(i * (tr // 8) - 1, 0), j))
    nxt = pl.BlockSpec((8, pair), lambda j, i: (jnp.minimum((i + 1) * (tr // 8), s // 8 - 1), j))
    return blk, prev, nxt


def _conv_gate_fwd(hpre, conv_w, conv_b):
    s = hpre.shape[0]
    tr = min(CONV_TR, s)
    blk, prev, _ = _conv_specs(tr, s)

    def body(x_ref, halo_ref, w_ref, b_ref, a_ref):
        i = pl.program_id(1)
        halo = jnp.where(i > 0, halo_ref[...], 0.0)
        h = _conv_fwd_block(x_ref[...], halo, w_ref[...], b_ref[...])
        hg, hv = h[:, :FF_HALF], h[:, FF_HALF:]
        a_ref[...] = (hg * _sigmoid(hg) * hv).astype(BF16)

    return pl.pallas_call(
        body, name="conv_gate_fwd", grid=(2, s // tr),
        in_specs=[blk, prev, pl.BlockSpec((3, 2 * FF_HALF), lambda j, i: (0, j)),
                  pl.BlockSpec((1, 2 * FF_HALF), lambda j, i: (0, j))],
        out_specs=pl.BlockSpec((tr, FF_HALF), lambda j, i: (i, j)),
        out_shape=jax.ShapeDtypeStruct((s, D_FF), BF16),
        compiler_params=_params(("parallel", "parallel")),
    )(hpre, hpre, conv_w, conv_b)


def _conv_gate_bwd(hpre, da, conv_w, conv_b):
    s = hpre.shape[0]
    tr = min(CONV_TR, s)
    blk, prev, _ = _conv_specs(tr, s)

    def body(x_ref, halo_ref, da_ref, w_ref, b_ref, dh_ref, db_ref, dw_ref):
        i = pl.program_id(1)
        halo = jnp.where(i > 0, halo_ref[...], 0.0)
        x = x_ref[...]
        h = _conv_fwd_block(x, halo, w_ref[...], b_ref[...])
        hg, hv = h[:, :FF_HALF], h[:, FF_HALF:]
        da_blk = da_ref[...].astype(F32)
        sg = _sigmoid(hg)
        dhg = da_blk * hv * (sg * (1.0 + hg * (1.0 - sg)))
        dhv = da_blk * (hg * sg)
        dh_ref[:, :FF_HALF] = dhg.astype(BF16)
        dh_ref[:, FF_HALF:] = dhv.astype(BF16)
        x2, x1 = _shift_down(x, halo, 2), _shift_down(x, halo, 1)
        parts = []
        for lo, dpart in ((0, dhg), (FF_HALF, dhv)):
            cols = slice(lo, lo + FF_HALF)
            parts.append((cols, _colsum(dpart), _colsum(dpart * x2[:, cols]), _colsum(dpart * x1[:, cols]),
                          _colsum(dpart * x[:, cols])))

        @pl.when(i == 0)
        def _():
            for cols, db, dw0, dw1, dw2 in parts:
                db_ref[:, cols] = db
                dw_ref[0:1, cols] = dw0
                dw_ref[1:2, cols] = dw1
                dw_ref[2:3, cols] = dw2

        @pl.when(i > 0)
        def _():
            for cols, db, dw0, dw1, dw2 in parts:
                db_ref[:, cols] += db
                dw_ref[0:1, cols] += dw0
                dw_ref[1:2, cols] += dw1
                dw_ref[2:3, cols] += dw2

    pair = 2 * FF_HALF
    return pl.pallas_call(
        body, name="conv_gate_bwd", grid=(2, s // tr),
        in_specs=[blk, prev, pl.BlockSpec((tr, FF_HALF), lambda j, i: (i, j)),
                  pl.BlockSpec((3, pair), lambda j, i: (0, j)), pl.BlockSpec((1, pair), lambda j, i: (0, j))],
        out_specs=[blk, pl.BlockSpec((1, pair), lambda j, i: (0, j)), pl.BlockSpec((3, pair), lambda j, i: (0, j))],
        out_shape=[jax.ShapeDtypeStruct((s, 2 * D_FF), BF16), jax.ShapeDtypeStruct((1, 2 * D_FF), F32),
                   jax.ShapeDtypeStruct((3, 2 * D_FF), F32)],
        compiler_params=_params(("parallel", "arbitrary")),
    )(hpre, hpre, da, conv_w, conv_b)


def _conv_input_bwd(dh, conv_w):
    s = dh.shape[0]
    tr = min(CONV_TR, s)
    blk, _, _ = _conv_specs(tr, s)
    nblk = s // tr

    def body(x_ref, halo_ref, w_ref, o_ref):
        i = pl.program_id(1)
        halo = jnp.where(i < nblk - 1, halo_ref[...].astype(F32), 0.0)
        x, w = x_ref[...].astype(F32), w_ref[...]
        o_ref[...] = (x * w[2:3, :] + _shift_up(x, halo, 1) * w[1:2, :] + _shift_up(x, halo, 2) * w[0:1, :]).astype(BF16)

    nxt = pl.BlockSpec((16, 2 * FF_HALF), lambda j, i: (jnp.minimum((i + 1) * (tr // 16), s // 16 - 1), j))
    return pl.pallas_call(
        body, name="conv_input_bwd", grid=(2, nblk),
        in_specs=[blk, nxt, pl.BlockSpec((3, 2 * FF_HALF), lambda j, i: (0, j))], out_specs=blk,
        out_shape=jax.ShapeDtypeStruct((s, 2 * D_FF), BF16),
        compiler_params=_params(("parallel", "parallel")),
    )(dh, dh, conv_w)


def _adamw_math(w, g, m, v):
    m = ADAM_B1 * m + (1.0 - ADAM_B1) * g
    v = ADAM_B2 * v + (1.0 - ADAM_B2) * (g * g)
    m_hat = m / (1.0 - ADAM_B1 ** ADAM_STEP)
    v_hat = v / (1.0 - ADAM_B2 ** ADAM_STEP)
    delta = -ADAM_LR * (m_hat / (jnp.sqrt(v_hat) + ADAM_EPS) + ADAM_WD * w)
    return delta, m, v


def _adamw(name, g8, w, m, v):
    r, c = w.shape
    tr = _row_tile(r, c)

    def body(g_ref, w_ref, m_ref, v_ref, go_ref, d_ref, mo_ref, vo_ref):
        g = g_ref[0].astype(F32)
        for d in range(1, N_DEV):
            g = g + g_ref[d].astype(F32)
        delta, mn, vn = _adamw_math(w_ref[...], g, m_ref[...], v_ref[...])
        go_ref[...] = g
        d_ref[...] = delta
        mo_ref[...] = mn
        vo_ref[...] = vn

    spec = pl.BlockSpec((tr, c), lambda i: (i, 0))
    return pl.pallas_call(
        body, name=name, grid=(r // tr,),
        in_specs=[pl.BlockSpec((N_DEV, tr, c), lambda i: (0, i, 0)), spec, spec, spec], out_specs=[spec] * 4,
        out_shape=[jax.ShapeDtypeStruct((r, c), F32)] * 4, compiler_params=_params(("parallel",)),
    )(g8, w, m, v)


def _adamw_ada(c_t, dmod, w, m, v):
    r, c = w.shape
    tr = _row_tile(r, c)

    def body(ct_ref, dm_ref, w_ref, m_ref, v_ref, go_ref, d_ref, mo_ref, vo_ref):
        ct, dm = ct_ref[...], dm_ref[...]
        g = ct[:, 0:1] * dm[0:1, :]
        for b in range(1, N_DEV):
            g = g + ct[:, b:b + 1] * dm[b:b + 1, :]
        delta, mn, vn = _adamw_math(w_ref[...], g, m_ref[...], v_ref[...])
        go_ref[...] = g
        d_ref[...] = delta
        mo_ref[...] = mn
        vo_ref[...] = vn

    spec = pl.BlockSpec((tr, c), lambda i: (i, 0))
    return pl.pallas_call(
        body, name="adamw_w_ada", grid=(r // tr,),
        in_specs=[pl.BlockSpec((tr, N_DEV), lambda i: (i, 0)), pl.BlockSpec((N_DEV, c), lambda i: (0, 0)),
                  spec, spec, spec],
        out_specs=[spec] * 4, out_shape=[jax.ShapeDtypeStruct((r, c), F32)] * 4,
        compiler_params=_params(("parallel",)),
    )(c_t, dmod, w, m, v)


def _cols_from_slots(g):
    n, r, c = g.shape
    return jnp.transpose(g, (1, 0, 2)).reshape(r, n * c)


def _cols_to_slots(w):
    r, c = w.shape
    return jnp.transpose(w.reshape(r, N_DEV, c // N_DEV), (1, 0, 2))


def _pair_cols(w):
    g0, g1 = w[..., 0:FF_HALF], w[..., FF_HALF:D_FF]
    v0, v1 = w[..., D_FF:D_FF + FF_HALF], w[..., D_FF + FF_HALF:]
    return jnp.concatenate([g0, v0, g1, v1], axis=-1)


def _unpair_cols(w):
    g0, v0 = w[..., 0:FF_HALF], w[..., FF_HALF:D_FF]
    g1, v1 = w[..., D_FF:D_FF + FF_HALF], w[..., D_FF + FF_HALF:]
    return jnp.concatenate([g0, g1, v0, v1], axis=-1)


def _row(v):
    return v.reshape(1, -1)


def kernel(x, c, w_ada, b_ada, w_in, b_forget, w_fox_proj, w_sb_proj, w_o, ln1_g, ln1_b, w_up, conv_w, conv_b, w_down, ln2_g, ln2_b, loss_target, m_w_ada, m_b_ada, m_w_in, m_b_forget, m_w_fox_proj, m_w_sb_proj, m_w_o, m_ln1_g, m_ln1_b, m_w_up, m_conv_w, m_conv_b, m_w_down, m_ln2_g, m_ln2_b, v_w_ada, v_b_ada, v_w_in, v_b_forget, v_w_fox_proj, v_w_sb_proj, v_w_o, v_ln1_g, v_ln1_b, v_w_up, v_conv_w, v_conv_b, v_w_down, v_ln2_g, v_ln2_b):
    s = x.shape[1]
    me = 4 * lax.axis_index("x") + 2 * lax.axis_index("y") + lax.axis_index("c")
    x2 = x.reshape(s, D_MODEL)
    tgt = loss_target.reshape(s, D_MODEL)

    b_ada_loc = lax.dynamic_slice(b_ada, (me * ADA_SHARD,), (ADA_SHARD,)).reshape(1, ADA_SHARD)
    c_all, mod = _mod_exchange(c, w_ada, b_ada_loc)
    mod = mod.reshape(N_MOD, 1, D_MODEL)
    sh1, sc1, gt1, sh2, sc2, gt2 = [mod[i] for i in range(N_MOD)]

    (g_in,) = _exchange("ag_w_in", [w_in.astype(BF16)], scatter=False)
    late_weights = _Ride([w_fox_proj.astype(BF16), w_sb_proj.astype(BF16), w_o.astype(BF16), w_up.astype(BF16),
                          w_down.astype(BF16), conv_w], scatter=False)
    w_in_f = _cols_from_slots(g_in)
    w_proj = jnp.concatenate(
        [w_in_f[:, 0:1536], w_in_f[:, 1544:3080], w_in_f[:, 3080:5128], w_in_f[:, 1536:1544],
         jnp.zeros((D_MODEL, W_PROJ - 5128), BF16)], axis=1)
    w_qkv, w_gates, w_f = w_proj[:, :W_QKV], w_proj[:, W_QKV:W_QKV + W_GATES], w_proj[:, W_QKV + W_GATES:W_QKV + W_GATES + W_F]
    conv_b_p = _pair_cols(_row(conv_b))
    b_f_pad = jnp.pad(_row(b_forget), ((0, 0), (0, LANES - N_HEADS)))

    (u1,) = _rowwise("modulate1", lambda xb, sc, sh: (xb * (1.0 + sc) + sh,),
                     [(x2, D_MODEL, 0)], [sc1, sh1], [(D_MODEL, BF16)], tr=512)
    qkv = _mm(u1, w_qkv, name="mm_qkv", out_dtype=BF16)
    gates = _mm(u1, w_gates, name="mm_gates")
    f_raw = _mm(u1, w_f, name="mm_forget")
    cum_col = _forget_cumsum(f_raw, b_f_pad)
    cum_row = jnp.transpose(cum_col[:, :N_HEADS]).reshape(N_HEADS // 2, 2, s)
    (y_fox, y_fox32, lse), (g_fox, g_sb, g_o, g_up, g_down, g_cw) = _fox_fwd(qkv, cum_col, cum_row, ride=late_weights)
    w_fox_f = _cols_from_slots(g_fox)
    w_sb_f = _cols_from_slots(g_sb)
    w_o_f = g_o.reshape(D_MODEL, D_MODEL)
    w_up_p = _pair_cols(_cols_from_slots(g_up))
    w_down_f = g_down.reshape(D_FF, D_MODEL)
    conv_w_p = _pair_cols(_cols_from_slots(g_cw))
    y_sb, sb_run = _sb_fwd(qkv)
    pf = _mm(y_fox, w_fox_f, name="mm_fox_proj")
    ps = _mm(y_sb, w_sb_f, name="mm_sb_proj")
    (merged,) = _rowwise("gate_merge", lambda ga, gb, a, b: (_sigmoid(ga) * a + _sigmoid(gb) * b,),
                         [(gates, D_MODEL, 0), (gates, D_MODEL, 1), (pf, D_MODEL, 0), (ps, D_MODEL, 0)], [],
                         [(D_MODEL, BF16)])
    attn_out = _mm(merged, w_o_f, name="mm_w_o")

    def ln_fwd(xb, fb, gt, g, b):
        xhat, _ = _ln_stats(ALPHA * xb + (1.0 + gt) * fb)
        return xhat * g + b

    def ln1_mod(xb, fb, gt, g, b, sc, sh):
        y = ln_fwd(xb, fb, gt, g, b)
        return y, y * (1.0 + sc) + sh

    x1, u2 = _rowwise("ln1_modulate2", ln1_mod, [(x2, D_MODEL, 0), (attn_out, D_MODEL, 0)],
                      [gt1, _row(ln1_g), _row(ln1_b), sc2, sh2], [(D_MODEL, F32), (D_MODEL, BF16)])

    hpre = _mm(u2, w_up_p, name="mm_w_up", tn=1408)
    act = _conv_gate_fwd(hpre, conv_w_p, conv_b_p)
    ffn_out = _mm(act, w_down_f, name="mm_w_down", tk=1408)

    def ln2_bwd(xb, fb, tb, gt, g, b):
        xhat, rstd = _ln_stats(ALPHA * xb + (1.0 + gt) * fb)
        err = (xhat * g + b) - tb
        dy = err * (1.0 / D_MODEL)
        dr = _ln_bwd(dy, xhat, rstd, g)
        return (dr * (1.0 + gt), ALPHA * dr,
                _colsum(err * err), _colsum(dy * xhat), _colsum(dy), _colsum(dr * fb))

    dffn, dx1_res, sq_err, d_ln2_g, d_ln2_b, d_gt2 = _rowwise(
        "ln2_bwd", ln2_bwd, [(x1, D_MODEL, 0), (ffn_out, D_MODEL, 0), (tgt, D_MODEL, 0)],
        [gt2, _row(ln2_g), _row(ln2_b)], [(D_MODEL, BF16), (D_MODEL, F32)], sums=[D_MODEL] * 4)
    loss = lax.psum(0.5 * jnp.sum(sq_err) / D_MODEL, ("x", "y", "c"))

    d_w_down = _mm(act, dffn, name="mm_d_w_down", ta=True, tm=1408, out_dtype=BF16)
    d_act = _mm(dffn, w_down_f, name="mm_d_act", tb=True, tn=1408, out_dtype=BF16)
    dh, d_conv_b_p, d_conv_w_p = _conv_gate_bwd(hpre, d_act, conv_w_p, conv_b_p)
    dhpre = _conv_input_bwd(dh, conv_w_p)
    d_w_up_p = _mm(u2, dhpre, name="mm_d_w_up", ta=True, tn=1408, out_dtype=BF16)
    du2 = _mm(dhpre, w_up_p, name="mm_d_u2", tb=True, tk=1408)

    def ln1_bwd(du, dres, x1b, xb, fb, sc, gt, g):
        dx1 = dres + du * (1.0 + sc)
        xhat, rstd = _ln_stats(ALPHA * xb + (1.0 + gt) * fb)
        dr = _ln_bwd(dx1, xhat, rstd, g)
        return (dr * (1.0 + gt), ALPHA * dr,
                _colsum(du * x1b), _colsum(du), _colsum(dx1 * xhat), _colsum(dx1), _colsum(dr * fb))

    d_attn, dx_res, d_sc2, d_sh2, d_ln1_g, d_ln1_b, d_gt1 = _rowwise(
        "ln1_bwd", ln1_bwd,
        [(du2, D_MODEL, 0), (dx1_res, D_MODEL, 0), (x1, D_MODEL, 0), (x2, D_MODEL, 0), (attn_out, D_MODEL, 0)],
        [sc2, gt1, _row(ln1_g)], [(D_MODEL, BF16), (D_MODEL, F32)], sums=[D_MODEL] * 5)

    d_w_o = _mm(merged, d_attn, name="mm_d_w_o", ta=True, out_dtype=BF16)
    d_merged = _mm(d_attn, w_o_f, name="mm_d_merged", tb=True)

    def merge_bwd(dm, ga, gb, a, b):
        sa, sb = _sigmoid(ga), _sigmoid(gb)
        return dm * a * sa * (1.0 - sa), dm * b * sb * (1.0 - sb), dm * sa, dm * sb

    d_ga, d_gb, d_pf, d_ps = _rowwise(
        "gate_merge_bwd", merge_bwd,
        [(d_merged, D_MODEL, 0), (gates, D_MODEL, 0), (gates, D_MODEL, 1), (pf, D_MODEL, 0), (ps, D_MODEL, 0)], [],
        [(D_MODEL, BF16)] * 4)
    d_w_fox = _mm(y_fox, d_pf, name="mm_d_w_fox", ta=True, out_dtype=BF16)
    d_w_sb = _mm(y_sb, d_ps, name="mm_d_w_sb", ta=True, out_dtype=BF16)
    d_y_fox = _mm(d_pf, w_fox_f, name="mm_d_y_fox", tb=True, out_dtype=BF16)
    d_y_sb = _mm(d_ps, w_sb_f, name="mm_d_y_sb", tb=True, out_dtype=BF16)
    early_grads = _Ride(
        [_cols_to_slots(d_w_fox), _cols_to_slots(d_w_sb), d_w_o.reshape(N_DEV, D_MODEL // N_DEV, D_MODEL),
         _cols_to_slots(_unpair_cols(d_w_up_p)), d_w_down.reshape(N_DEV, D_FF // N_DEV, D_MODEL)], scatter=True)
    (dq_a, dk_a, dv_a, d_cum_row, d_cum_q), early_slots = _fox_bwd(qkv, cum_col, cum_row, y_fox32, lse, d_y_fox,
                                                                    ride=early_grads)
    dq_b, dk_b, dv_b = _sb_bwd(qkv, sb_run, d_y_sb)
    d_cum = jnp.transpose(d_cum_row.reshape(N_HEADS, s)) + d_cum_q[:, ::HEAD_DIM]
    d_cum = jnp.pad(d_cum, ((0, 0), (0, LANES - N_HEADS)))
    d_f, d_b_forget = _forget_bwd(d_cum, f_raw, b_f_pad)
    d_proj = jnp.concatenate([dq_a, dk_a, dv_a, dq_b, dk_b, dv_b, d_ga, d_gb, d_f,
                              jnp.zeros((s, W_PROJ - W_QKV - W_GATES - W_F), BF16)], axis=1)
    d_w_proj = _mm(u1, d_proj, name="mm_d_w_in", ta=True, tn=896, out_dtype=BF16)
    d_w_in_f = jnp.concatenate([d_w_proj[:, 0:1536], d_w_proj[:, 5120:5128], d_w_proj[:, 1536:3072],
                                d_w_proj[:, 3072:5120]], axis=1)
    du1, (in_slots,) = _mm(d_proj, w_proj, name="mm_d_u1", tb=True, tk=896,
                           ride=_Ride([_cols_to_slots(d_w_in_f)], scatter=True))

    def x_bwd(du, dres, xb, sc):
        return dres + du * (1.0 + sc), _colsum(du * xb), _colsum(du)

    grad_x, d_sc1, d_sh1 = _rowwise("x_bwd", x_bwd, [(du1, D_MODEL, 0), (dx_res, D_MODEL, 0), (x2, D_MODEL, 0)],
                                    [sc1], [(D_MODEL, F32)], sums=[D_MODEL] * 2, tr=512)

    d_conv_b = _unpair_cols(d_conv_b_p)
    d_conv_w = _unpair_cols(d_conv_w_p)
    n_rep = N_MOD * D_MODEL + LANES + 4 * D_MODEL + 2 * D_FF
    small = jnp.concatenate(
        [d_sh1, d_sc1, d_gt1, d_sh2, d_sc2, d_gt2, d_b_forget, d_ln1_g, d_ln1_b, d_ln2_g, d_ln2_b, d_conv_b,
         d_conv_w.reshape(1, 6 * D_FF)], axis=1)
    n_small = small.shape[1] // LANES
    small = jnp.pad(small.reshape(n_small, LANES), ((0, 264 - n_small), (0, 0)))
    (small_all,) = _exchange("ag_small_grads", [small], scatter=False)
    rep8 = small_all[:, :n_rep // LANES, :]
    cw8 = small_all[:, n_rep // LANES:n_small, :].reshape(N_DEV, 3, 2 * D_FF)
    cw8 = lax.dynamic_slice(cw8, (0, 0, me * UP_SHARD), (N_DEV, 3, UP_SHARD))
    dmod8 = small_all[:, :N_MOD * D_MODEL // LANES, :].reshape(N_DEV, N_MOD * D_MODEL)
    dmod_loc = lax.dynamic_slice(dmod8, (0, me * ADA_SHARD), (N_DEV, ADA_SHARD))

    def pack_rep(b_a, b_f, g1, b1, g2, b2, cb):
        flat = jnp.concatenate([b_a, jnp.pad(b_f, (0, LANES - N_HEADS)), g1, b1, g2, b2, cb])
        return flat.reshape(n_rep // LANES, LANES)

    rep = _adamw("adamw_small", rep8, pack_rep(b_ada, b_forget, ln1_g, ln1_b, ln2_g, ln2_b, conv_b),
                 pack_rep(m_b_ada, m_b_forget, m_ln1_g, m_ln1_b, m_ln2_g, m_ln2_b, m_conv_b),
                 pack_rep(v_b_ada, v_b_forget, v_ln1_g, v_ln1_b, v_ln2_g, v_ln2_b, v_conv_b))

    def unpack_rep(p):
        flat = p.reshape(-1)
        o = N_MOD * D_MODEL
        return {"b_ada": flat[:o], "b_forget": flat[o:o + N_HEADS],
                "ln1_g": flat[o + 128:o + 1152], "ln1_b": flat[o + 1152:o + 2176],
                "ln2_g": flat[o + 2176:o + 3200], "ln2_b": flat[o + 3200:o + 4224], "conv_b": flat[o + 4224:]}

    rep = [unpack_rep(p) for p in rep]
    r_conv_w = _adamw("adamw_conv_w", cw8, conv_w, m_conv_w, v_conv_w)
    r_ada = _adamw_ada(jnp.transpose(c_all.reshape(N_DEV, D_MODEL)), dmod_loc, w_ada, m_w_ada, v_w_ada)

    r_in = _adamw("adamw_w_in", in_slots, w_in, m_w_in, v_w_in)
    r_fox = _adamw("adamw_w_fox", early_slots[0], w_fox_proj, m_w_fox_proj, v_w_fox_proj)
    r_sb = _adamw("adamw_w_sb", early_slots[1], w_sb_proj, m_w_sb_proj, v_w_sb_proj)
    r_o = _adamw("adamw_w_o", early_slots[2], w_o, m_w_o, v_w_o)
    r_up = _adamw("adamw_w_up", early_slots[3], w_up, m_w_up, v_w_up)
    r_down = _adamw("adamw_w_down", early_slots[4], w_down, m_w_down, v_w_down)

    def leaf(i):
        return [r_ada[i], rep[i]["b_ada"], r_in[i], rep[i]["b_forget"], r_fox[i], r_sb[i], r_o[i], rep[i]["ln1_g"],
                rep[i]["ln1_b"], r_up[i], r_conv_w[i], rep[i]["conv_b"], r_down[i], rep[i]["ln2_g"], rep[i]["ln2_b"]]

    return (loss, grad_x.reshape(1, s, D_MODEL), *leaf(0), *leaf(1), *leaf(2), *leaf(3))
```

```python
import functools

import jax
import jax.numpy as jnp
from jax import lax
from jax.experimental import pallas as pl
from jax.experimental.pallas import tpu as pltpu

F32 = jnp.float32
BF16 = jnp.bfloat16
MESH = pl.DeviceIdType.MESH
ANY = pl.BlockSpec(memory_space=pl.ANY)

N_DEV = 8
D_MODEL = 1024
HEAD_DIM = 64
N_HEADS = 8
ATTN_W = N_HEADS * HEAD_DIM
D_FF = 2816
FF_HALF = D_FF // 2
N_MOD = 6
ADA_SHARD = N_MOD * D_MODEL // N_DEV
IN_SHARD = 641
UP_SHARD = 704
ATTN_SCALE = HEAD_DIM ** -0.5
ALPHA = 2.0 ** 0.25
LN_EPS = 1e-5
LANES = 128
TQ = 512
SCAN_W = 256
VMEM_LIMIT = 56 * 1024 * 1024

ADAM_LR, ADAM_B1, ADAM_B2, ADAM_EPS, ADAM_WD, ADAM_STEP = 0.001, 0.9, 0.999, 1e-08, 0.01, 10

W_QKV, W_GATES, W_F = 3072, 2048, 128
W_PROJ = 5376


def _params(sem=None):
    return pltpu.CompilerParams(dimension_semantics=sem, vmem_limit_bytes=VMEM_LIMIT)


def _tile(n, cap):
    if n <= cap:
        return n
    best = None
    for t in range(LANES, cap + 1, LANES):
        if n % t == 0:
            best = t
    assert best is not None, (n, cap)
    return best


def _row_tile(r, width, budget=192 * 1024):
    if r * width <= budget or r % 16:
        return r
    best = 16
    for t in range(16, r + 1, 16):
        if r % t == 0 and t * width <= budget:
            best = t
    return best


def _me():
    x, y, c = lax.axis_index("x"), lax.axis_index("y"), lax.axis_index("c")
    return x, y, c, 4 * x + 2 * y + c


def _peer(r):
    x, y, c, _ = _me()
    px = 1 - x if r & 4 else x
    py = 1 - y if r & 2 else y
    pc = 1 - c if r & 1 else c
    return (px, py, pc), 4 * px + 2 * py + pc


class _Ride:
    def __init__(self, arrays, scatter):
        self.arrays, self.scatter, self.n = list(arrays), scatter, len(arrays)
        self.in_specs = [ANY] * self.n
        self.out_specs = [ANY] * self.n
        self.out_shape = [jax.ShapeDtypeStruct(a.shape if scatter else (N_DEV,) + a.shape, a.dtype) for a in arrays]
        self.scratch = [pltpu.SemaphoreType.DMA((self.n, N_DEV - 1)), pltpu.SemaphoreType.DMA((self.n, N_DEV - 1)),
                        pltpu.SemaphoreType.DMA((self.n,))]

    def _local(self, ins, outs, sems, a):
        me = _me()[3]
        return pltpu.make_async_copy(ins[a].at[me] if self.scatter else ins[a], outs[a].at[me], sems[2].at[a])

    def _remote(self, ins, outs, sems, a, r, arriving):
        me = _me()[3]
        peer, pidx = _peer(r)
        src = ins[a].at[me if arriving else pidx] if self.scatter else ins[a]
        return pltpu.make_async_remote_copy(
            src_ref=src, dst_ref=outs[a].at[pidx if arriving else me], send_sem=sems[0].at[a, r - 1],
            recv_sem=sems[1].at[a, r - 1], device_id=peer, device_id_type=MESH)

    def start(self, ins, outs, sems):
        for a in range(self.n):
            self._local(ins, outs, sems, a).start()
        for r in range(1, N_DEV):
            for a in range(self.n):
                self._remote(ins, outs, sems, a, r, False).start()

    def wait(self, ins, outs, sems):
        for r in range(1, N_DEV):
            for a in range(self.n):
                self._remote(ins, outs, sems, a, r, True).wait_recv()
        for r in range(1, N_DEV):
            for a in range(self.n):
                self._remote(ins, outs, sems, a, r, False).wait_send()
        for a in range(self.n):
            self._local(ins, outs, sems, a).wait()


def _exchange(name, arrays, scatter):
    ride = _Ride(arrays, scatter)

    def body(*refs):
        ins, outs, sems = refs[:ride.n], refs[ride.n:2 * ride.n], refs[2 * ride.n:]
        ride.start(ins, outs, sems)
        ride.wait(ins, outs, sems)

    return pl.pallas_call(body, name=name, in_specs=ride.in_specs, out_specs=ride.out_specs, out_shape=ride.out_shape,
                          scratch_shapes=ride.scratch)(*arrays)


def _with_ride(body, ride, n_in, n_out, grid):
    if ride is None:
        return body
    n = ride.n

    def wrapped(*refs):
        ins, rins = refs[:n_in], refs[n_in:n_in + n]
        outs, routs = refs[n_in + n:n_in + n + n_out], refs[n_in + n + n_out:n_in + 2 * n + n_out]
        rest = refs[n_in + 2 * n + n_out:]
        scratch, sems = rest[:len(rest) - 3], rest[len(rest) - 3:]
        ids = [pl.program_id(d) for d in range(len(grid))]
        first = functools.reduce(lambda p, q: p & q, [i == 0 for i in ids])
        last = functools.reduce(lambda p, q: p & q, [i == g - 1 for i, g in zip(ids, grid)])

        @pl.when(first)
        def _():
            ride.start(rins, routs, sems)

        body(*ins, *outs, *scratch)

        @pl.when(last)
        def _():
            ride.wait(rins, routs, sems)

    return wrapped


def _ride_call(body, ride, *, name, grid, in_specs, out_specs, out_shape, scratch_shapes, sem, args):
    n_in, n_out = len(in_specs), len(out_specs)
    if ride is None:
        res = pl.pallas_call(body, name=name, grid=grid, in_specs=in_specs, out_specs=out_specs, out_shape=out_shape,
                             scratch_shapes=scratch_shapes, compiler_params=_params(sem))(*args)
        return list(res), []
    res = pl.pallas_call(
        _with_ride(body, ride, n_in, n_out, grid), name=name, grid=grid,
        in_specs=list(in_specs) + ride.in_specs, out_specs=list(out_specs) + ride.out_specs,
        out_shape=list(out_shape) + ride.out_shape, scratch_shapes=list(scratch_shapes) + ride.scratch,
        compiler_params=_params(("arbitrary",) * len(grid)))(*args, *ride.arrays)
    return list(res[:n_out]), list(res[n_out:])


def _mm(a, b, *, name, ta=False, tb=False, out_dtype=F32, tm=1024, tn=1024, tk=1024, ride=None):
    m, k = (a.shape[1], a.shape[0]) if ta else a.shape
    n = b.shape[0] if tb else b.shape[1]
    assert (b.shape[1] if tb else b.shape[0]) == k
    tm, tn, tk = _tile(m, tm), _tile(n, tn), _tile(k, tk)
    nk = k // tk
    a_spec = pl.BlockSpec((tk, tm), lambda i, j, l: (l, i)) if ta else pl.BlockSpec((tm, tk), lambda i, j, l: (i, l))
    b_spec = pl.BlockSpec((tn, tk), lambda i, j, l: (j, l)) if tb else pl.BlockSpec((tk, tn), lambda i, j, l: (l, j))
    dims = (((0,) if ta else (1,), (1,) if tb else (0,)), ((), ()))

    def body(a_ref, b_ref, o_ref, *acc):
        p = lax.dot_general(a_ref[...].astype(BF16), b_ref[...].astype(BF16), dims, preferred_element_type=F32)
        if nk == 1:
            o_ref[...] = p.astype(out_dtype)
            return
        acc_ref = acc[0]
        step = pl.program_id(2)

        @pl.when(step == 0)
        def _():
            acc_ref[...] = p

        @pl.when(step > 0)
        def _():
            acc_ref[...] += p

        @pl.when(step == nk - 1)
        def _():
            o_ref[...] = acc_ref[...].astype(out_dtype)

    outs, rode = _ride_call(
        body, ride, name=name, grid=(m // tm, n // tn, nk), in_specs=[a_spec, b_spec],
        out_specs=[pl.BlockSpec((tm, tn), lambda i, j, l: (i, j))], out_shape=[jax.ShapeDtypeStruct((m, n), out_dtype)],
        scratch_shapes=[] if nk == 1 else [pltpu.VMEM((tm, tn), F32)], sem=("parallel", "parallel", "arbitrary"),
        args=(a, b))
    return outs[0] if ride is None else (outs[0], rode)


def _rowwise(name, fn, rows, vecs, outs, sums=(), tr=256):
    s = rows[0][0].shape[0]
    tr = min(tr, s)
    nr, nv, no = len(rows), len(vecs), len(outs)

    def body(*refs):
        vals = [r[...] for r in refs[:nr + nv]]
        res = fn(*vals)
        for o_ref, val in zip(refs[nr + nv:nr + nv + no], res[:no]):
            o_ref[...] = val.astype(o_ref.dtype)
        step = pl.program_id(0)
        for s_ref, val in zip(refs[nr + nv + no:], res[no:]):
            @pl.when(step == 0)
            def _(s_ref=s_ref, val=val):
                s_ref[...] = val

            @pl.when(step > 0)
            def _(s_ref=s_ref, val=val):
                s_ref[...] += val

    in_specs = [pl.BlockSpec((tr, w), functools.partial(lambda i, cb: (i, cb), cb=cb)) for _, w, cb in rows]
    in_specs += [pl.BlockSpec(v.shape, lambda i: (0, 0)) for v in vecs]
    out_specs = [pl.BlockSpec((tr, w), lambda i: (i, 0)) for w, _ in outs]
    out_specs += [pl.BlockSpec((1, w), lambda i: (0, 0)) for w in sums]
    out_shape = [jax.ShapeDtypeStruct((s, w), dt) for w, dt in outs]
    out_shape += [jax.ShapeDtypeStruct((1, w), F32) for w in sums]
    return pl.pallas_call(
        body, name=name, grid=(s // tr,), in_specs=in_specs, out_specs=out_specs, out_shape=out_shape,
        compiler_params=_params(("arbitrary",) if sums else ("parallel",)),
    )(*[r[0] for r in rows], *vecs)


def _colsum(x):
    return jnp.sum(x, axis=0, keepdims=True)


def _sigmoid(x):
    return 1.0 / (1.0 + jnp.exp(-x))


def _log_sigmoid(x):
    return jnp.minimum(x, 0.0) - jnp.log(1.0 + jnp.exp(-jnp.abs(x)))


def _ln_stats(r):
    mu = jnp.mean(r, axis=-1, keepdims=True)
    xc = r - mu
    var = jnp.mean(xc * xc, axis=-1, keepdims=True)
    rstd = lax.rsqrt(var + LN_EPS)
    return xc * rstd, rstd


def _ln_bwd(dy, xhat, rstd, g):
    dxh = dy * g
    m1 = jnp.mean(dxh, axis=-1, keepdims=True)
    m2 = jnp.mean(dxh * xhat, axis=-1, keepdims=True)
    return rstd * (dxh - m1 - xhat * m2)


def _mod_exchange(c_row, w_ada, b_ada_loc):
    def body(c_ref, w_ref, b_ref, call_ref, mod_ref, piece_ref, send_sems, recv_sems):
        me = _me()[3]
        call_ref[me] = c_ref[...]
        sent = []
        for r in range(1, N_DEV):
            peer, _ = _peer(r)
            cp = pltpu.make_async_remote_copy(
                src_ref=c_ref, dst_ref=call_ref.at[me], send_sem=send_sems.at[0, r - 1],
                recv_sem=recv_sems.at[0, r - 1], device_id=peer, device_id_type=MESH)
            cp.start()
            sent.append(cp)
        for r in range(1, N_DEV):
            peer, pidx = _peer(r)
            pltpu.make_async_remote_copy(
                src_ref=c_ref, dst_ref=call_ref.at[pidx], send_sem=send_sems.at[0, r - 1],
                recv_sem=recv_sems.at[0, r - 1], device_id=peer, device_id_type=MESH).wait_recv()
        c_all = jnp.concatenate([call_ref[d] for d in range(N_DEV)], axis=0)
        mod_loc = jnp.dot(c_all, w_ref[...], preferred_element_type=F32,
                          precision=lax.Precision.HIGHEST) + b_ref[...]
        for d in range(N_DEV):
            piece_ref[d] = mod_loc[d:d + 1, :]
        mod_ref[me] = piece_ref[me]
        for r in range(1, N_DEV):
            peer, pidx = _peer(r)
            cp = pltpu.make_async_remote_copy(
                src_ref=piece_ref.at[pidx], dst_ref=mod_ref.at[me], send_sem=send_sems.at[1, r - 1],
                recv_sem=recv_sems.at[1, r - 1], device_id=peer, device_id_type=MESH)
            cp.start()
            sent.append(cp)
        for r in range(1, N_DEV):
            peer, pidx = _peer(r)
            pltpu.make_async_remote_copy(
                src_ref=piece_ref.at[me], dst_ref=mod_ref.at[pidx], send_sem=send_sems.at[1, r - 1],
                recv_sem=recv_sems.at[1, r - 1], device_id=peer, device_id_type=MESH).wait_recv()
        for cp in sent:
            cp.wait_send()

    vmem = pl.BlockSpec(memory_space=pltpu.VMEM)
    return pl.pallas_call(
        body, name="mod_exchange", in_specs=[vmem, vmem, vmem], out_specs=[vmem, vmem],
        out_shape=[jax.ShapeDtypeStruct((N_DEV, 1, D_MODEL), F32), jax.ShapeDtypeStruct((N_DEV, 1, ADA_SHARD), F32)],
        scratch_shapes=[pltpu.VMEM((N_DEV, 1, ADA_SHARD), F32),
                        pltpu.SemaphoreType.DMA((2, N_DEV - 1)), pltpu.SemaphoreType.DMA((2, N_DEV - 1))],
        compiler_params=_params(),
    )(c_row, w_ada, b_ada_loc)


def _split3(x):
    hi = x.astype(BF16)
    r1 = x - hi.astype(F32)
    mid = r1.astype(BF16)
    lo = (r1 - mid.astype(F32)).astype(BF16)
    return hi, mid, lo


def _scan_rows(x_ref, o_ref, s, reverse, pre=None, post=None):
    tb = min(TQ, s)
    nb = s // tb
    row = lax.broadcasted_iota(jnp.int32, (tb, tb), 0)
    col = lax.broadcasted_iota(jnp.int32, (tb, tb), 1)
    tri = jnp.where((col >= row) if reverse else (col <= row), 1.0, 0.0).astype(BF16)

    def step(i, carry):
        blk = (nb - 1 - i) if reverse else i
        off = pl.multiple_of(blk * tb, tb)
        x = x_ref[pl.ds(off, tb), :]
        if pre is not None:
            x = pre(x, off)
        acc = carry
        for piece in _split3(x):
            acc = acc + jnp.dot(tri, piece, preferred_element_type=F32)
        o_ref[pl.ds(off, tb), :] = acc if post is None else post(acc, off)
        edge = acc[0:1, :] if reverse else acc[tb - 1:tb, :]
        return jnp.broadcast_to(edge, (tb, LANES))

    lax.fori_loop(0, nb, step, jnp.zeros((tb, LANES), F32))


def _forget_cumsum(f_raw, b_pad):
    s = f_raw.shape[0]

    def body(f_ref, b_ref, cum_ref):
        b = b_ref[...]
        _scan_rows(f_ref, cum_ref, s, False, pre=lambda x, off: _log_sigmoid(x + b))

    vmem = pl.BlockSpec(memory_space=pltpu.VMEM)
    return pl.pallas_call(body, name="forget_cumsum", in_specs=[vmem, vmem], out_specs=vmem,
                          out_shape=jax.ShapeDtypeStruct((s, LANES), F32), compiler_params=_params())(f_raw, b_pad)


def _forget_bwd(dcum, f_raw, b_pad):
    s = f_raw.shape[0]

    def body(d_ref, f_ref, b_ref, df_ref, db_ref, tmp_ref):
        b = b_ref[...]
        _scan_rows(d_ref, tmp_ref, s, True)
        df = tmp_ref[...] * _sigmoid(-(f_ref[...] + b))
        df_ref[...] = df.astype(BF16)
        db_ref[...] = _colsum(df)

    vmem = pl.BlockSpec(memory_space=pltpu.VMEM)
    return pl.pallas_call(
        body, name="forget_bwd", in_specs=[vmem, vmem, vmem], out_specs=[vmem, vmem],
        out_shape=[jax.ShapeDtypeStruct((s, LANES), BF16), jax.ShapeDtypeStruct((1, LANES), F32)],
        scratch_shapes=[pltpu.VMEM((s, LANES), F32)], compiler_params=_params())(dcum, f_raw, b_pad)


def _dot_nt(a, b):
    return lax.dot_general(a, b, (((1,), (1,)), ((), ())), preferred_element_type=F32)


def _head_masks():
    lane = lax.broadcasted_iota(jnp.int32, (TQ, LANES), 1)
    return lane, [lane < HEAD_DIM, lane >= HEAD_DIM]


def _pick(mask, x):
    return jnp.where(mask, x, jnp.zeros_like(x))


def _qkv_specs(s, col0):
    nb = ATTN_W // LANES
    return [pl.BlockSpec((TQ, LANES), lambda hp, qi: (qi, col0 + hp)),
            pl.BlockSpec((s, LANES), lambda hp, qi: (0, col0 + nb + hp)),
            pl.BlockSpec((s, LANES), lambda hp, qi: (0, col0 + 2 * nb + hp))]


def _pair_spec():
    return pl.BlockSpec((TQ, LANES), lambda hp, qi: (qi, hp))


def _fox_fwd(qkv, cum_col, cum_row, ride=None):
    s = qkv.shape[0]
    nq = s // TQ

    def body(q_ref, k_ref, v_ref, cc_ref, cr_ref, o_ref, o32_ref, lse_ref):
        hp, qi = pl.program_id(0), pl.program_id(1)
        lane, masks = _head_masks()
        row = lax.broadcasted_iota(jnp.int32, (TQ, TQ), 0)
        col = lax.broadcasted_iota(jnp.int32, (TQ, TQ), 1)
        causal = col <= row
        q2 = q_ref[...] * jnp.asarray(ATTN_SCALE, BF16)
        cc = cc_ref[...]
        qms = [_pick(masks[e], q2) for e in range(2)]
        cqs = [jnp.sum(jnp.where(lane == 2 * hp + e, cc, 0.0), axis=1, keepdims=True) for e in range(2)]

        def tile(kb, carry, masked):
            off = pl.multiple_of(kb * TQ, TQ)
            k2, v2 = k_ref[pl.ds(off, TQ), :], v_ref[pl.ds(off, TQ), :]
            new = []
            for e in range(2):
                m, acc = carry[e]
                sc = _dot_nt(qms[e], k2) + (cqs[e] - cr_ref[e:e + 1, pl.ds(off, TQ)])
                if masked:
                    sc = jnp.where(causal, sc, -jnp.inf)
                m_new = jnp.maximum(m, jnp.max(sc, axis=1, keepdims=True))
                p = jnp.exp(sc - m_new)
                v_ones = jnp.where(masks[e], v2, jnp.ones_like(v2))
                acc = jnp.exp(m - m_new) * acc + jnp.dot(p.astype(BF16), v_ones, preferred_element_type=F32)
                new.append((m_new, acc))
            return tuple(new)

        init = (jnp.full((TQ, 1), -jnp.inf, F32), jnp.zeros((TQ, LANES), F32))
        carry = lax.fori_loop(0, qi, lambda kb, cr: tile(kb, cr, False), tile(qi, (init, init), True))
        sums = [jnp.max(jnp.where(masks[1 - e], carry[e][1], 0.0), axis=1, keepdims=True) for e in range(2)]
        outs = [carry[e][1] / sums[e] for e in range(2)]
        lses = [carry[e][0] + jnp.log(sums[e]) for e in range(2)]
        out = jnp.where(masks[0], outs[0], outs[1])
        o_ref[...] = out.astype(BF16)
        o32_ref[...] = out
        lse_ref[...] = jnp.where(masks[0], lses[0], lses[1])

    return _ride_call(
        body, ride, name="fox_fwd", grid=(N_HEADS // 2, nq),
        in_specs=_qkv_specs(s, 0) + [pl.BlockSpec((TQ, LANES), lambda hp, qi: (qi, 0)),
                                     pl.BlockSpec((None, 2, s), lambda hp, qi: (hp, 0, 0))],
        out_specs=[_pair_spec(), _pair_spec(), _pair_spec()],
        out_shape=[jax.ShapeDtypeStruct((s, ATTN_W), BF16), jax.ShapeDtypeStruct((s, ATTN_W), F32),
                   jax.ShapeDtypeStruct((s, ATTN_W), F32)],
        scratch_shapes=[], sem=("parallel", "parallel"), args=(qkv, qkv, qkv, cum_col, cum_row))


def _write_transposed(acc_ref, out_ref):
    for c in range(out_ref.shape[0] // TQ):
        out_ref[c * TQ:(c + 1) * TQ, :] = jnp.transpose(acc_ref[:, c * TQ:(c + 1) * TQ]).astype(BF16)


def _fox_bwd(qkv, cum_col, cum_row, o, lse, do, ride=None):
    s = qkv.shape[0]
    nq = s // TQ

    def body(q_ref, k_ref, v_ref, cc_ref, cr_ref, o_ref, lse_ref, do_ref,
             dq_ref, dk_ref, dv_ref, dcr_ref, dcq_ref, dk_acc, dv_acc):
        hp, qi = pl.program_id(0), pl.program_id(1)

        @pl.when(qi == 0)
        def _():
            dk_acc[...] = jnp.zeros_like(dk_acc)
            dv_acc[...] = jnp.zeros_like(dv_acc)
            dcr_ref[...] = jnp.zeros_like(dcr_ref)

        lane, masks = _head_masks()
        row = lax.broadcasted_iota(jnp.int32, (TQ, TQ), 0)
        col = lax.broadcasted_iota(jnp.int32, (TQ, TQ), 1)
        causal = col <= row
        q2 = q_ref[...] * jnp.asarray(ATTN_SCALE, BF16)
        do2 = do_ref[...]
        prod = do2.astype(F32) * o_ref[...].astype(F32)
        lse2 = lse_ref[...]
        cc = cc_ref[...]
        qms = [_pick(masks[e], q2) for e in range(2)]
        doms = [_pick(masks[e], do2) for e in range(2)]
        deltas = [jnp.sum(jnp.where(masks[e], prod, 0.0), axis=1, keepdims=True) for e in range(2)]
        lses = [jnp.max(jnp.where(masks[e], lse2, -jnp.inf), axis=1, keepdims=True) for e in range(2)]
        cqs = [jnp.sum(jnp.where(lane == 2 * hp + e, cc, 0.0), axis=1, keepdims=True) for e in range(2)]
        qts = [jnp.transpose(qms[e].astype(F32)).astype(BF16) for e in range(2)]
        dots = [jnp.transpose(doms[e].astype(F32)).astype(BF16) for e in range(2)]

        def tile(kb, carry, masked):
            off = pl.multiple_of(kb * TQ, TQ)
            k2, v2 = k_ref[pl.ds(off, TQ), :], v_ref[pl.ds(off, TQ), :]
            new, dk, dv = [], None, None
            for e in range(2):
                dq, rowsum = carry[e]
                sc = _dot_nt(qms[e], k2) + (cqs[e] - cr_ref[e:e + 1, pl.ds(off, TQ)])
                p = jnp.exp(sc - lses[e])
                if masked:
                    p = jnp.where(causal, p, 0.0)
                ds = p * (_dot_nt(doms[e], v2) - deltas[e])
                dsb = ds.astype(BF16)
                dk_e = jnp.dot(qts[e], dsb, preferred_element_type=F32)
                dv_e = jnp.dot(dots[e], p.astype(BF16), preferred_element_type=F32)
                dk, dv = (dk_e, dv_e) if e == 0 else (dk + dk_e, dv + dv_e)
                dcr_ref[e:e + 1, pl.ds(off, TQ)] -= _colsum(ds)
                new.append((dq + jnp.dot(dsb, k2, preferred_element_type=F32),
                            rowsum + jnp.sum(ds, axis=1, keepdims=True)))
            dk_acc[:, pl.ds(off, TQ)] += dk
            dv_acc[:, pl.ds(off, TQ)] += dv
            return tuple(new)

        init = (jnp.zeros((TQ, LANES), F32), jnp.zeros((TQ, 1), F32))
        carry = tile(qi, lax.fori_loop(0, qi, lambda kb, cr: tile(kb, cr, False), (init, init)), True)
        dq_ref[...] = (jnp.where(masks[0], carry[0][0], carry[1][0]) * ATTN_SCALE).astype(BF16)
        dcq_ref[...] = jnp.where(masks[0], carry[0][1], carry[1][1])

        @pl.when(qi == nq - 1)
        def _():
            _write_transposed(dk_acc, dk_ref)
            _write_transposed(dv_acc, dv_ref)

    seq_spec = pl.BlockSpec((s, LANES), lambda hp, qi: (0, hp))
    return _ride_call(
        body, ride, name="fox_bwd", grid=(N_HEADS // 2, nq),
        in_specs=_qkv_specs(s, 0) + [pl.BlockSpec((TQ, LANES), lambda hp, qi: (qi, 0)),
                                     pl.BlockSpec((None, 2, s), lambda hp, qi: (hp, 0, 0)),
                                     _pair_spec(), _pair_spec(), _pair_spec()],
        out_specs=[_pair_spec(), seq_spec, seq_spec, pl.BlockSpec((None, 2, s), lambda hp, qi: (hp, 0, 0)),
                   _pair_spec()],
        out_shape=[jax.ShapeDtypeStruct((s, ATTN_W), BF16)] * 3 + [jax.ShapeDtypeStruct((N_HEADS // 2, 2, s), F32),
                                                                    jax.ShapeDtypeStruct((s, ATTN_W), F32)],
        scratch_shapes=[pltpu.VMEM((LANES, s), F32), pltpu.VMEM((LANES, s), F32)],
        sem=("parallel", "arbitrary"), args=(qkv, qkv, qkv, cum_col, cum_row, o, lse, do))


def _scan_matrix(reverse):
    row = lax.broadcasted_iota(jnp.int32, (SCAN_W, SCAN_W), 0)
    col = lax.broadcasted_iota(jnp.int32, (SCAN_W, SCAN_W), 1)
    return jnp.where((row > col) if reverse else (row < col), 1.0, 0.0).astype(BF16)


def _scan_cols(x, tri, reverse, init):
    nblk = x.shape[1] // SCAN_W
    parts, total = [None] * nblk, init
    far = 0 if reverse else SCAN_W - 1
    for b in (reversed(range(nblk)) if reverse else range(nblk)):
        blk = x[:, b * SCAN_W:(b + 1) * SCAN_W]
        part = jnp.dot(blk.astype(BF16), tri, preferred_element_type=F32)
        parts[b] = part + total
        total = total + (part[:, far:far + 1] + blk[:, far:far + 1])
    return (parts[0] if nblk == 1 else jnp.concatenate(parts, axis=1)), total


def _sb_logits(qm, k2):
    z = _dot_nt(qm, k2)
    neg_abs = lax.bitcast_convert_type(lax.bitcast_convert_type(z, jnp.uint32) | jnp.uint32(0x80000000), F32)
    soft = jnp.log(1.0 + jnp.exp(neg_abs))
    lb = jnp.minimum(z, 0.0) - soft
    return lb, lb - z


def _tri_base(qi):
    return (qi * (qi + 1)) // 2


def _sb_fwd(qkv):
    s = qkv.shape[0]
    nq = s // TQ

    def body(q_ref, k_ref, v_ref, o_ref, t_ref, buf, sems):
        hp, qi = pl.program_id(0), pl.program_id(1)
        _, masks = _head_masks()
        row = lax.broadcasted_iota(jnp.int32, (TQ, TQ), 0)
        col = lax.broadcasted_iota(jnp.int32, (TQ, TQ), 1)
        strict = col < row
        suffix = _scan_matrix(True)
        q2 = q_ref[...] * jnp.asarray(ATTN_SCALE, BF16)
        qms = [_pick(masks[e], q2) for e in range(2)]
        base = _tri_base(qi)

        def store(e, slot, kb):
            return pltpu.make_async_copy(buf.at[e, slot], t_ref.at[2 * hp + e, base + kb], sems.at[e, slot])

        def tile(n, carry, masked):
            kb = qi - n
            off = pl.multiple_of(kb * TQ, TQ)
            k2, v2 = k_ref[pl.ds(off, TQ), :], v_ref[pl.ds(off, TQ), :]
            slot = n % 2
            new = []
            for e in range(2):
                run, acc = carry[e]
                lb, lo = _sb_logits(qms[e], k2)
                if masked:
                    lo = jnp.where(strict, lo, 0.0)
                rest, run = _scan_cols(lo, suffix, True, run)
                a = jnp.exp(lb + rest)
                if masked:
                    a = jnp.where(strict, a, 0.0)
                ab = a.astype(BF16)
                acc = acc + jnp.dot(ab, v2, preferred_element_type=F32)
                if not masked:
                    @pl.when(n >= 2)
                    def _(e=e, slot=slot, kb=kb):
                        store(e, slot, kb + 2).wait()
                buf[e, slot, 0] = ab
                buf[e, slot, 1] = lb.astype(BF16)
                store(e, slot, kb).start()
                new.append((run, acc))
            return tuple(new)

        init = (jnp.zeros((TQ, 1), F32), jnp.zeros((TQ, LANES), F32))
        carry = lax.fori_loop(0, qi, lambda it, cr: tile(it + 1, cr, False), tile(0, (init, init), True))
        for e in range(2):
            store(e, qi % 2, 0).wait()

            @pl.when(qi >= 1)
            def _(e=e):
                store(e, (qi - 1) % 2, 1).wait()
        o_ref[...] = jnp.where(masks[0], carry[0][1], carry[1][1]).astype(BF16)

    ntri = nq * (nq + 1) // 2
    return pl.pallas_call(
        body, name="sb_fwd", grid=(N_HEADS // 2, nq), in_specs=_qkv_specs(s, 3 * ATTN_W // LANES),
        out_specs=[_pair_spec(), ANY],
        out_shape=[jax.ShapeDtypeStruct((s, ATTN_W), BF16), jax.ShapeDtypeStruct((N_HEADS, ntri, 2, TQ, TQ), BF16)],
        scratch_shapes=[pltpu.VMEM((2, 2, 2, TQ, TQ), BF16), pltpu.SemaphoreType.DMA((2, 2))],
        compiler_params=_params(("arbitrary", "arbitrary")),
    )(qkv, qkv, qkv)


def _sb_bwd(qkv, tiles, do):
    s = qkv.shape[0]
    nq = s // TQ

    def body(q_ref, k_ref, v_ref, t_ref, do_ref, dq_ref, dk_ref, dv_ref, dk_acc, dv_acc, buf, sems):
        hp, qi = pl.program_id(0), pl.program_id(1)

        @pl.when(qi == 0)
        def _():
            dk_acc[...] = jnp.zeros_like(dk_acc)
            dv_acc[...] = jnp.zeros_like(dv_acc)

        _, masks = _head_masks()
        row = lax.broadcasted_iota(jnp.int32, (TQ, TQ), 0)
        col = lax.broadcasted_iota(jnp.int32, (TQ, TQ), 1)
        strict = col < row
        prefix = _scan_matrix(False)
        q2 = q_ref[...] * jnp.asarray(ATTN_SCALE, BF16)
        do2 = do_ref[...]
        qms = [_pick(masks[e], q2) for e in range(2)]
        doms = [_pick(masks[e], do2) for e in range(2)]
        qts = [jnp.transpose(qms[e].astype(F32)).astype(BF16) for e in range(2)]
        dots = [jnp.transpose(doms[e].astype(F32)).astype(BF16) for e in range(2)]
        base = _tri_base(qi)

        def fetch(e, kb):
            return pltpu.make_async_copy(t_ref.at[2 * hp + e, base + kb], buf.at[e, kb % 2], sems.at[e, kb % 2])

        for e in range(2):
            fetch(e, 0).start()

        def tile(kb, carry, masked):
            off = pl.multiple_of(kb * TQ, TQ)
            k2, v2 = k_ref[pl.ds(off, TQ), :], v_ref[pl.ds(off, TQ), :]
            new, dk, dv = [], None, None
            for e in range(2):
                gsum, dq = carry[e]
                if not masked:
                    fetch(e, kb + 1).start()
                fetch(e, kb).wait()
                ab = buf[e, kb % 2, 0]
                beta = jnp.exp(buf[e, kb % 2, 1].astype(F32))
                g = ab.astype(F32) * _dot_nt(doms[e], v2)
                before, gsum = _scan_cols(g, prefix, False, gsum)
                dz = g - beta * (g + before)
                if masked:
                    dz = jnp.where(strict, dz, 0.0)
                dzb = dz.astype(BF16)
                dk_e = jnp.dot(qts[e], dzb, preferred_element_type=F32)
                dv_e = jnp.dot(dots[e], ab, preferred_element_type=F32)
                dk, dv = (dk_e, dv_e) if e == 0 else (dk + dk_e, dv + dv_e)
                new.append((gsum, dq + jnp.dot(dzb, k2, preferred_element_type=F32)))
            dk_acc[:, pl.ds(off, TQ)] += dk
            dv_acc[:, pl.ds(off, TQ)] += dv
            return tuple(new)

        init = (jnp.zeros((TQ, 1), F32), jnp.zeros((TQ, LANES), F32))
        carry = tile(qi, lax.fori_loop(0, qi, lambda kb, cr: tile(kb, cr, False), (init, init)), True)
        dq_ref[...] = (jnp.where(masks[0], carry[0][1], carry[1][1]) * ATTN_SCALE).astype(BF16)

        @pl.when(qi == nq - 1)
        def _():
            _write_transposed(dk_acc, dk_ref)
            _write_transposed(dv_acc, dv_ref)

    seq_spec = pl.BlockSpec((s, LANES), lambda hp, qi: (0, hp))
    return pl.pallas_call(
        body, name="sb_bwd", grid=(N_HEADS // 2, nq),
        in_specs=_qkv_specs(s, 3 * ATTN_W // LANES) + [ANY, _pair_spec()],
        out_specs=[_pair_spec(), seq_spec, seq_spec],
        out_shape=[jax.ShapeDtypeStruct((s, ATTN_W), BF16)] * 3,
        scratch_shapes=[pltpu.VMEM((LANES, s), F32), pltpu.VMEM((LANES, s), F32),
                        pltpu.VMEM((2, 2, 2, TQ, TQ), BF16), pltpu.SemaphoreType.DMA((2, 2))],
        compiler_params=_params(("arbitrary", "arbitrary")),
    )(qkv, qkv, qkv, tiles, do)


CONV_TR = 256


def _shift_down(x, halo, n):
    rolled = pltpu.roll(x, n, 0)
    rid = lax.broadcasted_iota(jnp.int32, x.shape, 0)
    for j in range(n):
        rolled = jnp.where(rid == j, halo[8 - n + j:8 - n + j + 1, :], rolled)
    return rolled


def _shift_up(x, halo, n):
    rows = x.shape[0]
    rolled = pltpu.roll(x, rows - n, 0)
    rid = lax.broadcasted_iota(jnp.int32, x.shape, 0)
    for j in range(n):
        rolled = jnp.where(rid == rows - n + j, halo[j:j + 1, :], rolled)
    return rolled


def _conv_fwd_block(x, halo, w, b):
    return b + _shift_down(x, halo, 2) * w[0:1, :] + _shift_down(x, halo, 1) * w[1:2, :] + x * w[2:3, :]


def _conv_specs(tr, s):
    pair = 2 * FF_HALF
    blk = pl.BlockSpec((tr, pair), lambda j, i: (i, j))
    prev = pl.BlockSpec((8, pair), lambda j, i: (jnp.maximum(i * (tr // 8) - 1, 0), j))
    nxt = pl.BlockSpec((8, pair), lambda j, i: (jnp.minimum((i + 1) * (tr // 8), s // 8 - 1), j))
    return blk, prev, nxt


def _conv_gate_fwd(hpre, conv_w, conv_b):
    s = hpre.shape[0]
    tr = min(CONV_TR, s)
    blk, prev, _ = _conv_specs(tr, s)

    def body(x_ref, halo_ref, w_ref, b_ref, a_ref):
        i = pl.program_id(1)
        halo = jnp.where(i > 0, halo_ref[...], 0.0)
        h = _conv_fwd_block(x_ref[...], halo, w_ref[...], b_ref[...])
        hg, hv = h[:, :FF_HALF], h[:, FF_HALF:]
        a_ref[...] = (hg * _sigmoid(hg) * hv).astype(BF16)

    return pl.pallas_call(
        body, name="conv_gate_fwd", grid=(2, s // tr),
        in_specs=[blk, prev, pl.BlockSpec((3, 2 * FF_HALF), lambda j, i: (0, j)),
                  pl.BlockSpec((1, 2 * FF_HALF), lambda j, i: (0, j))],
        out_specs=pl.BlockSpec((tr, FF_HALF), lambda j, i: (i, j)),
        out_shape=jax.ShapeDtypeStruct((s, D_FF), BF16),
        compiler_params=_params(("parallel", "parallel")),
    )(hpre, hpre, conv_w, conv_b)


def _conv_gate_bwd(hpre, da, conv_w, conv_b):
    s = hpre.shape[0]
    tr = min(CONV_TR, s)
    blk, prev, _ = _conv_specs(tr, s)

    def body(x_ref, halo_ref, da_ref, w_ref, b_ref, dh_ref, db_ref, dw_ref):
        i = pl.program_id(1)
        halo = jnp.where(i > 0, halo_ref[...], 0.0)
        x = x_ref[...]
        h = _conv_fwd_block(x, halo, w_ref[...], b_ref[...])
        hg, hv = h[:, :FF_HALF], h[:, FF_HALF:]
        da_blk = da_ref[...].astype(F32)
        sg = _sigmoid(hg)
        dhg = da_blk * hv * (sg * (1.0 + hg * (1.0 - sg)))
        dhv = da_blk * (hg * sg)
        dh_ref[:, :FF_HALF] = dhg.astype(BF16)
        dh_ref[:, FF_HALF:] = dhv.astype(BF16)
        x2, x1 = _shift_down(x, halo, 2), _shift_down(x, halo, 1)
        parts = []
        for lo, dpart in ((0, dhg), (FF_HALF, dhv)):
            cols = slice(lo, lo + FF_HALF)
            parts.append((cols, _colsum(dpart), _colsum(dpart * x2[:, cols]), _colsum(dpart * x1[:, cols]),
                          _colsum(dpart * x[:, cols])))

        @pl.when(i == 0)
        def _():
            for cols, db, dw0, dw1, dw2 in parts:
                db_ref[:, cols] = db
                dw_ref[0:1, cols] = dw0
                dw_ref[1:2, cols] = dw1
                dw_ref[2:3, cols] = dw2

        @pl.when(i > 0)
        def _():
            for cols, db, dw0, dw1, dw2 in parts:
                db_ref[:, cols] += db
                dw_ref[0:1, cols] += dw0
                dw_ref[1:2, cols] += dw1
                dw_ref[2:3, cols] += dw2

    pair = 2 * FF_HALF
    return pl.pallas_call(
        body, name="conv_gate_bwd", grid=(2, s // tr),
        in_specs=[blk, prev, pl.BlockSpec((tr, FF_HALF), lambda j, i: (i, j)),
                  pl.BlockSpec((3, pair), lambda j, i: (0, j)), pl.BlockSpec((1, pair), lambda j, i: (0, j))],
        out_specs=[blk, pl.BlockSpec((1, pair), lambda j, i: (0, j)), pl.BlockSpec((3, pair), lambda j, i: (0, j))],
        out_shape=[jax.ShapeDtypeStruct((s, 2 * D_FF), BF16), jax.ShapeDtypeStruct((1, 2 * D_FF), F32),
                   jax.ShapeDtypeStruct((3, 2 * D_FF), F32)],
        compiler_params=_params(("parallel", "arbitrary")),
    )(hpre, hpre, da, conv_w, conv_b)


def _conv_input_bwd(dh, conv_w):
    s = dh.shape[0]
    tr = min(CONV_TR, s)
    blk, _, _ = _conv_specs(tr, s)
    nblk = s // tr

    def body(x_ref, halo_ref, w_ref, o_ref):
        i = pl.program_id(1)
        halo = jnp.where(i < nblk - 1, halo_ref[...].astype(F32), 0.0)
        x, w = x_ref[...].astype(F32), w_ref[...]
        o_ref[...] = (x * w[2:3, :] + _shift_up(x, halo, 1) * w[1:2, :] + _shift_up(x, halo, 2) * w[0:1, :]).astype(BF16)

    nxt = pl.BlockSpec((16, 2 * FF_HALF), lambda j, i: (jnp.minimum((i + 1) * (tr // 16), s // 16 - 1), j))
    return pl.pallas_call(
        body, name="conv_input_bwd", grid=(2, nblk),
        in_specs=[blk, nxt, pl.BlockSpec((3, 2 * FF_HALF), lambda j, i: (0, j))], out_specs=blk,
        out_shape=jax.ShapeDtypeStruct((s, 2 * D_FF), BF16),
        compiler_params=_params(("parallel", "parallel")),
    )(dh, dh, conv_w)


def _adamw_math(w, g, m, v):
    m = ADAM_B1 * m + (1.0 - ADAM_B1) * g
    v = ADAM_B2 * v + (1.0 - ADAM_B2) * (g * g)
    m_hat = m / (1.0 - ADAM_B1 ** ADAM_STEP)
    v_hat = v / (1.0 - ADAM_B2 ** ADAM_STEP)
    delta = -ADAM_LR * (m_hat / (jnp.sqrt(v_hat) + ADAM_EPS) + ADAM_WD * w)
    return delta, m, v


def _adamw(name, g8, w, m, v):
    r, c = w.shape
    tr = _row_tile(r, c)

    def body(g_ref, w_ref, m_ref, v_ref, go_ref, d_ref, mo_ref, vo_ref):
        g = g_ref[0].astype(F32)
        for d in range(1, N_DEV):
            g = g + g_ref[d].astype(F32)
        delta, mn, vn = _adamw_math(w_ref[...], g, m_ref[...], v_ref[...])
        go_ref[...] = g
        d_ref[...] = delta
        mo_ref[...] = mn
        vo_ref[...] = vn

    spec = pl.BlockSpec((tr, c), lambda i: (i, 0))
    return pl.pallas_call(
        body, name=name, grid=(r // tr,),
        in_specs=[pl.BlockSpec((N_DEV, tr, c), lambda i: (0, i, 0)), spec, spec, spec], out_specs=[spec] * 4,
        out_shape=[jax.ShapeDtypeStruct((r, c), F32)] * 4, compiler_params=_params(("parallel",)),
    )(g8, w, m, v)


def _adamw_ada(c_t, dmod, w, m, v):
    r, c = w.shape
    tr = _row_tile(r, c)

    def body(ct_ref, dm_ref, w_ref, m_ref, v_ref, go_ref, d_ref, mo_ref, vo_ref):
        ct, dm = ct_ref[...], dm_ref[...]
        g = ct[:, 0:1] * dm[0:1, :]
        for b in range(1, N_DEV):
            g = g + ct[:, b:b + 1] * dm[b:b + 1, :]
        delta, mn, vn = _adamw_math(w_ref[...], g, m_ref[...], v_ref[...])
        go_ref[...] = g
        d_ref[...] = delta
        mo_ref[...] = mn
        vo_ref[...] = vn

    spec = pl.BlockSpec((tr, c), lambda i: (i, 0))
    return pl.pallas_call(
        body, name="adamw_w_ada", grid=(r // tr,),
        in_specs=[pl.BlockSpec((tr, N_DEV), lambda i: (i, 0)), pl.BlockSpec((N_DEV, c), lambda i: (0, 0)),
                  spec, spec, spec],
        out_specs=[spec] * 4, out_shape=[jax.ShapeDtypeStruct((r, c), F32)] * 4,
        compiler_params=_params(("parallel",)),
    )(c_t, dmod, w, m, v)


def _cols_from_slots(g):
    n, r, c = g.shape
    return jnp.transpose(g, (1, 0, 2)).reshape(r, n * c)


def _cols_to_slots(w):
    r, c = w.shape
    return jnp.transpose(w.reshape(r, N_DEV, c // N_DEV), (1, 0, 2))


def _pair_cols(w):
    g0, g1 = w[..., 0:FF_HALF], w[..., FF_HALF:D_FF]
    v0, v1 = w[..., D_FF:D_FF + FF_HALF], w[..., D_FF + FF_HALF:]
    return jnp.concatenate([g0, v0, g1, v1], axis=-1)


def _unpair_cols(w):
    g0, v0 = w[..., 0:FF_HALF], w[..., FF_HALF:D_FF]
    g1, v1 = w[..., D_FF:D_FF + FF_HALF], w[..., D_FF + FF_HALF:]
    return jnp.concatenate([g0, g1, v0, v1], axis=-1)


def _row(v):
    return v.reshape(1, -1)


def kernel(x, c, w_ada, b_ada, w_in, b_forget, w_fox_proj, w_sb_proj, w_o, ln1_g, ln1_b, w_up, conv_w, conv_b, w_down, ln2_g, ln2_b, loss_target, m_w_ada, m_b_ada, m_w_in, m_b_forget, m_w_fox_proj, m_w_sb_proj, m_w_o, m_ln1_g, m_ln1_b, m_w_up, m_conv_w, m_conv_b, m_w_down, m_ln2_g, m_ln2_b, v_w_ada, v_b_ada, v_w_in, v_b_forget, v_w_fox_proj, v_w_sb_proj, v_w_o, v_ln1_g, v_ln1_b, v_w_up, v_conv_w, v_conv_b, v_w_down, v_ln2_g, v_ln2_b):
    s = x.shape[1]
    me = 4 * lax.axis_index("x") + 2 * lax.axis_index("y") + lax.axis_index("c")
    x2 = x.reshape(s, D_MODEL)
    tgt = loss_target.reshape(s, D_MODEL)

    b_ada_loc = lax.dynamic_slice(b_ada, (me * ADA_SHARD,), (ADA_SHARD,)).reshape(1, ADA_SHARD)
    c_all, mod = _mod_exchange(c, w_ada, b_ada_loc)
    mod = mod.reshape(N_MOD, 1, D_MODEL)
    sh1, sc1, gt1, sh2, sc2, gt2 = [mod[i] for i in range(N_MOD)]

    (g_in,) = _exchange("ag_w_in", [w_in.astype(BF16)], scatter=False)
    late_weights = _Ride([w_fox_proj.astype(BF16), w_sb_proj.astype(BF16), w_o.astype(BF16), w_up.astype(BF16),
                          w_down.astype(BF16), conv_w], scatter=False)
    w_in_f = _cols_from_slots(g_in)
    w_proj = jnp.concatenate(
        [w_in_f[:, 0:1536], w_in_f[:, 1544:3080], w_in_f[:, 3080:5128], w_in_f[:, 1536:1544],
         jnp.zeros((D_MODEL, W_PROJ - 5128), BF16)], axis=1)
    w_qkv, w_gates, w_f = w_proj[:, :W_QKV], w_proj[:, W_QKV:W_QKV + W_GATES], w_proj[:, W_QKV + W_GATES:W_QKV + W_GATES + W_F]
    conv_b_p = _pair_cols(_row(conv_b))
    b_f_pad = jnp.pad(_row(b_forget), ((0, 0), (0, LANES - N_HEADS)))

    (u1,) = _rowwise("modulate1", lambda xb, sc, sh: (xb * (1.0 + sc) + sh,),
                     [(x2, D_MODEL, 0)], [sc1, sh1], [(D_MODEL, BF16)], tr=512)
    qkv = _mm(u1, w_qkv, name="mm_qkv", out_dtype=BF16)
    gates = _mm(u1, w_gates, name="mm_gates")
    f_raw = _mm(u1, w_f, name="mm_forget")
    cum_col = _forget_cumsum(f_raw, b_f_pad)
    cum_row = jnp.transpose(cum_col[:, :N_HEADS]).reshape(N_HEADS // 2, 2, s)
    (y_fox, y_fox32, lse), (g_fox, g_sb, g_o, g_up, g_down, g_cw) = _fox_fwd(qkv, cum_col, cum_row, ride=late_weights)
    w_fox_f = _cols_from_slots(g_fox)
    w_sb_f = _cols_from_slots(g_sb)
    w_o_f = g_o.reshape(D_MODEL, D_MODEL)
    w_up_p = _pair_cols(_cols_from_slots(g_up))
    w_down_f = g_down.reshape(D_FF, D_MODEL)
    conv_w_p = _pair_cols(_cols_from_slots(g_cw))
    y_sb, sb_run = _sb_fwd(qkv)
    pf = _mm(y_fox, w_fox_f, name="mm_fox_proj")
    ps = _mm(y_sb, w_sb_f, name="mm_sb_proj")
    (merged,) = _rowwise("gate_merge", lambda ga, gb, a, b: (_sigmoid(ga) * a + _sigmoid(gb) * b,),
                         [(gates, D_MODEL, 0), (gates, D_MODEL, 1), (pf, D_MODEL, 0), (ps, D_MODEL, 0)], [],
                         [(D_MODEL, BF16)])
    attn_out = _mm(merged, w_o_f, name="mm_w_o")

    def ln_fwd(xb, fb, gt, g, b):
        xhat, _ = _ln_stats(ALPHA * xb + (1.0 + gt) * fb)
        return xhat * g + b

    def ln1_mod(xb, fb, gt, g, b, sc, sh):
        y = ln_fwd(xb, fb, gt, g, b)
        return y, y * (1.0 + sc) + sh

    x1, u2 = _rowwise("ln1_modulate2", ln1_mod, [(x2, D_MODEL, 0), (attn_out, D_MODEL, 0)],
                      [gt1, _row(ln1_g), _row(ln1_b), sc2, sh2], [(D_MODEL, F32), (D_MODEL, BF16)])

    hpre = _mm(u2, w_up_p, name="mm_w_up", tn=1408)
    act = _conv_gate_fwd(hpre, conv_w_p, conv_b_p)
    ffn_out = _mm(act, w_down_f, name="mm_w_down", tk=1408)

    def ln2_bwd(xb, fb, tb, gt, g, b):
        xhat, rstd = _ln_stats(ALPHA * xb + (1.0 + gt) * fb)
        err = (xhat * g + b) - tb
        dy = err * (1.0 / D_MODEL)
        dr = _ln_bwd(dy, xhat, rstd, g)
        return (dr * (1.0 + gt), ALPHA * dr,
                _colsum(err * err), _colsum(dy * xhat), _colsum(dy), _colsum(dr * fb))

    dffn, dx1_res, sq_err, d_ln2_g, d_ln2_b, d_gt2 = _rowwise(
        "ln2_bwd", ln2_bwd, [(x1, D_MODEL, 0), (ffn_out, D_MODEL, 0), (tgt, D_MODEL, 0)],
        [gt2, _row(ln2_g), _row(ln2_b)], [(D_MODEL, BF16), (D_MODEL, F32)], sums=[D_MODEL] * 4)
    loss = lax.psum(0.5 * jnp.sum(sq_err) / D_MODEL, ("x", "y", "c"))

    d_w_down = _mm(act, dffn, name="mm_d_w_down", ta=True, tm=1408, out_dtype=BF16)
    d_act = _mm(dffn, w_down_f, name="mm_d_act", tb=True, tn=1408, out_dtype=BF16)
    dh, d_conv_b_p, d_conv_w_p = _conv_gate_bwd(hpre, d_act, conv_w_p, conv_b_p)
    dhpre = _conv_input_bwd(dh, conv_w_p)
    d_w_up_p = _mm(u2, dhpre, name="mm_d_w_up", ta=True, tn=1408, out_dtype=BF16)
    du2 = _mm(dhpre, w_up_p, name="mm_d_u2", tb=True, tk=1408)

    def ln1_bwd(du, dres, x1b, xb, fb, sc, gt, g):
        dx1 = dres + du * (1.0 + sc)
        xhat, rstd = _ln_stats(ALPHA * xb + (1.0 + gt) * fb)
        dr = _ln_bwd(dx1, xhat, rstd, g)
        return (dr * (1.0 + gt), ALPHA * dr,
                _colsum(du * x1b), _colsum(du), _colsum(dx1 * xhat), _colsum(dx1), _colsum(dr * fb))

    d_attn, dx_res, d_sc2, d_sh2, d_ln1_g, d_ln1_b, d_gt1 = _rowwise(
        "ln1_bwd", ln1_bwd,
        [(du2, D_MODEL, 0), (dx1_res, D_MODEL, 0), (x1, D_MODEL, 0), (x2, D_MODEL, 0), (attn_out, D_MODEL, 0)],
        [sc2, gt1, _row(ln1_g)], [(D_MODEL, BF16), (D_MODEL, F32)], sums=[D_MODEL] * 5)

    d_w_o = _mm(merged, d_attn, name="mm_d_w_o", ta=True, out_dtype=BF16)
    d_merged = _mm(d_attn, w_o_f, name="mm_d_merged", tb=True)

    def merge_bwd(dm, ga, gb, a, b):
        sa, sb = _sigmoid(ga), _sigmoid(gb)
        return dm * a * sa * (1.0 - sa), dm * b * sb * (1.0 - sb), dm * sa, dm * sb

    d_ga, d_gb, d_pf, d_ps = _rowwise(
        "gate_merge_bwd", merge_bwd,
        [(d_merged, D_MODEL, 0), (gates, D_MODEL, 0), (gates, D_MODEL, 1), (pf, D_MODEL, 0), (ps, D_MODEL, 0)], [],
        [(D_MODEL, BF16)] * 4)
    d_w_fox = _mm(y_fox, d_pf, name="mm_d_w_fox", ta=True, out_dtype=BF16)
    d_w_sb = _mm(y_sb, d_ps, name="mm_d_w_sb", ta=True, out_dtype=BF16)
    d_y_fox = _mm(d_pf, w_fox_f, name="mm_d_y_fox", tb=True, out_dtype=BF16)
    d_y_sb = _mm(d_ps, w_sb_f, name="mm_d_y_sb", tb=True, out_dtype=BF16)
    early_grads = _Ride(
        [_cols_to_slots(d_w_fox), _cols_to_slots(d_w_sb), d_w_o.reshape(N_DEV, D_MODEL // N_DEV, D_MODEL),
         _cols_to_slots(_unpair_cols(d_w_up_p)), d_w_down.reshape(N_DEV, D_FF // N_DEV, D_MODEL)], scatter=True)
    (dq_a, dk_a, dv_a, d_cum_row, d_cum_q), early_slots = _fox_bwd(qkv, cum_col, cum_row, y_fox32, lse, d_y_fox,
                                                                    ride=early_grads)
    dq_b, dk_b, dv_b = _sb_bwd(qkv, sb_run, d_y_sb)
    d_cum = jnp.transpose(d_cum_row.reshape(N_HEADS, s)) + d_cum_q[:, ::HEAD_DIM]
    d_cum = jnp.pad(d_cum, ((0, 0), (0, LANES - N_HEADS)))
    d_f, d_b_forget = _forget_bwd(d_cum, f_raw, b_f_pad)
    d_proj = jnp.concatenate([dq_a, dk_a, dv_a, dq_b, dk_b, dv_b, d_ga, d_gb, d_f,
                              jnp.zeros((s, W_PROJ - W_QKV - W_GATES - W_F), BF16)], axis=1)
    d_w_proj = _mm(u1, d_proj, name="mm_d_w_in", ta=True, tn=896, out_dtype=BF16)
    d_w_in_f = jnp.concatenate([d_w_proj[:, 0:1536], d_w_proj[:, 5120:5128], d_w_proj[:, 1536:3072],
                                d_w_proj[:, 3072:5120]], axis=1)
    du1, (in_slots,) = _mm(d_proj, w_proj, name="mm_d_u1", tb=True, tk=896,
                           ride=_Ride([_cols_to_slots(d_w_in_f)], scatter=True))

    def x_bwd(du, dres, xb, sc):
        return dres + du * (1.0 + sc), _colsum(du * xb), _colsum(du)

    grad_x, d_sc1, d_sh1 = _rowwise("x_bwd", x_bwd, [(du1, D_MODEL, 0), (dx_res, D_MODEL, 0), (x2, D_MODEL, 0)],
                                    [sc1], [(D_MODEL, F32)], sums=[D_MODEL] * 2, tr=512)

    d_conv_b = _unpair_cols(d_conv_b_p)
    d_conv_w = _unpair_cols(d_conv_w_p)
    n_rep = N_MOD * D_MODEL + LANES + 4 * D_MODEL + 2 * D_FF
    small = jnp.concatenate(
        [d_sh1, d_sc1, d_gt1, d_sh2, d_sc2, d_gt2, d_b_forget, d_ln1_g, d_ln1_b, d_ln2_g, d_ln2_b, d_conv_b,
         d_conv_w.reshape(1, 6 * D_FF)], axis=1)
    n_small = small.shape[1] // LANES
    small = jnp.pad(small.reshape(n_small, LANES), ((0, 264 - n_small), (0, 0)))
    (small_all,) = _exchange("ag_small_grads", [small], scatter=False)
    rep8 = small_all[:, :n_rep // LANES, :]
    cw8 = small_all[:, n_rep // LANES:n_small, :].reshape(N_DEV, 3, 2 * D_FF)
    cw8 = lax.dynamic_slice(cw8, (0, 0, me * UP_SHARD), (N_DEV, 3, UP_SHARD))
    dmod8 = small_all[:, :N_MOD * D_MODEL // LANES, :].reshape(N_DEV, N_MOD * D_MODEL)
    dmod_loc = lax.dynamic_slice(dmod8, (0, me * ADA_SHARD), (N_DEV, ADA_SHARD))

    def pack_rep(b_a, b_f, g1, b1, g2, b2, cb):
        flat = jnp.concatenate([b_a, jnp.pad(b_f, (0, LANES - N_HEADS)), g1, b1, g2, b2, cb])
        return flat.reshape(n_rep // LANES, LANES)

    rep = _adamw("adamw_small", rep8, pack_rep(b_ada, b_forget, ln1_g, ln1_b, ln2_g, ln2_b, conv_b),
                 pack_rep(m_b_ada, m_b_forget, m_ln1_g, m_ln1_b, m_ln2_g, m_ln2_b, m_conv_b),
                 pack_rep(v_b_ada, v_b_forget, v_ln1_g, v_ln1_b, v_ln2_g, v_ln2_b, v_conv_b))

    def unpack_rep(p):
        flat = p.reshape(-1)
        o = N_MOD * D_MODEL
        return {"b_ada": flat[:o], "b_forget": flat[o:o + N_HEADS],
                "ln1_g": flat[o + 128:o + 1152], "ln1_b": flat[o + 1152:o + 2176],
                "ln2_g": flat[o + 2176:o + 3200], "ln2_b": flat[o + 3200:o + 4224], "conv_b": flat[o + 4224:]}

    rep = [unpack_rep(p) for p in rep]
    r_conv_w = _adamw("adamw_conv_w", cw8, conv_w, m_conv_w, v_conv_w)
    r_ada = _adamw_ada(jnp.transpose(c_all.reshape(N_DEV, D_MODEL)), dmod_loc, w_ada, m_w_ada, v_w_ada)

    r_in = _adamw("adamw_w_in", in_slots, w_in, m_w_in, v_w_in)
    r_fox = _adamw("adamw_w_fox", early_slots[0], w_fox_proj, m_w_fox_proj, v_w_fox_proj)
    r_sb = _adamw("adamw_w_sb", early_slots[1], w_sb_proj, m_w_sb_proj, v_w_sb_proj)
    r_o = _adamw("adamw_w_o", early_slots[2], w_o, m_w_o, v_w_o)
    r_up = _adamw("adamw_w_up", early_slots[3], w_up, m_w_up, v_w_up)
    r_down = _adamw("adamw_w_down", early_slots[4], w_down, m_w_down, v_w_down)

    def leaf(i):
        return [r_ada[i], rep[i]["b_ada"], r_in[i], rep[i]["b_forget"], r_fox[i], r_sb[i], r_o[i], rep[i]["ln1_g"],
                rep[i]["ln1_b"], r_up[i], r_conv_w[i], rep[i]["conv_b"], r_down[i], rep[i]["ln2_g"], rep[i]["ln2_b"]]

    return (loss, grad_x.reshape(1, s, D_MODEL), *leaf(0), *leaf(1), *leaf(2), *leaf(3))
```

```python
import functools

import jax
import jax.numpy as jnp
from jax import lax
from jax.experimental import pallas as pl
from jax.experimental.pallas import tpu as pltpu

F32 = jnp.float32
BF16 = jnp.bfloat16
MESH = pl.DeviceIdType.MESH
ANY = pl.BlockSpec(memory_space=pl.ANY)

N_DEV = 8
D_MODEL = 1024
HEAD_DIM = 64
N_HEADS = 8
ATTN_W = N_HEADS * HEAD_DIM
D_FF = 2816
FF_HALF = D_FF // 2
N_MOD = 6
ADA_SHARD = N_MOD * D_MODEL // N_DEV
IN_SHARD = 641
UP_SHARD = 704
ATTN_SCALE = HEAD_DIM ** -0.5
ALPHA = 2.0 ** 0.25
LN_EPS = 1e-5
LANES = 128
TQ = 512
SCAN_W = 256
VMEM_LIMIT = 56 * 1024 * 1024

ADAM_LR, ADAM_B1, ADAM_B2, ADAM_EPS, ADAM_WD, ADAM_STEP = 0.001, 0.9, 0.999, 1e-08, 0.01, 10

W_QKV, W_GATES, W_F = 3072, 2048, 128
W_PROJ = 5376


def _params(sem=None):
    return pltpu.CompilerParams(dimension_semantics=sem, vmem_limit_bytes=VMEM_LIMIT)


def _tile(n, cap):
    if n <= cap:
        return n
    best = None
    for t in range(LANES, cap + 1, LANES):
        if n % t == 0:
            best = t
    assert best is not None, (n, cap)
    return best


def _row_tile(r, width, budget=192 * 1024):
    if r * width <= budget or r % 16:
        return r
    best = 16
    for t in range(16, r + 1, 16):
        if r % t == 0 and t * width <= budget:
            best = t
    return best


def _me():
    x, y, c = lax.axis_index("x"), lax.axis_index("y"), lax.axis_index("c")
    return x, y, c, 4 * x + 2 * y + c


def _peer(r):
    x, y, c, _ = _me()
    px = 1 - x if r & 4 else x
    py = 1 - y if r & 2 else y
    pc = 1 - c if r & 1 else c
    return (px, py, pc), 4 * px + 2 * py + pc


class _Ride:
    def __init__(self, arrays, scatter):
        self.arrays, self.scatter, self.n = list(arrays), scatter, len(arrays)
        self.in_specs = [ANY] * self.n
        self.out_specs = [ANY] * self.n
        self.out_shape = [jax.ShapeDtypeStruct(a.shape if scatter else (N_DEV,) + a.shape, a.dtype) for a in arrays]
        self.scratch = [pltpu.SemaphoreType.DMA((self.n, N_DEV - 1)), pltpu.SemaphoreType.DMA((self.n, N_DEV - 1)),
                        pltpu.SemaphoreType.DMA((self.n,))]

    def _local(self, ins, outs, sems, a):
        me = _me()[3]
        return pltpu.make_async_copy(ins[a].at[me] if self.scatter else ins[a], outs[a].at[me], sems[2].at[a])

    def _remote(self, ins, outs, sems, a, r, arriving):
        me = _me()[3]
        peer, pidx = _peer(r)
        src = ins[a].at[me if arriving else pidx] if self.scatter else ins[a]
        return pltpu.make_async_remote_copy(
            src_ref=src, dst_ref=outs[a].at[pidx if arriving else me], send_sem=sems[0].at[a, r - 1],
            recv_sem=sems[1].at[a, r - 1], device_id=peer, device_id_type=MESH)

    def start(self, ins, outs, sems):
        for a in range(self.n):
            self._local(ins, outs, sems, a).start()
        for r in range(1, N_DEV):
            for a in range(self.n):
                self._remote(ins, outs, sems, a, r, False).start()

    def wait(self, ins, outs, sems):
        for r in range(1, N_DEV):
            for a in range(self.n):
                self._remote(ins, outs, sems, a, r, True).wait_recv()
        for r in range(1, N_DEV):
            for a in range(self.n):
                self._remote(ins, outs, sems, a, r, False).wait_send()
        for a in range(self.n):
            self._local(ins, outs, sems, a).wait()


def _exchange(name, arrays, scatter):
    ride = _Ride(arrays, scatter)

    def body(*refs):
        ins, outs, sems = refs[:ride.n], refs[ride.n:2 * ride.n], refs[2 * ride.n:]
        ride.start(ins, outs, sems)
        ride.wait(ins, outs, sems)

    return pl.pallas_call(body, name=name, in_specs=ride.in_specs, out_specs=ride.out_specs, out_shape=ride.out_shape,
                          scratch_shapes=ride.scratch)(*arrays)


def _with_ride(body, ride, n_in, n_out, grid):
    if ride is None:
        return body
    n = ride.n

    def wrapped(*refs):
        ins, rins = refs[:n_in], refs[n_in:n_in + n]
        outs, routs = refs[n_in + n:n_in + n + n_out], refs[n_in + n + n_out:n_in + 2 * n + n_out]
        rest = refs[n_in + 2 * n + n_out:]
        scratch, sems = rest[:len(rest) - 3], rest[len(rest) - 3:]
        ids = [pl.program_id(d) for d in range(len(grid))]
        first = functools.reduce(lambda p, q: p & q, [i == 0 for i in ids])
        last = functools.reduce(lambda p, q: p & q, [i == g - 1 for i, g in zip(ids, grid)])

        @pl.when(first)
        def _():
            ride.start(rins, routs, sems)

        body(*ins, *outs, *scratch)

        @pl.when(last)
        def _():
            ride.wait(rins, routs, sems)

    return wrapped


def _ride_call(body, ride, *, name, grid, in_specs, out_specs, out_shape, scratch_shapes, sem, args):
    n_in, n_out = len(in_specs), len(out_specs)
    if ride is None:
        res = pl.pallas_call(body, name=name, grid=grid, in_specs=in_specs, out_specs=out_specs, out_shape=out_shape,
                             scratch_shapes=scratch_shapes, compiler_params=_params(sem))(*args)
        return list(res), []
    res = pl.pallas_call(
        _with_ride(body, ride, n_in, n_out, grid), name=name, grid=grid,
        in_specs=list(in_specs) + ride.in_specs, out_specs=list(out_specs) + ride.out_specs,
        out_shape=list(out_shape) + ride.out_shape, scratch_shapes=list(scratch_shapes) + ride.scratch,
        compiler_params=_params(("arbitrary",) * len(grid)))(*args, *ride.arrays)
    return list(res[:n_out]), list(res[n_out:])


def _mm(a, b, *, name, ta=False, tb=False, out_dtype=F32, tm=1024, tn=1024, tk=1024, ride=None):
    m, k = (a.shape[1], a.shape[0]) if ta else a.shape
    n = b.shape[0] if tb else b.shape[1]
    assert (b.shape[1] if tb else b.shape[0]) == k
    tm, tn, tk = _tile(m, tm), _tile(n, tn), _tile(k, tk)
    nk = k // tk
    a_spec = pl.BlockSpec((tk, tm), lambda i, j, l: (l, i)) if ta else pl.BlockSpec((tm, tk), lambda i, j, l: (i, l))
    b_spec = pl.BlockSpec((tn, tk), lambda i, j, l: (j, l)) if tb else pl.BlockSpec((tk, tn), lambda i, j, l: (l, j))
    dims = (((0,) if ta else (1,), (1,) if tb else (0,)), ((), ()))

    def body(a_ref, b_ref, o_ref, *acc):
        p = lax.dot_general(a_ref[...].astype(BF16), b_ref[...].astype(BF16), dims, preferred_element_type=F32)
        if nk == 1:
            o_ref[...] = p.astype(out_dtype)
            return
        acc_ref = acc[0]
        step = pl.program_id(2)

        @pl.when(step == 0)
        def _():
            acc_ref[...] = p

        @pl.when(step > 0)
        def _():
            acc_ref[...] += p

        @pl.when(step == nk - 1)
        def _():
            o_ref[...] = acc_ref[...].astype(out_dtype)

    outs, rode = _ride_call(
        body, ride, name=name, grid=(m // tm, n // tn, nk), in_specs=[a_spec, b_spec],
        out_specs=[pl.BlockSpec((tm, tn), lambda i, j, l: (i, j))], out_shape=[jax.ShapeDtypeStruct((m, n), out_dtype)],
        scratch_shapes=[] if nk == 1 else [pltpu.VMEM((tm, tn), F32)], sem=("parallel", "parallel", "arbitrary"),
        args=(a, b))
    return outs[0] if ride is None else (outs[0], rode)


def _rowwise(name, fn, rows, vecs, outs, sums=(), tr=256):
    s = rows[0][0].shape[0]
    tr = min(tr, s)
    nr, nv, no = len(rows), len(vecs), len(outs)

    def body(*refs):
        vals = [r[...] for r in refs[:nr + nv]]
        res = fn(*vals)
        for o_ref, val in zip(refs[nr + nv:nr + nv + no], res[:no]):
            o_ref[...] = val.astype(o_ref.dtype)
        step = pl.program_id(0)
        for s_ref, val in zip(refs[nr + nv + no:], res[no:]):
            @pl.when(step == 0)
            def _(s_ref=s_ref, val=val):
                s_ref[...] = val

            @pl.when(step > 0)
            def _(s_ref=s_ref, val=val):
                s_ref[...] += val

    in_specs = [pl.BlockSpec((tr, w), functools.partial(lambda i, cb: (i, cb), cb=cb)) for _, w, cb in rows]
    in_specs += [pl.BlockSpec(v.shape, lambda i: (0, 0)) for v in vecs]
    out_specs = [pl.BlockSpec((tr, w), lambda i: (i, 0)) for w, _ in outs]
    out_specs += [pl.BlockSpec((1, w), lambda i: (0, 0)) for w in sums]
    out_shape = [jax.ShapeDtypeStruct((s, w), dt) for w, dt in outs]
    out_shape += [jax.ShapeDtypeStruct((1, w), F32) for w in sums]
    return pl.pallas_call(
        body, name=name, grid=(s // tr,), in_specs=in_specs, out_specs=out_specs, out_shape=out_shape,
        compiler_params=_params(("arbitrary",) if sums else ("parallel",)),
    )(*[r[0] for r in rows], *vecs)


def _colsum(x):
    return jnp.sum(x, axis=0, keepdims=True)


def _sigmoid(x):
    return 1.0 / (1.0 + jnp.exp(-x))


def _log_sigmoid(x):
    return jnp.minimum(x, 0.0) - jnp.log(1.0 + jnp.exp(-jnp.abs(x)))


def _ln_stats(r):
    mu = jnp.mean(r, axis=-1, keepdims=True)
    xc = r - mu
    var = jnp.mean(xc * xc, axis=-1, keepdims=True)
    rstd = lax.rsqrt(var + LN_EPS)
    return xc * rstd, rstd


def _ln_bwd(dy, xhat, rstd, g):
    dxh = dy * g
    m1 = jnp.mean(dxh, axis=-1, keepdims=True)
    m2 = jnp.mean(dxh * xhat, axis=-1, keepdims=True)
    return rstd * (dxh - m1 - xhat * m2)


def _mod_exchange(c_row, w_ada, b_ada_loc):
    def body(c_ref, w_ref, b_ref, call_ref, mod_ref, piece_ref, send_sems, recv_sems):
        me = _me()[3]
        call_ref[me] = c_ref[...]
        sent = []
        for r in range(1, N_DEV):
            peer, _ = _peer(r)
            cp = pltpu.make_async_remote_copy(
                src_ref=c_ref, dst_ref=call_ref.at[me], send_sem=send_sems.at[0, r - 1],
                recv_sem=recv_sems.at[0, r - 1], device_id=peer, device_id_type=MESH)
            cp.start()
            sent.append(cp)
        for r in range(1, N_DEV):
            peer, pidx = _peer(r)
            pltpu.make_async_remote_copy(
                src_ref=c_ref, dst_ref=call_ref.at[pidx], send_sem=send_sems.at[0, r - 1],
                recv_sem=recv_sems.at[0, r - 1], device_id=peer, device_id_type=MESH).wait_recv()
        c_all = jnp.concatenate([call_ref[d] for d in range(N_DEV)], axis=0)
        mod_loc = jnp.dot(c_all, w_ref[...], preferred_element_type=F32,
                          precision=lax.Precision.HIGHEST) + b_ref[...]
        for d in range(N_DEV):
            piece_ref[d] = mod_loc[d:d + 1, :]
        mod_ref[me] = piece_ref[me]
        for r in range(1, N_DEV):
            peer, pidx = _peer(r)
            cp = pltpu.make_async_remote_copy(
                src_ref=piece_ref.at[pidx], dst_ref=mod_ref.at[me], send_sem=send_sems.at[1, r - 1],
                recv_sem=recv_sems.at[1, r - 1], device_id=peer, device_id_type=MESH)
            cp.start()
            sent.append(cp)
        for r in range(1, N_DEV):
            peer, pidx = _peer(r)
            pltpu.make_async_remote_copy(
                src_ref=piece_ref.at[me], dst_ref=mod_ref.at[pidx], send_sem=send_sems.at[1, r - 1],
                recv_sem=recv_sems.at[1, r - 1], device_id=peer, device_id_type=MESH).wait_recv()
        for cp in sent:
            cp.wait_send()

    vmem = pl.BlockSpec(memory_space=pltpu.VMEM)
    return pl.pallas_call(
        body, name="mod_exchange", in_specs=[vmem, vmem, vmem], out_specs=[vmem, vmem],
        out_shape=[jax.ShapeDtypeStruct((N_DEV, 1, D_MODEL), F32), jax.ShapeDtypeStruct((N_DEV, 1, ADA_SHARD), F32)],
        scratch_shapes=[pltpu.VMEM((N_DEV, 1, ADA_SHARD), F32),
                        pltpu.SemaphoreType.DMA((2, N_DEV - 1)), pltpu.SemaphoreType.DMA((2, N_DEV - 1))],
        compiler_params=_params(),
    )(c_row, w_ada, b_ada_loc)


def _split3(x):
    hi = x.astype(BF16)
    r1 = x - hi.astype(F32)
    mid = r1.astype(BF16)
    lo = (r1 - mid.astype(F32)).astype(BF16)
    return hi, mid, lo


def _scan_rows(x_ref, o_ref, s, reverse, pre=None, post=None):
    tb = min(TQ, s)
    nb = s // tb
    row = lax.broadcasted_iota(jnp.int32, (tb, tb), 0)
    col = lax.broadcasted_iota(jnp.int32, (tb, tb), 1)
    tri = jnp.where((col >= row) if reverse else (col <= row), 1.0, 0.0).astype(BF16)

    def step(i, carry):
        blk = (nb - 1 - i) if reverse else i
        off = pl.multiple_of(blk * tb, tb)
        x = x_ref[pl.ds(off, tb), :]
        if pre is not None:
            x = pre(x, off)
        acc = carry
        for piece in _split3(x):
            acc = acc + jnp.dot(tri, piece, preferred_element_type=F32)
        o_ref[pl.ds(off, tb), :] = acc if post is None else post(acc, off)
        edge = acc[0:1, :] if reverse else acc[tb - 1:tb, :]
        return jnp.broadcast_to(edge, (tb, LANES))

    lax.fori_loop(0, nb, step, jnp.zeros((tb, LANES), F32))


def _forget_cumsum(f_raw, b_pad):
    s = f_raw.shape[0]

    def body(f_ref, b_ref, cum_ref):
        b = b_ref[...]
        _scan_rows(f_ref, cum_ref, s, False, pre=lambda x, off: _log_sigmoid(x + b))

    vmem = pl.BlockSpec(memory_space=pltpu.VMEM)
    return pl.pallas_call(body, name="forget_cumsum", in_specs=[vmem, vmem], out_specs=vmem,
                          out_shape=jax.ShapeDtypeStruct((s, LANES), F32), compiler_params=_params())(f_raw, b_pad)


def _forget_bwd(dcum, f_raw, b_pad):
    s = f_raw.shape[0]

    def body(d_ref, f_ref, b_ref, df_ref, db_ref, tmp_ref):
        b = b_ref[...]
        _scan_rows(d_ref, tmp_ref, s, True)
        df = tmp_ref[...] * _sigmoid(-(f_ref[...] + b))
        df_ref[...] = df.astype(BF16)
        db_ref[...] = _colsum(df)

    vmem = pl.BlockSpec(memory_space=pltpu.VMEM)
    return pl.pallas_call(
        body, name="forget_bwd", in_specs=[vmem, vmem, vmem], out_specs=[vmem, vmem],
        out_shape=[jax.ShapeDtypeStruct((s, LANES), BF16), jax.ShapeDtypeStruct((1, LANES), F32)],
        scratch_shapes=[pltpu.VMEM((s, LANES), F32)], compiler_params=_params())(dcum, f_raw, b_pad)


def _dot_nt(a, b):
    return lax.dot_general(a, b, (((1,), (1,)), ((), ())), preferred_element_type=F32)


def _head_masks():
    lane = lax.broadcasted_iota(jnp.int32, (TQ, LANES), 1)
    return lane, [lane < HEAD_DIM, lane >= HEAD_DIM]


def _pick(mask, x):
    return jnp.where(mask, x, jnp.zeros_like(x))


def _qkv_specs(s, col0):
    nb = ATTN_W // LANES
    return [pl.BlockSpec((TQ, LANES), lambda hp, qi: (qi, col0 + hp)),
            pl.BlockSpec((s, LANES), lambda hp, qi: (0, col0 + nb + hp)),
            pl.BlockSpec((s, LANES), lambda hp, qi: (0, col0 + 2 * nb + hp))]


def _pair_spec():
    return pl.BlockSpec((TQ, LANES), lambda hp, qi: (qi, hp))


def _fox_fwd(qkv, cum_col, cum_row, ride=None):
    s = qkv.shape[0]
    nq = s // TQ

    def body(q_ref, k_ref, v_ref, cc_ref, cr_ref, o_ref, o32_ref, lse_ref):
        hp, qi = pl.program_id(0), pl.program_id(1)
        lane, masks = _head_masks()
        row = lax.broadcasted_iota(jnp.int32, (TQ, TQ), 0)
        col = lax.broadcasted_iota(jnp.int32, (TQ, TQ), 1)
        causal = col <= row
        q2 = q_ref[...] * jnp.asarray(ATTN_SCALE, BF16)
        cc = cc_ref[...]
        qms = [_pick(masks[e], q2) for e in range(2)]
        cqs = [jnp.sum(jnp.where(lane == 2 * hp + e, cc, 0.0), axis=1, keepdims=True) for e in range(2)]

        def tile(kb, carry, masked):
            off = pl.multiple_of(kb * TQ, TQ)
            k2, v2 = k_ref[pl.ds(off, TQ), :], v_ref[pl.ds(off, TQ), :]
            new = []
            for e in range(2):
                m, acc = carry[e]
                sc = _dot_nt(qms[e], k2) + (cqs[e] - cr_ref[e:e + 1, pl.ds(off, TQ)])
                if masked:
                    sc = jnp.where(causal, sc, -jnp.inf)
                m_new = jnp.maximum(m, jnp.max(sc, axis=1, keepdims=True))
                p = jnp.exp(sc - m_new)
                v_ones = jnp.where(masks[e], v2, jnp.ones_like(v2))
                acc = jnp.exp(m - m_new) * acc + jnp.dot(p.astype(BF16), v_ones, preferred_element_type=F32)
                new.append((m_new, acc))
            return tuple(new)

        init = (jnp.full((TQ, 1), -jnp.inf, F32), jnp.zeros((TQ, LANES), F32))
        carry = lax.fori_loop(0, qi, lambda kb, cr: tile(kb, cr, False), tile(qi, (init, init), True))
        sums = [jnp.max(jnp.where(masks[1 - e], carry[e][1], 0.0), axis=1, keepdims=True) for e in range(2)]
        outs = [carry[e][1] / sums[e] for e in range(2)]
        lses = [carry[e][0] + jnp.log(sums[e]) for e in range(2)]
        out = jnp.where(masks[0], outs[0], outs[1])
        o_ref[...] = out.astype(BF16)
        o32_ref[...] = out
        lse_ref[...] = jnp.where(masks[0], lses[0], lses[1])

    return _ride_call(
        body, ride, name="fox_fwd", grid=(N_HEADS // 2, nq),
        in_specs=_qkv_specs(s, 0) + [pl.BlockSpec((TQ, LANES), lambda hp, qi: (qi, 0)),
                                     pl.BlockSpec((None, 2, s), lambda hp, qi: (hp, 0, 0))],
        out_specs=[_pair_spec(), _pair_spec(), _pair_spec()],
        out_shape=[jax.ShapeDtypeStruct((s, ATTN_W), BF16), jax.ShapeDtypeStruct((s, ATTN_W), F32),
                   jax.ShapeDtypeStruct((s, ATTN_W), F32)],
        scratch_shapes=[], sem=("parallel", "parallel"), args=(qkv, qkv, qkv, cum_col, cum_row))


def _write_transposed(acc_ref, out_ref):
    for c in range(out_ref.shape[0] // TQ):
        out_ref[c * TQ:(c + 1) * TQ, :] = jnp.transpose(acc_ref[:, c * TQ:(c + 1) * TQ]).astype(BF16)


def _fox_bwd(qkv, cum_col, cum_row, o, lse, do, ride=None):
    s = qkv.shape[0]
    nq = s // TQ

    def body(q_ref, k_ref, v_ref, cc_ref, cr_ref, o_ref, lse_ref, do_ref,
             dq_ref, dk_ref, dv_ref, dcr_ref, dcq_ref, dk_acc, dv_acc):
        hp, qi = pl.program_id(0), pl.program_id(1)

        @pl.when(qi == 0)
        def _():
            dk_acc[...] = jnp.zeros_like(dk_acc)
            dv_acc[...] = jnp.zeros_like(dv_acc)
            dcr_ref[...] = jnp.zeros_like(dcr_ref)

        lane, masks = _head_masks()
        row = lax.broadcasted_iota(jnp.int32, (TQ, TQ), 0)
        col = lax.broadcasted_iota(jnp.int32, (TQ, TQ), 1)
        causal = col <= row
        q2 = q_ref[...] * jnp.asarray(ATTN_SCALE, BF16)
        do2 = do_ref[...]
        prod = do2.astype(F32) * o_ref[...].astype(F32)
        lse2 = lse_ref[...]
        cc = cc_ref[...]
        qms = [_pick(masks[e], q2) for e in range(2)]
        doms = [_pick(masks[e], do2) for e in range(2)]
        deltas = [jnp.sum(jnp.where(masks[e], prod, 0.0), axis=1, keepdims=True) for e in range(2)]
        lses = [jnp.max(jnp.where(masks[e], lse2, -jnp.inf), axis=1, keepdims=True) for e in range(2)]
        cqs = [jnp.sum(jnp.where(lane == 2 * hp + e, cc, 0.0), axis=1, keepdims=True) for e in range(2)]
        qts = [jnp.transpose(qms[e].astype(F32)).astype(BF16) for e in range(2)]
        dots = [jnp.transpose(doms[e].astype(F32)).astype(BF16) for e in range(2)]

        def tile(kb, carry, masked):
            off = pl.multiple_of(kb * TQ, TQ)
            k2, v2 = k_ref[pl.ds(off, TQ), :], v_ref[pl.ds(off, TQ), :]
            new, dk, dv = [], None, None
            for e in range(2):
                dq, rowsum = carry[e]
                sc = _dot_nt(qms[e], k2) + (cqs[e] - cr_ref[e:e + 1, pl.ds(off, TQ)])
                p = jnp.exp(sc - lses[e])
                if masked:
                    p = jnp.where(causal, p, 0.0)
                ds = p * (_dot_nt(doms[e], v2) - deltas[e])
                dsb = ds.astype(BF16)
                dk_e = jnp.dot(qts[e], dsb, preferred_element_type=F32)
                dv_e = jnp.dot(dots[e], p.astype(BF16), preferred_element_type=F32)
                dk, dv = (dk_e, dv_e) if e == 0 else (dk + dk_e, dv + dv_e)
                dcr_ref[e:e + 1, pl.ds(off, TQ)] -= _colsum(ds)
                new.append((dq + jnp.dot(dsb, k2, preferred_element_type=F32),
                            rowsum + jnp.sum(ds, axis=1, keepdims=True)))
            dk_acc[:, pl.ds(off, TQ)] += dk
            dv_acc[:, pl.ds(off, TQ)] += dv
            return tuple(new)

        init = (jnp.zeros((TQ, LANES), F32), jnp.zeros((TQ, 1), F32))
        carry = tile(qi, lax.fori_loop(0, qi, lambda kb, cr: tile(kb, cr, False), (init, init)), True)
        dq_ref[...] = (jnp.where(masks[0], carry[0][0], carry[1][0]) * ATTN_SCALE).astype(BF16)
        dcq_ref[...] = jnp.where(masks[0], carry[0][1], carry[1][1])

        @pl.when(qi == nq - 1)
        def _():
            _write_transposed(dk_acc, dk_ref)
            _write_transposed(dv_acc, dv_ref)

    seq_spec = pl.BlockSpec((s, LANES), lambda hp, qi: (0, hp))
    return _ride_call(
        body, ride, name="fox_bwd", grid=(N_HEADS // 2, nq),
        in_specs=_qkv_specs(s, 0) + [pl.BlockSpec((TQ, LANES), lambda hp, qi: (qi, 0)),
                                     pl.BlockSpec((None, 2, s), lambda hp, qi: (hp, 0, 0)),
                                     _pair_spec(), _pair_spec(), _pair_spec()],
        out_specs=[_pair_spec(), seq_spec, seq_spec, pl.BlockSpec((None, 2, s), lambda hp, qi: (hp, 0, 0)),
                   _pair_spec()],
        out_shape=[jax.ShapeDtypeStruct((s, ATTN_W), BF16)] * 3 + [jax.ShapeDtypeStruct((N_HEADS // 2, 2, s), F32),
                                                                    jax.ShapeDtypeStruct((s, ATTN_W), F32)],
        scratch_shapes=[pltpu.VMEM((LANES, s), F32), pltpu.VMEM((LANES, s), F32)],
        sem=("parallel", "arbitrary"), args=(qkv, qkv, qkv, cum_col, cum_row, o, lse, do))


def _scan_matrix(reverse):
    row = lax.broadcasted_iota(jnp.int32, (SCAN_W, SCAN_W), 0)
    col = lax.broadcasted_iota(jnp.int32, (SCAN_W, SCAN_W), 1)
    return jnp.where((row > col) if reverse else (row < col), 1.0, 0.0).astype(BF16)


def _scan_cols(x, tri, reverse, init):
    nblk = x.shape[1] // SCAN_W
    parts, total = [None] * nblk, init
    far = 0 if reverse else SCAN_W - 1
    for b in (reversed(range(nblk)) if reverse else range(nblk)):
        blk = x[:, b * SCAN_W:(b + 1) * SCAN_W]
        part = jnp.dot(blk.astype(BF16), tri, preferred_element_type=F32)
        parts[b] = part + total
        total = total + (part[:, far:far + 1] + blk[:, far:far + 1])
    return (parts[0] if nblk == 1 else jnp.concatenate(parts, axis=1)), total


def _sb_logits(qm, k2):
    z = _dot_nt(qm, k2)
    neg_abs = lax.bitcast_convert_type(lax.bitcast_convert_type(z, jnp.uint32) | jnp.uint32(0x80000000), F32)
    soft = jnp.log(1.0 + jnp.exp(neg_abs))
    lb = jnp.minimum(z, 0.0) - soft
    return lb, lb - z


def _tri_base(qi):
    return (qi * (qi + 1)) // 2


def _sb_fwd(qkv):
    s = qkv.shape[0]
    nq = s // TQ

    def body(q_ref, k_ref, v_ref, o_ref, t_ref, buf, sems):
        hp, qi = pl.program_id(0), pl.program_id(1)
        _, masks = _head_masks()
        row = lax.broadcasted_iota(jnp.int32, (TQ, TQ), 0)
        col = lax.broadcasted_iota(jnp.int32, (TQ, TQ), 1)
        strict = col < row
        suffix = _scan_matrix(True)
        q2 = q_ref[...] * jnp.asarray(ATTN_SCALE, BF16)
        qms = [_pick(masks[e], q2) for e in range(2)]
        base = _tri_base(qi)

        def store(e, slot, kb):
            return pltpu.make_async_copy(buf.at[e, slot], t_ref.at[2 * hp + e, base + kb], sems.at[e, slot])

        def tile(n, carry, masked):
            kb = qi - n
            off = pl.multiple_of(kb * TQ, TQ)
            k2, v2 = k_ref[pl.ds(off, TQ), :], v_ref[pl.ds(off, TQ), :]
            slot = n % 2
            new = []
            for e in range(2):
                run, acc = carry[e]
                lb, lo = _sb_logits(qms[e], k2)
                if masked:
                    lo = jnp.where(strict, lo, 0.0)
                rest, run = _scan_cols(lo, suffix, True, run)
                a = jnp.exp(lb + rest)
                if masked:
                    a = jnp.where(strict, a, 0.0)
                ab = a.astype(BF16)
                acc = acc + jnp.dot(ab, v2, preferred_element_type=F32)
                if not masked:
                    @pl.when(n >= 2)
                    def _(e=e, slot=slot, kb=kb):
                        store(e, slot, kb + 2).wait()
                buf[e, slot, 0] = ab
                buf[e, slot, 1] = lb.astype(BF16)
                store(e, slot, kb).start()
                new.append((run, acc))
            return tuple(new)

        init = (jnp.zeros((TQ, 1), F32), jnp.zeros((TQ, LANES), F32))
        carry = lax.fori_loop(0, qi, lambda it, cr: tile(it + 1, cr, False), tile(0, (init, init), True))
        for e in range(2):
            store(e, qi % 2, 0).wait()

            @pl.when(qi >= 1)
            def _(e=e):
                store(e, (qi - 1) % 2, 1).wait()
        o_ref[...] = jnp.where(masks[0], carry[0][1], carry[1][1]).astype(BF16)

    ntri = nq * (nq + 1) // 2
    return pl.pallas_call(
        body, name="sb_fwd", grid=(N_HEADS // 2, nq), in_specs=_qkv_specs(s, 3 * ATTN_W // LANES),
        out_specs=[_pair_spec(), ANY],
        out_shape=[jax.ShapeDtypeStruct((s, ATTN_W), BF16), jax.ShapeDtypeStruct((N_HEADS, ntri, 2, TQ, TQ), BF16)],
        scratch_shapes=[pltpu.VMEM((2, 2, 2, TQ, TQ), BF16), pltpu.SemaphoreType.DMA((2, 2))],
        compiler_params=_params(("arbitrary", "arbitrary")),
    )(qkv, qkv, qkv)


def _sb_bwd(qkv, tiles, do):
    s = qkv.shape[0]
    nq = s // TQ

    def body(q_ref, k_ref, v_ref, t_ref, do_ref, dq_ref, dk_ref, dv_ref, dk_acc, dv_acc, buf, sems):
        hp, qi = pl.program_id(0), pl.program_id(1)

        @pl.when(qi == 0)
        def _():
            dk_acc[...] = jnp.zeros_like(dk_acc)
            dv_acc[...] = jnp.zeros_like(dv_acc)

        _, masks = _head_masks()
        row = lax.broadcasted_iota(jnp.int32, (TQ, TQ), 0)
        col = lax.broadcasted_iota(jnp.int32, (TQ, TQ), 1)
        strict = col < row
        prefix = _scan_matrix(False)
        q2 = q_ref[...] * jnp.asarray(ATTN_SCALE, BF16)
        do2 = do_ref[...]
        qms = [_pick(masks[e], q2) for e in range(2)]
        doms = [_pick(masks[e], do2) for e in range(2)]
        qts = [jnp.transpose(qms[e].astype(F32)).astype(BF16) for e in range(2)]
        dots = [jnp.transpose(doms[e].astype(F32)).astype(BF16) for e in range(2)]
        base = _tri_base(qi)

        def fetch(e, kb):
            return pltpu.make_async_copy(t_ref.at[2 * hp + e, base + kb], buf.at[e, kb % 2], sems.at[e, kb % 2])

        for e in range(2):
            fetch(e, 0).start()

        def tile(kb, carry, masked):
            off = pl.multiple_of(kb * TQ, TQ)
            k2, v2 = k_ref[pl.ds(off, TQ), :], v_ref[pl.ds(off, TQ), :]
            new, dk, dv = [], None, None
            for e in range(2):
                gsum, dq = carry[e]
                if not masked:
                    fetch(e, kb + 1).start()
                fetch(e, kb).wait()
                ab = buf[e, kb % 2, 0]
                beta = jnp.exp(buf[e, kb % 2, 1].astype(F32))
                g = ab.astype(F32) * _dot_nt(doms[e], v2)
                before, gsum = _scan_cols(g, prefix, False, gsum)
                dz = g - beta * (g + before)
                if masked:
                    dz = jnp.where(strict, dz, 0.0)
                dzb = dz.astype(BF16)
                dk_e = jnp.dot(qts[e], dzb, preferred_element_type=F32)
                dv_e = jnp.dot(dots[e], ab, preferred_element_type=F32)
                dk, dv = (dk_e, dv_e) if e == 0 else (dk + dk_e, dv + dv_e)
                new.append((gsum, dq + jnp.dot(dzb, k2, preferred_element_type=F32)))
            dk_acc[:, pl.ds(off, TQ)] += dk
            dv_acc[:, pl.ds(off, TQ)] += dv
            return tuple(new)

        init = (jnp.zeros((TQ, 1), F32), jnp.zeros((TQ, LANES), F32))
        carry = tile(qi, lax.fori_loop(0, qi, lambda kb, cr: tile(kb, cr, False), (init, init)), True)
        dq_ref[...] = (jnp.where(masks[0], carry[0][1], carry[1][1]) * ATTN_SCALE).astype(BF16)

        @pl.when(qi == nq - 1)
        def _():
            _write_transposed(dk_acc, dk_ref)
            _write_transposed(dv_acc, dv_ref)

    seq_spec = pl.BlockSpec((s, LANES), lambda hp, qi: (0, hp))
    return pl.pallas_call(
        body, name="sb_bwd", grid=(N_HEADS // 2, nq),
        in_specs=_qkv_specs(s, 3 * ATTN_W // LANES) + [ANY, _pair_spec()],
        out_specs=[_pair_spec(), seq_spec, seq_spec],
        out_shape=[jax.ShapeDtypeStruct((s, ATTN_W), BF16)] * 3,
        scratch_shapes=[pltpu.VMEM((LANES, s), F32), pltpu.VMEM((LANES, s), F32),
                        pltpu.VMEM((2, 2, 2, TQ, TQ), BF16), pltpu.SemaphoreType.DMA((2, 2))],
        compiler_params=_params(("arbitrary", "arbitrary")),
    )(qkv, qkv, qkv, tiles, do)


CONV_TR = 256


def _shift_down(x, halo, n):
    rolled = pltpu.roll(x, n, 0)
    head = rolled[0:8, :]
    rid = lax.broadcasted_iota(jnp.int32, head.shape, 0)
    for j in range(n):
        head = jnp.where(rid == j, halo[8 - n + j:8 - n + j + 1, :], head)
    return jnp.concatenate([head, rolled[8:, :]], axis=0)


def _shift_up(x, halo, n):
    rows = x.shape[0]
    rolled = pltpu.roll(x, rows - n, 0)
    tail = rolled[rows - 8:, :]
    rid = lax.broadcasted_iota(jnp.int32, tail.shape, 0)
    for j in range(n):
        tail = jnp.where(rid == 8 - n + j, halo[j:j + 1, :], tail)
    return jnp.concatenate([rolled[:rows - 8, :], tail], axis=0)


def _conv_fwd_block(x, halo, w, b):
    return b + _shift_down(x, halo, 2) * w[0:1, :] + _shift_down(x, halo, 1) * w[1:2, :] + x * w[2:3, :]


def _conv_specs(tr, s):
    pair = 2 * FF_HALF
    blk = pl.BlockSpec((tr, pair), lambda j, i: (i, j))
    prev = pl.BlockSpec((8, pair), lambda j, i: (jnp.maximum(i * (tr // 8) - 1, 0), j))
    nxt = pl.BlockSpec((8, pair), lambda j, i: (jnp.minimum((i + 1) * (tr // 8), s // 8 - 1), j))
    return blk, prev, nxt


def _conv_gate_fwd(hpre, conv_w, conv_b):
    s = hpre.shape[0]
    tr = min(CONV_TR, s)
    blk, prev, _ = _conv_specs(tr, s)

    def body(x_ref, halo_ref, w_ref, b_ref, a_ref):
        i = pl.program_id(1)
        halo = jnp.where(i > 0, halo_ref[...], 0.0)
        h = _conv_fwd_block(x_ref[...], halo, w_ref[...], b_ref[...])
        hg, hv = h[:, :FF_HALF], h[:, FF_HALF:]
        a_ref[...] = (hg * _sigmoid(hg) * hv).astype(BF16)

    return pl.pallas_call(
        body, name="conv_gate_fwd", grid=(2, s // tr),
        in_specs=[blk, prev, pl.BlockSpec((3, 2 * FF_HALF), lambda j, i: (0, j)),
                  pl.BlockSpec((1, 2 * FF_HALF), lambda j, i: (0, j))],
        out_specs=pl.BlockSpec((tr, FF_HALF), lambda j, i: (i, j)),
        out_shape=jax.ShapeDtypeStruct((s, D_FF), BF16),
        compiler_params=_params(("parallel", "parallel")),
    )(hpre, hpre, conv_w, conv_b)


def _conv_gate_bwd(hpre, da, conv_w, conv_b):
    s = hpre.shape[0]
    tr = min(CONV_TR, s)
    blk, prev, _ = _conv_specs(tr, s)

    def body(x_ref, halo_ref, da_ref, w_ref, b_ref, dh_ref, db_ref, dw_ref):
        i = pl.program_id(1)
        halo = jnp.where(i > 0, halo_ref[...], 0.0)
        x = x_ref[...]
        h = _conv_fwd_block(x, halo, w_ref[...], b_ref[...])
        hg, hv = h[:, :FF_HALF], h[:, FF_HALF:]
        da_blk = da_ref[...].astype(F32)
        sg = _sigmoid(hg)
        dhg = da_blk * hv * (sg * (1.0 + hg * (1.0 - sg)))
        dhv = da_blk * (hg * sg)
        dh_ref[:, :FF_HALF] = dhg.astype(BF16)
        dh_ref[:, FF_HALF:] = dhv.astype(BF16)
        x2, x1 = _shift_down(x, halo, 2), _shift_down(x, halo, 1)
        parts = []
        for lo, dpart in ((0, dhg), (FF_HALF, dhv)):
            cols = slice(lo, lo + FF_HALF)
            parts.append((cols, _colsum(dpart), _colsum(dpart * x2[:, cols]), _colsum(dpart * x1[:, cols]),
                          _colsum(dpart * x[:, cols])))

        @pl.when(i == 0)
        def _():
            for cols, db, dw0, dw1, dw2 in parts:
                db_ref[:, cols] = db
                dw_ref[0:1, cols] = dw0
                dw_ref[1:2, cols] = dw1
                dw_ref[2:3, cols] = dw2

        @pl.when(i > 0)
        def _():
            for cols, db, dw0, dw1, dw2 in parts:
                db_ref[:, cols] += db
                dw_ref[0:1, cols] += dw0
                dw_ref[1:2, cols] += dw1
                dw_ref[2:3, cols] += dw2

    pair = 2 * FF_HALF
    return pl.pallas_call(
        body, name="conv_gate_bwd", grid=(2, s // tr),
        in_specs=[blk, prev, pl.BlockSpec((tr, FF_HALF), lambda j, i: (i, j)),
                  pl.BlockSpec((3, pair), lambda j, i: (0, j)), pl.BlockSpec((1, pair), lambda j, i: (0, j))],
        out_specs=[blk, pl.BlockSpec((1, pair), lambda j, i: (0, j)), pl.BlockSpec((3, pair), lambda j, i: (0, j))],
        out_shape=[jax.ShapeDtypeStruct((s, 2 * D_FF), BF16), jax.ShapeDtypeStruct((1, 2 * D_FF), F32),
                   jax.ShapeDtypeStruct((3, 2 * D_FF), F32)],
        compiler_params=_params(("parallel", "arbitrary")),
    )(hpre, hpre, da, conv_w, conv_b)


def _conv_input_bwd(dh, conv_w):
    s = dh.shape[0]
    tr = min(CONV_TR, s)
    blk, _, _ = _conv_specs(tr, s)
    nblk = s // tr

    def body(x_ref, halo_ref, w_ref, o_ref):
        i = pl.program_id(1)
        halo = jnp.where(i < nblk - 1, halo_ref[...].astype(F32), 0.0)
        x, w = x_ref[...].astype(F32), w_ref[...]
        o_ref[...] = (x * w[2:3, :] + _shift_up(x, halo, 1) * w[1:2, :] + _shift_up(x, halo, 2) * w[0:1, :]).astype(BF16)

    nxt = pl.BlockSpec((16, 2 * FF_HALF), lambda j, i: (jnp.minimum((i + 1) * (tr // 16), s // 16 - 1), j))
    return pl.pallas_call(
        body, name="conv_input_bwd", grid=(2, nblk),
        in_specs=[blk, nxt, pl.BlockSpec((3, 2 * FF_HALF), lambda j, i: (0, j))], out_specs=blk,
        out_shape=jax.ShapeDtypeStruct((s, 2 * D_FF), BF16),
        compiler_params=_params(("parallel", "parallel")),
    )(dh, dh, conv_w)


def _adamw_math(w, g, m, v):
    m = ADAM_B1 * m + (1.0 - ADAM_B1) * g
    v = ADAM_B2 * v + (1.0 - ADAM_B2) * (g * g)
    m_hat = m / (1.0 - ADAM_B1 ** ADAM_STEP)
    v_hat = v / (1.0 - ADAM_B2 ** ADAM_STEP)
    delta = -ADAM_LR * (m_hat / (jnp.sqrt(v_hat) + ADAM_EPS) + ADAM_WD * w)
    return delta, m, v


def _adamw(name, g8, w, m, v):
    r, c = w.shape
    tr = _row_tile(r, c)

    def body(g_ref, w_ref, m_ref, v_ref, go_ref, d_ref, mo_ref, vo_ref):
        g = g_ref[0].astype(F32)
        for d in range(1, N_DEV):
            g = g + g_ref[d].astype(F32)
        delta, mn, vn = _adamw_math(w_ref[...], g, m_ref[...], v_ref[...])
        go_ref[...] = g
        d_ref[...] = delta
        mo_ref[...] = mn
        vo_ref[...] = vn

    spec = pl.BlockSpec((tr, c), lambda i: (i, 0))
    return pl.pallas_call(
        body, name=name, grid=(r // tr,),
        in_specs=[pl.BlockSpec((N_DEV, tr, c), lambda i: (0, i, 0)), spec, spec, spec], out_specs=[spec] * 4,
        out_shape=[jax.ShapeDtypeStruct((r, c), F32)] * 4, compiler_params=_params(("parallel",)),
    )(g8, w, m, v)


def _adamw_ada(c_t, dmod, w, m, v):
    r, c = w.shape
    tr = _row_tile(r, c)

    def body(ct_ref, dm_ref, w_ref, m_ref, v_ref, go_ref, d_ref, mo_ref, vo_ref):
        ct, dm = ct_ref[...], dm_ref[...]
        g = ct[:, 0:1] * dm[0:1, :]
        for b in range(1, N_DEV):
            g = g + ct[:, b:b + 1] * dm[b:b + 1, :]
        delta, mn, vn = _adamw_math(w_ref[...], g, m_ref[...], v_ref[...])
        go_ref[...] = g
        d_ref[...] = delta
        mo_ref[...] = mn
        vo_ref[...] = vn

    spec = pl.BlockSpec((tr, c), lambda i: (i, 0))
    return pl.pallas_call(
        body, name="adamw_w_ada", grid=(r // tr,),
        in_specs=[pl.BlockSpec((tr, N_DEV), lambda i: (i, 0)), pl.BlockSpec((N_DEV, c), lambda i: (0, 0)),
                  spec, spec, spec],
        out_specs=[spec] * 4, out_shape=[jax.ShapeDtypeStruct((r, c), F32)] * 4,
        compiler_params=_params(("parallel",)),
    )(c_t, dmod, w, m, v)


def _cols_from_slots(g):
    n, r, c = g.shape
    return jnp.transpose(g, (1, 0, 2)).reshape(r, n * c)


def _cols_to_slots(w):
    r, c = w.shape
    return jnp.transpose(w.reshape(r, N_DEV, c // N_DEV), (1, 0, 2))


def _pair_cols(w):
    g0, g1 = w[..., 0:FF_HALF], w[..., FF_HALF:D_FF]
    v0, v1 = w[..., D_FF:D_FF + FF_HALF], w[..., D_FF + FF_HALF:]
    return jnp.concatenate([g0, v0, g1, v1], axis=-1)


def _unpair_cols(w):
    g0, v0 = w[..., 0:FF_HALF], w[..., FF_HALF:D_FF]
    g1, v1 = w[..., D_FF:D_FF + FF_HALF], w[..., D_FF + FF_HALF:]
    return jnp.concatenate([g0, g1, v0, v1], axis=-1)


def _row(v):
    return v.reshape(1, -1)


def kernel(x, c, w_ada, b_ada, w_in, b_forget, w_fox_proj, w_sb_proj, w_o, ln1_g, ln1_b, w_up, conv_w, conv_b, w_down, ln2_g, ln2_b, loss_target, m_w_ada, m_b_ada, m_w_in, m_b_forget, m_w_fox_proj, m_w_sb_proj, m_w_o, m_ln1_g, m_ln1_b, m_w_up, m_conv_w, m_conv_b, m_w_down, m_ln2_g, m_ln2_b, v_w_ada, v_b_ada, v_w_in, v_b_forget, v_w_fox_proj, v_w_sb_proj, v_w_o, v_ln1_g, v_ln1_b, v_w_up, v_conv_w, v_conv_b, v_w_down, v_ln2_g, v_ln2_b):
    s = x.shape[1]
    me = 4 * lax.axis_index("x") + 2 * lax.axis_index("y") + lax.axis_index("c")
    x2 = x.reshape(s, D_MODEL)
    tgt = loss_target.reshape(s, D_MODEL)

    b_ada_loc = lax.dynamic_slice(b_ada, (me * ADA_SHARD,), (ADA_SHARD,)).reshape(1, ADA_SHARD)
    c_all, mod = _mod_exchange(c, w_ada, b_ada_loc)
    mod = mod.reshape(N_MOD, 1, D_MODEL)
    sh1, sc1, gt1, sh2, sc2, gt2 = [mod[i] for i in range(N_MOD)]

    (g_in,) = _exchange("ag_w_in", [w_in.astype(BF16)], scatter=False)
    late_weights = _Ride([w_fox_proj.astype(BF16), w_sb_proj.astype(BF16), w_o.astype(BF16), w_up.astype(BF16),
                          w_down.astype(BF16), conv_w], scatter=False)
    w_in_f = _cols_from_slots(g_in)
    w_proj = jnp.concatenate(
        [w_in_f[:, 0:1536], w_in_f[:, 1544:3080], w_in_f[:, 3080:5128], w_in_f[:, 1536:1544],
         jnp.zeros((D_MODEL, W_PROJ - 5128), BF16)], axis=1)
    w_qkv, w_gates, w_f = w_proj[:, :W_QKV], w_proj[:, W_QKV:W_QKV + W_GATES], w_proj[:, W_QKV + W_GATES:W_QKV + W_GATES + W_F]
    conv_b_p = _pair_cols(_row(conv_b))
    b_f_pad = jnp.pad(_row(b_forget), ((0, 0), (0, LANES - N_HEADS)))

    (u1,) = _rowwise("modulate1", lambda xb, sc, sh: (xb * (1.0 + sc) + sh,),
                     [(x2, D_MODEL, 0)], [sc1, sh1], [(D_MODEL, BF16)], tr=512)
    qkv = _mm(u1, w_qkv, name="mm_qkv", out_dtype=BF16)
    gates = _mm(u1, w_gates, name="mm_gates")
    f_raw = _mm(u1, w_f, name="mm_forget")
    cum_col = _forget_cumsum(f_raw, b_f_pad)
    cum_row = jnp.transpose(cum_col[:, :N_HEADS]).reshape(N_HEADS // 2, 2, s)
    (y_fox, y_fox32, lse), (g_fox, g_sb, g_o, g_up, g_down, g_cw) = _fox_fwd(qkv, cum_col, cum_row, ride=late_weights)
    w_fox_f = _cols_from_slots(g_fox)
    w_sb_f = _cols_from_slots(g_sb)
    w_o_f = g_o.reshape(D_MODEL, D_MODEL)
    w_up_p = _pair_cols(_cols_from_slots(g_up))
    w_down_f = g_down.reshape(D_FF, D_MODEL)
    conv_w_p = _pair_cols(_cols_from_slots(g_cw))
    y_sb, sb_run = _sb_fwd(qkv)
    pf = _mm(y_fox, w_fox_f, name="mm_fox_proj")
    ps = _mm(y_sb, w_sb_f, name="mm_sb_proj")
    (merged,) = _rowwise("gate_merge", lambda ga, gb, a, b: (_sigmoid(ga) * a + _sigmoid(gb) * b,),
                         [(gates, D_MODEL, 0), (gates, D_MODEL, 1), (pf, D_MODEL, 0), (ps, D_MODEL, 0)], [],
                         [(D_MODEL, BF16)])
    attn_out = _mm(merged, w_o_f, name="mm_w_o")

    def ln_fwd(xb, fb, gt, g, b):
        xhat, _ = _ln_stats(ALPHA * xb + (1.0 + gt) * fb)
        return xhat * g + b

    def ln1_mod(xb, fb, gt, g, b, sc, sh):
        y = ln_fwd(xb, fb, gt, g, b)
        return y, y * (1.0 + sc) + sh

    x1, u2 = _rowwise("ln1_modulate2", ln1_mod, [(x2, D_MODEL, 0), (attn_out, D_MODEL, 0)],
                      [gt1, _row(ln1_g), _row(ln1_b), sc2, sh2], [(D_MODEL, F32), (D_MODEL, BF16)])

    hpre = _mm(u2, w_up_p, name="mm_w_up", tn=1408)
    act = _conv_gate_fwd(hpre, conv_w_p, conv_b_p)
    ffn_out = _mm(act, w_down_f, name="mm_w_down", tk=2816)

    def ln2_bwd(xb, fb, tb, gt, g, b):
        xhat, rstd = _ln_stats(ALPHA * xb + (1.0 + gt) * fb)
        err = (xhat * g + b) - tb
        dy = err * (1.0 / D_MODEL)
        dr = _ln_bwd(dy, xhat, rstd, g)
        return (dr * (1.0 + gt), ALPHA * dr,
                _colsum(err * err), _colsum(dy * xhat), _colsum(dy), _colsum(dr * fb))

    dffn, dx1_res, sq_err, d_ln2_g, d_ln2_b, d_gt2 = _rowwise(
        "ln2_bwd", ln2_bwd, [(x1, D_MODEL, 0), (ffn_out, D_MODEL, 0), (tgt, D_MODEL, 0)],
        [gt2, _row(ln2_g), _row(ln2_b)], [(D_MODEL, BF16), (D_MODEL, F32)], sums=[D_MODEL] * 4)
    loss = lax.psum(0.5 * jnp.sum(sq_err) / D_MODEL, ("x", "y", "c"))

    d_w_down = _mm(act, dffn, name="mm_d_w_down", ta=True, tm=1408, tk=2048, out_dtype=BF16)
    d_act = _mm(dffn, w_down_f, name="mm_d_act", tb=True, tn=1408, out_dtype=BF16)
    dh, d_conv_b_p, d_conv_w_p = _conv_gate_bwd(hpre, d_act, conv_w_p, conv_b_p)
    dhpre = _conv_input_bwd(dh, conv_w_p)
    d_w_up_p = _mm(u2, dhpre, name="mm_d_w_up", ta=True, tn=1408, tk=2048, out_dtype=BF16)
    du2 = _mm(dhpre, w_up_p, name="mm_d_u2", tb=True, tk=2816)

    def ln1_bwd(du, dres, x1b, xb, fb, sc, gt, g):
        dx1 = dres + du * (1.0 + sc)
        xhat, rstd = _ln_stats(ALPHA * xb + (1.0 + gt) * fb)
        dr = _ln_bwd(dx1, xhat, rstd, g)
        return (dr * (1.0 + gt), ALPHA * dr,
                _colsum(du * x1b), _colsum(du), _colsum(dx1 * xhat), _colsum(dx1), _colsum(dr * fb))

    d_attn, dx_res, d_sc2, d_sh2, d_ln1_g, d_ln1_b, d_gt1 = _rowwise(
        "ln1_bwd", ln1_bwd,
        [(du2, D_MODEL, 0), (dx1_res, D_MODEL, 0), (x1, D_MODEL, 0), (x2, D_MODEL, 0), (attn_out, D_MODEL, 0)],
        [sc2, gt1, _row(ln1_g)], [(D_MODEL, BF16), (D_MODEL, F32)], sums=[D_MODEL] * 5)

    d_w_o = _mm(merged, d_attn, name="mm_d_w_o", ta=True, out_dtype=BF16)
    d_merged = _mm(d_attn, w_o_f, name="mm_d_merged", tb=True)

    def merge_bwd(dm, ga, gb, a, b):
        sa, sb = _sigmoid(ga), _sigmoid(gb)
        return dm * a * sa * (1.0 - sa), dm * b * sb * (1.0 - sb), dm * sa, dm * sb

    d_ga, d_gb, d_pf, d_ps = _rowwise(
        "gate_merge_bwd", merge_bwd,
        [(d_merged, D_MODEL, 0), (gates, D_MODEL, 0), (gates, D_MODEL, 1), (pf, D_MODEL, 0), (ps, D_MODEL, 0)], [],
        [(D_MODEL, BF16)] * 4)
    d_w_fox = _mm(y_fox, d_pf, name="mm_d_w_fox", ta=True, out_dtype=BF16)
    d_w_sb = _mm(y_sb, d_ps, name="mm_d_w_sb", ta=True, out_dtype=BF16)
    d_y_fox = _mm(d_pf, w_fox_f, name="mm_d_y_fox", tb=True, out_dtype=BF16)
    d_y_sb = _mm(d_ps, w_sb_f, name="mm_d_y_sb", tb=True, out_dtype=BF16)
    early_grads = _Ride(
        [_cols_to_slots(d_w_fox), _cols_to_slots(d_w_sb), d_w_o.reshape(N_DEV, D_MODEL // N_DEV, D_MODEL),
         _cols_to_slots(_unpair_cols(d_w_up_p)), d_w_down.reshape(N_DEV, D_FF // N_DEV, D_MODEL)], scatter=True)
    (dq_a, dk_a, dv_a, d_cum_row, d_cum_q), early_slots = _fox_bwd(qkv, cum_col, cum_row, y_fox32, lse, d_y_fox,
                                                                    ride=early_grads)
    dq_b, dk_b, dv_b = _sb_bwd(qkv, sb_run, d_y_sb)
    d_cum = jnp.transpose(d_cum_row.reshape(N_HEADS, s)) + d_cum_q[:, ::HEAD_DIM]
    d_cum = jnp.pad(d_cum, ((0, 0), (0, LANES - N_HEADS)))
    d_f, d_b_forget = _forget_bwd(d_cum, f_raw, b_f_pad)
    d_proj = jnp.concatenate([dq_a, dk_a, dv_a, dq_b, dk_b, dv_b, d_ga, d_gb, d_f,
                              jnp.zeros((s, W_PROJ - W_QKV - W_GATES - W_F), BF16)], axis=1)
    d_w_proj = _mm(u1, d_proj, name="mm_d_w_in", ta=True, tn=896, tk=2048, out_dtype=BF16)
    d_w_in_f = jnp.concatenate([d_w_proj[:, 0:1536], d_w_proj[:, 5120:5128], d_w_proj[:, 1536:3072],
                                d_w_proj[:, 3072:5120]], axis=1)
    du1, (in_slots,) = _mm(d_proj, w_proj, name="mm_d_u1", tb=True, tk=1792,
                           ride=_Ride([_cols_to_slots(d_w_in_f)], scatter=True))

    def x_bwd(du, dres, xb, sc):
        return dres + du * (1.0 + sc), _colsum(du * xb), _colsum(du)

    grad_x, d_sc1, d_sh1 = _rowwise("x_bwd", x_bwd, [(du1, D_MODEL, 0), (dx_res, D_MODEL, 0), (x2, D_MODEL, 0)],
                                    [sc1], [(D_MODEL, F32)], sums=[D_MODEL] * 2, tr=512)

    d_conv_b = _unpair_cols(d_conv_b_p)
    d_conv_w = _unpair_cols(d_conv_w_p)
    n_rep = N_MOD * D_MODEL + LANES + 4 * D_MODEL + 2 * D_FF
    small = jnp.concatenate(
        [d_sh1, d_sc1, d_gt1, d_sh2, d_sc2, d_gt2, d_b_forget, d_ln1_g, d_ln1_b, d_ln2_g, d_ln2_b, d_conv_b,
         d_conv_w.reshape(1, 6 * D_FF)], axis=1)
    n_small = small.shape[1] // LANES
    small = jnp.pad(small.reshape(n_small, LANES), ((0, 264 - n_small), (0, 0)))
    (small_all,) = _exchange("ag_small_grads", [small], scatter=False)
    rep8 = small_all[:, :n_rep // LANES, :]
    cw8 = small_all[:, n_rep // LANES:n_small, :].reshape(N_DEV, 3, 2 * D_FF)
    cw8 = lax.dynamic_slice(cw8, (0, 0, me * UP_SHARD), (N_DEV, 3, UP_SHARD))
    dmod8 = small_all[:, :N_MOD * D_MODEL // LANES, :].reshape(N_DEV, N_MOD * D_MODEL)
    dmod_loc = lax.dynamic_slice(dmod8, (0, me * ADA_SHARD), (N_DEV, ADA_SHARD))

    def pack_rep(b_a, b_f, g1, b1, g2, b2, cb):
        flat = jnp.concatenate([b_a, jnp.pad(b_f, (0, LANES - N_HEADS)), g1, b1, g2, b2, cb])
        return flat.reshape(n_rep // LANES, LANES)

    rep = _adamw("adamw_small", rep8, pack_rep(b_ada, b_forget, ln1_g, ln1_b, ln2_g, ln2_b, conv_b),
                 pack_rep(m_b_ada, m_b_forget, m_ln1_g, m_ln1_b, m_ln2_g, m_ln2_b, m_conv_b),
                 pack_rep(v_b_ada, v_b_forget, v_ln1_g, v_ln1_b, v_ln2_g, v_ln2_b, v_conv_b))

    def unpack_rep(p):
        flat = p.reshape(-1)
        o = N_MOD * D_MODEL
        return {"b_ada": flat[:o], "b_forget": flat[o:o + N_HEADS],
                "ln1_g": flat[o + 128:o + 1152], "ln1_b": flat[o + 1152:o + 2176],
                "ln2_g": flat[o + 2176:o + 3200], "ln2_b": flat[o + 3200:o + 4224], "conv_b": flat[o + 4224:]}

    rep = [unpack_rep(p) for p in rep]
    r_conv_w = _adamw("adamw_conv_w", cw8, conv_w, m_conv_w, v_conv_w)
    r_ada = _adamw_ada(jnp.transpose(c_all.reshape(N_DEV, D_MODEL)), dmod_loc, w_ada, m_w_ada, v_w_ada)

    r_in = _adamw("adamw_w_in", in_slots, w_in, m_w_in, v_w_in)
    r_fox = _adamw("adamw_w_fox", early_slots[0], w_fox_proj, m_w_fox_proj, v_w_fox_proj)
    r_sb = _adamw("adamw_w_sb", early_slots[1], w_sb_proj, m_w_sb_proj, v_w_sb_proj)
    r_o = _adamw("adamw_w_o", early_slots[2], w_o, m_w_o, v_w_o)
    r_up = _adamw("adamw_w_up", early_slots[3], w_up, m_w_up, v_w_up)
    r_down = _adamw("adamw_w_down", early_slots[4], w_down, m_w_down, v_w_down)

    def leaf(i):
        return [r_ada[i], rep[i]["b_ada"], r_in[i], rep[i]["b_forget"], r_fox[i], r_sb[i], r_o[i], rep[i]["ln1_g"],
                rep[i]["ln1_b"], r_up[i], r_conv_w[i], rep[i]["conv_b"], r_down[i], rep[i]["ln2_g"], rep[i]["ln2_b"]]

    return (loss, grad_x.reshape(1, s, D_MODEL), *leaf(0), *leaf(1), *leaf(2), *leaf(3))
```

```python
import functools

import jax
import jax.numpy as jnp
from jax import lax
from jax.experimental import pallas as pl
from jax.experimental.pallas import tpu as pltpu

F32 = jnp.float32
BF16 = jnp.bfloat16
MESH = pl.DeviceIdType.MESH
ANY = pl.BlockSpec(memory_space=pl.ANY)

N_DEV = 8
D_MODEL = 1024
HEAD_DIM = 64
N_HEADS = 8
ATTN_W = N_HEADS * HEAD_DIM
D_FF = 2816
FF_HALF = D_FF // 2
N_MOD = 6
ADA_SHARD = N_MOD * D_MODEL // N_DEV
IN_SHARD = 641
UP_SHARD = 704
ATTN_SCALE = HEAD_DIM ** -0.5
ALPHA = 2.0 ** 0.25
LN_EPS = 1e-5
LANES = 128
TQ = 512
SCAN_W = 256
VMEM_LIMIT = 56 * 1024 * 1024

ADAM_LR, ADAM_B1, ADAM_B2, ADAM_EPS, ADAM_WD, ADAM_STEP = 0.001, 0.9, 0.999, 1e-08, 0.01, 10

W_QKV, W_GATES, W_F = 3072, 2048, 128
W_PROJ = 5376


def _params(sem=None):
    return pltpu.CompilerParams(dimension_semantics=sem, vmem_limit_bytes=VMEM_LIMIT)


def _tile(n, cap):
    if n <= cap:
        return n
    best = None
    for t in range(LANES, cap + 1, LANES):
        if n % t == 0:
            best = t
    assert best is not None, (n, cap)
    return best


def _row_tile(r, width, budget=192 * 1024):
    if r * width <= budget or r % 16:
        return r
    best = 16
    for t in range(16, r + 1, 16):
        if r % t == 0 and t * width <= budget:
            best = t
    return best


def _me():
    x, y, c = lax.axis_index("x"), lax.axis_index("y"), lax.axis_index("c")
    return x, y, c, 4 * x + 2 * y + c


def _peer(r):
    x, y, c, _ = _me()
    px = 1 - x if r & 4 else x
    py = 1 - y if r & 2 else y
    pc = 1 - c if r & 1 else c
    return (px, py, pc), 4 * px + 2 * py + pc


class _Ride:
    def __init__(self, arrays, scatter):
        self.arrays, self.scatter, self.n = list(arrays), scatter, len(arrays)
        self.in_specs = [ANY] * self.n
        self.out_specs = [ANY] * self.n
        self.out_shape = [jax.ShapeDtypeStruct(a.shape if scatter else (N_DEV,) + a.shape, a.dtype) for a in arrays]
        self.scratch = [pltpu.SemaphoreType.DMA((self.n, N_DEV - 1)), pltpu.SemaphoreType.DMA((self.n, N_DEV - 1)),
                        pltpu.SemaphoreType.DMA((self.n,))]

    def _local(self, ins, outs, sems, a):
        me = _me()[3]
        return pltpu.make_async_copy(ins[a].at[me] if self.scatter else ins[a], outs[a].at[me], sems[2].at[a])

    def _remote(self, ins, outs, sems, a, r, arriving):
        me = _me()[3]
        peer, pidx = _peer(r)
        src = ins[a].at[me if arriving else pidx] if self.scatter else ins[a]
        return pltpu.make_async_remote_copy(
            src_ref=src, dst_ref=outs[a].at[pidx if arriving else me], send_sem=sems[0].at[a, r - 1],
            recv_sem=sems[1].at[a, r - 1], device_id=peer, device_id_type=MESH)

    def start(self, ins, outs, sems):
        for a in range(self.n):
            self._local(ins, outs, sems, a).start()
        for r in range(1, N_DEV):
            for a in range(self.n):
                self._remote(ins, outs, sems, a, r, False).start()

    def wait(self, ins, outs, sems):
        for r in range(1, N_DEV):
            for a in range(self.n):
                self._remote(ins, outs, sems, a, r, True).wait_recv()
        for r in range(1, N_DEV):
            for a in range(self.n):
                self._remote(ins, outs, sems, a, r, False).wait_send()
        for a in range(self.n):
            self._local(ins, outs, sems, a).wait()


def _exchange(name, arrays, scatter):
    ride = _Ride(arrays, scatter)

    def body(*refs):
        ins, outs, sems = refs[:ride.n], refs[ride.n:2 * ride.n], refs[2 * ride.n:]
        ride.start(ins, outs, sems)
        ride.wait(ins, outs, sems)

    return pl.pallas_call(body, name=name, in_specs=ride.in_specs, out_specs=ride.out_specs, out_shape=ride.out_shape,
                          scratch_shapes=ride.scratch)(*arrays)


def _with_ride(body, ride, n_in, n_out, grid):
    if ride is None:
        return body
    n = ride.n

    def wrapped(*refs):
        ins, rins = refs[:n_in], refs[n_in:n_in + n]
        outs, routs = refs[n_in + n:n_in + n + n_out], refs[n_in + n + n_out:n_in + 2 * n + n_out]
        rest = refs[n_in + 2 * n + n_out:]
        scratch, sems = rest[:len(rest) - 3], rest[len(rest) - 3:]
        ids = [pl.program_id(d) for d in range(len(grid))]
        first = functools.reduce(lambda p, q: p & q, [i == 0 for i in ids])
        last = functools.reduce(lambda p, q: p & q, [i == g - 1 for i, g in zip(ids, grid)])

        @pl.when(first)
        def _():
            ride.start(rins, routs, sems)

        body(*ins, *outs, *scratch)

        @pl.when(last)
        def _():
            ride.wait(rins, routs, sems)

    return wrapped


def _ride_call(body, ride, *, name, grid, in_specs, out_specs, out_shape, scratch_shapes, sem, args):
    n_in, n_out = len(in_specs), len(out_specs)
    if ride is None:
        res = pl.pallas_call(body, name=name, grid=grid, in_specs=in_specs, out_specs=out_specs, out_shape=out_shape,
                             scratch_shapes=scratch_shapes, compiler_params=_params(sem))(*args)
        return list(res), []
    res = pl.pallas_call(
        _with_ride(body, ride, n_in, n_out, grid), name=name, grid=grid,
        in_specs=list(in_specs) + ride.in_specs, out_specs=list(out_specs) + ride.out_specs,
        out_shape=list(out_shape) + ride.out_shape, scratch_shapes=list(scratch_shapes) + ride.scratch,
        compiler_params=_params(("arbitrary",) * len(grid)))(*args, *ride.arrays)
    return list(res[:n_out]), list(res[n_out:])


def _mm(a, b, *, name, ta=False, tb=False, out_dtype=F32, tm=1024, tn=1024, tk=1024, ride=None):
    m, k = (a.shape[1], a.shape[0]) if ta else a.shape
    n = b.shape[0] if tb else b.shape[1]
    assert (b.shape[1] if tb else b.shape[0]) == k
    tm, tn, tk = _tile(m, tm), _tile(n, tn), _tile(k, tk)
    nk = k // tk
    a_spec = pl.BlockSpec((tk, tm), lambda i, j, l: (l, i)) if ta else pl.BlockSpec((tm, tk), lambda i, j, l: (i, l))
    b_spec = pl.BlockSpec((tn, tk), lambda i, j, l: (j, l)) if tb else pl.BlockSpec((tk, tn), lambda i, j, l: (l, j))
    dims = (((0,) if ta else (1,), (1,) if tb else (0,)), ((), ()))

    def body(a_ref, b_ref, o_ref, *acc):
        p = lax.dot_general(a_ref[...].astype(BF16), b_ref[...].astype(BF16), dims, preferred_element_type=F32)
        if nk == 1:
            o_ref[...] = p.astype(out_dtype)
            return
        acc_ref = acc[0]
        step = pl.program_id(2)

        @pl.when(step == 0)
        def _():
            acc_ref[...] = p

        @pl.when(step > 0)
        def _():
            acc_ref[...] += p

        @pl.when(step == nk - 1)
        def _():
            o_ref[...] = acc_ref[...].astype(out_dtype)

    outs, rode = _ride_call(
        body, ride, name=name, grid=(m // tm, n // tn, nk), in_specs=[a_spec, b_spec],
        out_specs=[pl.BlockSpec((tm, tn), lambda i, j, l: (i, j))], out_shape=[jax.ShapeDtypeStruct((m, n), out_dtype)],
        scratch_shapes=[] if nk == 1 else [pltpu.VMEM((tm, tn), F32)], sem=("parallel", "parallel", "arbitrary"),
        args=(a, b))
    return outs[0] if ride is None else (outs[0], rode)


def _rowwise(name, fn, rows, vecs, outs, sums=(), tr=256):
    s = rows[0][0].shape[0]
    tr = min(tr, s)
    nr, nv, no = len(rows), len(vecs), len(outs)

    def body(*refs):
        vals = [r[...] for r in refs[:nr + nv]]
        res = fn(*vals)
        for o_ref, val in zip(refs[nr + nv:nr + nv + no], res[:no]):
            o_ref[...] = val.astype(o_ref.dtype)
        step = pl.program_id(0)
        for s_ref, val in zip(refs[nr + nv + no:], res[no:]):
            @pl.when(step == 0)
            def _(s_ref=s_ref, val=val):
                s_ref[...] = val

            @pl.when(step > 0)
            def _(s_ref=s_ref, val=val):
                s_ref[...] += val

    in_specs = [pl.BlockSpec((tr, w), functools.partial(lambda i, cb: (i, cb), cb=cb)) for _, w, cb in rows]
    in_specs += [pl.BlockSpec(v.shape, lambda i: (0, 0)) for v in vecs]
    out_specs = [pl.BlockSpec((tr, w), lambda i: (i, 0)) for w, _ in outs]
    out_specs += [pl.BlockSpec((1, w), lambda i: (0, 0)) for w in sums]
    out_shape = [jax.ShapeDtypeStruct((s, w), dt) for w, dt in outs]
    out_shape += [jax.ShapeDtypeStruct((1, w), F32) for w in sums]
    return pl.pallas_call(
        body, name=name, grid=(s // tr,), in_specs=in_specs, out_specs=out_specs, out_shape=out_shape,
        compiler_params=_params(("arbitrary",) if sums else ("parallel",)),
    )(*[r[0] for r in rows], *vecs)


def _colsum(x):
    return jnp.sum(x, axis=0, keepdims=True)


def _sigmoid(x):
    return 1.0 / (1.0 + jnp.exp(-x))


def _log_sigmoid(x):
    return jnp.minimum(x, 0.0) - jnp.log(1.0 + jnp.exp(-jnp.abs(x)))


def _ln_stats(r):
    mu = jnp.mean(r, axis=-1, keepdims=True)
    xc = r - mu
    var = jnp.mean(xc * xc, axis=-1, keepdims=True)
    rstd = lax.rsqrt(var + LN_EPS)
    return xc * rstd, rstd


def _ln_bwd(dy, xhat, rstd, g):
    dxh = dy * g
    m1 = jnp.mean(dxh, axis=-1, keepdims=True)
    m2 = jnp.mean(dxh * xhat, axis=-1, keepdims=True)
    return rstd * (dxh - m1 - xhat * m2)


def _mod_exchange(c_row, w_ada, b_ada_loc):
    def body(c_ref, w_ref, b_ref, call_ref, mod_ref, piece_ref, send_sems, recv_sems):
        me = _me()[3]
        call_ref[me] = c_ref[...]
        sent = []
        for r in range(1, N_DEV):
            peer, _ = _peer(r)
            cp = pltpu.make_async_remote_copy(
                src_ref=c_ref, dst_ref=call_ref.at[me], send_sem=send_sems.at[0, r - 1],
                recv_sem=recv_sems.at[0, r - 1], device_id=peer, device_id_type=MESH)
            cp.start()
            sent.append(cp)
        for r in range(1, N_DEV):
            peer, pidx = _peer(r)
            pltpu.make_async_remote_copy(
                src_ref=c_ref, dst_ref=call_ref.at[pidx], send_sem=send_sems.at[0, r - 1],
                recv_sem=recv_sems.at[0, r - 1], device_id=peer, device_id_type=MESH).wait_recv()
        c_all = jnp.concatenate([call_ref[d] for d in range(N_DEV)], axis=0)
        mod_loc = jnp.dot(c_all, w_ref[...], preferred_element_type=F32,
                          precision=lax.Precision.HIGHEST) + b_ref[...]
        for d in range(N_DEV):
            piece_ref[d] = mod_loc[d:d + 1, :]
        mod_ref[me] = piece_ref[me]
        for r in range(1, N_DEV):
            peer, pidx = _peer(r)
            cp = pltpu.make_async_remote_copy(
                src_ref=piece_ref.at[pidx], dst_ref=mod_ref.at[me], send_sem=send_sems.at[1, r - 1],
                recv_sem=recv_sems.at[1, r - 1], device_id=peer, device_id_type=MESH)
            cp.start()
            sent.append(cp)
        for r in range(1, N_DEV):
            peer, pidx = _peer(r)
            pltpu.make_async_remote_copy(
                src_ref=piece_ref.at[me], dst_ref=mod_ref.at[pidx], send_sem=send_sems.at[1, r - 1],
                recv_sem=recv_sems.at[1, r - 1], device_id=peer, device_id_type=MESH).wait_recv()
        for cp in sent:
            cp.wait_send()

    vmem = pl.BlockSpec(memory_space=pltpu.VMEM)
    return pl.pallas_call(
        body, name="mod_exchange", in_specs=[vmem, vmem, vmem], out_specs=[vmem, vmem],
        out_shape=[jax.ShapeDtypeStruct((N_DEV, 1, D_MODEL), F32), jax.ShapeDtypeStruct((N_DEV, 1, ADA_SHARD), F32)],
        scratch_shapes=[pltpu.VMEM((N_DEV, 1, ADA_SHARD), F32),
                        pltpu.SemaphoreType.DMA((2, N_DEV - 1)), pltpu.SemaphoreType.DMA((2, N_DEV - 1))],
        compiler_params=_params(),
    )(c_row, w_ada, b_ada_loc)


def _split3(x):
    hi = x.astype(BF16)
    r1 = x - hi.astype(F32)
    mid = r1.astype(BF16)
    lo = (r1 - mid.astype(F32)).astype(BF16)
    return hi, mid, lo


def _scan_rows(x_ref, o_ref, s, reverse, pre=None, post=None):
    tb = min(TQ, s)
    nb = s // tb
    row = lax.broadcasted_iota(jnp.int32, (tb, tb), 0)
    col = lax.broadcasted_iota(jnp.int32, (tb, tb), 1)
    tri = jnp.where((col >= row) if reverse else (col <= row), 1.0, 0.0).astype(BF16)

    def step(i, carry):
        blk = (nb - 1 - i) if reverse else i
        off = pl.multiple_of(blk * tb, tb)
        x = x_ref[pl.ds(off, tb), :]
        if pre is not None:
            x = pre(x, off)
        acc = carry
        for piece in _split3(x):
            acc = acc + jnp.dot(tri, piece, preferred_element_type=F32)
        o_ref[pl.ds(off, tb), :] = acc if post is None else post(acc, off)
        edge = acc[0:1, :] if reverse else acc[tb - 1:tb, :]
        return jnp.broadcast_to(edge, (tb, LANES))

    lax.fori_loop(0, nb, step, jnp.zeros((tb, LANES), F32))


def _forget_cumsum(f_raw, b_pad):
    s = f_raw.shape[0]

    def body(f_ref, b_ref, cum_ref):
        b = b_ref[...]
        _scan_rows(f_ref, cum_ref, s, False, pre=lambda x, off: _log_sigmoid(x + b))

    vmem = pl.BlockSpec(memory_space=pltpu.VMEM)
    return pl.pallas_call(body, name="forget_cumsum", in_specs=[vmem, vmem], out_specs=vmem,
                          out_shape=jax.ShapeDtypeStruct((s, LANES), F32), compiler_params=_params())(f_raw, b_pad)


def _forget_bwd(dcum, f_raw, b_pad):
    s = f_raw.shape[0]

    def body(d_ref, f_ref, b_ref, df_ref, db_ref, tmp_ref):
        b = b_ref[...]
        _scan_rows(d_ref, tmp_ref, s, True)
        df = tmp_ref[...] * _sigmoid(-(f_ref[...] + b))
        df_ref[...] = df.astype(BF16)
        db_ref[...] = _colsum(df)

    vmem = pl.BlockSpec(memory_space=pltpu.VMEM)
    return pl.pallas_call(
        body, name="forget_bwd", in_specs=[vmem, vmem, vmem], out_specs=[vmem, vmem],
        out_shape=[jax.ShapeDtypeStruct((s, LANES), BF16), jax.ShapeDtypeStruct((1, LANES), F32)],
        scratch_shapes=[pltpu.VMEM((s, LANES), F32)], compiler_params=_params())(dcum, f_raw, b_pad)


def _dot_nt(a, b):
    return lax.dot_general(a, b, (((1,), (1,)), ((), ())), preferred_element_type=F32)


def _head_masks():
    lane = lax.broadcasted_iota(jnp.int32, (TQ, LANES), 1)
    return lane, [lane < HEAD_DIM, lane >= HEAD_DIM]


def _pick(mask, x):
    return jnp.where(mask, x, jnp.zeros_like(x))


def _qkv_specs(s, col0):
    nb = ATTN_W // LANES
    return [pl.BlockSpec((TQ, LANES), lambda hp, qi: (qi, col0 + hp)),
            pl.BlockSpec((s, LANES), lambda hp, qi: (0, col0 + nb + hp)),
            pl.BlockSpec((s, LANES), lambda hp, qi: (0, col0 + 2 * nb + hp))]


def _pair_spec():
    return pl.BlockSpec((TQ, LANES), lambda hp, qi: (qi, hp))


def _fox_fwd(qkv, cum_col, cum_row, ride=None):
    s = qkv.shape[0]
    nq = s // TQ

    def body(q_ref, k_ref, v_ref, cc_ref, cr_ref, o_ref, o32_ref, lse_ref):
        hp, qi = pl.program_id(0), pl.program_id(1)
        lane, masks = _head_masks()
        row = lax.broadcasted_iota(jnp.int32, (TQ, TQ), 0)
        col = lax.broadcasted_iota(jnp.int32, (TQ, TQ), 1)
        causal = col <= row
        q2 = q_ref[...] * jnp.asarray(ATTN_SCALE, BF16)
        cc = cc_ref[...]
        qms = [_pick(masks[e], q2) for e in range(2)]
        cqs = [jnp.sum(jnp.where(lane == 2 * hp + e, cc, 0.0), axis=1, keepdims=True) for e in range(2)]

        def tile(kb, carry, masked, width=1):
            off, span = pl.multiple_of(kb * TQ, TQ), width * TQ
            k2, v2 = k_ref[pl.ds(off, span), :], v_ref[pl.ds(off, span), :]
            head0 = lax.broadcasted_iota(jnp.int32, (span, LANES), 1) < HEAD_DIM
            new = []
            for e in range(2):
                m, acc = carry[e]
                sc = _dot_nt(qms[e], k2) + (cqs[e] - cr_ref[e:e + 1, pl.ds(off, span)])
                if masked:
                    sc = jnp.where(causal, sc, -jnp.inf)
                m_new = jnp.maximum(m, jnp.max(sc, axis=1, keepdims=True))
                p = jnp.exp(sc - m_new)
                v_ones = jnp.where(head0 if e == 0 else ~head0, v2, jnp.ones_like(v2))
                acc = jnp.exp(m - m_new) * acc + jnp.dot(p.astype(BF16), v_ones, preferred_element_type=F32)
                new.append((m_new, acc))
            return tuple(new)

        init = (jnp.full((TQ, 1), -jnp.inf, F32), jnp.zeros((TQ, LANES), F32))
        carry = lax.fori_loop(0, qi // 2, lambda j, cr: tile(2 * j, cr, False, 2), tile(qi, (init, init), True))
        carry = lax.cond(qi % 2 == 1, lambda cr: tile(qi - 1, cr, False), lambda cr: cr, carry)
        sums = [jnp.max(jnp.where(masks[1 - e], carry[e][1], 0.0), axis=1, keepdims=True) for e in range(2)]
        outs = [carry[e][1] / sums[e] for e in range(2)]
        lses = [carry[e][0] + jnp.log(sums[e]) for e in range(2)]
        out = jnp.where(masks[0], outs[0], outs[1])
        o_ref[...] = out.astype(BF16)
        o32_ref[...] = out
        lse_ref[...] = jnp.where(masks[0], lses[0], lses[1])

    return _ride_call(
        body, ride, name="fox_fwd", grid=(N_HEADS // 2, nq),
        in_specs=_qkv_specs(s, 0) + [pl.BlockSpec((TQ, LANES), lambda hp, qi: (qi, 0)),
                                     pl.BlockSpec((None, 2, s), lambda hp, qi: (hp, 0, 0))],
        out_specs=[_pair_spec(), _pair_spec(), _pair_spec()],
        out_shape=[jax.ShapeDtypeStruct((s, ATTN_W), BF16), jax.ShapeDtypeStruct((s, ATTN_W), F32),
                   jax.ShapeDtypeStruct((s, ATTN_W), F32)],
        scratch_shapes=[], sem=("parallel", "parallel"), args=(qkv, qkv, qkv, cum_col, cum_row))


def _write_transposed(acc_ref, out_ref):
    for c in range(out_ref.shape[0] // TQ):
        out_ref[c * TQ:(c + 1) * TQ, :] = jnp.transpose(acc_ref[:, c * TQ:(c + 1) * TQ]).astype(BF16)


def _fox_bwd(qkv, cum_col, cum_row, o, lse, do, ride=None):
    s = qkv.shape[0]
    nq = s // TQ

    def body(q_ref, k_ref, v_ref, cc_ref, cr_ref, o_ref, lse_ref, do_ref,
             dq_ref, dk_ref, dv_ref, dcr_ref, dcq_ref, dk_acc, dv_acc):
        hp, qi = pl.program_id(0), pl.program_id(1)

        @pl.when(qi == 0)
        def _():
            dk_acc[...] = jnp.zeros_like(dk_acc)
            dv_acc[...] = jnp.zeros_like(dv_acc)
            dcr_ref[...] = jnp.zeros_like(dcr_ref)

        lane, masks = _head_masks()
        row = lax.broadcasted_iota(jnp.int32, (TQ, TQ), 0)
        col = lax.broadcasted_iota(jnp.int32, (TQ, TQ), 1)
        causal = col <= row
        q2 = q_ref[...] * jnp.asarray(ATTN_SCALE, BF16)
        do2 = do_ref[...]
        prod = do2.astype(F32) * o_ref[...].astype(F32)
        lse2 = lse_ref[...]
        cc = cc_ref[...]
        qms = [_pick(masks[e], q2) for e in range(2)]
        doms = [_pick(masks[e], do2) for e in range(2)]
        deltas = [jnp.sum(jnp.where(masks[e], prod, 0.0), axis=1, keepdims=True) for e in range(2)]
        lses = [jnp.max(jnp.where(masks[e], lse2, -jnp.inf), axis=1, keepdims=True) for e in range(2)]
        cqs = [jnp.sum(jnp.where(lane == 2 * hp + e, cc, 0.0), axis=1, keepdims=True) for e in range(2)]
        qts = [jnp.transpose(qms[e].astype(F32)).astype(BF16) for e in range(2)]
        dots = [jnp.transpose(doms[e].astype(F32)).astype(BF16) for e in range(2)]

        def tile(kb, carry, masked, width=1):
            off, span = pl.multiple_of(kb * TQ, TQ), width * TQ
            k2, v2 = k_ref[pl.ds(off, span), :], v_ref[pl.ds(off, span), :]
            new, dk, dv = [], None, None
            for e in range(2):
                dq, rowsum = carry[e]
                sc = _dot_nt(qms[e], k2) + (cqs[e] - cr_ref[e:e + 1, pl.ds(off, span)])
                p = jnp.exp(sc - lses[e])
                if masked:
                    p = jnp.where(causal, p, 0.0)
                ds = p * (_dot_nt(doms[e], v2) - deltas[e])
                dsb = ds.astype(BF16)
                dk_e = jnp.dot(qts[e], dsb, preferred_element_type=F32)
                dv_e = jnp.dot(dots[e], p.astype(BF16), preferred_element_type=F32)
                dk, dv = (dk_e, dv_e) if e == 0 else (dk + dk_e, dv + dv_e)
                dcr_ref[e:e + 1, pl.ds(off, span)] -= _colsum(ds)
                new.append((dq + jnp.dot(dsb, k2, preferred_element_type=F32),
                            rowsum + jnp.sum(ds, axis=1, keepdims=True)))
            dk_acc[:, pl.ds(off, span)] += dk
            dv_acc[:, pl.ds(off, span)] += dv
            return tuple(new)

        init = (jnp.zeros((TQ, LANES), F32), jnp.zeros((TQ, 1), F32))
        carry = lax.fori_loop(0, qi // 2, lambda j, cr: tile(2 * j, cr, False, 2), (init, init))
        carry = lax.cond(qi % 2 == 1, lambda cr: tile(qi - 1, cr, False), lambda cr: cr, carry)
        carry = tile(qi, carry, True)
        dq_ref[...] = (jnp.where(masks[0], carry[0][0], carry[1][0]) * ATTN_SCALE).astype(BF16)
        dcq_ref[...] = jnp.where(masks[0], carry[0][1], carry[1][1])

        @pl.when(qi == nq - 1)
        def _():
            _write_transposed(dk_acc, dk_ref)
            _write_transposed(dv_acc, dv_ref)

    seq_spec = pl.BlockSpec((s, LANES), lambda hp, qi: (0, hp))
    return _ride_call(
        body, ride, name="fox_bwd", grid=(N_HEADS // 2, nq),
        in_specs=_qkv_specs(s, 0) + [pl.BlockSpec((TQ, LANES), lambda hp, qi: (qi, 0)),
                                     pl.BlockSpec((None, 2, s), lambda hp, qi: (hp, 0, 0)),
                                     _pair_spec(), _pair_spec(), _pair_spec()],
        out_specs=[_pair_spec(), seq_spec, seq_spec, pl.BlockSpec((None, 2, s), lambda hp, qi: (hp, 0, 0)),
                   _pair_spec()],
        out_shape=[jax.ShapeDtypeStruct((s, ATTN_W), BF16)] * 3 + [jax.ShapeDtypeStruct((N_HEADS // 2, 2, s), F32),
                                                                    jax.ShapeDtypeStruct((s, ATTN_W), F32)],
        scratch_shapes=[pltpu.VMEM((LANES, s), F32), pltpu.VMEM((LANES, s), F32)],
        sem=("parallel", "arbitrary"), args=(qkv, qkv, qkv, cum_col, cum_row, o, lse, do))


def _scan_matrix(reverse):
    row = lax.broadcasted_iota(jnp.int32, (SCAN_W, SCAN_W), 0)
    col = lax.broadcasted_iota(jnp.int32, (SCAN_W, SCAN_W), 1)
    return jnp.where((row > col) if reverse else (row < col), 1.0, 0.0).astype(BF16)


def _scan_cols(x, tri, reverse, init):
    nblk = x.shape[1] // SCAN_W
    parts, total = [None] * nblk, init
    far = 0 if reverse else SCAN_W - 1
    for b in (reversed(range(nblk)) if reverse else range(nblk)):
        blk = x[:, b * SCAN_W:(b + 1) * SCAN_W]
        part = jnp.dot(blk.astype(BF16), tri, preferred_element_type=F32)
        parts[b] = part + total
        total = total + (part[:, far:far + 1] + blk[:, far:far + 1])
    return (parts[0] if nblk == 1 else jnp.concatenate(parts, axis=1)), total


def _sb_logits(qm, k2):
    z = _dot_nt(qm, k2)
    neg_abs = lax.bitcast_convert_type(lax.bitcast_convert_type(z, jnp.uint32) | jnp.uint32(0x80000000), F32)
    soft = jnp.log(1.0 + jnp.exp(neg_abs))
    lb = jnp.minimum(z, 0.0) - soft
    return lb, lb - z


def _tri_base(qi):
    return (qi * (qi + 1)) // 2


def _sb_fwd(qkv):
    s = qkv.shape[0]
    nq = s // TQ

    def body(q_ref, k_ref, v_ref, o_ref, t_ref, buf, sems):
        hp, qi = pl.program_id(0), pl.program_id(1)
        _, masks = _head_masks()
        row = lax.broadcasted_iota(jnp.int32, (TQ, TQ), 0)
        col = lax.broadcasted_iota(jnp.int32, (TQ, TQ), 1)
        strict = col < row
        suffix = _scan_matrix(True)
        q2 = q_ref[...] * jnp.asarray(ATTN_SCALE, BF16)
        qms = [_pick(masks[e], q2) for e in range(2)]
        base = _tri_base(qi)

        def store(e, slot, kb):
            return pltpu.make_async_copy(buf.at[e, slot], t_ref.at[2 * hp + e, base + kb], sems.at[e, slot])

        def tile(n, carry, masked):
            kb = qi - n
            off = pl.multiple_of(kb * TQ, TQ)
            k2, v2 = k_ref[pl.ds(off, TQ), :], v_ref[pl.ds(off, TQ), :]
            slot = n % 2
            new = []
            for e in range(2):
                run, acc = carry[e]
                lb, lo = _sb_logits(qms[e], k2)
                if masked:
                    lo = jnp.where(strict, lo, 0.0)
                rest, run = _scan_cols(lo, suffix, True, run)
                a = jnp.exp(lb + rest)
                if masked:
                    a = jnp.where(strict, a, 0.0)
                ab = a.astype(BF16)
                acc = acc + jnp.dot(ab, v2, preferred_element_type=F32)
                if not masked:
                    @pl.when(n >= 2)
                    def _(e=e, slot=slot, kb=kb):
                        store(e, slot, kb + 2).wait()
                buf[e, slot, 0] = ab
                buf[e, slot, 1] = lb.astype(BF16)
                store(e, slot, kb).start()
                new.append((run, acc))
            return tuple(new)

        init = (jnp.zeros((TQ, 1), F32), jnp.zeros((TQ, LANES), F32))
        carry = lax.fori_loop(0, qi, lambda it, cr: tile(it + 1, cr, False), tile(0, (init, init), True))
        for e in range(2):
            store(e, qi % 2, 0).wait()

            @pl.when(qi >= 1)
            def _(e=e):
                store(e, (qi - 1) % 2, 1).wait()
        o_ref[...] = jnp.where(masks[0], carry[0][1], carry[1][1]).astype(BF16)

    ntri = nq * (nq + 1) // 2
    return pl.pallas_call(
        body, name="sb_fwd", grid=(N_HEADS // 2, nq), in_specs=_qkv_specs(s, 3 * ATTN_W // LANES),
        out_specs=[_pair_spec(), ANY],
        out_shape=[jax.ShapeDtypeStruct((s, ATTN_W), BF16), jax.ShapeDtypeStruct((N_HEADS, ntri, 2, TQ, TQ), BF16)],
        scratch_shapes=[pltpu.VMEM((2, 2, 2, TQ, TQ), BF16), pltpu.SemaphoreType.DMA((2, 2))],
        compiler_params=_params(("arbitrary", "arbitrary")),
    )(qkv, qkv, qkv)


def _sb_bwd(qkv, tiles, do):
    s = qkv.shape[0]
    nq = s // TQ

    def body(q_ref, k_ref, v_ref, t_ref, do_ref, dq_ref, dk_ref, dv_ref, dk_acc, dv_acc, buf, sems):
        hp, qi = pl.program_id(0), pl.program_id(1)

        @pl.when(qi == 0)
        def _():
            dk_acc[...] = jnp.zeros_like(dk_acc)
            dv_acc[...] = jnp.zeros_like(dv_acc)

        _, masks = _head_masks()
        row = lax.broadcasted_iota(jnp.int32, (TQ, TQ), 0)
        col = lax.broadcasted_iota(jnp.int32, (TQ, TQ), 1)
        strict = col < row
        prefix = _scan_matrix(False)
        q2 = q_ref[...] * jnp.asarray(ATTN_SCALE, BF16)
        do2 = do_ref[...]
        qms = [_pick(masks[e], q2) for e in range(2)]
        doms = [_pick(masks[e], do2) for e in range(2)]
        qts = [jnp.transpose(qms[e].astype(F32)).astype(BF16) for e in range(2)]
        dots = [jnp.transpose(doms[e].astype(F32)).astype(BF16) for e in range(2)]
        base = _tri_base(qi)

        def fetch(e, kb):
            return pltpu.make_async_copy(t_ref.at[2 * hp + e, base + kb], buf.at[e, kb % 2], sems.at[e, kb % 2])

        for e in range(2):
            fetch(e, 0).start()

        def tile(kb, carry, masked):
            off = pl.multiple_of(kb * TQ, TQ)
            k2, v2 = k_ref[pl.ds(off, TQ), :], v_ref[pl.ds(off, TQ), :]
            new, dk, dv = [], None, None
            for e in range(2):
                gsum, dq = carry[e]
                if not masked:
                    fetch(e, kb + 1).start()
                fetch(e, kb).wait()
                ab = buf[e, kb % 2, 0]
                beta = jnp.exp(buf[e, kb % 2, 1].astype(F32))
                g = ab.astype(F32) * _dot_nt(doms[e], v2)
                before, gsum = _scan_cols(g, prefix, False, gsum)
                dz = g - beta * (g + before)
                if masked:
                    dz = jnp.where(strict, dz, 0.0)
                dzb = dz.astype(BF16)
                dk_e = jnp.dot(qts[e], dzb, preferred_element_type=F32)
                dv_e = jnp.dot(dots[e], ab, preferred_element_type=F32)
                dk, dv = (dk_e, dv_e) if e == 0 else (dk + dk_e, dv + dv_e)
                new.append((gsum, dq + jnp.dot(dzb, k2, preferred_element_type=F32)))
            dk_acc[:, pl.ds(off, TQ)] += dk
            dv_acc[:, pl.ds(off, TQ)] += dv
            return tuple(new)

        init = (jnp.zeros((TQ, 1), F32), jnp.zeros((TQ, LANES), F32))
        carry = tile(qi, lax.fori_loop(0, qi, lambda kb, cr: tile(kb, cr, False), (init, init)), True)
        dq_ref[...] = (jnp.where(masks[0], carry[0][1], carry[1][1]) * ATTN_SCALE).astype(BF16)

        @pl.when(qi == nq - 1)
        def _():
            _write_transposed(dk_acc, dk_ref)
            _write_transposed(dv_acc, dv_ref)

    seq_spec = pl.BlockSpec((s, LANES), lambda hp, qi: (0, hp))
    return pl.pallas_call(
        body, name="sb_bwd", grid=(N_HEADS // 2, nq),
        in_specs=_qkv_specs(s, 3 * ATTN_W // LANES) + [ANY, _pair_spec()],
        out_specs=[_pair_spec(), seq_spec, seq_spec],
        out_shape=[jax.ShapeDtypeStruct((s, ATTN_W), BF16)] * 3,
        scratch_shapes=[pltpu.VMEM((LANES, s), F32), pltpu.VMEM((LANES, s), F32),
                        pltpu.VMEM((2, 2, 2, TQ, TQ), BF16), pltpu.SemaphoreType.DMA((2, 2))],
        compiler_params=_params(("arbitrary", "arbitrary")),
    )(qkv, qkv, qkv, tiles, do)


CONV_TR = 256


def _shift_down(x, halo, n):
    rolled = pltpu.roll(x, n, 0)
    head = rolled[0:8, :]
    rid = lax.broadcasted_iota(jnp.int32, head.shape, 0)
    for j in range(n):
        head = jnp.where(rid == j, halo[8 - n + j:8 - n + j + 1, :], head)
    return jnp.concatenate([head, rolled[8:, :]], axis=0)


def _shift_up(x, halo, n):
    rows = x.shape[0]
    rolled = pltpu.roll(x, rows - n, 0)
    tail = rolled[rows - 8:, :]
    rid = lax.broadcasted_iota(jnp.int32, tail.shape, 0)
    for j in range(n):
        tail = jnp.where(rid == 8 - n + j, halo[j:j + 1, :], tail)
    return jnp.concatenate([rolled[:rows - 8, :], tail], axis=0)


def _conv_fwd_block(x, halo, w, b):
    return b + _shift_down(x, halo, 2) * w[0:1, :] + _shift_down(x, halo, 1) * w[1:2, :] + x * w[2:3, :]


def _conv_specs(tr, s):
    pair = 2 * FF_HALF
    blk = pl.BlockSpec((tr, pair), lambda j, i: (i, j))
    prev = pl.BlockSpec((8, pair), lambda j, i: (jnp.maximum(i * (tr // 8) - 1, 0), j))
    nxt = pl.BlockSpec((8, pair), lambda j, i: (jnp.minimum((i + 1) * (tr // 8), s // 8 - 1), j))
    return blk, prev, nxt


def _conv_gate_fwd(hpre, conv_w, conv_b):
    s = hpre.shape[0]
    tr = min(CONV_TR, s)
    blk, prev, _ = _conv_specs(tr, s)

    def body(x_ref, halo_ref, w_ref, b_ref, a_ref):
        i = pl.program_id(1)
        halo = jnp.where(i > 0, halo_ref[...], 0.0)
        h = _conv_fwd_block(x_ref[...], halo, w_ref[...], b_ref[...])
        hg, hv = h[:, :FF_HALF], h[:, FF_HALF:]
        a_ref[...] = (hg * _sigmoid(hg) * hv).astype(BF16)

    return pl.pallas_call(
        body, name="conv_gate_fwd", grid=(2, s // tr),
        in_specs=[blk, prev, pl.BlockSpec((3, 2 * FF_HALF), lambda j, i: (0, j)),
                  pl.BlockSpec((1, 2 * FF_HALF), lambda j, i: (0, j))],
        out_specs=pl.BlockSpec((tr, FF_HALF), lambda j, i: (i, j)),
        out_shape=jax.ShapeDtypeStruct((s, D_FF), BF16),
        compiler_params=_params(("parallel", "parallel")),
    )(hpre, hpre, conv_w, conv_b)


def _conv_gate_bwd(hpre, da, conv_w, conv_b):
    s = hpre.shape[0]
    tr = min(CONV_TR, s)
    blk, prev, _ = _conv_specs(tr, s)

    def body(x_ref, halo_ref, da_ref, w_ref, b_ref, dh_ref, db_ref, dw_ref):
        i = pl.program_id(1)
        halo = jnp.where(i > 0, halo_ref[...], 0.0)
        x = x_ref[...]
        h = _conv_fwd_block(x, halo, w_ref[...], b_ref[...])
        hg, hv = h[:, :FF_HALF], h[:, FF_HALF:]
        da_blk = da_ref[...].astype(F32)
        sg = _sigmoid(hg)
        dhg = da_blk * hv * (sg * (1.0 + hg * (1.0 - sg)))
        dhv = da_blk * (hg * sg)
        dh_ref[:, :FF_HALF] = dhg.astype(BF16)
        dh_ref[:, FF_HALF:] = dhv.astype(BF16)
        x2, x1 = _shift_down(x, halo, 2), _shift_down(x, halo, 1)
        parts = []
        for lo, dpart in ((0, dhg), (FF_HALF, dhv)):
            cols = slice(lo, lo + FF_HALF)
            parts.append((cols, _colsum(dpart), _colsum(dpart * x2[:, cols]), _colsum(dpart * x1[:, cols]),
                          _colsum(dpart * x[:, cols])))

        @pl.when(i == 0)
        def _():
            for cols, db, dw0, dw1, dw2 in parts:
                db_ref[:, cols] = db
                dw_ref[0:1, cols] = dw0
                dw_ref[1:2, cols] = dw1
                dw_ref[2:3, cols] = dw2

        @pl.when(i > 0)
        def _():
            for cols, db, dw0, dw1, dw2 in parts:
                db_ref[:, cols] += db
                dw_ref[0:1, cols] += dw0
                dw_ref[1:2, cols] += dw1
                dw_ref[2:3, cols] += dw2

    pair = 2 * FF_HALF
    return pl.pallas_call(
        body, name="conv_gate_bwd", grid=(2, s // tr),
        in_specs=[blk, prev, pl.BlockSpec((tr, FF_HALF), lambda j, i: (i, j)),
                  pl.BlockSpec((3, pair), lambda j, i: (0, j)), pl.BlockSpec((1, pair), lambda j, i: (0, j))],
        out_specs=[blk, pl.BlockSpec((1, pair), lambda j, i: (0, j)), pl.BlockSpec((3, pair), lambda j, i: (0, j))],
        out_shape=[jax.ShapeDtypeStruct((s, 2 * D_FF), BF16), jax.ShapeDtypeStruct((1, 2 * D_FF), F32),
                   jax.ShapeDtypeStruct((3, 2 * D_FF), F32)],
        compiler_params=_params(("parallel", "arbitrary")),
    )(hpre, hpre, da, conv_w, conv_b)


def _conv_input_bwd(dh, conv_w):
    s = dh.shape[0]
    tr = min(CONV_TR, s)
    blk, _, _ = _conv_specs(tr, s)
    nblk = s // tr

    def body(x_ref, halo_ref, w_ref, o_ref):
        i = pl.program_id(1)
        halo = jnp.where(i < nblk - 1, halo_ref[...].astype(F32), 0.0)
        x, w = x_ref[...].astype(F32), w_ref[...]
        o_ref[...] = (x * w[2:3, :] + _shift_up(x, halo, 1) * w[1:2, :] + _shift_up(x, halo, 2) * w[0:1, :]).astype(BF16)

    nxt = pl.BlockSpec((16, 2 * FF_HALF), lambda j, i: (jnp.minimum((i + 1) * (tr // 16), s // 16 - 1), j))
    return pl.pallas_call(
        body, name="conv_input_bwd", grid=(2, nblk),
        in_specs=[blk, nxt, pl.BlockSpec((3, 2 * FF_HALF), lambda j, i: (0, j))], out_specs=blk,
        out_shape=jax.ShapeDtypeStruct((s, 2 * D_FF), BF16),
        compiler_params=_params(("parallel", "parallel")),
    )(dh, dh, conv_w)


def _adamw_math(w, g, m, v):
    m = ADAM_B1 * m + (1.0 - ADAM_B1) * g
    v = ADAM_B2 * v + (1.0 - ADAM_B2) * (g * g)
    m_hat = m / (1.0 - ADAM_B1 ** ADAM_STEP)
    v_hat = v / (1.0 - ADAM_B2 ** ADAM_STEP)
    delta = -ADAM_LR * (m_hat / (jnp.sqrt(v_hat) + ADAM_EPS) + ADAM_WD * w)
    return delta, m, v


def _adamw(name, g8, w, m, v):
    r, c = w.shape
    tr = _row_tile(r, c)

    def body(g_ref, w_ref, m_ref, v_ref, go_ref, d_ref, mo_ref, vo_ref):
        g = g_ref[0].astype(F32)
        for d in range(1, N_DEV):
            g = g + g_ref[d].astype(F32)
        delta, mn, vn = _adamw_math(w_ref[...], g, m_ref[...], v_ref[...])
        go_ref[...] = g
        d_ref[...] = delta
        mo_ref[...] = mn
        vo_ref[...] = vn

    spec = pl.BlockSpec((tr, c), lambda i: (i, 0))
    return pl.pallas_call(
        body, name=name, grid=(r // tr,),
        in_specs=[pl.BlockSpec((N_DEV, tr, c), lambda i: (0, i, 0)), spec, spec, spec], out_specs=[spec] * 4,
        out_shape=[jax.ShapeDtypeStruct((r, c), F32)] * 4, compiler_params=_params(("parallel",)),
    )(g8, w, m, v)


def _adamw_ada(c_t, dmod, w, m, v):
    r, c = w.shape
    tr = _row_tile(r, c)

    def body(ct_ref, dm_ref, w_ref, m_ref, v_ref, go_ref, d_ref, mo_ref, vo_ref):
        ct, dm = ct_ref[...], dm_ref[...]
        g = ct[:, 0:1] * dm[0:1, :]
        for b in range(1, N_DEV):
            g = g + ct[:, b:b + 1] * dm[b:b + 1, :]
        delta, mn, vn = _adamw_math(w_ref[...], g, m_ref[...], v_ref[...])
        go_ref[...] = g
        d_ref[...] = delta
        mo_ref[...] = mn
        vo_ref[...] = vn

    spec = pl.BlockSpec((tr, c), lambda i: (i, 0))
    return pl.pallas_call(
        body, name="adamw_w_ada", grid=(r // tr,),
        in_specs=[pl.BlockSpec((tr, N_DEV), lambda i: (i, 0)), pl.BlockSpec((N_DEV, c), lambda i: (0, 0)),
                  spec, spec, spec],
        out_specs=[spec] * 4, out_shape=[jax.ShapeDtypeStruct((r, c), F32)] * 4,
        compiler_params=_params(("parallel",)),
    )(c_t, dmod, w, m, v)


def _cols_from_slots(g):
    n, r, c = g.shape
    return jnp.transpose(g, (1, 0, 2)).reshape(r, n * c)


def _cols_to_slots(w):
    r, c = w.shape
    return jnp.transpose(w.reshape(r, N_DEV, c // N_DEV), (1, 0, 2))


def _pair_cols(w):
    g0, g1 = w[..., 0:FF_HALF], w[..., FF_HALF:D_FF]
    v0, v1 = w[..., D_FF:D_FF + FF_HALF], w[..., D_FF + FF_HALF:]
    return jnp.concatenate([g0, v0, g1, v1], axis=-1)


def _unpair_cols(w):
    g0, v0 = w[..., 0:FF_HALF], w[..., FF_HALF:D_FF]
    g1, v1 = w[..., D_FF:D_FF + FF_HALF], w[..., D_FF + FF_HALF:]
    return jnp.concatenate([g0, g1, v0, v1], axis=-1)


def _row(v):
    return v.reshape(1, -1)


def kernel(x, c, w_ada, b_ada, w_in, b_forget, w_fox_proj, w_sb_proj, w_o, ln1_g, ln1_b, w_up, conv_w, conv_b, w_down, ln2_g, ln2_b, loss_target, m_w_ada, m_b_ada, m_w_in, m_b_forget, m_w_fox_proj, m_w_sb_proj, m_w_o, m_ln1_g, m_ln1_b, m_w_up, m_conv_w, m_conv_b, m_w_down, m_ln2_g, m_ln2_b, v_w_ada, v_b_ada, v_w_in, v_b_forget, v_w_fox_proj, v_w_sb_proj, v_w_o, v_ln1_g, v_ln1_b, v_w_up, v_conv_w, v_conv_b, v_w_down, v_ln2_g, v_ln2_b):
    s = x.shape[1]
    me = 4 * lax.axis_index("x") + 2 * lax.axis_index("y") + lax.axis_index("c")
    x2 = x.reshape(s, D_MODEL)
    tgt = loss_target.reshape(s, D_MODEL)

    b_ada_loc = lax.dynamic_slice(b_ada, (me * ADA_SHARD,), (ADA_SHARD,)).reshape(1, ADA_SHARD)
    c_all, mod = _mod_exchange(c, w_ada, b_ada_loc)
    mod = mod.reshape(N_MOD, 1, D_MODEL)
    sh1, sc1, gt1, sh2, sc2, gt2 = [mod[i] for i in range(N_MOD)]

    (g_in,) = _exchange("ag_w_in", [w_in.astype(BF16)], scatter=False)
    late_weights = _Ride([w_fox_proj.astype(BF16), w_sb_proj.astype(BF16), w_o.astype(BF16), w_up.astype(BF16),
                          w_down.astype(BF16), conv_w], scatter=False)
    w_in_f = _cols_from_slots(g_in)
    w_proj = jnp.concatenate(
        [w_in_f[:, 0:1536], w_in_f[:, 1544:3080], w_in_f[:, 3080:5128], w_in_f[:, 1536:1544],
         jnp.zeros((D_MODEL, W_PROJ - 5128), BF16)], axis=1)
    w_qkv, w_gates, w_f = w_proj[:, :W_QKV], w_proj[:, W_QKV:W_QKV + W_GATES], w_proj[:, W_QKV + W_GATES:W_QKV + W_GATES + W_F]
    conv_b_p = _pair_cols(_row(conv_b))
    b_f_pad = jnp.pad(_row(b_forget), ((0, 0), (0, LANES - N_HEADS)))

    (u1,) = _rowwise("modulate1", lambda xb, sc, sh: (xb * (1.0 + sc) + sh,),
                     [(x2, D_MODEL, 0)], [sc1, sh1], [(D_MODEL, BF16)], tr=512)
    qkv = _mm(u1, w_qkv, name="mm_qkv", out_dtype=BF16)
    gates = _mm(u1, w_gates, name="mm_gates")
    f_raw = _mm(u1, w_f, name="mm_forget")
    cum_col = _forget_cumsum(f_raw, b_f_pad)
    cum_row = jnp.transpose(cum_col[:, :N_HEADS]).reshape(N_HEADS // 2, 2, s)
    (y_fox, y_fox32, lse), (g_fox, g_sb, g_o, g_up, g_down, g_cw) = _fox_fwd(qkv, cum_col, cum_row, ride=late_weights)
    w_fox_f = _cols_from_slots(g_fox)
    w_sb_f = _cols_from_slots(g_sb)
    w_o_f = g_o.reshape(D_MODEL, D_MODEL)
    w_up_p = _pair_cols(_cols_from_slots(g_up))
    w_down_f = g_down.reshape(D_FF, D_MODEL)
    conv_w_p = _pair_cols(_cols_from_slots(g_cw))
    y_sb, sb_run = _sb_fwd(qkv)
    pf = _mm(y_fox, w_fox_f, name="mm_fox_proj")
    ps = _mm(y_sb, w_sb_f, name="mm_sb_proj")
    (merged,) = _rowwise("gate_merge", lambda ga, gb, a, b: (_sigmoid(ga) * a + _sigmoid(gb) * b,),
                         [(gates, D_MODEL, 0), (gates, D_MODEL, 1), (pf, D_MODEL, 0), (ps, D_MODEL, 0)], [],
                         [(D_MODEL, BF16)])
    attn_out = _mm(merged, w_o_f, name="mm_w_o")

    def ln_fwd(xb, fb, gt, g, b):
        xhat, _ = _ln_stats(ALPHA * xb + (1.0 + gt) * fb)
        return xhat * g + b

    def ln1_mod(xb, fb, gt, g, b, sc, sh):
        y = ln_fwd(xb, fb, gt, g, b)
        return y, y * (1.0 + sc) + sh

    x1, u2 = _rowwise("ln1_modulate2", ln1_mod, [(x2, D_MODEL, 0), (attn_out, D_MODEL, 0)],
                      [gt1, _row(ln1_g), _row(ln1_b), sc2, sh2], [(D_MODEL, F32), (D_MODEL, BF16)])

    hpre = _mm(u2, w_up_p, name="mm_w_up", tn=1408)
    act = _conv_gate_fwd(hpre, conv_w_p, conv_b_p)
    ffn_out = _mm(act, w_down_f, name="mm_w_down", tk=2816)

    def ln2_bwd(xb, fb, tb, gt, g, b):
        xhat, rstd = _ln_stats(ALPHA * xb + (1.0 + gt) * fb)
        err = (xhat * g + b) - tb
        dy = err * (1.0 / D_MODEL)
        dr = _ln_bwd(dy, xhat, rstd, g)
        return (dr * (1.0 + gt), ALPHA * dr,
                _colsum(err * err), _colsum(dy * xhat), _colsum(dy), _colsum(dr * fb))

    dffn, dx1_res, sq_err, d_ln2_g, d_ln2_b, d_gt2 = _rowwise(
        "ln2_bwd", ln2_bwd, [(x1, D_MODEL, 0), (ffn_out, D_MODEL, 0), (tgt, D_MODEL, 0)],
        [gt2, _row(ln2_g), _row(ln2_b)], [(D_MODEL, BF16), (D_MODEL, F32)], sums=[D_MODEL] * 4)
    loss = lax.psum(0.5 * jnp.sum(sq_err) / D_MODEL, ("x", "y", "c"))

    d_w_down = _mm(act, dffn, name="mm_d_w_down", ta=True, tm=1408, tk=2048, out_dtype=BF16)
    d_act = _mm(dffn, w_down_f, name="mm_d_act", tb=True, tn=1408, out_dtype=BF16)
    dh, d_conv_b_p, d_conv_w_p = _conv_gate_bwd(hpre, d_act, conv_w_p, conv_b_p)
    dhpre = _conv_input_bwd(dh, conv_w_p)
    d_w_up_p = _mm(u2, dhpre, name="mm_d_w_up", ta=True, tn=1408, tk=2048, out_dtype=BF16)
    du2 = _mm(dhpre, w_up_p, name="mm_d_u2", tb=True, tk=2816)

    def ln1_bwd(du, dres, x1b, xb, fb, sc, gt, g):
        dx1 = dres + du * (1.0 + sc)
        xhat, rstd = _ln_stats(ALPHA * xb + (1.0 + gt) * fb)
        dr = _ln_bwd(dx1, xhat, rstd, g)
        return (dr * (1.0 + gt), ALPHA * dr,
                _colsum(du * x1b), _colsum(du), _colsum(dx1 * xhat), _colsum(dx1), _colsum(dr * fb))

    d_attn, dx_res, d_sc2, d_sh2, d_ln1_g, d_ln1_b, d_gt1 = _rowwise(
        "ln1_bwd", ln1_bwd,
        [(du2, D_MODEL, 0), (dx1_res, D_MODEL, 0), (x1, D_MODEL, 0), (x2, D_MODEL, 0), (attn_out, D_MODEL, 0)],
        [sc2, gt1, _row(ln1_g)], [(D_MODEL, BF16), (D_MODEL, F32)], sums=[D_MODEL] * 5)

    d_w_o = _mm(merged, d_attn, name="mm_d_w_o", ta=True, out_dtype=BF16)
    d_merged = _mm(d_attn, w_o_f, name="mm_d_merged", tb=True)

    def merge_bwd(dm, ga, gb, a, b):
        sa, sb = _sigmoid(ga), _sigmoid(gb)
        return dm * a * sa * (1.0 - sa), dm * b * sb * (1.0 - sb), dm * sa, dm * sb

    d_ga, d_gb, d_pf, d_ps = _rowwise(
        "gate_merge_bwd", merge_bwd,
        [(d_merged, D_MODEL, 0), (gates, D_MODEL, 0), (gates, D_MODEL, 1), (pf, D_MODEL, 0), (ps, D_MODEL, 0)], [],
        [(D_MODEL, BF16)] * 4)
    d_w_fox = _mm(y_fox, d_pf, name="mm_d_w_fox", ta=True, out_dtype=BF16)
    d_w_sb = _mm(y_sb, d_ps, name="mm_d_w_sb", ta=True, out_dtype=BF16)
    d_y_fox = _mm(d_pf, w_fox_f, name="mm_d_y_fox", tb=True, out_dtype=BF16)
    d_y_sb = _mm(d_ps, w_sb_f, name="mm_d_y_sb", tb=True, out_dtype=BF16)
    early_grads = _Ride(
        [_cols_to_slots(d_w_fox), _cols_to_slots(d_w_sb), d_w_o.reshape(N_DEV, D_MODEL // N_DEV, D_MODEL),
         _cols_to_slots(_unpair_cols(d_w_up_p)), d_w_down.reshape(N_DEV, D_FF // N_DEV, D_MODEL)], scatter=True)
    (dq_a, dk_a, dv_a, d_cum_row, d_cum_q), early_slots = _fox_bwd(qkv, cum_col, cum_row, y_fox32, lse, d_y_fox,
                                                                    ride=early_grads)
    dq_b, dk_b, dv_b = _sb_bwd(qkv, sb_run, d_y_sb)
    d_cum = jnp.transpose(d_cum_row.reshape(N_HEADS, s)) + d_cum_q[:, ::HEAD_DIM]
    d_cum = jnp.pad(d_cum, ((0, 0), (0, LANES - N_HEADS)))
    d_f, d_b_forget = _forget_bwd(d_cum, f_raw, b_f_pad)
    d_proj = jnp.concatenate([dq_a, dk_a, dv_a, dq_b, dk_b, dv_b, d_ga, d_gb, d_f,
                              jnp.zeros((s, W_PROJ - W_QKV - W_GATES - W_F), BF16)], axis=1)
    d_w_proj = _mm(u1, d_proj, name="mm_d_w_in", ta=True, tn=896, tk=2048, out_dtype=BF16)
    d_w_in_f = jnp.concatenate([d_w_proj[:, 0:1536], d_w_proj[:, 5120:5128], d_w_proj[:, 1536:3072],
                                d_w_proj[:, 3072:5120]], axis=1)
    du1, (in_slots,) = _mm(d_proj, w_proj, name="mm_d_u1", tb=True, tk=1792,
                           ride=_Ride([_cols_to_slots(d_w_in_f)], scatter=True))

    def x_bwd(du, dres, xb, sc):
        return dres + du * (1.0 + sc), _colsum(du * xb), _colsum(du)

    grad_x, d_sc1, d_sh1 = _rowwise("x_bwd", x_bwd, [(du1, D_MODEL, 0), (dx_res, D_MODEL, 0), (x2, D_MODEL, 0)],
                                    [sc1], [(D_MODEL, F32)], sums=[D_MODEL] * 2, tr=512)

    d_conv_b = _unpair_cols(d_conv_b_p)
    d_conv_w = _unpair_cols(d_conv_w_p)
    n_rep = N_MOD * D_MODEL + LANES + 4 * D_MODEL + 2 * D_FF
    small = jnp.concatenate(
        [d_sh1, d_sc1, d_gt1, d_sh2, d_sc2, d_gt2, d_b_forget, d_ln1_g, d_ln1_b, d_ln2_g, d_ln2_b, d_conv_b,
         d_conv_w.reshape(1, 6 * D_FF)], axis=1)
    n_small = small.shape[1] // LANES
    small = jnp.pad(small.reshape(n_small, LANES), ((0, 264 - n_small), (0, 0)))
    (small_all,) = _exchange("ag_small_grads", [small], scatter=False)
    rep8 = small_all[:, :n_rep // LANES, :]
    cw8 = small_all[:, n_rep // LANES:n_small, :].reshape(N_DEV, 3, 2 * D_FF)
    cw8 = lax.dynamic_slice(cw8, (0, 0, me * UP_SHARD), (N_DEV, 3, UP_SHARD))
    dmod8 = small_all[:, :N_MOD * D_MODEL // LANES, :].reshape(N_DEV, N_MOD * D_MODEL)
    dmod_loc = lax.dynamic_slice(dmod8, (0, me * ADA_SHARD), (N_DEV, ADA_SHARD))

    def pack_rep(b_a, b_f, g1, b1, g2, b2, cb):
        flat = jnp.concatenate([b_a, jnp.pad(b_f, (0, LANES - N_HEADS)), g1, b1, g2, b2, cb])
        return flat.reshape(n_rep // LANES, LANES)

    rep = _adamw("adamw_small", rep8, pack_rep(b_ada, b_forget, ln1_g, ln1_b, ln2_g, ln2_b, conv_b),
                 pack_rep(m_b_ada, m_b_forget, m_ln1_g, m_ln1_b, m_ln2_g, m_ln2_b, m_conv_b),
                 pack_rep(v_b_ada, v_b_forget, v_ln1_g, v_ln1_b, v_ln2_g, v_ln2_b, v_conv_b))

    def unpack_rep(p):
        flat = p.reshape(-1)
        o = N_MOD * D_MODEL
        return {"b_ada": flat[:o], "b_forget": flat[o:o + N_HEADS],
                "ln1_g": flat[o + 128:o + 1152], "ln1_b": flat[o + 1152:o + 2176],
                "ln2_g": flat[o + 2176:o + 3200], "ln2_b": flat[o + 3200:o + 4224], "conv_b": flat[o + 4224:]}

    rep = [unpack_rep(p) for p in rep]
    r_conv_w = _adamw("adamw_conv_w", cw8, conv_w, m_conv_w, v_conv_w)
    r_ada = _adamw_ada(jnp.transpose(c_all.reshape(N_DEV, D_MODEL)), dmod_loc, w_ada, m_w_ada, v_w_ada)

    r_in = _adamw("adamw_w_in", in_slots, w_in, m_w_in, v_w_in)
    r_fox = _adamw("adamw_w_fox", early_slots[0], w_fox_proj, m_w_fox_proj, v_w_fox_proj)
    r_sb = _adamw("adamw_w_sb", early_slots[1], w_sb_proj, m_w_sb_proj, v_w_sb_proj)
    r_o = _adamw("adamw_w_o", early_slots[2], w_o, m_w_o, v_w_o)
    r_up = _adamw("adamw_w_up", early_slots[3], w_up, m_w_up, v_w_up)
    r_down = _adamw("adamw_w_down", early_slots[4], w_down, m_w_down, v_w_down)

    def leaf(i):
        return [r_ada[i], rep[i]["b_ada"], r_in[i], rep[i]["b_forget"], r_fox[i], r_sb[i], r_o[i], rep[i]["ln1_g"],
                rep[i]["ln1_b"], r_up[i], r_conv_w[i], rep[i]["conv_b"], r_down[i], rep[i]["ln2_g"], rep[i]["ln2_b"]]

    return (loss, grad_x.reshape(1, s, D_MODEL), *leaf(0), *leaf(1), *leaf(2), *leaf(3))
```

```python
import functools

import jax
import jax.numpy as jnp
from jax import lax
from jax.experimental import pallas as pl
from jax.experimental.pallas import tpu as pltpu

F32 = jnp.float32
BF16 = jnp.bfloat16
MESH = pl.DeviceIdType.MESH
ANY = pl.BlockSpec(memory_space=pl.ANY)

N_DEV = 8
D_MODEL = 1024
HEAD_DIM = 64
N_HEADS = 8
ATTN_W = N_HEADS * HEAD_DIM
D_FF = 2816
FF_HALF = D_FF // 2
N_MOD = 6
ADA_SHARD = N_MOD * D_MODEL // N_DEV
IN_SHARD = 641
UP_SHARD = 704
ATTN_SCALE = HEAD_DIM ** -0.5
ALPHA = 2.0 ** 0.25
LN_EPS = 1e-5
LANES = 128
TQ = 512
SCAN_W = 256
VMEM_LIMIT = 56 * 1024 * 1024

ADAM_LR, ADAM_B1, ADAM_B2, ADAM_EPS, ADAM_WD, ADAM_STEP = 0.001, 0.9, 0.999, 1e-08, 0.01, 10

W_QKV, W_GATES, W_F = 3072, 2048, 128
W_PROJ = 5376


def _params(sem=None):
    return pltpu.CompilerParams(dimension_semantics=sem, vmem_limit_bytes=VMEM_LIMIT)


def _tile(n, cap):
    if n <= cap:
        return n
    best = None
    for t in range(LANES, cap + 1, LANES):
        if n % t == 0:
            best = t
    assert best is not None, (n, cap)
    return best


def _row_tile(r, width, budget=192 * 1024):
    if r * width <= budget or r % 16:
        return r
    best = 16
    for t in range(16, r + 1, 16):
        if r % t == 0 and t * width <= budget:
            best = t
    return best


def _me():
    x, y, c = lax.axis_index("x"), lax.axis_index("y"), lax.axis_index("c")
    return x, y, c, 4 * x + 2 * y + c


def _peer(r):
    x, y, c, _ = _me()
    px = 1 - x if r & 4 else x
    py = 1 - y if r & 2 else y
    pc = 1 - c if r & 1 else c
    return (px, py, pc), 4 * px + 2 * py + pc


class _Ride:
    def __init__(self, arrays, scatter):
        self.arrays, self.scatter, self.n = list(arrays), scatter, len(arrays)
        self.in_specs = [ANY] * self.n
        self.out_specs = [ANY] * self.n
        self.out_shape = [jax.ShapeDtypeStruct(a.shape if scatter else (N_DEV,) + a.shape, a.dtype) for a in arrays]
        self.scratch = [pltpu.SemaphoreType.DMA((self.n, N_DEV - 1)), pltpu.SemaphoreType.DMA((self.n, N_DEV - 1)),
                        pltpu.SemaphoreType.DMA((self.n,))]

    def _local(self, ins, outs, sems, a):
        me = _me()[3]
        return pltpu.make_async_copy(ins[a].at[me] if self.scatter else ins[a], outs[a].at[me], sems[2].at[a])

    def _remote(self, ins, outs, sems, a, r, arriving):
        me = _me()[3]
        peer, pidx = _peer(r)
        src = ins[a].at[me if arriving else pidx] if self.scatter else ins[a]
        return pltpu.make_async_remote_copy(
            src_ref=src, dst_ref=outs[a].at[pidx if arriving else me], send_sem=sems[0].at[a, r - 1],
            recv_sem=sems[1].at[a, r - 1], device_id=peer, device_id_type=MESH)

    def start(self, ins, outs, sems):
        for a in range(self.n):
            self._local(ins, outs, sems, a).start()
        for r in range(1, N_DEV):
            for a in range(self.n):
                self._remote(ins, outs, sems, a, r, False).start()

    def wait(self, ins, outs, sems):
        for r in range(1, N_DEV):
            for a in range(self.n):
                self._remote(ins, outs, sems, a, r, True).wait_recv()
        for r in range(1, N_DEV):
            for a in range(self.n):
                self._remote(ins, outs, sems, a, r, False).wait_send()
        for a in range(self.n):
            self._local(ins, outs, sems, a).wait()


def _exchange(name, arrays, scatter):
    ride = _Ride(arrays, scatter)

    def body(*refs):
        ins, outs, sems = refs[:ride.n], refs[ride.n:2 * ride.n], refs[2 * ride.n:]
        ride.start(ins, outs, sems)
        ride.wait(ins, outs, sems)

    return pl.pallas_call(body, name=name, in_specs=ride.in_specs, out_specs=ride.out_specs, out_shape=ride.out_shape,
                          scratch_shapes=ride.scratch)(*arrays)


def _with_ride(body, ride, n_in, n_out, grid):
    if ride is None:
        return body
    n = ride.n

    def wrapped(*refs):
        ins, rins = refs[:n_in], refs[n_in:n_in + n]
        outs, routs = refs[n_in + n:n_in + n + n_out], refs[n_in + n + n_out:n_in + 2 * n + n_out]
        rest = refs[n_in + 2 * n + n_out:]
        scratch, sems = rest[:len(rest) - 3], rest[len(rest) - 3:]
        ids = [pl.program_id(d) for d in range(len(grid))]
        first = functools.reduce(lambda p, q: p & q, [i == 0 for i in ids])
        last = functools.reduce(lambda p, q: p & q, [i == g - 1 for i, g in zip(ids, grid)])

        @pl.when(first)
        def _():
            ride.start(rins, routs, sems)

        body(*ins, *outs, *scratch)

        @pl.when(last)
        def _():
            ride.wait(rins, routs, sems)

    return wrapped


def _ride_call(body, ride, *, name, grid, in_specs, out_specs, out_shape, scratch_shapes, sem, args):
    n_in, n_out = len(in_specs), len(out_specs)
    if ride is None:
        res = pl.pallas_call(body, name=name, grid=grid, in_specs=in_specs, out_specs=out_specs, out_shape=out_shape,
                             scratch_shapes=scratch_shapes, compiler_params=_params(sem))(*args)
        return list(res), []
    res = pl.pallas_call(
        _with_ride(body, ride, n_in, n_out, grid), name=name, grid=grid,
        in_specs=list(in_specs) + ride.in_specs, out_specs=list(out_specs) + ride.out_specs,
        out_shape=list(out_shape) + ride.out_shape, scratch_shapes=list(scratch_shapes) + ride.scratch,
        compiler_params=_params(("arbitrary",) * len(grid)))(*args, *ride.arrays)
    return list(res[:n_out]), list(res[n_out:])


def _mm(a, b, *, name, ta=False, tb=False, out_dtype=F32, tm=1024, tn=1024, tk=1024, ride=None):
    m, k = (a.shape[1], a.shape[0]) if ta else a.shape
    n = b.shape[0] if tb else b.shape[1]
    assert (b.shape[1] if tb else b.shape[0]) == k
    tm, tn, tk = _tile(m, tm), _tile(n, tn), _tile(k, tk)
    nk = k // tk
    a_spec = pl.BlockSpec((tk, tm), lambda i, j, l: (l, i)) if ta else pl.BlockSpec((tm, tk), lambda i, j, l: (i, l))
    b_spec = pl.BlockSpec((tn, tk), lambda i, j, l: (j, l)) if tb else pl.BlockSpec((tk, tn), lambda i, j, l: (l, j))
    dims = (((0,) if ta else (1,), (1,) if tb else (0,)), ((), ()))

    def body(a_ref, b_ref, o_ref, *acc):
        p = lax.dot_general(a_ref[...].astype(BF16), b_ref[...].astype(BF16), dims, preferred_element_type=F32)
        if nk == 1:
            o_ref[...] = p.astype(out_dtype)
            return
        acc_ref = acc[0]
        step = pl.program_id(2)

        @pl.when(step == 0)
        def _():
            acc_ref[...] = p

        @pl.when(step > 0)
        def _():
            acc_ref[...] += p

        @pl.when(step == nk - 1)
        def _():
            o_ref[...] = acc_ref[...].astype(out_dtype)

    outs, rode = _ride_call(
        body, ride, name=name, grid=(m // tm, n // tn, nk), in_specs=[a_spec, b_spec],
        out_specs=[pl.BlockSpec((tm, tn), lambda i, j, l: (i, j))], out_shape=[jax.ShapeDtypeStruct((m, n), out_dtype)],
        scratch_shapes=[] if nk == 1 else [pltpu.VMEM((tm, tn), F32)], sem=("parallel", "parallel", "arbitrary"),
        args=(a, b))
    return outs[0] if ride is None else (outs[0], rode)


def _rowwise(name, fn, rows, vecs, outs, sums=(), tr=256):
    s = rows[0][0].shape[0]
    tr = min(tr, s)
    nr, nv, no = len(rows), len(vecs), len(outs)

    def body(*refs):
        vals = [r[...] for r in refs[:nr + nv]]
        res = fn(*vals)
        for o_ref, val in zip(refs[nr + nv:nr + nv + no], res[:no]):
            o_ref[...] = val.astype(o_ref.dtype)
        step = pl.program_id(0)
        for s_ref, val in zip(refs[nr + nv + no:], res[no:]):
            @pl.when(step == 0)
            def _(s_ref=s_ref, val=val):
                s_ref[...] = val

            @pl.when(step > 0)
            def _(s_ref=s_ref, val=val):
                s_ref[...] += val

    in_specs = [pl.BlockSpec((tr, w), functools.partial(lambda i, cb: (i, cb), cb=cb)) for _, w, cb in rows]
    in_specs += [pl.BlockSpec(v.shape, lambda i: (0, 0)) for v in vecs]
    out_specs = [pl.BlockSpec((tr, w), lambda i: (i, 0)) for w, _ in outs]
    out_specs += [pl.BlockSpec((1, w), lambda i: (0, 0)) for w in sums]
    out_shape = [jax.ShapeDtypeStruct((s, w), dt) for w, dt in outs]
    out_shape += [jax.ShapeDtypeStruct((1, w), F32) for w in sums]
    return pl.pallas_call(
        body, name=name, grid=(s // tr,), in_specs=in_specs, out_specs=out_specs, out_shape=out_shape,
        compiler_params=_params(("arbitrary",) if sums else ("parallel",)),
    )(*[r[0] for r in rows], *vecs)


def _colsum(x):
    return jnp.sum(x, axis=0, keepdims=True)


def _sigmoid(x):
    return 1.0 / (1.0 + jnp.exp(-x))


def _log_sigmoid(x):
    return jnp.minimum(x, 0.0) - jnp.log(1.0 + jnp.exp(-jnp.abs(x)))


def _ln_stats(r):
    mu = jnp.mean(r, axis=-1, keepdims=True)
    xc = r - mu
    var = jnp.mean(xc * xc, axis=-1, keepdims=True)
    rstd = lax.rsqrt(var + LN_EPS)
    return xc * rstd, rstd


def _ln_bwd(dy, xhat, rstd, g):
    dxh = dy * g
    m1 = jnp.mean(dxh, axis=-1, keepdims=True)
    m2 = jnp.mean(dxh * xhat, axis=-1, keepdims=True)
    return rstd * (dxh - m1 - xhat * m2)


def _mod_exchange(c_row, w_ada, b_ada_loc):
    def body(c_ref, w_ref, b_ref, call_ref, mod_ref, piece_ref, send_sems, recv_sems):
        me = _me()[3]
        call_ref[me] = c_ref[...]
        sent = []
        for r in range(1, N_DEV):
            peer, _ = _peer(r)
            cp = pltpu.make_async_remote_copy(
                src_ref=c_ref, dst_ref=call_ref.at[me], send_sem=send_sems.at[0, r - 1],
                recv_sem=recv_sems.at[0, r - 1], device_id=peer, device_id_type=MESH)
            cp.start()
            sent.append(cp)
        for r in range(1, N_DEV):
            peer, pidx = _peer(r)
            pltpu.make_async_remote_copy(
                src_ref=c_ref, dst_ref=call_ref.at[pidx], send_sem=send_sems.at[0, r - 1],
                recv_sem=recv_sems.at[0, r - 1], device_id=peer, device_id_type=MESH).wait_recv()
        c_all = jnp.concatenate([call_ref[d] for d in range(N_DEV)], axis=0)
        mod_loc = jnp.dot(c_all, w_ref[...], preferred_element_type=F32,
                          precision=lax.Precision.HIGHEST) + b_ref[...]
        for d in range(N_DEV):
            piece_ref[d] = mod_loc[d:d + 1, :]
        mod_ref[me] = piece_ref[me]
        for r in range(1, N_DEV):
            peer, pidx = _peer(r)
            cp = pltpu.make_async_remote_copy(
                src_ref=piece_ref.at[pidx], dst_ref=mod_ref.at[me], send_sem=send_sems.at[1, r - 1],
                recv_sem=recv_sems.at[1, r - 1], device_id=peer, device_id_type=MESH)
            cp.start()
            sent.append(cp)
        for r in range(1, N_DEV):
            peer, pidx = _peer(r)
            pltpu.make_async_remote_copy(
                src_ref=piece_ref.at[me], dst_ref=mod_ref.at[pidx], send_sem=send_sems.at[1, r - 1],
                recv_sem=recv_sems.at[1, r - 1], device_id=peer, device_id_type=MESH).wait_recv()
        for cp in sent:
            cp.wait_send()

    vmem = pl.BlockSpec(memory_space=pltpu.VMEM)
    return pl.pallas_call(
        body, name="mod_exchange", in_specs=[vmem, vmem, vmem], out_specs=[vmem, vmem],
        out_shape=[jax.ShapeDtypeStruct((N_DEV, 1, D_MODEL), F32), jax.ShapeDtypeStruct((N_DEV, 1, ADA_SHARD), F32)],
        scratch_shapes=[pltpu.VMEM((N_DEV, 1, ADA_SHARD), F32),
                        pltpu.SemaphoreType.DMA((2, N_DEV - 1)), pltpu.SemaphoreType.DMA((2, N_DEV - 1))],
        compiler_params=_params(),
    )(c_row, w_ada, b_ada_loc)


def _split3(x):
    hi = x.astype(BF16)
    r1 = x - hi.astype(F32)
    mid = r1.astype(BF16)
    lo = (r1 - mid.astype(F32)).astype(BF16)
    return hi, mid, lo


def _scan_rows(x_ref, o_ref, s, reverse, pre=None, post=None):
    tb = min(TQ, s)
    nb = s // tb
    row = lax.broadcasted_iota(jnp.int32, (tb, tb), 0)
    col = lax.broadcasted_iota(jnp.int32, (tb, tb), 1)
    tri = jnp.where((col >= row) if reverse else (col <= row), 1.0, 0.0).astype(BF16)

    def step(i, carry):
        blk = (nb - 1 - i) if reverse else i
        off = pl.multiple_of(blk * tb, tb)
        x = x_ref[pl.ds(off, tb), :]
        if pre is not None:
            x = pre(x, off)
        acc = carry
        for piece in _split3(x):
            acc = acc + jnp.dot(tri, piece, preferred_element_type=F32)
        o_ref[pl.ds(off, tb), :] = acc if post is None else post(acc, off)
        edge = acc[0:1, :] if reverse else acc[tb - 1:tb, :]
        return jnp.broadcast_to(edge, (tb, LANES))

    lax.fori_loop(0, nb, step, jnp.zeros((tb, LANES), F32))


def _forget_cumsum(f_raw, b_pad):
    s = f_raw.shape[0]

    def body(f_ref, b_ref, cum_ref):
        b = b_ref[...]
        _scan_rows(f_ref, cum_ref, s, False, pre=lambda x, off: _log_sigmoid(x + b))

    vmem = pl.BlockSpec(memory_space=pltpu.VMEM)
    return pl.pallas_call(body, name="forget_cumsum", in_specs=[vmem, vmem], out_specs=vmem,
                          out_shape=jax.ShapeDtypeStruct((s, LANES), F32), compiler_params=_params())(f_raw, b_pad)


def _forget_bwd(dcum, f_raw, b_pad):
    s = f_raw.shape[0]

    def body(d_ref, f_ref, b_ref, df_ref, db_ref, tmp_ref):
        b = b_ref[...]
        _scan_rows(d_ref, tmp_ref, s, True)
        df = tmp_ref[...] * _sigmoid(-(f_ref[...] + b))
        df_ref[...] = df.astype(BF16)
        db_ref[...] = _colsum(df)

    vmem = pl.BlockSpec(memory_space=pltpu.VMEM)
    return pl.pallas_call(
        body, name="forget_bwd", in_specs=[vmem, vmem, vmem], out_specs=[vmem, vmem],
        out_shape=[jax.ShapeDtypeStruct((s, LANES), BF16), jax.ShapeDtypeStruct((1, LANES), F32)],
        scratch_shapes=[pltpu.VMEM((s, LANES), F32)], compiler_params=_params())(dcum, f_raw, b_pad)


def _dot_nt(a, b):
    return lax.dot_general(a, b, (((1,), (1,)), ((), ())), preferred_element_type=F32)


def _head_masks():
    lane = lax.broadcasted_iota(jnp.int32, (TQ, LANES), 1)
    return lane, [lane < HEAD_DIM, lane >= HEAD_DIM]


def _pick(mask, x):
    return jnp.where(mask, x, jnp.zeros_like(x))


def _qkv_specs(s, col0):
    nb = ATTN_W // LANES
    return [pl.BlockSpec((TQ, LANES), lambda hp, qi: (qi, col0 + hp)),
            pl.BlockSpec((s, LANES), lambda hp, qi: (0, col0 + nb + hp)),
            pl.BlockSpec((s, LANES), lambda hp, qi: (0, col0 + 2 * nb + hp))]


def _pair_spec():
    return pl.BlockSpec((TQ, LANES), lambda hp, qi: (qi, hp))


def _fox_fwd(qkv, cum_col, cum_row, ride=None):
    s = qkv.shape[0]
    nq = s // TQ

    def body(q_ref, k_ref, v_ref, cc_ref, cr_ref, o_ref, o32_ref, lse_ref):
        hp, qi = pl.program_id(0), pl.program_id(1)
        lane, masks = _head_masks()
        row = lax.broadcasted_iota(jnp.int32, (TQ, TQ), 0)
        col = lax.broadcasted_iota(jnp.int32, (TQ, TQ), 1)
        causal = col <= row
        q2 = q_ref[...] * jnp.asarray(ATTN_SCALE, BF16)
        cc = cc_ref[...]
        qms = [_pick(masks[e], q2) for e in range(2)]
        cqs = [jnp.sum(jnp.where(lane == 2 * hp + e, cc, 0.0), axis=1, keepdims=True) for e in range(2)]

        def tile(kb, carry, masked, width=1):
            off, span = pl.multiple_of(kb * TQ, TQ), width * TQ
            k2, v2 = k_ref[pl.ds(off, span), :], v_ref[pl.ds(off, span), :]
            head0 = lax.broadcasted_iota(jnp.int32, (span, LANES), 1) < HEAD_DIM
            new = []
            for e in range(2):
                m, acc = carry[e]
                sc = _dot_nt(qms[e], k2) + (cqs[e] - cr_ref[e:e + 1, pl.ds(off, span)])
                if masked:
                    sc = jnp.where(causal, sc, -jnp.inf)
                m_new = jnp.maximum(m, jnp.max(sc, axis=1, keepdims=True))
                p = jnp.exp(sc - m_new)
                v_ones = jnp.where(head0 if e == 0 else ~head0, v2, jnp.ones_like(v2))
                acc = jnp.exp(m - m_new) * acc + jnp.dot(p.astype(BF16), v_ones, preferred_element_type=F32)
                new.append((m_new, acc))
            return tuple(new)

        init = (jnp.full((TQ, 1), -jnp.inf, F32), jnp.zeros((TQ, LANES), F32))
        carry = lax.fori_loop(0, qi // 2, lambda j, cr: tile(2 * j, cr, False, 2), tile(qi, (init, init), True))
        carry = lax.cond(qi % 2 == 1, lambda cr: tile(qi - 1, cr, False), lambda cr: cr, carry)
        sums = [jnp.max(jnp.where(masks[1 - e], carry[e][1], 0.0), axis=1, keepdims=True) for e in range(2)]
        outs = [carry[e][1] / sums[e] for e in range(2)]
        lses = [carry[e][0] + jnp.log(sums[e]) for e in range(2)]
        out = jnp.where(masks[0], outs[0], outs[1])
        o_ref[...] = out.astype(BF16)
        o32_ref[...] = out
        lse_ref[...] = jnp.where(masks[0], lses[0], lses[1])

    return _ride_call(
        body, ride, name="fox_fwd", grid=(N_HEADS // 2, nq),
        in_specs=_qkv_specs(s, 0) + [pl.BlockSpec((TQ, LANES), lambda hp, qi: (qi, 0)),
                                     pl.BlockSpec((None, 2, s), lambda hp, qi: (hp, 0, 0))],
        out_specs=[_pair_spec(), _pair_spec(), _pair_spec()],
        out_shape=[jax.ShapeDtypeStruct((s, ATTN_W), BF16), jax.ShapeDtypeStruct((s, ATTN_W), F32),
                   jax.ShapeDtypeStruct((s, ATTN_W), F32)],
        scratch_shapes=[], sem=("parallel", "parallel"), args=(qkv, qkv, qkv, cum_col, cum_row))


def _write_transposed(acc_ref, out_ref):
    for c in range(out_ref.shape[0] // TQ):
        out_ref[c * TQ:(c + 1) * TQ, :] = jnp.transpose(acc_ref[:, c * TQ:(c + 1) * TQ]).astype(BF16)


def _fox_bwd(qkv, cum_col, cum_row, o, lse, do, ride=None):
    s = qkv.shape[0]
    nq = s // TQ

    def body(q_ref, k_ref, v_ref, cc_ref, cr_ref, o_ref, lse_ref, do_ref,
             dq_ref, dk_ref, dv_ref, dcr_ref, dcq_ref, dk_acc, dv_acc):
        hp, qi = pl.program_id(0), pl.program_id(1)

        @pl.when(qi == 0)
        def _():
            dk_acc[...] = jnp.zeros_like(dk_acc)
            dv_acc[...] = jnp.zeros_like(dv_acc)
            dcr_ref[...] = jnp.zeros_like(dcr_ref)

        lane, masks = _head_masks()
        row = lax.broadcasted_iota(jnp.int32, (TQ, TQ), 0)
        col = lax.broadcasted_iota(jnp.int32, (TQ, TQ), 1)
        causal = col <= row
        q2 = q_ref[...] * jnp.asarray(ATTN_SCALE, BF16)
        do2 = do_ref[...]
        prod = do2.astype(F32) * o_ref[...].astype(F32)
        lse2 = lse_ref[...]
        cc = cc_ref[...]
        qms = [_pick(masks[e], q2) for e in range(2)]
        doms = [_pick(masks[e], do2) for e in range(2)]
        deltas = [jnp.sum(jnp.where(masks[e], prod, 0.0), axis=1, keepdims=True) for e in range(2)]
        lses = [jnp.max(jnp.where(masks[e], lse2, -jnp.inf), axis=1, keepdims=True) for e in range(2)]
        cqs = [jnp.sum(jnp.where(lane == 2 * hp + e, cc, 0.0), axis=1, keepdims=True) for e in range(2)]
        qts = [jnp.transpose(qms[e].astype(F32)).astype(BF16) for e in range(2)]
        dots = [jnp.transpose(doms[e].astype(F32)).astype(BF16) for e in range(2)]

        def tile(kb, carry, masked, width=1):
            off, span = pl.multiple_of(kb * TQ, TQ), width * TQ
            k2, v2 = k_ref[pl.ds(off, span), :], v_ref[pl.ds(off, span), :]
            new, dk, dv = [], None, None
            for e in range(2):
                dq, rowsum = carry[e]
                sc = _dot_nt(qms[e], k2) + (cqs[e] - cr_ref[e:e + 1, pl.ds(off, span)])
                p = jnp.exp(sc - lses[e])
                if masked:
                    p = jnp.where(causal, p, 0.0)
                ds = p * (_dot_nt(doms[e], v2) - deltas[e])
                dsb = ds.astype(BF16)
                dk_e = jnp.dot(qts[e], dsb, preferred_element_type=F32)
                dv_e = jnp.dot(dots[e], p.astype(BF16), preferred_element_type=F32)
                dk, dv = (dk_e, dv_e) if e == 0 else (dk + dk_e, dv + dv_e)
                dcr_ref[e:e + 1, pl.ds(off, span)] -= _colsum(ds)
                new.append((dq + jnp.dot(dsb, k2, preferred_element_type=F32),
                            rowsum + jnp.sum(ds, axis=1, keepdims=True)))
            dk_acc[:, pl.ds(off, span)] += dk
            dv_acc[:, pl.ds(off, span)] += dv
            return tuple(new)

        init = (jnp.zeros((TQ, LANES), F32), jnp.zeros((TQ, 1), F32))
        carry = lax.fori_loop(0, qi // 2, lambda j, cr: tile(2 * j, cr, False, 2), (init, init))
        carry = lax.cond(qi % 2 == 1, lambda cr: tile(qi - 1, cr, False), lambda cr: cr, carry)
        carry = tile(qi, carry, True)
        dq_ref[...] = (jnp.where(masks[0], carry[0][0], carry[1][0]) * ATTN_SCALE).astype(BF16)
        dcq_ref[...] = jnp.where(masks[0], carry[0][1], carry[1][1])

        @pl.when(qi == nq - 1)
        def _():
            _write_transposed(dk_acc, dk_ref)
            _write_transposed(dv_acc, dv_ref)

    seq_spec = pl.BlockSpec((s, LANES), lambda hp, qi: (0, hp))
    return _ride_call(
        body, ride, name="fox_bwd", grid=(N_HEADS // 2, nq),
        in_specs=_qkv_specs(s, 0) + [pl.BlockSpec((TQ, LANES), lambda hp, qi: (qi, 0)),
                                     pl.BlockSpec((None, 2, s), lambda hp, qi: (hp, 0, 0)),
                                     _pair_spec(), _pair_spec(), _pair_spec()],
        out_specs=[_pair_spec(), seq_spec, seq_spec, pl.BlockSpec((None, 2, s), lambda hp, qi: (hp, 0, 0)),
                   _pair_spec()],
        out_shape=[jax.ShapeDtypeStruct((s, ATTN_W), BF16)] * 3 + [jax.ShapeDtypeStruct((N_HEADS // 2, 2, s), F32),
                                                                    jax.ShapeDtypeStruct((s, ATTN_W), F32)],
        scratch_shapes=[pltpu.VMEM((LANES, s), F32), pltpu.VMEM((LANES, s), F32)],
        sem=("parallel", "arbitrary"), args=(qkv, qkv, qkv, cum_col, cum_row, o, lse, do))


def _scan_matrix(reverse):
    row = lax.broadcasted_iota(jnp.int32, (SCAN_W, SCAN_W), 0)
    col = lax.broadcasted_iota(jnp.int32, (SCAN_W, SCAN_W), 1)
    return jnp.where((row > col) if reverse else (row < col), 1.0, 0.0).astype(BF16)


def _scan_cols(x, tri, reverse, init):
    nblk = x.shape[1] // SCAN_W
    parts, total = [None] * nblk, init
    far = 0 if reverse else SCAN_W - 1
    for b in (reversed(range(nblk)) if reverse else range(nblk)):
        blk = x[:, b * SCAN_W:(b + 1) * SCAN_W]
        part = jnp.dot(blk.astype(BF16), tri, preferred_element_type=F32)
        parts[b] = part + total
        total = total + (part[:, far:far + 1] + blk[:, far:far + 1])
    return (parts[0] if nblk == 1 else jnp.concatenate(parts, axis=1)), total


def _sb_logits(qm, k2):
    z = _dot_nt(qm, k2)
    neg_abs = lax.bitcast_convert_type(lax.bitcast_convert_type(z, jnp.uint32) | jnp.uint32(0x80000000), F32)
    soft = jnp.log(1.0 + jnp.exp(neg_abs))
    lb = jnp.minimum(z, 0.0) - soft
    return lb, lb - z


TILE_SLOTS = 4


def _tri_base(qi):
    return (qi * (qi + 1)) // 2


def _sb_fwd(qkv):
    s = qkv.shape[0]
    nq = s // TQ

    def body(q_ref, k_ref, v_ref, o_ref, t_ref, buf, sems):
        hp, qi = pl.program_id(0), pl.program_id(1)
        _, masks = _head_masks()
        row = lax.broadcasted_iota(jnp.int32, (TQ, TQ), 0)
        col = lax.broadcasted_iota(jnp.int32, (TQ, TQ), 1)
        strict = col < row
        suffix = _scan_matrix(True)
        q2 = q_ref[...] * jnp.asarray(ATTN_SCALE, BF16)
        qms = [_pick(masks[e], q2) for e in range(2)]
        base = _tri_base(qi)

        def store(e, kb):
            slot = kb % TILE_SLOTS
            return pltpu.make_async_copy(buf.at[e, slot], t_ref.at[2 * hp + e, base + kb], sems.at[e, slot])

        def tile(kb, carry, masked, width=1):
            off, span = pl.multiple_of(kb * TQ, TQ), width * TQ
            k2, v2 = k_ref[pl.ds(off, span), :], v_ref[pl.ds(off, span), :]
            new = []
            for e in range(2):
                run, acc = carry[e]
                lb, lo = _sb_logits(qms[e], k2)
                if masked:
                    lo = jnp.where(strict, lo, 0.0)
                rest, run = _scan_cols(lo, suffix, True, run)
                a = jnp.exp(lb + rest)
                if masked:
                    a = jnp.where(strict, a, 0.0)
                ab, lbb = a.astype(BF16), lb.astype(BF16)
                acc = acc + jnp.dot(ab, v2, preferred_element_type=F32)
                for w in range(width):
                    blk = kb + w

                    @pl.when(blk + TILE_SLOTS <= qi)
                    def _(e=e, blk=blk):
                        store(e, blk + TILE_SLOTS).wait()
                    buf[e, blk % TILE_SLOTS, 0] = ab[:, w * TQ:(w + 1) * TQ]
                    buf[e, blk % TILE_SLOTS, 1] = lbb[:, w * TQ:(w + 1) * TQ]
                    store(e, blk).start()
                new.append((run, acc))
            return tuple(new)

        init = (jnp.zeros((TQ, 1), F32), jnp.zeros((TQ, LANES), F32))
        carry = tile(qi, (init, init), True)
        carry = lax.cond(qi % 2 == 1, lambda cr: tile(qi - 1, cr, False), lambda cr: cr, carry)
        pairs = qi // 2
        carry = lax.fori_loop(0, pairs, lambda it, cr: tile(2 * (pairs - 1 - it), cr, False, 2), carry)
        for e in range(2):
            for blk in range(TILE_SLOTS):
                @pl.when(qi >= blk)
                def _(e=e, blk=blk):
                    store(e, blk).wait()
        o_ref[...] = jnp.where(masks[0], carry[0][1], carry[1][1]).astype(BF16)

    ntri = nq * (nq + 1) // 2
    return pl.pallas_call(
        body, name="sb_fwd", grid=(N_HEADS // 2, nq), in_specs=_qkv_specs(s, 3 * ATTN_W // LANES),
        out_specs=[_pair_spec(), ANY],
        out_shape=[jax.ShapeDtypeStruct((s, ATTN_W), BF16), jax.ShapeDtypeStruct((N_HEADS, ntri, 2, TQ, TQ), BF16)],
        scratch_shapes=[pltpu.VMEM((2, TILE_SLOTS, 2, TQ, TQ), BF16), pltpu.SemaphoreType.DMA((2, TILE_SLOTS))],
        compiler_params=_params(("arbitrary", "arbitrary")),
    )(qkv, qkv, qkv)


def _sb_bwd(qkv, tiles, do):
    s = qkv.shape[0]
    nq = s // TQ

    def body(q_ref, k_ref, v_ref, t_ref, do_ref, dq_ref, dk_ref, dv_ref, dk_acc, dv_acc, buf, sems):
        hp, qi = pl.program_id(0), pl.program_id(1)

        @pl.when(qi == 0)
        def _():
            dk_acc[...] = jnp.zeros_like(dk_acc)
            dv_acc[...] = jnp.zeros_like(dv_acc)

        _, masks = _head_masks()
        row = lax.broadcasted_iota(jnp.int32, (TQ, TQ), 0)
        col = lax.broadcasted_iota(jnp.int32, (TQ, TQ), 1)
        strict = col < row
        prefix = _scan_matrix(False)
        q2 = q_ref[...] * jnp.asarray(ATTN_SCALE, BF16)
        do2 = do_ref[...]
        qms = [_pick(masks[e], q2) for e in range(2)]
        doms = [_pick(masks[e], do2) for e in range(2)]
        qts = [jnp.transpose(qms[e].astype(F32)).astype(BF16) for e in range(2)]
        dots = [jnp.transpose(doms[e].astype(F32)).astype(BF16) for e in range(2)]
        base = _tri_base(qi)

        def fetch(e, kb):
            slot = kb % TILE_SLOTS
            return pltpu.make_async_copy(t_ref.at[2 * hp + e, base + kb], buf.at[e, slot], sems.at[e, slot])

        for e in range(2):
            fetch(e, 0).start()

            @pl.when(qi >= 1)
            def _(e=e):
                fetch(e, 1).start()

        def tile(kb, carry, masked, width=1):
            off, span = pl.multiple_of(kb * TQ, TQ), width * TQ
            k2, v2 = k_ref[pl.ds(off, span), :], v_ref[pl.ds(off, span), :]
            new, dk, dv = [], None, None
            for e in range(2):
                gsum, dq = carry[e]
                if not masked:
                    for blk in range(2, 2 + width):
                        @pl.when(kb + blk <= qi)
                        def _(e=e, blk=blk):
                            fetch(e, kb + blk).start()
                for w in range(width):
                    fetch(e, kb + w).wait()
                slots = [(kb + w) % TILE_SLOTS for w in range(width)]
                ab = buf[e, slots[0], 0] if width == 1 else jnp.concatenate([buf[e, sl, 0] for sl in slots], axis=1)
                lbb = buf[e, slots[0], 1] if width == 1 else jnp.concatenate([buf[e, sl, 1] for sl in slots], axis=1)
                beta = jnp.exp(lbb.astype(F32))
                g = ab.astype(F32) * _dot_nt(doms[e], v2)
                before, gsum = _scan_cols(g, prefix, False, gsum)
                dz = g - beta * (g + before)
                if masked:
                    dz = jnp.where(strict, dz, 0.0)
                dzb = dz.astype(BF16)
                dk_e = jnp.dot(qts[e], dzb, preferred_element_type=F32)
                dv_e = jnp.dot(dots[e], ab, preferred_element_type=F32)
                dk, dv = (dk_e, dv_e) if e == 0 else (dk + dk_e, dv + dv_e)
                new.append((gsum, dq + jnp.dot(dzb, k2, preferred_element_type=F32)))
            dk_acc[:, pl.ds(off, span)] += dk
            dv_acc[:, pl.ds(off, span)] += dv
            return tuple(new)

        init = (jnp.zeros((TQ, 1), F32), jnp.zeros((TQ, LANES), F32))
        carry = lax.fori_loop(0, qi // 2, lambda j, cr: tile(2 * j, cr, False, 2), (init, init))
        carry = lax.cond(qi % 2 == 1, lambda cr: tile(qi - 1, cr, False), lambda cr: cr, carry)
        carry = tile(qi, carry, True)
        dq_ref[...] = (jnp.where(masks[0], carry[0][1], carry[1][1]) * ATTN_SCALE).astype(BF16)

        @pl.when(qi == nq - 1)
        def _():
            _write_transposed(dk_acc, dk_ref)
            _write_transposed(dv_acc, dv_ref)

    seq_spec = pl.BlockSpec((s, LANES), lambda hp, qi: (0, hp))
    return pl.pallas_call(
        body, name="sb_bwd", grid=(N_HEADS // 2, nq),
        in_specs=_qkv_specs(s, 3 * ATTN_W // LANES) + [ANY, _pair_spec()],
        out_specs=[_pair_spec(), seq_spec, seq_spec],
        out_shape=[jax.ShapeDtypeStruct((s, ATTN_W), BF16)] * 3,
        scratch_shapes=[pltpu.VMEM((LANES, s), F32), pltpu.VMEM((LANES, s), F32),
                        pltpu.VMEM((2, TILE_SLOTS, 2, TQ, TQ), BF16), pltpu.SemaphoreType.DMA((2, TILE_SLOTS))],
        compiler_params=_params(("arbitrary", "arbitrary")),
    )(qkv, qkv, qkv, tiles, do)


CONV_TR = 256


def _shift_down(x, halo, n):
    rolled = pltpu.roll(x, n, 0)
    head = rolled[0:8, :]
    rid = lax.broadcasted_iota(jnp.int32, head.shape, 0)
    for j in range(n):
        head = jnp.where(rid == j, halo[8 - n + j:8 - n + j + 1, :], head)
    return jnp.concatenate([head, rolled[8:, :]], axis=0)


def _shift_up(x, halo, n):
    rows = x.shape[0]
    rolled = pltpu.roll(x, rows - n, 0)
    tail = rolled[rows - 8:, :]
    rid = lax.broadcasted_iota(jnp.int32, tail.shape, 0)
    for j in range(n):
        tail = jnp.where(rid == 8 - n + j, halo[j:j + 1, :], tail)
    return jnp.concatenate([rolled[:rows - 8, :], tail], axis=0)


def _conv_fwd_block(x, halo, w, b):
    return b + _shift_down(x, halo, 2) * w[0:1, :] + _shift_down(x, halo, 1) * w[1:2, :] + x * w[2:3, :]


def _conv_specs(tr, s):
    pair = 2 * FF_HALF
    blk = pl.BlockSpec((tr, pair), lambda j, i: (i, j))
    prev = pl.BlockSpec((8, pair), lambda j, i: (jnp.maximum(i * (tr // 8) - 1, 0), j))
    nxt = pl.BlockSpec((8, pair), lambda j, i: (jnp.minimum((i + 1) * (tr // 8), s // 8 - 1), j))
    return blk, prev, nxt


def _conv_gate_fwd(hpre, conv_w, conv_b):
    s = hpre.shape[0]
    tr = min(CONV_TR, s)
    blk, prev, _ = _conv_specs(tr, s)

    def body(x_ref, halo_ref, w_ref, b_ref, a_ref):
        i = pl.program_id(1)
        halo = jnp.where(i > 0, halo_ref[...], 0.0)
        h = _conv_fwd_block(x_ref[...], halo, w_ref[...], b_ref[...])
        hg, hv = h[:, :FF_HALF], h[:, FF_HALF:]
        a_ref[...] = (hg * _sigmoid(hg) * hv).astype(BF16)

    return pl.pallas_call(
        body, name="conv_gate_fwd", grid=(2, s // tr),
        in_specs=[blk, prev, pl.BlockSpec((3, 2 * FF_HALF), lambda j, i: (0, j)),
                  pl.BlockSpec((1, 2 * FF_HALF), lambda j, i: (0, j))],
        out_specs=pl.BlockSpec((tr, FF_HALF), lambda j, i: (i, j)),
        out_shape=jax.ShapeDtypeStruct((s, D_FF), BF16),
        compiler_params=_params(("parallel", "parallel")),
    )(hpre, hpre, conv_w, conv_b)


def _conv_gate_bwd(hpre, da, conv_w, conv_b):
    s = hpre.shape[0]
    tr = min(CONV_TR, s)
    blk, prev, _ = _conv_specs(tr, s)

    def body(x_ref, halo_ref, da_ref, w_ref, b_ref, dh_ref, db_ref, dw_ref):
        i = pl.program_id(1)
        halo = jnp.where(i > 0, halo_ref[...], 0.0)
        x = x_ref[...]
        h = _conv_fwd_block(x, halo, w_ref[...], b_ref[...])
        hg, hv = h[:, :FF_HALF], h[:, FF_HALF:]
        da_blk = da_ref[...].astype(F32)
        sg = _sigmoid(hg)
        dhg = da_blk * hv * (sg * (1.0 + hg * (1.0 - sg)))
        dhv = da_blk * (hg * sg)
        dh_ref[:, :FF_HALF] = dhg.astype(BF16)
        dh_ref[:, FF_HALF:] = dhv.astype(BF16)
        x2, x1 = _shift_down(x, halo, 2), _shift_down(x, halo, 1)
        parts = []
        for lo, dpart in ((0, dhg), (FF_HALF, dhv)):
            cols = slice(lo, lo + FF_HALF)
            parts.append((cols, _colsum(dpart), _colsum(dpart * x2[:, cols]), _colsum(dpart * x1[:, cols]),
                          _colsum(dpart * x[:, cols])))

        @pl.when(i == 0)
        def _():
            for cols, db, dw0, dw1, dw2 in parts:
                db_ref[:, cols] = db
                dw_ref[0:1, cols] = dw0
                dw_ref[1:2, cols] = dw1
                dw_ref[2:3, cols] = dw2

        @pl.when(i > 0)
        def _():
            for cols, db, dw0, dw1, dw2 in parts:
                db_ref[:, cols] += db
                dw_ref[0:1, cols] += dw0
                dw_ref[1:2, cols] += dw1
                dw_ref[2:3, cols] += dw2

    pair = 2 * FF_HALF
    return pl.pallas_call(
        body, name="conv_gate_bwd", grid=(2, s // tr),
        in_specs=[blk, prev, pl.BlockSpec((tr, FF_HALF), lambda j, i: (i, j)),
                  pl.BlockSpec((3, pair), lambda j, i: (0, j)), pl.BlockSpec((1, pair), lambda j, i: (0, j))],
        out_specs=[blk, pl.BlockSpec((1, pair), lambda j, i: (0, j)), pl.BlockSpec((3, pair), lambda j, i: (0, j))],
        out_shape=[jax.ShapeDtypeStruct((s, 2 * D_FF), BF16), jax.ShapeDtypeStruct((1, 2 * D_FF), F32),
                   jax.ShapeDtypeStruct((3, 2 * D_FF), F32)],
        compiler_params=_params(("parallel", "arbitrary")),
    )(hpre, hpre, da, conv_w, conv_b)


def _conv_input_bwd(dh, conv_w):
    s = dh.shape[0]
    tr = min(CONV_TR, s)
    blk, _, _ = _conv_specs(tr, s)
    nblk = s // tr

    def body(x_ref, halo_ref, w_ref, o_ref):
        i = pl.program_id(1)
        halo = jnp.where(i < nblk - 1, halo_ref[...].astype(F32), 0.0)
        x, w = x_ref[...].astype(F32), w_ref[...]
        o_ref[...] = (x * w[2:3, :] + _shift_up(x, halo, 1) * w[1:2, :] + _shift_up(x, halo, 2) * w[0:1, :]).astype(BF16)

    nxt = pl.BlockSpec((16, 2 * FF_HALF), lambda j, i: (jnp.minimum((i + 1) * (tr // 16), s // 16 - 1), j))
    return pl.pallas_call(
        body, name="conv_input_bwd", grid=(2, nblk),
        in_specs=[blk, nxt, pl.BlockSpec((3, 2 * FF_HALF), lambda j, i: (0, j))], out_specs=blk,
        out_shape=jax.ShapeDtypeStruct((s, 2 * D_FF), BF16),
        compiler_params=_params(("parallel", "parallel")),
    )(dh, dh, conv_w)


def _adamw_math(w, g, m, v):
    m = ADAM_B1 * m + (1.0 - ADAM_B1) * g
    v = ADAM_B2 * v + (1.0 - ADAM_B2) * (g * g)
    m_hat = m / (1.0 - ADAM_B1 ** ADAM_STEP)
    v_hat = v / (1.0 - ADAM_B2 ** ADAM_STEP)
    delta = -ADAM_LR * (m_hat / (jnp.sqrt(v_hat) + ADAM_EPS) + ADAM_WD * w)
    return delta, m, v


def _adamw(name, g8, w, m, v):
    r, c = w.shape
    tr = _row_tile(r, c)

    def body(g_ref, w_ref, m_ref, v_ref, go_ref, d_ref, mo_ref, vo_ref):
        g = g_ref[0].astype(F32)
        for d in range(1, N_DEV):
            g = g + g_ref[d].astype(F32)
        delta, mn, vn = _adamw_math(w_ref[...], g, m_ref[...], v_ref[...])
        go_ref[...] = g
        d_ref[...] = delta
        mo_ref[...] = mn
        vo_ref[...] = vn

    spec = pl.BlockSpec((tr, c), lambda i: (i, 0))
    return pl.pallas_call(
        body, name=name, grid=(r // tr,),
        in_specs=[pl.BlockSpec((N_DEV, tr, c), lambda i: (0, i, 0)), spec, spec, spec], out_specs=[spec] * 4,
        out_shape=[jax.ShapeDtypeStruct((r, c), F32)] * 4, compiler_params=_params(("parallel",)),
    )(g8, w, m, v)


def _adamw_ada(c_t, dmod, w, m, v):
    r, c = w.shape
    tr = _row_tile(r, c)

    def body(ct_ref, dm_ref, w_ref, m_ref, v_ref, go_ref, d_ref, mo_ref, vo_ref):
        ct, dm = ct_ref[...], dm_ref[...]
        g = ct[:, 0:1] * dm[0:1, :]
        for b in range(1, N_DEV):
            g = g + ct[:, b:b + 1] * dm[b:b + 1, :]
        delta, mn, vn = _adamw_math(w_ref[...], g, m_ref[...], v_ref[...])
        go_ref[...] = g
        d_ref[...] = delta
        mo_ref[...] = mn
        vo_ref[...] = vn

    spec = pl.BlockSpec((tr, c), lambda i: (i, 0))
    return pl.pallas_call(
        body, name="adamw_w_ada", grid=(r // tr,),
        in_specs=[pl.BlockSpec((tr, N_DEV), lambda i: (i, 0)), pl.BlockSpec((N_DEV, c), lambda i: (0, 0)),
                  spec, spec, spec],
        out_specs=[spec] * 4, out_shape=[jax.ShapeDtypeStruct((r, c), F32)] * 4,
        compiler_params=_params(("parallel",)),
    )(c_t, dmod, w, m, v)


def _cols_from_slots(g):
    n, r, c = g.shape
    return jnp.transpose(g, (1, 0, 2)).reshape(r, n * c)


def _cols_to_slots(w):
    r, c = w.shape
    return jnp.transpose(w.reshape(r, N_DEV, c // N_DEV), (1, 0, 2))


def _pair_cols(w):
    g0, g1 = w[..., 0:FF_HALF], w[..., FF_HALF:D_FF]
    v0, v1 = w[..., D_FF:D_FF + FF_HALF], w[..., D_FF + FF_HALF:]
    return jnp.concatenate([g0, v0, g1, v1], axis=-1)


def _unpair_cols(w):
    g0, v0 = w[..., 0:FF_HALF], w[..., FF_HALF:D_FF]
    g1, v1 = w[..., D_FF:D_FF + FF_HALF], w[..., D_FF + FF_HALF:]
    return jnp.concatenate([g0, g1, v0, v1], axis=-1)


def _row(v):
    return v.reshape(1, -1)


def kernel(x, c, w_ada, b_ada, w_in, b_forget, w_fox_proj, w_sb_proj, w_o, ln1_g, ln1_b, w_up, conv_w, conv_b, w_down, ln2_g, ln2_b, loss_target, m_w_ada, m_b_ada, m_w_in, m_b_forget, m_w_fox_proj, m_w_sb_proj, m_w_o, m_ln1_g, m_ln1_b, m_w_up, m_conv_w, m_conv_b, m_w_down, m_ln2_g, m_ln2_b, v_w_ada, v_b_ada, v_w_in, v_b_forget, v_w_fox_proj, v_w_sb_proj, v_w_o, v_ln1_g, v_ln1_b, v_w_up, v_conv_w, v_conv_b, v_w_down, v_ln2_g, v_ln2_b):
    s = x.shape[1]
    me = 4 * lax.axis_index("x") + 2 * lax.axis_index("y") + lax.axis_index("c")
    x2 = x.reshape(s, D_MODEL)
    tgt = loss_target.reshape(s, D_MODEL)

    b_ada_loc = lax.dynamic_slice(b_ada, (me * ADA_SHARD,), (ADA_SHARD,)).reshape(1, ADA_SHARD)
    c_all, mod = _mod_exchange(c, w_ada, b_ada_loc)
    mod = mod.reshape(N_MOD, 1, D_MODEL)
    sh1, sc1, gt1, sh2, sc2, gt2 = [mod[i] for i in range(N_MOD)]

    (g_in,) = _exchange("ag_w_in", [w_in.astype(BF16)], scatter=False)
    late_weights = _Ride([w_fox_proj.astype(BF16), w_sb_proj.astype(BF16), w_o.astype(BF16), w_up.astype(BF16),
                          w_down.astype(BF16), conv_w], scatter=False)
    w_in_f = _cols_from_slots(g_in)
    w_proj = jnp.concatenate(
        [w_in_f[:, 0:1536], w_in_f[:, 1544:3080], w_in_f[:, 3080:5128], w_in_f[:, 1536:1544],
         jnp.zeros((D_MODEL, W_PROJ - 5128), BF16)], axis=1)
    w_qkv, w_gates, w_f = w_proj[:, :W_QKV], w_proj[:, W_QKV:W_QKV + W_GATES], w_proj[:, W_QKV + W_GATES:W_QKV + W_GATES + W_F]
    conv_b_p = _pair_cols(_row(conv_b))
    b_f_pad = jnp.pad(_row(b_forget), ((0, 0), (0, LANES - N_HEADS)))

    (u1,) = _rowwise("modulate1", lambda xb, sc, sh: (xb * (1.0 + sc) + sh,),
                     [(x2, D_MODEL, 0)], [sc1, sh1], [(D_MODEL, BF16)], tr=512)
    qkv = _mm(u1, w_qkv, name="mm_qkv", out_dtype=BF16)
    gates = _mm(u1, w_gates, name="mm_gates")
    f_raw = _mm(u1, w_f, name="mm_forget")
    cum_col = _forget_cumsum(f_raw, b_f_pad)
    cum_row = jnp.transpose(cum_col[:, :N_HEADS]).reshape(N_HEADS // 2, 2, s)
    (y_fox, y_fox32, lse), (g_fox, g_sb, g_o, g_up, g_down, g_cw) = _fox_fwd(qkv, cum_col, cum_row, ride=late_weights)
    w_fox_f = _cols_from_slots(g_fox)
    w_sb_f = _cols_from_slots(g_sb)
    w_o_f = g_o.reshape(D_MODEL, D_MODEL)
    w_up_p = _pair_cols(_cols_from_slots(g_up))
    w_down_f = g_down.reshape(D_FF, D_MODEL)
    conv_w_p = _pair_cols(_cols_from_slots(g_cw))
    y_sb, sb_run = _sb_fwd(qkv)
    pf = _mm(y_fox, w_fox_f, name="mm_fox_proj")
    ps = _mm(y_sb, w_sb_f, name="mm_sb_proj")
    (merged,) = _rowwise("gate_merge", lambda ga, gb, a, b: (_sigmoid(ga) * a + _sigmoid(gb) * b,),
                         [(gates, D_MODEL, 0), (gates, D_MODEL, 1), (pf, D_MODEL, 0), (ps, D_MODEL, 0)], [],
                         [(D_MODEL, BF16)])
    attn_out = _mm(merged, w_o_f, name="mm_w_o")

    def ln_fwd(xb, fb, gt, g, b):
        xhat, _ = _ln_stats(ALPHA * xb + (1.0 + gt) * fb)
        return xhat * g + b

    def ln1_mod(xb, fb, gt, g, b, sc, sh):
        y = ln_fwd(xb, fb, gt, g, b)
        return y, y * (1.0 + sc) + sh

    x1, u2 = _rowwise("ln1_modulate2", ln1_mod, [(x2, D_MODEL, 0), (attn_out, D_MODEL, 0)],
                      [gt1, _row(ln1_g), _row(ln1_b), sc2, sh2], [(D_MODEL, F32), (D_MODEL, BF16)])

    hpre = _mm(u2, w_up_p, name="mm_w_up", tn=1408)
    act = _conv_gate_fwd(hpre, conv_w_p, conv_b_p)
    ffn_out = _mm(act, w_down_f, name="mm_w_down", tk=2816)

    def ln2_bwd(xb, fb, tb, gt, g, b):
        xhat, rstd = _ln_stats(ALPHA * xb + (1.0 + gt) * fb)
        err = (xhat * g + b) - tb
        dy = err * (1.0 / D_MODEL)
        dr = _ln_bwd(dy, xhat, rstd, g)
        return (dr * (1.0 + gt), ALPHA * dr,
                _colsum(err * err), _colsum(dy * xhat), _colsum(dy), _colsum(dr * fb))

    dffn, dx1_res, sq_err, d_ln2_g, d_ln2_b, d_gt2 = _rowwise(
        "ln2_bwd", ln2_bwd, [(x1, D_MODEL, 0), (ffn_out, D_MODEL, 0), (tgt, D_MODEL, 0)],
        [gt2, _row(ln2_g), _row(ln2_b)], [(D_MODEL, BF16), (D_MODEL, F32)], sums=[D_MODEL] * 4)
    loss = lax.psum(0.5 * jnp.sum(sq_err) / D_MODEL, ("x", "y", "c"))

    d_w_down = _mm(act, dffn, name="mm_d_w_down", ta=True, tm=1408, tk=2048, out_dtype=BF16)
    d_act = _mm(dffn, w_down_f, name="mm_d_act", tb=True, tn=1408, out_dtype=BF16)
    dh, d_conv_b_p, d_conv_w_p = _conv_gate_bwd(hpre, d_act, conv_w_p, conv_b_p)
    dhpre = _conv_input_bwd(dh, conv_w_p)
    d_w_up_p = _mm(u2, dhpre, name="mm_d_w_up", ta=True, tn=1408, tk=2048, out_dtype=BF16)
    du2 = _mm(dhpre, w_up_p, name="mm_d_u2", tb=True, tk=2816)

    def ln1_bwd(du, dres, x1b, xb, fb, sc, gt, g):
        dx1 = dres + du * (1.0 + sc)
        xhat, rstd = _ln_stats(ALPHA * xb + (1.0 + gt) * fb)
        dr = _ln_bwd(dx1, xhat, rstd, g)
        return (dr * (1.0 + gt), ALPHA * dr,
                _colsum(du * x1b), _colsum(du), _colsum(dx1 * xhat), _colsum(dx1), _colsum(dr * fb))

    d_attn, dx_res, d_sc2, d_sh2, d_ln1_g, d_ln1_b, d_gt1 = _rowwise(
        "ln1_bwd", ln1_bwd,
        [(du2, D_MODEL, 0), (dx1_res, D_MODEL, 0), (x1, D_MODEL, 0), (x2, D_MODEL, 0), (attn_out, D_MODEL, 0)],
        [sc2, gt1, _row(ln1_g)], [(D_MODEL, BF16), (D_MODEL, F32)], sums=[D_MODEL] * 5)

    d_w_o = _mm(merged, d_attn, name="mm_d_w_o", ta=True, out_dtype=BF16)
    d_merged = _mm(d_attn, w_o_f, name="mm_d_merged", tb=True)

    def merge_bwd(dm, ga, gb, a, b):
        sa, sb = _sigmoid(ga), _sigmoid(gb)
        return dm * a * sa * (1.0 - sa), dm * b * sb * (1.0 - sb), dm * sa, dm * sb

    d_ga, d_gb, d_pf, d_ps = _rowwise(
        "gate_merge_bwd", merge_bwd,
        [(d_merged, D_MODEL, 0), (gates, D_MODEL, 0), (gates, D_MODEL, 1), (pf, D_MODEL, 0), (ps, D_MODEL, 0)], [],
        [(D_MODEL, BF16)] * 4)
    d_w_fox = _mm(y_fox, d_pf, name="mm_d_w_fox", ta=True, out_dtype=BF16)
    d_w_sb = _mm(y_sb, d_ps, name="mm_d_w_sb", ta=True, out_dtype=BF16)
    d_y_fox = _mm(d_pf, w_fox_f, name="mm_d_y_fox", tb=True, out_dtype=BF16)
    d_y_sb = _mm(d_ps, w_sb_f, name="mm_d_y_sb", tb=True, out_dtype=BF16)
    early_grads = _Ride(
        [_cols_to_slots(d_w_fox), _cols_to_slots(d_w_sb), d_w_o.reshape(N_DEV, D_MODEL // N_DEV, D_MODEL),
         _cols_to_slots(_unpair_cols(d_w_up_p)), d_w_down.reshape(N_DEV, D_FF // N_DEV, D_MODEL)], scatter=True)
    (dq_a, dk_a, dv_a, d_cum_row, d_cum_q), early_slots = _fox_bwd(qkv, cum_col, cum_row, y_fox32, lse, d_y_fox,
                                                                    ride=early_grads)
    dq_b, dk_b, dv_b = _sb_bwd(qkv, sb_run, d_y_sb)
    d_cum = jnp.transpose(d_cum_row.reshape(N_HEADS, s)) + d_cum_q[:, ::HEAD_DIM]
    d_cum = jnp.pad(d_cum, ((0, 0), (0, LANES - N_HEADS)))
    d_f, d_b_forget = _forget_bwd(d_cum, f_raw, b_f_pad)
    d_proj = jnp.concatenate([dq_a, dk_a, dv_a, dq_b, dk_b, dv_b, d_ga, d_gb, d_f,
                              jnp.zeros((s, W_PROJ - W_QKV - W_GATES - W_F), BF16)], axis=1)
    d_w_proj = _mm(u1, d_proj, name="mm_d_w_in", ta=True, tn=896, tk=2048, out_dtype=BF16)
    d_w_in_f = jnp.concatenate([d_w_proj[:, 0:1536], d_w_proj[:, 5120:5128], d_w_proj[:, 1536:3072],
                                d_w_proj[:, 3072:5120]], axis=1)
    du1, (in_slots,) = _mm(d_proj, w_proj, name="mm_d_u1", tb=True, tk=1792,
                           ride=_Ride([_cols_to_slots(d_w_in_f)], scatter=True))

    def x_bwd(du, dres, xb, sc):
        return dres + du * (1.0 + sc), _colsum(du * xb), _colsum(du)

    grad_x, d_sc1, d_sh1 = _rowwise("x_bwd", x_bwd, [(du1, D_MODEL, 0), (dx_res, D_MODEL, 0), (x2, D_MODEL, 0)],
                                    [sc1], [(D_MODEL, F32)], sums=[D_MODEL] * 2, tr=512)

    d_conv_b = _unpair_cols(d_conv_b_p)
    d_conv_w = _unpair_cols(d_conv_w_p)
    n_rep = N_MOD * D_MODEL + LANES + 4 * D_MODEL + 2 * D_FF
    small = jnp.concatenate(
        [d_sh1, d_sc1, d_gt1, d_sh2, d_sc2, d_gt2, d_b_forget, d_ln1_g, d_ln1_b, d_ln2_g, d_ln2_b, d_conv_b,
         d_conv_w.reshape(1, 6 * D_FF)], axis=1)
    n_small = small.shape[1] // LANES
    small = jnp.pad(small.reshape(n_small, LANES), ((0, 264 - n_small), (0, 0)))
    (small_all,) = _exchange("ag_small_grads", [small], scatter=False)
    rep8 = small_all[:, :n_rep // LANES, :]
    cw8 = small_all[:, n_rep // LANES:n_small, :].reshape(N_DEV, 3, 2 * D_FF)
    cw8 = lax.dynamic_slice(cw8, (0, 0, me * UP_SHARD), (N_DEV, 3, UP_SHARD))
    dmod8 = small_all[:, :N_MOD * D_MODEL // LANES, :].reshape(N_DEV, N_MOD * D_MODEL)
    dmod_loc = lax.dynamic_slice(dmod8, (0, me * ADA_SHARD), (N_DEV, ADA_SHARD))

    def pack_rep(b_a, b_f, g1, b1, g2, b2, cb):
        flat = jnp.concatenate([b_a, jnp.pad(b_f, (0, LANES - N_HEADS)), g1, b1, g2, b2, cb])
        return flat.reshape(n_rep // LANES, LANES)

    rep = _adamw("adamw_small", rep8, pack_rep(b_ada, b_forget, ln1_g, ln1_b, ln2_g, ln2_b, conv_b),
                 pack_rep(m_b_ada, m_b_forget, m_ln1_g, m_ln1_b, m_ln2_g, m_ln2_b, m_conv_b),
                 pack_rep(v_b_ada, v_b_forget, v_ln1_g, v_ln1_b, v_ln2_g, v_ln2_b, v_conv_b))

    def unpack_rep(p):
        flat = p.reshape(-1)
        o = N_MOD * D_MODEL
        return {"b_ada": flat[:o], "b_forget": flat[o:o + N_HEADS],
                "ln1_g": flat[o + 128:o + 1152], "ln1_b": flat[o + 1152:o + 2176],
                "ln2_g": flat[o + 2176:o + 3200], "ln2_b": flat[o + 3200:o + 4224], "conv_b": flat[o + 4224:]}

    rep = [unpack_rep(p) for p in rep]
    r_conv_w = _adamw("adamw_conv_w", cw8, conv_w, m_conv_w, v_conv_w)
    r_ada = _adamw_ada(jnp.transpose(c_all.reshape(N_DEV, D_MODEL)), dmod_loc, w_ada, m_w_ada, v_w_ada)

    r_in = _adamw("adamw_w_in", in_slots, w_in, m_w_in, v_w_in)
    r_fox = _adamw("adamw_w_fox", early_slots[0], w_fox_proj, m_w_fox_proj, v_w_fox_proj)
    r_sb = _adamw("adamw_w_sb", early_slots[1], w_sb_proj, m_w_sb_proj, v_w_sb_proj)
    r_o = _adamw("adamw_w_o", early_slots[2], w_o, m_w_o, v_w_o)
    r_up = _adamw("adamw_w_up", early_slots[3], w_up, m_w_up, v_w_up)
    r_down = _adamw("adamw_w_down", early_slots[4], w_down, m_w_down, v_w_down)

    def leaf(i):
        return [r_ada[i], rep[i]["b_ada"], r_in[i], rep[i]["b_forget"], r_fox[i], r_sb[i], r_o[i], rep[i]["ln1_g"],
                rep[i]["ln1_b"], r_up[i], r_conv_w[i], rep[i]["conv_b"], r_down[i], rep[i]["ln2_g"], rep[i]["ln2_b"]]

    return (loss, grad_x.reshape(1, s, D_MODEL), *leaf(0), *leaf(1), *leaf(2), *leaf(3))
```

```python
import functools

import jax
import jax.numpy as jnp
from jax import lax
from jax.experimental import pallas as pl
from jax.experimental.pallas import tpu as pltpu

F32 = jnp.float32
BF16 = jnp.bfloat16
MESH = pl.DeviceIdType.MESH
ANY = pl.BlockSpec(memory_space=pl.ANY)

N_DEV = 8
D_MODEL = 1024
HEAD_DIM = 64
N_HEADS = 8
ATTN_W = N_HEADS * HEAD_DIM
D_FF = 2816
FF_HALF = D_FF // 2
N_MOD = 6
ADA_SHARD = N_MOD * D_MODEL // N_DEV
IN_SHARD = 641
UP_SHARD = 704
ATTN_SCALE = HEAD_DIM ** -0.5
ALPHA = 2.0 ** 0.25
LN_EPS = 1e-5
LANES = 128
TQ = 512
SCAN_W = 256
VMEM_LIMIT = 56 * 1024 * 1024

ADAM_LR, ADAM_B1, ADAM_B2, ADAM_EPS, ADAM_WD, ADAM_STEP = 0.001, 0.9, 0.999, 1e-08, 0.01, 10

W_QKV, W_GATES, W_F = 3072, 2048, 128
W_PROJ = 5376


def _params(sem=None):
    return pltpu.CompilerParams(dimension_semantics=sem, vmem_limit_bytes=VMEM_LIMIT)


def _tile(n, cap):
    if n <= cap:
        return n
    best = None
    for t in range(LANES, cap + 1, LANES):
        if n % t == 0:
            best = t
    assert best is not None, (n, cap)
    return best


def _row_tile(r, width, budget=192 * 1024):
    if r * width <= budget or r % 16:
        return r
    best = 16
    for t in range(16, r + 1, 16):
        if r % t == 0 and t * width <= budget:
            best = t
    return best


def _me():
    x, y, c = lax.axis_index("x"), lax.axis_index("y"), lax.axis_index("c")
    return x, y, c, 4 * x + 2 * y + c


def _peer(r):
    x, y, c, _ = _me()
    px = 1 - x if r & 4 else x
    py = 1 - y if r & 2 else y
    pc = 1 - c if r & 1 else c
    return (px, py, pc), 4 * px + 2 * py + pc


class _Ride:
    def __init__(self, arrays, scatter):
        self.arrays, self.scatter, self.n = list(arrays), scatter, len(arrays)
        self.in_specs = [ANY] * self.n
        self.out_specs = [ANY] * self.n
        self.out_shape = [jax.ShapeDtypeStruct(a.shape if scatter else (N_DEV,) + a.shape, a.dtype) for a in arrays]
        self.scratch = [pltpu.SemaphoreType.DMA((self.n, N_DEV - 1)), pltpu.SemaphoreType.DMA((self.n, N_DEV - 1)),
                        pltpu.SemaphoreType.DMA((self.n,))]

    def _local(self, ins, outs, sems, a):
        me = _me()[3]
        return pltpu.make_async_copy(ins[a].at[me] if self.scatter else ins[a], outs[a].at[me], sems[2].at[a])

    def _remote(self, ins, outs, sems, a, r, arriving):
        me = _me()[3]
        peer, pidx = _peer(r)
        src = ins[a].at[me if arriving else pidx] if self.scatter else ins[a]
        return pltpu.make_async_remote_copy(
            src_ref=src, dst_ref=outs[a].at[pidx if arriving else me], send_sem=sems[0].at[a, r - 1],
            recv_sem=sems[1].at[a, r - 1], device_id=peer, device_id_type=MESH)

    def start(self, ins, outs, sems):
        for a in range(self.n):
            self._local(ins, outs, sems, a).start()
        for r in range(1, N_DEV):
            for a in range(self.n):
                self._remote(ins, outs, sems, a, r, False).start()

    def wait(self, ins, outs, sems):
        for r in range(1, N_DEV):
            for a in range(self.n):
                self._remote(ins, outs, sems, a, r, True).wait_recv()
        for r in range(1, N_DEV):
            for a in range(self.n):
                self._remote(ins, outs, sems, a, r, False).wait_send()
        for a in range(self.n):
            self._local(ins, outs, sems, a).wait()


def _exchange(name, arrays, scatter):
    ride = _Ride(arrays, scatter)

    def body(*refs):
        ins, outs, sems = refs[:ride.n], refs[ride.n:2 * ride.n], refs[2 * ride.n:]
        ride.start(ins, outs, sems)
        ride.wait(ins, outs, sems)

    return pl.pallas_call(body, name=name, in_specs=ride.in_specs, out_specs=ride.out_specs, out_shape=ride.out_shape,
                          scratch_shapes=ride.scratch)(*arrays)


def _with_ride(body, ride, n_in, n_out, grid):
    if ride is None:
        return body
    n = ride.n

    def wrapped(*refs):
        ins, rins = refs[:n_in], refs[n_in:n_in + n]
        outs, routs = refs[n_in + n:n_in + n + n_out], refs[n_in + n + n_out:n_in + 2 * n + n_out]
        rest = refs[n_in + 2 * n + n_out:]
        scratch, sems = rest[:len(rest) - 3], rest[len(rest) - 3:]
        ids = [pl.program_id(d) for d in range(len(grid))]
        first = functools.reduce(lambda p, q: p & q, [i == 0 for i in ids])
        last = functools.reduce(lambda p, q: p & q, [i == g - 1 for i, g in zip(ids, grid)])

        @pl.when(first)
        def _():
            ride.start(rins, routs, sems)

        body(*ins, *outs, *scratch)

        @pl.when(last)
        def _():
            ride.wait(rins, routs, sems)

    return wrapped


def _ride_call(body, ride, *, name, grid, in_specs, out_specs, out_shape, scratch_shapes, sem, args):
    n_in, n_out = len(in_specs), len(out_specs)
    if ride is None:
        res = pl.pallas_call(body, name=name, grid=grid, in_specs=in_specs, out_specs=out_specs, out_shape=out_shape,
                             scratch_shapes=scratch_shapes, compiler_params=_params(sem))(*args)
        return list(res), []
    res = pl.pallas_call(
        _with_ride(body, ride, n_in, n_out, grid), name=name, grid=grid,
        in_specs=list(in_specs) + ride.in_specs, out_specs=list(out_specs) + ride.out_specs,
        out_shape=list(out_shape) + ride.out_shape, scratch_shapes=list(scratch_shapes) + ride.scratch,
        compiler_params=_params(("arbitrary",) * len(grid)))(*args, *ride.arrays)
    return list(res[:n_out]), list(res[n_out:])


def _mm(a, b, *, name, ta=False, tb=False, out_dtype=F32, tm=1024, tn=1024, tk=1024, ride=None):
    m, k = (a.shape[1], a.shape[0]) if ta else a.shape
    n = b.shape[0] if tb else b.shape[1]
    assert (b.shape[1] if tb else b.shape[0]) == k
    tm, tn, tk = _tile(m, tm), _tile(n, tn), _tile(k, tk)
    nk = k // tk
    a_spec = pl.BlockSpec((tk, tm), lambda i, j, l: (l, i)) if ta else pl.BlockSpec((tm, tk), lambda i, j, l: (i, l))
    b_spec = pl.BlockSpec((tn, tk), lambda i, j, l: (j, l)) if tb else pl.BlockSpec((tk, tn), lambda i, j, l: (l, j))
    dims = (((0,) if ta else (1,), (1,) if tb else (0,)), ((), ()))

    def body(a_ref, b_ref, o_ref, *acc):
        p = lax.dot_general(a_ref[...].astype(BF16), b_ref[...].astype(BF16), dims, preferred_element_type=F32)
        if nk == 1:
            o_ref[...] = p.astype(out_dtype)
            return
        acc_ref = acc[0]
        step = pl.program_id(2)

        @pl.when(step == 0)
        def _():
            acc_ref[...] = p

        @pl.when(step > 0)
        def _():
            acc_ref[...] += p

        @pl.when(step == nk - 1)
        def _():
            o_ref[...] = acc_ref[...].astype(out_dtype)

    outs, rode = _ride_call(
        body, ride, name=name, grid=(m // tm, n // tn, nk), in_specs=[a_spec, b_spec],
        out_specs=[pl.BlockSpec((tm, tn), lambda i, j, l: (i, j))], out_shape=[jax.ShapeDtypeStruct((m, n), out_dtype)],
        scratch_shapes=[] if nk == 1 else [pltpu.VMEM((tm, tn), F32)], sem=("parallel", "parallel", "arbitrary"),
        args=(a, b))
    return outs[0] if ride is None else (outs[0], rode)


def _rowwise(name, fn, rows, vecs, outs, sums=(), tr=256):
    s = rows[0][0].shape[0]
    tr = min(tr, s)
    nr, nv, no = len(rows), len(vecs), len(outs)

    def body(*refs):
        vals = [r[...] for r in refs[:nr + nv]]
        res = fn(*vals)
        for o_ref, val in zip(refs[nr + nv:nr + nv + no], res[:no]):
            o_ref[...] = val.astype(o_ref.dtype)
        step = pl.program_id(0)
        for s_ref, val in zip(refs[nr + nv + no:], res[no:]):
            @pl.when(step == 0)
            def _(s_ref=s_ref, val=val):
                s_ref[...] = val

            @pl.when(step > 0)
            def _(s_ref=s_ref, val=val):
                s_ref[...] += val

    in_specs = [pl.BlockSpec((tr, w), functools.partial(lambda i, cb: (i, cb), cb=cb)) for _, w, cb in rows]
    in_specs += [pl.BlockSpec(v.shape, lambda i: (0, 0)) for v in vecs]
    out_specs = [pl.BlockSpec((tr, w), lambda i: (i, 0)) for w, _ in outs]
    out_specs += [pl.BlockSpec((1, w), lambda i: (0, 0)) for w in sums]
    out_shape = [jax.ShapeDtypeStruct((s, w), dt) for w, dt in outs]
    out_shape += [jax.ShapeDtypeStruct((1, w), F32) for w in sums]
    return pl.pallas_call(
        body, name=name, grid=(s // tr,), in_specs=in_specs, out_specs=out_specs, out_shape=out_shape,
        compiler_params=_params(("arbitrary",) if sums else ("parallel",)),
    )(*[r[0] for r in rows], *vecs)


def _colsum(x):
    return jnp.sum(x, axis=0, keepdims=True)


def _sigmoid(x):
    return 1.0 / (1.0 + jnp.exp(-x))


def _log_sigmoid(x):
    return jnp.minimum(x, 0.0) - jnp.log(1.0 + jnp.exp(-jnp.abs(x)))


def _ln_stats(r):
    mu = jnp.mean(r, axis=-1, keepdims=True)
    xc = r - mu
    var = jnp.mean(xc * xc, axis=-1, keepdims=True)
    rstd = lax.rsqrt(var + LN_EPS)
    return xc * rstd, rstd


def _ln_bwd(dy, xhat, rstd, g):
    dxh = dy * g
    m1 = jnp.mean(dxh, axis=-1, keepdims=True)
    m2 = jnp.mean(dxh * xhat, axis=-1, keepdims=True)
    return rstd * (dxh - m1 - xhat * m2)


def _mod_exchange(c_row, w_ada, b_ada_loc):
    def body(c_ref, w_ref, b_ref, call_ref, mod_ref, piece_ref, send_sems, recv_sems):
        me = _me()[3]
        call_ref[me] = c_ref[...]
        sent = []
        for r in range(1, N_DEV):
            peer, _ = _peer(r)
            cp = pltpu.make_async_remote_copy(
                src_ref=c_ref, dst_ref=call_ref.at[me], send_sem=send_sems.at[0, r - 1],
                recv_sem=recv_sems.at[0, r - 1], device_id=peer, device_id_type=MESH)
            cp.start()
            sent.append(cp)
        for r in range(1, N_DEV):
            peer, pidx = _peer(r)
            pltpu.make_async_remote_copy(
                src_ref=c_ref, dst_ref=call_ref.at[pidx], send_sem=send_sems.at[0, r - 1],
                recv_sem=recv_sems.at[0, r - 1], device_id=peer, device_id_type=MESH).wait_recv()
        c_all = jnp.concatenate([call_ref[d] for d in range(N_DEV)], axis=0)
        mod_loc = jnp.dot(c_all, w_ref[...], preferred_element_type=F32,
                          precision=lax.Precision.HIGHEST) + b_ref[...]
        for d in range(N_DEV):
            piece_ref[d] = mod_loc[d:d + 1, :]
        mod_ref[me] = piece_ref[me]
        for r in range(1, N_DEV):
            peer, pidx = _peer(r)
            cp = pltpu.make_async_remote_copy(
                src_ref=piece_ref.at[pidx], dst_ref=mod_ref.at[me], send_sem=send_sems.at[1, r - 1],
                recv_sem=recv_sems.at[1, r - 1], device_id=peer, device_id_type=MESH)
            cp.start()
            sent.append(cp)
        for r in range(1, N_DEV):
            peer, pidx = _peer(r)
            pltpu.make_async_remote_copy(
                src_ref=piece_ref.at[me], dst_ref=mod_ref.at[pidx], send_sem=send_sems.at[1, r - 1],
                recv_sem=recv_sems.at[1, r - 1], device_id=peer, device_id_type=MESH).wait_recv()
        for cp in sent:
            cp.wait_send()

    vmem = pl.BlockSpec(memory_space=pltpu.VMEM)
    return pl.pallas_call(
        body, name="mod_exchange", in_specs=[vmem, vmem, vmem], out_specs=[vmem, vmem],
        out_shape=[jax.ShapeDtypeStruct((N_DEV, 1, D_MODEL), F32), jax.ShapeDtypeStruct((N_DEV, 1, ADA_SHARD), F32)],
        scratch_shapes=[pltpu.VMEM((N_DEV, 1, ADA_SHARD), F32),
                        pltpu.SemaphoreType.DMA((2, N_DEV - 1)), pltpu.SemaphoreType.DMA((2, N_DEV - 1))],
        compiler_params=_params(),
    )(c_row, w_ada, b_ada_loc)


def _split3(x):
    hi = x.astype(BF16)
    r1 = x - hi.astype(F32)
    mid = r1.astype(BF16)
    lo = (r1 - mid.astype(F32)).astype(BF16)
    return hi, mid, lo


def _scan_rows(x_ref, o_ref, s, reverse, pre=None, post=None):
    tb = min(TQ, s)
    nb = s // tb
    row = lax.broadcasted_iota(jnp.int32, (tb, tb), 0)
    col = lax.broadcasted_iota(jnp.int32, (tb, tb), 1)
    tri = jnp.where((col >= row) if reverse else (col <= row), 1.0, 0.0).astype(BF16)

    def step(i, carry):
        blk = (nb - 1 - i) if reverse else i
        off = pl.multiple_of(blk * tb, tb)
        x = x_ref[pl.ds(off, tb), :]
        if pre is not None:
            x = pre(x, off)
        acc = carry
        for piece in _split3(x):
            acc = acc + jnp.dot(tri, piece, preferred_element_type=F32)
        o_ref[pl.ds(off, tb), :] = acc if post is None else post(acc, off)
        edge = acc[0:1, :] if reverse else acc[tb - 1:tb, :]
        return jnp.broadcast_to(edge, (tb, LANES))

    lax.fori_loop(0, nb, step, jnp.zeros((tb, LANES), F32))


def _forget_cumsum(f_raw, b_pad):
    s = f_raw.shape[0]

    def body(f_ref, b_ref, cum_ref):
        b = b_ref[...]
        _scan_rows(f_ref, cum_ref, s, False, pre=lambda x, off: _log_sigmoid(x + b))

    vmem = pl.BlockSpec(memory_space=pltpu.VMEM)
    return pl.pallas_call(body, name="forget_cumsum", in_specs=[vmem, vmem], out_specs=vmem,
                          out_shape=jax.ShapeDtypeStruct((s, LANES), F32), compiler_params=_params())(f_raw, b_pad)


def _forget_bwd(dcum, f_raw, b_pad):
    s = f_raw.shape[0]

    def body(d_ref, f_ref, b_ref, df_ref, db_ref, tmp_ref):
        b = b_ref[...]
        _scan_rows(d_ref, tmp_ref, s, True)
        df = tmp_ref[...] * _sigmoid(-(f_ref[...] + b))
        df_ref[...] = df.astype(BF16)
        db_ref[...] = _colsum(df)

    vmem = pl.BlockSpec(memory_space=pltpu.VMEM)
    return pl.pallas_call(
        body, name="forget_bwd", in_specs=[vmem, vmem, vmem], out_specs=[vmem, vmem],
        out_shape=[jax.ShapeDtypeStruct((s, LANES), BF16), jax.ShapeDtypeStruct((1, LANES), F32)],
        scratch_shapes=[pltpu.VMEM((s, LANES), F32)], compiler_params=_params())(dcum, f_raw, b_pad)


def _dot_nt(a, b):
    return lax.dot_general(a, b, (((1,), (1,)), ((), ())), preferred_element_type=F32)


def _head_masks():
    lane = lax.broadcasted_iota(jnp.int32, (TQ, LANES), 1)
    return lane, [lane < HEAD_DIM, lane >= HEAD_DIM]


def _pick(mask, x):
    return jnp.where(mask, x, jnp.zeros_like(x))


def _qkv_specs(s, col0):
    nb = ATTN_W // LANES
    return [pl.BlockSpec((TQ, LANES), lambda hp, qi: (qi, col0 + hp)),
            pl.BlockSpec((s, LANES), lambda hp, qi: (0, col0 + nb + hp)),
            pl.BlockSpec((s, LANES), lambda hp, qi: (0, col0 + 2 * nb + hp))]


def _pair_spec():
    return pl.BlockSpec((TQ, LANES), lambda hp, qi: (qi, hp))


def _fox_fwd(qkv, cum_col, cum_row, ride=None):
    s = qkv.shape[0]
    nq = s // TQ

    def body(q_ref, k_ref, v_ref, cc_ref, cr_ref, o_ref, o32_ref, lse_ref):
        hp, qi = pl.program_id(0), pl.program_id(1)
        lane, masks = _head_masks()
        row = lax.broadcasted_iota(jnp.int32, (TQ, TQ), 0)
        col = lax.broadcasted_iota(jnp.int32, (TQ, TQ), 1)
        causal = col <= row
        q2 = q_ref[...] * jnp.asarray(ATTN_SCALE, BF16)
        cc = cc_ref[...]
        qms = [_pick(masks[e], q2) for e in range(2)]
        cqs = [jnp.sum(jnp.where(lane == 2 * hp + e, cc, 0.0), axis=1, keepdims=True) for e in range(2)]

        def tile(kb, carry, masked, width=1):
            off, span = pl.multiple_of(kb * TQ, TQ), width * TQ
            k2, v2 = k_ref[pl.ds(off, span), :], v_ref[pl.ds(off, span), :]
            head0 = lax.broadcasted_iota(jnp.int32, (span, LANES), 1) < HEAD_DIM
            new = []
            for e in range(2):
                m, acc = carry[e]
                sc = _dot_nt(qms[e], k2) + (cqs[e] - cr_ref[e:e + 1, pl.ds(off, span)])
                if masked:
                    sc = jnp.where(causal, sc, -jnp.inf)
                m_new = jnp.maximum(m, jnp.max(sc, axis=1, keepdims=True))
                p = jnp.exp(sc - m_new)
                v_ones = jnp.where(head0 if e == 0 else ~head0, v2, jnp.ones_like(v2))
                acc = jnp.exp(m - m_new) * acc + jnp.dot(p.astype(BF16), v_ones, preferred_element_type=F32)
                new.append((m_new, acc))
            return tuple(new)

        init = (jnp.full((TQ, 1), -jnp.inf, F32), jnp.zeros((TQ, LANES), F32))
        carry = lax.fori_loop(0, qi // 2, lambda j, cr: tile(2 * j, cr, False, 2), tile(qi, (init, init), True))
        carry = lax.cond(qi % 2 == 1, lambda cr: tile(qi - 1, cr, False), lambda cr: cr, carry)
        sums = [jnp.max(jnp.where(masks[1 - e], carry[e][1], 0.0), axis=1, keepdims=True) for e in range(2)]
        outs = [carry[e][1] / sums[e] for e in range(2)]
        lses = [carry[e][0] + jnp.log(sums[e]) for e in range(2)]
        out = jnp.where(masks[0], outs[0], outs[1])
        o_ref[...] = out.astype(BF16)
        o32_ref[...] = out
        lse_ref[...] = jnp.where(masks[0], lses[0], lses[1])

    return _ride_call(
        body, ride, name="fox_fwd", grid=(N_HEADS // 2, nq),
        in_specs=_qkv_specs(s, 0) + [pl.BlockSpec((TQ, LANES), lambda hp, qi: (qi, 0)),
                                     pl.BlockSpec((None, 2, s), lambda hp, qi: (hp, 0, 0))],
        out_specs=[_pair_spec(), _pair_spec(), _pair_spec()],
        out_shape=[jax.ShapeDtypeStruct((s, ATTN_W), BF16), jax.ShapeDtypeStruct((s, ATTN_W), F32),
                   jax.ShapeDtypeStruct((s, ATTN_W), F32)],
        scratch_shapes=[], sem=("parallel", "parallel"), args=(qkv, qkv, qkv, cum_col, cum_row))


def _write_transposed(acc_ref, out_ref):
    for c in range(out_ref.shape[0] // TQ):
        out_ref[c * TQ:(c + 1) * TQ, :] = jnp.transpose(acc_ref[:, c * TQ:(c + 1) * TQ]).astype(BF16)


def _fox_bwd(qkv, cum_col, cum_row, o, lse, do, ride=None):
    s = qkv.shape[0]
    nq = s // TQ

    def body(q_ref, k_ref, v_ref, cc_ref, cr_ref, o_ref, lse_ref, do_ref,
             dq_ref, dk_ref, dv_ref, dcr_ref, dk_acc, dv_acc):
        hp, qi = pl.program_id(0), pl.program_id(1)

        @pl.when(qi == 0)
        def _():
            dk_acc[...] = jnp.zeros_like(dk_acc)
            dv_acc[...] = jnp.zeros_like(dv_acc)
            dcr_ref[...] = jnp.zeros_like(dcr_ref)

        lane, masks = _head_masks()
        row = lax.broadcasted_iota(jnp.int32, (TQ, TQ), 0)
        col = lax.broadcasted_iota(jnp.int32, (TQ, TQ), 1)
        causal = col <= row
        q2 = q_ref[...] * jnp.asarray(ATTN_SCALE, BF16)
        do2 = do_ref[...]
        prod = do2.astype(F32) * o_ref[...].astype(F32)
        lse2 = lse_ref[...]
        cc = cc_ref[...]
        qms = [_pick(masks[e], q2) for e in range(2)]
        doms = [_pick(masks[e], do2) for e in range(2)]
        deltas = [jnp.sum(jnp.where(masks[e], prod, 0.0), axis=1, keepdims=True) for e in range(2)]
        lses = [jnp.max(jnp.where(masks[e], lse2, -jnp.inf), axis=1, keepdims=True) for e in range(2)]
        cqs = [jnp.sum(jnp.where(lane == 2 * hp + e, cc, 0.0), axis=1, keepdims=True) for e in range(2)]
        qts = [jnp.transpose(qms[e].astype(F32)).astype(BF16) for e in range(2)]
        dots = [jnp.transpose(doms[e].astype(F32)).astype(BF16) for e in range(2)]
        ones_row = [HEAD_DIM * (1 - e) for e in range(2)]
        trow = lax.broadcasted_iota(jnp.int32, (LANES, TQ), 0)
        qts = [jnp.where(trow == ones_row[e], jnp.ones_like(qts[e]), qts[e]) for e in range(2)]

        def tile(kb, carry, masked, width=1):
            off, span = pl.multiple_of(kb * TQ, TQ), width * TQ
            k2, v2 = k_ref[pl.ds(off, span), :], v_ref[pl.ds(off, span), :]
            head0 = lax.broadcasted_iota(jnp.int32, (span, LANES), 1) < HEAD_DIM
            new, dks, dv = [], [], None
            for e in range(2):
                dq = carry[e]
                sc = _dot_nt(qms[e], k2) + (cqs[e] - cr_ref[e:e + 1, pl.ds(off, span)])
                p = jnp.exp(sc - lses[e])
                if masked:
                    p = jnp.where(causal, p, 0.0)
                ds = p * (_dot_nt(doms[e], v2) - deltas[e])
                dsb = ds.astype(BF16)
                dk_e = jnp.dot(qts[e], dsb, preferred_element_type=F32)
                dv_e = jnp.dot(dots[e], p.astype(BF16), preferred_element_type=F32)
                dks.append(dk_e)
                dv = dv_e if e == 0 else dv + dv_e
                dcr_ref[e:e + 1, pl.ds(off, span)] -= dk_e[ones_row[e]:ones_row[e] + 1, :]
                k_ones = jnp.where(head0 if e == 0 else ~head0, k2, jnp.ones_like(k2))
                new.append(dq + jnp.dot(dsb, k_ones, preferred_element_type=F32))
            krow = lax.broadcasted_iota(jnp.int32, (LANES, span), 0)
            dk_acc[:, pl.ds(off, span)] += jnp.where(krow < HEAD_DIM, dks[0], dks[1])
            dv_acc[:, pl.ds(off, span)] += dv
            return tuple(new)

        init = jnp.zeros((TQ, LANES), F32)
        carry = lax.fori_loop(0, qi // 2, lambda j, cr: tile(2 * j, cr, False, 2), (init, init))
        carry = lax.cond(qi % 2 == 1, lambda cr: tile(qi - 1, cr, False), lambda cr: cr, carry)
        carry = tile(qi, carry, True)
        dq_ref[...] = (jnp.where(masks[0], carry[0], carry[1]) * ATTN_SCALE).astype(BF16)
        for e in range(2):
            dcr_ref[e:e + 1, pl.ds(pl.multiple_of(qi * TQ, TQ), TQ)] += jnp.transpose(carry[e])[
                ones_row[e]:ones_row[e] + 1, :]

        @pl.when(qi == nq - 1)
        def _():
            _write_transposed(dk_acc, dk_ref)
            _write_transposed(dv_acc, dv_ref)

    seq_spec = pl.BlockSpec((s, LANES), lambda hp, qi: (0, hp))
    return _ride_call(
        body, ride, name="fox_bwd", grid=(N_HEADS // 2, nq),
        in_specs=_qkv_specs(s, 0) + [pl.BlockSpec((TQ, LANES), lambda hp, qi: (qi, 0)),
                                     pl.BlockSpec((None, 2, s), lambda hp, qi: (hp, 0, 0)),
                                     _pair_spec(), _pair_spec(), _pair_spec()],
        out_specs=[_pair_spec(), seq_spec, seq_spec, pl.BlockSpec((None, 2, s), lambda hp, qi: (hp, 0, 0))],
        out_shape=[jax.ShapeDtypeStruct((s, ATTN_W), BF16)] * 3 + [jax.ShapeDtypeStruct((N_HEADS // 2, 2, s), F32)],
        scratch_shapes=[pltpu.VMEM((LANES, s), F32), pltpu.VMEM((LANES, s), F32)],
        sem=("parallel", "arbitrary"), args=(qkv, qkv, qkv, cum_col, cum_row, o, lse, do))


def _scan_matrix(reverse):
    row = lax.broadcasted_iota(jnp.int32, (SCAN_W, SCAN_W), 0)
    col = lax.broadcasted_iota(jnp.int32, (SCAN_W, SCAN_W), 1)
    return jnp.where((row > col) if reverse else (row < col), 1.0, 0.0).astype(BF16)


def _scan_cols(x, tri, reverse, init):
    nblk = x.shape[1] // SCAN_W
    parts, total = [None] * nblk, init
    far = 0 if reverse else SCAN_W - 1
    for b in (reversed(range(nblk)) if reverse else range(nblk)):
        blk = x[:, b * SCAN_W:(b + 1) * SCAN_W]
        part = jnp.dot(blk.astype(BF16), tri, preferred_element_type=F32)
        parts[b] = part + total
        total = total + (part[:, far:far + 1] + blk[:, far:far + 1])
    return (parts[0] if nblk == 1 else jnp.concatenate(parts, axis=1)), total


def _sb_logits(qm, k2):
    z = _dot_nt(qm, k2)
    neg_abs = lax.bitcast_convert_type(lax.bitcast_convert_type(z, jnp.uint32) | jnp.uint32(0x80000000), F32)
    soft = jnp.log(1.0 + jnp.exp(neg_abs))
    lb = jnp.minimum(z, 0.0) - soft
    return lb, lb - z


TILE_SLOTS = 4


def _tri_base(qi):
    return (qi * (qi + 1)) // 2


def _sb_fwd(qkv):
    s = qkv.shape[0]
    nq = s // TQ

    def body(q_ref, k_ref, v_ref, o_ref, t_ref, buf, sems):
        hp, qi = pl.program_id(0), pl.program_id(1)
        _, masks = _head_masks()
        row = lax.broadcasted_iota(jnp.int32, (TQ, TQ), 0)
        col = lax.broadcasted_iota(jnp.int32, (TQ, TQ), 1)
        strict = col < row
        suffix = _scan_matrix(True)
        q2 = q_ref[...] * jnp.asarray(ATTN_SCALE, BF16)
        qms = [_pick(masks[e], q2) for e in range(2)]
        base = _tri_base(qi)

        def store(e, kb):
            slot = kb % TILE_SLOTS
            return pltpu.make_async_copy(buf.at[e, slot], t_ref.at[2 * hp + e, base + kb], sems.at[e, slot])

        def tile(kb, carry, masked, width=1):
            off, span = pl.multiple_of(kb * TQ, TQ), width * TQ
            k2, v2 = k_ref[pl.ds(off, span), :], v_ref[pl.ds(off, span), :]
            new = []
            for e in range(2):
                run, acc = carry[e]
                lb, lo = _sb_logits(qms[e], k2)
                if masked:
                    lo = jnp.where(strict, lo, 0.0)
                rest, run = _scan_cols(lo, suffix, True, run)
                a = jnp.exp(lb + rest)
                if masked:
                    a = jnp.where(strict, a, 0.0)
                ab, lbb = a.astype(BF16), lb.astype(BF16)
                acc = acc + jnp.dot(ab, v2, preferred_element_type=F32)
                for w in range(width):
                    blk = kb + w

                    @pl.when(blk + TILE_SLOTS <= qi)
                    def _(e=e, blk=blk):
                        store(e, blk + TILE_SLOTS).wait()
                    buf[e, blk % TILE_SLOTS, 0] = ab[:, w * TQ:(w + 1) * TQ]
                    buf[e, blk % TILE_SLOTS, 1] = lbb[:, w * TQ:(w + 1) * TQ]
                    store(e, blk).start()
                new.append((run, acc))
            return tuple(new)

        init = (jnp.zeros((TQ, 1), F32), jnp.zeros((TQ, LANES), F32))
        carry = tile(qi, (init, init), True)
        carry = lax.cond(qi % 2 == 1, lambda cr: tile(qi - 1, cr, False), lambda cr: cr, carry)
        pairs = qi // 2
        carry = lax.fori_loop(0, pairs, lambda it, cr: tile(2 * (pairs - 1 - it), cr, False, 2), carry)
        for e in range(2):
            for blk in range(TILE_SLOTS):
                @pl.when(qi >= blk)
                def _(e=e, blk=blk):
                    store(e, blk).wait()
        o_ref[...] = jnp.where(masks[0], carry[0][1], carry[1][1]).astype(BF16)

    ntri = nq * (nq + 1) // 2
    return pl.pallas_call(
        body, name="sb_fwd", grid=(N_HEADS // 2, nq), in_specs=_qkv_specs(s, 3 * ATTN_W // LANES),
        out_specs=[_pair_spec(), ANY],
        out_shape=[jax.ShapeDtypeStruct((s, ATTN_W), BF16), jax.ShapeDtypeStruct((N_HEADS, ntri, 2, TQ, TQ), BF16)],
        scratch_shapes=[pltpu.VMEM((2, TILE_SLOTS, 2, TQ, TQ), BF16), pltpu.SemaphoreType.DMA((2, TILE_SLOTS))],
        compiler_params=_params(("arbitrary", "arbitrary")),
    )(qkv, qkv, qkv)


def _sb_bwd(qkv, tiles, do):
    s = qkv.shape[0]
    nq = s // TQ

    def body(q_ref, k_ref, v_ref, t_ref, do_ref, dq_ref, dk_ref, dv_ref, dk_acc, dv_acc, buf, sems):
        hp, qi = pl.program_id(0), pl.program_id(1)

        @pl.when(qi == 0)
        def _():
            dk_acc[...] = jnp.zeros_like(dk_acc)
            dv_acc[...] = jnp.zeros_like(dv_acc)

        _, masks = _head_masks()
        row = lax.broadcasted_iota(jnp.int32, (TQ, TQ), 0)
        col = lax.broadcasted_iota(jnp.int32, (TQ, TQ), 1)
        strict = col < row
        prefix = _scan_matrix(False)
        q2 = q_ref[...] * jnp.asarray(ATTN_SCALE, BF16)
        do2 = do_ref[...]
        qms = [_pick(masks[e], q2) for e in range(2)]
        doms = [_pick(masks[e], do2) for e in range(2)]
        qts = [jnp.transpose(qms[e].astype(F32)).astype(BF16) for e in range(2)]
        dots = [jnp.transpose(doms[e].astype(F32)).astype(BF16) for e in range(2)]
        base = _tri_base(qi)

        def fetch(e, kb):
            slot = kb % TILE_SLOTS
            return pltpu.make_async_copy(t_ref.at[2 * hp + e, base + kb], buf.at[e, slot], sems.at[e, slot])

        for e in range(2):
            fetch(e, 0).start()

            @pl.when(qi >= 1)
            def _(e=e):
                fetch(e, 1).start()

        def tile(kb, carry, masked, width=1):
            off, span = pl.multiple_of(kb * TQ, TQ), width * TQ
            k2, v2 = k_ref[pl.ds(off, span), :], v_ref[pl.ds(off, span), :]
            new, dk, dv = [], None, None
            for e in range(2):
                gsum, dq = carry[e]
                if not masked:
                    for blk in range(2, 2 + width):
                        @pl.when(kb + blk <= qi)
                        def _(e=e, blk=blk):
                            fetch(e, kb + blk).start()
                for w in range(width):
                    fetch(e, kb + w).wait()
                slots = [(kb + w) % TILE_SLOTS for w in range(width)]
                ab = buf[e, slots[0], 0] if width == 1 else jnp.concatenate([buf[e, sl, 0] for sl in slots], axis=1)
                lbb = buf[e, slots[0], 1] if width == 1 else jnp.concatenate([buf[e, sl, 1] for sl in slots], axis=1)
                beta = jnp.exp(lbb.astype(F32))
                g = ab.astype(F32) * _dot_nt(doms[e], v2)
                before, gsum = _scan_cols(g, prefix, False, gsum)
                dz = g - beta * (g + before)
                if masked:
                    dz = jnp.where(strict, dz, 0.0)
                dzb = dz.astype(BF16)
                dk_e = jnp.dot(qts[e], dzb, preferred_element_type=F32)
                dv_e = jnp.dot(dots[e], ab, preferred_element_type=F32)
                dk, dv = (dk_e, dv_e) if e == 0 else (dk + dk_e, dv + dv_e)
                new.append((gsum, dq + jnp.dot(dzb, k2, preferred_element_type=F32)))
            dk_acc[:, pl.ds(off, span)] += dk
            dv_acc[:, pl.ds(off, span)] += dv
            return tuple(new)

        init = (jnp.zeros((TQ, 1), F32), jnp.zeros((TQ, LANES), F32))
        carry = lax.fori_loop(0, qi // 2, lambda j, cr: tile(2 * j, cr, False, 2), (init, init))
        carry = lax.cond(qi % 2 == 1, lambda cr: tile(qi - 1, cr, False), lambda cr: cr, carry)
        carry = tile(qi, carry, True)
        dq_ref[...] = (jnp.where(masks[0], carry[0][1], carry[1][1]) * ATTN_SCALE).astype(BF16)

        @pl.when(qi == nq - 1)
        def _():
            _write_transposed(dk_acc, dk_ref)
            _write_transposed(dv_acc, dv_ref)

    seq_spec = pl.BlockSpec((s, LANES), lambda hp, qi: (0, hp))
    return pl.pallas_call(
        body, name="sb_bwd", grid=(N_HEADS // 2, nq),
        in_specs=_qkv_specs(s, 3 * ATTN_W // LANES) + [ANY, _pair_spec()],
        out_specs=[_pair_spec(), seq_spec, seq_spec],
        out_shape=[jax.ShapeDtypeStruct((s, ATTN_W), BF16)] * 3,
        scratch_shapes=[pltpu.VMEM((LANES, s), F32), pltpu.VMEM((LANES, s), F32),
                        pltpu.VMEM((2, TILE_SLOTS, 2, TQ, TQ), BF16), pltpu.SemaphoreType.DMA((2, TILE_SLOTS))],
        compiler_params=_params(("arbitrary", "arbitrary")),
    )(qkv, qkv, qkv, tiles, do)


CONV_TR = 256


def _shift_down(x, halo, n):
    rolled = pltpu.roll(x, n, 0)
    head = rolled[0:8, :]
    rid = lax.broadcasted_iota(jnp.int32, head.shape, 0)
    for j in range(n):
        head = jnp.where(rid == j, halo[8 - n + j:8 - n + j + 1, :], head)
    return jnp.concatenate([head, rolled[8:, :]], axis=0)


def _shift_up(x, halo, n):
    rows = x.shape[0]
    rolled = pltpu.roll(x, rows - n, 0)
    tail = rolled[rows - 8:, :]
    rid = lax.broadcasted_iota(jnp.int32, tail.shape, 0)
    for j in range(n):
        tail = jnp.where(rid == 8 - n + j, halo[j:j + 1, :], tail)
    return jnp.concatenate([rolled[:rows - 8, :], tail], axis=0)


def _conv_fwd_block(x, halo, w, b):
    return b + _shift_down(x, halo, 2) * w[0:1, :] + _shift_down(x, halo, 1) * w[1:2, :] + x * w[2:3, :]


def _conv_specs(tr, s):
    pair = 2 * FF_HALF
    blk = pl.BlockSpec((tr, pair), lambda j, i: (i, j))
    prev = pl.BlockSpec((8, pair), lambda j, i: (jnp.maximum(i * (tr // 8) - 1, 0), j))
    nxt = pl.BlockSpec((8, pair), lambda j, i: (jnp.minimum((i + 1) * (tr // 8), s // 8 - 1), j))
    return blk, prev, nxt


def _conv_gate_fwd(hpre, conv_w, conv_b):
    s = hpre.shape[0]
    tr = min(CONV_TR, s)
    blk, prev, _ = _conv_specs(tr, s)

    def body(x_ref, halo_ref, w_ref, b_ref, a_ref):
        i = pl.program_id(1)
        halo = jnp.where(i > 0, halo_ref[...], 0.0)
        h = _conv_fwd_block(x_ref[...], halo, w_ref[...], b_ref[...])
        hg, hv = h[:, :FF_HALF], h[:, FF_HALF:]
        a_ref[...] = (hg * _sigmoid(hg) * hv).astype(BF16)

    return pl.pallas_call(
        body, name="conv_gate_fwd", grid=(2, s // tr),
        in_specs=[blk, prev, pl.BlockSpec((3, 2 * FF_HALF), lambda j, i: (0, j)),
                  pl.BlockSpec((1, 2 * FF_HALF), lambda j, i: (0, j))],
        out_specs=pl.BlockSpec((tr, FF_HALF), lambda j, i: (i, j)),
        out_shape=jax.ShapeDtypeStruct((s, D_FF), BF16),
        compiler_params=_params(("parallel", "parallel")),
    )(hpre, hpre, conv_w, conv_b)


def _conv_gate_bwd(hpre, da, conv_w, conv_b):
    s = hpre.shape[0]
    tr = min(CONV_TR, s)
    blk, prev, _ = _conv_specs(tr, s)

    def body(x_ref, halo_ref, da_ref, w_ref, b_ref, dh_ref, db_ref, dw_ref):
        i = pl.program_id(1)
        halo = jnp.where(i > 0, halo_ref[...], 0.0)
        x = x_ref[...]
        h = _conv_fwd_block(x, halo, w_ref[...], b_ref[...])
        hg, hv = h[:, :FF_HALF], h[:, FF_HALF:]
        da_blk = da_ref[...].astype(F32)
        sg = _sigmoid(hg)
        dhg = da_blk * hv * (sg * (1.0 + hg * (1.0 - sg)))
        dhv = da_blk * (hg * sg)
        dh_ref[:, :FF_HALF] = dhg.astype(BF16)
        dh_ref[:, FF_HALF:] = dhv.astype(BF16)
        x2, x1 = _shift_down(x, halo, 2), _shift_down(x, halo, 1)
        parts = []
        for lo, dpart in ((0, dhg), (FF_HALF, dhv)):
            cols = slice(lo, lo + FF_HALF)
            parts.append((cols, _colsum(dpart), _colsum(dpart * x2[:, cols]), _colsum(dpart * x1[:, cols]),
                          _colsum(dpart * x[:, cols])))

        @pl.when(i == 0)
        def _():
            for cols, db, dw0, dw1, dw2 in parts:
                db_ref[:, cols] = db
                dw_ref[0:1, cols] = dw0
                dw_ref[1:2, cols] = dw1
                dw_ref[2:3, cols] = dw2

        @pl.when(i > 0)
        def _():
            for cols, db, dw0, dw1, dw2 in parts:
                db_ref[:, cols] += db
                dw_ref[0:1, cols] += dw0
                dw_ref[1:2, cols] += dw1
                dw_ref[2:3, cols] += dw2

    pair = 2 * FF_HALF
    return pl.pallas_call(
        body, name="conv_gate_bwd", grid=(2, s // tr),
        in_specs=[blk, prev, pl.BlockSpec((tr, FF_HALF), lambda j, i: (i, j)),
                  pl.BlockSpec((3, pair), lambda j, i: (0, j)), pl.BlockSpec((1, pair), lambda j, i: (0, j))],
        out_specs=[blk, pl.BlockSpec((1, pair), lambda j, i: (0, j)), pl.BlockSpec((3, pair), lambda j, i: (0, j))],
        out_shape=[jax.ShapeDtypeStruct((s, 2 * D_FF), BF16), jax.ShapeDtypeStruct((1, 2 * D_FF), F32),
                   jax.ShapeDtypeStruct((3, 2 * D_FF), F32)],
        compiler_params=_params(("parallel", "arbitrary")),
    )(hpre, hpre, da, conv_w, conv_b)


def _conv_input_bwd(dh, conv_w):
    s = dh.shape[0]
    tr = min(CONV_TR, s)
    blk, _, _ = _conv_specs(tr, s)
    nblk = s // tr

    def body(x_ref, halo_ref, w_ref, o_ref):
        i = pl.program_id(1)
        halo = jnp.where(i < nblk - 1, halo_ref[...].astype(F32), 0.0)
        x, w = x_ref[...].astype(F32), w_ref[...]
        o_ref[...] = (x * w[2:3, :] + _shift_up(x, halo, 1) * w[1:2, :] + _shift_up(x, halo, 2) * w[0:1, :]).astype(BF16)

    nxt = pl.BlockSpec((16, 2 * FF_HALF), lambda j, i: (jnp.minimum((i + 1) * (tr // 16), s // 16 - 1), j))
    return pl.pallas_call(
        body, name="conv_input_bwd", grid=(2, nblk),
        in_specs=[blk, nxt, pl.BlockSpec((3, 2 * FF_HALF), lambda j, i: (0, j))], out_specs=blk,
        out_shape=jax.ShapeDtypeStruct((s, 2 * D_FF), BF16),
        compiler_params=_params(("parallel", "parallel")),
    )(dh, dh, conv_w)


def _adamw_math(w, g, m, v):
    m = ADAM_B1 * m + (1.0 - ADAM_B1) * g
    v = ADAM_B2 * v + (1.0 - ADAM_B2) * (g * g)
    m_hat = m / (1.0 - ADAM_B1 ** ADAM_STEP)
    v_hat = v / (1.0 - ADAM_B2 ** ADAM_STEP)
    delta = -ADAM_LR * (m_hat / (jnp.sqrt(v_hat) + ADAM_EPS) + ADAM_WD * w)
    return delta, m, v


def _adamw(name, g8, w, m, v):
    r, c = w.shape
    tr = _row_tile(r, c)

    def body(g_ref, w_ref, m_ref, v_ref, go_ref, d_ref, mo_ref, vo_ref):
        g = g_ref[0].astype(F32)
        for d in range(1, N_DEV):
            g = g + g_ref[d].astype(F32)
        delta, mn, vn = _adamw_math(w_ref[...], g, m_ref[...], v_ref[...])
        go_ref[...] = g
        d_ref[...] = delta
        mo_ref[...] = mn
        vo_ref[...] = vn

    spec = pl.BlockSpec((tr, c), lambda i: (i, 0))
    return pl.pallas_call(
        body, name=name, grid=(r // tr,),
        in_specs=[pl.BlockSpec((N_DEV, tr, c), lambda i: (0, i, 0)), spec, spec, spec], out_specs=[spec] * 4,
        out_shape=[jax.ShapeDtypeStruct((r, c), F32)] * 4, compiler_params=_params(("parallel",)),
    )(g8, w, m, v)


def _adamw_ada(c_t, dmod, w, m, v):
    r, c = w.shape
    tr = _row_tile(r, c)

    def body(ct_ref, dm_ref, w_ref, m_ref, v_ref, go_ref, d_ref, mo_ref, vo_ref):
        ct, dm = ct_ref[...], dm_ref[...]
        g = ct[:, 0:1] * dm[0:1, :]
        for b in range(1, N_DEV):
            g = g + ct[:, b:b + 1] * dm[b:b + 1, :]
        delta, mn, vn = _adamw_math(w_ref[...], g, m_ref[...], v_ref[...])
        go_ref[...] = g
        d_ref[...] = delta
        mo_ref[...] = mn
        vo_ref[...] = vn

    spec = pl.BlockSpec((tr, c), lambda i: (i, 0))
    return pl.pallas_call(
        body, name="adamw_w_ada", grid=(r // tr,),
        in_specs=[pl.BlockSpec((tr, N_DEV), lambda i: (i, 0)), pl.BlockSpec((N_DEV, c), lambda i: (0, 0)),
                  spec, spec, spec],
        out_specs=[spec] * 4, out_shape=[jax.ShapeDtypeStruct((r, c), F32)] * 4,
        compiler_params=_params(("parallel",)),
    )(c_t, dmod, w, m, v)


def _cols_from_slots(g):
    n, r, c = g.shape
    return jnp.transpose(g, (1, 0, 2)).reshape(r, n * c)


def _cols_to_slots(w):
    r, c = w.shape
    return jnp.transpose(w.reshape(r, N_DEV, c // N_DEV), (1, 0, 2))


def _pair_cols(w):
    g0, g1 = w[..., 0:FF_HALF], w[..., FF_HALF:D_FF]
    v0, v1 = w[..., D_FF:D_FF + FF_HALF], w[..., D_FF + FF_HALF:]
    return jnp.concatenate([g0, v0, g1, v1], axis=-1)


def _unpair_cols(w):
    g0, v0 = w[..., 0:FF_HALF], w[..., FF_HALF:D_FF]
    g1, v1 = w[..., D_FF:D_FF + FF_HALF], w[..., D_FF + FF_HALF:]
    return jnp.concatenate([g0, g1, v0, v1], axis=-1)


def _row(v):
    return v.reshape(1, -1)


def kernel(x, c, w_ada, b_ada, w_in, b_forget, w_fox_proj, w_sb_proj, w_o, ln1_g, ln1_b, w_up, conv_w, conv_b, w_down, ln2_g, ln2_b, loss_target, m_w_ada, m_b_ada, m_w_in, m_b_forget, m_w_fox_proj, m_w_sb_proj, m_w_o, m_ln1_g, m_ln1_b, m_w_up, m_conv_w, m_conv_b, m_w_down, m_ln2_g, m_ln2_b, v_w_ada, v_b_ada, v_w_in, v_b_forget, v_w_fox_proj, v_w_sb_proj, v_w_o, v_ln1_g, v_ln1_b, v_w_up, v_conv_w, v_conv_b, v_w_down, v_ln2_g, v_ln2_b):
    s = x.shape[1]
    me = 4 * lax.axis_index("x") + 2 * lax.axis_index("y") + lax.axis_index("c")
    x2 = x.reshape(s, D_MODEL)
    tgt = loss_target.reshape(s, D_MODEL)

    b_ada_loc = lax.dynamic_slice(b_ada, (me * ADA_SHARD,), (ADA_SHARD,)).reshape(1, ADA_SHARD)
    c_all, mod = _mod_exchange(c, w_ada, b_ada_loc)
    mod = mod.reshape(N_MOD, 1, D_MODEL)
    sh1, sc1, gt1, sh2, sc2, gt2 = [mod[i] for i in range(N_MOD)]

    (g_in,) = _exchange("ag_w_in", [w_in.astype(BF16)], scatter=False)
    late_weights = _Ride([w_fox_proj.astype(BF16), w_sb_proj.astype(BF16), w_o.astype(BF16), w_up.astype(BF16),
                          w_down.astype(BF16), conv_w], scatter=False)
    w_in_f = _cols_from_slots(g_in)
    w_proj = jnp.concatenate(
        [w_in_f[:, 0:1536], w_in_f[:, 1544:3080], w_in_f[:, 3080:5128], w_in_f[:, 1536:1544],
         jnp.zeros((D_MODEL, W_PROJ - 5128), BF16)], axis=1)
    w_qkv, w_gates, w_f = w_proj[:, :W_QKV], w_proj[:, W_QKV:W_QKV + W_GATES], w_proj[:, W_QKV + W_GATES:W_QKV + W_GATES + W_F]
    conv_b_p = _pair_cols(_row(conv_b))
    b_f_pad = jnp.pad(_row(b_forget), ((0, 0), (0, LANES - N_HEADS)))

    (u1,) = _rowwise("modulate1", lambda xb, sc, sh: (xb * (1.0 + sc) + sh,),
                     [(x2, D_MODEL, 0)], [sc1, sh1], [(D_MODEL, BF16)], tr=512)
    qkv = _mm(u1, w_qkv, name="mm_qkv", out_dtype=BF16)
    gates = _mm(u1, w_gates, name="mm_gates")
    f_raw = _mm(u1, w_f, name="mm_forget")
    cum_col = _forget_cumsum(f_raw, b_f_pad)
    cum_row = jnp.transpose(cum_col[:, :N_HEADS]).reshape(N_HEADS // 2, 2, s)
    (y_fox, y_fox32, lse), (g_fox, g_sb, g_o, g_up, g_down, g_cw) = _fox_fwd(qkv, cum_col, cum_row, ride=late_weights)
    w_fox_f = _cols_from_slots(g_fox)
    w_sb_f = _cols_from_slots(g_sb)
    w_o_f = g_o.reshape(D_MODEL, D_MODEL)
    w_up_p = _pair_cols(_cols_from_slots(g_up))
    w_down_f = g_down.reshape(D_FF, D_MODEL)
    conv_w_p = _pair_cols(_cols_from_slots(g_cw))
    y_sb, sb_run = _sb_fwd(qkv)
    pf = _mm(y_fox, w_fox_f, name="mm_fox_proj", out_dtype=BF16)
    ps = _mm(y_sb, w_sb_f, name="mm_sb_proj", out_dtype=BF16)
    (merged,) = _rowwise("gate_merge", lambda ga, gb, a, b: (_sigmoid(ga) * a + _sigmoid(gb) * b,),
                         [(gates, D_MODEL, 0), (gates, D_MODEL, 1), (pf, D_MODEL, 0), (ps, D_MODEL, 0)], [],
                         [(D_MODEL, BF16)])
    attn_out = _mm(merged, w_o_f, name="mm_w_o")

    def ln_fwd(xb, fb, gt, g, b):
        xhat, _ = _ln_stats(ALPHA * xb + (1.0 + gt) * fb)
        return xhat * g + b

    def ln1_mod(xb, fb, gt, g, b, sc, sh):
        y = ln_fwd(xb, fb, gt, g, b)
        return y, y * (1.0 + sc) + sh

    x1, u2 = _rowwise("ln1_modulate2", ln1_mod, [(x2, D_MODEL, 0), (attn_out, D_MODEL, 0)],
                      [gt1, _row(ln1_g), _row(ln1_b), sc2, sh2], [(D_MODEL, F32), (D_MODEL, BF16)])

    hpre = _mm(u2, w_up_p, name="mm_w_up", tn=1408)
    act = _conv_gate_fwd(hpre, conv_w_p, conv_b_p)
    ffn_out = _mm(act, w_down_f, name="mm_w_down", tk=2816)

    def ln2_bwd(xb, fb, tb, gt, g, b):
        xhat, rstd = _ln_stats(ALPHA * xb + (1.0 + gt) * fb)
        err = (xhat * g + b) - tb
        dy = err * (1.0 / D_MODEL)
        dr = _ln_bwd(dy, xhat, rstd, g)
        return (dr * (1.0 + gt), ALPHA * dr,
                _colsum(err * err), _colsum(dy * xhat), _colsum(dy), _colsum(dr * fb))

    dffn, dx1_res, sq_err, d_ln2_g, d_ln2_b, d_gt2 = _rowwise(
        "ln2_bwd", ln2_bwd, [(x1, D_MODEL, 0), (ffn_out, D_MODEL, 0), (tgt, D_MODEL, 0)],
        [gt2, _row(ln2_g), _row(ln2_b)], [(D_MODEL, BF16), (D_MODEL, F32)], sums=[D_MODEL] * 4)
    loss = lax.psum(0.5 * jnp.sum(sq_err) / D_MODEL, ("x", "y", "c"))

    d_w_down = _mm(act, dffn, name="mm_d_w_down", ta=True, tm=1408, tk=2048, out_dtype=BF16)
    d_act = _mm(dffn, w_down_f, name="mm_d_act", tb=True, tn=1408, out_dtype=BF16)
    dh, d_conv_b_p, d_conv_w_p = _conv_gate_bwd(hpre, d_act, conv_w_p, conv_b_p)
    dhpre = _conv_input_bwd(dh, conv_w_p)
    d_w_up_p = _mm(u2, dhpre, name="mm_d_w_up", ta=True, tn=1408, tk=2048, out_dtype=BF16)
    du2 = _mm(dhpre, w_up_p, name="mm_d_u2", tb=True, tk=2816)

    def ln1_bwd(du, dres, x1b, xb, fb, sc, gt, g):
        dx1 = dres + du * (1.0 + sc)
        xhat, rstd = _ln_stats(ALPHA * xb + (1.0 + gt) * fb)
        dr = _ln_bwd(dx1, xhat, rstd, g)
        return (dr * (1.0 + gt), ALPHA * dr,
                _colsum(du * x1b), _colsum(du), _colsum(dx1 * xhat), _colsum(dx1), _colsum(dr * fb))

    d_attn, dx_res, d_sc2, d_sh2, d_ln1_g, d_ln1_b, d_gt1 = _rowwise(
        "ln1_bwd", ln1_bwd,
        [(du2, D_MODEL, 0), (dx1_res, D_MODEL, 0), (x1, D_MODEL, 0), (x2, D_MODEL, 0), (attn_out, D_MODEL, 0)],
        [sc2, gt1, _row(ln1_g)], [(D_MODEL, BF16), (D_MODEL, F32)], sums=[D_MODEL] * 5)

    d_w_o = _mm(merged, d_attn, name="mm_d_w_o", ta=True, out_dtype=BF16)
    d_merged = _mm(d_attn, w_o_f, name="mm_d_merged", tb=True, out_dtype=BF16)

    def merge_bwd(dm, ga, gb, a, b):
        dm, a, b = dm.astype(F32), a.astype(F32), b.astype(F32)
        sa, sb = _sigmoid(ga), _sigmoid(gb)
        return dm * a * sa * (1.0 - sa), dm * b * sb * (1.0 - sb), dm * sa, dm * sb

    d_ga, d_gb, d_pf, d_ps = _rowwise(
        "gate_merge_bwd", merge_bwd,
        [(d_merged, D_MODEL, 0), (gates, D_MODEL, 0), (gates, D_MODEL, 1), (pf, D_MODEL, 0), (ps, D_MODEL, 0)], [],
        [(D_MODEL, BF16)] * 4)
    d_w_fox = _mm(y_fox, d_pf, name="mm_d_w_fox", ta=True, out_dtype=BF16)
    d_w_sb = _mm(y_sb, d_ps, name="mm_d_w_sb", ta=True, out_dtype=BF16)
    d_y_fox = _mm(d_pf, w_fox_f, name="mm_d_y_fox", tb=True, out_dtype=BF16)
    d_y_sb = _mm(d_ps, w_sb_f, name="mm_d_y_sb", tb=True, out_dtype=BF16)
    early_grads = _Ride(
        [_cols_to_slots(d_w_fox), _cols_to_slots(d_w_sb), d_w_o.reshape(N_DEV, D_MODEL // N_DEV, D_MODEL),
         _cols_to_slots(_unpair_cols(d_w_up_p)), d_w_down.reshape(N_DEV, D_FF // N_DEV, D_MODEL)], scatter=True)
    (dq_a, dk_a, dv_a, d_cum_row), early_slots = _fox_bwd(qkv, cum_col, cum_row, y_fox32, lse, d_y_fox,
                                                                    ride=early_grads)
    dq_b, dk_b, dv_b = _sb_bwd(qkv, sb_run, d_y_sb)
    d_cum = jnp.transpose(d_cum_row.reshape(N_HEADS, s))
    d_cum = jnp.pad(d_cum, ((0, 0), (0, LANES - N_HEADS)))
    d_f, d_b_forget = _forget_bwd(d_cum, f_raw, b_f_pad)
    d_proj = jnp.concatenate([dq_a, dk_a, dv_a, dq_b, dk_b, dv_b, d_ga, d_gb, d_f,
                              jnp.zeros((s, W_PROJ - W_QKV - W_GATES - W_F), BF16)], axis=1)
    d_w_proj = _mm(u1, d_proj, name="mm_d_w_in", ta=True, tn=896, tk=2048, out_dtype=BF16)
    d_w_in_f = jnp.concatenate([d_w_proj[:, 0:1536], d_w_proj[:, 5120:5128], d_w_proj[:, 1536:3072],
                                d_w_proj[:, 3072:5120]], axis=1)
    du1, (in_slots,) = _mm(d_proj, w_proj, name="mm_d_u1", tb=True, tk=1792,
                           ride=_Ride([_cols_to_slots(d_w_in_f)], scatter=True))

    def x_bwd(du, dres, xb, sc):
        return dres + du * (1.0 + sc), _colsum(du * xb), _colsum(du)

    grad_x, d_sc1, d_sh1 = _rowwise("x_bwd", x_bwd, [(du1, D_MODEL, 0), (dx_res, D_MODEL, 0), (x2, D_MODEL, 0)],
                                    [sc1], [(D_MODEL, F32)], sums=[D_MODEL] * 2, tr=512)

    d_conv_b = _unpair_cols(d_conv_b_p)
    d_conv_w = _unpair_cols(d_conv_w_p)
    n_rep = N_MOD * D_MODEL + LANES + 4 * D_MODEL + 2 * D_FF
    small = jnp.concatenate(
        [d_sh1, d_sc1, d_gt1, d_sh2, d_sc2, d_gt2, d_b_forget, d_ln1_g, d_ln1_b, d_ln2_g, d_ln2_b, d_conv_b,
         d_conv_w.reshape(1, 6 * D_FF)], axis=1)
    n_small = small.shape[1] // LANES
    small = jnp.pad(small.reshape(n_small, LANES), ((0, 264 - n_small), (0, 0)))
    (small_all,) = _exchange("ag_small_grads", [small], scatter=False)
    rep8 = small_all[:, :n_rep // LANES, :]
    cw8 = small_all[:, n_rep // LANES:n_small, :].reshape(N_DEV, 3, 2 * D_FF)
    cw8 = lax.dynamic_slice(cw8, (0, 0, me * UP_SHARD), (N_DEV, 3, UP_SHARD))
    dmod8 = small_all[:, :N_MOD * D_MODEL // LANES, :].reshape(N_DEV, N_MOD * D_MODEL)
    dmod_loc = lax.dynamic_slice(dmod8, (0, me * ADA_SHARD), (N_DEV, ADA_SHARD))

    def pack_rep(b_a, b_f, g1, b1, g2, b2, cb):
        flat = jnp.concatenate([b_a, jnp.pad(b_f, (0, LANES - N_HEADS)), g1, b1, g2, b2, cb])
        return flat.reshape(n_rep // LANES, LANES)

    rep = _adamw("adamw_small", rep8, pack_rep(b_ada, b_forget, ln1_g, ln1_b, ln2_g, ln2_b, conv_b),
                 pack_rep(m_b_ada, m_b_forget, m_ln1_g, m_ln1_b, m_ln2_g, m_ln2_b, m_conv_b),
                 pack_rep(v_b_ada, v_b_forget, v_ln1_g, v_ln1_b, v_ln2_g, v_ln2_b, v_conv_b))

    def unpack_rep(p):
        flat = p.reshape(-1)
        o = N_MOD * D_MODEL
        return {"b_ada": flat[:o], "b_forget": flat[o:o + N_HEADS],
                "ln1_g": flat[o + 128:o + 1152], "ln1_b": flat[o + 1152:o + 2176],
                "ln2_g": flat[o + 2176:o + 3200], "ln2_b": flat[o + 3200:o + 4224], "conv_b": flat[o + 4224:]}

    rep = [unpack_rep(p) for p in rep]
    r_conv_w = _adamw("adamw_conv_w", cw8, conv_w, m_conv_w, v_conv_w)
    r_ada = _adamw_ada(jnp.transpose(c_all.reshape(N_DEV, D_MODEL)), dmod_loc, w_ada, m_w_ada, v_w_ada)

    r_in = _adamw("adamw_w_in", in_slots, w_in, m_w_in, v_w_in)
    r_fox = _adamw("adamw_w_fox", early_slots[0], w_fox_proj, m_w_fox_proj, v_w_fox_proj)
    r_sb = _adamw("adamw_w_sb", early_slots[1], w_sb_proj, m_w_sb_proj, v_w_sb_proj)
    r_o = _adamw("adamw_w_o", early_slots[2], w_o, m_w_o, v_w_o)
    r_up = _adamw("adamw_w_up", early_slots[3], w_up, m_w_up, v_w_up)
    r_down = _adamw("adamw_w_down", early_slots[4], w_down, m_w_down, v_w_down)

    def leaf(i):
        return [r_ada[i], rep[i]["b_ada"], r_in[i], rep[i]["b_forget"], r_fox[i], r_sb[i], r_o[i], rep[i]["ln1_g"],
                rep[i]["ln1_b"], r_up[i], r_conv_w[i], rep[i]["conv_b"], r_down[i], rep[i]["ln2_g"], rep[i]["ln2_b"]]

    return (loss, grad_x.reshape(1, s, D_MODEL), *leaf(0), *leaf(1), *leaf(2), *leaf(3))
```

```python
import functools

import jax
import jax.numpy as jnp
from jax import lax
from jax.experimental import pallas as pl
from jax.experimental.pallas import tpu as pltpu

F32 = jnp.float32
BF16 = jnp.bfloat16
MESH = pl.DeviceIdType.MESH
ANY = pl.BlockSpec(memory_space=pl.ANY)

N_DEV = 8
D_MODEL = 1024
HEAD_DIM = 64
N_HEADS = 8
ATTN_W = N_HEADS * HEAD_DIM
D_FF = 2816
FF_HALF = D_FF // 2
N_MOD = 6
ADA_SHARD = N_MOD * D_MODEL // N_DEV
IN_SHARD = 641
UP_SHARD = 704
ATTN_SCALE = HEAD_DIM ** -0.5
ALPHA = 2.0 ** 0.25
LN_EPS = 1e-5
LANES = 128
TQ = 512
SCAN_W = 256
VMEM_LIMIT = 56 * 1024 * 1024

ADAM_LR, ADAM_B1, ADAM_B2, ADAM_EPS, ADAM_WD, ADAM_STEP = 0.001, 0.9, 0.999, 1e-08, 0.01, 10

W_QKV, W_GATES, W_F = 3072, 2048, 128
W_PROJ = 5376


def _params(sem=None):
    return pltpu.CompilerParams(dimension_semantics=sem, vmem_limit_bytes=VMEM_LIMIT)


def _tile(n, cap):
    if n <= cap:
        return n
    best = None
    for t in range(LANES, cap + 1, LANES):
        if n % t == 0:
            best = t
    assert best is not None, (n, cap)
    return best


def _row_tile(r, width, budget=192 * 1024):
    if r * width <= budget or r % 16:
        return r
    best = 16
    for t in range(16, r + 1, 16):
        if r % t == 0 and t * width <= budget:
            best = t
    return best


def _me():
    x, y, c = lax.axis_index("x"), lax.axis_index("y"), lax.axis_index("c")
    return x, y, c, 4 * x + 2 * y + c


def _peer(r):
    x, y, c, _ = _me()
    px = 1 - x if r & 4 else x
    py = 1 - y if r & 2 else y
    pc = 1 - c if r & 1 else c
    return (px, py, pc), 4 * px + 2 * py + pc


class _Ride:
    def __init__(self, arrays, scatter):
        self.arrays, self.scatter, self.n = list(arrays), scatter, len(arrays)
        self.in_specs = [ANY] * self.n
        self.out_specs = [ANY] * self.n
        self.out_shape = [jax.ShapeDtypeStruct(a.shape if scatter else (N_DEV,) + a.shape, a.dtype) for a in arrays]
        self.scratch = [pltpu.SemaphoreType.DMA((self.n, N_DEV - 1)), pltpu.SemaphoreType.DMA((self.n, N_DEV - 1)),
                        pltpu.SemaphoreType.DMA((self.n,))]

    def _local(self, ins, outs, sems, a):
        me = _me()[3]
        return pltpu.make_async_copy(ins[a].at[me] if self.scatter else ins[a], outs[a].at[me], sems[2].at[a])

    def _remote(self, ins, outs, sems, a, r, arriving):
        me = _me()[3]
        peer, pidx = _peer(r)
        src = ins[a].at[me if arriving else pidx] if self.scatter else ins[a]
        return pltpu.make_async_remote_copy(
            src_ref=src, dst_ref=outs[a].at[pidx if arriving else me], send_sem=sems[0].at[a, r - 1],
            recv_sem=sems[1].at[a, r - 1], device_id=peer, device_id_type=MESH)

    def start(self, ins, outs, sems):
        for a in range(self.n):
            self._local(ins, outs, sems, a).start()
        for r in range(1, N_DEV):
            for a in range(self.n):
                self._remote(ins, outs, sems, a, r, False).start()

    def wait(self, ins, outs, sems):
        for r in range(1, N_DEV):
            for a in range(self.n):
                self._remote(ins, outs, sems, a, r, True).wait_recv()
        for r in range(1, N_DEV):
            for a in range(self.n):
                self._remote(ins, outs, sems, a, r, False).wait_send()
        for a in range(self.n):
            self._local(ins, outs, sems, a).wait()


def _exchange(name, arrays, scatter):
    ride = _Ride(arrays, scatter)

    def body(*refs):
        ins, outs, sems = refs[:ride.n], refs[ride.n:2 * ride.n], refs[2 * ride.n:]
        ride.start(ins, outs, sems)
        ride.wait(ins, outs, sems)

    return pl.pallas_call(body, name=name, in_specs=ride.in_specs, out_specs=ride.out_specs, out_shape=ride.out_shape,
                          scratch_shapes=ride.scratch)(*arrays)


def _allgather_two_level(name, a):
    def body(a_ref, out_ref, send_sems, recv_sems, local_sem):
        x, y, c, me = _me()
        sibling = (x, y, 1 - c)
        chips = [(1 - x, y), (x, 1 - y), (1 - x, 1 - y)]

        def idx(px, py, pc):
            return 4 * px + 2 * py + pc

        def copy(k, block, to, src=None):
            slot = out_ref.at[idx(*block)]
            return pltpu.make_async_remote_copy(
                src_ref=slot if src is None else src, dst_ref=slot, send_sem=send_sems.at[k], recv_sem=recv_sems.at[k],
                device_id=to, device_id_type=MESH)

        mine = pltpu.make_async_copy(a_ref, out_ref.at[me], local_sem)
        mine.start()
        first = [copy(0, (x, y, c), sibling, src=a_ref)]
        first += [copy(1 + j, (x, y, c), (*chip, c), src=a_ref) for j, chip in enumerate(chips)]
        for cp in first:
            cp.start()
        passed = [copy(4 + j, (*chip, c), sibling) for j, chip in enumerate(chips)]
        for j, chip in enumerate(chips):
            copy(1 + j, (*chip, c), (x, y, c)).wait_recv()
            passed[j].start()
        copy(0, sibling, (x, y, c)).wait_recv()
        for j, chip in enumerate(chips):
            copy(4 + j, (*chip, 1 - c), (x, y, c)).wait_recv()
        for cp in first + passed:
            cp.wait_send()
        mine.wait()

    return pl.pallas_call(
        body, name=name, in_specs=[ANY], out_specs=ANY,
        out_shape=jax.ShapeDtypeStruct((N_DEV,) + a.shape, a.dtype),
        scratch_shapes=[pltpu.SemaphoreType.DMA((N_DEV - 1,)), pltpu.SemaphoreType.DMA((N_DEV - 1,)),
                        pltpu.SemaphoreType.DMA],
    )(a)


def _with_ride(body, ride, n_in, n_out, grid):
    if ride is None:
        return body
    n = ride.n

    def wrapped(*refs):
        ins, rins = refs[:n_in], refs[n_in:n_in + n]
        outs, routs = refs[n_in + n:n_in + n + n_out], refs[n_in + n + n_out:n_in + 2 * n + n_out]
        rest = refs[n_in + 2 * n + n_out:]
        scratch, sems = rest[:len(rest) - 3], rest[len(rest) - 3:]
        ids = [pl.program_id(d) for d in range(len(grid))]
        first = functools.reduce(lambda p, q: p & q, [i == 0 for i in ids])
        last = functools.reduce(lambda p, q: p & q, [i == g - 1 for i, g in zip(ids, grid)])

        @pl.when(first)
        def _():
            ride.start(rins, routs, sems)

        body(*ins, *outs, *scratch)

        @pl.when(last)
        def _():
            ride.wait(rins, routs, sems)

    return wrapped


def _ride_call(body, ride, *, name, grid, in_specs, out_specs, out_shape, scratch_shapes, sem, args):
    n_in, n_out = len(in_specs), len(out_specs)
    if ride is None:
        res = pl.pallas_call(body, name=name, grid=grid, in_specs=in_specs, out_specs=out_specs, out_shape=out_shape,
                             scratch_shapes=scratch_shapes, compiler_params=_params(sem))(*args)
        return list(res), []
    res = pl.pallas_call(
        _with_ride(body, ride, n_in, n_out, grid), name=name, grid=grid,
        in_specs=list(in_specs) + ride.in_specs, out_specs=list(out_specs) + ride.out_specs,
        out_shape=list(out_shape) + ride.out_shape, scratch_shapes=list(scratch_shapes) + ride.scratch,
        compiler_params=_params(("arbitrary",) * len(grid)))(*args, *ride.arrays)
    return list(res[:n_out]), list(res[n_out:])


def _mm(a, b, *, name, ta=False, tb=False, out_dtype=F32, tm=1024, tn=1024, tk=1024, ride=None):
    m, k = (a.shape[1], a.shape[0]) if ta else a.shape
    n = b.shape[0] if tb else b.shape[1]
    assert (b.shape[1] if tb else b.shape[0]) == k
    tm, tn, tk = _tile(m, tm), _tile(n, tn), _tile(k, tk)
    nk = k // tk
    a_spec = pl.BlockSpec((tk, tm), lambda i, j, l: (l, i)) if ta else pl.BlockSpec((tm, tk), lambda i, j, l: (i, l))
    b_spec = pl.BlockSpec((tn, tk), lambda i, j, l: (j, l)) if tb else pl.BlockSpec((tk, tn), lambda i, j, l: (l, j))
    dims = (((0,) if ta else (1,), (1,) if tb else (0,)), ((), ()))

    def body(a_ref, b_ref, o_ref, *acc):
        p = lax.dot_general(a_ref[...].astype(BF16), b_ref[...].astype(BF16), dims, preferred_element_type=F32)
        if nk == 1:
            o_ref[...] = p.astype(out_dtype)
            return
        acc_ref = acc[0]
        step = pl.program_id(2)

        @pl.when(step == 0)
        def _():
            acc_ref[...] = p

        @pl.when(step > 0)
        def _():
            acc_ref[...] += p

        @pl.when(step == nk - 1)
        def _():
            o_ref[...] = acc_ref[...].astype(out_dtype)

    outs, rode = _ride_call(
        body, ride, name=name, grid=(m // tm, n // tn, nk), in_specs=[a_spec, b_spec],
        out_specs=[pl.BlockSpec((tm, tn), lambda i, j, l: (i, j))], out_shape=[jax.ShapeDtypeStruct((m, n), out_dtype)],
        scratch_shapes=[] if nk == 1 else [pltpu.VMEM((tm, tn), F32)], sem=("parallel", "parallel", "arbitrary"),
        args=(a, b))
    return outs[0] if ride is None else (outs[0], rode)


def _rowwise(name, fn, rows, vecs, outs, sums=(), tr=256):
    s = rows[0][0].shape[0]
    tr = min(tr, s)
    nr, nv, no = len(rows), len(vecs), len(outs)

    def body(*refs):
        vals = [r[...] for r in refs[:nr + nv]]
        res = fn(*vals)
        for o_ref, val in zip(refs[nr + nv:nr + nv + no], res[:no]):
            o_ref[...] = val.astype(o_ref.dtype)
        step = pl.program_id(0)
        for s_ref, val in zip(refs[nr + nv + no:], res[no:]):
            @pl.when(step == 0)
            def _(s_ref=s_ref, val=val):
                s_ref[...] = val

            @pl.when(step > 0)
            def _(s_ref=s_ref, val=val):
                s_ref[...] += val

    in_specs = [pl.BlockSpec((tr, w), functools.partial(lambda i, cb: (i, cb), cb=cb)) for _, w, cb in rows]
    in_specs += [pl.BlockSpec(v.shape, lambda i: (0, 0)) for v in vecs]
    out_specs = [pl.BlockSpec((tr, w), lambda i: (i, 0)) for w, _ in outs]
    out_specs += [pl.BlockSpec((1, w), lambda i: (0, 0)) for w in sums]
    out_shape = [jax.ShapeDtypeStruct((s, w), dt) for w, dt in outs]
    out_shape += [jax.ShapeDtypeStruct((1, w), F32) for w in sums]
    return pl.pallas_call(
        body, name=name, grid=(s // tr,), in_specs=in_specs, out_specs=out_specs, out_shape=out_shape,
        compiler_params=_params(("arbitrary",) if sums else ("parallel",)),
    )(*[r[0] for r in rows], *vecs)


def _colsum(x):
    return jnp.sum(x, axis=0, keepdims=True)


def _sigmoid(x):
    return 1.0 / (1.0 + jnp.exp(-x))


def _log_sigmoid(x):
    return jnp.minimum(x, 0.0) - jnp.log(1.0 + jnp.exp(-jnp.abs(x)))


def _ln_stats(r):
    mu = jnp.mean(r, axis=-1, keepdims=True)
    xc = r - mu
    var = jnp.mean(xc * xc, axis=-1, keepdims=True)
    rstd = lax.rsqrt(var + LN_EPS)
    return xc * rstd, rstd


def _ln_bwd(dy, xhat, rstd, g):
    dxh = dy * g
    m1 = jnp.mean(dxh, axis=-1, keepdims=True)
    m2 = jnp.mean(dxh * xhat, axis=-1, keepdims=True)
    return rstd * (dxh - m1 - xhat * m2)


def _mod_exchange(c_row, w_ada, b_ada_loc):
    def body(c_ref, w_ref, b_ref, call_ref, mod_ref, piece_ref, send_sems, recv_sems):
        me = _me()[3]
        call_ref[me] = c_ref[...]
        sent = []
        for r in range(1, N_DEV):
            peer, _ = _peer(r)
            cp = pltpu.make_async_remote_copy(
                src_ref=c_ref, dst_ref=call_ref.at[me], send_sem=send_sems.at[0, r - 1],
                recv_sem=recv_sems.at[0, r - 1], device_id=peer, device_id_type=MESH)
            cp.start()
            sent.append(cp)
        for r in range(1, N_DEV):
            peer, pidx = _peer(r)
            pltpu.make_async_remote_copy(
                src_ref=c_ref, dst_ref=call_ref.at[pidx], send_sem=send_sems.at[0, r - 1],
                recv_sem=recv_sems.at[0, r - 1], device_id=peer, device_id_type=MESH).wait_recv()
        c_all = jnp.concatenate([call_ref[d] for d in range(N_DEV)], axis=0)
        mod_loc = jnp.dot(c_all, w_ref[...], preferred_element_type=F32,
                          precision=lax.Precision.HIGHEST) + b_ref[...]
        for d in range(N_DEV):
            piece_ref[d] = mod_loc[d:d + 1, :]
        mod_ref[me] = piece_ref[me]
        for r in range(1, N_DEV):
            peer, pidx = _peer(r)
            cp = pltpu.make_async_remote_copy(
                src_ref=piece_ref.at[pidx], dst_ref=mod_ref.at[me], send_sem=send_sems.at[1, r - 1],
                recv_sem=recv_sems.at[1, r - 1], device_id=peer, device_id_type=MESH)
            cp.start()
            sent.append(cp)
        for r in range(1, N_DEV):
            peer, pidx = _peer(r)
            pltpu.make_async_remote_copy(
                src_ref=piece_ref.at[me], dst_ref=mod_ref.at[pidx], send_sem=send_sems.at[1, r - 1],
                recv_sem=recv_sems.at[1, r - 1], device_id=peer, device_id_type=MESH).wait_recv()
        for cp in sent:
            cp.wait_send()

    vmem = pl.BlockSpec(memory_space=pltpu.VMEM)
    return pl.pallas_call(
        body, name="mod_exchange", in_specs=[vmem, vmem, vmem], out_specs=[vmem, vmem],
        out_shape=[jax.ShapeDtypeStruct((N_DEV, 1, D_MODEL), F32), jax.ShapeDtypeStruct((N_DEV, 1, ADA_SHARD), F32)],
        scratch_shapes=[pltpu.VMEM((N_DEV, 1, ADA_SHARD), F32),
                        pltpu.SemaphoreType.DMA((2, N_DEV - 1)), pltpu.SemaphoreType.DMA((2, N_DEV - 1))],
        compiler_params=_params(),
    )(c_row, w_ada, b_ada_loc)


def _split3(x):
    hi = x.astype(BF16)
    r1 = x - hi.astype(F32)
    mid = r1.astype(BF16)
    lo = (r1 - mid.astype(F32)).astype(BF16)
    return hi, mid, lo


def _scan_rows(x_ref, o_ref, s, reverse, pre=None, post=None):
    tb = min(TQ, s)
    nb = s // tb
    row = lax.broadcasted_iota(jnp.int32, (tb, tb), 0)
    col = lax.broadcasted_iota(jnp.int32, (tb, tb), 1)
    tri = jnp.where((col >= row) if reverse else (col <= row), 1.0, 0.0).astype(BF16)

    def step(i, carry):
        blk = (nb - 1 - i) if reverse else i
        off = pl.multiple_of(blk * tb, tb)
        x = x_ref[pl.ds(off, tb), :]
        if pre is not None:
            x = pre(x, off)
        acc = carry
        for piece in _split3(x):
            acc = acc + jnp.dot(tri, piece, preferred_element_type=F32)
        o_ref[pl.ds(off, tb), :] = acc if post is None else post(acc, off)
        edge = acc[0:1, :] if reverse else acc[tb - 1:tb, :]
        return jnp.broadcast_to(edge, (tb, LANES))

    lax.fori_loop(0, nb, step, jnp.zeros((tb, LANES), F32))


def _forget_cumsum(f_raw, b_pad):
    s = f_raw.shape[0]

    def body(f_ref, b_ref, cum_ref):
        b = b_ref[...]
        _scan_rows(f_ref, cum_ref, s, False, pre=lambda x, off: _log_sigmoid(x + b))

    vmem = pl.BlockSpec(memory_space=pltpu.VMEM)
    return pl.pallas_call(body, name="forget_cumsum", in_specs=[vmem, vmem], out_specs=vmem,
                          out_shape=jax.ShapeDtypeStruct((s, LANES), F32), compiler_params=_params())(f_raw, b_pad)


def _forget_bwd(dcum, f_raw, b_pad):
    s = f_raw.shape[0]

    def body(d_ref, f_ref, b_ref, df_ref, db_ref, tmp_ref):
        b = b_ref[...]
        _scan_rows(d_ref, tmp_ref, s, True)
        df = tmp_ref[...] * _sigmoid(-(f_ref[...] + b))
        df_ref[...] = df.astype(BF16)
        db_ref[...] = _colsum(df)

    vmem = pl.BlockSpec(memory_space=pltpu.VMEM)
    return pl.pallas_call(
        body, name="forget_bwd", in_specs=[vmem, vmem, vmem], out_specs=[vmem, vmem],
        out_shape=[jax.ShapeDtypeStruct((s, LANES), BF16), jax.ShapeDtypeStruct((1, LANES), F32)],
        scratch_shapes=[pltpu.VMEM((s, LANES), F32)], compiler_params=_params())(dcum, f_raw, b_pad)


def _dot_nt(a, b):
    return lax.dot_general(a, b, (((1,), (1,)), ((), ())), preferred_element_type=F32)


def _head_masks():
    lane = lax.broadcasted_iota(jnp.int32, (TQ, LANES), 1)
    return lane, [lane < HEAD_DIM, lane >= HEAD_DIM]


def _pick(mask, x):
    return jnp.where(mask, x, jnp.zeros_like(x))


def _qkv_specs(s, col0):
    nb = ATTN_W // LANES
    return [pl.BlockSpec((TQ, LANES), lambda hp, qi: (qi, col0 + hp)),
            pl.BlockSpec((s, LANES), lambda hp, qi: (0, col0 + nb + hp)),
            pl.BlockSpec((s, LANES), lambda hp, qi: (0, col0 + 2 * nb + hp))]


def _pair_spec():
    return pl.BlockSpec((TQ, LANES), lambda hp, qi: (qi, hp))


def _fox_fwd(qkv, cum_col, cum_row, ride=None):
    s = qkv.shape[0]
    nq = s // TQ

    def body(q_ref, k_ref, v_ref, cc_ref, cr_ref, o_ref, o32_ref, lse_ref):
        hp, qi = pl.program_id(0), pl.program_id(1)
        lane, masks = _head_masks()
        row = lax.broadcasted_iota(jnp.int32, (TQ, TQ), 0)
        col = lax.broadcasted_iota(jnp.int32, (TQ, TQ), 1)
        causal = col <= row
        q2 = q_ref[...] * jnp.asarray(ATTN_SCALE, BF16)
        cc = cc_ref[...]
        qms = [_pick(masks[e], q2) for e in range(2)]
        cqs = [jnp.sum(jnp.where(lane == 2 * hp + e, cc, 0.0), axis=1, keepdims=True) for e in range(2)]

        def tile(kb, carry, masked, width=1):
            off, span = pl.multiple_of(kb * TQ, TQ), width * TQ
            k2, v2 = k_ref[pl.ds(off, span), :], v_ref[pl.ds(off, span), :]
            head0 = lax.broadcasted_iota(jnp.int32, (span, LANES), 1) < HEAD_DIM
            new = []
            for e in range(2):
                m, acc = carry[e]
                sc = _dot_nt(qms[e], k2) + (cqs[e] - cr_ref[e:e + 1, pl.ds(off, span)])
                if masked:
                    sc = jnp.where(causal, sc, -jnp.inf)
                m_new = jnp.maximum(m, jnp.max(sc, axis=1, keepdims=True))
                p = jnp.exp(sc - m_new)
                v_ones = jnp.where(head0 if e == 0 else ~head0, v2, jnp.ones_like(v2))
                acc = jnp.exp(m - m_new) * acc + jnp.dot(p.astype(BF16), v_ones, preferred_element_type=F32)
                new.append((m_new, acc))
            return tuple(new)

        init = (jnp.full((TQ, 1), -jnp.inf, F32), jnp.zeros((TQ, LANES), F32))
        carry = lax.fori_loop(0, qi // 2, lambda j, cr: tile(2 * j, cr, False, 2), tile(qi, (init, init), True))
        carry = lax.cond(qi % 2 == 1, lambda cr: tile(qi - 1, cr, False), lambda cr: cr, carry)
        sums = [jnp.max(jnp.where(masks[1 - e], carry[e][1], 0.0), axis=1, keepdims=True) for e in range(2)]
        outs = [carry[e][1] / sums[e] for e in range(2)]
        lses = [carry[e][0] + jnp.log(sums[e]) for e in range(2)]
        out = jnp.where(masks[0], outs[0], outs[1])
        o_ref[...] = out.astype(BF16)
        o32_ref[...] = out
        lse_ref[...] = jnp.where(masks[0], lses[0], lses[1])

    return _ride_call(
        body, ride, name="fox_fwd", grid=(N_HEADS // 2, nq),
        in_specs=_qkv_specs(s, 0) + [pl.BlockSpec((TQ, LANES), lambda hp, qi: (qi, 0)),
                                     pl.BlockSpec((None, 2, s), lambda hp, qi: (hp, 0, 0))],
        out_specs=[_pair_spec(), _pair_spec(), _pair_spec()],
        out_shape=[jax.ShapeDtypeStruct((s, ATTN_W), BF16), jax.ShapeDtypeStruct((s, ATTN_W), F32),
                   jax.ShapeDtypeStruct((s, ATTN_W), F32)],
        scratch_shapes=[], sem=("parallel", "parallel"), args=(qkv, qkv, qkv, cum_col, cum_row))


def _write_transposed(acc_ref, out_ref):
    for c in range(out_ref.shape[0] // TQ):
        out_ref[c * TQ:(c + 1) * TQ, :] = jnp.transpose(acc_ref[:, c * TQ:(c + 1) * TQ]).astype(BF16)


def _fox_bwd(qkv, cum_col, cum_row, o, lse, do, ride=None):
    s = qkv.shape[0]
    nq = s // TQ

    def body(q_ref, k_ref, v_ref, cc_ref, cr_ref, o_ref, lse_ref, do_ref,
             dq_ref, dk_ref, dv_ref, dcr_ref, dk_acc, dv_acc):
        hp, qi = pl.program_id(0), pl.program_id(1)

        @pl.when(qi == 0)
        def _():
            dk_acc[...] = jnp.zeros_like(dk_acc)
            dv_acc[...] = jnp.zeros_like(dv_acc)
            dcr_ref[...] = jnp.zeros_like(dcr_ref)

        lane, masks = _head_masks()
        row = lax.broadcasted_iota(jnp.int32, (TQ, TQ), 0)
        col = lax.broadcasted_iota(jnp.int32, (TQ, TQ), 1)
        causal = col <= row
        q2 = q_ref[...] * jnp.asarray(ATTN_SCALE, BF16)
        do2 = do_ref[...]
        prod = do2.astype(F32) * o_ref[...].astype(F32)
        lse2 = lse_ref[...]
        cc = cc_ref[...]
        qms = [_pick(masks[e], q2) for e in range(2)]
        doms = [_pick(masks[e], do2) for e in range(2)]
        deltas = [jnp.sum(jnp.where(masks[e], prod, 0.0), axis=1, keepdims=True) for e in range(2)]
        lses = [jnp.max(jnp.where(masks[e], lse2, -jnp.inf), axis=1, keepdims=True) for e in range(2)]
        cqs = [jnp.sum(jnp.where(lane == 2 * hp + e, cc, 0.0), axis=1, keepdims=True) for e in range(2)]
        qts = [jnp.transpose(qms[e].astype(F32)).astype(BF16) for e in range(2)]
        dots = [jnp.transpose(doms[e].astype(F32)).astype(BF16) for e in range(2)]
        ones_row = [HEAD_DIM * (1 - e) for e in range(2)]
        trow = lax.broadcasted_iota(jnp.int32, (LANES, TQ), 0)
        qts = [jnp.where(trow == ones_row[e], jnp.ones_like(qts[e]), qts[e]) for e in range(2)]

        def tile(kb, carry, masked, width=1):
            off, span = pl.multiple_of(kb * TQ, TQ), width * TQ
            k2, v2 = k_ref[pl.ds(off, span), :], v_ref[pl.ds(off, span), :]
            head0 = lax.broadcasted_iota(jnp.int32, (span, LANES), 1) < HEAD_DIM
            new, dks, dv = [], [], None
            for e in range(2):
                dq = carry[e]
                sc = _dot_nt(qms[e], k2) + (cqs[e] - cr_ref[e:e + 1, pl.ds(off, span)])
                p = jnp.exp(sc - lses[e])
                if masked:
                    p = jnp.where(causal, p, 0.0)
                ds = p * (_dot_nt(doms[e], v2) - deltas[e])
                dsb = ds.astype(BF16)
                dk_e = jnp.dot(qts[e], dsb, preferred_element_type=F32)
                dv_e = jnp.dot(dots[e], p.astype(BF16), preferred_element_type=F32)
                dks.append(dk_e)
                dv = dv_e if e == 0 else dv + dv_e
                dcr_ref[e:e + 1, pl.ds(off, span)] -= dk_e[ones_row[e]:ones_row[e] + 1, :]
                k_ones = jnp.where(head0 if e == 0 else ~head0, k2, jnp.ones_like(k2))
                new.append(dq + jnp.dot(dsb, k_ones, preferred_element_type=F32))
            krow = lax.broadcasted_iota(jnp.int32, (LANES, span), 0)
            dk_acc[:, pl.ds(off, span)] += jnp.where(krow < HEAD_DIM, dks[0], dks[1])
            dv_acc[:, pl.ds(off, span)] += dv
            return tuple(new)

        init = jnp.zeros((TQ, LANES), F32)
        carry = lax.fori_loop(0, qi // 2, lambda j, cr: tile(2 * j, cr, False, 2), (init, init))
        carry = lax.cond(qi % 2 == 1, lambda cr: tile(qi - 1, cr, False), lambda cr: cr, carry)
        carry = tile(qi, carry, True)
        dq_ref[...] = (jnp.where(masks[0], carry[0], carry[1]) * ATTN_SCALE).astype(BF16)
        for e in range(2):
            dcr_ref[e:e + 1, pl.ds(pl.multiple_of(qi * TQ, TQ), TQ)] += jnp.transpose(carry[e])[
                ones_row[e]:ones_row[e] + 1, :]

        @pl.when(qi == nq - 1)
        def _():
            _write_transposed(dk_acc, dk_ref)
            _write_transposed(dv_acc, dv_ref)

    seq_spec = pl.BlockSpec((s, LANES), lambda hp, qi: (0, hp))
    return _ride_call(
        body, ride, name="fox_bwd", grid=(N_HEADS // 2, nq),
        in_specs=_qkv_specs(s, 0) + [pl.BlockSpec((TQ, LANES), lambda hp, qi: (qi, 0)),
                                     pl.BlockSpec((None, 2, s), lambda hp, qi: (hp, 0, 0)),
                                     _pair_spec(), _pair_spec(), _pair_spec()],
        out_specs=[_pair_spec(), seq_spec, seq_spec, pl.BlockSpec((None, 2, s), lambda hp, qi: (hp, 0, 0))],
        out_shape=[jax.ShapeDtypeStruct((s, ATTN_W), BF16)] * 3 + [jax.ShapeDtypeStruct((N_HEADS // 2, 2, s), F32)],
        scratch_shapes=[pltpu.VMEM((LANES, s), F32), pltpu.VMEM((LANES, s), F32)],
        sem=("parallel", "arbitrary"), args=(qkv, qkv, qkv, cum_col, cum_row, o, lse, do))


def _scan_matrix(reverse):
    row = lax.broadcasted_iota(jnp.int32, (SCAN_W, SCAN_W), 0)
    col = lax.broadcasted_iota(jnp.int32, (SCAN_W, SCAN_W), 1)
    return jnp.where((row > col) if reverse else (row < col), 1.0, 0.0).astype(BF16)


def _scan_cols(x, tri, reverse, init):
    nblk = x.shape[1] // SCAN_W
    parts, total = [None] * nblk, init
    far = 0 if reverse else SCAN_W - 1
    for b in (reversed(range(nblk)) if reverse else range(nblk)):
        blk = x[:, b * SCAN_W:(b + 1) * SCAN_W]
        part = jnp.dot(blk.astype(BF16), tri, preferred_element_type=F32)
        parts[b] = part + total
        total = total + (part[:, far:far + 1] + blk[:, far:far + 1])
    return (parts[0] if nblk == 1 else jnp.concatenate(parts, axis=1)), total


def _sb_logits(qm, k2):
    z = _dot_nt(qm, k2)
    neg_abs = lax.bitcast_convert_type(lax.bitcast_convert_type(z, jnp.uint32) | jnp.uint32(0x80000000), F32)
    soft = jnp.log(1.0 + jnp.exp(neg_abs))
    lb = jnp.minimum(z, 0.0) - soft
    return lb, lb - z


TILE_SLOTS = 4


def _tri_base(qi):
    return (qi * (qi + 1)) // 2


def _sb_fwd(qkv):
    s = qkv.shape[0]
    nq = s // TQ

    def body(q_ref, k_ref, v_ref, o_ref, t_ref, buf, sems):
        hp, qi = pl.program_id(0), pl.program_id(1)
        _, masks = _head_masks()
        row = lax.broadcasted_iota(jnp.int32, (TQ, TQ), 0)
        col = lax.broadcasted_iota(jnp.int32, (TQ, TQ), 1)
        strict = col < row
        suffix = _scan_matrix(True)
        q2 = q_ref[...] * jnp.asarray(ATTN_SCALE, BF16)
        qms = [_pick(masks[e], q2) for e in range(2)]
        base = _tri_base(qi)

        def store(e, kb):
            slot = kb % TILE_SLOTS
            return pltpu.make_async_copy(buf.at[e, slot], t_ref.at[2 * hp + e, base + kb], sems.at[e, slot])

        def tile(kb, carry, masked, width=1):
            off, span = pl.multiple_of(kb * TQ, TQ), width * TQ
            k2, v2 = k_ref[pl.ds(off, span), :], v_ref[pl.ds(off, span), :]
            new = []
            for e in range(2):
                run, acc = carry[e]
                lb, lo = _sb_logits(qms[e], k2)
                if masked:
                    lo = jnp.where(strict, lo, 0.0)
                rest, run = _scan_cols(lo, suffix, True, run)
                a = jnp.exp(lb + rest)
                if masked:
                    a = jnp.where(strict, a, 0.0)
                ab, lbb = a.astype(BF16), lb.astype(BF16)
                acc = acc + jnp.dot(ab, v2, preferred_element_type=F32)
                for w in range(width):
                    blk = kb + w

                    @pl.when(blk + TILE_SLOTS <= qi)
                    def _(e=e, blk=blk):
                        store(e, blk + TILE_SLOTS).wait()
                    buf[e, blk % TILE_SLOTS, 0] = ab[:, w * TQ:(w + 1) * TQ]
                    buf[e, blk % TILE_SLOTS, 1] = lbb[:, w * TQ:(w + 1) * TQ]
                    store(e, blk).start()
                new.append((run, acc))
            return tuple(new)

        init = (jnp.zeros((TQ, 1), F32), jnp.zeros((TQ, LANES), F32))
        carry = tile(qi, (init, init), True)
        carry = lax.cond(qi % 2 == 1, lambda cr: tile(qi - 1, cr, False), lambda cr: cr, carry)
        pairs = qi // 2
        carry = lax.fori_loop(0, pairs, lambda it, cr: tile(2 * (pairs - 1 - it), cr, False, 2), carry)
        for e in range(2):
            for blk in range(TILE_SLOTS):
                @pl.when(qi >= blk)
                def _(e=e, blk=blk):
                    store(e, blk).wait()
        o_ref[...] = jnp.where(masks[0], carry[0][1], carry[1][1]).astype(BF16)

    ntri = nq * (nq + 1) // 2
    return pl.pallas_call(
        body, name="sb_fwd", grid=(N_HEADS // 2, nq), in_specs=_qkv_specs(s, 3 * ATTN_W // LANES),
        out_specs=[_pair_spec(), ANY],
        out_shape=[jax.ShapeDtypeStruct((s, ATTN_W), BF16), jax.ShapeDtypeStruct((N_HEADS, ntri, 2, TQ, TQ), BF16)],
        scratch_shapes=[pltpu.VMEM((2, TILE_SLOTS, 2, TQ, TQ), BF16), pltpu.SemaphoreType.DMA((2, TILE_SLOTS))],
        compiler_params=_params(("arbitrary", "arbitrary")),
    )(qkv, qkv, qkv)


def _sb_bwd(qkv, tiles, do):
    s = qkv.shape[0]
    nq = s // TQ

    def body(q_ref, k_ref, v_ref, t_ref, do_ref, dq_ref, dk_ref, dv_ref, dk_acc, dv_acc, buf, sems):
        hp, qi = pl.program_id(0), pl.program_id(1)

        @pl.when(qi == 0)
        def _():
            dk_acc[...] = jnp.zeros_like(dk_acc)
            dv_acc[...] = jnp.zeros_like(dv_acc)

        _, masks = _head_masks()
        row = lax.broadcasted_iota(jnp.int32, (TQ, TQ), 0)
        col = lax.broadcasted_iota(jnp.int32, (TQ, TQ), 1)
        strict = col < row
        prefix = _scan_matrix(False)
        q2 = q_ref[...] * jnp.asarray(ATTN_SCALE, BF16)
        do2 = do_ref[...]
        qms = [_pick(masks[e], q2) for e in range(2)]
        doms = [_pick(masks[e], do2) for e in range(2)]
        qts = [jnp.transpose(qms[e].astype(F32)).astype(BF16) for e in range(2)]
        dots = [jnp.transpose(doms[e].astype(F32)).astype(BF16) for e in range(2)]
        base = _tri_base(qi)

        def fetch(e, kb):
            slot = kb % TILE_SLOTS
            return pltpu.make_async_copy(t_ref.at[2 * hp + e, base + kb], buf.at[e, slot], sems.at[e, slot])

        for e in range(2):
            fetch(e, 0).start()

            @pl.when(qi >= 1)
            def _(e=e):
                fetch(e, 1).start()

        def tile(kb, carry, masked, width=1):
            off, span = pl.multiple_of(kb * TQ, TQ), width * TQ
            k2, v2 = k_ref[pl.ds(off, span), :], v_ref[pl.ds(off, span), :]
            new, dk, dv = [], None, None
            for e in range(2):
                gsum, dq = carry[e]
                if not masked:
                    for blk in range(2, 2 + width):
                        @pl.when(kb + blk <= qi)
                        def _(e=e, blk=blk):
                            fetch(e, kb + blk).start()
                for w in range(width):
                    fetch(e, kb + w).wait()
                slots = [(kb + w) % TILE_SLOTS for w in range(width)]
                ab = buf[e, slots[0], 0] if width == 1 else jnp.concatenate([buf[e, sl, 0] for sl in slots], axis=1)
                lbb = buf[e, slots[0], 1] if width == 1 else jnp.concatenate([buf[e, sl, 1] for sl in slots], axis=1)
                beta = jnp.exp(lbb.astype(F32))
                g = ab.astype(F32) * _dot_nt(doms[e], v2)
                before, gsum = _scan_cols(g, prefix, False, gsum)
                dz = g - beta * (g + before)
                if masked:
                    dz = jnp.where(strict, dz, 0.0)
                dzb = dz.astype(BF16)
                dk_e = jnp.dot(qts[e], dzb, preferred_element_type=F32)
                dv_e = jnp.dot(dots[e], ab, preferred_element_type=F32)
                dk, dv = (dk_e, dv_e) if e == 0 else (dk + dk_e, dv + dv_e)
                new.append((gsum, dq + jnp.dot(dzb, k2, preferred_element_type=F32)))
            dk_acc[:, pl.ds(off, span)] += dk
            dv_acc[:, pl.ds(off, span)] += dv
            return tuple(new)

        init = (jnp.zeros((TQ, 1), F32), jnp.zeros((TQ, LANES), F32))
        carry = lax.fori_loop(0, qi // 2, lambda j, cr: tile(2 * j, cr, False, 2), (init, init))
        carry = lax.cond(qi % 2 == 1, lambda cr: tile(qi - 1, cr, False), lambda cr: cr, carry)
        carry = tile(qi, carry, True)
        dq_ref[...] = (jnp.where(masks[0], carry[0][1], carry[1][1]) * ATTN_SCALE).astype(BF16)

        @pl.when(qi == nq - 1)
        def _():
            _write_transposed(dk_acc, dk_ref)
            _write_transposed(dv_acc, dv_ref)

    seq_spec = pl.BlockSpec((s, LANES), lambda hp, qi: (0, hp))
    return pl.pallas_call(
        body, name="sb_bwd", grid=(N_HEADS // 2, nq),
        in_specs=_qkv_specs(s, 3 * ATTN_W // LANES) + [ANY, _pair_spec()],
        out_specs=[_pair_spec(), seq_spec, seq_spec],
        out_shape=[jax.ShapeDtypeStruct((s, ATTN_W), BF16)] * 3,
        scratch_shapes=[pltpu.VMEM((LANES, s), F32), pltpu.VMEM((LANES, s), F32),
                        pltpu.VMEM((2, TILE_SLOTS, 2, TQ, TQ), BF16), pltpu.SemaphoreType.DMA((2, TILE_SLOTS))],
        compiler_params=_params(("arbitrary", "arbitrary")),
    )(qkv, qkv, qkv, tiles, do)


CONV_TR = 256


def _shift_down(x, halo, n):
    rolled = pltpu.roll(x, n, 0)
    head = rolled[0:8, :]
    rid = lax.broadcasted_iota(jnp.int32, head.shape, 0)
    for j in range(n):
        head = jnp.where(rid == j, halo[8 - n + j:8 - n + j + 1, :], head)
    return jnp.concatenate([head, rolled[8:, :]], axis=0)


def _shift_up(x, halo, n):
    rows = x.shape[0]
    rolled = pltpu.roll(x, rows - n, 0)
    tail = rolled[rows - 8:, :]
    rid = lax.broadcasted_iota(jnp.int32, tail.shape, 0)
    for j in range(n):
        tail = jnp.where(rid == 8 - n + j, halo[j:j + 1, :], tail)
    return jnp.concatenate([rolled[:rows - 8, :], tail], axis=0)


def _conv_fwd_block(x, halo, w, b):
    return b + _shift_down(x, halo, 2) * w[0:1, :] + _shift_down(x, halo, 1) * w[1:2, :] + x * w[2:3, :]


def _conv_specs(tr, s):
    pair = 2 * FF_HALF
    blk = pl.BlockSpec((tr, pair), lambda j, i: (i, j))
    prev = pl.BlockSpec((8, pair), lambda j, i: (jnp.maximum(i * (tr // 8) - 1, 0), j))
    nxt = pl.BlockSpec((8, pair), lambda j, i: (jnp.minimum((i + 1) * (tr // 8), s // 8 - 1), j))
    return blk, prev, nxt


def _conv_gate_fwd(hpre, conv_w, conv_b):
    s = hpre.shape[0]
    tr = min(CONV_TR, s)
    blk, prev, _ = _conv_specs(tr, s)

    def body(x_ref, halo_ref, w_ref, b_ref, a_ref):
        i = pl.program_id(1)
        halo = jnp.where(i > 0, halo_ref[...], 0.0)
        h = _conv_fwd_block(x_ref[...], halo, w_ref[...], b_ref[...])
        hg, hv = h[:, :FF_HALF], h[:, FF_HALF:]
        a_ref[...] = (hg * _sigmoid(hg) * hv).astype(BF16)

    return pl.pallas_call(
        body, name="conv_gate_fwd", grid=(2, s // tr),
        in_specs=[blk, prev, pl.BlockSpec((3, 2 * FF_HALF), lambda j, i: (0, j)),
                  pl.BlockSpec((1, 2 * FF_HALF), lambda j, i: (0, j))],
        out_specs=pl.BlockSpec((tr, FF_HALF), lambda j, i: (i, j)),
        out_shape=jax.ShapeDtypeStruct((s, D_FF), BF16),
        compiler_params=_params(("parallel", "parallel")),
    )(hpre, hpre, conv_w, conv_b)


def _conv_gate_bwd(hpre, da, conv_w, conv_b):
    s = hpre.shape[0]
    tr = min(CONV_TR, s)
    blk, prev, _ = _conv_specs(tr, s)

    def body(x_ref, halo_ref, da_ref, w_ref, b_ref, dh_ref, db_ref, dw_ref):
        i = pl.program_id(1)
        halo = jnp.where(i > 0, halo_ref[...], 0.0)
        x = x_ref[...]
        h = _conv_fwd_block(x, halo, w_ref[...], b_ref[...])
        hg, hv = h[:, :FF_HALF], h[:, FF_HALF:]
        da_blk = da_ref[...].astype(F32)
        sg = _sigmoid(hg)
        dhg = da_blk * hv * (sg * (1.0 + hg * (1.0 - sg)))
        dhv = da_blk * (hg * sg)
        dh_ref[:, :FF_HALF] = dhg.astype(BF16)
        dh_ref[:, FF_HALF:] = dhv.astype(BF16)
        x2, x1 = _shift_down(x, halo, 2), _shift_down(x, halo, 1)
        parts = []
        for lo, dpart in ((0, dhg), (FF_HALF, dhv)):
            cols = slice(lo, lo + FF_HALF)
            parts.append((cols, _colsum(dpart), _colsum(dpart * x2[:, cols]), _colsum(dpart * x1[:, cols]),
                          _colsum(dpart * x[:, cols])))

        @pl.when(i == 0)
        def _():
            for cols, db, dw0, dw1, dw2 in parts:
                db_ref[:, cols] = db
                dw_ref[0:1, cols] = dw0
                dw_ref[1:2, cols] = dw1
                dw_ref[2:3, cols] = dw2

        @pl.when(i > 0)
        def _():
            for cols, db, dw0, dw1, dw2 in parts:
                db_ref[:, cols] += db
                dw_ref[0:1, cols] += dw0
                dw_ref[1:2, cols] += dw1
                dw_ref[2:3, cols] += dw2

    pair = 2 * FF_HALF
    return pl.pallas_call(
        body, name="conv_gate_bwd", grid=(2, s // tr),
        in_specs=[blk, prev, pl.BlockSpec((tr, FF_HALF), lambda j, i: (i, j)),
                  pl.BlockSpec((3, pair), lambda j, i: (0, j)), pl.BlockSpec((1, pair), lambda j, i: (0, j))],
        out_specs=[blk, pl.BlockSpec((1, pair), lambda j, i: (0, j)), pl.BlockSpec((3, pair), lambda j, i: (0, j))],
        out_shape=[jax.ShapeDtypeStruct((s, 2 * D_FF), BF16), jax.ShapeDtypeStruct((1, 2 * D_FF), F32),
                   jax.ShapeDtypeStruct((3, 2 * D_FF), F32)],
        compiler_params=_params(("parallel", "arbitrary")),
    )(hpre, hpre, da, conv_w, conv_b)


def _conv_input_bwd(dh, conv_w):
    s = dh.shape[0]
    tr = min(CONV_TR, s)
    blk, _, _ = _conv_specs(tr, s)
    nblk = s // tr

    def body(x_ref, halo_ref, w_ref, o_ref):
        i = pl.program_id(1)
        halo = jnp.where(i < nblk - 1, halo_ref[...].astype(F32), 0.0)
        x, w = x_ref[...].astype(F32), w_ref[...]
        o_ref[...] = (x * w[2:3, :] + _shift_up(x, halo, 1) * w[1:2, :] + _shift_up(x, halo, 2) * w[0:1, :]).astype(BF16)

    nxt = pl.BlockSpec((16, 2 * FF_HALF), lambda j, i: (jnp.minimum((i + 1) * (tr // 16), s // 16 - 1), j))
    return pl.pallas_call(
        body, name="conv_input_bwd", grid=(2, nblk),
        in_specs=[blk, nxt, pl.BlockSpec((3, 2 * FF_HALF), lambda j, i: (0, j))], out_specs=blk,
        out_shape=jax.ShapeDtypeStruct((s, 2 * D_FF), BF16),
        compiler_params=_params(("parallel", "parallel")),
    )(dh, dh, conv_w)


def _adamw_math(w, g, m, v):
    m = ADAM_B1 * m + (1.0 - ADAM_B1) * g
    v = ADAM_B2 * v + (1.0 - ADAM_B2) * (g * g)
    m_hat = m / (1.0 - ADAM_B1 ** ADAM_STEP)
    v_hat = v / (1.0 - ADAM_B2 ** ADAM_STEP)
    delta = -ADAM_LR * (m_hat / (jnp.sqrt(v_hat) + ADAM_EPS) + ADAM_WD * w)
    return delta, m, v


def _adamw(name, g8, w, m, v):
    r, c = w.shape
    tr = _row_tile(r, c)

    def body(g_ref, w_ref, m_ref, v_ref, go_ref, d_ref, mo_ref, vo_ref):
        g = g_ref[0].astype(F32)
        for d in range(1, N_DEV):
            g = g + g_ref[d].astype(F32)
        delta, mn, vn = _adamw_math(w_ref[...], g, m_ref[...], v_ref[...])
        go_ref[...] = g
        d_ref[...] = delta
        mo_ref[...] = mn
        vo_ref[...] = vn

    spec = pl.BlockSpec((tr, c), lambda i: (i, 0))
    return pl.pallas_call(
        body, name=name, grid=(r // tr,),
        in_specs=[pl.BlockSpec((N_DEV, tr, c), lambda i: (0, i, 0)), spec, spec, spec], out_specs=[spec] * 4,
        out_shape=[jax.ShapeDtypeStruct((r, c), F32)] * 4, compiler_params=_params(("parallel",)),
    )(g8, w, m, v)


def _adamw_ada(c_t, dmod, w, m, v):
    r, c = w.shape
    tr = _row_tile(r, c)

    def body(ct_ref, dm_ref, w_ref, m_ref, v_ref, go_ref, d_ref, mo_ref, vo_ref):
        ct, dm = ct_ref[...], dm_ref[...]
        g = ct[:, 0:1] * dm[0:1, :]
        for b in range(1, N_DEV):
            g = g + ct[:, b:b + 1] * dm[b:b + 1, :]
        delta, mn, vn = _adamw_math(w_ref[...], g, m_ref[...], v_ref[...])
        go_ref[...] = g
        d_ref[...] = delta
        mo_ref[...] = mn
        vo_ref[...] = vn

    spec = pl.BlockSpec((tr, c), lambda i: (i, 0))
    return pl.pallas_call(
        body, name="adamw_w_ada", grid=(r // tr,),
        in_specs=[pl.BlockSpec((tr, N_DEV), lambda i: (i, 0)), pl.BlockSpec((N_DEV, c), lambda i: (0, 0)),
                  spec, spec, spec],
        out_specs=[spec] * 4, out_shape=[jax.ShapeDtypeStruct((r, c), F32)] * 4,
        compiler_params=_params(("parallel",)),
    )(c_t, dmod, w, m, v)


def _cols_from_slots(g):
    n, r, c = g.shape
    return jnp.transpose(g, (1, 0, 2)).reshape(r, n * c)


def _cols_to_slots(w):
    r, c = w.shape
    return jnp.transpose(w.reshape(r, N_DEV, c // N_DEV), (1, 0, 2))


def _pair_cols(w):
    g0, g1 = w[..., 0:FF_HALF], w[..., FF_HALF:D_FF]
    v0, v1 = w[..., D_FF:D_FF + FF_HALF], w[..., D_FF + FF_HALF:]
    return jnp.concatenate([g0, v0, g1, v1], axis=-1)


def _unpair_cols(w):
    g0, v0 = w[..., 0:FF_HALF], w[..., FF_HALF:D_FF]
    g1, v1 = w[..., D_FF:D_FF + FF_HALF], w[..., D_FF + FF_HALF:]
    return jnp.concatenate([g0, g1, v0, v1], axis=-1)


def _row(v):
    return v.reshape(1, -1)


def kernel(x, c, w_ada, b_ada, w_in, b_forget, w_fox_proj, w_sb_proj, w_o, ln1_g, ln1_b, w_up, conv_w, conv_b, w_down, ln2_g, ln2_b, loss_target, m_w_ada, m_b_ada, m_w_in, m_b_forget, m_w_fox_proj, m_w_sb_proj, m_w_o, m_ln1_g, m_ln1_b, m_w_up, m_conv_w, m_conv_b, m_w_down, m_ln2_g, m_ln2_b, v_w_ada, v_b_ada, v_w_in, v_b_forget, v_w_fox_proj, v_w_sb_proj, v_w_o, v_ln1_g, v_ln1_b, v_w_up, v_conv_w, v_conv_b, v_w_down, v_ln2_g, v_ln2_b):
    s = x.shape[1]
    me = 4 * lax.axis_index("x") + 2 * lax.axis_index("y") + lax.axis_index("c")
    x2 = x.reshape(s, D_MODEL)
    tgt = loss_target.reshape(s, D_MODEL)

    b_ada_loc = lax.dynamic_slice(b_ada, (me * ADA_SHARD,), (ADA_SHARD,)).reshape(1, ADA_SHARD)
    c_all, mod = _mod_exchange(c, w_ada, b_ada_loc)
    mod = mod.reshape(N_MOD, 1, D_MODEL)
    sh1, sc1, gt1, sh2, sc2, gt2 = [mod[i] for i in range(N_MOD)]

    g_in = _allgather_two_level("ag_w_in", w_in.astype(BF16))
    late_weights = _Ride([w_fox_proj.astype(BF16), w_sb_proj.astype(BF16), w_o.astype(BF16), w_up.astype(BF16),
                          w_down.astype(BF16), conv_w], scatter=False)
    w_in_f = _cols_from_slots(g_in)
    w_proj = jnp.concatenate(
        [w_in_f[:, 0:1536], w_in_f[:, 1544:3080], w_in_f[:, 3080:5128], w_in_f[:, 1536:1544],
         jnp.zeros((D_MODEL, W_PROJ - 5128), BF16)], axis=1)
    w_qkv, w_gates, w_f = w_proj[:, :W_QKV], w_proj[:, W_QKV:W_QKV + W_GATES], w_proj[:, W_QKV + W_GATES:W_QKV + W_GATES + W_F]
    conv_b_p = _pair_cols(_row(conv_b))
    b_f_pad = jnp.pad(_row(b_forget), ((0, 0), (0, LANES - N_HEADS)))

    (u1,) = _rowwise("modulate1", lambda xb, sc, sh: (xb * (1.0 + sc) + sh,),
                     [(x2, D_MODEL, 0)], [sc1, sh1], [(D_MODEL, BF16)], tr=512)
    qkv = _mm(u1, w_qkv, name="mm_qkv", out_dtype=BF16)
    gates = _mm(u1, w_gates, name="mm_gates")
    f_raw = _mm(u1, w_f, name="mm_forget")
    cum_col = _forget_cumsum(f_raw, b_f_pad)
    cum_row = jnp.transpose(cum_col[:, :N_HEADS]).reshape(N_HEADS // 2, 2, s)
    (y_fox, y_fox32, lse), (g_fox, g_sb, g_o, g_up, g_down, g_cw) = _fox_fwd(qkv, cum_col, cum_row, ride=late_weights)
    w_fox_f = _cols_from_slots(g_fox)
    w_sb_f = _cols_from_slots(g_sb)
    w_o_f = g_o.reshape(D_MODEL, D_MODEL)
    w_up_p = _pair_cols(_cols_from_slots(g_up))
    w_down_f = g_down.reshape(D_FF, D_MODEL)
    conv_w_p = _pair_cols(_cols_from_slots(g_cw))
    y_sb, sb_run = _sb_fwd(qkv)
    pf = _mm(y_fox, w_fox_f, name="mm_fox_proj", out_dtype=BF16)
    ps = _mm(y_sb, w_sb_f, name="mm_sb_proj", out_dtype=BF16)
    (merged,) = _rowwise("gate_merge", lambda ga, gb, a, b: (_sigmoid(ga) * a + _sigmoid(gb) * b,),
                         [(gates, D_MODEL, 0), (gates, D_MODEL, 1), (pf, D_MODEL, 0), (ps, D_MODEL, 0)], [],
                         [(D_MODEL, BF16)])
    attn_out = _mm(merged, w_o_f, name="mm_w_o")

    def ln_fwd(xb, fb, gt, g, b):
        xhat, _ = _ln_stats(ALPHA * xb + (1.0 + gt) * fb)
        return xhat * g + b

    def ln1_mod(xb, fb, gt, g, b, sc, sh):
        y = ln_fwd(xb, fb, gt, g, b)
        return y, y * (1.0 + sc) + sh

    x1, u2 = _rowwise("ln1_modulate2", ln1_mod, [(x2, D_MODEL, 0), (attn_out, D_MODEL, 0)],
                      [gt1, _row(ln1_g), _row(ln1_b), sc2, sh2], [(D_MODEL, F32), (D_MODEL, BF16)])

    hpre = _mm(u2, w_up_p, name="mm_w_up", tn=1408)
    act = _conv_gate_fwd(hpre, conv_w_p, conv_b_p)
    ffn_out = _mm(act, w_down_f, name="mm_w_down", tk=2816)

    def ln2_bwd(xb, fb, tb, gt, g, b):
        xhat, rstd = _ln_stats(ALPHA * xb + (1.0 + gt) * fb)
        err = (xhat * g + b) - tb
        dy = err * (1.0 / D_MODEL)
        dr = _ln_bwd(dy, xhat, rstd, g)
        return (dr * (1.0 + gt), ALPHA * dr,
                _colsum(err * err), _colsum(dy * xhat), _colsum(dy), _colsum(dr * fb))

    dffn, dx1_res, sq_err, d_ln2_g, d_ln2_b, d_gt2 = _rowwise(
        "ln2_bwd", ln2_bwd, [(x1, D_MODEL, 0), (ffn_out, D_MODEL, 0), (tgt, D_MODEL, 0)],
        [gt2, _row(ln2_g), _row(ln2_b)], [(D_MODEL, BF16), (D_MODEL, F32)], sums=[D_MODEL] * 4)
    loss = lax.psum(0.5 * jnp.sum(sq_err) / D_MODEL, ("x", "y", "c"))

    d_w_down = _mm(act, dffn, name="mm_d_w_down", ta=True, tm=1408, tk=2048, out_dtype=BF16)
    d_act = _mm(dffn, w_down_f, name="mm_d_act", tb=True, tn=1408, out_dtype=BF16)
    dh, d_conv_b_p, d_conv_w_p = _conv_gate_bwd(hpre, d_act, conv_w_p, conv_b_p)
    dhpre = _conv_input_bwd(dh, conv_w_p)
    d_w_up_p = _mm(u2, dhpre, name="mm_d_w_up", ta=True, tn=1408, tk=2048, out_dtype=BF16)
    du2 = _mm(dhpre, w_up_p, name="mm_d_u2", tb=True, tk=2816)

    def ln1_bwd(du, dres, x1b, xb, fb, sc, gt, g):
        dx1 = dres + du * (1.0 + sc)
        xhat, rstd = _ln_stats(ALPHA * xb + (1.0 + gt) * fb)
        dr = _ln_bwd(dx1, xhat, rstd, g)
        return (dr * (1.0 + gt), ALPHA * dr,
                _colsum(du * x1b), _colsum(du), _colsum(dx1 * xhat), _colsum(dx1), _colsum(dr * fb))

    d_attn, dx_res, d_sc2, d_sh2, d_ln1_g, d_ln1_b, d_gt1 = _rowwise(
        "ln1_bwd", ln1_bwd,
        [(du2, D_MODEL, 0), (dx1_res, D_MODEL, 0), (x1, D_MODEL, 0), (x2, D_MODEL, 0), (attn_out, D_MODEL, 0)],
        [sc2, gt1, _row(ln1_g)], [(D_MODEL, BF16), (D_MODEL, F32)], sums=[D_MODEL] * 5)

    d_w_o = _mm(merged, d_attn, name="mm_d_w_o", ta=True, out_dtype=BF16)
    d_merged = _mm(d_attn, w_o_f, name="mm_d_merged", tb=True, out_dtype=BF16)

    def merge_bwd(dm, ga, gb, a, b):
        dm, a, b = dm.astype(F32), a.astype(F32), b.astype(F32)
        sa, sb = _sigmoid(ga), _sigmoid(gb)
        return dm * a * sa * (1.0 - sa), dm * b * sb * (1.0 - sb), dm * sa, dm * sb

    d_ga, d_gb, d_pf, d_ps = _rowwise(
        "gate_merge_bwd", merge_bwd,
        [(d_merged, D_MODEL, 0), (gates, D_MODEL, 0), (gates, D_MODEL, 1), (pf, D_MODEL, 0), (ps, D_MODEL, 0)], [],
        [(D_MODEL, BF16)] * 4)
    d_w_fox = _mm(y_fox, d_pf, name="mm_d_w_fox", ta=True, out_dtype=BF16)
    d_w_sb = _mm(y_sb, d_ps, name="mm_d_w_sb", ta=True, out_dtype=BF16)
    d_y_fox = _mm(d_pf, w_fox_f, name="mm_d_y_fox", tb=True, out_dtype=BF16)
    d_y_sb = _mm(d_ps, w_sb_f, name="mm_d_y_sb", tb=True, out_dtype=BF16)
    early_grads = _Ride(
        [_cols_to_slots(d_w_fox), _cols_to_slots(d_w_sb), d_w_o.reshape(N_DEV, D_MODEL // N_DEV, D_MODEL),
         _cols_to_slots(_unpair_cols(d_w_up_p)), d_w_down.reshape(N_DEV, D_FF // N_DEV, D_MODEL)], scatter=True)
    (dq_a, dk_a, dv_a, d_cum_row), early_slots = _fox_bwd(qkv, cum_col, cum_row, y_fox32, lse, d_y_fox,
                                                                    ride=early_grads)
    dq_b, dk_b, dv_b = _sb_bwd(qkv, sb_run, d_y_sb)
    d_cum = jnp.transpose(d_cum_row.reshape(N_HEADS, s))
    d_cum = jnp.pad(d_cum, ((0, 0), (0, LANES - N_HEADS)))
    d_f, d_b_forget = _forget_bwd(d_cum, f_raw, b_f_pad)
    d_proj = jnp.concatenate([dq_a, dk_a, dv_a, dq_b, dk_b, dv_b, d_ga, d_gb, d_f,
                              jnp.zeros((s, W_PROJ - W_QKV - W_GATES - W_F), BF16)], axis=1)
    d_w_proj = _mm(u1, d_proj, name="mm_d_w_in", ta=True, tn=896, tk=2048, out_dtype=BF16)
    d_w_in_f = jnp.concatenate([d_w_proj[:, 0:1536], d_w_proj[:, 5120:5128], d_w_proj[:, 1536:3072],
                                d_w_proj[:, 3072:5120]], axis=1)
    du1, (in_slots,) = _mm(d_proj, w_proj, name="mm_d_u1", tb=True, tk=1792,
                           ride=_Ride([_cols_to_slots(d_w_in_f)], scatter=True))

    def x_bwd(du, dres, xb, sc):
        return dres + du * (1.0 + sc), _colsum(du * xb), _colsum(du)

    grad_x, d_sc1, d_sh1 = _rowwise("x_bwd", x_bwd, [(du1, D_MODEL, 0), (dx_res, D_MODEL, 0), (x2, D_MODEL, 0)],
                                    [sc1], [(D_MODEL, F32)], sums=[D_MODEL] * 2, tr=512)

    d_conv_b = _unpair_cols(d_conv_b_p)
    d_conv_w = _unpair_cols(d_conv_w_p)
    n_rep = N_MOD * D_MODEL + LANES + 4 * D_MODEL + 2 * D_FF
    small = jnp.concatenate(
        [d_sh1, d_sc1, d_gt1, d_sh2, d_sc2, d_gt2, d_b_forget, d_ln1_g, d_ln1_b, d_ln2_g, d_ln2_b, d_conv_b,
         d_conv_w.reshape(1, 6 * D_FF)], axis=1)
    n_small = small.shape[1] // LANES
    small = jnp.pad(small.reshape(n_small, LANES), ((0, 264 - n_small), (0, 0)))
    (small_all,) = _exchange("ag_small_grads", [small], scatter=False)
    rep8 = small_all[:, :n_rep // LANES, :]
    cw8 = small_all[:, n_rep // LANES:n_small, :].reshape(N_DEV, 3, 2 * D_FF)
    cw8 = lax.dynamic_slice(cw8, (0, 0, me * UP_SHARD), (N_DEV, 3, UP_SHARD))
    dmod8 = small_all[:, :N_MOD * D_MODEL // LANES, :].reshape(N_DEV, N_MOD * D_MODEL)
    dmod_loc = lax.dynamic_slice(dmod8, (0, me * ADA_SHARD), (N_DEV, ADA_SHARD))

    def pack_rep(b_a, b_f, g1, b1, g2, b2, cb):
        flat = jnp.concatenate([b_a, jnp.pad(b_f, (0, LANES - N_HEADS)), g1, b1, g2, b2, cb])
        return flat.reshape(n_rep // LANES, LANES)

    rep = _adamw("adamw_small", rep8, pack_rep(b_ada, b_forget, ln1_g, ln1_b, ln2_g, ln2_b, conv_b),
                 pack_rep(m_b_ada, m_b_forget, m_ln1_g, m_ln1_b, m_ln2_g, m_ln2_b, m_conv_b),
                 pack_rep(v_b_ada, v_b_forget, v_ln1_g, v_ln1_b, v_ln2_g, v_ln2_b, v_conv_b))

    def unpack_rep(p):
        flat = p.reshape(-1)
        o = N_MOD * D_MODEL
        return {"b_ada": flat[:o], "b_forget": flat[o:o + N_HEADS],
                "ln1_g": flat[o + 128:o + 1152], "ln1_b": flat[o + 1152:o + 2176],
                "ln2_g": flat[o + 2176:o + 3200], "ln2_b": flat[o + 3200:o + 4224], "conv_b": flat[o + 4224:]}

    rep = [unpack_rep(p) for p in rep]
    r_conv_w = _adamw("adamw_conv_w", cw8, conv_w, m_conv_w, v_conv_w)
    r_ada = _adamw_ada(jnp.transpose(c_all.reshape(N_DEV, D_MODEL)), dmod_loc, w_ada, m_w_ada, v_w_ada)

    r_in = _adamw("adamw_w_in", in_slots, w_in, m_w_in, v_w_in)
    r_fox = _adamw("adamw_w_fox", early_slots[0], w_fox_proj, m_w_fox_proj, v_w_fox_proj)
    r_sb = _adamw("adamw_w_sb", early_slots[1], w_sb_proj, m_w_sb_proj, v_w_sb_proj)
    r_o = _adamw("adamw_w_o", early_slots[2], w_o, m_w_o, v_w_o)
    r_up = _adamw("adamw_w_up", early_slots[3], w_up, m_w_up, v_w_up)
    r_down = _adamw("adamw_w_down", early_slots[4], w_down, m_w_down, v_w_down)

    def leaf(i):
        return [r_ada[i], rep[i]["b_ada"], r_in[i], rep[i]["b_forget"], r_fox[i], r_sb[i], r_o[i], rep[i]["ln1_g"],
                rep[i]["ln1_b"], r_up[i], r_conv_w[i], rep[i]["conv_b"], r_down[i], rep[i]["ln2_g"], rep[i]["ln2_b"]]

    return (loss, grad_x.reshape(1, s, D_MODEL), *leaf(0), *leaf(1), *leaf(2), *leaf(3))
```

```python
import functools

import jax
import jax.numpy as jnp
from jax import lax
from jax.experimental import pallas as pl
from jax.experimental.pallas import tpu as pltpu

F32 = jnp.float32
BF16 = jnp.bfloat16
MESH = pl.DeviceIdType.MESH
ANY = pl.BlockSpec(memory_space=pl.ANY)

N_DEV = 8
D_MODEL = 1024
HEAD_DIM = 64
N_HEADS = 8
ATTN_W = N_HEADS * HEAD_DIM
D_FF = 2816
FF_HALF = D_FF // 2
N_MOD = 6
ADA_SHARD = N_MOD * D_MODEL // N_DEV
IN_SHARD = 641
UP_SHARD = 704
ATTN_SCALE = HEAD_DIM ** -0.5
ALPHA = 2.0 ** 0.25
LN_EPS = 1e-5
LANES = 128
TQ = 512
SCAN_W = 256
VMEM_LIMIT = 56 * 1024 * 1024

ADAM_LR, ADAM_B1, ADAM_B2, ADAM_EPS, ADAM_WD, ADAM_STEP = 0.001, 0.9, 0.999, 1e-08, 0.01, 10

W_QKV, W_GATES, W_F = 3072, 2048, 128
W_PROJ = 5376


def _params(sem=None):
    return pltpu.CompilerParams(dimension_semantics=sem, vmem_limit_bytes=VMEM_LIMIT)


def _tile(n, cap):
    if n <= cap:
        return n
    best = None
    for t in range(LANES, cap + 1, LANES):
        if n % t == 0:
            best = t
    assert best is not None, (n, cap)
    return best


def _row_tile(r, width, budget=192 * 1024):
    if r * width <= budget or r % 16:
        return r
    best = 16
    for t in range(16, r + 1, 16):
        if r % t == 0 and t * width <= budget:
            best = t
    return best


def _me():
    x, y, c = lax.axis_index("x"), lax.axis_index("y"), lax.axis_index("c")
    return x, y, c, 4 * x + 2 * y + c


def _peer(r):
    x, y, c, _ = _me()
    px = 1 - x if r & 4 else x
    py = 1 - y if r & 2 else y
    pc = 1 - c if r & 1 else c
    return (px, py, pc), 4 * px + 2 * py + pc


class _Ride:
    def __init__(self, arrays, scatter):
        self.arrays, self.scatter, self.n = list(arrays), scatter, len(arrays)
        self.in_specs = [ANY] * self.n
        self.out_specs = [ANY] * self.n
        self.out_shape = [jax.ShapeDtypeStruct(a.shape if scatter else (N_DEV,) + a.shape, a.dtype) for a in arrays]
        self.scratch = [pltpu.SemaphoreType.DMA((self.n, N_DEV - 1)), pltpu.SemaphoreType.DMA((self.n, N_DEV - 1)),
                        pltpu.SemaphoreType.DMA((self.n,))]

    def _local(self, ins, outs, sems, a):
        me = _me()[3]
        return pltpu.make_async_copy(ins[a].at[me] if self.scatter else ins[a], outs[a].at[me], sems[2].at[a])

    def _remote(self, ins, outs, sems, a, r, arriving):
        me = _me()[3]
        peer, pidx = _peer(r)
        src = ins[a].at[me if arriving else pidx] if self.scatter else ins[a]
        return pltpu.make_async_remote_copy(
            src_ref=src, dst_ref=outs[a].at[pidx if arriving else me], send_sem=sems[0].at[a, r - 1],
            recv_sem=sems[1].at[a, r - 1], device_id=peer, device_id_type=MESH)

    def start(self, ins, outs, sems):
        for a in range(self.n):
            self._local(ins, outs, sems, a).start()
        for r in range(1, N_DEV):
            for a in range(self.n):
                self._remote(ins, outs, sems, a, r, False).start()

    def wait(self, ins, outs, sems):
        for r in range(1, N_DEV):
            for a in range(self.n):
                self._remote(ins, outs, sems, a, r, True).wait_recv()
        for r in range(1, N_DEV):
            for a in range(self.n):
                self._remote(ins, outs, sems, a, r, False).wait_send()
        for a in range(self.n):
            self._local(ins, outs, sems, a).wait()


def _exchange(name, arrays, scatter):
    ride = _Ride(arrays, scatter)

    def body(*refs):
        ins, outs, sems = refs[:ride.n], refs[ride.n:2 * ride.n], refs[2 * ride.n:]
        ride.start(ins, outs, sems)
        ride.wait(ins, outs, sems)

    return pl.pallas_call(body, name=name, in_specs=ride.in_specs, out_specs=ride.out_specs, out_shape=ride.out_shape,
                          scratch_shapes=ride.scratch)(*arrays)


def _allgather_two_level(name, a):
    def body(a_ref, out_ref, send_sems, recv_sems, local_sem):
        x, y, c, me = _me()
        sibling = (x, y, 1 - c)
        chips = [(1 - x, y), (x, 1 - y), (1 - x, 1 - y)]

        def idx(px, py, pc):
            return 4 * px + 2 * py + pc

        def copy(k, block, to, src=None):
            slot = out_ref.at[idx(*block)]
            return pltpu.make_async_remote_copy(
                src_ref=slot if src is None else src, dst_ref=slot, send_sem=send_sems.at[k], recv_sem=recv_sems.at[k],
                device_id=to, device_id_type=MESH)

        mine = pltpu.make_async_copy(a_ref, out_ref.at[me], local_sem)
        mine.start()
        first = [copy(0, (x, y, c), sibling, src=a_ref)]
        first += [copy(1 + j, (x, y, c), (*chip, c), src=a_ref) for j, chip in enumerate(chips)]
        for cp in first:
            cp.start()
        passed = [copy(4 + j, (*chip, c), sibling) for j, chip in enumerate(chips)]
        for j, chip in enumerate(chips):
            copy(1 + j, (*chip, c), (x, y, c)).wait_recv()
            passed[j].start()
        copy(0, sibling, (x, y, c)).wait_recv()
        for j, chip in enumerate(chips):
            copy(4 + j, (*chip, 1 - c), (x, y, c)).wait_recv()
        for cp in first + passed:
            cp.wait_send()
        mine.wait()

    return pl.pallas_call(
        body, name=name, in_specs=[ANY], out_specs=ANY,
        out_shape=jax.ShapeDtypeStruct((N_DEV,) + a.shape, a.dtype),
        scratch_shapes=[pltpu.SemaphoreType.DMA((N_DEV - 1,)), pltpu.SemaphoreType.DMA((N_DEV - 1,)),
                        pltpu.SemaphoreType.DMA],
    )(a)


def _with_ride(body, ride, n_in, n_out, grid):
    if ride is None:
        return body
    n = ride.n

    def wrapped(*refs):
        ins, rins = refs[:n_in], refs[n_in:n_in + n]
        outs, routs = refs[n_in + n:n_in + n + n_out], refs[n_in + n + n_out:n_in + 2 * n + n_out]
        rest = refs[n_in + 2 * n + n_out:]
        scratch, sems = rest[:len(rest) - 3], rest[len(rest) - 3:]
        ids = [pl.program_id(d) for d in range(len(grid))]
        first = functools.reduce(lambda p, q: p & q, [i == 0 for i in ids])
        last = functools.reduce(lambda p, q: p & q, [i == g - 1 for i, g in zip(ids, grid)])

        @pl.when(first)
        def _():
            ride.start(rins, routs, sems)

        body(*ins, *outs, *scratch)

        @pl.when(last)
        def _():
            ride.wait(rins, routs, sems)

    return wrapped


def _ride_call(body, ride, *, name, grid, in_specs, out_specs, out_shape, scratch_shapes, sem, args):
    n_in, n_out = len(in_specs), len(out_specs)
    if ride is None:
        res = pl.pallas_call(body, name=name, grid=grid, in_specs=in_specs, out_specs=out_specs, out_shape=out_shape,
                             scratch_shapes=scratch_shapes, compiler_params=_params(sem))(*args)
        return list(res), []
    res = pl.pallas_call(
        _with_ride(body, ride, n_in, n_out, grid), name=name, grid=grid,
        in_specs=list(in_specs) + ride.in_specs, out_specs=list(out_specs) + ride.out_specs,
        out_shape=list(out_shape) + ride.out_shape, scratch_shapes=list(scratch_shapes) + ride.scratch,
        compiler_params=_params(("arbitrary",) * len(grid)))(*args, *ride.arrays)
    return list(res[:n_out]), list(res[n_out:])


def _mm(a, b, *, name, ta=False, tb=False, out_dtype=F32, tm=1024, tn=1024, tk=1024, ride=None):
    m, k = (a.shape[1], a.shape[0]) if ta else a.shape
    n = b.shape[0] if tb else b.shape[1]
    assert (b.shape[1] if tb else b.shape[0]) == k
    tm, tn, tk = _tile(m, tm), _tile(n, tn), _tile(k, tk)
    nk = k // tk
    a_spec = pl.BlockSpec((tk, tm), lambda i, j, l: (l, i)) if ta else pl.BlockSpec((tm, tk), lambda i, j, l: (i, l))
    b_spec = pl.BlockSpec((tn, tk), lambda i, j, l: (j, l)) if tb else pl.BlockSpec((tk, tn), lambda i, j, l: (l, j))
    dims = (((0,) if ta else (1,), (1,) if tb else (0,)), ((), ()))

    def body(a_ref, b_ref, o_ref, *acc):
        p = lax.dot_general(a_ref[...].astype(BF16), b_ref[...].astype(BF16), dims, preferred_element_type=F32)
        if nk == 1:
            o_ref[...] = p.astype(out_dtype)
            return
        acc_ref = acc[0]
        step = pl.program_id(2)

        @pl.when(step == 0)
        def _():
            acc_ref[...] = p

        @pl.when(step > 0)
        def _():
            acc_ref[...] += p

        @pl.when(step == nk - 1)
        def _():
            o_ref[...] = acc_ref[...].astype(out_dtype)

    outs, rode = _ride_call(
        body, ride, name=name, grid=(m // tm, n // tn, nk), in_specs=[a_spec, b_spec],
        out_specs=[pl.BlockSpec((tm, tn), lambda i, j, l: (i, j))], out_shape=[jax.ShapeDtypeStruct((m, n), out_dtype)],
        scratch_shapes=[] if nk == 1 else [pltpu.VMEM((tm, tn), F32)], sem=("parallel", "parallel", "arbitrary"),
        args=(a, b))
    return outs[0] if ride is None else (outs[0], rode)


def _rowwise(name, fn, rows, vecs, outs, sums=(), tr=512):
    s = rows[0][0].shape[0]
    tr = min(tr, s)
    nr, nv, no = len(rows), len(vecs), len(outs)

    def body(*refs):
        vals = [r[...] for r in refs[:nr + nv]]
        res = fn(*vals)
        for o_ref, val in zip(refs[nr + nv:nr + nv + no], res[:no]):
            o_ref[...] = val.astype(o_ref.dtype)
        step = pl.program_id(0)
        for s_ref, val in zip(refs[nr + nv + no:], res[no:]):
            @pl.when(step == 0)
            def _(s_ref=s_ref, val=val):
                s_ref[...] = val

            @pl.when(step > 0)
            def _(s_ref=s_ref, val=val):
                s_ref[...] += val

    in_specs = [pl.BlockSpec((tr, w), functools.partial(lambda i, cb: (i, cb), cb=cb)) for _, w, cb in rows]
    in_specs += [pl.BlockSpec(v.shape, lambda i: (0, 0)) for v in vecs]
    out_specs = [pl.BlockSpec((tr, w), lambda i: (i, 0)) for w, _ in outs]
    out_specs += [pl.BlockSpec((1, w), lambda i: (0, 0)) for w in sums]
    out_shape = [jax.ShapeDtypeStruct((s, w), dt) for w, dt in outs]
    out_shape += [jax.ShapeDtypeStruct((1, w), F32) for w in sums]
    return pl.pallas_call(
        body, name=name, grid=(s // tr,), in_specs=in_specs, out_specs=out_specs, out_shape=out_shape,
        compiler_params=_params(("arbitrary",) if sums else ("parallel",)),
    )(*[r[0] for r in rows], *vecs)


def _colsum(x):
    return jnp.sum(x, axis=0, keepdims=True)


def _sigmoid(x):
    return 1.0 / (1.0 + jnp.exp(-x))


def _log_sigmoid(x):
    return jnp.minimum(x, 0.0) - jnp.log(1.0 + jnp.exp(-jnp.abs(x)))


def _ln_stats(r):
    mu = jnp.mean(r, axis=-1, keepdims=True)
    xc = r - mu
    var = jnp.mean(xc * xc, axis=-1, keepdims=True)
    rstd = lax.rsqrt(var + LN_EPS)
    return xc * rstd, rstd


def _ln_bwd(dy, xhat, rstd, g):
    dxh = dy * g
    m1 = jnp.mean(dxh, axis=-1, keepdims=True)
    m2 = jnp.mean(dxh * xhat, axis=-1, keepdims=True)
    return rstd * (dxh - m1 - xhat * m2)


def _mod_exchange(c_row, w_ada, b_ada_loc):
    def body(c_ref, w_ref, b_ref, call_ref, mod_ref, piece_ref, send_sems, recv_sems):
        me = _me()[3]
        call_ref[me] = c_ref[...]
        sent = []
        for r in range(1, N_DEV):
            peer, _ = _peer(r)
            cp = pltpu.make_async_remote_copy(
                src_ref=c_ref, dst_ref=call_ref.at[me], send_sem=send_sems.at[0, r - 1],
                recv_sem=recv_sems.at[0, r - 1], device_id=peer, device_id_type=MESH)
            cp.start()
            sent.append(cp)
        for r in range(1, N_DEV):
            peer, pidx = _peer(r)
            pltpu.make_async_remote_copy(
                src_ref=c_ref, dst_ref=call_ref.at[pidx], send_sem=send_sems.at[0, r - 1],
                recv_sem=recv_sems.at[0, r - 1], device_id=peer, device_id_type=MESH).wait_recv()
        c_all = jnp.concatenate([call_ref[d] for d in range(N_DEV)], axis=0)
        mod_loc = jnp.dot(c_all, w_ref[...], preferred_element_type=F32,
                          precision=lax.Precision.HIGHEST) + b_ref[...]
        for d in range(N_DEV):
            piece_ref[d] = mod_loc[d:d + 1, :]
        mod_ref[me] = piece_ref[me]
        for r in range(1, N_DEV):
            peer, pidx = _peer(r)
            cp = pltpu.make_async_remote_copy(
                src_ref=piece_ref.at[pidx], dst_ref=mod_ref.at[me], send_sem=send_sems.at[1, r - 1],
                recv_sem=recv_sems.at[1, r - 1], device_id=peer, device_id_type=MESH)
            cp.start()
            sent.append(cp)
        for r in range(1, N_DEV):
            peer, pidx = _peer(r)
            pltpu.make_async_remote_copy(
                src_ref=piece_ref.at[me], dst_ref=mod_ref.at[pidx], send_sem=send_sems.at[1, r - 1],
                recv_sem=recv_sems.at[1, r - 1], device_id=peer, device_id_type=MESH).wait_recv()
        for cp in sent:
            cp.wait_send()

    vmem = pl.BlockSpec(memory_space=pltpu.VMEM)
    return pl.pallas_call(
        body, name="mod_exchange", in_specs=[vmem, vmem, vmem], out_specs=[vmem, vmem],
        out_shape=[jax.ShapeDtypeStruct((N_DEV, 1, D_MODEL), F32), jax.ShapeDtypeStruct((N_DEV, 1, ADA_SHARD), F32)],
        scratch_shapes=[pltpu.VMEM((N_DEV, 1, ADA_SHARD), F32),
                        pltpu.SemaphoreType.DMA((2, N_DEV - 1)), pltpu.SemaphoreType.DMA((2, N_DEV - 1))],
        compiler_params=_params(),
    )(c_row, w_ada, b_ada_loc)


def _split3(x):
    hi = x.astype(BF16)
    r1 = x - hi.astype(F32)
    mid = r1.astype(BF16)
    lo = (r1 - mid.astype(F32)).astype(BF16)
    return hi, mid, lo


def _scan_rows(x_ref, o_ref, s, reverse, pre=None, post=None):
    tb = min(TQ, s)
    nb = s // tb
    row = lax.broadcasted_iota(jnp.int32, (tb, tb), 0)
    col = lax.broadcasted_iota(jnp.int32, (tb, tb), 1)
    tri = jnp.where((col >= row) if reverse else (col <= row), 1.0, 0.0).astype(BF16)

    def step(i, carry):
        blk = (nb - 1 - i) if reverse else i
        off = pl.multiple_of(blk * tb, tb)
        x = x_ref[pl.ds(off, tb), :]
        if pre is not None:
            x = pre(x, off)
        acc = carry
        for piece in _split3(x):
            acc = acc + jnp.dot(tri, piece, preferred_element_type=F32)
        o_ref[pl.ds(off, tb), :] = acc if post is None else post(acc, off)
        edge = acc[0:1, :] if reverse else acc[tb - 1:tb, :]
        return jnp.broadcast_to(edge, (tb, LANES))

    lax.fori_loop(0, nb, step, jnp.zeros((tb, LANES), F32))


def _forget_cumsum(f_raw, b_pad):
    s = f_raw.shape[0]

    def body(f_ref, b_ref, cum_ref):
        b = b_ref[...]
        _scan_rows(f_ref, cum_ref, s, False, pre=lambda x, off: _log_sigmoid(x + b))

    vmem = pl.BlockSpec(memory_space=pltpu.VMEM)
    return pl.pallas_call(body, name="forget_cumsum", in_specs=[vmem, vmem], out_specs=vmem,
                          out_shape=jax.ShapeDtypeStruct((s, LANES), F32), compiler_params=_params())(f_raw, b_pad)


def _forget_bwd(dcum, f_raw, b_pad):
    s = f_raw.shape[0]

    def body(d_ref, f_ref, b_ref, df_ref, db_ref, tmp_ref):
        b = b_ref[...]
        _scan_rows(d_ref, tmp_ref, s, True)
        df = tmp_ref[...] * _sigmoid(-(f_ref[...] + b))
        df_ref[...] = df.astype(BF16)
        db_ref[...] = _colsum(df)

    vmem = pl.BlockSpec(memory_space=pltpu.VMEM)
    return pl.pallas_call(
        body, name="forget_bwd", in_specs=[vmem, vmem, vmem], out_specs=[vmem, vmem],
        out_shape=[jax.ShapeDtypeStruct((s, LANES), BF16), jax.ShapeDtypeStruct((1, LANES), F32)],
        scratch_shapes=[pltpu.VMEM((s, LANES), F32)], compiler_params=_params())(dcum, f_raw, b_pad)


def _dot_nt(a, b):
    return lax.dot_general(a, b, (((1,), (1,)), ((), ())), preferred_element_type=F32)


def _head_masks():
    lane = lax.broadcasted_iota(jnp.int32, (TQ, LANES), 1)
    return lane, [lane < HEAD_DIM, lane >= HEAD_DIM]


def _pick(mask, x):
    return jnp.where(mask, x, jnp.zeros_like(x))


def _qkv_specs(s, col0):
    nb = ATTN_W // LANES
    return [pl.BlockSpec((TQ, LANES), lambda hp, qi: (qi, col0 + hp)),
            pl.BlockSpec((s, LANES), lambda hp, qi: (0, col0 + nb + hp)),
            pl.BlockSpec((s, LANES), lambda hp, qi: (0, col0 + 2 * nb + hp))]


def _pair_spec():
    return pl.BlockSpec((TQ, LANES), lambda hp, qi: (qi, hp))


def _fox_fwd(qkv, cum_col, cum_row, ride=None):
    s = qkv.shape[0]
    nq = s // TQ

    def body(q_ref, k_ref, v_ref, cc_ref, cr_ref, o_ref, o32_ref, lse_ref):
        hp, qi = pl.program_id(0), pl.program_id(1)
        lane, masks = _head_masks()
        row = lax.broadcasted_iota(jnp.int32, (TQ, TQ), 0)
        col = lax.broadcasted_iota(jnp.int32, (TQ, TQ), 1)
        causal = col <= row
        q2 = q_ref[...] * jnp.asarray(ATTN_SCALE, BF16)
        cc = cc_ref[...]
        qms = [_pick(masks[e], q2) for e in range(2)]
        cqs = [jnp.sum(jnp.where(lane == 2 * hp + e, cc, 0.0), axis=1, keepdims=True) for e in range(2)]

        def tile(kb, carry, masked, width=1):
            off, span = pl.multiple_of(kb * TQ, TQ), width * TQ
            k2, v2 = k_ref[pl.ds(off, span), :], v_ref[pl.ds(off, span), :]
            head0 = lax.broadcasted_iota(jnp.int32, (span, LANES), 1) < HEAD_DIM
            new = []
            for e in range(2):
                m, acc = carry[e]
                sc = _dot_nt(qms[e], k2) + (cqs[e] - cr_ref[e:e + 1, pl.ds(off, span)])
                if masked:
                    sc = jnp.where(causal, sc, -jnp.inf)
                m_new = jnp.maximum(m, jnp.max(sc, axis=1, keepdims=True))
                p = jnp.exp(sc - m_new)
                v_ones = jnp.where(head0 if e == 0 else ~head0, v2, jnp.ones_like(v2))
                acc = jnp.exp(m - m_new) * acc + jnp.dot(p.astype(BF16), v_ones, preferred_element_type=F32)
                new.append((m_new, acc))
            return tuple(new)

        init = (jnp.full((TQ, 1), -jnp.inf, F32), jnp.zeros((TQ, LANES), F32))
        quads = qi // 4
        carry = lax.fori_loop(0, quads, lambda j, cr: tile(4 * j, cr, False, 4), tile(qi, (init, init), True))
        carry = lax.cond(qi % 4 >= 2, lambda cr: tile(4 * quads, cr, False, 2), lambda cr: cr, carry)
        carry = lax.cond(qi % 2 == 1, lambda cr: tile(qi - 1, cr, False), lambda cr: cr, carry)
        sums = [jnp.max(jnp.where(masks[1 - e], carry[e][1], 0.0), axis=1, keepdims=True) for e in range(2)]
        outs = [carry[e][1] / sums[e] for e in range(2)]
        lses = [carry[e][0] + jnp.log(sums[e]) for e in range(2)]
        out = jnp.where(masks[0], outs[0], outs[1])
        o_ref[...] = out.astype(BF16)
        o32_ref[...] = out
        lse_ref[...] = jnp.where(masks[0], lses[0], lses[1])

    return _ride_call(
        body, ride, name="fox_fwd", grid=(N_HEADS // 2, nq),
        in_specs=_qkv_specs(s, 0) + [pl.BlockSpec((TQ, LANES), lambda hp, qi: (qi, 0)),
                                     pl.BlockSpec((None, 2, s), lambda hp, qi: (hp, 0, 0))],
        out_specs=[_pair_spec(), _pair_spec(), _pair_spec()],
        out_shape=[jax.ShapeDtypeStruct((s, ATTN_W), BF16), jax.ShapeDtypeStruct((s, ATTN_W), F32),
                   jax.ShapeDtypeStruct((s, ATTN_W), F32)],
        scratch_shapes=[], sem=("parallel", "parallel"), args=(qkv, qkv, qkv, cum_col, cum_row))


def _write_transposed(acc_ref, out_ref):
    for c in range(out_ref.shape[0] // TQ):
        out_ref[c * TQ:(c + 1) * TQ, :] = jnp.transpose(acc_ref[:, c * TQ:(c + 1) * TQ]).astype(BF16)


def _fox_bwd(qkv, cum_col, cum_row, o, lse, do, ride=None):
    s = qkv.shape[0]
    nq = s // TQ

    def body(q_ref, k_ref, v_ref, cc_ref, cr_ref, o_ref, lse_ref, do_ref,
             dq_ref, dk_ref, dv_ref, dcr_ref, dk_acc, dv_acc):
        hp, qi = pl.program_id(0), pl.program_id(1)

        @pl.when(qi == 0)
        def _():
            dk_acc[...] = jnp.zeros_like(dk_acc)
            dv_acc[...] = jnp.zeros_like(dv_acc)
            dcr_ref[...] = jnp.zeros_like(dcr_ref)

        lane, masks = _head_masks()
        row = lax.broadcasted_iota(jnp.int32, (TQ, TQ), 0)
        col = lax.broadcasted_iota(jnp.int32, (TQ, TQ), 1)
        causal = col <= row
        q2 = q_ref[...] * jnp.asarray(ATTN_SCALE, BF16)
        do2 = do_ref[...]
        prod = do2.astype(F32) * o_ref[...].astype(F32)
        lse2 = lse_ref[...]
        cc = cc_ref[...]
        qms = [_pick(masks[e], q2) for e in range(2)]
        doms = [_pick(masks[e], do2) for e in range(2)]
        deltas = [jnp.sum(jnp.where(masks[e], prod, 0.0), axis=1, keepdims=True) for e in range(2)]
        lses = [jnp.max(jnp.where(masks[e], lse2, -jnp.inf), axis=1, keepdims=True) for e in range(2)]
        cqs = [jnp.sum(jnp.where(lane == 2 * hp + e, cc, 0.0), axis=1, keepdims=True) for e in range(2)]
        qts = [jnp.transpose(qms[e].astype(F32)).astype(BF16) for e in range(2)]
        dots = [jnp.transpose(doms[e].astype(F32)).astype(BF16) for e in range(2)]
        ones_row = [HEAD_DIM * (1 - e) for e in range(2)]
        trow = lax.broadcasted_iota(jnp.int32, (LANES, TQ), 0)
        qts = [jnp.where(trow == ones_row[e], jnp.ones_like(qts[e]), qts[e]) for e in range(2)]

        def tile(kb, carry, masked, width=1):
            off, span = pl.multiple_of(kb * TQ, TQ), width * TQ
            k2, v2 = k_ref[pl.ds(off, span), :], v_ref[pl.ds(off, span), :]
            head0 = lax.broadcasted_iota(jnp.int32, (span, LANES), 1) < HEAD_DIM
            new, dks, dv = [], [], None
            for e in range(2):
                dq = carry[e]
                sc = _dot_nt(qms[e], k2) + (cqs[e] - cr_ref[e:e + 1, pl.ds(off, span)])
                p = jnp.exp(sc - lses[e])
                if masked:
                    p = jnp.where(causal, p, 0.0)
                ds = p * (_dot_nt(doms[e], v2) - deltas[e])
                dsb = ds.astype(BF16)
                dk_e = jnp.dot(qts[e], dsb, preferred_element_type=F32)
                dv_e = jnp.dot(dots[e], p.astype(BF16), preferred_element_type=F32)
                dks.append(dk_e)
                dv = dv_e if e == 0 else dv + dv_e
                dcr_ref[e:e + 1, pl.ds(off, span)] -= dk_e[ones_row[e]:ones_row[e] + 1, :]
                k_ones = jnp.where(head0 if e == 0 else ~head0, k2, jnp.ones_like(k2))
                new.append(dq + jnp.dot(dsb, k_ones, preferred_element_type=F32))
            krow = lax.broadcasted_iota(jnp.int32, (LANES, span), 0)
            dk_acc[:, pl.ds(off, span)] += jnp.where(krow < HEAD_DIM, dks[0], dks[1])
            dv_acc[:, pl.ds(off, span)] += dv
            return tuple(new)

        init = jnp.zeros((TQ, LANES), F32)
        carry = lax.fori_loop(0, qi // 2, lambda j, cr: tile(2 * j, cr, False, 2), (init, init))
        carry = lax.cond(qi % 2 == 1, lambda cr: tile(qi - 1, cr, False), lambda cr: cr, carry)
        carry = tile(qi, carry, True)
        dq_ref[...] = (jnp.where(masks[0], carry[0], carry[1]) * ATTN_SCALE).astype(BF16)
        for e in range(2):
            dcr_ref[e:e + 1, pl.ds(pl.multiple_of(qi * TQ, TQ), TQ)] += jnp.transpose(carry[e])[
                ones_row[e]:ones_row[e] + 1, :]

        @pl.when(qi == nq - 1)
        def _():
            _write_transposed(dk_acc, dk_ref)
            _write_transposed(dv_acc, dv_ref)

    seq_spec = pl.BlockSpec((s, LANES), lambda hp, qi: (0, hp))
    return _ride_call(
        body, ride, name="fox_bwd", grid=(N_HEADS // 2, nq),
        in_specs=_qkv_specs(s, 0) + [pl.BlockSpec((TQ, LANES), lambda hp, qi: (qi, 0)),
                                     pl.BlockSpec((None, 2, s), lambda hp, qi: (hp, 0, 0)),
                                     _pair_spec(), _pair_spec(), _pair_spec()],
        out_specs=[_pair_spec(), seq_spec, seq_spec, pl.BlockSpec((None, 2, s), lambda hp, qi: (hp, 0, 0))],
        out_shape=[jax.ShapeDtypeStruct((s, ATTN_W), BF16)] * 3 + [jax.ShapeDtypeStruct((N_HEADS // 2, 2, s), F32)],
        scratch_shapes=[pltpu.VMEM((LANES, s), F32), pltpu.VMEM((LANES, s), F32)],
        sem=("parallel", "arbitrary"), args=(qkv, qkv, qkv, cum_col, cum_row, o, lse, do))


def _scan_matrix(reverse):
    row = lax.broadcasted_iota(jnp.int32, (SCAN_W, SCAN_W), 0)
    col = lax.broadcasted_iota(jnp.int32, (SCAN_W, SCAN_W), 1)
    return jnp.where((row > col) if reverse else (row < col), 1.0, 0.0).astype(BF16)


def _scan_cols(x, tri, reverse, init):
    nblk = x.shape[1] // SCAN_W
    parts, total = [None] * nblk, init
    far = 0 if reverse else SCAN_W - 1
    for b in (reversed(range(nblk)) if reverse else range(nblk)):
        blk = x[:, b * SCAN_W:(b + 1) * SCAN_W]
        part = jnp.dot(blk.astype(BF16), tri, preferred_element_type=F32)
        parts[b] = part + total
        total = total + (part[:, far:far + 1] + blk[:, far:far + 1])
    return (parts[0] if nblk == 1 else jnp.concatenate(parts, axis=1)), total


def _sb_logits(qm, k2):
    z = _dot_nt(qm, k2)
    neg_abs = lax.bitcast_convert_type(lax.bitcast_convert_type(z, jnp.uint32) | jnp.uint32(0x80000000), F32)
    soft = jnp.log(1.0 + jnp.exp(neg_abs))
    lb = jnp.minimum(z, 0.0) - soft
    return lb, lb - z


TILE_SLOTS = 4


def _tri_base(qi):
    return (qi * (qi + 1)) // 2


def _sb_fwd(qkv):
    s = qkv.shape[0]
    nq = s // TQ

    def body(q_ref, k_ref, v_ref, o_ref, t_ref, buf, sems):
        hp, qi = pl.program_id(0), pl.program_id(1)
        _, masks = _head_masks()
        row = lax.broadcasted_iota(jnp.int32, (TQ, TQ), 0)
        col = lax.broadcasted_iota(jnp.int32, (TQ, TQ), 1)
        strict = col < row
        suffix = _scan_matrix(True)
        q2 = q_ref[...] * jnp.asarray(ATTN_SCALE, BF16)
        qms = [_pick(masks[e], q2) for e in range(2)]
        base = _tri_base(qi)

        def store(e, kb):
            slot = kb % TILE_SLOTS
            return pltpu.make_async_copy(buf.at[e, slot], t_ref.at[2 * hp + e, base + kb], sems.at[e, slot])

        def tile(kb, carry, masked, width=1):
            off, span = pl.multiple_of(kb * TQ, TQ), width * TQ
            k2, v2 = k_ref[pl.ds(off, span), :], v_ref[pl.ds(off, span), :]
            new = []
            for e in range(2):
                run, acc = carry[e]
                lb, lo = _sb_logits(qms[e], k2)
                if masked:
                    lo = jnp.where(strict, lo, 0.0)
                rest, run = _scan_cols(lo, suffix, True, run)
                a = jnp.exp(lb + rest)
                if masked:
                    a = jnp.where(strict, a, 0.0)
                ab, lbb = a.astype(BF16), lb.astype(BF16)
                acc = acc + jnp.dot(ab, v2, preferred_element_type=F32)
                for w in range(width):
                    blk = kb + w

                    @pl.when(blk + TILE_SLOTS <= qi)
                    def _(e=e, blk=blk):
                        store(e, blk + TILE_SLOTS).wait()
                    buf[e, blk % TILE_SLOTS, 0] = ab[:, w * TQ:(w + 1) * TQ]
                    buf[e, blk % TILE_SLOTS, 1] = lbb[:, w * TQ:(w + 1) * TQ]
                    store(e, blk).start()
                new.append((run, acc))
            return tuple(new)

        init = (jnp.zeros((TQ, 1), F32), jnp.zeros((TQ, LANES), F32))
        carry = tile(qi, (init, init), True)
        carry = lax.cond(qi % 2 == 1, lambda cr: tile(qi - 1, cr, False), lambda cr: cr, carry)
        pairs = qi // 2
        carry = lax.fori_loop(0, pairs, lambda it, cr: tile(2 * (pairs - 1 - it), cr, False, 2), carry)
        for e in range(2):
            for blk in range(TILE_SLOTS):
                @pl.when(qi >= blk)
                def _(e=e, blk=blk):
                    store(e, blk).wait()
        o_ref[...] = jnp.where(masks[0], carry[0][1], carry[1][1]).astype(BF16)

    ntri = nq * (nq + 1) // 2
    return pl.pallas_call(
        body, name="sb_fwd", grid=(N_HEADS // 2, nq), in_specs=_qkv_specs(s, 3 * ATTN_W // LANES),
        out_specs=[_pair_spec(), ANY],
        out_shape=[jax.ShapeDtypeStruct((s, ATTN_W), BF16), jax.ShapeDtypeStruct((N_HEADS, ntri, 2, TQ, TQ), BF16)],
        scratch_shapes=[pltpu.VMEM((2, TILE_SLOTS, 2, TQ, TQ), BF16), pltpu.SemaphoreType.DMA((2, TILE_SLOTS))],
        compiler_params=_params(("arbitrary", "arbitrary")),
    )(qkv, qkv, qkv)


def _sb_bwd(qkv, tiles, do):
    s = qkv.shape[0]
    nq = s // TQ

    def body(q_ref, k_ref, v_ref, t_ref, do_ref, dq_ref, dk_ref, dv_ref, dk_acc, dv_acc, buf, sems):
        hp, qi = pl.program_id(0), pl.program_id(1)

        @pl.when(qi == 0)
        def _():
            dk_acc[...] = jnp.zeros_like(dk_acc)
            dv_acc[...] = jnp.zeros_like(dv_acc)

        _, masks = _head_masks()
        row = lax.broadcasted_iota(jnp.int32, (TQ, TQ), 0)
        col = lax.broadcasted_iota(jnp.int32, (TQ, TQ), 1)
        strict = col < row
        prefix = _scan_matrix(False)
        q2 = q_ref[...] * jnp.asarray(ATTN_SCALE, BF16)
        do2 = do_ref[...]
        qms = [_pick(masks[e], q2) for e in range(2)]
        doms = [_pick(masks[e], do2) for e in range(2)]
        qts = [jnp.transpose(qms[e].astype(F32)).astype(BF16) for e in range(2)]
        dots = [jnp.transpose(doms[e].astype(F32)).astype(BF16) for e in range(2)]
        base = _tri_base(qi)

        def fetch(e, kb):
            slot = kb % TILE_SLOTS
            return pltpu.make_async_copy(t_ref.at[2 * hp + e, base + kb], buf.at[e, slot], sems.at[e, slot])

        for e in range(2):
            fetch(e, 0).start()

            @pl.when(qi >= 1)
            def _(e=e):
                fetch(e, 1).start()

        def tile(kb, carry, masked, width=1):
            off, span = pl.multiple_of(kb * TQ, TQ), width * TQ
            k2, v2 = k_ref[pl.ds(off, span), :], v_ref[pl.ds(off, span), :]
            new, dk, dv = [], None, None
            for e in range(2):
                gsum, dq = carry[e]
                if not masked:
                    for blk in range(2, 2 + width):
                        @pl.when(kb + blk <= qi)
                        def _(e=e, blk=blk):
                            fetch(e, kb + blk).start()
                for w in range(width):
                    fetch(e, kb + w).wait()
                slots = [(kb + w) % TILE_SLOTS for w in range(width)]
                ab = buf[e, slots[0], 0] if width == 1 else jnp.concatenate([buf[e, sl, 0] for sl in slots], axis=1)
                lbb = buf[e, slots[0], 1] if width == 1 else jnp.concatenate([buf[e, sl, 1] for sl in slots], axis=1)
                beta = jnp.exp(lbb.astype(F32))
                g = ab.astype(F32) * _dot_nt(doms[e], v2)
                before, gsum = _scan_cols(g, prefix, False, gsum)
                dz = g - beta * (g + before)
                if masked:
                    dz = jnp.where(strict, dz, 0.0)
                dzb = dz.astype(BF16)
                dk_e = jnp.dot(qts[e], dzb, preferred_element_type=F32)
                dv_e = jnp.dot(dots[e], ab, preferred_element_type=F32)
                dk, dv = (dk_e, dv_e) if e == 0 else (dk + dk_e, dv + dv_e)
                new.append((gsum, dq + jnp.dot(dzb, k2, preferred_element_type=F32)))
            dk_acc[:, pl.ds(off, span)] += dk
            dv_acc[:, pl.ds(off, span)] += dv
            return tuple(new)

        init = (jnp.zeros((TQ, 1), F32), jnp.zeros((TQ, LANES), F32))
        carry = lax.fori_loop(0, qi // 2, lambda j, cr: tile(2 * j, cr, False, 2), (init, init))
        carry = lax.cond(qi % 2 == 1, lambda cr: tile(qi - 1, cr, False), lambda cr: cr, carry)
        carry = tile(qi, carry, True)
        dq_ref[...] = (jnp.where(masks[0], carry[0][1], carry[1][1]) * ATTN_SCALE).astype(BF16)

        @pl.when(qi == nq - 1)
        def _():
            _write_transposed(dk_acc, dk_ref)
            _write_transposed(dv_acc, dv_ref)

    seq_spec = pl.BlockSpec((s, LANES), lambda hp, qi: (0, hp))
    return pl.pallas_call(
        body, name="sb_bwd", grid=(N_HEADS // 2, nq),
        in_specs=_qkv_specs(s, 3 * ATTN_W // LANES) + [ANY, _pair_spec()],
        out_specs=[_pair_spec(), seq_spec, seq_spec],
        out_shape=[jax.ShapeDtypeStruct((s, ATTN_W), BF16)] * 3,
        scratch_shapes=[pltpu.VMEM((LANES, s), F32), pltpu.VMEM((LANES, s), F32),
                        pltpu.VMEM((2, TILE_SLOTS, 2, TQ, TQ), BF16), pltpu.SemaphoreType.DMA((2, TILE_SLOTS))],
        compiler_params=_params(("arbitrary", "arbitrary")),
    )(qkv, qkv, qkv, tiles, do)


CONV_TR = 256


def _shift_down(x, halo, n):
    rolled = pltpu.roll(x, n, 0)
    head = rolled[0:8, :]
    rid = lax.broadcasted_iota(jnp.int32, head.shape, 0)
    for j in range(n):
        head = jnp.where(rid == j, halo[8 - n + j:8 - n + j + 1, :], head)
    return jnp.concatenate([head, rolled[8:, :]], axis=0)


def _shift_up(x, halo, n):
    rows = x.shape[0]
    rolled = pltpu.roll(x, rows - n, 0)
    tail = rolled[rows - 8:, :]
    rid = lax.broadcasted_iota(jnp.int32, tail.shape, 0)
    for j in range(n):
        tail = jnp.where(rid == 8 - n + j, halo[j:j + 1, :], tail)
    return jnp.concatenate([rolled[:rows - 8, :], tail], axis=0)


def _conv_fwd_block(x, halo, w, b):
    return b + _shift_down(x, halo, 2) * w[0:1, :] + _shift_down(x, halo, 1) * w[1:2, :] + x * w[2:3, :]


def _conv_specs(tr, s):
    pair = 2 * FF_HALF
    blk = pl.BlockSpec((tr, pair), lambda j, i: (i, j))
    prev = pl.BlockSpec((8, pair), lambda j, i: (jnp.maximum(i * (tr // 8) - 1, 0), j))
    nxt = pl.BlockSpec((8, pair), lambda j, i: (jnp.minimum((i + 1) * (tr // 8), s // 8 - 1), j))
    return blk, prev, nxt


def _conv_gate_fwd(hpre, conv_w, conv_b):
    s = hpre.shape[0]
    tr = min(CONV_TR, s)
    blk, prev, _ = _conv_specs(tr, s)

    def body(x_ref, halo_ref, w_ref, b_ref, a_ref):
        i = pl.program_id(1)
        halo = jnp.where(i > 0, halo_ref[...], 0.0)
        h = _conv_fwd_block(x_ref[...], halo, w_ref[...], b_ref[...])
        hg, hv = h[:, :FF_HALF], h[:, FF_HALF:]
        a_ref[...] = (hg * _sigmoid(hg) * hv).astype(BF16)

    return pl.pallas_call(
        body, name="conv_gate_fwd", grid=(2, s // tr),
        in_specs=[blk, prev, pl.BlockSpec((3, 2 * FF_HALF), lambda j, i: (0, j)),
                  pl.BlockSpec((1, 2 * FF_HALF), lambda j, i: (0, j))],
        out_specs=pl.BlockSpec((tr, FF_HALF), lambda j, i: (i, j)),
        out_shape=jax.ShapeDtypeStruct((s, D_FF), BF16),
        compiler_params=_params(("parallel", "parallel")),
    )(hpre, hpre, conv_w, conv_b)


def _conv_gate_bwd(hpre, da, conv_w, conv_b):
    s = hpre.shape[0]
    tr = min(CONV_TR, s)
    blk, prev, _ = _conv_specs(tr, s)

    def body(x_ref, halo_ref, da_ref, w_ref, b_ref, dh_ref, db_ref, dw_ref):
        i = pl.program_id(1)
        halo = jnp.where(i > 0, halo_ref[...], 0.0)
        x = x_ref[...]
        h = _conv_fwd_block(x, halo, w_ref[...], b_ref[...])
        hg, hv = h[:, :FF_HALF], h[:, FF_HALF:]
        da_blk = da_ref[...].astype(F32)
        sg = _sigmoid(hg)
        dhg = da_blk * hv * (sg * (1.0 + hg * (1.0 - sg)))
        dhv = da_blk * (hg * sg)
        dh_ref[:, :FF_HALF] = dhg.astype(BF16)
        dh_ref[:, FF_HALF:] = dhv.astype(BF16)
        x2, x1 = _shift_down(x, halo, 2), _shift_down(x, halo, 1)
        parts = []
        for lo, dpart in ((0, dhg), (FF_HALF, dhv)):
            cols = slice(lo, lo + FF_HALF)
            parts.append((cols, _colsum(dpart), _colsum(dpart * x2[:, cols]), _colsum(dpart * x1[:, cols]),
                          _colsum(dpart * x[:, cols])))

        @pl.when(i == 0)
        def _():
            for cols, db, dw0, dw1, dw2 in parts:
                db_ref[:, cols] = db
                dw_ref[0:1, cols] = dw0
                dw_ref[1:2, cols] = dw1
                dw_ref[2:3, cols] = dw2

        @pl.when(i > 0)
        def _():
            for cols, db, dw0, dw1, dw2 in parts:
                db_ref[:, cols] += db
                dw_ref[0:1, cols] += dw0
                dw_ref[1:2, cols] += dw1
                dw_ref[2:3, cols] += dw2

    pair = 2 * FF_HALF
    return pl.pallas_call(
        body, name="conv_gate_bwd", grid=(2, s // tr),
        in_specs=[blk, prev, pl.BlockSpec((tr, FF_HALF), lambda j, i: (i, j)),
                  pl.BlockSpec((3, pair), lambda j, i: (0, j)), pl.BlockSpec((1, pair), lambda j, i: (0, j))],
        out_specs=[blk, pl.BlockSpec((1, pair), lambda j, i: (0, j)), pl.BlockSpec((3, pair), lambda j, i: (0, j))],
        out_shape=[jax.ShapeDtypeStruct((s, 2 * D_FF), BF16), jax.ShapeDtypeStruct((1, 2 * D_FF), F32),
                   jax.ShapeDtypeStruct((3, 2 * D_FF), F32)],
        compiler_params=_params(("parallel", "arbitrary")),
    )(hpre, hpre, da, conv_w, conv_b)


def _conv_input_bwd(dh, conv_w):
    s = dh.shape[0]
    tr = min(CONV_TR, s)
    blk, _, _ = _conv_specs(tr, s)
    nblk = s // tr

    def body(x_ref, halo_ref, w_ref, o_ref):
        i = pl.program_id(1)
        halo = jnp.where(i < nblk - 1, halo_ref[...].astype(F32), 0.0)
        x, w = x_ref[...].astype(F32), w_ref[...]
        o_ref[...] = (x * w[2:3, :] + _shift_up(x, halo, 1) * w[1:2, :] + _shift_up(x, halo, 2) * w[0:1, :]).astype(BF16)

    nxt = pl.BlockSpec((16, 2 * FF_HALF), lambda j, i: (jnp.minimum((i + 1) * (tr // 16), s // 16 - 1), j))
    return pl.pallas_call(
        body, name="conv_input_bwd", grid=(2, nblk),
        in_specs=[blk, nxt, pl.BlockSpec((3, 2 * FF_HALF), lambda j, i: (0, j))], out_specs=blk,
        out_shape=jax.ShapeDtypeStruct((s, 2 * D_FF), BF16),
        compiler_params=_params(("parallel", "parallel")),
    )(dh, dh, conv_w)


def _adamw_math(w, g, m, v):
    m = ADAM_B1 * m + (1.0 - ADAM_B1) * g
    v = ADAM_B2 * v + (1.0 - ADAM_B2) * (g * g)
    m_hat = m / (1.0 - ADAM_B1 ** ADAM_STEP)
    v_hat = v / (1.0 - ADAM_B2 ** ADAM_STEP)
    delta = -ADAM_LR * (m_hat / (jnp.sqrt(v_hat) + ADAM_EPS) + ADAM_WD * w)
    return delta, m, v


def _adamw(name, g8, w, m, v):
    r, c = w.shape
    tr = _row_tile(r, c)

    def body(g_ref, w_ref, m_ref, v_ref, go_ref, d_ref, mo_ref, vo_ref):
        g = g_ref[0].astype(F32)
        for d in range(1, N_DEV):
            g = g + g_ref[d].astype(F32)
        delta, mn, vn = _adamw_math(w_ref[...], g, m_ref[...], v_ref[...])
        go_ref[...] = g
        d_ref[...] = delta
        mo_ref[...] = mn
        vo_ref[...] = vn

    spec = pl.BlockSpec((tr, c), lambda i: (i, 0))
    return pl.pallas_call(
        body, name=name, grid=(r // tr,),
        in_specs=[pl.BlockSpec((N_DEV, tr, c), lambda i: (0, i, 0)), spec, spec, spec], out_specs=[spec] * 4,
        out_shape=[jax.ShapeDtypeStruct((r, c), F32)] * 4, compiler_params=_params(("parallel",)),
    )(g8, w, m, v)


def _adamw_ada(c_t, dmod, w, m, v):
    r, c = w.shape
    tr = _row_tile(r, c)

    def body(ct_ref, dm_ref, w_ref, m_ref, v_ref, go_ref, d_ref, mo_ref, vo_ref):
        ct, dm = ct_ref[...], dm_ref[...]
        g = ct[:, 0:1] * dm[0:1, :]
        for b in range(1, N_DEV):
            g = g + ct[:, b:b + 1] * dm[b:b + 1, :]
        delta, mn, vn = _adamw_math(w_ref[...], g, m_ref[...], v_ref[...])
        go_ref[...] = g
        d_ref[...] = delta
        mo_ref[...] = mn
        vo_ref[...] = vn

    spec = pl.BlockSpec((tr, c), lambda i: (i, 0))
    return pl.pallas_call(
        body, name="adamw_w_ada", grid=(r // tr,),
        in_specs=[pl.BlockSpec((tr, N_DEV), lambda i: (i, 0)), pl.BlockSpec((N_DEV, c), lambda i: (0, 0)),
                  spec, spec, spec],
        out_specs=[spec] * 4, out_shape=[jax.ShapeDtypeStruct((r, c), F32)] * 4,
        compiler_params=_params(("parallel",)),
    )(c_t, dmod, w, m, v)


def _cols_from_slots(g):
    n, r, c = g.shape
    return jnp.transpose(g, (1, 0, 2)).reshape(r, n * c)


def _cols_to_slots(w):
    r, c = w.shape
    return jnp.transpose(w.reshape(r, N_DEV, c // N_DEV), (1, 0, 2))


def _pair_cols(w):
    g0, g1 = w[..., 0:FF_HALF], w[..., FF_HALF:D_FF]
    v0, v1 = w[..., D_FF:D_FF + FF_HALF], w[..., D_FF + FF_HALF:]
    return jnp.concatenate([g0, v0, g1, v1], axis=-1)


def _unpair_cols(w):
    g0, v0 = w[..., 0:FF_HALF], w[..., FF_HALF:D_FF]
    g1, v1 = w[..., D_FF:D_FF + FF_HALF], w[..., D_FF + FF_HALF:]
    return jnp.concatenate([g0, g1, v0, v1], axis=-1)


def _row(v):
    return v.reshape(1, -1)


def kernel(x, c, w_ada, b_ada, w_in, b_forget, w_fox_proj, w_sb_proj, w_o, ln1_g, ln1_b, w_up, conv_w, conv_b, w_down, ln2_g, ln2_b, loss_target, m_w_ada, m_b_ada, m_w_in, m_b_forget, m_w_fox_proj, m_w_sb_proj, m_w_o, m_ln1_g, m_ln1_b, m_w_up, m_conv_w, m_conv_b, m_w_down, m_ln2_g, m_ln2_b, v_w_ada, v_b_ada, v_w_in, v_b_forget, v_w_fox_proj, v_w_sb_proj, v_w_o, v_ln1_g, v_ln1_b, v_w_up, v_conv_w, v_conv_b, v_w_down, v_ln2_g, v_ln2_b):
    s = x.shape[1]
    me = 4 * lax.axis_index("x") + 2 * lax.axis_index("y") + lax.axis_index("c")
    x2 = x.reshape(s, D_MODEL)
    tgt = loss_target.reshape(s, D_MODEL)

    b_ada_loc = lax.dynamic_slice(b_ada, (me * ADA_SHARD,), (ADA_SHARD,)).reshape(1, ADA_SHARD)
    c_all, mod = _mod_exchange(c, w_ada, b_ada_loc)
    mod = mod.reshape(N_MOD, 1, D_MODEL)
    sh1, sc1, gt1, sh2, sc2, gt2 = [mod[i] for i in range(N_MOD)]

    g_in = _allgather_two_level("ag_w_in", w_in.astype(BF16))
    late_weights = _Ride([w_fox_proj.astype(BF16), w_sb_proj.astype(BF16), w_o.astype(BF16), w_up.astype(BF16),
                          w_down.astype(BF16), conv_w], scatter=False)
    w_in_f = _cols_from_slots(g_in)
    w_proj = jnp.concatenate(
        [w_in_f[:, 0:1536], w_in_f[:, 1544:3080], w_in_f[:, 3080:5128], w_in_f[:, 1536:1544],
         jnp.zeros((D_MODEL, W_PROJ - 5128), BF16)], axis=1)
    w_qkv, w_gates, w_f = w_proj[:, :W_QKV], w_proj[:, W_QKV:W_QKV + W_GATES], w_proj[:, W_QKV + W_GATES:W_QKV + W_GATES + W_F]
    conv_b_p = _pair_cols(_row(conv_b))
    b_f_pad = jnp.pad(_row(b_forget), ((0, 0), (0, LANES - N_HEADS)))

    (u1,) = _rowwise("modulate1", lambda xb, sc, sh: (xb * (1.0 + sc) + sh,),
                     [(x2, D_MODEL, 0)], [sc1, sh1], [(D_MODEL, BF16)], tr=512)
    qkv = _mm(u1, w_qkv, name="mm_qkv", out_dtype=BF16)
    gates = _mm(u1, w_gates, name="mm_gates")
    f_raw = _mm(u1, w_f, name="mm_forget")
    cum_col = _forget_cumsum(f_raw, b_f_pad)
    cum_row = jnp.transpose(cum_col[:, :N_HEADS]).reshape(N_HEADS // 2, 2, s)
    (y_fox, y_fox32, lse), (g_fox, g_sb, g_o, g_up, g_down, g_cw) = _fox_fwd(qkv, cum_col, cum_row, ride=late_weights)
    w_fox_f = _cols_from_slots(g_fox)
    w_sb_f = _cols_from_slots(g_sb)
    w_o_f = g_o.reshape(D_MODEL, D_MODEL)
    w_up_p = _pair_cols(_cols_from_slots(g_up))
    w_down_f = g_down.reshape(D_FF, D_MODEL)
    conv_w_p = _pair_cols(_cols_from_slots(g_cw))
    y_sb, sb_run = _sb_fwd(qkv)
    pf = _mm(y_fox, w_fox_f, name="mm_fox_proj", out_dtype=BF16)
    ps = _mm(y_sb, w_sb_f, name="mm_sb_proj", out_dtype=BF16)
    (merged,) = _rowwise("gate_merge", lambda ga, gb, a, b: (_sigmoid(ga) * a + _sigmoid(gb) * b,),
                         [(gates, D_MODEL, 0), (gates, D_MODEL, 1), (pf, D_MODEL, 0), (ps, D_MODEL, 0)], [],
                         [(D_MODEL, BF16)])
    attn_out = _mm(merged, w_o_f, name="mm_w_o")

    def ln_fwd(xb, fb, gt, g, b):
        xhat, _ = _ln_stats(ALPHA * xb + (1.0 + gt) * fb)
        return xhat * g + b

    def ln1_mod(xb, fb, gt, g, b, sc, sh):
        y = ln_fwd(xb, fb, gt, g, b)
        return y, y * (1.0 + sc) + sh

    x1, u2 = _rowwise("ln1_modulate2", ln1_mod, [(x2, D_MODEL, 0), (attn_out, D_MODEL, 0)],
                      [gt1, _row(ln1_g), _row(ln1_b), sc2, sh2], [(D_MODEL, F32), (D_MODEL, BF16)])

    hpre = _mm(u2, w_up_p, name="mm_w_up", tn=1408)
    act = _conv_gate_fwd(hpre, conv_w_p, conv_b_p)
    ffn_out = _mm(act, w_down_f, name="mm_w_down", tk=2816)

    def ln2_bwd(xb, fb, tb, gt, g, b):
        xhat, rstd = _ln_stats(ALPHA * xb + (1.0 + gt) * fb)
        err = (xhat * g + b) - tb
        dy = err * (1.0 / D_MODEL)
        dr = _ln_bwd(dy, xhat, rstd, g)
        return (dr * (1.0 + gt), ALPHA * dr,
                _colsum(err * err), _colsum(dy * xhat), _colsum(dy), _colsum(dr * fb))

    dffn, dx1_res, sq_err, d_ln2_g, d_ln2_b, d_gt2 = _rowwise(
        "ln2_bwd", ln2_bwd, [(x1, D_MODEL, 0), (ffn_out, D_MODEL, 0), (tgt, D_MODEL, 0)],
        [gt2, _row(ln2_g), _row(ln2_b)], [(D_MODEL, BF16), (D_MODEL, F32)], sums=[D_MODEL] * 4)
    loss = lax.psum(0.5 * jnp.sum(sq_err) / D_MODEL, ("x", "y", "c"))

    d_w_down = _mm(act, dffn, name="mm_d_w_down", ta=True, tm=1408, tk=2048, out_dtype=BF16)
    d_act = _mm(dffn, w_down_f, name="mm_d_act", tb=True, tn=1408, out_dtype=BF16)
    dh, d_conv_b_p, d_conv_w_p = _conv_gate_bwd(hpre, d_act, conv_w_p, conv_b_p)
    dhpre = _conv_input_bwd(dh, conv_w_p)
    d_w_up_p = _mm(u2, dhpre, name="mm_d_w_up", ta=True, tn=1408, tk=2048, out_dtype=BF16)
    du2 = _mm(dhpre, w_up_p, name="mm_d_u2", tb=True, tk=2816)

    def ln1_bwd(du, dres, x1b, xb, fb, sc, gt, g):
        dx1 = dres + du * (1.0 + sc)
        xhat, rstd = _ln_stats(ALPHA * xb + (1.0 + gt) * fb)
        dr = _ln_bwd(dx1, xhat, rstd, g)
        return (dr * (1.0 + gt), ALPHA * dr,
                _colsum(du * x1b), _colsum(du), _colsum(dx1 * xhat), _colsum(dx1), _colsum(dr * fb))

    d_attn, dx_res, d_sc2, d_sh2, d_ln1_g, d_ln1_b, d_gt1 = _rowwise(
        "ln1_bwd", ln1_bwd,
        [(du2, D_MODEL, 0), (dx1_res, D_MODEL, 0), (x1, D_MODEL, 0), (x2, D_MODEL, 0), (attn_out, D_MODEL, 0)],
        [sc2, gt1, _row(ln1_g)], [(D_MODEL, BF16), (D_MODEL, F32)], sums=[D_MODEL] * 5)

    d_w_o = _mm(merged, d_attn, name="mm_d_w_o", ta=True, out_dtype=BF16)
    d_merged = _mm(d_attn, w_o_f, name="mm_d_merged", tb=True, out_dtype=BF16)

    def merge_bwd(dm, ga, gb, a, b):
        dm, a, b = dm.astype(F32), a.astype(F32), b.astype(F32)
        sa, sb = _sigmoid(ga), _sigmoid(gb)
        return dm * a * sa * (1.0 - sa), dm * b * sb * (1.0 - sb), dm * sa, dm * sb

    d_ga, d_gb, d_pf, d_ps = _rowwise(
        "gate_merge_bwd", merge_bwd,
        [(d_merged, D_MODEL, 0), (gates, D_MODEL, 0), (gates, D_MODEL, 1), (pf, D_MODEL, 0), (ps, D_MODEL, 0)], [],
        [(D_MODEL, BF16)] * 4)
    d_w_fox = _mm(y_fox, d_pf, name="mm_d_w_fox", ta=True, out_dtype=BF16)
    d_w_sb = _mm(y_sb, d_ps, name="mm_d_w_sb", ta=True, out_dtype=BF16)
    d_y_fox = _mm(d_pf, w_fox_f, name="mm_d_y_fox", tb=True, out_dtype=BF16)
    d_y_sb = _mm(d_ps, w_sb_f, name="mm_d_y_sb", tb=True, out_dtype=BF16)
    early_grads = _Ride(
        [_cols_to_slots(d_w_fox), _cols_to_slots(d_w_sb), d_w_o.reshape(N_DEV, D_MODEL // N_DEV, D_MODEL),
         _cols_to_slots(_unpair_cols(d_w_up_p)), d_w_down.reshape(N_DEV, D_FF // N_DEV, D_MODEL)], scatter=True)
    (dq_a, dk_a, dv_a, d_cum_row), early_slots = _fox_bwd(qkv, cum_col, cum_row, y_fox32, lse, d_y_fox,
                                                                    ride=early_grads)
    dq_b, dk_b, dv_b = _sb_bwd(qkv, sb_run, d_y_sb)
    d_cum = jnp.transpose(d_cum_row.reshape(N_HEADS, s))
    d_cum = jnp.pad(d_cum, ((0, 0), (0, LANES - N_HEADS)))
    d_f, d_b_forget = _forget_bwd(d_cum, f_raw, b_f_pad)
    d_proj = jnp.concatenate([dq_a, dk_a, dv_a, dq_b, dk_b, dv_b, d_ga, d_gb, d_f,
                              jnp.zeros((s, W_PROJ - W_QKV - W_GATES - W_F), BF16)], axis=1)
    d_w_proj = _mm(u1, d_proj, name="mm_d_w_in", ta=True, tn=896, tk=2048, out_dtype=BF16)
    d_w_in_f = jnp.concatenate([d_w_proj[:, 0:1536], d_w_proj[:, 5120:5128], d_w_proj[:, 1536:3072],
                                d_w_proj[:, 3072:5120]], axis=1)
    du1, (in_slots,) = _mm(d_proj, w_proj, name="mm_d_u1", tb=True, tk=2688,
                           ride=_Ride([_cols_to_slots(d_w_in_f)], scatter=True))

    def x_bwd(du, dres, xb, sc):
        return dres + du * (1.0 + sc), _colsum(du * xb), _colsum(du)

    grad_x, d_sc1, d_sh1 = _rowwise("x_bwd", x_bwd, [(du1, D_MODEL, 0), (dx_res, D_MODEL, 0), (x2, D_MODEL, 0)],
                                    [sc1], [(D_MODEL, F32)], sums=[D_MODEL] * 2, tr=512)

    d_conv_b = _unpair_cols(d_conv_b_p)
    d_conv_w = _unpair_cols(d_conv_w_p)
    n_rep = N_MOD * D_MODEL + LANES + 4 * D_MODEL + 2 * D_FF
    small = jnp.concatenate(
        [d_sh1, d_sc1, d_gt1, d_sh2, d_sc2, d_gt2, d_b_forget, d_ln1_g, d_ln1_b, d_ln2_g, d_ln2_b, d_conv_b,
         d_conv_w.reshape(1, 6 * D_FF)], axis=1)
    n_small = small.shape[1] // LANES
    small = jnp.pad(small.reshape(n_small, LANES), ((0, 264 - n_small), (0, 0)))
    (small_all,) = _exchange("ag_small_grads", [small], scatter=False)
    rep8 = small_all[:, :n_rep // LANES, :]
    cw8 = small_all[:, n_rep // LANES:n_small, :].reshape(N_DEV, 3, 2 * D_FF)
    cw8 = lax.dynamic_slice(cw8, (0, 0, me * UP_SHARD), (N_DEV, 3, UP_SHARD))
    dmod8 = small_all[:, :N_MOD * D_MODEL // LANES, :].reshape(N_DEV, N_MOD * D_MODEL)
    dmod_loc = lax.dynamic_slice(dmod8, (0, me * ADA_SHARD), (N_DEV, ADA_SHARD))

    def pack_rep(b_a, b_f, g1, b1, g2, b2, cb):
        flat = jnp.concatenate([b_a, jnp.pad(b_f, (0, LANES - N_HEADS)), g1, b1, g2, b2, cb])
        return flat.reshape(n_rep // LANES, LANES)

    rep = _adamw("adamw_small", rep8, pack_rep(b_ada, b_forget, ln1_g, ln1_b, ln2_g, ln2_b, conv_b),
                 pack_rep(m_b_ada, m_b_forget, m_ln1_g, m_ln1_b, m_ln2_g, m_ln2_b, m_conv_b),
                 pack_rep(v_b_ada, v_b_forget, v_ln1_g, v_ln1_b, v_ln2_g, v_ln2_b, v_conv_b))

    def unpack_rep(p):
        flat = p.reshape(-1)
        o = N_MOD * D_MODEL
        return {"b_ada": flat[:o], "b_forget": flat[o:o + N_HEADS],
                "ln1_g": flat[o + 128:o + 1152], "ln1_b": flat[o + 1152:o + 2176],
                "ln2_g": flat[o + 2176:o + 3200], "ln2_b": flat[o + 3200:o + 4224], "conv_b": flat[o + 4224:]}

    rep = [unpack_rep(p) for p in rep]
    r_conv_w = _adamw("adamw_conv_w", cw8, conv_w, m_conv_w, v_conv_w)
    r_ada = _adamw_ada(jnp.transpose(c_all.reshape(N_DEV, D_MODEL)), dmod_loc, w_ada, m_w_ada, v_w_ada)

    r_in = _adamw("adamw_w_in", in_slots, w_in, m_w_in, v_w_in)
    r_fox = _adamw("adamw_w_fox", early_slots[0], w_fox_proj, m_w_fox_proj, v_w_fox_proj)
    r_sb = _adamw("adamw_w_sb", early_slots[1], w_sb_proj, m_w_sb_proj, v_w_sb_proj)
    r_o = _adamw("adamw_w_o", early_slots[2], w_o, m_w_o, v_w_o)
    r_up = _adamw("adamw_w_up", early_slots[3], w_up, m_w_up, v_w_up)
    r_down = _adamw("adamw_w_down", early_slots[4], w_down, m_w_down, v_w_down)

    def leaf(i):
        return [r_ada[i], rep[i]["b_ada"], r_in[i], rep[i]["b_forget"], r_fox[i], r_sb[i], r_o[i], rep[i]["ln1_g"],
                rep[i]["ln1_b"], r_up[i], r_conv_w[i], rep[i]["conv_b"], r_down[i], rep[i]["ln2_g"], rep[i]["ln2_b"]]

    return (loss, grad_x.reshape(1, s, D_MODEL), *leaf(0), *leaf(1), *leaf(2), *leaf(3))
```

```python
import functools

import jax
import jax.numpy as jnp
from jax import lax
from jax.experimental import pallas as pl
from jax.experimental.pallas import tpu as pltpu

F32 = jnp.float32
BF16 = jnp.bfloat16
MESH = pl.DeviceIdType.MESH
ANY = pl.BlockSpec(memory_space=pl.ANY)

N_DEV = 8
D_MODEL = 1024
HEAD_DIM = 64
N_HEADS = 8
ATTN_W = N_HEADS * HEAD_DIM
D_FF = 2816
FF_HALF = D_FF // 2
N_MOD = 6
ADA_SHARD = N_MOD * D_MODEL // N_DEV
IN_SHARD = 641
UP_SHARD = 704
ATTN_SCALE = HEAD_DIM ** -0.5
ALPHA = 2.0 ** 0.25
LN_EPS = 1e-5
LANES = 128
TQ = 512
SCAN_W = 256
VMEM_LIMIT = 56 * 1024 * 1024

ADAM_LR, ADAM_B1, ADAM_B2, ADAM_EPS, ADAM_WD, ADAM_STEP = 0.001, 0.9, 0.999, 1e-08, 0.01, 10

W_QKV, W_GATES, W_F = 3072, 2048, 128
W_PROJ = 5376


def _params(sem=None):
    return pltpu.CompilerParams(dimension_semantics=sem, vmem_limit_bytes=VMEM_LIMIT)


def _tile(n, cap):
    if n <= cap:
        return n
    best = None
    for t in range(LANES, cap + 1, LANES):
        if n % t == 0:
            best = t
    assert best is not None, (n, cap)
    return best


def _row_tile(r, width, budget=192 * 1024):
    if r * width <= budget or r % 16:
        return r
    best = 16
    for t in range(16, r + 1, 16):
        if r % t == 0 and t * width <= budget:
            best = t
    return best


def _me():
    x, y, c = lax.axis_index("x"), lax.axis_index("y"), lax.axis_index("c")
    return x, y, c, 4 * x + 2 * y + c


def _peer(r):
    x, y, c, _ = _me()
    px = 1 - x if r & 4 else x
    py = 1 - y if r & 2 else y
    pc = 1 - c if r & 1 else c
    return (px, py, pc), 4 * px + 2 * py + pc


class _Ride:
    def __init__(self, arrays, scatter):
        self.arrays, self.scatter, self.n = list(arrays), scatter, len(arrays)
        self.in_specs = [ANY] * self.n
        self.out_specs = [ANY] * self.n
        self.out_shape = [jax.ShapeDtypeStruct(a.shape if scatter else (N_DEV,) + a.shape, a.dtype) for a in arrays]
        self.scratch = [pltpu.SemaphoreType.DMA((self.n, N_DEV - 1)), pltpu.SemaphoreType.DMA((self.n, N_DEV - 1)),
                        pltpu.SemaphoreType.DMA((self.n,))]

    def _local(self, ins, outs, sems, a):
        me = _me()[3]
        return pltpu.make_async_copy(ins[a].at[me] if self.scatter else ins[a], outs[a].at[me], sems[2].at[a])

    def _remote(self, ins, outs, sems, a, r, arriving):
        me = _me()[3]
        peer, pidx = _peer(r)
        src = ins[a].at[me if arriving else pidx] if self.scatter else ins[a]
        return pltpu.make_async_remote_copy(
            src_ref=src, dst_ref=outs[a].at[pidx if arriving else me], send_sem=sems[0].at[a, r - 1],
            recv_sem=sems[1].at[a, r - 1], device_id=peer, device_id_type=MESH)

    def start(self, ins, outs, sems):
        for a in range(self.n):
            self._local(ins, outs, sems, a).start()
        for r in range(1, N_DEV):
            for a in range(self.n):
                self._remote(ins, outs, sems, a, r, False).start()

    def wait(self, ins, outs, sems):
        for r in range(1, N_DEV):
            for a in range(self.n):
                self._remote(ins, outs, sems, a, r, True).wait_recv()
        for r in range(1, N_DEV):
            for a in range(self.n):
                self._remote(ins, outs, sems, a, r, False).wait_send()
        for a in range(self.n):
            self._local(ins, outs, sems, a).wait()


def _exchange(name, arrays, scatter):
    ride = _Ride(arrays, scatter)

    def body(*refs):
        ins, outs, sems = refs[:ride.n], refs[ride.n:2 * ride.n], refs[2 * ride.n:]
        ride.start(ins, outs, sems)
        ride.wait(ins, outs, sems)

    return pl.pallas_call(body, name=name, in_specs=ride.in_specs, out_specs=ride.out_specs, out_shape=ride.out_shape,
                          scratch_shapes=ride.scratch)(*arrays)


def _allgather_two_level(name, a):
    def body(a_ref, out_ref, send_sems, recv_sems, local_sem):
        x, y, c, me = _me()
        sibling = (x, y, 1 - c)
        chips = [(1 - x, y), (x, 1 - y), (1 - x, 1 - y)]

        def idx(px, py, pc):
            return 4 * px + 2 * py + pc

        def copy(k, block, to, src=None):
            slot = out_ref.at[idx(*block)]
            return pltpu.make_async_remote_copy(
                src_ref=slot if src is None else src, dst_ref=slot, send_sem=send_sems.at[k], recv_sem=recv_sems.at[k],
                device_id=to, device_id_type=MESH)

        mine = pltpu.make_async_copy(a_ref, out_ref.at[me], local_sem)
        mine.start()
        first = [copy(0, (x, y, c), sibling, src=a_ref)]
        first += [copy(1 + j, (x, y, c), (*chip, c), src=a_ref) for j, chip in enumerate(chips)]
        for cp in first:
            cp.start()
        passed = [copy(4 + j, (*chip, c), sibling) for j, chip in enumerate(chips)]
        for j, chip in enumerate(chips):
            copy(1 + j, (*chip, c), (x, y, c)).wait_recv()
            passed[j].start()
        copy(0, sibling, (x, y, c)).wait_recv()
        for j, chip in enumerate(chips):
            copy(4 + j, (*chip, 1 - c), (x, y, c)).wait_recv()
        for cp in first + passed:
            cp.wait_send()
        mine.wait()

    return pl.pallas_call(
        body, name=name, in_specs=[ANY], out_specs=ANY,
        out_shape=jax.ShapeDtypeStruct((N_DEV,) + a.shape, a.dtype),
        scratch_shapes=[pltpu.SemaphoreType.DMA((N_DEV - 1,)), pltpu.SemaphoreType.DMA((N_DEV - 1,)),
                        pltpu.SemaphoreType.DMA],
    )(a)


def _with_ride(body, ride, n_in, n_out, grid):
    if ride is None:
        return body
    n = ride.n

    def wrapped(*refs):
        ins, rins = refs[:n_in], refs[n_in:n_in + n]
        outs, routs = refs[n_in + n:n_in + n + n_out], refs[n_in + n + n_out:n_in + 2 * n + n_out]
        rest = refs[n_in + 2 * n + n_out:]
        scratch, sems = rest[:len(rest) - 3], rest[len(rest) - 3:]
        ids = [pl.program_id(d) for d in range(len(grid))]
        first = functools.reduce(lambda p, q: p & q, [i == 0 for i in ids])
        last = functools.reduce(lambda p, q: p & q, [i == g - 1 for i, g in zip(ids, grid)])

        @pl.when(first)
        def _():
            ride.start(rins, routs, sems)

        body(*ins, *outs, *scratch)

        @pl.when(last)
        def _():
            ride.wait(rins, routs, sems)

    return wrapped


def _ride_call(body, ride, *, name, grid, in_specs, out_specs, out_shape, scratch_shapes, sem, args):
    n_in, n_out = len(in_specs), len(out_specs)
    if ride is None:
        res = pl.pallas_call(body, name=name, grid=grid, in_specs=in_specs, out_specs=out_specs, out_shape=out_shape,
                             scratch_shapes=scratch_shapes, compiler_params=_params(sem))(*args)
        return list(res), []
    res = pl.pallas_call(
        _with_ride(body, ride, n_in, n_out, grid), name=name, grid=grid,
        in_specs=list(in_specs) + ride.in_specs, out_specs=list(out_specs) + ride.out_specs,
        out_shape=list(out_shape) + ride.out_shape, scratch_shapes=list(scratch_shapes) + ride.scratch,
        compiler_params=_params(("arbitrary",) * len(grid)))(*args, *ride.arrays)
    return list(res[:n_out]), list(res[n_out:])


def _mm(a, b, *, name, ta=False, tb=False, out_dtype=F32, tm=1024, tn=1024, tk=1024, ride=None):
    m, k = (a.shape[1], a.shape[0]) if ta else a.shape
    n = b.shape[0] if tb else b.shape[1]
    assert (b.shape[1] if tb else b.shape[0]) == k
    tm, tn, tk = _tile(m, tm), _tile(n, tn), _tile(k, tk)
    nk = k // tk
    a_spec = pl.BlockSpec((tk, tm), lambda i, j, l: (l, i)) if ta else pl.BlockSpec((tm, tk), lambda i, j, l: (i, l))
    b_spec = pl.BlockSpec((tn, tk), lambda i, j, l: (j, l)) if tb else pl.BlockSpec((tk, tn), lambda i, j, l: (l, j))
    dims = (((0,) if ta else (1,), (1,) if tb else (0,)), ((), ()))

    def body(a_ref, b_ref, o_ref, *acc):
        p = lax.dot_general(a_ref[...].astype(BF16), b_ref[...].astype(BF16), dims, preferred_element_type=F32)
        if nk == 1:
            o_ref[...] = p.astype(out_dtype)
            return
        acc_ref = acc[0]
        step = pl.program_id(2)

        @pl.when(step == 0)
        def _():
            acc_ref[...] = p

        @pl.when(step > 0)
        def _():
            acc_ref[...] += p

        @pl.when(step == nk - 1)
        def _():
            o_ref[...] = acc_ref[...].astype(out_dtype)

    outs, rode = _ride_call(
        body, ride, name=name, grid=(m // tm, n // tn, nk), in_specs=[a_spec, b_spec],
        out_specs=[pl.BlockSpec((tm, tn), lambda i, j, l: (i, j))], out_shape=[jax.ShapeDtypeStruct((m, n), out_dtype)],
        scratch_shapes=[] if nk == 1 else [pltpu.VMEM((tm, tn), F32)], sem=("parallel", "parallel", "arbitrary"),
        args=(a, b))
    return outs[0] if ride is None else (outs[0], rode)


def _rowwise(name, fn, rows, vecs, outs, sums=(), tr=512):
    s = rows[0][0].shape[0]
    tr = min(tr, s)
    nr, nv, no = len(rows), len(vecs), len(outs)

    def body(*refs):
        vals = [r[...] for r in refs[:nr + nv]]
        res = fn(*vals)
        for o_ref, val in zip(refs[nr + nv:nr + nv + no], res[:no]):
            o_ref[...] = val.astype(o_ref.dtype)
        step = pl.program_id(0)
        for s_ref, val in zip(refs[nr + nv + no:], res[no:]):
            @pl.when(step == 0)
            def _(s_ref=s_ref, val=val):
                s_ref[...] = val

            @pl.when(step > 0)
            def _(s_ref=s_ref, val=val):
                s_ref[...] += val

    in_specs = [pl.BlockSpec((tr, w), functools.partial(lambda i, cb: (i, cb), cb=cb)) for _, w, cb in rows]
    in_specs += [pl.BlockSpec(v.shape, lambda i: (0, 0)) for v in vecs]
    out_specs = [pl.BlockSpec((tr, w), lambda i: (i, 0)) for w, _ in outs]
    out_specs += [pl.BlockSpec((1, w), lambda i: (0, 0)) for w in sums]
    out_shape = [jax.ShapeDtypeStruct((s, w), dt) for w, dt in outs]
    out_shape += [jax.ShapeDtypeStruct((1, w), F32) for w in sums]
    return pl.pallas_call(
        body, name=name, grid=(s // tr,), in_specs=in_specs, out_specs=out_specs, out_shape=out_shape,
        compiler_params=_params(("arbitrary",) if sums else ("parallel",)),
    )(*[r[0] for r in rows], *vecs)


def _colsum(x):
    return jnp.sum(x, axis=0, keepdims=True)


def _sigmoid(x):
    return 1.0 / (1.0 + jnp.exp(-x))


def _log_sigmoid(x):
    return jnp.minimum(x, 0.0) - jnp.log(1.0 + jnp.exp(-jnp.abs(x)))


def _ln_stats(r):
    mu = jnp.mean(r, axis=-1, keepdims=True)
    xc = r - mu
    var = jnp.mean(xc * xc, axis=-1, keepdims=True)
    rstd = lax.rsqrt(var + LN_EPS)
    return xc * rstd, rstd


def _ln_bwd(dy, xhat, rstd, g):
    dxh = dy * g
    m1 = jnp.mean(dxh, axis=-1, keepdims=True)
    m2 = jnp.mean(dxh * xhat, axis=-1, keepdims=True)
    return rstd * (dxh - m1 - xhat * m2)


def _mod_exchange(c_row, w_ada, b_ada_loc):
    def body(c_ref, w_ref, b_ref, call_ref, mod_ref, piece_ref, send_sems, recv_sems):
        me = _me()[3]
        call_ref[me] = c_ref[...]
        sent = []
        for r in range(1, N_DEV):
            peer, _ = _peer(r)
            cp = pltpu.make_async_remote_copy(
                src_ref=c_ref, dst_ref=call_ref.at[me], send_sem=send_sems.at[0, r - 1],
                recv_sem=recv_sems.at[0, r - 1], device_id=peer, device_id_type=MESH)
            cp.start()
            sent.append(cp)
        for r in range(1, N_DEV):
            peer, pidx = _peer(r)
            pltpu.make_async_remote_copy(
                src_ref=c_ref, dst_ref=call_ref.at[pidx], send_sem=send_sems.at[0, r - 1],
                recv_sem=recv_sems.at[0, r - 1], device_id=peer, device_id_type=MESH).wait_recv()
        c_all = jnp.concatenate([call_ref[d] for d in range(N_DEV)], axis=0)
        mod_loc = jnp.dot(c_all, w_ref[...], preferred_element_type=F32,
                          precision=lax.Precision.HIGHEST) + b_ref[...]
        for d in range(N_DEV):
            piece_ref[d] = mod_loc[d:d + 1, :]
        mod_ref[me] = piece_ref[me]
        for r in range(1, N_DEV):
            peer, pidx = _peer(r)
            cp = pltpu.make_async_remote_copy(
                src_ref=piece_ref.at[pidx], dst_ref=mod_ref.at[me], send_sem=send_sems.at[1, r - 1],
                recv_sem=recv_sems.at[1, r - 1], device_id=peer, device_id_type=MESH)
            cp.start()
            sent.append(cp)
        for r in range(1, N_DEV):
            peer, pidx = _peer(r)
            pltpu.make_async_remote_copy(
                src_ref=piece_ref.at[me], dst_ref=mod_ref.at[pidx], send_sem=send_sems.at[1, r - 1],
                recv_sem=recv_sems.at[1, r - 1], device_id=peer, device_id_type=MESH).wait_recv()
        for cp in sent:
            cp.wait_send()

    vmem = pl.BlockSpec(memory_space=pltpu.VMEM)
    return pl.pallas_call(
        body, name="mod_exchange", in_specs=[vmem, vmem, vmem], out_specs=[vmem, vmem],
        out_shape=[jax.ShapeDtypeStruct((N_DEV, 1, D_MODEL), F32), jax.ShapeDtypeStruct((N_DEV, 1, ADA_SHARD), F32)],
        scratch_shapes=[pltpu.VMEM((N_DEV, 1, ADA_SHARD), F32),
                        pltpu.SemaphoreType.DMA((2, N_DEV - 1)), pltpu.SemaphoreType.DMA((2, N_DEV - 1))],
        compiler_params=_params(),
    )(c_row, w_ada, b_ada_loc)


def _split3(x):
    hi = x.astype(BF16)
    r1 = x - hi.astype(F32)
    mid = r1.astype(BF16)
    lo = (r1 - mid.astype(F32)).astype(BF16)
    return hi, mid, lo


def _scan_rows(x_ref, o_ref, s, reverse, pre=None, post=None):
    tb = min(TQ, s)
    nb = s // tb
    row = lax.broadcasted_iota(jnp.int32, (tb, tb), 0)
    col = lax.broadcasted_iota(jnp.int32, (tb, tb), 1)
    tri = jnp.where((col >= row) if reverse else (col <= row), 1.0, 0.0).astype(BF16)

    def step(i, carry):
        blk = (nb - 1 - i) if reverse else i
        off = pl.multiple_of(blk * tb, tb)
        x = x_ref[pl.ds(off, tb), :]
        if pre is not None:
            x = pre(x, off)
        acc = carry
        for piece in _split3(x):
            acc = acc + jnp.dot(tri, piece, preferred_element_type=F32)
        o_ref[pl.ds(off, tb), :] = acc if post is None else post(acc, off)
        edge = acc[0:1, :] if reverse else acc[tb - 1:tb, :]
        return jnp.broadcast_to(edge, (tb, LANES))

    lax.fori_loop(0, nb, step, jnp.zeros((tb, LANES), F32))


def _forget_cumsum(f_raw, b_pad):
    s = f_raw.shape[0]

    def body(f_ref, b_ref, cum_ref):
        b = b_ref[...]
        _scan_rows(f_ref, cum_ref, s, False, pre=lambda x, off: _log_sigmoid(x + b))

    vmem = pl.BlockSpec(memory_space=pltpu.VMEM)
    return pl.pallas_call(body, name="forget_cumsum", in_specs=[vmem, vmem], out_specs=vmem,
                          out_shape=jax.ShapeDtypeStruct((s, LANES), F32), compiler_params=_params())(f_raw, b_pad)


def _forget_bwd(dcum, f_raw, b_pad):
    s = f_raw.shape[0]

    def body(d_ref, f_ref, b_ref, df_ref, db_ref, tmp_ref):
        b = b_ref[...]
        _scan_rows(d_ref, tmp_ref, s, True)
        df = tmp_ref[...] * _sigmoid(-(f_ref[...] + b))
        df_ref[...] = df.astype(BF16)
        db_ref[...] = _colsum(df)

    vmem = pl.BlockSpec(memory_space=pltpu.VMEM)
    return pl.pallas_call(
        body, name="forget_bwd", in_specs=[vmem, vmem, vmem], out_specs=[vmem, vmem],
        out_shape=[jax.ShapeDtypeStruct((s, LANES), BF16), jax.ShapeDtypeStruct((1, LANES), F32)],
        scratch_shapes=[pltpu.VMEM((s, LANES), F32)], compiler_params=_params())(dcum, f_raw, b_pad)


def _dot_nt(a, b):
    return lax.dot_general(a, b, (((1,), (1,)), ((), ())), preferred_element_type=F32)


def _head_masks():
    lane = lax.broadcasted_iota(jnp.int32, (TQ, LANES), 1)
    return lane, [lane < HEAD_DIM, lane >= HEAD_DIM]


def _pick(mask, x):
    return jnp.where(mask, x, jnp.zeros_like(x))


def _qkv_specs(s, col0):
    nb = ATTN_W // LANES
    return [pl.BlockSpec((TQ, LANES), lambda hp, qi: (qi, col0 + hp)),
            pl.BlockSpec((s, LANES), lambda hp, qi: (0, col0 + nb + hp)),
            pl.BlockSpec((s, LANES), lambda hp, qi: (0, col0 + 2 * nb + hp))]


def _pair_spec():
    return pl.BlockSpec((TQ, LANES), lambda hp, qi: (qi, hp))


def _diag_mask(width, strict):
    row = lax.broadcasted_iota(jnp.int32, (TQ, width * TQ), 0) + (width - 1) * TQ
    col = lax.broadcasted_iota(jnp.int32, (TQ, width * TQ), 1)
    return (col < row) if strict else (col <= row)


def _fox_fwd(qkv, cum_col, cum_row, ride=None):
    s = qkv.shape[0]
    nq = s // TQ

    def body(q_ref, k_ref, v_ref, cc_ref, cr_ref, o_ref, o32_ref, lse_ref):
        hp, qi = pl.program_id(0), pl.program_id(1)
        lane, masks = _head_masks()
        q2 = q_ref[...] * jnp.asarray(ATTN_SCALE, BF16)
        cc = cc_ref[...]
        qms = [_pick(masks[e], q2) for e in range(2)]
        cqs = [jnp.sum(jnp.where(lane == 2 * hp + e, cc, 0.0), axis=1, keepdims=True) for e in range(2)]

        def tile(kb, carry, masked, width=1):
            off, span = pl.multiple_of(kb * TQ, TQ), width * TQ
            k2, v2 = k_ref[pl.ds(off, span), :], v_ref[pl.ds(off, span), :]
            head0 = lax.broadcasted_iota(jnp.int32, (span, LANES), 1) < HEAD_DIM
            new = []
            for e in range(2):
                m, acc = carry[e]
                sc = _dot_nt(qms[e], k2) + (cqs[e] - cr_ref[e:e + 1, pl.ds(off, span)])
                if masked:
                    sc = jnp.where(_diag_mask(width, False), sc, -jnp.inf)
                m_new = jnp.maximum(m, jnp.max(sc, axis=1, keepdims=True))
                p = jnp.exp(sc - m_new)
                v_ones = jnp.where(head0 if e == 0 else ~head0, v2, jnp.ones_like(v2))
                acc = jnp.exp(m - m_new) * acc + jnp.dot(p.astype(BF16), v_ones, preferred_element_type=F32)
                new.append((m_new, acc))
            return tuple(new)

        init = (jnp.full((TQ, 1), -jnp.inf, F32), jnp.zeros((TQ, LANES), F32))
        carry = lax.cond(qi % 2 == 1, lambda cr: tile(qi - 1, cr, True, 2), lambda cr: tile(qi, cr, True), (init, init))
        quads = qi // 4
        carry = lax.fori_loop(0, quads, lambda j, cr: tile(4 * j, cr, False, 4), carry)
        carry = lax.cond(qi % 4 >= 2, lambda cr: tile(4 * quads, cr, False, 2), lambda cr: cr, carry)
        sums = [jnp.max(jnp.where(masks[1 - e], carry[e][1], 0.0), axis=1, keepdims=True) for e in range(2)]
        outs = [carry[e][1] / sums[e] for e in range(2)]
        lses = [carry[e][0] + jnp.log(sums[e]) for e in range(2)]
        out = jnp.where(masks[0], outs[0], outs[1])
        o_ref[...] = out.astype(BF16)
        o32_ref[...] = out
        lse_ref[...] = jnp.where(masks[0], lses[0], lses[1])

    return _ride_call(
        body, ride, name="fox_fwd", grid=(N_HEADS // 2, nq),
        in_specs=_qkv_specs(s, 0) + [pl.BlockSpec((TQ, LANES), lambda hp, qi: (qi, 0)),
                                     pl.BlockSpec((None, 2, s), lambda hp, qi: (hp, 0, 0))],
        out_specs=[_pair_spec(), _pair_spec(), _pair_spec()],
        out_shape=[jax.ShapeDtypeStruct((s, ATTN_W), BF16), jax.ShapeDtypeStruct((s, ATTN_W), F32),
                   jax.ShapeDtypeStruct((s, ATTN_W), F32)],
        scratch_shapes=[], sem=("parallel", "parallel"), args=(qkv, qkv, qkv, cum_col, cum_row))


def _write_transposed(acc_ref, out_ref):
    for c in range(out_ref.shape[0] // TQ):
        out_ref[c * TQ:(c + 1) * TQ, :] = jnp.transpose(acc_ref[:, c * TQ:(c + 1) * TQ]).astype(BF16)


def _fox_bwd(qkv, cum_col, cum_row, o, lse, do, ride=None):
    s = qkv.shape[0]
    nq = s // TQ

    def body(q_ref, k_ref, v_ref, cc_ref, cr_ref, o_ref, lse_ref, do_ref,
             dq_ref, dk_ref, dv_ref, dcr_ref, dk_acc, dv_acc):
        hp, qi = pl.program_id(0), pl.program_id(1)

        @pl.when(qi == 0)
        def _():
            dk_acc[...] = jnp.zeros_like(dk_acc)
            dv_acc[...] = jnp.zeros_like(dv_acc)
            dcr_ref[...] = jnp.zeros_like(dcr_ref)

        lane, masks = _head_masks()
        q2 = q_ref[...] * jnp.asarray(ATTN_SCALE, BF16)
        do2 = do_ref[...]
        prod = do2.astype(F32) * o_ref[...].astype(F32)
        lse2 = lse_ref[...]
        cc = cc_ref[...]
        qms = [_pick(masks[e], q2) for e in range(2)]
        doms = [_pick(masks[e], do2) for e in range(2)]
        deltas = [jnp.sum(jnp.where(masks[e], prod, 0.0), axis=1, keepdims=True) for e in range(2)]
        lses = [jnp.max(jnp.where(masks[e], lse2, -jnp.inf), axis=1, keepdims=True) for e in range(2)]
        cqs = [jnp.sum(jnp.where(lane == 2 * hp + e, cc, 0.0), axis=1, keepdims=True) for e in range(2)]
        qts = [jnp.transpose(qms[e].astype(F32)).astype(BF16) for e in range(2)]
        dots = [jnp.transpose(doms[e].astype(F32)).astype(BF16) for e in range(2)]
        ones_row = [HEAD_DIM * (1 - e) for e in range(2)]
        trow = lax.broadcasted_iota(jnp.int32, (LANES, TQ), 0)
        qts = [jnp.where(trow == ones_row[e], jnp.ones_like(qts[e]), qts[e]) for e in range(2)]

        def tile(kb, carry, masked, width=1):
            off, span = pl.multiple_of(kb * TQ, TQ), width * TQ
            k2, v2 = k_ref[pl.ds(off, span), :], v_ref[pl.ds(off, span), :]
            head0 = lax.broadcasted_iota(jnp.int32, (span, LANES), 1) < HEAD_DIM
            new, dks, dv = [], [], None
            for e in range(2):
                dq = carry[e]
                sc = _dot_nt(qms[e], k2) + (cqs[e] - cr_ref[e:e + 1, pl.ds(off, span)])
                p = jnp.exp(sc - lses[e])
                if masked:
                    p = jnp.where(_diag_mask(width, False), p, 0.0)
                ds = p * (_dot_nt(doms[e], v2) - deltas[e])
                dsb = ds.astype(BF16)
                dk_e = jnp.dot(qts[e], dsb, preferred_element_type=F32)
                dv_e = jnp.dot(dots[e], p.astype(BF16), preferred_element_type=F32)
                dks.append(dk_e)
                dv = dv_e if e == 0 else dv + dv_e
                dcr_ref[e:e + 1, pl.ds(off, span)] -= dk_e[ones_row[e]:ones_row[e] + 1, :]
                k_ones = jnp.where(head0 if e == 0 else ~head0, k2, jnp.ones_like(k2))
                new.append(dq + jnp.dot(dsb, k_ones, preferred_element_type=F32))
            krow = lax.broadcasted_iota(jnp.int32, (LANES, span), 0)
            dk_acc[:, pl.ds(off, span)] += jnp.where(krow < HEAD_DIM, dks[0], dks[1])
            dv_acc[:, pl.ds(off, span)] += dv
            return tuple(new)

        init = jnp.zeros((TQ, LANES), F32)
        carry = lax.fori_loop(0, qi // 2, lambda j, cr: tile(2 * j, cr, False, 2), (init, init))
        carry = lax.cond(qi % 2 == 1, lambda cr: tile(qi - 1, cr, True, 2), lambda cr: tile(qi, cr, True), carry)
        dq_ref[...] = (jnp.where(masks[0], carry[0], carry[1]) * ATTN_SCALE).astype(BF16)
        for e in range(2):
            dcr_ref[e:e + 1, pl.ds(pl.multiple_of(qi * TQ, TQ), TQ)] += jnp.transpose(carry[e])[
                ones_row[e]:ones_row[e] + 1, :]

        @pl.when(qi == nq - 1)
        def _():
            _write_transposed(dk_acc, dk_ref)
            _write_transposed(dv_acc, dv_ref)

    seq_spec = pl.BlockSpec((s, LANES), lambda hp, qi: (0, hp))
    return _ride_call(
        body, ride, name="fox_bwd", grid=(N_HEADS // 2, nq),
        in_specs=_qkv_specs(s, 0) + [pl.BlockSpec((TQ, LANES), lambda hp, qi: (qi, 0)),
                                     pl.BlockSpec((None, 2, s), lambda hp, qi: (hp, 0, 0)),
                                     _pair_spec(), _pair_spec(), _pair_spec()],
        out_specs=[_pair_spec(), seq_spec, seq_spec, pl.BlockSpec((None, 2, s), lambda hp, qi: (hp, 0, 0))],
        out_shape=[jax.ShapeDtypeStruct((s, ATTN_W), BF16)] * 3 + [jax.ShapeDtypeStruct((N_HEADS // 2, 2, s), F32)],
        scratch_shapes=[pltpu.VMEM((LANES, s), F32), pltpu.VMEM((LANES, s), F32)],
        sem=("parallel", "arbitrary"), args=(qkv, qkv, qkv, cum_col, cum_row, o, lse, do))


def _scan_matrix(reverse):
    row = lax.broadcasted_iota(jnp.int32, (SCAN_W, SCAN_W), 0)
    col = lax.broadcasted_iota(jnp.int32, (SCAN_W, SCAN_W), 1)
    return jnp.where((row > col) if reverse else (row < col), 1.0, 0.0).astype(BF16)


def _scan_cols(x, tri, reverse, init):
    nblk = x.shape[1] // SCAN_W
    parts, total = [None] * nblk, init
    far = 0 if reverse else SCAN_W - 1
    for b in (reversed(range(nblk)) if reverse else range(nblk)):
        blk = x[:, b * SCAN_W:(b + 1) * SCAN_W]
        part = jnp.dot(blk.astype(BF16), tri, preferred_element_type=F32)
        parts[b] = part + total
        total = total + (part[:, far:far + 1] + blk[:, far:far + 1])
    return (parts[0] if nblk == 1 else jnp.concatenate(parts, axis=1)), total


def _sb_logits(qm, k2):
    z = _dot_nt(qm, k2)
    neg_abs = lax.bitcast_convert_type(lax.bitcast_convert_type(z, jnp.uint32) | jnp.uint32(0x80000000), F32)
    soft = jnp.log(1.0 + jnp.exp(neg_abs))
    lb = jnp.minimum(z, 0.0) - soft
    return lb, lb - z


TILE_SLOTS = 4


def _tri_base(qi):
    return (qi * (qi + 1)) // 2


def _sb_fwd(qkv):
    s = qkv.shape[0]
    nq = s // TQ

    def body(q_ref, k_ref, v_ref, o_ref, t_ref, buf, sems):
        hp, qi = pl.program_id(0), pl.program_id(1)
        _, masks = _head_masks()
        suffix = _scan_matrix(True)
        q2 = q_ref[...] * jnp.asarray(ATTN_SCALE, BF16)
        qms = [_pick(masks[e], q2) for e in range(2)]
        base = _tri_base(qi)

        def store(e, kb):
            slot = kb % TILE_SLOTS
            return pltpu.make_async_copy(buf.at[e, slot], t_ref.at[2 * hp + e, base + kb], sems.at[e, slot])

        def tile(kb, carry, masked, width=1):
            off, span = pl.multiple_of(kb * TQ, TQ), width * TQ
            k2, v2 = k_ref[pl.ds(off, span), :], v_ref[pl.ds(off, span), :]
            new = []
            for e in range(2):
                run, acc = carry[e]
                lb, lo = _sb_logits(qms[e], k2)
                if masked:
                    strict = _diag_mask(width, True)
                    lo = jnp.where(strict, lo, 0.0)
                rest, run = _scan_cols(lo, suffix, True, run)
                a = jnp.exp(lb + rest)
                if masked:
                    a = jnp.where(strict, a, 0.0)
                ab, lbb = a.astype(BF16), lb.astype(BF16)
                acc = acc + jnp.dot(ab, v2, preferred_element_type=F32)
                for w in range(width):
                    blk = kb + w

                    @pl.when(blk + TILE_SLOTS <= qi)
                    def _(e=e, blk=blk):
                        store(e, blk + TILE_SLOTS).wait()
                    buf[e, blk % TILE_SLOTS, 0] = ab[:, w * TQ:(w + 1) * TQ]
                    buf[e, blk % TILE_SLOTS, 1] = lbb[:, w * TQ:(w + 1) * TQ]
                    store(e, blk).start()
                new.append((run, acc))
            return tuple(new)

        init = (jnp.zeros((TQ, 1), F32), jnp.zeros((TQ, LANES), F32))
        carry = lax.cond(qi % 2 == 1, lambda cr: tile(qi - 1, cr, True, 2), lambda cr: tile(qi, cr, True), (init, init))
        pairs = qi // 2
        carry = lax.fori_loop(0, pairs, lambda it, cr: tile(2 * (pairs - 1 - it), cr, False, 2), carry)
        for e in range(2):
            for blk in range(TILE_SLOTS):
                @pl.when(qi >= blk)
                def _(e=e, blk=blk):
                    store(e, blk).wait()
        o_ref[...] = jnp.where(masks[0], carry[0][1], carry[1][1]).astype(BF16)

    ntri = nq * (nq + 1) // 2
    return pl.pallas_call(
        body, name="sb_fwd", grid=(N_HEADS // 2, nq), in_specs=_qkv_specs(s, 3 * ATTN_W // LANES),
        out_specs=[_pair_spec(), ANY],
        out_shape=[jax.ShapeDtypeStruct((s, ATTN_W), BF16), jax.ShapeDtypeStruct((N_HEADS, ntri, 2, TQ, TQ), BF16)],
        scratch_shapes=[pltpu.VMEM((2, TILE_SLOTS, 2, TQ, TQ), BF16), pltpu.SemaphoreType.DMA((2, TILE_SLOTS))],
        compiler_params=_params(("arbitrary", "arbitrary")),
    )(qkv, qkv, qkv)


def _sb_bwd(qkv, tiles, do):
    s = qkv.shape[0]
    nq = s // TQ

    def body(q_ref, k_ref, v_ref, t_ref, do_ref, dq_ref, dk_ref, dv_ref, dk_acc, dv_acc, buf, sems):
        hp, qi = pl.program_id(0), pl.program_id(1)

        @pl.when(qi == 0)
        def _():
            dk_acc[...] = jnp.zeros_like(dk_acc)
            dv_acc[...] = jnp.zeros_like(dv_acc)

        _, masks = _head_masks()
        prefix = _scan_matrix(False)
        q2 = q_ref[...] * jnp.asarray(ATTN_SCALE, BF16)
        do2 = do_ref[...]
        qms = [_pick(masks[e], q2) for e in range(2)]
        doms = [_pick(masks[e], do2) for e in range(2)]
        qts = [jnp.transpose(qms[e].astype(F32)).astype(BF16) for e in range(2)]
        dots = [jnp.transpose(doms[e].astype(F32)).astype(BF16) for e in range(2)]
        base = _tri_base(qi)

        def fetch(e, kb):
            slot = kb % TILE_SLOTS
            return pltpu.make_async_copy(t_ref.at[2 * hp + e, base + kb], buf.at[e, slot], sems.at[e, slot])

        for e in range(2):
            fetch(e, 0).start()

            @pl.when(qi >= 1)
            def _(e=e):
                fetch(e, 1).start()

        def tile(kb, carry, masked, width=1):
            off, span = pl.multiple_of(kb * TQ, TQ), width * TQ
            k2, v2 = k_ref[pl.ds(off, span), :], v_ref[pl.ds(off, span), :]
            new, dk, dv = [], None, None
            for e in range(2):
                gsum, dq = carry[e]
                if not masked:
                    for blk in range(2, 2 + width):
                        @pl.when(kb + blk <= qi)
                        def _(e=e, blk=blk):
                            fetch(e, kb + blk).start()
                for w in range(width):
                    fetch(e, kb + w).wait()
                slots = [(kb + w) % TILE_SLOTS for w in range(width)]
                ab = buf[e, slots[0], 0] if width == 1 else jnp.concatenate([buf[e, sl, 0] for sl in slots], axis=1)
                lbb = buf[e, slots[0], 1] if width == 1 else jnp.concatenate([buf[e, sl, 1] for sl in slots], axis=1)
                beta = jnp.exp(lbb.astype(F32))
                g = ab.astype(F32) * _dot_nt(doms[e], v2)
                before, gsum = _scan_cols(g, prefix, False, gsum)
                dz = g - beta * (g + before)
                if masked:
                    dz = jnp.where(_diag_mask(width, True), dz, 0.0)
                dzb = dz.astype(BF16)
                dk_e = jnp.dot(qts[e], dzb, preferred_element_type=F32)
                dv_e = jnp.dot(dots[e], ab, preferred_element_type=F32)
                dk, dv = (dk_e, dv_e) if e == 0 else (dk + dk_e, dv + dv_e)
                new.append((gsum, dq + jnp.dot(dzb, k2, preferred_element_type=F32)))
            dk_acc[:, pl.ds(off, span)] += dk
            dv_acc[:, pl.ds(off, span)] += dv
            return tuple(new)

        init = (jnp.zeros((TQ, 1), F32), jnp.zeros((TQ, LANES), F32))
        carry = lax.fori_loop(0, qi // 2, lambda j, cr: tile(2 * j, cr, False, 2), (init, init))
        carry = lax.cond(qi % 2 == 1, lambda cr: tile(qi - 1, cr, True, 2), lambda cr: tile(qi, cr, True), carry)
        dq_ref[...] = (jnp.where(masks[0], carry[0][1], carry[1][1]) * ATTN_SCALE).astype(BF16)

        @pl.when(qi == nq - 1)
        def _():
            _write_transposed(dk_acc, dk_ref)
            _write_transposed(dv_acc, dv_ref)

    seq_spec = pl.BlockSpec((s, LANES), lambda hp, qi: (0, hp))
    return pl.pallas_call(
        body, name="sb_bwd", grid=(N_HEADS // 2, nq),
        in_specs=_qkv_specs(s, 3 * ATTN_W // LANES) + [ANY, _pair_spec()],
        out_specs=[_pair_spec(), seq_spec, seq_spec],
        out_shape=[jax.ShapeDtypeStruct((s, ATTN_W), BF16)] * 3,
        scratch_shapes=[pltpu.VMEM((LANES, s), F32), pltpu.VMEM((LANES, s), F32),
                        pltpu.VMEM((2, TILE_SLOTS, 2, TQ, TQ), BF16), pltpu.SemaphoreType.DMA((2, TILE_SLOTS))],
        compiler_params=_params(("arbitrary", "arbitrary")),
    )(qkv, qkv, qkv, tiles, do)


CONV_TR = 256


def _shift_down(x, halo, n):
    rolled = pltpu.roll(x, n, 0)
    head = rolled[0:8, :]
    rid = lax.broadcasted_iota(jnp.int32, head.shape, 0)
    for j in range(n):
        head = jnp.where(rid == j, halo[8 - n + j:8 - n + j + 1, :], head)
    return jnp.concatenate([head, rolled[8:, :]], axis=0)


def _shift_up(x, halo, n):
    rows = x.shape[0]
    rolled = pltpu.roll(x, rows - n, 0)
    tail = rolled[rows - 8:, :]
    rid = lax.broadcasted_iota(jnp.int32, tail.shape, 0)
    for j in range(n):
        tail = jnp.where(rid == 8 - n + j, halo[j:j + 1, :], tail)
    return jnp.concatenate([rolled[:rows - 8, :], tail], axis=0)


def _conv_fwd_block(x, halo, w, b):
    return b + _shift_down(x, halo, 2) * w[0:1, :] + _shift_down(x, halo, 1) * w[1:2, :] + x * w[2:3, :]


def _conv_specs(tr, s):
    pair = 2 * FF_HALF
    blk = pl.BlockSpec((tr, pair), lambda j, i: (i, j))
    prev = pl.BlockSpec((8, pair), lambda j, i: (jnp.maximum(i * (tr // 8) - 1, 0), j))
    nxt = pl.BlockSpec((8, pair), lambda j, i: (jnp.minimum((i + 1) * (tr // 8), s // 8 - 1), j))
    return blk, prev, nxt


def _conv_gate_fwd(hpre, conv_w, conv_b):
    s = hpre.shape[0]
    tr = min(CONV_TR, s)
    blk, prev, _ = _conv_specs(tr, s)

    def body(x_ref, halo_ref, w_ref, b_ref, a_ref):
        i = pl.program_id(1)
        halo = jnp.where(i > 0, halo_ref[...], 0.0)
        h = _conv_fwd_block(x_ref[...], halo, w_ref[...], b_ref[...])
        hg, hv = h[:, :FF_HALF], h[:, FF_HALF:]
        a_ref[...] = (hg * _sigmoid(hg) * hv).astype(BF16)

    return pl.pallas_call(
        body, name="conv_gate_fwd", grid=(2, s // tr),
        in_specs=[blk, prev, pl.BlockSpec((3, 2 * FF_HALF), lambda j, i: (0, j)),
                  pl.BlockSpec((1, 2 * FF_HALF), lambda j, i: (0, j))],
        out_specs=pl.BlockSpec((tr, FF_HALF), lambda j, i: (i, j)),
        out_shape=jax.ShapeDtypeStruct((s, D_FF), BF16),
        compiler_params=_params(("parallel", "parallel")),
    )(hpre, hpre, conv_w, conv_b)


def _conv_gate_bwd(hpre, da, conv_w, conv_b):
    s = hpre.shape[0]
    tr = min(CONV_TR, s)
    blk, prev, _ = _conv_specs(tr, s)

    def body(x_ref, halo_ref, da_ref, w_ref, b_ref, dh_ref, db_ref, dw_ref):
        i = pl.program_id(1)
        halo = jnp.where(i > 0, halo_ref[...], 0.0)
        x = x_ref[...]
        h = _conv_fwd_block(x, halo, w_ref[...], b_ref[...])
        hg, hv = h[:, :FF_HALF], h[:, FF_HALF:]
        da_blk = da_ref[...].astype(F32)
        sg = _sigmoid(hg)
        dhg = da_blk * hv * (sg * (1.0 + hg * (1.0 - sg)))
        dhv = da_blk * (hg * sg)
        dh_ref[:, :FF_HALF] = dhg.astype(BF16)
        dh_ref[:, FF_HALF:] = dhv.astype(BF16)
        x2, x1 = _shift_down(x, halo, 2), _shift_down(x, halo, 1)
        parts = []
        for lo, dpart in ((0, dhg), (FF_HALF, dhv)):
            cols = slice(lo, lo + FF_HALF)
            parts.append((cols, _colsum(dpart), _colsum(dpart * x2[:, cols]), _colsum(dpart * x1[:, cols]),
                          _colsum(dpart * x[:, cols])))

        @pl.when(i == 0)
        def _():
            for cols, db, dw0, dw1, dw2 in parts:
                db_ref[:, cols] = db
                dw_ref[0:1, cols] = dw0
                dw_ref[1:2, cols] = dw1
                dw_ref[2:3, cols] = dw2

        @pl.when(i > 0)
        def _():
            for cols, db, dw0, dw1, dw2 in parts:
                db_ref[:, cols] += db
                dw_ref[0:1, cols] += dw0
                dw_ref[1:2, cols] += dw1
                dw_ref[2:3, cols] += dw2

    pair = 2 * FF_HALF
    return pl.pallas_call(
        body, name="conv_gate_bwd", grid=(2, s // tr),
        in_specs=[blk, prev, pl.BlockSpec((tr, FF_HALF), lambda j, i: (i, j)),
                  pl.BlockSpec((3, pair), lambda j, i: (0, j)), pl.BlockSpec((1, pair), lambda j, i: (0, j))],
        out_specs=[blk, pl.BlockSpec((1, pair), lambda j, i: (0, j)), pl.BlockSpec((3, pair), lambda j, i: (0, j))],
        out_shape=[jax.ShapeDtypeStruct((s, 2 * D_FF), BF16), jax.ShapeDtypeStruct((1, 2 * D_FF), F32),
                   jax.ShapeDtypeStruct((3, 2 * D_FF), F32)],
        compiler_params=_params(("parallel", "arbitrary")),
    )(hpre, hpre, da, conv_w, conv_b)


def _conv_input_bwd(dh, conv_w):
    s = dh.shape[0]
    tr = min(CONV_TR, s)
    blk, _, _ = _conv_specs(tr, s)
    nblk = s // tr

    def body(x_ref, halo_ref, w_ref, o_ref):
        i = pl.program_id(1)
        halo = jnp.where(i < nblk - 1, halo_ref[...].astype(F32), 0.0)
        x, w = x_ref[...].astype(F32), w_ref[...]
        o_ref[...] = (x * w[2:3, :] + _shift_up(x, halo, 1) * w[1:2, :] + _shift_up(x, halo, 2) * w[0:1, :]).astype(BF16)

    nxt = pl.BlockSpec((16, 2 * FF_HALF), lambda j, i: (jnp.minimum((i + 1) * (tr // 16), s // 16 - 1), j))
    return pl.pallas_call(
        body, name="conv_input_bwd", grid=(2, nblk),
        in_specs=[blk, nxt, pl.BlockSpec((3, 2 * FF_HALF), lambda j, i: (0, j))], out_specs=blk,
        out_shape=jax.ShapeDtypeStruct((s, 2 * D_FF), BF16),
        compiler_params=_params(("parallel", "parallel")),
    )(dh, dh, conv_w)


def _adamw_math(w, g, m, v):
    m = ADAM_B1 * m + (1.0 - ADAM_B1) * g
    v = ADAM_B2 * v + (1.0 - ADAM_B2) * (g * g)
    m_hat = m / (1.0 - ADAM_B1 ** ADAM_STEP)
    v_hat = v / (1.0 - ADAM_B2 ** ADAM_STEP)
    delta = -ADAM_LR * (m_hat / (jnp.sqrt(v_hat) + ADAM_EPS) + ADAM_WD * w)
    return delta, m, v


def _adamw(name, g8, w, m, v):
    r, c = w.shape
    tr = _row_tile(r, c)

    def body(g_ref, w_ref, m_ref, v_ref, go_ref, d_ref, mo_ref, vo_ref):
        g = g_ref[0].astype(F32)
        for d in range(1, N_DEV):
            g = g + g_ref[d].astype(F32)
        delta, mn, vn = _adamw_math(w_ref[...], g, m_ref[...], v_ref[...])
        go_ref[...] = g
        d_ref[...] = delta
        mo_ref[...] = mn
        vo_ref[...] = vn

    spec = pl.BlockSpec((tr, c), lambda i: (i, 0))
    return pl.pallas_call(
        body, name=name, grid=(r // tr,),
        in_specs=[pl.BlockSpec((N_DEV, tr, c), lambda i: (0, i, 0)), spec, spec, spec], out_specs=[spec] * 4,
        out_shape=[jax.ShapeDtypeStruct((r, c), F32)] * 4, compiler_params=_params(("parallel",)),
    )(g8, w, m, v)


def _adamw_ada(c_t, dmod, w, m, v):
    r, c = w.shape
    tr = _row_tile(r, c)

    def body(ct_ref, dm_ref, w_ref, m_ref, v_ref, go_ref, d_ref, mo_ref, vo_ref):
        ct, dm = ct_ref[...], dm_ref[...]
        g = ct[:, 0:1] * dm[0:1, :]
        for b in range(1, N_DEV):
            g = g + ct[:, b:b + 1] * dm[b:b + 1, :]
        delta, mn, vn = _adamw_math(w_ref[...], g, m_ref[...], v_ref[...])
        go_ref[...] = g
        d_ref[...] = delta
        mo_ref[...] = mn
        vo_ref[...] = vn

    spec = pl.BlockSpec((tr, c), lambda i: (i, 0))
    return pl.pallas_call(
        body, name="adamw_w_ada", grid=(r // tr,),
        in_specs=[pl.BlockSpec((tr, N_DEV), lambda i: (i, 0)), pl.BlockSpec((N_DEV, c), lambda i: (0, 0)),
                  spec, spec, spec],
        out_specs=[spec] * 4, out_shape=[jax.ShapeDtypeStruct((r, c), F32)] * 4,
        compiler_params=_params(("parallel",)),
    )(c_t, dmod, w, m, v)


def _cols_from_slots(g):
    n, r, c = g.shape
    return jnp.transpose(g, (1, 0, 2)).reshape(r, n * c)


def _cols_to_slots(w):
    r, c = w.shape
    return jnp.transpose(w.reshape(r, N_DEV, c // N_DEV), (1, 0, 2))


def _pair_cols(w):
    g0, g1 = w[..., 0:FF_HALF], w[..., FF_HALF:D_FF]
    v0, v1 = w[..., D_FF:D_FF + FF_HALF], w[..., D_FF + FF_HALF:]
    return jnp.concatenate([g0, v0, g1, v1], axis=-1)


def _unpair_cols(w):
    g0, v0 = w[..., 0:FF_HALF], w[..., FF_HALF:D_FF]
    g1, v1 = w[..., D_FF:D_FF + FF_HALF], w[..., D_FF + FF_HALF:]
    return jnp.concatenate([g0, g1, v0, v1], axis=-1)


def _row(v):
    return v.reshape(1, -1)


def kernel(x, c, w_ada, b_ada, w_in, b_forget, w_fox_proj, w_sb_proj, w_o, ln1_g, ln1_b, w_up, conv_w, conv_b, w_down, ln2_g, ln2_b, loss_target, m_w_ada, m_b_ada, m_w_in, m_b_forget, m_w_fox_proj, m_w_sb_proj, m_w_o, m_ln1_g, m_ln1_b, m_w_up, m_conv_w, m_conv_b, m_w_down, m_ln2_g, m_ln2_b, v_w_ada, v_b_ada, v_w_in, v_b_forget, v_w_fox_proj, v_w_sb_proj, v_w_o, v_ln1_g, v_ln1_b, v_w_up, v_conv_w, v_conv_b, v_w_down, v_ln2_g, v_ln2_b):
    s = x.shape[1]
    me = 4 * lax.axis_index("x") + 2 * lax.axis_index("y") + lax.axis_index("c")
    x2 = x.reshape(s, D_MODEL)
    tgt = loss_target.reshape(s, D_MODEL)

    b_ada_loc = lax.dynamic_slice(b_ada, (me * ADA_SHARD,), (ADA_SHARD,)).reshape(1, ADA_SHARD)
    c_all, mod = _mod_exchange(c, w_ada, b_ada_loc)
    mod = mod.reshape(N_MOD, 1, D_MODEL)
    sh1, sc1, gt1, sh2, sc2, gt2 = [mod[i] for i in range(N_MOD)]

    g_in = _allgather_two_level("ag_w_in", w_in.astype(BF16))
    late_weights = _Ride([w_fox_proj.astype(BF16), w_sb_proj.astype(BF16), w_o.astype(BF16), w_up.astype(BF16),
                          w_down.astype(BF16), conv_w], scatter=False)
    w_in_f = _cols_from_slots(g_in)
    w_proj = jnp.concatenate(
        [w_in_f[:, 0:1536], w_in_f[:, 1544:3080], w_in_f[:, 3080:5128], w_in_f[:, 1536:1544],
         jnp.zeros((D_MODEL, W_PROJ - 5128), BF16)], axis=1)
    w_qkv, w_gates, w_f = w_proj[:, :W_QKV], w_proj[:, W_QKV:W_QKV + W_GATES], w_proj[:, W_QKV + W_GATES:W_QKV + W_GATES + W_F]
    conv_b_p = _pair_cols(_row(conv_b))
    b_f_pad = jnp.pad(_row(b_forget), ((0, 0), (0, LANES - N_HEADS)))

    (u1,) = _rowwise("modulate1", lambda xb, sc, sh: (xb * (1.0 + sc) + sh,),
                     [(x2, D_MODEL, 0)], [sc1, sh1], [(D_MODEL, BF16)], tr=512)
    qkv = _mm(u1, w_qkv, name="mm_qkv", out_dtype=BF16)
    gates = _mm(u1, w_gates, name="mm_gates")
    f_raw = _mm(u1, w_f, name="mm_forget")
    cum_col = _forget_cumsum(f_raw, b_f_pad)
    cum_row = jnp.transpose(cum_col[:, :N_HEADS]).reshape(N_HEADS // 2, 2, s)
    (y_fox, y_fox32, lse), (g_fox, g_sb, g_o, g_up, g_down, g_cw) = _fox_fwd(qkv, cum_col, cum_row, ride=late_weights)
    w_fox_f = _cols_from_slots(g_fox)
    w_sb_f = _cols_from_slots(g_sb)
    w_o_f = g_o.reshape(D_MODEL, D_MODEL)
    w_up_p = _pair_cols(_cols_from_slots(g_up))
    w_down_f = g_down.reshape(D_FF, D_MODEL)
    conv_w_p = _pair_cols(_cols_from_slots(g_cw))
    y_sb, sb_run = _sb_fwd(qkv)
    pf = _mm(y_fox, w_fox_f, name="mm_fox_proj", out_dtype=BF16)
    ps = _mm(y_sb, w_sb_f, name="mm_sb_proj", out_dtype=BF16)
    (merged,) = _rowwise("gate_merge", lambda ga, gb, a, b: (_sigmoid(ga) * a + _sigmoid(gb) * b,),
                         [(gates, D_MODEL, 0), (gates, D_MODEL, 1), (pf, D_MODEL, 0), (ps, D_MODEL, 0)], [],
                         [(D_MODEL, BF16)])
    attn_out = _mm(merged, w_o_f, name="mm_w_o")

    def ln_fwd(xb, fb, gt, g, b):
        xhat, _ = _ln_stats(ALPHA * xb + (1.0 + gt) * fb)
        return xhat * g + b

    def ln1_mod(xb, fb, gt, g, b, sc, sh):
        y = ln_fwd(xb, fb, gt, g, b)
        return y, y * (1.0 + sc) + sh

    x1, u2 = _rowwise("ln1_modulate2", ln1_mod, [(x2, D_MODEL, 0), (attn_out, D_MODEL, 0)],
                      [gt1, _row(ln1_g), _row(ln1_b), sc2, sh2], [(D_MODEL, F32), (D_MODEL, BF16)])

    hpre = _mm(u2, w_up_p, name="mm_w_up", tn=1408)
    act = _conv_gate_fwd(hpre, conv_w_p, conv_b_p)
    ffn_out = _mm(act, w_down_f, name="mm_w_down", tk=2816)

    def ln2_bwd(xb, fb, tb, gt, g, b):
        xhat, rstd = _ln_stats(ALPHA * xb + (1.0 + gt) * fb)
        err = (xhat * g + b) - tb
        dy = err * (1.0 / D_MODEL)
        dr = _ln_bwd(dy, xhat, rstd, g)
        return (dr * (1.0 + gt), ALPHA * dr,
                _colsum(err * err), _colsum(dy * xhat), _colsum(dy), _colsum(dr * fb))

    dffn, dx1_res, sq_err, d_ln2_g, d_ln2_b, d_gt2 = _rowwise(
        "ln2_bwd", ln2_bwd, [(x1, D_MODEL, 0), (ffn_out, D_MODEL, 0), (tgt, D_MODEL, 0)],
        [gt2, _row(ln2_g), _row(ln2_b)], [(D_MODEL, BF16), (D_MODEL, F32)], sums=[D_MODEL] * 4)
    loss = lax.psum(0.5 * jnp.sum(sq_err) / D_MODEL, ("x", "y", "c"))

    d_w_down = _mm(act, dffn, name="mm_d_w_down", ta=True, tm=1408, tk=2048, out_dtype=BF16)
    d_act = _mm(dffn, w_down_f, name="mm_d_act", tb=True, tn=1408, out_dtype=BF16)
    dh, d_conv_b_p, d_conv_w_p = _conv_gate_bwd(hpre, d_act, conv_w_p, conv_b_p)
    dhpre = _conv_input_bwd(dh, conv_w_p)
    d_w_up_p = _mm(u2, dhpre, name="mm_d_w_up", ta=True, tn=1408, tk=2048, out_dtype=BF16)
    du2 = _mm(dhpre, w_up_p, name="mm_d_u2", tb=True, tk=2816)

    def ln1_bwd(du, dres, x1b, xb, fb, sc, gt, g):
        dx1 = dres + du * (1.0 + sc)
        xhat, rstd = _ln_stats(ALPHA * xb + (1.0 + gt) * fb)
        dr = _ln_bwd(dx1, xhat, rstd, g)
        return (dr * (1.0 + gt), ALPHA * dr,
                _colsum(du * x1b), _colsum(du), _colsum(dx1 * xhat), _colsum(dx1), _colsum(dr * fb))

    d_attn, dx_res, d_sc2, d_sh2, d_ln1_g, d_ln1_b, d_gt1 = _rowwise(
        "ln1_bwd", ln1_bwd,
        [(du2, D_MODEL, 0), (dx1_res, D_MODEL, 0), (x1, D_MODEL, 0), (x2, D_MODEL, 0), (attn_out, D_MODEL, 0)],
        [sc2, gt1, _row(ln1_g)], [(D_MODEL, BF16), (D_MODEL, F32)], sums=[D_MODEL] * 5)

    d_w_o = _mm(merged, d_attn, name="mm_d_w_o", ta=True, out_dtype=BF16)
    d_merged = _mm(d_attn, w_o_f, name="mm_d_merged", tb=True, out_dtype=BF16)

    def merge_bwd(dm, ga, gb, a, b):
        dm, a, b = dm.astype(F32), a.astype(F32), b.astype(F32)
        sa, sb = _sigmoid(ga), _sigmoid(gb)
        return dm * a * sa * (1.0 - sa), dm * b * sb * (1.0 - sb), dm * sa, dm * sb

    d_ga, d_gb, d_pf, d_ps = _rowwise(
        "gate_merge_bwd", merge_bwd,
        [(d_merged, D_MODEL, 0), (gates, D_MODEL, 0), (gates, D_MODEL, 1), (pf, D_MODEL, 0), (ps, D_MODEL, 0)], [],
        [(D_MODEL, BF16)] * 4)
    d_w_fox = _mm(y_fox, d_pf, name="mm_d_w_fox", ta=True, out_dtype=BF16)
    d_w_sb = _mm(y_sb, d_ps, name="mm_d_w_sb", ta=True, out_dtype=BF16)
    d_y_fox = _mm(d_pf, w_fox_f, name="mm_d_y_fox", tb=True, out_dtype=BF16)
    d_y_sb = _mm(d_ps, w_sb_f, name="mm_d_y_sb", tb=True, out_dtype=BF16)
    early_grads = _Ride(
        [_cols_to_slots(d_w_fox), _cols_to_slots(d_w_sb), d_w_o.reshape(N_DEV, D_MODEL // N_DEV, D_MODEL),
         _cols_to_slots(_unpair_cols(d_w_up_p)), d_w_down.reshape(N_DEV, D_FF // N_DEV, D_MODEL)], scatter=True)
    (dq_a, dk_a, dv_a, d_cum_row), early_slots = _fox_bwd(qkv, cum_col, cum_row, y_fox32, lse, d_y_fox,
                                                                    ride=early_grads)
    dq_b, dk_b, dv_b = _sb_bwd(qkv, sb_run, d_y_sb)
    d_cum = jnp.transpose(d_cum_row.reshape(N_HEADS, s))
    d_cum = jnp.pad(d_cum, ((0, 0), (0, LANES - N_HEADS)))
    d_f, d_b_forget = _forget_bwd(d_cum, f_raw, b_f_pad)
    d_proj = jnp.concatenate([dq_a, dk_a, dv_a, dq_b, dk_b, dv_b, d_ga, d_gb, d_f,
                              jnp.zeros((s, W_PROJ - W_QKV - W_GATES - W_F), BF16)], axis=1)
    d_w_proj = _mm(u1, d_proj, name="mm_d_w_in", ta=True, tn=896, tk=2048, out_dtype=BF16)
    d_w_in_f = jnp.concatenate([d_w_proj[:, 0:1536], d_w_proj[:, 5120:5128], d_w_proj[:, 1536:3072],
                                d_w_proj[:, 3072:5120]], axis=1)
    du1, (in_slots,) = _mm(d_proj, w_proj, name="mm_d_u1", tb=True, tk=2688,
                           ride=_Ride([_cols_to_slots(d_w_in_f)], scatter=True))

    def x_bwd(du, dres, xb, sc):
        return dres + du * (1.0 + sc), _colsum(du * xb), _colsum(du)

    grad_x, d_sc1, d_sh1 = _rowwise("x_bwd", x_bwd, [(du1, D_MODEL, 0), (dx_res, D_MODEL, 0), (x2, D_MODEL, 0)],
                                    [sc1], [(D_MODEL, F32)], sums=[D_MODEL] * 2, tr=512)

    d_conv_b = _unpair_cols(d_conv_b_p)
    d_conv_w = _unpair_cols(d_conv_w_p)
    n_rep = N_MOD * D_MODEL + LANES + 4 * D_MODEL + 2 * D_FF
    small = jnp.concatenate(
        [d_sh1, d_sc1, d_gt1, d_sh2, d_sc2, d_gt2, d_b_forget, d_ln1_g, d_ln1_b, d_ln2_g, d_ln2_b, d_conv_b,
         d_conv_w.reshape(1, 6 * D_FF)], axis=1)
    n_small = small.shape[1] // LANES
    small = jnp.pad(small.reshape(n_small, LANES), ((0, 264 - n_small), (0, 0)))
    (small_all,) = _exchange("ag_small_grads", [small], scatter=False)
    rep8 = small_all[:, :n_rep // LANES, :]
    cw8 = small_all[:, n_rep // LANES:n_small, :].reshape(N_DEV, 3, 2 * D_FF)
    cw8 = lax.dynamic_slice(cw8, (0, 0, me * UP_SHARD), (N_DEV, 3, UP_SHARD))
    dmod8 = small_all[:, :N_MOD * D_MODEL // LANES, :].reshape(N_DEV, N_MOD * D_MODEL)
    dmod_loc = lax.dynamic_slice(dmod8, (0, me * ADA_SHARD), (N_DEV, ADA_SHARD))

    def pack_rep(b_a, b_f, g1, b1, g2, b2, cb):
        flat = jnp.concatenate([b_a, jnp.pad(b_f, (0, LANES - N_HEADS)), g1, b1, g2, b2, cb])
        return flat.reshape(n_rep // LANES, LANES)

    rep = _adamw("adamw_small", rep8, pack_rep(b_ada, b_forget, ln1_g, ln1_b, ln2_g, ln2_b, conv_b),
                 pack_rep(m_b_ada, m_b_forget, m_ln1_g, m_ln1_b, m_ln2_g, m_ln2_b, m_conv_b),
                 pack_rep(v_b_ada, v_b_forget, v_ln1_g, v_ln1_b, v_ln2_g, v_ln2_b, v_conv_b))

    def unpack_rep(p):
        flat = p.reshape(-1)
        o = N_MOD * D_MODEL
        return {"b_ada": flat[:o], "b_forget": flat[o:o + N_HEADS],
                "ln1_g": flat[o + 128:o + 1152], "ln1_b": flat[o + 1152:o + 2176],
                "ln2_g": flat[o + 2176:o + 3200], "ln2_b": flat[o + 3200:o + 4224], "conv_b": flat[o + 4224:]}

    rep = [unpack_rep(p) for p in rep]
    r_conv_w = _adamw("adamw_conv_w", cw8, conv_w, m_conv_w, v_conv_w)
    r_ada = _adamw_ada(jnp.transpose(c_all.reshape(N_DEV, D_MODEL)), dmod_loc, w_ada, m_w_ada, v_w_ada)

    r_in = _adamw("adamw_w_in", in_slots, w_in, m_w_in, v_w_in)
    r_fox = _adamw("adamw_w_fox", early_slots[0], w_fox_proj, m_w_fox_proj, v_w_fox_proj)
    r_sb = _adamw("adamw_w_sb", early_slots[1], w_sb_proj, m_w_sb_proj, v_w_sb_proj)
    r_o = _adamw("adamw_w_o", early_slots[2], w_o, m_w_o, v_w_o)
    r_up = _adamw("adamw_w_up", early_slots[3], w_up, m_w_up, v_w_up)
    r_down = _adamw("adamw_w_down", early_slots[4], w_down, m_w_down, v_w_down)

    def leaf(i):
        return [r_ada[i], rep[i]["b_ada"], r_in[i], rep[i]["b_forget"], r_fox[i], r_sb[i], r_o[i], rep[i]["ln1_g"],
                rep[i]["ln1_b"], r_up[i], r_conv_w[i], rep[i]["conv_b"], r_down[i], rep[i]["ln2_g"], rep[i]["ln2_b"]]

    return (loss, grad_x.reshape(1, s, D_MODEL), *leaf(0), *leaf(1), *leaf(2), *leaf(3))
```

```python
import functools

import jax
import jax.numpy as jnp
from jax import lax
from jax.experimental import pallas as pl
from jax.experimental.pallas import tpu as pltpu

F32 = jnp.float32
BF16 = jnp.bfloat16
MESH = pl.DeviceIdType.MESH
ANY = pl.BlockSpec(memory_space=pl.ANY)

N_DEV = 8
D_MODEL = 1024
HEAD_DIM = 64
N_HEADS = 8
ATTN_W = N_HEADS * HEAD_DIM
D_FF = 2816
FF_HALF = D_FF // 2
N_MOD = 6
ADA_SHARD = N_MOD * D_MODEL // N_DEV
IN_SHARD = 641
UP_SHARD = 704
ATTN_SCALE = HEAD_DIM ** -0.5
ALPHA = 2.0 ** 0.25
LN_EPS = 1e-5
LANES = 128
TQ = 512
SCAN_W = 256
VMEM_LIMIT = 56 * 1024 * 1024

ADAM_LR, ADAM_B1, ADAM_B2, ADAM_EPS, ADAM_WD, ADAM_STEP = 0.001, 0.9, 0.999, 1e-08, 0.01, 10

W_QKV, W_GATES, W_F = 3072, 2048, 128
W_PROJ = 5376


def _params(sem=None):
    return pltpu.CompilerParams(dimension_semantics=sem, vmem_limit_bytes=VMEM_LIMIT)


def _tile(n, cap):
    if n <= cap:
        return n
    best = None
    for t in range(LANES, cap + 1, LANES):
        if n % t == 0:
            best = t
    assert best is not None, (n, cap)
    return best


def _row_tile(r, width, budget=192 * 1024):
    if r * width <= budget or r % 16:
        return r
    best = 16
    for t in range(16, r + 1, 16):
        if r % t == 0 and t * width <= budget:
            best = t
    return best


def _me():
    x, y, c = lax.axis_index("x"), lax.axis_index("y"), lax.axis_index("c")
    return x, y, c, 4 * x + 2 * y + c


def _peer(r):
    x, y, c, _ = _me()
    px = 1 - x if r & 4 else x
    py = 1 - y if r & 2 else y
    pc = 1 - c if r & 1 else c
    return (px, py, pc), 4 * px + 2 * py + pc


class _Ride:
    def __init__(self, arrays, scatter):
        self.arrays, self.scatter, self.n = list(arrays), scatter, len(arrays)
        self.in_specs = [ANY] * self.n
        self.out_specs = [ANY] * self.n
        self.out_shape = [jax.ShapeDtypeStruct(a.shape if scatter else (N_DEV,) + a.shape, a.dtype) for a in arrays]
        self.scratch = [pltpu.SemaphoreType.DMA((self.n, N_DEV - 1)), pltpu.SemaphoreType.DMA((self.n, N_DEV - 1)),
                        pltpu.SemaphoreType.DMA((self.n,))]

    def _local(self, ins, outs, sems, a):
        me = _me()[3]
        return pltpu.make_async_copy(ins[a].at[me] if self.scatter else ins[a], outs[a].at[me], sems[2].at[a])

    def _remote(self, ins, outs, sems, a, r, arriving):
        me = _me()[3]
        peer, pidx = _peer(r)
        src = ins[a].at[me if arriving else pidx] if self.scatter else ins[a]
        return pltpu.make_async_remote_copy(
            src_ref=src, dst_ref=outs[a].at[pidx if arriving else me], send_sem=sems[0].at[a, r - 1],
            recv_sem=sems[1].at[a, r - 1], device_id=peer, device_id_type=MESH)

    def start(self, ins, outs, sems):
        for a in range(self.n):
            self._local(ins, outs, sems, a).start()
        for r in range(1, N_DEV):
            for a in range(self.n):
                self._remote(ins, outs, sems, a, r, False).start()

    def wait(self, ins, outs, sems):
        for r in range(1, N_DEV):
            for a in range(self.n):
                self._remote(ins, outs, sems, a, r, True).wait_recv()
        for r in range(1, N_DEV):
            for a in range(self.n):
                self._remote(ins, outs, sems, a, r, False).wait_send()
        for a in range(self.n):
            self._local(ins, outs, sems, a).wait()


def _exchange(name, arrays, scatter):
    ride = _Ride(arrays, scatter)

    def body(*refs):
        ins, outs, sems = refs[:ride.n], refs[ride.n:2 * ride.n], refs[2 * ride.n:]
        ride.start(ins, outs, sems)
        ride.wait(ins, outs, sems)

    return pl.pallas_call(body, name=name, in_specs=ride.in_specs, out_specs=ride.out_specs, out_shape=ride.out_shape,
                          scratch_shapes=ride.scratch)(*arrays)


def _allgather_two_level(name, a):
    def body(a_ref, out_ref, send_sems, recv_sems, local_sem):
        x, y, c, me = _me()
        sibling = (x, y, 1 - c)
        chips = [(1 - x, y), (x, 1 - y), (1 - x, 1 - y)]

        def idx(px, py, pc):
            return 4 * px + 2 * py + pc

        def copy(k, block, to, src=None):
            slot = out_ref.at[idx(*block)]
            return pltpu.make_async_remote_copy(
                src_ref=slot if src is None else src, dst_ref=slot, send_sem=send_sems.at[k], recv_sem=recv_sems.at[k],
                device_id=to, device_id_type=MESH)

        mine = pltpu.make_async_copy(a_ref, out_ref.at[me], local_sem)
        mine.start()
        first = [copy(0, (x, y, c), sibling, src=a_ref)]
        first += [copy(1 + j, (x, y, c), (*chip, c), src=a_ref) for j, chip in enumerate(chips)]
        for cp in first:
            cp.start()
        passed = [copy(4 + j, (*chip, c), sibling) for j, chip in enumerate(chips)]
        for j, chip in enumerate(chips):
            copy(1 + j, (*chip, c), (x, y, c)).wait_recv()
            passed[j].start()
        copy(0, sibling, (x, y, c)).wait_recv()
        for j, chip in enumerate(chips):
            copy(4 + j, (*chip, 1 - c), (x, y, c)).wait_recv()
        for cp in first + passed:
            cp.wait_send()
        mine.wait()

    return pl.pallas_call(
        body, name=name, in_specs=[ANY], out_specs=ANY,
        out_shape=jax.ShapeDtypeStruct((N_DEV,) + a.shape, a.dtype),
        scratch_shapes=[pltpu.SemaphoreType.DMA((N_DEV - 1,)), pltpu.SemaphoreType.DMA((N_DEV - 1,)),
                        pltpu.SemaphoreType.DMA],
    )(a)


def _with_ride(body, ride, n_in, n_out, grid):
    if ride is None:
        return body
    n = ride.n

    def wrapped(*refs):
        ins, rins = refs[:n_in], refs[n_in:n_in + n]
        outs, routs = refs[n_in + n:n_in + n + n_out], refs[n_in + n + n_out:n_in + 2 * n + n_out]
        rest = refs[n_in + 2 * n + n_out:]
        scratch, sems = rest[:len(rest) - 3], rest[len(rest) - 3:]
        ids = [pl.program_id(d) for d in range(len(grid))]
        first = functools.reduce(lambda p, q: p & q, [i == 0 for i in ids])
        last = functools.reduce(lambda p, q: p & q, [i == g - 1 for i, g in zip(ids, grid)])

        @pl.when(first)
        def _():
            ride.start(rins, routs, sems)

        body(*ins, *outs, *scratch)

        @pl.when(last)
        def _():
            ride.wait(rins, routs, sems)

    return wrapped


def _ride_call(body, ride, *, name, grid, in_specs, out_specs, out_shape, scratch_shapes, sem, args):
    n_in, n_out = len(in_specs), len(out_specs)
    if ride is None:
        res = pl.pallas_call(body, name=name, grid=grid, in_specs=in_specs, out_specs=out_specs, out_shape=out_shape,
                             scratch_shapes=scratch_shapes, compiler_params=_params(sem))(*args)
        return list(res), []
    res = pl.pallas_call(
        _with_ride(body, ride, n_in, n_out, grid), name=name, grid=grid,
        in_specs=list(in_specs) + ride.in_specs, out_specs=list(out_specs) + ride.out_specs,
        out_shape=list(out_shape) + ride.out_shape, scratch_shapes=list(scratch_shapes) + ride.scratch,
        compiler_params=_params(("arbitrary",) * len(grid)))(*args, *ride.arrays)
    return list(res[:n_out]), list(res[n_out:])


def _mm(a, b, *, name, ta=False, tb=False, out_dtype=F32, tm=1024, tn=1024, tk=1024, ride=None):
    m, k = (a.shape[1], a.shape[0]) if ta else a.shape
    n = b.shape[0] if tb else b.shape[1]
    assert (b.shape[1] if tb else b.shape[0]) == k
    tm, tn, tk = _tile(m, tm), _tile(n, tn), _tile(k, tk)
    nk = k // tk
    a_spec = pl.BlockSpec((tk, tm), lambda i, j, l: (l, i)) if ta else pl.BlockSpec((tm, tk), lambda i, j, l: (i, l))
    b_spec = pl.BlockSpec((tn, tk), lambda i, j, l: (j, l)) if tb else pl.BlockSpec((tk, tn), lambda i, j, l: (l, j))
    dims = (((0,) if ta else (1,), (1,) if tb else (0,)), ((), ()))

    def body(a_ref, b_ref, o_ref, *acc):
        p = lax.dot_general(a_ref[...].astype(BF16), b_ref[...].astype(BF16), dims, preferred_element_type=F32)
        if nk == 1:
            o_ref[...] = p.astype(out_dtype)
            return
        acc_ref = acc[0]
        step = pl.program_id(2)

        @pl.when(step == 0)
        def _():
            acc_ref[...] = p

        @pl.when(step > 0)
        def _():
            acc_ref[...] += p

        @pl.when(step == nk - 1)
        def _():
            o_ref[...] = acc_ref[...].astype(out_dtype)

    outs, rode = _ride_call(
        body, ride, name=name, grid=(m // tm, n // tn, nk), in_specs=[a_spec, b_spec],
        out_specs=[pl.BlockSpec((tm, tn), lambda i, j, l: (i, j))], out_shape=[jax.ShapeDtypeStruct((m, n), out_dtype)],
        scratch_shapes=[] if nk == 1 else [pltpu.VMEM((tm, tn), F32)], sem=("parallel", "parallel", "arbitrary"),
        args=(a, b))
    return outs[0] if ride is None else (outs[0], rode)


def _rowwise(name, fn, rows, vecs, outs, sums=(), tr=512):
    s = rows[0][0].shape[0]
    tr = min(tr, s)
    nr, nv, no = len(rows), len(vecs), len(outs)

    def body(*refs):
        vals = [r[...] for r in refs[:nr + nv]]
        res = fn(*vals)
        for o_ref, val in zip(refs[nr + nv:nr + nv + no], res[:no]):
            o_ref[...] = val.astype(o_ref.dtype)
        step = pl.program_id(0)
        for s_ref, val in zip(refs[nr + nv + no:], res[no:]):
            @pl.when(step == 0)
            def _(s_ref=s_ref, val=val):
                s_ref[...] = val

            @pl.when(step > 0)
            def _(s_ref=s_ref, val=val):
                s_ref[...] += val

    in_specs = [pl.BlockSpec((tr, w), functools.partial(lambda i, cb: (i, cb), cb=cb)) for _, w, cb in rows]
    in_specs += [pl.BlockSpec(v.shape, lambda i: (0, 0)) for v in vecs]
    out_specs = [pl.BlockSpec((tr, w), lambda i: (i, 0)) for w, _ in outs]
    out_specs += [pl.BlockSpec((1, w), lambda i: (0, 0)) for w in sums]
    out_shape = [jax.ShapeDtypeStruct((s, w), dt) for w, dt in outs]
    out_shape += [jax.ShapeDtypeStruct((1, w), F32) for w in sums]
    return pl.pallas_call(
        body, name=name, grid=(s // tr,), in_specs=in_specs, out_specs=out_specs, out_shape=out_shape,
        compiler_params=_params(("arbitrary",) if sums else ("parallel",)),
    )(*[r[0] for r in rows], *vecs)


def _colsum(x):
    return jnp.sum(x, axis=0, keepdims=True)


def _sigmoid(x):
    return 1.0 / (1.0 + jnp.exp(-x))


def _log_sigmoid(x):
    return jnp.minimum(x, 0.0) - jnp.log(1.0 + jnp.exp(-jnp.abs(x)))


def _ln_stats(r):
    mu = jnp.mean(r, axis=-1, keepdims=True)
    xc = r - mu
    var = jnp.mean(xc * xc, axis=-1, keepdims=True)
    rstd = lax.rsqrt(var + LN_EPS)
    return xc * rstd, rstd


def _ln_bwd(dy, xhat, rstd, g):
    dxh = dy * g
    m1 = jnp.mean(dxh, axis=-1, keepdims=True)
    m2 = jnp.mean(dxh * xhat, axis=-1, keepdims=True)
    return rstd * (dxh - m1 - xhat * m2)


def _mod_exchange(c_row, w_ada, b_ada_loc):
    def body(c_ref, w_ref, b_ref, call_ref, mod_ref, piece_ref, send_sems, recv_sems):
        me = _me()[3]
        call_ref[me] = c_ref[...]
        sent = []
        for r in range(1, N_DEV):
            peer, _ = _peer(r)
            cp = pltpu.make_async_remote_copy(
                src_ref=c_ref, dst_ref=call_ref.at[me], send_sem=send_sems.at[0, r - 1],
                recv_sem=recv_sems.at[0, r - 1], device_id=peer, device_id_type=MESH)
            cp.start()
            sent.append(cp)
        for r in range(1, N_DEV):
            peer, pidx = _peer(r)
            pltpu.make_async_remote_copy(
                src_ref=c_ref, dst_ref=call_ref.at[pidx], send_sem=send_sems.at[0, r - 1],
                recv_sem=recv_sems.at[0, r - 1], device_id=peer, device_id_type=MESH).wait_recv()
        c_all = jnp.concatenate([call_ref[d] for d in range(N_DEV)], axis=0)
        mod_loc = jnp.dot(c_all, w_ref[...], preferred_element_type=F32,
                          precision=lax.Precision.HIGHEST) + b_ref[...]
        for d in range(N_DEV):
            piece_ref[d] = mod_loc[d:d + 1, :]
        mod_ref[me] = piece_ref[me]
        for r in range(1, N_DEV):
            peer, pidx = _peer(r)
            cp = pltpu.make_async_remote_copy(
                src_ref=piece_ref.at[pidx], dst_ref=mod_ref.at[me], send_sem=send_sems.at[1, r - 1],
                recv_sem=recv_sems.at[1, r - 1], device_id=peer, device_id_type=MESH)
            cp.start()
            sent.append(cp)
        for r in range(1, N_DEV):
            peer, pidx = _peer(r)
            pltpu.make_async_remote_copy(
                src_ref=piece_ref.at[me], dst_ref=mod_ref.at[pidx], send_sem=send_sems.at[1, r - 1],
                recv_sem=recv_sems.at[1, r - 1], device_id=peer, device_id_type=MESH).wait_recv()
        for cp in sent:
            cp.wait_send()

    vmem = pl.BlockSpec(memory_space=pltpu.VMEM)
    return pl.pallas_call(
        body, name="mod_exchange", in_specs=[vmem, vmem, vmem], out_specs=[vmem, vmem],
        out_shape=[jax.ShapeDtypeStruct((N_DEV, 1, D_MODEL), F32), jax.ShapeDtypeStruct((N_DEV, 1, ADA_SHARD), F32)],
        scratch_shapes=[pltpu.VMEM((N_DEV, 1, ADA_SHARD), F32),
                        pltpu.SemaphoreType.DMA((2, N_DEV - 1)), pltpu.SemaphoreType.DMA((2, N_DEV - 1))],
        compiler_params=_params(),
    )(c_row, w_ada, b_ada_loc)


def _split3(x):
    hi = x.astype(BF16)
    r1 = x - hi.astype(F32)
    mid = r1.astype(BF16)
    lo = (r1 - mid.astype(F32)).astype(BF16)
    return hi, mid, lo


def _scan_rows(x_ref, o_ref, s, reverse, pre=None, post=None):
    tb = min(TQ, s)
    nb = s // tb
    row = lax.broadcasted_iota(jnp.int32, (tb, tb), 0)
    col = lax.broadcasted_iota(jnp.int32, (tb, tb), 1)
    tri = jnp.where((col >= row) if reverse else (col <= row), 1.0, 0.0).astype(BF16)

    def step(i, carry):
        blk = (nb - 1 - i) if reverse else i
        off = pl.multiple_of(blk * tb, tb)
        x = x_ref[pl.ds(off, tb), :]
        if pre is not None:
            x = pre(x, off)
        acc = carry
        for piece in _split3(x):
            acc = acc + jnp.dot(tri, piece, preferred_element_type=F32)
        o_ref[pl.ds(off, tb), :] = acc if post is None else post(acc, off)
        edge = acc[0:1, :] if reverse else acc[tb - 1:tb, :]
        return jnp.broadcast_to(edge, (tb, LANES))

    lax.fori_loop(0, nb, step, jnp.zeros((tb, LANES), F32))


def _forget_cumsum(f_raw, b_pad):
    s = f_raw.shape[0]

    def body(f_ref, b_ref, cum_ref):
        b = b_ref[...]
        _scan_rows(f_ref, cum_ref, s, False, pre=lambda x, off: _log_sigmoid(x + b))

    vmem = pl.BlockSpec(memory_space=pltpu.VMEM)
    return pl.pallas_call(body, name="forget_cumsum", in_specs=[vmem, vmem], out_specs=vmem,
                          out_shape=jax.ShapeDtypeStruct((s, LANES), F32), compiler_params=_params())(f_raw, b_pad)


def _forget_bwd(dcum, f_raw, b_pad):
    s = f_raw.shape[0]

    def body(d_ref, f_ref, b_ref, df_ref, db_ref, tmp_ref):
        b = b_ref[...]
        _scan_rows(d_ref, tmp_ref, s, True)
        df = tmp_ref[...] * _sigmoid(-(f_ref[...] + b))
        df_ref[...] = df.astype(BF16)
        db_ref[...] = _colsum(df)

    vmem = pl.BlockSpec(memory_space=pltpu.VMEM)
    return pl.pallas_call(
        body, name="forget_bwd", in_specs=[vmem, vmem, vmem], out_specs=[vmem, vmem],
        out_shape=[jax.ShapeDtypeStruct((s, LANES), BF16), jax.ShapeDtypeStruct((1, LANES), F32)],
        scratch_shapes=[pltpu.VMEM((s, LANES), F32)], compiler_params=_params())(dcum, f_raw, b_pad)


def _dot_nt(a, b):
    return lax.dot_general(a, b, (((1,), (1,)), ((), ())), preferred_element_type=F32)


def _head_masks():
    lane = lax.broadcasted_iota(jnp.int32, (TQ, LANES), 1)
    return lane, [lane < HEAD_DIM, lane >= HEAD_DIM]


def _pick(mask, x):
    return jnp.where(mask, x, jnp.zeros_like(x))


def _qkv_specs(s, col0):
    nb = ATTN_W // LANES
    return [pl.BlockSpec((TQ, LANES), lambda hp, qi: (qi, col0 + hp)),
            pl.BlockSpec((s, LANES), lambda hp, qi: (0, col0 + nb + hp)),
            pl.BlockSpec((s, LANES), lambda hp, qi: (0, col0 + 2 * nb + hp))]


def _pair_spec():
    return pl.BlockSpec((TQ, LANES), lambda hp, qi: (qi, hp))


def _diag_mask(width, strict):
    row = lax.broadcasted_iota(jnp.int32, (TQ, width * TQ), 0) + (width - 1) * TQ
    col = lax.broadcasted_iota(jnp.int32, (TQ, width * TQ), 1)
    return (col < row) if strict else (col <= row)


def _fox_fwd(qkv, cum_col, cum_row, ride=None):
    s = qkv.shape[0]
    nq = s // TQ

    def body(q_ref, k_ref, v_ref, cc_ref, cr_ref, o_ref, o32_ref, lse_ref):
        hp, qi = pl.program_id(0), pl.program_id(1)
        lane, masks = _head_masks()
        q2 = q_ref[...] * jnp.asarray(ATTN_SCALE, BF16)
        cc = cc_ref[...]
        qms = [_pick(masks[e], q2) for e in range(2)]
        cqs = [jnp.sum(jnp.where(lane == 2 * hp + e, cc, 0.0), axis=1, keepdims=True) for e in range(2)]

        def tile(kb, carry, masked, width=1):
            off, span = pl.multiple_of(kb * TQ, TQ), width * TQ
            k2, v2 = k_ref[pl.ds(off, span), :], v_ref[pl.ds(off, span), :]
            head0 = lax.broadcasted_iota(jnp.int32, (span, LANES), 1) < HEAD_DIM
            new = []
            for e in range(2):
                m, acc = carry[e]
                sc = _dot_nt(qms[e], k2) + (cqs[e] - cr_ref[e:e + 1, pl.ds(off, span)])
                if masked:
                    sc = jnp.where(_diag_mask(width, False), sc, -jnp.inf)
                m_new = jnp.maximum(m, jnp.max(sc, axis=1, keepdims=True))
                p = jnp.exp(sc - m_new)
                v_ones = jnp.where(head0 if e == 0 else ~head0, v2, jnp.ones_like(v2))
                acc = jnp.exp(m - m_new) * acc + jnp.dot(p.astype(BF16), v_ones, preferred_element_type=F32)
                new.append((m_new, acc))
            return tuple(new)

        init = (jnp.full((TQ, 1), -jnp.inf, F32), jnp.zeros((TQ, LANES), F32))
        carry = lax.cond(qi % 2 == 1, lambda cr: tile(qi - 1, cr, True, 2), lambda cr: tile(qi, cr, True), (init, init))
        quads = qi // 4
        carry = lax.fori_loop(0, quads, lambda j, cr: tile(4 * j, cr, False, 4), carry)
        carry = lax.cond(qi % 4 >= 2, lambda cr: tile(4 * quads, cr, False, 2), lambda cr: cr, carry)
        sums = [jnp.max(jnp.where(masks[1 - e], carry[e][1], 0.0), axis=1, keepdims=True) for e in range(2)]
        outs = [carry[e][1] / sums[e] for e in range(2)]
        lses = [carry[e][0] + jnp.log(sums[e]) for e in range(2)]
        out = jnp.where(masks[0], outs[0], outs[1])
        o_ref[...] = out.astype(BF16)
        o32_ref[...] = out
        lse_ref[...] = jnp.where(masks[0], lses[0], lses[1])

    return _ride_call(
        body, ride, name="fox_fwd", grid=(N_HEADS // 2, nq),
        in_specs=_qkv_specs(s, 0) + [pl.BlockSpec((TQ, LANES), lambda hp, qi: (qi, 0)),
                                     pl.BlockSpec((None, 2, s), lambda hp, qi: (hp, 0, 0))],
        out_specs=[_pair_spec(), _pair_spec(), _pair_spec()],
        out_shape=[jax.ShapeDtypeStruct((s, ATTN_W), BF16), jax.ShapeDtypeStruct((s, ATTN_W), F32),
                   jax.ShapeDtypeStruct((s, ATTN_W), F32)],
        scratch_shapes=[], sem=("parallel", "parallel"), args=(qkv, qkv, qkv, cum_col, cum_row))


def _write_transposed(acc_ref, out_ref):
    for c in range(out_ref.shape[0] // TQ):
        out_ref[c * TQ:(c + 1) * TQ, :] = jnp.transpose(acc_ref[:, c * TQ:(c + 1) * TQ]).astype(BF16)


def _fox_bwd(qkv, cum_col, cum_row, o, lse, do, ride=None):
    s = qkv.shape[0]
    nq = s // TQ

    def body(q_ref, k_ref, v_ref, cc_ref, cr_ref, o_ref, lse_ref, do_ref,
             dq_ref, dk_ref, dv_ref, dcr_ref, dk_acc, dv_acc):
        hp, qi = pl.program_id(0), pl.program_id(1)

        @pl.when(qi == 0)
        def _():
            dk_acc[...] = jnp.zeros_like(dk_acc)
            dv_acc[...] = jnp.zeros_like(dv_acc)
            dcr_ref[...] = jnp.zeros_like(dcr_ref)

        lane, masks = _head_masks()
        q2 = q_ref[...] * jnp.asarray(ATTN_SCALE, BF16)
        do2 = do_ref[...]
        prod = do2.astype(F32) * o_ref[...].astype(F32)
        lse2 = lse_ref[...]
        cc = cc_ref[...]
        qms = [_pick(masks[e], q2) for e in range(2)]
        doms = [_pick(masks[e], do2) for e in range(2)]
        deltas = [jnp.sum(jnp.where(masks[e], prod, 0.0), axis=1, keepdims=True) for e in range(2)]
        lses = [jnp.max(jnp.where(masks[e], lse2, -jnp.inf), axis=1, keepdims=True) for e in range(2)]
        cqs = [jnp.sum(jnp.where(lane == 2 * hp + e, cc, 0.0), axis=1, keepdims=True) for e in range(2)]
        qts = [jnp.transpose(qms[e].astype(F32)).astype(BF16) for e in range(2)]
        dots = [jnp.transpose(doms[e].astype(F32)).astype(BF16) for e in range(2)]
        ones_row = [HEAD_DIM * (1 - e) for e in range(2)]
        trow = lax.broadcasted_iota(jnp.int32, (LANES, TQ), 0)
        qts = [jnp.where(trow == ones_row[e], jnp.ones_like(qts[e]), qts[e]) for e in range(2)]

        def tile(kb, carry, masked, width=1):
            off, span = pl.multiple_of(kb * TQ, TQ), width * TQ
            k2, v2 = k_ref[pl.ds(off, span), :], v_ref[pl.ds(off, span), :]
            head0 = lax.broadcasted_iota(jnp.int32, (span, LANES), 1) < HEAD_DIM
            new, dks, dv = [], [], None
            for e in range(2):
                dq = carry[e]
                sc = _dot_nt(qms[e], k2) + (cqs[e] - cr_ref[e:e + 1, pl.ds(off, span)])
                p = jnp.exp(sc - lses[e])
                if masked:
                    p = jnp.where(_diag_mask(width, False), p, 0.0)
                ds = p * (_dot_nt(doms[e], v2) - deltas[e])
                dsb = ds.astype(BF16)
                dk_e = jnp.dot(qts[e], dsb, preferred_element_type=F32)
                dv_e = jnp.dot(dots[e], p.astype(BF16), preferred_element_type=F32)
                dks.append(dk_e)
                dv = dv_e if e == 0 else dv + dv_e
                dcr_ref[e:e + 1, pl.ds(off, span)] -= dk_e[ones_row[e]:ones_row[e] + 1, :]
                k_ones = jnp.where(head0 if e == 0 else ~head0, k2, jnp.ones_like(k2))
                new.append(dq + jnp.dot(dsb, k_ones, preferred_element_type=F32))
            krow = lax.broadcasted_iota(jnp.int32, (LANES, span), 0)
            dk_acc[:, pl.ds(off, span)] += jnp.where(krow < HEAD_DIM, dks[0], dks[1])
            dv_acc[:, pl.ds(off, span)] += dv
            return tuple(new)

        init = jnp.zeros((TQ, LANES), F32)
        carry = lax.fori_loop(0, qi // 2, lambda j, cr: tile(2 * j, cr, False, 2), (init, init))
        carry = lax.cond(qi % 2 == 1, lambda cr: tile(qi - 1, cr, True, 2), lambda cr: tile(qi, cr, True), carry)
        dq_ref[...] = (jnp.where(masks[0], carry[0], carry[1]) * ATTN_SCALE).astype(BF16)
        for e in range(2):
            dcr_ref[e:e + 1, pl.ds(pl.multiple_of(qi * TQ, TQ), TQ)] += jnp.transpose(carry[e])[
                ones_row[e]:ones_row[e] + 1, :]

        @pl.when(qi == nq - 1)
        def _():
            _write_transposed(dk_acc, dk_ref)
            _write_transposed(dv_acc, dv_ref)

    seq_spec = pl.BlockSpec((s, LANES), lambda hp, qi: (0, hp))
    return _ride_call(
        body, ride, name="fox_bwd", grid=(N_HEADS // 2, nq),
        in_specs=_qkv_specs(s, 0) + [pl.BlockSpec((TQ, LANES), lambda hp, qi: (qi, 0)),
                                     pl.BlockSpec((None, 2, s), lambda hp, qi: (hp, 0, 0)),
                                     _pair_spec(), _pair_spec(), _pair_spec()],
        out_specs=[_pair_spec(), seq_spec, seq_spec, pl.BlockSpec((None, 2, s), lambda hp, qi: (hp, 0, 0))],
        out_shape=[jax.ShapeDtypeStruct((s, ATTN_W), BF16)] * 3 + [jax.ShapeDtypeStruct((N_HEADS // 2, 2, s), F32)],
        scratch_shapes=[pltpu.VMEM((LANES, s), F32), pltpu.VMEM((LANES, s), F32)],
        sem=("parallel", "arbitrary"), args=(qkv, qkv, qkv, cum_col, cum_row, o, lse, do))


def _scan_matrix(reverse):
    row = lax.broadcasted_iota(jnp.int32, (SCAN_W, SCAN_W), 0)
    col = lax.broadcasted_iota(jnp.int32, (SCAN_W, SCAN_W), 1)
    return jnp.where((row > col) if reverse else (row < col), 1.0, 0.0).astype(BF16)


def _scan_cols(x, tri, reverse, init):
    nblk = x.shape[1] // SCAN_W
    parts, total = [None] * nblk, init
    far = 0 if reverse else SCAN_W - 1
    for b in (reversed(range(nblk)) if reverse else range(nblk)):
        blk = x[:, b * SCAN_W:(b + 1) * SCAN_W]
        part = jnp.dot(blk.astype(BF16), tri, preferred_element_type=F32)
        parts[b] = part + total
        total = total + (part[:, far:far + 1] + blk[:, far:far + 1])
    return (parts[0] if nblk == 1 else jnp.concatenate(parts, axis=1)), total


def _sb_logits(qm, k2):
    z = _dot_nt(qm, k2)
    neg_abs = lax.bitcast_convert_type(lax.bitcast_convert_type(z, jnp.uint32) | jnp.uint32(0x80000000), F32)
    soft = jnp.log(1.0 + jnp.exp(neg_abs))
    lb = jnp.minimum(z, 0.0) - soft
    return lb, lb - z


TILE_SLOTS = 4


def _tri_base(qi):
    return (qi * (qi + 1)) // 2


def _sb_fwd(qkv):
    s = qkv.shape[0]
    nq = s // TQ

    def body(q_ref, k_ref, v_ref, o_ref, t_ref, buf, sems):
        hp, qi = pl.program_id(0), pl.program_id(1)
        _, masks = _head_masks()
        suffix = _scan_matrix(True)
        q2 = q_ref[...] * jnp.asarray(ATTN_SCALE, BF16)
        qms = [_pick(masks[e], q2) for e in range(2)]
        base = _tri_base(qi)

        def store(e, kb):
            slot = kb % TILE_SLOTS
            return pltpu.make_async_copy(buf.at[e, slot], t_ref.at[2 * hp + e, base + kb], sems.at[e, slot])

        def tile(kb, carry, masked, width=1):
            off, span = pl.multiple_of(kb * TQ, TQ), width * TQ
            k2, v2 = k_ref[pl.ds(off, span), :], v_ref[pl.ds(off, span), :]
            new = []
            for e in range(2):
                run, acc = carry[e]
                lb, lo = _sb_logits(qms[e], k2)
                if masked:
                    strict = _diag_mask(width, True)
                    lo = jnp.where(strict, lo, 0.0)
                rest, run = _scan_cols(lo, suffix, True, run)
                a = jnp.exp(lb + rest)
                if masked:
                    a = jnp.where(strict, a, 0.0)
                ab, lbb = a.astype(BF16), lb.astype(BF16)
                acc = acc + jnp.dot(ab, v2, preferred_element_type=F32)
                for w in range(width):
                    blk = kb + w

                    @pl.when(blk + TILE_SLOTS <= qi)
                    def _(e=e, blk=blk):
                        store(e, blk + TILE_SLOTS).wait()
                    buf[e, blk % TILE_SLOTS, 0] = ab[:, w * TQ:(w + 1) * TQ]
                    buf[e, blk % TILE_SLOTS, 1] = lbb[:, w * TQ:(w + 1) * TQ]
                    store(e, blk).start()
                new.append((run, acc))
            return tuple(new)

        init = (jnp.zeros((TQ, 1), F32), jnp.zeros((TQ, LANES), F32))
        carry = lax.cond(qi % 2 == 1, lambda cr: tile(qi - 1, cr, True, 2), lambda cr: tile(qi, cr, True), (init, init))
        pairs = qi // 2
        carry = lax.fori_loop(0, pairs, lambda it, cr: tile(2 * (pairs - 1 - it), cr, False, 2), carry)
        for e in range(2):
            for blk in range(TILE_SLOTS):
                @pl.when(qi >= blk)
                def _(e=e, blk=blk):
                    store(e, blk).wait()
        o_ref[...] = jnp.where(masks[0], carry[0][1], carry[1][1]).astype(BF16)

    ntri = nq * (nq + 1) // 2
    return pl.pallas_call(
        body, name="sb_fwd", grid=(N_HEADS // 2, nq), in_specs=_qkv_specs(s, 3 * ATTN_W // LANES),
        out_specs=[_pair_spec(), ANY],
        out_shape=[jax.ShapeDtypeStruct((s, ATTN_W), BF16), jax.ShapeDtypeStruct((N_HEADS, ntri, 2, TQ, TQ), BF16)],
        scratch_shapes=[pltpu.VMEM((2, TILE_SLOTS, 2, TQ, TQ), BF16), pltpu.SemaphoreType.DMA((2, TILE_SLOTS))],
        compiler_params=_params(("arbitrary", "arbitrary")),
    )(qkv, qkv, qkv)


def _sb_bwd(qkv, tiles, do):
    s = qkv.shape[0]
    nq = s // TQ

    def body(q_ref, k_ref, v_ref, t_ref, do_ref, dq_ref, dk_ref, dv_ref, dk_acc, dv_acc, buf, sems):
        hp, qi = pl.program_id(0), pl.program_id(1)

        @pl.when(qi == 0)
        def _():
            dk_acc[...] = jnp.zeros_like(dk_acc)
            dv_acc[...] = jnp.zeros_like(dv_acc)

        _, masks = _head_masks()
        prefix = _scan_matrix(False)
        q2 = q_ref[...] * jnp.asarray(ATTN_SCALE, BF16)
        do2 = do_ref[...]
        qms = [_pick(masks[e], q2) for e in range(2)]
        doms = [_pick(masks[e], do2) for e in range(2)]
        qts = [jnp.transpose(qms[e].astype(F32)).astype(BF16) for e in range(2)]
        dots = [jnp.transpose(doms[e].astype(F32)).astype(BF16) for e in range(2)]
        base = _tri_base(qi)

        def fetch(e, kb):
            slot = kb % TILE_SLOTS
            return pltpu.make_async_copy(t_ref.at[2 * hp + e, base + kb], buf.at[e, slot], sems.at[e, slot])

        for e in range(2):
            fetch(e, 0).start()

            @pl.when(qi >= 1)
            def _(e=e):
                fetch(e, 1).start()

        def tile(kb, carry, masked, width=1):
            off, span = pl.multiple_of(kb * TQ, TQ), width * TQ
            k2, v2 = k_ref[pl.ds(off, span), :], v_ref[pl.ds(off, span), :]
            new, dk, dv = [], None, None
            for e in range(2):
                gsum, dq = carry[e]
                if not masked:
                    for blk in range(2, 2 + width):
                        @pl.when(kb + blk <= qi)
                        def _(e=e, blk=blk):
                            fetch(e, kb + blk).start()
                for w in range(width):
                    fetch(e, kb + w).wait()
                slots = [(kb + w) % TILE_SLOTS for w in range(width)]
                ab = buf[e, slots[0], 0] if width == 1 else jnp.concatenate([buf[e, sl, 0] for sl in slots], axis=1)
                lbb = buf[e, slots[0], 1] if width == 1 else jnp.concatenate([buf[e, sl, 1] for sl in slots], axis=1)
                beta = jnp.exp(lbb.astype(F32))
                g = ab.astype(F32) * _dot_nt(doms[e], v2)
                before, gsum = _scan_cols(g, prefix, False, gsum)
                dz = g - beta * (g + before)
                if masked:
                    dz = jnp.where(_diag_mask(width, True), dz, 0.0)
                dzb = dz.astype(BF16)
                dk_e = jnp.dot(qts[e], dzb, preferred_element_type=F32)
                dv_e = jnp.dot(dots[e], ab, preferred_element_type=F32)
                dk, dv = (dk_e, dv_e) if e == 0 else (dk + dk_e, dv + dv_e)
                new.append((gsum, dq + jnp.dot(dzb, k2, preferred_element_type=F32)))
            dk_acc[:, pl.ds(off, span)] += dk
            dv_acc[:, pl.ds(off, span)] += dv
            return tuple(new)

        init = (jnp.zeros((TQ, 1), F32), jnp.zeros((TQ, LANES), F32))
        carry = lax.fori_loop(0, qi // 2, lambda j, cr: tile(2 * j, cr, False, 2), (init, init))
        carry = lax.cond(qi % 2 == 1, lambda cr: tile(qi - 1, cr, True, 2), lambda cr: tile(qi, cr, True), carry)
        dq_ref[...] = (jnp.where(masks[0], carry[0][1], carry[1][1]) * ATTN_SCALE).astype(BF16)

        @pl.when(qi == nq - 1)
        def _():
            _write_transposed(dk_acc, dk_ref)
            _write_transposed(dv_acc, dv_ref)

    seq_spec = pl.BlockSpec((s, LANES), lambda hp, qi: (0, hp))
    return pl.pallas_call(
        body, name="sb_bwd", grid=(N_HEADS // 2, nq),
        in_specs=_qkv_specs(s, 3 * ATTN_W // LANES) + [ANY, _pair_spec()],
        out_specs=[_pair_spec(), seq_spec, seq_spec],
        out_shape=[jax.ShapeDtypeStruct((s, ATTN_W), BF16)] * 3,
        scratch_shapes=[pltpu.VMEM((LANES, s), F32), pltpu.VMEM((LANES, s), F32),
                        pltpu.VMEM((2, TILE_SLOTS, 2, TQ, TQ), BF16), pltpu.SemaphoreType.DMA((2, TILE_SLOTS))],
        compiler_params=_params(("arbitrary", "arbitrary")),
    )(qkv, qkv, qkv, tiles, do)


CONV_TR = 256


def _shift_down(x, halo, n):
    rolled = pltpu.roll(x, n, 0)
    head = rolled[0:8, :]
    rid = lax.broadcasted_iota(jnp.int32, head.shape, 0)
    for j in range(n):
        head = jnp.where(rid == j, halo[8 - n + j:8 - n + j + 1, :], head)
    return jnp.concatenate([head, rolled[8:, :]], axis=0)


def _shift_up(x, halo, n):
    rows = x.shape[0]
    rolled = pltpu.roll(x, rows - n, 0)
    tail = rolled[rows - 8:, :]
    rid = lax.broadcasted_iota(jnp.int32, tail.shape, 0)
    for j in range(n):
        tail = jnp.where(rid == 8 - n + j, halo[j:j + 1, :], tail)
    return jnp.concatenate([rolled[:rows - 8, :], tail], axis=0)


def _conv_fwd_block(x, halo, w, b):
    return b + _shift_down(x, halo, 2) * w[0:1, :] + _shift_down(x, halo, 1) * w[1:2, :] + x * w[2:3, :]


def _conv_specs(tr, s):
    pair = 2 * FF_HALF
    blk = pl.BlockSpec((tr, pair), lambda j, i: (i, j))
    prev = pl.BlockSpec((8, pair), lambda j, i: (jnp.maximum(i * (tr // 8) - 1, 0), j))
    nxt = pl.BlockSpec((8, pair), lambda j, i: (jnp.minimum((i + 1) * (tr // 8), s // 8 - 1), j))
    return blk, prev, nxt


def _conv_gate_fwd(hpre, conv_w, conv_b):
    s = hpre.shape[0]
    tr = min(CONV_TR, s)
    blk, prev, _ = _conv_specs(tr, s)

    def body(x_ref, halo_ref, w_ref, b_ref, a_ref):
        i = pl.program_id(1)
        halo = jnp.where(i > 0, halo_ref[...], 0.0)
        h = _conv_fwd_block(x_ref[...], halo, w_ref[...], b_ref[...])
        hg, hv = h[:, :FF_HALF], h[:, FF_HALF:]
        a_ref[...] = (hg * _sigmoid(hg) * hv).astype(BF16)

    return pl.pallas_call(
        body, name="conv_gate_fwd", grid=(2, s // tr),
        in_specs=[blk, prev, pl.BlockSpec((3, 2 * FF_HALF), lambda j, i: (0, j)),
                  pl.BlockSpec((1, 2 * FF_HALF), lambda j, i: (0, j))],
        out_specs=pl.BlockSpec((tr, FF_HALF), lambda j, i: (i, j)),
        out_shape=jax.ShapeDtypeStruct((s, D_FF), BF16),
        compiler_params=_params(("parallel", "parallel")),
    )(hpre, hpre, conv_w, conv_b)


def _conv_gate_bwd(hpre, da, conv_w, conv_b):
    s = hpre.shape[0]
    tr = min(CONV_TR, s)
    blk, prev, _ = _conv_specs(tr, s)

    def body(x_ref, halo_ref, da_ref, w_ref, b_ref, dh_ref, db_ref, dw_ref):
        i = pl.program_id(1)
        halo = jnp.where(i > 0, halo_ref[...], 0.0)
        x = x_ref[...]
        h = _conv_fwd_block(x, halo, w_ref[...], b_ref[...])
        hg, hv = h[:, :FF_HALF], h[:, FF_HALF:]
        da_blk = da_ref[...].astype(F32)
        sg = _sigmoid(hg)
        dhg = da_blk * hv * (sg * (1.0 + hg * (1.0 - sg)))
        dhv = da_blk * (hg * sg)
        dh_ref[:, :FF_HALF] = dhg.astype(BF16)
        dh_ref[:, FF_HALF:] = dhv.astype(BF16)
        x2, x1 = _shift_down(x, halo, 2), _shift_down(x, halo, 1)
        parts = []
        for lo, dpart in ((0, dhg), (FF_HALF, dhv)):
            cols = slice(lo, lo + FF_HALF)
            parts.append((cols, _colsum(dpart), _colsum(dpart * x2[:, cols]), _colsum(dpart * x1[:, cols]),
                          _colsum(dpart * x[:, cols])))

        @pl.when(i == 0)
        def _():
            for cols, db, dw0, dw1, dw2 in parts:
                db_ref[:, cols] = db
                dw_ref[0:1, cols] = dw0
                dw_ref[1:2, cols] = dw1
                dw_ref[2:3, cols] = dw2

        @pl.when(i > 0)
        def _():
            for cols, db, dw0, dw1, dw2 in parts:
                db_ref[:, cols] += db
                dw_ref[0:1, cols] += dw0
                dw_ref[1:2, cols] += dw1
                dw_ref[2:3, cols] += dw2

    pair = 2 * FF_HALF
    return pl.pallas_call(
        body, name="conv_gate_bwd", grid=(2, s // tr),
        in_specs=[blk, prev, pl.BlockSpec((tr, FF_HALF), lambda j, i: (i, j)),
                  pl.BlockSpec((3, pair), lambda j, i: (0, j)), pl.BlockSpec((1, pair), lambda j, i: (0, j))],
        out_specs=[blk, pl.BlockSpec((1, pair), lambda j, i: (0, j)), pl.BlockSpec((3, pair), lambda j, i: (0, j))],
        out_shape=[jax.ShapeDtypeStruct((s, 2 * D_FF), BF16), jax.ShapeDtypeStruct((1, 2 * D_FF), F32),
                   jax.ShapeDtypeStruct((3, 2 * D_FF), F32)],
        compiler_params=_params(("parallel", "arbitrary")),
    )(hpre, hpre, da, conv_w, conv_b)


def _conv_input_bwd(dh, conv_w):
    s = dh.shape[0]
    tr = min(CONV_TR, s)
    blk, _, _ = _conv_specs(tr, s)
    nblk = s // tr

    def body(x_ref, halo_ref, w_ref, o_ref):
        i = pl.program_id(1)
        halo = jnp.where(i < nblk - 1, halo_ref[...].astype(F32), 0.0)
        x, w = x_ref[...].astype(F32), w_ref[...]
        o_ref[...] = (x * w[2:3, :] + _shift_up(x, halo, 1) * w[1:2, :] + _shift_up(x, halo, 2) * w[0:1, :]).astype(BF16)

    nxt = pl.BlockSpec((16, 2 * FF_HALF), lambda j, i: (jnp.minimum((i + 1) * (tr // 16), s // 16 - 1), j))
    return pl.pallas_call(
        body, name="conv_input_bwd", grid=(2, nblk),
        in_specs=[blk, nxt, pl.BlockSpec((3, 2 * FF_HALF), lambda j, i: (0, j))], out_specs=blk,
        out_shape=jax.ShapeDtypeStruct((s, 2 * D_FF), BF16),
        compiler_params=_params(("parallel", "parallel")),
    )(dh, dh, conv_w)


def _adamw_math(w, g, m, v):
    m = ADAM_B1 * m + (1.0 - ADAM_B1) * g
    v = ADAM_B2 * v + (1.0 - ADAM_B2) * (g * g)
    m_hat = m / (1.0 - ADAM_B1 ** ADAM_STEP)
    v_hat = v / (1.0 - ADAM_B2 ** ADAM_STEP)
    delta = -ADAM_LR * (m_hat / (jnp.sqrt(v_hat) + ADAM_EPS) + ADAM_WD * w)
    return delta, m, v


def _adamw(name, g8, w, m, v):
    r, c = w.shape
    tr = _row_tile(r, c)

    def body(g_ref, w_ref, m_ref, v_ref, go_ref, d_ref, mo_ref, vo_ref):
        g = g_ref[0].astype(F32)
        for d in range(1, N_DEV):
            g = g + g_ref[d].astype(F32)
        delta, mn, vn = _adamw_math(w_ref[...], g, m_ref[...], v_ref[...])
        go_ref[...] = g
        d_ref[...] = delta
        mo_ref[...] = mn
        vo_ref[...] = vn

    spec = pl.BlockSpec((tr, c), lambda i: (i, 0))
    return pl.pallas_call(
        body, name=name, grid=(r // tr,),
        in_specs=[pl.BlockSpec((N_DEV, tr, c), lambda i: (0, i, 0)), spec, spec, spec], out_specs=[spec] * 4,
        out_shape=[jax.ShapeDtypeStruct((r, c), F32)] * 4, compiler_params=_params(("parallel",)),
    )(g8, w, m, v)


def _adamw_ada(c_t, dmod, w, m, v):
    r, c = w.shape
    tr = _row_tile(r, c)

    def body(ct_ref, dm_ref, w_ref, m_ref, v_ref, go_ref, d_ref, mo_ref, vo_ref):
        ct, dm = ct_ref[...], dm_ref[...]
        g = ct[:, 0:1] * dm[0:1, :]
        for b in range(1, N_DEV):
            g = g + ct[:, b:b + 1] * dm[b:b + 1, :]
        delta, mn, vn = _adamw_math(w_ref[...], g, m_ref[...], v_ref[...])
        go_ref[...] = g
        d_ref[...] = delta
        mo_ref[...] = mn
        vo_ref[...] = vn

    spec = pl.BlockSpec((tr, c), lambda i: (i, 0))
    return pl.pallas_call(
        body, name="adamw_w_ada", grid=(r // tr,),
        in_specs=[pl.BlockSpec((tr, N_DEV), lambda i: (i, 0)), pl.BlockSpec((N_DEV, c), lambda i: (0, 0)),
                  spec, spec, spec],
        out_specs=[spec] * 4, out_shape=[jax.ShapeDtypeStruct((r, c), F32)] * 4,
        compiler_params=_params(("parallel",)),
    )(c_t, dmod, w, m, v)


def _cols_from_slots(g):
    n, r, c = g.shape
    return jnp.transpose(g, (1, 0, 2)).reshape(r, n * c)


def _cols_to_slots(w):
    r, c = w.shape
    return jnp.transpose(w.reshape(r, N_DEV, c // N_DEV), (1, 0, 2))


def _pair_cols(w):
    g0, g1 = w[..., 0:FF_HALF], w[..., FF_HALF:D_FF]
    v0, v1 = w[..., D_FF:D_FF + FF_HALF], w[..., D_FF + FF_HALF:]
    return jnp.concatenate([g0, v0, g1, v1], axis=-1)


def _unpair_cols(w):
    g0, v0 = w[..., 0:FF_HALF], w[..., FF_HALF:D_FF]
    g1, v1 = w[..., D_FF:D_FF + FF_HALF], w[..., D_FF + FF_HALF:]
    return jnp.concatenate([g0, g1, v0, v1], axis=-1)


def _row(v):
    return v.reshape(1, -1)


def kernel(x, c, w_ada, b_ada, w_in, b_forget, w_fox_proj, w_sb_proj, w_o, ln1_g, ln1_b, w_up, conv_w, conv_b, w_down, ln2_g, ln2_b, loss_target, m_w_ada, m_b_ada, m_w_in, m_b_forget, m_w_fox_proj, m_w_sb_proj, m_w_o, m_ln1_g, m_ln1_b, m_w_up, m_conv_w, m_conv_b, m_w_down, m_ln2_g, m_ln2_b, v_w_ada, v_b_ada, v_w_in, v_b_forget, v_w_fox_proj, v_w_sb_proj, v_w_o, v_ln1_g, v_ln1_b, v_w_up, v_conv_w, v_conv_b, v_w_down, v_ln2_g, v_ln2_b):
    s = x.shape[1]
    me = 4 * lax.axis_index("x") + 2 * lax.axis_index("y") + lax.axis_index("c")
    x2 = x.reshape(s, D_MODEL)
    tgt = loss_target.reshape(s, D_MODEL)

    b_ada_loc = lax.dynamic_slice(b_ada, (me * ADA_SHARD,), (ADA_SHARD,)).reshape(1, ADA_SHARD)
    c_all, mod = _mod_exchange(c, w_ada, b_ada_loc)
    mod = mod.reshape(N_MOD, 1, D_MODEL)
    sh1, sc1, gt1, sh2, sc2, gt2 = [mod[i] for i in range(N_MOD)]

    g_in = _allgather_two_level("ag_w_in", w_in.astype(BF16))
    late_weights = _Ride([w_fox_proj.astype(BF16), w_sb_proj.astype(BF16), w_o.astype(BF16), w_up.astype(BF16),
                          w_down.astype(BF16), conv_w], scatter=False)
    w_in_f = _cols_from_slots(g_in)
    w_proj = jnp.concatenate(
        [w_in_f[:, 0:1536], w_in_f[:, 1544:3080], w_in_f[:, 3080:5128], w_in_f[:, 1536:1544],
         jnp.zeros((D_MODEL, W_PROJ - 5128), BF16)], axis=1)
    w_qkv, w_gates, w_f = w_proj[:, :W_QKV], w_proj[:, W_QKV:W_QKV + W_GATES], w_proj[:, W_QKV + W_GATES:W_QKV + W_GATES + W_F]
    conv_b_p = _pair_cols(_row(conv_b))
    b_f_pad = jnp.pad(_row(b_forget), ((0, 0), (0, LANES - N_HEADS)))

    (u1,) = _rowwise("modulate1", lambda xb, sc, sh: (xb * (1.0 + sc) + sh,),
                     [(x2, D_MODEL, 0)], [sc1, sh1], [(D_MODEL, BF16)], tr=512)
    qkv = _mm(u1, w_qkv, name="mm_qkv", out_dtype=BF16)
    gates = _mm(u1, w_gates, name="mm_gates")
    f_raw = _mm(u1, w_f, name="mm_forget")
    cum_col = _forget_cumsum(f_raw, b_f_pad)
    cum_row = jnp.transpose(cum_col[:, :N_HEADS]).reshape(N_HEADS // 2, 2, s)
    (y_fox, y_fox32, lse), (g_fox, g_sb, g_o, g_up, g_down, g_cw) = _fox_fwd(qkv, cum_col, cum_row, ride=late_weights)
    w_fox_f = _cols_from_slots(g_fox)
    w_sb_f = _cols_from_slots(g_sb)
    w_o_f = g_o.reshape(D_MODEL, D_MODEL)
    w_up_p = _pair_cols(_cols_from_slots(g_up))
    w_down_f = g_down.reshape(D_FF, D_MODEL)
    conv_w_p = _pair_cols(_cols_from_slots(g_cw))
    y_sb, sb_run = _sb_fwd(qkv)
    pf = _mm(y_fox, w_fox_f, name="mm_fox_proj", out_dtype=BF16)
    ps = _mm(y_sb, w_sb_f, name="mm_sb_proj", out_dtype=BF16)
    (merged,) = _rowwise("gate_merge", lambda ga, gb, a, b: (_sigmoid(ga) * a + _sigmoid(gb) * b,),
                         [(gates, D_MODEL, 0), (gates, D_MODEL, 1), (pf, D_MODEL, 0), (ps, D_MODEL, 0)], [],
                         [(D_MODEL, BF16)])
    attn_out = _mm(merged, w_o_f, name="mm_w_o")

    def ln_fwd(xb, fb, gt, g, b):
        xhat, _ = _ln_stats(ALPHA * xb + (1.0 + gt) * fb)
        return xhat * g + b

    def ln1_mod(xb, fb, gt, g, b, sc, sh):
        y = ln_fwd(xb, fb, gt, g, b)
        return y, y * (1.0 + sc) + sh

    x1, u2 = _rowwise("ln1_modulate2", ln1_mod, [(x2, D_MODEL, 0), (attn_out, D_MODEL, 0)],
                      [gt1, _row(ln1_g), _row(ln1_b), sc2, sh2], [(D_MODEL, F32), (D_MODEL, BF16)])

    hpre = _mm(u2, w_up_p, name="mm_w_up", tn=1408)
    act = _conv_gate_fwd(hpre, conv_w_p, conv_b_p)
    ffn_out = _mm(act, w_down_f, name="mm_w_down", tk=2816)

    def ln2_bwd(xb, fb, tb, gt, g, b):
        xhat, rstd = _ln_stats(ALPHA * xb + (1.0 + gt) * fb)
        err = (xhat * g + b) - tb
        dy = err * (1.0 / D_MODEL)
        dr = _ln_bwd(dy, xhat, rstd, g)
        return (dr * (1.0 + gt), ALPHA * dr,
                _colsum(err * err), _colsum(dy * xhat), _colsum(dy), _colsum(dr * fb))

    dffn, dx1_res, sq_err, d_ln2_g, d_ln2_b, d_gt2 = _rowwise(
        "ln2_bwd", ln2_bwd, [(x1, D_MODEL, 0), (ffn_out, D_MODEL, 0), (tgt, D_MODEL, 0)],
        [gt2, _row(ln2_g), _row(ln2_b)], [(D_MODEL, BF16), (D_MODEL, F32)], sums=[D_MODEL] * 4)
    loss = lax.psum(0.5 * jnp.sum(sq_err) / D_MODEL, ("x", "y", "c"))

    d_w_down = _mm(act, dffn, name="mm_d_w_down", ta=True, tm=1408, tk=2048, out_dtype=BF16)
    d_act = _mm(dffn, w_down_f, name="mm_d_act", tb=True, tn=1408, out_dtype=BF16)
    dh, d_conv_b_p, d_conv_w_p = _conv_gate_bwd(hpre, d_act, conv_w_p, conv_b_p)
    dhpre = _conv_input_bwd(dh, conv_w_p)
    d_w_up_p = _mm(u2, dhpre, name="mm_d_w_up", ta=True, tn=1408, tk=2048, out_dtype=BF16)
    du2 = _mm(dhpre, w_up_p, name="mm_d_u2", tb=True, tk=2816)

    def ln1_bwd(du, dres, x1b, xb, fb, sc, gt, g):
        dx1 = dres + du * (1.0 + sc)
        xhat, rstd = _ln_stats(ALPHA * xb + (1.0 + gt) * fb)
        dr = _ln_bwd(dx1, xhat, rstd, g)
        return (dr * (1.0 + gt), ALPHA * dr,
                _colsum(du * x1b), _colsum(du), _colsum(dx1 * xhat), _colsum(dx1), _colsum(dr * fb))

    d_attn, dx_res, d_sc2, d_sh2, d_ln1_g, d_ln1_b, d_gt1 = _rowwise(
        "ln1_bwd", ln1_bwd,
        [(du2, D_MODEL, 0), (dx1_res, D_MODEL, 0), (x1, D_MODEL, 0), (x2, D_MODEL, 0), (attn_out, D_MODEL, 0)],
        [sc2, gt1, _row(ln1_g)], [(D_MODEL, BF16), (D_MODEL, F32)], sums=[D_MODEL] * 5)

    d_w_o = _mm(merged, d_attn, name="mm_d_w_o", ta=True, out_dtype=BF16)
    d_merged = _mm(d_attn, w_o_f, name="mm_d_merged", tb=True, out_dtype=BF16)

    def merge_bwd(dm, ga, gb, a, b):
        dm, a, b = dm.astype(F32), a.astype(F32), b.astype(F32)
        sa, sb = _sigmoid(ga), _sigmoid(gb)
        return dm * a * sa * (1.0 - sa), dm * b * sb * (1.0 - sb), dm * sa, dm * sb

    d_ga, d_gb, d_pf, d_ps = _rowwise(
        "gate_merge_bwd", merge_bwd,
        [(d_merged, D_MODEL, 0), (gates, D_MODEL, 0), (gates, D_MODEL, 1), (pf, D_MODEL, 0), (ps, D_MODEL, 0)], [],
        [(D_MODEL, BF16)] * 4)
    d_w_fox = _mm(y_fox, d_pf, name="mm_d_w_fox", ta=True, out_dtype=BF16)
    d_w_sb = _mm(y_sb, d_ps, name="mm_d_w_sb", ta=True, out_dtype=BF16)
    d_y_fox = _mm(d_pf, w_fox_f, name="mm_d_y_fox", tb=True, out_dtype=BF16)
    d_y_sb = _mm(d_ps, w_sb_f, name="mm_d_y_sb", tb=True, out_dtype=BF16)
    early_grads = _Ride(
        [_cols_to_slots(d_w_fox), _cols_to_slots(d_w_sb), d_w_o.reshape(N_DEV, D_MODEL // N_DEV, D_MODEL),
         _cols_to_slots(_unpair_cols(d_w_up_p)), d_w_down.reshape(N_DEV, D_FF // N_DEV, D_MODEL)], scatter=True)
    (dq_a, dk_a, dv_a, d_cum_row), early_slots = _fox_bwd(qkv, cum_col, cum_row, y_fox32, lse, d_y_fox,
                                                                    ride=early_grads)
    dq_b, dk_b, dv_b = _sb_bwd(qkv, sb_run, d_y_sb)
    d_cum = jnp.transpose(d_cum_row.reshape(N_HEADS, s))
    d_cum = jnp.pad(d_cum, ((0, 0), (0, LANES - N_HEADS)))
    d_f, d_b_forget = _forget_bwd(d_cum, f_raw, b_f_pad)
    d_proj = jnp.concatenate([dq_a, dk_a, dv_a, dq_b, dk_b, dv_b, d_ga, d_gb, d_f,
                              jnp.zeros((s, W_PROJ - W_QKV - W_GATES - W_F), BF16)], axis=1)
    d_w_proj = _mm(u1, d_proj, name="mm_d_w_in", ta=True, tn=896, tk=2048, out_dtype=BF16)
    d_w_in_f = jnp.concatenate([d_w_proj[:, 0:1536], d_w_proj[:, 5120:5128], d_w_proj[:, 1536:3072],
                                d_w_proj[:, 3072:5120]], axis=1)
    du1, (in_slots,) = _mm(d_proj, w_proj, name="mm_d_u1", tb=True, tk=2688,
                           ride=_Ride([_cols_to_slots(d_w_in_f)], scatter=True))

    def x_bwd(du, dres, xb, sc):
        return dres + du * (1.0 + sc), _colsum(du * xb), _colsum(du)

    grad_x, d_sc1, d_sh1 = _rowwise("x_bwd", x_bwd, [(du1, D_MODEL, 0), (dx_res, D_MODEL, 0), (x2, D_MODEL, 0)],
                                    [sc1], [(D_MODEL, F32)], sums=[D_MODEL] * 2, tr=512)

    d_conv_b = _unpair_cols(d_conv_b_p)
    d_conv_w = _unpair_cols(d_conv_w_p)
    cb_pad = N_MOD * D_MODEL - 2 * D_FF
    n_rep = N_MOD * D_MODEL + D_MODEL + 4 * D_MODEL + 2 * D_FF + cb_pad
    small = jnp.concatenate(
        [d_sh1, d_sc1, d_gt1, d_sh2, d_sc2, d_gt2, jnp.pad(d_b_forget, ((0, 0), (0, D_MODEL - LANES))),
         d_ln1_g, d_ln1_b, d_ln2_g, d_ln2_b, jnp.pad(d_conv_b, ((0, 0), (0, cb_pad))),
         d_conv_w.reshape(1, 6 * D_FF)], axis=1)
    n_small = small.shape[1] // LANES
    small = jnp.pad(small.reshape(n_small, LANES), ((0, -n_small % 8), (0, 0)))
    (small_all,) = _exchange("ag_small_grads", [small], scatter=False)
    rep8 = small_all[:, :n_rep // LANES, :]
    cw8 = small_all[:, n_rep // LANES:n_small, :].reshape(N_DEV, 3, 2 * D_FF)
    cw8 = lax.dynamic_slice(cw8, (0, 0, me * UP_SHARD), (N_DEV, 3, UP_SHARD))
    dmod8 = small_all[:, :N_MOD * D_MODEL // LANES, :].reshape(N_DEV, N_MOD * D_MODEL)
    dmod_loc = lax.dynamic_slice(dmod8, (0, me * ADA_SHARD), (N_DEV, ADA_SHARD))

    def pack_rep(b_a, b_f, g1, b1, g2, b2, cb):
        flat = jnp.concatenate([b_a, jnp.pad(b_f, (0, D_MODEL - N_HEADS)), g1, b1, g2, b2, jnp.pad(cb, (0, cb_pad))])
        return flat.reshape(n_rep // LANES, LANES)

    rep = _adamw("adamw_small", rep8, pack_rep(b_ada, b_forget, ln1_g, ln1_b, ln2_g, ln2_b, conv_b),
                 pack_rep(m_b_ada, m_b_forget, m_ln1_g, m_ln1_b, m_ln2_g, m_ln2_b, m_conv_b),
                 pack_rep(v_b_ada, v_b_forget, v_ln1_g, v_ln1_b, v_ln2_g, v_ln2_b, v_conv_b))

    def unpack_rep(p):
        flat = p.reshape(-1)
        o = N_MOD * D_MODEL
        d = D_MODEL
        return {"b_ada": flat[:o], "b_forget": flat[o:o + N_HEADS],
                "ln1_g": flat[o + d:o + 2 * d], "ln1_b": flat[o + 2 * d:o + 3 * d],
                "ln2_g": flat[o + 3 * d:o + 4 * d], "ln2_b": flat[o + 4 * d:o + 5 * d],
                "conv_b": flat[o + 5 * d:o + 5 * d + 2 * D_FF]}

    rep = [unpack_rep(p) for p in rep]
    r_conv_w = _adamw("adamw_conv_w", cw8, conv_w, m_conv_w, v_conv_w)
    r_ada = _adamw_ada(jnp.transpose(c_all.reshape(N_DEV, D_MODEL)), dmod_loc, w_ada, m_w_ada, v_w_ada)

    r_in = _adamw("adamw_w_in", in_slots, w_in, m_w_in, v_w_in)
    r_fox = _adamw("adamw_w_fox", early_slots[0], w_fox_proj, m_w_fox_proj, v_w_fox_proj)
    r_sb = _adamw("adamw_w_sb", early_slots[1], w_sb_proj, m_w_sb_proj, v_w_sb_proj)
    r_o = _adamw("adamw_w_o", early_slots[2], w_o, m_w_o, v_w_o)
    r_up = _adamw("adamw_w_up", early_slots[3], w_up, m_w_up, v_w_up)
    r_down = _adamw("adamw_w_down", early_slots[4], w_down, m_w_down, v_w_down)

    def leaf(i):
        return [r_ada[i], rep[i]["b_ada"], r_in[i], rep[i]["b_forget"], r_fox[i], r_sb[i], r_o[i], rep[i]["ln1_g"],
                rep[i]["ln1_b"], r_up[i], r_conv_w[i], rep[i]["conv_b"], r_down[i], rep[i]["ln2_g"], rep[i]["ln2_b"]]

    return (loss, grad_x.reshape(1, s, D_MODEL), *leaf(0), *leaf(1), *leaf(2), *leaf(3))
```

```python
import functools

import jax
import jax.numpy as jnp
from jax import lax
from jax.experimental import pallas as pl
from jax.experimental.pallas import tpu as pltpu

F32 = jnp.float32
BF16 = jnp.bfloat16
MESH = pl.DeviceIdType.MESH
ANY = pl.BlockSpec(memory_space=pl.ANY)

N_DEV = 8
D_MODEL = 1024
HEAD_DIM = 64
N_HEADS = 8
ATTN_W = N_HEADS * HEAD_DIM
D_FF = 2816
FF_HALF = D_FF // 2
N_MOD = 6
ADA_SHARD = N_MOD * D_MODEL // N_DEV
IN_SHARD = 641
UP_SHARD = 704
ATTN_SCALE = HEAD_DIM ** -0.5
ALPHA = 2.0 ** 0.25
LN_EPS = 1e-5
LANES = 128
TQ = 512
SCAN_W = 256
VMEM_LIMIT = 56 * 1024 * 1024

ADAM_LR, ADAM_B1, ADAM_B2, ADAM_EPS, ADAM_WD, ADAM_STEP = 0.001, 0.9, 0.999, 1e-08, 0.01, 10

W_QKV, W_GATES, W_F = 3072, 2048, 128
W_PROJ = 5376


def _params(sem=None):
    return pltpu.CompilerParams(dimension_semantics=sem, vmem_limit_bytes=VMEM_LIMIT)


def _tile(n, cap):
    if n <= cap:
        return n
    best = None
    for t in range(LANES, cap + 1, LANES):
        if n % t == 0:
            best = t
    assert best is not None, (n, cap)
    return best


def _row_tile(r, width, budget=192 * 1024):
    if r * width <= budget or r % 16:
        return r
    best = 16
    for t in range(16, r + 1, 16):
        if r % t == 0 and t * width <= budget:
            best = t
    return best


def _me():
    x, y, c = lax.axis_index("x"), lax.axis_index("y"), lax.axis_index("c")
    return x, y, c, 4 * x + 2 * y + c


def _peer(r):
    x, y, c, _ = _me()
    px = 1 - x if r & 4 else x
    py = 1 - y if r & 2 else y
    pc = 1 - c if r & 1 else c
    return (px, py, pc), 4 * px + 2 * py + pc


class _Ride:
    def __init__(self, arrays, scatter):
        self.arrays, self.scatter, self.n = list(arrays), scatter, len(arrays)
        self.in_specs = [ANY] * self.n
        self.out_specs = [ANY] * self.n
        self.out_shape = [jax.ShapeDtypeStruct(a.shape if scatter else (N_DEV,) + a.shape, a.dtype) for a in arrays]
        self.scratch = [pltpu.SemaphoreType.DMA((self.n, N_DEV - 1)), pltpu.SemaphoreType.DMA((self.n, N_DEV - 1)),
                        pltpu.SemaphoreType.DMA((self.n,))]

    def _local(self, ins, outs, sems, a):
        me = _me()[3]
        return pltpu.make_async_copy(ins[a].at[me] if self.scatter else ins[a], outs[a].at[me], sems[2].at[a])

    def _remote(self, ins, outs, sems, a, r, arriving):
        me = _me()[3]
        peer, pidx = _peer(r)
        src = ins[a].at[me if arriving else pidx] if self.scatter else ins[a]
        return pltpu.make_async_remote_copy(
            src_ref=src, dst_ref=outs[a].at[pidx if arriving else me], send_sem=sems[0].at[a, r - 1],
            recv_sem=sems[1].at[a, r - 1], device_id=peer, device_id_type=MESH)

    def start(self, ins, outs, sems):
        for a in range(self.n):
            self._local(ins, outs, sems, a).start()
        for r in range(1, N_DEV):
            for a in range(self.n):
                self._remote(ins, outs, sems, a, r, False).start()

    def wait(self, ins, outs, sems):
        for r in range(1, N_DEV):
            for a in range(self.n):
                self._remote(ins, outs, sems, a, r, True).wait_recv()
        for r in range(1, N_DEV):
            for a in range(self.n):
                self._remote(ins, outs, sems, a, r, False).wait_send()
        for a in range(self.n):
            self._local(ins, outs, sems, a).wait()


def _exchange(name, arrays, scatter):
    ride = _Ride(arrays, scatter)

    def body(*refs):
        ins, outs, sems = refs[:ride.n], refs[ride.n:2 * ride.n], refs[2 * ride.n:]
        ride.start(ins, outs, sems)
        ride.wait(ins, outs, sems)

    return pl.pallas_call(body, name=name, in_specs=ride.in_specs, out_specs=ride.out_specs, out_shape=ride.out_shape,
                          scratch_shapes=ride.scratch)(*arrays)


def _allgather_two_level(name, a):
    def body(a_ref, out_ref, send_sems, recv_sems, local_sem):
        x, y, c, me = _me()
        sibling = (x, y, 1 - c)
        chips = [(1 - x, y), (x, 1 - y), (1 - x, 1 - y)]

        def idx(px, py, pc):
            return 4 * px + 2 * py + pc

        def copy(k, block, to, src=None):
            slot = out_ref.at[idx(*block)]
            return pltpu.make_async_remote_copy(
                src_ref=slot if src is None else src, dst_ref=slot, send_sem=send_sems.at[k], recv_sem=recv_sems.at[k],
                device_id=to, device_id_type=MESH)

        mine = pltpu.make_async_copy(a_ref, out_ref.at[me], local_sem)
        mine.start()
        first = [copy(0, (x, y, c), sibling, src=a_ref)]
        first += [copy(1 + j, (x, y, c), (*chip, c), src=a_ref) for j, chip in enumerate(chips)]
        for cp in first:
            cp.start()
        passed = [copy(4 + j, (*chip, c), sibling) for j, chip in enumerate(chips)]
        for j, chip in enumerate(chips):
            copy(1 + j, (*chip, c), (x, y, c)).wait_recv()
            passed[j].start()
        copy(0, sibling, (x, y, c)).wait_recv()
        for j, chip in enumerate(chips):
            copy(4 + j, (*chip, 1 - c), (x, y, c)).wait_recv()
        for cp in first + passed:
            cp.wait_send()
        mine.wait()

    return pl.pallas_call(
        body, name=name, in_specs=[ANY], out_specs=ANY,
        out_shape=jax.ShapeDtypeStruct((N_DEV,) + a.shape, a.dtype),
        scratch_shapes=[pltpu.SemaphoreType.DMA((N_DEV - 1,)), pltpu.SemaphoreType.DMA((N_DEV - 1,)),
                        pltpu.SemaphoreType.DMA],
    )(a)


def _with_ride(body, ride, n_in, n_out, grid):
    if ride is None:
        return body
    n = ride.n

    def wrapped(*refs):
        ins, rins = refs[:n_in], refs[n_in:n_in + n]
        outs, routs = refs[n_in + n:n_in + n + n_out], refs[n_in + n + n_out:n_in + 2 * n + n_out]
        rest = refs[n_in + 2 * n + n_out:]
        scratch, sems = rest[:len(rest) - 3], rest[len(rest) - 3:]
        ids = [pl.program_id(d) for d in range(len(grid))]
        first = functools.reduce(lambda p, q: p & q, [i == 0 for i in ids])
        last = functools.reduce(lambda p, q: p & q, [i == g - 1 for i, g in zip(ids, grid)])

        @pl.when(first)
        def _():
            ride.start(rins, routs, sems)

        body(*ins, *outs, *scratch)

        @pl.when(last)
        def _():
            ride.wait(rins, routs, sems)

    return wrapped


def _ride_call(body, ride, *, name, grid, in_specs, out_specs, out_shape, scratch_shapes, sem, args):
    n_in, n_out = len(in_specs), len(out_specs)
    if ride is None:
        res = pl.pallas_call(body, name=name, grid=grid, in_specs=in_specs, out_specs=out_specs, out_shape=out_shape,
                             scratch_shapes=scratch_shapes, compiler_params=_params(sem))(*args)
        return list(res), []
    res = pl.pallas_call(
        _with_ride(body, ride, n_in, n_out, grid), name=name, grid=grid,
        in_specs=list(in_specs) + ride.in_specs, out_specs=list(out_specs) + ride.out_specs,
        out_shape=list(out_shape) + ride.out_shape, scratch_shapes=list(scratch_shapes) + ride.scratch,
        compiler_params=_params(("arbitrary",) * len(grid)))(*args, *ride.arrays)
    return list(res[:n_out]), list(res[n_out:])


def _mm(a, b, *, name, ta=False, tb=False, out_dtype=F32, tm=1024, tn=1024, tk=1024, ride=None):
    m, k = (a.shape[1], a.shape[0]) if ta else a.shape
    n = b.shape[0] if tb else b.shape[1]
    assert (b.shape[1] if tb else b.shape[0]) == k
    tm, tn, tk = _tile(m, tm), _tile(n, tn), _tile(k, tk)
    nk = k // tk
    a_spec = pl.BlockSpec((tk, tm), lambda i, j, l: (l, i)) if ta else pl.BlockSpec((tm, tk), lambda i, j, l: (i, l))
    b_spec = pl.BlockSpec((tn, tk), lambda i, j, l: (j, l)) if tb else pl.BlockSpec((tk, tn), lambda i, j, l: (l, j))
    dims = (((0,) if ta else (1,), (1,) if tb else (0,)), ((), ()))

    def body(a_ref, b_ref, o_ref, *acc):
        p = lax.dot_general(a_ref[...].astype(BF16), b_ref[...].astype(BF16), dims, preferred_element_type=F32)
        if nk == 1:
            o_ref[...] = p.astype(out_dtype)
            return
        acc_ref = acc[0]
        step = pl.program_id(2)

        @pl.when(step == 0)
        def _():
            acc_ref[...] = p

        @pl.when(step > 0)
        def _():
            acc_ref[...] += p

        @pl.when(step == nk - 1)
        def _():
            o_ref[...] = acc_ref[...].astype(out_dtype)

    outs, rode = _ride_call(
        body, ride, name=name, grid=(m // tm, n // tn, nk), in_specs=[a_spec, b_spec],
        out_specs=[pl.BlockSpec((tm, tn), lambda i, j, l: (i, j))], out_shape=[jax.ShapeDtypeStruct((m, n), out_dtype)],
        scratch_shapes=[] if nk == 1 else [pltpu.VMEM((tm, tn), F32)], sem=("parallel", "parallel", "arbitrary"),
        args=(a, b))
    return outs[0] if ride is None else (outs[0], rode)


def _rowwise(name, fn, rows, vecs, outs, sums=(), tr=512):
    s = rows[0][0].shape[0]
    tr = min(tr, s)
    nr, nv, no = len(rows), len(vecs), len(outs)

    def body(*refs):
        vals = [r[...] for r in refs[:nr + nv]]
        res = fn(*vals)
        for o_ref, val in zip(refs[nr + nv:nr + nv + no], res[:no]):
            o_ref[...] = val.astype(o_ref.dtype)
        step = pl.program_id(0)
        for s_ref, val in zip(refs[nr + nv + no:], res[no:]):
            @pl.when(step == 0)
            def _(s_ref=s_ref, val=val):
                s_ref[...] = val

            @pl.when(step > 0)
            def _(s_ref=s_ref, val=val):
                s_ref[...] += val

    in_specs = [pl.BlockSpec((tr, w), functools.partial(lambda i, cb: (i, cb), cb=cb)) for _, w, cb in rows]
    in_specs += [pl.BlockSpec(v.shape, lambda i: (0, 0)) for v in vecs]
    out_specs = [pl.BlockSpec((tr, w), lambda i: (i, 0)) for w, _ in outs]
    out_specs += [pl.BlockSpec((1, w), lambda i: (0, 0)) for w in sums]
    out_shape = [jax.ShapeDtypeStruct((s, w), dt) for w, dt in outs]
    out_shape += [jax.ShapeDtypeStruct((1, w), F32) for w in sums]
    return pl.pallas_call(
        body, name=name, grid=(s // tr,), in_specs=in_specs, out_specs=out_specs, out_shape=out_shape,
        compiler_params=_params(("arbitrary",) if sums else ("parallel",)),
    )(*[r[0] for r in rows], *vecs)


def _colsum(x):
    return jnp.sum(x, axis=0, keepdims=True)


def _sigmoid(x):
    return 1.0 / (1.0 + jnp.exp(-x))


def _log_sigmoid(x):
    return jnp.minimum(x, 0.0) - jnp.log(1.0 + jnp.exp(-jnp.abs(x)))


def _ln_stats(r):
    mu = jnp.mean(r, axis=-1, keepdims=True)
    xc = r - mu
    var = jnp.mean(xc * xc, axis=-1, keepdims=True)
    rstd = lax.rsqrt(var + LN_EPS)
    return xc * rstd, rstd


def _ln_bwd(dy, xhat, rstd, g):
    dxh = dy * g
    m1 = jnp.mean(dxh, axis=-1, keepdims=True)
    m2 = jnp.mean(dxh * xhat, axis=-1, keepdims=True)
    return rstd * (dxh - m1 - xhat * m2)


def _mod_exchange(c_row, w_ada, b_ada_loc):
    def body(c_ref, w_ref, b_ref, call_ref, mod_ref, piece_ref, send_sems, recv_sems):
        me = _me()[3]
        call_ref[me] = c_ref[...]
        sent = []
        for r in range(1, N_DEV):
            peer, _ = _peer(r)
            cp = pltpu.make_async_remote_copy(
                src_ref=c_ref, dst_ref=call_ref.at[me], send_sem=send_sems.at[0, r - 1],
                recv_sem=recv_sems.at[0, r - 1], device_id=peer, device_id_type=MESH)
            cp.start()
            sent.append(cp)
        for r in range(1, N_DEV):
            peer, pidx = _peer(r)
            pltpu.make_async_remote_copy(
                src_ref=c_ref, dst_ref=call_ref.at[pidx], send_sem=send_sems.at[0, r - 1],
                recv_sem=recv_sems.at[0, r - 1], device_id=peer, device_id_type=MESH).wait_recv()
        c_all = jnp.concatenate([call_ref[d] for d in range(N_DEV)], axis=0)
        mod_loc = jnp.dot(c_all, w_ref[...], preferred_element_type=F32,
                          precision=lax.Precision.HIGHEST) + b_ref[...]
        for d in range(N_DEV):
            piece_ref[d] = mod_loc[d:d + 1, :]
        mod_ref[me] = piece_ref[me]
        for r in range(1, N_DEV):
            peer, pidx = _peer(r)
            cp = pltpu.make_async_remote_copy(
                src_ref=piece_ref.at[pidx], dst_ref=mod_ref.at[me], send_sem=send_sems.at[1, r - 1],
                recv_sem=recv_sems.at[1, r - 1], device_id=peer, device_id_type=MESH)
            cp.start()
            sent.append(cp)
        for r in range(1, N_DEV):
            peer, pidx = _peer(r)
            pltpu.make_async_remote_copy(
                src_ref=piece_ref.at[me], dst_ref=mod_ref.at[pidx], send_sem=send_sems.at[1, r - 1],
                recv_sem=recv_sems.at[1, r - 1], device_id=peer, device_id_type=MESH).wait_recv()
        for cp in sent:
            cp.wait_send()

    vmem = pl.BlockSpec(memory_space=pltpu.VMEM)
    return pl.pallas_call(
        body, name="mod_exchange", in_specs=[vmem, vmem, vmem], out_specs=[vmem, vmem],
        out_shape=[jax.ShapeDtypeStruct((N_DEV, 1, D_MODEL), F32), jax.ShapeDtypeStruct((N_DEV, 1, ADA_SHARD), F32)],
        scratch_shapes=[pltpu.VMEM((N_DEV, 1, ADA_SHARD), F32),
                        pltpu.SemaphoreType.DMA((2, N_DEV - 1)), pltpu.SemaphoreType.DMA((2, N_DEV - 1))],
        compiler_params=_params(),
    )(c_row, w_ada, b_ada_loc)


def _split3(x):
    hi = x.astype(BF16)
    r1 = x - hi.astype(F32)
    mid = r1.astype(BF16)
    lo = (r1 - mid.astype(F32)).astype(BF16)
    return hi, mid, lo


def _scan_rows(x_ref, o_ref, s, reverse, pre=None, post=None):
    tb = min(TQ, s)
    nb = s // tb
    row = lax.broadcasted_iota(jnp.int32, (tb, tb), 0)
    col = lax.broadcasted_iota(jnp.int32, (tb, tb), 1)
    tri = jnp.where((col >= row) if reverse else (col <= row), 1.0, 0.0).astype(BF16)

    def step(i, carry):
        blk = (nb - 1 - i) if reverse else i
        off = pl.multiple_of(blk * tb, tb)
        x = x_ref[pl.ds(off, tb), :]
        if pre is not None:
            x = pre(x, off)
        acc = carry
        for piece in _split3(x):
            acc = acc + jnp.dot(tri, piece, preferred_element_type=F32)
        o_ref[pl.ds(off, tb), :] = acc if post is None else post(acc, off)
        edge = acc[0:1, :] if reverse else acc[tb - 1:tb, :]
        return jnp.broadcast_to(edge, (tb, LANES))

    lax.fori_loop(0, nb, step, jnp.zeros((tb, LANES), F32))


def _forget_cumsum(f_raw, b_pad):
    s = f_raw.shape[0]

    def body(f_ref, b_ref, cum_ref):
        b = b_ref[...]
        _scan_rows(f_ref, cum_ref, s, False, pre=lambda x, off: _log_sigmoid(x + b))

    vmem = pl.BlockSpec(memory_space=pltpu.VMEM)
    return pl.pallas_call(body, name="forget_cumsum", in_specs=[vmem, vmem], out_specs=vmem,
                          out_shape=jax.ShapeDtypeStruct((s, LANES), F32), compiler_params=_params())(f_raw, b_pad)


def _forget_bwd(dcum, f_raw, b_pad):
    s = f_raw.shape[0]

    def body(d_ref, f_ref, b_ref, df_ref, db_ref, tmp_ref):
        b = b_ref[...]
        _scan_rows(d_ref, tmp_ref, s, True)
        df = tmp_ref[...] * _sigmoid(-(f_ref[...] + b))
        df_ref[...] = df.astype(BF16)
        db_ref[...] = _colsum(df)

    vmem = pl.BlockSpec(memory_space=pltpu.VMEM)
    return pl.pallas_call(
        body, name="forget_bwd", in_specs=[vmem, vmem, vmem], out_specs=[vmem, vmem],
        out_shape=[jax.ShapeDtypeStruct((s, LANES), BF16), jax.ShapeDtypeStruct((1, LANES), F32)],
        scratch_shapes=[pltpu.VMEM((s, LANES), F32)], compiler_params=_params())(dcum, f_raw, b_pad)


def _dot_nt(a, b):
    return lax.dot_general(a, b, (((1,), (1,)), ((), ())), preferred_element_type=F32)


def _head_masks():
    lane = lax.broadcasted_iota(jnp.int32, (TQ, LANES), 1)
    return lane, [lane < HEAD_DIM, lane >= HEAD_DIM]


def _pick(mask, x):
    return jnp.where(mask, x, jnp.zeros_like(x))


def _qkv_specs(s, col0):
    nb = ATTN_W // LANES
    return [pl.BlockSpec((TQ, LANES), lambda hp, qi: (qi, col0 + hp)),
            pl.BlockSpec((s, LANES), lambda hp, qi: (0, col0 + nb + hp)),
            pl.BlockSpec((s, LANES), lambda hp, qi: (0, col0 + 2 * nb + hp))]


def _pair_spec():
    return pl.BlockSpec((TQ, LANES), lambda hp, qi: (qi, hp))


def _diag_mask(width, strict):
    row = lax.broadcasted_iota(jnp.int32, (TQ, width * TQ), 0) + (width - 1) * TQ
    col = lax.broadcasted_iota(jnp.int32, (TQ, width * TQ), 1)
    return (col < row) if strict else (col <= row)


def _fox_fwd(qkv, cum_col, cum_row, ride=None):
    s = qkv.shape[0]
    nq = s // TQ

    def body(q_ref, k_ref, v_ref, cc_ref, cr_ref, o_ref, o32_ref, lse_ref):
        hp, qi = pl.program_id(0), pl.program_id(1)
        lane, masks = _head_masks()
        q2 = q_ref[...] * jnp.asarray(ATTN_SCALE, BF16)
        cc = cc_ref[...]
        qms = [_pick(masks[e], q2) for e in range(2)]
        cqs = [jnp.sum(jnp.where(lane == 2 * hp + e, cc, 0.0), axis=1, keepdims=True) for e in range(2)]

        def tile(kb, carry, masked, width=1):
            off, span = pl.multiple_of(kb * TQ, TQ), width * TQ
            k2, v2 = k_ref[pl.ds(off, span), :], v_ref[pl.ds(off, span), :]
            head0 = lax.broadcasted_iota(jnp.int32, (span, LANES), 1) < HEAD_DIM
            new = []
            for e in range(2):
                m, acc = carry[e]
                sc = _dot_nt(qms[e], k2) + (cqs[e] - cr_ref[e:e + 1, pl.ds(off, span)])
                if masked:
                    sc = jnp.where(_diag_mask(width, False), sc, -jnp.inf)
                m_new = jnp.maximum(m, jnp.max(sc, axis=1, keepdims=True))
                p = jnp.exp(sc - m_new)
                v_ones = jnp.where(head0 if e == 0 else ~head0, v2, jnp.ones_like(v2))
                acc = jnp.exp(m - m_new) * acc + jnp.dot(p.astype(BF16), v_ones, preferred_element_type=F32)
                new.append((m_new, acc))
            return tuple(new)

        init = (jnp.full((TQ, 1), -jnp.inf, F32), jnp.zeros((TQ, LANES), F32))
        carry = lax.cond(qi % 2 == 1, lambda cr: tile(qi - 1, cr, True, 2), lambda cr: tile(qi, cr, True), (init, init))
        quads = qi // 4
        carry = lax.fori_loop(0, quads, lambda j, cr: tile(4 * j, cr, False, 4), carry)
        carry = lax.cond(qi % 4 >= 2, lambda cr: tile(4 * quads, cr, False, 2), lambda cr: cr, carry)
        sums = [jnp.max(jnp.where(masks[1 - e], carry[e][1], 0.0), axis=1, keepdims=True) for e in range(2)]
        outs = [carry[e][1] / sums[e] for e in range(2)]
        lses = [carry[e][0] + jnp.log(sums[e]) for e in range(2)]
        out = jnp.where(masks[0], outs[0], outs[1])
        o_ref[...] = out.astype(BF16)
        o32_ref[...] = out
        lse_ref[...] = jnp.where(masks[0], lses[0], lses[1])

    return _ride_call(
        body, ride, name="fox_fwd", grid=(N_HEADS // 2, nq),
        in_specs=_qkv_specs(s, 0) + [pl.BlockSpec((TQ, LANES), lambda hp, qi: (qi, 0)),
                                     pl.BlockSpec((None, 2, s), lambda hp, qi: (hp, 0, 0))],
        out_specs=[_pair_spec(), _pair_spec(), _pair_spec()],
        out_shape=[jax.ShapeDtypeStruct((s, ATTN_W), BF16), jax.ShapeDtypeStruct((s, ATTN_W), F32),
                   jax.ShapeDtypeStruct((s, ATTN_W), F32)],
        scratch_shapes=[], sem=("parallel", "parallel"), args=(qkv, qkv, qkv, cum_col, cum_row))


def _write_transposed(acc_ref, out_ref):
    for c in range(out_ref.shape[0] // TQ):
        out_ref[c * TQ:(c + 1) * TQ, :] = jnp.transpose(acc_ref[:, c * TQ:(c + 1) * TQ]).astype(BF16)


def _fox_bwd(qkv, cum_col, cum_row, o, lse, do, ride=None):
    s = qkv.shape[0]
    nq = s // TQ

    def body(q_ref, k_ref, v_ref, cc_ref, cr_ref, o_ref, lse_ref, do_ref,
             dq_ref, dk_ref, dv_ref, dcr_ref, dk_acc, dv_acc):
        hp, qi = pl.program_id(0), pl.program_id(1)

        @pl.when(qi == 0)
        def _():
            dk_acc[...] = jnp.zeros_like(dk_acc)
            dv_acc[...] = jnp.zeros_like(dv_acc)
            dcr_ref[...] = jnp.zeros_like(dcr_ref)

        lane, masks = _head_masks()
        q2 = q_ref[...] * jnp.asarray(ATTN_SCALE, BF16)
        do2 = do_ref[...]
        prod = do2.astype(F32) * o_ref[...].astype(F32)
        lse2 = lse_ref[...]
        cc = cc_ref[...]
        qms = [_pick(masks[e], q2) for e in range(2)]
        doms = [_pick(masks[e], do2) for e in range(2)]
        deltas = [jnp.sum(jnp.where(masks[e], prod, 0.0), axis=1, keepdims=True) for e in range(2)]
        lses = [jnp.max(jnp.where(masks[e], lse2, -jnp.inf), axis=1, keepdims=True) for e in range(2)]
        cqs = [jnp.sum(jnp.where(lane == 2 * hp + e, cc, 0.0), axis=1, keepdims=True) for e in range(2)]
        qts = [jnp.transpose(qms[e].astype(F32)).astype(BF16) for e in range(2)]
        dots = [jnp.transpose(doms[e].astype(F32)).astype(BF16) for e in range(2)]
        ones_row = [HEAD_DIM * (1 - e) for e in range(2)]
        trow = lax.broadcasted_iota(jnp.int32, (LANES, TQ), 0)
        qts = [jnp.where(trow == ones_row[e], jnp.ones_like(qts[e]), qts[e]) for e in range(2)]

        def tile(kb, carry, masked, width=1):
            off, span = pl.multiple_of(kb * TQ, TQ), width * TQ
            k2, v2 = k_ref[pl.ds(off, span), :], v_ref[pl.ds(off, span), :]
            head0 = lax.broadcasted_iota(jnp.int32, (span, LANES), 1) < HEAD_DIM
            new, dks, dv = [], [], None
            for e in range(2):
                dq = carry[e]
                sc = _dot_nt(qms[e], k2) + (cqs[e] - cr_ref[e:e + 1, pl.ds(off, span)])
                p = jnp.exp(sc - lses[e])
                if masked:
                    p = jnp.where(_diag_mask(width, False), p, 0.0)
                ds = p * (_dot_nt(doms[e], v2) - deltas[e])
                dsb = ds.astype(BF16)
                dk_e = jnp.dot(qts[e], dsb, preferred_element_type=F32)
                dv_e = jnp.dot(dots[e], p.astype(BF16), preferred_element_type=F32)
                dks.append(dk_e)
                dv = dv_e if e == 0 else dv + dv_e
                dcr_ref[e:e + 1, pl.ds(off, span)] -= dk_e[ones_row[e]:ones_row[e] + 1, :]
                k_ones = jnp.where(head0 if e == 0 else ~head0, k2, jnp.ones_like(k2))
                new.append(dq + jnp.dot(dsb, k_ones, preferred_element_type=F32))
            krow = lax.broadcasted_iota(jnp.int32, (LANES, span), 0)
            dk_acc[:, pl.ds(off, span)] += jnp.where(krow < HEAD_DIM, dks[0], dks[1])
            dv_acc[:, pl.ds(off, span)] += dv
            return tuple(new)

        init = jnp.zeros((TQ, LANES), F32)
        carry = lax.fori_loop(0, qi // 2, lambda j, cr: tile(2 * j, cr, False, 2), (init, init))
        carry = lax.cond(qi % 2 == 1, lambda cr: tile(qi - 1, cr, True, 2), lambda cr: tile(qi, cr, True), carry)
        dq_ref[...] = (jnp.where(masks[0], carry[0], carry[1]) * ATTN_SCALE).astype(BF16)
        for e in range(2):
            dcr_ref[e:e + 1, pl.ds(pl.multiple_of(qi * TQ, TQ), TQ)] += jnp.transpose(carry[e])[
                ones_row[e]:ones_row[e] + 1, :]

        @pl.when(qi == nq - 1)
        def _():
            _write_transposed(dk_acc, dk_ref)
            _write_transposed(dv_acc, dv_ref)

    seq_spec = pl.BlockSpec((s, LANES), lambda hp, qi: (0, hp))
    return _ride_call(
        body, ride, name="fox_bwd", grid=(N_HEADS // 2, nq),
        in_specs=_qkv_specs(s, 0) + [pl.BlockSpec((TQ, LANES), lambda hp, qi: (qi, 0)),
                                     pl.BlockSpec((None, 2, s), lambda hp, qi: (hp, 0, 0)),
                                     _pair_spec(), _pair_spec(), _pair_spec()],
        out_specs=[_pair_spec(), seq_spec, seq_spec, pl.BlockSpec((None, 2, s), lambda hp, qi: (hp, 0, 0))],
        out_shape=[jax.ShapeDtypeStruct((s, ATTN_W), BF16)] * 3 + [jax.ShapeDtypeStruct((N_HEADS // 2, 2, s), F32)],
        scratch_shapes=[pltpu.VMEM((LANES, s), F32), pltpu.VMEM((LANES, s), F32)],
        sem=("parallel", "arbitrary"), args=(qkv, qkv, qkv, cum_col, cum_row, o, lse, do))


def _scan_matrix(reverse):
    row = lax.broadcasted_iota(jnp.int32, (SCAN_W, SCAN_W), 0)
    col = lax.broadcasted_iota(jnp.int32, (SCAN_W, SCAN_W), 1)
    return jnp.where((row > col) if reverse else (row < col), 1.0, 0.0).astype(BF16)


def _scan_cols(x, tri, reverse, init):
    nblk = x.shape[1] // SCAN_W
    parts, total = [None] * nblk, init
    far = 0 if reverse else SCAN_W - 1
    for b in (reversed(range(nblk)) if reverse else range(nblk)):
        blk = x[:, b * SCAN_W:(b + 1) * SCAN_W]
        part = jnp.dot(blk.astype(BF16), tri, preferred_element_type=F32)
        parts[b] = part + total
        total = total + (part[:, far:far + 1] + blk[:, far:far + 1])
    return (parts[0] if nblk == 1 else jnp.concatenate(parts, axis=1)), total


def _sb_logits(qm, k2):
    z = _dot_nt(qm, k2)
    neg_abs = lax.bitcast_convert_type(lax.bitcast_convert_type(z, jnp.uint32) | jnp.uint32(0x80000000), F32)
    soft = jnp.log(1.0 + jnp.exp(neg_abs))
    lb = jnp.minimum(z, 0.0) - soft
    return lb, lb - z


TILE_SLOTS = 4


def _tri_base(qi):
    return (qi * (qi + 1)) // 2


def _sb_fwd(qkv):
    s = qkv.shape[0]
    nq = s // TQ

    def body(q_ref, k_ref, v_ref, o_ref, t_ref, buf, sems):
        hp, qi = pl.program_id(0), pl.program_id(1)
        _, masks = _head_masks()
        suffix = _scan_matrix(True)
        q2 = q_ref[...] * jnp.asarray(ATTN_SCALE, BF16)
        qms = [_pick(masks[e], q2) for e in range(2)]
        base = _tri_base(qi)

        def store(e, kb):
            slot = kb % TILE_SLOTS
            return pltpu.make_async_copy(buf.at[e, slot], t_ref.at[2 * hp + e, base + kb], sems.at[e, slot])

        def tile(kb, carry, masked, width=1):
            off, span = pl.multiple_of(kb * TQ, TQ), width * TQ
            k2, v2 = k_ref[pl.ds(off, span), :], v_ref[pl.ds(off, span), :]
            new = []
            for e in range(2):
                run, acc = carry[e]
                lb, lo = _sb_logits(qms[e], k2)
                if masked:
                    strict = _diag_mask(width, True)
                    lo = jnp.where(strict, lo, 0.0)
                rest, run = _scan_cols(lo, suffix, True, run)
                a = jnp.exp(lb + rest)
                if masked:
                    a = jnp.where(strict, a, 0.0)
                ab, lbb = a.astype(BF16), lb.astype(BF16)
                acc = acc + jnp.dot(ab, v2, preferred_element_type=F32)
                for w in range(width):
                    blk = kb + w

                    @pl.when(blk + TILE_SLOTS <= qi)
                    def _(e=e, blk=blk):
                        store(e, blk + TILE_SLOTS).wait()
                    buf[e, blk % TILE_SLOTS, 0] = ab[:, w * TQ:(w + 1) * TQ]
                    buf[e, blk % TILE_SLOTS, 1] = lbb[:, w * TQ:(w + 1) * TQ]
                    store(e, blk).start()
                new.append((run, acc))
            return tuple(new)

        init = (jnp.zeros((TQ, 1), F32), jnp.zeros((TQ, LANES), F32))
        carry = lax.cond(qi % 2 == 1, lambda cr: tile(qi - 1, cr, True, 2), lambda cr: tile(qi, cr, True), (init, init))
        pairs = qi // 2
        carry = lax.fori_loop(0, pairs, lambda it, cr: tile(2 * (pairs - 1 - it), cr, False, 2), carry)
        for e in range(2):
            for blk in range(TILE_SLOTS):
                @pl.when(qi >= blk)
                def _(e=e, blk=blk):
                    store(e, blk).wait()
        o_ref[...] = jnp.where(masks[0], carry[0][1], carry[1][1]).astype(BF16)

    ntri = nq * (nq + 1) // 2
    return pl.pallas_call(
        body, name="sb_fwd", grid=(N_HEADS // 2, nq), in_specs=_qkv_specs(s, 3 * ATTN_W // LANES),
        out_specs=[_pair_spec(), ANY],
        out_shape=[jax.ShapeDtypeStruct((s, ATTN_W), BF16), jax.ShapeDtypeStruct((N_HEADS, ntri, 2, TQ, TQ), BF16)],
        scratch_shapes=[pltpu.VMEM((2, TILE_SLOTS, 2, TQ, TQ), BF16), pltpu.SemaphoreType.DMA((2, TILE_SLOTS))],
        compiler_params=_params(("arbitrary", "arbitrary")),
    )(qkv, qkv, qkv)


def _sb_bwd(qkv, tiles, do):
    s = qkv.shape[0]
    nq = s // TQ

    def body(q_ref, k_ref, v_ref, t_ref, do_ref, dq_ref, dk_ref, dv_ref, dk_acc, dv_acc, buf, sems):
        hp, qi = pl.program_id(0), pl.program_id(1)

        @pl.when(qi == 0)
        def _():
            dk_acc[...] = jnp.zeros_like(dk_acc)
            dv_acc[...] = jnp.zeros_like(dv_acc)

        _, masks = _head_masks()
        prefix = _scan_matrix(False)
        q2 = q_ref[...] * jnp.asarray(ATTN_SCALE, BF16)
        do2 = do_ref[...]
        qms = [_pick(masks[e], q2) for e in range(2)]
        doms = [_pick(masks[e], do2) for e in range(2)]
        qts = [jnp.transpose(qms[e].astype(F32)).astype(BF16) for e in range(2)]
        dots = [jnp.transpose(doms[e].astype(F32)).astype(BF16) for e in range(2)]
        base = _tri_base(qi)

        def fetch(e, kb):
            slot = kb % TILE_SLOTS
            return pltpu.make_async_copy(t_ref.at[2 * hp + e, base + kb], buf.at[e, slot], sems.at[e, slot])

        for e in range(2):
            fetch(e, 0).start()

            @pl.when(qi >= 1)
            def _(e=e):
                fetch(e, 1).start()

        def tile(kb, carry, masked, width=1):
            off, span = pl.multiple_of(kb * TQ, TQ), width * TQ
            k2, v2 = k_ref[pl.ds(off, span), :], v_ref[pl.ds(off, span), :]
            new, dk, dv = [], None, None
            for e in range(2):
                gsum, dq = carry[e]
                if not masked:
                    for blk in range(2, 2 + width):
                        @pl.when(kb + blk <= qi)
                        def _(e=e, blk=blk):
                            fetch(e, kb + blk).start()
                for w in range(width):
                    fetch(e, kb + w).wait()
                slots = [(kb + w) % TILE_SLOTS for w in range(width)]
                ab = buf[e, slots[0], 0] if width == 1 else jnp.concatenate([buf[e, sl, 0] for sl in slots], axis=1)
                lbb = buf[e, slots[0], 1] if width == 1 else jnp.concatenate([buf[e, sl, 1] for sl in slots], axis=1)
                beta = jnp.exp(lbb.astype(F32))
                g = ab.astype(F32) * _dot_nt(doms[e], v2)
                before, gsum = _scan_cols(g, prefix, False, gsum)
                dz = g - beta * (g + before)
                if masked:
                    dz = jnp.where(_diag_mask(width, True), dz, 0.0)
                dzb = dz.astype(BF16)
                dk_e = jnp.dot(qts[e], dzb, preferred_element_type=F32)
                dv_e = jnp.dot(dots[e], ab, preferred_element_type=F32)
                dk, dv = (dk_e, dv_e) if e == 0 else (dk + dk_e, dv + dv_e)
                new.append((gsum, dq + jnp.dot(dzb, k2, preferred_element_type=F32)))
            dk_acc[:, pl.ds(off, span)] += dk
            dv_acc[:, pl.ds(off, span)] += dv
            return tuple(new)

        init = (jnp.zeros((TQ, 1), F32), jnp.zeros((TQ, LANES), F32))
        carry = lax.fori_loop(0, qi // 2, lambda j, cr: tile(2 * j, cr, False, 2), (init, init))
        carry = lax.cond(qi % 2 == 1, lambda cr: tile(qi - 1, cr, True, 2), lambda cr: tile(qi, cr, True), carry)
        dq_ref[...] = (jnp.where(masks[0], carry[0][1], carry[1][1]) * ATTN_SCALE).astype(BF16)

        @pl.when(qi == nq - 1)
        def _():
            _write_transposed(dk_acc, dk_ref)
            _write_transposed(dv_acc, dv_ref)

    seq_spec = pl.BlockSpec((s, LANES), lambda hp, qi: (0, hp))
    return pl.pallas_call(
        body, name="sb_bwd", grid=(N_HEADS // 2, nq),
        in_specs=_qkv_specs(s, 3 * ATTN_W // LANES) + [ANY, _pair_spec()],
        out_specs=[_pair_spec(), seq_spec, seq_spec],
        out_shape=[jax.ShapeDtypeStruct((s, ATTN_W), BF16)] * 3,
        scratch_shapes=[pltpu.VMEM((LANES, s), F32), pltpu.VMEM((LANES, s), F32),
                        pltpu.VMEM((2, TILE_SLOTS, 2, TQ, TQ), BF16), pltpu.SemaphoreType.DMA((2, TILE_SLOTS))],
        compiler_params=_params(("arbitrary", "arbitrary")),
    )(qkv, qkv, qkv, tiles, do)


CONV_TR = 256


def _shift_down(x, halo, n):
    rolled = pltpu.roll(x, n, 0)
    head = rolled[0:8, :]
    rid = lax.broadcasted_iota(jnp.int32, head.shape, 0)
    for j in range(n):
        head = jnp.where(rid == j, halo[8 - n + j:8 - n + j + 1, :], head)
    return jnp.concatenate([head, rolled[8:, :]], axis=0)


def _shift_up(x, halo, n):
    rows = x.shape[0]
    rolled = pltpu.roll(x, rows - n, 0)
    tail = rolled[rows - 8:, :]
    rid = lax.broadcasted_iota(jnp.int32, tail.shape, 0)
    for j in range(n):
        tail = jnp.where(rid == 8 - n + j, halo[j:j + 1, :], tail)
    return jnp.concatenate([rolled[:rows - 8, :], tail], axis=0)


def _conv_fwd_block(x, halo, w, b):
    return b + _shift_down(x, halo, 2) * w[0:1, :] + _shift_down(x, halo, 1) * w[1:2, :] + x * w[2:3, :]


def _conv_specs(tr, s):
    pair = 2 * FF_HALF
    blk = pl.BlockSpec((tr, pair), lambda j, i: (i, j))
    prev = pl.BlockSpec((8, pair), lambda j, i: (jnp.maximum(i * (tr // 8) - 1, 0), j))
    nxt = pl.BlockSpec((8, pair), lambda j, i: (jnp.minimum((i + 1) * (tr // 8), s // 8 - 1), j))
    return blk, prev, nxt


def _conv_gate_fwd(hpre, conv_w, conv_b):
    s = hpre.shape[0]
    tr = min(CONV_TR, s)
    blk, prev, _ = _conv_specs(tr, s)

    def body(x_ref, halo_ref, w_ref, b_ref, a_ref):
        i = pl.program_id(1)
        halo = jnp.where(i > 0, halo_ref[...], 0.0)
        h = _conv_fwd_block(x_ref[...], halo, w_ref[...], b_ref[...])
        hg, hv = h[:, :FF_HALF], h[:, FF_HALF:]
        a_ref[...] = (hg * _sigmoid(hg) * hv).astype(BF16)

    return pl.pallas_call(
        body, name="conv_gate_fwd", grid=(2, s // tr),
        in_specs=[blk, prev, pl.BlockSpec((3, 2 * FF_HALF), lambda j, i: (0, j)),
                  pl.BlockSpec((1, 2 * FF_HALF), lambda j, i: (0, j))],
        out_specs=pl.BlockSpec((tr, FF_HALF), lambda j, i: (i, j)),
        out_shape=jax.ShapeDtypeStruct((s, D_FF), BF16),
        compiler_params=_params(("parallel", "parallel")),
    )(hpre, hpre, conv_w, conv_b)


def _conv_gate_bwd(hpre, da, conv_w, conv_b):
    s = hpre.shape[0]
    tr = min(CONV_TR, s)
    blk, prev, _ = _conv_specs(tr, s)

    def body(x_ref, halo_ref, da_ref, w_ref, b_ref, dh_ref, db_ref, dw_ref):
        i = pl.program_id(1)
        halo = jnp.where(i > 0, halo_ref[...], 0.0)
        x = x_ref[...]
        h = _conv_fwd_block(x, halo, w_ref[...], b_ref[...])
        hg, hv = h[:, :FF_HALF], h[:, FF_HALF:]
        da_blk = da_ref[...].astype(F32)
        sg = _sigmoid(hg)
        dhg = da_blk * hv * (sg * (1.0 + hg * (1.0 - sg)))
        dhv = da_blk * (hg * sg)
        dh_ref[:, :FF_HALF] = dhg.astype(BF16)
        dh_ref[:, FF_HALF:] = dhv.astype(BF16)
        x2, x1 = _shift_down(x, halo, 2), _shift_down(x, halo, 1)
        parts = []
        for lo, dpart in ((0, dhg), (FF_HALF, dhv)):
            cols = slice(lo, lo + FF_HALF)
            parts.append((cols, _colsum(dpart), _colsum(dpart * x2[:, cols]), _colsum(dpart * x1[:, cols]),
                          _colsum(dpart * x[:, cols])))

        @pl.when(i == 0)
        def _():
            for cols, db, dw0, dw1, dw2 in parts:
                db_ref[:, cols] = db
                dw_ref[0:1, cols] = dw0
                dw_ref[1:2, cols] = dw1
                dw_ref[2:3, cols] = dw2

        @pl.when(i > 0)
        def _():
            for cols, db, dw0, dw1, dw2 in parts:
                db_ref[:, cols] += db
                dw_ref[0:1, cols] += dw0
                dw_ref[1:2, cols] += dw1
                dw_ref[2:3, cols] += dw2

    pair = 2 * FF_HALF
    return pl.pallas_call(
        body, name="conv_gate_bwd", grid=(2, s // tr),
        in_specs=[blk, prev, pl.BlockSpec((tr, FF_HALF), lambda j, i: (i, j)),
                  pl.BlockSpec((3, pair), lambda j, i: (0, j)), pl.BlockSpec((1, pair), lambda j, i: (0, j))],
        out_specs=[blk, pl.BlockSpec((1, pair), lambda j, i: (0, j)), pl.BlockSpec((3, pair), lambda j, i: (0, j))],
        out_shape=[jax.ShapeDtypeStruct((s, 2 * D_FF), BF16), jax.ShapeDtypeStruct((1, 2 * D_FF), F32),
                   jax.ShapeDtypeStruct((3, 2 * D_FF), F32)],
        compiler_params=_params(("parallel", "arbitrary")),
    )(hpre, hpre, da, conv_w, conv_b)


def _conv_input_bwd(dh, conv_w):
    s = dh.shape[0]
    tr = min(CONV_TR, s)
    blk, _, _ = _conv_specs(tr, s)
    nblk = s // tr

    def body(x_ref, halo_ref, w_ref, o_ref):
        i = pl.program_id(1)
        halo = jnp.where(i < nblk - 1, halo_ref[...].astype(F32), 0.0)
        x, w = x_ref[...].astype(F32), w_ref[...]
        o_ref[...] = (x * w[2:3, :] + _shift_up(x, halo, 1) * w[1:2, :] + _shift_up(x, halo, 2) * w[0:1, :]).astype(BF16)

    nxt = pl.BlockSpec((16, 2 * FF_HALF), lambda j, i: (jnp.minimum((i + 1) * (tr // 16), s // 16 - 1), j))
    return pl.pallas_call(
        body, name="conv_input_bwd", grid=(2, nblk),
        in_specs=[blk, nxt, pl.BlockSpec((3, 2 * FF_HALF), lambda j, i: (0, j))], out_specs=blk,
        out_shape=jax.ShapeDtypeStruct((s, 2 * D_FF), BF16),
        compiler_params=_params(("parallel", "parallel")),
    )(dh, dh, conv_w)


def _adamw_math(w, g, m, v):
    m = ADAM_B1 * m + (1.0 - ADAM_B1) * g
    v = ADAM_B2 * v + (1.0 - ADAM_B2) * (g * g)
    m_hat = m / (1.0 - ADAM_B1 ** ADAM_STEP)
    v_hat = v / (1.0 - ADAM_B2 ** ADAM_STEP)
    delta = -ADAM_LR * (m_hat / (jnp.sqrt(v_hat) + ADAM_EPS) + ADAM_WD * w)
    return delta, m, v


def _adamw(name, g8, w, m, v):
    r, c = w.shape
    tr = _row_tile(r, c)

    def body(g_ref, w_ref, m_ref, v_ref, go_ref, d_ref, mo_ref, vo_ref):
        g = g_ref[0].astype(F32)
        for d in range(1, N_DEV):
            g = g + g_ref[d].astype(F32)
        delta, mn, vn = _adamw_math(w_ref[...], g, m_ref[...], v_ref[...])
        go_ref[...] = g
        d_ref[...] = delta
        mo_ref[...] = mn
        vo_ref[...] = vn

    spec = pl.BlockSpec((tr, c), lambda i: (i, 0))
    return pl.pallas_call(
        body, name=name, grid=(r // tr,),
        in_specs=[pl.BlockSpec((N_DEV, tr, c), lambda i: (0, i, 0)), spec, spec, spec], out_specs=[spec] * 4,
        out_shape=[jax.ShapeDtypeStruct((r, c), F32)] * 4, compiler_params=_params(("parallel",)),
    )(g8, w, m, v)


def _adamw_ada(c_t, dmod, w, m, v):
    r, c = w.shape
    tr = _row_tile(r, c)

    def body(ct_ref, dm_ref, w_ref, m_ref, v_ref, go_ref, d_ref, mo_ref, vo_ref):
        ct, dm = ct_ref[...], dm_ref[...]
        g = ct[:, 0:1] * dm[0:1, :]
        for b in range(1, N_DEV):
            g = g + ct[:, b:b + 1] * dm[b:b + 1, :]
        delta, mn, vn = _adamw_math(w_ref[...], g, m_ref[...], v_ref[...])
        go_ref[...] = g
        d_ref[...] = delta
        mo_ref[...] = mn
        vo_ref[...] = vn

    spec = pl.BlockSpec((tr, c), lambda i: (i, 0))
    return pl.pallas_call(
        body, name="adamw_w_ada", grid=(r // tr,),
        in_specs=[pl.BlockSpec((tr, N_DEV), lambda i: (i, 0)), pl.BlockSpec((N_DEV, c), lambda i: (0, 0)),
                  spec, spec, spec],
        out_specs=[spec] * 4, out_shape=[jax.ShapeDtypeStruct((r, c), F32)] * 4,
        compiler_params=_params(("parallel",)),
    )(c_t, dmod, w, m, v)


def _cols_from_slots(g):
    n, r, c = g.shape
    return jnp.transpose(g, (1, 0, 2)).reshape(r, n * c)


def _cols_to_slots(w):
    r, c = w.shape
    return jnp.transpose(w.reshape(r, N_DEV, c // N_DEV), (1, 0, 2))


def _pair_cols(w):
    g0, g1 = w[..., 0:FF_HALF], w[..., FF_HALF:D_FF]
    v0, v1 = w[..., D_FF:D_FF + FF_HALF], w[..., D_FF + FF_HALF:]
    return jnp.concatenate([g0, v0, g1, v1], axis=-1)


def _unpair_cols(w):
    g0, v0 = w[..., 0:FF_HALF], w[..., FF_HALF:D_FF]
    g1, v1 = w[..., D_FF:D_FF + FF_HALF], w[..., D_FF + FF_HALF:]
    return jnp.concatenate([g0, g1, v0, v1], axis=-1)


def _row(v):
    return v.reshape(1, -1)


def kernel(x, c, w_ada, b_ada, w_in, b_forget, w_fox_proj, w_sb_proj, w_o, ln1_g, ln1_b, w_up, conv_w, conv_b, w_down, ln2_g, ln2_b, loss_target, m_w_ada, m_b_ada, m_w_in, m_b_forget, m_w_fox_proj, m_w_sb_proj, m_w_o, m_ln1_g, m_ln1_b, m_w_up, m_conv_w, m_conv_b, m_w_down, m_ln2_g, m_ln2_b, v_w_ada, v_b_ada, v_w_in, v_b_forget, v_w_fox_proj, v_w_sb_proj, v_w_o, v_ln1_g, v_ln1_b, v_w_up, v_conv_w, v_conv_b, v_w_down, v_ln2_g, v_ln2_b):
    s = x.shape[1]
    me = 4 * lax.axis_index("x") + 2 * lax.axis_index("y") + lax.axis_index("c")
    x2 = x.reshape(s, D_MODEL)
    tgt = loss_target.reshape(s, D_MODEL)

    b_ada_loc = lax.dynamic_slice(b_ada, (me * ADA_SHARD,), (ADA_SHARD,)).reshape(1, ADA_SHARD)
    c_all, mod = _mod_exchange(c, w_ada, b_ada_loc)
    mod = mod.reshape(N_MOD, 1, D_MODEL)
    sh1, sc1, gt1, sh2, sc2, gt2 = [mod[i] for i in range(N_MOD)]

    g_in = _allgather_two_level("ag_w_in", w_in.astype(BF16))
    late_weights = _Ride([w_fox_proj.astype(BF16), w_sb_proj.astype(BF16), w_o.astype(BF16), w_up.astype(BF16),
                          w_down.astype(BF16), conv_w], scatter=False)
    w_in_f = _cols_from_slots(g_in)
    w_proj = jnp.concatenate(
        [w_in_f[:, 0:1536], w_in_f[:, 1544:3080], w_in_f[:, 3080:5128], w_in_f[:, 1536:1544],
         jnp.zeros((D_MODEL, W_PROJ - 5128), BF16)], axis=1)
    w_qkv, w_gates, w_f = w_proj[:, :W_QKV], w_proj[:, W_QKV:W_QKV + W_GATES], w_proj[:, W_QKV + W_GATES:W_QKV + W_GATES + W_F]
    conv_b_p = _pair_cols(_row(conv_b))
    b_f_pad = jnp.pad(_row(b_forget), ((0, 0), (0, LANES - N_HEADS)))

    (u1,) = _rowwise("modulate1", lambda xb, sc, sh: (xb * (1.0 + sc) + sh,),
                     [(x2, D_MODEL, 0)], [sc1, sh1], [(D_MODEL, BF16)], tr=512)
    qkv = _mm(u1, w_qkv, name="mm_qkv", out_dtype=BF16)
    gates = _mm(u1, w_gates, name="mm_gates")
    f_raw = _mm(u1, w_f, name="mm_forget")
    cum_col = _forget_cumsum(f_raw, b_f_pad)
    cum_row = jnp.transpose(cum_col[:, :N_HEADS]).reshape(N_HEADS // 2, 2, s)
    (y_fox, y_fox32, lse), (g_fox, g_sb, g_o, g_up, g_down, g_cw) = _fox_fwd(qkv, cum_col, cum_row, ride=late_weights)
    w_fox_f = _cols_from_slots(g_fox)
    w_sb_f = _cols_from_slots(g_sb)
    w_o_f = g_o.reshape(D_MODEL, D_MODEL)
    w_up_p = _pair_cols(_cols_from_slots(g_up))
    w_down_f = g_down.reshape(D_FF, D_MODEL)
    conv_w_p = _pair_cols(_cols_from_slots(g_cw))
    y_sb, sb_run = _sb_fwd(qkv)
    pf = _mm(y_fox, w_fox_f, name="mm_fox_proj", out_dtype=BF16)
    ps = _mm(y_sb, w_sb_f, name="mm_sb_proj", out_dtype=BF16)
    (merged,) = _rowwise("gate_merge", lambda ga, gb, a, b: (_sigmoid(ga) * a + _sigmoid(gb) * b,),
                         [(gates, D_MODEL, 0), (gates, D_MODEL, 1), (pf, D_MODEL, 0), (ps, D_MODEL, 0)], [],
                         [(D_MODEL, BF16)])
    attn_out = _mm(merged, w_o_f, name="mm_w_o")

    def ln_fwd(xb, fb, gt, g, b):
        xhat, _ = _ln_stats(ALPHA * xb + (1.0 + gt) * fb)
        return xhat * g + b

    def ln1_mod(xb, fb, gt, g, b, sc, sh):
        y = ln_fwd(xb, fb, gt, g, b)
        return y, y * (1.0 + sc) + sh

    x1, u2 = _rowwise("ln1_modulate2", ln1_mod, [(x2, D_MODEL, 0), (attn_out, D_MODEL, 0)],
                      [gt1, _row(ln1_g), _row(ln1_b), sc2, sh2], [(D_MODEL, F32), (D_MODEL, BF16)])

    hpre = _mm(u2, w_up_p, name="mm_w_up", tn=1408)
    act = _conv_gate_fwd(hpre, conv_w_p, conv_b_p)
    ffn_out = _mm(act, w_down_f, name="mm_w_down", tk=2816)

    def ln2_bwd(xb, fb, tb, gt, g, b):
        xhat, rstd = _ln_stats(ALPHA * xb + (1.0 + gt) * fb)
        err = (xhat * g + b) - tb
        dy = err * (1.0 / D_MODEL)
        dr = _ln_bwd(dy, xhat, rstd, g)
        return (dr * (1.0 + gt), ALPHA * dr,
                _colsum(err * err), _colsum(dy * xhat), _colsum(dy), _colsum(dr * fb))

    dffn, dx1_res, sq_err, d_ln2_g, d_ln2_b, d_gt2 = _rowwise(
        "ln2_bwd", ln2_bwd, [(x1, D_MODEL, 0), (ffn_out, D_MODEL, 0), (tgt, D_MODEL, 0)],
        [gt2, _row(ln2_g), _row(ln2_b)], [(D_MODEL, BF16), (D_MODEL, F32)], sums=[D_MODEL] * 4)
    loss = lax.psum(0.5 * jnp.sum(sq_err) / D_MODEL, ("x", "y", "c"))

    d_w_down = _mm(act, dffn, name="mm_d_w_down", ta=True, tm=1408, tk=2048, out_dtype=BF16)
    d_act = _mm(dffn, w_down_f, name="mm_d_act", tb=True, tn=1408, out_dtype=BF16)
    dh, d_conv_b_p, d_conv_w_p = _conv_gate_bwd(hpre, d_act, conv_w_p, conv_b_p)
    dhpre = _conv_input_bwd(dh, conv_w_p)
    d_w_up_p = _mm(u2, dhpre, name="mm_d_w_up", ta=True, tn=1408, tk=2048, out_dtype=BF16)
    du2 = _mm(dhpre, w_up_p, name="mm_d_u2", tb=True, tk=2816)

    def ln1_bwd(du, dres, x1b, xb, fb, sc, gt, g):
        dx1 = dres + du * (1.0 + sc)
        xhat, rstd = _ln_stats(ALPHA * xb + (1.0 + gt) * fb)
        dr = _ln_bwd(dx1, xhat, rstd, g)
        return (dr * (1.0 + gt), ALPHA * dr,
                _colsum(du * x1b), _colsum(du), _colsum(dx1 * xhat), _colsum(dx1), _colsum(dr * fb))

    d_attn, dx_res, d_sc2, d_sh2, d_ln1_g, d_ln1_b, d_gt1 = _rowwise(
        "ln1_bwd", ln1_bwd,
        [(du2, D_MODEL, 0), (dx1_res, D_MODEL, 0), (x1, D_MODEL, 0), (x2, D_MODEL, 0), (attn_out, D_MODEL, 0)],
        [sc2, gt1, _row(ln1_g)], [(D_MODEL, BF16), (D_MODEL, F32)], sums=[D_MODEL] * 5)

    d_w_o = _mm(merged, d_attn, name="mm_d_w_o", ta=True, out_dtype=BF16)
    d_merged = _mm(d_attn, w_o_f, name="mm_d_merged", tb=True, out_dtype=BF16)

    def merge_bwd(dm, ga, gb, a, b):
        dm, a, b = dm.astype(F32), a.astype(F32), b.astype(F32)
        sa, sb = _sigmoid(ga), _sigmoid(gb)
        return dm * a * sa * (1.0 - sa), dm * b * sb * (1.0 - sb), dm * sa, dm * sb

    d_ga, d_gb, d_pf, d_ps = _rowwise(
        "gate_merge_bwd", merge_bwd,
        [(d_merged, D_MODEL, 0), (gates, D_MODEL, 0), (gates, D_MODEL, 1), (pf, D_MODEL, 0), (ps, D_MODEL, 0)], [],
        [(D_MODEL, BF16)] * 4)
    d_w_fox = _mm(y_fox, d_pf, name="mm_d_w_fox", ta=True, out_dtype=BF16)
    d_w_sb = _mm(y_sb, d_ps, name="mm_d_w_sb", ta=True, out_dtype=BF16)
    d_y_fox = _mm(d_pf, w_fox_f, name="mm_d_y_fox", tb=True, out_dtype=BF16)
    d_y_sb = _mm(d_ps, w_sb_f, name="mm_d_y_sb", tb=True, out_dtype=BF16)
    early_grads = _Ride(
        [_cols_to_slots(d_w_fox), _cols_to_slots(d_w_sb), d_w_o.reshape(N_DEV, D_MODEL // N_DEV, D_MODEL),
         _cols_to_slots(_unpair_cols(d_w_up_p)), d_w_down.reshape(N_DEV, D_FF // N_DEV, D_MODEL)], scatter=True)
    (dq_a, dk_a, dv_a, d_cum_row), early_slots = _fox_bwd(qkv, cum_col, cum_row, y_fox32, lse, d_y_fox,
                                                                    ride=early_grads)
    dq_b, dk_b, dv_b = _sb_bwd(qkv, sb_run, d_y_sb)
    d_cum = jnp.transpose(d_cum_row.reshape(N_HEADS, s))
    d_cum = jnp.pad(d_cum, ((0, 0), (0, LANES - N_HEADS)))
    d_f, d_b_forget = _forget_bwd(d_cum, f_raw, b_f_pad)
    d_proj = jnp.concatenate([dq_a, dk_a, dv_a, dq_b, dk_b, dv_b, d_ga, d_gb, d_f,
                              jnp.zeros((s, W_PROJ - W_QKV - W_GATES - W_F), BF16)], axis=1)
    d_w_proj = _mm(u1, d_proj, name="mm_d_w_in", ta=True, tn=896, tk=2048, out_dtype=BF16)
    d_w_in_f = jnp.concatenate([d_w_proj[:, 0:1536], d_w_proj[:, 5120:5128], d_w_proj[:, 1536:3072],
                                d_w_proj[:, 3072:5120]], axis=1)
    du1, (in_slots,) = _mm(d_proj, w_proj, name="mm_d_u1", tb=True, tk=2688,
                           ride=_Ride([_cols_to_slots(d_w_in_f)], scatter=True))

    def x_bwd(du, dres, xb, sc):
        return dres + du * (1.0 + sc), _colsum(du * xb), _colsum(du)

    grad_x, d_sc1, d_sh1 = _rowwise("x_bwd", x_bwd, [(du1, D_MODEL, 0), (dx_res, D_MODEL, 0), (x2, D_MODEL, 0)],
                                    [sc1], [(D_MODEL, F32)], sums=[D_MODEL] * 2, tr=512)

    d_conv_b = _unpair_cols(d_conv_b_p)
    d_conv_w = _unpair_cols(d_conv_w_p)
    cb_pad = N_MOD * D_MODEL - 2 * D_FF
    n_rep = N_MOD * D_MODEL + D_MODEL + 4 * D_MODEL + 2 * D_FF + cb_pad
    small = jnp.concatenate(
        [d_sh1, d_sc1, d_gt1, d_sh2, d_sc2, d_gt2, jnp.pad(d_b_forget, ((0, 0), (0, D_MODEL - LANES))),
         d_ln1_g, d_ln1_b, d_ln2_g, d_ln2_b, jnp.pad(d_conv_b, ((0, 0), (0, cb_pad))),
         d_conv_w.reshape(1, 6 * D_FF)], axis=1)
    n_small = small.shape[1] // LANES
    small = jnp.pad(small.reshape(n_small, LANES), ((0, -n_small % 8), (0, 0)))
    (small_all,) = _exchange("ag_small_grads", [small], scatter=False)
    rep8 = small_all[:, :n_rep // LANES, :]
    cw8 = small_all[:, n_rep // LANES:n_small, :].reshape(N_DEV, 3, 2 * D_FF)
    cw8 = lax.dynamic_slice(cw8, (0, 0, me * UP_SHARD), (N_DEV, 3, UP_SHARD))
    dmod8 = small_all[:, :N_MOD * D_MODEL // LANES, :].reshape(N_DEV, N_MOD * D_MODEL)
    dmod_loc = lax.dynamic_slice(dmod8, (0, me * ADA_SHARD), (N_DEV, ADA_SHARD))

    def pack_rep(b_a, b_f, g1, b1, g2, b2, cb):
        pieces = [b_a, jnp.pad(b_f, (0, D_MODEL - N_HEADS)), g1, b1, g2, b2, jnp.pad(cb, (0, cb_pad))]
        return jnp.concatenate([p.reshape(-1, LANES) for p in pieces], axis=0)

    rep = _adamw("adamw_small", rep8, pack_rep(b_ada, b_forget, ln1_g, ln1_b, ln2_g, ln2_b, conv_b),
                 pack_rep(m_b_ada, m_b_forget, m_ln1_g, m_ln1_b, m_ln2_g, m_ln2_b, m_conv_b),
                 pack_rep(v_b_ada, v_b_forget, v_ln1_g, v_ln1_b, v_ln2_g, v_ln2_b, v_conv_b))

    def unpack_rep(p):
        o, d = N_MOD * D_MODEL // LANES, D_MODEL // LANES

        def rows(lo, n):
            return p[lo:lo + n].reshape(-1)

        return {"b_ada": rows(0, o), "b_forget": rows(o, d)[:N_HEADS],
                "ln1_g": rows(o + d, d), "ln1_b": rows(o + 2 * d, d), "ln2_g": rows(o + 3 * d, d),
                "ln2_b": rows(o + 4 * d, d), "conv_b": rows(o + 5 * d, o)[:2 * D_FF]}

    rep = [unpack_rep(p) for p in rep]
    r_conv_w = _adamw("adamw_conv_w", cw8, conv_w, m_conv_w, v_conv_w)
    r_ada = _adamw_ada(jnp.transpose(c_all.reshape(N_DEV, D_MODEL)), dmod_loc, w_ada, m_w_ada, v_w_ada)

    r_in = _adamw("adamw_w_in", in_slots, w_in, m_w_in, v_w_in)
    r_fox = _adamw("adamw_w_fox", early_slots[0], w_fox_proj, m_w_fox_proj, v_w_fox_proj)
    r_sb = _adamw("adamw_w_sb", early_slots[1], w_sb_proj, m_w_sb_proj, v_w_sb_proj)
    r_o = _adamw("adamw_w_o", early_slots[2], w_o, m_w_o, v_w_o)
    r_up = _adamw("adamw_w_up", early_slots[3], w_up, m_w_up, v_w_up)
    r_down = _adamw("adamw_w_down", early_slots[4], w_down, m_w_down, v_w_down)

    def leaf(i):
        return [r_ada[i], rep[i]["b_ada"], r_in[i], rep[i]["b_forget"], r_fox[i], r_sb[i], r_o[i], rep[i]["ln1_g"],
                rep[i]["ln1_b"], r_up[i], r_conv_w[i], rep[i]["conv_b"], r_down[i], rep[i]["ln2_g"], rep[i]["ln2_b"]]

    return (loss, grad_x.reshape(1, s, D_MODEL), *leaf(0), *leaf(1), *leaf(2), *leaf(3))
```

```python
import functools

import jax
import jax.numpy as jnp
from jax import lax
from jax.experimental import pallas as pl
from jax.experimental.pallas import tpu as pltpu

F32 = jnp.float32
BF16 = jnp.bfloat16
MESH = pl.DeviceIdType.MESH
ANY = pl.BlockSpec(memory_space=pl.ANY)

N_DEV = 8
D_MODEL = 1024
HEAD_DIM = 64
N_HEADS = 8
ATTN_W = N_HEADS * HEAD_DIM
D_FF = 2816
FF_HALF = D_FF // 2
N_MOD = 6
ADA_SHARD = N_MOD * D_MODEL // N_DEV
IN_SHARD = 641
UP_SHARD = 704
ATTN_SCALE = HEAD_DIM ** -0.5
ALPHA = 2.0 ** 0.25
LN_EPS = 1e-5
LANES = 128
TQ = 512
SCAN_W = 256
VMEM_LIMIT = 56 * 1024 * 1024

ADAM_LR, ADAM_B1, ADAM_B2, ADAM_EPS, ADAM_WD, ADAM_STEP = 0.001, 0.9, 0.999, 1e-08, 0.01, 10

W_QKV, W_GATES, W_F = 3072, 2048, 128
W_PROJ = 5376


def _params(sem=None):
    return pltpu.CompilerParams(dimension_semantics=sem, vmem_limit_bytes=VMEM_LIMIT)


def _tile(n, cap):
    if n <= cap:
        return n
    best = None
    for t in range(LANES, cap + 1, LANES):
        if n % t == 0:
            best = t
    assert best is not None, (n, cap)
    return best


def _row_tile(r, width, budget=192 * 1024):
    if r * width <= budget or r % 16:
        return r
    best = 16
    for t in range(16, r + 1, 16):
        if r % t == 0 and t * width <= budget:
            best = t
    return best


def _me():
    x, y, c = lax.axis_index("x"), lax.axis_index("y"), lax.axis_index("c")
    return x, y, c, 4 * x + 2 * y + c


def _peer(r):
    x, y, c, _ = _me()
    px = 1 - x if r & 4 else x
    py = 1 - y if r & 2 else y
    pc = 1 - c if r & 1 else c
    return (px, py, pc), 4 * px + 2 * py + pc


class _Ride:
    def __init__(self, arrays, scatter):
        self.arrays, self.scatter, self.n = list(arrays), scatter, len(arrays)
        self.in_specs = [ANY] * self.n
        self.out_specs = [ANY] * self.n
        self.out_shape = [jax.ShapeDtypeStruct(a.shape if scatter else (N_DEV,) + a.shape, a.dtype) for a in arrays]
        self.scratch = [pltpu.SemaphoreType.DMA((self.n, N_DEV - 1)), pltpu.SemaphoreType.DMA((self.n, N_DEV - 1)),
                        pltpu.SemaphoreType.DMA((self.n,))]

    def _local(self, ins, outs, sems, a):
        me = _me()[3]
        return pltpu.make_async_copy(ins[a].at[me] if self.scatter else ins[a], outs[a].at[me], sems[2].at[a])

    def _remote(self, ins, outs, sems, a, r, arriving):
        me = _me()[3]
        peer, pidx = _peer(r)
        src = ins[a].at[me if arriving else pidx] if self.scatter else ins[a]
        return pltpu.make_async_remote_copy(
            src_ref=src, dst_ref=outs[a].at[pidx if arriving else me], send_sem=sems[0].at[a, r - 1],
            recv_sem=sems[1].at[a, r - 1], device_id=peer, device_id_type=MESH)

    def start(self, ins, outs, sems):
        for a in range(self.n):
            self._local(ins, outs, sems, a).start()
        for r in range(1, N_DEV):
            for a in range(self.n):
                self._remote(ins, outs, sems, a, r, False).start()

    def wait(self, ins, outs, sems):
        for r in range(1, N_DEV):
            for a in range(self.n):
                self._remote(ins, outs, sems, a, r, True).wait_recv()
        for r in range(1, N_DEV):
            for a in range(self.n):
                self._remote(ins, outs, sems, a, r, False).wait_send()
        for a in range(self.n):
            self._local(ins, outs, sems, a).wait()


def _exchange(name, arrays, scatter):
    ride = _Ride(arrays, scatter)

    def body(*refs):
        ins, outs, sems = refs[:ride.n], refs[ride.n:2 * ride.n], refs[2 * ride.n:]
        ride.start(ins, outs, sems)
        ride.wait(ins, outs, sems)

    return pl.pallas_call(body, name=name, in_specs=ride.in_specs, out_specs=ride.out_specs, out_shape=ride.out_shape,
                          scratch_shapes=ride.scratch)(*arrays)


def _allgather_two_level(name, a):
    def body(a_ref, out_ref, send_sems, recv_sems, local_sem):
        x, y, c, me = _me()
        sibling = (x, y, 1 - c)
        chips = [(1 - x, y), (x, 1 - y), (1 - x, 1 - y)]

        def idx(px, py, pc):
            return 4 * px + 2 * py + pc

        def copy(k, block, to, src=None):
            slot = out_ref.at[idx(*block)]
            return pltpu.make_async_remote_copy(
                src_ref=slot if src is None else src, dst_ref=slot, send_sem=send_sems.at[k], recv_sem=recv_sems.at[k],
                device_id=to, device_id_type=MESH)

        mine = pltpu.make_async_copy(a_ref, out_ref.at[me], local_sem)
        mine.start()
        first = [copy(0, (x, y, c), sibling, src=a_ref)]
        first += [copy(1 + j, (x, y, c), (*chip, c), src=a_ref) for j, chip in enumerate(chips)]
        for cp in first:
            cp.start()
        passed = [copy(4 + j, (*chip, c), sibling) for j, chip in enumerate(chips)]
        for j, chip in enumerate(chips):
            copy(1 + j, (*chip, c), (x, y, c)).wait_recv()
            passed[j].start()
        copy(0, sibling, (x, y, c)).wait_recv()
        for j, chip in enumerate(chips):
            copy(4 + j, (*chip, 1 - c), (x, y, c)).wait_recv()
        for cp in first + passed:
            cp.wait_send()
        mine.wait()

    return pl.pallas_call(
        body, name=name, in_specs=[ANY], out_specs=ANY,
        out_shape=jax.ShapeDtypeStruct((N_DEV,) + a.shape, a.dtype),
        scratch_shapes=[pltpu.SemaphoreType.DMA((N_DEV - 1,)), pltpu.SemaphoreType.DMA((N_DEV - 1,)),
                        pltpu.SemaphoreType.DMA],
    )(a)


def _with_ride(body, ride, n_in, n_out, grid):
    if ride is None:
        return body
    n = ride.n

    def wrapped(*refs):
        ins, rins = refs[:n_in], refs[n_in:n_in + n]
        outs, routs = refs[n_in + n:n_in + n + n_out], refs[n_in + n + n_out:n_in + 2 * n + n_out]
        rest = refs[n_in + 2 * n + n_out:]
        scratch, sems = rest[:len(rest) - 3], rest[len(rest) - 3:]
        ids = [pl.program_id(d) for d in range(len(grid))]
        first = functools.reduce(lambda p, q: p & q, [i == 0 for i in ids])
        last = functools.reduce(lambda p, q: p & q, [i == g - 1 for i, g in zip(ids, grid)])

        @pl.when(first)
        def _():
            ride.start(rins, routs, sems)

        body(*ins, *outs, *scratch)

        @pl.when(last)
        def _():
            ride.wait(rins, routs, sems)

    return wrapped


def _ride_call(body, ride, *, name, grid, in_specs, out_specs, out_shape, scratch_shapes, sem, args):
    n_in, n_out = len(in_specs), len(out_specs)
    if ride is None:
        res = pl.pallas_call(body, name=name, grid=grid, in_specs=in_specs, out_specs=out_specs, out_shape=out_shape,
                             scratch_shapes=scratch_shapes, compiler_params=_params(sem))(*args)
        return list(res), []
    res = pl.pallas_call(
        _with_ride(body, ride, n_in, n_out, grid), name=name, grid=grid,
        in_specs=list(in_specs) + ride.in_specs, out_specs=list(out_specs) + ride.out_specs,
        out_shape=list(out_shape) + ride.out_shape, scratch_shapes=list(scratch_shapes) + ride.scratch,
        compiler_params=_params(("arbitrary",) * len(grid)))(*args, *ride.arrays)
    return list(res[:n_out]), list(res[n_out:])


def _mm(a, b, *, name, ta=False, tb=False, out_dtype=F32, tm=1024, tn=1024, tk=1024, ride=None):
    m, k = (a.shape[1], a.shape[0]) if ta else a.shape
    n = b.shape[0] if tb else b.shape[1]
    assert (b.shape[1] if tb else b.shape[0]) == k
    tm, tn, tk = _tile(m, tm), _tile(n, tn), _tile(k, tk)
    nk = k // tk
    a_spec = pl.BlockSpec((tk, tm), lambda i, j, l: (l, i)) if ta else pl.BlockSpec((tm, tk), lambda i, j, l: (i, l))
    b_spec = pl.BlockSpec((tn, tk), lambda i, j, l: (j, l)) if tb else pl.BlockSpec((tk, tn), lambda i, j, l: (l, j))
    dims = (((0,) if ta else (1,), (1,) if tb else (0,)), ((), ()))

    def body(a_ref, b_ref, o_ref, *acc):
        p = lax.dot_general(a_ref[...].astype(BF16), b_ref[...].astype(BF16), dims, preferred_element_type=F32)
        if nk == 1:
            o_ref[...] = p.astype(out_dtype)
            return
        acc_ref = acc[0]
        step = pl.program_id(2)

        @pl.when(step == 0)
        def _():
            acc_ref[...] = p

        @pl.when(step > 0)
        def _():
            acc_ref[...] += p

        @pl.when(step == nk - 1)
        def _():
            o_ref[...] = acc_ref[...].astype(out_dtype)

    outs, rode = _ride_call(
        body, ride, name=name, grid=(m // tm, n // tn, nk), in_specs=[a_spec, b_spec],
        out_specs=[pl.BlockSpec((tm, tn), lambda i, j, l: (i, j))], out_shape=[jax.ShapeDtypeStruct((m, n), out_dtype)],
        scratch_shapes=[] if nk == 1 else [pltpu.VMEM((tm, tn), F32)], sem=("parallel", "parallel", "arbitrary"),
        args=(a, b))
    return outs[0] if ride is None else (outs[0], rode)


def _rowwise(name, fn, rows, vecs, outs, sums=(), tr=512):
    s = rows[0][0].shape[0]
    tr = min(tr, s)
    nr, nv, no = len(rows), len(vecs), len(outs)

    def body(*refs):
        vals = [r[...] for r in refs[:nr + nv]]
        res = fn(*vals)
        for o_ref, val in zip(refs[nr + nv:nr + nv + no], res[:no]):
            o_ref[...] = val.astype(o_ref.dtype)
        step = pl.program_id(0)
        for s_ref, val in zip(refs[nr + nv + no:], res[no:]):
            @pl.when(step == 0)
            def _(s_ref=s_ref, val=val):
                s_ref[...] = val

            @pl.when(step > 0)
            def _(s_ref=s_ref, val=val):
                s_ref[...] += val

    in_specs = [pl.BlockSpec((tr, w), functools.partial(lambda i, cb: (i, cb), cb=cb)) for _, w, cb in rows]
    in_specs += [pl.BlockSpec(v.shape, lambda i: (0, 0)) for v in vecs]
    out_specs = [pl.BlockSpec((tr, w), lambda i: (i, 0)) for w, _ in outs]
    out_specs += [pl.BlockSpec((1, w), lambda i: (0, 0)) for w in sums]
    out_shape = [jax.ShapeDtypeStruct((s, w), dt) for w, dt in outs]
    out_shape += [jax.ShapeDtypeStruct((1, w), F32) for w in sums]
    return pl.pallas_call(
        body, name=name, grid=(s // tr,), in_specs=in_specs, out_specs=out_specs, out_shape=out_shape,
        compiler_params=_params(("arbitrary",) if sums else ("parallel",)),
    )(*[r[0] for r in rows], *vecs)


def _colsum(x):
    return jnp.sum(x, axis=0, keepdims=True)


def _sigmoid(x):
    return 1.0 / (1.0 + jnp.exp(-x))


def _log_sigmoid(x):
    return jnp.minimum(x, 0.0) - jnp.log(1.0 + jnp.exp(-jnp.abs(x)))


def _ln_stats(r):
    mu = jnp.mean(r, axis=-1, keepdims=True)
    xc = r - mu
    var = jnp.mean(xc * xc, axis=-1, keepdims=True)
    rstd = lax.rsqrt(var + LN_EPS)
    return xc * rstd, rstd


def _ln_bwd(dy, xhat, rstd, g):
    dxh = dy * g
    m1 = jnp.mean(dxh, axis=-1, keepdims=True)
    m2 = jnp.mean(dxh * xhat, axis=-1, keepdims=True)
    return rstd * (dxh - m1 - xhat * m2)


def _mod_exchange(c_row, w_ada, b_ada_loc):
    def body(c_ref, w_ref, b_ref, call_ref, mod_ref, piece_ref, send_sems, recv_sems):
        me = _me()[3]
        call_ref[me] = c_ref[...]
        sent = []
        for r in range(1, N_DEV):
            peer, _ = _peer(r)
            cp = pltpu.make_async_remote_copy(
                src_ref=c_ref, dst_ref=call_ref.at[me], send_sem=send_sems.at[0, r - 1],
                recv_sem=recv_sems.at[0, r - 1], device_id=peer, device_id_type=MESH)
            cp.start()
            sent.append(cp)
        for r in range(1, N_DEV):
            peer, pidx = _peer(r)
            pltpu.make_async_remote_copy(
                src_ref=c_ref, dst_ref=call_ref.at[pidx], send_sem=send_sems.at[0, r - 1],
                recv_sem=recv_sems.at[0, r - 1], device_id=peer, device_id_type=MESH).wait_recv()
        c_all = jnp.concatenate([call_ref[d] for d in range(N_DEV)], axis=0)
        mod_loc = jnp.dot(c_all, w_ref[...], preferred_element_type=F32,
                          precision=lax.Precision.HIGHEST) + b_ref[...]
        for d in range(N_DEV):
            piece_ref[d] = mod_loc[d:d + 1, :]
        mod_ref[me] = piece_ref[me]
        for r in range(1, N_DEV):
            peer, pidx = _peer(r)
            cp = pltpu.make_async_remote_copy(
                src_ref=piece_ref.at[pidx], dst_ref=mod_ref.at[me], send_sem=send_sems.at[1, r - 1],
                recv_sem=recv_sems.at[1, r - 1], device_id=peer, device_id_type=MESH)
            cp.start()
            sent.append(cp)
        for r in range(1, N_DEV):
            peer, pidx = _peer(r)
            pltpu.make_async_remote_copy(
                src_ref=piece_ref.at[me], dst_ref=mod_ref.at[pidx], send_sem=send_sems.at[1, r - 1],
                recv_sem=recv_sems.at[1, r - 1], device_id=peer, device_id_type=MESH).wait_recv()
        for cp in sent:
            cp.wait_send()

    vmem = pl.BlockSpec(memory_space=pltpu.VMEM)
    return pl.pallas_call(
        body, name="mod_exchange", in_specs=[vmem, vmem, vmem], out_specs=[vmem, vmem],
        out_shape=[jax.ShapeDtypeStruct((N_DEV, 1, D_MODEL), F32), jax.ShapeDtypeStruct((N_DEV, 1, ADA_SHARD), F32)],
        scratch_shapes=[pltpu.VMEM((N_DEV, 1, ADA_SHARD), F32),
                        pltpu.SemaphoreType.DMA((2, N_DEV - 1)), pltpu.SemaphoreType.DMA((2, N_DEV - 1))],
        compiler_params=_params(),
    )(c_row, w_ada, b_ada_loc)


def _split3(x):
    hi = x.astype(BF16)
    r1 = x - hi.astype(F32)
    mid = r1.astype(BF16)
    lo = (r1 - mid.astype(F32)).astype(BF16)
    return hi, mid, lo


def _scan_rows(x_ref, o_ref, s, reverse, pre=None, post=None):
    tb = min(TQ, s)
    nb = s // tb
    row = lax.broadcasted_iota(jnp.int32, (tb, tb), 0)
    col = lax.broadcasted_iota(jnp.int32, (tb, tb), 1)
    tri = jnp.where((col >= row) if reverse else (col <= row), 1.0, 0.0).astype(BF16)

    def step(i, carry):
        blk = (nb - 1 - i) if reverse else i
        off = pl.multiple_of(blk * tb, tb)
        x = x_ref[pl.ds(off, tb), :]
        if pre is not None:
            x = pre(x, off)
        acc = carry
        for piece in _split3(x):
            acc = acc + jnp.dot(tri, piece, preferred_element_type=F32)
        o_ref[pl.ds(off, tb), :] = acc if post is None else post(acc, off)
        edge = acc[0:1, :] if reverse else acc[tb - 1:tb, :]
        return jnp.broadcast_to(edge, (tb, LANES))

    lax.fori_loop(0, nb, step, jnp.zeros((tb, LANES), F32))


def _forget_cumsum(f_raw, b_pad):
    s = f_raw.shape[0]

    def body(f_ref, b_ref, cum_ref):
        b = b_ref[...]
        _scan_rows(f_ref, cum_ref, s, False, pre=lambda x, off: _log_sigmoid(x + b))

    vmem = pl.BlockSpec(memory_space=pltpu.VMEM)
    return pl.pallas_call(body, name="forget_cumsum", in_specs=[vmem, vmem], out_specs=vmem,
                          out_shape=jax.ShapeDtypeStruct((s, LANES), F32), compiler_params=_params())(f_raw, b_pad)


def _forget_bwd(dcum, f_raw, b_pad):
    s = f_raw.shape[0]

    def body(d_ref, f_ref, b_ref, df_ref, db_ref, tmp_ref):
        b = b_ref[...]
        _scan_rows(d_ref, tmp_ref, s, True)
        df = tmp_ref[...] * _sigmoid(-(f_ref[...] + b))
        df_ref[...] = df.astype(BF16)
        db_ref[...] = _colsum(df)

    vmem = pl.BlockSpec(memory_space=pltpu.VMEM)
    return pl.pallas_call(
        body, name="forget_bwd", in_specs=[vmem, vmem, vmem], out_specs=[vmem, vmem],
        out_shape=[jax.ShapeDtypeStruct((s, LANES), BF16), jax.ShapeDtypeStruct((1, LANES), F32)],
        scratch_shapes=[pltpu.VMEM((s, LANES), F32)], compiler_params=_params())(dcum, f_raw, b_pad)


def _dot_nt(a, b):
    return lax.dot_general(a, b, (((1,), (1,)), ((), ())), preferred_element_type=F32)


def _head_masks():
    lane = lax.broadcasted_iota(jnp.int32, (TQ, LANES), 1)
    return lane, [lane < HEAD_DIM, lane >= HEAD_DIM]


def _pick(mask, x):
    return jnp.where(mask, x, jnp.zeros_like(x))


def _qkv_specs(s, col0):
    nb = ATTN_W // LANES
    return [pl.BlockSpec((TQ, LANES), lambda hp, qi: (qi, col0 + hp)),
            pl.BlockSpec((s, LANES), lambda hp, qi: (0, col0 + nb + hp)),
            pl.BlockSpec((s, LANES), lambda hp, qi: (0, col0 + 2 * nb + hp))]


def _pair_spec():
    return pl.BlockSpec((TQ, LANES), lambda hp, qi: (qi, hp))


def _diag_mask(width, strict):
    row = lax.broadcasted_iota(jnp.int32, (TQ, width * TQ), 0) + (width - 1) * TQ
    col = lax.broadcasted_iota(jnp.int32, (TQ, width * TQ), 1)
    return (col < row) if strict else (col <= row)


def _fox_fwd(qkv, cum_col, cum_row, ride=None):
    s = qkv.shape[0]
    nq = s // TQ

    def body(q_ref, k_ref, v_ref, cc_ref, cr_ref, o_ref, o32_ref, lse_ref):
        hp, qi = pl.program_id(0), pl.program_id(1)
        lane, masks = _head_masks()
        q2 = q_ref[...] * jnp.asarray(ATTN_SCALE, BF16)
        cc = cc_ref[...]
        qms = [_pick(masks[e], q2) for e in range(2)]
        cqs = [jnp.sum(jnp.where(lane == 2 * hp + e, cc, 0.0), axis=1, keepdims=True) for e in range(2)]

        def tile(kb, carry, masked, width=1):
            off, span = pl.multiple_of(kb * TQ, TQ), width * TQ
            k2, v2 = k_ref[pl.ds(off, span), :], v_ref[pl.ds(off, span), :]
            head0 = lax.broadcasted_iota(jnp.int32, (span, LANES), 1) < HEAD_DIM
            new = []
            for e in range(2):
                m, acc = carry[e]
                sc = _dot_nt(qms[e], k2) + (cqs[e] - cr_ref[e:e + 1, pl.ds(off, span)])
                if masked:
                    sc = jnp.where(_diag_mask(width, False), sc, -jnp.inf)
                m_new = jnp.maximum(m, jnp.max(sc, axis=1, keepdims=True))
                p = jnp.exp(sc - m_new)
                v_ones = jnp.where(head0 if e == 0 else ~head0, v2, jnp.ones_like(v2))
                acc = jnp.exp(m - m_new) * acc + jnp.dot(p.astype(BF16), v_ones, preferred_element_type=F32)
                new.append((m_new, acc))
            return tuple(new)

        init = (jnp.full((TQ, 1), -jnp.inf, F32), jnp.zeros((TQ, LANES), F32))
        carry = lax.cond(qi % 2 == 1, lambda cr: tile(qi - 1, cr, True, 2), lambda cr: tile(qi, cr, True), (init, init))
        quads = qi // 4
        carry = lax.fori_loop(0, quads, lambda j, cr: tile(4 * j, cr, False, 4), carry)
        carry = lax.cond(qi % 4 >= 2, lambda cr: tile(4 * quads, cr, False, 2), lambda cr: cr, carry)
        sums = [jnp.max(jnp.where(masks[1 - e], carry[e][1], 0.0), axis=1, keepdims=True) for e in range(2)]
        outs = [carry[e][1] / sums[e] for e in range(2)]
        lses = [carry[e][0] + jnp.log(sums[e]) for e in range(2)]
        out = jnp.where(masks[0], outs[0], outs[1])
        o_ref[...] = out.astype(BF16)
        o32_ref[...] = out
        lse_ref[...] = jnp.where(masks[0], lses[0], lses[1])

    return _ride_call(
        body, ride, name="fox_fwd", grid=(N_HEADS // 2, nq),
        in_specs=_qkv_specs(s, 0) + [pl.BlockSpec((TQ, LANES), lambda hp, qi: (qi, 0)),
                                     pl.BlockSpec((None, 2, s), lambda hp, qi: (hp, 0, 0))],
        out_specs=[_pair_spec(), _pair_spec(), _pair_spec()],
        out_shape=[jax.ShapeDtypeStruct((s, ATTN_W), BF16), jax.ShapeDtypeStruct((s, ATTN_W), F32),
                   jax.ShapeDtypeStruct((s, ATTN_W), F32)],
        scratch_shapes=[], sem=("parallel", "parallel"), args=(qkv, qkv, qkv, cum_col, cum_row))


def _write_transposed(acc_ref, out_ref):
    for c in range(out_ref.shape[0] // TQ):
        out_ref[c * TQ:(c + 1) * TQ, :] = jnp.transpose(acc_ref[:, c * TQ:(c + 1) * TQ]).astype(BF16)


def _fox_bwd(qkv, cum_col, cum_row, o, lse, do, ride=None):
    s = qkv.shape[0]
    nq = s // TQ

    def body(q_ref, k_ref, v_ref, cc_ref, cr_ref, o_ref, lse_ref, do_ref,
             dq_ref, dk_ref, dv_ref, dcr_ref, dk_acc, dv_acc):
        hp, qi = pl.program_id(0), pl.program_id(1)

        @pl.when(qi == 0)
        def _():
            dk_acc[...] = jnp.zeros_like(dk_acc)
            dv_acc[...] = jnp.zeros_like(dv_acc)
            dcr_ref[...] = jnp.zeros_like(dcr_ref)

        lane, masks = _head_masks()
        q2 = q_ref[...] * jnp.asarray(ATTN_SCALE, BF16)
        do2 = do_ref[...]
        prod = do2.astype(F32) * o_ref[...].astype(F32)
        lse2 = lse_ref[...]
        cc = cc_ref[...]
        qms = [_pick(masks[e], q2) for e in range(2)]
        doms = [_pick(masks[e], do2) for e in range(2)]
        deltas = [jnp.sum(jnp.where(masks[e], prod, 0.0), axis=1, keepdims=True) for e in range(2)]
        lses = [jnp.max(jnp.where(masks[e], lse2, -jnp.inf), axis=1, keepdims=True) for e in range(2)]
        cqs = [jnp.sum(jnp.where(lane == 2 * hp + e, cc, 0.0), axis=1, keepdims=True) for e in range(2)]
        qts = [jnp.transpose(qms[e].astype(F32)).astype(BF16) for e in range(2)]
        dots = [jnp.transpose(doms[e].astype(F32)).astype(BF16) for e in range(2)]
        ones_row = [HEAD_DIM * (1 - e) for e in range(2)]
        trow = lax.broadcasted_iota(jnp.int32, (LANES, TQ), 0)
        qts = [jnp.where(trow == ones_row[e], jnp.ones_like(qts[e]), qts[e]) for e in range(2)]

        def tile(kb, carry, masked, width=1):
            off, span = pl.multiple_of(kb * TQ, TQ), width * TQ
            k2, v2 = k_ref[pl.ds(off, span), :], v_ref[pl.ds(off, span), :]
            head0 = lax.broadcasted_iota(jnp.int32, (span, LANES), 1) < HEAD_DIM
            new, dks, dv = [], [], None
            for e in range(2):
                dq = carry[e]
                sc = _dot_nt(qms[e], k2) + (cqs[e] - cr_ref[e:e + 1, pl.ds(off, span)])
                p = jnp.exp(sc - lses[e])
                if masked:
                    p = jnp.where(_diag_mask(width, False), p, 0.0)
                ds = p * (_dot_nt(doms[e], v2) - deltas[e])
                dsb = ds.astype(BF16)
                dk_e = jnp.dot(qts[e], dsb, preferred_element_type=F32)
                dv_e = jnp.dot(dots[e], p.astype(BF16), preferred_element_type=F32)
                dks.append(dk_e)
                dv = dv_e if e == 0 else dv + dv_e
                dcr_ref[e:e + 1, pl.ds(off, span)] -= dk_e[ones_row[e]:ones_row[e] + 1, :]
                k_ones = jnp.where(head0 if e == 0 else ~head0, k2, jnp.ones_like(k2))
                new.append(dq + jnp.dot(dsb, k_ones, preferred_element_type=F32))
            krow = lax.broadcasted_iota(jnp.int32, (LANES, span), 0)
            dk_acc[:, pl.ds(off, span)] += jnp.where(krow < HEAD_DIM, dks[0], dks[1])
            dv_acc[:, pl.ds(off, span)] += dv
            return tuple(new)

        init = jnp.zeros((TQ, LANES), F32)
        carry = lax.fori_loop(0, qi // 2, lambda j, cr: tile(2 * j, cr, False, 2), (init, init))
        carry = lax.cond(qi % 2 == 1, lambda cr: tile(qi - 1, cr, True, 2), lambda cr: tile(qi, cr, True), carry)
        dq_ref[...] = (jnp.where(masks[0], carry[0], carry[1]) * ATTN_SCALE).astype(BF16)
        for e in range(2):
            dcr_ref[e:e + 1, pl.ds(pl.multiple_of(qi * TQ, TQ), TQ)] += jnp.transpose(carry[e])[
                ones_row[e]:ones_row[e] + 1, :]

        @pl.when(qi == nq - 1)
        def _():
            _write_transposed(dk_acc, dk_ref)
            _write_transposed(dv_acc, dv_ref)

    seq_spec = pl.BlockSpec((s, LANES), lambda hp, qi: (0, hp))
    return _ride_call(
        body, ride, name="fox_bwd", grid=(N_HEADS // 2, nq),
        in_specs=_qkv_specs(s, 0) + [pl.BlockSpec((TQ, LANES), lambda hp, qi: (qi, 0)),
                                     pl.BlockSpec((None, 2, s), lambda hp, qi: (hp, 0, 0)),
                                     _pair_spec(), _pair_spec(), _pair_spec()],
        out_specs=[_pair_spec(), seq_spec, seq_spec, pl.BlockSpec((None, 2, s), lambda hp, qi: (hp, 0, 0))],
        out_shape=[jax.ShapeDtypeStruct((s, ATTN_W), BF16)] * 3 + [jax.ShapeDtypeStruct((N_HEADS // 2, 2, s), F32)],
        scratch_shapes=[pltpu.VMEM((LANES, s), F32), pltpu.VMEM((LANES, s), F32)],
        sem=("parallel", "arbitrary"), args=(qkv, qkv, qkv, cum_col, cum_row, o, lse, do))


def _scan_matrix(reverse):
    row = lax.broadcasted_iota(jnp.int32, (SCAN_W, SCAN_W), 0)
    col = lax.broadcasted_iota(jnp.int32, (SCAN_W, SCAN_W), 1)
    return jnp.where((row > col) if reverse else (row < col), 1.0, 0.0).astype(BF16)


def _scan_cols(x, tri, reverse, init):
    nblk = x.shape[1] // SCAN_W
    parts, total = [None] * nblk, init
    far = 0 if reverse else SCAN_W - 1
    for b in (reversed(range(nblk)) if reverse else range(nblk)):
        blk = x[:, b * SCAN_W:(b + 1) * SCAN_W]
        part = jnp.dot(blk.astype(BF16), tri, preferred_element_type=F32)
        parts[b] = part + total
        total = total + (part[:, far:far + 1] + blk[:, far:far + 1])
    return (parts[0] if nblk == 1 else jnp.concatenate(parts, axis=1)), total


def _sb_logits(qm, k2):
    z = _dot_nt(qm, k2)
    neg_abs = lax.bitcast_convert_type(lax.bitcast_convert_type(z, jnp.uint32) | jnp.uint32(0x80000000), F32)
    soft = jnp.log(1.0 + jnp.exp(neg_abs))
    lb = jnp.minimum(z, 0.0) - soft
    return lb, lb - z


TILE_SLOTS = 4


def _tri_base(qi):
    return (qi * (qi + 1)) // 2


def _sb_fwd(qkv):
    s = qkv.shape[0]
    nq = s // TQ

    def body(q_ref, k_ref, v_ref, o_ref, t_ref, buf, sems):
        hp, qi = pl.program_id(0), pl.program_id(1)
        _, masks = _head_masks()
        suffix = _scan_matrix(True)
        q2 = q_ref[...] * jnp.asarray(ATTN_SCALE, BF16)
        qms = [_pick(masks[e], q2) for e in range(2)]
        base = _tri_base(qi)

        def store(e, kb):
            slot = kb % TILE_SLOTS
            return pltpu.make_async_copy(buf.at[e, slot], t_ref.at[2 * hp + e, base + kb], sems.at[e, slot])

        def tile(kb, carry, masked, width=1):
            off, span = pl.multiple_of(kb * TQ, TQ), width * TQ
            k2, v2 = k_ref[pl.ds(off, span), :], v_ref[pl.ds(off, span), :]
            new = []
            for e in range(2):
                run, acc = carry[e]
                lb, lo = _sb_logits(qms[e], k2)
                if masked:
                    strict = _diag_mask(width, True)
                    lo = jnp.where(strict, lo, 0.0)
                rest, run = _scan_cols(lo, suffix, True, run)
                a = jnp.exp(lb + rest)
                if masked:
                    a = jnp.where(strict, a, 0.0)
                ab, lbb = a.astype(BF16), lb.astype(BF16)
                acc = acc + jnp.dot(ab, v2, preferred_element_type=F32)
                for w in range(width):
                    blk = kb + w

                    @pl.when(blk + TILE_SLOTS <= qi)
                    def _(e=e, blk=blk):
                        store(e, blk + TILE_SLOTS).wait()
                    buf[e, blk % TILE_SLOTS, 0] = ab[:, w * TQ:(w + 1) * TQ]
                    buf[e, blk % TILE_SLOTS, 1] = lbb[:, w * TQ:(w + 1) * TQ]
                    store(e, blk).start()
                new.append((run, acc))
            return tuple(new)

        init = (jnp.zeros((TQ, 1), F32), jnp.zeros((TQ, LANES), F32))
        carry = lax.cond(qi % 2 == 1, lambda cr: tile(qi - 1, cr, True, 2), lambda cr: tile(qi, cr, True), (init, init))
        pairs = qi // 2
        carry = lax.fori_loop(0, pairs, lambda it, cr: tile(2 * (pairs - 1 - it), cr, False, 2), carry)
        for e in range(2):
            for blk in range(TILE_SLOTS):
                @pl.when(qi >= blk)
                def _(e=e, blk=blk):
                    store(e, blk).wait()
        o_ref[...] = jnp.where(masks[0], carry[0][1], carry[1][1]).astype(BF16)

    ntri = nq * (nq + 1) // 2
    return pl.pallas_call(
        body, name="sb_fwd", grid=(N_HEADS // 2, nq), in_specs=_qkv_specs(s, 3 * ATTN_W // LANES),
        out_specs=[_pair_spec(), ANY],
        out_shape=[jax.ShapeDtypeStruct((s, ATTN_W), BF16), jax.ShapeDtypeStruct((N_HEADS, ntri, 2, TQ, TQ), BF16)],
        scratch_shapes=[pltpu.VMEM((2, TILE_SLOTS, 2, TQ, TQ), BF16), pltpu.SemaphoreType.DMA((2, TILE_SLOTS))],
        compiler_params=_params(("arbitrary", "arbitrary")),
    )(qkv, qkv, qkv)


def _sb_bwd(qkv, tiles, do):
    s = qkv.shape[0]
    nq = s // TQ

    def body(q_ref, k_ref, v_ref, t_ref, do_ref, dq_ref, dk_ref, dv_ref, dk_acc, dv_acc, buf, sems):
        hp, qi = pl.program_id(0), pl.program_id(1)

        @pl.when(qi == 0)
        def _():
            dk_acc[...] = jnp.zeros_like(dk_acc)
            dv_acc[...] = jnp.zeros_like(dv_acc)

        _, masks = _head_masks()
        prefix = _scan_matrix(False)
        q2 = q_ref[...] * jnp.asarray(ATTN_SCALE, BF16)
        do2 = do_ref[...]
        qms = [_pick(masks[e], q2) for e in range(2)]
        doms = [_pick(masks[e], do2) for e in range(2)]
        qts = [jnp.transpose(qms[e].astype(F32)).astype(BF16) for e in range(2)]
        dots = [jnp.transpose(doms[e].astype(F32)).astype(BF16) for e in range(2)]
        base = _tri_base(qi)

        def fetch(e, kb):
            slot = kb % TILE_SLOTS
            return pltpu.make_async_copy(t_ref.at[2 * hp + e, base + kb], buf.at[e, slot], sems.at[e, slot])

        for e in range(2):
            fetch(e, 0).start()

            @pl.when(qi >= 1)
            def _(e=e):
                fetch(e, 1).start()

        def tile(kb, carry, masked, width=1):
            off, span = pl.multiple_of(kb * TQ, TQ), width * TQ
            k2, v2 = k_ref[pl.ds(off, span), :], v_ref[pl.ds(off, span), :]
            new, dk, dv = [], None, None
            for e in range(2):
                gsum, dq = carry[e]
                if not masked:
                    for blk in range(2, 2 + width):
                        @pl.when(kb + blk <= qi)
                        def _(e=e, blk=blk):
                            fetch(e, kb + blk).start()
                for w in range(width):
                    fetch(e, kb + w).wait()
                slots = [(kb + w) % TILE_SLOTS for w in range(width)]
                ab = buf[e, slots[0], 0] if width == 1 else jnp.concatenate([buf[e, sl, 0] for sl in slots], axis=1)
                lbb = buf[e, slots[0], 1] if width == 1 else jnp.concatenate([buf[e, sl, 1] for sl in slots], axis=1)
                beta = jnp.exp(lbb.astype(F32))
                g = ab.astype(F32) * _dot_nt(doms[e], v2)
                before, gsum = _scan_cols(g, prefix, False, gsum)
                dz = g - beta * (g + before)
                if masked:
                    dz = jnp.where(_diag_mask(width, True), dz, 0.0)
                dzb = dz.astype(BF16)
                dk_e = jnp.dot(qts[e], dzb, preferred_element_type=F32)
                dv_e = jnp.dot(dots[e], ab, preferred_element_type=F32)
                dk, dv = (dk_e, dv_e) if e == 0 else (dk + dk_e, dv + dv_e)
                new.append((gsum, dq + jnp.dot(dzb, k2, preferred_element_type=F32)))
            dk_acc[:, pl.ds(off, span)] += dk
            dv_acc[:, pl.ds(off, span)] += dv
            return tuple(new)

        init = (jnp.zeros((TQ, 1), F32), jnp.zeros((TQ, LANES), F32))
        carry = lax.fori_loop(0, qi // 2, lambda j, cr: tile(2 * j, cr, False, 2), (init, init))
        carry = lax.cond(qi % 2 == 1, lambda cr: tile(qi - 1, cr, True, 2), lambda cr: tile(qi, cr, True), carry)
        dq_ref[...] = (jnp.where(masks[0], carry[0][1], carry[1][1]) * ATTN_SCALE).astype(BF16)

        @pl.when(qi == nq - 1)
        def _():
            _write_transposed(dk_acc, dk_ref)
            _write_transposed(dv_acc, dv_ref)

    seq_spec = pl.BlockSpec((s, LANES), lambda hp, qi: (0, hp))
    return pl.pallas_call(
        body, name="sb_bwd", grid=(N_HEADS // 2, nq),
        in_specs=_qkv_specs(s, 3 * ATTN_W // LANES) + [ANY, _pair_spec()],
        out_specs=[_pair_spec(), seq_spec, seq_spec],
        out_shape=[jax.ShapeDtypeStruct((s, ATTN_W), BF16)] * 3,
        scratch_shapes=[pltpu.VMEM((LANES, s), F32), pltpu.VMEM((LANES, s), F32),
                        pltpu.VMEM((2, TILE_SLOTS, 2, TQ, TQ), BF16), pltpu.SemaphoreType.DMA((2, TILE_SLOTS))],
        compiler_params=_params(("arbitrary", "arbitrary")),
    )(qkv, qkv, qkv, tiles, do)


CONV_TR = 256


def _shift_down(x, halo, n):
    rolled = pltpu.roll(x, n, 0)
    head = rolled[0:8, :]
    rid = lax.broadcasted_iota(jnp.int32, head.shape, 0)
    for j in range(n):
        head = jnp.where(rid == j, halo[8 - n + j:8 - n + j + 1, :], head)
    return jnp.concatenate([head, rolled[8:, :]], axis=0)


def _shift_up(x, halo, n):
    rows = x.shape[0]
    rolled = pltpu.roll(x, rows - n, 0)
    tail = rolled[rows - 8:, :]
    rid = lax.broadcasted_iota(jnp.int32, tail.shape, 0)
    for j in range(n):
        tail = jnp.where(rid == 8 - n + j, halo[j:j + 1, :], tail)
    return jnp.concatenate([rolled[:rows - 8, :], tail], axis=0)


def _conv_fwd_block(x, halo, w, b):
    return b + _shift_down(x, halo, 2) * w[0:1, :] + _shift_down(x, halo, 1) * w[1:2, :] + x * w[2:3, :]


def _conv_specs(tr, s):
    pair = 2 * FF_HALF
    blk = pl.BlockSpec((tr, pair), lambda j, i: (i, j))
    prev = pl.BlockSpec((8, pair), lambda j, i: (jnp.maximum(i * (tr // 8) - 1, 0), j))
    nxt = pl.BlockSpec((8, pair), lambda j, i: (jnp.minimum((i + 1) * (tr // 8), s // 8 - 1), j))
    return blk, prev, nxt


def _conv_gate_fwd(hpre, conv_w, conv_b):
    s = hpre.shape[0]
    tr = min(CONV_TR, s)
    blk, prev, _ = _conv_specs(tr, s)

    def body(x_ref, halo_ref, w_ref, b_ref, a_ref):
        i = pl.program_id(1)
        halo = jnp.where(i > 0, halo_ref[...], 0.0)
        h = _conv_fwd_block(x_ref[...], halo, w_ref[...], b_ref[...])
        hg, hv = h[:, :FF_HALF], h[:, FF_HALF:]
        a_ref[...] = (hg * _sigmoid(hg) * hv).astype(BF16)

    return pl.pallas_call(
        body, name="conv_gate_fwd", grid=(2, s // tr),
        in_specs=[blk, prev, pl.BlockSpec((3, 2 * FF_HALF), lambda j, i: (0, j)),
                  pl.BlockSpec((1, 2 * FF_HALF), lambda j, i: (0, j))],
        out_specs=pl.BlockSpec((tr, FF_HALF), lambda j, i: (i, j)),
        out_shape=jax.ShapeDtypeStruct((s, D_FF), BF16),
        compiler_params=_params(("parallel", "parallel")),
    )(hpre, hpre, conv_w, conv_b)


def _conv_gate_bwd(hpre, da, conv_w, conv_b):
    s = hpre.shape[0]
    tr = min(CONV_TR, s)
    blk, prev, _ = _conv_specs(tr, s)

    def body(x_ref, halo_ref, da_ref, w_ref, b_ref, dh_ref, db_ref, dw_ref):
        i = pl.program_id(1)
        halo = jnp.where(i > 0, halo_ref[...], 0.0)
        x = x_ref[...]
        h = _conv_fwd_block(x, halo, w_ref[...], b_ref[...])
        hg, hv = h[:, :FF_HALF], h[:, FF_HALF:]
        da_blk = da_ref[...].astype(F32)
        sg = _sigmoid(hg)
        dhg = da_blk * hv * (sg * (1.0 + hg * (1.0 - sg)))
        dhv = da_blk * (hg * sg)
        dh_ref[:, :FF_HALF] = dhg.astype(BF16)
        dh_ref[:, FF_HALF:] = dhv.astype(BF16)
        x2, x1 = _shift_down(x, halo, 2), _shift_down(x, halo, 1)
        parts = []
        for lo, dpart in ((0, dhg), (FF_HALF, dhv)):
            cols = slice(lo, lo + FF_HALF)
            parts.append((cols, _colsum(dpart), _colsum(dpart * x2[:, cols]), _colsum(dpart * x1[:, cols]),
                          _colsum(dpart * x[:, cols])))

        @pl.when(i == 0)
        def _():
            for cols, db, dw0, dw1, dw2 in parts:
                db_ref[:, cols] = db
                dw_ref[0:1, cols] = dw0
                dw_ref[1:2, cols] = dw1
                dw_ref[2:3, cols] = dw2

        @pl.when(i > 0)
        def _():
            for cols, db, dw0, dw1, dw2 in parts:
                db_ref[:, cols] += db
                dw_ref[0:1, cols] += dw0
                dw_ref[1:2, cols] += dw1
                dw_ref[2:3, cols] += dw2

    pair = 2 * FF_HALF
    return pl.pallas_call(
        body, name="conv_gate_bwd", grid=(2, s // tr),
        in_specs=[blk, prev, pl.BlockSpec((tr, FF_HALF), lambda j, i: (i, j)),
                  pl.BlockSpec((3, pair), lambda j, i: (0, j)), pl.BlockSpec((1, pair), lambda j, i: (0, j))],
        out_specs=[blk, pl.BlockSpec((1, pair), lambda j, i: (0, j)), pl.BlockSpec((3, pair), lambda j, i: (0, j))],
        out_shape=[jax.ShapeDtypeStruct((s, 2 * D_FF), BF16), jax.ShapeDtypeStruct((1, 2 * D_FF), F32),
                   jax.ShapeDtypeStruct((3, 2 * D_FF), F32)],
        compiler_params=_params(("parallel", "arbitrary")),
    )(hpre, hpre, da, conv_w, conv_b)


def _conv_input_bwd(dh, conv_w):
    s = dh.shape[0]
    tr = min(CONV_TR, s)
    blk, _, _ = _conv_specs(tr, s)
    nblk = s // tr

    def body(x_ref, halo_ref, w_ref, o_ref):
        i = pl.program_id(1)
        halo = jnp.where(i < nblk - 1, halo_ref[...].astype(F32), 0.0)
        x, w = x_ref[...].astype(F32), w_ref[...]
        o_ref[...] = (x * w[2:3, :] + _shift_up(x, halo, 1) * w[1:2, :] + _shift_up(x, halo, 2) * w[0:1, :]).astype(BF16)

    nxt = pl.BlockSpec((16, 2 * FF_HALF), lambda j, i: (jnp.minimum((i + 1) * (tr // 16), s // 16 - 1), j))
    return pl.pallas_call(
        body, name="conv_input_bwd", grid=(2, nblk),
        in_specs=[blk, nxt, pl.BlockSpec((3, 2 * FF_HALF), lambda j, i: (0, j))], out_specs=blk,
        out_shape=jax.ShapeDtypeStruct((s, 2 * D_FF), BF16),
        compiler_params=_params(("parallel", "parallel")),
    )(dh, dh, conv_w)


def _adamw_math(w, g, m, v):
    m = ADAM_B1 * m + (1.0 - ADAM_B1) * g
    v = ADAM_B2 * v + (1.0 - ADAM_B2) * (g * g)
    m_hat = m / (1.0 - ADAM_B1 ** ADAM_STEP)
    v_hat = v / (1.0 - ADAM_B2 ** ADAM_STEP)
    delta = -ADAM_LR * (m_hat / (jnp.sqrt(v_hat) + ADAM_EPS) + ADAM_WD * w)
    return delta, m, v


def _adamw(name, g8, w, m, v):
    r, c = w.shape
    tr = _row_tile(r, c)

    def body(g_ref, w_ref, m_ref, v_ref, go_ref, d_ref, mo_ref, vo_ref):
        g = g_ref[0].astype(F32)
        for d in range(1, N_DEV):
            g = g + g_ref[d].astype(F32)
        delta, mn, vn = _adamw_math(w_ref[...], g, m_ref[...], v_ref[...])
        go_ref[...] = g
        d_ref[...] = delta
        mo_ref[...] = mn
        vo_ref[...] = vn

    spec = pl.BlockSpec((tr, c), lambda i: (i, 0))
    return pl.pallas_call(
        body, name=name, grid=(r // tr,),
        in_specs=[pl.BlockSpec((N_DEV, tr, c), lambda i: (0, i, 0)), spec, spec, spec], out_specs=[spec] * 4,
        out_shape=[jax.ShapeDtypeStruct((r, c), F32)] * 4, compiler_params=_params(("parallel",)),
    )(g8, w, m, v)


def _adamw_small(g8, ws, ms, vs):
    n = len(ws)
    offsets = [sum(w.shape[0] for w in ws[:i]) for i in range(n)]

    def body(*refs):
        g_ref, w_refs, m_refs, v_refs = refs[0], refs[1:1 + n], refs[1 + n:1 + 2 * n], refs[1 + 2 * n:1 + 3 * n]
        outs = refs[1 + 3 * n:]
        for i in range(n):
            rows = w_refs[i].shape[0]
            g = g_ref[0, offsets[i]:offsets[i] + rows, :]
            for d in range(1, N_DEV):
                g = g + g_ref[d, offsets[i]:offsets[i] + rows, :]
            delta, mn, vn = _adamw_math(w_refs[i][...], g, m_refs[i][...], v_refs[i][...])
            for k, val in enumerate((g, delta, mn, vn)):
                outs[k * n + i][...] = val

    vmem = pl.BlockSpec(memory_space=pltpu.VMEM)
    res = pl.pallas_call(
        body, name="adamw_small", in_specs=[vmem] * (1 + 3 * n), out_specs=[vmem] * (4 * n),
        out_shape=[jax.ShapeDtypeStruct(w.shape, F32) for _ in range(4) for w in ws], compiler_params=_params(),
    )(g8, *ws, *ms, *vs)
    return [res[k * n:(k + 1) * n] for k in range(4)]


def _adamw_ada(c_t, dmod, w, m, v):
    r, c = w.shape
    tr = _row_tile(r, c)

    def body(ct_ref, dm_ref, w_ref, m_ref, v_ref, go_ref, d_ref, mo_ref, vo_ref):
        ct, dm = ct_ref[...], dm_ref[...]
        g = ct[:, 0:1] * dm[0:1, :]
        for b in range(1, N_DEV):
            g = g + ct[:, b:b + 1] * dm[b:b + 1, :]
        delta, mn, vn = _adamw_math(w_ref[...], g, m_ref[...], v_ref[...])
        go_ref[...] = g
        d_ref[...] = delta
        mo_ref[...] = mn
        vo_ref[...] = vn

    spec = pl.BlockSpec((tr, c), lambda i: (i, 0))
    return pl.pallas_call(
        body, name="adamw_w_ada", grid=(r // tr,),
        in_specs=[pl.BlockSpec((tr, N_DEV), lambda i: (i, 0)), pl.BlockSpec((N_DEV, c), lambda i: (0, 0)),
                  spec, spec, spec],
        out_specs=[spec] * 4, out_shape=[jax.ShapeDtypeStruct((r, c), F32)] * 4,
        compiler_params=_params(("parallel",)),
    )(c_t, dmod, w, m, v)


def _cols_from_slots(g):
    n, r, c = g.shape
    return jnp.transpose(g, (1, 0, 2)).reshape(r, n * c)


def _cols_to_slots(w):
    r, c = w.shape
    return jnp.transpose(w.reshape(r, N_DEV, c // N_DEV), (1, 0, 2))


def _pair_cols(w):
    g0, g1 = w[..., 0:FF_HALF], w[..., FF_HALF:D_FF]
    v0, v1 = w[..., D_FF:D_FF + FF_HALF], w[..., D_FF + FF_HALF:]
    return jnp.concatenate([g0, v0, g1, v1], axis=-1)


def _unpair_cols(w):
    g0, v0 = w[..., 0:FF_HALF], w[..., FF_HALF:D_FF]
    g1, v1 = w[..., D_FF:D_FF + FF_HALF], w[..., D_FF + FF_HALF:]
    return jnp.concatenate([g0, g1, v0, v1], axis=-1)


def _row(v):
    return v.reshape(1, -1)


def kernel(x, c, w_ada, b_ada, w_in, b_forget, w_fox_proj, w_sb_proj, w_o, ln1_g, ln1_b, w_up, conv_w, conv_b, w_down, ln2_g, ln2_b, loss_target, m_w_ada, m_b_ada, m_w_in, m_b_forget, m_w_fox_proj, m_w_sb_proj, m_w_o, m_ln1_g, m_ln1_b, m_w_up, m_conv_w, m_conv_b, m_w_down, m_ln2_g, m_ln2_b, v_w_ada, v_b_ada, v_w_in, v_b_forget, v_w_fox_proj, v_w_sb_proj, v_w_o, v_ln1_g, v_ln1_b, v_w_up, v_conv_w, v_conv_b, v_w_down, v_ln2_g, v_ln2_b):
    s = x.shape[1]
    me = 4 * lax.axis_index("x") + 2 * lax.axis_index("y") + lax.axis_index("c")
    x2 = x.reshape(s, D_MODEL)
    tgt = loss_target.reshape(s, D_MODEL)

    b_ada_loc = lax.dynamic_slice(b_ada, (me * ADA_SHARD,), (ADA_SHARD,)).reshape(1, ADA_SHARD)
    c_all, mod = _mod_exchange(c, w_ada, b_ada_loc)
    mod = mod.reshape(N_MOD, 1, D_MODEL)
    sh1, sc1, gt1, sh2, sc2, gt2 = [mod[i] for i in range(N_MOD)]

    g_in = _allgather_two_level("ag_w_in", w_in.astype(BF16))
    late_weights = _Ride([w_fox_proj.astype(BF16), w_sb_proj.astype(BF16), w_o.astype(BF16), w_up.astype(BF16),
                          w_down.astype(BF16), conv_w], scatter=False)
    w_in_f = _cols_from_slots(g_in)
    w_proj = jnp.concatenate(
        [w_in_f[:, 0:1536], w_in_f[:, 1544:3080], w_in_f[:, 3080:5128], w_in_f[:, 1536:1544],
         jnp.zeros((D_MODEL, W_PROJ - 5128), BF16)], axis=1)
    w_qkv, w_gates, w_f = w_proj[:, :W_QKV], w_proj[:, W_QKV:W_QKV + W_GATES], w_proj[:, W_QKV + W_GATES:W_QKV + W_GATES + W_F]
    conv_b_p = _pair_cols(_row(conv_b))
    b_f_pad = jnp.pad(_row(b_forget), ((0, 0), (0, LANES - N_HEADS)))

    (u1,) = _rowwise("modulate1", lambda xb, sc, sh: (xb * (1.0 + sc) + sh,),
                     [(x2, D_MODEL, 0)], [sc1, sh1], [(D_MODEL, BF16)], tr=512)
    qkv = _mm(u1, w_qkv, name="mm_qkv", out_dtype=BF16)
    gates = _mm(u1, w_gates, name="mm_gates")
    f_raw = _mm(u1, w_f, name="mm_forget")
    cum_col = _forget_cumsum(f_raw, b_f_pad)
    cum_row = jnp.transpose(cum_col[:, :N_HEADS]).reshape(N_HEADS // 2, 2, s)
    (y_fox, y_fox32, lse), (g_fox, g_sb, g_o, g_up, g_down, g_cw) = _fox_fwd(qkv, cum_col, cum_row, ride=late_weights)
    w_fox_f = _cols_from_slots(g_fox)
    w_sb_f = _cols_from_slots(g_sb)
    w_o_f = g_o.reshape(D_MODEL, D_MODEL)
    w_up_p = _pair_cols(_cols_from_slots(g_up))
    w_down_f = g_down.reshape(D_FF, D_MODEL)
    conv_w_p = _pair_cols(_cols_from_slots(g_cw))
    y_sb, sb_run = _sb_fwd(qkv)
    pf = _mm(y_fox, w_fox_f, name="mm_fox_proj", out_dtype=BF16)
    ps = _mm(y_sb, w_sb_f, name="mm_sb_proj", out_dtype=BF16)
    (merged,) = _rowwise("gate_merge", lambda ga, gb, a, b: (_sigmoid(ga) * a + _sigmoid(gb) * b,),
                         [(gates, D_MODEL, 0), (gates, D_MODEL, 1), (pf, D_MODEL, 0), (ps, D_MODEL, 0)], [],
                         [(D_MODEL, BF16)])
    attn_out = _mm(merged, w_o_f, name="mm_w_o")

    def ln_fwd(xb, fb, gt, g, b):
        xhat, _ = _ln_stats(ALPHA * xb + (1.0 + gt) * fb)
        return xhat * g + b

    def ln1_mod(xb, fb, gt, g, b, sc, sh):
        y = ln_fwd(xb, fb, gt, g, b)
        return y, y * (1.0 + sc) + sh

    x1, u2 = _rowwise("ln1_modulate2", ln1_mod, [(x2, D_MODEL, 0), (attn_out, D_MODEL, 0)],
                      [gt1, _row(ln1_g), _row(ln1_b), sc2, sh2], [(D_MODEL, F32), (D_MODEL, BF16)])

    hpre = _mm(u2, w_up_p, name="mm_w_up", tn=1408)
    act = _conv_gate_fwd(hpre, conv_w_p, conv_b_p)
    ffn_out = _mm(act, w_down_f, name="mm_w_down", tk=2816)

    def ln2_bwd(xb, fb, tb, gt, g, b):
        xhat, rstd = _ln_stats(ALPHA * xb + (1.0 + gt) * fb)
        err = (xhat * g + b) - tb
        dy = err * (1.0 / D_MODEL)
        dr = _ln_bwd(dy, xhat, rstd, g)
        return (dr * (1.0 + gt), ALPHA * dr,
                _colsum(err * err), _colsum(dy * xhat), _colsum(dy), _colsum(dr * fb))

    dffn, dx1_res, sq_err, d_ln2_g, d_ln2_b, d_gt2 = _rowwise(
        "ln2_bwd", ln2_bwd, [(x1, D_MODEL, 0), (ffn_out, D_MODEL, 0), (tgt, D_MODEL, 0)],
        [gt2, _row(ln2_g), _row(ln2_b)], [(D_MODEL, BF16), (D_MODEL, F32)], sums=[D_MODEL] * 4)
    loss = lax.psum(0.5 * jnp.sum(sq_err) / D_MODEL, ("x", "y", "c"))

    d_w_down = _mm(act, dffn, name="mm_d_w_down", ta=True, tm=1408, tk=2048, out_dtype=BF16)
    d_act = _mm(dffn, w_down_f, name="mm_d_act", tb=True, tn=1408, out_dtype=BF16)
    dh, d_conv_b_p, d_conv_w_p = _conv_gate_bwd(hpre, d_act, conv_w_p, conv_b_p)
    dhpre = _conv_input_bwd(dh, conv_w_p)
    d_w_up_p = _mm(u2, dhpre, name="mm_d_w_up", ta=True, tn=1408, tk=2048, out_dtype=BF16)
    du2 = _mm(dhpre, w_up_p, name="mm_d_u2", tb=True, tk=2816)

    def ln1_bwd(du, dres, x1b, xb, fb, sc, gt, g):
        dx1 = dres + du * (1.0 + sc)
        xhat, rstd = _ln_stats(ALPHA * xb + (1.0 + gt) * fb)
        dr = _ln_bwd(dx1, xhat, rstd, g)
        return (dr * (1.0 + gt), ALPHA * dr,
                _colsum(du * x1b), _colsum(du), _colsum(dx1 * xhat), _colsum(dx1), _colsum(dr * fb))

    d_attn, dx_res, d_sc2, d_sh2, d_ln1_g, d_ln1_b, d_gt1 = _rowwise(
        "ln1_bwd", ln1_bwd,
        [(du2, D_MODEL, 0), (dx1_res, D_MODEL, 0), (x1, D_MODEL, 0), (x2, D_MODEL, 0), (attn_out, D_MODEL, 0)],
        [sc2, gt1, _row(ln1_g)], [(D_MODEL, BF16), (D_MODEL, F32)], sums=[D_MODEL] * 5)

    d_w_o = _mm(merged, d_attn, name="mm_d_w_o", ta=True, out_dtype=BF16)
    d_merged = _mm(d_attn, w_o_f, name="mm_d_merged", tb=True, out_dtype=BF16)

    def merge_bwd(dm, ga, gb, a, b):
        dm, a, b = dm.astype(F32), a.astype(F32), b.astype(F32)
        sa, sb = _sigmoid(ga), _sigmoid(gb)
        return dm * a * sa * (1.0 - sa), dm * b * sb * (1.0 - sb), dm * sa, dm * sb

    d_ga, d_gb, d_pf, d_ps = _rowwise(
        "gate_merge_bwd", merge_bwd,
        [(d_merged, D_MODEL, 0), (gates, D_MODEL, 0), (gates, D_MODEL, 1), (pf, D_MODEL, 0), (ps, D_MODEL, 0)], [],
        [(D_MODEL, BF16)] * 4)
    d_w_fox = _mm(y_fox, d_pf, name="mm_d_w_fox", ta=True, out_dtype=BF16)
    d_w_sb = _mm(y_sb, d_ps, name="mm_d_w_sb", ta=True, out_dtype=BF16)
    d_y_fox = _mm(d_pf, w_fox_f, name="mm_d_y_fox", tb=True, out_dtype=BF16)
    d_y_sb = _mm(d_ps, w_sb_f, name="mm_d_y_sb", tb=True, out_dtype=BF16)
    early_grads = _Ride(
        [_cols_to_slots(d_w_fox), _cols_to_slots(d_w_sb), d_w_o.reshape(N_DEV, D_MODEL // N_DEV, D_MODEL),
         _cols_to_slots(_unpair_cols(d_w_up_p)), d_w_down.reshape(N_DEV, D_FF // N_DEV, D_MODEL)], scatter=True)
    (dq_a, dk_a, dv_a, d_cum_row), early_slots = _fox_bwd(qkv, cum_col, cum_row, y_fox32, lse, d_y_fox,
                                                                    ride=early_grads)
    dq_b, dk_b, dv_b = _sb_bwd(qkv, sb_run, d_y_sb)
    d_cum = jnp.transpose(d_cum_row.reshape(N_HEADS, s))
    d_cum = jnp.pad(d_cum, ((0, 0), (0, LANES - N_HEADS)))
    d_f, d_b_forget = _forget_bwd(d_cum, f_raw, b_f_pad)
    d_proj = jnp.concatenate([dq_a, dk_a, dv_a, dq_b, dk_b, dv_b, d_ga, d_gb, d_f,
                              jnp.zeros((s, W_PROJ - W_QKV - W_GATES - W_F), BF16)], axis=1)
    d_w_proj = _mm(u1, d_proj, name="mm_d_w_in", ta=True, tn=896, tk=2048, out_dtype=BF16)
    d_w_in_f = jnp.concatenate([d_w_proj[:, 0:1536], d_w_proj[:, 5120:5128], d_w_proj[:, 1536:3072],
                                d_w_proj[:, 3072:5120]], axis=1)
    du1, (in_slots,) = _mm(d_proj, w_proj, name="mm_d_u1", tb=True, tk=2688,
                           ride=_Ride([_cols_to_slots(d_w_in_f)], scatter=True))

    def x_bwd(du, dres, xb, sc):
        return dres + du * (1.0 + sc), _colsum(du * xb), _colsum(du)

    grad_x, d_sc1, d_sh1 = _rowwise("x_bwd", x_bwd, [(du1, D_MODEL, 0), (dx_res, D_MODEL, 0), (x2, D_MODEL, 0)],
                                    [sc1], [(D_MODEL, F32)], sums=[D_MODEL] * 2, tr=512)

    d_conv_b = _unpair_cols(d_conv_b_p)
    d_conv_w = _unpair_cols(d_conv_w_p)
    cb_pad = N_MOD * D_MODEL - 2 * D_FF
    small = jnp.concatenate(
        [d_sh1, d_sc1, d_gt1, d_sh2, d_sc2, d_gt2, jnp.pad(d_b_forget, ((0, 0), (0, D_MODEL - LANES))),
         d_ln1_g, d_ln1_b, d_ln2_g, d_ln2_b, jnp.pad(d_conv_b, ((0, 0), (0, cb_pad))),
         d_conv_w.reshape(1, 6 * D_FF)], axis=1)
    n_small = small.shape[1] // LANES
    small = jnp.pad(small.reshape(n_small, LANES), ((0, -n_small % 8), (0, 0)))
    (small_all,) = _exchange("ag_small_grads", [small], scatter=False)
    n_rep = n_small - 6 * D_FF // LANES
    cw8 = small_all[:, n_rep:n_small, :].reshape(N_DEV, 3, 2 * D_FF)
    cw8 = lax.dynamic_slice(cw8, (0, 0, me * UP_SHARD), (N_DEV, 3, UP_SHARD))
    dmod8 = small_all[:, :N_MOD * D_MODEL // LANES, :].reshape(N_DEV, N_MOD * D_MODEL)
    dmod_loc = lax.dynamic_slice(dmod8, (0, me * ADA_SHARD), (N_DEV, ADA_SHARD))

    def rows_of(a):
        return a.reshape(-1, LANES)

    def forget_rows(a):
        return rows_of(jnp.pad(a, (0, D_MODEL - N_HEADS)))

    def conv_b_rows(a):
        return rows_of(jnp.pad(a, (0, cb_pad)))

    rep_sizes = {"b_ada": N_MOD * D_MODEL, "b_forget": N_HEADS, "ln1_g": D_MODEL, "ln1_b": D_MODEL, "ln2_g": D_MODEL,
                 "ln2_b": D_MODEL, "conv_b": 2 * D_FF}
    rep_w = [rows_of(b_ada), forget_rows(b_forget), rows_of(ln1_g), rows_of(ln1_b), rows_of(ln2_g), rows_of(ln2_b),
             conv_b_rows(conv_b)]
    rep_m = [rows_of(m_b_ada), forget_rows(m_b_forget), rows_of(m_ln1_g), rows_of(m_ln1_b), rows_of(m_ln2_g),
             rows_of(m_ln2_b), conv_b_rows(m_conv_b)]
    rep_v = [rows_of(v_b_ada), forget_rows(v_b_forget), rows_of(v_ln1_g), rows_of(v_ln1_b), rows_of(v_ln2_g),
             rows_of(v_ln2_b), conv_b_rows(v_conv_b)]
    rep_out = _adamw_small(small_all, rep_w, rep_m, rep_v)
    rep = [{name: a.reshape(-1)[:size] for (name, size), a in zip(rep_sizes.items(), outs)} for outs in rep_out]
    r_conv_w = _adamw("adamw_conv_w", cw8, conv_w, m_conv_w, v_conv_w)
    r_ada = _adamw_ada(jnp.transpose(c_all.reshape(N_DEV, D_MODEL)), dmod_loc, w_ada, m_w_ada, v_w_ada)

    r_in = _adamw("adamw_w_in", in_slots, w_in, m_w_in, v_w_in)
    r_fox = _adamw("adamw_w_fox", early_slots[0], w_fox_proj, m_w_fox_proj, v_w_fox_proj)
    r_sb = _adamw("adamw_w_sb", early_slots[1], w_sb_proj, m_w_sb_proj, v_w_sb_proj)
    r_o = _adamw("adamw_w_o", early_slots[2], w_o, m_w_o, v_w_o)
    r_up = _adamw("adamw_w_up", early_slots[3], w_up, m_w_up, v_w_up)
    r_down = _adamw("adamw_w_down", early_slots[4], w_down, m_w_down, v_w_down)

    def leaf(i):
        return [r_ada[i], rep[i]["b_ada"], r_in[i], rep[i]["b_forget"], r_fox[i], r_sb[i], r_o[i], rep[i]["ln1_g"],
                rep[i]["ln1_b"], r_up[i], r_conv_w[i], rep[i]["conv_b"], r_down[i], rep[i]["ln2_g"], rep[i]["ln2_b"]]

    return (loss, grad_x.reshape(1, s, D_MODEL), *leaf(0), *leaf(1), *leaf(2), *leaf(3))
```

```python
import functools

import jax
import jax.numpy as jnp
from jax import lax
from jax.experimental import pallas as pl
from jax.experimental.pallas import tpu as pltpu

F32 = jnp.float32
BF16 = jnp.bfloat16
MESH = pl.DeviceIdType.MESH
ANY = pl.BlockSpec(memory_space=pl.ANY)

N_DEV = 8
D_MODEL = 1024
HEAD_DIM = 64
N_HEADS = 8
ATTN_W = N_HEADS * HEAD_DIM
D_FF = 2816
FF_HALF = D_FF // 2
N_MOD = 6
ADA_SHARD = N_MOD * D_MODEL // N_DEV
IN_SHARD = 641
UP_SHARD = 704
ATTN_SCALE = HEAD_DIM ** -0.5
ALPHA = 2.0 ** 0.25
LN_EPS = 1e-5
LANES = 128
TQ = 512
SCAN_W = 256
VMEM_LIMIT = 56 * 1024 * 1024

ADAM_LR, ADAM_B1, ADAM_B2, ADAM_EPS, ADAM_WD, ADAM_STEP = 0.001, 0.9, 0.999, 1e-08, 0.01, 10

W_QKV, W_GATES, W_F = 3072, 2048, 128
W_PROJ = 5376


def _params(sem=None):
    return pltpu.CompilerParams(dimension_semantics=sem, vmem_limit_bytes=VMEM_LIMIT)


def _tile(n, cap):
    if n <= cap:
        return n
    best = None
    for t in range(LANES, cap + 1, LANES):
        if n % t == 0:
            best = t
    assert best is not None, (n, cap)
    return best


def _row_tile(r, width, budget=192 * 1024):
    if r * width <= budget or r % 16:
        return r
    best = 16
    for t in range(16, r + 1, 16):
        if r % t == 0 and t * width <= budget:
            best = t
    return best


def _me():
    x, y, c = lax.axis_index("x"), lax.axis_index("y"), lax.axis_index("c")
    return x, y, c, 4 * x + 2 * y + c


def _peer(r):
    x, y, c, _ = _me()
    px = 1 - x if r & 4 else x
    py = 1 - y if r & 2 else y
    pc = 1 - c if r & 1 else c
    return (px, py, pc), 4 * px + 2 * py + pc


class _Ride:
    def __init__(self, arrays, scatter):
        self.arrays, self.scatter, self.n = list(arrays), scatter, len(arrays)
        self.in_specs = [ANY] * self.n
        self.out_specs = [ANY] * self.n
        self.out_shape = [jax.ShapeDtypeStruct(a.shape if scatter else (N_DEV,) + a.shape, a.dtype) for a in arrays]
        self.scratch = [pltpu.SemaphoreType.DMA((self.n, N_DEV - 1)), pltpu.SemaphoreType.DMA((self.n, N_DEV - 1)),
                        pltpu.SemaphoreType.DMA((self.n,))]

    def _local(self, ins, outs, sems, a):
        me = _me()[3]
        return pltpu.make_async_copy(ins[a].at[me] if self.scatter else ins[a], outs[a].at[me], sems[2].at[a])

    def _remote(self, ins, outs, sems, a, r, arriving):
        me = _me()[3]
        peer, pidx = _peer(r)
        src = ins[a].at[me if arriving else pidx] if self.scatter else ins[a]
        return pltpu.make_async_remote_copy(
            src_ref=src, dst_ref=outs[a].at[pidx if arriving else me], send_sem=sems[0].at[a, r - 1],
            recv_sem=sems[1].at[a, r - 1], device_id=peer, device_id_type=MESH)

    def start(self, ins, outs, sems):
        for a in range(self.n):
            self._local(ins, outs, sems, a).start()
        for r in range(1, N_DEV):
            for a in range(self.n):
                self._remote(ins, outs, sems, a, r, False).start()

    def wait(self, ins, outs, sems):
        for r in range(1, N_DEV):
            for a in range(self.n):
                self._remote(ins, outs, sems, a, r, True).wait_recv()
        for r in range(1, N_DEV):
            for a in range(self.n):
                self._remote(ins, outs, sems, a, r, False).wait_send()
        for a in range(self.n):
            self._local(ins, outs, sems, a).wait()


def _exchange(name, arrays, scatter):
    ride = _Ride(arrays, scatter)

    def body(*refs):
        ins, outs, sems = refs[:ride.n], refs[ride.n:2 * ride.n], refs[2 * ride.n:]
        ride.start(ins, outs, sems)
        ride.wait(ins, outs, sems)

    return pl.pallas_call(body, name=name, in_specs=ride.in_specs, out_specs=ride.out_specs, out_shape=ride.out_shape,
                          scratch_shapes=ride.scratch)(*arrays)


def _allgather_two_level(name, a):
    def body(a_ref, out_ref, send_sems, recv_sems, local_sem):
        x, y, c, me = _me()
        sibling = (x, y, 1 - c)
        chips = [(1 - x, y), (x, 1 - y), (1 - x, 1 - y)]

        def idx(px, py, pc):
            return 4 * px + 2 * py + pc

        def copy(k, block, to, src=None):
            slot = out_ref.at[idx(*block)]
            return pltpu.make_async_remote_copy(
                src_ref=slot if src is None else src, dst_ref=slot, send_sem=send_sems.at[k], recv_sem=recv_sems.at[k],
                device_id=to, device_id_type=MESH)

        mine = pltpu.make_async_copy(a_ref, out_ref.at[me], local_sem)
        mine.start()
        first = [copy(0, (x, y, c), sibling, src=a_ref)]
        first += [copy(1 + j, (x, y, c), (*chip, c), src=a_ref) for j, chip in enumerate(chips)]
        for cp in first:
            cp.start()
        passed = [copy(4 + j, (*chip, c), sibling) for j, chip in enumerate(chips)]
        for j, chip in enumerate(chips):
            copy(1 + j, (*chip, c), (x, y, c)).wait_recv()
            passed[j].start()
        copy(0, sibling, (x, y, c)).wait_recv()
        for j, chip in enumerate(chips):
            copy(4 + j, (*chip, 1 - c), (x, y, c)).wait_recv()
        for cp in first + passed:
            cp.wait_send()
        mine.wait()

    return pl.pallas_call(
        body, name=name, in_specs=[ANY], out_specs=ANY,
        out_shape=jax.ShapeDtypeStruct((N_DEV,) + a.shape, a.dtype),
        scratch_shapes=[pltpu.SemaphoreType.DMA((N_DEV - 1,)), pltpu.SemaphoreType.DMA((N_DEV - 1,)),
                        pltpu.SemaphoreType.DMA],
    )(a)


def _with_ride(body, ride, n_in, n_out, grid):
    if ride is None:
        return body
    n = ride.n

    def wrapped(*refs):
        ins, rins = refs[:n_in], refs[n_in:n_in + n]
        outs, routs = refs[n_in + n:n_in + n + n_out], refs[n_in + n + n_out:n_in + 2 * n + n_out]
        rest = refs[n_in + 2 * n + n_out:]
        scratch, sems = rest[:len(rest) - 3], rest[len(rest) - 3:]
        ids = [pl.program_id(d) for d in range(len(grid))]
        first = functools.reduce(lambda p, q: p & q, [i == 0 for i in ids])
        last = functools.reduce(lambda p, q: p & q, [i == g - 1 for i, g in zip(ids, grid)])

        @pl.when(first)
        def _():
            ride.start(rins, routs, sems)

        body(*ins, *outs, *scratch)

        @pl.when(last)
        def _():
            ride.wait(rins, routs, sems)

    return wrapped


def _ride_call(body, ride, *, name, grid, in_specs, out_specs, out_shape, scratch_shapes, sem, args):
    n_in, n_out = len(in_specs), len(out_specs)
    if ride is None:
        res = pl.pallas_call(body, name=name, grid=grid, in_specs=in_specs, out_specs=out_specs, out_shape=out_shape,
                             scratch_shapes=scratch_shapes, compiler_params=_params(sem))(*args)
        return list(res), []
    res = pl.pallas_call(
        _with_ride(body, ride, n_in, n_out, grid), name=name, grid=grid,
        in_specs=list(in_specs) + ride.in_specs, out_specs=list(out_specs) + ride.out_specs,
        out_shape=list(out_shape) + ride.out_shape, scratch_shapes=list(scratch_shapes) + ride.scratch,
        compiler_params=_params(("arbitrary",) * len(grid)))(*args, *ride.arrays)
    return list(res[:n_out]), list(res[n_out:])


def _mm(a, b, *, name, ta=False, tb=False, out_dtype=F32, tm=1024, tn=1024, tk=1024, ride=None):
    m, k = (a.shape[1], a.shape[0]) if ta else a.shape
    n = b.shape[0] if tb else b.shape[1]
    assert (b.shape[1] if tb else b.shape[0]) == k
    tm, tn, tk = _tile(m, tm), _tile(n, tn), _tile(k, tk)
    nk = k // tk
    a_spec = pl.BlockSpec((tk, tm), lambda i, j, l: (l, i)) if ta else pl.BlockSpec((tm, tk), lambda i, j, l: (i, l))
    b_spec = pl.BlockSpec((tn, tk), lambda i, j, l: (j, l)) if tb else pl.BlockSpec((tk, tn), lambda i, j, l: (l, j))
    dims = (((0,) if ta else (1,), (1,) if tb else (0,)), ((), ()))

    def body(a_ref, b_ref, o_ref, *acc):
        p = lax.dot_general(a_ref[...].astype(BF16), b_ref[...].astype(BF16), dims, preferred_element_type=F32)
        if nk == 1:
            o_ref[...] = p.astype(out_dtype)
            return
        acc_ref = acc[0]
        step = pl.program_id(2)

        @pl.when(step == 0)
        def _():
            acc_ref[...] = p

        @pl.when(step > 0)
        def _():
            acc_ref[...] += p

        @pl.when(step == nk - 1)
        def _():
            o_ref[...] = acc_ref[...].astype(out_dtype)

    outs, rode = _ride_call(
        body, ride, name=name, grid=(m // tm, n // tn, nk), in_specs=[a_spec, b_spec],
        out_specs=[pl.BlockSpec((tm, tn), lambda i, j, l: (i, j))], out_shape=[jax.ShapeDtypeStruct((m, n), out_dtype)],
        scratch_shapes=[] if nk == 1 else [pltpu.VMEM((tm, tn), F32)], sem=("parallel", "parallel", "arbitrary"),
        args=(a, b))
    return outs[0] if ride is None else (outs[0], rode)


def _rowwise(name, fn, rows, vecs, outs, sums=(), tr=512):
    s = rows[0][0].shape[0]
    tr = min(tr, s)
    nr, nv, no = len(rows), len(vecs), len(outs)

    def body(*refs):
        vals = [r[...] for r in refs[:nr + nv]]
        res = fn(*vals)
        for o_ref, val in zip(refs[nr + nv:nr + nv + no], res[:no]):
            o_ref[...] = val.astype(o_ref.dtype)
        step = pl.program_id(0)
        for s_ref, val in zip(refs[nr + nv + no:], res[no:]):
            @pl.when(step == 0)
            def _(s_ref=s_ref, val=val):
                s_ref[...] = val

            @pl.when(step > 0)
            def _(s_ref=s_ref, val=val):
                s_ref[...] += val

    in_specs = [pl.BlockSpec((tr, w), functools.partial(lambda i, cb: (i, cb), cb=cb)) for _, w, cb in rows]
    in_specs += [pl.BlockSpec(v.shape, lambda i: (0, 0)) for v in vecs]
    out_specs = [pl.BlockSpec((tr, w), lambda i: (i, 0)) for w, _ in outs]
    out_specs += [pl.BlockSpec((1, w), lambda i: (0, 0)) for w in sums]
    out_shape = [jax.ShapeDtypeStruct((s, w), dt) for w, dt in outs]
    out_shape += [jax.ShapeDtypeStruct((1, w), F32) for w in sums]
    return pl.pallas_call(
        body, name=name, grid=(s // tr,), in_specs=in_specs, out_specs=out_specs, out_shape=out_shape,
        compiler_params=_params(("arbitrary",) if sums else ("parallel",)),
    )(*[r[0] for r in rows], *vecs)


def _colsum(x):
    return jnp.sum(x, axis=0, keepdims=True)


def _sigmoid(x):
    return 1.0 / (1.0 + jnp.exp(-x))


def _log_sigmoid(x):
    return jnp.minimum(x, 0.0) - jnp.log(1.0 + jnp.exp(-jnp.abs(x)))


def _ln_stats(r):
    mu = jnp.mean(r, axis=-1, keepdims=True)
    xc = r - mu
    var = jnp.mean(xc * xc, axis=-1, keepdims=True)
    rstd = lax.rsqrt(var + LN_EPS)
    return xc * rstd, rstd


def _ln_bwd(dy, xhat, rstd, g):
    dxh = dy * g
    m1 = jnp.mean(dxh, axis=-1, keepdims=True)
    m2 = jnp.mean(dxh * xhat, axis=-1, keepdims=True)
    return rstd * (dxh - m1 - xhat * m2)


def _mod_exchange(c_row, w_ada, b_ada_loc):
    def body(c_ref, w_ref, b_ref, call_ref, mod_ref, piece_ref, send_sems, recv_sems):
        me = _me()[3]
        call_ref[me] = c_ref[...]
        sent = []
        for r in range(1, N_DEV):
            peer, _ = _peer(r)
            cp = pltpu.make_async_remote_copy(
                src_ref=c_ref, dst_ref=call_ref.at[me], send_sem=send_sems.at[0, r - 1],
                recv_sem=recv_sems.at[0, r - 1], device_id=peer, device_id_type=MESH)
            cp.start()
            sent.append(cp)
        for r in range(1, N_DEV):
            peer, pidx = _peer(r)
            pltpu.make_async_remote_copy(
                src_ref=c_ref, dst_ref=call_ref.at[pidx], send_sem=send_sems.at[0, r - 1],
                recv_sem=recv_sems.at[0, r - 1], device_id=peer, device_id_type=MESH).wait_recv()
        c_all = jnp.concatenate([call_ref[d] for d in range(N_DEV)], axis=0)
        mod_loc = jnp.dot(c_all, w_ref[...], preferred_element_type=F32,
                          precision=lax.Precision.HIGHEST) + b_ref[...]
        for d in range(N_DEV):
            piece_ref[d] = mod_loc[d:d + 1, :]
        mod_ref[me] = piece_ref[me]
        for r in range(1, N_DEV):
            peer, pidx = _peer(r)
            cp = pltpu.make_async_remote_copy(
                src_ref=piece_ref.at[pidx], dst_ref=mod_ref.at[me], send_sem=send_sems.at[1, r - 1],
                recv_sem=recv_sems.at[1, r - 1], device_id=peer, device_id_type=MESH)
            cp.start()
            sent.append(cp)
        for r in range(1, N_DEV):
            peer, pidx = _peer(r)
            pltpu.make_async_remote_copy(
                src_ref=piece_ref.at[me], dst_ref=mod_ref.at[pidx], send_sem=send_sems.at[1, r - 1],
                recv_sem=recv_sems.at[1, r - 1], device_id=peer, device_id_type=MESH).wait_recv()
        for cp in sent:
            cp.wait_send()

    vmem = pl.BlockSpec(memory_space=pltpu.VMEM)
    return pl.pallas_call(
        body, name="mod_exchange", in_specs=[vmem, vmem, vmem], out_specs=[vmem, vmem],
        out_shape=[jax.ShapeDtypeStruct((N_DEV, 1, D_MODEL), F32), jax.ShapeDtypeStruct((N_DEV, 1, ADA_SHARD), F32)],
        scratch_shapes=[pltpu.VMEM((N_DEV, 1, ADA_SHARD), F32),
                        pltpu.SemaphoreType.DMA((2, N_DEV - 1)), pltpu.SemaphoreType.DMA((2, N_DEV - 1))],
        compiler_params=_params(),
    )(c_row, w_ada, b_ada_loc)


def _split3(x):
    hi = x.astype(BF16)
    r1 = x - hi.astype(F32)
    mid = r1.astype(BF16)
    lo = (r1 - mid.astype(F32)).astype(BF16)
    return hi, mid, lo


def _scan_rows(x_ref, o_ref, s, reverse, pre=None, post=None):
    tb = min(TQ, s)
    nb = s // tb
    row = lax.broadcasted_iota(jnp.int32, (tb, tb), 0)
    col = lax.broadcasted_iota(jnp.int32, (tb, tb), 1)
    tri = jnp.where((col >= row) if reverse else (col <= row), 1.0, 0.0).astype(BF16)

    def step(i, carry):
        blk = (nb - 1 - i) if reverse else i
        off = pl.multiple_of(blk * tb, tb)
        x = x_ref[pl.ds(off, tb), :]
        if pre is not None:
            x = pre(x, off)
        acc = carry
        for piece in _split3(x):
            acc = acc + jnp.dot(tri, piece, preferred_element_type=F32)
        o_ref[pl.ds(off, tb), :] = acc if post is None else post(acc, off)
        edge = acc[0:1, :] if reverse else acc[tb - 1:tb, :]
        return jnp.broadcast_to(edge, (tb, LANES))

    lax.fori_loop(0, nb, step, jnp.zeros((tb, LANES), F32))


def _forget_cumsum(f_raw, b_pad):
    s = f_raw.shape[0]

    def body(f_ref, b_ref, cum_ref):
        b = b_ref[...]
        _scan_rows(f_ref, cum_ref, s, False, pre=lambda x, off: _log_sigmoid(x + b))

    vmem = pl.BlockSpec(memory_space=pltpu.VMEM)
    return pl.pallas_call(body, name="forget_cumsum", in_specs=[vmem, vmem], out_specs=vmem,
                          out_shape=jax.ShapeDtypeStruct((s, LANES), F32), compiler_params=_params())(f_raw, b_pad)


def _forget_bwd(dcum, f_raw, b_pad):
    s = f_raw.shape[0]

    def body(d_ref, f_ref, b_ref, df_ref, db_ref, tmp_ref):
        b = b_ref[...]
        _scan_rows(d_ref, tmp_ref, s, True)
        df = tmp_ref[...] * _sigmoid(-(f_ref[...] + b))
        df_ref[...] = df.astype(BF16)
        db_ref[...] = _colsum(df)

    vmem = pl.BlockSpec(memory_space=pltpu.VMEM)
    return pl.pallas_call(
        body, name="forget_bwd", in_specs=[vmem, vmem, vmem], out_specs=[vmem, vmem],
        out_shape=[jax.ShapeDtypeStruct((s, LANES), BF16), jax.ShapeDtypeStruct((1, LANES), F32)],
        scratch_shapes=[pltpu.VMEM((s, LANES), F32)], compiler_params=_params())(dcum, f_raw, b_pad)


def _dot_nt(a, b):
    return lax.dot_general(a, b, (((1,), (1,)), ((), ())), preferred_element_type=F32)


def _head_masks():
    lane = lax.broadcasted_iota(jnp.int32, (TQ, LANES), 1)
    return lane, [lane < HEAD_DIM, lane >= HEAD_DIM]


def _pick(mask, x):
    return jnp.where(mask, x, jnp.zeros_like(x))


def _qkv_specs(s, col0):
    nb = ATTN_W // LANES
    return [pl.BlockSpec((TQ, LANES), lambda hp, qi: (qi, col0 + hp)),
            pl.BlockSpec((s, LANES), lambda hp, qi: (0, col0 + nb + hp)),
            pl.BlockSpec((s, LANES), lambda hp, qi: (0, col0 + 2 * nb + hp))]


def _pair_spec():
    return pl.BlockSpec((TQ, LANES), lambda hp, qi: (qi, hp))


def _diag_mask(width, strict):
    row = lax.broadcasted_iota(jnp.int32, (TQ, width * TQ), 0) + (width - 1) * TQ
    col = lax.broadcasted_iota(jnp.int32, (TQ, width * TQ), 1)
    return (col < row) if strict else (col <= row)


def _fox_fwd(qkv, cum_col, cum_row, ride=None):
    s = qkv.shape[0]
    nq = s // TQ

    def body(q_ref, k_ref, v_ref, cc_ref, cr_ref, o_ref, o32_ref, lse_ref):
        hp, qi = pl.program_id(0), pl.program_id(1)
        lane, masks = _head_masks()
        q2 = q_ref[...] * jnp.asarray(ATTN_SCALE, BF16)
        cc = cc_ref[...]
        qms = [_pick(masks[e], q2) for e in range(2)]
        cqs = [jnp.sum(jnp.where(lane == 2 * hp + e, cc, 0.0), axis=1, keepdims=True) for e in range(2)]

        def tile(kb, carry, masked, width=1):
            off, span = pl.multiple_of(kb * TQ, TQ), width * TQ
            k2, v2 = k_ref[pl.ds(off, span), :], v_ref[pl.ds(off, span), :]
            head0 = lax.broadcasted_iota(jnp.int32, (span, LANES), 1) < HEAD_DIM
            new = []
            for e in range(2):
                m, acc = carry[e]
                sc = _dot_nt(qms[e], k2) + (cqs[e] - cr_ref[e:e + 1, pl.ds(off, span)])
                if masked:
                    sc = jnp.where(_diag_mask(width, False), sc, -jnp.inf)
                m_new = jnp.maximum(m, jnp.max(sc, axis=1, keepdims=True))
                p = jnp.exp(sc - m_new)
                v_ones = jnp.where(head0 if e == 0 else ~head0, v2, jnp.ones_like(v2))
                acc = jnp.exp(m - m_new) * acc + jnp.dot(p.astype(BF16), v_ones, preferred_element_type=F32)
                new.append((m_new, acc))
            return tuple(new)

        init = (jnp.full((TQ, 1), -jnp.inf, F32), jnp.zeros((TQ, LANES), F32))
        carry = lax.cond(qi % 2 == 1, lambda cr: tile(qi - 1, cr, True, 2), lambda cr: tile(qi, cr, True), (init, init))
        quads = qi // 4
        carry = lax.fori_loop(0, quads, lambda j, cr: tile(4 * j, cr, False, 4), carry)
        carry = lax.cond(qi % 4 >= 2, lambda cr: tile(4 * quads, cr, False, 2), lambda cr: cr, carry)
        sums = [jnp.max(jnp.where(masks[1 - e], carry[e][1], 0.0), axis=1, keepdims=True) for e in range(2)]
        outs = [carry[e][1] / sums[e] for e in range(2)]
        lses = [carry[e][0] + jnp.log(sums[e]) for e in range(2)]
        out = jnp.where(masks[0], outs[0], outs[1])
        o_ref[...] = out.astype(BF16)
        o32_ref[...] = out
        lse_ref[...] = jnp.where(masks[0], lses[0], lses[1])

    return _ride_call(
        body, ride, name="fox_fwd", grid=(N_HEADS // 2, nq),
        in_specs=_qkv_specs(s, 0) + [pl.BlockSpec((TQ, LANES), lambda hp, qi: (qi, 0)),
                                     pl.BlockSpec((None, 2, s), lambda hp, qi: (hp, 0, 0))],
        out_specs=[_pair_spec(), _pair_spec(), _pair_spec()],
        out_shape=[jax.ShapeDtypeStruct((s, ATTN_W), BF16), jax.ShapeDtypeStruct((s, ATTN_W), F32),
                   jax.ShapeDtypeStruct((s, ATTN_W), F32)],
        scratch_shapes=[], sem=("parallel", "parallel"), args=(qkv, qkv, qkv, cum_col, cum_row))


def _write_transposed(acc_ref, out_ref):
    for c in range(out_ref.shape[0] // TQ):
        out_ref[c * TQ:(c + 1) * TQ, :] = jnp.transpose(acc_ref[:, c * TQ:(c + 1) * TQ]).astype(BF16)


def _fox_bwd(qkv, cum_col, cum_row, o, lse, do, ride=None):
    s = qkv.shape[0]
    nq = s // TQ

    def body(q_ref, k_ref, v_ref, cc_ref, cr_ref, o_ref, lse_ref, do_ref,
             dq_ref, dk_ref, dv_ref, dcr_ref, dk_acc, dv_acc):
        hp, qi = pl.program_id(0), pl.program_id(1)

        @pl.when(qi == 0)
        def _():
            dk_acc[...] = jnp.zeros_like(dk_acc)
            dv_acc[...] = jnp.zeros_like(dv_acc)
            dcr_ref[...] = jnp.zeros_like(dcr_ref)

        lane, masks = _head_masks()
        q2 = q_ref[...] * jnp.asarray(ATTN_SCALE, BF16)
        do2 = do_ref[...]
        prod = do2.astype(F32) * o_ref[...].astype(F32)
        lse2 = lse_ref[...]
        cc = cc_ref[...]
        qms = [_pick(masks[e], q2) for e in range(2)]
        doms = [_pick(masks[e], do2) for e in range(2)]
        deltas = [jnp.sum(jnp.where(masks[e], prod, 0.0), axis=1, keepdims=True) for e in range(2)]
        lses = [jnp.max(jnp.where(masks[e], lse2, -jnp.inf), axis=1, keepdims=True) for e in range(2)]
        cqs = [jnp.sum(jnp.where(lane == 2 * hp + e, cc, 0.0), axis=1, keepdims=True) for e in range(2)]
        qts = [jnp.transpose(qms[e].astype(F32)).astype(BF16) for e in range(2)]
        dots = [jnp.transpose(doms[e].astype(F32)).astype(BF16) for e in range(2)]
        ones_row = [HEAD_DIM * (1 - e) for e in range(2)]
        trow = lax.broadcasted_iota(jnp.int32, (LANES, TQ), 0)
        qts = [jnp.where(trow == ones_row[e], jnp.ones_like(qts[e]), qts[e]) for e in range(2)]

        def tile(kb, carry, masked, width=1):
            off, span = pl.multiple_of(kb * TQ, TQ), width * TQ
            k2, v2 = k_ref[pl.ds(off, span), :], v_ref[pl.ds(off, span), :]
            head0 = lax.broadcasted_iota(jnp.int32, (span, LANES), 1) < HEAD_DIM
            new, dks, dv = [], [], None
            for e in range(2):
                dq = carry[e]
                sc = _dot_nt(qms[e], k2) + (cqs[e] - cr_ref[e:e + 1, pl.ds(off, span)])
                p = jnp.exp(sc - lses[e])
                if masked:
                    p = jnp.where(_diag_mask(width, False), p, 0.0)
                ds = p * (_dot_nt(doms[e], v2) - deltas[e])
                dsb = ds.astype(BF16)
                dk_e = jnp.dot(qts[e], dsb, preferred_element_type=F32)
                dv_e = jnp.dot(dots[e], p.astype(BF16), preferred_element_type=F32)
                dks.append(dk_e)
                dv = dv_e if e == 0 else dv + dv_e
                dcr_ref[e:e + 1, pl.ds(off, span)] -= dk_e[ones_row[e]:ones_row[e] + 1, :]
                k_ones = jnp.where(head0 if e == 0 else ~head0, k2, jnp.ones_like(k2))
                new.append(dq + jnp.dot(dsb, k_ones, preferred_element_type=F32))
            krow = lax.broadcasted_iota(jnp.int32, (LANES, span), 0)
            dk_acc[:, pl.ds(off, span)] += jnp.where(krow < HEAD_DIM, dks[0], dks[1])
            dv_acc[:, pl.ds(off, span)] += dv
            return tuple(new)

        init = jnp.zeros((TQ, LANES), F32)
        carry = lax.fori_loop(0, qi // 2, lambda j, cr: tile(2 * j, cr, False, 2), (init, init))
        carry = lax.cond(qi % 2 == 1, lambda cr: tile(qi - 1, cr, True, 2), lambda cr: tile(qi, cr, True), carry)
        dq_ref[...] = (jnp.where(masks[0], carry[0], carry[1]) * ATTN_SCALE).astype(BF16)
        for e in range(2):
            dcr_ref[e:e + 1, pl.ds(pl.multiple_of(qi * TQ, TQ), TQ)] += jnp.transpose(carry[e])[
                ones_row[e]:ones_row[e] + 1, :]

        @pl.when(qi == nq - 1)
        def _():
            _write_transposed(dk_acc, dk_ref)
            _write_transposed(dv_acc, dv_ref)

    seq_spec = pl.BlockSpec((s, LANES), lambda hp, qi: (0, hp))
    return _ride_call(
        body, ride, name="fox_bwd", grid=(N_HEADS // 2, nq),
        in_specs=_qkv_specs(s, 0) + [pl.BlockSpec((TQ, LANES), lambda hp, qi: (qi, 0)),
                                     pl.BlockSpec((None, 2, s), lambda hp, qi: (hp, 0, 0)),
                                     _pair_spec(), _pair_spec(), _pair_spec()],
        out_specs=[_pair_spec(), seq_spec, seq_spec, pl.BlockSpec((None, 2, s), lambda hp, qi: (hp, 0, 0))],
        out_shape=[jax.ShapeDtypeStruct((s, ATTN_W), BF16)] * 3 + [jax.ShapeDtypeStruct((N_HEADS // 2, 2, s), F32)],
        scratch_shapes=[pltpu.VMEM((LANES, s), F32), pltpu.VMEM((LANES, s), F32)],
        sem=("parallel", "arbitrary"), args=(qkv, qkv, qkv, cum_col, cum_row, o, lse, do))


def _scan_matrix(reverse):
    row = lax.broadcasted_iota(jnp.int32, (SCAN_W, SCAN_W), 0)
    col = lax.broadcasted_iota(jnp.int32, (SCAN_W, SCAN_W), 1)
    return jnp.where((row > col) if reverse else (row < col), 1.0, 0.0).astype(BF16)


def _scan_cols(x, tri, reverse, init):
    nblk = x.shape[1] // SCAN_W
    parts, total = [None] * nblk, init
    far = 0 if reverse else SCAN_W - 1
    for b in (reversed(range(nblk)) if reverse else range(nblk)):
        blk = x[:, b * SCAN_W:(b + 1) * SCAN_W]
        part = jnp.dot(blk.astype(BF16), tri, preferred_element_type=F32)
        parts[b] = part + total
        total = total + (part[:, far:far + 1] + blk[:, far:far + 1])
    return (parts[0] if nblk == 1 else jnp.concatenate(parts, axis=1)), total


def _sb_logits(qm, k2):
    z = _dot_nt(qm, k2)
    neg_abs = lax.bitcast_convert_type(lax.bitcast_convert_type(z, jnp.uint32) | jnp.uint32(0x80000000), F32)
    soft = jnp.log(1.0 + jnp.exp(neg_abs))
    lb = jnp.minimum(z, 0.0) - soft
    return lb, lb - z


TILE_SLOTS = 4


def _tri_base(qi):
    return (qi * (qi + 1)) // 2


def _sb_fwd(qkv):
    s = qkv.shape[0]
    nq = s // TQ

    def body(q_ref, k_ref, v_ref, o_ref, t_ref, buf, sems):
        hp, qi = pl.program_id(0), pl.program_id(1)
        _, masks = _head_masks()
        suffix = _scan_matrix(True)
        q2 = q_ref[...] * jnp.asarray(ATTN_SCALE, BF16)
        qms = [_pick(masks[e], q2) for e in range(2)]
        base = _tri_base(qi)

        def store(e, kb):
            slot = kb % TILE_SLOTS
            return pltpu.make_async_copy(buf.at[e, slot], t_ref.at[2 * hp + e, base + kb], sems.at[e, slot])

        def tile(kb, carry, masked, width=1):
            off, span = pl.multiple_of(kb * TQ, TQ), width * TQ
            k2, v2 = k_ref[pl.ds(off, span), :], v_ref[pl.ds(off, span), :]
            new = []
            for e in range(2):
                run, acc = carry[e]
                lb, lo = _sb_logits(qms[e], k2)
                if masked:
                    strict = _diag_mask(width, True)
                    lo = jnp.where(strict, lo, 0.0)
                rest, run = _scan_cols(lo, suffix, True, run)
                a = jnp.exp(lb + rest)
                if masked:
                    a = jnp.where(strict, a, 0.0)
                ab, lbb = a.astype(BF16), lb.astype(BF16)
                acc = acc + jnp.dot(ab, v2, preferred_element_type=F32)
                for w in range(width):
                    blk = kb + w

                    @pl.when(blk + TILE_SLOTS <= qi)
                    def _(e=e, blk=blk):
                        store(e, blk + TILE_SLOTS).wait()
                    buf[e, blk % TILE_SLOTS, 0] = ab[:, w * TQ:(w + 1) * TQ]
                    buf[e, blk % TILE_SLOTS, 1] = lbb[:, w * TQ:(w + 1) * TQ]
                    store(e, blk).start()
                new.append((run, acc))
            return tuple(new)

        init = (jnp.zeros((TQ, 1), F32), jnp.zeros((TQ, LANES), F32))
        carry = lax.cond(qi % 2 == 1, lambda cr: tile(qi - 1, cr, True, 2), lambda cr: tile(qi, cr, True), (init, init))
        pairs = qi // 2
        carry = lax.fori_loop(0, pairs, lambda it, cr: tile(2 * (pairs - 1 - it), cr, False, 2), carry)
        for e in range(2):
            for blk in range(TILE_SLOTS):
                @pl.when(qi >= blk)
                def _(e=e, blk=blk):
                    store(e, blk).wait()
        o_ref[...] = jnp.where(masks[0], carry[0][1], carry[1][1]).astype(BF16)

    ntri = nq * (nq + 1) // 2
    return pl.pallas_call(
        body, name="sb_fwd", grid=(N_HEADS // 2, nq), in_specs=_qkv_specs(s, 3 * ATTN_W // LANES),
        out_specs=[_pair_spec(), ANY],
        out_shape=[jax.ShapeDtypeStruct((s, ATTN_W), BF16), jax.ShapeDtypeStruct((N_HEADS, ntri, 2, TQ, TQ), BF16)],
        scratch_shapes=[pltpu.VMEM((2, TILE_SLOTS, 2, TQ, TQ), BF16), pltpu.SemaphoreType.DMA((2, TILE_SLOTS))],
        compiler_params=_params(("arbitrary", "arbitrary")),
    )(qkv, qkv, qkv)


def _sb_bwd(qkv, tiles, do):
    s = qkv.shape[0]
    nq = s // TQ

    def body(q_ref, k_ref, v_ref, t_ref, do_ref, dq_ref, dk_ref, dv_ref, dk_acc, dv_acc, buf, sems):
        hp, qi = pl.program_id(0), pl.program_id(1)

        @pl.when(qi == 0)
        def _():
            dk_acc[...] = jnp.zeros_like(dk_acc)
            dv_acc[...] = jnp.zeros_like(dv_acc)

        _, masks = _head_masks()
        prefix = _scan_matrix(False)
        q2 = q_ref[...] * jnp.asarray(ATTN_SCALE, BF16)
        do2 = do_ref[...]
        qms = [_pick(masks[e], q2) for e in range(2)]
        doms = [_pick(masks[e], do2) for e in range(2)]
        qts = [jnp.transpose(qms[e].astype(F32)).astype(BF16) for e in range(2)]
        dots = [jnp.transpose(doms[e].astype(F32)).astype(BF16) for e in range(2)]
        base = _tri_base(qi)

        def fetch(e, kb):
            slot = kb % TILE_SLOTS
            return pltpu.make_async_copy(t_ref.at[2 * hp + e, base + kb], buf.at[e, slot], sems.at[e, slot])

        for e in range(2):
            fetch(e, 0).start()

            @pl.when(qi >= 1)
            def _(e=e):
                fetch(e, 1).start()

        def tile(kb, carry, masked, width=1):
            off, span = pl.multiple_of(kb * TQ, TQ), width * TQ
            k2, v2 = k_ref[pl.ds(off, span), :], v_ref[pl.ds(off, span), :]
            new, dk, dv = [], None, None
            for e in range(2):
                gsum, dq = carry[e]
                if not masked:
                    for blk in range(2, 2 + width):
                        @pl.when(kb + blk <= qi)
                        def _(e=e, blk=blk):
                            fetch(e, kb + blk).start()
                for w in range(width):
                    fetch(e, kb + w).wait()
                slots = [(kb + w) % TILE_SLOTS for w in range(width)]
                ab = buf[e, slots[0], 0] if width == 1 else jnp.concatenate([buf[e, sl, 0] for sl in slots], axis=1)
                lbb = buf[e, slots[0], 1] if width == 1 else jnp.concatenate([buf[e, sl, 1] for sl in slots], axis=1)
                beta = jnp.exp(lbb.astype(F32))
                g = ab.astype(F32) * _dot_nt(doms[e], v2)
                before, gsum = _scan_cols(g, prefix, False, gsum)
                dz = g - beta * (g + before)
                if masked:
                    dz = jnp.where(_diag_mask(width, True), dz, 0.0)
                dzb = dz.astype(BF16)
                dk_e = jnp.dot(qts[e], dzb, preferred_element_type=F32)
                dv_e = jnp.dot(dots[e], ab, preferred_element_type=F32)
                dk, dv = (dk_e, dv_e) if e == 0 else (dk + dk_e, dv + dv_e)
                new.append((gsum, dq + jnp.dot(dzb, k2, preferred_element_type=F32)))
            dk_acc[:, pl.ds(off, span)] += dk
            dv_acc[:, pl.ds(off, span)] += dv
            return tuple(new)

        init = (jnp.zeros((TQ, 1), F32), jnp.zeros((TQ, LANES), F32))
        carry = lax.fori_loop(0, qi // 2, lambda j, cr: tile(2 * j, cr, False, 2), (init, init))
        carry = lax.cond(qi % 2 == 1, lambda cr: tile(qi - 1, cr, True, 2), lambda cr: tile(qi, cr, True), carry)
        dq_ref[...] = (jnp.where(masks[0], carry[0][1], carry[1][1]) * ATTN_SCALE).astype(BF16)

        @pl.when(qi == nq - 1)
        def _():
            _write_transposed(dk_acc, dk_ref)
            _write_transposed(dv_acc, dv_ref)

    seq_spec = pl.BlockSpec((s, LANES), lambda hp, qi: (0, hp))
    return pl.pallas_call(
        body, name="sb_bwd", grid=(N_HEADS // 2, nq),
        in_specs=_qkv_specs(s, 3 * ATTN_W // LANES) + [ANY, _pair_spec()],
        out_specs=[_pair_spec(), seq_spec, seq_spec],
        out_shape=[jax.ShapeDtypeStruct((s, ATTN_W), BF16)] * 3,
        scratch_shapes=[pltpu.VMEM((LANES, s), F32), pltpu.VMEM((LANES, s), F32),
                        pltpu.VMEM((2, TILE_SLOTS, 2, TQ, TQ), BF16), pltpu.SemaphoreType.DMA((2, TILE_SLOTS))],
        compiler_params=_params(("arbitrary", "arbitrary")),
    )(qkv, qkv, qkv, tiles, do)


CONV_TR = 512


def _shift_down(x, halo, n):
    rolled = pltpu.roll(x, n, 0)
    head = rolled[0:8, :]
    rid = lax.broadcasted_iota(jnp.int32, head.shape, 0)
    for j in range(n):
        head = jnp.where(rid == j, halo[8 - n + j:8 - n + j + 1, :], head)
    return jnp.concatenate([head, rolled[8:, :]], axis=0)


def _shift_up(x, halo, n):
    rows = x.shape[0]
    rolled = pltpu.roll(x, rows - n, 0)
    tail = rolled[rows - 8:, :]
    rid = lax.broadcasted_iota(jnp.int32, tail.shape, 0)
    for j in range(n):
        tail = jnp.where(rid == 8 - n + j, halo[j:j + 1, :], tail)
    return jnp.concatenate([rolled[:rows - 8, :], tail], axis=0)


def _conv_fwd_block(x, halo, w, b):
    return b + _shift_down(x, halo, 2) * w[0:1, :] + _shift_down(x, halo, 1) * w[1:2, :] + x * w[2:3, :]


def _conv_specs(tr, s):
    pair = 2 * FF_HALF
    blk = pl.BlockSpec((tr, pair), lambda j, i: (i, j))
    prev = pl.BlockSpec((8, pair), lambda j, i: (jnp.maximum(i * (tr // 8) - 1, 0), j))
    nxt = pl.BlockSpec((8, pair), lambda j, i: (jnp.minimum((i + 1) * (tr // 8), s // 8 - 1), j))
    return blk, prev, nxt


def _conv_gate_fwd(hpre, conv_w, conv_b):
    s = hpre.shape[0]
    tr = min(CONV_TR, s)
    blk, prev, _ = _conv_specs(tr, s)

    def body(x_ref, halo_ref, w_ref, b_ref, a_ref):
        i = pl.program_id(1)
        halo = jnp.where(i > 0, halo_ref[...], 0.0)
        h = _conv_fwd_block(x_ref[...], halo, w_ref[...], b_ref[...])
        hg, hv = h[:, :FF_HALF], h[:, FF_HALF:]
        a_ref[...] = (hg * _sigmoid(hg) * hv).astype(BF16)

    return pl.pallas_call(
        body, name="conv_gate_fwd", grid=(2, s // tr),
        in_specs=[blk, prev, pl.BlockSpec((3, 2 * FF_HALF), lambda j, i: (0, j)),
                  pl.BlockSpec((1, 2 * FF_HALF), lambda j, i: (0, j))],
        out_specs=pl.BlockSpec((tr, FF_HALF), lambda j, i: (i, j)),
        out_shape=jax.ShapeDtypeStruct((s, D_FF), BF16),
        compiler_params=_params(("parallel", "parallel")),
    )(hpre, hpre, conv_w, conv_b)


def _conv_gate_bwd(hpre, da, conv_w, conv_b):
    s = hpre.shape[0]
    tr = min(CONV_TR, s)
    blk, prev, _ = _conv_specs(tr, s)

    def body(x_ref, halo_ref, da_ref, w_ref, b_ref, dh_ref, db_ref, dw_ref):
        i = pl.program_id(1)
        halo = jnp.where(i > 0, halo_ref[...], 0.0)
        x = x_ref[...]
        h = _conv_fwd_block(x, halo, w_ref[...], b_ref[...])
        hg, hv = h[:, :FF_HALF], h[:, FF_HALF:]
        da_blk = da_ref[...].astype(F32)
        sg = _sigmoid(hg)
        dhg = da_blk * hv * (sg * (1.0 + hg * (1.0 - sg)))
        dhv = da_blk * (hg * sg)
        dh_ref[:, :FF_HALF] = dhg.astype(BF16)
        dh_ref[:, FF_HALF:] = dhv.astype(BF16)
        x2, x1 = _shift_down(x, halo, 2), _shift_down(x, halo, 1)
        parts = []
        for lo, dpart in ((0, dhg), (FF_HALF, dhv)):
            cols = slice(lo, lo + FF_HALF)
            parts.append((cols, _colsum(dpart), _colsum(dpart * x2[:, cols]), _colsum(dpart * x1[:, cols]),
                          _colsum(dpart * x[:, cols])))

        @pl.when(i == 0)
        def _():
            for cols, db, dw0, dw1, dw2 in parts:
                db_ref[:, cols] = db
                dw_ref[0:1, cols] = dw0
                dw_ref[1:2, cols] = dw1
                dw_ref[2:3, cols] = dw2

        @pl.when(i > 0)
        def _():
            for cols, db, dw0, dw1, dw2 in parts:
                db_ref[:, cols] += db
                dw_ref[0:1, cols] += dw0
                dw_ref[1:2, cols] += dw1
                dw_ref[2:3, cols] += dw2

    pair = 2 * FF_HALF
    return pl.pallas_call(
        body, name="conv_gate_bwd", grid=(2, s // tr),
        in_specs=[blk, prev, pl.BlockSpec((tr, FF_HALF), lambda j, i: (i, j)),
                  pl.BlockSpec((3, pair), lambda j, i: (0, j)), pl.BlockSpec((1, pair), lambda j, i: (0, j))],
        out_specs=[blk, pl.BlockSpec((1, pair), lambda j, i: (0, j)), pl.BlockSpec((3, pair), lambda j, i: (0, j))],
        out_shape=[jax.ShapeDtypeStruct((s, 2 * D_FF), BF16), jax.ShapeDtypeStruct((1, 2 * D_FF), F32),
                   jax.ShapeDtypeStruct((3, 2 * D_FF), F32)],
        compiler_params=_params(("parallel", "arbitrary")),
    )(hpre, hpre, da, conv_w, conv_b)


def _conv_input_bwd(dh, conv_w):
    s = dh.shape[0]
    tr = min(CONV_TR, s)
    blk, _, _ = _conv_specs(tr, s)
    nblk = s // tr

    def body(x_ref, halo_ref, w_ref, o_ref):
        i = pl.program_id(1)
        halo = jnp.where(i < nblk - 1, halo_ref[...].astype(F32), 0.0)
        x, w = x_ref[...].astype(F32), w_ref[...]
        o_ref[...] = (x * w[2:3, :] + _shift_up(x, halo, 1) * w[1:2, :] + _shift_up(x, halo, 2) * w[0:1, :]).astype(BF16)

    nxt = pl.BlockSpec((16, 2 * FF_HALF), lambda j, i: (jnp.minimum((i + 1) * (tr // 16), s // 16 - 1), j))
    return pl.pallas_call(
        body, name="conv_input_bwd", grid=(2, nblk),
        in_specs=[blk, nxt, pl.BlockSpec((3, 2 * FF_HALF), lambda j, i: (0, j))], out_specs=blk,
        out_shape=jax.ShapeDtypeStruct((s, 2 * D_FF), BF16),
        compiler_params=_params(("parallel", "parallel")),
    )(dh, dh, conv_w)


def _adamw_math(w, g, m, v):
    m = ADAM_B1 * m + (1.0 - ADAM_B1) * g
    v = ADAM_B2 * v + (1.0 - ADAM_B2) * (g * g)
    m_hat = m / (1.0 - ADAM_B1 ** ADAM_STEP)
    v_hat = v / (1.0 - ADAM_B2 ** ADAM_STEP)
    delta = -ADAM_LR * (m_hat / (jnp.sqrt(v_hat) + ADAM_EPS) + ADAM_WD * w)
    return delta, m, v


def _adamw(name, g8, w, m, v):
    r, c = w.shape
    tr = _row_tile(r, c)

    def body(g_ref, w_ref, m_ref, v_ref, go_ref, d_ref, mo_ref, vo_ref):
        g = g_ref[0].astype(F32)
        for d in range(1, N_DEV):
            g = g + g_ref[d].astype(F32)
        delta, mn, vn = _adamw_math(w_ref[...], g, m_ref[...], v_ref[...])
        go_ref[...] = g
        d_ref[...] = delta
        mo_ref[...] = mn
        vo_ref[...] = vn

    spec = pl.BlockSpec((tr, c), lambda i: (i, 0))
    return pl.pallas_call(
        body, name=name, grid=(r // tr,),
        in_specs=[pl.BlockSpec((N_DEV, tr, c), lambda i: (0, i, 0)), spec, spec, spec], out_specs=[spec] * 4,
        out_shape=[jax.ShapeDtypeStruct((r, c), F32)] * 4, compiler_params=_params(("parallel",)),
    )(g8, w, m, v)


def _adamw_small(g8, ws, ms, vs):
    n = len(ws)
    offsets = [sum(w.shape[0] for w in ws[:i]) for i in range(n)]

    def body(*refs):
        g_ref, w_refs, m_refs, v_refs = refs[0], refs[1:1 + n], refs[1 + n:1 + 2 * n], refs[1 + 2 * n:1 + 3 * n]
        outs = refs[1 + 3 * n:]
        for i in range(n):
            rows = w_refs[i].shape[0]
            g = g_ref[0, offsets[i]:offsets[i] + rows, :]
            for d in range(1, N_DEV):
                g = g + g_ref[d, offsets[i]:offsets[i] + rows, :]
            delta, mn, vn = _adamw_math(w_refs[i][...], g, m_refs[i][...], v_refs[i][...])
            for k, val in enumerate((g, delta, mn, vn)):
                outs[k * n + i][...] = val

    vmem = pl.BlockSpec(memory_space=pltpu.VMEM)
    res = pl.pallas_call(
        body, name="adamw_small", in_specs=[vmem] * (1 + 3 * n), out_specs=[vmem] * (4 * n),
        out_shape=[jax.ShapeDtypeStruct(w.shape, F32) for _ in range(4) for w in ws], compiler_params=_params(),
    )(g8, *ws, *ms, *vs)
    return [res[k * n:(k + 1) * n] for k in range(4)]


def _adamw_ada(c_t, dmod, w, m, v):
    r, c = w.shape
    tr = _row_tile(r, c)

    def body(ct_ref, dm_ref, w_ref, m_ref, v_ref, go_ref, d_ref, mo_ref, vo_ref):
        ct, dm = ct_ref[...], dm_ref[...]
        g = ct[:, 0:1] * dm[0:1, :]
        for b in range(1, N_DEV):
            g = g + ct[:, b:b + 1] * dm[b:b + 1, :]
        delta, mn, vn = _adamw_math(w_ref[...], g, m_ref[...], v_ref[...])
        go_ref[...] = g
        d_ref[...] = delta
        mo_ref[...] = mn
        vo_ref[...] = vn

    spec = pl.BlockSpec((tr, c), lambda i: (i, 0))
    return pl.pallas_call(
        body, name="adamw_w_ada", grid=(r // tr,),
        in_specs=[pl.BlockSpec((tr, N_DEV), lambda i: (i, 0)), pl.BlockSpec((N_DEV, c), lambda i: (0, 0)),
                  spec, spec, spec],
        out_specs=[spec] * 4, out_shape=[jax.ShapeDtypeStruct((r, c), F32)] * 4,
        compiler_params=_params(("parallel",)),
    )(c_t, dmod, w, m, v)


def _cols_from_slots(g):
    n, r, c = g.shape
    return jnp.transpose(g, (1, 0, 2)).reshape(r, n * c)


def _cols_to_slots(w):
    r, c = w.shape
    return jnp.transpose(w.reshape(r, N_DEV, c // N_DEV), (1, 0, 2))


def _pair_cols(w):
    g0, g1 = w[..., 0:FF_HALF], w[..., FF_HALF:D_FF]
    v0, v1 = w[..., D_FF:D_FF + FF_HALF], w[..., D_FF + FF_HALF:]
    return jnp.concatenate([g0, v0, g1, v1], axis=-1)


def _unpair_cols(w):
    g0, v0 = w[..., 0:FF_HALF], w[..., FF_HALF:D_FF]
    g1, v1 = w[..., D_FF:D_FF + FF_HALF], w[..., D_FF + FF_HALF:]
    return jnp.concatenate([g0, g1, v0, v1], axis=-1)


def _row(v):
    return v.reshape(1, -1)


def kernel(x, c, w_ada, b_ada, w_in, b_forget, w_fox_proj, w_sb_proj, w_o, ln1_g, ln1_b, w_up, conv_w, conv_b, w_down, ln2_g, ln2_b, loss_target, m_w_ada, m_b_ada, m_w_in, m_b_forget, m_w_fox_proj, m_w_sb_proj, m_w_o, m_ln1_g, m_ln1_b, m_w_up, m_conv_w, m_conv_b, m_w_down, m_ln2_g, m_ln2_b, v_w_ada, v_b_ada, v_w_in, v_b_forget, v_w_fox_proj, v_w_sb_proj, v_w_o, v_ln1_g, v_ln1_b, v_w_up, v_conv_w, v_conv_b, v_w_down, v_ln2_g, v_ln2_b):
    s = x.shape[1]
    me = 4 * lax.axis_index("x") + 2 * lax.axis_index("y") + lax.axis_index("c")
    x2 = x.reshape(s, D_MODEL)
    tgt = loss_target.reshape(s, D_MODEL)

    b_ada_loc = lax.dynamic_slice(b_ada, (me * ADA_SHARD,), (ADA_SHARD,)).reshape(1, ADA_SHARD)
    c_all, mod = _mod_exchange(c, w_ada, b_ada_loc)
    mod = mod.reshape(N_MOD, 1, D_MODEL)
    sh1, sc1, gt1, sh2, sc2, gt2 = [mod[i] for i in range(N_MOD)]

    g_in = _allgather_two_level("ag_w_in", w_in.astype(BF16))
    late_weights = _Ride([w_fox_proj.astype(BF16), w_sb_proj.astype(BF16), w_o.astype(BF16), w_up.astype(BF16),
                          w_down.astype(BF16), conv_w], scatter=False)
    w_in_f = _cols_from_slots(g_in)
    w_proj = jnp.concatenate(
        [w_in_f[:, 0:1536], w_in_f[:, 1544:3080], w_in_f[:, 3080:5128], w_in_f[:, 1536:1544],
         jnp.zeros((D_MODEL, W_PROJ - 5128), BF16)], axis=1)
    w_qkv, w_gates, w_f = w_proj[:, :W_QKV], w_proj[:, W_QKV:W_QKV + W_GATES], w_proj[:, W_QKV + W_GATES:W_QKV + W_GATES + W_F]
    conv_b_p = _pair_cols(_row(conv_b))
    b_f_pad = jnp.pad(_row(b_forget), ((0, 0), (0, LANES - N_HEADS)))

    (u1,) = _rowwise("modulate1", lambda xb, sc, sh: (xb * (1.0 + sc) + sh,),
                     [(x2, D_MODEL, 0)], [sc1, sh1], [(D_MODEL, BF16)], tr=512)
    qkv = _mm(u1, w_qkv, name="mm_qkv", out_dtype=BF16)
    gates = _mm(u1, w_gates, name="mm_gates")
    f_raw = _mm(u1, w_f, name="mm_forget")
    cum_col = _forget_cumsum(f_raw, b_f_pad)
    cum_row = jnp.transpose(cum_col[:, :N_HEADS]).reshape(N_HEADS // 2, 2, s)
    (y_fox, y_fox32, lse), (g_fox, g_sb, g_o, g_up, g_down, g_cw) = _fox_fwd(qkv, cum_col, cum_row, ride=late_weights)
    w_fox_f = _cols_from_slots(g_fox)
    w_sb_f = _cols_from_slots(g_sb)
    w_o_f = g_o.reshape(D_MODEL, D_MODEL)
    w_up_p = _pair_cols(_cols_from_slots(g_up))
    w_down_f = g_down.reshape(D_FF, D_MODEL)
    conv_w_p = _pair_cols(_cols_from_slots(g_cw))
    y_sb, sb_run = _sb_fwd(qkv)
    pf = _mm(y_fox, w_fox_f, name="mm_fox_proj", out_dtype=BF16)
    ps = _mm(y_sb, w_sb_f, name="mm_sb_proj", out_dtype=BF16)
    (merged,) = _rowwise("gate_merge", lambda ga, gb, a, b: (_sigmoid(ga) * a + _sigmoid(gb) * b,),
                         [(gates, D_MODEL, 0), (gates, D_MODEL, 1), (pf, D_MODEL, 0), (ps, D_MODEL, 0)], [],
                         [(D_MODEL, BF16)])
    attn_out = _mm(merged, w_o_f, name="mm_w_o")

    def ln_fwd(xb, fb, gt, g, b):
        xhat, _ = _ln_stats(ALPHA * xb + (1.0 + gt) * fb)
        return xhat * g + b

    def ln1_mod(xb, fb, gt, g, b, sc, sh):
        y = ln_fwd(xb, fb, gt, g, b)
        return y, y * (1.0 + sc) + sh

    x1, u2 = _rowwise("ln1_modulate2", ln1_mod, [(x2, D_MODEL, 0), (attn_out, D_MODEL, 0)],
                      [gt1, _row(ln1_g), _row(ln1_b), sc2, sh2], [(D_MODEL, F32), (D_MODEL, BF16)])

    hpre = _mm(u2, w_up_p, name="mm_w_up", tn=1408)
    act = _conv_gate_fwd(hpre, conv_w_p, conv_b_p)
    ffn_out = _mm(act, w_down_f, name="mm_w_down", tk=2816)

    def ln2_bwd(xb, fb, tb, gt, g, b):
        xhat, rstd = _ln_stats(ALPHA * xb + (1.0 + gt) * fb)
        err = (xhat * g + b) - tb
        dy = err * (1.0 / D_MODEL)
        dr = _ln_bwd(dy, xhat, rstd, g)
        return (dr * (1.0 + gt), ALPHA * dr,
                _colsum(err * err), _colsum(dy * xhat), _colsum(dy), _colsum(dr * fb))

    dffn, dx1_res, sq_err, d_ln2_g, d_ln2_b, d_gt2 = _rowwise(
        "ln2_bwd", ln2_bwd, [(x1, D_MODEL, 0), (ffn_out, D_MODEL, 0), (tgt, D_MODEL, 0)],
        [gt2, _row(ln2_g), _row(ln2_b)], [(D_MODEL, BF16), (D_MODEL, F32)], sums=[D_MODEL] * 4)
    loss = lax.psum(0.5 * jnp.sum(sq_err) / D_MODEL, ("x", "y", "c"))

    d_w_down = _mm(act, dffn, name="mm_d_w_down", ta=True, tm=1408, tk=2048, out_dtype=BF16)
    d_act = _mm(dffn, w_down_f, name="mm_d_act", tb=True, tn=1408, out_dtype=BF16)
    dh, d_conv_b_p, d_conv_w_p = _conv_gate_bwd(hpre, d_act, conv_w_p, conv_b_p)
    dhpre = _conv_input_bwd(dh, conv_w_p)
    d_w_up_p = _mm(u2, dhpre, name="mm_d_w_up", ta=True, tn=1408, tk=2048, out_dtype=BF16)
    du2 = _mm(dhpre, w_up_p, name="mm_d_u2", tb=True, tk=2816)

    def ln1_bwd(du, dres, x1b, xb, fb, sc, gt, g):
        dx1 = dres + du * (1.0 + sc)
        xhat, rstd = _ln_stats(ALPHA * xb + (1.0 + gt) * fb)
        dr = _ln_bwd(dx1, xhat, rstd, g)
        return (dr * (1.0 + gt), ALPHA * dr,
                _colsum(du * x1b), _colsum(du), _colsum(dx1 * xhat), _colsum(dx1), _colsum(dr * fb))

    d_attn, dx_res, d_sc2, d_sh2, d_ln1_g, d_ln1_b, d_gt1 = _rowwise(
        "ln1_bwd", ln1_bwd,
        [(du2, D_MODEL, 0), (dx1_res, D_MODEL, 0), (x1, D_MODEL, 0), (x2, D_MODEL, 0), (attn_out, D_MODEL, 0)],
        [sc2, gt1, _row(ln1_g)], [(D_MODEL, BF16), (D_MODEL, F32)], sums=[D_MODEL] * 5)

    d_w_o = _mm(merged, d_attn, name="mm_d_w_o", ta=True, out_dtype=BF16)
    d_merged = _mm(d_attn, w_o_f, name="mm_d_merged", tb=True, out_dtype=BF16)

    def merge_bwd(dm, ga, gb, a, b):
        dm, a, b = dm.astype(F32), a.astype(F32), b.astype(F32)
        sa, sb = _sigmoid(ga), _sigmoid(gb)
        return dm * a * sa * (1.0 - sa), dm * b * sb * (1.0 - sb), dm * sa, dm * sb

    d_ga, d_gb, d_pf, d_ps = _rowwise(
        "gate_merge_bwd", merge_bwd,
        [(d_merged, D_MODEL, 0), (gates, D_MODEL, 0), (gates, D_MODEL, 1), (pf, D_MODEL, 0), (ps, D_MODEL, 0)], [],
        [(D_MODEL, BF16)] * 4)
    d_w_fox = _mm(y_fox, d_pf, name="mm_d_w_fox", ta=True, out_dtype=BF16)
    d_w_sb = _mm(y_sb, d_ps, name="mm_d_w_sb", ta=True, out_dtype=BF16)
    d_y_fox = _mm(d_pf, w_fox_f, name="mm_d_y_fox", tb=True, out_dtype=BF16)
    d_y_sb = _mm(d_ps, w_sb_f, name="mm_d_y_sb", tb=True, out_dtype=BF16)
    early_grads = _Ride(
        [_cols_to_slots(d_w_fox), _cols_to_slots(d_w_sb), d_w_o.reshape(N_DEV, D_MODEL // N_DEV, D_MODEL),
         _cols_to_slots(_unpair_cols(d_w_up_p)), d_w_down.reshape(N_DEV, D_FF // N_DEV, D_MODEL)], scatter=True)
    (dq_a, dk_a, dv_a, d_cum_row), early_slots = _fox_bwd(qkv, cum_col, cum_row, y_fox32, lse, d_y_fox,
                                                                    ride=early_grads)
    dq_b, dk_b, dv_b = _sb_bwd(qkv, sb_run, d_y_sb)
    d_cum = jnp.transpose(d_cum_row.reshape(N_HEADS, s))
    d_cum = jnp.pad(d_cum, ((0, 0), (0, LANES - N_HEADS)))
    d_f, d_b_forget = _forget_bwd(d_cum, f_raw, b_f_pad)
    d_proj = jnp.concatenate([dq_a, dk_a, dv_a, dq_b, dk_b, dv_b, d_ga, d_gb, d_f,
                              jnp.zeros((s, W_PROJ - W_QKV - W_GATES - W_F), BF16)], axis=1)
    d_w_proj = _mm(u1, d_proj, name="mm_d_w_in", ta=True, tn=896, tk=2048, out_dtype=BF16)
    d_w_in_f = jnp.concatenate([d_w_proj[:, 0:1536], d_w_proj[:, 5120:5128], d_w_proj[:, 1536:3072],
                                d_w_proj[:, 3072:5120]], axis=1)
    du1, (in_slots,) = _mm(d_proj, w_proj, name="mm_d_u1", tb=True, tk=2688,
                           ride=_Ride([_cols_to_slots(d_w_in_f)], scatter=True))

    def x_bwd(du, dres, xb, sc):
        return dres + du * (1.0 + sc), _colsum(du * xb), _colsum(du)

    grad_x, d_sc1, d_sh1 = _rowwise("x_bwd", x_bwd, [(du1, D_MODEL, 0), (dx_res, D_MODEL, 0), (x2, D_MODEL, 0)],
                                    [sc1], [(D_MODEL, F32)], sums=[D_MODEL] * 2, tr=512)

    d_conv_b = _unpair_cols(d_conv_b_p)
    d_conv_w = _unpair_cols(d_conv_w_p)
    cb_pad = N_MOD * D_MODEL - 2 * D_FF
    small = jnp.concatenate(
        [d_sh1, d_sc1, d_gt1, d_sh2, d_sc2, d_gt2, jnp.pad(d_b_forget, ((0, 0), (0, D_MODEL - LANES))),
         d_ln1_g, d_ln1_b, d_ln2_g, d_ln2_b, jnp.pad(d_conv_b, ((0, 0), (0, cb_pad))),
         d_conv_w.reshape(1, 6 * D_FF)], axis=1)
    n_small = small.shape[1] // LANES
    small = jnp.pad(small.reshape(n_small, LANES), ((0, -n_small % 8), (0, 0)))
    (small_all,) = _exchange("ag_small_grads", [small], scatter=False)
    n_rep = n_small - 6 * D_FF // LANES
    cw8 = small_all[:, n_rep:n_small, :].reshape(N_DEV, 3, 2 * D_FF)
    cw8 = lax.dynamic_slice(cw8, (0, 0, me * UP_SHARD), (N_DEV, 3, UP_SHARD))
    dmod8 = small_all[:, :N_MOD * D_MODEL // LANES, :].reshape(N_DEV, N_MOD * D_MODEL)
    dmod_loc = lax.dynamic_slice(dmod8, (0, me * ADA_SHARD), (N_DEV, ADA_SHARD))

    def rows_of(a):
        return a.reshape(-1, LANES)

    def forget_rows(a):
        return rows_of(jnp.pad(a, (0, D_MODEL - N_HEADS)))

    def conv_b_rows(a):
        return rows_of(jnp.pad(a, (0, cb_pad)))

    rep_sizes = {"b_ada": N_MOD * D_MODEL, "b_forget": N_HEADS, "ln1_g": D_MODEL, "ln1_b": D_MODEL, "ln2_g": D_MODEL,
                 "ln2_b": D_MODEL, "conv_b": 2 * D_FF}
    rep_w = [rows_of(b_ada), forget_rows(b_forget), rows_of(ln1_g), rows_of(ln1_b), rows_of(ln2_g), rows_of(ln2_b),
             conv_b_rows(conv_b)]
    rep_m = [rows_of(m_b_ada), forget_rows(m_b_forget), rows_of(m_ln1_g), rows_of(m_ln1_b), rows_of(m_ln2_g),
             rows_of(m_ln2_b), conv_b_rows(m_conv_b)]
    rep_v = [rows_of(v_b_ada), forget_rows(v_b_forget), rows_of(v_ln1_g), rows_of(v_ln1_b), rows_of(v_ln2_g),
             rows_of(v_ln2_b), conv_b_rows(v_conv_b)]
    rep_out = _adamw_small(small_all, rep_w, rep_m, rep_v)
    rep = [{name: a.reshape(-1)[:size] for (name, size), a in zip(rep_sizes.items(), outs)} for outs in rep_out]
    r_conv_w = _adamw("adamw_conv_w", cw8, conv_w, m_conv_w, v_conv_w)
    r_ada = _adamw_ada(jnp.transpose(c_all.reshape(N_DEV, D_MODEL)), dmod_loc, w_ada, m_w_ada, v_w_ada)

    r_in = _adamw("adamw_w_in", in_slots, w_in, m_w_in, v_w_in)
    r_fox = _adamw("adamw_w_fox", early_slots[0], w_fox_proj, m_w_fox_proj, v_w_fox_proj)
    r_sb = _adamw("adamw_w_sb", early_slots[1], w_sb_proj, m_w_sb_proj, v_w_sb_proj)
    r_o = _adamw("adamw_w_o", early_slots[2], w_o, m_w_o, v_w_o)
    r_up = _adamw("adamw_w_up", early_slots[3], w_up, m_w_up, v_w_up)
    r_down = _adamw("adamw_w_down", early_slots[4], w_down, m_w_down, v_w_down)

    def leaf(i):
        return [r_ada[i], rep[i]["b_ada"], r_in[i], rep[i]["b_forget"], r_fox[i], r_sb[i], r_o[i], rep[i]["ln1_g"],
                rep[i]["ln1_b"], r_up[i], r_conv_w[i], rep[i]["conv_b"], r_down[i], rep[i]["ln2_g"], rep[i]["ln2_b"]]

    return (loss, grad_x.reshape(1, s, D_MODEL), *leaf(0), *leaf(1), *leaf(2), *leaf(3))
```

```python
import functools

import jax
import jax.numpy as jnp
from jax import lax
from jax.experimental import pallas as pl
from jax.experimental.pallas import tpu as pltpu

F32 = jnp.float32
BF16 = jnp.bfloat16
MESH = pl.DeviceIdType.MESH
ANY = pl.BlockSpec(memory_space=pl.ANY)

N_DEV = 8
D_MODEL = 1024
HEAD_DIM = 64
N_HEADS = 8
ATTN_W = N_HEADS * HEAD_DIM
D_FF = 2816
FF_HALF = D_FF // 2
N_MOD = 6
ADA_SHARD = N_MOD * D_MODEL // N_DEV
IN_SHARD = 641
UP_SHARD = 704
ATTN_SCALE = HEAD_DIM ** -0.5
ALPHA = 2.0 ** 0.25
LN_EPS = 1e-5
LANES = 128
TQ = 512
SCAN_W = 256
VMEM_LIMIT = 56 * 1024 * 1024

ADAM_LR, ADAM_B1, ADAM_B2, ADAM_EPS, ADAM_WD, ADAM_STEP = 0.001, 0.9, 0.999, 1e-08, 0.01, 10

W_QKV, W_GATES, W_F = 3072, 2048, 128
W_PROJ = 5376


def _params(sem=None):
    return pltpu.CompilerParams(dimension_semantics=sem, vmem_limit_bytes=VMEM_LIMIT)


def _tile(n, cap):
    if n <= cap:
        return n
    best = None
    for t in range(LANES, cap + 1, LANES):
        if n % t == 0:
            best = t
    assert best is not None, (n, cap)
    return best


def _row_tile(r, width, budget=192 * 1024):
    if r * width <= budget or r % 16:
        return r
    best = 16
    for t in range(16, r + 1, 16):
        if r % t == 0 and t * width <= budget:
            best = t
    return best


def _me():
    x, y, c = lax.axis_index("x"), lax.axis_index("y"), lax.axis_index("c")
    return x, y, c, 4 * x + 2 * y + c


def _peer(r):
    x, y, c, _ = _me()
    px = 1 - x if r & 4 else x
    py = 1 - y if r & 2 else y
    pc = 1 - c if r & 1 else c
    return (px, py, pc), 4 * px + 2 * py + pc


class _Ride:
    def __init__(self, arrays, scatter):
        self.arrays, self.scatter, self.n = list(arrays), scatter, len(arrays)
        self.in_specs = [ANY] * self.n
        self.out_specs = [ANY] * self.n
        self.out_shape = [jax.ShapeDtypeStruct(a.shape if scatter else (N_DEV,) + a.shape, a.dtype) for a in arrays]
        self.scratch = [pltpu.SemaphoreType.DMA((self.n, N_DEV - 1)), pltpu.SemaphoreType.DMA((self.n, N_DEV - 1)),
                        pltpu.SemaphoreType.DMA((self.n,))]

    def _local(self, ins, outs, sems, a):
        me = _me()[3]
        return pltpu.make_async_copy(ins[a].at[me] if self.scatter else ins[a], outs[a].at[me], sems[2].at[a])

    def _remote(self, ins, outs, sems, a, r, arriving):
        me = _me()[3]
        peer, pidx = _peer(r)
        src = ins[a].at[me if arriving else pidx] if self.scatter else ins[a]
        return pltpu.make_async_remote_copy(
            src_ref=src, dst_ref=outs[a].at[pidx if arriving else me], send_sem=sems[0].at[a, r - 1],
            recv_sem=sems[1].at[a, r - 1], device_id=peer, device_id_type=MESH)

    def start(self, ins, outs, sems):
        for a in range(self.n):
            self._local(ins, outs, sems, a).start()
        for r in range(1, N_DEV):
            for a in range(self.n):
                self._remote(ins, outs, sems, a, r, False).start()

    def wait(self, ins, outs, sems):
        for r in range(1, N_DEV):
            for a in range(self.n):
                self._remote(ins, outs, sems, a, r, True).wait_recv()
        for r in range(1, N_DEV):
            for a in range(self.n):
                self._remote(ins, outs, sems, a, r, False).wait_send()
        for a in range(self.n):
            self._local(ins, outs, sems, a).wait()


def _exchange(name, arrays, scatter):
    ride = _Ride(arrays, scatter)

    def body(*refs):
        ins, outs, sems = refs[:ride.n], refs[ride.n:2 * ride.n], refs[2 * ride.n:]
        ride.start(ins, outs, sems)
        ride.wait(ins, outs, sems)

    return pl.pallas_call(body, name=name, in_specs=ride.in_specs, out_specs=ride.out_specs, out_shape=ride.out_shape,
                          scratch_shapes=ride.scratch)(*arrays)


def _allgather_two_level(name, a):
    def body(a_ref, out_ref, send_sems, recv_sems, local_sem):
        x, y, c, me = _me()
        sibling = (x, y, 1 - c)
        chips = [(1 - x, y), (x, 1 - y), (1 - x, 1 - y)]

        def idx(px, py, pc):
            return 4 * px + 2 * py + pc

        def copy(k, block, to, src=None):
            slot = out_ref.at[idx(*block)]
            return pltpu.make_async_remote_copy(
                src_ref=slot if src is None else src, dst_ref=slot, send_sem=send_sems.at[k], recv_sem=recv_sems.at[k],
                device_id=to, device_id_type=MESH)

        mine = pltpu.make_async_copy(a_ref, out_ref.at[me], local_sem)
        mine.start()
        first = [copy(0, (x, y, c), sibling, src=a_ref)]
        first += [copy(1 + j, (x, y, c), (*chip, c), src=a_ref) for j, chip in enumerate(chips)]
        for cp in first:
            cp.start()
        passed = [copy(4 + j, (*chip, c), sibling) for j, chip in enumerate(chips)]
        for j, chip in enumerate(chips):
            copy(1 + j, (*chip, c), (x, y, c)).wait_recv()
            passed[j].start()
        copy(0, sibling, (x, y, c)).wait_recv()
        for j, chip in enumerate(chips):
            copy(4 + j, (*chip, 1 - c), (x, y, c)).wait_recv()
        for cp in first + passed:
            cp.wait_send()
        mine.wait()

    return pl.pallas_call(
        body, name=name, in_specs=[ANY], out_specs=ANY,
        out_shape=jax.ShapeDtypeStruct((N_DEV,) + a.shape, a.dtype),
        scratch_shapes=[pltpu.SemaphoreType.DMA((N_DEV - 1,)), pltpu.SemaphoreType.DMA((N_DEV - 1,)),
                        pltpu.SemaphoreType.DMA],
    )(a)


def _with_ride(body, ride, n_in, n_out, grid):
    if ride is None:
        return body
    n = ride.n

    def wrapped(*refs):
        ins, rins = refs[:n_in], refs[n_in:n_in + n]
        outs, routs = refs[n_in + n:n_in + n + n_out], refs[n_in + n + n_out:n_in + 2 * n + n_out]
        rest = refs[n_in + 2 * n + n_out:]
        scratch, sems = rest[:len(rest) - 3], rest[len(rest) - 3:]
        ids = [pl.program_id(d) for d in range(len(grid))]
        first = functools.reduce(lambda p, q: p & q, [i == 0 for i in ids])
        last = functools.reduce(lambda p, q: p & q, [i == g - 1 for i, g in zip(ids, grid)])

        @pl.when(first)
        def _():
            ride.start(rins, routs, sems)

        body(*ins, *outs, *scratch)

        @pl.when(last)
        def _():
            ride.wait(rins, routs, sems)

    return wrapped


def _ride_call(body, ride, *, name, grid, in_specs, out_specs, out_shape, scratch_shapes, sem, args):
    n_in, n_out = len(in_specs), len(out_specs)
    if ride is None:
        res = pl.pallas_call(body, name=name, grid=grid, in_specs=in_specs, out_specs=out_specs, out_shape=out_shape,
                             scratch_shapes=scratch_shapes, compiler_params=_params(sem))(*args)
        return list(res), []
    res = pl.pallas_call(
        _with_ride(body, ride, n_in, n_out, grid), name=name, grid=grid,
        in_specs=list(in_specs) + ride.in_specs, out_specs=list(out_specs) + ride.out_specs,
        out_shape=list(out_shape) + ride.out_shape, scratch_shapes=list(scratch_shapes) + ride.scratch,
        compiler_params=_params(("arbitrary",) * len(grid)))(*args, *ride.arrays)
    return list(res[:n_out]), list(res[n_out:])


def _mm(a, b, *, name, ta=False, tb=False, out_dtype=F32, tm=1024, tn=1024, tk=1024, ride=None):
    m, k = (a.shape[1], a.shape[0]) if ta else a.shape
    n = b.shape[0] if tb else b.shape[1]
    assert (b.shape[1] if tb else b.shape[0]) == k
    tm, tn, tk = _tile(m, tm), _tile(n, tn), _tile(k, tk)
    nk = k // tk
    a_spec = pl.BlockSpec((tk, tm), lambda i, j, l: (l, i)) if ta else pl.BlockSpec((tm, tk), lambda i, j, l: (i, l))
    b_spec = pl.BlockSpec((tn, tk), lambda i, j, l: (j, l)) if tb else pl.BlockSpec((tk, tn), lambda i, j, l: (l, j))
    dims = (((0,) if ta else (1,), (1,) if tb else (0,)), ((), ()))

    def body(a_ref, b_ref, o_ref, *acc):
        p = lax.dot_general(a_ref[...].astype(BF16), b_ref[...].astype(BF16), dims, preferred_element_type=F32)
        if nk == 1:
            o_ref[...] = p.astype(out_dtype)
            return
        acc_ref = acc[0]
        step = pl.program_id(2)

        @pl.when(step == 0)
        def _():
            acc_ref[...] = p

        @pl.when(step > 0)
        def _():
            acc_ref[...] += p

        @pl.when(step == nk - 1)
        def _():
            o_ref[...] = acc_ref[...].astype(out_dtype)

    outs, rode = _ride_call(
        body, ride, name=name, grid=(m // tm, n // tn, nk), in_specs=[a_spec, b_spec],
        out_specs=[pl.BlockSpec((tm, tn), lambda i, j, l: (i, j))], out_shape=[jax.ShapeDtypeStruct((m, n), out_dtype)],
        scratch_shapes=[] if nk == 1 else [pltpu.VMEM((tm, tn), F32)], sem=("parallel", "parallel", "arbitrary"),
        args=(a, b))
    return outs[0] if ride is None else (outs[0], rode)


def _mm_pieces_nt(pieces, w, *, name, tm=512, ride=None):
    s, n = pieces[0].shape[0], w.shape[0]
    tm = min(tm, s)
    widths = [p.shape[1] for p in pieces]
    offs = [sum(widths[:i]) for i in range(len(pieces))]

    def body(*refs):
        p_refs, w_ref, o_ref = refs[:len(pieces)], refs[len(pieces)], refs[len(pieces) + 1]
        acc = None
        for p_ref, off, width in zip(p_refs, offs, widths):
            part = _dot_nt(p_ref[...], w_ref[:, off:off + width])
            acc = part if acc is None else acc + part
        o_ref[...] = acc

    outs, rode = _ride_call(
        body, ride, name=name, grid=(s // tm,),
        in_specs=[pl.BlockSpec((tm, width), lambda i: (i, 0)) for width in widths]
        + [pl.BlockSpec(w.shape, lambda i: (0, 0))],
        out_specs=[pl.BlockSpec((tm, n), lambda i: (i, 0))], out_shape=[jax.ShapeDtypeStruct((s, n), F32)],
        scratch_shapes=[], sem=("parallel",), args=(*pieces, w))
    return outs[0] if ride is None else (outs[0], rode)


def _mm_pieces_tn(a, pieces, *, name, tk=1024):
    s, m = a.shape
    tk = min(tk, s)
    nk = s // tk
    n = len(pieces)
    dims = (((0,), (0,)), ((), ()))

    def body(*refs):
        a_ref, p_refs, o_refs, accs = refs[0], refs[1:1 + n], refs[1 + n:1 + 2 * n], refs[1 + 2 * n:]
        step = pl.program_id(0)
        a_blk = a_ref[...]
        for p_ref, o_ref, acc in zip(p_refs, o_refs, accs):
            part = lax.dot_general(a_blk, p_ref[...], dims, preferred_element_type=F32)

            @pl.when(step == 0)
            def _(acc=acc, part=part):
                acc[...] = part

            @pl.when(step > 0)
            def _(acc=acc, part=part):
                acc[...] += part

            @pl.when(step == nk - 1)
            def _(acc=acc, o_ref=o_ref):
                o_ref[...] = acc[...].astype(BF16)

    return pl.pallas_call(
        body, name=name, grid=(nk,),
        in_specs=[pl.BlockSpec((tk, m), lambda l: (l, 0))] + [pl.BlockSpec((tk, p.shape[1]), lambda l: (l, 0)) for p in pieces],
        out_specs=[pl.BlockSpec((m, p.shape[1]), lambda l: (0, 0)) for p in pieces],
        out_shape=[jax.ShapeDtypeStruct((m, p.shape[1]), BF16) for p in pieces],
        scratch_shapes=[pltpu.VMEM((m, p.shape[1]), F32) for p in pieces],
        compiler_params=_params(("arbitrary",)),
    )(a, *pieces)


def _rowwise(name, fn, rows, vecs, outs, sums=(), tr=512):
    s = rows[0][0].shape[0]
    tr = min(tr, s)
    nr, nv, no = len(rows), len(vecs), len(outs)

    def body(*refs):
        vals = [r[...] for r in refs[:nr + nv]]
        res = fn(*vals)
        for o_ref, val in zip(refs[nr + nv:nr + nv + no], res[:no]):
            o_ref[...] = val.astype(o_ref.dtype)
        step = pl.program_id(0)
        for s_ref, val in zip(refs[nr + nv + no:], res[no:]):
            @pl.when(step == 0)
            def _(s_ref=s_ref, val=val):
                s_ref[...] = val

            @pl.when(step > 0)
            def _(s_ref=s_ref, val=val):
                s_ref[...] += val

    in_specs = [pl.BlockSpec((tr, w), functools.partial(lambda i, cb: (i, cb), cb=cb)) for _, w, cb in rows]
    in_specs += [pl.BlockSpec(v.shape, lambda i: (0, 0)) for v in vecs]
    out_specs = [pl.BlockSpec((tr, w), lambda i: (i, 0)) for w, _ in outs]
    out_specs += [pl.BlockSpec((1, w), lambda i: (0, 0)) for w in sums]
    out_shape = [jax.ShapeDtypeStruct((s, w), dt) for w, dt in outs]
    out_shape += [jax.ShapeDtypeStruct((1, w), F32) for w in sums]
    return pl.pallas_call(
        body, name=name, grid=(s // tr,), in_specs=in_specs, out_specs=out_specs, out_shape=out_shape,
        compiler_params=_params(("arbitrary",) if sums else ("parallel",)),
    )(*[r[0] for r in rows], *vecs)


def _colsum(x):
    return jnp.sum(x, axis=0, keepdims=True)


def _sigmoid(x):
    return 1.0 / (1.0 + jnp.exp(-x))


def _log_sigmoid(x):
    return jnp.minimum(x, 0.0) - jnp.log(1.0 + jnp.exp(-jnp.abs(x)))


def _ln_stats(r):
    mu = jnp.mean(r, axis=-1, keepdims=True)
    xc = r - mu
    var = jnp.mean(xc * xc, axis=-1, keepdims=True)
    rstd = lax.rsqrt(var + LN_EPS)
    return xc * rstd, rstd


def _ln_bwd(dy, xhat, rstd, g):
    dxh = dy * g
    m1 = jnp.mean(dxh, axis=-1, keepdims=True)
    m2 = jnp.mean(dxh * xhat, axis=-1, keepdims=True)
    return rstd * (dxh - m1 - xhat * m2)


def _mod_exchange(c_row, w_ada, b_ada_loc):
    def body(c_ref, w_ref, b_ref, call_ref, mod_ref, piece_ref, send_sems, recv_sems):
        me = _me()[3]
        call_ref[me] = c_ref[...]
        sent = []
        for r in range(1, N_DEV):
            peer, _ = _peer(r)
            cp = pltpu.make_async_remote_copy(
                src_ref=c_ref, dst_ref=call_ref.at[me], send_sem=send_sems.at[0, r - 1],
                recv_sem=recv_sems.at[0, r - 1], device_id=peer, device_id_type=MESH)
            cp.start()
            sent.append(cp)
        for r in range(1, N_DEV):
            peer, pidx = _peer(r)
            pltpu.make_async_remote_copy(
                src_ref=c_ref, dst_ref=call_ref.at[pidx], send_sem=send_sems.at[0, r - 1],
                recv_sem=recv_sems.at[0, r - 1], device_id=peer, device_id_type=MESH).wait_recv()
        c_all = jnp.concatenate([call_ref[d] for d in range(N_DEV)], axis=0)
        mod_loc = jnp.dot(c_all, w_ref[...], preferred_element_type=F32,
                          precision=lax.Precision.HIGHEST) + b_ref[...]
        for d in range(N_DEV):
            piece_ref[d] = mod_loc[d:d + 1, :]
        mod_ref[me] = piece_ref[me]
        for r in range(1, N_DEV):
            peer, pidx = _peer(r)
            cp = pltpu.make_async_remote_copy(
                src_ref=piece_ref.at[pidx], dst_ref=mod_ref.at[me], send_sem=send_sems.at[1, r - 1],
                recv_sem=recv_sems.at[1, r - 1], device_id=peer, device_id_type=MESH)
            cp.start()
            sent.append(cp)
        for r in range(1, N_DEV):
            peer, pidx = _peer(r)
            pltpu.make_async_remote_copy(
                src_ref=piece_ref.at[me], dst_ref=mod_ref.at[pidx], send_sem=send_sems.at[1, r - 1],
                recv_sem=recv_sems.at[1, r - 1], device_id=peer, device_id_type=MESH).wait_recv()
        for cp in sent:
            cp.wait_send()

    vmem = pl.BlockSpec(memory_space=pltpu.VMEM)
    return pl.pallas_call(
        body, name="mod_exchange", in_specs=[vmem, vmem, vmem], out_specs=[vmem, vmem],
        out_shape=[jax.ShapeDtypeStruct((N_DEV, 1, D_MODEL), F32), jax.ShapeDtypeStruct((N_DEV, 1, ADA_SHARD), F32)],
        scratch_shapes=[pltpu.VMEM((N_DEV, 1, ADA_SHARD), F32),
                        pltpu.SemaphoreType.DMA((2, N_DEV - 1)), pltpu.SemaphoreType.DMA((2, N_DEV - 1))],
        compiler_params=_params(),
    )(c_row, w_ada, b_ada_loc)


def _split3(x):
    hi = x.astype(BF16)
    r1 = x - hi.astype(F32)
    mid = r1.astype(BF16)
    lo = (r1 - mid.astype(F32)).astype(BF16)
    return hi, mid, lo


def _scan_rows(x_ref, o_ref, s, reverse, pre=None, post=None):
    tb = min(TQ, s)
    nb = s // tb
    row = lax.broadcasted_iota(jnp.int32, (tb, tb), 0)
    col = lax.broadcasted_iota(jnp.int32, (tb, tb), 1)
    tri = jnp.where((col >= row) if reverse else (col <= row), 1.0, 0.0).astype(BF16)

    def step(i, carry):
        blk = (nb - 1 - i) if reverse else i
        off = pl.multiple_of(blk * tb, tb)
        x = x_ref[pl.ds(off, tb), :]
        if pre is not None:
            x = pre(x, off)
        acc = carry
        for piece in _split3(x):
            acc = acc + jnp.dot(tri, piece, preferred_element_type=F32)
        o_ref[pl.ds(off, tb), :] = acc if post is None else post(acc, off)
        edge = acc[0:1, :] if reverse else acc[tb - 1:tb, :]
        return jnp.broadcast_to(edge, (tb, LANES))

    lax.fori_loop(0, nb, step, jnp.zeros((tb, LANES), F32))


def _forget_cumsum(f_raw, b_pad):
    s = f_raw.shape[0]

    def body(f_ref, b_ref, cum_ref):
        b = b_ref[...]
        _scan_rows(f_ref, cum_ref, s, False, pre=lambda x, off: _log_sigmoid(x + b))

    vmem = pl.BlockSpec(memory_space=pltpu.VMEM)
    return pl.pallas_call(body, name="forget_cumsum", in_specs=[vmem, vmem], out_specs=vmem,
                          out_shape=jax.ShapeDtypeStruct((s, LANES), F32), compiler_params=_params())(f_raw, b_pad)


def _forget_bwd(dcum, f_raw, b_pad):
    s = f_raw.shape[0]

    def body(d_ref, f_ref, b_ref, df_ref, db_ref, tmp_ref):
        b = b_ref[...]
        _scan_rows(d_ref, tmp_ref, s, True)
        df = tmp_ref[...] * _sigmoid(-(f_ref[...] + b))
        df_ref[...] = df.astype(BF16)
        db_ref[...] = _colsum(df)

    vmem = pl.BlockSpec(memory_space=pltpu.VMEM)
    return pl.pallas_call(
        body, name="forget_bwd", in_specs=[vmem, vmem, vmem], out_specs=[vmem, vmem],
        out_shape=[jax.ShapeDtypeStruct((s, LANES), BF16), jax.ShapeDtypeStruct((1, LANES), F32)],
        scratch_shapes=[pltpu.VMEM((s, LANES), F32)], compiler_params=_params())(dcum, f_raw, b_pad)


def _dot_nt(a, b):
    return lax.dot_general(a, b, (((1,), (1,)), ((), ())), preferred_element_type=F32)


def _head_masks():
    lane = lax.broadcasted_iota(jnp.int32, (TQ, LANES), 1)
    return lane, [lane < HEAD_DIM, lane >= HEAD_DIM]


def _pick(mask, x):
    return jnp.where(mask, x, jnp.zeros_like(x))


def _qkv_specs(s, col0):
    nb = ATTN_W // LANES
    return [pl.BlockSpec((TQ, LANES), lambda hp, qi: (qi, col0 + hp)),
            pl.BlockSpec((s, LANES), lambda hp, qi: (0, col0 + nb + hp)),
            pl.BlockSpec((s, LANES), lambda hp, qi: (0, col0 + 2 * nb + hp))]


def _pair_spec():
    return pl.BlockSpec((TQ, LANES), lambda hp, qi: (qi, hp))


def _diag_mask(width, strict):
    row = lax.broadcasted_iota(jnp.int32, (TQ, width * TQ), 0) + (width - 1) * TQ
    col = lax.broadcasted_iota(jnp.int32, (TQ, width * TQ), 1)
    return (col < row) if strict else (col <= row)


def _fox_fwd(qkv, cum_col, cum_row, ride=None):
    s = qkv.shape[0]
    nq = s // TQ

    def body(q_ref, k_ref, v_ref, cc_ref, cr_ref, o_ref, o32_ref, lse_ref):
        hp, qi = pl.program_id(0), pl.program_id(1)
        lane, masks = _head_masks()
        q2 = q_ref[...] * jnp.asarray(ATTN_SCALE, BF16)
        cc = cc_ref[...]
        qms = [_pick(masks[e], q2) for e in range(2)]
        cqs = [jnp.sum(jnp.where(lane == 2 * hp + e, cc, 0.0), axis=1, keepdims=True) for e in range(2)]

        def tile(kb, carry, masked, width=1):
            off, span = pl.multiple_of(kb * TQ, TQ), width * TQ
            k2, v2 = k_ref[pl.ds(off, span), :], v_ref[pl.ds(off, span), :]
            head0 = lax.broadcasted_iota(jnp.int32, (span, LANES), 1) < HEAD_DIM
            new = []
            for e in range(2):
                m, acc = carry[e]
                sc = _dot_nt(qms[e], k2) + (cqs[e] - cr_ref[e:e + 1, pl.ds(off, span)])
                if masked:
                    sc = jnp.where(_diag_mask(width, False), sc, -jnp.inf)
                m_new = jnp.maximum(m, jnp.max(sc, axis=1, keepdims=True))
                p = jnp.exp(sc - m_new)
                v_ones = jnp.where(head0 if e == 0 else ~head0, v2, jnp.ones_like(v2))
                acc = jnp.exp(m - m_new) * acc + jnp.dot(p.astype(BF16), v_ones, preferred_element_type=F32)
                new.append((m_new, acc))
            return tuple(new)

        init = (jnp.full((TQ, 1), -jnp.inf, F32), jnp.zeros((TQ, LANES), F32))
        carry = lax.cond(qi % 2 == 1, lambda cr: tile(qi - 1, cr, True, 2), lambda cr: tile(qi, cr, True), (init, init))
        quads = qi // 4
        carry = lax.fori_loop(0, quads, lambda j, cr: tile(4 * j, cr, False, 4), carry)
        carry = lax.cond(qi % 4 >= 2, lambda cr: tile(4 * quads, cr, False, 2), lambda cr: cr, carry)
        sums = [jnp.max(jnp.where(masks[1 - e], carry[e][1], 0.0), axis=1, keepdims=True) for e in range(2)]
        outs = [carry[e][1] / sums[e] for e in range(2)]
        lses = [carry[e][0] + jnp.log(sums[e]) for e in range(2)]
        out = jnp.where(masks[0], outs[0], outs[1])
        o_ref[...] = out.astype(BF16)
        o32_ref[...] = out
        lse_ref[...] = jnp.where(masks[0], lses[0], lses[1])

    return _ride_call(
        body, ride, name="fox_fwd", grid=(N_HEADS // 2, nq),
        in_specs=_qkv_specs(s, 0) + [pl.BlockSpec((TQ, LANES), lambda hp, qi: (qi, 0)),
                                     pl.BlockSpec((None, 2, s), lambda hp, qi: (hp, 0, 0))],
        out_specs=[_pair_spec(), _pair_spec(), _pair_spec()],
        out_shape=[jax.ShapeDtypeStruct((s, ATTN_W), BF16), jax.ShapeDtypeStruct((s, ATTN_W), F32),
                   jax.ShapeDtypeStruct((s, ATTN_W), F32)],
        scratch_shapes=[], sem=("parallel", "parallel"), args=(qkv, qkv, qkv, cum_col, cum_row))


def _write_transposed(acc_ref, out_ref):
    for c in range(out_ref.shape[0] // TQ):
        out_ref[c * TQ:(c + 1) * TQ, :] = jnp.transpose(acc_ref[:, c * TQ:(c + 1) * TQ]).astype(BF16)


def _fox_bwd(qkv, cum_col, cum_row, o, lse, do, ride=None):
    s = qkv.shape[0]
    nq = s // TQ

    def body(q_ref, k_ref, v_ref, cc_ref, cr_ref, o_ref, lse_ref, do_ref,
             dq_ref, dk_ref, dv_ref, dcr_ref, dk_acc, dv_acc):
        hp, qi = pl.program_id(0), pl.program_id(1)

        @pl.when(qi == 0)
        def _():
            dk_acc[...] = jnp.zeros_like(dk_acc)
            dv_acc[...] = jnp.zeros_like(dv_acc)
            dcr_ref[...] = jnp.zeros_like(dcr_ref)

        lane, masks = _head_masks()
        q2 = q_ref[...] * jnp.asarray(ATTN_SCALE, BF16)
        do2 = do_ref[...]
        prod = do2.astype(F32) * o_ref[...].astype(F32)
        lse2 = lse_ref[...]
        cc = cc_ref[...]
        qms = [_pick(masks[e], q2) for e in range(2)]
        doms = [_pick(masks[e], do2) for e in range(2)]
        deltas = [jnp.sum(jnp.where(masks[e], prod, 0.0), axis=1, keepdims=True) for e in range(2)]
        lses = [jnp.max(jnp.where(masks[e], lse2, -jnp.inf), axis=1, keepdims=True) for e in range(2)]
        cqs = [jnp.sum(jnp.where(lane == 2 * hp + e, cc, 0.0), axis=1, keepdims=True) for e in range(2)]
        qts = [jnp.transpose(qms[e].astype(F32)).astype(BF16) for e in range(2)]
        dots = [jnp.transpose(doms[e].astype(F32)).astype(BF16) for e in range(2)]
        ones_row = [HEAD_DIM * (1 - e) for e in range(2)]
        trow = lax.broadcasted_iota(jnp.int32, (LANES, TQ), 0)
        qts = [jnp.where(trow == ones_row[e], jnp.ones_like(qts[e]), qts[e]) for e in range(2)]

        def tile(kb, carry, masked, width=1):
            off, span = pl.multiple_of(kb * TQ, TQ), width * TQ
            k2, v2 = k_ref[pl.ds(off, span), :], v_ref[pl.ds(off, span), :]
            head0 = lax.broadcasted_iota(jnp.int32, (span, LANES), 1) < HEAD_DIM
            new, dks, dv = [], [], None
            for e in range(2):
                dq = carry[e]
                sc = _dot_nt(qms[e], k2) + (cqs[e] - cr_ref[e:e + 1, pl.ds(off, span)])
                p = jnp.exp(sc - lses[e])
                if masked:
                    p = jnp.where(_diag_mask(width, False), p, 0.0)
                ds = p * (_dot_nt(doms[e], v2) - deltas[e])
                dsb = ds.astype(BF16)
                dk_e = jnp.dot(qts[e], dsb, preferred_element_type=F32)
                dv_e = jnp.dot(dots[e], p.astype(BF16), preferred_element_type=F32)
                dks.append(dk_e)
                dv = dv_e if e == 0 else dv + dv_e
                dcr_ref[e:e + 1, pl.ds(off, span)] -= dk_e[ones_row[e]:ones_row[e] + 1, :]
                k_ones = jnp.where(head0 if e == 0 else ~head0, k2, jnp.ones_like(k2))
                new.append(dq + jnp.dot(dsb, k_ones, preferred_element_type=F32))
            krow = lax.broadcasted_iota(jnp.int32, (LANES, span), 0)
            dk_acc[:, pl.ds(off, span)] += jnp.where(krow < HEAD_DIM, dks[0], dks[1])
            dv_acc[:, pl.ds(off, span)] += dv
            return tuple(new)

        init = jnp.zeros((TQ, LANES), F32)
        carry = lax.fori_loop(0, qi // 2, lambda j, cr: tile(2 * j, cr, False, 2), (init, init))
        carry = lax.cond(qi % 2 == 1, lambda cr: tile(qi - 1, cr, True, 2), lambda cr: tile(qi, cr, True), carry)
        dq_ref[...] = (jnp.where(masks[0], carry[0], carry[1]) * ATTN_SCALE).astype(BF16)
        for e in range(2):
            dcr_ref[e:e + 1, pl.ds(pl.multiple_of(qi * TQ, TQ), TQ)] += jnp.transpose(carry[e])[
                ones_row[e]:ones_row[e] + 1, :]

        @pl.when(qi == nq - 1)
        def _():
            _write_transposed(dk_acc, dk_ref)
            _write_transposed(dv_acc, dv_ref)

    seq_spec = pl.BlockSpec((s, LANES), lambda hp, qi: (0, hp))
    return _ride_call(
        body, ride, name="fox_bwd", grid=(N_HEADS // 2, nq),
        in_specs=_qkv_specs(s, 0) + [pl.BlockSpec((TQ, LANES), lambda hp, qi: (qi, 0)),
                                     pl.BlockSpec((None, 2, s), lambda hp, qi: (hp, 0, 0)),
                                     _pair_spec(), _pair_spec(), _pair_spec()],
        out_specs=[_pair_spec(), seq_spec, seq_spec, pl.BlockSpec((None, 2, s), lambda hp, qi: (hp, 0, 0))],
        out_shape=[jax.ShapeDtypeStruct((s, ATTN_W), BF16)] * 3 + [jax.ShapeDtypeStruct((N_HEADS // 2, 2, s), F32)],
        scratch_shapes=[pltpu.VMEM((LANES, s), F32), pltpu.VMEM((LANES, s), F32)],
        sem=("parallel", "arbitrary"), args=(qkv, qkv, qkv, cum_col, cum_row, o, lse, do))


def _scan_matrix(reverse):
    row = lax.broadcasted_iota(jnp.int32, (SCAN_W, SCAN_W), 0)
    col = lax.broadcasted_iota(jnp.int32, (SCAN_W, SCAN_W), 1)
    return jnp.where((row > col) if reverse else (row < col), 1.0, 0.0).astype(BF16)


def _scan_cols(x, tri, reverse, init):
    nblk = x.shape[1] // SCAN_W
    parts, total = [None] * nblk, init
    far = 0 if reverse else SCAN_W - 1
    for b in (reversed(range(nblk)) if reverse else range(nblk)):
        blk = x[:, b * SCAN_W:(b + 1) * SCAN_W]
        part = jnp.dot(blk.astype(BF16), tri, preferred_element_type=F32)
        parts[b] = part + total
        total = total + (part[:, far:far + 1] + blk[:, far:far + 1])
    return (parts[0] if nblk == 1 else jnp.concatenate(parts, axis=1)), total


def _sb_logits(qm, k2):
    z = _dot_nt(qm, k2)
    neg_abs = lax.bitcast_convert_type(lax.bitcast_convert_type(z, jnp.uint32) | jnp.uint32(0x80000000), F32)
    soft = jnp.log(1.0 + jnp.exp(neg_abs))
    lb = jnp.minimum(z, 0.0) - soft
    return lb, lb - z


TILE_SLOTS = 4


def _tri_base(qi):
    return (qi * (qi + 1)) // 2


def _sb_fwd(qkv):
    s = qkv.shape[0]
    nq = s // TQ

    def body(q_ref, k_ref, v_ref, o_ref, t_ref, buf, sems):
        hp, qi = pl.program_id(0), pl.program_id(1)
        _, masks = _head_masks()
        suffix = _scan_matrix(True)
        q2 = q_ref[...] * jnp.asarray(ATTN_SCALE, BF16)
        qms = [_pick(masks[e], q2) for e in range(2)]
        base = _tri_base(qi)

        def store(e, kb):
            slot = kb % TILE_SLOTS
            return pltpu.make_async_copy(buf.at[e, slot], t_ref.at[2 * hp + e, base + kb], sems.at[e, slot])

        def tile(kb, carry, masked, width=1):
            off, span = pl.multiple_of(kb * TQ, TQ), width * TQ
            k2, v2 = k_ref[pl.ds(off, span), :], v_ref[pl.ds(off, span), :]
            new = []
            for e in range(2):
                run, acc = carry[e]
                lb, lo = _sb_logits(qms[e], k2)
                if masked:
                    strict = _diag_mask(width, True)
                    lo = jnp.where(strict, lo, 0.0)
                rest, run = _scan_cols(lo, suffix, True, run)
                a = jnp.exp(lb + rest)
                if masked:
                    a = jnp.where(strict, a, 0.0)
                ab, lbb = a.astype(BF16), lb.astype(BF16)
                acc = acc + jnp.dot(ab, v2, preferred_element_type=F32)
                for w in range(width):
                    blk = kb + w

                    @pl.when(blk + TILE_SLOTS <= qi)
                    def _(e=e, blk=blk):
                        store(e, blk + TILE_SLOTS).wait()
                    buf[e, blk % TILE_SLOTS, 0] = ab[:, w * TQ:(w + 1) * TQ]
                    buf[e, blk % TILE_SLOTS, 1] = lbb[:, w * TQ:(w + 1) * TQ]
                    store(e, blk).start()
                new.append((run, acc))
            return tuple(new)

        init = (jnp.zeros((TQ, 1), F32), jnp.zeros((TQ, LANES), F32))
        carry = lax.cond(qi % 2 == 1, lambda cr: tile(qi - 1, cr, True, 2), lambda cr: tile(qi, cr, True), (init, init))
        pairs = qi // 2
        carry = lax.fori_loop(0, pairs, lambda it, cr: tile(2 * (pairs - 1 - it), cr, False, 2), carry)
        for e in range(2):
            for blk in range(TILE_SLOTS):
                @pl.when(qi >= blk)
                def _(e=e, blk=blk):
                    store(e, blk).wait()
        o_ref[...] = jnp.where(masks[0], carry[0][1], carry[1][1]).astype(BF16)

    ntri = nq * (nq + 1) // 2
    return pl.pallas_call(
        body, name="sb_fwd", grid=(N_HEADS // 2, nq), in_specs=_qkv_specs(s, 3 * ATTN_W // LANES),
        out_specs=[_pair_spec(), ANY],
        out_shape=[jax.ShapeDtypeStruct((s, ATTN_W), BF16), jax.ShapeDtypeStruct((N_HEADS, ntri, 2, TQ, TQ), BF16)],
        scratch_shapes=[pltpu.VMEM((2, TILE_SLOTS, 2, TQ, TQ), BF16), pltpu.SemaphoreType.DMA((2, TILE_SLOTS))],
        compiler_params=_params(("arbitrary", "arbitrary")),
    )(qkv, qkv, qkv)


def _sb_bwd(qkv, tiles, do):
    s = qkv.shape[0]
    nq = s // TQ

    def body(q_ref, k_ref, v_ref, t_ref, do_ref, dq_ref, dk_ref, dv_ref, dk_acc, dv_acc, buf, sems):
        hp, qi = pl.program_id(0), pl.program_id(1)

        @pl.when(qi == 0)
        def _():
            dk_acc[...] = jnp.zeros_like(dk_acc)
            dv_acc[...] = jnp.zeros_like(dv_acc)

        _, masks = _head_masks()
        prefix = _scan_matrix(False)
        q2 = q_ref[...] * jnp.asarray(ATTN_SCALE, BF16)
        do2 = do_ref[...]
        qms = [_pick(masks[e], q2) for e in range(2)]
        doms = [_pick(masks[e], do2) for e in range(2)]
        qts = [jnp.transpose(qms[e].astype(F32)).astype(BF16) for e in range(2)]
        dots = [jnp.transpose(doms[e].astype(F32)).astype(BF16) for e in range(2)]
        base = _tri_base(qi)

        def fetch(e, kb):
            slot = kb % TILE_SLOTS
            return pltpu.make_async_copy(t_ref.at[2 * hp + e, base + kb], buf.at[e, slot], sems.at[e, slot])

        for e in range(2):
            fetch(e, 0).start()

            @pl.when(qi >= 1)
            def _(e=e):
                fetch(e, 1).start()

        def tile(kb, carry, masked, width=1):
            off, span = pl.multiple_of(kb * TQ, TQ), width * TQ
            k2, v2 = k_ref[pl.ds(off, span), :], v_ref[pl.ds(off, span), :]
            new, dk, dv = [], None, None
            for e in range(2):
                gsum, dq = carry[e]
                if not masked:
                    for blk in range(2, 2 + width):
                        @pl.when(kb + blk <= qi)
                        def _(e=e, blk=blk):
                            fetch(e, kb + blk).start()
                for w in range(width):
                    fetch(e, kb + w).wait()
                slots = [(kb + w) % TILE_SLOTS for w in range(width)]
                ab = buf[e, slots[0], 0] if width == 1 else jnp.concatenate([buf[e, sl, 0] for sl in slots], axis=1)
                lbb = buf[e, slots[0], 1] if width == 1 else jnp.concatenate([buf[e, sl, 1] for sl in slots], axis=1)
                beta = jnp.exp(lbb.astype(F32))
                g = ab.astype(F32) * _dot_nt(doms[e], v2)
                before, gsum = _scan_cols(g, prefix, False, gsum)
                dz = g - beta * (g + before)
                if masked:
                    dz = jnp.where(_diag_mask(width, True), dz, 0.0)
                dzb = dz.astype(BF16)
                dk_e = jnp.dot(qts[e], dzb, preferred_element_type=F32)
                dv_e = jnp.dot(dots[e], ab, preferred_element_type=F32)
                dk, dv = (dk_e, dv_e) if e == 0 else (dk + dk_e, dv + dv_e)
                new.append((gsum, dq + jnp.dot(dzb, k2, preferred_element_type=F32)))
            dk_acc[:, pl.ds(off, span)] += dk
            dv_acc[:, pl.ds(off, span)] += dv
            return tuple(new)

        init = (jnp.zeros((TQ, 1), F32), jnp.zeros((TQ, LANES), F32))
        carry = lax.fori_loop(0, qi // 2, lambda j, cr: tile(2 * j, cr, False, 2), (init, init))
        carry = lax.cond(qi % 2 == 1, lambda cr: tile(qi - 1, cr, True, 2), lambda cr: tile(qi, cr, True), carry)
        dq_ref[...] = (jnp.where(masks[0], carry[0][1], carry[1][1]) * ATTN_SCALE).astype(BF16)

        @pl.when(qi == nq - 1)
        def _():
            _write_transposed(dk_acc, dk_ref)
            _write_transposed(dv_acc, dv_ref)

    seq_spec = pl.BlockSpec((s, LANES), lambda hp, qi: (0, hp))
    return pl.pallas_call(
        body, name="sb_bwd", grid=(N_HEADS // 2, nq),
        in_specs=_qkv_specs(s, 3 * ATTN_W // LANES) + [ANY, _pair_spec()],
        out_specs=[_pair_spec(), seq_spec, seq_spec],
        out_shape=[jax.ShapeDtypeStruct((s, ATTN_W), BF16)] * 3,
        scratch_shapes=[pltpu.VMEM((LANES, s), F32), pltpu.VMEM((LANES, s), F32),
                        pltpu.VMEM((2, TILE_SLOTS, 2, TQ, TQ), BF16), pltpu.SemaphoreType.DMA((2, TILE_SLOTS))],
        compiler_params=_params(("arbitrary", "arbitrary")),
    )(qkv, qkv, qkv, tiles, do)


CONV_TR = 512


def _shift_down(x, halo, n):
    rolled = pltpu.roll(x, n, 0)
    head = rolled[0:8, :]
    rid = lax.broadcasted_iota(jnp.int32, head.shape, 0)
    for j in range(n):
        head = jnp.where(rid == j, halo[8 - n + j:8 - n + j + 1, :], head)
    return jnp.concatenate([head, rolled[8:, :]], axis=0)


def _shift_up(x, halo, n):
    rows = x.shape[0]
    rolled = pltpu.roll(x, rows - n, 0)
    tail = rolled[rows - 8:, :]
    rid = lax.broadcasted_iota(jnp.int32, tail.shape, 0)
    for j in range(n):
        tail = jnp.where(rid == 8 - n + j, halo[j:j + 1, :], tail)
    return jnp.concatenate([rolled[:rows - 8, :], tail], axis=0)


def _conv_fwd_block(x, halo, w, b):
    return b + _shift_down(x, halo, 2) * w[0:1, :] + _shift_down(x, halo, 1) * w[1:2, :] + x * w[2:3, :]


def _conv_specs(tr, s):
    pair = 2 * FF_HALF
    blk = pl.BlockSpec((tr, pair), lambda j, i: (i, j))
    prev = pl.BlockSpec((8, pair), lambda j, i: (jnp.maximum(i * (tr // 8) - 1, 0), j))
    nxt = pl.BlockSpec((8, pair), lambda j, i: (jnp.minimum((i + 1) * (tr // 8), s // 8 - 1), j))
    return blk, prev, nxt


def _conv_gate_fwd(hpre, conv_w, conv_b):
    s = hpre.shape[0]
    tr = min(CONV_TR, s)
    blk, prev, _ = _conv_specs(tr, s)

    def body(x_ref, halo_ref, w_ref, b_ref, a_ref):
        i = pl.program_id(1)
        halo = jnp.where(i > 0, halo_ref[...], 0.0)
        h = _conv_fwd_block(x_ref[...], halo, w_ref[...], b_ref[...])
        hg, hv = h[:, :FF_HALF], h[:, FF_HALF:]
        a_ref[...] = (hg * _sigmoid(hg) * hv).astype(BF16)

    return pl.pallas_call(
        body, name="conv_gate_fwd", grid=(2, s // tr),
        in_specs=[blk, prev, pl.BlockSpec((3, 2 * FF_HALF), lambda j, i: (0, j)),
                  pl.BlockSpec((1, 2 * FF_HALF), lambda j, i: (0, j))],
        out_specs=pl.BlockSpec((tr, FF_HALF), lambda j, i: (i, j)),
        out_shape=jax.ShapeDtypeStruct((s, D_FF), BF16),
        compiler_params=_params(("parallel", "parallel")),
    )(hpre, hpre, conv_w, conv_b)


def _conv_gate_bwd(hpre, da, conv_w, conv_b):
    s = hpre.shape[0]
    tr = min(CONV_TR, s)
    blk, prev, _ = _conv_specs(tr, s)

    def body(x_ref, halo_ref, da_ref, w_ref, b_ref, dh_ref, db_ref, dw_ref):
        i = pl.program_id(1)
        halo = jnp.where(i > 0, halo_ref[...], 0.0)
        x = x_ref[...]
        h = _conv_fwd_block(x, halo, w_ref[...], b_ref[...])
        hg, hv = h[:, :FF_HALF], h[:, FF_HALF:]
        da_blk = da_ref[...].astype(F32)
        sg = _sigmoid(hg)
        dhg = da_blk * hv * (sg * (1.0 + hg * (1.0 - sg)))
        dhv = da_blk * (hg * sg)
        dh_ref[:, :FF_HALF] = dhg.astype(BF16)
        dh_ref[:, FF_HALF:] = dhv.astype(BF16)
        x2, x1 = _shift_down(x, halo, 2), _shift_down(x, halo, 1)
        parts = []
        for lo, dpart in ((0, dhg), (FF_HALF, dhv)):
            cols = slice(lo, lo + FF_HALF)
            parts.append((cols, _colsum(dpart), _colsum(dpart * x2[:, cols]), _colsum(dpart * x1[:, cols]),
                          _colsum(dpart * x[:, cols])))

        @pl.when(i == 0)
        def _():
            for cols, db, dw0, dw1, dw2 in parts:
                db_ref[:, cols] = db
                dw_ref[0:1, cols] = dw0
                dw_ref[1:2, cols] = dw1
                dw_ref[2:3, cols] = dw2

        @pl.when(i > 0)
        def _():
            for cols, db, dw0, dw1, dw2 in parts:
                db_ref[:, cols] += db
                dw_ref[0:1, cols] += dw0
                dw_ref[1:2, cols] += dw1
                dw_ref[2:3, cols] += dw2

    pair = 2 * FF_HALF
    return pl.pallas_call(
        body, name="conv_gate_bwd", grid=(2, s // tr),
        in_specs=[blk, prev, pl.BlockSpec((tr, FF_HALF), lambda j, i: (i, j)),
                  pl.BlockSpec((3, pair), lambda j, i: (0, j)), pl.BlockSpec((1, pair), lambda j, i: (0, j))],
        out_specs=[blk, pl.BlockSpec((1, pair), lambda j, i: (0, j)), pl.BlockSpec((3, pair), lambda j, i: (0, j))],
        out_shape=[jax.ShapeDtypeStruct((s, 2 * D_FF), BF16), jax.ShapeDtypeStruct((1, 2 * D_FF), F32),
                   jax.ShapeDtypeStruct((3, 2 * D_FF), F32)],
        compiler_params=_params(("parallel", "arbitrary")),
    )(hpre, hpre, da, conv_w, conv_b)


def _conv_input_bwd(dh, conv_w):
    s = dh.shape[0]
    tr = min(CONV_TR, s)
    blk, _, _ = _conv_specs(tr, s)
    nblk = s // tr

    def body(x_ref, halo_ref, w_ref, o_ref):
        i = pl.program_id(1)
        halo = jnp.where(i < nblk - 1, halo_ref[...].astype(F32), 0.0)
        x, w = x_ref[...].astype(F32), w_ref[...]
        o_ref[...] = (x * w[2:3, :] + _shift_up(x, halo, 1) * w[1:2, :] + _shift_up(x, halo, 2) * w[0:1, :]).astype(BF16)

    nxt = pl.BlockSpec((16, 2 * FF_HALF), lambda j, i: (jnp.minimum((i + 1) * (tr // 16), s // 16 - 1), j))
    return pl.pallas_call(
        body, name="conv_input_bwd", grid=(2, nblk),
        in_specs=[blk, nxt, pl.BlockSpec((3, 2 * FF_HALF), lambda j, i: (0, j))], out_specs=blk,
        out_shape=jax.ShapeDtypeStruct((s, 2 * D_FF), BF16),
        compiler_params=_params(("parallel", "parallel")),
    )(dh, dh, conv_w)


def _adamw_math(w, g, m, v):
    m = ADAM_B1 * m + (1.0 - ADAM_B1) * g
    v = ADAM_B2 * v + (1.0 - ADAM_B2) * (g * g)
    m_hat = m / (1.0 - ADAM_B1 ** ADAM_STEP)
    v_hat = v / (1.0 - ADAM_B2 ** ADAM_STEP)
    delta = -ADAM_LR * (m_hat / (jnp.sqrt(v_hat) + ADAM_EPS) + ADAM_WD * w)
    return delta, m, v


def _adamw(name, g8, w, m, v):
    r, c = w.shape
    tr = _row_tile(r, c)

    def body(g_ref, w_ref, m_ref, v_ref, go_ref, d_ref, mo_ref, vo_ref):
        g = g_ref[0].astype(F32)
        for d in range(1, N_DEV):
            g = g + g_ref[d].astype(F32)
        delta, mn, vn = _adamw_math(w_ref[...], g, m_ref[...], v_ref[...])
        go_ref[...] = g
        d_ref[...] = delta
        mo_ref[...] = mn
        vo_ref[...] = vn

    spec = pl.BlockSpec((tr, c), lambda i: (i, 0))
    return pl.pallas_call(
        body, name=name, grid=(r // tr,),
        in_specs=[pl.BlockSpec((N_DEV, tr, c), lambda i: (0, i, 0)), spec, spec, spec], out_specs=[spec] * 4,
        out_shape=[jax.ShapeDtypeStruct((r, c), F32)] * 4, compiler_params=_params(("parallel",)),
    )(g8, w, m, v)


def _adamw_small(g8, ws, ms, vs):
    n = len(ws)
    offsets = [sum(w.shape[0] for w in ws[:i]) for i in range(n)]

    def body(*refs):
        g_ref, w_refs, m_refs, v_refs = refs[0], refs[1:1 + n], refs[1 + n:1 + 2 * n], refs[1 + 2 * n:1 + 3 * n]
        outs = refs[1 + 3 * n:]
        for i in range(n):
            rows = w_refs[i].shape[0]
            g = g_ref[0, offsets[i]:offsets[i] + rows, :]
            for d in range(1, N_DEV):
                g = g + g_ref[d, offsets[i]:offsets[i] + rows, :]
            delta, mn, vn = _adamw_math(w_refs[i][...], g, m_refs[i][...], v_refs[i][...])
            for k, val in enumerate((g, delta, mn, vn)):
                outs[k * n + i][...] = val

    vmem = pl.BlockSpec(memory_space=pltpu.VMEM)
    res = pl.pallas_call(
        body, name="adamw_small", in_specs=[vmem] * (1 + 3 * n), out_specs=[vmem] * (4 * n),
        out_shape=[jax.ShapeDtypeStruct(w.shape, F32) for _ in range(4) for w in ws], compiler_params=_params(),
    )(g8, *ws, *ms, *vs)
    return [res[k * n:(k + 1) * n] for k in range(4)]


def _adamw_ada(c_t, dmod, w, m, v):
    r, c = w.shape
    tr = _row_tile(r, c)

    def body(ct_ref, dm_ref, w_ref, m_ref, v_ref, go_ref, d_ref, mo_ref, vo_ref):
        ct, dm = ct_ref[...], dm_ref[...]
        g = ct[:, 0:1] * dm[0:1, :]
        for b in range(1, N_DEV):
            g = g + ct[:, b:b + 1] * dm[b:b + 1, :]
        delta, mn, vn = _adamw_math(w_ref[...], g, m_ref[...], v_ref[...])
        go_ref[...] = g
        d_ref[...] = delta
        mo_ref[...] = mn
        vo_ref[...] = vn

    spec = pl.BlockSpec((tr, c), lambda i: (i, 0))
    return pl.pallas_call(
        body, name="adamw_w_ada", grid=(r // tr,),
        in_specs=[pl.BlockSpec((tr, N_DEV), lambda i: (i, 0)), pl.BlockSpec((N_DEV, c), lambda i: (0, 0)),
                  spec, spec, spec],
        out_specs=[spec] * 4, out_shape=[jax.ShapeDtypeStruct((r, c), F32)] * 4,
        compiler_params=_params(("parallel",)),
    )(c_t, dmod, w, m, v)


def _cols_from_slots(g):
    n, r, c = g.shape
    return jnp.transpose(g, (1, 0, 2)).reshape(r, n * c)


def _cols_to_slots(w):
    r, c = w.shape
    return jnp.transpose(w.reshape(r, N_DEV, c // N_DEV), (1, 0, 2))


def _pair_cols(w):
    g0, g1 = w[..., 0:FF_HALF], w[..., FF_HALF:D_FF]
    v0, v1 = w[..., D_FF:D_FF + FF_HALF], w[..., D_FF + FF_HALF:]
    return jnp.concatenate([g0, v0, g1, v1], axis=-1)


def _unpair_cols(w):
    g0, v0 = w[..., 0:FF_HALF], w[..., FF_HALF:D_FF]
    g1, v1 = w[..., D_FF:D_FF + FF_HALF], w[..., D_FF + FF_HALF:]
    return jnp.concatenate([g0, g1, v0, v1], axis=-1)


def _row(v):
    return v.reshape(1, -1)


def kernel(x, c, w_ada, b_ada, w_in, b_forget, w_fox_proj, w_sb_proj, w_o, ln1_g, ln1_b, w_up, conv_w, conv_b, w_down, ln2_g, ln2_b, loss_target, m_w_ada, m_b_ada, m_w_in, m_b_forget, m_w_fox_proj, m_w_sb_proj, m_w_o, m_ln1_g, m_ln1_b, m_w_up, m_conv_w, m_conv_b, m_w_down, m_ln2_g, m_ln2_b, v_w_ada, v_b_ada, v_w_in, v_b_forget, v_w_fox_proj, v_w_sb_proj, v_w_o, v_ln1_g, v_ln1_b, v_w_up, v_conv_w, v_conv_b, v_w_down, v_ln2_g, v_ln2_b):
    s = x.shape[1]
    me = 4 * lax.axis_index("x") + 2 * lax.axis_index("y") + lax.axis_index("c")
    x2 = x.reshape(s, D_MODEL)
    tgt = loss_target.reshape(s, D_MODEL)

    b_ada_loc = lax.dynamic_slice(b_ada, (me * ADA_SHARD,), (ADA_SHARD,)).reshape(1, ADA_SHARD)
    c_all, mod = _mod_exchange(c, w_ada, b_ada_loc)
    mod = mod.reshape(N_MOD, 1, D_MODEL)
    sh1, sc1, gt1, sh2, sc2, gt2 = [mod[i] for i in range(N_MOD)]

    g_in = _allgather_two_level("ag_w_in", w_in.astype(BF16))
    late_weights = _Ride([w_fox_proj.astype(BF16), w_sb_proj.astype(BF16), w_o.astype(BF16), w_up.astype(BF16),
                          w_down.astype(BF16), conv_w], scatter=False)
    w_in_f = _cols_from_slots(g_in)
    w_proj = jnp.concatenate(
        [w_in_f[:, 0:1536], w_in_f[:, 1544:3080], w_in_f[:, 3080:5128], w_in_f[:, 1536:1544],
         jnp.zeros((D_MODEL, W_PROJ - 5128), BF16)], axis=1)
    w_qkv, w_gates, w_f = w_proj[:, :W_QKV], w_proj[:, W_QKV:W_QKV + W_GATES], w_proj[:, W_QKV + W_GATES:W_QKV + W_GATES + W_F]
    conv_b_p = _pair_cols(_row(conv_b))
    b_f_pad = jnp.pad(_row(b_forget), ((0, 0), (0, LANES - N_HEADS)))

    (u1,) = _rowwise("modulate1", lambda xb, sc, sh: (xb * (1.0 + sc) + sh,),
                     [(x2, D_MODEL, 0)], [sc1, sh1], [(D_MODEL, BF16)], tr=512)
    qkv = _mm(u1, w_qkv, name="mm_qkv", out_dtype=BF16)
    gates = _mm(u1, w_gates, name="mm_gates")
    f_raw = _mm(u1, w_f, name="mm_forget")
    cum_col = _forget_cumsum(f_raw, b_f_pad)
    cum_row = jnp.transpose(cum_col[:, :N_HEADS]).reshape(N_HEADS // 2, 2, s)
    (y_fox, y_fox32, lse), (g_fox, g_sb, g_o, g_up, g_down, g_cw) = _fox_fwd(qkv, cum_col, cum_row, ride=late_weights)
    w_fox_f = _cols_from_slots(g_fox)
    w_sb_f = _cols_from_slots(g_sb)
    w_o_f = g_o.reshape(D_MODEL, D_MODEL)
    w_up_p = _pair_cols(_cols_from_slots(g_up))
    w_down_f = g_down.reshape(D_FF, D_MODEL)
    conv_w_p = _pair_cols(_cols_from_slots(g_cw))
    y_sb, sb_run = _sb_fwd(qkv)
    pf = _mm(y_fox, w_fox_f, name="mm_fox_proj", out_dtype=BF16)
    ps = _mm(y_sb, w_sb_f, name="mm_sb_proj", out_dtype=BF16)
    (merged,) = _rowwise("gate_merge", lambda ga, gb, a, b: (_sigmoid(ga) * a + _sigmoid(gb) * b,),
                         [(gates, D_MODEL, 0), (gates, D_MODEL, 1), (pf, D_MODEL, 0), (ps, D_MODEL, 0)], [],
                         [(D_MODEL, BF16)])
    attn_out = _mm(merged, w_o_f, name="mm_w_o")

    def ln_fwd(xb, fb, gt, g, b):
        xhat, _ = _ln_stats(ALPHA * xb + (1.0 + gt) * fb)
        return xhat * g + b

    def ln1_mod(xb, fb, gt, g, b, sc, sh):
        y = ln_fwd(xb, fb, gt, g, b)
        return y, y * (1.0 + sc) + sh

    x1, u2 = _rowwise("ln1_modulate2", ln1_mod, [(x2, D_MODEL, 0), (attn_out, D_MODEL, 0)],
                      [gt1, _row(ln1_g), _row(ln1_b), sc2, sh2], [(D_MODEL, F32), (D_MODEL, BF16)])

    hpre = _mm(u2, w_up_p, name="mm_w_up", tn=1408)
    act = _conv_gate_fwd(hpre, conv_w_p, conv_b_p)
    ffn_out = _mm(act, w_down_f, name="mm_w_down", tk=2816)

    def ln2_bwd(xb, fb, tb, gt, g, b):
        xhat, rstd = _ln_stats(ALPHA * xb + (1.0 + gt) * fb)
        err = (xhat * g + b) - tb
        dy = err * (1.0 / D_MODEL)
        dr = _ln_bwd(dy, xhat, rstd, g)
        return (dr * (1.0 + gt), ALPHA * dr,
                _colsum(err * err), _colsum(dy * xhat), _colsum(dy), _colsum(dr * fb))

    dffn, dx1_res, sq_err, d_ln2_g, d_ln2_b, d_gt2 = _rowwise(
        "ln2_bwd", ln2_bwd, [(x1, D_MODEL, 0), (ffn_out, D_MODEL, 0), (tgt, D_MODEL, 0)],
        [gt2, _row(ln2_g), _row(ln2_b)], [(D_MODEL, BF16), (D_MODEL, F32)], sums=[D_MODEL] * 4)
    loss = lax.psum(0.5 * jnp.sum(sq_err) / D_MODEL, ("x", "y", "c"))

    d_w_down = _mm(act, dffn, name="mm_d_w_down", ta=True, tm=1408, tk=2048, out_dtype=BF16)
    d_act = _mm(dffn, w_down_f, name="mm_d_act", tb=True, tn=1408, out_dtype=BF16)
    dh, d_conv_b_p, d_conv_w_p = _conv_gate_bwd(hpre, d_act, conv_w_p, conv_b_p)
    dhpre = _conv_input_bwd(dh, conv_w_p)
    d_w_up_p = _mm(u2, dhpre, name="mm_d_w_up", ta=True, tn=1408, tk=2048, out_dtype=BF16)
    du2 = _mm(dhpre, w_up_p, name="mm_d_u2", tb=True, tk=2816)

    def ln1_bwd(du, dres, x1b, xb, fb, sc, gt, g):
        dx1 = dres + du * (1.0 + sc)
        xhat, rstd = _ln_stats(ALPHA * xb + (1.0 + gt) * fb)
        dr = _ln_bwd(dx1, xhat, rstd, g)
        return (dr * (1.0 + gt), ALPHA * dr,
                _colsum(du * x1b), _colsum(du), _colsum(dx1 * xhat), _colsum(dx1), _colsum(dr * fb))

    d_attn, dx_res, d_sc2, d_sh2, d_ln1_g, d_ln1_b, d_gt1 = _rowwise(
        "ln1_bwd", ln1_bwd,
        [(du2, D_MODEL, 0), (dx1_res, D_MODEL, 0), (x1, D_MODEL, 0), (x2, D_MODEL, 0), (attn_out, D_MODEL, 0)],
        [sc2, gt1, _row(ln1_g)], [(D_MODEL, BF16), (D_MODEL, F32)], sums=[D_MODEL] * 5)

    d_w_o = _mm(merged, d_attn, name="mm_d_w_o", ta=True, out_dtype=BF16)
    d_merged = _mm(d_attn, w_o_f, name="mm_d_merged", tb=True, out_dtype=BF16)

    def merge_bwd(dm, ga, gb, a, b):
        dm, a, b = dm.astype(F32), a.astype(F32), b.astype(F32)
        sa, sb = _sigmoid(ga), _sigmoid(gb)
        return dm * a * sa * (1.0 - sa), dm * b * sb * (1.0 - sb), dm * sa, dm * sb

    d_ga, d_gb, d_pf, d_ps = _rowwise(
        "gate_merge_bwd", merge_bwd,
        [(d_merged, D_MODEL, 0), (gates, D_MODEL, 0), (gates, D_MODEL, 1), (pf, D_MODEL, 0), (ps, D_MODEL, 0)], [],
        [(D_MODEL, BF16)] * 4)
    d_w_fox = _mm(y_fox, d_pf, name="mm_d_w_fox", ta=True, out_dtype=BF16)
    d_w_sb = _mm(y_sb, d_ps, name="mm_d_w_sb", ta=True, out_dtype=BF16)
    d_y_fox = _mm(d_pf, w_fox_f, name="mm_d_y_fox", tb=True, out_dtype=BF16)
    d_y_sb = _mm(d_ps, w_sb_f, name="mm_d_y_sb", tb=True, out_dtype=BF16)
    early_grads = _Ride(
        [_cols_to_slots(d_w_fox), _cols_to_slots(d_w_sb), d_w_o.reshape(N_DEV, D_MODEL // N_DEV, D_MODEL),
         _cols_to_slots(_unpair_cols(d_w_up_p)), d_w_down.reshape(N_DEV, D_FF // N_DEV, D_MODEL)], scatter=True)
    (dq_a, dk_a, dv_a, d_cum_row), early_slots = _fox_bwd(qkv, cum_col, cum_row, y_fox32, lse, d_y_fox,
                                                                    ride=early_grads)
    dq_b, dk_b, dv_b = _sb_bwd(qkv, sb_run, d_y_sb)
    d_cum = jnp.transpose(d_cum_row.reshape(N_HEADS, s))
    d_cum = jnp.pad(d_cum, ((0, 0), (0, LANES - N_HEADS)))
    d_f, d_b_forget = _forget_bwd(d_cum, f_raw, b_f_pad)
    d_qkv = [dq_a, dk_a, dv_a, dq_b, dk_b, dv_b]
    g_qkv = list(_mm_pieces_tn(u1, d_qkv, name="mm_d_w_in_qkv"))
    g_ga, g_gb, g_f = _mm_pieces_tn(u1, [d_ga, d_gb, d_f], name="mm_d_w_in_gates")
    d_w_in_f = jnp.concatenate(g_qkv[:3] + [g_f[:, :N_HEADS]] + g_qkv[3:] + [g_ga, g_gb], axis=1)
    du1, (in_slots,) = _mm_pieces_nt(d_qkv + [d_ga, d_gb, d_f], w_proj, name="mm_d_u1",
                                     ride=_Ride([_cols_to_slots(d_w_in_f)], scatter=True))

    def x_bwd(du, dres, xb, sc):
        return dres + du * (1.0 + sc), _colsum(du * xb), _colsum(du)

    grad_x, d_sc1, d_sh1 = _rowwise("x_bwd", x_bwd, [(du1, D_MODEL, 0), (dx_res, D_MODEL, 0), (x2, D_MODEL, 0)],
                                    [sc1], [(D_MODEL, F32)], sums=[D_MODEL] * 2, tr=512)

    d_conv_b = _unpair_cols(d_conv_b_p)
    d_conv_w = _unpair_cols(d_conv_w_p)
    cb_pad = N_MOD * D_MODEL - 2 * D_FF
    small = jnp.concatenate(
        [d_sh1, d_sc1, d_gt1, d_sh2, d_sc2, d_gt2, jnp.pad(d_b_forget, ((0, 0), (0, D_MODEL - LANES))),
         d_ln1_g, d_ln1_b, d_ln2_g, d_ln2_b, jnp.pad(d_conv_b, ((0, 0), (0, cb_pad))),
         d_conv_w.reshape(1, 6 * D_FF)], axis=1)
    n_small = small.shape[1] // LANES
    small = jnp.pad(small.reshape(n_small, LANES), ((0, -n_small % 8), (0, 0)))
    (small_all,) = _exchange("ag_small_grads", [small], scatter=False)
    n_rep = n_small - 6 * D_FF // LANES
    cw8 = small_all[:, n_rep:n_small, :].reshape(N_DEV, 3, 2 * D_FF)
    cw8 = lax.dynamic_slice(cw8, (0, 0, me * UP_SHARD), (N_DEV, 3, UP_SHARD))
    dmod8 = small_all[:, :N_MOD * D_MODEL // LANES, :].reshape(N_DEV, N_MOD * D_MODEL)
    dmod_loc = lax.dynamic_slice(dmod8, (0, me * ADA_SHARD), (N_DEV, ADA_SHARD))

    def rows_of(a):
        return a.reshape(-1, LANES)

    def forget_rows(a):
        return rows_of(jnp.pad(a, (0, D_MODEL - N_HEADS)))

    def conv_b_rows(a):
        return rows_of(jnp.pad(a, (0, cb_pad)))

    rep_sizes = {"b_ada": N_MOD * D_MODEL, "b_forget": N_HEADS, "ln1_g": D_MODEL, "ln1_b": D_MODEL, "ln2_g": D_MODEL,
                 "ln2_b": D_MODEL, "conv_b": 2 * D_FF}
    rep_w = [rows_of(b_ada), forget_rows(b_forget), rows_of(ln1_g), rows_of(ln1_b), rows_of(ln2_g), rows_of(ln2_b),
             conv_b_rows(conv_b)]
    rep_m = [rows_of(m_b_ada), forget_rows(m_b_forget), rows_of(m_ln1_g), rows_of(m_ln1_b), rows_of(m_ln2_g),
             rows_of(m_ln2_b), conv_b_rows(m_conv_b)]
    rep_v = [rows_of(v_b_ada), forget_rows(v_b_forget), rows_of(v_ln1_g), rows_of(v_ln1_b), rows_of(v_ln2_g),
             rows_of(v_ln2_b), conv_b_rows(v_conv_b)]
    rep_out = _adamw_small(small_all, rep_w, rep_m, rep_v)
    rep = [{name: a.reshape(-1)[:size] for (name, size), a in zip(rep_sizes.items(), outs)} for outs in rep_out]
    r_conv_w = _adamw("adamw_conv_w", cw8, conv_w, m_conv_w, v_conv_w)
    r_ada = _adamw_ada(jnp.transpose(c_all.reshape(N_DEV, D_MODEL)), dmod_loc, w_ada, m_w_ada, v_w_ada)

    r_in = _adamw("adamw_w_in", in_slots, w_in, m_w_in, v_w_in)
    r_fox = _adamw("adamw_w_fox", early_slots[0], w_fox_proj, m_w_fox_proj, v_w_fox_proj)
    r_sb = _adamw("adamw_w_sb", early_slots[1], w_sb_proj, m_w_sb_proj, v_w_sb_proj)
    r_o = _adamw("adamw_w_o", early_slots[2], w_o, m_w_o, v_w_o)
    r_up = _adamw("adamw_w_up", early_slots[3], w_up, m_w_up, v_w_up)
    r_down = _adamw("adamw_w_down", early_slots[4], w_down, m_w_down, v_w_down)

    def leaf(i):
        return [r_ada[i], rep[i]["b_ada"], r_in[i], rep[i]["b_forget"], r_fox[i], r_sb[i], r_o[i], rep[i]["ln1_g"],
                rep[i]["ln1_b"], r_up[i], r_conv_w[i], rep[i]["conv_b"], r_down[i], rep[i]["ln2_g"], rep[i]["ln2_b"]]

    return (loss, grad_x.reshape(1, s, D_MODEL), *leaf(0), *leaf(1), *leaf(2), *leaf(3))
```

```python
import functools

import jax
import jax.numpy as jnp
from jax import lax
from jax.experimental import pallas as pl
from jax.experimental.pallas import tpu as pltpu

F32 = jnp.float32
BF16 = jnp.bfloat16
MESH = pl.DeviceIdType.MESH
ANY = pl.BlockSpec(memory_space=pl.ANY)

N_DEV = 8
D_MODEL = 1024
HEAD_DIM = 64
N_HEADS = 8
ATTN_W = N_HEADS * HEAD_DIM
D_FF = 2816
FF_HALF = D_FF // 2
N_MOD = 6
ADA_SHARD = N_MOD * D_MODEL // N_DEV
IN_SHARD = 641
UP_SHARD = 704
ATTN_SCALE = HEAD_DIM ** -0.5
ALPHA = 2.0 ** 0.25
LN_EPS = 1e-5
LANES = 128
TQ = 512
SCAN_W = 256
VMEM_LIMIT = 56 * 1024 * 1024

ADAM_LR, ADAM_B1, ADAM_B2, ADAM_EPS, ADAM_WD, ADAM_STEP = 0.001, 0.9, 0.999, 1e-08, 0.01, 10

W_QKV, W_GATES, W_F = 3072, 2048, 128
W_PROJ = 5376


def _params(sem=None):
    return pltpu.CompilerParams(dimension_semantics=sem, vmem_limit_bytes=VMEM_LIMIT)


def _tile(n, cap):
    if n <= cap:
        return n
    best = None
    for t in range(LANES, cap + 1, LANES):
        if n % t == 0:
            best = t
    assert best is not None, (n, cap)
    return best


def _row_tile(r, width, budget=192 * 1024):
    if r * width <= budget or r % 16:
        return r
    best = 16
    for t in range(16, r + 1, 16):
        if r % t == 0 and t * width <= budget:
            best = t
    return best


def _me():
    x, y, c = lax.axis_index("x"), lax.axis_index("y"), lax.axis_index("c")
    return x, y, c, 4 * x + 2 * y + c


def _peer(r):
    x, y, c, _ = _me()
    px = 1 - x if r & 4 else x
    py = 1 - y if r & 2 else y
    pc = 1 - c if r & 1 else c
    return (px, py, pc), 4 * px + 2 * py + pc


class _Ride:
    def __init__(self, arrays, scatter):
        self.arrays, self.scatter, self.n = list(arrays), scatter, len(arrays)
        self.in_specs = [ANY] * self.n
        self.out_specs = [ANY] * self.n
        self.out_shape = [jax.ShapeDtypeStruct(a.shape if scatter else (N_DEV,) + a.shape, a.dtype) for a in arrays]
        self.scratch = [pltpu.SemaphoreType.DMA((self.n, N_DEV - 1)), pltpu.SemaphoreType.DMA((self.n, N_DEV - 1)),
                        pltpu.SemaphoreType.DMA((self.n,))]

    def _local(self, ins, outs, sems, a):
        me = _me()[3]
        return pltpu.make_async_copy(ins[a].at[me] if self.scatter else ins[a], outs[a].at[me], sems[2].at[a])

    def _remote(self, ins, outs, sems, a, r, arriving):
        me = _me()[3]
        peer, pidx = _peer(r)
        src = ins[a].at[me if arriving else pidx] if self.scatter else ins[a]
        return pltpu.make_async_remote_copy(
            src_ref=src, dst_ref=outs[a].at[pidx if arriving else me], send_sem=sems[0].at[a, r - 1],
            recv_sem=sems[1].at[a, r - 1], device_id=peer, device_id_type=MESH)

    def start(self, ins, outs, sems):
        for a in range(self.n):
            self._local(ins, outs, sems, a).start()
        for r in range(1, N_DEV):
            for a in range(self.n):
                self._remote(ins, outs, sems, a, r, False).start()

    def wait(self, ins, outs, sems):
        for r in range(1, N_DEV):
            for a in range(self.n):
                self._remote(ins, outs, sems, a, r, True).wait_recv()
        for r in range(1, N_DEV):
            for a in range(self.n):
                self._remote(ins, outs, sems, a, r, False).wait_send()
        for a in range(self.n):
            self._local(ins, outs, sems, a).wait()


def _exchange(name, arrays, scatter):
    ride = _Ride(arrays, scatter)

    def body(*refs):
        ins, outs, sems = refs[:ride.n], refs[ride.n:2 * ride.n], refs[2 * ride.n:]
        ride.start(ins, outs, sems)
        ride.wait(ins, outs, sems)

    return pl.pallas_call(body, name=name, in_specs=ride.in_specs, out_specs=ride.out_specs, out_shape=ride.out_shape,
                          scratch_shapes=ride.scratch)(*arrays)


def _allgather_two_level(name, a):
    def body(a_ref, out_ref, send_sems, recv_sems, local_sem):
        x, y, c, me = _me()
        sibling = (x, y, 1 - c)
        chips = [(1 - x, y), (x, 1 - y), (1 - x, 1 - y)]

        def idx(px, py, pc):
            return 4 * px + 2 * py + pc

        def copy(k, block, to, src=None):
            slot = out_ref.at[idx(*block)]
            return pltpu.make_async_remote_copy(
                src_ref=slot if src is None else src, dst_ref=slot, send_sem=send_sems.at[k], recv_sem=recv_sems.at[k],
                device_id=to, device_id_type=MESH)

        mine = pltpu.make_async_copy(a_ref, out_ref.at[me], local_sem)
        mine.start()
        first = [copy(0, (x, y, c), sibling, src=a_ref)]
        first += [copy(1 + j, (x, y, c), (*chip, c), src=a_ref) for j, chip in enumerate(chips)]
        for cp in first:
            cp.start()
        passed = [copy(4 + j, (*chip, c), sibling) for j, chip in enumerate(chips)]
        for j, chip in enumerate(chips):
            copy(1 + j, (*chip, c), (x, y, c)).wait_recv()
            passed[j].start()
        copy(0, sibling, (x, y, c)).wait_recv()
        for j, chip in enumerate(chips):
            copy(4 + j, (*chip, 1 - c), (x, y, c)).wait_recv()
        for cp in first + passed:
            cp.wait_send()
        mine.wait()

    return pl.pallas_call(
        body, name=name, in_specs=[ANY], out_specs=ANY,
        out_shape=jax.ShapeDtypeStruct((N_DEV,) + a.shape, a.dtype),
        scratch_shapes=[pltpu.SemaphoreType.DMA((N_DEV - 1,)), pltpu.SemaphoreType.DMA((N_DEV - 1,)),
                        pltpu.SemaphoreType.DMA],
    )(a)


def _with_ride(body, ride, n_in, n_out, grid):
    if ride is None:
        return body
    n = ride.n

    def wrapped(*refs):
        ins, rins = refs[:n_in], refs[n_in:n_in + n]
        outs, routs = refs[n_in + n:n_in + n + n_out], refs[n_in + n + n_out:n_in + 2 * n + n_out]
        rest = refs[n_in + 2 * n + n_out:]
        scratch, sems = rest[:len(rest) - 3], rest[len(rest) - 3:]
        ids = [pl.program_id(d) for d in range(len(grid))]
        first = functools.reduce(lambda p, q: p & q, [i == 0 for i in ids])
        last = functools.reduce(lambda p, q: p & q, [i == g - 1 for i, g in zip(ids, grid)])

        @pl.when(first)
        def _():
            ride.start(rins, routs, sems)

        body(*ins, *outs, *scratch)

        @pl.when(last)
        def _():
            ride.wait(rins, routs, sems)

    return wrapped


def _ride_call(body, ride, *, name, grid, in_specs, out_specs, out_shape, scratch_shapes, sem, args):
    n_in, n_out = len(in_specs), len(out_specs)
    if ride is None:
        res = pl.pallas_call(body, name=name, grid=grid, in_specs=in_specs, out_specs=out_specs, out_shape=out_shape,
                             scratch_shapes=scratch_shapes, compiler_params=_params(sem))(*args)
        return list(res), []
    res = pl.pallas_call(
        _with_ride(body, ride, n_in, n_out, grid), name=name, grid=grid,
        in_specs=list(in_specs) + ride.in_specs, out_specs=list(out_specs) + ride.out_specs,
        out_shape=list(out_shape) + ride.out_shape, scratch_shapes=list(scratch_shapes) + ride.scratch,
        compiler_params=_params(("arbitrary",) * len(grid)))(*args, *ride.arrays)
    return list(res[:n_out]), list(res[n_out:])


def _mm(a, b, *, name, ta=False, tb=False, out_dtype=F32, tm=1024, tn=1024, tk=1024, ride=None):
    m, k = (a.shape[1], a.shape[0]) if ta else a.shape
    n = b.shape[0] if tb else b.shape[1]
    assert (b.shape[1] if tb else b.shape[0]) == k
    tm, tn, tk = _tile(m, tm), _tile(n, tn), _tile(k, tk)
    nk = k // tk
    a_spec = pl.BlockSpec((tk, tm), lambda i, j, l: (l, i)) if ta else pl.BlockSpec((tm, tk), lambda i, j, l: (i, l))
    b_spec = pl.BlockSpec((tn, tk), lambda i, j, l: (j, l)) if tb else pl.BlockSpec((tk, tn), lambda i, j, l: (l, j))
    dims = (((0,) if ta else (1,), (1,) if tb else (0,)), ((), ()))

    def body(a_ref, b_ref, o_ref, *acc):
        p = lax.dot_general(a_ref[...].astype(BF16), b_ref[...].astype(BF16), dims, preferred_element_type=F32)
        if nk == 1:
            o_ref[...] = p.astype(out_dtype)
            return
        acc_ref = acc[0]
        step = pl.program_id(2)

        @pl.when(step == 0)
        def _():
            acc_ref[...] = p

        @pl.when(step > 0)
        def _():
            acc_ref[...] += p

        @pl.when(step == nk - 1)
        def _():
            o_ref[...] = acc_ref[...].astype(out_dtype)

    outs, rode = _ride_call(
        body, ride, name=name, grid=(m // tm, n // tn, nk), in_specs=[a_spec, b_spec],
        out_specs=[pl.BlockSpec((tm, tn), lambda i, j, l: (i, j))], out_shape=[jax.ShapeDtypeStruct((m, n), out_dtype)],
        scratch_shapes=[] if nk == 1 else [pltpu.VMEM((tm, tn), F32)], sem=("parallel", "parallel", "arbitrary"),
        args=(a, b))
    return outs[0] if ride is None else (outs[0], rode)


def _mm_pieces_nt(pieces, w, *, name, tm=512, ride=None):
    s, n = pieces[0].shape[0], w.shape[0]
    tm = min(tm, s)
    widths = [p.shape[1] for p in pieces]
    offs = [sum(widths[:i]) for i in range(len(pieces))]

    def body(*refs):
        p_refs, w_ref, o_ref = refs[:len(pieces)], refs[len(pieces)], refs[len(pieces) + 1]
        acc = None
        for p_ref, off, width in zip(p_refs, offs, widths):
            part = _dot_nt(p_ref[...], w_ref[:, off:off + width])
            acc = part if acc is None else acc + part
        o_ref[...] = acc

    outs, rode = _ride_call(
        body, ride, name=name, grid=(s // tm,),
        in_specs=[pl.BlockSpec((tm, width), lambda i: (i, 0)) for width in widths]
        + [pl.BlockSpec(w.shape, lambda i: (0, 0))],
        out_specs=[pl.BlockSpec((tm, n), lambda i: (i, 0))], out_shape=[jax.ShapeDtypeStruct((s, n), F32)],
        scratch_shapes=[], sem=("parallel",), args=(*pieces, w))
    return outs[0] if ride is None else (outs[0], rode)


def _mm_pieces_tn(a, pieces, *, name, tk=1024):
    s, m = a.shape
    tk = min(tk, s)
    nk = s // tk
    n = len(pieces)
    dims = (((0,), (0,)), ((), ()))

    def body(*refs):
        a_ref, p_refs, o_refs, accs = refs[0], refs[1:1 + n], refs[1 + n:1 + 2 * n], refs[1 + 2 * n:]
        step = pl.program_id(0)
        a_blk = a_ref[...]
        for p_ref, o_ref, acc in zip(p_refs, o_refs, accs):
            part = lax.dot_general(a_blk, p_ref[...], dims, preferred_element_type=F32)

            @pl.when(step == 0)
            def _(acc=acc, part=part):
                acc[...] = part

            @pl.when(step > 0)
            def _(acc=acc, part=part):
                acc[...] += part

            @pl.when(step == nk - 1)
            def _(acc=acc, o_ref=o_ref):
                o_ref[...] = acc[...].astype(BF16)

    return pl.pallas_call(
        body, name=name, grid=(nk,),
        in_specs=[pl.BlockSpec((tk, m), lambda l: (l, 0))] + [pl.BlockSpec((tk, p.shape[1]), lambda l: (l, 0)) for p in pieces],
        out_specs=[pl.BlockSpec((m, p.shape[1]), lambda l: (0, 0)) for p in pieces],
        out_shape=[jax.ShapeDtypeStruct((m, p.shape[1]), BF16) for p in pieces],
        scratch_shapes=[pltpu.VMEM((m, p.shape[1]), F32) for p in pieces],
        compiler_params=_params(("arbitrary",)),
    )(a, *pieces)


def _rowwise(name, fn, rows, vecs, outs, sums=(), tr=512):
    s = rows[0][0].shape[0]
    tr = min(tr, s)
    nr, nv, no = len(rows), len(vecs), len(outs)

    def body(*refs):
        vals = [r[...] for r in refs[:nr + nv]]
        res = fn(*vals)
        for o_ref, val in zip(refs[nr + nv:nr + nv + no], res[:no]):
            o_ref[...] = val.astype(o_ref.dtype)
        step = pl.program_id(0)
        for s_ref, val in zip(refs[nr + nv + no:], res[no:]):
            @pl.when(step == 0)
            def _(s_ref=s_ref, val=val):
                s_ref[...] = val

            @pl.when(step > 0)
            def _(s_ref=s_ref, val=val):
                s_ref[...] += val

    in_specs = [pl.BlockSpec((tr, w), functools.partial(lambda i, cb: (i, cb), cb=cb)) for _, w, cb in rows]
    in_specs += [pl.BlockSpec(v.shape, lambda i: (0, 0)) for v in vecs]
    out_specs = [pl.BlockSpec((tr, w), lambda i: (i, 0)) for w, _ in outs]
    out_specs += [pl.BlockSpec((1, w), lambda i: (0, 0)) for w in sums]
    out_shape = [jax.ShapeDtypeStruct((s, w), dt) for w, dt in outs]
    out_shape += [jax.ShapeDtypeStruct((1, w), F32) for w in sums]
    return pl.pallas_call(
        body, name=name, grid=(s // tr,), in_specs=in_specs, out_specs=out_specs, out_shape=out_shape,
        compiler_params=_params(("arbitrary",) if sums else ("parallel",)),
    )(*[r[0] for r in rows], *vecs)


def _colsum(x):
    return jnp.sum(x, axis=0, keepdims=True)


def _sigmoid(x):
    return 1.0 / (1.0 + jnp.exp(-x))


def _log_sigmoid(x):
    return jnp.minimum(x, 0.0) - jnp.log(1.0 + jnp.exp(-jnp.abs(x)))


def _ln_stats(r):
    mu = jnp.mean(r, axis=-1, keepdims=True)
    xc = r - mu
    var = jnp.mean(xc * xc, axis=-1, keepdims=True)
    rstd = lax.rsqrt(var + LN_EPS)
    return xc * rstd, rstd


def _ln_bwd(dy, xhat, rstd, g):
    dxh = dy * g
    m1 = jnp.mean(dxh, axis=-1, keepdims=True)
    m2 = jnp.mean(dxh * xhat, axis=-1, keepdims=True)
    return rstd * (dxh - m1 - xhat * m2)


def _mod_exchange(c_row, w_ada, b_ada_loc):
    def body(c_ref, w_ref, b_ref, call_ref, mod_ref, piece_ref, send_sems, recv_sems):
        me = _me()[3]
        call_ref[me] = c_ref[...]
        sent = []
        for r in range(1, N_DEV):
            peer, _ = _peer(r)
            cp = pltpu.make_async_remote_copy(
                src_ref=c_ref, dst_ref=call_ref.at[me], send_sem=send_sems.at[0, r - 1],
                recv_sem=recv_sems.at[0, r - 1], device_id=peer, device_id_type=MESH)
            cp.start()
            sent.append(cp)
        for r in range(1, N_DEV):
            peer, pidx = _peer(r)
            pltpu.make_async_remote_copy(
                src_ref=c_ref, dst_ref=call_ref.at[pidx], send_sem=send_sems.at[0, r - 1],
                recv_sem=recv_sems.at[0, r - 1], device_id=peer, device_id_type=MESH).wait_recv()
        c_all = jnp.concatenate([call_ref[d] for d in range(N_DEV)], axis=0)
        mod_loc = jnp.dot(c_all, w_ref[...], preferred_element_type=F32,
                          precision=lax.Precision.HIGHEST) + b_ref[...]
        for d in range(N_DEV):
            piece_ref[d] = mod_loc[d:d + 1, :]
        mod_ref[me] = piece_ref[me]
        for r in range(1, N_DEV):
            peer, pidx = _peer(r)
            cp = pltpu.make_async_remote_copy(
                src_ref=piece_ref.at[pidx], dst_ref=mod_ref.at[me], send_sem=send_sems.at[1, r - 1],
                recv_sem=recv_sems.at[1, r - 1], device_id=peer, device_id_type=MESH)
            cp.start()
            sent.append(cp)
        for r in range(1, N_DEV):
            peer, pidx = _peer(r)
            pltpu.make_async_remote_copy(
                src_ref=piece_ref.at[me], dst_ref=mod_ref.at[pidx], send_sem=send_sems.at[1, r - 1],
                recv_sem=recv_sems.at[1, r - 1], device_id=peer, device_id_type=MESH).wait_recv()
        for cp in sent:
            cp.wait_send()

    vmem = pl.BlockSpec(memory_space=pltpu.VMEM)
    return pl.pallas_call(
        body, name="mod_exchange", in_specs=[vmem, vmem, vmem], out_specs=[vmem, vmem],
        out_shape=[jax.ShapeDtypeStruct((N_DEV, 1, D_MODEL), F32), jax.ShapeDtypeStruct((N_DEV, 1, ADA_SHARD), F32)],
        scratch_shapes=[pltpu.VMEM((N_DEV, 1, ADA_SHARD), F32),
                        pltpu.SemaphoreType.DMA((2, N_DEV - 1)), pltpu.SemaphoreType.DMA((2, N_DEV - 1))],
        compiler_params=_params(),
    )(c_row, w_ada, b_ada_loc)


def _split3(x):
    hi = x.astype(BF16)
    r1 = x - hi.astype(F32)
    mid = r1.astype(BF16)
    lo = (r1 - mid.astype(F32)).astype(BF16)
    return hi, mid, lo


def _scan_rows(x_ref, o_ref, s, reverse, pre=None, post=None):
    tb = min(TQ, s)
    nb = s // tb
    row = lax.broadcasted_iota(jnp.int32, (tb, tb), 0)
    col = lax.broadcasted_iota(jnp.int32, (tb, tb), 1)
    tri = jnp.where((col >= row) if reverse else (col <= row), 1.0, 0.0).astype(BF16)

    def step(i, carry):
        blk = (nb - 1 - i) if reverse else i
        off = pl.multiple_of(blk * tb, tb)
        x = x_ref[pl.ds(off, tb), :]
        if pre is not None:
            x = pre(x, off)
        acc = carry
        for piece in _split3(x):
            acc = acc + jnp.dot(tri, piece, preferred_element_type=F32)
        o_ref[pl.ds(off, tb), :] = acc if post is None else post(acc, off)
        edge = acc[0:1, :] if reverse else acc[tb - 1:tb, :]
        return jnp.broadcast_to(edge, (tb, LANES))

    lax.fori_loop(0, nb, step, jnp.zeros((tb, LANES), F32))


def _forget_cumsum(f_raw, b_pad):
    s = f_raw.shape[0]

    def body(f_ref, b_ref, cum_ref):
        b = b_ref[...]
        _scan_rows(f_ref, cum_ref, s, False, pre=lambda x, off: _log_sigmoid(x + b))

    vmem = pl.BlockSpec(memory_space=pltpu.VMEM)
    return pl.pallas_call(body, name="forget_cumsum", in_specs=[vmem, vmem], out_specs=vmem,
                          out_shape=jax.ShapeDtypeStruct((s, LANES), F32), compiler_params=_params())(f_raw, b_pad)


def _forget_bwd(dcum, f_raw, b_pad):
    s = f_raw.shape[0]

    def body(d_ref, f_ref, b_ref, df_ref, db_ref, tmp_ref):
        b = b_ref[...]
        _scan_rows(d_ref, tmp_ref, s, True)
        df = tmp_ref[...] * _sigmoid(-(f_ref[...] + b))
        df_ref[...] = df.astype(BF16)
        db_ref[...] = _colsum(df)

    vmem = pl.BlockSpec(memory_space=pltpu.VMEM)
    return pl.pallas_call(
        body, name="forget_bwd", in_specs=[vmem, vmem, vmem], out_specs=[vmem, vmem],
        out_shape=[jax.ShapeDtypeStruct((s, LANES), BF16), jax.ShapeDtypeStruct((1, LANES), F32)],
        scratch_shapes=[pltpu.VMEM((s, LANES), F32)], compiler_params=_params())(dcum, f_raw, b_pad)


def _dot_nt(a, b):
    return lax.dot_general(a, b, (((1,), (1,)), ((), ())), preferred_element_type=F32)


def _head_masks():
    lane = lax.broadcasted_iota(jnp.int32, (TQ, LANES), 1)
    return lane, [lane < HEAD_DIM, lane >= HEAD_DIM]


def _pick(mask, x):
    return jnp.where(mask, x, jnp.zeros_like(x))


def _qkv_specs(s, col0):
    nb = ATTN_W // LANES
    return [pl.BlockSpec((TQ, LANES), lambda hp, qi: (qi, col0 + hp)),
            pl.BlockSpec((s, LANES), lambda hp, qi: (0, col0 + nb + hp)),
            pl.BlockSpec((s, LANES), lambda hp, qi: (0, col0 + 2 * nb + hp))]


def _pair_spec():
    return pl.BlockSpec((TQ, LANES), lambda hp, qi: (qi, hp))


def _diag_mask(width, strict):
    row = lax.broadcasted_iota(jnp.int32, (TQ, width * TQ), 0) + (width - 1) * TQ
    col = lax.broadcasted_iota(jnp.int32, (TQ, width * TQ), 1)
    return (col < row) if strict else (col <= row)


def _fox_fwd(qkv, cum_col, cum_row, ride=None):
    s = qkv.shape[0]
    nq = s // TQ

    def body(q_ref, k_ref, v_ref, cc_ref, cr_ref, o_ref, o32_ref, lse_ref):
        hp, qi = pl.program_id(0), pl.program_id(1)
        lane, masks = _head_masks()
        q2 = q_ref[...] * jnp.asarray(ATTN_SCALE, BF16)
        cc = cc_ref[...]
        qms = [_pick(masks[e], q2) for e in range(2)]
        cqs = [jnp.sum(jnp.where(lane == 2 * hp + e, cc, 0.0), axis=1, keepdims=True) for e in range(2)]

        def tile(kb, carry, masked, width=1):
            off, span = pl.multiple_of(kb * TQ, TQ), width * TQ
            k2, v2 = k_ref[pl.ds(off, span), :], v_ref[pl.ds(off, span), :]
            head0 = lax.broadcasted_iota(jnp.int32, (span, LANES), 1) < HEAD_DIM
            new = []
            for e in range(2):
                m, acc = carry[e]
                sc = _dot_nt(qms[e], k2) + (cqs[e] - cr_ref[e:e + 1, pl.ds(off, span)])
                if masked:
                    sc = jnp.where(_diag_mask(width, False), sc, -jnp.inf)
                m_new = jnp.maximum(m, jnp.max(sc, axis=1, keepdims=True))
                p = jnp.exp(sc - m_new)
                v_ones = jnp.where(head0 if e == 0 else ~head0, v2, jnp.ones_like(v2))
                acc = jnp.exp(m - m_new) * acc + jnp.dot(p.astype(BF16), v_ones, preferred_element_type=F32)
                new.append((m_new, acc))
            return tuple(new)

        init = (jnp.full((TQ, 1), -jnp.inf, F32), jnp.zeros((TQ, LANES), F32))
        carry = lax.cond(qi % 2 == 1, lambda cr: tile(qi - 1, cr, True, 2), lambda cr: tile(qi, cr, True), (init, init))
        quads = qi // 4
        carry = lax.fori_loop(0, quads, lambda j, cr: tile(4 * j, cr, False, 4), carry)
        carry = lax.cond(qi % 4 >= 2, lambda cr: tile(4 * quads, cr, False, 2), lambda cr: cr, carry)
        sums = [jnp.max(jnp.where(masks[1 - e], carry[e][1], 0.0), axis=1, keepdims=True) for e in range(2)]
        outs = [carry[e][1] / sums[e] for e in range(2)]
        lses = [carry[e][0] + jnp.log(sums[e]) for e in range(2)]
        out = jnp.where(masks[0], outs[0], outs[1])
        o_ref[...] = out.astype(BF16)
        o32_ref[...] = out
        lse_ref[...] = jnp.where(masks[0], lses[0], lses[1])

    return _ride_call(
        body, ride, name="fox_fwd", grid=(N_HEADS // 2, nq),
        in_specs=_qkv_specs(s, 0) + [pl.BlockSpec((TQ, LANES), lambda hp, qi: (qi, 0)),
                                     pl.BlockSpec((None, 2, s), lambda hp, qi: (hp, 0, 0))],
        out_specs=[_pair_spec(), _pair_spec(), _pair_spec()],
        out_shape=[jax.ShapeDtypeStruct((s, ATTN_W), BF16), jax.ShapeDtypeStruct((s, ATTN_W), F32),
                   jax.ShapeDtypeStruct((s, ATTN_W), F32)],
        scratch_shapes=[], sem=("parallel", "parallel"), args=(qkv, qkv, qkv, cum_col, cum_row))


def _write_transposed(acc_ref, out_ref):
    for c in range(out_ref.shape[0] // TQ):
        out_ref[c * TQ:(c + 1) * TQ, :] = jnp.transpose(acc_ref[:, c * TQ:(c + 1) * TQ]).astype(BF16)


def _fox_bwd(qkv, cum_col, cum_row, o, lse, do, ride=None):
    s = qkv.shape[0]
    nq = s // TQ

    def body(q_ref, k_ref, v_ref, cc_ref, cr_ref, o_ref, lse_ref, do_ref,
             dq_ref, dk_ref, dv_ref, dcr_ref, dk_acc, dv_acc):
        hp, qi = pl.program_id(0), pl.program_id(1)

        @pl.when(qi == 0)
        def _():
            dk_acc[...] = jnp.zeros_like(dk_acc)
            dv_acc[...] = jnp.zeros_like(dv_acc)
            dcr_ref[...] = jnp.zeros_like(dcr_ref)

        lane, masks = _head_masks()
        q2 = q_ref[...] * jnp.asarray(ATTN_SCALE, BF16)
        do2 = do_ref[...]
        prod = do2.astype(F32) * o_ref[...].astype(F32)
        lse2 = lse_ref[...]
        cc = cc_ref[...]
        qms = [_pick(masks[e], q2) for e in range(2)]
        doms = [_pick(masks[e], do2) for e in range(2)]
        deltas = [jnp.sum(jnp.where(masks[e], prod, 0.0), axis=1, keepdims=True) for e in range(2)]
        lses = [jnp.max(jnp.where(masks[e], lse2, -jnp.inf), axis=1, keepdims=True) for e in range(2)]
        cqs = [jnp.sum(jnp.where(lane == 2 * hp + e, cc, 0.0), axis=1, keepdims=True) for e in range(2)]
        qts = [jnp.transpose(qms[e].astype(F32)).astype(BF16) for e in range(2)]
        dots = [jnp.transpose(doms[e].astype(F32)).astype(BF16) for e in range(2)]
        ones_row = [HEAD_DIM * (1 - e) for e in range(2)]
        trow = lax.broadcasted_iota(jnp.int32, (LANES, TQ), 0)
        qts = [jnp.where(trow == ones_row[e], jnp.ones_like(qts[e]), qts[e]) for e in range(2)]

        def tile(kb, carry, masked, width=1):
            off, span = pl.multiple_of(kb * TQ, TQ), width * TQ
            k2, v2 = k_ref[pl.ds(off, span), :], v_ref[pl.ds(off, span), :]
            head0 = lax.broadcasted_iota(jnp.int32, (span, LANES), 1) < HEAD_DIM
            new, dks, dv = [], [], None
            for e in range(2):
                dq = carry[e]
                sc = _dot_nt(qms[e], k2) + (cqs[e] - cr_ref[e:e + 1, pl.ds(off, span)])
                p = jnp.exp(sc - lses[e])
                if masked:
                    p = jnp.where(_diag_mask(width, False), p, 0.0)
                ds = p * (_dot_nt(doms[e], v2) - deltas[e])
                dsb = ds.astype(BF16)
                dk_e = jnp.dot(qts[e], dsb, preferred_element_type=F32)
                dv_e = jnp.dot(dots[e], p.astype(BF16), preferred_element_type=F32)
                dks.append(dk_e)
                dv = dv_e if e == 0 else dv + dv_e
                dcr_ref[e:e + 1, pl.ds(off, span)] -= dk_e[ones_row[e]:ones_row[e] + 1, :]
                k_ones = jnp.where(head0 if e == 0 else ~head0, k2, jnp.ones_like(k2))
                new.append(dq + jnp.dot(dsb, k_ones, preferred_element_type=F32))
            krow = lax.broadcasted_iota(jnp.int32, (LANES, span), 0)
            dk_acc[:, pl.ds(off, span)] += jnp.where(krow < HEAD_DIM, dks[0], dks[1])
            dv_acc[:, pl.ds(off, span)] += dv
            return tuple(new)

        init = jnp.zeros((TQ, LANES), F32)
        carry = lax.fori_loop(0, qi // 2, lambda j, cr: tile(2 * j, cr, False, 2), (init, init))
        carry = lax.cond(qi % 2 == 1, lambda cr: tile(qi - 1, cr, True, 2), lambda cr: tile(qi, cr, True), carry)
        dq_ref[...] = (jnp.where(masks[0], carry[0], carry[1]) * ATTN_SCALE).astype(BF16)
        for e in range(2):
            dcr_ref[e:e + 1, pl.ds(pl.multiple_of(qi * TQ, TQ), TQ)] += jnp.transpose(carry[e])[
                ones_row[e]:ones_row[e] + 1, :]

        @pl.when(qi == nq - 1)
        def _():
            _write_transposed(dk_acc, dk_ref)
            _write_transposed(dv_acc, dv_ref)

    seq_spec = pl.BlockSpec((s, LANES), lambda hp, qi: (0, hp))
    return _ride_call(
        body, ride, name="fox_bwd", grid=(N_HEADS // 2, nq),
        in_specs=_qkv_specs(s, 0) + [pl.BlockSpec((TQ, LANES), lambda hp, qi: (qi, 0)),
                                     pl.BlockSpec((None, 2, s), lambda hp, qi: (hp, 0, 0)),
                                     _pair_spec(), _pair_spec(), _pair_spec()],
        out_specs=[_pair_spec(), seq_spec, seq_spec, pl.BlockSpec((None, 2, s), lambda hp, qi: (hp, 0, 0))],
        out_shape=[jax.ShapeDtypeStruct((s, ATTN_W), BF16)] * 3 + [jax.ShapeDtypeStruct((N_HEADS // 2, 2, s), F32)],
        scratch_shapes=[pltpu.VMEM((LANES, s), F32), pltpu.VMEM((LANES, s), F32)],
        sem=("parallel", "arbitrary"), args=(qkv, qkv, qkv, cum_col, cum_row, o, lse, do))


def _scan_matrix(reverse):
    row = lax.broadcasted_iota(jnp.int32, (SCAN_W, SCAN_W), 0)
    col = lax.broadcasted_iota(jnp.int32, (SCAN_W, SCAN_W), 1)
    return jnp.where((row > col) if reverse else (row < col), 1.0, 0.0).astype(BF16)


def _scan_cols(x, tri, reverse, init):
    nblk = x.shape[1] // SCAN_W
    parts, total = [None] * nblk, init
    far = 0 if reverse else SCAN_W - 1
    for b in (reversed(range(nblk)) if reverse else range(nblk)):
        blk = x[:, b * SCAN_W:(b + 1) * SCAN_W]
        part = jnp.dot(blk.astype(BF16), tri, preferred_element_type=F32)
        parts[b] = part + total
        total = total + (part[:, far:far + 1] + blk[:, far:far + 1])
    return (parts[0] if nblk == 1 else jnp.concatenate(parts, axis=1)), total


def _sb_logits(qm, k2):
    z = _dot_nt(qm, k2)
    neg_abs = lax.bitcast_convert_type(lax.bitcast_convert_type(z, jnp.uint32) | jnp.uint32(0x80000000), F32)
    soft = jnp.log(1.0 + jnp.exp(neg_abs))
    lb = jnp.minimum(z, 0.0) - soft
    return lb, lb - z


TILE_SLOTS = 4


def _tri_base(qi):
    return (qi * (qi + 1)) // 2


def _sb_fwd(qkv):
    s = qkv.shape[0]
    nq = s // TQ

    def body(q_ref, k_ref, v_ref, o_ref, t_ref, buf, sems):
        hp, qi = pl.program_id(0), pl.program_id(1)
        _, masks = _head_masks()
        suffix = _scan_matrix(True)
        q2 = q_ref[...] * jnp.asarray(ATTN_SCALE, BF16)
        qms = [_pick(masks[e], q2) for e in range(2)]
        base = _tri_base(qi)

        def store(e, kb):
            slot = kb % TILE_SLOTS
            return pltpu.make_async_copy(buf.at[e, slot], t_ref.at[2 * hp + e, base + kb], sems.at[e, slot])

        def tile(kb, carry, masked, width=1):
            off, span = pl.multiple_of(kb * TQ, TQ), width * TQ
            k2, v2 = k_ref[pl.ds(off, span), :], v_ref[pl.ds(off, span), :]
            new = []
            for e in range(2):
                run, acc = carry[e]
                lb, lo = _sb_logits(qms[e], k2)
                if masked:
                    strict = _diag_mask(width, True)
                    lo = jnp.where(strict, lo, 0.0)
                rest, run = _scan_cols(lo, suffix, True, run)
                a = jnp.exp(lb + rest)
                if masked:
                    a = jnp.where(strict, a, 0.0)
                ab, lbb = a.astype(BF16), lb.astype(BF16)
                acc = acc + jnp.dot(ab, v2, preferred_element_type=F32)
                for w in range(width):
                    blk = kb + w

                    @pl.when(blk + TILE_SLOTS <= qi)
                    def _(e=e, blk=blk):
                        store(e, blk + TILE_SLOTS).wait()
                    buf[e, blk % TILE_SLOTS, 0] = ab[:, w * TQ:(w + 1) * TQ]
                    buf[e, blk % TILE_SLOTS, 1] = lbb[:, w * TQ:(w + 1) * TQ]
                    store(e, blk).start()
                new.append((run, acc))
            return tuple(new)

        init = (jnp.zeros((TQ, 1), F32), jnp.zeros((TQ, LANES), F32))
        carry = lax.cond(qi % 2 == 1, lambda cr: tile(qi - 1, cr, True, 2), lambda cr: tile(qi, cr, True), (init, init))
        pairs = qi // 2
        carry = lax.fori_loop(0, pairs, lambda it, cr: tile(2 * (pairs - 1 - it), cr, False, 2), carry)
        for e in range(2):
            for blk in range(TILE_SLOTS):
                @pl.when(qi >= blk)
                def _(e=e, blk=blk):
                    store(e, blk).wait()
        o_ref[...] = jnp.where(masks[0], carry[0][1], carry[1][1]).astype(BF16)

    ntri = nq * (nq + 1) // 2
    return pl.pallas_call(
        body, name="sb_fwd", grid=(N_HEADS // 2, nq), in_specs=_qkv_specs(s, 3 * ATTN_W // LANES),
        out_specs=[_pair_spec(), ANY],
        out_shape=[jax.ShapeDtypeStruct((s, ATTN_W), BF16), jax.ShapeDtypeStruct((N_HEADS, ntri, 2, TQ, TQ), BF16)],
        scratch_shapes=[pltpu.VMEM((2, TILE_SLOTS, 2, TQ, TQ), BF16), pltpu.SemaphoreType.DMA((2, TILE_SLOTS))],
        compiler_params=_params(("arbitrary", "arbitrary")),
    )(qkv, qkv, qkv)


def _sb_bwd(qkv, tiles, do):
    s = qkv.shape[0]
    nq = s // TQ

    def body(q_ref, k_ref, v_ref, t_ref, do_ref, dq_ref, dk_ref, dv_ref, dk_acc, dv_acc, buf, sems):
        hp, qi = pl.program_id(0), pl.program_id(1)

        @pl.when(qi == 0)
        def _():
            dk_acc[...] = jnp.zeros_like(dk_acc)
            dv_acc[...] = jnp.zeros_like(dv_acc)

        _, masks = _head_masks()
        prefix = _scan_matrix(False)
        q2 = q_ref[...] * jnp.asarray(ATTN_SCALE, BF16)
        do2 = do_ref[...]
        qms = [_pick(masks[e], q2) for e in range(2)]
        doms = [_pick(masks[e], do2) for e in range(2)]
        qts = [jnp.transpose(qms[e].astype(F32)).astype(BF16) for e in range(2)]
        dots = [jnp.transpose(doms[e].astype(F32)).astype(BF16) for e in range(2)]
        base = _tri_base(qi)

        def fetch(e, kb):
            slot = kb % TILE_SLOTS
            return pltpu.make_async_copy(t_ref.at[2 * hp + e, base + kb], buf.at[e, slot], sems.at[e, slot])

        for e in range(2):
            fetch(e, 0).start()

            @pl.when(qi >= 1)
            def _(e=e):
                fetch(e, 1).start()

        def tile(kb, carry, masked, width=1):
            off, span = pl.multiple_of(kb * TQ, TQ), width * TQ
            k2, v2 = k_ref[pl.ds(off, span), :], v_ref[pl.ds(off, span), :]
            new, dk, dv = [], None, None
            for e in range(2):
                gsum, dq = carry[e]
                if not masked:
                    for blk in range(2, 2 + width):
                        @pl.when(kb + blk <= qi)
                        def _(e=e, blk=blk):
                            fetch(e, kb + blk).start()
                for w in range(width):
                    fetch(e, kb + w).wait()
                slots = [(kb + w) % TILE_SLOTS for w in range(width)]
                ab = buf[e, slots[0], 0] if width == 1 else jnp.concatenate([buf[e, sl, 0] for sl in slots], axis=1)
                lbb = buf[e, slots[0], 1] if width == 1 else jnp.concatenate([buf[e, sl, 1] for sl in slots], axis=1)
                beta = jnp.exp(lbb.astype(F32))
                g = ab.astype(F32) * _dot_nt(doms[e], v2)
                before, gsum = _scan_cols(g, prefix, False, gsum)
                dz = g - beta * (g + before)
                if masked:
                    dz = jnp.where(_diag_mask(width, True), dz, 0.0)
                dzb = dz.astype(BF16)
                dk_e = jnp.dot(qts[e], dzb, preferred_element_type=F32)
                dv_e = jnp.dot(dots[e], ab, preferred_element_type=F32)
                dk, dv = (dk_e, dv_e) if e == 0 else (dk + dk_e, dv + dv_e)
                new.append((gsum, dq + jnp.dot(dzb, k2, preferred_element_type=F32)))
            dk_acc[:, pl.ds(off, span)] += dk
            dv_acc[:, pl.ds(off, span)] += dv
            return tuple(new)

        init = (jnp.zeros((TQ, 1), F32), jnp.zeros((TQ, LANES), F32))
        carry = lax.fori_loop(0, qi // 2, lambda j, cr: tile(2 * j, cr, False, 2), (init, init))
        carry = lax.cond(qi % 2 == 1, lambda cr: tile(qi - 1, cr, True, 2), lambda cr: tile(qi, cr, True), carry)
        dq_ref[...] = (jnp.where(masks[0], carry[0][1], carry[1][1]) * ATTN_SCALE).astype(BF16)

        @pl.when(qi == nq - 1)
        def _():
            _write_transposed(dk_acc, dk_ref)
            _write_transposed(dv_acc, dv_ref)

    seq_spec = pl.BlockSpec((s, LANES), lambda hp, qi: (0, hp))
    return pl.pallas_call(
        body, name="sb_bwd", grid=(N_HEADS // 2, nq),
        in_specs=_qkv_specs(s, 3 * ATTN_W // LANES) + [ANY, _pair_spec()],
        out_specs=[_pair_spec(), seq_spec, seq_spec],
        out_shape=[jax.ShapeDtypeStruct((s, ATTN_W), BF16)] * 3,
        scratch_shapes=[pltpu.VMEM((LANES, s), F32), pltpu.VMEM((LANES, s), F32),
                        pltpu.VMEM((2, TILE_SLOTS, 2, TQ, TQ), BF16), pltpu.SemaphoreType.DMA((2, TILE_SLOTS))],
        compiler_params=_params(("arbitrary", "arbitrary")),
    )(qkv, qkv, qkv, tiles, do)


CONV_TR = 512


def _shift_down(x, halo, n):
    rolled = pltpu.roll(x, n, 0)
    head = rolled[0:8, :]
    rid = lax.broadcasted_iota(jnp.int32, head.shape, 0)
    for j in range(n):
        head = jnp.where(rid == j, halo[8 - n + j:8 - n + j + 1, :], head)
    return jnp.concatenate([head, rolled[8:, :]], axis=0)


def _shift_up(x, halo, n):
    rows = x.shape[0]
    rolled = pltpu.roll(x, rows - n, 0)
    tail = rolled[rows - 8:, :]
    rid = lax.broadcasted_iota(jnp.int32, tail.shape, 0)
    for j in range(n):
        tail = jnp.where(rid == 8 - n + j, halo[j:j + 1, :], tail)
    return jnp.concatenate([rolled[:rows - 8, :], tail], axis=0)


def _conv_fwd_block(x, halo, w, b):
    return b + _shift_down(x, halo, 2) * w[0:1, :] + _shift_down(x, halo, 1) * w[1:2, :] + x * w[2:3, :]


def _conv_specs(tr, s):
    pair = 2 * FF_HALF
    blk = pl.BlockSpec((tr, pair), lambda j, i: (i, j))
    prev = pl.BlockSpec((8, pair), lambda j, i: (jnp.maximum(i * (tr // 8) - 1, 0), j))
    nxt = pl.BlockSpec((8, pair), lambda j, i: (jnp.minimum((i + 1) * (tr // 8), s // 8 - 1), j))
    return blk, prev, nxt


def _conv_gate_fwd(hpre, conv_w, conv_b):
    s = hpre.shape[0]
    tr = min(CONV_TR, s)
    blk, prev, _ = _conv_specs(tr, s)

    def body(x_ref, halo_ref, w_ref, b_ref, a_ref):
        i = pl.program_id(1)
        halo = jnp.where(i > 0, halo_ref[...], 0.0)
        h = _conv_fwd_block(x_ref[...], halo, w_ref[...], b_ref[...])
        hg, hv = h[:, :FF_HALF], h[:, FF_HALF:]
        a_ref[...] = (hg * _sigmoid(hg) * hv).astype(BF16)

    return pl.pallas_call(
        body, name="conv_gate_fwd", grid=(2, s // tr),
        in_specs=[blk, prev, pl.BlockSpec((3, 2 * FF_HALF), lambda j, i: (0, j)),
                  pl.BlockSpec((1, 2 * FF_HALF), lambda j, i: (0, j))],
        out_specs=pl.BlockSpec((tr, FF_HALF), lambda j, i: (i, j)),
        out_shape=jax.ShapeDtypeStruct((s, D_FF), BF16),
        compiler_params=_params(("parallel", "parallel")),
    )(hpre, hpre, conv_w, conv_b)


def _conv_gate_bwd(hpre, da, conv_w, conv_b):
    s = hpre.shape[0]
    tr = min(CONV_TR, s)
    blk, prev, _ = _conv_specs(tr, s)

    def body(x_ref, halo_ref, da_ref, w_ref, b_ref, dh_ref, db_ref, dw_ref):
        i = pl.program_id(1)
        halo = jnp.where(i > 0, halo_ref[...], 0.0)
        x = x_ref[...]
        h = _conv_fwd_block(x, halo, w_ref[...], b_ref[...])
        hg, hv = h[:, :FF_HALF], h[:, FF_HALF:]
        da_blk = da_ref[...].astype(F32)
        sg = _sigmoid(hg)
        dhg = da_blk * hv * (sg * (1.0 + hg * (1.0 - sg)))
        dhv = da_blk * (hg * sg)
        dh_ref[:, :FF_HALF] = dhg.astype(BF16)
        dh_ref[:, FF_HALF:] = dhv.astype(BF16)
        x2, x1 = _shift_down(x, halo, 2), _shift_down(x, halo, 1)
        parts = []
        for lo, dpart in ((0, dhg), (FF_HALF, dhv)):
            cols = slice(lo, lo + FF_HALF)
            parts.append((cols, _colsum(dpart), _colsum(dpart * x2[:, cols]), _colsum(dpart * x1[:, cols]),
                          _colsum(dpart * x[:, cols])))

        @pl.when(i == 0)
        def _():
            for cols, db, dw0, dw1, dw2 in parts:
                db_ref[:, cols] = db
                dw_ref[0:1, cols] = dw0
                dw_ref[1:2, cols] = dw1
                dw_ref[2:3, cols] = dw2

        @pl.when(i > 0)
        def _():
            for cols, db, dw0, dw1, dw2 in parts:
                db_ref[:, cols] += db
                dw_ref[0:1, cols] += dw0
                dw_ref[1:2, cols] += dw1
                dw_ref[2:3, cols] += dw2

    pair = 2 * FF_HALF
    return pl.pallas_call(
        body, name="conv_gate_bwd", grid=(2, s // tr),
        in_specs=[blk, prev, pl.BlockSpec((tr, FF_HALF), lambda j, i: (i, j)),
                  pl.BlockSpec((3, pair), lambda j, i: (0, j)), pl.BlockSpec((1, pair), lambda j, i: (0, j))],
        out_specs=[blk, pl.BlockSpec((1, pair), lambda j, i: (0, j)), pl.BlockSpec((3, pair), lambda j, i: (0, j))],
        out_shape=[jax.ShapeDtypeStruct((s, 2 * D_FF), BF16), jax.ShapeDtypeStruct((1, 2 * D_FF), F32),
                   jax.ShapeDtypeStruct((3, 2 * D_FF), F32)],
        compiler_params=_params(("parallel", "arbitrary")),
    )(hpre, hpre, da, conv_w, conv_b)


def _conv_input_bwd(dh, conv_w):
    s = dh.shape[0]
    tr = min(CONV_TR, s)
    blk, _, _ = _conv_specs(tr, s)
    nblk = s // tr

    def body(x_ref, halo_ref, w_ref, o_ref):
        i = pl.program_id(1)
        halo = jnp.where(i < nblk - 1, halo_ref[...].astype(F32), 0.0)
        x, w = x_ref[...].astype(F32), w_ref[...]
        o_ref[...] = (x * w[2:3, :] + _shift_up(x, halo, 1) * w[1:2, :] + _shift_up(x, halo, 2) * w[0:1, :]).astype(BF16)

    nxt = pl.BlockSpec((16, 2 * FF_HALF), lambda j, i: (jnp.minimum((i + 1) * (tr // 16), s // 16 - 1), j))
    return pl.pallas_call(
        body, name="conv_input_bwd", grid=(2, nblk),
        in_specs=[blk, nxt, pl.BlockSpec((3, 2 * FF_HALF), lambda j, i: (0, j))], out_specs=blk,
        out_shape=jax.ShapeDtypeStruct((s, 2 * D_FF), BF16),
        compiler_params=_params(("parallel", "parallel")),
    )(dh, dh, conv_w)


def _adamw_math(w, g, m, v):
    m = ADAM_B1 * m + (1.0 - ADAM_B1) * g
    v = ADAM_B2 * v + (1.0 - ADAM_B2) * (g * g)
    m_hat = m / (1.0 - ADAM_B1 ** ADAM_STEP)
    v_hat = v / (1.0 - ADAM_B2 ** ADAM_STEP)
    delta = -ADAM_LR * (m_hat / (jnp.sqrt(v_hat) + ADAM_EPS) + ADAM_WD * w)
    return delta, m, v


def _adamw(name, g8, w, m, v):
    r, c = w.shape
    tr = _row_tile(r, c)

    def body(g_ref, w_ref, m_ref, v_ref, go_ref, d_ref, mo_ref, vo_ref):
        g = g_ref[0].astype(F32)
        for d in range(1, N_DEV):
            g = g + g_ref[d].astype(F32)
        delta, mn, vn = _adamw_math(w_ref[...], g, m_ref[...], v_ref[...])
        go_ref[...] = g
        d_ref[...] = delta
        mo_ref[...] = mn
        vo_ref[...] = vn

    spec = pl.BlockSpec((tr, c), lambda i: (i, 0))
    return pl.pallas_call(
        body, name=name, grid=(r // tr,),
        in_specs=[pl.BlockSpec((N_DEV, tr, c), lambda i: (0, i, 0)), spec, spec, spec], out_specs=[spec] * 4,
        out_shape=[jax.ShapeDtypeStruct((r, c), F32)] * 4, compiler_params=_params(("parallel",)),
    )(g8, w, m, v)


def _adamw_small(g8, ws, ms, vs):
    n = len(ws)
    offsets = [sum(w.shape[0] for w in ws[:i]) for i in range(n)]

    def body(*refs):
        g_ref, w_refs, m_refs, v_refs = refs[0], refs[1:1 + n], refs[1 + n:1 + 2 * n], refs[1 + 2 * n:1 + 3 * n]
        outs = refs[1 + 3 * n:]
        for i in range(n):
            rows = w_refs[i].shape[0]
            g = g_ref[0, offsets[i]:offsets[i] + rows, :]
            for d in range(1, N_DEV):
                g = g + g_ref[d, offsets[i]:offsets[i] + rows, :]
            delta, mn, vn = _adamw_math(w_refs[i][...], g, m_refs[i][...], v_refs[i][...])
            for k, val in enumerate((g, delta, mn, vn)):
                outs[k * n + i][...] = val

    vmem = pl.BlockSpec(memory_space=pltpu.VMEM)
    res = pl.pallas_call(
        body, name="adamw_small", in_specs=[vmem] * (1 + 3 * n), out_specs=[vmem] * (4 * n),
        out_shape=[jax.ShapeDtypeStruct(w.shape, F32) for _ in range(4) for w in ws], compiler_params=_params(),
    )(g8, *ws, *ms, *vs)
    return [res[k * n:(k + 1) * n] for k in range(4)]


def _adamw_ada(c_t, dmod, w, m, v):
    r, c = w.shape
    tr = _row_tile(r, c)

    def body(ct_ref, dm_ref, w_ref, m_ref, v_ref, go_ref, d_ref, mo_ref, vo_ref):
        ct, dm = ct_ref[...], dm_ref[...]
        g = ct[:, 0:1] * dm[0:1, :]
        for b in range(1, N_DEV):
            g = g + ct[:, b:b + 1] * dm[b:b + 1, :]
        delta, mn, vn = _adamw_math(w_ref[...], g, m_ref[...], v_ref[...])
        go_ref[...] = g
        d_ref[...] = delta
        mo_ref[...] = mn
        vo_ref[...] = vn

    spec = pl.BlockSpec((tr, c), lambda i: (i, 0))
    return pl.pallas_call(
        body, name="adamw_w_ada", grid=(r // tr,),
        in_specs=[pl.BlockSpec((tr, N_DEV), lambda i: (i, 0)), pl.BlockSpec((N_DEV, c), lambda i: (0, 0)),
                  spec, spec, spec],
        out_specs=[spec] * 4, out_shape=[jax.ShapeDtypeStruct((r, c), F32)] * 4,
        compiler_params=_params(("parallel",)),
    )(c_t, dmod, w, m, v)


def _cols_from_slots(g):
    n, r, c = g.shape
    return jnp.transpose(g, (1, 0, 2)).reshape(r, n * c)


def _cols_to_slots(w):
    r, c = w.shape
    return jnp.transpose(w.reshape(r, N_DEV, c // N_DEV), (1, 0, 2))


def _pair_cols(w):
    g0, g1 = w[..., 0:FF_HALF], w[..., FF_HALF:D_FF]
    v0, v1 = w[..., D_FF:D_FF + FF_HALF], w[..., D_FF + FF_HALF:]
    return jnp.concatenate([g0, v0, g1, v1], axis=-1)


def _unpair_cols(w):
    g0, v0 = w[..., 0:FF_HALF], w[..., FF_HALF:D_FF]
    g1, v1 = w[..., D_FF:D_FF + FF_HALF], w[..., D_FF + FF_HALF:]
    return jnp.concatenate([g0, g1, v0, v1], axis=-1)


def _row(v):
    return v.reshape(1, -1)


def kernel(x, c, w_ada, b_ada, w_in, b_forget, w_fox_proj, w_sb_proj, w_o, ln1_g, ln1_b, w_up, conv_w, conv_b, w_down, ln2_g, ln2_b, loss_target, m_w_ada, m_b_ada, m_w_in, m_b_forget, m_w_fox_proj, m_w_sb_proj, m_w_o, m_ln1_g, m_ln1_b, m_w_up, m_conv_w, m_conv_b, m_w_down, m_ln2_g, m_ln2_b, v_w_ada, v_b_ada, v_w_in, v_b_forget, v_w_fox_proj, v_w_sb_proj, v_w_o, v_ln1_g, v_ln1_b, v_w_up, v_conv_w, v_conv_b, v_w_down, v_ln2_g, v_ln2_b):
    s = x.shape[1]
    me = 4 * lax.axis_index("x") + 2 * lax.axis_index("y") + lax.axis_index("c")
    x2 = x.reshape(s, D_MODEL)
    tgt = loss_target.reshape(s, D_MODEL)

    b_ada_loc = lax.dynamic_slice(b_ada, (me * ADA_SHARD,), (ADA_SHARD,)).reshape(1, ADA_SHARD)
    c_all, mod = _mod_exchange(c, w_ada, b_ada_loc)
    mod = mod.reshape(N_MOD, 1, D_MODEL)
    sh1, sc1, gt1, sh2, sc2, gt2 = [mod[i] for i in range(N_MOD)]

    g_in = _allgather_two_level("ag_w_in", w_in.astype(BF16))
    late_weights = _Ride([w_fox_proj.astype(BF16), w_sb_proj.astype(BF16), w_o.astype(BF16), w_up.astype(BF16),
                          w_down.astype(BF16), conv_w], scatter=False)
    w_in_f = _cols_from_slots(g_in)
    w_proj = jnp.concatenate(
        [w_in_f[:, 0:1536], w_in_f[:, 1544:3080], w_in_f[:, 3080:5128], w_in_f[:, 1536:1544],
         jnp.zeros((D_MODEL, W_PROJ - 5128), BF16)], axis=1)
    w_qkv, w_gates, w_f = w_proj[:, :W_QKV], w_proj[:, W_QKV:W_QKV + W_GATES], w_proj[:, W_QKV + W_GATES:W_QKV + W_GATES + W_F]
    conv_b_p = _pair_cols(_row(conv_b))
    b_f_pad = jnp.pad(_row(b_forget), ((0, 0), (0, LANES - N_HEADS)))

    (u1,) = _rowwise("modulate1", lambda xb, sc, sh: (xb * (1.0 + sc) + sh,),
                     [(x2, D_MODEL, 0)], [sc1, sh1], [(D_MODEL, BF16)], tr=512)
    qkv = _mm(u1, w_qkv, name="mm_qkv", out_dtype=BF16)
    gates = _mm(u1, w_gates, name="mm_gates")
    f_raw = _mm(u1, w_f, name="mm_forget")
    cum_col = _forget_cumsum(f_raw, b_f_pad)
    cum_row = jnp.transpose(cum_col[:, :N_HEADS]).reshape(N_HEADS // 2, 2, s)
    (y_fox, y_fox32, lse), (g_fox, g_sb, g_o, g_up, g_down, g_cw) = _fox_fwd(qkv, cum_col, cum_row, ride=late_weights)
    w_fox_f = _cols_from_slots(g_fox)
    w_sb_f = _cols_from_slots(g_sb)
    w_o_f = g_o.reshape(D_MODEL, D_MODEL)
    w_up_p = _pair_cols(_cols_from_slots(g_up))
    w_down_f = g_down.reshape(D_FF, D_MODEL)
    conv_w_p = _pair_cols(_cols_from_slots(g_cw))
    y_sb, sb_run = _sb_fwd(qkv)
    pf = _mm(y_fox, w_fox_f, name="mm_fox_proj", out_dtype=BF16)
    ps = _mm(y_sb, w_sb_f, name="mm_sb_proj", out_dtype=BF16)
    (merged,) = _rowwise("gate_merge", lambda ga, gb, a, b: (_sigmoid(ga) * a + _sigmoid(gb) * b,),
                         [(gates, D_MODEL, 0), (gates, D_MODEL, 1), (pf, D_MODEL, 0), (ps, D_MODEL, 0)], [],
                         [(D_MODEL, BF16)])
    attn_out = _mm(merged, w_o_f, name="mm_w_o")

    def ln_fwd(xb, fb, gt, g, b):
        xhat, _ = _ln_stats(ALPHA * xb + (1.0 + gt) * fb)
        return xhat * g + b

    def ln1_mod(xb, fb, gt, g, b, sc, sh):
        y = ln_fwd(xb, fb, gt, g, b)
        return y, y * (1.0 + sc) + sh

    x1, u2 = _rowwise("ln1_modulate2", ln1_mod, [(x2, D_MODEL, 0), (attn_out, D_MODEL, 0)],
                      [gt1, _row(ln1_g), _row(ln1_b), sc2, sh2], [(D_MODEL, F32), (D_MODEL, BF16)])

    hpre = _mm(u2, w_up_p, name="mm_w_up", tn=1408)
    act = _conv_gate_fwd(hpre, conv_w_p, conv_b_p)
    ffn_out = _mm(act, w_down_f, name="mm_w_down", tk=2816)

    def ln2_bwd(xb, fb, tb, gt, g, b):
        xhat, rstd = _ln_stats(ALPHA * xb + (1.0 + gt) * fb)
        err = (xhat * g + b) - tb
        dy = err * (1.0 / D_MODEL)
        dr = _ln_bwd(dy, xhat, rstd, g)
        return (dr * (1.0 + gt), ALPHA * dr,
                _colsum(err * err), _colsum(dy * xhat), _colsum(dy), _colsum(dr * fb))

    dffn, dx1_res, sq_err, d_ln2_g, d_ln2_b, d_gt2 = _rowwise(
        "ln2_bwd", ln2_bwd, [(x1, D_MODEL, 0), (ffn_out, D_MODEL, 0), (tgt, D_MODEL, 0)],
        [gt2, _row(ln2_g), _row(ln2_b)], [(D_MODEL, BF16), (D_MODEL, F32)], sums=[D_MODEL] * 4)
    loss_part = jnp.broadcast_to(0.5 * jnp.sum(sq_err) / D_MODEL, (1, LANES))

    d_w_down = _mm(act, dffn, name="mm_d_w_down", ta=True, tm=1408, tk=2048, out_dtype=BF16)
    d_act = _mm(dffn, w_down_f, name="mm_d_act", tb=True, tn=1408, out_dtype=BF16)
    dh, d_conv_b_p, d_conv_w_p = _conv_gate_bwd(hpre, d_act, conv_w_p, conv_b_p)
    dhpre = _conv_input_bwd(dh, conv_w_p)
    d_w_up_p = _mm(u2, dhpre, name="mm_d_w_up", ta=True, tn=1408, tk=2048, out_dtype=BF16)
    du2 = _mm(dhpre, w_up_p, name="mm_d_u2", tb=True, tk=2816)

    def ln1_bwd(du, dres, x1b, xb, fb, sc, gt, g):
        dx1 = dres + du * (1.0 + sc)
        xhat, rstd = _ln_stats(ALPHA * xb + (1.0 + gt) * fb)
        dr = _ln_bwd(dx1, xhat, rstd, g)
        return (dr * (1.0 + gt), ALPHA * dr,
                _colsum(du * x1b), _colsum(du), _colsum(dx1 * xhat), _colsum(dx1), _colsum(dr * fb))

    d_attn, dx_res, d_sc2, d_sh2, d_ln1_g, d_ln1_b, d_gt1 = _rowwise(
        "ln1_bwd", ln1_bwd,
        [(du2, D_MODEL, 0), (dx1_res, D_MODEL, 0), (x1, D_MODEL, 0), (x2, D_MODEL, 0), (attn_out, D_MODEL, 0)],
        [sc2, gt1, _row(ln1_g)], [(D_MODEL, BF16), (D_MODEL, F32)], sums=[D_MODEL] * 5)

    d_w_o = _mm(merged, d_attn, name="mm_d_w_o", ta=True, out_dtype=BF16)
    d_merged = _mm(d_attn, w_o_f, name="mm_d_merged", tb=True, out_dtype=BF16)

    def merge_bwd(dm, ga, gb, a, b):
        dm, a, b = dm.astype(F32), a.astype(F32), b.astype(F32)
        sa, sb = _sigmoid(ga), _sigmoid(gb)
        return dm * a * sa * (1.0 - sa), dm * b * sb * (1.0 - sb), dm * sa, dm * sb

    d_ga, d_gb, d_pf, d_ps = _rowwise(
        "gate_merge_bwd", merge_bwd,
        [(d_merged, D_MODEL, 0), (gates, D_MODEL, 0), (gates, D_MODEL, 1), (pf, D_MODEL, 0), (ps, D_MODEL, 0)], [],
        [(D_MODEL, BF16)] * 4)
    d_w_fox = _mm(y_fox, d_pf, name="mm_d_w_fox", ta=True, out_dtype=BF16)
    d_w_sb = _mm(y_sb, d_ps, name="mm_d_w_sb", ta=True, out_dtype=BF16)
    d_y_fox = _mm(d_pf, w_fox_f, name="mm_d_y_fox", tb=True, out_dtype=BF16)
    d_y_sb = _mm(d_ps, w_sb_f, name="mm_d_y_sb", tb=True, out_dtype=BF16)
    early_grads = _Ride(
        [_cols_to_slots(d_w_fox), _cols_to_slots(d_w_sb), d_w_o.reshape(N_DEV, D_MODEL // N_DEV, D_MODEL),
         _cols_to_slots(_unpair_cols(d_w_up_p)), d_w_down.reshape(N_DEV, D_FF // N_DEV, D_MODEL)], scatter=True)
    (dq_a, dk_a, dv_a, d_cum_row), early_slots = _fox_bwd(qkv, cum_col, cum_row, y_fox32, lse, d_y_fox,
                                                                    ride=early_grads)
    dq_b, dk_b, dv_b = _sb_bwd(qkv, sb_run, d_y_sb)
    d_cum = jnp.transpose(d_cum_row.reshape(N_HEADS, s))
    d_cum = jnp.pad(d_cum, ((0, 0), (0, LANES - N_HEADS)))
    d_f, d_b_forget = _forget_bwd(d_cum, f_raw, b_f_pad)
    d_qkv = [dq_a, dk_a, dv_a, dq_b, dk_b, dv_b]
    g_qkv = list(_mm_pieces_tn(u1, d_qkv, name="mm_d_w_in_qkv"))
    g_ga, g_gb, g_f = _mm_pieces_tn(u1, [d_ga, d_gb, d_f], name="mm_d_w_in_gates")
    d_w_in_f = jnp.concatenate(g_qkv[:3] + [g_f[:, :N_HEADS]] + g_qkv[3:] + [g_ga, g_gb], axis=1)
    du1, (in_slots,) = _mm_pieces_nt(d_qkv + [d_ga, d_gb, d_f], w_proj, name="mm_d_u1",
                                     ride=_Ride([_cols_to_slots(d_w_in_f)], scatter=True))

    def x_bwd(du, dres, xb, sc):
        return dres + du * (1.0 + sc), _colsum(du * xb), _colsum(du)

    grad_x, d_sc1, d_sh1 = _rowwise("x_bwd", x_bwd, [(du1, D_MODEL, 0), (dx_res, D_MODEL, 0), (x2, D_MODEL, 0)],
                                    [sc1], [(D_MODEL, F32)], sums=[D_MODEL] * 2, tr=512)

    d_conv_b = _unpair_cols(d_conv_b_p)
    d_conv_w = _unpair_cols(d_conv_w_p)
    cb_pad = N_MOD * D_MODEL - 2 * D_FF
    small = jnp.concatenate(
        [d_sh1, d_sc1, d_gt1, d_sh2, d_sc2, d_gt2, jnp.pad(d_b_forget, ((0, 0), (0, D_MODEL - LANES))),
         d_ln1_g, d_ln1_b, d_ln2_g, d_ln2_b, jnp.pad(d_conv_b, ((0, 0), (0, cb_pad))),
         d_conv_w.reshape(1, 6 * D_FF), loss_part], axis=1)
    n_small = small.shape[1] // LANES - 1
    small = jnp.pad(small.reshape(n_small + 1, LANES), ((0, -(n_small + 1) % 8), (0, 0)))
    (small_all,) = _exchange("ag_small_grads", [small], scatter=False)
    loss = jnp.sum(small_all[:, n_small, 0])
    n_rep = n_small - 6 * D_FF // LANES
    cw8 = small_all[:, n_rep:n_small, :].reshape(N_DEV, 3, 2 * D_FF)
    cw8 = lax.dynamic_slice(cw8, (0, 0, me * UP_SHARD), (N_DEV, 3, UP_SHARD))
    dmod8 = small_all[:, :N_MOD * D_MODEL // LANES, :].reshape(N_DEV, N_MOD * D_MODEL)
    dmod_loc = lax.dynamic_slice(dmod8, (0, me * ADA_SHARD), (N_DEV, ADA_SHARD))

    def rows_of(a):
        return a.reshape(-1, LANES)

    def forget_rows(a):
        return rows_of(jnp.pad(a, (0, D_MODEL - N_HEADS)))

    def conv_b_rows(a):
        return rows_of(jnp.pad(a, (0, cb_pad)))

    rep_sizes = {"b_ada": N_MOD * D_MODEL, "b_forget": N_HEADS, "ln1_g": D_MODEL, "ln1_b": D_MODEL, "ln2_g": D_MODEL,
                 "ln2_b": D_MODEL, "conv_b": 2 * D_FF}
    rep_w = [rows_of(b_ada), forget_rows(b_forget), rows_of(ln1_g), rows_of(ln1_b), rows_of(ln2_g), rows_of(ln2_b),
             conv_b_rows(conv_b)]
    rep_m = [rows_of(m_b_ada), forget_rows(m_b_forget), rows_of(m_ln1_g), rows_of(m_ln1_b), rows_of(m_ln2_g),
             rows_of(m_ln2_b), conv_b_rows(m_conv_b)]
    rep_v = [rows_of(v_b_ada), forget_rows(v_b_forget), rows_of(v_ln1_g), rows_of(v_ln1_b), rows_of(v_ln2_g),
             rows_of(v_ln2_b), conv_b_rows(v_conv_b)]
    rep_out = _adamw_small(small_all, rep_w, rep_m, rep_v)
    rep = [{name: a.reshape(-1)[:size] for (name, size), a in zip(rep_sizes.items(), outs)} for outs in rep_out]
    r_conv_w = _adamw("adamw_conv_w", cw8, conv_w, m_conv_w, v_conv_w)
    r_ada = _adamw_ada(jnp.transpose(c_all.reshape(N_DEV, D_MODEL)), dmod_loc, w_ada, m_w_ada, v_w_ada)

    r_in = _adamw("adamw_w_in", in_slots, w_in, m_w_in, v_w_in)
    r_fox = _adamw("adamw_w_fox", early_slots[0], w_fox_proj, m_w_fox_proj, v_w_fox_proj)
    r_sb = _adamw("adamw_w_sb", early_slots[1], w_sb_proj, m_w_sb_proj, v_w_sb_proj)
    r_o = _adamw("adamw_w_o", early_slots[2], w_o, m_w_o, v_w_o)
    r_up = _adamw("adamw_w_up", early_slots[3], w_up, m_w_up, v_w_up)
    r_down = _adamw("adamw_w_down", early_slots[4], w_down, m_w_down, v_w_down)

    def leaf(i):
        return [r_ada[i], rep[i]["b_ada"], r_in[i], rep[i]["b_forget"], r_fox[i], r_sb[i], r_o[i], rep[i]["ln1_g"],
                rep[i]["ln1_b"], r_up[i], r_conv_w[i], rep[i]["conv_b"], r_down[i], rep[i]["ln2_g"], rep[i]["ln2_b"]]

    return (loss, grad_x.reshape(1, s, D_MODEL), *leaf(0), *leaf(1), *leaf(2), *leaf(3))
```

```python
import functools

import jax
import jax.numpy as jnp
from jax import lax
from jax.experimental import pallas as pl
from jax.experimental.pallas import tpu as pltpu

F32 = jnp.float32
BF16 = jnp.bfloat16
MESH = pl.DeviceIdType.MESH
ANY = pl.BlockSpec(memory_space=pl.ANY)

N_DEV = 8
D_MODEL = 1024
HEAD_DIM = 64
N_HEADS = 8
ATTN_W = N_HEADS * HEAD_DIM
D_FF = 2816
FF_HALF = D_FF // 2
N_MOD = 6
ADA_SHARD = N_MOD * D_MODEL // N_DEV
IN_SHARD = 641
UP_SHARD = 704
ATTN_SCALE = HEAD_DIM ** -0.5
ALPHA = 2.0 ** 0.25
LN_EPS = 1e-5
LANES = 128
TQ = 512
SCAN_W = 256
VMEM_LIMIT = 56 * 1024 * 1024

ADAM_LR, ADAM_B1, ADAM_B2, ADAM_EPS, ADAM_WD, ADAM_STEP = 0.001, 0.9, 0.999, 1e-08, 0.01, 10

W_QKV, W_GATES, W_F = 3072, 2048, 128
W_PROJ = 5376


def _params(sem=None):
    return pltpu.CompilerParams(dimension_semantics=sem, vmem_limit_bytes=VMEM_LIMIT)


def _tile(n, cap):
    if n <= cap:
        return n
    best = None
    for t in range(LANES, cap + 1, LANES):
        if n % t == 0:
            best = t
    assert best is not None, (n, cap)
    return best


def _row_tile(r, width, budget=192 * 1024):
    if r * width <= budget or r % 16:
        return r
    best = 16
    for t in range(16, r + 1, 16):
        if r % t == 0 and t * width <= budget:
            best = t
    return best


def _me():
    x, y, c = lax.axis_index("x"), lax.axis_index("y"), lax.axis_index("c")
    return x, y, c, 4 * x + 2 * y + c


def _peer(r):
    x, y, c, _ = _me()
    px = 1 - x if r & 4 else x
    py = 1 - y if r & 2 else y
    pc = 1 - c if r & 1 else c
    return (px, py, pc), 4 * px + 2 * py + pc


class _Ride:
    def __init__(self, arrays, scatter):
        self.arrays, self.scatter, self.n = list(arrays), scatter, len(arrays)
        self.in_specs = [ANY] * self.n
        self.out_specs = [ANY] * self.n
        self.out_shape = [jax.ShapeDtypeStruct(a.shape if scatter else (N_DEV,) + a.shape, a.dtype) for a in arrays]
        self.scratch = [pltpu.SemaphoreType.DMA((self.n, N_DEV - 1)), pltpu.SemaphoreType.DMA((self.n, N_DEV - 1)),
                        pltpu.SemaphoreType.DMA((self.n,))]

    def _local(self, ins, outs, sems, a):
        me = _me()[3]
        return pltpu.make_async_copy(ins[a].at[me] if self.scatter else ins[a], outs[a].at[me], sems[2].at[a])

    def _remote(self, ins, outs, sems, a, r, arriving):
        me = _me()[3]
        peer, pidx = _peer(r)
        src = ins[a].at[me if arriving else pidx] if self.scatter else ins[a]
        return pltpu.make_async_remote_copy(
            src_ref=src, dst_ref=outs[a].at[pidx if arriving else me], send_sem=sems[0].at[a, r - 1],
            recv_sem=sems[1].at[a, r - 1], device_id=peer, device_id_type=MESH)

    def start(self, ins, outs, sems):
        for a in range(self.n):
            self._local(ins, outs, sems, a).start()
        for r in range(1, N_DEV):
            for a in range(self.n):
                self._remote(ins, outs, sems, a, r, False).start()

    def wait(self, ins, outs, sems):
        for r in range(1, N_DEV):
            for a in range(self.n):
                self._remote(ins, outs, sems, a, r, True).wait_recv()
        for r in range(1, N_DEV):
            for a in range(self.n):
                self._remote(ins, outs, sems, a, r, False).wait_send()
        for a in range(self.n):
            self._local(ins, outs, sems, a).wait()


def _exchange(name, arrays, scatter):
    ride = _Ride(arrays, scatter)

    def body(*refs):
        ins, outs, sems = refs[:ride.n], refs[ride.n:2 * ride.n], refs[2 * ride.n:]
        ride.start(ins, outs, sems)
        ride.wait(ins, outs, sems)

    return pl.pallas_call(body, name=name, in_specs=ride.in_specs, out_specs=ride.out_specs, out_shape=ride.out_shape,
                          scratch_shapes=ride.scratch)(*arrays)


def _allgather_two_level(name, a):
    def body(a_ref, out_ref, send_sems, recv_sems, local_sem):
        x, y, c, me = _me()
        sibling = (x, y, 1 - c)
        chips = [(1 - x, y), (x, 1 - y), (1 - x, 1 - y)]

        def idx(px, py, pc):
            return 4 * px + 2 * py + pc

        def copy(k, block, to, src=None):
            slot = out_ref.at[idx(*block)]
            return pltpu.make_async_remote_copy(
                src_ref=slot if src is None else src, dst_ref=slot, send_sem=send_sems.at[k], recv_sem=recv_sems.at[k],
                device_id=to, device_id_type=MESH)

        mine = pltpu.make_async_copy(a_ref, out_ref.at[me], local_sem)
        mine.start()
        first = [copy(0, (x, y, c), sibling, src=a_ref)]
        first += [copy(1 + j, (x, y, c), (*chip, c), src=a_ref) for j, chip in enumerate(chips)]
        for cp in first:
            cp.start()
        passed = [copy(4 + j, (*chip, c), sibling) for j, chip in enumerate(chips)]
        for j, chip in enumerate(chips):
            copy(1 + j, (*chip, c), (x, y, c)).wait_recv()
            passed[j].start()
        copy(0, sibling, (x, y, c)).wait_recv()
        for j, chip in enumerate(chips):
            copy(4 + j, (*chip, 1 - c), (x, y, c)).wait_recv()
        for cp in first + passed:
            cp.wait_send()
        mine.wait()

    return pl.pallas_call(
        body, name=name, in_specs=[ANY], out_specs=ANY,
        out_shape=jax.ShapeDtypeStruct((N_DEV,) + a.shape, a.dtype),
        scratch_shapes=[pltpu.SemaphoreType.DMA((N_DEV - 1,)), pltpu.SemaphoreType.DMA((N_DEV - 1,)),
                        pltpu.SemaphoreType.DMA],
    )(a)


def _with_ride(body, ride, n_in, n_out, grid):
    if ride is None:
        return body
    n = ride.n

    def wrapped(*refs):
        ins, rins = refs[:n_in], refs[n_in:n_in + n]
        outs, routs = refs[n_in + n:n_in + n + n_out], refs[n_in + n + n_out:n_in + 2 * n + n_out]
        rest = refs[n_in + 2 * n + n_out:]
        scratch, sems = rest[:len(rest) - 3], rest[len(rest) - 3:]
        ids = [pl.program_id(d) for d in range(len(grid))]
        first = functools.reduce(lambda p, q: p & q, [i == 0 for i in ids])
        last = functools.reduce(lambda p, q: p & q, [i == g - 1 for i, g in zip(ids, grid)])

        @pl.when(first)
        def _():
            ride.start(rins, routs, sems)

        body(*ins, *outs, *scratch)

        @pl.when(last)
        def _():
            ride.wait(rins, routs, sems)

    return wrapped


def _ride_call(body, ride, *, name, grid, in_specs, out_specs, out_shape, scratch_shapes, sem, args):
    n_in, n_out = len(in_specs), len(out_specs)
    if ride is None:
        res = pl.pallas_call(body, name=name, grid=grid, in_specs=in_specs, out_specs=out_specs, out_shape=out_shape,
                             scratch_shapes=scratch_shapes, compiler_params=_params(sem))(*args)
        return list(res), []
    res = pl.pallas_call(
        _with_ride(body, ride, n_in, n_out, grid), name=name, grid=grid,
        in_specs=list(in_specs) + ride.in_specs, out_specs=list(out_specs) + ride.out_specs,
        out_shape=list(out_shape) + ride.out_shape, scratch_shapes=list(scratch_shapes) + ride.scratch,
        compiler_params=_params(("arbitrary",) * len(grid)))(*args, *ride.arrays)
    return list(res[:n_out]), list(res[n_out:])


def _mm(a, b, *, name, ta=False, tb=False, out_dtype=F32, tm=1024, tn=1024, tk=1024, ride=None):
    m, k = (a.shape[1], a.shape[0]) if ta else a.shape
    n = b.shape[0] if tb else b.shape[1]
    assert (b.shape[1] if tb else b.shape[0]) == k
    tm, tn, tk = _tile(m, tm), _tile(n, tn), _tile(k, tk)
    nk = k // tk
    a_spec = pl.BlockSpec((tk, tm), lambda i, j, l: (l, i)) if ta else pl.BlockSpec((tm, tk), lambda i, j, l: (i, l))
    b_spec = pl.BlockSpec((tn, tk), lambda i, j, l: (j, l)) if tb else pl.BlockSpec((tk, tn), lambda i, j, l: (l, j))
    dims = (((0,) if ta else (1,), (1,) if tb else (0,)), ((), ()))

    def body(a_ref, b_ref, o_ref, *acc):
        p = lax.dot_general(a_ref[...].astype(BF16), b_ref[...].astype(BF16), dims, preferred_element_type=F32)
        if nk == 1:
            o_ref[...] = p.astype(out_dtype)
            return
        acc_ref = acc[0]
        step = pl.program_id(2)

        @pl.when(step == 0)
        def _():
            acc_ref[...] = p

        @pl.when(step > 0)
        def _():
            acc_ref[...] += p

        @pl.when(step == nk - 1)
        def _():
            o_ref[...] = acc_ref[...].astype(out_dtype)

    outs, rode = _ride_call(
        body, ride, name=name, grid=(m // tm, n // tn, nk), in_specs=[a_spec, b_spec],
        out_specs=[pl.BlockSpec((tm, tn), lambda i, j, l: (i, j))], out_shape=[jax.ShapeDtypeStruct((m, n), out_dtype)],
        scratch_shapes=[] if nk == 1 else [pltpu.VMEM((tm, tn), F32)], sem=("parallel", "parallel", "arbitrary"),
        args=(a, b))
    return outs[0] if ride is None else (outs[0], rode)


def _mm_pieces_nt(pieces, w, *, name, epilogue, rows, vecs, n_sums, tm=512, ride=None):
    s, n = pieces[0].shape[0], w.shape[0]
    tm = min(tm, s)
    widths = [p.shape[1] for p in pieces]
    offs = [sum(widths[:i]) for i in range(len(pieces))]
    n_p, n_r, n_v = len(pieces), len(rows), len(vecs)

    def body(*refs):
        p_refs, w_ref = refs[:n_p], refs[n_p]
        extra = [r[...] for r in refs[n_p + 1:n_p + 1 + n_r + n_v]]
        o_ref, sum_refs = refs[n_p + 1 + n_r + n_v], refs[n_p + 2 + n_r + n_v:]
        acc = None
        for p_ref, off, width in zip(p_refs, offs, widths):
            part = _dot_nt(p_ref[...], w_ref[:, off:off + width])
            acc = part if acc is None else acc + part
        res = epilogue(acc, *extra)
        o_ref[...] = res[0]
        step = pl.program_id(0)
        for s_ref, val in zip(sum_refs, res[1:]):
            @pl.when(step == 0)
            def _(s_ref=s_ref, val=val):
                s_ref[...] = val

            @pl.when(step > 0)
            def _(s_ref=s_ref, val=val):
                s_ref[...] += val

    row_spec = pl.BlockSpec((tm, n), lambda i: (i, 0))
    vec_spec = pl.BlockSpec((1, n), lambda i: (0, 0))
    return _ride_call(
        body, ride, name=name, grid=(s // tm,),
        in_specs=[pl.BlockSpec((tm, width), lambda i: (i, 0)) for width in widths]
        + [pl.BlockSpec(w.shape, lambda i: (0, 0))] + [row_spec] * n_r + [vec_spec] * n_v,
        out_specs=[row_spec] + [vec_spec] * n_sums,
        out_shape=[jax.ShapeDtypeStruct((s, n), F32)] + [jax.ShapeDtypeStruct((1, n), F32)] * n_sums,
        scratch_shapes=[], sem=("arbitrary",), args=(*pieces, w, *rows, *vecs))


def _mm_pieces_tn(a, pieces, *, name, tk=1024):
    s, m = a.shape
    tk = min(tk, s)
    nk = s // tk
    n = len(pieces)
    dims = (((0,), (0,)), ((), ()))

    def body(*refs):
        a_ref, p_refs, o_refs, accs = refs[0], refs[1:1 + n], refs[1 + n:1 + 2 * n], refs[1 + 2 * n:]
        step = pl.program_id(0)
        a_blk = a_ref[...]
        for p_ref, o_ref, acc in zip(p_refs, o_refs, accs):
            part = lax.dot_general(a_blk, p_ref[...], dims, preferred_element_type=F32)

            @pl.when(step == 0)
            def _(acc=acc, part=part):
                acc[...] = part

            @pl.when(step > 0)
            def _(acc=acc, part=part):
                acc[...] += part

            @pl.when(step == nk - 1)
            def _(acc=acc, o_ref=o_ref):
                o_ref[...] = acc[...].astype(BF16)

    return pl.pallas_call(
        body, name=name, grid=(nk,),
        in_specs=[pl.BlockSpec((tk, m), lambda l: (l, 0))] + [pl.BlockSpec((tk, p.shape[1]), lambda l: (l, 0)) for p in pieces],
        out_specs=[pl.BlockSpec((m, p.shape[1]), lambda l: (0, 0)) for p in pieces],
        out_shape=[jax.ShapeDtypeStruct((m, p.shape[1]), BF16) for p in pieces],
        scratch_shapes=[pltpu.VMEM((m, p.shape[1]), F32) for p in pieces],
        compiler_params=_params(("arbitrary",)),
    )(a, *pieces)


def _rowwise(name, fn, rows, vecs, outs, sums=(), tr=512):
    s = rows[0][0].shape[0]
    tr = min(tr, s)
    nr, nv, no = len(rows), len(vecs), len(outs)

    def body(*refs):
        vals = [r[...] for r in refs[:nr + nv]]
        res = fn(*vals)
        for o_ref, val in zip(refs[nr + nv:nr + nv + no], res[:no]):
            o_ref[...] = val.astype(o_ref.dtype)
        step = pl.program_id(0)
        for s_ref, val in zip(refs[nr + nv + no:], res[no:]):
            @pl.when(step == 0)
            def _(s_ref=s_ref, val=val):
                s_ref[...] = val

            @pl.when(step > 0)
            def _(s_ref=s_ref, val=val):
                s_ref[...] += val

    in_specs = [pl.BlockSpec((tr, w), functools.partial(lambda i, cb: (i, cb), cb=cb)) for _, w, cb in rows]
    in_specs += [pl.BlockSpec(v.shape, lambda i: (0, 0)) for v in vecs]
    out_specs = [pl.BlockSpec((tr, w), lambda i: (i, 0)) for w, _ in outs]
    out_specs += [pl.BlockSpec((1, w), lambda i: (0, 0)) for w in sums]
    out_shape = [jax.ShapeDtypeStruct((s, w), dt) for w, dt in outs]
    out_shape += [jax.ShapeDtypeStruct((1, w), F32) for w in sums]
    return pl.pallas_call(
        body, name=name, grid=(s // tr,), in_specs=in_specs, out_specs=out_specs, out_shape=out_shape,
        compiler_params=_params(("arbitrary",) if sums else ("parallel",)),
    )(*[r[0] for r in rows], *vecs)


def _colsum(x):
    return jnp.sum(x, axis=0, keepdims=True)


def _sigmoid(x):
    return 1.0 / (1.0 + jnp.exp(-x))


def _log_sigmoid(x):
    return jnp.minimum(x, 0.0) - jnp.log(1.0 + jnp.exp(-jnp.abs(x)))


def _ln_stats(r):
    mu = jnp.mean(r, axis=-1, keepdims=True)
    xc = r - mu
    var = jnp.mean(xc * xc, axis=-1, keepdims=True)
    rstd = lax.rsqrt(var + LN_EPS)
    return xc * rstd, rstd


def _ln_bwd(dy, xhat, rstd, g):
    dxh = dy * g
    m1 = jnp.mean(dxh, axis=-1, keepdims=True)
    m2 = jnp.mean(dxh * xhat, axis=-1, keepdims=True)
    return rstd * (dxh - m1 - xhat * m2)


def _mod_exchange(c_row, w_ada, b_ada_loc):
    def body(c_ref, w_ref, b_ref, call_ref, mod_ref, piece_ref, send_sems, recv_sems):
        me = _me()[3]
        call_ref[me] = c_ref[...]
        sent = []
        for r in range(1, N_DEV):
            peer, _ = _peer(r)
            cp = pltpu.make_async_remote_copy(
                src_ref=c_ref, dst_ref=call_ref.at[me], send_sem=send_sems.at[0, r - 1],
                recv_sem=recv_sems.at[0, r - 1], device_id=peer, device_id_type=MESH)
            cp.start()
            sent.append(cp)
        for r in range(1, N_DEV):
            peer, pidx = _peer(r)
            pltpu.make_async_remote_copy(
                src_ref=c_ref, dst_ref=call_ref.at[pidx], send_sem=send_sems.at[0, r - 1],
                recv_sem=recv_sems.at[0, r - 1], device_id=peer, device_id_type=MESH).wait_recv()
        c_all = jnp.concatenate([call_ref[d] for d in range(N_DEV)], axis=0)
        mod_loc = jnp.dot(c_all, w_ref[...], preferred_element_type=F32,
                          precision=lax.Precision.HIGHEST) + b_ref[...]
        for d in range(N_DEV):
            piece_ref[d] = mod_loc[d:d + 1, :]
        mod_ref[me] = piece_ref[me]
        for r in range(1, N_DEV):
            peer, pidx = _peer(r)
            cp = pltpu.make_async_remote_copy(
                src_ref=piece_ref.at[pidx], dst_ref=mod_ref.at[me], send_sem=send_sems.at[1, r - 1],
                recv_sem=recv_sems.at[1, r - 1], device_id=peer, device_id_type=MESH)
            cp.start()
            sent.append(cp)
        for r in range(1, N_DEV):
            peer, pidx = _peer(r)
            pltpu.make_async_remote_copy(
                src_ref=piece_ref.at[me], dst_ref=mod_ref.at[pidx], send_sem=send_sems.at[1, r - 1],
                recv_sem=recv_sems.at[1, r - 1], device_id=peer, device_id_type=MESH).wait_recv()
        for cp in sent:
            cp.wait_send()

    vmem = pl.BlockSpec(memory_space=pltpu.VMEM)
    return pl.pallas_call(
        body, name="mod_exchange", in_specs=[vmem, vmem, vmem], out_specs=[vmem, vmem],
        out_shape=[jax.ShapeDtypeStruct((N_DEV, 1, D_MODEL), F32), jax.ShapeDtypeStruct((N_DEV, 1, ADA_SHARD), F32)],
        scratch_shapes=[pltpu.VMEM((N_DEV, 1, ADA_SHARD), F32),
                        pltpu.SemaphoreType.DMA((2, N_DEV - 1)), pltpu.SemaphoreType.DMA((2, N_DEV - 1))],
        compiler_params=_params(),
    )(c_row, w_ada, b_ada_loc)


def _split3(x):
    hi = x.astype(BF16)
    r1 = x - hi.astype(F32)
    mid = r1.astype(BF16)
    lo = (r1 - mid.astype(F32)).astype(BF16)
    return hi, mid, lo


def _scan_rows(x_ref, o_ref, s, reverse, pre=None, post=None):
    tb = min(TQ, s)
    nb = s // tb
    row = lax.broadcasted_iota(jnp.int32, (tb, tb), 0)
    col = lax.broadcasted_iota(jnp.int32, (tb, tb), 1)
    tri = jnp.where((col >= row) if reverse else (col <= row), 1.0, 0.0).astype(BF16)

    def step(i, carry):
        blk = (nb - 1 - i) if reverse else i
        off = pl.multiple_of(blk * tb, tb)
        x = x_ref[pl.ds(off, tb), :]
        if pre is not None:
            x = pre(x, off)
        acc = carry
        for piece in _split3(x):
            acc = acc + jnp.dot(tri, piece, preferred_element_type=F32)
        o_ref[pl.ds(off, tb), :] = acc if post is None else post(acc, off)
        edge = acc[0:1, :] if reverse else acc[tb - 1:tb, :]
        return jnp.broadcast_to(edge, (tb, LANES))

    lax.fori_loop(0, nb, step, jnp.zeros((tb, LANES), F32))


def _forget_cumsum(f_raw, b_pad):
    s = f_raw.shape[0]

    def body(f_ref, b_ref, cum_ref):
        b = b_ref[...]
        _scan_rows(f_ref, cum_ref, s, False, pre=lambda x, off: _log_sigmoid(x + b))

    vmem = pl.BlockSpec(memory_space=pltpu.VMEM)
    return pl.pallas_call(body, name="forget_cumsum", in_specs=[vmem, vmem], out_specs=vmem,
                          out_shape=jax.ShapeDtypeStruct((s, LANES), F32), compiler_params=_params())(f_raw, b_pad)


def _forget_bwd(dcum, f_raw, b_pad):
    s = f_raw.shape[0]

    def body(d_ref, f_ref, b_ref, df_ref, db_ref, tmp_ref):
        b = b_ref[...]
        _scan_rows(d_ref, tmp_ref, s, True)
        df = tmp_ref[...] * _sigmoid(-(f_ref[...] + b))
        df_ref[...] = df.astype(BF16)
        db_ref[...] = _colsum(df)

    vmem = pl.BlockSpec(memory_space=pltpu.VMEM)
    return pl.pallas_call(
        body, name="forget_bwd", in_specs=[vmem, vmem, vmem], out_specs=[vmem, vmem],
        out_shape=[jax.ShapeDtypeStruct((s, LANES), BF16), jax.ShapeDtypeStruct((1, LANES), F32)],
        scratch_shapes=[pltpu.VMEM((s, LANES), F32)], compiler_params=_params())(dcum, f_raw, b_pad)


def _dot_nt(a, b):
    return lax.dot_general(a, b, (((1,), (1,)), ((), ())), preferred_element_type=F32)


def _head_masks():
    lane = lax.broadcasted_iota(jnp.int32, (TQ, LANES), 1)
    return lane, [lane < HEAD_DIM, lane >= HEAD_DIM]


def _pick(mask, x):
    return jnp.where(mask, x, jnp.zeros_like(x))


def _qkv_specs(s, col0):
    nb = ATTN_W // LANES
    return [pl.BlockSpec((TQ, LANES), lambda hp, qi: (qi, col0 + hp)),
            pl.BlockSpec((s, LANES), lambda hp, qi: (0, col0 + nb + hp)),
            pl.BlockSpec((s, LANES), lambda hp, qi: (0, col0 + 2 * nb + hp))]


def _pair_spec():
    return pl.BlockSpec((TQ, LANES), lambda hp, qi: (qi, hp))


def _diag_mask(width, strict):
    row = lax.broadcasted_iota(jnp.int32, (TQ, width * TQ), 0) + (width - 1) * TQ
    col = lax.broadcasted_iota(jnp.int32, (TQ, width * TQ), 1)
    return (col < row) if strict else (col <= row)


def _fox_fwd(qkv, cum_col, cum_row, ride=None):
    s = qkv.shape[0]
    nq = s // TQ

    def body(q_ref, k_ref, v_ref, cc_ref, cr_ref, o_ref, o32_ref, lse_ref):
        hp, qi = pl.program_id(0), pl.program_id(1)
        lane, masks = _head_masks()
        q2 = q_ref[...] * jnp.asarray(ATTN_SCALE, BF16)
        cc = cc_ref[...]
        qms = [_pick(masks[e], q2) for e in range(2)]
        cqs = [jnp.sum(jnp.where(lane == 2 * hp + e, cc, 0.0), axis=1, keepdims=True) for e in range(2)]

        def tile(kb, carry, masked, width=1):
            off, span = pl.multiple_of(kb * TQ, TQ), width * TQ
            k2, v2 = k_ref[pl.ds(off, span), :], v_ref[pl.ds(off, span), :]
            head0 = lax.broadcasted_iota(jnp.int32, (span, LANES), 1) < HEAD_DIM
            new = []
            for e in range(2):
                m, acc = carry[e]
                sc = _dot_nt(qms[e], k2) + (cqs[e] - cr_ref[e:e + 1, pl.ds(off, span)])
                if masked:
                    sc = jnp.where(_diag_mask(width, False), sc, -jnp.inf)
                m_new = jnp.maximum(m, jnp.max(sc, axis=1, keepdims=True))
                p = jnp.exp(sc - m_new)
                v_ones = jnp.where(head0 if e == 0 else ~head0, v2, jnp.ones_like(v2))
                acc = jnp.exp(m - m_new) * acc + jnp.dot(p.astype(BF16), v_ones, preferred_element_type=F32)
                new.append((m_new, acc))
            return tuple(new)

        init = (jnp.full((TQ, 1), -jnp.inf, F32), jnp.zeros((TQ, LANES), F32))
        carry = lax.cond(qi % 2 == 1, lambda cr: tile(qi - 1, cr, True, 2), lambda cr: tile(qi, cr, True), (init, init))
        quads = qi // 4
        carry = lax.fori_loop(0, quads, lambda j, cr: tile(4 * j, cr, False, 4), carry)
        carry = lax.cond(qi % 4 >= 2, lambda cr: tile(4 * quads, cr, False, 2), lambda cr: cr, carry)
        sums = [jnp.max(jnp.where(masks[1 - e], carry[e][1], 0.0), axis=1, keepdims=True) for e in range(2)]
        outs = [carry[e][1] / sums[e] for e in range(2)]
        lses = [carry[e][0] + jnp.log(sums[e]) for e in range(2)]
        out = jnp.where(masks[0], outs[0], outs[1])
        o_ref[...] = out.astype(BF16)
        o32_ref[...] = out
        lse_ref[...] = jnp.where(masks[0], lses[0], lses[1])

    return _ride_call(
        body, ride, name="fox_fwd", grid=(N_HEADS // 2, nq),
        in_specs=_qkv_specs(s, 0) + [pl.BlockSpec((TQ, LANES), lambda hp, qi: (qi, 0)),
                                     pl.BlockSpec((None, 2, s), lambda hp, qi: (hp, 0, 0))],
        out_specs=[_pair_spec(), _pair_spec(), _pair_spec()],
        out_shape=[jax.ShapeDtypeStruct((s, ATTN_W), BF16), jax.ShapeDtypeStruct((s, ATTN_W), F32),
                   jax.ShapeDtypeStruct((s, ATTN_W), F32)],
        scratch_shapes=[], sem=("parallel", "parallel"), args=(qkv, qkv, qkv, cum_col, cum_row))


def _write_transposed(acc_ref, out_ref):
    for c in range(out_ref.shape[0] // TQ):
        out_ref[c * TQ:(c + 1) * TQ, :] = jnp.transpose(acc_ref[:, c * TQ:(c + 1) * TQ]).astype(BF16)


def _fox_bwd(qkv, cum_col, cum_row, o, lse, do, ride=None):
    s = qkv.shape[0]
    nq = s // TQ

    def body(q_ref, k_ref, v_ref, cc_ref, cr_ref, o_ref, lse_ref, do_ref,
             dq_ref, dk_ref, dv_ref, dcr_ref, dk_acc, dv_acc):
        hp, qi = pl.program_id(0), pl.program_id(1)

        @pl.when(qi == 0)
        def _():
            dk_acc[...] = jnp.zeros_like(dk_acc)
            dv_acc[...] = jnp.zeros_like(dv_acc)
            dcr_ref[...] = jnp.zeros_like(dcr_ref)

        lane, masks = _head_masks()
        q2 = q_ref[...] * jnp.asarray(ATTN_SCALE, BF16)
        do2 = do_ref[...]
        prod = do2.astype(F32) * o_ref[...].astype(F32)
        lse2 = lse_ref[...]
        cc = cc_ref[...]
        qms = [_pick(masks[e], q2) for e in range(2)]
        doms = [_pick(masks[e], do2) for e in range(2)]
        deltas = [jnp.sum(jnp.where(masks[e], prod, 0.0), axis=1, keepdims=True) for e in range(2)]
        lses = [jnp.max(jnp.where(masks[e], lse2, -jnp.inf), axis=1, keepdims=True) for e in range(2)]
        cqs = [jnp.sum(jnp.where(lane == 2 * hp + e, cc, 0.0), axis=1, keepdims=True) for e in range(2)]
        qts = [jnp.transpose(qms[e].astype(F32)).astype(BF16) for e in range(2)]
        dots = [jnp.transpose(doms[e].astype(F32)).astype(BF16) for e in range(2)]
        ones_row = [HEAD_DIM * (1 - e) for e in range(2)]
        trow = lax.broadcasted_iota(jnp.int32, (LANES, TQ), 0)
        qts = [jnp.where(trow == ones_row[e], jnp.ones_like(qts[e]), qts[e]) for e in range(2)]

        def tile(kb, carry, masked, width=1):
            off, span = pl.multiple_of(kb * TQ, TQ), width * TQ
            k2, v2 = k_ref[pl.ds(off, span), :], v_ref[pl.ds(off, span), :]
            head0 = lax.broadcasted_iota(jnp.int32, (span, LANES), 1) < HEAD_DIM
            new, dks, dv = [], [], None
            for e in range(2):
                dq = carry[e]
                sc = _dot_nt(qms[e], k2) + (cqs[e] - cr_ref[e:e + 1, pl.ds(off, span)])
                p = jnp.exp(sc - lses[e])
                if masked:
                    p = jnp.where(_diag_mask(width, False), p, 0.0)
                ds = p * (_dot_nt(doms[e], v2) - deltas[e])
                dsb = ds.astype(BF16)
                dk_e = jnp.dot(qts[e], dsb, preferred_element_type=F32)
                dv_e = jnp.dot(dots[e], p.astype(BF16), preferred_element_type=F32)
                dks.append(dk_e)
                dv = dv_e if e == 0 else dv + dv_e
                dcr_ref[e:e + 1, pl.ds(off, span)] -= dk_e[ones_row[e]:ones_row[e] + 1, :]
                k_ones = jnp.where(head0 if e == 0 else ~head0, k2, jnp.ones_like(k2))
                new.append(dq + jnp.dot(dsb, k_ones, preferred_element_type=F32))
            krow = lax.broadcasted_iota(jnp.int32, (LANES, span), 0)
            dk_acc[:, pl.ds(off, span)] += jnp.where(krow < HEAD_DIM, dks[0], dks[1])
            dv_acc[:, pl.ds(off, span)] += dv
            return tuple(new)

        init = jnp.zeros((TQ, LANES), F32)
        carry = lax.fori_loop(0, qi // 2, lambda j, cr: tile(2 * j, cr, False, 2), (init, init))
        carry = lax.cond(qi % 2 == 1, lambda cr: tile(qi - 1, cr, True, 2), lambda cr: tile(qi, cr, True), carry)
        dq_ref[...] = (jnp.where(masks[0], carry[0], carry[1]) * ATTN_SCALE).astype(BF16)
        for e in range(2):
            dcr_ref[e:e + 1, pl.ds(pl.multiple_of(qi * TQ, TQ), TQ)] += jnp.transpose(carry[e])[
                ones_row[e]:ones_row[e] + 1, :]

        @pl.when(qi == nq - 1)
        def _():
            _write_transposed(dk_acc, dk_ref)
            _write_transposed(dv_acc, dv_ref)

    seq_spec = pl.BlockSpec((s, LANES), lambda hp, qi: (0, hp))
    return _ride_call(
        body, ride, name="fox_bwd", grid=(N_HEADS // 2, nq),
        in_specs=_qkv_specs(s, 0) + [pl.BlockSpec((TQ, LANES), lambda hp, qi: (qi, 0)),
                                     pl.BlockSpec((None, 2, s), lambda hp, qi: (hp, 0, 0)),
                                     _pair_spec(), _pair_spec(), _pair_spec()],
        out_specs=[_pair_spec(), seq_spec, seq_spec, pl.BlockSpec((None, 2, s), lambda hp, qi: (hp, 0, 0))],
        out_shape=[jax.ShapeDtypeStruct((s, ATTN_W), BF16)] * 3 + [jax.ShapeDtypeStruct((N_HEADS // 2, 2, s), F32)],
        scratch_shapes=[pltpu.VMEM((LANES, s), F32), pltpu.VMEM((LANES, s), F32)],
        sem=("parallel", "arbitrary"), args=(qkv, qkv, qkv, cum_col, cum_row, o, lse, do))


def _scan_matrix(reverse):
    row = lax.broadcasted_iota(jnp.int32, (SCAN_W, SCAN_W), 0)
    col = lax.broadcasted_iota(jnp.int32, (SCAN_W, SCAN_W), 1)
    return jnp.where((row > col) if reverse else (row < col), 1.0, 0.0).astype(BF16)


def _scan_cols(x, tri, reverse, init):
    nblk = x.shape[1] // SCAN_W
    parts, total = [None] * nblk, init
    far = 0 if reverse else SCAN_W - 1
    for b in (reversed(range(nblk)) if reverse else range(nblk)):
        blk = x[:, b * SCAN_W:(b + 1) * SCAN_W]
        part = jnp.dot(blk.astype(BF16), tri, preferred_element_type=F32)
        parts[b] = part + total
        total = total + (part[:, far:far + 1] + blk[:, far:far + 1])
    return (parts[0] if nblk == 1 else jnp.concatenate(parts, axis=1)), total


def _sb_logits(qm, k2):
    z = _dot_nt(qm, k2)
    neg_abs = lax.bitcast_convert_type(lax.bitcast_convert_type(z, jnp.uint32) | jnp.uint32(0x80000000), F32)
    soft = jnp.log(1.0 + jnp.exp(neg_abs))
    lb = jnp.minimum(z, 0.0) - soft
    return lb, lb - z


TILE_SLOTS = 4


def _tri_base(qi):
    return (qi * (qi + 1)) // 2


def _sb_fwd(qkv):
    s = qkv.shape[0]
    nq = s // TQ

    def body(q_ref, k_ref, v_ref, o_ref, t_ref, buf, sems):
        hp, qi = pl.program_id(0), pl.program_id(1)
        _, masks = _head_masks()
        suffix = _scan_matrix(True)
        q2 = q_ref[...] * jnp.asarray(ATTN_SCALE, BF16)
        qms = [_pick(masks[e], q2) for e in range(2)]
        base = _tri_base(qi)

        def store(e, kb):
            slot = kb % TILE_SLOTS
            return pltpu.make_async_copy(buf.at[e, slot], t_ref.at[2 * hp + e, base + kb], sems.at[e, slot])

        def tile(kb, carry, masked, width=1):
            off, span = pl.multiple_of(kb * TQ, TQ), width * TQ
            k2, v2 = k_ref[pl.ds(off, span), :], v_ref[pl.ds(off, span), :]
            new = []
            for e in range(2):
                run, acc = carry[e]
                lb, lo = _sb_logits(qms[e], k2)
                if masked:
                    strict = _diag_mask(width, True)
                    lo = jnp.where(strict, lo, 0.0)
                rest, run = _scan_cols(lo, suffix, True, run)
                a = jnp.exp(lb + rest)
                if masked:
                    a = jnp.where(strict, a, 0.0)
                ab, lbb = a.astype(BF16), lb.astype(BF16)
                acc = acc + jnp.dot(ab, v2, preferred_element_type=F32)
                for w in range(width):
                    blk = kb + w

                    @pl.when(blk + TILE_SLOTS <= qi)
                    def _(e=e, blk=blk):
                        store(e, blk + TILE_SLOTS).wait()
                    buf[e, blk % TILE_SLOTS, 0] = ab[:, w * TQ:(w + 1) * TQ]
                    buf[e, blk % TILE_SLOTS, 1] = lbb[:, w * TQ:(w + 1) * TQ]
                    store(e, blk).start()
                new.append((run, acc))
            return tuple(new)

        init = (jnp.zeros((TQ, 1), F32), jnp.zeros((TQ, LANES), F32))
        carry = lax.cond(qi % 2 == 1, lambda cr: tile(qi - 1, cr, True, 2), lambda cr: tile(qi, cr, True), (init, init))
        pairs = qi // 2
        carry = lax.fori_loop(0, pairs, lambda it, cr: tile(2 * (pairs - 1 - it), cr, False, 2), carry)
        for e in range(2):
            for blk in range(TILE_SLOTS):
                @pl.when(qi >= blk)
                def _(e=e, blk=blk):
                    store(e, blk).wait()
        o_ref[...] = jnp.where(masks[0], carry[0][1], carry[1][1]).astype(BF16)

    ntri = nq * (nq + 1) // 2
    return pl.pallas_call(
        body, name="sb_fwd", grid=(N_HEADS // 2, nq), in_specs=_qkv_specs(s, 3 * ATTN_W // LANES),
        out_specs=[_pair_spec(), ANY],
        out_shape=[jax.ShapeDtypeStruct((s, ATTN_W), BF16), jax.ShapeDtypeStruct((N_HEADS, ntri, 2, TQ, TQ), BF16)],
        scratch_shapes=[pltpu.VMEM((2, TILE_SLOTS, 2, TQ, TQ), BF16), pltpu.SemaphoreType.DMA((2, TILE_SLOTS))],
        compiler_params=_params(("arbitrary", "arbitrary")),
    )(qkv, qkv, qkv)


def _sb_bwd(qkv, tiles, do):
    s = qkv.shape[0]
    nq = s // TQ

    def body(q_ref, k_ref, v_ref, t_ref, do_ref, dq_ref, dk_ref, dv_ref, dk_acc, dv_acc, buf, sems):
        hp, qi = pl.program_id(0), pl.program_id(1)

        @pl.when(qi == 0)
        def _():
            dk_acc[...] = jnp.zeros_like(dk_acc)
            dv_acc[...] = jnp.zeros_like(dv_acc)

        _, masks = _head_masks()
        prefix = _scan_matrix(False)
        q2 = q_ref[...] * jnp.asarray(ATTN_SCALE, BF16)
        do2 = do_ref[...]
        qms = [_pick(masks[e], q2) for e in range(2)]
        doms = [_pick(masks[e], do2) for e in range(2)]
        qts = [jnp.transpose(qms[e].astype(F32)).astype(BF16) for e in range(2)]
        dots = [jnp.transpose(doms[e].astype(F32)).astype(BF16) for e in range(2)]
        base = _tri_base(qi)

        def fetch(e, kb):
            slot = kb % TILE_SLOTS
            return pltpu.make_async_copy(t_ref.at[2 * hp + e, base + kb], buf.at[e, slot], sems.at[e, slot])

        for e in range(2):
            fetch(e, 0).start()

            @pl.when(qi >= 1)
            def _(e=e):
                fetch(e, 1).start()

        def tile(kb, carry, masked, width=1):
            off, span = pl.multiple_of(kb * TQ, TQ), width * TQ
            k2, v2 = k_ref[pl.ds(off, span), :], v_ref[pl.ds(off, span), :]
            new, dk, dv = [], None, None
            for e in range(2):
                gsum, dq = carry[e]
                if not masked:
                    for blk in range(2, 2 + width):
                        @pl.when(kb + blk <= qi)
                        def _(e=e, blk=blk):
                            fetch(e, kb + blk).start()
                for w in range(width):
                    fetch(e, kb + w).wait()
                slots = [(kb + w) % TILE_SLOTS for w in range(width)]
                ab = buf[e, slots[0], 0] if width == 1 else jnp.concatenate([buf[e, sl, 0] for sl in slots], axis=1)
                lbb = buf[e, slots[0], 1] if width == 1 else jnp.concatenate([buf[e, sl, 1] for sl in slots], axis=1)
                beta = jnp.exp(lbb.astype(F32))
                g = ab.astype(F32) * _dot_nt(doms[e], v2)
                before, gsum = _scan_cols(g, prefix, False, gsum)
                dz = g - beta * (g + before)
                if masked:
                    dz = jnp.where(_diag_mask(width, True), dz, 0.0)
                dzb = dz.astype(BF16)
                dk_e = jnp.dot(qts[e], dzb, preferred_element_type=F32)
                dv_e = jnp.dot(dots[e], ab, preferred_element_type=F32)
                dk, dv = (dk_e, dv_e) if e == 0 else (dk + dk_e, dv + dv_e)
                new.append((gsum, dq + jnp.dot(dzb, k2, preferred_element_type=F32)))
            dk_acc[:, pl.ds(off, span)] += dk
            dv_acc[:, pl.ds(off, span)] += dv
            return tuple(new)

        init = (jnp.zeros((TQ, 1), F32), jnp.zeros((TQ, LANES), F32))
        carry = lax.fori_loop(0, qi // 2, lambda j, cr: tile(2 * j, cr, False, 2), (init, init))
        carry = lax.cond(qi % 2 == 1, lambda cr: tile(qi - 1, cr, True, 2), lambda cr: tile(qi, cr, True), carry)
        dq_ref[...] = (jnp.where(masks[0], carry[0][1], carry[1][1]) * ATTN_SCALE).astype(BF16)

        @pl.when(qi == nq - 1)
        def _():
            _write_transposed(dk_acc, dk_ref)
            _write_transposed(dv_acc, dv_ref)

    seq_spec = pl.BlockSpec((s, LANES), lambda hp, qi: (0, hp))
    return pl.pallas_call(
        body, name="sb_bwd", grid=(N_HEADS // 2, nq),
        in_specs=_qkv_specs(s, 3 * ATTN_W // LANES) + [ANY, _pair_spec()],
        out_specs=[_pair_spec(), seq_spec, seq_spec],
        out_shape=[jax.ShapeDtypeStruct((s, ATTN_W), BF16)] * 3,
        scratch_shapes=[pltpu.VMEM((LANES, s), F32), pltpu.VMEM((LANES, s), F32),
                        pltpu.VMEM((2, TILE_SLOTS, 2, TQ, TQ), BF16), pltpu.SemaphoreType.DMA((2, TILE_SLOTS))],
        compiler_params=_params(("arbitrary", "arbitrary")),
    )(qkv, qkv, qkv, tiles, do)


CONV_TR = 512


def _shift_down(x, halo, n):
    rolled = pltpu.roll(x, n, 0)
    head = rolled[0:8, :]
    rid = lax.broadcasted_iota(jnp.int32, head.shape, 0)
    for j in range(n):
        head = jnp.where(rid == j, halo[8 - n + j:8 - n + j + 1, :], head)
    return jnp.concatenate([head, rolled[8:, :]], axis=0)


def _shift_up(x, halo, n):
    rows = x.shape[0]
    rolled = pltpu.roll(x, rows - n, 0)
    tail = rolled[rows - 8:, :]
    rid = lax.broadcasted_iota(jnp.int32, tail.shape, 0)
    for j in range(n):
        tail = jnp.where(rid == 8 - n + j, halo[j:j + 1, :], tail)
    return jnp.concatenate([rolled[:rows - 8, :], tail], axis=0)


def _conv_fwd_block(x, halo, w, b):
    return b + _shift_down(x, halo, 2) * w[0:1, :] + _shift_down(x, halo, 1) * w[1:2, :] + x * w[2:3, :]


def _conv_specs(tr, s):
    pair = 2 * FF_HALF
    blk = pl.BlockSpec((tr, pair), lambda j, i: (i, j))
    prev = pl.BlockSpec((8, pair), lambda j, i: (jnp.maximum(i * (tr // 8) - 1, 0), j))
    nxt = pl.BlockSpec((8, pair), lambda j, i: (jnp.minimum((i + 1) * (tr // 8), s // 8 - 1), j))
    return blk, prev, nxt


def _conv_gate_fwd(hpre, conv_w, conv_b):
    s = hpre.shape[0]
    tr = min(CONV_TR, s)
    blk, prev, _ = _conv_specs(tr, s)

    def body(x_ref, halo_ref, w_ref, b_ref, a_ref):
        i = pl.program_id(1)
        halo = jnp.where(i > 0, halo_ref[...], 0.0)
        h = _conv_fwd_block(x_ref[...], halo, w_ref[...], b_ref[...])
        hg, hv = h[:, :FF_HALF], h[:, FF_HALF:]
        a_ref[...] = (hg * _sigmoid(hg) * hv).astype(BF16)

    return pl.pallas_call(
        body, name="conv_gate_fwd", grid=(2, s // tr),
        in_specs=[blk, prev, pl.BlockSpec((3, 2 * FF_HALF), lambda j, i: (0, j)),
                  pl.BlockSpec((1, 2 * FF_HALF), lambda j, i: (0, j))],
        out_specs=pl.BlockSpec((tr, FF_HALF), lambda j, i: (i, j)),
        out_shape=jax.ShapeDtypeStruct((s, D_FF), BF16),
        compiler_params=_params(("parallel", "parallel")),
    )(hpre, hpre, conv_w, conv_b)


def _conv_gate_bwd(hpre, da, conv_w, conv_b):
    s = hpre.shape[0]
    tr = min(CONV_TR, s)
    blk, prev, _ = _conv_specs(tr, s)

    def body(x_ref, halo_ref, da_ref, w_ref, b_ref, dh_ref, db_ref, dw_ref):
        i = pl.program_id(1)
        halo = jnp.where(i > 0, halo_ref[...], 0.0)
        x = x_ref[...]
        h = _conv_fwd_block(x, halo, w_ref[...], b_ref[...])
        hg, hv = h[:, :FF_HALF], h[:, FF_HALF:]
        da_blk = da_ref[...].astype(F32)
        sg = _sigmoid(hg)
        dhg = da_blk * hv * (sg * (1.0 + hg * (1.0 - sg)))
        dhv = da_blk * (hg * sg)
        dh_ref[:, :FF_HALF] = dhg.astype(BF16)
        dh_ref[:, FF_HALF:] = dhv.astype(BF16)
        x2, x1 = _shift_down(x, halo, 2), _shift_down(x, halo, 1)
        parts = []
        for lo, dpart in ((0, dhg), (FF_HALF, dhv)):
            cols = slice(lo, lo + FF_HALF)
            parts.append((cols, _colsum(dpart), _colsum(dpart * x2[:, cols]), _colsum(dpart * x1[:, cols]),
                          _colsum(dpart * x[:, cols])))

        @pl.when(i == 0)
        def _():
            for cols, db, dw0, dw1, dw2 in parts:
                db_ref[:, cols] = db
                dw_ref[0:1, cols] = dw0
                dw_ref[1:2, cols] = dw1
                dw_ref[2:3, cols] = dw2

        @pl.when(i > 0)
        def _():
            for cols, db, dw0, dw1, dw2 in parts:
                db_ref[:, cols] += db
                dw_ref[0:1, cols] += dw0
                dw_ref[1:2, cols] += dw1
                dw_ref[2:3, cols] += dw2

    pair = 2 * FF_HALF
    return pl.pallas_call(
        body, name="conv_gate_bwd", grid=(2, s // tr),
        in_specs=[blk, prev, pl.BlockSpec((tr, FF_HALF), lambda j, i: (i, j)),
                  pl.BlockSpec((3, pair), lambda j, i: (0, j)), pl.BlockSpec((1, pair), lambda j, i: (0, j))],
        out_specs=[blk, pl.BlockSpec((1, pair), lambda j, i: (0, j)), pl.BlockSpec((3, pair), lambda j, i: (0, j))],
        out_shape=[jax.ShapeDtypeStruct((s, 2 * D_FF), BF16), jax.ShapeDtypeStruct((1, 2 * D_FF), F32),
                   jax.ShapeDtypeStruct((3, 2 * D_FF), F32)],
        compiler_params=_params(("parallel", "arbitrary")),
    )(hpre, hpre, da, conv_w, conv_b)


def _conv_input_bwd(dh, conv_w):
    s = dh.shape[0]
    tr = min(CONV_TR, s)
    blk, _, _ = _conv_specs(tr, s)
    nblk = s // tr

    def body(x_ref, halo_ref, w_ref, o_ref):
        i = pl.program_id(1)
        halo = jnp.where(i < nblk - 1, halo_ref[...].astype(F32), 0.0)
        x, w = x_ref[...].astype(F32), w_ref[...]
        o_ref[...] = (x * w[2:3, :] + _shift_up(x, halo, 1) * w[1:2, :] + _shift_up(x, halo, 2) * w[0:1, :]).astype(BF16)

    nxt = pl.BlockSpec((16, 2 * FF_HALF), lambda j, i: (jnp.minimum((i + 1) * (tr // 16), s // 16 - 1), j))
    return pl.pallas_call(
        body, name="conv_input_bwd", grid=(2, nblk),
        in_specs=[blk, nxt, pl.BlockSpec((3, 2 * FF_HALF), lambda j, i: (0, j))], out_specs=blk,
        out_shape=jax.ShapeDtypeStruct((s, 2 * D_FF), BF16),
        compiler_params=_params(("parallel", "parallel")),
    )(dh, dh, conv_w)


def _adamw_math(w, g, m, v):
    m = ADAM_B1 * m + (1.0 - ADAM_B1) * g
    v = ADAM_B2 * v + (1.0 - ADAM_B2) * (g * g)
    m_hat = m / (1.0 - ADAM_B1 ** ADAM_STEP)
    v_hat = v / (1.0 - ADAM_B2 ** ADAM_STEP)
    delta = -ADAM_LR * (m_hat / (jnp.sqrt(v_hat) + ADAM_EPS) + ADAM_WD * w)
    return delta, m, v


def _adamw(name, g8, w, m, v):
    r, c = w.shape
    tr = _row_tile(r, c)

    def body(g_ref, w_ref, m_ref, v_ref, go_ref, d_ref, mo_ref, vo_ref):
        g = g_ref[0].astype(F32)
        for d in range(1, N_DEV):
            g = g + g_ref[d].astype(F32)
        delta, mn, vn = _adamw_math(w_ref[...], g, m_ref[...], v_ref[...])
        go_ref[...] = g
        d_ref[...] = delta
        mo_ref[...] = mn
        vo_ref[...] = vn

    spec = pl.BlockSpec((tr, c), lambda i: (i, 0))
    return pl.pallas_call(
        body, name=name, grid=(r // tr,),
        in_specs=[pl.BlockSpec((N_DEV, tr, c), lambda i: (0, i, 0)), spec, spec, spec], out_specs=[spec] * 4,
        out_shape=[jax.ShapeDtypeStruct((r, c), F32)] * 4, compiler_params=_params(("parallel",)),
    )(g8, w, m, v)


def _adamw_small(g8, ws, ms, vs):
    n = len(ws)
    offsets = [sum(w.shape[0] for w in ws[:i]) for i in range(n)]

    def body(*refs):
        g_ref, w_refs, m_refs, v_refs = refs[0], refs[1:1 + n], refs[1 + n:1 + 2 * n], refs[1 + 2 * n:1 + 3 * n]
        outs = refs[1 + 3 * n:]
        for i in range(n):
            rows = w_refs[i].shape[0]
            g = g_ref[0, offsets[i]:offsets[i] + rows, :]
            for d in range(1, N_DEV):
                g = g + g_ref[d, offsets[i]:offsets[i] + rows, :]
            delta, mn, vn = _adamw_math(w_refs[i][...], g, m_refs[i][...], v_refs[i][...])
            for k, val in enumerate((g, delta, mn, vn)):
                outs[k * n + i][...] = val

    vmem = pl.BlockSpec(memory_space=pltpu.VMEM)
    res = pl.pallas_call(
        body, name="adamw_small", in_specs=[vmem] * (1 + 3 * n), out_specs=[vmem] * (4 * n),
        out_shape=[jax.ShapeDtypeStruct(w.shape, F32) for _ in range(4) for w in ws], compiler_params=_params(),
    )(g8, *ws, *ms, *vs)
    return [res[k * n:(k + 1) * n] for k in range(4)]


def _adamw_ada(c_t, dmod, w, m, v):
    r, c = w.shape
    tr = _row_tile(r, c)

    def body(ct_ref, dm_ref, w_ref, m_ref, v_ref, go_ref, d_ref, mo_ref, vo_ref):
        ct, dm = ct_ref[...], dm_ref[...]
        g = ct[:, 0:1] * dm[0:1, :]
        for b in range(1, N_DEV):
            g = g + ct[:, b:b + 1] * dm[b:b + 1, :]
        delta, mn, vn = _adamw_math(w_ref[...], g, m_ref[...], v_ref[...])
        go_ref[...] = g
        d_ref[...] = delta
        mo_ref[...] = mn
        vo_ref[...] = vn

    spec = pl.BlockSpec((tr, c), lambda i: (i, 0))
    return pl.pallas_call(
        body, name="adamw_w_ada", grid=(r // tr,),
        in_specs=[pl.BlockSpec((tr, N_DEV), lambda i: (i, 0)), pl.BlockSpec((N_DEV, c), lambda i: (0, 0)),
                  spec, spec, spec],
        out_specs=[spec] * 4, out_shape=[jax.ShapeDtypeStruct((r, c), F32)] * 4,
        compiler_params=_params(("parallel",)),
    )(c_t, dmod, w, m, v)


def _cols_from_slots(g):
    n, r, c = g.shape
    return jnp.transpose(g, (1, 0, 2)).reshape(r, n * c)


def _cols_to_slots(w):
    r, c = w.shape
    return jnp.transpose(w.reshape(r, N_DEV, c // N_DEV), (1, 0, 2))


def _pair_cols(w):
    g0, g1 = w[..., 0:FF_HALF], w[..., FF_HALF:D_FF]
    v0, v1 = w[..., D_FF:D_FF + FF_HALF], w[..., D_FF + FF_HALF:]
    return jnp.concatenate([g0, v0, g1, v1], axis=-1)


def _unpair_cols(w):
    g0, v0 = w[..., 0:FF_HALF], w[..., FF_HALF:D_FF]
    g1, v1 = w[..., D_FF:D_FF + FF_HALF], w[..., D_FF + FF_HALF:]
    return jnp.concatenate([g0, g1, v0, v1], axis=-1)


def _row(v):
    return v.reshape(1, -1)


def kernel(x, c, w_ada, b_ada, w_in, b_forget, w_fox_proj, w_sb_proj, w_o, ln1_g, ln1_b, w_up, conv_w, conv_b, w_down, ln2_g, ln2_b, loss_target, m_w_ada, m_b_ada, m_w_in, m_b_forget, m_w_fox_proj, m_w_sb_proj, m_w_o, m_ln1_g, m_ln1_b, m_w_up, m_conv_w, m_conv_b, m_w_down, m_ln2_g, m_ln2_b, v_w_ada, v_b_ada, v_w_in, v_b_forget, v_w_fox_proj, v_w_sb_proj, v_w_o, v_ln1_g, v_ln1_b, v_w_up, v_conv_w, v_conv_b, v_w_down, v_ln2_g, v_ln2_b):
    s = x.shape[1]
    me = 4 * lax.axis_index("x") + 2 * lax.axis_index("y") + lax.axis_index("c")
    x2 = x.reshape(s, D_MODEL)
    tgt = loss_target.reshape(s, D_MODEL)

    b_ada_loc = lax.dynamic_slice(b_ada, (me * ADA_SHARD,), (ADA_SHARD,)).reshape(1, ADA_SHARD)
    c_all, mod = _mod_exchange(c, w_ada, b_ada_loc)
    mod = mod.reshape(N_MOD, 1, D_MODEL)
    sh1, sc1, gt1, sh2, sc2, gt2 = [mod[i] for i in range(N_MOD)]

    g_in = _allgather_two_level("ag_w_in", w_in.astype(BF16))
    late_weights = _Ride([w_fox_proj.astype(BF16), w_sb_proj.astype(BF16), w_o.astype(BF16), w_up.astype(BF16),
                          w_down.astype(BF16), conv_w], scatter=False)
    w_in_f = _cols_from_slots(g_in)
    w_proj = jnp.concatenate(
        [w_in_f[:, 0:1536], w_in_f[:, 1544:3080], w_in_f[:, 3080:5128], w_in_f[:, 1536:1544],
         jnp.zeros((D_MODEL, W_PROJ - 5128), BF16)], axis=1)
    w_qkv, w_gates, w_f = w_proj[:, :W_QKV], w_proj[:, W_QKV:W_QKV + W_GATES], w_proj[:, W_QKV + W_GATES:W_QKV + W_GATES + W_F]
    conv_b_p = _pair_cols(_row(conv_b))
    b_f_pad = jnp.pad(_row(b_forget), ((0, 0), (0, LANES - N_HEADS)))

    (u1,) = _rowwise("modulate1", lambda xb, sc, sh: (xb * (1.0 + sc) + sh,),
                     [(x2, D_MODEL, 0)], [sc1, sh1], [(D_MODEL, BF16)], tr=512)
    qkv = _mm(u1, w_qkv, name="mm_qkv", out_dtype=BF16)
    gates = _mm(u1, w_gates, name="mm_gates")
    f_raw = _mm(u1, w_f, name="mm_forget")
    cum_col = _forget_cumsum(f_raw, b_f_pad)
    cum_row = jnp.transpose(cum_col[:, :N_HEADS]).reshape(N_HEADS // 2, 2, s)
    (y_fox, y_fox32, lse), (g_fox, g_sb, g_o, g_up, g_down, g_cw) = _fox_fwd(qkv, cum_col, cum_row, ride=late_weights)
    w_fox_f = _cols_from_slots(g_fox)
    w_sb_f = _cols_from_slots(g_sb)
    w_o_f = g_o.reshape(D_MODEL, D_MODEL)
    w_up_p = _pair_cols(_cols_from_slots(g_up))
    w_down_f = g_down.reshape(D_FF, D_MODEL)
    conv_w_p = _pair_cols(_cols_from_slots(g_cw))
    y_sb, sb_run = _sb_fwd(qkv)
    pf = _mm(y_fox, w_fox_f, name="mm_fox_proj", out_dtype=BF16)
    ps = _mm(y_sb, w_sb_f, name="mm_sb_proj", out_dtype=BF16)
    (merged,) = _rowwise("gate_merge", lambda ga, gb, a, b: (_sigmoid(ga) * a + _sigmoid(gb) * b,),
                         [(gates, D_MODEL, 0), (gates, D_MODEL, 1), (pf, D_MODEL, 0), (ps, D_MODEL, 0)], [],
                         [(D_MODEL, BF16)])
    attn_out = _mm(merged, w_o_f, name="mm_w_o")

    def ln_fwd(xb, fb, gt, g, b):
        xhat, _ = _ln_stats(ALPHA * xb + (1.0 + gt) * fb)
        return xhat * g + b

    def ln1_mod(xb, fb, gt, g, b, sc, sh):
        y = ln_fwd(xb, fb, gt, g, b)
        return y, y * (1.0 + sc) + sh

    x1, u2 = _rowwise("ln1_modulate2", ln1_mod, [(x2, D_MODEL, 0), (attn_out, D_MODEL, 0)],
                      [gt1, _row(ln1_g), _row(ln1_b), sc2, sh2], [(D_MODEL, F32), (D_MODEL, BF16)])

    hpre = _mm(u2, w_up_p, name="mm_w_up", tn=1408)
    act = _conv_gate_fwd(hpre, conv_w_p, conv_b_p)
    ffn_out = _mm(act, w_down_f, name="mm_w_down", tk=2816)

    def ln2_bwd(xb, fb, tb, gt, g, b):
        xhat, rstd = _ln_stats(ALPHA * xb + (1.0 + gt) * fb)
        err = (xhat * g + b) - tb
        dy = err * (1.0 / D_MODEL)
        dr = _ln_bwd(dy, xhat, rstd, g)
        return (dr * (1.0 + gt), ALPHA * dr,
                _colsum(err * err), _colsum(dy * xhat), _colsum(dy), _colsum(dr * fb))

    dffn, dx1_res, sq_err, d_ln2_g, d_ln2_b, d_gt2 = _rowwise(
        "ln2_bwd", ln2_bwd, [(x1, D_MODEL, 0), (ffn_out, D_MODEL, 0), (tgt, D_MODEL, 0)],
        [gt2, _row(ln2_g), _row(ln2_b)], [(D_MODEL, BF16), (D_MODEL, F32)], sums=[D_MODEL] * 4)
    loss_part = jnp.broadcast_to(0.5 * jnp.sum(sq_err) / D_MODEL, (1, LANES))

    d_w_down = _mm(act, dffn, name="mm_d_w_down", ta=True, tm=1408, tk=2048, out_dtype=BF16)
    d_act = _mm(dffn, w_down_f, name="mm_d_act", tb=True, tn=1408, out_dtype=BF16)
    dh, d_conv_b_p, d_conv_w_p = _conv_gate_bwd(hpre, d_act, conv_w_p, conv_b_p)
    dhpre = _conv_input_bwd(dh, conv_w_p)
    d_w_up_p = _mm(u2, dhpre, name="mm_d_w_up", ta=True, tn=1408, tk=2048, out_dtype=BF16)
    du2 = _mm(dhpre, w_up_p, name="mm_d_u2", tb=True, tk=2816)

    def ln1_bwd(du, dres, x1b, xb, fb, sc, gt, g):
        dx1 = dres + du * (1.0 + sc)
        xhat, rstd = _ln_stats(ALPHA * xb + (1.0 + gt) * fb)
        dr = _ln_bwd(dx1, xhat, rstd, g)
        return (dr * (1.0 + gt), ALPHA * dr,
                _colsum(du * x1b), _colsum(du), _colsum(dx1 * xhat), _colsum(dx1), _colsum(dr * fb))

    d_attn, dx_res, d_sc2, d_sh2, d_ln1_g, d_ln1_b, d_gt1 = _rowwise(
        "ln1_bwd", ln1_bwd,
        [(du2, D_MODEL, 0), (dx1_res, D_MODEL, 0), (x1, D_MODEL, 0), (x2, D_MODEL, 0), (attn_out, D_MODEL, 0)],
        [sc2, gt1, _row(ln1_g)], [(D_MODEL, BF16), (D_MODEL, F32)], sums=[D_MODEL] * 5)

    d_w_o = _mm(merged, d_attn, name="mm_d_w_o", ta=True, out_dtype=BF16)
    d_merged = _mm(d_attn, w_o_f, name="mm_d_merged", tb=True, out_dtype=BF16)

    def merge_bwd(dm, ga, gb, a, b):
        dm, a, b = dm.astype(F32), a.astype(F32), b.astype(F32)
        sa, sb = _sigmoid(ga), _sigmoid(gb)
        return dm * a * sa * (1.0 - sa), dm * b * sb * (1.0 - sb), dm * sa, dm * sb

    d_ga, d_gb, d_pf, d_ps = _rowwise(
        "gate_merge_bwd", merge_bwd,
        [(d_merged, D_MODEL, 0), (gates, D_MODEL, 0), (gates, D_MODEL, 1), (pf, D_MODEL, 0), (ps, D_MODEL, 0)], [],
        [(D_MODEL, BF16)] * 4)
    d_w_fox = _mm(y_fox, d_pf, name="mm_d_w_fox", ta=True, out_dtype=BF16)
    d_w_sb = _mm(y_sb, d_ps, name="mm_d_w_sb", ta=True, out_dtype=BF16)
    d_y_fox = _mm(d_pf, w_fox_f, name="mm_d_y_fox", tb=True, out_dtype=BF16)
    d_y_sb = _mm(d_ps, w_sb_f, name="mm_d_y_sb", tb=True, out_dtype=BF16)
    early_grads = _Ride(
        [_cols_to_slots(d_w_fox), _cols_to_slots(d_w_sb), d_w_o.reshape(N_DEV, D_MODEL // N_DEV, D_MODEL),
         _cols_to_slots(_unpair_cols(d_w_up_p)), d_w_down.reshape(N_DEV, D_FF // N_DEV, D_MODEL)], scatter=True)
    (dq_a, dk_a, dv_a, d_cum_row), early_slots = _fox_bwd(qkv, cum_col, cum_row, y_fox32, lse, d_y_fox,
                                                                    ride=early_grads)
    dq_b, dk_b, dv_b = _sb_bwd(qkv, sb_run, d_y_sb)
    d_cum = jnp.transpose(d_cum_row.reshape(N_HEADS, s))
    d_cum = jnp.pad(d_cum, ((0, 0), (0, LANES - N_HEADS)))
    d_f, d_b_forget = _forget_bwd(d_cum, f_raw, b_f_pad)
    d_qkv = [dq_a, dk_a, dv_a, dq_b, dk_b, dv_b]
    g_qkv = list(_mm_pieces_tn(u1, d_qkv, name="mm_d_w_in_qkv"))
    g_ga, g_gb, g_f = _mm_pieces_tn(u1, [d_ga, d_gb, d_f], name="mm_d_w_in_gates")
    d_w_in_f = jnp.concatenate(g_qkv[:3] + [g_f[:, :N_HEADS]] + g_qkv[3:] + [g_ga, g_gb], axis=1)
    def x_bwd(du, dres, xb, sc):
        return dres + du * (1.0 + sc), _colsum(du * xb), _colsum(du)

    (grad_x, d_sc1, d_sh1), (in_slots,) = _mm_pieces_nt(
        d_qkv + [d_ga, d_gb, d_f], w_proj, name="mm_d_u1_x_bwd", epilogue=x_bwd, rows=[dx_res, x2], vecs=[sc1],
        n_sums=2, ride=_Ride([_cols_to_slots(d_w_in_f)], scatter=True))

    d_conv_b = _unpair_cols(d_conv_b_p)
    d_conv_w = _unpair_cols(d_conv_w_p)
    cb_pad = N_MOD * D_MODEL - 2 * D_FF
    small = jnp.concatenate(
        [d_sh1, d_sc1, d_gt1, d_sh2, d_sc2, d_gt2, jnp.pad(d_b_forget, ((0, 0), (0, D_MODEL - LANES))),
         d_ln1_g, d_ln1_b, d_ln2_g, d_ln2_b, jnp.pad(d_conv_b, ((0, 0), (0, cb_pad))),
         d_conv_w.reshape(1, 6 * D_FF), loss_part], axis=1)
    n_small = small.shape[1] // LANES - 1
    small = jnp.pad(small.reshape(n_small + 1, LANES), ((0, -(n_small + 1) % 8), (0, 0)))
    (small_all,) = _exchange("ag_small_grads", [small], scatter=False)
    loss = jnp.sum(small_all[:, n_small, 0])
    n_rep = n_small - 6 * D_FF // LANES
    cw8 = small_all[:, n_rep:n_small, :].reshape(N_DEV, 3, 2 * D_FF)
    cw8 = lax.dynamic_slice(cw8, (0, 0, me * UP_SHARD), (N_DEV, 3, UP_SHARD))
    dmod8 = small_all[:, :N_MOD * D_MODEL // LANES, :].reshape(N_DEV, N_MOD * D_MODEL)
    dmod_loc = lax.dynamic_slice(dmod8, (0, me * ADA_SHARD), (N_DEV, ADA_SHARD))

    def rows_of(a):
        return a.reshape(-1, LANES)

    def forget_rows(a):
        return rows_of(jnp.pad(a, (0, D_MODEL - N_HEADS)))

    def conv_b_rows(a):
        return rows_of(jnp.pad(a, (0, cb_pad)))

    rep_sizes = {"b_ada": N_MOD * D_MODEL, "b_forget": N_HEADS, "ln1_g": D_MODEL, "ln1_b": D_MODEL, "ln2_g": D_MODEL,
                 "ln2_b": D_MODEL, "conv_b": 2 * D_FF}
    rep_w = [rows_of(b_ada), forget_rows(b_forget), rows_of(ln1_g), rows_of(ln1_b), rows_of(ln2_g), rows_of(ln2_b),
             conv_b_rows(conv_b)]
    rep_m = [rows_of(m_b_ada), forget_rows(m_b_forget), rows_of(m_ln1_g), rows_of(m_ln1_b), rows_of(m_ln2_g),
             rows_of(m_ln2_b), conv_b_rows(m_conv_b)]
    rep_v = [rows_of(v_b_ada), forget_rows(v_b_forget), rows_of(v_ln1_g), rows_of(v_ln1_b), rows_of(v_ln2_g),
             rows_of(v_ln2_b), conv_b_rows(v_conv_b)]
    rep_out = _adamw_small(small_all, rep_w, rep_m, rep_v)
    rep = [{name: a.reshape(-1)[:size] for (name, size), a in zip(rep_sizes.items(), outs)} for outs in rep_out]
    r_conv_w = _adamw("adamw_conv_w", cw8, conv_w, m_conv_w, v_conv_w)
    r_ada = _adamw_ada(jnp.transpose(c_all.reshape(N_DEV, D_MODEL)), dmod_loc, w_ada, m_w_ada, v_w_ada)

    r_in = _adamw("adamw_w_in", in_slots, w_in, m_w_in, v_w_in)
    r_fox = _adamw("adamw_w_fox", early_slots[0], w_fox_proj, m_w_fox_proj, v_w_fox_proj)
    r_sb = _adamw("adamw_w_sb", early_slots[1], w_sb_proj, m_w_sb_proj, v_w_sb_proj)
    r_o = _adamw("adamw_w_o", early_slots[2], w_o, m_w_o, v_w_o)
    r_up = _adamw("adamw_w_up", early_slots[3], w_up, m_w_up, v_w_up)
    r_down = _adamw("adamw_w_down", early_slots[4], w_down, m_w_down, v_w_down)

    def leaf(i):
        return [r_ada[i], rep[i]["b_ada"], r_in[i], rep[i]["b_forget"], r_fox[i], r_sb[i], r_o[i], rep[i]["ln1_g"],
                rep[i]["ln1_b"], r_up[i], r_conv_w[i], rep[i]["conv_b"], r_down[i], rep[i]["ln2_g"], rep[i]["ln2_b"]]

    return (loss, grad_x.reshape(1, s, D_MODEL), *leaf(0), *leaf(1), *leaf(2), *leaf(3))
```

```python
import functools

import jax
import jax.numpy as jnp
from jax import lax
from jax.experimental import pallas as pl
from jax.experimental.pallas import tpu as pltpu

F32 = jnp.float32
BF16 = jnp.bfloat16
MESH = pl.DeviceIdType.MESH
ANY = pl.BlockSpec(memory_space=pl.ANY)

N_DEV = 8
D_MODEL = 1024
HEAD_DIM = 64
N_HEADS = 8
ATTN_W = N_HEADS * HEAD_DIM
D_FF = 2816
FF_HALF = D_FF // 2
N_MOD = 6
ADA_SHARD = N_MOD * D_MODEL // N_DEV
IN_SHARD = 641
UP_SHARD = 704
ATTN_SCALE = HEAD_DIM ** -0.5
ALPHA = 2.0 ** 0.25
LN_EPS = 1e-5
LANES = 128
TQ = 512
SCAN_W = 256
VMEM_LIMIT = 56 * 1024 * 1024

ADAM_LR, ADAM_B1, ADAM_B2, ADAM_EPS, ADAM_WD, ADAM_STEP = 0.001, 0.9, 0.999, 1e-08, 0.01, 10

W_QKV, W_GATES, W_F = 3072, 2048, 128
W_PROJ = 5376


def _params(sem=None):
    return pltpu.CompilerParams(dimension_semantics=sem, vmem_limit_bytes=VMEM_LIMIT)


def _tile(n, cap):
    if n <= cap:
        return n
    best = None
    for t in range(LANES, cap + 1, LANES):
        if n % t == 0:
            best = t
    assert best is not None, (n, cap)
    return best


def _row_tile(r, width, budget=192 * 1024):
    if r * width <= budget or r % 16:
        return r
    best = 16
    for t in range(16, r + 1, 16):
        if r % t == 0 and t * width <= budget:
            best = t
    return best


def _me():
    x, y, c = lax.axis_index("x"), lax.axis_index("y"), lax.axis_index("c")
    return x, y, c, 4 * x + 2 * y + c


def _peer(r):
    x, y, c, _ = _me()
    px = 1 - x if r & 4 else x
    py = 1 - y if r & 2 else y
    pc = 1 - c if r & 1 else c
    return (px, py, pc), 4 * px + 2 * py + pc


class _Ride:
    def __init__(self, arrays, scatter):
        self.arrays, self.scatter, self.n = list(arrays), scatter, len(arrays)
        self.in_specs = [ANY] * self.n
        self.out_specs = [ANY] * self.n
        self.out_shape = [jax.ShapeDtypeStruct(a.shape if scatter else (N_DEV,) + a.shape, a.dtype) for a in arrays]
        self.scratch = [pltpu.SemaphoreType.DMA((self.n, N_DEV - 1)), pltpu.SemaphoreType.DMA((self.n, N_DEV - 1)),
                        pltpu.SemaphoreType.DMA((self.n,))]

    def _local(self, ins, outs, sems, a):
        me = _me()[3]
        return pltpu.make_async_copy(ins[a].at[me] if self.scatter else ins[a], outs[a].at[me], sems[2].at[a])

    def _remote(self, ins, outs, sems, a, r, arriving):
        me = _me()[3]
        peer, pidx = _peer(r)
        src = ins[a].at[me if arriving else pidx] if self.scatter else ins[a]
        return pltpu.make_async_remote_copy(
            src_ref=src, dst_ref=outs[a].at[pidx if arriving else me], send_sem=sems[0].at[a, r - 1],
            recv_sem=sems[1].at[a, r - 1], device_id=peer, device_id_type=MESH)

    def start(self, ins, outs, sems):
        for a in range(self.n):
            self._local(ins, outs, sems, a).start()
        for r in range(1, N_DEV):
            for a in range(self.n):
                self._remote(ins, outs, sems, a, r, False).start()

    def wait(self, ins, outs, sems):
        for r in range(1, N_DEV):
            for a in range(self.n):
                self._remote(ins, outs, sems, a, r, True).wait_recv()
        for r in range(1, N_DEV):
            for a in range(self.n):
                self._remote(ins, outs, sems, a, r, False).wait_send()
        for a in range(self.n):
            self._local(ins, outs, sems, a).wait()


def _exchange(name, arrays, scatter):
    ride = _Ride(arrays, scatter)

    def body(*refs):
        ins, outs, sems = refs[:ride.n], refs[ride.n:2 * ride.n], refs[2 * ride.n:]
        ride.start(ins, outs, sems)
        ride.wait(ins, outs, sems)

    return pl.pallas_call(body, name=name, in_specs=ride.in_specs, out_specs=ride.out_specs, out_shape=ride.out_shape,
                          scratch_shapes=ride.scratch)(*arrays)


def _allgather_two_level(name, a):
    def body(a_ref, out_ref, send_sems, recv_sems, local_sem):
        x, y, c, me = _me()
        sibling = (x, y, 1 - c)
        chips = [(1 - x, y), (x, 1 - y), (1 - x, 1 - y)]

        def idx(px, py, pc):
            return 4 * px + 2 * py + pc

        def copy(k, block, to, src=None):
            slot = out_ref.at[idx(*block)]
            return pltpu.make_async_remote_copy(
                src_ref=slot if src is None else src, dst_ref=slot, send_sem=send_sems.at[k], recv_sem=recv_sems.at[k],
                device_id=to, device_id_type=MESH)

        mine = pltpu.make_async_copy(a_ref, out_ref.at[me], local_sem)
        mine.start()
        first = [copy(0, (x, y, c), sibling, src=a_ref)]
        first += [copy(1 + j, (x, y, c), (*chip, c), src=a_ref) for j, chip in enumerate(chips)]
        for cp in first:
            cp.start()
        passed = [copy(4 + j, (*chip, c), sibling) for j, chip in enumerate(chips)]
        for j, chip in enumerate(chips):
            copy(1 + j, (*chip, c), (x, y, c)).wait_recv()
            passed[j].start()
        copy(0, sibling, (x, y, c)).wait_recv()
        for j, chip in enumerate(chips):
            copy(4 + j, (*chip, 1 - c), (x, y, c)).wait_recv()
        for cp in first + passed:
            cp.wait_send()
        mine.wait()

    return pl.pallas_call(
        body, name=name, in_specs=[ANY], out_specs=ANY,
        out_shape=jax.ShapeDtypeStruct((N_DEV,) + a.shape, a.dtype),
        scratch_shapes=[pltpu.SemaphoreType.DMA((N_DEV - 1,)), pltpu.SemaphoreType.DMA((N_DEV - 1,)),
                        pltpu.SemaphoreType.DMA],
    )(a)


def _with_ride(body, ride, n_in, n_out, grid):
    if ride is None:
        return body
    n = ride.n

    def wrapped(*refs):
        ins, rins = refs[:n_in], refs[n_in:n_in + n]
        outs, routs = refs[n_in + n:n_in + n + n_out], refs[n_in + n + n_out:n_in + 2 * n + n_out]
        rest = refs[n_in + 2 * n + n_out:]
        scratch, sems = rest[:len(rest) - 3], rest[len(rest) - 3:]
        ids = [pl.program_id(d) for d in range(len(grid))]
        first = functools.reduce(lambda p, q: p & q, [i == 0 for i in ids])
        last = functools.reduce(lambda p, q: p & q, [i == g - 1 for i, g in zip(ids, grid)])

        @pl.when(first)
        def _():
            ride.start(rins, routs, sems)

        body(*ins, *outs, *scratch)

        @pl.when(last)
        def _():
            ride.wait(rins, routs, sems)

    return wrapped


def _ride_call(body, ride, *, name, grid, in_specs, out_specs, out_shape, scratch_shapes, sem, args):
    n_in, n_out = len(in_specs), len(out_specs)
    if ride is None:
        res = pl.pallas_call(body, name=name, grid=grid, in_specs=in_specs, out_specs=out_specs, out_shape=out_shape,
                             scratch_shapes=scratch_shapes, compiler_params=_params(sem))(*args)
        return list(res), []
    res = pl.pallas_call(
        _with_ride(body, ride, n_in, n_out, grid), name=name, grid=grid,
        in_specs=list(in_specs) + ride.in_specs, out_specs=list(out_specs) + ride.out_specs,
        out_shape=list(out_shape) + ride.out_shape, scratch_shapes=list(scratch_shapes) + ride.scratch,
        compiler_params=_params(("arbitrary",) * len(grid)))(*args, *ride.arrays)
    return list(res[:n_out]), list(res[n_out:])


def _mm(a, b, *, name, ta=False, tb=False, out_dtype=F32, tm=1024, tn=1024, tk=1024, ride=None):
    m, k = (a.shape[1], a.shape[0]) if ta else a.shape
    n = b.shape[0] if tb else b.shape[1]
    assert (b.shape[1] if tb else b.shape[0]) == k
    tm, tn, tk = _tile(m, tm), _tile(n, tn), _tile(k, tk)
    nk = k // tk
    a_spec = pl.BlockSpec((tk, tm), lambda i, j, l: (l, i)) if ta else pl.BlockSpec((tm, tk), lambda i, j, l: (i, l))
    b_spec = pl.BlockSpec((tn, tk), lambda i, j, l: (j, l)) if tb else pl.BlockSpec((tk, tn), lambda i, j, l: (l, j))
    dims = (((0,) if ta else (1,), (1,) if tb else (0,)), ((), ()))

    def body(a_ref, b_ref, o_ref, *acc):
        p = lax.dot_general(a_ref[...].astype(BF16), b_ref[...].astype(BF16), dims, preferred_element_type=F32)
        if nk == 1:
            o_ref[...] = p.astype(out_dtype)
            return
        acc_ref = acc[0]
        step = pl.program_id(2)

        @pl.when(step == 0)
        def _():
            acc_ref[...] = p

        @pl.when(step > 0)
        def _():
            acc_ref[...] += p

        @pl.when(step == nk - 1)
        def _():
            o_ref[...] = acc_ref[...].astype(out_dtype)

    outs, rode = _ride_call(
        body, ride, name=name, grid=(m // tm, n // tn, nk), in_specs=[a_spec, b_spec],
        out_specs=[pl.BlockSpec((tm, tn), lambda i, j, l: (i, j))], out_shape=[jax.ShapeDtypeStruct((m, n), out_dtype)],
        scratch_shapes=[] if nk == 1 else [pltpu.VMEM((tm, tn), F32)], sem=("parallel", "parallel", "arbitrary"),
        args=(a, b))
    return outs[0] if ride is None else (outs[0], rode)


def _mm_pieces_nt(pieces, w, *, name, epilogue, rows, vecs, out_dtypes=(F32,), n_sums=0, tm=512, w_kn=False, ride=None):
    s, n = pieces[0].shape[0], (w.shape[1] if w_kn else w.shape[0])
    tm = min(tm, s)
    widths = [p.shape[1] for p in pieces]
    offs = [sum(widths[:i]) for i in range(len(pieces))]
    rows = [r if isinstance(r, tuple) else (r, 0) for r in rows]
    n_p, n_r, n_v, n_o = len(pieces), len(rows), len(vecs), len(out_dtypes)

    def body(*refs):
        p_refs, w_ref = refs[:n_p], refs[n_p]
        extra = [r[...] for r in refs[n_p + 1:n_p + 1 + n_r + n_v]]
        o_refs = refs[n_p + 1 + n_r + n_v:n_p + 1 + n_r + n_v + n_o]
        sum_refs = refs[n_p + 1 + n_r + n_v + n_o:]
        acc = None
        for p_ref, off, width in zip(p_refs, offs, widths):
            if w_kn:
                part = jnp.dot(p_ref[...], w_ref[off:off + width, :], preferred_element_type=F32)
            else:
                part = _dot_nt(p_ref[...], w_ref[:, off:off + width])
            acc = part if acc is None else acc + part
        res = epilogue(acc, *extra)
        for o_ref, val in zip(o_refs, res[:n_o]):
            o_ref[...] = val.astype(o_ref.dtype)
        step = pl.program_id(0)
        for s_ref, val in zip(sum_refs, res[n_o:]):
            @pl.when(step == 0)
            def _(s_ref=s_ref, val=val):
                s_ref[...] = val

            @pl.when(step > 0)
            def _(s_ref=s_ref, val=val):
                s_ref[...] += val

    row_spec = pl.BlockSpec((tm, n), lambda i: (i, 0))
    vec_spec = pl.BlockSpec((1, n), lambda i: (0, 0))
    return _ride_call(
        body, ride, name=name, grid=(s // tm,),
        in_specs=[pl.BlockSpec((tm, width), lambda i: (i, 0)) for width in widths]
        + [pl.BlockSpec(w.shape, lambda i: (0, 0))]
        + [pl.BlockSpec((tm, n), functools.partial(lambda i, cb: (i, cb), cb=cb)) for _, cb in rows] + [vec_spec] * n_v,
        out_specs=[row_spec] * n_o + [vec_spec] * n_sums,
        out_shape=[jax.ShapeDtypeStruct((s, n), dt) for dt in out_dtypes] + [jax.ShapeDtypeStruct((1, n), F32)] * n_sums,
        scratch_shapes=[], sem=("arbitrary",) if n_sums else ("parallel",),
        args=(*pieces, w, *[r for r, _ in rows], *vecs))


def _mm_pieces_tn(a, pieces, *, name, tk=1024):
    s, m = a.shape
    tk = min(tk, s)
    nk = s // tk
    n = len(pieces)
    dims = (((0,), (0,)), ((), ()))

    def body(*refs):
        a_ref, p_refs, o_refs, accs = refs[0], refs[1:1 + n], refs[1 + n:1 + 2 * n], refs[1 + 2 * n:]
        step = pl.program_id(0)
        a_blk = a_ref[...]
        for p_ref, o_ref, acc in zip(p_refs, o_refs, accs):
            part = lax.dot_general(a_blk, p_ref[...], dims, preferred_element_type=F32)

            @pl.when(step == 0)
            def _(acc=acc, part=part):
                acc[...] = part

            @pl.when(step > 0)
            def _(acc=acc, part=part):
                acc[...] += part

            @pl.when(step == nk - 1)
            def _(acc=acc, o_ref=o_ref):
                o_ref[...] = acc[...].astype(BF16)

    return pl.pallas_call(
        body, name=name, grid=(nk,),
        in_specs=[pl.BlockSpec((tk, m), lambda l: (l, 0))] + [pl.BlockSpec((tk, p.shape[1]), lambda l: (l, 0)) for p in pieces],
        out_specs=[pl.BlockSpec((m, p.shape[1]), lambda l: (0, 0)) for p in pieces],
        out_shape=[jax.ShapeDtypeStruct((m, p.shape[1]), BF16) for p in pieces],
        scratch_shapes=[pltpu.VMEM((m, p.shape[1]), F32) for p in pieces],
        compiler_params=_params(("arbitrary",)),
    )(a, *pieces)


def _rowwise(name, fn, rows, vecs, outs, sums=(), tr=512):
    s = rows[0][0].shape[0]
    tr = min(tr, s)
    nr, nv, no = len(rows), len(vecs), len(outs)

    def body(*refs):
        vals = [r[...] for r in refs[:nr + nv]]
        res = fn(*vals)
        for o_ref, val in zip(refs[nr + nv:nr + nv + no], res[:no]):
            o_ref[...] = val.astype(o_ref.dtype)
        step = pl.program_id(0)
        for s_ref, val in zip(refs[nr + nv + no:], res[no:]):
            @pl.when(step == 0)
            def _(s_ref=s_ref, val=val):
                s_ref[...] = val

            @pl.when(step > 0)
            def _(s_ref=s_ref, val=val):
                s_ref[...] += val

    in_specs = [pl.BlockSpec((tr, w), functools.partial(lambda i, cb: (i, cb), cb=cb)) for _, w, cb in rows]
    in_specs += [pl.BlockSpec(v.shape, lambda i: (0, 0)) for v in vecs]
    out_specs = [pl.BlockSpec((tr, w), lambda i: (i, 0)) for w, _ in outs]
    out_specs += [pl.BlockSpec((1, w), lambda i: (0, 0)) for w in sums]
    out_shape = [jax.ShapeDtypeStruct((s, w), dt) for w, dt in outs]
    out_shape += [jax.ShapeDtypeStruct((1, w), F32) for w in sums]
    return pl.pallas_call(
        body, name=name, grid=(s // tr,), in_specs=in_specs, out_specs=out_specs, out_shape=out_shape,
        compiler_params=_params(("arbitrary",) if sums else ("parallel",)),
    )(*[r[0] for r in rows], *vecs)


def _colsum(x):
    return jnp.sum(x, axis=0, keepdims=True)


def _sigmoid(x):
    return 1.0 / (1.0 + jnp.exp(-x))


def _log_sigmoid(x):
    return jnp.minimum(x, 0.0) - jnp.log(1.0 + jnp.exp(-jnp.abs(x)))


def _ln_stats(r):
    mu = jnp.mean(r, axis=-1, keepdims=True)
    xc = r - mu
    var = jnp.mean(xc * xc, axis=-1, keepdims=True)
    rstd = lax.rsqrt(var + LN_EPS)
    return xc * rstd, rstd


def _ln_bwd(dy, xhat, rstd, g):
    dxh = dy * g
    m1 = jnp.mean(dxh, axis=-1, keepdims=True)
    m2 = jnp.mean(dxh * xhat, axis=-1, keepdims=True)
    return rstd * (dxh - m1 - xhat * m2)


def _mod_exchange(c_row, w_ada, b_ada_loc):
    def body(c_ref, w_ref, b_ref, call_ref, mod_ref, piece_ref, send_sems, recv_sems):
        me = _me()[3]
        call_ref[me] = c_ref[...]
        sent = []
        for r in range(1, N_DEV):
            peer, _ = _peer(r)
            cp = pltpu.make_async_remote_copy(
                src_ref=c_ref, dst_ref=call_ref.at[me], send_sem=send_sems.at[0, r - 1],
                recv_sem=recv_sems.at[0, r - 1], device_id=peer, device_id_type=MESH)
            cp.start()
            sent.append(cp)
        for r in range(1, N_DEV):
            peer, pidx = _peer(r)
            pltpu.make_async_remote_copy(
                src_ref=c_ref, dst_ref=call_ref.at[pidx], send_sem=send_sems.at[0, r - 1],
                recv_sem=recv_sems.at[0, r - 1], device_id=peer, device_id_type=MESH).wait_recv()
        c_all = jnp.concatenate([call_ref[d] for d in range(N_DEV)], axis=0)
        mod_loc = jnp.dot(c_all, w_ref[...], preferred_element_type=F32,
                          precision=lax.Precision.HIGHEST) + b_ref[...]
        for d in range(N_DEV):
            piece_ref[d] = mod_loc[d:d + 1, :]
        mod_ref[me] = piece_ref[me]
        for r in range(1, N_DEV):
            peer, pidx = _peer(r)
            cp = pltpu.make_async_remote_copy(
                src_ref=piece_ref.at[pidx], dst_ref=mod_ref.at[me], send_sem=send_sems.at[1, r - 1],
                recv_sem=recv_sems.at[1, r - 1], device_id=peer, device_id_type=MESH)
            cp.start()
            sent.append(cp)
        for r in range(1, N_DEV):
            peer, pidx = _peer(r)
            pltpu.make_async_remote_copy(
                src_ref=piece_ref.at[me], dst_ref=mod_ref.at[pidx], send_sem=send_sems.at[1, r - 1],
                recv_sem=recv_sems.at[1, r - 1], device_id=peer, device_id_type=MESH).wait_recv()
        for cp in sent:
            cp.wait_send()

    vmem = pl.BlockSpec(memory_space=pltpu.VMEM)
    return pl.pallas_call(
        body, name="mod_exchange", in_specs=[vmem, vmem, vmem], out_specs=[vmem, vmem],
        out_shape=[jax.ShapeDtypeStruct((N_DEV, 1, D_MODEL), F32), jax.ShapeDtypeStruct((N_DEV, 1, ADA_SHARD), F32)],
        scratch_shapes=[pltpu.VMEM((N_DEV, 1, ADA_SHARD), F32),
                        pltpu.SemaphoreType.DMA((2, N_DEV - 1)), pltpu.SemaphoreType.DMA((2, N_DEV - 1))],
        compiler_params=_params(),
    )(c_row, w_ada, b_ada_loc)


def _split3(x):
    hi = x.astype(BF16)
    r1 = x - hi.astype(F32)
    mid = r1.astype(BF16)
    lo = (r1 - mid.astype(F32)).astype(BF16)
    return hi, mid, lo


def _scan_rows(x_ref, o_ref, s, reverse, pre=None, post=None):
    tb = min(TQ, s)
    nb = s // tb
    row = lax.broadcasted_iota(jnp.int32, (tb, tb), 0)
    col = lax.broadcasted_iota(jnp.int32, (tb, tb), 1)
    tri = jnp.where((col >= row) if reverse else (col <= row), 1.0, 0.0).astype(BF16)

    def step(i, carry):
        blk = (nb - 1 - i) if reverse else i
        off = pl.multiple_of(blk * tb, tb)
        x = x_ref[pl.ds(off, tb), :]
        if pre is not None:
            x = pre(x, off)
        acc = carry
        for piece in _split3(x):
            acc = acc + jnp.dot(tri, piece, preferred_element_type=F32)
        o_ref[pl.ds(off, tb), :] = acc if post is None else post(acc, off)
        edge = acc[0:1, :] if reverse else acc[tb - 1:tb, :]
        return jnp.broadcast_to(edge, (tb, LANES))

    lax.fori_loop(0, nb, step, jnp.zeros((tb, LANES), F32))


def _forget_cumsum(f_raw, b_pad):
    s = f_raw.shape[0]

    def body(f_ref, b_ref, cum_ref):
        b = b_ref[...]
        _scan_rows(f_ref, cum_ref, s, False, pre=lambda x, off: _log_sigmoid(x + b))

    vmem = pl.BlockSpec(memory_space=pltpu.VMEM)
    return pl.pallas_call(body, name="forget_cumsum", in_specs=[vmem, vmem], out_specs=vmem,
                          out_shape=jax.ShapeDtypeStruct((s, LANES), F32), compiler_params=_params())(f_raw, b_pad)


def _forget_bwd(dcum, f_raw, b_pad):
    s = f_raw.shape[0]

    def body(d_ref, f_ref, b_ref, df_ref, db_ref, tmp_ref):
        b = b_ref[...]
        _scan_rows(d_ref, tmp_ref, s, True)
        df = tmp_ref[...] * _sigmoid(-(f_ref[...] + b))
        df_ref[...] = df.astype(BF16)
        db_ref[...] = _colsum(df)

    vmem = pl.BlockSpec(memory_space=pltpu.VMEM)
    return pl.pallas_call(
        body, name="forget_bwd", in_specs=[vmem, vmem, vmem], out_specs=[vmem, vmem],
        out_shape=[jax.ShapeDtypeStruct((s, LANES), BF16), jax.ShapeDtypeStruct((1, LANES), F32)],
        scratch_shapes=[pltpu.VMEM((s, LANES), F32)], compiler_params=_params())(dcum, f_raw, b_pad)


def _dot_nt(a, b):
    return lax.dot_general(a, b, (((1,), (1,)), ((), ())), preferred_element_type=F32)


def _head_masks():
    lane = lax.broadcasted_iota(jnp.int32, (TQ, LANES), 1)
    return lane, [lane < HEAD_DIM, lane >= HEAD_DIM]


def _pick(mask, x):
    return jnp.where(mask, x, jnp.zeros_like(x))


def _qkv_specs(s, col0):
    nb = ATTN_W // LANES
    return [pl.BlockSpec((TQ, LANES), lambda hp, qi: (qi, col0 + hp)),
            pl.BlockSpec((s, LANES), lambda hp, qi: (0, col0 + nb + hp)),
            pl.BlockSpec((s, LANES), lambda hp, qi: (0, col0 + 2 * nb + hp))]


def _pair_spec():
    return pl.BlockSpec((TQ, LANES), lambda hp, qi: (qi, hp))


def _diag_mask(width, strict):
    row = lax.broadcasted_iota(jnp.int32, (TQ, width * TQ), 0) + (width - 1) * TQ
    col = lax.broadcasted_iota(jnp.int32, (TQ, width * TQ), 1)
    return (col < row) if strict else (col <= row)


def _fox_fwd(qkv, cum_col, cum_row, ride=None):
    s = qkv.shape[0]
    nq = s // TQ

    def body(q_ref, k_ref, v_ref, cc_ref, cr_ref, o_ref, o32_ref, lse_ref):
        hp, qi = pl.program_id(0), pl.program_id(1)
        lane, masks = _head_masks()
        q2 = q_ref[...] * jnp.asarray(ATTN_SCALE, BF16)
        cc = cc_ref[...]
        qms = [_pick(masks[e], q2) for e in range(2)]
        cqs = [jnp.sum(jnp.where(lane == 2 * hp + e, cc, 0.0), axis=1, keepdims=True) for e in range(2)]

        def tile(kb, carry, masked, width=1):
            off, span = pl.multiple_of(kb * TQ, TQ), width * TQ
            k2, v2 = k_ref[pl.ds(off, span), :], v_ref[pl.ds(off, span), :]
            head0 = lax.broadcasted_iota(jnp.int32, (span, LANES), 1) < HEAD_DIM
            new = []
            for e in range(2):
                m, acc = carry[e]
                sc = _dot_nt(qms[e], k2) + (cqs[e] - cr_ref[e:e + 1, pl.ds(off, span)])
                if masked:
                    sc = jnp.where(_diag_mask(width, False), sc, -jnp.inf)
                m_new = jnp.maximum(m, jnp.max(sc, axis=1, keepdims=True))
                p = jnp.exp(sc - m_new)
                v_ones = jnp.where(head0 if e == 0 else ~head0, v2, jnp.ones_like(v2))
                acc = jnp.exp(m - m_new) * acc + jnp.dot(p.astype(BF16), v_ones, preferred_element_type=F32)
                new.append((m_new, acc))
            return tuple(new)

        init = (jnp.full((TQ, 1), -jnp.inf, F32), jnp.zeros((TQ, LANES), F32))
        carry = lax.cond(qi % 2 == 1, lambda cr: tile(qi - 1, cr, True, 2), lambda cr: tile(qi, cr, True), (init, init))
        quads = qi // 4
        carry = lax.fori_loop(0, quads, lambda j, cr: tile(4 * j, cr, False, 4), carry)
        carry = lax.cond(qi % 4 >= 2, lambda cr: tile(4 * quads, cr, False, 2), lambda cr: cr, carry)
        sums = [jnp.max(jnp.where(masks[1 - e], carry[e][1], 0.0), axis=1, keepdims=True) for e in range(2)]
        outs = [carry[e][1] / sums[e] for e in range(2)]
        lses = [carry[e][0] + jnp.log(sums[e]) for e in range(2)]
        out = jnp.where(masks[0], outs[0], outs[1])
        o_ref[...] = out.astype(BF16)
        o32_ref[...] = out
        lse_ref[...] = jnp.where(masks[0], lses[0], lses[1])

    return _ride_call(
        body, ride, name="fox_fwd", grid=(N_HEADS // 2, nq),
        in_specs=_qkv_specs(s, 0) + [pl.BlockSpec((TQ, LANES), lambda hp, qi: (qi, 0)),
                                     pl.BlockSpec((None, 2, s), lambda hp, qi: (hp, 0, 0))],
        out_specs=[_pair_spec(), _pair_spec(), _pair_spec()],
        out_shape=[jax.ShapeDtypeStruct((s, ATTN_W), BF16), jax.ShapeDtypeStruct((s, ATTN_W), F32),
                   jax.ShapeDtypeStruct((s, ATTN_W), F32)],
        scratch_shapes=[], sem=("parallel", "parallel"), args=(qkv, qkv, qkv, cum_col, cum_row))


def _write_transposed(acc_ref, out_ref):
    for c in range(out_ref.shape[0] // TQ):
        out_ref[c * TQ:(c + 1) * TQ, :] = jnp.transpose(acc_ref[:, c * TQ:(c + 1) * TQ]).astype(BF16)


def _fox_bwd(qkv, cum_col, cum_row, o, lse, do, ride=None):
    s = qkv.shape[0]
    nq = s // TQ

    def body(q_ref, k_ref, v_ref, cc_ref, cr_ref, o_ref, lse_ref, do_ref,
             dq_ref, dk_ref, dv_ref, dcr_ref, dk_acc, dv_acc):
        hp, qi = pl.program_id(0), pl.program_id(1)

        @pl.when(qi == 0)
        def _():
            dk_acc[...] = jnp.zeros_like(dk_acc)
            dv_acc[...] = jnp.zeros_like(dv_acc)
            dcr_ref[...] = jnp.zeros_like(dcr_ref)

        lane, masks = _head_masks()
        q2 = q_ref[...] * jnp.asarray(ATTN_SCALE, BF16)
        do2 = do_ref[...]
        prod = do2.astype(F32) * o_ref[...].astype(F32)
        lse2 = lse_ref[...]
        cc = cc_ref[...]
        qms = [_pick(masks[e], q2) for e in range(2)]
        doms = [_pick(masks[e], do2) for e in range(2)]
        deltas = [jnp.sum(jnp.where(masks[e], prod, 0.0), axis=1, keepdims=True) for e in range(2)]
        lses = [jnp.max(jnp.where(masks[e], lse2, -jnp.inf), axis=1, keepdims=True) for e in range(2)]
        cqs = [jnp.sum(jnp.where(lane == 2 * hp + e, cc, 0.0), axis=1, keepdims=True) for e in range(2)]
        qts = [jnp.transpose(qms[e].astype(F32)).astype(BF16) for e in range(2)]
        dots = [jnp.transpose(doms[e].astype(F32)).astype(BF16) for e in range(2)]
        ones_row = [HEAD_DIM * (1 - e) for e in range(2)]
        trow = lax.broadcasted_iota(jnp.int32, (LANES, TQ), 0)
        qts = [jnp.where(trow == ones_row[e], jnp.ones_like(qts[e]), qts[e]) for e in range(2)]

        def tile(kb, carry, masked, width=1):
            off, span = pl.multiple_of(kb * TQ, TQ), width * TQ
            k2, v2 = k_ref[pl.ds(off, span), :], v_ref[pl.ds(off, span), :]
            head0 = lax.broadcasted_iota(jnp.int32, (span, LANES), 1) < HEAD_DIM
            new, dks, dv = [], [], None
            for e in range(2):
                dq = carry[e]
                sc = _dot_nt(qms[e], k2) + (cqs[e] - cr_ref[e:e + 1, pl.ds(off, span)])
                p = jnp.exp(sc - lses[e])
                if masked:
                    p = jnp.where(_diag_mask(width, False), p, 0.0)
                ds = p * (_dot_nt(doms[e], v2) - deltas[e])
                dsb = ds.astype(BF16)
                dk_e = jnp.dot(qts[e], dsb, preferred_element_type=F32)
                dv_e = jnp.dot(dots[e], p.astype(BF16), preferred_element_type=F32)
                dks.append(dk_e)
                dv = dv_e if e == 0 else dv + dv_e
                dcr_ref[e:e + 1, pl.ds(off, span)] -= dk_e[ones_row[e]:ones_row[e] + 1, :]
                k_ones = jnp.where(head0 if e == 0 else ~head0, k2, jnp.ones_like(k2))
                new.append(dq + jnp.dot(dsb, k_ones, preferred_element_type=F32))
            krow = lax.broadcasted_iota(jnp.int32, (LANES, span), 0)
            dk_acc[:, pl.ds(off, span)] += jnp.where(krow < HEAD_DIM, dks[0], dks[1])
            dv_acc[:, pl.ds(off, span)] += dv
            return tuple(new)

        init = jnp.zeros((TQ, LANES), F32)
        carry = lax.fori_loop(0, qi // 2, lambda j, cr: tile(2 * j, cr, False, 2), (init, init))
        carry = lax.cond(qi % 2 == 1, lambda cr: tile(qi - 1, cr, True, 2), lambda cr: tile(qi, cr, True), carry)
        dq_ref[...] = (jnp.where(masks[0], carry[0], carry[1]) * ATTN_SCALE).astype(BF16)
        for e in range(2):
            dcr_ref[e:e + 1, pl.ds(pl.multiple_of(qi * TQ, TQ), TQ)] += jnp.transpose(carry[e])[
                ones_row[e]:ones_row[e] + 1, :]

        @pl.when(qi == nq - 1)
        def _():
            _write_transposed(dk_acc, dk_ref)
            _write_transposed(dv_acc, dv_ref)

    seq_spec = pl.BlockSpec((s, LANES), lambda hp, qi: (0, hp))
    return _ride_call(
        body, ride, name="fox_bwd", grid=(N_HEADS // 2, nq),
        in_specs=_qkv_specs(s, 0) + [pl.BlockSpec((TQ, LANES), lambda hp, qi: (qi, 0)),
                                     pl.BlockSpec((None, 2, s), lambda hp, qi: (hp, 0, 0)),
                                     _pair_spec(), _pair_spec(), _pair_spec()],
        out_specs=[_pair_spec(), seq_spec, seq_spec, pl.BlockSpec((None, 2, s), lambda hp, qi: (hp, 0, 0))],
        out_shape=[jax.ShapeDtypeStruct((s, ATTN_W), BF16)] * 3 + [jax.ShapeDtypeStruct((N_HEADS // 2, 2, s), F32)],
        scratch_shapes=[pltpu.VMEM((LANES, s), F32), pltpu.VMEM((LANES, s), F32)],
        sem=("parallel", "arbitrary"), args=(qkv, qkv, qkv, cum_col, cum_row, o, lse, do))


def _scan_matrix(reverse):
    row = lax.broadcasted_iota(jnp.int32, (SCAN_W, SCAN_W), 0)
    col = lax.broadcasted_iota(jnp.int32, (SCAN_W, SCAN_W), 1)
    return jnp.where((row > col) if reverse else (row < col), 1.0, 0.0).astype(BF16)


def _scan_cols(x, tri, reverse, init):
    nblk = x.shape[1] // SCAN_W
    parts, total = [None] * nblk, init
    far = 0 if reverse else SCAN_W - 1
    for b in (reversed(range(nblk)) if reverse else range(nblk)):
        blk = x[:, b * SCAN_W:(b + 1) * SCAN_W]
        part = jnp.dot(blk.astype(BF16), tri, preferred_element_type=F32)
        parts[b] = part + total
        total = total + (part[:, far:far + 1] + blk[:, far:far + 1])
    return (parts[0] if nblk == 1 else jnp.concatenate(parts, axis=1)), total


def _sb_logits(qm, k2):
    z = _dot_nt(qm, k2)
    neg_abs = lax.bitcast_convert_type(lax.bitcast_convert_type(z, jnp.uint32) | jnp.uint32(0x80000000), F32)
    soft = jnp.log(1.0 + jnp.exp(neg_abs))
    lb = jnp.minimum(z, 0.0) - soft
    return lb, lb - z


TILE_SLOTS = 4


def _tri_base(qi):
    return (qi * (qi + 1)) // 2


def _sb_fwd(qkv):
    s = qkv.shape[0]
    nq = s // TQ

    def body(q_ref, k_ref, v_ref, o_ref, t_ref, buf, sems):
        hp, qi = pl.program_id(0), pl.program_id(1)
        _, masks = _head_masks()
        suffix = _scan_matrix(True)
        q2 = q_ref[...] * jnp.asarray(ATTN_SCALE, BF16)
        qms = [_pick(masks[e], q2) for e in range(2)]
        base = _tri_base(qi)

        def store(e, kb):
            slot = kb % TILE_SLOTS
            return pltpu.make_async_copy(buf.at[e, slot], t_ref.at[2 * hp + e, base + kb], sems.at[e, slot])

        def tile(kb, carry, masked, width=1):
            off, span = pl.multiple_of(kb * TQ, TQ), width * TQ
            k2, v2 = k_ref[pl.ds(off, span), :], v_ref[pl.ds(off, span), :]
            new = []
            for e in range(2):
                run, acc = carry[e]
                lb, lo = _sb_logits(qms[e], k2)
                if masked:
                    strict = _diag_mask(width, True)
                    lo = jnp.where(strict, lo, 0.0)
                rest, run = _scan_cols(lo, suffix, True, run)
                a = jnp.exp(lb + rest)
                if masked:
                    a = jnp.where(strict, a, 0.0)
                ab, lbb = a.astype(BF16), lb.astype(BF16)
                acc = acc + jnp.dot(ab, v2, preferred_element_type=F32)
                for w in range(width):
                    blk = kb + w

                    @pl.when(blk + TILE_SLOTS <= qi)
                    def _(e=e, blk=blk):
                        store(e, blk + TILE_SLOTS).wait()
                    buf[e, blk % TILE_SLOTS, 0] = ab[:, w * TQ:(w + 1) * TQ]
                    buf[e, blk % TILE_SLOTS, 1] = lbb[:, w * TQ:(w + 1) * TQ]
                    store(e, blk).start()
                new.append((run, acc))
            return tuple(new)

        init = (jnp.zeros((TQ, 1), F32), jnp.zeros((TQ, LANES), F32))
        carry = lax.cond(qi % 2 == 1, lambda cr: tile(qi - 1, cr, True, 2), lambda cr: tile(qi, cr, True), (init, init))
        pairs = qi // 2
        carry = lax.fori_loop(0, pairs, lambda it, cr: tile(2 * (pairs - 1 - it), cr, False, 2), carry)
        for e in range(2):
            for blk in range(TILE_SLOTS):
                @pl.when(qi >= blk)
                def _(e=e, blk=blk):
                    store(e, blk).wait()
        o_ref[...] = jnp.where(masks[0], carry[0][1], carry[1][1]).astype(BF16)

    ntri = nq * (nq + 1) // 2
    return pl.pallas_call(
        body, name="sb_fwd", grid=(N_HEADS // 2, nq), in_specs=_qkv_specs(s, 3 * ATTN_W // LANES),
        out_specs=[_pair_spec(), ANY],
        out_shape=[jax.ShapeDtypeStruct((s, ATTN_W), BF16), jax.ShapeDtypeStruct((N_HEADS, ntri, 2, TQ, TQ), BF16)],
        scratch_shapes=[pltpu.VMEM((2, TILE_SLOTS, 2, TQ, TQ), BF16), pltpu.SemaphoreType.DMA((2, TILE_SLOTS))],
        compiler_params=_params(("arbitrary", "arbitrary")),
    )(qkv, qkv, qkv)


def _sb_bwd(qkv, tiles, do):
    s = qkv.shape[0]
    nq = s // TQ

    def body(q_ref, k_ref, v_ref, t_ref, do_ref, dq_ref, dk_ref, dv_ref, dk_acc, dv_acc, buf, sems):
        hp, qi = pl.program_id(0), pl.program_id(1)

        @pl.when(qi == 0)
        def _():
            dk_acc[...] = jnp.zeros_like(dk_acc)
            dv_acc[...] = jnp.zeros_like(dv_acc)

        _, masks = _head_masks()
        prefix = _scan_matrix(False)
        q2 = q_ref[...] * jnp.asarray(ATTN_SCALE, BF16)
        do2 = do_ref[...]
        qms = [_pick(masks[e], q2) for e in range(2)]
        doms = [_pick(masks[e], do2) for e in range(2)]
        qts = [jnp.transpose(qms[e].astype(F32)).astype(BF16) for e in range(2)]
        dots = [jnp.transpose(doms[e].astype(F32)).astype(BF16) for e in range(2)]
        base = _tri_base(qi)

        def fetch(e, kb):
            slot = kb % TILE_SLOTS
            return pltpu.make_async_copy(t_ref.at[2 * hp + e, base + kb], buf.at[e, slot], sems.at[e, slot])

        for e in range(2):
            fetch(e, 0).start()

            @pl.when(qi >= 1)
            def _(e=e):
                fetch(e, 1).start()

        def tile(kb, carry, masked, width=1):
            off, span = pl.multiple_of(kb * TQ, TQ), width * TQ
            k2, v2 = k_ref[pl.ds(off, span), :], v_ref[pl.ds(off, span), :]
            new, dk, dv = [], None, None
            for e in range(2):
                gsum, dq = carry[e]
                if not masked:
                    for blk in range(2, 2 + width):
                        @pl.when(kb + blk <= qi)
                        def _(e=e, blk=blk):
                            fetch(e, kb + blk).start()
                for w in range(width):
                    fetch(e, kb + w).wait()
                slots = [(kb + w) % TILE_SLOTS for w in range(width)]
                ab = buf[e, slots[0], 0] if width == 1 else jnp.concatenate([buf[e, sl, 0] for sl in slots], axis=1)
                lbb = buf[e, slots[0], 1] if width == 1 else jnp.concatenate([buf[e, sl, 1] for sl in slots], axis=1)
                beta = jnp.exp(lbb.astype(F32))
                g = ab.astype(F32) * _dot_nt(doms[e], v2)
                before, gsum = _scan_cols(g, prefix, False, gsum)
                dz = g - beta * (g + before)
                if masked:
                    dz = jnp.where(_diag_mask(width, True), dz, 0.0)
                dzb = dz.astype(BF16)
                dk_e = jnp.dot(qts[e], dzb, preferred_element_type=F32)
                dv_e = jnp.dot(dots[e], ab, preferred_element_type=F32)
                dk, dv = (dk_e, dv_e) if e == 0 else (dk + dk_e, dv + dv_e)
                new.append((gsum, dq + jnp.dot(dzb, k2, preferred_element_type=F32)))
            dk_acc[:, pl.ds(off, span)] += dk
            dv_acc[:, pl.ds(off, span)] += dv
            return tuple(new)

        init = (jnp.zeros((TQ, 1), F32), jnp.zeros((TQ, LANES), F32))
        carry = lax.fori_loop(0, qi // 2, lambda j, cr: tile(2 * j, cr, False, 2), (init, init))
        carry = lax.cond(qi % 2 == 1, lambda cr: tile(qi - 1, cr, True, 2), lambda cr: tile(qi, cr, True), carry)
        dq_ref[...] = (jnp.where(masks[0], carry[0][1], carry[1][1]) * ATTN_SCALE).astype(BF16)

        @pl.when(qi == nq - 1)
        def _():
            _write_transposed(dk_acc, dk_ref)
            _write_transposed(dv_acc, dv_ref)

    seq_spec = pl.BlockSpec((s, LANES), lambda hp, qi: (0, hp))
    return pl.pallas_call(
        body, name="sb_bwd", grid=(N_HEADS // 2, nq),
        in_specs=_qkv_specs(s, 3 * ATTN_W // LANES) + [ANY, _pair_spec()],
        out_specs=[_pair_spec(), seq_spec, seq_spec],
        out_shape=[jax.ShapeDtypeStruct((s, ATTN_W), BF16)] * 3,
        scratch_shapes=[pltpu.VMEM((LANES, s), F32), pltpu.VMEM((LANES, s), F32),
                        pltpu.VMEM((2, TILE_SLOTS, 2, TQ, TQ), BF16), pltpu.SemaphoreType.DMA((2, TILE_SLOTS))],
        compiler_params=_params(("arbitrary", "arbitrary")),
    )(qkv, qkv, qkv, tiles, do)


CONV_TR = 512


def _shift_down(x, halo, n):
    rolled = pltpu.roll(x, n, 0)
    head = rolled[0:8, :]
    rid = lax.broadcasted_iota(jnp.int32, head.shape, 0)
    for j in range(n):
        head = jnp.where(rid == j, halo[8 - n + j:8 - n + j + 1, :], head)
    return jnp.concatenate([head, rolled[8:, :]], axis=0)


def _shift_up(x, halo, n):
    rows = x.shape[0]
    rolled = pltpu.roll(x, rows - n, 0)
    tail = rolled[rows - 8:, :]
    rid = lax.broadcasted_iota(jnp.int32, tail.shape, 0)
    for j in range(n):
        tail = jnp.where(rid == 8 - n + j, halo[j:j + 1, :], tail)
    return jnp.concatenate([rolled[:rows - 8, :], tail], axis=0)


def _conv_fwd_block(x, halo, w, b):
    return b + _shift_down(x, halo, 2) * w[0:1, :] + _shift_down(x, halo, 1) * w[1:2, :] + x * w[2:3, :]


def _conv_specs(tr, s):
    pair = 2 * FF_HALF
    blk = pl.BlockSpec((tr, pair), lambda j, i: (i, j))
    prev = pl.BlockSpec((8, pair), lambda j, i: (jnp.maximum(i * (tr // 8) - 1, 0), j))
    nxt = pl.BlockSpec((8, pair), lambda j, i: (jnp.minimum((i + 1) * (tr // 8), s // 8 - 1), j))
    return blk, prev, nxt


def _conv_gate_fwd(hpre, conv_w, conv_b):
    s = hpre.shape[0]
    tr = min(CONV_TR, s)
    blk, prev, _ = _conv_specs(tr, s)

    def body(x_ref, halo_ref, w_ref, b_ref, a_ref):
        i = pl.program_id(1)
        halo = jnp.where(i > 0, halo_ref[...], 0.0)
        h = _conv_fwd_block(x_ref[...], halo, w_ref[...], b_ref[...])
        hg, hv = h[:, :FF_HALF], h[:, FF_HALF:]
        a_ref[...] = (hg * _sigmoid(hg) * hv).astype(BF16)

    return pl.pallas_call(
        body, name="conv_gate_fwd", grid=(2, s // tr),
        in_specs=[blk, prev, pl.BlockSpec((3, 2 * FF_HALF), lambda j, i: (0, j)),
                  pl.BlockSpec((1, 2 * FF_HALF), lambda j, i: (0, j))],
        out_specs=pl.BlockSpec((tr, FF_HALF), lambda j, i: (i, j)),
        out_shape=jax.ShapeDtypeStruct((s, D_FF), BF16),
        compiler_params=_params(("parallel", "parallel")),
    )(hpre, hpre, conv_w, conv_b)


def _conv_gate_bwd(hpre, da, conv_w, conv_b):
    s = hpre.shape[0]
    tr = min(CONV_TR, s)
    blk, prev, _ = _conv_specs(tr, s)

    def body(x_ref, halo_ref, da_ref, w_ref, b_ref, dh_ref, db_ref, dw_ref):
        i = pl.program_id(1)
        halo = jnp.where(i > 0, halo_ref[...], 0.0)
        x = x_ref[...]
        h = _conv_fwd_block(x, halo, w_ref[...], b_ref[...])
        hg, hv = h[:, :FF_HALF], h[:, FF_HALF:]
        da_blk = da_ref[...].astype(F32)
        sg = _sigmoid(hg)
        dhg = da_blk * hv * (sg * (1.0 + hg * (1.0 - sg)))
        dhv = da_blk * (hg * sg)
        dh_ref[:, :FF_HALF] = dhg.astype(BF16)
        dh_ref[:, FF_HALF:] = dhv.astype(BF16)
        x2, x1 = _shift_down(x, halo, 2), _shift_down(x, halo, 1)
        parts = []
        for lo, dpart in ((0, dhg), (FF_HALF, dhv)):
            cols = slice(lo, lo + FF_HALF)
            parts.append((cols, _colsum(dpart), _colsum(dpart * x2[:, cols]), _colsum(dpart * x1[:, cols]),
                          _colsum(dpart * x[:, cols])))

        @pl.when(i == 0)
        def _():
            for cols, db, dw0, dw1, dw2 in parts:
                db_ref[:, cols] = db
                dw_ref[0:1, cols] = dw0
                dw_ref[1:2, cols] = dw1
                dw_ref[2:3, cols] = dw2

        @pl.when(i > 0)
        def _():
            for cols, db, dw0, dw1, dw2 in parts:
                db_ref[:, cols] += db
                dw_ref[0:1, cols] += dw0
                dw_ref[1:2, cols] += dw1
                dw_ref[2:3, cols] += dw2

    pair = 2 * FF_HALF
    return pl.pallas_call(
        body, name="conv_gate_bwd", grid=(2, s // tr),
        in_specs=[blk, prev, pl.BlockSpec((tr, FF_HALF), lambda j, i: (i, j)),
                  pl.BlockSpec((3, pair), lambda j, i: (0, j)), pl.BlockSpec((1, pair), lambda j, i: (0, j))],
        out_specs=[blk, pl.BlockSpec((1, pair), lambda j, i: (0, j)), pl.BlockSpec((3, pair), lambda j, i: (0, j))],
        out_shape=[jax.ShapeDtypeStruct((s, 2 * D_FF), BF16), jax.ShapeDtypeStruct((1, 2 * D_FF), F32),
                   jax.ShapeDtypeStruct((3, 2 * D_FF), F32)],
        compiler_params=_params(("parallel", "arbitrary")),
    )(hpre, hpre, da, conv_w, conv_b)


def _conv_input_bwd(dh, conv_w):
    s = dh.shape[0]
    tr = min(CONV_TR, s)
    blk, _, _ = _conv_specs(tr, s)
    nblk = s // tr

    def body(x_ref, halo_ref, w_ref, o_ref):
        i = pl.program_id(1)
        halo = jnp.where(i < nblk - 1, halo_ref[...].astype(F32), 0.0)
        x, w = x_ref[...].astype(F32), w_ref[...]
        o_ref[...] = (x * w[2:3, :] + _shift_up(x, halo, 1) * w[1:2, :] + _shift_up(x, halo, 2) * w[0:1, :]).astype(BF16)

    nxt = pl.BlockSpec((16, 2 * FF_HALF), lambda j, i: (jnp.minimum((i + 1) * (tr // 16), s // 16 - 1), j))
    return pl.pallas_call(
        body, name="conv_input_bwd", grid=(2, nblk),
        in_specs=[blk, nxt, pl.BlockSpec((3, 2 * FF_HALF), lambda j, i: (0, j))], out_specs=blk,
        out_shape=jax.ShapeDtypeStruct((s, 2 * D_FF), BF16),
        compiler_params=_params(("parallel", "parallel")),
    )(dh, dh, conv_w)


def _adamw_math(w, g, m, v):
    m = ADAM_B1 * m + (1.0 - ADAM_B1) * g
    v = ADAM_B2 * v + (1.0 - ADAM_B2) * (g * g)
    m_hat = m / (1.0 - ADAM_B1 ** ADAM_STEP)
    v_hat = v / (1.0 - ADAM_B2 ** ADAM_STEP)
    delta = -ADAM_LR * (m_hat / (jnp.sqrt(v_hat) + ADAM_EPS) + ADAM_WD * w)
    return delta, m, v


def _adamw(name, g8, w, m, v):
    r, c = w.shape
    tr = _row_tile(r, c)

    def body(g_ref, w_ref, m_ref, v_ref, go_ref, d_ref, mo_ref, vo_ref):
        g = g_ref[0].astype(F32)
        for d in range(1, N_DEV):
            g = g + g_ref[d].astype(F32)
        delta, mn, vn = _adamw_math(w_ref[...], g, m_ref[...], v_ref[...])
        go_ref[...] = g
        d_ref[...] = delta
        mo_ref[...] = mn
        vo_ref[...] = vn

    spec = pl.BlockSpec((tr, c), lambda i: (i, 0))
    return pl.pallas_call(
        body, name=name, grid=(r // tr,),
        in_specs=[pl.BlockSpec((N_DEV, tr, c), lambda i: (0, i, 0)), spec, spec, spec], out_specs=[spec] * 4,
        out_shape=[jax.ShapeDtypeStruct((r, c), F32)] * 4, compiler_params=_params(("parallel",)),
    )(g8, w, m, v)


def _adamw_small(g8, ws, ms, vs):
    n = len(ws)
    offsets = [sum(w.shape[0] for w in ws[:i]) for i in range(n)]

    def body(*refs):
        g_ref, w_refs, m_refs, v_refs = refs[0], refs[1:1 + n], refs[1 + n:1 + 2 * n], refs[1 + 2 * n:1 + 3 * n]
        outs = refs[1 + 3 * n:]
        for i in range(n):
            rows = w_refs[i].shape[0]
            g = g_ref[0, offsets[i]:offsets[i] + rows, :]
            for d in range(1, N_DEV):
                g = g + g_ref[d, offsets[i]:offsets[i] + rows, :]
            delta, mn, vn = _adamw_math(w_refs[i][...], g, m_refs[i][...], v_refs[i][...])
            for k, val in enumerate((g, delta, mn, vn)):
                outs[k * n + i][...] = val

    vmem = pl.BlockSpec(memory_space=pltpu.VMEM)
    res = pl.pallas_call(
        body, name="adamw_small", in_specs=[vmem] * (1 + 3 * n), out_specs=[vmem] * (4 * n),
        out_shape=[jax.ShapeDtypeStruct(w.shape, F32) for _ in range(4) for w in ws], compiler_params=_params(),
    )(g8, *ws, *ms, *vs)
    return [res[k * n:(k + 1) * n] for k in range(4)]


def _adamw_ada(c_t, dmod, w, m, v):
    r, c = w.shape
    tr = _row_tile(r, c)

    def body(ct_ref, dm_ref, w_ref, m_ref, v_ref, go_ref, d_ref, mo_ref, vo_ref):
        ct, dm = ct_ref[...], dm_ref[...]
        g = ct[:, 0:1] * dm[0:1, :]
        for b in range(1, N_DEV):
            g = g + ct[:, b:b + 1] * dm[b:b + 1, :]
        delta, mn, vn = _adamw_math(w_ref[...], g, m_ref[...], v_ref[...])
        go_ref[...] = g
        d_ref[...] = delta
        mo_ref[...] = mn
        vo_ref[...] = vn

    spec = pl.BlockSpec((tr, c), lambda i: (i, 0))
    return pl.pallas_call(
        body, name="adamw_w_ada", grid=(r // tr,),
        in_specs=[pl.BlockSpec((tr, N_DEV), lambda i: (i, 0)), pl.BlockSpec((N_DEV, c), lambda i: (0, 0)),
                  spec, spec, spec],
        out_specs=[spec] * 4, out_shape=[jax.ShapeDtypeStruct((r, c), F32)] * 4,
        compiler_params=_params(("parallel",)),
    )(c_t, dmod, w, m, v)


def _cols_from_slots(g):
    n, r, c = g.shape
    return jnp.transpose(g, (1, 0, 2)).reshape(r, n * c)


def _cols_to_slots(w):
    r, c = w.shape
    return jnp.transpose(w.reshape(r, N_DEV, c // N_DEV), (1, 0, 2))


def _pair_cols(w):
    g0, g1 = w[..., 0:FF_HALF], w[..., FF_HALF:D_FF]
    v0, v1 = w[..., D_FF:D_FF + FF_HALF], w[..., D_FF + FF_HALF:]
    return jnp.concatenate([g0, v0, g1, v1], axis=-1)


def _unpair_cols(w):
    g0, v0 = w[..., 0:FF_HALF], w[..., FF_HALF:D_FF]
    g1, v1 = w[..., D_FF:D_FF + FF_HALF], w[..., D_FF + FF_HALF:]
    return jnp.concatenate([g0, g1, v0, v1], axis=-1)


def _row(v):
    return v.reshape(1, -1)


def kernel(x, c, w_ada, b_ada, w_in, b_forget, w_fox_proj, w_sb_proj, w_o, ln1_g, ln1_b, w_up, conv_w, conv_b, w_down, ln2_g, ln2_b, loss_target, m_w_ada, m_b_ada, m_w_in, m_b_forget, m_w_fox_proj, m_w_sb_proj, m_w_o, m_ln1_g, m_ln1_b, m_w_up, m_conv_w, m_conv_b, m_w_down, m_ln2_g, m_ln2_b, v_w_ada, v_b_ada, v_w_in, v_b_forget, v_w_fox_proj, v_w_sb_proj, v_w_o, v_ln1_g, v_ln1_b, v_w_up, v_conv_w, v_conv_b, v_w_down, v_ln2_g, v_ln2_b):
    s = x.shape[1]
    me = 4 * lax.axis_index("x") + 2 * lax.axis_index("y") + lax.axis_index("c")
    x2 = x.reshape(s, D_MODEL)
    tgt = loss_target.reshape(s, D_MODEL)

    b_ada_loc = lax.dynamic_slice(b_ada, (me * ADA_SHARD,), (ADA_SHARD,)).reshape(1, ADA_SHARD)
    c_all, mod = _mod_exchange(c, w_ada, b_ada_loc)
    mod = mod.reshape(N_MOD, 1, D_MODEL)
    sh1, sc1, gt1, sh2, sc2, gt2 = [mod[i] for i in range(N_MOD)]

    g_in = _allgather_two_level("ag_w_in", w_in.astype(BF16))
    late_weights = _Ride([w_fox_proj.astype(BF16), w_sb_proj.astype(BF16), w_o.astype(BF16), w_up.astype(BF16),
                          w_down.astype(BF16), conv_w], scatter=False)
    w_in_f = _cols_from_slots(g_in)
    w_proj = jnp.concatenate(
        [w_in_f[:, 0:1536], w_in_f[:, 1544:3080], w_in_f[:, 3080:5128], w_in_f[:, 1536:1544],
         jnp.zeros((D_MODEL, W_PROJ - 5128), BF16)], axis=1)
    w_qkv, w_gates, w_f = w_proj[:, :W_QKV], w_proj[:, W_QKV:W_QKV + W_GATES], w_proj[:, W_QKV + W_GATES:W_QKV + W_GATES + W_F]
    conv_b_p = _pair_cols(_row(conv_b))
    b_f_pad = jnp.pad(_row(b_forget), ((0, 0), (0, LANES - N_HEADS)))

    (u1,) = _rowwise("modulate1", lambda xb, sc, sh: (xb * (1.0 + sc) + sh,),
                     [(x2, D_MODEL, 0)], [sc1, sh1], [(D_MODEL, BF16)], tr=512)
    qkv = _mm(u1, w_qkv, name="mm_qkv", out_dtype=BF16)
    gates = _mm(u1, w_gates, name="mm_gates")
    f_raw = _mm(u1, w_f, name="mm_forget")
    cum_col = _forget_cumsum(f_raw, b_f_pad)
    cum_row = jnp.transpose(cum_col[:, :N_HEADS]).reshape(N_HEADS // 2, 2, s)
    (y_fox, y_fox32, lse), (g_fox, g_sb, g_o, g_up, g_down, g_cw) = _fox_fwd(qkv, cum_col, cum_row, ride=late_weights)
    w_fox_f = _cols_from_slots(g_fox)
    w_sb_f = _cols_from_slots(g_sb)
    w_o_f = g_o.reshape(D_MODEL, D_MODEL)
    w_up_p = _pair_cols(_cols_from_slots(g_up))
    w_down_f = g_down.reshape(D_FF, D_MODEL)
    conv_w_p = _pair_cols(_cols_from_slots(g_cw))
    y_sb, sb_run = _sb_fwd(qkv)
    pf = _mm(y_fox, w_fox_f, name="mm_fox_proj", out_dtype=BF16)
    ps = _mm(y_sb, w_sb_f, name="mm_sb_proj", out_dtype=BF16)
    (merged,) = _rowwise("gate_merge", lambda ga, gb, a, b: (_sigmoid(ga) * a + _sigmoid(gb) * b,),
                         [(gates, D_MODEL, 0), (gates, D_MODEL, 1), (pf, D_MODEL, 0), (ps, D_MODEL, 0)], [],
                         [(D_MODEL, BF16)])
    def ln1_mod(fb, xb, gt, g, b, sc, sh):
        xhat, _ = _ln_stats(ALPHA * xb + (1.0 + gt) * fb)
        y = xhat * g + b
        return fb, y, y * (1.0 + sc) + sh

    (attn_out, x1, u2), _ = _mm_pieces_nt(
        [merged], w_o_f, name="mm_w_o_ln1", epilogue=ln1_mod, rows=[x2], vecs=[gt1, _row(ln1_g), _row(ln1_b), sc2, sh2],
        out_dtypes=(F32, F32, BF16), w_kn=True)

    hpre = _mm(u2, w_up_p, name="mm_w_up", tn=1408)
    act = _conv_gate_fwd(hpre, conv_w_p, conv_b_p)

    def ln2_bwd(fb, xb, tb, gt, g, b):
        xhat, rstd = _ln_stats(ALPHA * xb + (1.0 + gt) * fb)
        err = (xhat * g + b) - tb
        dy = err * (1.0 / D_MODEL)
        dr = _ln_bwd(dy, xhat, rstd, g)
        return (dr * (1.0 + gt), ALPHA * dr,
                _colsum(err * err), _colsum(dy * xhat), _colsum(dy), _colsum(dr * fb))

    (dffn, dx1_res, sq_err, d_ln2_g, d_ln2_b, d_gt2), _ = _mm_pieces_nt(
        [act], w_down_f, name="mm_w_down_ln2_bwd", epilogue=ln2_bwd, rows=[x1, tgt],
        vecs=[gt2, _row(ln2_g), _row(ln2_b)], out_dtypes=(BF16, F32), n_sums=4, w_kn=True)
    loss_part = jnp.broadcast_to(0.5 * jnp.sum(sq_err) / D_MODEL, (1, LANES))

    d_w_down = _mm(act, dffn, name="mm_d_w_down", ta=True, tm=1408, tk=2048, out_dtype=BF16)
    d_act = _mm(dffn, w_down_f, name="mm_d_act", tb=True, tn=1408, out_dtype=BF16)
    dh, d_conv_b_p, d_conv_w_p = _conv_gate_bwd(hpre, d_act, conv_w_p, conv_b_p)
    dhpre = _conv_input_bwd(dh, conv_w_p)
    d_w_up_p = _mm(u2, dhpre, name="mm_d_w_up", ta=True, tn=1408, tk=2048, out_dtype=BF16)
    def ln1_bwd(du, dres, x1b, xb, fb, sc, gt, g):
        dx1 = dres + du * (1.0 + sc)
        xhat, rstd = _ln_stats(ALPHA * xb + (1.0 + gt) * fb)
        dr = _ln_bwd(dx1, xhat, rstd, g)
        return (dr * (1.0 + gt), ALPHA * dr,
                _colsum(du * x1b), _colsum(du), _colsum(dx1 * xhat), _colsum(dx1), _colsum(dr * fb))

    (d_attn, dx_res, d_sc2, d_sh2, d_ln1_g, d_ln1_b, d_gt1), _ = _mm_pieces_nt(
        [dhpre], w_up_p, name="mm_d_u2_ln1_bwd", epilogue=ln1_bwd, rows=[dx1_res, x1, x2, attn_out],
        vecs=[sc2, gt1, _row(ln1_g)], out_dtypes=(BF16, F32), n_sums=5, tm=256)

    d_w_o = _mm(merged, d_attn, name="mm_d_w_o", ta=True, out_dtype=BF16)

    def merge_bwd(dm, ga, gb, a, b):
        a, b = a.astype(F32), b.astype(F32)
        sa, sb = _sigmoid(ga), _sigmoid(gb)
        return dm * a * sa * (1.0 - sa), dm * b * sb * (1.0 - sb), dm * sa, dm * sb

    (d_ga, d_gb, d_pf, d_ps), _ = _mm_pieces_nt(
        [d_attn], w_o_f, name="mm_d_merged_gate_bwd", epilogue=merge_bwd, rows=[(gates, 0), (gates, 1), pf, ps],
        vecs=[], out_dtypes=(BF16,) * 4)
    d_w_fox = _mm(y_fox, d_pf, name="mm_d_w_fox", ta=True, out_dtype=BF16)
    d_w_sb = _mm(y_sb, d_ps, name="mm_d_w_sb", ta=True, out_dtype=BF16)
    d_y_fox = _mm(d_pf, w_fox_f, name="mm_d_y_fox", tb=True, out_dtype=BF16)
    d_y_sb = _mm(d_ps, w_sb_f, name="mm_d_y_sb", tb=True, out_dtype=BF16)
    early_grads = _Ride(
        [_cols_to_slots(d_w_fox), _cols_to_slots(d_w_sb), d_w_o.reshape(N_DEV, D_MODEL // N_DEV, D_MODEL),
         _cols_to_slots(_unpair_cols(d_w_up_p)), d_w_down.reshape(N_DEV, D_FF // N_DEV, D_MODEL)], scatter=True)
    (dq_a, dk_a, dv_a, d_cum_row), early_slots = _fox_bwd(qkv, cum_col, cum_row, y_fox32, lse, d_y_fox,
                                                                    ride=early_grads)
    dq_b, dk_b, dv_b = _sb_bwd(qkv, sb_run, d_y_sb)
    d_cum = jnp.transpose(d_cum_row.reshape(N_HEADS, s))
    d_cum = jnp.pad(d_cum, ((0, 0), (0, LANES - N_HEADS)))
    d_f, d_b_forget = _forget_bwd(d_cum, f_raw, b_f_pad)
    d_qkv = [dq_a, dk_a, dv_a, dq_b, dk_b, dv_b]
    g_qkv = list(_mm_pieces_tn(u1, d_qkv, name="mm_d_w_in_qkv"))
    g_ga, g_gb, g_f = _mm_pieces_tn(u1, [d_ga, d_gb, d_f], name="mm_d_w_in_gates")
    d_w_in_f = jnp.concatenate(g_qkv[:3] + [g_f[:, :N_HEADS]] + g_qkv[3:] + [g_ga, g_gb], axis=1)
    def x_bwd(du, dres, xb, sc):
        return dres + du * (1.0 + sc), _colsum(du * xb), _colsum(du)

    (grad_x, d_sc1, d_sh1), (in_slots,) = _mm_pieces_nt(
        d_qkv + [d_ga, d_gb, d_f], w_proj, name="mm_d_u1_x_bwd", epilogue=x_bwd, rows=[dx_res, x2], vecs=[sc1],
        n_sums=2, ride=_Ride([_cols_to_slots(d_w_in_f)], scatter=True))

    d_conv_b = _unpair_cols(d_conv_b_p)
    d_conv_w = _unpair_cols(d_conv_w_p)
    cb_pad = N_MOD * D_MODEL - 2 * D_FF
    small = jnp.concatenate(
        [d_sh1, d_sc1, d_gt1, d_sh2, d_sc2, d_gt2, jnp.pad(d_b_forget, ((0, 0), (0, D_MODEL - LANES))),
         d_ln1_g, d_ln1_b, d_ln2_g, d_ln2_b, jnp.pad(d_conv_b, ((0, 0), (0, cb_pad))),
         d_conv_w.reshape(1, 6 * D_FF), loss_part], axis=1)
    n_small = small.shape[1] // LANES - 1
    small = jnp.pad(small.reshape(n_small + 1, LANES), ((0, -(n_small + 1) % 8), (0, 0)))
    (small_all,) = _exchange("ag_small_grads", [small], scatter=False)
    loss = jnp.sum(small_all[:, n_small, 0])
    n_rep = n_small - 6 * D_FF // LANES
    cw8 = small_all[:, n_rep:n_small, :].reshape(N_DEV, 3, 2 * D_FF)
    cw8 = lax.dynamic_slice(cw8, (0, 0, me * UP_SHARD), (N_DEV, 3, UP_SHARD))
    dmod8 = small_all[:, :N_MOD * D_MODEL // LANES, :].reshape(N_DEV, N_MOD * D_MODEL)
    dmod_loc = lax.dynamic_slice(dmod8, (0, me * ADA_SHARD), (N_DEV, ADA_SHARD))

    def rows_of(a):
        return a.reshape(-1, LANES)

    def forget_rows(a):
        return rows_of(jnp.pad(a, (0, D_MODEL - N_HEADS)))

    def conv_b_rows(a):
        return rows_of(jnp.pad(a, (0, cb_pad)))

    rep_sizes = {"b_ada": N_MOD * D_MODEL, "b_forget": N_HEADS, "ln1_g": D_MODEL, "ln1_b": D_MODEL, "ln2_g": D_MODEL,
                 "ln2_b": D_MODEL, "conv_b": 2 * D_FF}
    rep_w = [rows_of(b_ada), forget_rows(b_forget), rows_of(ln1_g), rows_of(ln1_b), rows_of(ln2_g), rows_of(ln2_b),
             conv_b_rows(conv_b)]
    rep_m = [rows_of(m_b_ada), forget_rows(m_b_forget), rows_of(m_ln1_g), rows_of(m_ln1_b), rows_of(m_ln2_g),
             rows_of(m_ln2_b), conv_b_rows(m_conv_b)]
    rep_v = [rows_of(v_b_ada), forget_rows(v_b_forget), rows_of(v_ln1_g), rows_of(v_ln1_b), rows_of(v_ln2_g),
             rows_of(v_ln2_b), conv_b_rows(v_conv_b)]
    rep_out = _adamw_small(small_all, rep_w, rep_m, rep_v)
    rep = [{name: a.reshape(-1)[:size] for (name, size), a in zip(rep_sizes.items(), outs)} for outs in rep_out]
    r_conv_w = _adamw("adamw_conv_w", cw8, conv_w, m_conv_w, v_conv_w)
    r_ada = _adamw_ada(jnp.transpose(c_all.reshape(N_DEV, D_MODEL)), dmod_loc, w_ada, m_w_ada, v_w_ada)

    r_in = _adamw("adamw_w_in", in_slots, w_in, m_w_in, v_w_in)
    r_fox = _adamw("adamw_w_fox", early_slots[0], w_fox_proj, m_w_fox_proj, v_w_fox_proj)
    r_sb = _adamw("adamw_w_sb", early_slots[1], w_sb_proj, m_w_sb_proj, v_w_sb_proj)
    r_o = _adamw("adamw_w_o", early_slots[2], w_o, m_w_o, v_w_o)
    r_up = _adamw("adamw_w_up", early_slots[3], w_up, m_w_up, v_w_up)
    r_down = _adamw("adamw_w_down", early_slots[4], w_down, m_w_down, v_w_down)

    def leaf(i):
        return [r_ada[i], rep[i]["b_ada"], r_in[i], rep[i]["b_forget"], r_fox[i], r_sb[i], r_o[i], rep[i]["ln1_g"],
                rep[i]["ln1_b"], r_up[i], r_conv_w[i], rep[i]["conv_b"], r_down[i], rep[i]["ln2_g"], rep[i]["ln2_b"]]

    return (loss, grad_x.reshape(1, s, D_MODEL), *leaf(0), *leaf(1), *leaf(2), *leaf(3))
```

```python
import functools

import jax
import jax.numpy as jnp
from jax import lax
from jax.experimental import pallas as pl
from jax.experimental.pallas import tpu as pltpu

F32 = jnp.float32
BF16 = jnp.bfloat16
MESH = pl.DeviceIdType.MESH
ANY = pl.BlockSpec(memory_space=pl.ANY)

N_DEV = 8
D_MODEL = 1024
HEAD_DIM = 64
N_HEADS = 8
ATTN_W = N_HEADS * HEAD_DIM
D_FF = 2816
FF_HALF = D_FF // 2
N_MOD = 6
ADA_SHARD = N_MOD * D_MODEL // N_DEV
IN_SHARD = 641
UP_SHARD = 704
ATTN_SCALE = HEAD_DIM ** -0.5
ALPHA = 2.0 ** 0.25
LN_EPS = 1e-5
LANES = 128
TQ = 512
SCAN_W = 256
VMEM_LIMIT = 56 * 1024 * 1024

ADAM_LR, ADAM_B1, ADAM_B2, ADAM_EPS, ADAM_WD, ADAM_STEP = 0.001, 0.9, 0.999, 1e-08, 0.01, 10

W_QKV, W_GATES, W_F = 3072, 2048, 128
W_PROJ = 5376


def _params(sem=None):
    return pltpu.CompilerParams(dimension_semantics=sem, vmem_limit_bytes=VMEM_LIMIT)


def _tile(n, cap):
    if n <= cap:
        return n
    best = None
    for t in range(LANES, cap + 1, LANES):
        if n % t == 0:
            best = t
    assert best is not None, (n, cap)
    return best


def _row_tile(r, width, budget=192 * 1024):
    if r * width <= budget or r % 16:
        return r
    best = 16
    for t in range(16, r + 1, 16):
        if r % t == 0 and t * width <= budget:
            best = t
    return best


def _me():
    x, y, c = lax.axis_index("x"), lax.axis_index("y"), lax.axis_index("c")
    return x, y, c, 4 * x + 2 * y + c


def _peer(r):
    x, y, c, _ = _me()
    px = 1 - x if r & 4 else x
    py = 1 - y if r & 2 else y
    pc = 1 - c if r & 1 else c
    return (px, py, pc), 4 * px + 2 * py + pc


class _Ride:
    def __init__(self, arrays, scatter):
        self.arrays, self.scatter, self.n = list(arrays), scatter, len(arrays)
        self.in_specs = [ANY] * self.n
        self.out_specs = [ANY] * self.n
        self.out_shape = [jax.ShapeDtypeStruct(a.shape if scatter else (N_DEV,) + a.shape, a.dtype) for a in arrays]
        self.scratch = [pltpu.SemaphoreType.DMA((self.n, N_DEV - 1)), pltpu.SemaphoreType.DMA((self.n, N_DEV - 1)),
                        pltpu.SemaphoreType.DMA((self.n,))]

    def _local(self, ins, outs, sems, a):
        me = _me()[3]
        return pltpu.make_async_copy(ins[a].at[me] if self.scatter else ins[a], outs[a].at[me], sems[2].at[a])

    def _remote(self, ins, outs, sems, a, r, arriving):
        me = _me()[3]
        peer, pidx = _peer(r)
        src = ins[a].at[me if arriving else pidx] if self.scatter else ins[a]
        return pltpu.make_async_remote_copy(
            src_ref=src, dst_ref=outs[a].at[pidx if arriving else me], send_sem=sems[0].at[a, r - 1],
            recv_sem=sems[1].at[a, r - 1], device_id=peer, device_id_type=MESH)

    def start(self, ins, outs, sems):
        for a in range(self.n):
            self._local(ins, outs, sems, a).start()
        for r in range(1, N_DEV):
            for a in range(self.n):
                self._remote(ins, outs, sems, a, r, False).start()

    def wait(self, ins, outs, sems):
        for r in range(1, N_DEV):
            for a in range(self.n):
                self._remote(ins, outs, sems, a, r, True).wait_recv()
        for r in range(1, N_DEV):
            for a in range(self.n):
                self._remote(ins, outs, sems, a, r, False).wait_send()
        for a in range(self.n):
            self._local(ins, outs, sems, a).wait()


def _exchange(name, arrays, scatter):
    ride = _Ride(arrays, scatter)

    def body(*refs):
        ins, outs, sems = refs[:ride.n], refs[ride.n:2 * ride.n], refs[2 * ride.n:]
        ride.start(ins, outs, sems)
        ride.wait(ins, outs, sems)

    return pl.pallas_call(body, name=name, in_specs=ride.in_specs, out_specs=ride.out_specs, out_shape=ride.out_shape,
                          scratch_shapes=ride.scratch)(*arrays)


def _allgather_two_level(name, a):
    def body(a_ref, out_ref, send_sems, recv_sems, local_sem):
        x, y, c, me = _me()
        sibling = (x, y, 1 - c)
        chips = [(1 - x, y), (x, 1 - y), (1 - x, 1 - y)]

        def idx(px, py, pc):
            return 4 * px + 2 * py + pc

        def copy(k, block, to, src=None):
            slot = out_ref.at[idx(*block)]
            return pltpu.make_async_remote_copy(
                src_ref=slot if src is None else src, dst_ref=slot, send_sem=send_sems.at[k], recv_sem=recv_sems.at[k],
                device_id=to, device_id_type=MESH)

        mine = pltpu.make_async_copy(a_ref, out_ref.at[me], local_sem)
        mine.start()
        first = [copy(0, (x, y, c), sibling, src=a_ref)]
        first += [copy(1 + j, (x, y, c), (*chip, c), src=a_ref) for j, chip in enumerate(chips)]
        for cp in first:
            cp.start()
        passed = [copy(4 + j, (*chip, c), sibling) for j, chip in enumerate(chips)]
        for j, chip in enumerate(chips):
            copy(1 + j, (*chip, c), (x, y, c)).wait_recv()
            passed[j].start()
        copy(0, sibling, (x, y, c)).wait_recv()
        for j, chip in enumerate(chips):
            copy(4 + j, (*chip, 1 - c), (x, y, c)).wait_recv()
        for cp in first + passed:
            cp.wait_send()
        mine.wait()

    return pl.pallas_call(
        body, name=name, in_specs=[ANY], out_specs=ANY,
        out_shape=jax.ShapeDtypeStruct((N_DEV,) + a.shape, a.dtype),
        scratch_shapes=[pltpu.SemaphoreType.DMA((N_DEV - 1,)), pltpu.SemaphoreType.DMA((N_DEV - 1,)),
                        pltpu.SemaphoreType.DMA],
    )(a)


def _with_ride(body, ride, n_in, n_out, grid):
    if ride is None:
        return body
    n = ride.n

    def wrapped(*refs):
        ins, rins = refs[:n_in], refs[n_in:n_in + n]
        outs, routs = refs[n_in + n:n_in + n + n_out], refs[n_in + n + n_out:n_in + 2 * n + n_out]
        rest = refs[n_in + 2 * n + n_out:]
        scratch, sems = rest[:len(rest) - 3], rest[len(rest) - 3:]
        ids = [pl.program_id(d) for d in range(len(grid))]
        first = functools.reduce(lambda p, q: p & q, [i == 0 for i in ids])
        last = functools.reduce(lambda p, q: p & q, [i == g - 1 for i, g in zip(ids, grid)])

        @pl.when(first)
        def _():
            ride.start(rins, routs, sems)

        body(*ins, *outs, *scratch)

        @pl.when(last)
        def _():
            ride.wait(rins, routs, sems)

    return wrapped


def _ride_call(body, ride, *, name, grid, in_specs, out_specs, out_shape, scratch_shapes, sem, args):
    n_in, n_out = len(in_specs), len(out_specs)
    if ride is None:
        res = pl.pallas_call(body, name=name, grid=grid, in_specs=in_specs, out_specs=out_specs, out_shape=out_shape,
                             scratch_shapes=scratch_shapes, compiler_params=_params(sem))(*args)
        return list(res), []
    res = pl.pallas_call(
        _with_ride(body, ride, n_in, n_out, grid), name=name, grid=grid,
        in_specs=list(in_specs) + ride.in_specs, out_specs=list(out_specs) + ride.out_specs,
        out_shape=list(out_shape) + ride.out_shape, scratch_shapes=list(scratch_shapes) + ride.scratch,
        compiler_params=_params(("arbitrary",) * len(grid)))(*args, *ride.arrays)
    return list(res[:n_out]), list(res[n_out:])


def _mm(a, b, *, name, ta=False, tb=False, out_dtype=F32, tm=1024, tn=1024, tk=1024, ride=None):
    m, k = (a.shape[1], a.shape[0]) if ta else a.shape
    n = b.shape[0] if tb else b.shape[1]
    assert (b.shape[1] if tb else b.shape[0]) == k
    tm, tn, tk = _tile(m, tm), _tile(n, tn), _tile(k, tk)
    nk = k // tk
    a_spec = pl.BlockSpec((tk, tm), lambda i, j, l: (l, i)) if ta else pl.BlockSpec((tm, tk), lambda i, j, l: (i, l))
    b_spec = pl.BlockSpec((tn, tk), lambda i, j, l: (j, l)) if tb else pl.BlockSpec((tk, tn), lambda i, j, l: (l, j))
    dims = (((0,) if ta else (1,), (1,) if tb else (0,)), ((), ()))

    def body(a_ref, b_ref, o_ref, *acc):
        p = lax.dot_general(a_ref[...].astype(BF16), b_ref[...].astype(BF16), dims, preferred_element_type=F32)
        if nk == 1:
            o_ref[...] = p.astype(out_dtype)
            return
        acc_ref = acc[0]
        step = pl.program_id(2)

        @pl.when(step == 0)
        def _():
            acc_ref[...] = p

        @pl.when(step > 0)
        def _():
            acc_ref[...] += p

        @pl.when(step == nk - 1)
        def _():
            o_ref[...] = acc_ref[...].astype(out_dtype)

    outs, rode = _ride_call(
        body, ride, name=name, grid=(m // tm, n // tn, nk), in_specs=[a_spec, b_spec],
        out_specs=[pl.BlockSpec((tm, tn), lambda i, j, l: (i, j))], out_shape=[jax.ShapeDtypeStruct((m, n), out_dtype)],
        scratch_shapes=[] if nk == 1 else [pltpu.VMEM((tm, tn), F32)], sem=("parallel", "parallel", "arbitrary"),
        args=(a, b))
    return outs[0] if ride is None else (outs[0], rode)


def _mm_pieces_nt(pieces, w, *, name, epilogue, rows, vecs, out_dtypes=(F32,), n_sums=0, tm=512, w_kn=False, ride=None,
                  prologue=None):
    if prologue is not None:
        assert not pieces
        pro_fn, prows, pvecs = prologue
        prows = [r if isinstance(r, tuple) else (r, 0) for r in prows]
        s, widths = prows[0][0].shape[0], [w.shape[0] if w_kn else w.shape[1]]
        lead_args = [r for r, _ in prows] + list(pvecs)
        lead_specs = [pl.BlockSpec((min(tm, s), widths[0]), functools.partial(lambda i, cb: (i, cb), cb=cb))
                      for _, cb in prows] + [pl.BlockSpec(v.shape, lambda i: (0, 0)) for v in pvecs]
        out_dtypes = (BF16,) + tuple(out_dtypes)
    else:
        s, widths = pieces[0].shape[0], [p.shape[1] for p in pieces]
        lead_args = list(pieces)
        lead_specs = [pl.BlockSpec((min(tm, s), width), lambda i: (i, 0)) for width in widths]
    n = w.shape[1] if w_kn else w.shape[0]
    tm = min(tm, s)
    offs = [sum(widths[:i]) for i in range(len(widths))]
    rows = [r if isinstance(r, tuple) else (r, 0) for r in rows]
    n_p, n_r, n_v, n_o = len(lead_args), len(rows), len(vecs), len(out_dtypes)

    def body(*refs):
        p_refs, w_ref = refs[:n_p], refs[n_p]
        extra = [r[...] for r in refs[n_p + 1:n_p + 1 + n_r + n_v]]
        o_refs = refs[n_p + 1 + n_r + n_v:n_p + 1 + n_r + n_v + n_o]
        sum_refs = refs[n_p + 1 + n_r + n_v + n_o:]
        if prologue is not None:
            made = pro_fn(*[r[...] for r in p_refs]).astype(BF16)
            operands = [made]
        else:
            operands = [r[...] for r in p_refs]
        acc = None
        for operand, off, width in zip(operands, offs, widths):
            if w_kn:
                part = jnp.dot(operand, w_ref[off:off + width, :], preferred_element_type=F32)
            else:
                part = _dot_nt(operand, w_ref[:, off:off + width])
            acc = part if acc is None else acc + part
        res = epilogue(acc, *extra)
        if prologue is not None:
            res = (made,) + tuple(res)
        for o_ref, val in zip(o_refs, res[:n_o]):
            o_ref[...] = val.astype(o_ref.dtype)
        step = pl.program_id(0)
        for s_ref, val in zip(sum_refs, res[n_o:]):
            @pl.when(step == 0)
            def _(s_ref=s_ref, val=val):
                s_ref[...] = val

            @pl.when(step > 0)
            def _(s_ref=s_ref, val=val):
                s_ref[...] += val

    vec_spec = pl.BlockSpec((1, n), lambda i: (0, 0))
    out_widths = ([widths[0]] if prologue is not None else []) + [n] * (n_o - (prologue is not None))
    return _ride_call(
        body, ride, name=name, grid=(s // tm,),
        in_specs=lead_specs + [pl.BlockSpec(w.shape, lambda i: (0, 0))]
        + [pl.BlockSpec((tm, n), functools.partial(lambda i, cb: (i, cb), cb=cb)) for _, cb in rows] + [vec_spec] * n_v,
        out_specs=[pl.BlockSpec((tm, ow), lambda i: (i, 0)) for ow in out_widths] + [vec_spec] * n_sums,
        out_shape=[jax.ShapeDtypeStruct((s, ow), dt) for ow, dt in zip(out_widths, out_dtypes)]
        + [jax.ShapeDtypeStruct((1, n), F32)] * n_sums,
        scratch_shapes=[], sem=("arbitrary",) if n_sums else ("parallel",),
        args=(*lead_args, w, *[r for r, _ in rows], *vecs))


def _mm_pieces_tn(a, pieces, *, name, tk=1024):
    s, m = a.shape
    tk = min(tk, s)
    nk = s // tk
    n = len(pieces)
    dims = (((0,), (0,)), ((), ()))

    def body(*refs):
        a_ref, p_refs, o_refs, accs = refs[0], refs[1:1 + n], refs[1 + n:1 + 2 * n], refs[1 + 2 * n:]
        step = pl.program_id(0)
        a_blk = a_ref[...]
        for p_ref, o_ref, acc in zip(p_refs, o_refs, accs):
            part = lax.dot_general(a_blk, p_ref[...], dims, preferred_element_type=F32)

            @pl.when(step == 0)
            def _(acc=acc, part=part):
                acc[...] = part

            @pl.when(step > 0)
            def _(acc=acc, part=part):
                acc[...] += part

            @pl.when(step == nk - 1)
            def _(acc=acc, o_ref=o_ref):
                o_ref[...] = acc[...].astype(BF16)

    return pl.pallas_call(
        body, name=name, grid=(nk,),
        in_specs=[pl.BlockSpec((tk, m), lambda l: (l, 0))] + [pl.BlockSpec((tk, p.shape[1]), lambda l: (l, 0)) for p in pieces],
        out_specs=[pl.BlockSpec((m, p.shape[1]), lambda l: (0, 0)) for p in pieces],
        out_shape=[jax.ShapeDtypeStruct((m, p.shape[1]), BF16) for p in pieces],
        scratch_shapes=[pltpu.VMEM((m, p.shape[1]), F32) for p in pieces],
        compiler_params=_params(("arbitrary",)),
    )(a, *pieces)


def _rowwise(name, fn, rows, vecs, outs, sums=(), tr=512):
    s = rows[0][0].shape[0]
    tr = min(tr, s)
    nr, nv, no = len(rows), len(vecs), len(outs)

    def body(*refs):
        vals = [r[...] for r in refs[:nr + nv]]
        res = fn(*vals)
        for o_ref, val in zip(refs[nr + nv:nr + nv + no], res[:no]):
            o_ref[...] = val.astype(o_ref.dtype)
        step = pl.program_id(0)
        for s_ref, val in zip(refs[nr + nv + no:], res[no:]):
            @pl.when(step == 0)
            def _(s_ref=s_ref, val=val):
                s_ref[...] = val

            @pl.when(step > 0)
            def _(s_ref=s_ref, val=val):
                s_ref[...] += val

    in_specs = [pl.BlockSpec((tr, w), functools.partial(lambda i, cb: (i, cb), cb=cb)) for _, w, cb in rows]
    in_specs += [pl.BlockSpec(v.shape, lambda i: (0, 0)) for v in vecs]
    out_specs = [pl.BlockSpec((tr, w), lambda i: (i, 0)) for w, _ in outs]
    out_specs += [pl.BlockSpec((1, w), lambda i: (0, 0)) for w in sums]
    out_shape = [jax.ShapeDtypeStruct((s, w), dt) for w, dt in outs]
    out_shape += [jax.ShapeDtypeStruct((1, w), F32) for w in sums]
    return pl.pallas_call(
        body, name=name, grid=(s // tr,), in_specs=in_specs, out_specs=out_specs, out_shape=out_shape,
        compiler_params=_params(("arbitrary",) if sums else ("parallel",)),
    )(*[r[0] for r in rows], *vecs)


def _colsum(x):
    return jnp.sum(x, axis=0, keepdims=True)


def _sigmoid(x):
    return 1.0 / (1.0 + jnp.exp(-x))


def _log_sigmoid(x):
    return jnp.minimum(x, 0.0) - jnp.log(1.0 + jnp.exp(-jnp.abs(x)))


def _ln_stats(r):
    mu = jnp.mean(r, axis=-1, keepdims=True)
    xc = r - mu
    var = jnp.mean(xc * xc, axis=-1, keepdims=True)
    rstd = lax.rsqrt(var + LN_EPS)
    return xc * rstd, rstd


def _ln_bwd(dy, xhat, rstd, g):
    dxh = dy * g
    m1 = jnp.mean(dxh, axis=-1, keepdims=True)
    m2 = jnp.mean(dxh * xhat, axis=-1, keepdims=True)
    return rstd * (dxh - m1 - xhat * m2)


def _mod_exchange(c_row, w_ada, b_ada_loc):
    def body(c_ref, w_ref, b_ref, call_ref, mod_ref, piece_ref, send_sems, recv_sems):
        me = _me()[3]
        call_ref[me] = c_ref[...]
        sent = []
        for r in range(1, N_DEV):
            peer, _ = _peer(r)
            cp = pltpu.make_async_remote_copy(
                src_ref=c_ref, dst_ref=call_ref.at[me], send_sem=send_sems.at[0, r - 1],
                recv_sem=recv_sems.at[0, r - 1], device_id=peer, device_id_type=MESH)
            cp.start()
            sent.append(cp)
        for r in range(1, N_DEV):
            peer, pidx = _peer(r)
            pltpu.make_async_remote_copy(
                src_ref=c_ref, dst_ref=call_ref.at[pidx], send_sem=send_sems.at[0, r - 1],
                recv_sem=recv_sems.at[0, r - 1], device_id=peer, device_id_type=MESH).wait_recv()
        c_all = jnp.concatenate([call_ref[d] for d in range(N_DEV)], axis=0)
        mod_loc = jnp.dot(c_all, w_ref[...], preferred_element_type=F32,
                          precision=lax.Precision.HIGHEST) + b_ref[...]
        for d in range(N_DEV):
            piece_ref[d] = mod_loc[d:d + 1, :]
        mod_ref[me] = piece_ref[me]
        for r in range(1, N_DEV):
            peer, pidx = _peer(r)
            cp = pltpu.make_async_remote_copy(
                src_ref=piece_ref.at[pidx], dst_ref=mod_ref.at[me], send_sem=send_sems.at[1, r - 1],
                recv_sem=recv_sems.at[1, r - 1], device_id=peer, device_id_type=MESH)
            cp.start()
            sent.append(cp)
        for r in range(1, N_DEV):
            peer, pidx = _peer(r)
            pltpu.make_async_remote_copy(
                src_ref=piece_ref.at[me], dst_ref=mod_ref.at[pidx], send_sem=send_sems.at[1, r - 1],
                recv_sem=recv_sems.at[1, r - 1], device_id=peer, device_id_type=MESH).wait_recv()
        for cp in sent:
            cp.wait_send()

    vmem = pl.BlockSpec(memory_space=pltpu.VMEM)
    return pl.pallas_call(
        body, name="mod_exchange", in_specs=[vmem, vmem, vmem], out_specs=[vmem, vmem],
        out_shape=[jax.ShapeDtypeStruct((N_DEV, 1, D_MODEL), F32), jax.ShapeDtypeStruct((N_DEV, 1, ADA_SHARD), F32)],
        scratch_shapes=[pltpu.VMEM((N_DEV, 1, ADA_SHARD), F32),
                        pltpu.SemaphoreType.DMA((2, N_DEV - 1)), pltpu.SemaphoreType.DMA((2, N_DEV - 1))],
        compiler_params=_params(),
    )(c_row, w_ada, b_ada_loc)


def _split3(x):
    hi = x.astype(BF16)
    r1 = x - hi.astype(F32)
    mid = r1.astype(BF16)
    lo = (r1 - mid.astype(F32)).astype(BF16)
    return hi, mid, lo


def _scan_rows(x_ref, o_ref, s, reverse, pre=None, post=None):
    tb = min(TQ, s)
    nb = s // tb
    row = lax.broadcasted_iota(jnp.int32, (tb, tb), 0)
    col = lax.broadcasted_iota(jnp.int32, (tb, tb), 1)
    tri = jnp.where((col >= row) if reverse else (col <= row), 1.0, 0.0).astype(BF16)

    def step(i, carry):
        blk = (nb - 1 - i) if reverse else i
        off = pl.multiple_of(blk * tb, tb)
        x = x_ref[pl.ds(off, tb), :]
        if pre is not None:
            x = pre(x, off)
        acc = carry
        for piece in _split3(x):
            acc = acc + jnp.dot(tri, piece, preferred_element_type=F32)
        o_ref[pl.ds(off, tb), :] = acc if post is None else post(acc, off)
        edge = acc[0:1, :] if reverse else acc[tb - 1:tb, :]
        return jnp.broadcast_to(edge, (tb, LANES))

    lax.fori_loop(0, nb, step, jnp.zeros((tb, LANES), F32))


def _forget_cumsum(f_raw, b_pad):
    s = f_raw.shape[0]

    def body(f_ref, b_ref, cum_ref):
        b = b_ref[...]
        _scan_rows(f_ref, cum_ref, s, False, pre=lambda x, off: _log_sigmoid(x + b))

    vmem = pl.BlockSpec(memory_space=pltpu.VMEM)
    return pl.pallas_call(body, name="forget_cumsum", in_specs=[vmem, vmem], out_specs=vmem,
                          out_shape=jax.ShapeDtypeStruct((s, LANES), F32), compiler_params=_params())(f_raw, b_pad)


def _forget_bwd(dcum, f_raw, b_pad):
    s = f_raw.shape[0]

    def body(d_ref, f_ref, b_ref, df_ref, db_ref, tmp_ref):
        b = b_ref[...]
        _scan_rows(d_ref, tmp_ref, s, True)
        df = tmp_ref[...] * _sigmoid(-(f_ref[...] + b))
        df_ref[...] = df.astype(BF16)
        db_ref[...] = _colsum(df)

    vmem = pl.BlockSpec(memory_space=pltpu.VMEM)
    return pl.pallas_call(
        body, name="forget_bwd", in_specs=[vmem, vmem, vmem], out_specs=[vmem, vmem],
        out_shape=[jax.ShapeDtypeStruct((s, LANES), BF16), jax.ShapeDtypeStruct((1, LANES), F32)],
        scratch_shapes=[pltpu.VMEM((s, LANES), F32)], compiler_params=_params())(dcum, f_raw, b_pad)


def _dot_nt(a, b):
    return lax.dot_general(a, b, (((1,), (1,)), ((), ())), preferred_element_type=F32)


def _head_masks():
    lane = lax.broadcasted_iota(jnp.int32, (TQ, LANES), 1)
    return lane, [lane < HEAD_DIM, lane >= HEAD_DIM]


def _pick(mask, x):
    return jnp.where(mask, x, jnp.zeros_like(x))


def _qkv_specs(s, col0):
    nb = ATTN_W // LANES
    return [pl.BlockSpec((TQ, LANES), lambda hp, qi: (qi, col0 + hp)),
            pl.BlockSpec((s, LANES), lambda hp, qi: (0, col0 + nb + hp)),
            pl.BlockSpec((s, LANES), lambda hp, qi: (0, col0 + 2 * nb + hp))]


def _pair_spec():
    return pl.BlockSpec((TQ, LANES), lambda hp, qi: (qi, hp))


def _diag_mask(width, strict):
    row = lax.broadcasted_iota(jnp.int32, (TQ, width * TQ), 0) + (width - 1) * TQ
    col = lax.broadcasted_iota(jnp.int32, (TQ, width * TQ), 1)
    return (col < row) if strict else (col <= row)


def _fox_fwd(qkv, cum_col, cum_row, ride=None):
    s = qkv.shape[0]
    nq = s // TQ

    def body(q_ref, k_ref, v_ref, cc_ref, cr_ref, o_ref, o32_ref, lse_ref):
        hp, qi = pl.program_id(0), pl.program_id(1)
        lane, masks = _head_masks()
        q2 = q_ref[...] * jnp.asarray(ATTN_SCALE, BF16)
        cc = cc_ref[...]
        qms = [_pick(masks[e], q2) for e in range(2)]
        cqs = [jnp.sum(jnp.where(lane == 2 * hp + e, cc, 0.0), axis=1, keepdims=True) for e in range(2)]

        def tile(kb, carry, masked, width=1):
            off, span = pl.multiple_of(kb * TQ, TQ), width * TQ
            k2, v2 = k_ref[pl.ds(off, span), :], v_ref[pl.ds(off, span), :]
            head0 = lax.broadcasted_iota(jnp.int32, (span, LANES), 1) < HEAD_DIM
            new = []
            for e in range(2):
                m, acc = carry[e]
                sc = _dot_nt(qms[e], k2) + (cqs[e] - cr_ref[e:e + 1, pl.ds(off, span)])
                if masked:
                    sc = jnp.where(_diag_mask(width, False), sc, -jnp.inf)
                m_new = jnp.maximum(m, jnp.max(sc, axis=1, keepdims=True))
                p = jnp.exp(sc - m_new)
                v_ones = jnp.where(head0 if e == 0 else ~head0, v2, jnp.ones_like(v2))
                acc = jnp.exp(m - m_new) * acc + jnp.dot(p.astype(BF16), v_ones, preferred_element_type=F32)
                new.append((m_new, acc))
            return tuple(new)

        init = (jnp.full((TQ, 1), -jnp.inf, F32), jnp.zeros((TQ, LANES), F32))
        carry = lax.cond(qi % 2 == 1, lambda cr: tile(qi - 1, cr, True, 2), lambda cr: tile(qi, cr, True), (init, init))
        quads = qi // 4
        carry = lax.fori_loop(0, quads, lambda j, cr: tile(4 * j, cr, False, 4), carry)
        carry = lax.cond(qi % 4 >= 2, lambda cr: tile(4 * quads, cr, False, 2), lambda cr: cr, carry)
        sums = [jnp.max(jnp.where(masks[1 - e], carry[e][1], 0.0), axis=1, keepdims=True) for e in range(2)]
        outs = [carry[e][1] / sums[e] for e in range(2)]
        lses = [carry[e][0] + jnp.log(sums[e]) for e in range(2)]
        out = jnp.where(masks[0], outs[0], outs[1])
        o_ref[...] = out.astype(BF16)
        o32_ref[...] = out
        lse_ref[...] = jnp.where(masks[0], lses[0], lses[1])

    return _ride_call(
        body, ride, name="fox_fwd", grid=(N_HEADS // 2, nq),
        in_specs=_qkv_specs(s, 0) + [pl.BlockSpec((TQ, LANES), lambda hp, qi: (qi, 0)),
                                     pl.BlockSpec((None, 2, s), lambda hp, qi: (hp, 0, 0))],
        out_specs=[_pair_spec(), _pair_spec(), _pair_spec()],
        out_shape=[jax.ShapeDtypeStruct((s, ATTN_W), BF16), jax.ShapeDtypeStruct((s, ATTN_W), F32),
                   jax.ShapeDtypeStruct((s, ATTN_W), F32)],
        scratch_shapes=[], sem=("parallel", "parallel"), args=(qkv, qkv, qkv, cum_col, cum_row))


def _write_transposed(acc_ref, out_ref):
    for c in range(out_ref.shape[0] // TQ):
        out_ref[c * TQ:(c + 1) * TQ, :] = jnp.transpose(acc_ref[:, c * TQ:(c + 1) * TQ]).astype(BF16)


def _fox_bwd(qkv, cum_col, cum_row, o, lse, do, ride=None):
    s = qkv.shape[0]
    nq = s // TQ

    def body(q_ref, k_ref, v_ref, cc_ref, cr_ref, o_ref, lse_ref, do_ref,
             dq_ref, dk_ref, dv_ref, dcr_ref, dk_acc, dv_acc):
        hp, qi = pl.program_id(0), pl.program_id(1)

        @pl.when(qi == 0)
        def _():
            dk_acc[...] = jnp.zeros_like(dk_acc)
            dv_acc[...] = jnp.zeros_like(dv_acc)
            dcr_ref[...] = jnp.zeros_like(dcr_ref)

        lane, masks = _head_masks()
        q2 = q_ref[...] * jnp.asarray(ATTN_SCALE, BF16)
        do2 = do_ref[...]
        prod = do2.astype(F32) * o_ref[...].astype(F32)
        lse2 = lse_ref[...]
        cc = cc_ref[...]
        qms = [_pick(masks[e], q2) for e in range(2)]
        doms = [_pick(masks[e], do2) for e in range(2)]
        deltas = [jnp.sum(jnp.where(masks[e], prod, 0.0), axis=1, keepdims=True) for e in range(2)]
        lses = [jnp.max(jnp.where(masks[e], lse2, -jnp.inf), axis=1, keepdims=True) for e in range(2)]
        cqs = [jnp.sum(jnp.where(lane == 2 * hp + e, cc, 0.0), axis=1, keepdims=True) for e in range(2)]
        qts = [jnp.transpose(qms[e].astype(F32)).astype(BF16) for e in range(2)]
        dots = [jnp.transpose(doms[e].astype(F32)).astype(BF16) for e in range(2)]
        ones_row = [HEAD_DIM * (1 - e) for e in range(2)]
        trow = lax.broadcasted_iota(jnp.int32, (LANES, TQ), 0)
        qts = [jnp.where(trow == ones_row[e], jnp.ones_like(qts[e]), qts[e]) for e in range(2)]

        def tile(kb, carry, masked, width=1):
            off, span = pl.multiple_of(kb * TQ, TQ), width * TQ
            k2, v2 = k_ref[pl.ds(off, span), :], v_ref[pl.ds(off, span), :]
            head0 = lax.broadcasted_iota(jnp.int32, (span, LANES), 1) < HEAD_DIM
            new, dks, dv = [], [], None
            for e in range(2):
                dq = carry[e]
                sc = _dot_nt(qms[e], k2) + (cqs[e] - cr_ref[e:e + 1, pl.ds(off, span)])
                p = jnp.exp(sc - lses[e])
                if masked:
                    p = jnp.where(_diag_mask(width, False), p, 0.0)
                ds = p * (_dot_nt(doms[e], v2) - deltas[e])
                dsb = ds.astype(BF16)
                dk_e = jnp.dot(qts[e], dsb, preferred_element_type=F32)
                dv_e = jnp.dot(dots[e], p.astype(BF16), preferred_element_type=F32)
                dks.append(dk_e)
                dv = dv_e if e == 0 else dv + dv_e
                dcr_ref[e:e + 1, pl.ds(off, span)] -= dk_e[ones_row[e]:ones_row[e] + 1, :]
                k_ones = jnp.where(head0 if e == 0 else ~head0, k2, jnp.ones_like(k2))
                new.append(dq + jnp.dot(dsb, k_ones, preferred_element_type=F32))
            krow = lax.broadcasted_iota(jnp.int32, (LANES, span), 0)
            dk_acc[:, pl.ds(off, span)] += jnp.where(krow < HEAD_DIM, dks[0], dks[1])
            dv_acc[:, pl.ds(off, span)] += dv
            return tuple(new)

        init = jnp.zeros((TQ, LANES), F32)
        carry = lax.fori_loop(0, qi // 2, lambda j, cr: tile(2 * j, cr, False, 2), (init, init))
        carry = lax.cond(qi % 2 == 1, lambda cr: tile(qi - 1, cr, True, 2), lambda cr: tile(qi, cr, True), carry)
        dq_ref[...] = (jnp.where(masks[0], carry[0], carry[1]) * ATTN_SCALE).astype(BF16)
        for e in range(2):
            dcr_ref[e:e + 1, pl.ds(pl.multiple_of(qi * TQ, TQ), TQ)] += jnp.transpose(carry[e])[
                ones_row[e]:ones_row[e] + 1, :]

        @pl.when(qi == nq - 1)
        def _():
            _write_transposed(dk_acc, dk_ref)
            _write_transposed(dv_acc, dv_ref)

    seq_spec = pl.BlockSpec((s, LANES), lambda hp, qi: (0, hp))
    return _ride_call(
        body, ride, name="fox_bwd", grid=(N_HEADS // 2, nq),
        in_specs=_qkv_specs(s, 0) + [pl.BlockSpec((TQ, LANES), lambda hp, qi: (qi, 0)),
                                     pl.BlockSpec((None, 2, s), lambda hp, qi: (hp, 0, 0)),
                                     _pair_spec(), _pair_spec(), _pair_spec()],
        out_specs=[_pair_spec(), seq_spec, seq_spec, pl.BlockSpec((None, 2, s), lambda hp, qi: (hp, 0, 0))],
        out_shape=[jax.ShapeDtypeStruct((s, ATTN_W), BF16)] * 3 + [jax.ShapeDtypeStruct((N_HEADS // 2, 2, s), F32)],
        scratch_shapes=[pltpu.VMEM((LANES, s), F32), pltpu.VMEM((LANES, s), F32)],
        sem=("parallel", "arbitrary"), args=(qkv, qkv, qkv, cum_col, cum_row, o, lse, do))


def _scan_matrix(reverse):
    row = lax.broadcasted_iota(jnp.int32, (SCAN_W, SCAN_W), 0)
    col = lax.broadcasted_iota(jnp.int32, (SCAN_W, SCAN_W), 1)
    return jnp.where((row > col) if reverse else (row < col), 1.0, 0.0).astype(BF16)


def _scan_cols(x, tri, reverse, init):
    nblk = x.shape[1] // SCAN_W
    parts, total = [None] * nblk, init
    far = 0 if reverse else SCAN_W - 1
    for b in (reversed(range(nblk)) if reverse else range(nblk)):
        blk = x[:, b * SCAN_W:(b + 1) * SCAN_W]
        part = jnp.dot(blk.astype(BF16), tri, preferred_element_type=F32)
        parts[b] = part + total
        total = total + (part[:, far:far + 1] + blk[:, far:far + 1])
    return (parts[0] if nblk == 1 else jnp.concatenate(parts, axis=1)), total


def _sb_logits(qm, k2):
    z = _dot_nt(qm, k2)
    neg_abs = lax.bitcast_convert_type(lax.bitcast_convert_type(z, jnp.uint32) | jnp.uint32(0x80000000), F32)
    soft = jnp.log(1.0 + jnp.exp(neg_abs))
    lb = jnp.minimum(z, 0.0) - soft
    return lb, lb - z


TILE_SLOTS = 4


def _tri_base(qi):
    return (qi * (qi + 1)) // 2


def _sb_fwd(qkv):
    s = qkv.shape[0]
    nq = s // TQ

    def body(q_ref, k_ref, v_ref, o_ref, t_ref, buf, sems):
        hp, qi = pl.program_id(0), pl.program_id(1)
        _, masks = _head_masks()
        suffix = _scan_matrix(True)
        q2 = q_ref[...] * jnp.asarray(ATTN_SCALE, BF16)
        qms = [_pick(masks[e], q2) for e in range(2)]
        base = _tri_base(qi)

        def store(e, kb):
            slot = kb % TILE_SLOTS
            return pltpu.make_async_copy(buf.at[e, slot], t_ref.at[2 * hp + e, base + kb], sems.at[e, slot])

        def tile(kb, carry, masked, width=1):
            off, span = pl.multiple_of(kb * TQ, TQ), width * TQ
            k2, v2 = k_ref[pl.ds(off, span), :], v_ref[pl.ds(off, span), :]
            new = []
            for e in range(2):
                run, acc = carry[e]
                lb, lo = _sb_logits(qms[e], k2)
                if masked:
                    strict = _diag_mask(width, True)
                    lo = jnp.where(strict, lo, 0.0)
                rest, run = _scan_cols(lo, suffix, True, run)
                a = jnp.exp(lb + rest)
                if masked:
                    a = jnp.where(strict, a, 0.0)
                ab, lbb = a.astype(BF16), lb.astype(BF16)
                acc = acc + jnp.dot(ab, v2, preferred_element_type=F32)
                for w in range(width):
                    blk = kb + w

                    @pl.when(blk + TILE_SLOTS <= qi)
                    def _(e=e, blk=blk):
                        store(e, blk + TILE_SLOTS).wait()
                    buf[e, blk % TILE_SLOTS, 0] = ab[:, w * TQ:(w + 1) * TQ]
                    buf[e, blk % TILE_SLOTS, 1] = lbb[:, w * TQ:(w + 1) * TQ]
                    store(e, blk).start()
                new.append((run, acc))
            return tuple(new)

        init = (jnp.zeros((TQ, 1), F32), jnp.zeros((TQ, LANES), F32))
        carry = lax.cond(qi % 2 == 1, lambda cr: tile(qi - 1, cr, True, 2), lambda cr: tile(qi, cr, True), (init, init))
        pairs = qi // 2
        carry = lax.fori_loop(0, pairs, lambda it, cr: tile(2 * (pairs - 1 - it), cr, False, 2), carry)
        for e in range(2):
            for blk in range(TILE_SLOTS):
                @pl.when(qi >= blk)
                def _(e=e, blk=blk):
                    store(e, blk).wait()
        o_ref[...] = jnp.where(masks[0], carry[0][1], carry[1][1]).astype(BF16)

    ntri = nq * (nq + 1) // 2
    return pl.pallas_call(
        body, name="sb_fwd", grid=(N_HEADS // 2, nq), in_specs=_qkv_specs(s, 3 * ATTN_W // LANES),
        out_specs=[_pair_spec(), ANY],
        out_shape=[jax.ShapeDtypeStruct((s, ATTN_W), BF16), jax.ShapeDtypeStruct((N_HEADS, ntri, 2, TQ, TQ), BF16)],
        scratch_shapes=[pltpu.VMEM((2, TILE_SLOTS, 2, TQ, TQ), BF16), pltpu.SemaphoreType.DMA((2, TILE_SLOTS))],
        compiler_params=_params(("arbitrary", "arbitrary")),
    )(qkv, qkv, qkv)


def _sb_bwd(qkv, tiles, do):
    s = qkv.shape[0]
    nq = s // TQ

    def body(q_ref, k_ref, v_ref, t_ref, do_ref, dq_ref, dk_ref, dv_ref, dk_acc, dv_acc, buf, sems):
        hp, qi = pl.program_id(0), pl.program_id(1)

        @pl.when(qi == 0)
        def _():
            dk_acc[...] = jnp.zeros_like(dk_acc)
            dv_acc[...] = jnp.zeros_like(dv_acc)

        _, masks = _head_masks()
        prefix = _scan_matrix(False)
        q2 = q_ref[...] * jnp.asarray(ATTN_SCALE, BF16)
        do2 = do_ref[...]
        qms = [_pick(masks[e], q2) for e in range(2)]
        doms = [_pick(masks[e], do2) for e in range(2)]
        qts = [jnp.transpose(qms[e].astype(F32)).astype(BF16) for e in range(2)]
        dots = [jnp.transpose(doms[e].astype(F32)).astype(BF16) for e in range(2)]
        base = _tri_base(qi)

        def fetch(e, kb):
            slot = kb % TILE_SLOTS
            return pltpu.make_async_copy(t_ref.at[2 * hp + e, base + kb], buf.at[e, slot], sems.at[e, slot])

        for e in range(2):
            fetch(e, 0).start()

            @pl.when(qi >= 1)
            def _(e=e):
                fetch(e, 1).start()

        def tile(kb, carry, masked, width=1):
            off, span = pl.multiple_of(kb * TQ, TQ), width * TQ
            k2, v2 = k_ref[pl.ds(off, span), :], v_ref[pl.ds(off, span), :]
            new, dk, dv = [], None, None
            for e in range(2):
                gsum, dq = carry[e]
                if not masked:
                    for blk in range(2, 2 + width):
                        @pl.when(kb + blk <= qi)
                        def _(e=e, blk=blk):
                            fetch(e, kb + blk).start()
                for w in range(width):
                    fetch(e, kb + w).wait()
                slots = [(kb + w) % TILE_SLOTS for w in range(width)]
                ab = buf[e, slots[0], 0] if width == 1 else jnp.concatenate([buf[e, sl, 0] for sl in slots], axis=1)
                lbb = buf[e, slots[0], 1] if width == 1 else jnp.concatenate([buf[e, sl, 1] for sl in slots], axis=1)
                beta = jnp.exp(lbb.astype(F32))
                g = ab.astype(F32) * _dot_nt(doms[e], v2)
                before, gsum = _scan_cols(g, prefix, False, gsum)
                dz = g - beta * (g + before)
                if masked:
                    dz = jnp.where(_diag_mask(width, True), dz, 0.0)
                dzb = dz.astype(BF16)
                dk_e = jnp.dot(qts[e], dzb, preferred_element_type=F32)
                dv_e = jnp.dot(dots[e], ab, preferred_element_type=F32)
                dk, dv = (dk_e, dv_e) if e == 0 else (dk + dk_e, dv + dv_e)
                new.append((gsum, dq + jnp.dot(dzb, k2, preferred_element_type=F32)))
            dk_acc[:, pl.ds(off, span)] += dk
            dv_acc[:, pl.ds(off, span)] += dv
            return tuple(new)

        init = (jnp.zeros((TQ, 1), F32), jnp.zeros((TQ, LANES), F32))
        carry = lax.fori_loop(0, qi // 2, lambda j, cr: tile(2 * j, cr, False, 2), (init, init))
        carry = lax.cond(qi % 2 == 1, lambda cr: tile(qi - 1, cr, True, 2), lambda cr: tile(qi, cr, True), carry)
        dq_ref[...] = (jnp.where(masks[0], carry[0][1], carry[1][1]) * ATTN_SCALE).astype(BF16)

        @pl.when(qi == nq - 1)
        def _():
            _write_transposed(dk_acc, dk_ref)
            _write_transposed(dv_acc, dv_ref)

    seq_spec = pl.BlockSpec((s, LANES), lambda hp, qi: (0, hp))
    return pl.pallas_call(
        body, name="sb_bwd", grid=(N_HEADS // 2, nq),
        in_specs=_qkv_specs(s, 3 * ATTN_W // LANES) + [ANY, _pair_spec()],
        out_specs=[_pair_spec(), seq_spec, seq_spec],
        out_shape=[jax.ShapeDtypeStruct((s, ATTN_W), BF16)] * 3,
        scratch_shapes=[pltpu.VMEM((LANES, s), F32), pltpu.VMEM((LANES, s), F32),
                        pltpu.VMEM((2, TILE_SLOTS, 2, TQ, TQ), BF16), pltpu.SemaphoreType.DMA((2, TILE_SLOTS))],
        compiler_params=_params(("arbitrary", "arbitrary")),
    )(qkv, qkv, qkv, tiles, do)


CONV_TR = 512


def _shift_down(x, halo, n):
    rolled = pltpu.roll(x, n, 0)
    head = rolled[0:8, :]
    rid = lax.broadcasted_iota(jnp.int32, head.shape, 0)
    for j in range(n):
        head = jnp.where(rid == j, halo[8 - n + j:8 - n + j + 1, :], head)
    return jnp.concatenate([head, rolled[8:, :]], axis=0)


def _shift_up(x, halo, n):
    rows = x.shape[0]
    rolled = pltpu.roll(x, rows - n, 0)
    tail = rolled[rows - 8:, :]
    rid = lax.broadcasted_iota(jnp.int32, tail.shape, 0)
    for j in range(n):
        tail = jnp.where(rid == 8 - n + j, halo[j:j + 1, :], tail)
    return jnp.concatenate([rolled[:rows - 8, :], tail], axis=0)


def _conv_fwd_block(x, halo, w, b):
    return b + _shift_down(x, halo, 2) * w[0:1, :] + _shift_down(x, halo, 1) * w[1:2, :] + x * w[2:3, :]


def _conv_specs(tr, s):
    pair = 2 * FF_HALF
    blk = pl.BlockSpec((tr, pair), lambda j, i: (i, j))
    prev = pl.BlockSpec((8, pair), lambda j, i: (jnp.maximum(i * (tr // 8) - 1, 0), j))
    nxt = pl.BlockSpec((8, pair), lambda j, i: (jnp.minimum((i + 1) * (tr // 8), s // 8 - 1), j))
    return blk, prev, nxt


def _conv_gate_fwd(hpre, conv_w, conv_b):
    s = hpre.shape[0]
    tr = min(CONV_TR, s)
    blk, prev, _ = _conv_specs(tr, s)

    def body(x_ref, halo_ref, w_ref, b_ref, a_ref):
        i = pl.program_id(1)
        halo = jnp.where(i > 0, halo_ref[...], 0.0)
        h = _conv_fwd_block(x_ref[...], halo, w_ref[...], b_ref[...])
        hg, hv = h[:, :FF_HALF], h[:, FF_HALF:]
        a_ref[...] = (hg * _sigmoid(hg) * hv).astype(BF16)

    return pl.pallas_call(
        body, name="conv_gate_fwd", grid=(2, s // tr),
        in_specs=[blk, prev, pl.BlockSpec((3, 2 * FF_HALF), lambda j, i: (0, j)),
                  pl.BlockSpec((1, 2 * FF_HALF), lambda j, i: (0, j))],
        out_specs=pl.BlockSpec((tr, FF_HALF), lambda j, i: (i, j)),
        out_shape=jax.ShapeDtypeStruct((s, D_FF), BF16),
        compiler_params=_params(("parallel", "parallel")),
    )(hpre, hpre, conv_w, conv_b)


def _conv_gate_bwd(hpre, da, conv_w, conv_b):
    s = hpre.shape[0]
    tr = min(CONV_TR, s)
    blk, prev, _ = _conv_specs(tr, s)

    def body(x_ref, halo_ref, da_ref, w_ref, b_ref, dh_ref, db_ref, dw_ref):
        i = pl.program_id(1)
        halo = jnp.where(i > 0, halo_ref[...], 0.0)
        x = x_ref[...]
        h = _conv_fwd_block(x, halo, w_ref[...], b_ref[...])
        hg, hv = h[:, :FF_HALF], h[:, FF_HALF:]
        da_blk = da_ref[...].astype(F32)
        sg = _sigmoid(hg)
        dhg = da_blk * hv * (sg * (1.0 + hg * (1.0 - sg)))
        dhv = da_blk * (hg * sg)
        dh_ref[:, :FF_HALF] = dhg.astype(BF16)
        dh_ref[:, FF_HALF:] = dhv.astype(BF16)
        x2, x1 = _shift_down(x, halo, 2), _shift_down(x, halo, 1)
        parts = []
        for lo, dpart in ((0, dhg), (FF_HALF, dhv)):
            cols = slice(lo, lo + FF_HALF)
            parts.append((cols, _colsum(dpart), _colsum(dpart * x2[:, cols]), _colsum(dpart * x1[:, cols]),
                          _colsum(dpart * x[:, cols])))

        @pl.when(i == 0)
        def _():
            for cols, db, dw0, dw1, dw2 in parts:
                db_ref[:, cols] = db
                dw_ref[0:1, cols] = dw0
                dw_ref[1:2, cols] = dw1
                dw_ref[2:3, cols] = dw2

        @pl.when(i > 0)
        def _():
            for cols, db, dw0, dw1, dw2 in parts:
                db_ref[:, cols] += db
                dw_ref[0:1, cols] += dw0
                dw_ref[1:2, cols] += dw1
                dw_ref[2:3, cols] += dw2

    pair = 2 * FF_HALF
    return pl.pallas_call(
        body, name="conv_gate_bwd", grid=(2, s // tr),
        in_specs=[blk, prev, pl.BlockSpec((tr, FF_HALF), lambda j, i: (i, j)),
                  pl.BlockSpec((3, pair), lambda j, i: (0, j)), pl.BlockSpec((1, pair), lambda j, i: (0, j))],
        out_specs=[blk, pl.BlockSpec((1, pair), lambda j, i: (0, j)), pl.BlockSpec((3, pair), lambda j, i: (0, j))],
        out_shape=[jax.ShapeDtypeStruct((s, 2 * D_FF), BF16), jax.ShapeDtypeStruct((1, 2 * D_FF), F32),
                   jax.ShapeDtypeStruct((3, 2 * D_FF), F32)],
        compiler_params=_params(("parallel", "arbitrary")),
    )(hpre, hpre, da, conv_w, conv_b)


def _conv_input_bwd(dh, conv_w):
    s = dh.shape[0]
    tr = min(CONV_TR, s)
    blk, _, _ = _conv_specs(tr, s)
    nblk = s // tr

    def body(x_ref, halo_ref, w_ref, o_ref):
        i = pl.program_id(1)
        halo = jnp.where(i < nblk - 1, halo_ref[...].astype(F32), 0.0)
        x, w = x_ref[...].astype(F32), w_ref[...]
        o_ref[...] = (x * w[2:3, :] + _shift_up(x, halo, 1) * w[1:2, :] + _shift_up(x, halo, 2) * w[0:1, :]).astype(BF16)

    nxt = pl.BlockSpec((16, 2 * FF_HALF), lambda j, i: (jnp.minimum((i + 1) * (tr // 16), s // 16 - 1), j))
    return pl.pallas_call(
        body, name="conv_input_bwd", grid=(2, nblk),
        in_specs=[blk, nxt, pl.BlockSpec((3, 2 * FF_HALF), lambda j, i: (0, j))], out_specs=blk,
        out_shape=jax.ShapeDtypeStruct((s, 2 * D_FF), BF16),
        compiler_params=_params(("parallel", "parallel")),
    )(dh, dh, conv_w)


def _adamw_math(w, g, m, v):
    m = ADAM_B1 * m + (1.0 - ADAM_B1) * g
    v = ADAM_B2 * v + (1.0 - ADAM_B2) * (g * g)
    m_hat = m / (1.0 - ADAM_B1 ** ADAM_STEP)
    v_hat = v / (1.0 - ADAM_B2 ** ADAM_STEP)
    delta = -ADAM_LR * (m_hat / (jnp.sqrt(v_hat) + ADAM_EPS) + ADAM_WD * w)
    return delta, m, v


def _adamw(name, g8, w, m, v):
    r, c = w.shape
    tr = _row_tile(r, c)

    def body(g_ref, w_ref, m_ref, v_ref, go_ref, d_ref, mo_ref, vo_ref):
        g = g_ref[0].astype(F32)
        for d in range(1, N_DEV):
            g = g + g_ref[d].astype(F32)
        delta, mn, vn = _adamw_math(w_ref[...], g, m_ref[...], v_ref[...])
        go_ref[...] = g
        d_ref[...] = delta
        mo_ref[...] = mn
        vo_ref[...] = vn

    spec = pl.BlockSpec((tr, c), lambda i: (i, 0))
    return pl.pallas_call(
        body, name=name, grid=(r // tr,),
        in_specs=[pl.BlockSpec((N_DEV, tr, c), lambda i: (0, i, 0)), spec, spec, spec], out_specs=[spec] * 4,
        out_shape=[jax.ShapeDtypeStruct((r, c), F32)] * 4, compiler_params=_params(("parallel",)),
    )(g8, w, m, v)


def _adamw_small(g8, ws, ms, vs):
    n = len(ws)
    offsets = [sum(w.shape[0] for w in ws[:i]) for i in range(n)]

    def body(*refs):
        g_ref, w_refs, m_refs, v_refs = refs[0], refs[1:1 + n], refs[1 + n:1 + 2 * n], refs[1 + 2 * n:1 + 3 * n]
        outs = refs[1 + 3 * n:]
        for i in range(n):
            rows = w_refs[i].shape[0]
            g = g_ref[0, offsets[i]:offsets[i] + rows, :]
            for d in range(1, N_DEV):
                g = g + g_ref[d, offsets[i]:offsets[i] + rows, :]
            delta, mn, vn = _adamw_math(w_refs[i][...], g, m_refs[i][...], v_refs[i][...])
            for k, val in enumerate((g, delta, mn, vn)):
                outs[k * n + i][...] = val

    vmem = pl.BlockSpec(memory_space=pltpu.VMEM)
    res = pl.pallas_call(
        body, name="adamw_small", in_specs=[vmem] * (1 + 3 * n), out_specs=[vmem] * (4 * n),
        out_shape=[jax.ShapeDtypeStruct(w.shape, F32) for _ in range(4) for w in ws], compiler_params=_params(),
    )(g8, *ws, *ms, *vs)
    return [res[k * n:(k + 1) * n] for k in range(4)]


def _adamw_ada(c_t, dmod, w, m, v):
    r, c = w.shape
    tr = _row_tile(r, c)

    def body(ct_ref, dm_ref, w_ref, m_ref, v_ref, go_ref, d_ref, mo_ref, vo_ref):
        ct, dm = ct_ref[...], dm_ref[...]
        g = ct[:, 0:1] * dm[0:1, :]
        for b in range(1, N_DEV):
            g = g + ct[:, b:b + 1] * dm[b:b + 1, :]
        delta, mn, vn = _adamw_math(w_ref[...], g, m_ref[...], v_ref[...])
        go_ref[...] = g
        d_ref[...] = delta
        mo_ref[...] = mn
        vo_ref[...] = vn

    spec = pl.BlockSpec((tr, c), lambda i: (i, 0))
    return pl.pallas_call(
        body, name="adamw_w_ada", grid=(r // tr,),
        in_specs=[pl.BlockSpec((tr, N_DEV), lambda i: (i, 0)), pl.BlockSpec((N_DEV, c), lambda i: (0, 0)),
                  spec, spec, spec],
        out_specs=[spec] * 4, out_shape=[jax.ShapeDtypeStruct((r, c), F32)] * 4,
        compiler_params=_params(("parallel",)),
    )(c_t, dmod, w, m, v)


def _cols_from_slots(g):
    n, r, c = g.shape
    return jnp.transpose(g, (1, 0, 2)).reshape(r, n * c)


def _cols_to_slots(w):
    r, c = w.shape
    return jnp.transpose(w.reshape(r, N_DEV, c // N_DEV), (1, 0, 2))


def _pair_cols(w):
    g0, g1 = w[..., 0:FF_HALF], w[..., FF_HALF:D_FF]
    v0, v1 = w[..., D_FF:D_FF + FF_HALF], w[..., D_FF + FF_HALF:]
    return jnp.concatenate([g0, v0, g1, v1], axis=-1)


def _unpair_cols(w):
    g0, v0 = w[..., 0:FF_HALF], w[..., FF_HALF:D_FF]
    g1, v1 = w[..., D_FF:D_FF + FF_HALF], w[..., D_FF + FF_HALF:]
    return jnp.concatenate([g0, g1, v0, v1], axis=-1)


def _row(v):
    return v.reshape(1, -1)


def kernel(x, c, w_ada, b_ada, w_in, b_forget, w_fox_proj, w_sb_proj, w_o, ln1_g, ln1_b, w_up, conv_w, conv_b, w_down, ln2_g, ln2_b, loss_target, m_w_ada, m_b_ada, m_w_in, m_b_forget, m_w_fox_proj, m_w_sb_proj, m_w_o, m_ln1_g, m_ln1_b, m_w_up, m_conv_w, m_conv_b, m_w_down, m_ln2_g, m_ln2_b, v_w_ada, v_b_ada, v_w_in, v_b_forget, v_w_fox_proj, v_w_sb_proj, v_w_o, v_ln1_g, v_ln1_b, v_w_up, v_conv_w, v_conv_b, v_w_down, v_ln2_g, v_ln2_b):
    s = x.shape[1]
    me = 4 * lax.axis_index("x") + 2 * lax.axis_index("y") + lax.axis_index("c")
    x2 = x.reshape(s, D_MODEL)
    tgt = loss_target.reshape(s, D_MODEL)

    b_ada_loc = lax.dynamic_slice(b_ada, (me * ADA_SHARD,), (ADA_SHARD,)).reshape(1, ADA_SHARD)
    c_all, mod = _mod_exchange(c, w_ada, b_ada_loc)
    mod = mod.reshape(N_MOD, 1, D_MODEL)
    sh1, sc1, gt1, sh2, sc2, gt2 = [mod[i] for i in range(N_MOD)]

    g_in = _allgather_two_level("ag_w_in", w_in.astype(BF16))
    late_weights = _Ride([w_fox_proj.astype(BF16), w_sb_proj.astype(BF16), w_o.astype(BF16), w_up.astype(BF16),
                          w_down.astype(BF16), conv_w], scatter=False)
    w_in_f = _cols_from_slots(g_in)
    w_proj = jnp.concatenate(
        [w_in_f[:, 0:1536], w_in_f[:, 1544:3080], w_in_f[:, 3080:5128], w_in_f[:, 1536:1544],
         jnp.zeros((D_MODEL, W_PROJ - 5128), BF16)], axis=1)
    w_qkv, w_gates, w_f = w_proj[:, :W_QKV], w_proj[:, W_QKV:W_QKV + W_GATES], w_proj[:, W_QKV + W_GATES:W_QKV + W_GATES + W_F]
    conv_b_p = _pair_cols(_row(conv_b))
    b_f_pad = jnp.pad(_row(b_forget), ((0, 0), (0, LANES - N_HEADS)))

    (u1, qkv), _ = _mm_pieces_nt(
        [], w_qkv, name="mm_qkv", prologue=(lambda xb, sc, sh: xb * (1.0 + sc) + sh, [x2], [sc1, sh1]),
        epilogue=lambda acc: (acc,), rows=[], vecs=[], out_dtypes=(BF16,), w_kn=True)
    gates = _mm(u1, w_gates, name="mm_gates")
    f_raw = _mm(u1, w_f, name="mm_forget")
    cum_col = _forget_cumsum(f_raw, b_f_pad)
    cum_row = jnp.transpose(cum_col[:, :N_HEADS]).reshape(N_HEADS // 2, 2, s)
    (y_fox, y_fox32, lse), (g_fox, g_sb, g_o, g_up, g_down, g_cw) = _fox_fwd(qkv, cum_col, cum_row, ride=late_weights)
    w_fox_f = _cols_from_slots(g_fox)
    w_sb_f = _cols_from_slots(g_sb)
    w_o_f = g_o.reshape(D_MODEL, D_MODEL)
    w_up_p = _pair_cols(_cols_from_slots(g_up))
    w_down_f = g_down.reshape(D_FF, D_MODEL)
    conv_w_p = _pair_cols(_cols_from_slots(g_cw))
    y_sb, sb_run = _sb_fwd(qkv)
    pf = _mm(y_fox, w_fox_f, name="mm_fox_proj", out_dtype=BF16)
    ps = _mm(y_sb, w_sb_f, name="mm_sb_proj", out_dtype=BF16)
    def ln1_mod(fb, xb, gt, g, b, sc, sh):
        xhat, _ = _ln_stats(ALPHA * xb + (1.0 + gt) * fb)
        y = xhat * g + b
        return fb, y, y * (1.0 + sc) + sh

    (merged, attn_out, x1, u2), _ = _mm_pieces_nt(
        [], w_o_f, name="mm_w_o_ln1",
        prologue=(lambda ga, gb, a, b: _sigmoid(ga) * a + _sigmoid(gb) * b, [(gates, 0), (gates, 1), pf, ps], []),
        epilogue=ln1_mod, rows=[x2], vecs=[gt1, _row(ln1_g), _row(ln1_b), sc2, sh2],
        out_dtypes=(F32, F32, BF16), w_kn=True)

    hpre = _mm(u2, w_up_p, name="mm_w_up", tn=1408)
    act = _conv_gate_fwd(hpre, conv_w_p, conv_b_p)

    def ln2_bwd(fb, xb, tb, gt, g, b):
        xhat, rstd = _ln_stats(ALPHA * xb + (1.0 + gt) * fb)
        err = (xhat * g + b) - tb
        dy = err * (1.0 / D_MODEL)
        dr = _ln_bwd(dy, xhat, rstd, g)
        return (dr * (1.0 + gt), ALPHA * dr,
                _colsum(err * err), _colsum(dy * xhat), _colsum(dy), _colsum(dr * fb))

    (dffn, dx1_res, sq_err, d_ln2_g, d_ln2_b, d_gt2), _ = _mm_pieces_nt(
        [act], w_down_f, name="mm_w_down_ln2_bwd", epilogue=ln2_bwd, rows=[x1, tgt],
        vecs=[gt2, _row(ln2_g), _row(ln2_b)], out_dtypes=(BF16, F32), n_sums=4, w_kn=True)
    loss_part = jnp.broadcast_to(0.5 * jnp.sum(sq_err) / D_MODEL, (1, LANES))

    d_w_down = _mm(act, dffn, name="mm_d_w_down", ta=True, tm=1408, tk=2048, out_dtype=BF16)
    d_act = _mm(dffn, w_down_f, name="mm_d_act", tb=True, tn=1408, out_dtype=BF16)
    dh, d_conv_b_p, d_conv_w_p = _conv_gate_bwd(hpre, d_act, conv_w_p, conv_b_p)
    dhpre = _conv_input_bwd(dh, conv_w_p)
    d_w_up_p = _mm(u2, dhpre, name="mm_d_w_up", ta=True, tn=1408, tk=2048, out_dtype=BF16)
    def ln1_bwd(du, dres, x1b, xb, fb, sc, gt, g):
        dx1 = dres + du * (1.0 + sc)
        xhat, rstd = _ln_stats(ALPHA * xb + (1.0 + gt) * fb)
        dr = _ln_bwd(dx1, xhat, rstd, g)
        return (dr * (1.0 + gt), ALPHA * dr,
                _colsum(du * x1b), _colsum(du), _colsum(dx1 * xhat), _colsum(dx1), _colsum(dr * fb))

    (d_attn, dx_res, d_sc2, d_sh2, d_ln1_g, d_ln1_b, d_gt1), _ = _mm_pieces_nt(
        [dhpre], w_up_p, name="mm_d_u2_ln1_bwd", epilogue=ln1_bwd, rows=[dx1_res, x1, x2, attn_out],
        vecs=[sc2, gt1, _row(ln1_g)], out_dtypes=(BF16, F32), n_sums=5, tm=256)

    d_w_o = _mm(merged, d_attn, name="mm_d_w_o", ta=True, out_dtype=BF16)

    def merge_bwd(dm, ga, gb, a, b):
        a, b = a.astype(F32), b.astype(F32)
        sa, sb = _sigmoid(ga), _sigmoid(gb)
        return dm * a * sa * (1.0 - sa), dm * b * sb * (1.0 - sb), dm * sa, dm * sb

    (d_ga, d_gb, d_pf, d_ps), _ = _mm_pieces_nt(
        [d_attn], w_o_f, name="mm_d_merged_gate_bwd", epilogue=merge_bwd, rows=[(gates, 0), (gates, 1), pf, ps],
        vecs=[], out_dtypes=(BF16,) * 4)
    d_w_fox = _mm(y_fox, d_pf, name="mm_d_w_fox", ta=True, out_dtype=BF16)
    d_w_sb = _mm(y_sb, d_ps, name="mm_d_w_sb", ta=True, out_dtype=BF16)
    d_y_fox = _mm(d_pf, w_fox_f, name="mm_d_y_fox", tb=True, out_dtype=BF16)
    d_y_sb = _mm(d_ps, w_sb_f, name="mm_d_y_sb", tb=True, out_dtype=BF16)
    early_grads = _Ride(
        [_cols_to_slots(d_w_fox), _cols_to_slots(d_w_sb), d_w_o.reshape(N_DEV, D_MODEL // N_DEV, D_MODEL),
         _cols_to_slots(_unpair_cols(d_w_up_p)), d_w_down.reshape(N_DEV, D_FF // N_DEV, D_MODEL)], scatter=True)
    (dq_a, dk_a, dv_a, d_cum_row), early_slots = _fox_bwd(qkv, cum_col, cum_row, y_fox32, lse, d_y_fox,
                                                                    ride=early_grads)
    dq_b, dk_b, dv_b = _sb_bwd(qkv, sb_run, d_y_sb)
    d_cum = jnp.transpose(d_cum_row.reshape(N_HEADS, s))
    d_cum = jnp.pad(d_cum, ((0, 0), (0, LANES - N_HEADS)))
    d_f, d_b_forget = _forget_bwd(d_cum, f_raw, b_f_pad)
    d_qkv = [dq_a, dk_a, dv_a, dq_b, dk_b, dv_b]
    g_qkv = list(_mm_pieces_tn(u1, d_qkv, name="mm_d_w_in_qkv"))
    g_ga, g_gb, g_f = _mm_pieces_tn(u1, [d_ga, d_gb, d_f], name="mm_d_w_in_gates")
    d_w_in_f = jnp.concatenate(g_qkv[:3] + [g_f[:, :N_HEADS]] + g_qkv[3:] + [g_ga, g_gb], axis=1)
    def x_bwd(du, dres, xb, sc):
        return dres + du * (1.0 + sc), _colsum(du * xb), _colsum(du)

    (grad_x, d_sc1, d_sh1), (in_slots,) = _mm_pieces_nt(
        d_qkv + [d_ga, d_gb, d_f], w_proj, name="mm_d_u1_x_bwd", epilogue=x_bwd, rows=[dx_res, x2], vecs=[sc1],
        n_sums=2, ride=_Ride([_cols_to_slots(d_w_in_f)], scatter=True))

    d_conv_b = _unpair_cols(d_conv_b_p)
    d_conv_w = _unpair_cols(d_conv_w_p)
    cb_pad = N_MOD * D_MODEL - 2 * D_FF
    small = jnp.concatenate(
        [d_sh1, d_sc1, d_gt1, d_sh2, d_sc2, d_gt2, jnp.pad(d_b_forget, ((0, 0), (0, D_MODEL - LANES))),
         d_ln1_g, d_ln1_b, d_ln2_g, d_ln2_b, jnp.pad(d_conv_b, ((0, 0), (0, cb_pad))),
         d_conv_w.reshape(1, 6 * D_FF), loss_part], axis=1)
    n_small = small.shape[1] // LANES - 1
    small = jnp.pad(small.reshape(n_small + 1, LANES), ((0, -(n_small + 1) % 8), (0, 0)))
    (small_all,) = _exchange("ag_small_grads", [small], scatter=False)
    loss = jnp.sum(small_all[:, n_small, 0])
    n_rep = n_small - 6 * D_FF // LANES
    cw8 = small_all[:, n_rep:n_small, :].reshape(N_DEV, 3, 2 * D_FF)
    cw8 = lax.dynamic_slice(cw8, (0, 0, me * UP_SHARD), (N_DEV, 3, UP_SHARD))
    dmod8 = small_all[:, :N_MOD * D_MODEL // LANES, :].reshape(N_DEV, N_MOD * D_MODEL)
    dmod_loc = lax.dynamic_slice(dmod8, (0, me * ADA_SHARD), (N_DEV, ADA_SHARD))

    def rows_of(a):
        return a.reshape(-1, LANES)

    def forget_rows(a):
        return rows_of(jnp.pad(a, (0, D_MODEL - N_HEADS)))

    def conv_b_rows(a):
        return rows_of(jnp.pad(a, (0, cb_pad)))

    rep_sizes = {"b_ada": N_MOD * D_MODEL, "b_forget": N_HEADS, "ln1_g": D_MODEL, "ln1_b": D_MODEL, "ln2_g": D_MODEL,
                 "ln2_b": D_MODEL, "conv_b": 2 * D_FF}
    rep_w = [rows_of(b_ada), forget_rows(b_forget), rows_of(ln1_g), rows_of(ln1_b), rows_of(ln2_g), rows_of(ln2_b),
             conv_b_rows(conv_b)]
    rep_m = [rows_of(m_b_ada), forget_rows(m_b_forget), rows_of(m_ln1_g), rows_of(m_ln1_b), rows_of(m_ln2_g),
             rows_of(m_ln2_b), conv_b_rows(m_conv_b)]
    rep_v = [rows_of(v_b_ada), forget_rows(v_b_forget), rows_of(v_ln1_g), rows_of(v_ln1_b), rows_of(v_ln2_g),
             rows_of(v_ln2_b), conv_b_rows(v_conv_b)]
    rep_out = _adamw_small(small_all, rep_w, rep_m, rep_v)
    rep = [{name: a.reshape(-1)[:size] for (name, size), a in zip(rep_sizes.items(), outs)} for outs in rep_out]
    r_conv_w = _adamw("adamw_conv_w", cw8, conv_w, m_conv_w, v_conv_w)
    r_ada = _adamw_ada(jnp.transpose(c_all.reshape(N_DEV, D_MODEL)), dmod_loc, w_ada, m_w_ada, v_w_ada)

    r_in = _adamw("adamw_w_in", in_slots, w_in, m_w_in, v_w_in)
    r_fox = _adamw("adamw_w_fox", early_slots[0], w_fox_proj, m_w_fox_proj, v_w_fox_proj)
    r_sb = _adamw("adamw_w_sb", early_slots[1], w_sb_proj, m_w_sb_proj, v_w_sb_proj)
    r_o = _adamw("adamw_w_o", early_slots[2], w_o, m_w_o, v_w_o)
    r_up = _adamw("adamw_w_up", early_slots[3], w_up, m_w_up, v_w_up)
    r_down = _adamw("adamw_w_down", early_slots[4], w_down, m_w_down, v_w_down)

    def leaf(i):
        return [r_ada[i], rep[i]["b_ada"], r_in[i], rep[i]["b_forget"], r_fox[i], r_sb[i], r_o[i], rep[i]["ln1_g"],
                rep[i]["ln1_b"], r_up[i], r_conv_w[i], rep[i]["conv_b"], r_down[i], rep[i]["ln2_g"], rep[i]["ln2_b"]]

    return (loss, grad_x.reshape(1, s, D_MODEL), *leaf(0), *leaf(1), *leaf(2), *leaf(3))
```

```python
import functools

import jax
import jax.numpy as jnp
from jax import lax
from jax.experimental import pallas as pl
from jax.experimental.pallas import tpu as pltpu

F32 = jnp.float32
BF16 = jnp.bfloat16
MESH = pl.DeviceIdType.MESH
ANY = pl.BlockSpec(memory_space=pl.ANY)

N_DEV = 8
D_MODEL = 1024
HEAD_DIM = 64
N_HEADS = 8
ATTN_W = N_HEADS * HEAD_DIM
D_FF = 2816
FF_HALF = D_FF // 2
N_MOD = 6
ADA_SHARD = N_MOD * D_MODEL // N_DEV
IN_SHARD = 641
UP_SHARD = 704
ATTN_SCALE = HEAD_DIM ** -0.5
ALPHA = 2.0 ** 0.25
LN_EPS = 1e-5
LANES = 128
TQ = 512
SCAN_W = 256
VMEM_LIMIT = 56 * 1024 * 1024

ADAM_LR, ADAM_B1, ADAM_B2, ADAM_EPS, ADAM_WD, ADAM_STEP = 0.001, 0.9, 0.999, 1e-08, 0.01, 10

W_QKV, W_GATES, W_F = 3072, 2048, 128
W_PROJ = 5376


def _params(sem=None):
    return pltpu.CompilerParams(dimension_semantics=sem, vmem_limit_bytes=VMEM_LIMIT)


def _tile(n, cap):
    if n <= cap:
        return n
    best = None
    for t in range(LANES, cap + 1, LANES):
        if n % t == 0:
            best = t
    assert best is not None, (n, cap)
    return best


def _row_tile(r, width, budget=192 * 1024):
    if r * width <= budget or r % 16:
        return r
    best = 16
    for t in range(16, r + 1, 16):
        if r % t == 0 and t * width <= budget:
            best = t
    return best


def _me():
    x, y, c = lax.axis_index("x"), lax.axis_index("y"), lax.axis_index("c")
    return x, y, c, 4 * x + 2 * y + c


def _peer(r):
    x, y, c, _ = _me()
    px = 1 - x if r & 4 else x
    py = 1 - y if r & 2 else y
    pc = 1 - c if r & 1 else c
    return (px, py, pc), 4 * px + 2 * py + pc


class _Ride:
    def __init__(self, arrays, scatter):
        self.arrays, self.scatter, self.n = list(arrays), scatter, len(arrays)
        self.in_specs = [ANY] * self.n
        self.out_specs = [ANY] * self.n
        self.out_shape = [jax.ShapeDtypeStruct(a.shape if scatter else (N_DEV,) + a.shape, a.dtype) for a in arrays]
        self.scratch = [pltpu.SemaphoreType.DMA((self.n, N_DEV - 1)), pltpu.SemaphoreType.DMA((self.n, N_DEV - 1)),
                        pltpu.SemaphoreType.DMA((self.n,))]

    def _local(self, ins, outs, sems, a):
        me = _me()[3]
        return pltpu.make_async_copy(ins[a].at[me] if self.scatter else ins[a], outs[a].at[me], sems[2].at[a])

    def _remote(self, ins, outs, sems, a, r, arriving):
        me = _me()[3]
        peer, pidx = _peer(r)
        src = ins[a].at[me if arriving else pidx] if self.scatter else ins[a]
        return pltpu.make_async_remote_copy(
            src_ref=src, dst_ref=outs[a].at[pidx if arriving else me], send_sem=sems[0].at[a, r - 1],
            recv_sem=sems[1].at[a, r - 1], device_id=peer, device_id_type=MESH)

    def start(self, ins, outs, sems):
        for a in range(self.n):
            self._local(ins, outs, sems, a).start()
        for r in range(1, N_DEV):
            for a in range(self.n):
                self._remote(ins, outs, sems, a, r, False).start()

    def wait(self, ins, outs, sems):
        for r in range(1, N_DEV):
            for a in range(self.n):
                self._remote(ins, outs, sems, a, r, True).wait_recv()
        for r in range(1, N_DEV):
            for a in range(self.n):
                self._remote(ins, outs, sems, a, r, False).wait_send()
        for a in range(self.n):
            self._local(ins, outs, sems, a).wait()


def _exchange(name, arrays, scatter):
    ride = _Ride(arrays, scatter)

    def body(*refs):
        ins, outs, sems = refs[:ride.n], refs[ride.n:2 * ride.n], refs[2 * ride.n:]
        ride.start(ins, outs, sems)
        ride.wait(ins, outs, sems)

    return pl.pallas_call(body, name=name, in_specs=ride.in_specs, out_specs=ride.out_specs, out_shape=ride.out_shape,
                          scratch_shapes=ride.scratch)(*arrays)


def _allgather_two_level(name, a):
    def body(a_ref, out_ref, send_sems, recv_sems, local_sem):
        x, y, c, me = _me()
        sibling = (x, y, 1 - c)
        chips = [(1 - x, y), (x, 1 - y), (1 - x, 1 - y)]

        def idx(px, py, pc):
            return 4 * px + 2 * py + pc

        def copy(k, block, to, src=None):
            slot = out_ref.at[idx(*block)]
            return pltpu.make_async_remote_copy(
                src_ref=slot if src is None else src, dst_ref=slot, send_sem=send_sems.at[k], recv_sem=recv_sems.at[k],
                device_id=to, device_id_type=MESH)

        mine = pltpu.make_async_copy(a_ref, out_ref.at[me], local_sem)
        mine.start()
        first = [copy(0, (x, y, c), sibling, src=a_ref)]
        first += [copy(1 + j, (x, y, c), (*chip, c), src=a_ref) for j, chip in enumerate(chips)]
        for cp in first:
            cp.start()
        passed = [copy(4 + j, (*chip, c), sibling) for j, chip in enumerate(chips)]
        for j, chip in enumerate(chips):
            copy(1 + j, (*chip, c), (x, y, c)).wait_recv()
            passed[j].start()
        copy(0, sibling, (x, y, c)).wait_recv()
        for j, chip in enumerate(chips):
            copy(4 + j, (*chip, 1 - c), (x, y, c)).wait_recv()
        for cp in first + passed:
            cp.wait_send()
        mine.wait()

    return pl.pallas_call(
        body, name=name, in_specs=[ANY], out_specs=ANY,
        out_shape=jax.ShapeDtypeStruct((N_DEV,) + a.shape, a.dtype),
        scratch_shapes=[pltpu.SemaphoreType.DMA((N_DEV - 1,)), pltpu.SemaphoreType.DMA((N_DEV - 1,)),
                        pltpu.SemaphoreType.DMA],
    )(a)


def _with_ride(body, ride, n_in, n_out, grid):
    if ride is None:
        return body
    n = ride.n

    def wrapped(*refs):
        ins, rins = refs[:n_in], refs[n_in:n_in + n]
        outs, routs = refs[n_in + n:n_in + n + n_out], refs[n_in + n + n_out:n_in + 2 * n + n_out]
        rest = refs[n_in + 2 * n + n_out:]
        scratch, sems = rest[:len(rest) - 3], rest[len(rest) - 3:]
        ids = [pl.program_id(d) for d in range(len(grid))]
        first = functools.reduce(lambda p, q: p & q, [i == 0 for i in ids])
        last = functools.reduce(lambda p, q: p & q, [i == g - 1 for i, g in zip(ids, grid)])

        @pl.when(first)
        def _():
            ride.start(rins, routs, sems)

        body(*ins, *outs, *scratch)

        @pl.when(last)
        def _():
            ride.wait(rins, routs, sems)

    return wrapped


def _ride_call(body, ride, *, name, grid, in_specs, out_specs, out_shape, scratch_shapes, sem, args):
    n_in, n_out = len(in_specs), len(out_specs)
    if ride is None:
        res = pl.pallas_call(body, name=name, grid=grid, in_specs=in_specs, out_specs=out_specs, out_shape=out_shape,
                             scratch_shapes=scratch_shapes, compiler_params=_params(sem))(*args)
        return list(res), []
    res = pl.pallas_call(
        _with_ride(body, ride, n_in, n_out, grid), name=name, grid=grid,
        in_specs=list(in_specs) + ride.in_specs, out_specs=list(out_specs) + ride.out_specs,
        out_shape=list(out_shape) + ride.out_shape, scratch_shapes=list(scratch_shapes) + ride.scratch,
        compiler_params=_params(("arbitrary",) * len(grid)))(*args, *ride.arrays)
    return list(res[:n_out]), list(res[n_out:])


def _mm(a, b, *, name, ta=False, tb=False, out_dtype=F32, tm=1024, tn=1024, tk=1024, ride=None):
    m, k = (a.shape[1], a.shape[0]) if ta else a.shape
    n = b.shape[0] if tb else b.shape[1]
    assert (b.shape[1] if tb else b.shape[0]) == k
    tm, tn, tk = _tile(m, tm), _tile(n, tn), _tile(k, tk)
    nk = k // tk
    a_spec = pl.BlockSpec((tk, tm), lambda i, j, l: (l, i)) if ta else pl.BlockSpec((tm, tk), lambda i, j, l: (i, l))
    b_spec = pl.BlockSpec((tn, tk), lambda i, j, l: (j, l)) if tb else pl.BlockSpec((tk, tn), lambda i, j, l: (l, j))
    dims = (((0,) if ta else (1,), (1,) if tb else (0,)), ((), ()))

    def body(a_ref, b_ref, o_ref, *acc):
        p = lax.dot_general(a_ref[...].astype(BF16), b_ref[...].astype(BF16), dims, preferred_element_type=F32)
        if nk == 1:
            o_ref[...] = p.astype(out_dtype)
            return
        acc_ref = acc[0]
        step = pl.program_id(2)

        @pl.when(step == 0)
        def _():
            acc_ref[...] = p

        @pl.when(step > 0)
        def _():
            acc_ref[...] += p

        @pl.when(step == nk - 1)
        def _():
            o_ref[...] = acc_ref[...].astype(out_dtype)

    outs, rode = _ride_call(
        body, ride, name=name, grid=(m // tm, n // tn, nk), in_specs=[a_spec, b_spec],
        out_specs=[pl.BlockSpec((tm, tn), lambda i, j, l: (i, j))], out_shape=[jax.ShapeDtypeStruct((m, n), out_dtype)],
        scratch_shapes=[] if nk == 1 else [pltpu.VMEM((tm, tn), F32)], sem=("parallel", "parallel", "arbitrary"),
        args=(a, b))
    return outs[0] if ride is None else (outs[0], rode)


def _mm_pieces_nt(pieces, w, *, name, epilogue, rows, vecs, out_dtypes=(F32,), n_sums=0, tm=512, w_kn=False, ride=None,
                  prologue=None):
    if prologue is not None:
        assert not pieces
        pro_fn, prows, pvecs = prologue
        prows = [r if isinstance(r, tuple) else (r, 0) for r in prows]
        s, widths = prows[0][0].shape[0], [w.shape[0] if w_kn else w.shape[1]]
        lead_args = [r for r, _ in prows] + list(pvecs)
        lead_specs = [pl.BlockSpec((min(tm, s), widths[0]), functools.partial(lambda i, cb: (i, cb), cb=cb))
                      for _, cb in prows] + [pl.BlockSpec(v.shape, lambda i: (0, 0)) for v in pvecs]
        out_dtypes = (BF16,) + tuple(out_dtypes)
    else:
        s, widths = pieces[0].shape[0], [p.shape[1] for p in pieces]
        lead_args = list(pieces)
        lead_specs = [pl.BlockSpec((min(tm, s), width), lambda i: (i, 0)) for width in widths]
    n = w.shape[1] if w_kn else w.shape[0]
    tm = min(tm, s)
    offs = [sum(widths[:i]) for i in range(len(widths))]
    rows = [r if isinstance(r, tuple) else (r, 0) for r in rows]
    n_p, n_r, n_v, n_o = len(lead_args), len(rows), len(vecs), len(out_dtypes)

    def body(*refs):
        p_refs, w_ref = refs[:n_p], refs[n_p]
        extra = [r[...] for r in refs[n_p + 1:n_p + 1 + n_r + n_v]]
        o_refs = refs[n_p + 1 + n_r + n_v:n_p + 1 + n_r + n_v + n_o]
        sum_refs = refs[n_p + 1 + n_r + n_v + n_o:]
        if prologue is not None:
            made = pro_fn(*[r[...] for r in p_refs]).astype(BF16)
            operands = [made]
        else:
            operands = [r[...] for r in p_refs]
        acc = None
        for operand, off, width in zip(operands, offs, widths):
            if w_kn:
                part = jnp.dot(operand, w_ref[off:off + width, :], preferred_element_type=F32)
            else:
                part = _dot_nt(operand, w_ref[:, off:off + width])
            acc = part if acc is None else acc + part
        res = epilogue(acc, *extra)
        if prologue is not None:
            res = (made,) + tuple(res)
        for o_ref, val in zip(o_refs, res[:n_o]):
            o_ref[...] = val.astype(o_ref.dtype)
        step = pl.program_id(0)
        for s_ref, val in zip(sum_refs, res[n_o:]):
            @pl.when(step == 0)
            def _(s_ref=s_ref, val=val):
                s_ref[...] = val

            @pl.when(step > 0)
            def _(s_ref=s_ref, val=val):
                s_ref[...] += val

    vec_spec = pl.BlockSpec((1, n), lambda i: (0, 0))
    out_widths = ([widths[0]] if prologue is not None else []) + [n] * (n_o - (prologue is not None))
    return _ride_call(
        body, ride, name=name, grid=(s // tm,),
        in_specs=lead_specs + [pl.BlockSpec(w.shape, lambda i: (0, 0))]
        + [pl.BlockSpec((tm, n), functools.partial(lambda i, cb: (i, cb), cb=cb)) for _, cb in rows] + [vec_spec] * n_v,
        out_specs=[pl.BlockSpec((tm, ow), lambda i: (i, 0)) for ow in out_widths] + [vec_spec] * n_sums,
        out_shape=[jax.ShapeDtypeStruct((s, ow), dt) for ow, dt in zip(out_widths, out_dtypes)]
        + [jax.ShapeDtypeStruct((1, n), F32)] * n_sums,
        scratch_shapes=[], sem=("arbitrary",) if n_sums else ("parallel",),
        args=(*lead_args, w, *[r for r, _ in rows], *vecs))


def _mm_pieces_tn(a, pieces, *, name, tk=1024):
    s, m = a.shape
    tk = min(tk, s)
    nk = s // tk
    n = len(pieces)
    dims = (((0,), (0,)), ((), ()))

    def body(*refs):
        a_ref, p_refs, o_refs, accs = refs[0], refs[1:1 + n], refs[1 + n:1 + 2 * n], refs[1 + 2 * n:]
        step = pl.program_id(0)
        a_blk = a_ref[...]
        for p_ref, o_ref, acc in zip(p_refs, o_refs, accs):
            part = lax.dot_general(a_blk, p_ref[...], dims, preferred_element_type=F32)

            @pl.when(step == 0)
            def _(acc=acc, part=part):
                acc[...] = part

            @pl.when(step > 0)
            def _(acc=acc, part=part):
                acc[...] += part

            @pl.when(step == nk - 1)
            def _(acc=acc, o_ref=o_ref):
                o_ref[...] = acc[...].astype(BF16)

    return pl.pallas_call(
        body, name=name, grid=(nk,),
        in_specs=[pl.BlockSpec((tk, m), lambda l: (l, 0))] + [pl.BlockSpec((tk, p.shape[1]), lambda l: (l, 0)) for p in pieces],
        out_specs=[pl.BlockSpec((m, p.shape[1]), lambda l: (0, 0)) for p in pieces],
        out_shape=[jax.ShapeDtypeStruct((m, p.shape[1]), BF16) for p in pieces],
        scratch_shapes=[pltpu.VMEM((m, p.shape[1]), F32) for p in pieces],
        compiler_params=_params(("arbitrary",)),
    )(a, *pieces)


def _rowwise(name, fn, rows, vecs, outs, sums=(), tr=512):
    s = rows[0][0].shape[0]
    tr = min(tr, s)
    nr, nv, no = len(rows), len(vecs), len(outs)

    def body(*refs):
        vals = [r[...] for r in refs[:nr + nv]]
        res = fn(*vals)
        for o_ref, val in zip(refs[nr + nv:nr + nv + no], res[:no]):
            o_ref[...] = val.astype(o_ref.dtype)
        step = pl.program_id(0)
        for s_ref, val in zip(refs[nr + nv + no:], res[no:]):
            @pl.when(step == 0)
            def _(s_ref=s_ref, val=val):
                s_ref[...] = val

            @pl.when(step > 0)
            def _(s_ref=s_ref, val=val):
                s_ref[...] += val

    in_specs = [pl.BlockSpec((tr, w), functools.partial(lambda i, cb: (i, cb), cb=cb)) for _, w, cb in rows]
    in_specs += [pl.BlockSpec(v.shape, lambda i: (0, 0)) for v in vecs]
    out_specs = [pl.BlockSpec((tr, w), lambda i: (i, 0)) for w, _ in outs]
    out_specs += [pl.BlockSpec((1, w), lambda i: (0, 0)) for w in sums]
    out_shape = [jax.ShapeDtypeStruct((s, w), dt) for w, dt in outs]
    out_shape += [jax.ShapeDtypeStruct((1, w), F32) for w in sums]
    return pl.pallas_call(
        body, name=name, grid=(s // tr,), in_specs=in_specs, out_specs=out_specs, out_shape=out_shape,
        compiler_params=_params(("arbitrary",) if sums else ("parallel",)),
    )(*[r[0] for r in rows], *vecs)


def _colsum(x):
    return jnp.sum(x, axis=0, keepdims=True)


def _sigmoid(x):
    return 1.0 / (1.0 + jnp.exp(-x))


def _log_sigmoid(x):
    return jnp.minimum(x, 0.0) - jnp.log(1.0 + jnp.exp(-jnp.abs(x)))


def _ln_stats(r):
    mu = jnp.mean(r, axis=-1, keepdims=True)
    xc = r - mu
    var = jnp.mean(xc * xc, axis=-1, keepdims=True)
    rstd = lax.rsqrt(var + LN_EPS)
    return xc * rstd, rstd


def _ln_bwd(dy, xhat, rstd, g):
    dxh = dy * g
    m1 = jnp.mean(dxh, axis=-1, keepdims=True)
    m2 = jnp.mean(dxh * xhat, axis=-1, keepdims=True)
    return rstd * (dxh - m1 - xhat * m2)


def _mod_exchange(c_row, w_ada, b_ada_loc):
    def body(c_ref, w_ref, b_ref, call_ref, mod_ref, piece_ref, send_sems, recv_sems):
        me = _me()[3]
        call_ref[me] = c_ref[...]
        sent = []
        for r in range(1, N_DEV):
            peer, _ = _peer(r)
            cp = pltpu.make_async_remote_copy(
                src_ref=c_ref, dst_ref=call_ref.at[me], send_sem=send_sems.at[0, r - 1],
                recv_sem=recv_sems.at[0, r - 1], device_id=peer, device_id_type=MESH)
            cp.start()
            sent.append(cp)
        for r in range(1, N_DEV):
            peer, pidx = _peer(r)
            pltpu.make_async_remote_copy(
                src_ref=c_ref, dst_ref=call_ref.at[pidx], send_sem=send_sems.at[0, r - 1],
                recv_sem=recv_sems.at[0, r - 1], device_id=peer, device_id_type=MESH).wait_recv()
        c_all = jnp.concatenate([call_ref[d] for d in range(N_DEV)], axis=0)
        mod_loc = jnp.dot(c_all, w_ref[...], preferred_element_type=F32,
                          precision=lax.Precision.HIGHEST) + b_ref[...]
        for d in range(N_DEV):
            piece_ref[d] = mod_loc[d:d + 1, :]
        mod_ref[me] = piece_ref[me]
        for r in range(1, N_DEV):
            peer, pidx = _peer(r)
            cp = pltpu.make_async_remote_copy(
                src_ref=piece_ref.at[pidx], dst_ref=mod_ref.at[me], send_sem=send_sems.at[1, r - 1],
                recv_sem=recv_sems.at[1, r - 1], device_id=peer, device_id_type=MESH)
            cp.start()
            sent.append(cp)
        for r in range(1, N_DEV):
            peer, pidx = _peer(r)
            pltpu.make_async_remote_copy(
                src_ref=piece_ref.at[me], dst_ref=mod_ref.at[pidx], send_sem=send_sems.at[1, r - 1],
                recv_sem=recv_sems.at[1, r - 1], device_id=peer, device_id_type=MESH).wait_recv()
        for cp in sent:
            cp.wait_send()

    vmem = pl.BlockSpec(memory_space=pltpu.VMEM)
    return pl.pallas_call(
        body, name="mod_exchange", in_specs=[vmem, vmem, vmem], out_specs=[vmem, vmem],
        out_shape=[jax.ShapeDtypeStruct((N_DEV, 1, D_MODEL), F32), jax.ShapeDtypeStruct((N_DEV, 1, ADA_SHARD), F32)],
        scratch_shapes=[pltpu.VMEM((N_DEV, 1, ADA_SHARD), F32),
                        pltpu.SemaphoreType.DMA((2, N_DEV - 1)), pltpu.SemaphoreType.DMA((2, N_DEV - 1))],
        compiler_params=_params(),
    )(c_row, w_ada, b_ada_loc)


def _split3(x):
    hi = x.astype(BF16)
    r1 = x - hi.astype(F32)
    mid = r1.astype(BF16)
    lo = (r1 - mid.astype(F32)).astype(BF16)
    return hi, mid, lo


def _scan_rows(x_ref, o_ref, s, reverse, pre=None, post=None):
    tb = min(TQ, s)
    nb = s // tb
    row = lax.broadcasted_iota(jnp.int32, (tb, tb), 0)
    col = lax.broadcasted_iota(jnp.int32, (tb, tb), 1)
    tri = jnp.where((col >= row) if reverse else (col <= row), 1.0, 0.0).astype(BF16)

    def step(i, carry):
        blk = (nb - 1 - i) if reverse else i
        off = pl.multiple_of(blk * tb, tb)
        x = x_ref[pl.ds(off, tb), :]
        if pre is not None:
            x = pre(x, off)
        acc = carry
        for piece in _split3(x):
            acc = acc + jnp.dot(tri, piece, preferred_element_type=F32)
        o_ref[pl.ds(off, tb), :] = acc if post is None else post(acc, off)
        edge = acc[0:1, :] if reverse else acc[tb - 1:tb, :]
        return jnp.broadcast_to(edge, (tb, LANES))

    lax.fori_loop(0, nb, step, jnp.zeros((tb, LANES), F32))


def _forget_cumsum(f_raw, b_pad):
    s = f_raw.shape[0]

    def body(f_ref, b_ref, cum_ref):
        b = b_ref[...]
        _scan_rows(f_ref, cum_ref, s, False, pre=lambda x, off: _log_sigmoid(x + b))

    vmem = pl.BlockSpec(memory_space=pltpu.VMEM)
    return pl.pallas_call(body, name="forget_cumsum", in_specs=[vmem, vmem], out_specs=vmem,
                          out_shape=jax.ShapeDtypeStruct((s, LANES), F32), compiler_params=_params())(f_raw, b_pad)


def _forget_bwd(dcum, f_raw, b_pad):
    s = f_raw.shape[0]

    def body(d_ref, f_ref, b_ref, df_ref, db_ref, tmp_ref):
        b = b_ref[...]
        _scan_rows(d_ref, tmp_ref, s, True)
        df = tmp_ref[...] * _sigmoid(-(f_ref[...] + b))
        df_ref[...] = df.astype(BF16)
        db_ref[...] = _colsum(df)

    vmem = pl.BlockSpec(memory_space=pltpu.VMEM)
    return pl.pallas_call(
        body, name="forget_bwd", in_specs=[vmem, vmem, vmem], out_specs=[vmem, vmem],
        out_shape=[jax.ShapeDtypeStruct((s, LANES), BF16), jax.ShapeDtypeStruct((1, LANES), F32)],
        scratch_shapes=[pltpu.VMEM((s, LANES), F32)], compiler_params=_params())(dcum, f_raw, b_pad)


def _dot_nt(a, b):
    return lax.dot_general(a, b, (((1,), (1,)), ((), ())), preferred_element_type=F32)


def _head_masks():
    lane = lax.broadcasted_iota(jnp.int32, (TQ, LANES), 1)
    return lane, [lane < HEAD_DIM, lane >= HEAD_DIM]


def _pick(mask, x):
    return jnp.where(mask, x, jnp.zeros_like(x))


def _qkv_specs(s, col0):
    nb = ATTN_W // LANES
    return [pl.BlockSpec((TQ, LANES), lambda hp, qi: (qi, col0 + hp)),
            pl.BlockSpec((s, LANES), lambda hp, qi: (0, col0 + nb + hp)),
            pl.BlockSpec((s, LANES), lambda hp, qi: (0, col0 + 2 * nb + hp))]


def _pair_spec():
    return pl.BlockSpec((TQ, LANES), lambda hp, qi: (qi, hp))


def _diag_mask(width, strict):
    row = lax.broadcasted_iota(jnp.int32, (TQ, width * TQ), 0) + (width - 1) * TQ
    col = lax.broadcasted_iota(jnp.int32, (TQ, width * TQ), 1)
    return (col < row) if strict else (col <= row)


def _fox_fwd(qkv, cum_col, cum_row, ride=None):
    s = qkv.shape[0]
    nq = s // TQ

    def body(q_ref, k_ref, v_ref, cc_ref, cr_ref, o_ref, o32_ref, lse_ref):
        hp, qi = pl.program_id(0), pl.program_id(1)
        lane, masks = _head_masks()
        q2 = q_ref[...] * jnp.asarray(ATTN_SCALE, BF16)
        cc = cc_ref[...]
        qms = [_pick(masks[e], q2) for e in range(2)]
        cqs = [jnp.sum(jnp.where(lane == 2 * hp + e, cc, 0.0), axis=1, keepdims=True) for e in range(2)]

        def tile(kb, carry, masked, width=1):
            off, span = pl.multiple_of(kb * TQ, TQ), width * TQ
            k2, v2 = k_ref[pl.ds(off, span), :], v_ref[pl.ds(off, span), :]
            head0 = lax.broadcasted_iota(jnp.int32, (span, LANES), 1) < HEAD_DIM
            new = []
            for e in range(2):
                m, acc = carry[e]
                sc = _dot_nt(qms[e], k2) + (cqs[e] - cr_ref[e:e + 1, pl.ds(off, span)])
                if masked:
                    sc = jnp.where(_diag_mask(width, False), sc, -jnp.inf)
                m_new = jnp.maximum(m, jnp.max(sc, axis=1, keepdims=True))
                p = jnp.exp(sc - m_new)
                v_ones = jnp.where(head0 if e == 0 else ~head0, v2, jnp.ones_like(v2))
                acc = jnp.exp(m - m_new) * acc + jnp.dot(p.astype(BF16), v_ones, preferred_element_type=F32)
                new.append((m_new, acc))
            return tuple(new)

        init = (jnp.full((TQ, 1), -jnp.inf, F32), jnp.zeros((TQ, LANES), F32))
        carry = lax.cond(qi % 2 == 1, lambda cr: tile(qi - 1, cr, True, 2), lambda cr: tile(qi, cr, True), (init, init))
        quads = qi // 4
        carry = lax.fori_loop(0, quads, lambda j, cr: tile(4 * j, cr, False, 4), carry)
        carry = lax.cond(qi % 4 >= 2, lambda cr: tile(4 * quads, cr, False, 2), lambda cr: cr, carry)
        sums = [jnp.max(jnp.where(masks[1 - e], carry[e][1], 0.0), axis=1, keepdims=True) for e in range(2)]
        outs = [carry[e][1] / sums[e] for e in range(2)]
        lses = [carry[e][0] + jnp.log(sums[e]) for e in range(2)]
        out = jnp.where(masks[0], outs[0], outs[1])
        o_ref[...] = out.astype(BF16)
        o32_ref[...] = out
        lse_ref[...] = jnp.where(masks[0], lses[0], lses[1])

    return _ride_call(
        body, ride, name="fox_fwd", grid=(N_HEADS // 2, nq),
        in_specs=_qkv_specs(s, 0) + [pl.BlockSpec((TQ, LANES), lambda hp, qi: (qi, 0)),
                                     pl.BlockSpec((None, 2, s), lambda hp, qi: (hp, 0, 0))],
        out_specs=[_pair_spec(), _pair_spec(), _pair_spec()],
        out_shape=[jax.ShapeDtypeStruct((s, ATTN_W), BF16), jax.ShapeDtypeStruct((s, ATTN_W), F32),
                   jax.ShapeDtypeStruct((s, ATTN_W), F32)],
        scratch_shapes=[], sem=("parallel", "parallel"), args=(qkv, qkv, qkv, cum_col, cum_row))


def _write_transposed(acc_ref, out_ref):
    for c in range(out_ref.shape[0] // TQ):
        out_ref[c * TQ:(c + 1) * TQ, :] = jnp.transpose(acc_ref[:, c * TQ:(c + 1) * TQ]).astype(BF16)


def _fox_bwd(qkv, cum_col, cum_row, o, lse, do, ride=None):
    s = qkv.shape[0]
    nq = s // TQ

    def body(q_ref, k_ref, v_ref, cc_ref, cr_ref, o_ref, lse_ref, do_ref,
             dq_ref, dk_ref, dv_ref, dcr_ref, dk_acc, dv_acc):
        hp, qi = pl.program_id(0), pl.program_id(1)

        @pl.when(qi == 0)
        def _():
            dk_acc[...] = jnp.zeros_like(dk_acc)
            dv_acc[...] = jnp.zeros_like(dv_acc)
            dcr_ref[...] = jnp.zeros_like(dcr_ref)

        lane, masks = _head_masks()
        q2 = q_ref[...] * jnp.asarray(ATTN_SCALE, BF16)
        do2 = do_ref[...]
        prod = do2.astype(F32) * o_ref[...].astype(F32)
        lse2 = lse_ref[...]
        cc = cc_ref[...]
        qms = [_pick(masks[e], q2) for e in range(2)]
        doms = [_pick(masks[e], do2) for e in range(2)]
        deltas = [jnp.sum(jnp.where(masks[e], prod, 0.0), axis=1, keepdims=True) for e in range(2)]
        lses = [jnp.max(jnp.where(masks[e], lse2, -jnp.inf), axis=1, keepdims=True) for e in range(2)]
        cqs = [jnp.sum(jnp.where(lane == 2 * hp + e, cc, 0.0), axis=1, keepdims=True) for e in range(2)]
        qts = [jnp.transpose(qms[e].astype(F32)).astype(BF16) for e in range(2)]
        dots = [jnp.transpose(doms[e].astype(F32)).astype(BF16) for e in range(2)]
        ones_row = [HEAD_DIM * (1 - e) for e in range(2)]
        trow = lax.broadcasted_iota(jnp.int32, (LANES, TQ), 0)
        qts = [jnp.where(trow == ones_row[e], jnp.ones_like(qts[e]), qts[e]) for e in range(2)]

        def tile(kb, carry, masked, width=1):
            off, span = pl.multiple_of(kb * TQ, TQ), width * TQ
            k2, v2 = k_ref[pl.ds(off, span), :], v_ref[pl.ds(off, span), :]
            head0 = lax.broadcasted_iota(jnp.int32, (span, LANES), 1) < HEAD_DIM
            new, dks, dv = [], [], None
            for e in range(2):
                dq = carry[e]
                sc = _dot_nt(qms[e], k2) + (cqs[e] - cr_ref[e:e + 1, pl.ds(off, span)])
                p = jnp.exp(sc - lses[e])
                if masked:
                    p = jnp.where(_diag_mask(width, False), p, 0.0)
                ds = p * (_dot_nt(doms[e], v2) - deltas[e])
                dsb = ds.astype(BF16)
                dk_e = jnp.dot(qts[e], dsb, preferred_element_type=F32)
                dv_e = jnp.dot(dots[e], p.astype(BF16), preferred_element_type=F32)
                dks.append(dk_e)
                dv = dv_e if e == 0 else dv + dv_e
                dcr_ref[e:e + 1, pl.ds(off, span)] -= dk_e[ones_row[e]:ones_row[e] + 1, :]
                k_ones = jnp.where(head0 if e == 0 else ~head0, k2, jnp.ones_like(k2))
                new.append(dq + jnp.dot(dsb, k_ones, preferred_element_type=F32))
            krow = lax.broadcasted_iota(jnp.int32, (LANES, span), 0)
            dk_acc[:, pl.ds(off, span)] += jnp.where(krow < HEAD_DIM, dks[0], dks[1])
            dv_acc[:, pl.ds(off, span)] += dv
            return tuple(new)

        init = jnp.zeros((TQ, LANES), F32)
        carry = lax.fori_loop(0, qi // 2, lambda j, cr: tile(2 * j, cr, False, 2), (init, init))
        carry = lax.cond(qi % 2 == 1, lambda cr: tile(qi - 1, cr, True, 2), lambda cr: tile(qi, cr, True), carry)
        dq_ref[...] = (jnp.where(masks[0], carry[0], carry[1]) * ATTN_SCALE).astype(BF16)
        for e in range(2):
            dcr_ref[e:e + 1, pl.ds(pl.multiple_of(qi * TQ, TQ), TQ)] += jnp.transpose(carry[e])[
                ones_row[e]:ones_row[e] + 1, :]

        @pl.when(qi == nq - 1)
        def _():
            _write_transposed(dk_acc, dk_ref)
            _write_transposed(dv_acc, dv_ref)

    seq_spec = pl.BlockSpec((s, LANES), lambda hp, qi: (0, hp))
    return _ride_call(
        body, ride, name="fox_bwd", grid=(N_HEADS // 2, nq),
        in_specs=_qkv_specs(s, 0) + [pl.BlockSpec((TQ, LANES), lambda hp, qi: (qi, 0)),
                                     pl.BlockSpec((None, 2, s), lambda hp, qi: (hp, 0, 0)),
                                     _pair_spec(), _pair_spec(), _pair_spec()],
        out_specs=[_pair_spec(), seq_spec, seq_spec, pl.BlockSpec((None, 2, s), lambda hp, qi: (hp, 0, 0))],
        out_shape=[jax.ShapeDtypeStruct((s, ATTN_W), BF16)] * 3 + [jax.ShapeDtypeStruct((N_HEADS // 2, 2, s), F32)],
        scratch_shapes=[pltpu.VMEM((LANES, s), F32), pltpu.VMEM((LANES, s), F32)],
        sem=("parallel", "arbitrary"), args=(qkv, qkv, qkv, cum_col, cum_row, o, lse, do))


def _scan_matrix(reverse):
    row = lax.broadcasted_iota(jnp.int32, (SCAN_W, SCAN_W), 0)
    col = lax.broadcasted_iota(jnp.int32, (SCAN_W, SCAN_W), 1)
    return jnp.where((row > col) if reverse else (row < col), 1.0, 0.0).astype(BF16)


def _scan_cols(x, tri, reverse, init):
    nblk = x.shape[1] // SCAN_W
    parts, total = [None] * nblk, init
    far = 0 if reverse else SCAN_W - 1
    for b in (reversed(range(nblk)) if reverse else range(nblk)):
        blk = x[:, b * SCAN_W:(b + 1) * SCAN_W]
        part = jnp.dot(blk.astype(BF16), tri, preferred_element_type=F32)
        parts[b] = part + total
        total = total + (part[:, far:far + 1] + blk[:, far:far + 1])
    return (parts[0] if nblk == 1 else jnp.concatenate(parts, axis=1)), total


def _sb_logits(qm, k2):
    z = _dot_nt(qm, k2)
    neg_abs = lax.bitcast_convert_type(lax.bitcast_convert_type(z, jnp.uint32) | jnp.uint32(0x80000000), F32)
    soft = jnp.log(1.0 + jnp.exp(neg_abs))
    lb = jnp.minimum(z, 0.0) - soft
    return lb, lb - z


TILE_SLOTS = 4


def _tri_base(qi):
    return (qi * (qi + 1)) // 2


def _sb_fwd(qkv):
    s = qkv.shape[0]
    nq = s // TQ

    def body(q_ref, k_ref, v_ref, o_ref, t_ref, buf, sems):
        hp, qi = pl.program_id(0), pl.program_id(1)
        _, masks = _head_masks()
        suffix = _scan_matrix(True)
        q2 = q_ref[...] * jnp.asarray(ATTN_SCALE, BF16)
        qms = [_pick(masks[e], q2) for e in range(2)]
        base = _tri_base(qi)

        def store(e, kb):
            slot = kb % TILE_SLOTS
            return pltpu.make_async_copy(buf.at[e, slot], t_ref.at[2 * hp + e, base + kb], sems.at[e, slot])

        def tile(kb, carry, masked, width=1):
            off, span = pl.multiple_of(kb * TQ, TQ), width * TQ
            k2, v2 = k_ref[pl.ds(off, span), :], v_ref[pl.ds(off, span), :]
            new = []
            for e in range(2):
                run, acc = carry[e]
                lb, lo = _sb_logits(qms[e], k2)
                if masked:
                    strict = _diag_mask(width, True)
                    lo = jnp.where(strict, lo, 0.0)
                rest, run = _scan_cols(lo, suffix, True, run)
                a = jnp.exp(lb + rest)
                if masked:
                    a = jnp.where(strict, a, 0.0)
                ab, lbb = a.astype(BF16), lb.astype(BF16)
                acc = acc + jnp.dot(ab, v2, preferred_element_type=F32)
                for w in range(width):
                    blk = kb + w

                    @pl.when(blk + TILE_SLOTS <= qi)
                    def _(e=e, blk=blk):
                        store(e, blk + TILE_SLOTS).wait()
                    buf[e, blk % TILE_SLOTS, 0] = ab[:, w * TQ:(w + 1) * TQ]
                    buf[e, blk % TILE_SLOTS, 1] = lbb[:, w * TQ:(w + 1) * TQ]
                    store(e, blk).start()
                new.append((run, acc))
            return tuple(new)

        init = (jnp.zeros((TQ, 1), F32), jnp.zeros((TQ, LANES), F32))
        carry = lax.cond(qi % 2 == 1, lambda cr: tile(qi - 1, cr, True, 2), lambda cr: tile(qi, cr, True), (init, init))
        pairs = qi // 2
        carry = lax.fori_loop(0, pairs, lambda it, cr: tile(2 * (pairs - 1 - it), cr, False, 2), carry)
        for e in range(2):
            for blk in range(TILE_SLOTS):
                @pl.when(qi >= blk)
                def _(e=e, blk=blk):
                    store(e, blk).wait()
        o_ref[...] = jnp.where(masks[0], carry[0][1], carry[1][1]).astype(BF16)

    ntri = nq * (nq + 1) // 2
    return pl.pallas_call(
        body, name="sb_fwd", grid=(N_HEADS // 2, nq), in_specs=_qkv_specs(s, 3 * ATTN_W // LANES),
        out_specs=[_pair_spec(), ANY],
        out_shape=[jax.ShapeDtypeStruct((s, ATTN_W), BF16), jax.ShapeDtypeStruct((N_HEADS, ntri, 2, TQ, TQ), BF16)],
        scratch_shapes=[pltpu.VMEM((2, TILE_SLOTS, 2, TQ, TQ), BF16), pltpu.SemaphoreType.DMA((2, TILE_SLOTS))],
        compiler_params=_params(("arbitrary", "arbitrary")),
    )(qkv, qkv, qkv)


def _sb_bwd(qkv, tiles, do):
    s = qkv.shape[0]
    nq = s // TQ

    def body(q_ref, k_ref, v_ref, t_ref, do_ref, dq_ref, dk_ref, dv_ref, dk_acc, dv_acc, buf, sems):
        hp, qi = pl.program_id(0), pl.program_id(1)

        @pl.when(qi == 0)
        def _():
            dk_acc[...] = jnp.zeros_like(dk_acc)
            dv_acc[...] = jnp.zeros_like(dv_acc)

        _, masks = _head_masks()
        prefix = _scan_matrix(False)
        q2 = q_ref[...] * jnp.asarray(ATTN_SCALE, BF16)
        do2 = do_ref[...]
        qms = [_pick(masks[e], q2) for e in range(2)]
        doms = [_pick(masks[e], do2) for e in range(2)]
        qts = [jnp.transpose(qms[e].astype(F32)).astype(BF16) for e in range(2)]
        dots = [jnp.transpose(doms[e].astype(F32)).astype(BF16) for e in range(2)]
        base = _tri_base(qi)

        def fetch(e, kb):
            slot = kb % TILE_SLOTS
            return pltpu.make_async_copy(t_ref.at[2 * hp + e, base + kb], buf.at[e, slot], sems.at[e, slot])

        for e in range(2):
            fetch(e, 0).start()

            @pl.when(qi >= 1)
            def _(e=e):
                fetch(e, 1).start()

        def tile(kb, carry, masked, width=1):
            off, span = pl.multiple_of(kb * TQ, TQ), width * TQ
            k2, v2 = k_ref[pl.ds(off, span), :], v_ref[pl.ds(off, span), :]
            new, dk, dv = [], None, None
            for e in range(2):
                gsum, dq = carry[e]
                if not masked:
                    for blk in range(2, 2 + width):
                        @pl.when(kb + blk <= qi)
                        def _(e=e, blk=blk):
                            fetch(e, kb + blk).start()
                for w in range(width):
                    fetch(e, kb + w).wait()
                slots = [(kb + w) % TILE_SLOTS for w in range(width)]
                ab = buf[e, slots[0], 0] if width == 1 else jnp.concatenate([buf[e, sl, 0] for sl in slots], axis=1)
                lbb = buf[e, slots[0], 1] if width == 1 else jnp.concatenate([buf[e, sl, 1] for sl in slots], axis=1)
                beta = jnp.exp(lbb.astype(F32))
                g = ab.astype(F32) * _dot_nt(doms[e], v2)
                before, gsum = _scan_cols(g, prefix, False, gsum)
                dz = g - beta * (g + before)
                if masked:
                    dz = jnp.where(_diag_mask(width, True), dz, 0.0)
                dzb = dz.astype(BF16)
                dk_e = jnp.dot(qts[e], dzb, preferred_element_type=F32)
                dv_e = jnp.dot(dots[e], ab, preferred_element_type=F32)
                dk, dv = (dk_e, dv_e) if e == 0 else (dk + dk_e, dv + dv_e)
                new.append((gsum, dq + jnp.dot(dzb, k2, preferred_element_type=F32)))
            dk_acc[:, pl.ds(off, span)] += dk
            dv_acc[:, pl.ds(off, span)] += dv
            return tuple(new)

        init = (jnp.zeros((TQ, 1), F32), jnp.zeros((TQ, LANES), F32))
        carry = lax.fori_loop(0, qi // 2, lambda j, cr: tile(2 * j, cr, False, 2), (init, init))
        carry = lax.cond(qi % 2 == 1, lambda cr: tile(qi - 1, cr, True, 2), lambda cr: tile(qi, cr, True), carry)
        dq_ref[...] = (jnp.where(masks[0], carry[0][1], carry[1][1]) * ATTN_SCALE).astype(BF16)

        @pl.when(qi == nq - 1)
        def _():
            _write_transposed(dk_acc, dk_ref)
            _write_transposed(dv_acc, dv_ref)

    seq_spec = pl.BlockSpec((s, LANES), lambda hp, qi: (0, hp))
    return pl.pallas_call(
        body, name="sb_bwd", grid=(N_HEADS // 2, nq),
        in_specs=_qkv_specs(s, 3 * ATTN_W // LANES) + [ANY, _pair_spec()],
        out_specs=[_pair_spec(), seq_spec, seq_spec],
        out_shape=[jax.ShapeDtypeStruct((s, ATTN_W), BF16)] * 3,
        scratch_shapes=[pltpu.VMEM((LANES, s), F32), pltpu.VMEM((LANES, s), F32),
                        pltpu.VMEM((2, TILE_SLOTS, 2, TQ, TQ), BF16), pltpu.SemaphoreType.DMA((2, TILE_SLOTS))],
        compiler_params=_params(("arbitrary", "arbitrary")),
    )(qkv, qkv, qkv, tiles, do)


CONV_TR = 512


def _shift_down(x, halo, n):
    rolled = pltpu.roll(x, n, 0)
    head = rolled[0:8, :]
    rid = lax.broadcasted_iota(jnp.int32, head.shape, 0)
    for j in range(n):
        head = jnp.where(rid == j, halo[8 - n + j:8 - n + j + 1, :], head)
    return jnp.concatenate([head, rolled[8:, :]], axis=0)


def _shift_up(x, halo, n):
    rows = x.shape[0]
    rolled = pltpu.roll(x, rows - n, 0)
    tail = rolled[rows - 8:, :]
    rid = lax.broadcasted_iota(jnp.int32, tail.shape, 0)
    for j in range(n):
        tail = jnp.where(rid == 8 - n + j, halo[j:j + 1, :], tail)
    return jnp.concatenate([rolled[:rows - 8, :], tail], axis=0)


def _conv_fwd_block(x, halo, w, b):
    return b + _shift_down(x, halo, 2) * w[0:1, :] + _shift_down(x, halo, 1) * w[1:2, :] + x * w[2:3, :]


def _conv_specs(tr, s):
    pair = 2 * FF_HALF
    blk = pl.BlockSpec((tr, pair), lambda j, i: (i, j))
    prev = pl.BlockSpec((8, pair), lambda j, i: (jnp.maximum(i * (tr // 8) - 1, 0), j))
    nxt = pl.BlockSpec((8, pair), lambda j, i: (jnp.minimum((i + 1) * (tr // 8), s // 8 - 1), j))
    return blk, prev, nxt


def _conv_gate_fwd(hpre, conv_w, conv_b):
    s = hpre.shape[0]
    tr = min(CONV_TR, s)
    blk, prev, _ = _conv_specs(tr, s)

    def body(x_ref, halo_ref, w_ref, b_ref, a_ref):
        i = pl.program_id(1)
        halo = jnp.where(i > 0, halo_ref[...], 0.0)
        h = _conv_fwd_block(x_ref[...], halo, w_ref[...], b_ref[...])
        hg, hv = h[:, :FF_HALF], h[:, FF_HALF:]
        a_ref[...] = (hg * _sigmoid(hg) * hv).astype(BF16)

    return pl.pallas_call(
        body, name="conv_gate_fwd", grid=(2, s // tr),
        in_specs=[blk, prev, pl.BlockSpec((3, 2 * FF_HALF), lambda j, i: (0, j)),
                  pl.BlockSpec((1, 2 * FF_HALF), lambda j, i: (0, j))],
        out_specs=pl.BlockSpec((tr, FF_HALF), lambda j, i: (i, j)),
        out_shape=jax.ShapeDtypeStruct((s, D_FF), BF16),
        compiler_params=_params(("parallel", "parallel")),
    )(hpre, hpre, conv_w, conv_b)


def _conv_gate_bwd(hpre, da, conv_w, conv_b):
    s = hpre.shape[0]
    tr = min(CONV_TR, s)
    blk, prev, _ = _conv_specs(tr, s)

    def body(x_ref, halo_ref, da_ref, w_ref, b_ref, dh_ref, db_ref, dw_ref):
        i = pl.program_id(1)
        halo = jnp.where(i > 0, halo_ref[...], 0.0)
        x = x_ref[...]
        h = _conv_fwd_block(x, halo, w_ref[...], b_ref[...])
        hg, hv = h[:, :FF_HALF], h[:, FF_HALF:]
        da_blk = da_ref[...].astype(F32)
        sg = _sigmoid(hg)
        dhg = da_blk * hv * (sg * (1.0 + hg * (1.0 - sg)))
        dhv = da_blk * (hg * sg)
        dh_ref[:, :FF_HALF] = dhg.astype(BF16)
        dh_ref[:, FF_HALF:] = dhv.astype(BF16)
        x2, x1 = _shift_down(x, halo, 2), _shift_down(x, halo, 1)
        parts = []
        for lo, dpart in ((0, dhg), (FF_HALF, dhv)):
            cols = slice(lo, lo + FF_HALF)
            parts.append((cols, _colsum(dpart), _colsum(dpart * x2[:, cols]), _colsum(dpart * x1[:, cols]),
                          _colsum(dpart * x[:, cols])))

        @pl.when(i == 0)
        def _():
            for cols, db, dw0, dw1, dw2 in parts:
                db_ref[:, cols] = db
                dw_ref[0:1, cols] = dw0
                dw_ref[1:2, cols] = dw1
                dw_ref[2:3, cols] = dw2

        @pl.when(i > 0)
        def _():
            for cols, db, dw0, dw1, dw2 in parts:
                db_ref[:, cols] += db
                dw_ref[0:1, cols] += dw0
                dw_ref[1:2, cols] += dw1
                dw_ref[2:3, cols] += dw2

    pair = 2 * FF_HALF
    return pl.pallas_call(
        body, name="conv_gate_bwd", grid=(2, s // tr),
        in_specs=[blk, prev, pl.BlockSpec((tr, FF_HALF), lambda j, i: (i, j)),
                  pl.BlockSpec((3, pair), lambda j, i: (0, j)), pl.BlockSpec((1, pair), lambda j, i: (0, j))],
        out_specs=[blk, pl.BlockSpec((1, pair), lambda j, i: (0, j)), pl.BlockSpec((3, pair), lambda j, i: (0, j))],
        out_shape=[jax.ShapeDtypeStruct((s, 2 * D_FF), BF16), jax.ShapeDtypeStruct((1, 2 * D_FF), F32),
                   jax.ShapeDtypeStruct((3, 2 * D_FF), F32)],
        compiler_params=_params(("parallel", "arbitrary")),
    )(hpre, hpre, da, conv_w, conv_b)


def _conv_input_bwd(dh, conv_w):
    s = dh.shape[0]
    tr = min(CONV_TR, s)
    blk, _, _ = _conv_specs(tr, s)
    nblk = s // tr

    def body(x_ref, halo_ref, w_ref, o_ref):
        i = pl.program_id(1)
        halo = jnp.where(i < nblk - 1, halo_ref[...].astype(F32), 0.0)
        x, w = x_ref[...].astype(F32), w_ref[...]
        o_ref[...] = (x * w[2:3, :] + _shift_up(x, halo, 1) * w[1:2, :] + _shift_up(x, halo, 2) * w[0:1, :]).astype(BF16)

    nxt = pl.BlockSpec((16, 2 * FF_HALF), lambda j, i: (jnp.minimum((i + 1) * (tr // 16), s // 16 - 1), j))
    return pl.pallas_call(
        body, name="conv_input_bwd", grid=(2, nblk),
        in_specs=[blk, nxt, pl.BlockSpec((3, 2 * FF_HALF), lambda j, i: (0, j))], out_specs=blk,
        out_shape=jax.ShapeDtypeStruct((s, 2 * D_FF), BF16),
        compiler_params=_params(("parallel", "parallel")),
    )(dh, dh, conv_w)


def _adamw_math(w, g, m, v):
    m = ADAM_B1 * m + (1.0 - ADAM_B1) * g
    v = ADAM_B2 * v + (1.0 - ADAM_B2) * (g * g)
    m_hat = m / (1.0 - ADAM_B1 ** ADAM_STEP)
    v_hat = v / (1.0 - ADAM_B2 ** ADAM_STEP)
    delta = -ADAM_LR * (m_hat / (jnp.sqrt(v_hat) + ADAM_EPS) + ADAM_WD * w)
    return delta, m, v


def _adamw(name, g8, w, m, v):
    r, c = w.shape
    tr = _row_tile(r, c)

    def body(g_ref, w_ref, m_ref, v_ref, go_ref, d_ref, mo_ref, vo_ref):
        g = g_ref[0].astype(F32)
        for d in range(1, N_DEV):
            g = g + g_ref[d].astype(F32)
        delta, mn, vn = _adamw_math(w_ref[...], g, m_ref[...], v_ref[...])
        go_ref[...] = g
        d_ref[...] = delta
        mo_ref[...] = mn
        vo_ref[...] = vn

    spec = pl.BlockSpec((tr, c), lambda i: (i, 0))
    return pl.pallas_call(
        body, name=name, grid=(r // tr,),
        in_specs=[pl.BlockSpec((N_DEV, tr, c), lambda i: (0, i, 0)), spec, spec, spec], out_specs=[spec] * 4,
        out_shape=[jax.ShapeDtypeStruct((r, c), F32)] * 4, compiler_params=_params(("parallel",)),
    )(g8, w, m, v)


def _adamw_small(g8, ws, ms, vs):
    n = len(ws)
    offsets = [sum(w.shape[0] for w in ws[:i]) for i in range(n)]

    def body(*refs):
        g_ref, w_refs, m_refs, v_refs = refs[0], refs[1:1 + n], refs[1 + n:1 + 2 * n], refs[1 + 2 * n:1 + 3 * n]
        outs = refs[1 + 3 * n:]
        for i in range(n):
            rows = w_refs[i].shape[0]
            g = g_ref[0, offsets[i]:offsets[i] + rows, :]
            for d in range(1, N_DEV):
                g = g + g_ref[d, offsets[i]:offsets[i] + rows, :]
            delta, mn, vn = _adamw_math(w_refs[i][...], g, m_refs[i][...], v_refs[i][...])
            for k, val in enumerate((g, delta, mn, vn)):
                outs[k * n + i][...] = val

    vmem = pl.BlockSpec(memory_space=pltpu.VMEM)
    res = pl.pallas_call(
        body, name="adamw_small", in_specs=[vmem] * (1 + 3 * n), out_specs=[vmem] * (4 * n),
        out_shape=[jax.ShapeDtypeStruct(w.shape, F32) for _ in range(4) for w in ws], compiler_params=_params(),
    )(g8, *ws, *ms, *vs)
    return [res[k * n:(k + 1) * n] for k in range(4)]


def _adamw_ada(c_t, dmod, w, m, v):
    r, c = w.shape
    tr = _row_tile(r, c)

    def body(ct_ref, dm_ref, w_ref, m_ref, v_ref, go_ref, d_ref, mo_ref, vo_ref):
        ct, dm = ct_ref[...], dm_ref[...]
        g = ct[:, 0:1] * dm[0:1, :]
        for b in range(1, N_DEV):
            g = g + ct[:, b:b + 1] * dm[b:b + 1, :]
        delta, mn, vn = _adamw_math(w_ref[...], g, m_ref[...], v_ref[...])
        go_ref[...] = g
        d_ref[...] = delta
        mo_ref[...] = mn
        vo_ref[...] = vn

    spec = pl.BlockSpec((tr, c), lambda i: (i, 0))
    return pl.pallas_call(
        body, name="adamw_w_ada", grid=(r // tr,),
        in_specs=[pl.BlockSpec((tr, N_DEV), lambda i: (i, 0)), pl.BlockSpec((N_DEV, c), lambda i: (0, 0)),
                  spec, spec, spec],
        out_specs=[spec] * 4, out_shape=[jax.ShapeDtypeStruct((r, c), F32)] * 4,
        compiler_params=_params(("parallel",)),
    )(c_t, dmod, w, m, v)


def _cols_from_slots(g):
    n, r, c = g.shape
    return jnp.transpose(g, (1, 0, 2)).reshape(r, n * c)


def _cols_to_slots(w):
    r, c = w.shape
    return jnp.transpose(w.reshape(r, N_DEV, c // N_DEV), (1, 0, 2))


def _pair_cols(w):
    g0, g1 = w[..., 0:FF_HALF], w[..., FF_HALF:D_FF]
    v0, v1 = w[..., D_FF:D_FF + FF_HALF], w[..., D_FF + FF_HALF:]
    return jnp.concatenate([g0, v0, g1, v1], axis=-1)


def _unpair_cols(w):
    g0, v0 = w[..., 0:FF_HALF], w[..., FF_HALF:D_FF]
    g1, v1 = w[..., D_FF:D_FF + FF_HALF], w[..., D_FF + FF_HALF:]
    return jnp.concatenate([g0, g1, v0, v1], axis=-1)


def _row(v):
    return v.reshape(1, -1)


def kernel(x, c, w_ada, b_ada, w_in, b_forget, w_fox_proj, w_sb_proj, w_o, ln1_g, ln1_b, w_up, conv_w, conv_b, w_down, ln2_g, ln2_b, loss_target, m_w_ada, m_b_ada, m_w_in, m_b_forget, m_w_fox_proj, m_w_sb_proj, m_w_o, m_ln1_g, m_ln1_b, m_w_up, m_conv_w, m_conv_b, m_w_down, m_ln2_g, m_ln2_b, v_w_ada, v_b_ada, v_w_in, v_b_forget, v_w_fox_proj, v_w_sb_proj, v_w_o, v_ln1_g, v_ln1_b, v_w_up, v_conv_w, v_conv_b, v_w_down, v_ln2_g, v_ln2_b):
    s = x.shape[1]
    me = 4 * lax.axis_index("x") + 2 * lax.axis_index("y") + lax.axis_index("c")
    x2 = x.reshape(s, D_MODEL)
    tgt = loss_target.reshape(s, D_MODEL)

    b_ada_loc = lax.dynamic_slice(b_ada, (me * ADA_SHARD,), (ADA_SHARD,)).reshape(1, ADA_SHARD)
    c_all, mod = _mod_exchange(c, w_ada, b_ada_loc)
    mod = mod.reshape(N_MOD, 1, D_MODEL)
    sh1, sc1, gt1, sh2, sc2, gt2 = [mod[i] for i in range(N_MOD)]

    g_in = _allgather_two_level("ag_w_in", w_in.astype(BF16))
    late_weights = _Ride([w_fox_proj.astype(BF16), w_sb_proj.astype(BF16), w_o.astype(BF16), w_up.astype(BF16),
                          w_down.astype(BF16), conv_w], scatter=False)
    w_in_f = _cols_from_slots(g_in)
    w_proj = jnp.concatenate(
        [w_in_f[:, 0:1536], w_in_f[:, 1544:3080], w_in_f[:, 3080:5128], w_in_f[:, 1536:1544],
         jnp.zeros((D_MODEL, W_PROJ - 5128), BF16)], axis=1)
    w_qkv, w_gates, w_f = w_proj[:, :W_QKV], w_proj[:, W_QKV:W_QKV + W_GATES], w_proj[:, W_QKV + W_GATES:W_QKV + W_GATES + W_F]
    conv_b_p = _pair_cols(_row(conv_b))
    b_f_pad = jnp.pad(_row(b_forget), ((0, 0), (0, LANES - N_HEADS)))

    (u1, qkv), _ = _mm_pieces_nt(
        [], w_qkv, name="mm_qkv", prologue=(lambda xb, sc, sh: xb * (1.0 + sc) + sh, [x2], [sc1, sh1]),
        epilogue=lambda acc: (acc,), rows=[], vecs=[], out_dtypes=(BF16,), w_kn=True)
    gates = _mm(u1, jnp.concatenate([w_gates, w_f], axis=1), name="mm_gates_forget", tn=W_GATES + W_F)
    f_raw = gates[:, W_GATES:]
    cum_col = _forget_cumsum(f_raw, b_f_pad)
    cum_row = jnp.transpose(cum_col[:, :N_HEADS]).reshape(N_HEADS // 2, 2, s)
    (y_fox, y_fox32, lse), (g_fox, g_sb, g_o, g_up, g_down, g_cw) = _fox_fwd(qkv, cum_col, cum_row, ride=late_weights)
    w_fox_f = _cols_from_slots(g_fox)
    w_sb_f = _cols_from_slots(g_sb)
    w_o_f = g_o.reshape(D_MODEL, D_MODEL)
    w_up_p = _pair_cols(_cols_from_slots(g_up))
    w_down_f = g_down.reshape(D_FF, D_MODEL)
    conv_w_p = _pair_cols(_cols_from_slots(g_cw))
    y_sb, sb_run = _sb_fwd(qkv)
    pf = _mm(y_fox, w_fox_f, name="mm_fox_proj", out_dtype=BF16)
    ps = _mm(y_sb, w_sb_f, name="mm_sb_proj", out_dtype=BF16)
    def ln1_mod(fb, xb, gt, g, b, sc, sh):
        xhat, _ = _ln_stats(ALPHA * xb + (1.0 + gt) * fb)
        y = xhat * g + b
        return fb, y, y * (1.0 + sc) + sh

    (merged, attn_out, x1, u2), _ = _mm_pieces_nt(
        [], w_o_f, name="mm_w_o_ln1",
        prologue=(lambda ga, gb, a, b: _sigmoid(ga) * a + _sigmoid(gb) * b, [(gates, 0), (gates, 1), pf, ps], []),
        epilogue=ln1_mod, rows=[x2], vecs=[gt1, _row(ln1_g), _row(ln1_b), sc2, sh2],
        out_dtypes=(F32, F32, BF16), w_kn=True)

    hpre = _mm(u2, w_up_p, name="mm_w_up", tn=1408)
    act = _conv_gate_fwd(hpre, conv_w_p, conv_b_p)

    def ln2_bwd(fb, xb, tb, gt, g, b):
        xhat, rstd = _ln_stats(ALPHA * xb + (1.0 + gt) * fb)
        err = (xhat * g + b) - tb
        dy = err * (1.0 / D_MODEL)
        dr = _ln_bwd(dy, xhat, rstd, g)
        return (dr * (1.0 + gt), ALPHA * dr,
                _colsum(err * err), _colsum(dy * xhat), _colsum(dy), _colsum(dr * fb))

    (dffn, dx1_res, sq_err, d_ln2_g, d_ln2_b, d_gt2), _ = _mm_pieces_nt(
        [act], w_down_f, name="mm_w_down_ln2_bwd", epilogue=ln2_bwd, rows=[x1, tgt],
        vecs=[gt2, _row(ln2_g), _row(ln2_b)], out_dtypes=(BF16, F32), n_sums=4, w_kn=True)
    loss_part = jnp.broadcast_to(0.5 * jnp.sum(sq_err) / D_MODEL, (1, LANES))

    d_w_down = _mm(act, dffn, name="mm_d_w_down", ta=True, tm=1408, tk=2048, out_dtype=BF16)
    d_act = _mm(dffn, w_down_f, name="mm_d_act", tb=True, tn=1408, out_dtype=BF16)
    dh, d_conv_b_p, d_conv_w_p = _conv_gate_bwd(hpre, d_act, conv_w_p, conv_b_p)
    dhpre = _conv_input_bwd(dh, conv_w_p)
    d_w_up_p = _mm(u2, dhpre, name="mm_d_w_up", ta=True, tn=1408, tk=2048, out_dtype=BF16)
    def ln1_bwd(du, dres, x1b, xb, fb, sc, gt, g):
        dx1 = dres + du * (1.0 + sc)
        xhat, rstd = _ln_stats(ALPHA * xb + (1.0 + gt) * fb)
        dr = _ln_bwd(dx1, xhat, rstd, g)
        return (dr * (1.0 + gt), ALPHA * dr,
                _colsum(du * x1b), _colsum(du), _colsum(dx1 * xhat), _colsum(dx1), _colsum(dr * fb))

    (d_attn, dx_res, d_sc2, d_sh2, d_ln1_g, d_ln1_b, d_gt1), _ = _mm_pieces_nt(
        [dhpre], w_up_p, name="mm_d_u2_ln1_bwd", epilogue=ln1_bwd, rows=[dx1_res, x1, x2, attn_out],
        vecs=[sc2, gt1, _row(ln1_g)], out_dtypes=(BF16, F32), n_sums=5, tm=256)

    d_w_o = _mm(merged, d_attn, name="mm_d_w_o", ta=True, out_dtype=BF16)

    def merge_bwd(dm, ga, gb, a, b):
        a, b = a.astype(F32), b.astype(F32)
        sa, sb = _sigmoid(ga), _sigmoid(gb)
        return dm * a * sa * (1.0 - sa), dm * b * sb * (1.0 - sb), dm * sa, dm * sb

    (d_ga, d_gb, d_pf, d_ps), _ = _mm_pieces_nt(
        [d_attn], w_o_f, name="mm_d_merged_gate_bwd", epilogue=merge_bwd, rows=[(gates, 0), (gates, 1), pf, ps],
        vecs=[], out_dtypes=(BF16,) * 4)
    d_w_fox = _mm(y_fox, d_pf, name="mm_d_w_fox", ta=True, out_dtype=BF16)
    d_w_sb = _mm(y_sb, d_ps, name="mm_d_w_sb", ta=True, out_dtype=BF16)
    d_y_fox = _mm(d_pf, w_fox_f, name="mm_d_y_fox", tb=True, out_dtype=BF16)
    d_y_sb = _mm(d_ps, w_sb_f, name="mm_d_y_sb", tb=True, out_dtype=BF16)
    early_grads = _Ride(
        [_cols_to_slots(d_w_fox), _cols_to_slots(d_w_sb), d_w_o.reshape(N_DEV, D_MODEL // N_DEV, D_MODEL),
         _cols_to_slots(_unpair_cols(d_w_up_p)), d_w_down.reshape(N_DEV, D_FF // N_DEV, D_MODEL)], scatter=True)
    (dq_a, dk_a, dv_a, d_cum_row), early_slots = _fox_bwd(qkv, cum_col, cum_row, y_fox32, lse, d_y_fox,
                                                                    ride=early_grads)
    dq_b, dk_b, dv_b = _sb_bwd(qkv, sb_run, d_y_sb)
    d_cum = jnp.transpose(d_cum_row.reshape(N_HEADS, s))
    d_cum = jnp.pad(d_cum, ((0, 0), (0, LANES - N_HEADS)))
    d_f, d_b_forget = _forget_bwd(d_cum, f_raw, b_f_pad)
    d_qkv = [dq_a, dk_a, dv_a, dq_b, dk_b, dv_b]
    g_qkv = list(_mm_pieces_tn(u1, d_qkv, name="mm_d_w_in_qkv"))
    g_ga, g_gb, g_f = _mm_pieces_tn(u1, [d_ga, d_gb, d_f], name="mm_d_w_in_gates")
    d_w_in_f = jnp.concatenate(g_qkv[:3] + [g_f[:, :N_HEADS]] + g_qkv[3:] + [g_ga, g_gb], axis=1)
    def x_bwd(du, dres, xb, sc):
        return dres + du * (1.0 + sc), _colsum(du * xb), _colsum(du)

    (grad_x, d_sc1, d_sh1), (in_slots,) = _mm_pieces_nt(
        d_qkv + [d_ga, d_gb, d_f], w_proj, name="mm_d_u1_x_bwd", epilogue=x_bwd, rows=[dx_res, x2], vecs=[sc1],
        n_sums=2, ride=_Ride([_cols_to_slots(d_w_in_f)], scatter=True))

    d_conv_b = _unpair_cols(d_conv_b_p)
    d_conv_w = _unpair_cols(d_conv_w_p)
    cb_pad = N_MOD * D_MODEL - 2 * D_FF
    small = jnp.concatenate(
        [d_sh1, d_sc1, d_gt1, d_sh2, d_sc2, d_gt2, jnp.pad(d_b_forget, ((0, 0), (0, D_MODEL - LANES))),
         d_ln1_g, d_ln1_b, d_ln2_g, d_ln2_b, jnp.pad(d_conv_b, ((0, 0), (0, cb_pad))),
         d_conv_w.reshape(1, 6 * D_FF), loss_part], axis=1)
    n_small = small.shape[1] // LANES - 1
    small = jnp.pad(small.reshape(n_small + 1, LANES), ((0, -(n_small + 1) % 8), (0, 0)))
    (small_all,) = _exchange("ag_small_grads", [small], scatter=False)
    loss = jnp.sum(small_all[:, n_small, 0])
    n_rep = n_small - 6 * D_FF // LANES
    cw8 = small_all[:, n_rep:n_small, :].reshape(N_DEV, 3, 2 * D_FF)
    cw8 = lax.dynamic_slice(cw8, (0, 0, me * UP_SHARD), (N_DEV, 3, UP_SHARD))
    dmod8 = small_all[:, :N_MOD * D_MODEL // LANES, :].reshape(N_DEV, N_MOD * D_MODEL)
    dmod_loc = lax.dynamic_slice(dmod8, (0, me * ADA_SHARD), (N_DEV, ADA_SHARD))

    def rows_of(a):
        return a.reshape(-1, LANES)

    def forget_rows(a):
        return rows_of(jnp.pad(a, (0, D_MODEL - N_HEADS)))

    def conv_b_rows(a):
        return rows_of(jnp.pad(a, (0, cb_pad)))

    rep_sizes = {"b_ada": N_MOD * D_MODEL, "b_forget": N_HEADS, "ln1_g": D_MODEL, "ln1_b": D_MODEL, "ln2_g": D_MODEL,
                 "ln2_b": D_MODEL, "conv_b": 2 * D_FF}
    rep_w = [rows_of(b_ada), forget_rows(b_forget), rows_of(ln1_g), rows_of(ln1_b), rows_of(ln2_g), rows_of(ln2_b),
             conv_b_rows(conv_b)]
    rep_m = [rows_of(m_b_ada), forget_rows(m_b_forget), rows_of(m_ln1_g), rows_of(m_ln1_b), rows_of(m_ln2_g),
             rows_of(m_ln2_b), conv_b_rows(m_conv_b)]
    rep_v = [rows_of(v_b_ada), forget_rows(v_b_forget), rows_of(v_ln1_g), rows_of(v_ln1_b), rows_of(v_ln2_g),
             rows_of(v_ln2_b), conv_b_rows(v_conv_b)]
    rep_out = _adamw_small(small_all, rep_w, rep_m, rep_v)
    rep = [{name: a.reshape(-1)[:size] for (name, size), a in zip(rep_sizes.items(), outs)} for outs in rep_out]
    r_conv_w = _adamw("adamw_conv_w", cw8, conv_w, m_conv_w, v_conv_w)
    r_ada = _adamw_ada(jnp.transpose(c_all.reshape(N_DEV, D_MODEL)), dmod_loc, w_ada, m_w_ada, v_w_ada)

    r_in = _adamw("adamw_w_in", in_slots, w_in, m_w_in, v_w_in)
    r_fox = _adamw("adamw_w_fox", early_slots[0], w_fox_proj, m_w_fox_proj, v_w_fox_proj)
    r_sb = _adamw("adamw_w_sb", early_slots[1], w_sb_proj, m_w_sb_proj, v_w_sb_proj)
    r_o = _adamw("adamw_w_o", early_slots[2], w_o, m_w_o, v_w_o)
    r_up = _adamw("adamw_w_up", early_slots[3], w_up, m_w_up, v_w_up)
    r_down = _adamw("adamw_w_down", early_slots[4], w_down, m_w_down, v_w_down)

    def leaf(i):
        return [r_ada[i], rep[i]["b_ada"], r_in[i], rep[i]["b_forget"], r_fox[i], r_sb[i], r_o[i], rep[i]["ln1_g"],
                rep[i]["ln1_b"], r_up[i], r_conv_w[i], rep[i]["conv_b"], r_down[i], rep[i]["ln2_g"], rep[i]["ln2_b"]]

    return (loss, grad_x.reshape(1, s, D_MODEL), *leaf(0), *leaf(1), *leaf(2), *leaf(3))
```

```python
import functools

import jax
import jax.numpy as jnp
from jax import lax
from jax.experimental import pallas as pl
from jax.experimental.pallas import tpu as pltpu

F32 = jnp.float32
BF16 = jnp.bfloat16
MESH = pl.DeviceIdType.MESH
ANY = pl.BlockSpec(memory_space=pl.ANY)

N_DEV = 8
D_MODEL = 1024
HEAD_DIM = 64
N_HEADS = 8
ATTN_W = N_HEADS * HEAD_DIM
D_FF = 2816
FF_HALF = D_FF // 2
N_MOD = 6
ADA_SHARD = N_MOD * D_MODEL // N_DEV
IN_SHARD = 641
UP_SHARD = 704
ATTN_SCALE = HEAD_DIM ** -0.5
ALPHA = 2.0 ** 0.25
LN_EPS = 1e-5
LANES = 128
TQ = 512
SCAN_W = 256
VMEM_LIMIT = 56 * 1024 * 1024

ADAM_LR, ADAM_B1, ADAM_B2, ADAM_EPS, ADAM_WD, ADAM_STEP = 0.001, 0.9, 0.999, 1e-08, 0.01, 10

W_QKV, W_GATES, W_F = 3072, 2048, 128
W_PROJ = 5376


def _params(sem=None):
    return pltpu.CompilerParams(dimension_semantics=sem, vmem_limit_bytes=VMEM_LIMIT)


def _tile(n, cap):
    if n <= cap:
        return n
    best = None
    for t in range(LANES, cap + 1, LANES):
        if n % t == 0:
            best = t
    assert best is not None, (n, cap)
    return best


def _row_tile(r, width, budget=192 * 1024):
    if r * width <= budget or r % 16:
        return r
    best = 16
    for t in range(16, r + 1, 16):
        if r % t == 0 and t * width <= budget:
            best = t
    return best


def _me():
    x, y, c = lax.axis_index("x"), lax.axis_index("y"), lax.axis_index("c")
    return x, y, c, 4 * x + 2 * y + c


def _peer(r):
    x, y, c, _ = _me()
    px = 1 - x if r & 4 else x
    py = 1 - y if r & 2 else y
    pc = 1 - c if r & 1 else c
    return (px, py, pc), 4 * px + 2 * py + pc


class _Ride:
    def __init__(self, arrays, scatter):
        self.arrays, self.scatter, self.n = list(arrays), scatter, len(arrays)
        self.in_specs = [ANY] * self.n
        self.out_specs = [ANY] * self.n
        self.out_shape = [jax.ShapeDtypeStruct(a.shape if scatter else (N_DEV,) + a.shape, a.dtype) for a in arrays]
        self.scratch = [pltpu.SemaphoreType.DMA((self.n, N_DEV - 1)), pltpu.SemaphoreType.DMA((self.n, N_DEV - 1)),
                        pltpu.SemaphoreType.DMA((self.n,))]

    def _local(self, ins, outs, sems, a):
        me = _me()[3]
        return pltpu.make_async_copy(ins[a].at[me] if self.scatter else ins[a], outs[a].at[me], sems[2].at[a])

    def _remote(self, ins, outs, sems, a, r, arriving):
        me = _me()[3]
        peer, pidx = _peer(r)
        src = ins[a].at[me if arriving else pidx] if self.scatter else ins[a]
        return pltpu.make_async_remote_copy(
            src_ref=src, dst_ref=outs[a].at[pidx if arriving else me], send_sem=sems[0].at[a, r - 1],
            recv_sem=sems[1].at[a, r - 1], device_id=peer, device_id_type=MESH)

    def start(self, ins, outs, sems):
        for a in range(self.n):
            self._local(ins, outs, sems, a).start()
        for r in range(1, N_DEV):
            for a in range(self.n):
                self._remote(ins, outs, sems, a, r, False).start()

    def wait(self, ins, outs, sems):
        for r in range(1, N_DEV):
            for a in range(self.n):
                self._remote(ins, outs, sems, a, r, True).wait_recv()
        for r in range(1, N_DEV):
            for a in range(self.n):
                self._remote(ins, outs, sems, a, r, False).wait_send()
        for a in range(self.n):
            self._local(ins, outs, sems, a).wait()


def _exchange(name, arrays, scatter):
    ride = _Ride(arrays, scatter)

    def body(*refs):
        ins, outs, sems = refs[:ride.n], refs[ride.n:2 * ride.n], refs[2 * ride.n:]
        ride.start(ins, outs, sems)
        ride.wait(ins, outs, sems)

    return pl.pallas_call(body, name=name, in_specs=ride.in_specs, out_specs=ride.out_specs, out_shape=ride.out_shape,
                          scratch_shapes=ride.scratch)(*arrays)


def _allgather_two_level(name, a):
    def body(a_ref, out_ref, send_sems, recv_sems, local_sem):
        x, y, c, me = _me()
        sibling = (x, y, 1 - c)
        chips = [(1 - x, y), (x, 1 - y), (1 - x, 1 - y)]

        def idx(px, py, pc):
            return 4 * px + 2 * py + pc

        def copy(k, block, to, src=None):
            slot = out_ref.at[idx(*block)]
            return pltpu.make_async_remote_copy(
                src_ref=slot if src is None else src, dst_ref=slot, send_sem=send_sems.at[k], recv_sem=recv_sems.at[k],
                device_id=to, device_id_type=MESH)

        mine = pltpu.make_async_copy(a_ref, out_ref.at[me], local_sem)
        mine.start()
        first = [copy(0, (x, y, c), sibling, src=a_ref)]
        first += [copy(1 + j, (x, y, c), (*chip, c), src=a_ref) for j, chip in enumerate(chips)]
        for cp in first:
            cp.start()
        passed = [copy(4 + j, (*chip, c), sibling) for j, chip in enumerate(chips)]
        for j, chip in enumerate(chips):
            copy(1 + j, (*chip, c), (x, y, c)).wait_recv()
            passed[j].start()
        copy(0, sibling, (x, y, c)).wait_recv()
        for j, chip in enumerate(chips):
            copy(4 + j, (*chip, 1 - c), (x, y, c)).wait_recv()
        for cp in first + passed:
            cp.wait_send()
        mine.wait()

    return pl.pallas_call(
        body, name=name, in_specs=[ANY], out_specs=ANY,
        out_shape=jax.ShapeDtypeStruct((N_DEV,) + a.shape, a.dtype),
        scratch_shapes=[pltpu.SemaphoreType.DMA((N_DEV - 1,)), pltpu.SemaphoreType.DMA((N_DEV - 1,)),
                        pltpu.SemaphoreType.DMA],
    )(a)


def _with_ride(body, ride, n_in, n_out, grid):
    if ride is None:
        return body
    n = ride.n

    def wrapped(*refs):
        ins, rins = refs[:n_in], refs[n_in:n_in + n]
        outs, routs = refs[n_in + n:n_in + n + n_out], refs[n_in + n + n_out:n_in + 2 * n + n_out]
        rest = refs[n_in + 2 * n + n_out:]
        scratch, sems = rest[:len(rest) - 3], rest[len(rest) - 3:]
        ids = [pl.program_id(d) for d in range(len(grid))]
        first = functools.reduce(lambda p, q: p & q, [i == 0 for i in ids])
        last = functools.reduce(lambda p, q: p & q, [i == g - 1 for i, g in zip(ids, grid)])

        @pl.when(first)
        def _():
            ride.start(rins, routs, sems)

        body(*ins, *outs, *scratch)

        @pl.when(last)
        def _():
            ride.wait(rins, routs, sems)

    return wrapped


def _ride_call(body, ride, *, name, grid, in_specs, out_specs, out_shape, scratch_shapes, sem, args):
    n_in, n_out = len(in_specs), len(out_specs)
    if ride is None:
        res = pl.pallas_call(body, name=name, grid=grid, in_specs=in_specs, out_specs=out_specs, out_shape=out_shape,
                             scratch_shapes=scratch_shapes, compiler_params=_params(sem))(*args)
        return list(res), []
    res = pl.pallas_call(
        _with_ride(body, ride, n_in, n_out, grid), name=name, grid=grid,
        in_specs=list(in_specs) + ride.in_specs, out_specs=list(out_specs) + ride.out_specs,
        out_shape=list(out_shape) + ride.out_shape, scratch_shapes=list(scratch_shapes) + ride.scratch,
        compiler_params=_params(("arbitrary",) * len(grid)))(*args, *ride.arrays)
    return list(res[:n_out]), list(res[n_out:])


def _mm(a, b, *, name, ta=False, tb=False, out_dtype=F32, tm=1024, tn=1024, tk=1024, ride=None):
    m, k = (a.shape[1], a.shape[0]) if ta else a.shape
    n = b.shape[0] if tb else b.shape[1]
    assert (b.shape[1] if tb else b.shape[0]) == k
    tm, tn, tk = _tile(m, tm), _tile(n, tn), _tile(k, tk)
    nk = k // tk
    a_spec = pl.BlockSpec((tk, tm), lambda i, j, l: (l, i)) if ta else pl.BlockSpec((tm, tk), lambda i, j, l: (i, l))
    b_spec = pl.BlockSpec((tn, tk), lambda i, j, l: (j, l)) if tb else pl.BlockSpec((tk, tn), lambda i, j, l: (l, j))
    dims = (((0,) if ta else (1,), (1,) if tb else (0,)), ((), ()))

    def body(a_ref, b_ref, o_ref, *acc):
        p = lax.dot_general(a_ref[...].astype(BF16), b_ref[...].astype(BF16), dims, preferred_element_type=F32)
        if nk == 1:
            o_ref[...] = p.astype(out_dtype)
            return
        acc_ref = acc[0]
        step = pl.program_id(2)

        @pl.when(step == 0)
        def _():
            acc_ref[...] = p

        @pl.when(step > 0)
        def _():
            acc_ref[...] += p

        @pl.when(step == nk - 1)
        def _():
            o_ref[...] = acc_ref[...].astype(out_dtype)

    outs, rode = _ride_call(
        body, ride, name=name, grid=(m // tm, n // tn, nk), in_specs=[a_spec, b_spec],
        out_specs=[pl.BlockSpec((tm, tn), lambda i, j, l: (i, j))], out_shape=[jax.ShapeDtypeStruct((m, n), out_dtype)],
        scratch_shapes=[] if nk == 1 else [pltpu.VMEM((tm, tn), F32)], sem=("parallel", "parallel", "arbitrary"),
        args=(a, b))
    return outs[0] if ride is None else (outs[0], rode)


def _mm_pieces_nt(pieces, w, *, name, epilogue, rows, vecs, out_dtypes=(F32,), n_sums=0, tm=512, w_kn=False, ride=None,
                  prologue=None):
    if prologue is not None:
        assert not pieces
        pro_fn, prows, pvecs = prologue
        prows = [r if isinstance(r, tuple) else (r, 0) for r in prows]
        s, widths = prows[0][0].shape[0], [w.shape[0] if w_kn else w.shape[1]]
        lead_args = [r for r, _ in prows] + list(pvecs)
        lead_specs = [pl.BlockSpec((min(tm, s), widths[0]), functools.partial(lambda i, cb: (i, cb), cb=cb))
                      for _, cb in prows] + [pl.BlockSpec(v.shape, lambda i: (0, 0)) for v in pvecs]
        out_dtypes = (BF16,) + tuple(out_dtypes)
    else:
        s, widths = pieces[0].shape[0], [p.shape[1] for p in pieces]
        lead_args = list(pieces)
        lead_specs = [pl.BlockSpec((min(tm, s), width), lambda i: (i, 0)) for width in widths]
    n = w.shape[1] if w_kn else w.shape[0]
    tm = min(tm, s)
    offs = [sum(widths[:i]) for i in range(len(widths))]
    rows = [r if isinstance(r, tuple) else (r, 0) for r in rows]
    n_p, n_r, n_v, n_o = len(lead_args), len(rows), len(vecs), len(out_dtypes)

    def body(*refs):
        p_refs, w_ref = refs[:n_p], refs[n_p]
        extra = [r[...] for r in refs[n_p + 1:n_p + 1 + n_r + n_v]]
        o_refs = refs[n_p + 1 + n_r + n_v:n_p + 1 + n_r + n_v + n_o]
        sum_refs = refs[n_p + 1 + n_r + n_v + n_o:]
        if prologue is not None:
            made = pro_fn(*[r[...] for r in p_refs]).astype(BF16)
            operands = [made]
        else:
            operands = [r[...] for r in p_refs]
        acc = None
        for operand, off, width in zip(operands, offs, widths):
            if w_kn:
                part = jnp.dot(operand, w_ref[off:off + width, :], preferred_element_type=F32)
            else:
                part = _dot_nt(operand, w_ref[:, off:off + width])
            acc = part if acc is None else acc + part
        res = epilogue(acc, *extra)
        if prologue is not None:
            res = (made,) + tuple(res)
        for o_ref, val in zip(o_refs, res[:n_o]):
            o_ref[...] = val.astype(o_ref.dtype)
        step = pl.program_id(0)
        for s_ref, val in zip(sum_refs, res[n_o:]):
            @pl.when(step == 0)
            def _(s_ref=s_ref, val=val):
                s_ref[...] = val

            @pl.when(step > 0)
            def _(s_ref=s_ref, val=val):
                s_ref[...] += val

    vec_spec = pl.BlockSpec((1, n), lambda i: (0, 0))
    out_widths = ([widths[0]] if prologue is not None else []) + [n] * (n_o - (prologue is not None))
    return _ride_call(
        body, ride, name=name, grid=(s // tm,),
        in_specs=lead_specs + [pl.BlockSpec(w.shape, lambda i: (0, 0))]
        + [pl.BlockSpec((tm, n), functools.partial(lambda i, cb: (i, cb), cb=cb)) for _, cb in rows] + [vec_spec] * n_v,
        out_specs=[pl.BlockSpec((tm, ow), lambda i: (i, 0)) for ow in out_widths] + [vec_spec] * n_sums,
        out_shape=[jax.ShapeDtypeStruct((s, ow), dt) for ow, dt in zip(out_widths, out_dtypes)]
        + [jax.ShapeDtypeStruct((1, n), F32)] * n_sums,
        scratch_shapes=[], sem=("arbitrary",) if n_sums else ("parallel",),
        args=(*lead_args, w, *[r for r, _ in rows], *vecs))


def _mm_pieces_tn(a, pieces, *, name, tk=1024):
    s, m = a.shape
    tk = min(tk, s)
    nk = s // tk
    n = len(pieces)
    dims = (((0,), (0,)), ((), ()))

    def body(*refs):
        a_ref, p_refs, o_refs, accs = refs[0], refs[1:1 + n], refs[1 + n:1 + 2 * n], refs[1 + 2 * n:]
        step = pl.program_id(0)
        a_blk = a_ref[...]
        for p_ref, o_ref, acc in zip(p_refs, o_refs, accs):
            part = lax.dot_general(a_blk, p_ref[...], dims, preferred_element_type=F32)

            @pl.when(step == 0)
            def _(acc=acc, part=part):
                acc[...] = part

            @pl.when(step > 0)
            def _(acc=acc, part=part):
                acc[...] += part

            @pl.when(step == nk - 1)
            def _(acc=acc, o_ref=o_ref):
                o_ref[...] = acc[...].astype(BF16)

    return pl.pallas_call(
        body, name=name, grid=(nk,),
        in_specs=[pl.BlockSpec((tk, m), lambda l: (l, 0))] + [pl.BlockSpec((tk, p.shape[1]), lambda l: (l, 0)) for p in pieces],
        out_specs=[pl.BlockSpec((m, p.shape[1]), lambda l: (0, 0)) for p in pieces],
        out_shape=[jax.ShapeDtypeStruct((m, p.shape[1]), BF16) for p in pieces],
        scratch_shapes=[pltpu.VMEM((m, p.shape[1]), F32) for p in pieces],
        compiler_params=_params(("arbitrary",)),
    )(a, *pieces)


def _rowwise(name, fn, rows, vecs, outs, sums=(), tr=512):
    s = rows[0][0].shape[0]
    tr = min(tr, s)
    nr, nv, no = len(rows), len(vecs), len(outs)

    def body(*refs):
        vals = [r[...] for r in refs[:nr + nv]]
        res = fn(*vals)
        for o_ref, val in zip(refs[nr + nv:nr + nv + no], res[:no]):
            o_ref[...] = val.astype(o_ref.dtype)
        step = pl.program_id(0)
        for s_ref, val in zip(refs[nr + nv + no:], res[no:]):
            @pl.when(step == 0)
            def _(s_ref=s_ref, val=val):
                s_ref[...] = val

            @pl.when(step > 0)
            def _(s_ref=s_ref, val=val):
                s_ref[...] += val

    in_specs = [pl.BlockSpec((tr, w), functools.partial(lambda i, cb: (i, cb), cb=cb)) for _, w, cb in rows]
    in_specs += [pl.BlockSpec(v.shape, lambda i: (0, 0)) for v in vecs]
    out_specs = [pl.BlockSpec((tr, w), lambda i: (i, 0)) for w, _ in outs]
    out_specs += [pl.BlockSpec((1, w), lambda i: (0, 0)) for w in sums]
    out_shape = [jax.ShapeDtypeStruct((s, w), dt) for w, dt in outs]
    out_shape += [jax.ShapeDtypeStruct((1, w), F32) for w in sums]
    return pl.pallas_call(
        body, name=name, grid=(s // tr,), in_specs=in_specs, out_specs=out_specs, out_shape=out_shape,
        compiler_params=_params(("arbitrary",) if sums else ("parallel",)),
    )(*[r[0] for r in rows], *vecs)


def _colsum(x):
    return jnp.sum(x, axis=0, keepdims=True)


def _sigmoid(x):
    return 1.0 / (1.0 + jnp.exp(-x))


def _log_sigmoid(x):
    return jnp.minimum(x, 0.0) - jnp.log(1.0 + jnp.exp(-jnp.abs(x)))


def _ln_stats(r):
    mu = jnp.mean(r, axis=-1, keepdims=True)
    xc = r - mu
    var = jnp.mean(xc * xc, axis=-1, keepdims=True)
    rstd = lax.rsqrt(var + LN_EPS)
    return xc * rstd, rstd


def _ln_bwd(dy, xhat, rstd, g):
    dxh = dy * g
    m1 = jnp.mean(dxh, axis=-1, keepdims=True)
    m2 = jnp.mean(dxh * xhat, axis=-1, keepdims=True)
    return rstd * (dxh - m1 - xhat * m2)


def _mod_exchange(c_row, w_ada, b_ada_loc):
    def body(c_ref, w_ref, b_ref, call_ref, mod_ref, piece_ref, send_sems, recv_sems):
        me = _me()[3]
        call_ref[me] = c_ref[...]
        sent = []
        for r in range(1, N_DEV):
            peer, _ = _peer(r)
            cp = pltpu.make_async_remote_copy(
                src_ref=c_ref, dst_ref=call_ref.at[me], send_sem=send_sems.at[0, r - 1],
                recv_sem=recv_sems.at[0, r - 1], device_id=peer, device_id_type=MESH)
            cp.start()
            sent.append(cp)
        for r in range(1, N_DEV):
            peer, pidx = _peer(r)
            pltpu.make_async_remote_copy(
                src_ref=c_ref, dst_ref=call_ref.at[pidx], send_sem=send_sems.at[0, r - 1],
                recv_sem=recv_sems.at[0, r - 1], device_id=peer, device_id_type=MESH).wait_recv()
        c_all = jnp.concatenate([call_ref[d] for d in range(N_DEV)], axis=0)
        mod_loc = jnp.dot(c_all, w_ref[...], preferred_element_type=F32,
                          precision=lax.Precision.HIGHEST) + b_ref[...]
        for d in range(N_DEV):
            piece_ref[d] = mod_loc[d:d + 1, :]
        mod_ref[me] = piece_ref[me]
        for r in range(1, N_DEV):
            peer, pidx = _peer(r)
            cp = pltpu.make_async_remote_copy(
                src_ref=piece_ref.at[pidx], dst_ref=mod_ref.at[me], send_sem=send_sems.at[1, r - 1],
                recv_sem=recv_sems.at[1, r - 1], device_id=peer, device_id_type=MESH)
            cp.start()
            sent.append(cp)
        for r in range(1, N_DEV):
            peer, pidx = _peer(r)
            pltpu.make_async_remote_copy(
                src_ref=piece_ref.at[me], dst_ref=mod_ref.at[pidx], send_sem=send_sems.at[1, r - 1],
                recv_sem=recv_sems.at[1, r - 1], device_id=peer, device_id_type=MESH).wait_recv()
        for cp in sent:
            cp.wait_send()

    vmem = pl.BlockSpec(memory_space=pltpu.VMEM)
    return pl.pallas_call(
        body, name="mod_exchange", in_specs=[vmem, vmem, vmem], out_specs=[vmem, vmem],
        out_shape=[jax.ShapeDtypeStruct((N_DEV, 1, D_MODEL), F32), jax.ShapeDtypeStruct((N_DEV, 1, ADA_SHARD), F32)],
        scratch_shapes=[pltpu.VMEM((N_DEV, 1, ADA_SHARD), F32),
                        pltpu.SemaphoreType.DMA((2, N_DEV - 1)), pltpu.SemaphoreType.DMA((2, N_DEV - 1))],
        compiler_params=_params(),
    )(c_row, w_ada, b_ada_loc)


def _split3(x):
    hi = x.astype(BF16)
    r1 = x - hi.astype(F32)
    mid = r1.astype(BF16)
    lo = (r1 - mid.astype(F32)).astype(BF16)
    return hi, mid, lo


def _scan_rows(x_ref, o_ref, s, reverse, pre=None, post=None):
    tb = min(TQ, s)
    nb = s // tb
    row = lax.broadcasted_iota(jnp.int32, (tb, tb), 0)
    col = lax.broadcasted_iota(jnp.int32, (tb, tb), 1)
    tri = jnp.where((col >= row) if reverse else (col <= row), 1.0, 0.0).astype(BF16)

    def step(i, carry):
        blk = (nb - 1 - i) if reverse else i
        off = pl.multiple_of(blk * tb, tb)
        x = x_ref[pl.ds(off, tb), :]
        if pre is not None:
            x = pre(x, off)
        acc = carry
        for piece in _split3(x):
            acc = acc + jnp.dot(tri, piece, preferred_element_type=F32)
        o_ref[pl.ds(off, tb), :] = acc if post is None else post(acc, off)
        edge = acc[0:1, :] if reverse else acc[tb - 1:tb, :]
        return jnp.broadcast_to(edge, (tb, LANES))

    lax.fori_loop(0, nb, step, jnp.zeros((tb, LANES), F32))


def _forget_cumsum(f_raw, b_pad):
    s = f_raw.shape[0]

    def body(f_ref, b_ref, cum_ref):
        b = b_ref[...]
        _scan_rows(f_ref, cum_ref, s, False, pre=lambda x, off: _log_sigmoid(x + b))

    vmem = pl.BlockSpec(memory_space=pltpu.VMEM)
    return pl.pallas_call(body, name="forget_cumsum", in_specs=[vmem, vmem], out_specs=vmem,
                          out_shape=jax.ShapeDtypeStruct((s, LANES), F32), compiler_params=_params())(f_raw, b_pad)


def _forget_bwd(dcum, f_raw, b_pad):
    s = f_raw.shape[0]

    def body(d_ref, f_ref, b_ref, df_ref, db_ref, tmp_ref):
        b = b_ref[...]
        _scan_rows(d_ref, tmp_ref, s, True)
        df = tmp_ref[...] * _sigmoid(-(f_ref[...] + b))
        df_ref[...] = df.astype(BF16)
        db_ref[...] = _colsum(df)

    vmem = pl.BlockSpec(memory_space=pltpu.VMEM)
    return pl.pallas_call(
        body, name="forget_bwd", in_specs=[vmem, vmem, vmem], out_specs=[vmem, vmem],
        out_shape=[jax.ShapeDtypeStruct((s, LANES), BF16), jax.ShapeDtypeStruct((1, LANES), F32)],
        scratch_shapes=[pltpu.VMEM((s, LANES), F32)], compiler_params=_params())(dcum, f_raw, b_pad)


def _dot_nt(a, b):
    return lax.dot_general(a, b, (((1,), (1,)), ((), ())), preferred_element_type=F32)


def _head_masks():
    lane = lax.broadcasted_iota(jnp.int32, (TQ, LANES), 1)
    return lane, [lane < HEAD_DIM, lane >= HEAD_DIM]


def _pick(mask, x):
    return jnp.where(mask, x, jnp.zeros_like(x))


def _qkv_specs(s, col0):
    nb = ATTN_W // LANES
    return [pl.BlockSpec((TQ, LANES), lambda hp, qi: (qi, col0 + hp)),
            pl.BlockSpec((s, LANES), lambda hp, qi: (0, col0 + nb + hp)),
            pl.BlockSpec((s, LANES), lambda hp, qi: (0, col0 + 2 * nb + hp))]


def _pair_spec():
    return pl.BlockSpec((TQ, LANES), lambda hp, qi: (qi, hp))


def _diag_mask(width, strict):
    row = lax.broadcasted_iota(jnp.int32, (TQ, width * TQ), 0) + (width - 1) * TQ
    col = lax.broadcasted_iota(jnp.int32, (TQ, width * TQ), 1)
    return (col < row) if strict else (col <= row)


def _fox_fwd(qkv, cum_col, cum_row, ride=None):
    s = qkv.shape[0]
    nq = s // TQ

    def body(q_ref, k_ref, v_ref, cc_ref, cr_ref, o_ref, o32_ref, lse_ref):
        hp, qi = pl.program_id(0), pl.program_id(1)
        lane, masks = _head_masks()
        q2 = q_ref[...] * jnp.asarray(ATTN_SCALE, BF16)
        cc = cc_ref[...]
        qms = [_pick(masks[e], q2) for e in range(2)]
        cqs = [jnp.sum(jnp.where(lane == 2 * hp + e, cc, 0.0), axis=1, keepdims=True) for e in range(2)]

        def tile(kb, carry, masked, width=1):
            off, span = pl.multiple_of(kb * TQ, TQ), width * TQ
            k2, v2 = k_ref[pl.ds(off, span), :], v_ref[pl.ds(off, span), :]
            head0 = lax.broadcasted_iota(jnp.int32, (span, LANES), 1) < HEAD_DIM
            new = []
            for e in range(2):
                m, acc = carry[e]
                sc = _dot_nt(qms[e], k2) + (cqs[e] - cr_ref[e:e + 1, pl.ds(off, span)])
                if masked:
                    sc = jnp.where(_diag_mask(width, False), sc, -jnp.inf)
                m_new = jnp.maximum(m, jnp.max(sc, axis=1, keepdims=True))
                p = jnp.exp(sc - m_new)
                v_ones = jnp.where(head0 if e == 0 else ~head0, v2, jnp.ones_like(v2))
                acc = jnp.exp(m - m_new) * acc + jnp.dot(p.astype(BF16), v_ones, preferred_element_type=F32)
                new.append((m_new, acc))
            return tuple(new)

        init = (jnp.full((TQ, 1), -jnp.inf, F32), jnp.zeros((TQ, LANES), F32))
        carry = lax.cond(qi % 2 == 1, lambda cr: tile(qi - 1, cr, True, 2), lambda cr: tile(qi, cr, True), (init, init))
        quads = qi // 4
        carry = lax.fori_loop(0, quads, lambda j, cr: tile(4 * j, cr, False, 4), carry)
        carry = lax.cond(qi % 4 >= 2, lambda cr: tile(4 * quads, cr, False, 2), lambda cr: cr, carry)
        sums = [jnp.max(jnp.where(masks[1 - e], carry[e][1], 0.0), axis=1, keepdims=True) for e in range(2)]
        outs = [carry[e][1] / sums[e] for e in range(2)]
        lses = [carry[e][0] + jnp.log(sums[e]) for e in range(2)]
        out = jnp.where(masks[0], outs[0], outs[1])
        o_ref[...] = out.astype(BF16)
        o32_ref[...] = out
        lse_ref[...] = jnp.where(masks[0], lses[0], lses[1])

    return _ride_call(
        body, ride, name="fox_fwd", grid=(N_HEADS // 2, nq),
        in_specs=_qkv_specs(s, 0) + [pl.BlockSpec((TQ, LANES), lambda hp, qi: (qi, 0)),
                                     pl.BlockSpec((None, 2, s), lambda hp, qi: (hp, 0, 0))],
        out_specs=[_pair_spec(), _pair_spec(), _pair_spec()],
        out_shape=[jax.ShapeDtypeStruct((s, ATTN_W), BF16), jax.ShapeDtypeStruct((s, ATTN_W), F32),
                   jax.ShapeDtypeStruct((s, ATTN_W), F32)],
        scratch_shapes=[], sem=("parallel", "parallel"), args=(qkv, qkv, qkv, cum_col, cum_row))


def _write_transposed(acc_ref, out_ref):
    for c in range(out_ref.shape[0] // TQ):
        out_ref[c * TQ:(c + 1) * TQ, :] = jnp.transpose(acc_ref[:, c * TQ:(c + 1) * TQ]).astype(BF16)


def _fox_bwd(qkv, cum_col, cum_row, o, lse, do, ride=None):
    s = qkv.shape[0]
    nq = s // TQ

    def body(q_ref, k_ref, v_ref, cc_ref, cr_ref, o_ref, lse_ref, do_ref,
             dq_ref, dk_ref, dv_ref, dcr_ref, dk_acc, dv_acc):
        hp, qi = pl.program_id(0), pl.program_id(1)

        @pl.when(qi == 0)
        def _():
            dk_acc[...] = jnp.zeros_like(dk_acc)
            dv_acc[...] = jnp.zeros_like(dv_acc)
            dcr_ref[...] = jnp.zeros_like(dcr_ref)

        lane, masks = _head_masks()
        q2 = q_ref[...] * jnp.asarray(ATTN_SCALE, BF16)
        do2 = do_ref[...]
        prod = do2.astype(F32) * o_ref[...].astype(F32)
        lse2 = lse_ref[...]
        cc = cc_ref[...]
        qms = [_pick(masks[e], q2) for e in range(2)]
        doms = [_pick(masks[e], do2) for e in range(2)]
        deltas = [jnp.sum(jnp.where(masks[e], prod, 0.0), axis=1, keepdims=True) for e in range(2)]
        lses = [jnp.max(jnp.where(masks[e], lse2, -jnp.inf), axis=1, keepdims=True) for e in range(2)]
        cqs = [jnp.sum(jnp.where(lane == 2 * hp + e, cc, 0.0), axis=1, keepdims=True) for e in range(2)]
        qts = [jnp.transpose(qms[e].astype(F32)).astype(BF16) for e in range(2)]
        dots = [jnp.transpose(doms[e].astype(F32)).astype(BF16) for e in range(2)]
        ones_row = [HEAD_DIM * (1 - e) for e in range(2)]
        trow = lax.broadcasted_iota(jnp.int32, (LANES, TQ), 0)
        qts = [jnp.where(trow == ones_row[e], jnp.ones_like(qts[e]), qts[e]) for e in range(2)]

        def tile(kb, carry, masked, width=1):
            off, span = pl.multiple_of(kb * TQ, TQ), width * TQ
            k2, v2 = k_ref[pl.ds(off, span), :], v_ref[pl.ds(off, span), :]
            head0 = lax.broadcasted_iota(jnp.int32, (span, LANES), 1) < HEAD_DIM
            new, dks, dv = [], [], None
            for e in range(2):
                dq = carry[e]
                sc = _dot_nt(qms[e], k2) + (cqs[e] - cr_ref[e:e + 1, pl.ds(off, span)])
                p = jnp.exp(sc - lses[e])
                if masked:
                    p = jnp.where(_diag_mask(width, False), p, 0.0)
                ds = p * (_dot_nt(doms[e], v2) - deltas[e])
                dsb = ds.astype(BF16)
                dk_e = jnp.dot(qts[e], dsb, preferred_element_type=F32)
                dv_e = jnp.dot(dots[e], p.astype(BF16), preferred_element_type=F32)
                dks.append(dk_e)
                dv = dv_e if e == 0 else dv + dv_e
                dcr_ref[e:e + 1, pl.ds(off, span)] -= dk_e[ones_row[e]:ones_row[e] + 1, :]
                k_ones = jnp.where(head0 if e == 0 else ~head0, k2, jnp.ones_like(k2))
                new.append(dq + jnp.dot(dsb, k_ones, preferred_element_type=F32))
            krow = lax.broadcasted_iota(jnp.int32, (LANES, span), 0)
            dk_acc[:, pl.ds(off, span)] += jnp.where(krow < HEAD_DIM, dks[0], dks[1])
            dv_acc[:, pl.ds(off, span)] += dv
            return tuple(new)

        init = jnp.zeros((TQ, LANES), F32)
        carry = lax.fori_loop(0, qi // 2, lambda j, cr: tile(2 * j, cr, False, 2), (init, init))
        carry = lax.cond(qi % 2 == 1, lambda cr: tile(qi - 1, cr, True, 2), lambda cr: tile(qi, cr, True), carry)
        dq_ref[...] = (jnp.where(masks[0], carry[0], carry[1]) * ATTN_SCALE).astype(BF16)
        for e in range(2):
            dcr_ref[e:e + 1, pl.ds(pl.multiple_of(qi * TQ, TQ), TQ)] += jnp.transpose(carry[e])[
                ones_row[e]:ones_row[e] + 1, :]

        @pl.when(qi == nq - 1)
        def _():
            _write_transposed(dk_acc, dk_ref)
            _write_transposed(dv_acc, dv_ref)

    seq_spec = pl.BlockSpec((s, LANES), lambda hp, qi: (0, hp))
    return _ride_call(
        body, ride, name="fox_bwd", grid=(N_HEADS // 2, nq),
        in_specs=_qkv_specs(s, 0) + [pl.BlockSpec((TQ, LANES), lambda hp, qi: (qi, 0)),
                                     pl.BlockSpec((None, 2, s), lambda hp, qi: (hp, 0, 0)),
                                     _pair_spec(), _pair_spec(), _pair_spec()],
        out_specs=[_pair_spec(), seq_spec, seq_spec, pl.BlockSpec((None, 2, s), lambda hp, qi: (hp, 0, 0))],
        out_shape=[jax.ShapeDtypeStruct((s, ATTN_W), BF16)] * 3 + [jax.ShapeDtypeStruct((N_HEADS // 2, 2, s), F32)],
        scratch_shapes=[pltpu.VMEM((LANES, s), F32), pltpu.VMEM((LANES, s), F32)],
        sem=("parallel", "arbitrary"), args=(qkv, qkv, qkv, cum_col, cum_row, o, lse, do))


def _scan_matrix(reverse):
    row = lax.broadcasted_iota(jnp.int32, (SCAN_W, SCAN_W), 0)
    col = lax.broadcasted_iota(jnp.int32, (SCAN_W, SCAN_W), 1)
    return jnp.where((row > col) if reverse else (row < col), 1.0, 0.0).astype(BF16)


def _scan_cols(x, tri, reverse, init):
    nblk = x.shape[1] // SCAN_W
    parts, total = [None] * nblk, init
    far = 0 if reverse else SCAN_W - 1
    for b in (reversed(range(nblk)) if reverse else range(nblk)):
        blk = x[:, b * SCAN_W:(b + 1) * SCAN_W]
        part = jnp.dot(blk.astype(BF16), tri, preferred_element_type=F32)
        parts[b] = part + total
        total = total + (part[:, far:far + 1] + blk[:, far:far + 1])
    return (parts[0] if nblk == 1 else jnp.concatenate(parts, axis=1)), total


def _sb_logits(qm, k2):
    z = _dot_nt(qm, k2)
    neg_abs = lax.bitcast_convert_type(lax.bitcast_convert_type(z, jnp.uint32) | jnp.uint32(0x80000000), F32)
    soft = jnp.log(1.0 + jnp.exp(neg_abs))
    lb = jnp.minimum(z, 0.0) - soft
    return lb, lb - z


TILE_SLOTS = 4


def _tri_base(qi):
    return (qi * (qi + 1)) // 2


def _sb_fwd(qkv):
    s = qkv.shape[0]
    nq = s // TQ

    def body(q_ref, k_ref, v_ref, o_ref, t_ref, buf, sems):
        hp, qi = pl.program_id(0), pl.program_id(1)
        _, masks = _head_masks()
        suffix = _scan_matrix(True)
        q2 = q_ref[...] * jnp.asarray(ATTN_SCALE, BF16)
        qms = [_pick(masks[e], q2) for e in range(2)]
        base = _tri_base(qi)

        def store(e, kb):
            slot = kb % TILE_SLOTS
            return pltpu.make_async_copy(buf.at[e, slot], t_ref.at[2 * hp + e, base + kb], sems.at[e, slot])

        def tile(kb, carry, masked, width=1):
            off, span = pl.multiple_of(kb * TQ, TQ), width * TQ
            k2, v2 = k_ref[pl.ds(off, span), :], v_ref[pl.ds(off, span), :]
            new, stored = [], []
            for e in range(2):
                run, acc = carry[e]
                lb, lo = _sb_logits(qms[e], k2)
                if masked:
                    strict = _diag_mask(width, True)
                    lo = jnp.where(strict, lo, 0.0)
                rest, run = _scan_cols(lo, suffix, True, run)
                a = jnp.exp(lb + rest)
                if masked:
                    a = jnp.where(strict, a, 0.0)
                ab, lbb = a.astype(BF16), lb.astype(BF16)
                acc = acc + jnp.dot(ab, v2, preferred_element_type=F32)
                stored.append((ab, lbb))
                new.append((run, acc))
            for e in range(2):
                ab, lbb = stored[e]
                for w in range(width):
                    blk = kb + w

                    @pl.when(blk + TILE_SLOTS <= qi)
                    def _(e=e, blk=blk):
                        store(e, blk + TILE_SLOTS).wait()
                    buf[e, blk % TILE_SLOTS, 0] = ab[:, w * TQ:(w + 1) * TQ]
                    buf[e, blk % TILE_SLOTS, 1] = lbb[:, w * TQ:(w + 1) * TQ]
                    store(e, blk).start()
            return tuple(new)

        init = (jnp.zeros((TQ, 1), F32), jnp.zeros((TQ, LANES), F32))
        carry = lax.cond(qi % 2 == 1, lambda cr: tile(qi - 1, cr, True, 2), lambda cr: tile(qi, cr, True), (init, init))
        pairs = qi // 2
        carry = lax.fori_loop(0, pairs, lambda it, cr: tile(2 * (pairs - 1 - it), cr, False, 2), carry)
        for e in range(2):
            for blk in range(TILE_SLOTS):
                @pl.when(qi >= blk)
                def _(e=e, blk=blk):
                    store(e, blk).wait()
        o_ref[...] = jnp.where(masks[0], carry[0][1], carry[1][1]).astype(BF16)

    ntri = nq * (nq + 1) // 2
    return pl.pallas_call(
        body, name="sb_fwd", grid=(N_HEADS // 2, nq), in_specs=_qkv_specs(s, 3 * ATTN_W // LANES),
        out_specs=[_pair_spec(), ANY],
        out_shape=[jax.ShapeDtypeStruct((s, ATTN_W), BF16), jax.ShapeDtypeStruct((N_HEADS, ntri, 2, TQ, TQ), BF16)],
        scratch_shapes=[pltpu.VMEM((2, TILE_SLOTS, 2, TQ, TQ), BF16), pltpu.SemaphoreType.DMA((2, TILE_SLOTS))],
        compiler_params=_params(("arbitrary", "arbitrary")),
    )(qkv, qkv, qkv)


def _sb_bwd(qkv, tiles, do):
    s = qkv.shape[0]
    nq = s // TQ

    def body(q_ref, k_ref, v_ref, t_ref, do_ref, dq_ref, dk_ref, dv_ref, dk_acc, dv_acc, buf, sems):
        hp, qi = pl.program_id(0), pl.program_id(1)

        @pl.when(qi == 0)
        def _():
            dk_acc[...] = jnp.zeros_like(dk_acc)
            dv_acc[...] = jnp.zeros_like(dv_acc)

        _, masks = _head_masks()
        prefix = _scan_matrix(False)
        q2 = q_ref[...] * jnp.asarray(ATTN_SCALE, BF16)
        do2 = do_ref[...]
        qms = [_pick(masks[e], q2) for e in range(2)]
        doms = [_pick(masks[e], do2) for e in range(2)]
        qts = [jnp.transpose(qms[e].astype(F32)).astype(BF16) for e in range(2)]
        dots = [jnp.transpose(doms[e].astype(F32)).astype(BF16) for e in range(2)]
        base = _tri_base(qi)

        def fetch(e, kb):
            slot = kb % TILE_SLOTS
            return pltpu.make_async_copy(t_ref.at[2 * hp + e, base + kb], buf.at[e, slot], sems.at[e, slot])

        for e in range(2):
            fetch(e, 0).start()

            @pl.when(qi >= 1)
            def _(e=e):
                fetch(e, 1).start()

        def tile(kb, carry, masked, width=1):
            off, span = pl.multiple_of(kb * TQ, TQ), width * TQ
            k2, v2 = k_ref[pl.ds(off, span), :], v_ref[pl.ds(off, span), :]
            new, dk, dv = [], None, None
            for e in range(2):
                if not masked:
                    for blk in range(2, 2 + width):
                        @pl.when(kb + blk <= qi)
                        def _(e=e, blk=blk):
                            fetch(e, kb + blk).start()
                for w in range(width):
                    fetch(e, kb + w).wait()
            for e in range(2):
                gsum, dq = carry[e]
                slots = [(kb + w) % TILE_SLOTS for w in range(width)]
                ab = buf[e, slots[0], 0] if width == 1 else jnp.concatenate([buf[e, sl, 0] for sl in slots], axis=1)
                lbb = buf[e, slots[0], 1] if width == 1 else jnp.concatenate([buf[e, sl, 1] for sl in slots], axis=1)
                beta = jnp.exp(lbb.astype(F32))
                g = ab.astype(F32) * _dot_nt(doms[e], v2)
                before, gsum = _scan_cols(g, prefix, False, gsum)
                dz = g - beta * (g + before)
                if masked:
                    dz = jnp.where(_diag_mask(width, True), dz, 0.0)
                dzb = dz.astype(BF16)
                dk_e = jnp.dot(qts[e], dzb, preferred_element_type=F32)
                dv_e = jnp.dot(dots[e], ab, preferred_element_type=F32)
                dk, dv = (dk_e, dv_e) if e == 0 else (dk + dk_e, dv + dv_e)
                new.append((gsum, dq + jnp.dot(dzb, k2, preferred_element_type=F32)))
            dk_acc[:, pl.ds(off, span)] += dk
            dv_acc[:, pl.ds(off, span)] += dv
            return tuple(new)

        init = (jnp.zeros((TQ, 1), F32), jnp.zeros((TQ, LANES), F32))
        carry = lax.fori_loop(0, qi // 2, lambda j, cr: tile(2 * j, cr, False, 2), (init, init))
        carry = lax.cond(qi % 2 == 1, lambda cr: tile(qi - 1, cr, True, 2), lambda cr: tile(qi, cr, True), carry)
        dq_ref[...] = (jnp.where(masks[0], carry[0][1], carry[1][1]) * ATTN_SCALE).astype(BF16)

        @pl.when(qi == nq - 1)
        def _():
            _write_transposed(dk_acc, dk_ref)
            _write_transposed(dv_acc, dv_ref)

    seq_spec = pl.BlockSpec((s, LANES), lambda hp, qi: (0, hp))
    return pl.pallas_call(
        body, name="sb_bwd", grid=(N_HEADS // 2, nq),
        in_specs=_qkv_specs(s, 3 * ATTN_W // LANES) + [ANY, _pair_spec()],
        out_specs=[_pair_spec(), seq_spec, seq_spec],
        out_shape=[jax.ShapeDtypeStruct((s, ATTN_W), BF16)] * 3,
        scratch_shapes=[pltpu.VMEM((LANES, s), F32), pltpu.VMEM((LANES, s), F32),
                        pltpu.VMEM((2, TILE_SLOTS, 2, TQ, TQ), BF16), pltpu.SemaphoreType.DMA((2, TILE_SLOTS))],
        compiler_params=_params(("arbitrary", "arbitrary")),
    )(qkv, qkv, qkv, tiles, do)


CONV_TR = 512


def _shift_down(x, halo, n):
    rolled = pltpu.roll(x, n, 0)
    head = rolled[0:8, :]
    rid = lax.broadcasted_iota(jnp.int32, head.shape, 0)
    for j in range(n):
        head = jnp.where(rid == j, halo[8 - n + j:8 - n + j + 1, :], head)
    return jnp.concatenate([head, rolled[8:, :]], axis=0)


def _shift_up(x, halo, n):
    rows = x.shape[0]
    rolled = pltpu.roll(x, rows - n, 0)
    tail = rolled[rows - 8:, :]
    rid = lax.broadcasted_iota(jnp.int32, tail.shape, 0)
    for j in range(n):
        tail = jnp.where(rid == 8 - n + j, halo[j:j + 1, :], tail)
    return jnp.concatenate([rolled[:rows - 8, :], tail], axis=0)


def _conv_fwd_block(x, halo, w, b):
    return b + _shift_down(x, halo, 2) * w[0:1, :] + _shift_down(x, halo, 1) * w[1:2, :] + x * w[2:3, :]


def _conv_specs(tr, s):
    pair = 2 * FF_HALF
    blk = pl.BlockSpec((tr, pair), lambda j, i: (i, j))
    prev = pl.BlockSpec((8, pair), lambda j, i: (jnp.maximum(i * (tr // 8) - 1, 0), j))
    nxt = pl.BlockSpec((8, pair), lambda j, i: (jnp.minimum((i + 1) * (tr // 8), s // 8 - 1), j))
    return blk, prev, nxt


def _conv_gate_fwd(hpre, conv_w, conv_b):
    s = hpre.shape[0]
    tr = min(CONV_TR, s)
    blk, prev, _ = _conv_specs(tr, s)

    def body(x_ref, halo_ref, w_ref, b_ref, a_ref):
        i = pl.program_id(1)
        halo = jnp.where(i > 0, halo_ref[...], 0.0)
        h = _conv_fwd_block(x_ref[...], halo, w_ref[...], b_ref[...])
        hg, hv = h[:, :FF_HALF], h[:, FF_HALF:]
        a_ref[...] = (hg * _sigmoid(hg) * hv).astype(BF16)

    return pl.pallas_call(
        body, name="conv_gate_fwd", grid=(2, s // tr),
        in_specs=[blk, prev, pl.BlockSpec((3, 2 * FF_HALF), lambda j, i: (0, j)),
                  pl.BlockSpec((1, 2 * FF_HALF), lambda j, i: (0, j))],
        out_specs=pl.BlockSpec((tr, FF_HALF), lambda j, i: (i, j)),
        out_shape=jax.ShapeDtypeStruct((s, D_FF), BF16),
        compiler_params=_params(("parallel", "parallel")),
    )(hpre, hpre, conv_w, conv_b)


def _conv_gate_bwd(hpre, da, conv_w, conv_b):
    s = hpre.shape[0]
    tr = min(CONV_TR, s)
    blk, prev, _ = _conv_specs(tr, s)

    def body(x_ref, halo_ref, da_ref, w_ref, b_ref, dh_ref, db_ref, dw_ref):
        i = pl.program_id(1)
        halo = jnp.where(i > 0, halo_ref[...], 0.0)
        x = x_ref[...]
        h = _conv_fwd_block(x, halo, w_ref[...], b_ref[...])
        hg, hv = h[:, :FF_HALF], h[:, FF_HALF:]
        da_blk = da_ref[...].astype(F32)
        sg = _sigmoid(hg)
        dhg = da_blk * hv * (sg * (1.0 + hg * (1.0 - sg)))
        dhv = da_blk * (hg * sg)
        dh_ref[:, :FF_HALF] = dhg.astype(BF16)
        dh_ref[:, FF_HALF:] = dhv.astype(BF16)
        x2, x1 = _shift_down(x, halo, 2), _shift_down(x, halo, 1)
        parts = []
        for lo, dpart in ((0, dhg), (FF_HALF, dhv)):
            cols = slice(lo, lo + FF_HALF)
            parts.append((cols, _colsum(dpart), _colsum(dpart * x2[:, cols]), _colsum(dpart * x1[:, cols]),
                          _colsum(dpart * x[:, cols])))

        @pl.when(i == 0)
        def _():
            for cols, db, dw0, dw1, dw2 in parts:
                db_ref[:, cols] = db
                dw_ref[0:1, cols] = dw0
                dw_ref[1:2, cols] = dw1
                dw_ref[2:3, cols] = dw2

        @pl.when(i > 0)
        def _():
            for cols, db, dw0, dw1, dw2 in parts:
                db_ref[:, cols] += db
                dw_ref[0:1, cols] += dw0
                dw_ref[1:2, cols] += dw1
                dw_ref[2:3, cols] += dw2

    pair = 2 * FF_HALF
    return pl.pallas_call(
        body, name="conv_gate_bwd", grid=(2, s // tr),
        in_specs=[blk, prev, pl.BlockSpec((tr, FF_HALF), lambda j, i: (i, j)),
                  pl.BlockSpec((3, pair), lambda j, i: (0, j)), pl.BlockSpec((1, pair), lambda j, i: (0, j))],
        out_specs=[blk, pl.BlockSpec((1, pair), lambda j, i: (0, j)), pl.BlockSpec((3, pair), lambda j, i: (0, j))],
        out_shape=[jax.ShapeDtypeStruct((s, 2 * D_FF), BF16), jax.ShapeDtypeStruct((1, 2 * D_FF), F32),
                   jax.ShapeDtypeStruct((3, 2 * D_FF), F32)],
        compiler_params=_params(("parallel", "arbitrary")),
    )(hpre, hpre, da, conv_w, conv_b)


def _conv_input_bwd(dh, conv_w):
    s = dh.shape[0]
    tr = min(CONV_TR, s)
    blk, _, _ = _conv_specs(tr, s)
    nblk = s // tr

    def body(x_ref, halo_ref, w_ref, o_ref):
        i = pl.program_id(1)
        halo = jnp.where(i < nblk - 1, halo_ref[...].astype(F32), 0.0)
        x, w = x_ref[...].astype(F32), w_ref[...]
        o_ref[...] = (x * w[2:3, :] + _shift_up(x, halo, 1) * w[1:2, :] + _shift_up(x, halo, 2) * w[0:1, :]).astype(BF16)

    nxt = pl.BlockSpec((16, 2 * FF_HALF), lambda j, i: (jnp.minimum((i + 1) * (tr // 16), s // 16 - 1), j))
    return pl.pallas_call(
        body, name="conv_input_bwd", grid=(2, nblk),
        in_specs=[blk, nxt, pl.BlockSpec((3, 2 * FF_HALF), lambda j, i: (0, j))], out_specs=blk,
        out_shape=jax.ShapeDtypeStruct((s, 2 * D_FF), BF16),
        compiler_params=_params(("parallel", "parallel")),
    )(dh, dh, conv_w)


def _adamw_math(w, g, m, v):
    m = ADAM_B1 * m + (1.0 - ADAM_B1) * g
    v = ADAM_B2 * v + (1.0 - ADAM_B2) * (g * g)
    m_hat = m / (1.0 - ADAM_B1 ** ADAM_STEP)
    v_hat = v / (1.0 - ADAM_B2 ** ADAM_STEP)
    delta = -ADAM_LR * (m_hat / (jnp.sqrt(v_hat) + ADAM_EPS) + ADAM_WD * w)
    return delta, m, v


def _adamw(name, g8, w, m, v):
    r, c = w.shape
    tr = _row_tile(r, c)

    def body(g_ref, w_ref, m_ref, v_ref, go_ref, d_ref, mo_ref, vo_ref):
        g = g_ref[0].astype(F32)
        for d in range(1, N_DEV):
            g = g + g_ref[d].astype(F32)
        delta, mn, vn = _adamw_math(w_ref[...], g, m_ref[...], v_ref[...])
        go_ref[...] = g
        d_ref[...] = delta
        mo_ref[...] = mn
        vo_ref[...] = vn

    spec = pl.BlockSpec((tr, c), lambda i: (i, 0))
    return pl.pallas_call(
        body, name=name, grid=(r // tr,),
        in_specs=[pl.BlockSpec((N_DEV, tr, c), lambda i: (0, i, 0)), spec, spec, spec], out_specs=[spec] * 4,
        out_shape=[jax.ShapeDtypeStruct((r, c), F32)] * 4, compiler_params=_params(("parallel",)),
    )(g8, w, m, v)


def _adamw_small(g8, ws, ms, vs):
    n = len(ws)
    offsets = [sum(w.shape[0] for w in ws[:i]) for i in range(n)]

    def body(*refs):
        g_ref, w_refs, m_refs, v_refs = refs[0], refs[1:1 + n], refs[1 + n:1 + 2 * n], refs[1 + 2 * n:1 + 3 * n]
        outs = refs[1 + 3 * n:]
        for i in range(n):
            rows = w_refs[i].shape[0]
            g = g_ref[0, offsets[i]:offsets[i] + rows, :]
            for d in range(1, N_DEV):
                g = g + g_ref[d, offsets[i]:offsets[i] + rows, :]
            delta, mn, vn = _adamw_math(w_refs[i][...], g, m_refs[i][...], v_refs[i][...])
            for k, val in enumerate((g, delta, mn, vn)):
                outs[k * n + i][...] = val

    vmem = pl.BlockSpec(memory_space=pltpu.VMEM)
    res = pl.pallas_call(
        body, name="adamw_small", in_specs=[vmem] * (1 + 3 * n), out_specs=[vmem] * (4 * n),
        out_shape=[jax.ShapeDtypeStruct(w.shape, F32) for _ in range(4) for w in ws], compiler_params=_params(),
    )(g8, *ws, *ms, *vs)
    return [res[k * n:(k + 1) * n] for k in range(4)]


def _adamw_ada(c_t, dmod, w, m, v):
    r, c = w.shape
    tr = _row_tile(r, c)

    def body(ct_ref, dm_ref, w_ref, m_ref, v_ref, go_ref, d_ref, mo_ref, vo_ref):
        ct, dm = ct_ref[...], dm_ref[...]
        g = ct[:, 0:1] * dm[0:1, :]
        for b in range(1, N_DEV):
            g = g + ct[:, b:b + 1] * dm[b:b + 1, :]
        delta, mn, vn = _adamw_math(w_ref[...], g, m_ref[...], v_ref[...])
        go_ref[...] = g
        d_ref[...] = delta
        mo_ref[...] = mn
        vo_ref[...] = vn

    spec = pl.BlockSpec((tr, c), lambda i: (i, 0))
    return pl.pallas_call(
        body, name="adamw_w_ada", grid=(r // tr,),
        in_specs=[pl.BlockSpec((tr, N_DEV), lambda i: (i, 0)), pl.BlockSpec((N_DEV, c), lambda i: (0, 0)),
                  spec, spec, spec],
        out_specs=[spec] * 4, out_shape=[jax.ShapeDtypeStruct((r, c), F32)] * 4,
        compiler_params=_params(("parallel",)),
    )(c_t, dmod, w, m, v)


def _cols_from_slots(g):
    n, r, c = g.shape
    return jnp.transpose(g, (1, 0, 2)).reshape(r, n * c)


def _cols_to_slots(w):
    r, c = w.shape
    return jnp.transpose(w.reshape(r, N_DEV, c // N_DEV), (1, 0, 2))


def _pair_cols(w):
    g0, g1 = w[..., 0:FF_HALF], w[..., FF_HALF:D_FF]
    v0, v1 = w[..., D_FF:D_FF + FF_HALF], w[..., D_FF + FF_HALF:]
    return jnp.concatenate([g0, v0, g1, v1], axis=-1)


def _unpair_cols(w):
    g0, v0 = w[..., 0:FF_HALF], w[..., FF_HALF:D_FF]
    g1, v1 = w[..., D_FF:D_FF + FF_HALF], w[..., D_FF + FF_HALF:]
    return jnp.concatenate([g0, g1, v0, v1], axis=-1)


def _row(v):
    return v.reshape(1, -1)


def kernel(x, c, w_ada, b_ada, w_in, b_forget, w_fox_proj, w_sb_proj, w_o, ln1_g, ln1_b, w_up, conv_w, conv_b, w_down, ln2_g, ln2_b, loss_target, m_w_ada, m_b_ada, m_w_in, m_b_forget, m_w_fox_proj, m_w_sb_proj, m_w_o, m_ln1_g, m_ln1_b, m_w_up, m_conv_w, m_conv_b, m_w_down, m_ln2_g, m_ln2_b, v_w_ada, v_b_ada, v_w_in, v_b_forget, v_w_fox_proj, v_w_sb_proj, v_w_o, v_ln1_g, v_ln1_b, v_w_up, v_conv_w, v_conv_b, v_w_down, v_ln2_g, v_ln2_b):
    s = x.shape[1]
    me = 4 * lax.axis_index("x") + 2 * lax.axis_index("y") + lax.axis_index("c")
    x2 = x.reshape(s, D_MODEL)
    tgt = loss_target.reshape(s, D_MODEL)

    b_ada_loc = lax.dynamic_slice(b_ada, (me * ADA_SHARD,), (ADA_SHARD,)).reshape(1, ADA_SHARD)
    c_all, mod = _mod_exchange(c, w_ada, b_ada_loc)
    mod = mod.reshape(N_MOD, 1, D_MODEL)
    sh1, sc1, gt1, sh2, sc2, gt2 = [mod[i] for i in range(N_MOD)]

    g_in = _allgather_two_level("ag_w_in", w_in.astype(BF16))
    late_weights = _Ride([w_fox_proj.astype(BF16), w_sb_proj.astype(BF16), w_o.astype(BF16), w_up.astype(BF16),
                          w_down.astype(BF16), conv_w], scatter=False)
    w_in_f = _cols_from_slots(g_in)
    w_proj = jnp.concatenate(
        [w_in_f[:, 0:1536], w_in_f[:, 1544:3080], w_in_f[:, 3080:5128], w_in_f[:, 1536:1544],
         jnp.zeros((D_MODEL, W_PROJ - 5128), BF16)], axis=1)
    w_qkv, w_gates, w_f = w_proj[:, :W_QKV], w_proj[:, W_QKV:W_QKV + W_GATES], w_proj[:, W_QKV + W_GATES:W_QKV + W_GATES + W_F]
    conv_b_p = _pair_cols(_row(conv_b))
    b_f_pad = jnp.pad(_row(b_forget), ((0, 0), (0, LANES - N_HEADS)))

    (u1, qkv), _ = _mm_pieces_nt(
        [], w_qkv, name="mm_qkv", prologue=(lambda xb, sc, sh: xb * (1.0 + sc) + sh, [x2], [sc1, sh1]),
        epilogue=lambda acc: (acc,), rows=[], vecs=[], out_dtypes=(BF16,), w_kn=True)
    gates = _mm(u1, jnp.concatenate([w_gates, w_f], axis=1), name="mm_gates_forget", tn=W_GATES + W_F)
    f_raw = gates[:, W_GATES:]
    cum_col = _forget_cumsum(f_raw, b_f_pad)
    cum_row = jnp.transpose(cum_col[:, :N_HEADS]).reshape(N_HEADS // 2, 2, s)
    (y_fox, y_fox32, lse), (g_fox, g_sb, g_o, g_up, g_down, g_cw) = _fox_fwd(qkv, cum_col, cum_row, ride=late_weights)
    w_fox_f = _cols_from_slots(g_fox)
    w_sb_f = _cols_from_slots(g_sb)
    w_o_f = g_o.reshape(D_MODEL, D_MODEL)
    w_up_p = _pair_cols(_cols_from_slots(g_up))
    w_down_f = g_down.reshape(D_FF, D_MODEL)
    conv_w_p = _pair_cols(_cols_from_slots(g_cw))
    y_sb, sb_run = _sb_fwd(qkv)
    pf = _mm(y_fox, w_fox_f, name="mm_fox_proj", out_dtype=BF16)
    ps = _mm(y_sb, w_sb_f, name="mm_sb_proj", out_dtype=BF16)
    def ln1_mod(fb, xb, gt, g, b, sc, sh):
        xhat, _ = _ln_stats(ALPHA * xb + (1.0 + gt) * fb)
        y = xhat * g + b
        return fb, y, y * (1.0 + sc) + sh

    (merged, attn_out, x1, u2), _ = _mm_pieces_nt(
        [], w_o_f, name="mm_w_o_ln1",
        prologue=(lambda ga, gb, a, b: _sigmoid(ga) * a + _sigmoid(gb) * b, [(gates, 0), (gates, 1), pf, ps], []),
        epilogue=ln1_mod, rows=[x2], vecs=[gt1, _row(ln1_g), _row(ln1_b), sc2, sh2],
        out_dtypes=(F32, F32, BF16), w_kn=True)

    hpre = _mm(u2, w_up_p, name="mm_w_up", tn=1408)
    act = _conv_gate_fwd(hpre, conv_w_p, conv_b_p)

    def ln2_bwd(fb, xb, tb, gt, g, b):
        xhat, rstd = _ln_stats(ALPHA * xb + (1.0 + gt) * fb)
        err = (xhat * g + b) - tb
        dy = err * (1.0 / D_MODEL)
        dr = _ln_bwd(dy, xhat, rstd, g)
        return (dr * (1.0 + gt), ALPHA * dr,
                _colsum(err * err), _colsum(dy * xhat), _colsum(dy), _colsum(dr * fb))

    (dffn, dx1_res, sq_err, d_ln2_g, d_ln2_b, d_gt2), _ = _mm_pieces_nt(
        [act], w_down_f, name="mm_w_down_ln2_bwd", epilogue=ln2_bwd, rows=[x1, tgt],
        vecs=[gt2, _row(ln2_g), _row(ln2_b)], out_dtypes=(BF16, F32), n_sums=4, w_kn=True)
    loss_part = jnp.broadcast_to(0.5 * jnp.sum(sq_err) / D_MODEL, (1, LANES))

    d_w_down = _mm(act, dffn, name="mm_d_w_down", ta=True, tm=1408, tk=2048, out_dtype=BF16)
    d_act = _mm(dffn, w_down_f, name="mm_d_act", tb=True, tn=1408, out_dtype=BF16)
    dh, d_conv_b_p, d_conv_w_p = _conv_gate_bwd(hpre, d_act, conv_w_p, conv_b_p)
    dhpre = _conv_input_bwd(dh, conv_w_p)
    d_w_up_p = _mm(u2, dhpre, name="mm_d_w_up", ta=True, tn=1408, tk=2048, out_dtype=BF16)
    def ln1_bwd(du, dres, x1b, xb, fb, sc, gt, g):
        dx1 = dres + du * (1.0 + sc)
        xhat, rstd = _ln_stats(ALPHA * xb + (1.0 + gt) * fb)
        dr = _ln_bwd(dx1, xhat, rstd, g)
        return (dr * (1.0 + gt), ALPHA * dr,
                _colsum(du * x1b), _colsum(du), _colsum(dx1 * xhat), _colsum(dx1), _colsum(dr * fb))

    (d_attn, dx_res, d_sc2, d_sh2, d_ln1_g, d_ln1_b, d_gt1), _ = _mm_pieces_nt(
        [dhpre], w_up_p, name="mm_d_u2_ln1_bwd", epilogue=ln1_bwd, rows=[dx1_res, x1, x2, attn_out],
        vecs=[sc2, gt1, _row(ln1_g)], out_dtypes=(BF16, F32), n_sums=5, tm=256)

    d_w_o = _mm(merged, d_attn, name="mm_d_w_o", ta=True, out_dtype=BF16)

    def merge_bwd(dm, ga, gb, a, b):
        a, b = a.astype(F32), b.astype(F32)
        sa, sb = _sigmoid(ga), _sigmoid(gb)
        return dm * a * sa * (1.0 - sa), dm * b * sb * (1.0 - sb), dm * sa, dm * sb

    (d_ga, d_gb, d_pf, d_ps), _ = _mm_pieces_nt(
        [d_attn], w_o_f, name="mm_d_merged_gate_bwd", epilogue=merge_bwd, rows=[(gates, 0), (gates, 1), pf, ps],
        vecs=[], out_dtypes=(BF16,) * 4)
    d_w_fox = _mm(y_fox, d_pf, name="mm_d_w_fox", ta=True, out_dtype=BF16)
    d_w_sb = _mm(y_sb, d_ps, name="mm_d_w_sb", ta=True, out_dtype=BF16)
    d_y_fox = _mm(d_pf, w_fox_f, name="mm_d_y_fox", tb=True, out_dtype=BF16)
    d_y_sb = _mm(d_ps, w_sb_f, name="mm_d_y_sb", tb=True, out_dtype=BF16)
    early_grads = _Ride(
        [_cols_to_slots(d_w_fox), _cols_to_slots(d_w_sb), d_w_o.reshape(N_DEV, D_MODEL // N_DEV, D_MODEL),
         _cols_to_slots(_unpair_cols(d_w_up_p)), d_w_down.reshape(N_DEV, D_FF // N_DEV, D_MODEL)], scatter=True)
    (dq_a, dk_a, dv_a, d_cum_row), early_slots = _fox_bwd(qkv, cum_col, cum_row, y_fox32, lse, d_y_fox,
                                                                    ride=early_grads)
    dq_b, dk_b, dv_b = _sb_bwd(qkv, sb_run, d_y_sb)
    d_cum = jnp.transpose(d_cum_row.reshape(N_HEADS, s))
    d_cum = jnp.pad(d_cum, ((0, 0), (0, LANES - N_HEADS)))
    d_f, d_b_forget = _forget_bwd(d_cum, f_raw, b_f_pad)
    d_qkv = [dq_a, dk_a, dv_a, dq_b, dk_b, dv_b]
    g_qkv = list(_mm_pieces_tn(u1, d_qkv, name="mm_d_w_in_qkv"))
    g_ga, g_gb, g_f = _mm_pieces_tn(u1, [d_ga, d_gb, d_f], name="mm_d_w_in_gates")
    d_w_in_f = jnp.concatenate(g_qkv[:3] + [g_f[:, :N_HEADS]] + g_qkv[3:] + [g_ga, g_gb], axis=1)
    def x_bwd(du, dres, xb, sc):
        return dres + du * (1.0 + sc), _colsum(du * xb), _colsum(du)

    (grad_x, d_sc1, d_sh1), (in_slots,) = _mm_pieces_nt(
        d_qkv + [d_ga, d_gb, d_f], w_proj, name="mm_d_u1_x_bwd", epilogue=x_bwd, rows=[dx_res, x2], vecs=[sc1],
        n_sums=2, ride=_Ride([_cols_to_slots(d_w_in_f)], scatter=True))

    d_conv_b = _unpair_cols(d_conv_b_p)
    d_conv_w = _unpair_cols(d_conv_w_p)
    cb_pad = N_MOD * D_MODEL - 2 * D_FF
    small = jnp.concatenate(
        [d_sh1, d_sc1, d_gt1, d_sh2, d_sc2, d_gt2, jnp.pad(d_b_forget, ((0, 0), (0, D_MODEL - LANES))),
         d_ln1_g, d_ln1_b, d_ln2_g, d_ln2_b, jnp.pad(d_conv_b, ((0, 0), (0, cb_pad))),
         d_conv_w.reshape(1, 6 * D_FF), loss_part], axis=1)
    n_small = small.shape[1] // LANES - 1
    small = jnp.pad(small.reshape(n_small + 1, LANES), ((0, -(n_small + 1) % 8), (0, 0)))
    (small_all,) = _exchange("ag_small_grads", [small], scatter=False)
    loss = jnp.sum(small_all[:, n_small, 0])
    n_rep = n_small - 6 * D_FF // LANES
    cw8 = small_all[:, n_rep:n_small, :].reshape(N_DEV, 3, 2 * D_FF)
    cw8 = lax.dynamic_slice(cw8, (0, 0, me * UP_SHARD), (N_DEV, 3, UP_SHARD))
    dmod8 = small_all[:, :N_MOD * D_MODEL // LANES, :].reshape(N_DEV, N_MOD * D_MODEL)
    dmod_loc = lax.dynamic_slice(dmod8, (0, me * ADA_SHARD), (N_DEV, ADA_SHARD))

    def rows_of(a):
        return a.reshape(-1, LANES)

    def forget_rows(a):
        return rows_of(jnp.pad(a, (0, D_MODEL - N_HEADS)))

    def conv_b_rows(a):
        return rows_of(jnp.pad(a, (0, cb_pad)))

    rep_sizes = {"b_ada": N_MOD * D_MODEL, "b_forget": N_HEADS, "ln1_g": D_MODEL, "ln1_b": D_MODEL, "ln2_g": D_MODEL,
                 "ln2_b": D_MODEL, "conv_b": 2 * D_FF}
    rep_w = [rows_of(b_ada), forget_rows(b_forget), rows_of(ln1_g), rows_of(ln1_b), rows_of(ln2_g), rows_of(ln2_b),
             conv_b_rows(conv_b)]
    rep_m = [rows_of(m_b_ada), forget_rows(m_b_forget), rows_of(m_ln1_g), rows_of(m_ln1_b), rows_of(m_ln2_g),
             rows_of(m_ln2_b), conv_b_rows(m_conv_b)]
    rep_v = [rows_of(v_b_ada), forget_rows(v_b_forget), rows_of(v_ln1_g), rows_of(v_ln1_b), rows_of(v_ln2_g),
             rows_of(v_ln2_b), conv_b_rows(v_conv_b)]
    rep_out = _adamw_small(small_all, rep_w, rep_m, rep_v)
    rep = [{name: a.reshape(-1)[:size] for (name, size), a in zip(rep_sizes.items(), outs)} for outs in rep_out]
    r_conv_w = _adamw("adamw_conv_w", cw8, conv_w, m_conv_w, v_conv_w)
    r_ada = _adamw_ada(jnp.transpose(c_all.reshape(N_DEV, D_MODEL)), dmod_loc, w_ada, m_w_ada, v_w_ada)

    r_in = _adamw("adamw_w_in", in_slots, w_in, m_w_in, v_w_in)
    r_fox = _adamw("adamw_w_fox", early_slots[0], w_fox_proj, m_w_fox_proj, v_w_fox_proj)
    r_sb = _adamw("adamw_w_sb", early_slots[1], w_sb_proj, m_w_sb_proj, v_w_sb_proj)
    r_o = _adamw("adamw_w_o", early_slots[2], w_o, m_w_o, v_w_o)
    r_up = _adamw("adamw_w_up", early_slots[3], w_up, m_w_up, v_w_up)
    r_down = _adamw("adamw_w_down", early_slots[4], w_down, m_w_down, v_w_down)

    def leaf(i):
        return [r_ada[i], rep[i]["b_ada"], r_in[i], rep[i]["b_forget"], r_fox[i], r_sb[i], r_o[i], rep[i]["ln1_g"],
                rep[i]["ln1_b"], r_up[i], r_conv_w[i], rep[i]["conv_b"], r_down[i], rep[i]["ln2_g"], rep[i]["ln2_b"]]

    return (loss, grad_x.reshape(1, s, D_MODEL), *leaf(0), *leaf(1), *leaf(2), *leaf(3))
```

```python
import functools

import jax
import jax.numpy as jnp
from jax import lax
from jax.experimental import pallas as pl
from jax.experimental.pallas import tpu as pltpu

F32 = jnp.float32
BF16 = jnp.bfloat16
MESH = pl.DeviceIdType.MESH
ANY = pl.BlockSpec(memory_space=pl.ANY)

N_DEV = 8
D_MODEL = 1024
HEAD_DIM = 64
N_HEADS = 8
ATTN_W = N_HEADS * HEAD_DIM
D_FF = 2816
FF_HALF = D_FF // 2
N_MOD = 6
ADA_SHARD = N_MOD * D_MODEL // N_DEV
IN_SHARD = 641
UP_SHARD = 704
ATTN_SCALE = HEAD_DIM ** -0.5
ALPHA = 2.0 ** 0.25
LN_EPS = 1e-5
LANES = 128
TQ = 512
SCAN_W = 256
VMEM_LIMIT = 56 * 1024 * 1024

ADAM_LR, ADAM_B1, ADAM_B2, ADAM_EPS, ADAM_WD, ADAM_STEP = 0.001, 0.9, 0.999, 1e-08, 0.01, 10

W_QKV, W_GATES, W_F = 3072, 2048, 128
W_PROJ = 5376


def _params(sem=None):
    return pltpu.CompilerParams(dimension_semantics=sem, vmem_limit_bytes=VMEM_LIMIT)


def _tile(n, cap):
    if n <= cap:
        return n
    best = None
    for t in range(LANES, cap + 1, LANES):
        if n % t == 0:
            best = t
    assert best is not None, (n, cap)
    return best


def _row_tile(r, width, budget=192 * 1024):
    if r * width <= budget or r % 16:
        return r
    best = 16
    for t in range(16, r + 1, 16):
        if r % t == 0 and t * width <= budget:
            best = t
    return best


def _me():
    x, y, c = lax.axis_index("x"), lax.axis_index("y"), lax.axis_index("c")
    return x, y, c, 4 * x + 2 * y + c


def _peer(r):
    x, y, c, _ = _me()
    px = 1 - x if r & 4 else x
    py = 1 - y if r & 2 else y
    pc = 1 - c if r & 1 else c
    return (px, py, pc), 4 * px + 2 * py + pc


class _Ride:
    def __init__(self, arrays, scatter):
        self.arrays, self.scatter, self.n = list(arrays), scatter, len(arrays)
        self.in_specs = [ANY] * self.n
        self.out_specs = [ANY] * self.n
        self.out_shape = [jax.ShapeDtypeStruct(a.shape if scatter else (N_DEV,) + a.shape, a.dtype) for a in arrays]
        self.scratch = [pltpu.SemaphoreType.DMA((self.n, N_DEV - 1)), pltpu.SemaphoreType.DMA((self.n, N_DEV - 1)),
                        pltpu.SemaphoreType.DMA((self.n,))]

    def _local(self, ins, outs, sems, a):
        me = _me()[3]
        return pltpu.make_async_copy(ins[a].at[me] if self.scatter else ins[a], outs[a].at[me], sems[2].at[a])

    def _remote(self, ins, outs, sems, a, r, arriving):
        me = _me()[3]
        peer, pidx = _peer(r)
        src = ins[a].at[me if arriving else pidx] if self.scatter else ins[a]
        return pltpu.make_async_remote_copy(
            src_ref=src, dst_ref=outs[a].at[pidx if arriving else me], send_sem=sems[0].at[a, r - 1],
            recv_sem=sems[1].at[a, r - 1], device_id=peer, device_id_type=MESH)

    def start(self, ins, outs, sems):
        for a in range(self.n):
            self._local(ins, outs, sems, a).start()
        for r in range(1, N_DEV):
            for a in range(self.n):
                self._remote(ins, outs, sems, a, r, False).start()

    def wait(self, ins, outs, sems):
        for r in range(1, N_DEV):
            for a in range(self.n):
                self._remote(ins, outs, sems, a, r, True).wait_recv()
        for r in range(1, N_DEV):
            for a in range(self.n):
                self._remote(ins, outs, sems, a, r, False).wait_send()
        for a in range(self.n):
            self._local(ins, outs, sems, a).wait()


def _exchange(name, arrays, scatter):
    ride = _Ride(arrays, scatter)

    def body(*refs):
        ins, outs, sems = refs[:ride.n], refs[ride.n:2 * ride.n], refs[2 * ride.n:]
        ride.start(ins, outs, sems)
        ride.wait(ins, outs, sems)

    return pl.pallas_call(body, name=name, in_specs=ride.in_specs, out_specs=ride.out_specs, out_shape=ride.out_shape,
                          scratch_shapes=ride.scratch)(*arrays)


def _allgather_two_level(name, a):
    def body(a_ref, out_ref, send_sems, recv_sems, local_sem):
        x, y, c, me = _me()
        sibling = (x, y, 1 - c)
        chips = [(1 - x, y), (x, 1 - y), (1 - x, 1 - y)]

        def idx(px, py, pc):
            return 4 * px + 2 * py + pc

        def copy(k, block, to, src=None):
            slot = out_ref.at[idx(*block)]
            return pltpu.make_async_remote_copy(
                src_ref=slot if src is None else src, dst_ref=slot, send_sem=send_sems.at[k], recv_sem=recv_sems.at[k],
                device_id=to, device_id_type=MESH)

        mine = pltpu.make_async_copy(a_ref, out_ref.at[me], local_sem)
        mine.start()
        first = [copy(0, (x, y, c), sibling, src=a_ref)]
        first += [copy(1 + j, (x, y, c), (*chip, c), src=a_ref) for j, chip in enumerate(chips)]
        for cp in first:
            cp.start()
        passed = [copy(4 + j, (*chip, c), sibling) for j, chip in enumerate(chips)]
        for j, chip in enumerate(chips):
            copy(1 + j, (*chip, c), (x, y, c)).wait_recv()
            passed[j].start()
        copy(0, sibling, (x, y, c)).wait_recv()
        for j, chip in enumerate(chips):
            copy(4 + j, (*chip, 1 - c), (x, y, c)).wait_recv()
        for cp in first + passed:
            cp.wait_send()
        mine.wait()

    return pl.pallas_call(
        body, name=name, in_specs=[ANY], out_specs=ANY,
        out_shape=jax.ShapeDtypeStruct((N_DEV,) + a.shape, a.dtype),
        scratch_shapes=[pltpu.SemaphoreType.DMA((N_DEV - 1,)), pltpu.SemaphoreType.DMA((N_DEV - 1,)),
                        pltpu.SemaphoreType.DMA],
    )(a)


def _with_ride(body, ride, n_in, n_out, grid):
    if ride is None:
        return body
    n = ride.n

    def wrapped(*refs):
        ins, rins = refs[:n_in], refs[n_in:n_in + n]
        outs, routs = refs[n_in + n:n_in + n + n_out], refs[n_in + n + n_out:n_in + 2 * n + n_out]
        rest = refs[n_in + 2 * n + n_out:]
        scratch, sems = rest[:len(rest) - 3], rest[len(rest) - 3:]
        ids = [pl.program_id(d) for d in range(len(grid))]
        first = functools.reduce(lambda p, q: p & q, [i == 0 for i in ids])
        last = functools.reduce(lambda p, q: p & q, [i == g - 1 for i, g in zip(ids, grid)])

        @pl.when(first)
        def _():
            ride.start(rins, routs, sems)

        body(*ins, *outs, *scratch)

        @pl.when(last)
        def _():
            ride.wait(rins, routs, sems)

    return wrapped


def _ride_call(body, ride, *, name, grid, in_specs, out_specs, out_shape, scratch_shapes, sem, args):
    n_in, n_out = len(in_specs), len(out_specs)
    if ride is None:
        res = pl.pallas_call(body, name=name, grid=grid, in_specs=in_specs, out_specs=out_specs, out_shape=out_shape,
                             scratch_shapes=scratch_shapes, compiler_params=_params(sem))(*args)
        return list(res), []
    res = pl.pallas_call(
        _with_ride(body, ride, n_in, n_out, grid), name=name, grid=grid,
        in_specs=list(in_specs) + ride.in_specs, out_specs=list(out_specs) + ride.out_specs,
        out_shape=list(out_shape) + ride.out_shape, scratch_shapes=list(scratch_shapes) + ride.scratch,
        compiler_params=_params(("arbitrary",) * len(grid)))(*args, *ride.arrays)
    return list(res[:n_out]), list(res[n_out:])


def _mm(a, b, *, name, ta=False, tb=False, out_dtype=F32, tm=1024, tn=1024, tk=1024, ride=None):
    m, k = (a.shape[1], a.shape[0]) if ta else a.shape
    n = b.shape[0] if tb else b.shape[1]
    assert (b.shape[1] if tb else b.shape[0]) == k
    tm, tn, tk = _tile(m, tm), _tile(n, tn), _tile(k, tk)
    nk = k // tk
    a_spec = pl.BlockSpec((tk, tm), lambda i, j, l: (l, i)) if ta else pl.BlockSpec((tm, tk), lambda i, j, l: (i, l))
    b_spec = pl.BlockSpec((tn, tk), lambda i, j, l: (j, l)) if tb else pl.BlockSpec((tk, tn), lambda i, j, l: (l, j))
    dims = (((0,) if ta else (1,), (1,) if tb else (0,)), ((), ()))

    def body(a_ref, b_ref, o_ref, *acc):
        p = lax.dot_general(a_ref[...].astype(BF16), b_ref[...].astype(BF16), dims, preferred_element_type=F32)
        if nk == 1:
            o_ref[...] = p.astype(out_dtype)
            return
        acc_ref = acc[0]
        step = pl.program_id(2)

        @pl.when(step == 0)
        def _():
            acc_ref[...] = p

        @pl.when(step > 0)
        def _():
            acc_ref[...] += p

        @pl.when(step == nk - 1)
        def _():
            o_ref[...] = acc_ref[...].astype(out_dtype)

    outs, rode = _ride_call(
        body, ride, name=name, grid=(m // tm, n // tn, nk), in_specs=[a_spec, b_spec],
        out_specs=[pl.BlockSpec((tm, tn), lambda i, j, l: (i, j))], out_shape=[jax.ShapeDtypeStruct((m, n), out_dtype)],
        scratch_shapes=[] if nk == 1 else [pltpu.VMEM((tm, tn), F32)], sem=("parallel", "parallel", "arbitrary"),
        args=(a, b))
    return outs[0] if ride is None else (outs[0], rode)


def _mm_pieces_nt(pieces, w, *, name, epilogue, rows, vecs, out_dtypes=(F32,), n_sums=0, tm=512, w_kn=False, ride=None,
                  prologue=None):
    if prologue is not None:
        assert not pieces
        pro_fn, prows, pvecs = prologue
        prows = [r if isinstance(r, tuple) else (r, 0) for r in prows]
        s, widths = prows[0][0].shape[0], [w.shape[0] if w_kn else w.shape[1]]
        lead_args = [r for r, _ in prows] + list(pvecs)
        lead_specs = [pl.BlockSpec((min(tm, s), widths[0]), functools.partial(lambda i, cb: (i, cb), cb=cb))
                      for _, cb in prows] + [pl.BlockSpec(v.shape, lambda i: (0, 0)) for v in pvecs]
        out_dtypes = (BF16,) + tuple(out_dtypes)
    else:
        s, widths = pieces[0].shape[0], [p.shape[1] for p in pieces]
        lead_args = list(pieces)
        lead_specs = [pl.BlockSpec((min(tm, s), width), lambda i: (i, 0)) for width in widths]
    n = w.shape[1] if w_kn else w.shape[0]
    tm = min(tm, s)
    offs = [sum(widths[:i]) for i in range(len(widths))]
    rows = [r if isinstance(r, tuple) else (r, 0) for r in rows]
    n_p, n_r, n_v, n_o = len(lead_args), len(rows), len(vecs), len(out_dtypes)

    def body(*refs):
        p_refs, w_ref = refs[:n_p], refs[n_p]
        extra = [r[...] for r in refs[n_p + 1:n_p + 1 + n_r + n_v]]
        o_refs = refs[n_p + 1 + n_r + n_v:n_p + 1 + n_r + n_v + n_o]
        sum_refs = refs[n_p + 1 + n_r + n_v + n_o:]
        if prologue is not None:
            made = pro_fn(*[r[...] for r in p_refs]).astype(BF16)
            operands = [made]
        else:
            operands = [r[...] for r in p_refs]
        acc = None
        for operand, off, width in zip(operands, offs, widths):
            if w_kn:
                part = jnp.dot(operand, w_ref[off:off + width, :], preferred_element_type=F32)
            else:
                part = _dot_nt(operand, w_ref[:, off:off + width])
            acc = part if acc is None else acc + part
        res = epilogue(acc, *extra)
        if prologue is not None:
            res = (made,) + tuple(res)
        for o_ref, val in zip(o_refs, res[:n_o]):
            o_ref[...] = val.astype(o_ref.dtype)
        step = pl.program_id(0)
        for s_ref, val in zip(sum_refs, res[n_o:]):
            @pl.when(step == 0)
            def _(s_ref=s_ref, val=val):
                s_ref[...] = val

            @pl.when(step > 0)
            def _(s_ref=s_ref, val=val):
                s_ref[...] += val

    vec_spec = pl.BlockSpec((1, n), lambda i: (0, 0))
    out_widths = ([widths[0]] if prologue is not None else []) + [n] * (n_o - (prologue is not None))
    return _ride_call(
        body, ride, name=name, grid=(s // tm,),
        in_specs=lead_specs + [pl.BlockSpec(w.shape, lambda i: (0, 0))]
        + [pl.BlockSpec((tm, n), functools.partial(lambda i, cb: (i, cb), cb=cb)) for _, cb in rows] + [vec_spec] * n_v,
        out_specs=[pl.BlockSpec((tm, ow), lambda i: (i, 0)) for ow in out_widths] + [vec_spec] * n_sums,
        out_shape=[jax.ShapeDtypeStruct((s, ow), dt) for ow, dt in zip(out_widths, out_dtypes)]
        + [jax.ShapeDtypeStruct((1, n), F32)] * n_sums,
        scratch_shapes=[], sem=("arbitrary",) if n_sums else ("parallel",),
        args=(*lead_args, w, *[r for r, _ in rows], *vecs))


def _mm_pieces_tn(a, pieces, *, name, tk=1024):
    s, m = a.shape
    tk = min(tk, s)
    nk = s // tk
    n = len(pieces)
    dims = (((0,), (0,)), ((), ()))

    def body(*refs):
        a_ref, p_refs, o_refs, accs = refs[0], refs[1:1 + n], refs[1 + n:1 + 2 * n], refs[1 + 2 * n:]
        step = pl.program_id(0)
        a_blk = a_ref[...]
        for p_ref, o_ref, acc in zip(p_refs, o_refs, accs):
            part = lax.dot_general(a_blk, p_ref[...], dims, preferred_element_type=F32)

            @pl.when(step == 0)
            def _(acc=acc, part=part):
                acc[...] = part

            @pl.when(step > 0)
            def _(acc=acc, part=part):
                acc[...] += part

            @pl.when(step == nk - 1)
            def _(acc=acc, o_ref=o_ref):
                o_ref[...] = acc[...].astype(BF16)

    return pl.pallas_call(
        body, name=name, grid=(nk,),
        in_specs=[pl.BlockSpec((tk, m), lambda l: (l, 0))] + [pl.BlockSpec((tk, p.shape[1]), lambda l: (l, 0)) for p in pieces],
        out_specs=[pl.BlockSpec((m, p.shape[1]), lambda l: (0, 0)) for p in pieces],
        out_shape=[jax.ShapeDtypeStruct((m, p.shape[1]), BF16) for p in pieces],
        scratch_shapes=[pltpu.VMEM((m, p.shape[1]), F32) for p in pieces],
        compiler_params=_params(("arbitrary",)),
    )(a, *pieces)


def _rowwise(name, fn, rows, vecs, outs, sums=(), tr=512):
    s = rows[0][0].shape[0]
    tr = min(tr, s)
    nr, nv, no = len(rows), len(vecs), len(outs)

    def body(*refs):
        vals = [r[...] for r in refs[:nr + nv]]
        res = fn(*vals)
        for o_ref, val in zip(refs[nr + nv:nr + nv + no], res[:no]):
            o_ref[...] = val.astype(o_ref.dtype)
        step = pl.program_id(0)
        for s_ref, val in zip(refs[nr + nv + no:], res[no:]):
            @pl.when(step == 0)
            def _(s_ref=s_ref, val=val):
                s_ref[...] = val

            @pl.when(step > 0)
            def _(s_ref=s_ref, val=val):
                s_ref[...] += val

    in_specs = [pl.BlockSpec((tr, w), functools.partial(lambda i, cb: (i, cb), cb=cb)) for _, w, cb in rows]
    in_specs += [pl.BlockSpec(v.shape, lambda i: (0, 0)) for v in vecs]
    out_specs = [pl.BlockSpec((tr, w), lambda i: (i, 0)) for w, _ in outs]
    out_specs += [pl.BlockSpec((1, w), lambda i: (0, 0)) for w in sums]
    out_shape = [jax.ShapeDtypeStruct((s, w), dt) for w, dt in outs]
    out_shape += [jax.ShapeDtypeStruct((1, w), F32) for w in sums]
    return pl.pallas_call(
        body, name=name, grid=(s // tr,), in_specs=in_specs, out_specs=out_specs, out_shape=out_shape,
        compiler_params=_params(("arbitrary",) if sums else ("parallel",)),
    )(*[r[0] for r in rows], *vecs)


def _colsum(x):
    return jnp.sum(x, axis=0, keepdims=True)


def _sigmoid(x):
    return 1.0 / (1.0 + jnp.exp(-x))


def _log_sigmoid(x):
    return jnp.minimum(x, 0.0) - jnp.log(1.0 + jnp.exp(-jnp.abs(x)))


def _ln_stats(r):
    mu = jnp.mean(r, axis=-1, keepdims=True)
    xc = r - mu
    var = jnp.mean(xc * xc, axis=-1, keepdims=True)
    rstd = lax.rsqrt(var + LN_EPS)
    return xc * rstd, rstd


def _ln_bwd(dy, xhat, rstd, g):
    dxh = dy * g
    m1 = jnp.mean(dxh, axis=-1, keepdims=True)
    m2 = jnp.mean(dxh * xhat, axis=-1, keepdims=True)
    return rstd * (dxh - m1 - xhat * m2)


def _mod_exchange(c_row, w_ada, b_ada_loc):
    def body(c_ref, w_ref, b_ref, call_ref, mod_ref, piece_ref, send_sems, recv_sems):
        me = _me()[3]
        call_ref[me] = c_ref[...]
        sent = []
        for r in range(1, N_DEV):
            peer, _ = _peer(r)
            cp = pltpu.make_async_remote_copy(
                src_ref=c_ref, dst_ref=call_ref.at[me], send_sem=send_sems.at[0, r - 1],
                recv_sem=recv_sems.at[0, r - 1], device_id=peer, device_id_type=MESH)
            cp.start()
            sent.append(cp)
        for r in range(1, N_DEV):
            peer, pidx = _peer(r)
            pltpu.make_async_remote_copy(
                src_ref=c_ref, dst_ref=call_ref.at[pidx], send_sem=send_sems.at[0, r - 1],
                recv_sem=recv_sems.at[0, r - 1], device_id=peer, device_id_type=MESH).wait_recv()
        c_all = jnp.concatenate([call_ref[d] for d in range(N_DEV)], axis=0)
        mod_loc = jnp.dot(c_all, w_ref[...], preferred_element_type=F32,
                          precision=lax.Precision.HIGHEST) + b_ref[...]
        for d in range(N_DEV):
            piece_ref[d] = mod_loc[d:d + 1, :]
        mod_ref[me] = piece_ref[me]
        for r in range(1, N_DEV):
            peer, pidx = _peer(r)
            cp = pltpu.make_async_remote_copy(
                src_ref=piece_ref.at[pidx], dst_ref=mod_ref.at[me], send_sem=send_sems.at[1, r - 1],
                recv_sem=recv_sems.at[1, r - 1], device_id=peer, device_id_type=MESH)
            cp.start()
            sent.append(cp)
        for r in range(1, N_DEV):
            peer, pidx = _peer(r)
            pltpu.make_async_remote_copy(
                src_ref=piece_ref.at[me], dst_ref=mod_ref.at[pidx], send_sem=send_sems.at[1, r - 1],
                recv_sem=recv_sems.at[1, r - 1], device_id=peer, device_id_type=MESH).wait_recv()
        for cp in sent:
            cp.wait_send()

    vmem = pl.BlockSpec(memory_space=pltpu.VMEM)
    return pl.pallas_call(
        body, name="mod_exchange", in_specs=[vmem, vmem, vmem], out_specs=[vmem, vmem],
        out_shape=[jax.ShapeDtypeStruct((N_DEV, 1, D_MODEL), F32), jax.ShapeDtypeStruct((N_DEV, 1, ADA_SHARD), F32)],
        scratch_shapes=[pltpu.VMEM((N_DEV, 1, ADA_SHARD), F32),
                        pltpu.SemaphoreType.DMA((2, N_DEV - 1)), pltpu.SemaphoreType.DMA((2, N_DEV - 1))],
        compiler_params=_params(),
    )(c_row, w_ada, b_ada_loc)


def _split3(x):
    hi = x.astype(BF16)
    r1 = x - hi.astype(F32)
    mid = r1.astype(BF16)
    lo = (r1 - mid.astype(F32)).astype(BF16)
    return hi, mid, lo


def _scan_rows(x_ref, o_ref, s, reverse, pre=None, post=None):
    tb = min(TQ, s)
    nb = s // tb
    row = lax.broadcasted_iota(jnp.int32, (tb, tb), 0)
    col = lax.broadcasted_iota(jnp.int32, (tb, tb), 1)
    tri = jnp.where((col >= row) if reverse else (col <= row), 1.0, 0.0).astype(BF16)

    def step(i, carry):
        blk = (nb - 1 - i) if reverse else i
        off = pl.multiple_of(blk * tb, tb)
        x = x_ref[pl.ds(off, tb), :]
        if pre is not None:
            x = pre(x, off)
        acc = carry
        for piece in _split3(x):
            acc = acc + jnp.dot(tri, piece, preferred_element_type=F32)
        o_ref[pl.ds(off, tb), :] = acc if post is None else post(acc, off)
        edge = acc[0:1, :] if reverse else acc[tb - 1:tb, :]
        return jnp.broadcast_to(edge, (tb, LANES))

    lax.fori_loop(0, nb, step, jnp.zeros((tb, LANES), F32))


def _forget_cumsum(f_raw, b_pad):
    s = f_raw.shape[0]

    def body(f_ref, b_ref, cum_ref):
        b = b_ref[...]
        _scan_rows(f_ref, cum_ref, s, False, pre=lambda x, off: _log_sigmoid(x + b))

    vmem = pl.BlockSpec(memory_space=pltpu.VMEM)
    return pl.pallas_call(body, name="forget_cumsum", in_specs=[vmem, vmem], out_specs=vmem,
                          out_shape=jax.ShapeDtypeStruct((s, LANES), F32), compiler_params=_params())(f_raw, b_pad)


def _forget_bwd(dcum, f_raw, b_pad):
    s = f_raw.shape[0]

    def body(d_ref, f_ref, b_ref, df_ref, db_ref, tmp_ref):
        b = b_ref[...]
        _scan_rows(d_ref, tmp_ref, s, True)
        df = tmp_ref[...] * _sigmoid(-(f_ref[...] + b))
        df_ref[...] = df.astype(BF16)
        db_ref[...] = _colsum(df)

    vmem = pl.BlockSpec(memory_space=pltpu.VMEM)
    return pl.pallas_call(
        body, name="forget_bwd", in_specs=[vmem, vmem, vmem], out_specs=[vmem, vmem],
        out_shape=[jax.ShapeDtypeStruct((s, LANES), BF16), jax.ShapeDtypeStruct((1, LANES), F32)],
        scratch_shapes=[pltpu.VMEM((s, LANES), F32)], compiler_params=_params())(dcum, f_raw, b_pad)


def _dot_nt(a, b):
    return lax.dot_general(a, b, (((1,), (1,)), ((), ())), preferred_element_type=F32)


def _head_masks():
    lane = lax.broadcasted_iota(jnp.int32, (TQ, LANES), 1)
    return lane, [lane < HEAD_DIM, lane >= HEAD_DIM]


def _pick(mask, x):
    return jnp.where(mask, x, jnp.zeros_like(x))


def _qkv_specs(s, col0):
    nb = ATTN_W // LANES
    return [pl.BlockSpec((TQ, LANES), lambda hp, qi: (qi, col0 + hp)),
            pl.BlockSpec((s, LANES), lambda hp, qi: (0, col0 + nb + hp)),
            pl.BlockSpec((s, LANES), lambda hp, qi: (0, col0 + 2 * nb + hp))]


def _pair_spec():
    return pl.BlockSpec((TQ, LANES), lambda hp, qi: (qi, hp))


def _diag_mask(width, strict):
    row = lax.broadcasted_iota(jnp.int32, (TQ, width * TQ), 0) + (width - 1) * TQ
    col = lax.broadcasted_iota(jnp.int32, (TQ, width * TQ), 1)
    return (col < row) if strict else (col <= row)


def _fox_fwd(qkv, cum_col, cum_row, ride=None):
    s = qkv.shape[0]
    nq = s // TQ

    def body(q_ref, k_ref, v_ref, cc_ref, cr_ref, o_ref, o32_ref, lse_ref):
        hp, qi = pl.program_id(0), pl.program_id(1)
        lane, masks = _head_masks()
        q2 = q_ref[...] * jnp.asarray(ATTN_SCALE, BF16)
        cc = cc_ref[...]
        qms = [_pick(masks[e], q2) for e in range(2)]
        cqs = [jnp.sum(jnp.where(lane == 2 * hp + e, cc, 0.0), axis=1, keepdims=True) for e in range(2)]

        def tile(kb, carry, masked, width=1):
            off, span = pl.multiple_of(kb * TQ, TQ), width * TQ
            k2, v2 = k_ref[pl.ds(off, span), :], v_ref[pl.ds(off, span), :]
            head0 = lax.broadcasted_iota(jnp.int32, (span, LANES), 1) < HEAD_DIM
            new = []
            for e in range(2):
                m, acc = carry[e]
                sc = _dot_nt(qms[e], k2) + (cqs[e] - cr_ref[e:e + 1, pl.ds(off, span)])
                if masked:
                    sc = jnp.where(_diag_mask(width, False), sc, -jnp.inf)
                m_new = jnp.maximum(m, jnp.max(sc, axis=1, keepdims=True))
                p = jnp.exp(sc - m_new)
                v_ones = jnp.where(head0 if e == 0 else ~head0, v2, jnp.ones_like(v2))
                acc = jnp.exp(m - m_new) * acc + jnp.dot(p.astype(BF16), v_ones, preferred_element_type=F32)
                new.append((m_new, acc))
            return tuple(new)

        init = (jnp.full((TQ, 1), -jnp.inf, F32), jnp.zeros((TQ, LANES), F32))
        carry = lax.cond(qi % 2 == 1, lambda cr: tile(qi - 1, cr, True, 2), lambda cr: tile(qi, cr, True), (init, init))
        quads = qi // 4
        carry = lax.fori_loop(0, quads, lambda j, cr: tile(4 * j, cr, False, 4), carry)
        carry = lax.cond(qi % 4 >= 2, lambda cr: tile(4 * quads, cr, False, 2), lambda cr: cr, carry)
        sums = [jnp.max(jnp.where(masks[1 - e], carry[e][1], 0.0), axis=1, keepdims=True) for e in range(2)]
        outs = [carry[e][1] / sums[e] for e in range(2)]
        lses = [carry[e][0] + jnp.log(sums[e]) for e in range(2)]
        out = jnp.where(masks[0], outs[0], outs[1])
        o_ref[...] = out.astype(BF16)
        o32_ref[...] = out
        lse_ref[...] = jnp.where(masks[0], lses[0], lses[1])

    return _ride_call(
        body, ride, name="fox_fwd", grid=(N_HEADS // 2, nq),
        in_specs=_qkv_specs(s, 0) + [pl.BlockSpec((TQ, LANES), lambda hp, qi: (qi, 0)),
                                     pl.BlockSpec((None, 2, s), lambda hp, qi: (hp, 0, 0))],
        out_specs=[_pair_spec(), _pair_spec(), _pair_spec()],
        out_shape=[jax.ShapeDtypeStruct((s, ATTN_W), BF16), jax.ShapeDtypeStruct((s, ATTN_W), F32),
                   jax.ShapeDtypeStruct((s, ATTN_W), F32)],
        scratch_shapes=[], sem=("parallel", "parallel"), args=(qkv, qkv, qkv, cum_col, cum_row))


def _write_transposed(acc_ref, out_ref):
    for c in range(out_ref.shape[0] // TQ):
        out_ref[c * TQ:(c + 1) * TQ, :] = jnp.transpose(acc_ref[:, c * TQ:(c + 1) * TQ]).astype(BF16)


def _fox_bwd(qkv, cum_col, cum_row, o, lse, do, ride=None):
    s = qkv.shape[0]
    nq = s // TQ

    def body(q_ref, k_ref, v_ref, cc_ref, cr_ref, o_ref, lse_ref, do_ref,
             dq_ref, dk_ref, dv_ref, dcr_ref, dk_acc, dv_acc):
        hp, qi = pl.program_id(0), pl.program_id(1)

        @pl.when(qi == 0)
        def _():
            dk_acc[...] = jnp.zeros_like(dk_acc)
            dv_acc[...] = jnp.zeros_like(dv_acc)
            dcr_ref[...] = jnp.zeros_like(dcr_ref)

        lane, masks = _head_masks()
        q2 = q_ref[...] * jnp.asarray(ATTN_SCALE, BF16)
        do2 = do_ref[...]
        prod = do2.astype(F32) * o_ref[...].astype(F32)
        lse2 = lse_ref[...]
        cc = cc_ref[...]
        qms = [_pick(masks[e], q2) for e in range(2)]
        doms = [_pick(masks[e], do2) for e in range(2)]
        deltas = [jnp.sum(jnp.where(masks[e], prod, 0.0), axis=1, keepdims=True) for e in range(2)]
        lses = [jnp.max(jnp.where(masks[e], lse2, -jnp.inf), axis=1, keepdims=True) for e in range(2)]
        cqs = [jnp.sum(jnp.where(lane == 2 * hp + e, cc, 0.0), axis=1, keepdims=True) for e in range(2)]
        qts = [jnp.transpose(qms[e].astype(F32)).astype(BF16) for e in range(2)]
        dots = [jnp.transpose(doms[e].astype(F32)).astype(BF16) for e in range(2)]
        ones_row = [HEAD_DIM * (1 - e) for e in range(2)]
        trow = lax.broadcasted_iota(jnp.int32, (LANES, TQ), 0)
        qts = [jnp.where(trow == ones_row[e], jnp.ones_like(qts[e]), qts[e]) for e in range(2)]

        def tile(kb, carry, masked, width=1):
            off, span = pl.multiple_of(kb * TQ, TQ), width * TQ
            k2, v2 = k_ref[pl.ds(off, span), :], v_ref[pl.ds(off, span), :]
            head0 = lax.broadcasted_iota(jnp.int32, (span, LANES), 1) < HEAD_DIM
            new, dks, dv = [], [], None
            for e in range(2):
                dq = carry[e]
                sc = _dot_nt(qms[e], k2) + (cqs[e] - cr_ref[e:e + 1, pl.ds(off, span)])
                p = jnp.exp(sc - lses[e])
                if masked:
                    p = jnp.where(_diag_mask(width, False), p, 0.0)
                ds = p * (_dot_nt(doms[e], v2) - deltas[e])
                dsb = ds.astype(BF16)
                dk_e = jnp.dot(qts[e], dsb, preferred_element_type=F32)
                dv_e = jnp.dot(dots[e], p.astype(BF16), preferred_element_type=F32)
                dks.append(dk_e)
                dv = dv_e if e == 0 else dv + dv_e
                dcr_ref[e:e + 1, pl.ds(off, span)] -= dk_e[ones_row[e]:ones_row[e] + 1, :]
                k_ones = jnp.where(head0 if e == 0 else ~head0, k2, jnp.ones_like(k2))
                new.append(dq + jnp.dot(dsb, k_ones, preferred_element_type=F32))
            krow = lax.broadcasted_iota(jnp.int32, (LANES, span), 0)
            dk_acc[:, pl.ds(off, span)] += jnp.where(krow < HEAD_DIM, dks[0], dks[1])
            dv_acc[:, pl.ds(off, span)] += dv
            return tuple(new)

        init = jnp.zeros((TQ, LANES), F32)
        carry = lax.fori_loop(0, qi // 2, lambda j, cr: tile(2 * j, cr, False, 2), (init, init))
        carry = lax.cond(qi % 2 == 1, lambda cr: tile(qi - 1, cr, True, 2), lambda cr: tile(qi, cr, True), carry)
        dq_ref[...] = (jnp.where(masks[0], carry[0], carry[1]) * ATTN_SCALE).astype(BF16)
        for e in range(2):
            dcr_ref[e:e + 1, pl.ds(pl.multiple_of(qi * TQ, TQ), TQ)] += jnp.transpose(carry[e])[
                ones_row[e]:ones_row[e] + 1, :]

        @pl.when(qi == nq - 1)
        def _():
            _write_transposed(dk_acc, dk_ref)
            _write_transposed(dv_acc, dv_ref)

    seq_spec = pl.BlockSpec((s, LANES), lambda hp, qi: (0, hp))
    return _ride_call(
        body, ride, name="fox_bwd", grid=(N_HEADS // 2, nq),
        in_specs=_qkv_specs(s, 0) + [pl.BlockSpec((TQ, LANES), lambda hp, qi: (qi, 0)),
                                     pl.BlockSpec((None, 2, s), lambda hp, qi: (hp, 0, 0)),
                                     _pair_spec(), _pair_spec(), _pair_spec()],
        out_specs=[_pair_spec(), seq_spec, seq_spec, pl.BlockSpec((None, 2, s), lambda hp, qi: (hp, 0, 0))],
        out_shape=[jax.ShapeDtypeStruct((s, ATTN_W), BF16)] * 3 + [jax.ShapeDtypeStruct((N_HEADS // 2, 2, s), F32)],
        scratch_shapes=[pltpu.VMEM((LANES, s), F32), pltpu.VMEM((LANES, s), F32)],
        sem=("parallel", "arbitrary"), args=(qkv, qkv, qkv, cum_col, cum_row, o, lse, do))


def _scan_matrix(reverse):
    row = lax.broadcasted_iota(jnp.int32, (SCAN_W, SCAN_W), 0)
    col = lax.broadcasted_iota(jnp.int32, (SCAN_W, SCAN_W), 1)
    return jnp.where((row > col) if reverse else (row < col), 1.0, 0.0).astype(BF16)


def _scan_cols(x, tri, reverse, init):
    nblk = x.shape[1] // SCAN_W
    parts, total = [None] * nblk, init
    far = 0 if reverse else SCAN_W - 1
    for b in (reversed(range(nblk)) if reverse else range(nblk)):
        blk = x[:, b * SCAN_W:(b + 1) * SCAN_W]
        part = jnp.dot(blk.astype(BF16), tri, preferred_element_type=F32)
        parts[b] = part + total
        total = total + (part[:, far:far + 1] + blk[:, far:far + 1])
    return (parts[0] if nblk == 1 else jnp.concatenate(parts, axis=1)), total


def _sb_logits(qm, k2):
    z = _dot_nt(qm, k2)
    neg_abs = lax.bitcast_convert_type(lax.bitcast_convert_type(z, jnp.uint32) | jnp.uint32(0x80000000), F32)
    soft = jnp.log(1.0 + jnp.exp(neg_abs))
    lb = jnp.minimum(z, 0.0) - soft
    return lb, lb - z


TILE_SLOTS = 4


def _tri_base(qi):
    return (qi * (qi + 1)) // 2


def _sb_fwd(qkv):
    s = qkv.shape[0]
    nq = s // TQ

    def body(q_ref, k_ref, v_ref, o_ref, t_ref, buf, sems):
        hp, qi = pl.program_id(0), pl.program_id(1)
        _, masks = _head_masks()
        suffix = _scan_matrix(True)
        q2 = q_ref[...] * jnp.asarray(ATTN_SCALE, BF16)
        qms = [_pick(masks[e], q2) for e in range(2)]
        base = _tri_base(qi)

        def store(e, kb):
            slot = kb % TILE_SLOTS
            return pltpu.make_async_copy(buf.at[e, slot], t_ref.at[2 * hp + e, base + kb], sems.at[e, slot])

        def tile(kb, carry, masked, width=1):
            off, span = pl.multiple_of(kb * TQ, TQ), width * TQ
            k2, v2 = k_ref[pl.ds(off, span), :], v_ref[pl.ds(off, span), :]
            new, stored = [], []
            for e in range(2):
                run, acc = carry[e]
                lb, lo = _sb_logits(qms[e], k2)
                if masked:
                    strict = _diag_mask(width, True)
                    lo = jnp.where(strict, lo, 0.0)
                rest, run = _scan_cols(lo, suffix, True, run)
                a = jnp.exp(lb + rest)
                if masked:
                    a = jnp.where(strict, a, 0.0)
                ab, lbb = a.astype(BF16), lb.astype(BF16)
                acc = acc + jnp.dot(ab, v2, preferred_element_type=F32)
                stored.append((ab, lbb))
                new.append((run, acc))
            for e in range(2):
                ab, lbb = stored[e]
                for w in range(width):
                    blk = kb + w

                    @pl.when(blk + TILE_SLOTS <= qi)
                    def _(e=e, blk=blk):
                        store(e, blk + TILE_SLOTS).wait()
                    buf[e, blk % TILE_SLOTS, 0] = ab[:, w * TQ:(w + 1) * TQ]
                    buf[e, blk % TILE_SLOTS, 1] = lbb[:, w * TQ:(w + 1) * TQ]
                    store(e, blk).start()
            return tuple(new)

        init = (jnp.zeros((TQ, 1), F32), jnp.zeros((TQ, LANES), F32))
        carry = lax.cond(qi % 2 == 1, lambda cr: tile(qi - 1, cr, True, 2), lambda cr: tile(qi, cr, True), (init, init))
        pairs = qi // 2
        carry = lax.fori_loop(0, pairs, lambda it, cr: tile(2 * (pairs - 1 - it), cr, False, 2), carry)
        for e in range(2):
            for blk in range(TILE_SLOTS):
                @pl.when(qi >= blk)
                def _(e=e, blk=blk):
                    store(e, blk).wait()
        o_ref[...] = jnp.where(masks[0], carry[0][1], carry[1][1]).astype(BF16)

    ntri = nq * (nq + 1) // 2
    return pl.pallas_call(
        body, name="sb_fwd", grid=(N_HEADS // 2, nq), in_specs=_qkv_specs(s, 3 * ATTN_W // LANES),
        out_specs=[_pair_spec(), ANY],
        out_shape=[jax.ShapeDtypeStruct((s, ATTN_W), BF16), jax.ShapeDtypeStruct((N_HEADS, ntri, 2, TQ, TQ), BF16)],
        scratch_shapes=[pltpu.VMEM((2, TILE_SLOTS, 2, TQ, TQ), BF16), pltpu.SemaphoreType.DMA((2, TILE_SLOTS))],
        compiler_params=_params(("arbitrary", "arbitrary")),
    )(qkv, qkv, qkv)


def _sb_bwd(qkv, tiles, do):
    s = qkv.shape[0]
    nq = s // TQ

    def body(q_ref, k_ref, v_ref, t_ref, do_ref, dq_ref, dk_ref, dv_ref, dk_acc, dv_acc, buf, sems):
        hp, qi = pl.program_id(0), pl.program_id(1)

        @pl.when(qi == 0)
        def _():
            dk_acc[...] = jnp.zeros_like(dk_acc)
            dv_acc[...] = jnp.zeros_like(dv_acc)

        _, masks = _head_masks()
        prefix = _scan_matrix(False)
        q2 = q_ref[...] * jnp.asarray(ATTN_SCALE, BF16)
        do2 = do_ref[...]
        qms = [_pick(masks[e], q2) for e in range(2)]
        doms = [_pick(masks[e], do2) for e in range(2)]
        qts = [jnp.transpose(qms[e].astype(F32)).astype(BF16) for e in range(2)]
        dots = [jnp.transpose(doms[e].astype(F32)).astype(BF16) for e in range(2)]
        base = _tri_base(qi)

        def fetch(e, kb):
            slot = kb % TILE_SLOTS
            return pltpu.make_async_copy(t_ref.at[2 * hp + e, base + kb], buf.at[e, slot], sems.at[e, slot])

        for e in range(2):
            fetch(e, 0).start()

            @pl.when(qi >= 1)
            def _(e=e):
                fetch(e, 1).start()

        def tile(kb, carry, masked, width=1):
            off, span = pl.multiple_of(kb * TQ, TQ), width * TQ
            k2, v2 = k_ref[pl.ds(off, span), :], v_ref[pl.ds(off, span), :]
            new, dk, dv = [], None, None
            for e in range(2):
                if not masked:
                    for blk in range(2, 2 + width):
                        @pl.when(kb + blk <= qi)
                        def _(e=e, blk=blk):
                            fetch(e, kb + blk).start()
                for w in range(width):
                    fetch(e, kb + w).wait()
            for e in range(2):
                gsum, dq = carry[e]
                slots = [(kb + w) % TILE_SLOTS for w in range(width)]
                ab = buf[e, slots[0], 0] if width == 1 else jnp.concatenate([buf[e, sl, 0] for sl in slots], axis=1)
                lbb = buf[e, slots[0], 1] if width == 1 else jnp.concatenate([buf[e, sl, 1] for sl in slots], axis=1)
                beta = jnp.exp(lbb.astype(F32))
                g = ab.astype(F32) * _dot_nt(doms[e], v2)
                before, gsum = _scan_cols(g, prefix, False, gsum)
                dz = g - beta * (g + before)
                if masked:
                    dz = jnp.where(_diag_mask(width, True), dz, 0.0)
                dzb = dz.astype(BF16)
                dk_e = jnp.dot(qts[e], dzb, preferred_element_type=F32)
                dv_e = jnp.dot(dots[e], ab, preferred_element_type=F32)
                dk, dv = (dk_e, dv_e) if e == 0 else (dk + dk_e, dv + dv_e)
                new.append((gsum, dq + jnp.dot(dzb, k2, preferred_element_type=F32)))
            dk_acc[:, pl.ds(off, span)] += dk
            dv_acc[:, pl.ds(off, span)] += dv
            return tuple(new)

        init = (jnp.zeros((TQ, 1), F32), jnp.zeros((TQ, LANES), F32))
        carry = lax.fori_loop(0, qi // 2, lambda j, cr: tile(2 * j, cr, False, 2), (init, init))
        carry = lax.cond(qi % 2 == 1, lambda cr: tile(qi - 1, cr, True, 2), lambda cr: tile(qi, cr, True), carry)
        dq_ref[...] = (jnp.where(masks[0], carry[0][1], carry[1][1]) * ATTN_SCALE).astype(BF16)

        @pl.when(qi == nq - 1)
        def _():
            _write_transposed(dk_acc, dk_ref)
            _write_transposed(dv_acc, dv_ref)

    seq_spec = pl.BlockSpec((s, LANES), lambda hp, qi: (0, hp))
    return pl.pallas_call(
        body, name="sb_bwd", grid=(N_HEADS // 2, nq),
        in_specs=_qkv_specs(s, 3 * ATTN_W // LANES) + [ANY, _pair_spec()],
        out_specs=[_pair_spec(), seq_spec, seq_spec],
        out_shape=[jax.ShapeDtypeStruct((s, ATTN_W), BF16)] * 3,
        scratch_shapes=[pltpu.VMEM((LANES, s), F32), pltpu.VMEM((LANES, s), F32),
                        pltpu.VMEM((2, TILE_SLOTS, 2, TQ, TQ), BF16), pltpu.SemaphoreType.DMA((2, TILE_SLOTS))],
        compiler_params=_params(("arbitrary", "arbitrary")),
    )(qkv, qkv, qkv, tiles, do)


CONV_TR = 512


def _shift_down(x, halo, n):
    rolled = pltpu.roll(x, n, 0)
    head = rolled[0:8, :]
    rid = lax.broadcasted_iota(jnp.int32, head.shape, 0)
    for j in range(n):
        head = jnp.where(rid == j, halo[8 - n + j:8 - n + j + 1, :], head)
    return jnp.concatenate([head, rolled[8:, :]], axis=0)


def _shift_up(x, halo, n):
    rows = x.shape[0]
    rolled = pltpu.roll(x, rows - n, 0)
    tail = rolled[rows - 8:, :]
    rid = lax.broadcasted_iota(jnp.int32, tail.shape, 0)
    for j in range(n):
        tail = jnp.where(rid == 8 - n + j, halo[j:j + 1, :], tail)
    return jnp.concatenate([rolled[:rows - 8, :], tail], axis=0)


def _conv_fwd_block(x, halo, w, b):
    return b + _shift_down(x, halo, 2) * w[0:1, :] + _shift_down(x, halo, 1) * w[1:2, :] + x * w[2:3, :]


def _conv_specs(tr, s):
    pair = 2 * FF_HALF
    blk = pl.BlockSpec((tr, pair), lambda j, i: (i, j))
    prev = pl.BlockSpec((8, pair), lambda j, i: (jnp.maximum(i * (tr // 8) - 1, 0), j))
    nxt = pl.BlockSpec((8, pair), lambda j, i: (jnp.minimum((i + 1) * (tr // 8), s // 8 - 1), j))
    return blk, prev, nxt


def _conv_gate_fwd(hpre, conv_w, conv_b):
    s = hpre.shape[0]
    tr = min(CONV_TR, s)
    blk, prev, _ = _conv_specs(tr, s)

    def body(x_ref, halo_ref, w_ref, b_ref, a_ref):
        i = pl.program_id(1)
        halo = jnp.where(i > 0, halo_ref[...], 0.0)
        h = _conv_fwd_block(x_ref[...], halo, w_ref[...], b_ref[...])
        hg, hv = h[:, :FF_HALF], h[:, FF_HALF:]
        a_ref[...] = (hg * _sigmoid(hg) * hv).astype(BF16)

    return pl.pallas_call(
        body, name="conv_gate_fwd", grid=(2, s // tr),
        in_specs=[blk, prev, pl.BlockSpec((3, 2 * FF_HALF), lambda j, i: (0, j)),
                  pl.BlockSpec((1, 2 * FF_HALF), lambda j, i: (0, j))],
        out_specs=pl.BlockSpec((tr, FF_HALF), lambda j, i: (i, j)),
        out_shape=jax.ShapeDtypeStruct((s, D_FF), BF16),
        compiler_params=_params(("parallel", "parallel")),
    )(hpre, hpre, conv_w, conv_b)


def _conv_gate_bwd(hpre, da, conv_w, conv_b):
    s = hpre.shape[0]
    tr = min(CONV_TR, s)
    blk, prev, _ = _conv_specs(tr, s)

    def body(x_ref, halo_ref, da_ref, w_ref, b_ref, dh_ref, db_ref, dw_ref):
        i = pl.program_id(1)
        halo = jnp.where(i > 0, halo_ref[...], 0.0)
        x = x_ref[...]
        h = _conv_fwd_block(x, halo, w_ref[...], b_ref[...])
        hg, hv = h[:, :FF_HALF], h[:, FF_HALF:]
        da_blk = da_ref[...].astype(F32)
        sg = _sigmoid(hg)
        dhg = da_blk * hv * (sg * (1.0 + hg * (1.0 - sg)))
        dhv = da_blk * (hg * sg)
        dh_ref[:, :FF_HALF] = dhg.astype(BF16)
        dh_ref[:, FF_HALF:] = dhv.astype(BF16)
        x2, x1 = _shift_down(x, halo, 2), _shift_down(x, halo, 1)
        parts = []
        for lo, dpart in ((0, dhg), (FF_HALF, dhv)):
            cols = slice(lo, lo + FF_HALF)
            parts.append((cols, _colsum(dpart), _colsum(dpart * x2[:, cols]), _colsum(dpart * x1[:, cols]),
                          _colsum(dpart * x[:, cols])))

        @pl.when(i == 0)
        def _():
            for cols, db, dw0, dw1, dw2 in parts:
                db_ref[:, cols] = db
                dw_ref[0:1, cols] = dw0
                dw_ref[1:2, cols] = dw1
                dw_ref[2:3, cols] = dw2

        @pl.when(i > 0)
        def _():
            for cols, db, dw0, dw1, dw2 in parts:
                db_ref[:, cols] += db
                dw_ref[0:1, cols] += dw0
                dw_ref[1:2, cols] += dw1
                dw_ref[2:3, cols] += dw2

    pair = 2 * FF_HALF
    return pl.pallas_call(
        body, name="conv_gate_bwd", grid=(2, s // tr),
        in_specs=[blk, prev, pl.BlockSpec((tr, FF_HALF), lambda j, i: (i, j)),
                  pl.BlockSpec((3, pair), lambda j, i: (0, j)), pl.BlockSpec((1, pair), lambda j, i: (0, j))],
        out_specs=[blk, pl.BlockSpec((1, pair), lambda j, i: (0, j)), pl.BlockSpec((3, pair), lambda j, i: (0, j))],
        out_shape=[jax.ShapeDtypeStruct((s, 2 * D_FF), BF16), jax.ShapeDtypeStruct((1, 2 * D_FF), F32),
                   jax.ShapeDtypeStruct((3, 2 * D_FF), F32)],
        compiler_params=_params(("parallel", "arbitrary")),
    )(hpre, hpre, da, conv_w, conv_b)


def _conv_input_bwd(dh, conv_w):
    s = dh.shape[0]
    tr = min(CONV_TR, s)
    blk, _, _ = _conv_specs(tr, s)
    nblk = s // tr

    def body(x_ref, halo_ref, w_ref, o_ref):
        i = pl.program_id(1)
        halo = jnp.where(i < nblk - 1, halo_ref[...].astype(F32), 0.0)
        x, w = x_ref[...].astype(F32), w_ref[...]
        o_ref[...] = (x * w[2:3, :] + _shift_up(x, halo, 1) * w[1:2, :] + _shift_up(x, halo, 2) * w[0:1, :]).astype(BF16)

    nxt = pl.BlockSpec((16, 2 * FF_HALF), lambda j, i: (jnp.minimum((i + 1) * (tr // 16), s // 16 - 1), j))
    return pl.pallas_call(
        body, name="conv_input_bwd", grid=(2, nblk),
        in_specs=[blk, nxt, pl.BlockSpec((3, 2 * FF_HALF), lambda j, i: (0, j))], out_specs=blk,
        out_shape=jax.ShapeDtypeStruct((s, 2 * D_FF), BF16),
        compiler_params=_params(("parallel", "parallel")),
    )(dh, dh, conv_w)


def _adamw_math(w, g, m, v):
    m = ADAM_B1 * m + (1.0 - ADAM_B1) * g
    v = ADAM_B2 * v + (1.0 - ADAM_B2) * (g * g)
    m_hat = m / (1.0 - ADAM_B1 ** ADAM_STEP)
    v_hat = v / (1.0 - ADAM_B2 ** ADAM_STEP)
    delta = -ADAM_LR * (m_hat / (jnp.sqrt(v_hat) + ADAM_EPS) + ADAM_WD * w)
    return delta, m, v


def _adamw(name, g8, w, m, v):
    r, c = w.shape
    tr = _row_tile(r, c)

    def body(g_ref, w_ref, m_ref, v_ref, go_ref, d_ref, mo_ref, vo_ref):
        g = g_ref[0].astype(F32)
        for d in range(1, N_DEV):
            g = g + g_ref[d].astype(F32)
        delta, mn, vn = _adamw_math(w_ref[...], g, m_ref[...], v_ref[...])
        go_ref[...] = g
        d_ref[...] = delta
        mo_ref[...] = mn
        vo_ref[...] = vn

    spec = pl.BlockSpec((tr, c), lambda i: (i, 0))
    return pl.pallas_call(
        body, name=name, grid=(r // tr,),
        in_specs=[pl.BlockSpec((N_DEV, tr, c), lambda i: (0, i, 0)), spec, spec, spec], out_specs=[spec] * 4,
        out_shape=[jax.ShapeDtypeStruct((r, c), F32)] * 4, compiler_params=_params(("parallel",)),
    )(g8, w, m, v)


def _adamw_small(g8, ws, ms, vs):
    n = len(ws)
    offsets = [sum(w.shape[0] for w in ws[:i]) for i in range(n)]

    def body(*refs):
        g_ref, w_refs, m_refs, v_refs = refs[0], refs[1:1 + n], refs[1 + n:1 + 2 * n], refs[1 + 2 * n:1 + 3 * n]
        outs = refs[1 + 3 * n:]
        for i in range(n):
            rows = w_refs[i].shape[0]
            g = g_ref[0, offsets[i]:offsets[i] + rows, :]
            for d in range(1, N_DEV):
                g = g + g_ref[d, offsets[i]:offsets[i] + rows, :]
            delta, mn, vn = _adamw_math(w_refs[i][...], g, m_refs[i][...], v_refs[i][...])
            for k, val in enumerate((g, delta, mn, vn)):
                outs[k * n + i][...] = val

    vmem = pl.BlockSpec(memory_space=pltpu.VMEM)
    res = pl.pallas_call(
        body, name="adamw_small", in_specs=[vmem] * (1 + 3 * n), out_specs=[vmem] * (4 * n),
        out_shape=[jax.ShapeDtypeStruct(w.shape, F32) for _ in range(4) for w in ws], compiler_params=_params(),
    )(g8, *ws, *ms, *vs)
    return [res[k * n:(k + 1) * n] for k in range(4)]


def _adamw_ada(c_t, dmod, w, m, v):
    r, c = w.shape
    tr = _row_tile(r, c)

    def body(ct_ref, dm_ref, w_ref, m_ref, v_ref, go_ref, d_ref, mo_ref, vo_ref):
        ct, dm = ct_ref[...], dm_ref[...]
        g = ct[:, 0:1] * dm[0:1, :]
        for b in range(1, N_DEV):
            g = g + ct[:, b:b + 1] * dm[b:b + 1, :]
        delta, mn, vn = _adamw_math(w_ref[...], g, m_ref[...], v_ref[...])
        go_ref[...] = g
        d_ref[...] = delta
        mo_ref[...] = mn
        vo_ref[...] = vn

    spec = pl.BlockSpec((tr, c), lambda i: (i, 0))
    return pl.pallas_call(
        body, name="adamw_w_ada", grid=(r // tr,),
        in_specs=[pl.BlockSpec((tr, N_DEV), lambda i: (i, 0)), pl.BlockSpec((N_DEV, c), lambda i: (0, 0)),
                  spec, spec, spec],
        out_specs=[spec] * 4, out_shape=[jax.ShapeDtypeStruct((r, c), F32)] * 4,
        compiler_params=_params(("parallel",)),
    )(c_t, dmod, w, m, v)


def _cols_from_slots(g):
    n, r, c = g.shape
    return jnp.transpose(g, (1, 0, 2)).reshape(r, n * c)


def _cols_to_slots(w):
    r, c = w.shape
    return jnp.transpose(w.reshape(r, N_DEV, c // N_DEV), (1, 0, 2))


def _pair_cols(w):
    g0, g1 = w[..., 0:FF_HALF], w[..., FF_HALF:D_FF]
    v0, v1 = w[..., D_FF:D_FF + FF_HALF], w[..., D_FF + FF_HALF:]
    return jnp.concatenate([g0, v0, g1, v1], axis=-1)


def _unpair_cols(w):
    g0, v0 = w[..., 0:FF_HALF], w[..., FF_HALF:D_FF]
    g1, v1 = w[..., D_FF:D_FF + FF_HALF], w[..., D_FF + FF_HALF:]
    return jnp.concatenate([g0, g1, v0, v1], axis=-1)


def _row(v):
    return v.reshape(1, -1)


def kernel(x, c, w_ada, b_ada, w_in, b_forget, w_fox_proj, w_sb_proj, w_o, ln1_g, ln1_b, w_up, conv_w, conv_b, w_down, ln2_g, ln2_b, loss_target, m_w_ada, m_b_ada, m_w_in, m_b_forget, m_w_fox_proj, m_w_sb_proj, m_w_o, m_ln1_g, m_ln1_b, m_w_up, m_conv_w, m_conv_b, m_w_down, m_ln2_g, m_ln2_b, v_w_ada, v_b_ada, v_w_in, v_b_forget, v_w_fox_proj, v_w_sb_proj, v_w_o, v_ln1_g, v_ln1_b, v_w_up, v_conv_w, v_conv_b, v_w_down, v_ln2_g, v_ln2_b):
    s = x.shape[1]
    me = 4 * lax.axis_index("x") + 2 * lax.axis_index("y") + lax.axis_index("c")
    x2 = x.reshape(s, D_MODEL)
    tgt = loss_target.reshape(s, D_MODEL)

    b_ada_loc = lax.dynamic_slice(b_ada, (me * ADA_SHARD,), (ADA_SHARD,)).reshape(1, ADA_SHARD)
    c_all, mod = _mod_exchange(c, w_ada, b_ada_loc)
    mod = mod.reshape(N_MOD, 1, D_MODEL)
    sh1, sc1, gt1, sh2, sc2, gt2 = [mod[i] for i in range(N_MOD)]

    g_in = _allgather_two_level("ag_w_in", w_in.astype(BF16))
    late_weights = _Ride([w_fox_proj.astype(BF16), w_sb_proj.astype(BF16), w_o.astype(BF16), w_up.astype(BF16),
                          w_down.astype(BF16), conv_w], scatter=False)
    w_in_f = _cols_from_slots(g_in)
    w_proj = jnp.concatenate(
        [w_in_f[:, 0:1536], w_in_f[:, 1544:3080], w_in_f[:, 3080:5128], w_in_f[:, 1536:1544],
         jnp.zeros((D_MODEL, W_PROJ - 5128), BF16)], axis=1)
    w_qkv, w_gates, w_f = w_proj[:, :W_QKV], w_proj[:, W_QKV:W_QKV + W_GATES], w_proj[:, W_QKV + W_GATES:W_QKV + W_GATES + W_F]
    conv_b_p = _pair_cols(_row(conv_b))
    b_f_pad = jnp.pad(_row(b_forget), ((0, 0), (0, LANES - N_HEADS)))

    (u1, qkv), _ = _mm_pieces_nt(
        [], w_qkv, name="mm_qkv", prologue=(lambda xb, sc, sh: xb * (1.0 + sc) + sh, [x2], [sc1, sh1]),
        epilogue=lambda acc: (acc,), rows=[], vecs=[], out_dtypes=(BF16,), w_kn=True)
    gates = _mm(u1, jnp.concatenate([w_gates, w_f], axis=1), name="mm_gates_forget", tn=W_GATES + W_F)
    f_raw = gates[:, W_GATES:]
    cum_col = _forget_cumsum(f_raw, b_f_pad)
    cum_row = jnp.transpose(cum_col[:, :N_HEADS]).reshape(N_HEADS // 2, 2, s)
    (y_fox, y_fox32, lse), (g_fox, g_sb, g_o, g_up, g_down, g_cw) = _fox_fwd(qkv, cum_col, cum_row, ride=late_weights)
    w_fox_f = _cols_from_slots(g_fox)
    w_sb_f = _cols_from_slots(g_sb)
    w_o_f = g_o.reshape(D_MODEL, D_MODEL)
    w_up_p = _pair_cols(_cols_from_slots(g_up))
    w_down_f = g_down.reshape(D_FF, D_MODEL)
    conv_w_p = _pair_cols(_cols_from_slots(g_cw))
    y_sb, sb_run = _sb_fwd(qkv)
    pf = _mm(y_fox, w_fox_f, name="mm_fox_proj", out_dtype=BF16)
    ps = _mm(y_sb, w_sb_f, name="mm_sb_proj", out_dtype=BF16)
    def ln1_mod(fb, xb, gt, g, b, sc, sh):
        xhat, _ = _ln_stats(ALPHA * xb + (1.0 + gt) * fb)
        y = xhat * g + b
        return fb, y, y * (1.0 + sc) + sh

    (merged, attn_out, x1, u2), _ = _mm_pieces_nt(
        [], w_o_f, name="mm_w_o_ln1",
        prologue=(lambda ga, gb, a, b: _sigmoid(ga) * a + _sigmoid(gb) * b, [(gates, 0), (gates, 1), pf, ps], []),
        epilogue=ln1_mod, rows=[x2], vecs=[gt1, _row(ln1_g), _row(ln1_b), sc2, sh2],
        out_dtypes=(F32, F32, BF16), w_kn=True)

    hpre = _mm(u2, w_up_p, name="mm_w_up", tn=1408)
    act = _conv_gate_fwd(hpre, conv_w_p, conv_b_p)

    def ln2_bwd(fb, xb, tb, gt, g, b):
        xhat, rstd = _ln_stats(ALPHA * xb + (1.0 + gt) * fb)
        err = (xhat * g + b) - tb
        dy = err * (1.0 / D_MODEL)
        dr = _ln_bwd(dy, xhat, rstd, g)
        return (dr * (1.0 + gt), ALPHA * dr,
                _colsum(err * err), _colsum(dy * xhat), _colsum(dy), _colsum(dr * fb))

    (dffn, dx1_res, sq_err, d_ln2_g, d_ln2_b, d_gt2), _ = _mm_pieces_nt(
        [act], w_down_f, name="mm_w_down_ln2_bwd", epilogue=ln2_bwd, rows=[x1, tgt],
        vecs=[gt2, _row(ln2_g), _row(ln2_b)], out_dtypes=(BF16, F32), n_sums=4, w_kn=True)
    loss_part = jnp.broadcast_to(0.5 * jnp.sum(sq_err) / D_MODEL, (1, LANES))

    d_w_down = _mm(act, dffn, name="mm_d_w_down", ta=True, tm=1408, tk=2048, out_dtype=BF16)
    d_act = _mm(dffn, w_down_f, name="mm_d_act", tb=True, tn=1408, out_dtype=BF16)
    dh, d_conv_b_p, d_conv_w_p = _conv_gate_bwd(hpre, d_act, conv_w_p, conv_b_p)
    dhpre = _conv_input_bwd(dh, conv_w_p)
    d_w_up_p = _mm(u2, dhpre, name="mm_d_w_up", ta=True, tn=1408, tk=2048, out_dtype=BF16)
    def ln1_bwd(du, dres, x1b, xb, fb, sc, gt, g):
        dx1 = dres + du * (1.0 + sc)
        xhat, rstd = _ln_stats(ALPHA * xb + (1.0 + gt) * fb)
        dr = _ln_bwd(dx1, xhat, rstd, g)
        return (dr * (1.0 + gt), ALPHA * dr,
                _colsum(du * x1b), _colsum(du), _colsum(dx1 * xhat), _colsum(dx1), _colsum(dr * fb))

    (d_attn, dx_res, d_sc2, d_sh2, d_ln1_g, d_ln1_b, d_gt1), _ = _mm_pieces_nt(
        [dhpre], w_up_p, name="mm_d_u2_ln1_bwd", epilogue=ln1_bwd, rows=[dx1_res, x1, x2, attn_out],
        vecs=[sc2, gt1, _row(ln1_g)], out_dtypes=(BF16, F32), n_sums=5, tm=256)

    d_w_o = _mm(merged, d_attn, name="mm_d_w_o", ta=True, out_dtype=BF16)

    def merge_bwd(dm, ga, gb, a, b):
        a, b = a.astype(F32), b.astype(F32)
        sa, sb = _sigmoid(ga), _sigmoid(gb)
        return dm * a * sa * (1.0 - sa), dm * b * sb * (1.0 - sb), dm * sa, dm * sb

    (d_ga, d_gb, d_pf, d_ps), _ = _mm_pieces_nt(
        [d_attn], w_o_f, name="mm_d_merged_gate_bwd", epilogue=merge_bwd, rows=[(gates, 0), (gates, 1), pf, ps],
        vecs=[], out_dtypes=(BF16,) * 4)
    d_w_fox = _mm(y_fox, d_pf, name="mm_d_w_fox", ta=True, out_dtype=BF16)
    d_w_sb = _mm(y_sb, d_ps, name="mm_d_w_sb", ta=True, out_dtype=BF16)
    d_y_fox = _mm(d_pf, w_fox_f, name="mm_d_y_fox", tb=True, out_dtype=BF16)
    d_y_sb = _mm(d_ps, w_sb_f, name="mm_d_y_sb", tb=True, out_dtype=BF16)
    early_grads = _Ride(
        [_cols_to_slots(d_w_fox), _cols_to_slots(d_w_sb), d_w_o.reshape(N_DEV, D_MODEL // N_DEV, D_MODEL),
         _cols_to_slots(_unpair_cols(d_w_up_p)), d_w_down.reshape(N_DEV, D_FF // N_DEV, D_MODEL)], scatter=True)
    (dq_a, dk_a, dv_a, d_cum_row), early_slots = _fox_bwd(qkv, cum_col, cum_row, y_fox32, lse, d_y_fox,
                                                                    ride=early_grads)
    dq_b, dk_b, dv_b = _sb_bwd(qkv, sb_run, d_y_sb)
    d_cum = jnp.transpose(d_cum_row.reshape(N_HEADS, s))
    d_cum = jnp.pad(d_cum, ((0, 0), (0, LANES - N_HEADS)))
    d_f, d_b_forget = _forget_bwd(d_cum, f_raw, b_f_pad)
    d_qkv = [dq_a, dk_a, dv_a, dq_b, dk_b, dv_b]
    g_qkv = list(_mm_pieces_tn(u1, d_qkv, name="mm_d_w_in_qkv"))
    g_ga, g_gb, g_f = _mm_pieces_tn(u1, [d_ga, d_gb, d_f], name="mm_d_w_in_gates", tk=2048)
    d_w_in_f = jnp.concatenate(g_qkv[:3] + [g_f[:, :N_HEADS]] + g_qkv[3:] + [g_ga, g_gb], axis=1)
    def x_bwd(du, dres, xb, sc):
        return dres + du * (1.0 + sc), _colsum(du * xb), _colsum(du)

    (grad_x, d_sc1, d_sh1), (in_slots,) = _mm_pieces_nt(
        d_qkv + [d_ga, d_gb, d_f], w_proj, name="mm_d_u1_x_bwd", epilogue=x_bwd, rows=[dx_res, x2], vecs=[sc1],
        n_sums=2, ride=_Ride([_cols_to_slots(d_w_in_f)], scatter=True))

    d_conv_b = _unpair_cols(d_conv_b_p)
    d_conv_w = _unpair_cols(d_conv_w_p)
    cb_pad = N_MOD * D_MODEL - 2 * D_FF
    small = jnp.concatenate(
        [d_sh1, d_sc1, d_gt1, d_sh2, d_sc2, d_gt2, jnp.pad(d_b_forget, ((0, 0), (0, D_MODEL - LANES))),
         d_ln1_g, d_ln1_b, d_ln2_g, d_ln2_b, jnp.pad(d_conv_b, ((0, 0), (0, cb_pad))),
         d_conv_w.reshape(1, 6 * D_FF), loss_part], axis=1)
    n_small = small.shape[1] // LANES - 1
    small = jnp.pad(small.reshape(n_small + 1, LANES), ((0, -(n_small + 1) % 8), (0, 0)))
    (small_all,) = _exchange("ag_small_grads", [small], scatter=False)
    loss = jnp.sum(small_all[:, n_small, 0])
    n_rep = n_small - 6 * D_FF // LANES
    cw8 = small_all[:, n_rep:n_small, :].reshape(N_DEV, 3, 2 * D_FF)
    cw8 = lax.dynamic_slice(cw8, (0, 0, me * UP_SHARD), (N_DEV, 3, UP_SHARD))
    dmod8 = small_all[:, :N_MOD * D_MODEL // LANES, :].reshape(N_DEV, N_MOD * D_MODEL)
    dmod_loc = lax.dynamic_slice(dmod8, (0, me * ADA_SHARD), (N_DEV, ADA_SHARD))

    def rows_of(a):
        return a.reshape(-1, LANES)

    def forget_rows(a):
        return rows_of(jnp.pad(a, (0, D_MODEL - N_HEADS)))

    def conv_b_rows(a):
        return rows_of(jnp.pad(a, (0, cb_pad)))

    rep_sizes = {"b_ada": N_MOD * D_MODEL, "b_forget": N_HEADS, "ln1_g": D_MODEL, "ln1_b": D_MODEL, "ln2_g": D_MODEL,
                 "ln2_b": D_MODEL, "conv_b": 2 * D_FF}
    rep_w = [rows_of(b_ada), forget_rows(b_forget), rows_of(ln1_g), rows_of(ln1_b), rows_of(ln2_g), rows_of(ln2_b),
             conv_b_rows(conv_b)]
    rep_m = [rows_of(m_b_ada), forget_rows(m_b_forget), rows_of(m_ln1_g), rows_of(m_ln1_b), rows_of(m_ln2_g),
             rows_of(m_ln2_b), conv_b_rows(m_conv_b)]
    rep_v = [rows_of(v_b_ada), forget_rows(v_b_forget), rows_of(v_ln1_g), rows_of(v_ln1_b), rows_of(v_ln2_g),
             rows_of(v_ln2_b), conv_b_rows(v_conv_b)]
    rep_out = _adamw_small(small_all, rep_w, rep_m, rep_v)
    rep = [{name: a.reshape(-1)[:size] for (name, size), a in zip(rep_sizes.items(), outs)} for outs in rep_out]
    r_conv_w = _adamw("adamw_conv_w", cw8, conv_w, m_conv_w, v_conv_w)
    r_ada = _adamw_ada(jnp.transpose(c_all.reshape(N_DEV, D_MODEL)), dmod_loc, w_ada, m_w_ada, v_w_ada)

    r_in = _adamw("adamw_w_in", in_slots, w_in, m_w_in, v_w_in)
    r_fox = _adamw("adamw_w_fox", early_slots[0], w_fox_proj, m_w_fox_proj, v_w_fox_proj)
    r_sb = _adamw("adamw_w_sb", early_slots[1], w_sb_proj, m_w_sb_proj, v_w_sb_proj)
    r_o = _adamw("adamw_w_o", early_slots[2], w_o, m_w_o, v_w_o)
    r_up = _adamw("adamw_w_up", early_slots[3], w_up, m_w_up, v_w_up)
    r_down = _adamw("adamw_w_down", early_slots[4], w_down, m_w_down, v_w_down)

    def leaf(i):
        return [r_ada[i], rep[i]["b_ada"], r_in[i], rep[i]["b_forget"], r_fox[i], r_sb[i], r_o[i], rep[i]["ln1_g"],
                rep[i]["ln1_b"], r_up[i], r_conv_w[i], rep[i]["conv_b"], r_down[i], rep[i]["ln2_g"], rep[i]["ln2_b"]]

    return (loss, grad_x.reshape(1, s, D_MODEL), *leaf(0), *leaf(1), *leaf(2), *leaf(3))
```
